```python
import jax, jax.numpy as jnp
from jax import lax
import numpy as np

D_MODEL = 2048
BATCH = 8
SEQ = 2048
DEPTH = 2

CHUNK = 64
N_MIXERS = 2
EPS = 1e-6

GLA_HEADS = 4
GLA_DK = D_MODEL // 2
GLA_DV = D_MODEL
GLA_DK_HEAD = GLA_DK // GLA_HEADS
GLA_DV_HEAD = GLA_DV // GLA_HEADS
GLA_GATE_RANK = 16
GLA_TAU = 16.0
GLA_IN = 2 * GLA_DK + 2 * GLA_DV + GLA_GATE_RANK

SGU_WIDTH = D_MODEL
SGU_BLOCK = 128
SGU_GROUPS = 8
SGU_GROUP_DIM = SGU_WIDTH // SGU_GROUPS
SGU_IN = 3 * SGU_WIDTH

N_GLA_LAYERS = (DEPTH + 1) // 2
N_SGU_LAYERS = DEPTH // 2

kernel_name = "hybrid_gla_sgu_sandwich_trunk"


def rmsnorm(x, gain):
    xf = x.astype(jnp.float32)
    y = xf * lax.rsqrt(jnp.mean(xf * xf, axis=-1, keepdims=True) + EPS)
    return (y * gain.astype(jnp.float32)).astype(x.dtype)


def gla_mixer(h, w_in, w_gate2, b_gate, o_gain, w_out):
    B, S, _ = h.shape
    nc = S // CHUNK
    proj = h @ w_in
    q, k, v, g, glr = jnp.split(
        proj, [GLA_DK, 2 * GLA_DK, 2 * GLA_DK + GLA_DV, 2 * GLA_DK + 2 * GLA_DV], axis=-1)
    log_a = jax.nn.log_sigmoid((glr @ w_gate2 + b_gate).astype(jnp.float32)) / GLA_TAU

    def to_chunks(t, dh):
        return t.astype(jnp.float32).reshape(B, nc, CHUNK, GLA_HEADS, dh).transpose(1, 0, 3, 2, 4)

    qc = to_chunks(q, GLA_DK_HEAD) * (GLA_DK_HEAD ** -0.5)
    kc = to_chunks(k, GLA_DK_HEAD)
    vc = to_chunks(v, GLA_DV_HEAD)
    la = to_chunks(log_a, GLA_DK_HEAD)
    bcum = jnp.cumsum(la, axis=3)
    b_end = bcum[:, :, :, -1:, :]
    k_dec = kc * jnp.exp(b_end - bcum)
    decay = jnp.exp(b_end[:, :, :, 0, :])

    def step(state, xs):
        q_i, k_i, v_i, d_i = xs
        state = state * d_i[..., None] + jnp.einsum('bhck,bhcv->bhkv', k_i, v_i)
        return state, jnp.einsum('bhck,bhkv->bhcv', q_i, state)

    s0 = jnp.zeros((B, GLA_HEADS, GLA_DK_HEAD, GLA_DV_HEAD), jnp.float32)
    _, o = lax.scan(step, s0, (qc, k_dec, vc, decay))
    o = o.transpose(1, 0, 3, 2, 4).reshape(B, S, GLA_HEADS, GLA_DV_HEAD)
    o = o * lax.rsqrt(jnp.mean(o * o, axis=-1, keepdims=True) + EPS)
    o = o.reshape(B, S, GLA_DV) * o_gain.astype(jnp.float32)
    o = o.astype(h.dtype) * jax.nn.silu(g)
    return o @ w_out


def sgu_mixer(h, w_in, ln_gain, ln_bias, w_spatial, b_spatial, w_out):
    B, S, _ = h.shape
    nb = S // SGU_BLOCK
    proj = h @ w_in
    u, v, g = jnp.split(proj, 3, axis=-1)
    u = jax.nn.gelu(u)
    vf = jax.nn.gelu(v).astype(jnp.float32)
    mu = jnp.mean(vf, axis=-1, keepdims=True)
    var = jnp.mean(jnp.square(vf - mu), axis=-1, keepdims=True)
    vn = (vf - mu) * lax.rsqrt(var + EPS) * ln_gain.astype(jnp.float32) + ln_bias.astype(jnp.float32)
    vn = vn.reshape(B, nb, SGU_BLOCK, SGU_GROUPS, SGU_GROUP_DIM)
    pos_chunk = jnp.arange(SGU_BLOCK) // CHUNK
    mask = pos_chunk[:, None] >= pos_chunk[None, :]
    ws = jnp.where(mask[None], w_spatial, 0).astype(jnp.float32)
    vs = jnp.einsum('gij,bnjgd->bnigd', ws, vn) \
        + b_spatial.astype(jnp.float32).T[None, None, :, :, None]
    vs = vs.reshape(B, S, SGU_WIDTH).astype(h.dtype)
    return (u * vs * jax.nn.silu(g)) @ w_out


def _fwd_setup_inputs(seed: int = 0) -> dict:
    key = jax.random.key(seed)
    ks = jax.random.split(key, 15)
    f32 = jnp.float32
    nrm = lambda k, shape, scale: jax.random.normal(k, shape, f32) * scale
    return {
        "x": nrm(ks[0], (BATCH, SEQ, D_MODEL), 1.0),
        "norm_pre": 1.0 + nrm(ks[1], (DEPTH, D_MODEL), 0.02),
        "norm_post": 1.0 + nrm(ks[2], (DEPTH, D_MODEL), 0.02),
        "gla_w_in": nrm(ks[3], (N_GLA_LAYERS, D_MODEL, GLA_IN), D_MODEL ** -0.5),
        "gla_w_gate2": nrm(ks[4], (N_GLA_LAYERS, GLA_GATE_RANK, GLA_DK), GLA_GATE_RANK ** -0.5),
        "gla_b_gate": nrm(ks[5], (N_GLA_LAYERS, GLA_DK), 0.1),
        "gla_o_gain": 1.0 + nrm(ks[6], (N_GLA_LAYERS, GLA_DV), 0.02),
        "gla_w_out": nrm(ks[7], (N_GLA_LAYERS, GLA_DV, D_MODEL), GLA_DV ** -0.5),
        "sgu_w_in": nrm(ks[8], (N_SGU_LAYERS, D_MODEL, SGU_IN), D_MODEL ** -0.5),
        "sgu_ln_gain": 1.0 + nrm(ks[9], (N_SGU_LAYERS, SGU_WIDTH), 0.02),
        "sgu_ln_bias": nrm(ks[10], (N_SGU_LAYERS, SGU_WIDTH), 0.02),
        "sgu_w_spatial": nrm(ks[11], (N_SGU_LAYERS, SGU_GROUPS, SGU_BLOCK, SGU_BLOCK), SGU_BLOCK ** -0.5),
        "sgu_b_spatial": 1.0 + nrm(ks[12], (N_SGU_LAYERS, SGU_GROUPS, SGU_BLOCK), 0.1),
        "sgu_w_out": nrm(ks[13], (N_SGU_LAYERS, SGU_WIDTH, D_MODEL), SGU_WIDTH ** -0.5),
    }


def _fwd_reference(x, norm_pre, norm_post, gla_w_in, gla_w_gate2, gla_b_gate, gla_o_gain,
              gla_w_out, sgu_w_in, sgu_ln_gain, sgu_ln_bias, sgu_w_spatial, sgu_b_spatial,
              sgu_w_out):
    for i in range(DEPTH):
        h = rmsnorm(x, norm_pre[i])
        j = i // N_MIXERS
        if i % N_MIXERS == 0:
            y = gla_mixer(h, gla_w_in[j], gla_w_gate2[j], gla_b_gate[j], gla_o_gain[j], gla_w_out[j])
        else:
            y = sgu_mixer(h, sgu_w_in[j], sgu_ln_gain[j], sgu_ln_bias[j], sgu_w_spatial[j],
                          sgu_b_spatial[j], sgu_w_out[j])
        x = x + rmsnorm(y, norm_post[i])
    return x


import jax as _jax
import jax.numpy as _jnp

TWIN_FORMAT = 'train_step'
FWD_PARAMS = ['x', 'norm_pre', 'norm_post', 'gla_w_in', 'gla_w_gate2', 'gla_b_gate', 'gla_o_gain', 'gla_w_out', 'sgu_w_in', 'sgu_ln_gain', 'sgu_ln_bias', 'sgu_w_spatial', 'sgu_b_spatial', 'sgu_w_out']
TWIN_WEIGHTS = ['norm_pre', 'norm_post', 'gla_w_in', 'gla_w_gate2', 'gla_b_gate', 'gla_o_gain', 'gla_w_out', 'sgu_w_in', 'sgu_ln_gain', 'sgu_ln_bias', 'sgu_w_spatial', 'sgu_b_spatial', 'sgu_w_out']
TWIN_DIFF_INPUT = 'x'
TWIN_INPUTS = ['x', 'norm_pre', 'norm_post', 'gla_w_in', 'gla_w_gate2', 'gla_b_gate', 'gla_o_gain', 'gla_w_out', 'sgu_w_in', 'sgu_ln_gain', 'sgu_ln_bias', 'sgu_w_spatial', 'sgu_b_spatial', 'sgu_w_out', 'loss_target', 'm_norm_pre', 'm_norm_post', 'm_gla_w_in', 'm_gla_w_gate2', 'm_gla_b_gate', 'm_gla_o_gain', 'm_gla_w_out', 'm_sgu_w_in', 'm_sgu_ln_gain', 'm_sgu_ln_bias', 'm_sgu_w_spatial', 'm_sgu_b_spatial', 'm_sgu_w_out', 'v_norm_pre', 'v_norm_post', 'v_gla_w_in', 'v_gla_w_gate2', 'v_gla_b_gate', 'v_gla_o_gain', 'v_gla_w_out', 'v_sgu_w_in', 'v_sgu_ln_gain', 'v_sgu_ln_bias', 'v_sgu_w_spatial', 'v_sgu_b_spatial', 'v_sgu_w_out']
TWIN_OUTPUTS = ['loss', 'grad_x', 'grad_norm_pre', 'grad_norm_post', 'grad_gla_w_in', 'grad_gla_w_gate2', 'grad_gla_b_gate', 'grad_gla_o_gain', 'grad_gla_w_out', 'grad_sgu_w_in', 'grad_sgu_ln_gain', 'grad_sgu_ln_bias', 'grad_sgu_w_spatial', 'grad_sgu_b_spatial', 'grad_sgu_w_out', 'delta_norm_pre', 'delta_norm_post', 'delta_gla_w_in', 'delta_gla_w_gate2', 'delta_gla_b_gate', 'delta_gla_o_gain', 'delta_gla_w_out', 'delta_sgu_w_in', 'delta_sgu_ln_gain', 'delta_sgu_ln_bias', 'delta_sgu_w_spatial', 'delta_sgu_b_spatial', 'delta_sgu_w_out', 'new_m_norm_pre', 'new_m_norm_post', 'new_m_gla_w_in', 'new_m_gla_w_gate2', 'new_m_gla_b_gate', 'new_m_gla_o_gain', 'new_m_gla_w_out', 'new_m_sgu_w_in', 'new_m_sgu_ln_gain', 'new_m_sgu_ln_bias', 'new_m_sgu_w_spatial', 'new_m_sgu_b_spatial', 'new_m_sgu_w_out', 'new_v_norm_pre', 'new_v_norm_post', 'new_v_gla_w_in', 'new_v_gla_w_gate2', 'new_v_gla_b_gate', 'new_v_gla_o_gain', 'new_v_gla_w_out', 'new_v_sgu_w_in', 'new_v_sgu_ln_gain', 'new_v_sgu_ln_bias', 'new_v_sgu_w_spatial', 'new_v_sgu_b_spatial', 'new_v_sgu_w_out']
TWIN_LEAF_KINDS = {'loss': 'loss', 'grad_x': 'grad_x', 'grad_norm_pre': 'grad_w', 'grad_norm_post': 'grad_w', 'grad_gla_w_in': 'grad_w', 'grad_gla_w_gate2': 'grad_w', 'grad_gla_b_gate': 'grad_w', 'grad_gla_o_gain': 'grad_w', 'grad_gla_w_out': 'grad_w', 'grad_sgu_w_in': 'grad_w', 'grad_sgu_ln_gain': 'grad_w', 'grad_sgu_ln_bias': 'grad_w', 'grad_sgu_w_spatial': 'grad_w', 'grad_sgu_b_spatial': 'grad_w', 'grad_sgu_w_out': 'grad_w', 'delta_norm_pre': 'delta_w', 'delta_norm_post': 'delta_w', 'delta_gla_w_in': 'delta_w', 'delta_gla_w_gate2': 'delta_w', 'delta_gla_b_gate': 'delta_w', 'delta_gla_o_gain': 'delta_w', 'delta_gla_w_out': 'delta_w', 'delta_sgu_w_in': 'delta_w', 'delta_sgu_ln_gain': 'delta_w', 'delta_sgu_ln_bias': 'delta_w', 'delta_sgu_w_spatial': 'delta_w', 'delta_sgu_b_spatial': 'delta_w', 'delta_sgu_w_out': 'delta_w', 'new_m_norm_pre': 'new_m', 'new_m_norm_post': 'new_m', 'new_m_gla_w_in': 'new_m', 'new_m_gla_w_gate2': 'new_m', 'new_m_gla_b_gate': 'new_m', 'new_m_gla_o_gain': 'new_m', 'new_m_gla_w_out': 'new_m', 'new_m_sgu_w_in': 'new_m', 'new_m_sgu_ln_gain': 'new_m', 'new_m_sgu_ln_bias': 'new_m', 'new_m_sgu_w_spatial': 'new_m', 'new_m_sgu_b_spatial': 'new_m', 'new_m_sgu_w_out': 'new_m', 'new_v_norm_pre': 'new_v', 'new_v_norm_post': 'new_v', 'new_v_gla_w_in': 'new_v', 'new_v_gla_w_gate2': 'new_v', 'new_v_gla_b_gate': 'new_v', 'new_v_gla_o_gain': 'new_v', 'new_v_gla_w_out': 'new_v', 'new_v_sgu_w_in': 'new_v', 'new_v_sgu_ln_gain': 'new_v', 'new_v_sgu_ln_bias': 'new_v', 'new_v_sgu_w_spatial': 'new_v', 'new_v_sgu_b_spatial': 'new_v', 'new_v_sgu_w_out': 'new_v'}


def _forward(args):
    return _fwd_reference(*[args[k] for k in FWD_PARAMS])


def _output_shape():
    out = _jax.eval_shape(lambda: _forward(_fwd_setup_inputs(0)))
    return out.shape, out.dtype

N_MICROBATCH = 1
ADAM_LR = 0.001
ADAM_B1 = 0.9
ADAM_B2 = 0.999
ADAM_EPS = 1e-08
ADAM_WD = 0.01
ADAM_STEP = 10
PER_EXAMPLE_BATCH_AXIS = {'x': 0, 'loss_target': 0}
SHARED_INPUTS = []
_WEIGHT_DTYPES = {'norm_pre': _jnp.float32, 'norm_post': _jnp.float32, 'gla_w_in': _jnp.float32, 'gla_w_gate2': _jnp.float32, 'gla_b_gate': _jnp.float32, 'gla_o_gain': _jnp.float32, 'gla_w_out': _jnp.float32, 'sgu_w_in': _jnp.float32, 'sgu_ln_gain': _jnp.float32, 'sgu_ln_bias': _jnp.float32, 'sgu_w_spatial': _jnp.float32, 'sgu_b_spatial': _jnp.float32, 'sgu_w_out': _jnp.float32}
MOMENT_SCALE = {'norm_pre': 2.821316e-01, 'norm_post': 8.003148e+00, 'gla_w_in': 1.970057e-01, 'gla_w_gate2': 2.725120e-02, 'gla_b_gate': 1.050431e-01, 'gla_o_gain': 1.740812e-01, 'gla_w_out': 1.717593e-01, 'sgu_w_in': 1.097645e-01, 'sgu_ln_gain': 6.913450e-02, 'sgu_ln_bias': 6.962039e-02, 'sgu_w_spatial': 9.880116e-02, 'sgu_b_spatial': 1.127290e-01, 'sgu_w_out': 1.270575e-01}


def _to_microbatches(a, axis):
    t = _jnp.moveaxis(a, axis, 0)
    t = t.reshape((N_MICROBATCH, t.shape[0] // N_MICROBATCH) + t.shape[1:])
    return _jnp.moveaxis(t, 1, axis + 1)


def setup_inputs(seed: int = 0) -> dict:
    inp = _fwd_setup_inputs(seed)
    key = _jax.random.fold_in(_jax.random.key(seed), 7919)
    shape, _ = _output_shape()
    out = dict(inp)
    out["loss_target"] = _jax.random.normal(_jax.random.fold_in(key, 0), shape, _jnp.float32)
    for i, name in enumerate(TWIN_WEIGHTS):
        w = inp[name].astype(_jnp.float32)
        if MOMENT_SCALE is None:
            s = _jnp.sqrt(_jnp.mean(_jnp.square(w)) + 1e-30)
        else:
            s = MOMENT_SCALE[name]
        km, kv = _jax.random.split(_jax.random.fold_in(key, i + 1))
        out[name] = w
        out["m_" + name] = s * _jax.random.normal(km, w.shape, _jnp.float32)
        out["v_" + name] = (s * s) * _jax.random.uniform(kv, w.shape, _jnp.float32, 0.5, 1.5)
    if N_MICROBATCH > 1:
        for name, axis in PER_EXAMPLE_BATCH_AXIS.items():
            out[name] = _to_microbatches(out[name], axis)
    return {'x': out['x'], 'norm_pre': out['norm_pre'], 'norm_post': out['norm_post'], 'gla_w_in': out['gla_w_in'], 'gla_w_gate2': out['gla_w_gate2'], 'gla_b_gate': out['gla_b_gate'], 'gla_o_gain': out['gla_o_gain'], 'gla_w_out': out['gla_w_out'], 'sgu_w_in': out['sgu_w_in'], 'sgu_ln_gain': out['sgu_ln_gain'], 'sgu_ln_bias': out['sgu_ln_bias'], 'sgu_w_spatial': out['sgu_w_spatial'], 'sgu_b_spatial': out['sgu_b_spatial'], 'sgu_w_out': out['sgu_w_out'], 'loss_target': out['loss_target'], 'm_norm_pre': out['m_norm_pre'], 'm_norm_post': out['m_norm_post'], 'm_gla_w_in': out['m_gla_w_in'], 'm_gla_w_gate2': out['m_gla_w_gate2'], 'm_gla_b_gate': out['m_gla_b_gate'], 'm_gla_o_gain': out['m_gla_o_gain'], 'm_gla_w_out': out['m_gla_w_out'], 'm_sgu_w_in': out['m_sgu_w_in'], 'm_sgu_ln_gain': out['m_sgu_ln_gain'], 'm_sgu_ln_bias': out['m_sgu_ln_bias'], 'm_sgu_w_spatial': out['m_sgu_w_spatial'], 'm_sgu_b_spatial': out['m_sgu_b_spatial'], 'm_sgu_w_out': out['m_sgu_w_out'], 'v_norm_pre': out['v_norm_pre'], 'v_norm_post': out['v_norm_post'], 'v_gla_w_in': out['v_gla_w_in'], 'v_gla_w_gate2': out['v_gla_w_gate2'], 'v_gla_b_gate': out['v_gla_b_gate'], 'v_gla_o_gain': out['v_gla_o_gain'], 'v_gla_w_out': out['v_gla_w_out'], 'v_sgu_w_in': out['v_sgu_w_in'], 'v_sgu_ln_gain': out['v_sgu_ln_gain'], 'v_sgu_ln_bias': out['v_sgu_ln_bias'], 'v_sgu_w_spatial': out['v_sgu_w_spatial'], 'v_sgu_b_spatial': out['v_sgu_b_spatial'], 'v_sgu_w_out': out['v_sgu_w_out']}


def _loss(weights, diff, rest, loss_target):
    with _jax.named_scope("forward"):
        args = {**rest, TWIN_DIFF_INPUT: diff, **{k: w.astype(_WEIGHT_DTYPES[k]) for k, w in weights.items()}}
        y = _forward(args)
    with _jax.named_scope("loss_head"):
        err = _jnp.square(y.astype(_jnp.float32) - loss_target)
        return 0.5 * _jnp.sum(_jnp.mean(err, axis=-1)) if err.ndim else 0.5 * err


def _adamw(w, g, m, v):
    m = ADAM_B1 * m + (1.0 - ADAM_B1) * g
    v = ADAM_B2 * v + (1.0 - ADAM_B2) * _jnp.square(g)
    m_hat = m / (1.0 - ADAM_B1 ** ADAM_STEP)
    v_hat = v / (1.0 - ADAM_B2 ** ADAM_STEP)
    delta = -ADAM_LR * (m_hat / (_jnp.sqrt(v_hat) + ADAM_EPS) + ADAM_WD * w)
    return delta, m, v


def reference(x, norm_pre, norm_post, gla_w_in, gla_w_gate2, gla_b_gate, gla_o_gain, gla_w_out, sgu_w_in, sgu_ln_gain, sgu_ln_bias, sgu_w_spatial, sgu_b_spatial, sgu_w_out, loss_target, m_norm_pre, m_norm_post, m_gla_w_in, m_gla_w_gate2, m_gla_b_gate, m_gla_o_gain, m_gla_w_out, m_sgu_w_in, m_sgu_ln_gain, m_sgu_ln_bias, m_sgu_w_spatial, m_sgu_b_spatial, m_sgu_w_out, v_norm_pre, v_norm_post, v_gla_w_in, v_gla_w_gate2, v_gla_b_gate, v_gla_o_gain, v_gla_w_out, v_sgu_w_in, v_sgu_ln_gain, v_sgu_ln_bias, v_sgu_w_spatial, v_sgu_b_spatial, v_sgu_w_out):
    given = dict(x=x, norm_pre=norm_pre, norm_post=norm_post, gla_w_in=gla_w_in, gla_w_gate2=gla_w_gate2, gla_b_gate=gla_b_gate, gla_o_gain=gla_o_gain, gla_w_out=gla_w_out, sgu_w_in=sgu_w_in, sgu_ln_gain=sgu_ln_gain, sgu_ln_bias=sgu_ln_bias, sgu_w_spatial=sgu_w_spatial, sgu_b_spatial=sgu_b_spatial, sgu_w_out=sgu_w_out, loss_target=loss_target, m_norm_pre=m_norm_pre, m_norm_post=m_norm_post, m_gla_w_in=m_gla_w_in, m_gla_w_gate2=m_gla_w_gate2, m_gla_b_gate=m_gla_b_gate, m_gla_o_gain=m_gla_o_gain, m_gla_w_out=m_gla_w_out, m_sgu_w_in=m_sgu_w_in, m_sgu_ln_gain=m_sgu_ln_gain, m_sgu_ln_bias=m_sgu_ln_bias, m_sgu_w_spatial=m_sgu_w_spatial, m_sgu_b_spatial=m_sgu_b_spatial, m_sgu_w_out=m_sgu_w_out, v_norm_pre=v_norm_pre, v_norm_post=v_norm_post, v_gla_w_in=v_gla_w_in, v_gla_w_gate2=v_gla_w_gate2, v_gla_b_gate=v_gla_b_gate, v_gla_o_gain=v_gla_o_gain, v_gla_w_out=v_gla_w_out, v_sgu_w_in=v_sgu_w_in, v_sgu_ln_gain=v_sgu_ln_gain, v_sgu_ln_bias=v_sgu_ln_bias, v_sgu_w_spatial=v_sgu_w_spatial, v_sgu_b_spatial=v_sgu_b_spatial, v_sgu_w_out=v_sgu_w_out)
    weights = {n: given[n] for n in TWIN_WEIGHTS}
    shared = {n: given[n] for n in SHARED_INPUTS}
    per_example = {n: given[n] for n in ['x']}
    grad_fn = _jax.value_and_grad(_loss, argnums=(0, 1))

    def one_microbatch(ex, loss_target):
        ex = dict(ex)
        diff = ex.pop(TWIN_DIFF_INPUT)
        return grad_fn(weights, diff, {**shared, **ex}, loss_target)

    if N_MICROBATCH == 1:
        loss, (grad_w, grad_x) = one_microbatch(per_example, given["loss_target"])
    else:
        def body(carry, xs):
            loss_sum, grad_sum = carry
            l_k, (gw_k, gx_k) = one_microbatch(xs[0], xs[1])
            with _jax.named_scope("update"):
                return (loss_sum + l_k, _jax.tree.map(_jnp.add, grad_sum, gw_k)), gx_k

        init = (_jnp.zeros((), _jnp.float32), _jax.tree.map(_jnp.zeros_like, weights))
        (loss, grad_w), grad_x = _jax.lax.scan(body, init, (per_example, given["loss_target"]))
    with _jax.named_scope("update"):
        delta_w, new_m, new_v = {}, {}, {}
        for n in TWIN_WEIGHTS:
            delta_w[n], new_m[n], new_v[n] = _adamw(weights[n], grad_w[n], given["m_" + n], given["v_" + n])
    return (loss, grad_x, *[grad_w[n] for n in TWIN_WEIGHTS], *[delta_w[n] for n in TWIN_WEIGHTS],
            *[new_m[n] for n in TWIN_WEIGHTS], *[new_v[n] for n in TWIN_WEIGHTS])
```

```python
import functools

import jax
import jax.numpy as jnp
from jax import lax
from jax.experimental import pallas as pl
from jax.experimental.pallas import tpu as pltpu

F32 = jnp.float32
BF16 = jnp.bfloat16

N_DEV = 8
S = 2048
D = 2048
H = 4
DK = 256
DV = 512
C = 64
NC = S // C
GLA_COLS = 6160
GLA_PAD = 6272
Q0, K0, V0, G0, LR0 = 0, 1024, 2048, 4096, 6144
LR = 16
LRP = 128
SGU_COLS = 6144
SGU_BLOCK = 128
SGU_G = 8
SGU_GD = 256
EPS = 1e-6
GLA_TAU = 16.0

ADAM_LR, ADAM_B1, ADAM_B2, ADAM_EPS, ADAM_WD, ADAM_STEP = 0.001, 0.9, 0.999, 1e-08, 0.01, 10

V7X_VMEM_BYTES = 64 * 1024 * 1024
VMEM_CEILING = V7X_VMEM_BYTES - 6 * 1024 * 1024
MESH = pl.DeviceIdType.MESH
HIGHEST = lax.Precision.HIGHEST


def _sds(shape, dtype):
    return jax.ShapeDtypeStruct(shape, dtype)


def _nbytes(shape, dtype):
    n = 1
    for s in shape:
        n *= s
    return n * jnp.dtype(dtype).itemsize


def _cparams(sem, block_bytes, scratch_bytes=0):
    est = 2 * block_bytes + scratch_bytes
    limit = min(VMEM_CEILING, max(32 * 1024 * 1024, 2 * est))
    return pltpu.CompilerParams(dimension_semantics=sem, vmem_limit_bytes=limit)


def _dot(a, b, dims=(((1,), (0,)), ((), ())), precision=None):
    return lax.dot_general(a, b, dims, precision=precision, preferred_element_type=F32)


NN = (((1,), (0,)), ((), ()))
TN = (((0,), (0,)), ((), ()))
NT = (((1,), (1,)), ((), ()))


def _mm(a, b, mode, out_dtype, *, tm, tn, tk, name):
    if mode == "nn":
        (m, k), (k2, n) = a.shape, b.shape
        a_blk, a_map = (tm, tk), (lambda i, j, kk: (i, kk))
        b_blk, b_map = (tk, tn), (lambda i, j, kk: (kk, j))
        dims = NN
    elif mode == "tn":
        (k, m), (k2, n) = a.shape, b.shape
        a_blk, a_map = (tk, tm), (lambda i, j, kk: (kk, i))
        b_blk, b_map = (tk, tn), (lambda i, j, kk: (kk, j))
        dims = TN
    else:
        (m, k), (n, k2) = a.shape, b.shape
        a_blk, a_map = (tm, tk), (lambda i, j, kk: (i, kk))
        b_blk, b_map = (tn, tk), (lambda i, j, kk: (j, kk))
        dims = NT
    assert k == k2 and m % tm == 0 and n % tn == 0 and k % tk == 0, (a.shape, b.shape, mode)
    nk = k // tk

    def body(a_ref, b_ref, o_ref, *scratch):
        p = _dot(a_ref[...], b_ref[...], dims)
        if nk == 1:
            o_ref[...] = p.astype(out_dtype)
        else:
            acc_ref, = scratch
            kk = pl.program_id(2)

            @pl.when(kk == 0)
            def _():
                acc_ref[...] = p

            @pl.when(kk > 0)
            def _():
                acc_ref[...] += p

            @pl.when(kk == nk - 1)
            def _():
                o_ref[...] = acc_ref[...].astype(out_dtype)

    blocks = _nbytes(a_blk, a.dtype) + _nbytes(b_blk, b.dtype) + _nbytes((tm, tn), out_dtype)
    scratch = [] if nk == 1 else [pltpu.VMEM((tm, tn), F32)]
    return pl.pallas_call(
        body,
        grid=(m // tm, n // tn, nk),
        in_specs=[pl.BlockSpec(a_blk, a_map), pl.BlockSpec(b_blk, b_map)],
        out_specs=pl.BlockSpec((tm, tn), lambda i, j, kk: (i, j)),
        out_shape=_sds((m, n), out_dtype),
        scratch_shapes=scratch,
        compiler_params=_cparams(("parallel", "parallel", "arbitrary"), blocks, _nbytes((tm, tn), F32) * (nk > 1)),
        name=name,
    )(a, b)


RB = 256


def _row_spec(width):
    return pl.BlockSpec((RB, width), lambda i: (i, 0))


def _vec_spec(width):
    return pl.BlockSpec((1, width), lambda i: (0, 0))


def _rinv(x):
    return lax.rsqrt(jnp.mean(x * x, axis=-1, keepdims=True) + EPS)


def _norm_bwd(dyn, xhat, r):
    return r * (dyn - xhat * jnp.mean(dyn * xhat, axis=-1, keepdims=True))


def _colsum(x):
    return jnp.sum(x, axis=0, keepdims=True)


def _accumulate(ref, value):
    @pl.when(pl.program_id(0) == 0)
    def _():
        ref[...] = value

    @pl.when(pl.program_id(0) > 0)
    def _():
        ref[...] += value


def _prenorm(x, gain):
    def body(x_ref, g_ref, h_ref):
        xv = x_ref[...]
        h_ref[...] = (xv * _rinv(xv) * g_ref[...]).astype(BF16)

    return pl.pallas_call(
        body, grid=(S // RB,), in_specs=[_row_spec(D), _vec_spec(D)], out_specs=_row_spec(D),
        out_shape=_sds((S, D), BF16),
        compiler_params=_cparams(("parallel",), RB * D * 6), name="prenorm",
    )(x, gain)


def _mid_fwd(x, y, npost, npre):
    def body(x_ref, y_ref, po_ref, pr_ref, x1_ref, h1_ref):
        yv = y_ref[...]
        x1 = x_ref[...] + yv * _rinv(yv) * po_ref[...]
        x1_ref[...] = x1
        h1_ref[...] = (x1 * _rinv(x1) * pr_ref[...]).astype(BF16)

    return pl.pallas_call(
        body, grid=(S // RB,), in_specs=[_row_spec(D), _row_spec(D), _vec_spec(D), _vec_spec(D)],
        out_specs=[_row_spec(D), _row_spec(D)], out_shape=[_sds((S, D), F32), _sds((S, D), BF16)],
        compiler_params=_cparams(("parallel",), RB * D * 14), name="mid_fwd",
    )(x, y, npost, npre)


def _final(x1, y1, tgt, npost):
    def body(x_ref, y_ref, t_ref, po_ref, loss_ref, dx_ref, dy_ref, dpo_ref):
        yv = y_ref[...]
        r = _rinv(yv)
        yhat = yv * r
        err = x_ref[...] + yhat * po_ref[...] - t_ref[...]
        dx = err * (1.0 / D)
        dx_ref[...] = dx
        dy_ref[...] = _norm_bwd(dx * po_ref[...], yhat, r).astype(BF16)
        _accumulate(loss_ref, _colsum(err * err))
        _accumulate(dpo_ref, _colsum(dx * yhat))

    return pl.pallas_call(
        body, grid=(S // RB,), in_specs=[_row_spec(D), _row_spec(D), _row_spec(D), _vec_spec(D)],
        out_specs=[_vec_spec(D), _row_spec(D), _row_spec(D), _vec_spec(D)],
        out_shape=[_sds((1, D), F32), _sds((S, D), F32), _sds((S, D), BF16), _sds((1, D), F32)],
        compiler_params=_cparams(("arbitrary",), RB * D * 18), name="final",
    )(x1, y1, tgt, npost)


def _mid_bwd(dx2, dh1, x1, y0, npre, npost):
    def body(dx2_ref, dh_ref, x_ref, y_ref, pr_ref, po_ref, dx1_ref, dy_ref, dpr_ref, dpo_ref):
        xv = x_ref[...]
        r = _rinv(xv)
        xhat = xv * r
        dh = dh_ref[...]
        dx1 = dx2_ref[...] + _norm_bwd(dh * pr_ref[...], xhat, r)
        dx1_ref[...] = dx1
        yv = y_ref[...]
        ry = _rinv(yv)
        yhat = yv * ry
        dy_ref[...] = _norm_bwd(dx1 * po_ref[...], yhat, ry).astype(BF16)
        _accumulate(dpr_ref, _colsum(dh * xhat))
        _accumulate(dpo_ref, _colsum(dx1 * yhat))

    return pl.pallas_call(
        body, grid=(S // RB,),
        in_specs=[_row_spec(D)] * 4 + [_vec_spec(D)] * 2,
        out_specs=[_row_spec(D), _row_spec(D), _vec_spec(D), _vec_spec(D)],
        out_shape=[_sds((S, D), F32), _sds((S, D), BF16), _sds((1, D), F32), _sds((1, D), F32)],
        compiler_params=_cparams(("arbitrary",), RB * D * 22), name="mid_bwd",
    )(dx2, dh1, x1, y0, npre, npost)


def _first_bwd(dx1, dh0, x0, npre):
    def body(dx1_ref, dh_ref, x_ref, pr_ref, gx_ref, dpr_ref):
        xv = x_ref[...]
        r = _rinv(xv)
        xhat = xv * r
        dh = dh_ref[...]
        gx_ref[...] = dx1_ref[...] + _norm_bwd(dh * pr_ref[...], xhat, r)
        _accumulate(dpr_ref, _colsum(dh * xhat))

    return pl.pallas_call(
        body, grid=(S // RB,), in_specs=[_row_spec(D)] * 3 + [_vec_spec(D)],
        out_specs=[_row_spec(D), _vec_spec(D)], out_shape=[_sds((S, D), F32), _sds((1, D), F32)],
        compiler_params=_cparams(("arbitrary",), RB * D * 16), name="first_bwd",
    )(dx1, dh0, x0, npre)


GLA_RB = 256
GLA_CPB = GLA_RB // C


def _sigmoid(x):
    return 1.0 / (1.0 + jnp.exp(-x))


def _tri(strict):
    r = lax.broadcasted_iota(jnp.int32, (C, C), 0)
    c = lax.broadcasted_iota(jnp.int32, (C, C), 1)
    return jnp.where(c < r if strict else c <= r, 1.0, 0.0).astype(F32)


def _gla_gate(glr_b, w2_h, b_h, tri):
    z = _dot(glr_b, w2_h) + b_h
    log_a = (jnp.minimum(z, 0.0) - jnp.log(1.0 + jnp.exp(-jnp.abs(z)))) * (1.0 / GLA_TAU)
    bcum = _dot(tri, log_a, precision=HIGHEST)
    b_end = jnp.sum(log_a, axis=0, keepdims=True)
    return z, jnp.exp(b_end - bcum), jnp.exp(b_end)


def _gla_fwd(proj, w2p, bgate, ogain):
    def body(p_ref, w2_ref, b_ref, og_ref, y_ref, st_out_ref, st_ref):
        @pl.when(pl.program_id(0) == 0)
        def _():
            st_ref[...] = jnp.zeros_like(st_ref)

        tri = _tri(False)

        def chunk(ci, carry):
            rows = pl.ds(pl.multiple_of(ci * C, C), C)
            glr_b = p_ref[rows, LR0:LR0 + LRP].astype(BF16)
            for h in range(H):
                _, ea, dec = _gla_gate(glr_b, w2_ref[:, h * DK:(h + 1) * DK], b_ref[:, h * DK:(h + 1) * DK], tri)
                k_dec = (p_ref[rows, K0 + h * DK:K0 + (h + 1) * DK] * ea).astype(BF16)
                v_b = p_ref[rows, V0 + h * DV:V0 + (h + 1) * DV].astype(BF16)
                st = st_ref[h] * dec + _dot(v_b, k_dec, TN)
                st_ref[h] = st
                st_b = st.astype(BF16)
                st_out_ref[ci, h] = st_b
                q_b = (p_ref[rows, Q0 + h * DK:Q0 + (h + 1) * DK] * (DK ** -0.5)).astype(BF16)
                o = _dot(q_b, st_b, NT)
                on = o * _rinv(o)
                g = p_ref[rows, G0 + h * DV:G0 + (h + 1) * DV]
                y_ref[rows, h * DV:(h + 1) * DV] = (on * og_ref[:, h * DV:(h + 1) * DV] * (g * _sigmoid(g))).astype(BF16)
            return carry

        lax.fori_loop(0, GLA_CPB, chunk, 0)

    blocks = GLA_RB * GLA_PAD * 4 + GLA_RB * D * 2 + GLA_CPB * H * DV * DK * 2
    return pl.pallas_call(
        body, grid=(S // GLA_RB,),
        in_specs=[pl.BlockSpec((GLA_RB, GLA_PAD), lambda i: (i, 0)),
                  pl.BlockSpec((LRP, H * DK), lambda i: (0, 0)),
                  pl.BlockSpec((1, H * DK), lambda i: (0, 0)),
                  pl.BlockSpec((1, H * DV), lambda i: (0, 0))],
        out_specs=[pl.BlockSpec((GLA_RB, H * DV), lambda i: (i, 0)),
                   pl.BlockSpec((GLA_CPB, H, DV, DK), lambda i: (i, 0, 0, 0))],
        out_shape=[_sds((S, H * DV), BF16), _sds((NC, H, DV, DK), BF16)],
        scratch_shapes=[pltpu.VMEM((H, DV, DK), F32)],
        compiler_params=_cparams(("arbitrary",), blocks, H * DV * DK * 4), name="gla_fwd",
    )(proj, w2p, bgate, ogain)


def _gla_bwd(proj, dypre, states, w2p, bgate, ogain):
    nb = S // GLA_RB

    def body(p_ref, dy_ref, st_blk_ref, st_prev_ref, w2_ref, b_ref, og_ref,
             dp_ref, dog_ref, dbg_ref, dw2_ref, r_ref):
        step = pl.program_id(0)

        @pl.when(step == 0)
        def _():
            r_ref[...] = jnp.zeros_like(r_ref)
            dog_ref[...] = jnp.zeros_like(dog_ref)
            dbg_ref[...] = jnp.zeros_like(dbg_ref)
            dw2_ref[...] = jnp.zeros_like(dw2_ref)

        tri = _tri(False)
        tri_strict = _tri(True)
        has_prev = jnp.where(step < nb - 1, 1.0, 0.0).astype(F32)

        def chunk(ci, st_prev_of):
            rows = pl.ds(ci * C if isinstance(ci, int) else pl.multiple_of(ci * C, C), C)
            glr_b = p_ref[rows, LR0:LR0 + LRP].astype(BF16)
            dglr = jnp.zeros((C, LRP), F32)
            for h in range(H):
                kcol = slice(h * DK, (h + 1) * DK)
                vcol = slice(h * DV, (h + 1) * DV)
                w2_h = w2_ref[:, kcol]
                z, ea, dec = _gla_gate(glr_b, w2_h, b_ref[:, kcol], tri)
                k_dec = p_ref[rows, K0 + h * DK:K0 + (h + 1) * DK] * ea
                k_dec_b = k_dec.astype(BF16)
                v_b = p_ref[rows, V0 + h * DV:V0 + (h + 1) * DV].astype(BF16)
                q_b = (p_ref[rows, Q0 + h * DK:Q0 + (h + 1) * DK] * (DK ** -0.5)).astype(BF16)
                st_b = st_blk_ref[ci, h]
                o = _dot(q_b, st_b, NT)
                rinv = _rinv(o)
                on = o * rinv
                g = p_ref[rows, G0 + h * DV:G0 + (h + 1) * DV]
                sg = _sigmoid(g)
                og = og_ref[:, vcol]
                dyp = dy_ref[rows, vcol]
                dp_ref[rows, G0 + h * DV:G0 + (h + 1) * DV] = (dyp * (on * og) * (sg * (1.0 + g * (1.0 - sg)))).astype(BF16)
                dpn = dyp * (g * sg)
                dog_ref[:, vcol] += _colsum(dpn * on)
                do_b = _norm_bwd(dpn * og, on, rinv).astype(BF16)
                gt = _dot(do_b, q_b, TN) + r_ref[h]
                gt_b = gt.astype(BF16)
                dp_ref[rows, Q0 + h * DK:Q0 + (h + 1) * DK] = (_dot(do_b, st_b) * (DK ** -0.5)).astype(BF16)
                dkd = _dot(v_b, gt_b)
                dp_ref[rows, V0 + h * DV:V0 + (h + 1) * DV] = _dot(k_dec_b, gt_b, NT).astype(BF16)
                dp_ref[rows, K0 + h * DK:K0 + (h + 1) * DK] = (dkd * ea).astype(BF16)
                ddec = _colsum(gt * st_prev_of(h))
                dla = _dot(tri_strict, dkd * k_dec, precision=HIGHEST) + ddec * dec
                dz = dla * (1.0 / GLA_TAU) * (1.0 - _sigmoid(z))
                dz_b = dz.astype(BF16)
                r_ref[h] = gt * dec
                dbg_ref[:, kcol] += _colsum(dz)
                dw2_ref[:, kcol] += _dot(glr_b, dz_b, TN)
                dglr = dglr + _dot(dz_b, w2_h, NT)
            dp_ref[rows, LR0:LR0 + LRP] = dglr.astype(BF16)

        def later_chunk(t, carry):
            ci = GLA_CPB - 1 - t
            chunk(ci, lambda h: st_blk_ref[ci - 1, h].astype(F32))
            return carry

        lax.fori_loop(0, GLA_CPB - 1, later_chunk, 0)
        chunk(0, lambda h: st_prev_ref[0, h].astype(F32) * has_prev)

    blocks = (GLA_RB * GLA_PAD * 4 + GLA_RB * D * 4 + (GLA_CPB + 1) * H * DV * DK * 2 + GLA_RB * GLA_PAD * 2)
    rev = lambda i: nb - 1 - i
    return pl.pallas_call(
        body, grid=(nb,),
        in_specs=[pl.BlockSpec((GLA_RB, GLA_PAD), lambda i: (rev(i), 0)),
                  pl.BlockSpec((GLA_RB, H * DV), lambda i: (rev(i), 0)),
                  pl.BlockSpec((GLA_CPB, H, DV, DK), lambda i: (rev(i), 0, 0, 0)),
                  pl.BlockSpec((1, H, DV, DK), lambda i: (jnp.maximum(rev(i) * GLA_CPB - 1, 0), 0, 0, 0)),
                  pl.BlockSpec((LRP, H * DK), lambda i: (0, 0)),
                  pl.BlockSpec((1, H * DK), lambda i: (0, 0)),
                  pl.BlockSpec((1, H * DV), lambda i: (0, 0))],
        out_specs=[pl.BlockSpec((GLA_RB, GLA_PAD), lambda i: (rev(i), 0)),
                   pl.BlockSpec((1, H * DV), lambda i: (0, 0)),
                   pl.BlockSpec((1, H * DK), lambda i: (0, 0)),
                   pl.BlockSpec((LRP, H * DK), lambda i: (0, 0))],
        out_shape=[_sds((S, GLA_PAD), BF16), _sds((1, H * DV), F32), _sds((1, H * DK), F32), _sds((LRP, H * DK), F32)],
        scratch_shapes=[pltpu.VMEM((H, DV, DK), F32)],
        compiler_params=_cparams(("arbitrary",), blocks, H * DV * DK * 4), name="gla_bwd",
    )(proj, dypre, states, states, w2p, bgate, ogain)


SGU_RB = 256
GELU_C = 0.7978845608028654
GELU_A = 0.044715


def _gelu(x):
    return 0.5 * x * (1.0 + jnp.tanh(GELU_C * (x + GELU_A * x * x * x)))


def _gelu_grad(x):
    t = jnp.tanh(GELU_C * (x + GELU_A * x * x * x))
    return 0.5 * (1.0 + t) + 0.5 * x * (1.0 - t * t) * (GELU_C * (1.0 + 3.0 * GELU_A * x * x))


def _causal_mask(transposed=False):
    i = lax.broadcasted_iota(jnp.int32, (SGU_BLOCK, SGU_BLOCK), 1 if transposed else 0)
    j = lax.broadcasted_iota(jnp.int32, (SGU_BLOCK, SGU_BLOCK), 0 if transposed else 1)
    return (i >= C) | (j < C)


def _layer_norm(vf, gain, bias):
    mu = jnp.mean(vf, axis=-1, keepdims=True)
    cen = vf - mu
    rstd = lax.rsqrt(jnp.mean(cen * cen, axis=-1, keepdims=True) + EPS)
    xhat = cen * rstd
    return xhat, rstd, xhat * gain + bias


def _sgu_fwd(proj, lng, lnb, ws, bsb):
    def body(p_ref, g_ref, b_ref, ws_ref, bs_ref, o_ref):
        mask = _causal_mask()
        for n in range(SGU_RB // SGU_BLOCK):
            rows = slice(n * SGU_BLOCK, (n + 1) * SGU_BLOCK)
            _, _, vn = _layer_norm(_gelu(p_ref[rows, D:2 * D]), g_ref[...], b_ref[...])
            vn_b = vn.astype(BF16)
            for gi in range(SGU_G):
                cols = slice(gi * SGU_GD, (gi + 1) * SGU_GD)
                w = jnp.where(mask, ws_ref[gi], 0.0).astype(BF16)
                vs = _dot(w, vn_b[:, cols]) + bs_ref[gi]
                gate = p_ref[rows, 2 * D + gi * SGU_GD:2 * D + (gi + 1) * SGU_GD]
                o_ref[rows, cols] = (_gelu(p_ref[rows, cols]) * vs * (gate * _sigmoid(gate))).astype(BF16)

    blocks = SGU_RB * SGU_COLS * 4 + SGU_RB * D * 2 + SGU_G * SGU_BLOCK * (SGU_BLOCK + SGU_GD) * 4
    return pl.pallas_call(
        body, grid=(S // SGU_RB,),
        in_specs=[pl.BlockSpec((SGU_RB, SGU_COLS), lambda i: (i, 0)),
                  pl.BlockSpec((1, D), lambda i: (0, 0)), pl.BlockSpec((1, D), lambda i: (0, 0)),
                  pl.BlockSpec((SGU_G, SGU_BLOCK, SGU_BLOCK), lambda i: (0, 0, 0)),
                  pl.BlockSpec((SGU_G, SGU_BLOCK, SGU_GD), lambda i: (0, 0, 0))],
        out_specs=pl.BlockSpec((SGU_RB, D), lambda i: (i, 0)),
        out_shape=_sds((S, D), BF16),
        compiler_params=_cparams(("parallel",), blocks), name="sgu_fwd",
    )(proj, lng, lnb, ws, bsb)


def _sgu_bwd(proj, dpre, lng, lnb, ws, wst, bsb):
    nsteps = S // SGU_RB

    def body(p_ref, d_ref, g_ref, b_ref, ws_ref, wst_ref, bs_ref,
             dp_ref, dg_ref, db_ref, dws_ref, dbs_ref, dvn_ref, dvs_acc_ref):
        step = pl.program_id(0)

        @pl.when(step == 0)
        def _():
            dg_ref[...] = jnp.zeros_like(dg_ref)
            db_ref[...] = jnp.zeros_like(db_ref)
            dws_ref[...] = jnp.zeros_like(dws_ref)
            dvs_acc_ref[...] = jnp.zeros_like(dvs_acc_ref)

        mask = _causal_mask()
        maskt = _causal_mask(transposed=True)
        for n in range(SGU_RB // SGU_BLOCK):
            rows = slice(n * SGU_BLOCK, (n + 1) * SGU_BLOCK)
            v = p_ref[rows, D:2 * D]
            xhat, rstd, vn = _layer_norm(_gelu(v), g_ref[...], b_ref[...])
            vn_b = vn.astype(BF16)
            for gi in range(SGU_G):
                cols = slice(gi * SGU_GD, (gi + 1) * SGU_GD)
                w = jnp.where(mask, ws_ref[gi], 0.0).astype(BF16)
                wt = jnp.where(maskt, wst_ref[gi], 0.0).astype(BF16)
                vs = _dot(w, vn_b[:, cols]) + bs_ref[gi]
                u = p_ref[rows, cols]
                gate = p_ref[rows, 2 * D + gi * SGU_GD:2 * D + (gi + 1) * SGU_GD]
                sg = _sigmoid(gate)
                gu = _gelu(u)
                dpre_g = d_ref[rows, cols]
                t = dpre_g * (gate * sg)
                dp_ref[rows, cols] = (t * vs * _gelu_grad(u)).astype(BF16)
                dp_ref[rows, 2 * D + gi * SGU_GD:2 * D + (gi + 1) * SGU_GD] = (
                    dpre_g * gu * vs * (sg * (1.0 + gate * (1.0 - sg)))).astype(BF16)
                dvs = t * gu
                dvs_b = dvs.astype(BF16)
                dvs_acc_ref[:, cols] += dvs
                dws_ref[gi] += _dot(dvs_b, vn_b[:, cols], NT)
                dvn_ref[:, cols] = _dot(wt, dvs_b)
            dvn = dvn_ref[...]
            dg_ref[...] += _colsum(dvn * xhat)
            db_ref[...] += _colsum(dvn)
            dxh = dvn * g_ref[...]
            dvf = rstd * (dxh - jnp.mean(dxh, axis=-1, keepdims=True) - xhat * jnp.mean(dxh * xhat, axis=-1, keepdims=True))
            dp_ref[rows, D:2 * D] = (dvf * _gelu_grad(v)).astype(BF16)

        @pl.when(step == nsteps - 1)
        def _():
            lane = lax.broadcasted_iota(jnp.int32, (SGU_BLOCK, SGU_BLOCK), 1)
            out = jnp.zeros((SGU_BLOCK, SGU_BLOCK), F32)
            for gi in range(SGU_G):
                out = out + jnp.where(lane == gi, jnp.sum(dvs_acc_ref[:, gi * SGU_GD:(gi + 1) * SGU_GD], axis=1, keepdims=True), 0.0)
                dws_ref[gi] = jnp.where(mask, dws_ref[gi], 0.0)
            dbs_ref[...] = out

    blocks = SGU_RB * SGU_COLS * 6 + SGU_RB * D * 4 + SGU_G * SGU_BLOCK * (3 * SGU_BLOCK + SGU_GD) * 4
    const3 = lambda i: (0, 0, 0)
    return pl.pallas_call(
        body, grid=(nsteps,),
        in_specs=[pl.BlockSpec((SGU_RB, SGU_COLS), lambda i: (i, 0)),
                  pl.BlockSpec((SGU_RB, D), lambda i: (i, 0)),
                  pl.BlockSpec((1, D), lambda i: (0, 0)), pl.BlockSpec((1, D), lambda i: (0, 0)),
                  pl.BlockSpec((SGU_G, SGU_BLOCK, SGU_BLOCK), const3),
                  pl.BlockSpec((SGU_G, SGU_BLOCK, SGU_BLOCK), const3),
                  pl.BlockSpec((SGU_G, SGU_BLOCK, SGU_GD), const3)],
        out_specs=[pl.BlockSpec((SGU_RB, SGU_COLS), lambda i: (i, 0)),
                   pl.BlockSpec((1, D), lambda i: (0, 0)), pl.BlockSpec((1, D), lambda i: (0, 0)),
                   pl.BlockSpec((SGU_G, SGU_BLOCK, SGU_BLOCK), const3),
                   pl.BlockSpec((SGU_BLOCK, SGU_BLOCK), lambda i: (0, 0))],
        out_shape=[_sds((S, SGU_COLS), BF16), _sds((1, D), F32), _sds((1, D), F32),
                   _sds((SGU_G, SGU_BLOCK, SGU_BLOCK), F32), _sds((SGU_BLOCK, SGU_BLOCK), F32)],
        scratch_shapes=[pltpu.VMEM((SGU_BLOCK, D), F32), pltpu.VMEM((SGU_BLOCK, D), F32)],
        compiler_params=_cparams(("arbitrary",), blocks, 2 * SGU_BLOCK * D * 4), name="sgu_bwd",
    )(proj, dpre, lng, lnb, ws, wst, bsb)


def _adamw_math(w, g, m, v):
    m = ADAM_B1 * m + (1.0 - ADAM_B1) * g
    v = ADAM_B2 * v + (1.0 - ADAM_B2) * (g * g)
    m_hat = m / (1.0 - ADAM_B1 ** ADAM_STEP)
    v_hat = v / (1.0 - ADAM_B2 ** ADAM_STEP)
    delta = -ADAM_LR * (m_hat / (jnp.sqrt(v_hat) + ADAM_EPS) + ADAM_WD * w)
    return delta, m, v


def _sum_adamw(parts, w, m, v, *, tr, name):
    r, c = w.shape
    assert r % tr == 0

    def body(p_ref, w_ref, m_ref, v_ref, g_ref, d_ref, nm_ref, nv_ref):
        g = p_ref[0].astype(F32)
        for j in range(1, N_DEV):
            g = g + p_ref[j].astype(F32)
        g_ref[...] = g
        d_ref[...], nm_ref[...], nv_ref[...] = _adamw_math(w_ref[...], g, m_ref[...], v_ref[...])

    spec = pl.BlockSpec((tr, c), lambda i: (i, 0))
    cpad = -(-c // 128) * 128
    blocks = N_DEV * tr * cpad * jnp.dtype(parts.dtype).itemsize + 7 * tr * cpad * 4
    return pl.pallas_call(
        body, grid=(r // tr,),
        in_specs=[pl.BlockSpec((N_DEV, tr, c), lambda i: (0, i, 0)), spec, spec, spec],
        out_specs=[spec] * 4, out_shape=[_sds((r, c), F32)] * 4,
        compiler_params=_cparams(("parallel",), blocks), name=name,
    )(parts, w, m, v)


def _sum_parts(parts, name):
    n, r, c = parts.shape

    def body(p_ref, o_ref):
        g = p_ref[0]
        for j in range(1, n):
            g = g + p_ref[j]
        o_ref[...] = g

    return pl.pallas_call(
        body, grid=(1,), in_specs=[pl.BlockSpec((n, r, c), lambda i: (0, 0, 0))],
        out_specs=pl.BlockSpec((r, c), lambda i: (0, 0)), out_shape=_sds((r, c), F32),
        compiler_params=_cparams(("arbitrary",), (n + 1) * r * c * 4), name=name,
    )(parts)


def _adamw(w, g, m, v, name):
    def body(w_ref, g_ref, m_ref, v_ref, d_ref, nm_ref, nv_ref):
        d_ref[...], nm_ref[...], nv_ref[...] = _adamw_math(w_ref[...], g_ref[...], m_ref[...], v_ref[...])

    spec = pl.BlockSpec(w.shape, lambda i: (0, 0))
    return pl.pallas_call(
        body, grid=(1,), in_specs=[spec] * 4, out_specs=[spec] * 3, out_shape=[_sds(w.shape, F32)] * 3,
        compiler_params=_cparams(("arbitrary",), 7 * _nbytes(w.shape, F32)), name=name,
    )(w, g, m, v)


def _place():
    return lax.axis_index("x"), lax.axis_index("y"), lax.axis_index("c")


def _index_of(px, py, pc):
    return 4 * px + 2 * py + pc


HBM_SPEC = pl.BlockSpec(memory_space=pl.ANY)


def _all_gather(shards, name):
    n = len(shards)

    def body(*refs):
        ins, outs = refs[:n], refs[n:2 * n]
        send_sems, recv_sems, local_sems = refs[2 * n:]
        x, y, c = _place()
        me, sibling = (x, y, c), (x, y, 1 - c)
        chips = [(1 - x, y), (x, 1 - y), (1 - x, 1 - y)]

        def copy(i, k, block, to, src=None):
            dst = outs[i].at[_index_of(*block)]
            return pltpu.make_async_remote_copy(
                src_ref=dst if src is None else src, dst_ref=dst,
                send_sem=send_sems.at[i * 7 + k], recv_sem=recv_sems.at[i * 7 + k],
                device_id=to, device_id_type=MESH)

        own = [pltpu.make_async_copy(ins[i], outs[i].at[_index_of(*me)], local_sems.at[i]) for i in range(n)]
        for cp in own:
            cp.start()
        first = []
        for i in range(n):
            first.append(copy(i, 0, me, sibling, src=ins[i]))
            first += [copy(i, 1 + j, me, (*chip, c), src=ins[i]) for j, chip in enumerate(chips)]
        for cp in first:
            cp.start()
        passed = []
        for i in range(n):
            for j, chip in enumerate(chips):
                copy(i, 1 + j, (*chip, c), me).wait_recv()
                cp = copy(i, 4 + j, (*chip, c), sibling)
                cp.start()
                passed.append(cp)
        for i in range(n):
            copy(i, 0, sibling, me).wait_recv()
            for j, chip in enumerate(chips):
                copy(i, 4 + j, (*chip, 1 - c), me).wait_recv()
        for cp in first + passed:
            cp.wait_send()
        for cp in own:
            cp.wait()

    return pl.pallas_call(
        body,
        in_specs=[HBM_SPEC] * n, out_specs=[HBM_SPEC] * n,
        out_shape=[_sds((N_DEV,) + s.shape, s.dtype) for s in shards],
        scratch_shapes=[pltpu.SemaphoreType.DMA((7 * n,)), pltpu.SemaphoreType.DMA((7 * n,)),
                        pltpu.SemaphoreType.DMA((n,))],
        name=name,
    )(*shards)


def _exchange(parts, name):
    n = len(parts)

    def body(*refs):
        ins, outs = refs[:n], refs[n:2 * n]
        send_sems, recv_sems, local_sems = refs[2 * n:]
        x, y, c = _place()
        me = _index_of(x, y, c)

        def peer(k):
            return (1 - x if k & 4 else x, 1 - y if k & 2 else y, 1 - c if k & 1 else c)

        def copy(i, k):
            to = peer(k)
            return pltpu.make_async_remote_copy(
                src_ref=ins[i].at[_index_of(*to)], dst_ref=outs[i].at[me],
                send_sem=send_sems.at[i * 7 + k - 1], recv_sem=recv_sems.at[i * 7 + k - 1],
                device_id=to, device_id_type=MESH)

        def arrival(i, k):
            return pltpu.make_async_remote_copy(
                src_ref=ins[i].at[me], dst_ref=outs[i].at[_index_of(*peer(k))],
                send_sem=send_sems.at[i * 7 + k - 1], recv_sem=recv_sems.at[i * 7 + k - 1],
                device_id=peer(k), device_id_type=MESH)

        own = [pltpu.make_async_copy(ins[i].at[me], outs[i].at[me], local_sems.at[i]) for i in range(n)]
        for cp in own:
            cp.start()
        sends = [copy(i, k) for i in range(n) for k in range(1, 8)]
        for cp in sends:
            cp.start()
        for i in range(n):
            for k in range(1, 8):
                arrival(i, k).wait_recv()
        for cp in sends:
            cp.wait_send()
        for cp in own:
            cp.wait()

    return pl.pallas_call(
        body,
        in_specs=[HBM_SPEC] * n, out_specs=[HBM_SPEC] * n,
        out_shape=[_sds(p.shape, p.dtype) for p in parts],
        scratch_shapes=[pltpu.SemaphoreType.DMA((7 * n,)), pltpu.SemaphoreType.DMA((7 * n,)),
                        pltpu.SemaphoreType.DMA((n,))],
        name=name,
    )(*parts)


def _blocks_to_columns(g):
    n, r, c = g.shape
    return jnp.transpose(g, (1, 0, 2)).reshape(r, n * c)


def _columns_to_blocks(w, c):
    r = w.shape[0]
    return jnp.transpose(w.reshape(r, N_DEV, c), (1, 0, 2))


def kernel(x, norm_pre, norm_post, gla_w_in, gla_w_gate2, gla_b_gate, gla_o_gain, gla_w_out, sgu_w_in, sgu_ln_gain, sgu_ln_bias, sgu_w_spatial, sgu_b_spatial, sgu_w_out, loss_target, m_norm_pre, m_norm_post, m_gla_w_in, m_gla_w_gate2, m_gla_b_gate, m_gla_o_gain, m_gla_w_out, m_sgu_w_in, m_sgu_ln_gain, m_sgu_ln_bias, m_sgu_w_spatial, m_sgu_b_spatial, m_sgu_w_out, v_norm_pre, v_norm_post, v_gla_w_in, v_gla_w_gate2, v_gla_b_gate, v_gla_o_gain, v_gla_w_out, v_sgu_w_in, v_sgu_ln_gain, v_sgu_ln_bias, v_sgu_w_spatial, v_sgu_b_spatial, v_sgu_w_out):
    me = _index_of(*_place())
    x0 = x.reshape(S, D)
    tgt = loss_target.reshape(S, D)

    small = jnp.concatenate([jnp.pad(gla_w_gate2[0].reshape(4, 512), ((0, 4), (0, 0))),
                             jnp.pad(jnp.concatenate([sgu_ln_gain, sgu_ln_bias], axis=1), ((0, 7), (0, 0)))], axis=0)
    gathered = _all_gather([gla_w_in[0].astype(BF16), gla_w_out[0].astype(BF16),
                            sgu_w_in[0].astype(BF16), sgu_w_out[0].astype(BF16), small], "gather_weights")
    wg_in = jnp.pad(_blocks_to_columns(gathered[0]), ((0, 0), (0, GLA_PAD - GLA_COLS)))
    wg_out = gathered[1].reshape(D, D)
    ws_in = _blocks_to_columns(gathered[2])
    ws_out = gathered[3].reshape(D, D)
    w2 = _blocks_to_columns(gathered[4][:, :4, :].reshape(N_DEV, LR, 128))
    w2p = jnp.pad(w2, ((0, LRP - LR), (0, 0))).astype(BF16)
    lng = gathered[4][:, 8, :256].reshape(1, D)
    lnb = gathered[4][:, 8, 256:].reshape(1, D)
    (loss_cols, grad_x, dwg_in, dwg_out, dws_in, dws_out, dnpre, dnpost, dbgate, dogain, dw2, dlng, dlnb, dwsp, dbsp) = _local_step(
        x0, tgt, norm_pre, norm_post, wg_in, wg_out, ws_in, ws_out, w2p, gla_b_gate, gla_o_gain, lng, lnb,
        sgu_w_spatial[0], sgu_b_spatial[0])
    loss = lax.psum(0.5 * jnp.sum(loss_cols) / D, ("x", "y", "c"))

    recv = _exchange([_columns_to_blocks(dwg_in[:, :GLA_COLS], GLA_COLS // N_DEV), dwg_out.reshape(N_DEV, D // N_DEV, D),
                      _columns_to_blocks(dws_in, SGU_COLS // N_DEV), dws_out.reshape(N_DEV, D // N_DEV, D)], "exchange_grads")
    g_gwi, d_gwi, nm_gwi, nv_gwi = _sum_adamw(recv[0], gla_w_in[0], m_gla_w_in[0], v_gla_w_in[0], tr=256, name="adamw_gla_w_in")
    g_gwo, d_gwo, nm_gwo, nv_gwo = _sum_adamw(recv[1], gla_w_out[0], m_gla_w_out[0], v_gla_w_out[0], tr=256, name="adamw_gla_w_out")
    g_swi, d_swi, nm_swi, nv_swi = _sum_adamw(recv[2], sgu_w_in[0], m_sgu_w_in[0], v_sgu_w_in[0], tr=256, name="adamw_sgu_w_in")
    g_swo, d_swo, nm_swo, nv_swo = _sum_adamw(recv[3], sgu_w_out[0], m_sgu_w_out[0], v_sgu_w_out[0], tr=256, name="adamw_sgu_w_out")

    pieces = [dnpre, dnpost, dbgate, dogain, dw2[:LR], dlng, dlnb, dwsp, jnp.transpose(dbsp[:, :SGU_G])]
    flat = jnp.concatenate([p.reshape(-1) for p in pieces])
    total = _sum_parts(_all_gather([flat.reshape(-1, 128)], "gather_small_grads")[0], "sum_small_grads").reshape(-1)
    grads, off = [], 0
    for p in pieces:
        grads.append(total[off:off + p.size].reshape(p.shape))
        off += p.size
    g_npre, g_npost, g_bgate, g_ogain, g_w2_full, g_lng_full, g_lnb_full, g_wsp, g_bsp = grads
    g_w2 = lax.dynamic_slice(g_w2_full, (0, me * 128), (LR, 128))
    g_lng = lax.dynamic_slice(g_lng_full, (0, me * 256), (1, 256))
    g_lnb = lax.dynamic_slice(g_lnb_full, (0, me * 256), (1, 256))
    small_g = [g_npre, g_npost, g_w2, g_bgate, g_ogain, g_lng, g_lnb, g_wsp, g_bsp]
    small_w = [norm_pre, norm_post, gla_w_gate2[0], gla_b_gate, gla_o_gain, sgu_ln_gain, sgu_ln_bias, sgu_w_spatial[0], sgu_b_spatial[0]]
    small_m = [m_norm_pre, m_norm_post, m_gla_w_gate2[0], m_gla_b_gate, m_gla_o_gain, m_sgu_ln_gain, m_sgu_ln_bias, m_sgu_w_spatial[0], m_sgu_b_spatial[0]]
    small_v = [v_norm_pre, v_norm_post, v_gla_w_gate2[0], v_gla_b_gate, v_gla_o_gain, v_sgu_ln_gain, v_sgu_ln_bias, v_sgu_w_spatial[0], v_sgu_b_spatial[0]]
    pack = lambda parts: jnp.concatenate([p.reshape(-1) for p in parts]).reshape(-1, 128)
    d_pack, nm_pack, nv_pack = _adamw(pack(small_w), pack(small_g), pack(small_m), pack(small_v), "adamw_small")

    def unpack(packed, like):
        flat_p, outs, o = packed.reshape(-1), [], 0
        for p in like:
            outs.append(flat_p[o:o + p.size].reshape(p.shape))
            o += p.size
        return outs

    out_shapes = [norm_pre, norm_post, gla_w_gate2, gla_b_gate, gla_o_gain, sgu_ln_gain, sgu_ln_bias, sgu_w_spatial, sgu_b_spatial]
    sg_ = [g.reshape(s.shape) for g, s in zip(small_g, out_shapes)]
    sd_, sm_, sv_ = (unpack(pk, out_shapes) for pk in (d_pack, nm_pack, nv_pack))

    def assemble(small_list, w_in_g, w_out_g, w_in_s, w_out_s):
        npre_, npost_, w2_, bg_, og_, lg_, lb_, wsp_, bsp_ = small_list
        return [npre_, npost_, w_in_g[None], w2_, bg_, og_, w_out_g[None], w_in_s[None], lg_, lb_, wsp_, bsp_, w_out_s[None]]

    return (loss, grad_x.reshape(1, S, D),
            *assemble(sg_, g_gwi, g_gwo, g_swi, g_swo),
            *assemble(sd_, d_gwi, d_gwo, d_swi, d_swo),
            *assemble(sm_, nm_gwi, nm_gwo, nm_swi, nm_swo),
            *assemble(sv_, nv_gwi, nv_gwo, nv_swi, nv_swo))


def _local_step(x0, tgt, norm_pre, norm_post, wg_in, wg_out, ws_in, ws_out, w2p, gla_b_gate, gla_o_gain, lng, lnb, ws, bs):
    wst = jnp.transpose(ws, (0, 2, 1))
    bsb = jnp.broadcast_to(bs[:, :, None], (SGU_G, SGU_BLOCK, SGU_GD))
    npre0, npre1 = norm_pre[0:1], norm_pre[1:2]
    npost0, npost1 = norm_post[0:1], norm_post[1:2]

    h0 = _prenorm(x0, npre0)
    proj0 = _mm(h0, wg_in, "nn", F32, tm=1024, tn=896, tk=D, name="gla_in")
    ypre0, states = _gla_fwd(proj0, w2p, gla_b_gate, gla_o_gain)
    y0 = _mm(ypre0, wg_out, "nn", F32, tm=1024, tn=1024, tk=D, name="gla_out")
    x1, h1 = _mid_fwd(x0, y0, npost0, npre1)
    proj1 = _mm(h1, ws_in, "nn", F32, tm=1024, tn=1024, tk=D, name="sgu_in")
    pre1 = _sgu_fwd(proj1, lng, lnb, ws, bsb)
    y1 = _mm(pre1, ws_out, "nn", F32, tm=1024, tn=1024, tk=D, name="sgu_out")
    loss_cols, dx2, dy1, dnpost1 = _final(x1, y1, tgt, npost1)

    dws_out = _mm(pre1, dy1, "tn", BF16, tm=1024, tn=1024, tk=S, name="sgu_out_dw")
    dpre1 = _mm(dy1, ws_out, "nt", F32, tm=1024, tn=1024, tk=D, name="sgu_out_dx")
    dproj1, dlng, dlnb, dwsp, dbsp = _sgu_bwd(proj1, dpre1, lng, lnb, ws, wst, bsb)
    dws_in = _mm(h1, dproj1, "tn", BF16, tm=1024, tn=1024, tk=S, name="sgu_in_dw")
    dh1 = _mm(dproj1, ws_in, "nt", F32, tm=1024, tn=1024, tk=1024, name="sgu_in_dx")
    dx1, dy0, dnpre1, dnpost0 = _mid_bwd(dx2, dh1, x1, y0, npre1, npost0)
    dwg_out = _mm(ypre0, dy0, "tn", BF16, tm=1024, tn=1024, tk=S, name="gla_out_dw")
    dypre0 = _mm(dy0, wg_out, "nt", F32, tm=1024, tn=1024, tk=D, name="gla_out_dx")
    dproj0, dogain, dbgate, dw2 = _gla_bwd(proj0, dypre0, states, w2p, gla_b_gate, gla_o_gain)
    dwg_in = _mm(h0, dproj0, "tn", BF16, tm=1024, tn=896, tk=S, name="gla_in_dw")
    dh0 = _mm(dproj0, wg_in, "nt", F32, tm=1024, tn=1024, tk=896, name="gla_in_dx")
    grad_x, dnpre0 = _first_bwd(dx1, dh0, x0, npre0)
    return (loss_cols, grad_x, dwg_in, dwg_out, dws_in, dws_out,
            jnp.concatenate([dnpre0, dnpre1], 0), jnp.concatenate([dnpost0, dnpost1], 0),
            dbgate, dogain, dw2, dlng, dlnb, dwsp, dbsp)
```

```python
import functools

import jax
import jax.numpy as jnp
from jax import lax
from jax.experimental import pallas as pl
from jax.experimental.pallas import tpu as pltpu

F32 = jnp.float32
BF16 = jnp.bfloat16

N_DEV = 8
S = 2048
D = 2048
H = 4
DK = 256
DV = 512
C = 64
NC = S // C
GLA_COLS = 6160
GLA_PAD = 6272
Q0, K0, V0, G0, LR0 = 0, 1024, 2048, 4096, 6144
LR = 16
LRP = 128
SGU_COLS = 6144
SGU_BLOCK = 128
SGU_G = 8
SGU_GD = 256
EPS = 1e-6
GLA_TAU = 16.0

ADAM_LR, ADAM_B1, ADAM_B2, ADAM_EPS, ADAM_WD, ADAM_STEP = 0.001, 0.9, 0.999, 1e-08, 0.01, 10

V7X_VMEM_BYTES = 64 * 1024 * 1024
VMEM_CEILING = V7X_VMEM_BYTES - 6 * 1024 * 1024
MESH = pl.DeviceIdType.MESH
HIGHEST = lax.Precision.HIGHEST
HBM_SPEC = pl.BlockSpec(memory_space=pl.ANY)


def _sds(shape, dtype):
    return jax.ShapeDtypeStruct(tuple(shape), dtype)


def _nbytes(shape, dtype):
    n = 1
    for s in shape:
        n *= s
    return n * jnp.dtype(dtype).itemsize


def _dot(a, b, dims=(((1,), (0,)), ((), ())), precision=None):
    return lax.dot_general(a, b, dims, precision=precision, preferred_element_type=F32)


NN = (((1,), (0,)), ((), ()))
TN = (((0,), (0,)), ((), ()))
NT = (((1,), (1,)), ((), ()))


def _place():
    return lax.axis_index("x"), lax.axis_index("y"), lax.axis_index("c")


def _index_of(px, py, pc):
    return 4 * px + 2 * py + pc


def _chips(x, y):
    return [(1 - x, y), (x, 1 - y), (1 - x, 1 - y)]


def _rcopy(src, dst, send_sem, recv_sem, to):
    return pltpu.make_async_remote_copy(src_ref=src, dst_ref=dst, send_sem=send_sem, recv_sem=recv_sem,
                                        device_id=to, device_id_type=MESH)


class _Move:
    def __init__(self, ins, n_remote, n_local, make):
        self.ins, self.n_remote, self.n_local, self.make = list(ins), n_remote, n_local, make

    def start(self, in_refs, buf, sems):
        sends, _, local = self.make(in_refs, buf, *sems)
        for cp in local + sends:
            cp.start()

    def finish(self, in_refs, buf, sems):
        sends, arrivals, local = self.make(in_refs, buf, *sems)
        for cp in arrivals:
            cp.wait_recv()
        for cp in sends:
            cp.wait_send()
        for cp in local:
            cp.wait()


class _Phase:
    def __init__(self, like, so_far, moves):
        self.like, self.so_far, self.moves = like, so_far, list(moves)


def _gather_send(shard, r0, nr):
    def make(in_refs, g, ss, rs, ls):
        sh, = in_refs
        x, y, c = _place()
        me = _index_of(x, y, c)
        rows = pl.ds(r0, nr)
        peers = [(x, y, 1 - c)] + [(px, py, c) for px, py in _chips(x, y)]
        sends = [_rcopy(sh.at[rows], g.at[me, rows], ss.at[k], rs.at[k], p) for k, p in enumerate(peers)]
        arrivals = [_rcopy(sh.at[rows], g.at[_index_of(*p), rows], ss.at[k], rs.at[k], p) for k, p in enumerate(peers)]
        return sends, arrivals, [pltpu.make_async_copy(sh.at[rows], g.at[me, rows], ls.at[0])]

    return _Move([shard], 4, 1, make)


def _gather_pass(r0, nr):
    def make(in_refs, g, ss, rs, ls):
        x, y, c = _place()
        rows = pl.ds(r0, nr)
        sends = [_rcopy(g.at[_index_of(px, py, c), rows], g.at[_index_of(px, py, c), rows], ss.at[j], rs.at[j], (x, y, 1 - c))
                 for j, (px, py) in enumerate(_chips(x, y))]
        arrivals = [_rcopy(g.at[_index_of(px, py, c), rows], g.at[_index_of(px, py, 1 - c), rows], ss.at[j], rs.at[j], (x, y, 1 - c))
                    for j, (px, py) in enumerate(_chips(x, y))]
        return sends, arrivals, []

    return _Move([], 3, 0, make)


def _reduce_pair(parts, src_r0, dst_r0, nr):
    def make(in_refs, a, ss, rs, ls):
        p, = in_refs
        x, y, c = _place()
        src, dst = pl.ds(src_r0, nr), pl.ds(dst_r0, nr)
        chips = [(x, y)] + _chips(x, y)
        sends = [_rcopy(p.at[_index_of(px, py, 1 - c), src], a.at[k, dst], ss.at[k], rs.at[k], (x, y, 1 - c))
                 for k, (px, py) in enumerate(chips)]
        local = [pltpu.make_async_copy(p.at[_index_of(px, py, c), src], a.at[4 + k, dst], ls.at[k])
                 for k, (px, py) in enumerate(chips)]
        return sends, sends, local

    return _Move([parts], 4, 4, make)


def _reduce_cross(sums, src_r0, dst_r0, nr):
    def make(in_refs, b, ss, rs, ls):
        t, = in_refs
        x, y, c = _place()
        src, dst = pl.ds(src_r0, nr), pl.ds(dst_r0, nr)
        sends = [_rcopy(t.at[j, src], b.at[j, dst], ss.at[j], rs.at[j], (px, py, c)) for j, (px, py) in enumerate(_chips(x, y))]
        return sends, sends, []

    return _Move([sums], 3, 0, make)


def _hosted(body, *, name, grid, in_specs, out_specs, out_shape, args, scratch_shapes=(), block_bytes, scratch_bytes=0,
            phases=()):
    n_in, n_out, n_scr = len(args), len(out_shape), len(scratch_shapes)
    all_args, all_out_shape, sems, aliases, layout = list(args), list(out_shape), [], {}, []
    for j, ph in enumerate(phases):
        counts = []
        for mv in ph.moves:
            all_args += mv.ins
            counts.append(len(mv.ins))
            sems += [pltpu.SemaphoreType.DMA((mv.n_remote,)), pltpu.SemaphoreType.DMA((mv.n_remote,)),
                     pltpu.SemaphoreType.DMA((max(mv.n_local, 1),))]
        if ph.so_far is not None:
            aliases[len(all_args)] = n_out + j
            all_args.append(ph.so_far)
        layout.append((counts, ph.so_far is not None))
        all_out_shape.append(ph.like)
    n_extra_in = len(all_args) - n_in

    def wrapped(*refs):
        ins, pos = refs[:n_in], n_in
        move_ins = []
        for counts, continues in layout:
            per_move = []
            for cnt in counts:
                per_move.append(refs[pos:pos + cnt])
                pos += cnt
            pos += continues
            move_ins.append(per_move)
        outs = refs[pos:pos + n_out]
        bufs = refs[pos + n_out:pos + n_out + len(phases)]
        pos += n_out + len(phases)
        scratch = refs[pos:pos + n_scr]
        pos += n_scr
        move_sems = []
        for ph in phases:
            per_move = []
            for _ in ph.moves:
                per_move.append(refs[pos:pos + 3])
                pos += 3
            move_sems.append(per_move)

        def each_move(fn_name):
            for ph, buf, per_in, per_sem in zip(phases, bufs, move_ins, move_sems):
                for mv, mv_in, mv_sem in zip(ph.moves, per_in, per_sem):
                    getattr(mv, fn_name)(mv_in, buf, mv_sem)

        if phases:
            first = functools.reduce(jnp.logical_and, [pl.program_id(a) == 0 for a in range(len(grid))])
            last = functools.reduce(jnp.logical_and, [pl.program_id(a) == grid[a] - 1 for a in range(len(grid))])
            pl.when(first)(lambda: each_move("start"))
        body(*ins, *outs, *scratch)
        if phases:
            pl.when(last)(lambda: each_move("finish"))

    est = 2 * block_bytes + scratch_bytes
    results = pl.pallas_call(
        wrapped, grid=grid,
        in_specs=list(in_specs) + [HBM_SPEC] * n_extra_in,
        out_specs=list(out_specs) + [HBM_SPEC] * len(phases),
        out_shape=all_out_shape,
        scratch_shapes=list(scratch_shapes) + sems,
        input_output_aliases=aliases,
        compiler_params=pltpu.CompilerParams(dimension_semantics=("arbitrary",) * len(grid),
                                             vmem_limit_bytes=min(VMEM_CEILING, max(32 * 1024 * 1024, 2 * est))),
        name=name,
    )(*all_args)
    return list(results[:n_out]), list(results[n_out:])


def _all_gather(shards, name):
    n = len(shards)

    def body(*refs):
        ins, outs = refs[:n], refs[n:2 * n]
        send_sems, recv_sems, local_sems = refs[2 * n:]
        x, y, c = _place()
        me, sibling = (x, y, c), (x, y, 1 - c)
        chips = _chips(x, y)

        def copy(i, k, block, to, src=None):
            dst = outs[i].at[_index_of(*block)]
            return _rcopy(dst if src is None else src, dst, send_sems.at[i * 7 + k], recv_sems.at[i * 7 + k], to)

        own = [pltpu.make_async_copy(ins[i], outs[i].at[_index_of(*me)], local_sems.at[i]) for i in range(n)]
        for cp in own:
            cp.start()
        first = []
        for i in range(n):
            first.append(copy(i, 0, me, sibling, src=ins[i]))
            first += [copy(i, 1 + j, me, (*chip, c), src=ins[i]) for j, chip in enumerate(chips)]
        for cp in first:
            cp.start()
        passed = []
        for i in range(n):
            for j, chip in enumerate(chips):
                copy(i, 1 + j, (*chip, c), me).wait_recv()
                cp = copy(i, 4 + j, (*chip, c), sibling)
                cp.start()
                passed.append(cp)
        for i in range(n):
            copy(i, 0, sibling, me).wait_recv()
            for j, chip in enumerate(chips):
                copy(i, 4 + j, (*chip, 1 - c), me).wait_recv()
        for cp in first + passed:
            cp.wait_send()
        for cp in own:
            cp.wait()

    return pl.pallas_call(
        body,
        in_specs=[HBM_SPEC] * n, out_specs=[HBM_SPEC] * n,
        out_shape=[_sds((N_DEV,) + s.shape, s.dtype) for s in shards],
        scratch_shapes=[pltpu.SemaphoreType.DMA((7 * n,)), pltpu.SemaphoreType.DMA((7 * n,)),
                        pltpu.SemaphoreType.DMA((n,))],
        name=name,
    )(*shards)


def _mm(a, b, mode, out_dtype, *, tm, tn, tk, name, b_blocked=False, out_blocked=False, m_tiles=None, phases=()):
    if mode == "nn":
        (m, k), dims = a.shape, NN
        a_blk, a_map = (tm, tk), (lambda i, j, kk: (i, kk))
        if b_blocked:
            assert b.shape[1] == k and b.shape[2] == tn and tk == k
            n = b.shape[0] * tn
            b_spec = pl.BlockSpec((None, tk, tn), lambda i, j, kk: (j, kk, 0))
        else:
            assert b.shape[0] == k
            n = b.shape[1]
            b_spec = pl.BlockSpec((tk, tn), lambda i, j, kk: (kk, j))
    elif mode == "tn":
        (k, m), n, dims = a.shape, b.shape[1], TN
        assert b.shape[0] == k
        first = 0 if m_tiles is None else m_tiles[0]
        a_blk, a_map = (tk, tm), (lambda i, j, kk: (kk, i + first))
        b_spec = pl.BlockSpec((tk, tn), lambda i, j, kk: (kk, j))
    else:
        (m, k), dims = a.shape, NT
        a_blk, a_map = (tm, tk), (lambda i, j, kk: (i, kk))
        if b_blocked:
            assert b.shape[0] * b.shape[2] == k and b.shape[2] == tk
            n = b.shape[1]
            b_spec = pl.BlockSpec((None, tn, tk), lambda i, j, kk: (kk, j, 0))
        else:
            assert b.shape[1] == k
            n = b.shape[0]
            b_spec = pl.BlockSpec((tn, tk), lambda i, j, kk: (j, kk))
    assert m % tm == 0 and n % tn == 0 and k % tk == 0, (a.shape, b.shape, mode)
    nk = k // tk
    n_row_tiles = m // tm if m_tiles is None else m_tiles[1]
    if out_blocked:
        out_shape, out_spec = _sds((n // tn, n_row_tiles * tm, tn), out_dtype), pl.BlockSpec((None, tm, tn), lambda i, j, kk: (j, i, 0))
    else:
        out_shape, out_spec = _sds((n_row_tiles * tm, n), out_dtype), pl.BlockSpec((tm, tn), lambda i, j, kk: (i, j))

    def body(a_ref, b_ref, o_ref, *scratch):
        p = _dot(a_ref[...], b_ref[...], dims)
        if nk == 1:
            o_ref[...] = p.astype(out_dtype)
        else:
            acc_ref, = scratch
            kk = pl.program_id(2)

            @pl.when(kk == 0)
            def _():
                acc_ref[...] = p

            @pl.when(kk > 0)
            def _():
                acc_ref[...] += p

            @pl.when(kk == nk - 1)
            def _():
                o_ref[...] = acc_ref[...].astype(out_dtype)

    blocks = _nbytes(a_blk, a.dtype) + tk * tn * jnp.dtype(b.dtype).itemsize + _nbytes((tm, tn), out_dtype)
    outs, bufs = _hosted(
        body, name=name, grid=(n_row_tiles, n // tn, nk),
        in_specs=[pl.BlockSpec(a_blk, a_map), b_spec], out_specs=[out_spec], out_shape=[out_shape], args=[a, b],
        scratch_shapes=[] if nk == 1 else [pltpu.VMEM((tm, tn), F32)],
        block_bytes=blocks, scratch_bytes=_nbytes((tm, tn), F32) * (nk > 1), phases=phases)
    return outs[0], bufs


RB = 256


def _row_spec(width):
    return pl.BlockSpec((RB, width), lambda i: (i, 0))


def _vec_spec(width):
    return pl.BlockSpec((1, width), lambda i: (0, 0))


def _rinv(x):
    return lax.rsqrt(jnp.mean(x * x, axis=-1, keepdims=True) + EPS)


def _norm_bwd(dyn, xhat, r):
    return r * (dyn - xhat * jnp.mean(dyn * xhat, axis=-1, keepdims=True))


def _colsum(x):
    return jnp.sum(x, axis=0, keepdims=True)


def _accumulate(ref, value):
    @pl.when(pl.program_id(0) == 0)
    def _():
        ref[...] = value

    @pl.when(pl.program_id(0) > 0)
    def _():
        ref[...] += value


def _prenorm(x, gain):
    def body(x_ref, g_ref, h_ref):
        xv = x_ref[...]
        h_ref[...] = (xv * _rinv(xv) * g_ref[...]).astype(BF16)

    outs, _ = _hosted(body, name="prenorm", grid=(S // RB,), in_specs=[_row_spec(D), _vec_spec(D)], out_specs=[_row_spec(D)],
                      out_shape=[_sds((S, D), BF16)], args=[x, gain], block_bytes=RB * D * 6)
    return outs[0]


def _mid_fwd(x, y, npost, npre, phases=()):
    def body(x_ref, y_ref, po_ref, pr_ref, x1_ref, h1_ref):
        yv = y_ref[...]
        x1 = x_ref[...] + yv * _rinv(yv) * po_ref[...]
        x1_ref[...] = x1
        h1_ref[...] = (x1 * _rinv(x1) * pr_ref[...]).astype(BF16)

    return _hosted(body, name="mid_fwd", grid=(S // RB,), in_specs=[_row_spec(D), _row_spec(D), _vec_spec(D), _vec_spec(D)],
                   out_specs=[_row_spec(D), _row_spec(D)], out_shape=[_sds((S, D), F32), _sds((S, D), BF16)],
                   args=[x, y, npost, npre], block_bytes=RB * D * 14, phases=phases)


def _final(x1, y1, tgt, npost):
    def body(x_ref, y_ref, t_ref, po_ref, loss_ref, dx_ref, dy_ref, dpo_ref):
        yv = y_ref[...]
        r = _rinv(yv)
        yhat = yv * r
        err = x_ref[...] + yhat * po_ref[...] - t_ref[...]
        dx = err * (1.0 / D)
        dx_ref[...] = dx
        dy_ref[...] = _norm_bwd(dx * po_ref[...], yhat, r).astype(BF16)
        _accumulate(loss_ref, _colsum(err * err))
        _accumulate(dpo_ref, _colsum(dx * yhat))

    outs, _ = _hosted(body, name="final", grid=(S // RB,), in_specs=[_row_spec(D), _row_spec(D), _row_spec(D), _vec_spec(D)],
                      out_specs=[_vec_spec(D), _row_spec(D), _row_spec(D), _vec_spec(D)],
                      out_shape=[_sds((1, D), F32), _sds((S, D), F32), _sds((S, D), BF16), _sds((1, D), F32)],
                      args=[x1, y1, tgt, npost], block_bytes=RB * D * 18)
    return outs


def _mid_bwd(dx2, dh1, x1, y0, npre, npost, phases=()):
    def body(dx2_ref, dh_ref, x_ref, y_ref, pr_ref, po_ref, dx1_ref, dy_ref, dpr_ref, dpo_ref):
        xv = x_ref[...]
        r = _rinv(xv)
        xhat = xv * r
        dh = dh_ref[...]
        dx1 = dx2_ref[...] + _norm_bwd(dh * pr_ref[...], xhat, r)
        dx1_ref[...] = dx1
        yv = y_ref[...]
        ry = _rinv(yv)
        yhat = yv * ry
        dy_ref[...] = _norm_bwd(dx1 * po_ref[...], yhat, ry).astype(BF16)
        _accumulate(dpr_ref, _colsum(dh * xhat))
        _accumulate(dpo_ref, _colsum(dx1 * yhat))

    return _hosted(body, name="mid_bwd", grid=(S // RB,), in_specs=[_row_spec(D)] * 4 + [_vec_spec(D)] * 2,
                   out_specs=[_row_spec(D), _row_spec(D), _vec_spec(D), _vec_spec(D)],
                   out_shape=[_sds((S, D), F32), _sds((S, D), BF16), _sds((1, D), F32), _sds((1, D), F32)],
                   args=[dx2, dh1, x1, y0, npre, npost], block_bytes=RB * D * 22, phases=phases)


def _first_bwd(dx1, dh0, x0, npre, phases=()):
    def body(dx1_ref, dh_ref, x_ref, pr_ref, gx_ref, dpr_ref):
        xv = x_ref[...]
        r = _rinv(xv)
        xhat = xv * r
        dh = dh_ref[...]
        gx_ref[...] = dx1_ref[...] + _norm_bwd(dh * pr_ref[...], xhat, r)
        _accumulate(dpr_ref, _colsum(dh * xhat))

    return _hosted(body, name="first_bwd", grid=(S // RB,), in_specs=[_row_spec(D)] * 3 + [_vec_spec(D)],
                   out_specs=[_row_spec(D), _vec_spec(D)], out_shape=[_sds((S, D), F32), _sds((1, D), F32)],
                   args=[dx1, dh0, x0, npre], block_bytes=RB * D * 16, phases=phases)


GLA_RB = 256
GLA_CPB = GLA_RB // C


def _sigmoid(x):
    return 1.0 / (1.0 + jnp.exp(-x))


def _tri(strict):
    r = lax.broadcasted_iota(jnp.int32, (C, C), 0)
    c = lax.broadcasted_iota(jnp.int32, (C, C), 1)
    return jnp.where(c < r if strict else c <= r, 1.0, 0.0).astype(F32)


def _gla_gate(glr_b, w2_h, b_h, tri):
    z = _dot(glr_b, w2_h) + b_h
    log_a = (jnp.minimum(z, 0.0) - jnp.log(1.0 + jnp.exp(-jnp.abs(z)))) * (1.0 / GLA_TAU)
    bcum = _dot(tri, log_a, precision=HIGHEST)
    b_end = jnp.sum(log_a, axis=0, keepdims=True)
    return z, jnp.exp(b_end - bcum), jnp.exp(b_end)


def _gla_fwd(proj, w2p, bgate, ogain, phases=()):
    def body(p_ref, w2_ref, b_ref, og_ref, y_ref, st_out_ref, st_ref):
        @pl.when(pl.program_id(0) == 0)
        def _():
            st_ref[...] = jnp.zeros_like(st_ref)

        tri = _tri(False)

        def chunk(ci, carry):
            rows = pl.ds(pl.multiple_of(ci * C, C), C)
            glr_b = p_ref[rows, LR0:LR0 + LRP].astype(BF16)
            for h in range(H):
                _, ea, dec = _gla_gate(glr_b, w2_ref[:, h * DK:(h + 1) * DK], b_ref[:, h * DK:(h + 1) * DK], tri)
                k_dec = (p_ref[rows, K0 + h * DK:K0 + (h + 1) * DK] * ea).astype(BF16)
                v_b = p_ref[rows, V0 + h * DV:V0 + (h + 1) * DV].astype(BF16)
                st = st_ref[h] * dec + _dot(v_b, k_dec, TN)
                st_ref[h] = st
                st_b = st.astype(BF16)
                st_out_ref[ci, h] = st_b
                q_b = (p_ref[rows, Q0 + h * DK:Q0 + (h + 1) * DK] * (DK ** -0.5)).astype(BF16)
                o = _dot(q_b, st_b, NT)
                on = o * _rinv(o)
                g = p_ref[rows, G0 + h * DV:G0 + (h + 1) * DV]
                y_ref[rows, h * DV:(h + 1) * DV] = (on * og_ref[:, h * DV:(h + 1) * DV] * (g * _sigmoid(g))).astype(BF16)
            return carry

        lax.fori_loop(0, GLA_CPB, chunk, 0)

    blocks = GLA_RB * GLA_PAD * 4 + GLA_RB * D * 2 + GLA_CPB * H * DV * DK * 2
    return _hosted(
        body, name="gla_fwd", grid=(S // GLA_RB,),
        in_specs=[pl.BlockSpec((GLA_RB, GLA_PAD), lambda i: (i, 0)),
                  pl.BlockSpec((LRP, H * DK), lambda i: (0, 0)),
                  pl.BlockSpec((1, H * DK), lambda i: (0, 0)),
                  pl.BlockSpec((1, H * DV), lambda i: (0, 0))],
        out_specs=[pl.BlockSpec((GLA_RB, H * DV), lambda i: (i, 0)),
                   pl.BlockSpec((GLA_CPB, H, DV, DK), lambda i: (i, 0, 0, 0))],
        out_shape=[_sds((S, H * DV), BF16), _sds((NC, H, DV, DK), BF16)],
        args=[proj, w2p, bgate, ogain], scratch_shapes=[pltpu.VMEM((H, DV, DK), F32)],
        block_bytes=blocks, scratch_bytes=H * DV * DK * 4, phases=phases)


def _gla_bwd(proj, dypre, states, w2p, bgate, ogain, phases=()):
    nb = S // GLA_RB

    def body(p_ref, dy_ref, st_blk_ref, st_prev_ref, w2_ref, b_ref, og_ref,
             dp_ref, dog_ref, dbg_ref, dw2_ref, r_ref):
        step = pl.program_id(0)

        @pl.when(step == 0)
        def _():
            r_ref[...] = jnp.zeros_like(r_ref)
            dog_ref[...] = jnp.zeros_like(dog_ref)
            dbg_ref[...] = jnp.zeros_like(dbg_ref)
            dw2_ref[...] = jnp.zeros_like(dw2_ref)

        tri = _tri(False)
        tri_strict = _tri(True)
        has_prev = jnp.where(step < nb - 1, 1.0, 0.0).astype(F32)

        def chunk(ci, st_prev_of):
            rows = pl.ds(ci * C if isinstance(ci, int) else pl.multiple_of(ci * C, C), C)
            glr_b = p_ref[rows, LR0:LR0 + LRP].astype(BF16)
            dglr = jnp.zeros((C, LRP), F32)
            for h in range(H):
                kcol = slice(h * DK, (h + 1) * DK)
                vcol = slice(h * DV, (h + 1) * DV)
                w2_h = w2_ref[:, kcol]
                z, ea, dec = _gla_gate(glr_b, w2_h, b_ref[:, kcol], tri)
                k_dec = p_ref[rows, K0 + h * DK:K0 + (h + 1) * DK] * ea
                k_dec_b = k_dec.astype(BF16)
                v_b = p_ref[rows, V0 + h * DV:V0 + (h + 1) * DV].astype(BF16)
                q_b = (p_ref[rows, Q0 + h * DK:Q0 + (h + 1) * DK] * (DK ** -0.5)).astype(BF16)
                st_b = st_blk_ref[ci, h]
                o = _dot(q_b, st_b, NT)
                rinv = _rinv(o)
                on = o * rinv
                g = p_ref[rows, G0 + h * DV:G0 + (h + 1) * DV]
                sg = _sigmoid(g)
                og = og_ref[:, vcol]
                dyp = dy_ref[rows, vcol]
                dp_ref[rows, G0 + h * DV:G0 + (h + 1) * DV] = (dyp * (on * og) * (sg * (1.0 + g * (1.0 - sg)))).astype(BF16)
                dpn = dyp * (g * sg)
                dog_ref[:, vcol] += _colsum(dpn * on)
                do_b = _norm_bwd(dpn * og, on, rinv).astype(BF16)
                gt = _dot(do_b, q_b, TN) + r_ref[h]
                gt_b = gt.astype(BF16)
                dp_ref[rows, Q0 + h * DK:Q0 + (h + 1) * DK] = (_dot(do_b, st_b) * (DK ** -0.5)).astype(BF16)
                dkd = _dot(v_b, gt_b)
                dp_ref[rows, V0 + h * DV:V0 + (h + 1) * DV] = _dot(k_dec_b, gt_b, NT).astype(BF16)
                dp_ref[rows, K0 + h * DK:K0 + (h + 1) * DK] = (dkd * ea).astype(BF16)
                ddec = _colsum(gt * st_prev_of(h))
                dla = _dot(tri_strict, dkd * k_dec, precision=HIGHEST) + ddec * dec
                dz = dla * (1.0 / GLA_TAU) * (1.0 - _sigmoid(z))
                dz_b = dz.astype(BF16)
                r_ref[h] = gt * dec
                dbg_ref[:, kcol] += _colsum(dz)
                dw2_ref[:, kcol] += _dot(glr_b, dz_b, TN)
                dglr = dglr + _dot(dz_b, w2_h, NT)
            dp_ref[rows, LR0:LR0 + LRP] = dglr.astype(BF16)

        def later_chunk(t, carry):
            ci = GLA_CPB - 1 - t
            chunk(ci, lambda h: st_blk_ref[ci - 1, h].astype(F32))
            return carry

        lax.fori_loop(0, GLA_CPB - 1, later_chunk, 0)
        chunk(0, lambda h: st_prev_ref[0, h].astype(F32) * has_prev)

    blocks = (GLA_RB * GLA_PAD * 4 + GLA_RB * D * 4 + (GLA_CPB + 1) * H * DV * DK * 2 + GLA_RB * GLA_PAD * 2)
    rev = lambda i: nb - 1 - i
    return _hosted(
        body, name="gla_bwd", grid=(nb,),
        in_specs=[pl.BlockSpec((GLA_RB, GLA_PAD), lambda i: (rev(i), 0)),
                  pl.BlockSpec((GLA_RB, H * DV), lambda i: (rev(i), 0)),
                  pl.BlockSpec((GLA_CPB, H, DV, DK), lambda i: (rev(i), 0, 0, 0)),
                  pl.BlockSpec((1, H, DV, DK), lambda i: (jnp.maximum(rev(i) * GLA_CPB - 1, 0), 0, 0, 0)),
                  pl.BlockSpec((LRP, H * DK), lambda i: (0, 0)),
                  pl.BlockSpec((1, H * DK), lambda i: (0, 0)),
                  pl.BlockSpec((1, H * DV), lambda i: (0, 0))],
        out_specs=[pl.BlockSpec((GLA_RB, GLA_PAD), lambda i: (rev(i), 0)),
                   pl.BlockSpec((1, H * DV), lambda i: (0, 0)),
                   pl.BlockSpec((1, H * DK), lambda i: (0, 0)),
                   pl.BlockSpec((LRP, H * DK), lambda i: (0, 0))],
        out_shape=[_sds((S, GLA_PAD), BF16), _sds((1, H * DV), F32), _sds((1, H * DK), F32), _sds((LRP, H * DK), F32)],
        args=[proj, dypre, states, states, w2p, bgate, ogain], scratch_shapes=[pltpu.VMEM((H, DV, DK), F32)],
        block_bytes=blocks, scratch_bytes=H * DV * DK * 4, phases=phases)


SGU_RB = 256
GELU_C = 0.7978845608028654
GELU_A = 0.044715


def _gelu(x):
    return 0.5 * x * (1.0 + jnp.tanh(GELU_C * (x + GELU_A * x * x * x)))


def _gelu_grad(x):
    t = jnp.tanh(GELU_C * (x + GELU_A * x * x * x))
    return 0.5 * (1.0 + t) + 0.5 * x * (1.0 - t * t) * (GELU_C * (1.0 + 3.0 * GELU_A * x * x))


def _causal_mask(transposed=False):
    i = lax.broadcasted_iota(jnp.int32, (SGU_BLOCK, SGU_BLOCK), 1 if transposed else 0)
    j = lax.broadcasted_iota(jnp.int32, (SGU_BLOCK, SGU_BLOCK), 0 if transposed else 1)
    return (i >= C) | (j < C)


def _layer_norm(vf, gain, bias):
    mu = jnp.mean(vf, axis=-1, keepdims=True)
    cen = vf - mu
    rstd = lax.rsqrt(jnp.mean(cen * cen, axis=-1, keepdims=True) + EPS)
    xhat = cen * rstd
    return xhat, rstd, xhat * gain + bias


def _sgu_fwd(proj, lng, lnb, ws, bsb, phases=()):
    def body(p_ref, g_ref, b_ref, ws_ref, bs_ref, o_ref):
        mask = _causal_mask()
        for n in range(SGU_RB // SGU_BLOCK):
            rows = slice(n * SGU_BLOCK, (n + 1) * SGU_BLOCK)
            _, _, vn = _layer_norm(_gelu(p_ref[rows, D:2 * D]), g_ref[...], b_ref[...])
            vn_b = vn.astype(BF16)
            for gi in range(SGU_G):
                cols = slice(gi * SGU_GD, (gi + 1) * SGU_GD)
                w = jnp.where(mask, ws_ref[gi], 0.0).astype(BF16)
                vs = _dot(w, vn_b[:, cols]) + bs_ref[gi]
                gate = p_ref[rows, 2 * D + gi * SGU_GD:2 * D + (gi + 1) * SGU_GD]
                o_ref[rows, cols] = (_gelu(p_ref[rows, cols]) * vs * (gate * _sigmoid(gate))).astype(BF16)

    blocks = SGU_RB * SGU_COLS * 4 + SGU_RB * D * 2 + SGU_G * SGU_BLOCK * (SGU_BLOCK + SGU_GD) * 4
    return _hosted(
        body, name="sgu_fwd", grid=(S // SGU_RB,),
        in_specs=[pl.BlockSpec((SGU_RB, SGU_COLS), lambda i: (i, 0)),
                  pl.BlockSpec((1, D), lambda i: (0, 0)), pl.BlockSpec((1, D), lambda i: (0, 0)),
                  pl.BlockSpec((SGU_G, SGU_BLOCK, SGU_BLOCK), lambda i: (0, 0, 0)),
                  pl.BlockSpec((SGU_G, SGU_BLOCK, SGU_GD), lambda i: (0, 0, 0))],
        out_specs=[pl.BlockSpec((SGU_RB, D), lambda i: (i, 0))], out_shape=[_sds((S, D), BF16)],
        args=[proj, lng, lnb, ws, bsb], block_bytes=blocks, phases=phases)


def _sgu_bwd(proj, dpre, lng, lnb, ws, wst, bsb, phases=()):
    nsteps = S // SGU_RB

    def body(p_ref, d_ref, g_ref, b_ref, ws_ref, wst_ref, bs_ref,
             dp_ref, dg_ref, db_ref, dws_ref, dbs_ref, dvn_ref, dvs_acc_ref):
        step = pl.program_id(0)

        @pl.when(step == 0)
        def _():
            dg_ref[...] = jnp.zeros_like(dg_ref)
            db_ref[...] = jnp.zeros_like(db_ref)
            dws_ref[...] = jnp.zeros_like(dws_ref)
            dvs_acc_ref[...] = jnp.zeros_like(dvs_acc_ref)

        mask = _causal_mask()
        maskt = _causal_mask(transposed=True)
        for n in range(SGU_RB // SGU_BLOCK):
            rows = slice(n * SGU_BLOCK, (n + 1) * SGU_BLOCK)
            v = p_ref[rows, D:2 * D]
            xhat, rstd, vn = _layer_norm(_gelu(v), g_ref[...], b_ref[...])
            vn_b = vn.astype(BF16)
            for gi in range(SGU_G):
                cols = slice(gi * SGU_GD, (gi + 1) * SGU_GD)
                w = jnp.where(mask, ws_ref[gi], 0.0).astype(BF16)
                wt = jnp.where(maskt, wst_ref[gi], 0.0).astype(BF16)
                vs = _dot(w, vn_b[:, cols]) + bs_ref[gi]
                u = p_ref[rows, cols]
                gate = p_ref[rows, 2 * D + gi * SGU_GD:2 * D + (gi + 1) * SGU_GD]
                sg = _sigmoid(gate)
                gu = _gelu(u)
                dpre_g = d_ref[rows, cols]
                t = dpre_g * (gate * sg)
                dp_ref[rows, cols] = (t * vs * _gelu_grad(u)).astype(BF16)
                dp_ref[rows, 2 * D + gi * SGU_GD:2 * D + (gi + 1) * SGU_GD] = (
                    dpre_g * gu * vs * (sg * (1.0 + gate * (1.0 - sg)))).astype(BF16)
                dvs = t * gu
                dvs_b = dvs.astype(BF16)
                dvs_acc_ref[:, cols] += dvs
                dws_ref[gi] += _dot(dvs_b, vn_b[:, cols], NT)
                dvn_ref[:, cols] = _dot(wt, dvs_b)
            dvn = dvn_ref[...]
            dg_ref[...] += _colsum(dvn * xhat)
            db_ref[...] += _colsum(dvn)
            dxh = dvn * g_ref[...]
            dvf = rstd * (dxh - jnp.mean(dxh, axis=-1, keepdims=True) - xhat * jnp.mean(dxh * xhat, axis=-1, keepdims=True))
            dp_ref[rows, D:2 * D] = (dvf * _gelu_grad(v)).astype(BF16)

        @pl.when(step == nsteps - 1)
        def _():
            lane = lax.broadcasted_iota(jnp.int32, (SGU_BLOCK, SGU_BLOCK), 1)
            out = jnp.zeros((SGU_BLOCK, SGU_BLOCK), F32)
            for gi in range(SGU_G):
                out = out + jnp.where(lane == gi, jnp.sum(dvs_acc_ref[:, gi * SGU_GD:(gi + 1) * SGU_GD], axis=1, keepdims=True), 0.0)
                dws_ref[gi] = jnp.where(mask, dws_ref[gi], 0.0)
            dbs_ref[...] = out

    blocks = SGU_RB * SGU_COLS * 6 + SGU_RB * D * 4 + SGU_G * SGU_BLOCK * (3 * SGU_BLOCK + SGU_GD) * 4
    const3 = lambda i: (0, 0, 0)
    return _hosted(
        body, name="sgu_bwd", grid=(nsteps,),
        in_specs=[pl.BlockSpec((SGU_RB, SGU_COLS), lambda i: (i, 0)),
                  pl.BlockSpec((SGU_RB, D), lambda i: (i, 0)),
                  pl.BlockSpec((1, D), lambda i: (0, 0)), pl.BlockSpec((1, D), lambda i: (0, 0)),
                  pl.BlockSpec((SGU_G, SGU_BLOCK, SGU_BLOCK), const3),
                  pl.BlockSpec((SGU_G, SGU_BLOCK, SGU_BLOCK), const3),
                  pl.BlockSpec((SGU_G, SGU_BLOCK, SGU_GD), const3)],
        out_specs=[pl.BlockSpec((SGU_RB, SGU_COLS), lambda i: (i, 0)),
                   pl.BlockSpec((1, D), lambda i: (0, 0)), pl.BlockSpec((1, D), lambda i: (0, 0)),
                   pl.BlockSpec((SGU_G, SGU_BLOCK, SGU_BLOCK), const3),
                   pl.BlockSpec((SGU_BLOCK, SGU_BLOCK), lambda i: (0, 0))],
        out_shape=[_sds((S, SGU_COLS), BF16), _sds((1, D), F32), _sds((1, D), F32),
                   _sds((SGU_G, SGU_BLOCK, SGU_BLOCK), F32), _sds((SGU_BLOCK, SGU_BLOCK), F32)],
        args=[proj, dpre, lng, lnb, ws, wst, bsb],
        scratch_shapes=[pltpu.VMEM((SGU_BLOCK, D), F32), pltpu.VMEM((SGU_BLOCK, D), F32)],
        block_bytes=blocks, scratch_bytes=2 * SGU_BLOCK * D * 4, phases=phases)


def _pair_sum(a, r0, nr, name):
    _, _, c = a.shape
    tr = 256
    assert r0 % tr == 0 and nr % tr == 0

    def body(own_ref, sib_ref, o_ref):
        o_ref[...] = (own_ref[...].astype(F32) + sib_ref[...].astype(F32)).astype(BF16)

    cpad = -(-c // 128) * 128
    outs, _ = _hosted(
        body, name=name, grid=(3, nr // tr),
        in_specs=[pl.BlockSpec((None, tr, c), lambda j, i: (5 + j, r0 // tr + i, 0)),
                  pl.BlockSpec((None, tr, c), lambda j, i: (1 + j, r0 // tr + i, 0))],
        out_specs=[pl.BlockSpec((None, tr, c), lambda j, i: (j, i, 0))], out_shape=[_sds((3, nr, c), BF16)],
        args=[a, a], block_bytes=3 * tr * cpad * 2)
    return outs[0]


def _adamw_math(w, g, m, v):
    m = ADAM_B1 * m + (1.0 - ADAM_B1) * g
    v = ADAM_B2 * v + (1.0 - ADAM_B2) * (g * g)
    m_hat = m / (1.0 - ADAM_B1 ** ADAM_STEP)
    v_hat = v / (1.0 - ADAM_B2 ** ADAM_STEP)
    delta = -ADAM_LR * (m_hat / (jnp.sqrt(v_hat) + ADAM_EPS) + ADAM_WD * w)
    return delta, m, v


def _sum_adamw(a, b, w, m, v, *, name, phases=()):
    r, c = w.shape
    tr = 256

    def body(own_ref, sib_ref, far_ref, w_ref, m_ref, v_ref, g_ref, d_ref, nm_ref, nv_ref):
        g = own_ref[...].astype(F32) + sib_ref[...].astype(F32)
        for j in range(3):
            g = g + far_ref[j].astype(F32)
        g_ref[...] = g
        d_ref[...], nm_ref[...], nv_ref[...] = _adamw_math(w_ref[...], g, m_ref[...], v_ref[...])

    spec = pl.BlockSpec((tr, c), lambda i: (i, 0))
    cpad = -(-c // 128) * 128
    return _hosted(
        body, name=name, grid=(r // tr,),
        in_specs=[pl.BlockSpec((None, tr, c), lambda i: (4, i, 0)), pl.BlockSpec((None, tr, c), lambda i: (0, i, 0)),
                  pl.BlockSpec((3, tr, c), lambda i: (0, i, 0)), spec, spec, spec],
        out_specs=[spec] * 4, out_shape=[_sds((r, c), F32)] * 4, args=[a, a, b, w, m, v],
        block_bytes=5 * tr * cpad * 2 + 7 * tr * cpad * 4, phases=phases)


def _sum_parts(parts, name):
    n, r, c = parts.shape

    def body(p_ref, o_ref):
        g = p_ref[0]
        for j in range(1, n):
            g = g + p_ref[j]
        o_ref[...] = g

    outs, _ = _hosted(body, name=name, grid=(1,), in_specs=[pl.BlockSpec((n, r, c), lambda i: (0, 0, 0))],
                      out_specs=[pl.BlockSpec((r, c), lambda i: (0, 0))], out_shape=[_sds((r, c), F32)], args=[parts],
                      block_bytes=(n + 1) * r * c * 4)
    return outs[0]


def _adamw(w, g, m, v, name):
    def body(w_ref, g_ref, m_ref, v_ref, d_ref, nm_ref, nv_ref):
        d_ref[...], nm_ref[...], nv_ref[...] = _adamw_math(w_ref[...], g_ref[...], m_ref[...], v_ref[...])

    spec = pl.BlockSpec(w.shape, lambda i: (0, 0))
    outs, _ = _hosted(body, name=name, grid=(1,), in_specs=[spec] * 4, out_specs=[spec] * 3, out_shape=[_sds(w.shape, F32)] * 3,
                      args=[w, g, m, v], block_bytes=7 * _nbytes(w.shape, F32))
    return outs


def _blocks_to_columns(g):
    n, r, c = g.shape
    return jnp.transpose(g, (1, 0, 2)).reshape(r, n * c)


def _columns_to_blocks(w, c):
    r = w.shape[0]
    return jnp.transpose(w.reshape(r, N_DEV, c), (1, 0, 2))


def _pack(parts):
    return jnp.concatenate([p.reshape(-1) for p in parts]).reshape(-1, 128)


def _unpack(packed, like):
    flat, outs, off = packed.reshape(-1), [], 0
    for p in like:
        outs.append(flat[off:off + p.size].reshape(p.shape))
        off += p.size
    return outs


def kernel(x, norm_pre, norm_post, gla_w_in, gla_w_gate2, gla_b_gate, gla_o_gain, gla_w_out, sgu_w_in, sgu_ln_gain, sgu_ln_bias, sgu_w_spatial, sgu_b_spatial, sgu_w_out, loss_target, m_norm_pre, m_norm_post, m_gla_w_in, m_gla_w_gate2, m_gla_b_gate, m_gla_o_gain, m_gla_w_out, m_sgu_w_in, m_sgu_ln_gain, m_sgu_ln_bias, m_sgu_w_spatial, m_sgu_b_spatial, m_sgu_w_out, v_norm_pre, v_norm_post, v_gla_w_in, v_gla_w_gate2, v_gla_b_gate, v_gla_o_gain, v_gla_w_out, v_sgu_w_in, v_sgu_ln_gain, v_sgu_ln_bias, v_sgu_w_spatial, v_sgu_b_spatial, v_sgu_w_out):
    me = _index_of(*_place())
    x0 = x.reshape(S, D)
    tgt = loss_target.reshape(S, D)
    npre0, npre1 = norm_pre[0:1], norm_pre[1:2]
    npost0, npost1 = norm_post[0:1], norm_post[1:2]
    ws = sgu_w_spatial[0]
    wst = jnp.transpose(ws, (0, 2, 1))
    bsb = jnp.broadcast_to(sgu_b_spatial[0][:, :, None], (SGU_G, SGU_BLOCK, SGU_GD))
    W_ROWS = D // N_DEV
    IN_COLS_G, IN_COLS_S = GLA_COLS // N_DEV, SGU_COLS // N_DEV

    s_gwi, s_gwo = gla_w_in[0].astype(BF16), gla_w_out[0].astype(BF16)
    s_swi, s_swo = sgu_w_in[0].astype(BF16), sgu_w_out[0].astype(BF16)
    small = jnp.concatenate([jnp.pad(gla_w_gate2[0].reshape(4, 512), ((0, 4), (0, 0))),
                             jnp.pad(jnp.concatenate([sgu_ln_gain, sgu_ln_bias], axis=1), ((0, 7), (0, 0)))], axis=0)

    g_gwi, g_small = _all_gather([s_gwi, small], "gather_first")
    wg_in = jnp.pad(_blocks_to_columns(g_gwi), ((0, 0), (0, GLA_PAD - GLA_COLS)))
    w2 = _blocks_to_columns(g_small[:, :4, :].reshape(N_DEV, LR, 128))
    w2p = jnp.pad(w2, ((0, LRP - LR), (0, 0))).astype(BF16)
    lng = g_small[:, 8, :256].reshape(1, D)
    lnb = g_small[:, 8, 256:].reshape(1, D)
    like_gwo, like_swi = _sds((N_DEV, W_ROWS, D), BF16), _sds((N_DEV, D, IN_COLS_S), BF16)

    h0 = _prenorm(x0, npre0)
    proj0, (g_gwo, g_swi) = _mm(h0, wg_in, "nn", F32, tm=1024, tn=896, tk=D, name="gla_in", phases=[
        _Phase(like_gwo, None, [_gather_send(s_gwo, 0, W_ROWS)]),
        _Phase(like_swi, None, [_gather_send(s_swi, 0, 256)])])
    (ypre0, states), (g_gwo, g_swi) = _gla_fwd(proj0, w2p, gla_b_gate, gla_o_gain, phases=[
        _Phase(like_gwo, g_gwo, [_gather_pass(0, W_ROWS)]),
        _Phase(like_swi, g_swi, [_gather_send(s_swi, 256, 1280), _gather_pass(0, 256)])])
    wg_out = g_gwo.reshape(D, D)
    y0, (g_swi,) = _mm(ypre0, wg_out, "nn", F32, tm=1024, tn=1024, tk=D, name="gla_out", phases=[
        _Phase(like_swi, g_swi, [_gather_send(s_swi, 1536, 512), _gather_pass(256, 1280)])])
    (x1, h1), (g_swi,) = _mid_fwd(x0, y0, npost0, npre1, phases=[_Phase(like_swi, g_swi, [_gather_pass(1536, 512)])])
    proj1, (g_swo,) = _mm(h1, g_swi, "nn", F32, tm=1024, tn=IN_COLS_S, tk=D, name="sgu_in", b_blocked=True, phases=[
        _Phase(like_gwo, None, [_gather_send(s_swo, 0, W_ROWS)])])
    (pre1,), (g_swo,) = _sgu_fwd(proj1, lng, lnb, ws, bsb, phases=[_Phase(like_gwo, g_swo, [_gather_pass(0, W_ROWS)])])
    ws_out = g_swo.reshape(D, D)
    y1, _ = _mm(pre1, ws_out, "nn", F32, tm=1024, tn=1024, tk=D, name="sgu_out")
    loss_cols, dx2, dy1, dnpost1 = _final(x1, y1, tgt, npost1)
    loss = lax.psum(0.5 * jnp.sum(loss_cols) / D, ("x", "y", "c"))

    like_a_out, like_b_out = _sds((N_DEV, W_ROWS, D), BF16), _sds((3, W_ROWS, D), BF16)
    like_a_swi, like_b_swi = _sds((N_DEV, D, IN_COLS_S), BF16), _sds((3, D, IN_COLS_S), BF16)
    like_a_gwi, like_b_gwi = _sds((N_DEV, D, IN_COLS_G), BF16), _sds((3, D, IN_COLS_G), BF16)

    dws_out, _ = _mm(pre1, dy1, "tn", BF16, tm=1024, tn=1024, tk=S, name="sgu_out_dw")
    dpre1, (a_swo,) = _mm(dy1, ws_out, "nt", F32, tm=1024, tn=1024, tk=D, name="sgu_out_dx", phases=[
        _Phase(like_a_out, None, [_reduce_pair(dws_out.reshape(N_DEV, W_ROWS, D), 0, 0, W_ROWS)])])
    t_swo = _pair_sum(a_swo, 0, W_ROWS, "pair_sum_sgu_w_out")
    (dproj1, dlng, dlnb, dwsp, dbsp), (b_swo,) = _sgu_bwd(proj1, dpre1, lng, lnb, ws, wst, bsb, phases=[
        _Phase(like_b_out, None, [_reduce_cross(t_swo, 0, 0, W_ROWS)])])
    p_swi, _ = _mm(h1, dproj1, "tn", BF16, tm=1024, tn=IN_COLS_S, tk=S, name="sgu_in_dw", out_blocked=True)
    dh1, (a_swi,) = _mm(dproj1, g_swi, "nt", F32, tm=1024, tn=1024, tk=IN_COLS_S, name="sgu_in_dx", b_blocked=True, phases=[
        _Phase(like_a_swi, None, [_reduce_pair(p_swi, 0, 0, D)])])
    t_swi = _pair_sum(a_swi, 0, D, "pair_sum_sgu_w_in")
    (dx1, dy0, dnpre1, dnpost0), (b_swi,) = _mid_bwd(dx2, dh1, x1, y0, npre1, npost0, phases=[
        _Phase(like_b_swi, None, [_reduce_cross(t_swi, 0, 0, 512)])])
    dwg_out, (b_swi,) = _mm(ypre0, dy0, "tn", BF16, tm=1024, tn=1024, tk=S, name="gla_out_dw", phases=[
        _Phase(like_b_swi, b_swi, [_reduce_cross(t_swi, 512, 512, 512)])])
    dypre0, (b_swi, a_gwo) = _mm(dy0, wg_out, "nt", F32, tm=1024, tn=1024, tk=D, name="gla_out_dx", phases=[
        _Phase(like_b_swi, b_swi, [_reduce_cross(t_swi, 1024, 1024, 256)]),
        _Phase(like_a_out, None, [_reduce_pair(dwg_out.reshape(N_DEV, W_ROWS, D), 0, 0, W_ROWS)])])
    t_gwo = _pair_sum(a_gwo, 0, W_ROWS, "pair_sum_gla_w_out")
    late = [dnpre1, dnpost1, dlng, dlnb, dwsp, jnp.transpose(dbsp[:, :SGU_G])]
    late_pack = _pack(late)
    (dproj0, dogain, dbgate, dw2), (b_swi, b_gwo, g_late) = _gla_bwd(proj0, dypre0, states, w2p, gla_b_gate, gla_o_gain, phases=[
        _Phase(like_b_swi, b_swi, [_reduce_cross(t_swi, 1280, 1280, 768)]),
        _Phase(like_b_out, None, [_reduce_cross(t_gwo, 0, 0, W_ROWS)]),
        _Phase(_sds((N_DEV,) + late_pack.shape, F32), None, [_gather_send(late_pack, 0, late_pack.shape[0])])])
    half = D // 2
    dwg_in_a, (g_late,) = _mm(h0, dproj0, "tn", BF16, tm=half, tn=896, tk=S, name="gla_in_dw_a", m_tiles=(0, 1), phases=[
        _Phase(_sds((N_DEV,) + late_pack.shape, F32), g_late, [_gather_pass(0, late_pack.shape[0])])])
    p_gwi_a = _columns_to_blocks(dwg_in_a[:, :GLA_COLS], IN_COLS_G)
    dwg_in_b, (a_gwi,) = _mm(h0, dproj0, "tn", BF16, tm=half, tn=896, tk=S, name="gla_in_dw_b", m_tiles=(1, 1), phases=[
        _Phase(like_a_gwi, None, [_reduce_pair(p_gwi_a, 0, 0, half)])])
    p_gwi_b = _columns_to_blocks(dwg_in_b[:, :GLA_COLS], IN_COLS_G)
    t_gwi_a = _pair_sum(a_gwi, 0, half, "pair_sum_gla_w_in_a")
    dh0, (b_gwi, a_gwi) = _mm(dproj0, wg_in, "nt", F32, tm=1024, tn=1024, tk=896, name="gla_in_dx", phases=[
        _Phase(like_b_gwi, None, [_reduce_cross(t_gwi_a, 0, 0, half)]),
        _Phase(like_a_gwi, a_gwi, [_reduce_pair(p_gwi_b, 0, half, half)])])
    t_gwi_b = _pair_sum(a_gwi, half, half, "pair_sum_gla_w_in_b")
    (grad_x, dnpre0), (b_gwi,) = _first_bwd(dx1, dh0, x0, npre0, phases=[
        _Phase(like_b_gwi, b_gwi, [_reduce_cross(t_gwi_b, 0, half, 256)])])

    early = [dnpre0, dnpost0, dbgate, dogain, dw2[:LR]]
    early_pack = _pack(early)
    like_early = _sds((N_DEV,) + early_pack.shape, F32)
    (g_swo, d_swo, nm_swo, nv_swo), (b_gwi, g_early) = _sum_adamw(
        a_swo, b_swo, sgu_w_out[0], m_sgu_w_out[0], v_sgu_w_out[0], name="adamw_sgu_w_out", phases=[
            _Phase(like_b_gwi, b_gwi, [_reduce_cross(t_gwi_b, 256, half + 256, 256)]),
            _Phase(like_early, None, [_gather_send(early_pack, 0, early_pack.shape[0])])])
    (g_swi_, d_swi, nm_swi, nv_swi), (b_gwi, g_early) = _sum_adamw(
        a_swi, b_swi, sgu_w_in[0], m_sgu_w_in[0], v_sgu_w_in[0], name="adamw_sgu_w_in", phases=[
            _Phase(like_b_gwi, b_gwi, [_reduce_cross(t_gwi_b, 512, half + 512, 256)]),
            _Phase(like_early, g_early, [_gather_pass(0, early_pack.shape[0])])])
    (g_gwo_, d_gwo, nm_gwo, nv_gwo), (b_gwi,) = _sum_adamw(
        a_gwo, b_gwo, gla_w_out[0], m_gla_w_out[0], v_gla_w_out[0], name="adamw_gla_w_out", phases=[
            _Phase(like_b_gwi, b_gwi, [_reduce_cross(t_gwi_b, 768, half + 768, 256)])])
    (g_gwi_, d_gwi, nm_gwi, nv_gwi), _ = _sum_adamw(
        a_gwi, b_gwi, gla_w_in[0], m_gla_w_in[0], v_gla_w_in[0], name="adamw_gla_w_in")

    g_npre1, g_npost1, g_lng_full, g_lnb_full, g_wsp, g_bsp = _unpack(_sum_parts(g_late, "sum_late_small_grads"), late)
    g_npre0, g_npost0, g_bgate, g_ogain, g_w2_full = _unpack(_sum_parts(g_early, "sum_early_small_grads"), early)
    g_w2 = lax.dynamic_slice(g_w2_full, (0, me * 128), (LR, 128))
    g_lng = lax.dynamic_slice(g_lng_full, (0, me * 256), (1, 256))
    g_lnb = lax.dynamic_slice(g_lnb_full, (0, me * 256), (1, 256))
    small_g = [jnp.concatenate([g_npre0, g_npre1], 0), jnp.concatenate([g_npost0, g_npost1], 0), g_w2, g_bgate, g_ogain,
               g_lng, g_lnb, g_wsp, g_bsp]
    small_w = [norm_pre, norm_post, gla_w_gate2[0], gla_b_gate, gla_o_gain, sgu_ln_gain, sgu_ln_bias, sgu_w_spatial[0], sgu_b_spatial[0]]
    small_m = [m_norm_pre, m_norm_post, m_gla_w_gate2[0], m_gla_b_gate, m_gla_o_gain, m_sgu_ln_gain, m_sgu_ln_bias, m_sgu_w_spatial[0], m_sgu_b_spatial[0]]
    small_v = [v_norm_pre, v_norm_post, v_gla_w_gate2[0], v_gla_b_gate, v_gla_o_gain, v_sgu_ln_gain, v_sgu_ln_bias, v_sgu_w_spatial[0], v_sgu_b_spatial[0]]
    d_pack, nm_pack, nv_pack = _adamw(_pack(small_w), _pack(small_g), _pack(small_m), _pack(small_v), "adamw_small")

    out_like = [norm_pre, norm_post, gla_w_gate2, gla_b_gate, gla_o_gain, sgu_ln_gain, sgu_ln_bias, sgu_w_spatial, sgu_b_spatial]
    sg_ = [g.reshape(s.shape) for g, s in zip(small_g, out_like)]
    sd_, sm_, sv_ = (_unpack(pk, out_like) for pk in (d_pack, nm_pack, nv_pack))

    def assemble(small_list, w_in_g, w_out_g, w_in_s, w_out_s):
        npre_, npost_, w2_, bg_, og_, lg_, lb_, wsp_, bsp_ = small_list
        return [npre_, npost_, w_in_g[None], w2_, bg_, og_, w_out_g[None], w_in_s[None], lg_, lb_, wsp_, bsp_, w_out_s[None]]

    return (loss, grad_x.reshape(1, S, D),
            *assemble(sg_, g_gwi_, g_gwo_, g_swi_, g_swo),
            *assemble(sd_, d_gwi, d_gwo, d_swi, d_swo),
            *assemble(sm_, nm_gwi, nm_gwo, nm_swi, nm_swo),
            *assemble(sv_, nv_gwi, nv_gwo, nv_swi, nv_swo))
```

```python
import functools

import jax
import jax.numpy as jnp
from jax import lax
from jax.experimental import pallas as pl
from jax.experimental.pallas import tpu as pltpu

F32 = jnp.float32
BF16 = jnp.bfloat16

N_DEV = 8
S = 2048
D = 2048
H = 4
DK = 256
DV = 512
C = 64
NC = S // C
GLA_COLS = 6160
GLA_PAD = 6272
Q0, K0, V0, G0, LR0 = 0, 1024, 2048, 4096, 6144
LR = 16
LRP = 128
SGU_COLS = 6144
SGU_BLOCK = 128
SGU_G = 8
SGU_GD = 256
EPS = 1e-6
GLA_TAU = 16.0

ADAM_LR, ADAM_B1, ADAM_B2, ADAM_EPS, ADAM_WD, ADAM_STEP = 0.001, 0.9, 0.999, 1e-08, 0.01, 10

V7X_VMEM_BYTES = 64 * 1024 * 1024
VMEM_CEILING = V7X_VMEM_BYTES - 6 * 1024 * 1024
MESH = pl.DeviceIdType.MESH
HIGHEST = lax.Precision.HIGHEST
HBM_SPEC = pl.BlockSpec(memory_space=pl.ANY)


def _sds(shape, dtype):
    return jax.ShapeDtypeStruct(tuple(shape), dtype)


def _nbytes(shape, dtype):
    n = 1
    for s in shape:
        n *= s
    return n * jnp.dtype(dtype).itemsize


def _dot(a, b, dims=(((1,), (0,)), ((), ())), precision=None):
    return lax.dot_general(a, b, dims, precision=precision, preferred_element_type=F32)


NN = (((1,), (0,)), ((), ()))
TN = (((0,), (0,)), ((), ()))
NT = (((1,), (1,)), ((), ()))


def _place():
    return lax.axis_index("x"), lax.axis_index("y"), lax.axis_index("c")


def _index_of(px, py, pc):
    return 4 * px + 2 * py + pc


def _chips(x, y):
    return [(1 - x, y), (x, 1 - y), (1 - x, 1 - y)]


def _rcopy(src, dst, send_sem, recv_sem, to):
    return pltpu.make_async_remote_copy(src_ref=src, dst_ref=dst, send_sem=send_sem, recv_sem=recv_sem,
                                        device_id=to, device_id_type=MESH)


class _Move:
    def __init__(self, ins, n_remote, n_local, make):
        self.ins, self.n_remote, self.n_local, self.make = list(ins), n_remote, n_local, make

    def start(self, in_refs, buf, sems):
        sends, _, local = self.make(in_refs, buf, *sems)
        for cp in local + sends:
            cp.start()

    def finish(self, in_refs, buf, sems):
        sends, arrivals, local = self.make(in_refs, buf, *sems)
        for cp in arrivals:
            cp.wait_recv()
        for cp in sends:
            cp.wait_send()
        for cp in local:
            cp.wait()


class _Phase:
    def __init__(self, like, so_far, moves):
        self.like, self.so_far, self.moves = like, so_far, list(moves)


def _gather_send(shard, r0, nr):
    def make(in_refs, g, ss, rs, ls):
        sh, = in_refs
        x, y, c = _place()
        me = _index_of(x, y, c)
        rows = pl.ds(r0, nr)
        peers = [(x, y, 1 - c)] + [(px, py, c) for px, py in _chips(x, y)]
        sends = [_rcopy(sh.at[rows], g.at[me, rows], ss.at[k], rs.at[k], p) for k, p in enumerate(peers)]
        arrivals = [_rcopy(sh.at[rows], g.at[_index_of(*p), rows], ss.at[k], rs.at[k], p) for k, p in enumerate(peers)]
        return sends, arrivals, [pltpu.make_async_copy(sh.at[rows], g.at[me, rows], ls.at[0])]

    return _Move([shard], 4, 1, make)


def _gather_pass(r0, nr):
    def make(in_refs, g, ss, rs, ls):
        x, y, c = _place()
        rows = pl.ds(r0, nr)
        sends = [_rcopy(g.at[_index_of(px, py, c), rows], g.at[_index_of(px, py, c), rows], ss.at[j], rs.at[j], (x, y, 1 - c))
                 for j, (px, py) in enumerate(_chips(x, y))]
        arrivals = [_rcopy(g.at[_index_of(px, py, c), rows], g.at[_index_of(px, py, 1 - c), rows], ss.at[j], rs.at[j], (x, y, 1 - c))
                    for j, (px, py) in enumerate(_chips(x, y))]
        return sends, arrivals, []

    return _Move([], 3, 0, make)


def _reduce_pair(parts, src_r0, dst_r0, nr):
    def make(in_refs, a, ss, rs, ls):
        p, = in_refs
        x, y, c = _place()
        src, dst = pl.ds(src_r0, nr), pl.ds(dst_r0, nr)
        sends = [_rcopy(p.at[_index_of(px, py, 1 - c), src], a.at[k, dst], ss.at[k], rs.at[k], (x, y, 1 - c))
                 for k, (px, py) in enumerate([(x, y)] + _chips(x, y))]
        return sends, sends, []

    return _Move([parts], 4, 0, make)


def _own_blocks(parts):
    x, y, c = _place()
    take = lambda px, py: lax.dynamic_index_in_dim(parts, _index_of(px, py, c), 0, keepdims=False)
    return take(x, y), jnp.stack([take(px, py) for px, py in _chips(x, y)])


def _reduce_cross(sums, src_r0, dst_r0, nr):
    def make(in_refs, b, ss, rs, ls):
        t, = in_refs
        x, y, c = _place()
        src, dst = pl.ds(src_r0, nr), pl.ds(dst_r0, nr)
        sends = [_rcopy(t.at[j, src], b.at[j, dst], ss.at[j], rs.at[j], (px, py, c)) for j, (px, py) in enumerate(_chips(x, y))]
        return sends, sends, []

    return _Move([sums], 3, 0, make)


def _hosted(body, *, name, grid, in_specs, out_specs, out_shape, args, scratch_shapes=(), block_bytes, scratch_bytes=0,
            phases=()):
    n_in, n_out, n_scr = len(args), len(out_shape), len(scratch_shapes)
    all_args, all_out_shape, sems, aliases, layout = list(args), list(out_shape), [], {}, []
    for j, ph in enumerate(phases):
        counts = []
        for mv in ph.moves:
            all_args += mv.ins
            counts.append(len(mv.ins))
            sems += [pltpu.SemaphoreType.DMA((mv.n_remote,)), pltpu.SemaphoreType.DMA((mv.n_remote,)),
                     pltpu.SemaphoreType.DMA((max(mv.n_local, 1),))]
        if ph.so_far is not None:
            aliases[len(all_args)] = n_out + j
            all_args.append(ph.so_far)
        layout.append((counts, ph.so_far is not None))
        all_out_shape.append(ph.like)
    n_extra_in = len(all_args) - n_in

    def wrapped(*refs):
        ins, pos = refs[:n_in], n_in
        move_ins = []
        for counts, continues in layout:
            per_move = []
            for cnt in counts:
                per_move.append(refs[pos:pos + cnt])
                pos += cnt
            pos += continues
            move_ins.append(per_move)
        outs = refs[pos:pos + n_out]
        bufs = refs[pos + n_out:pos + n_out + len(phases)]
        pos += n_out + len(phases)
        scratch = refs[pos:pos + n_scr]
        pos += n_scr
        move_sems = []
        for ph in phases:
            per_move = []
            for _ in ph.moves:
                per_move.append(refs[pos:pos + 3])
                pos += 3
            move_sems.append(per_move)

        def each_move(fn_name):
            for ph, buf, per_in, per_sem in zip(phases, bufs, move_ins, move_sems):
                for mv, mv_in, mv_sem in zip(ph.moves, per_in, per_sem):
                    getattr(mv, fn_name)(mv_in, buf, mv_sem)

        if phases:
            first = functools.reduce(jnp.logical_and, [pl.program_id(a) == 0 for a in range(len(grid))])
            last = functools.reduce(jnp.logical_and, [pl.program_id(a) == grid[a] - 1 for a in range(len(grid))])
            pl.when(first)(lambda: each_move("start"))
        body(*ins, *outs, *scratch)
        if phases:
            pl.when(last)(lambda: each_move("finish"))

    est = 2 * block_bytes + scratch_bytes
    results = pl.pallas_call(
        wrapped, grid=grid,
        in_specs=list(in_specs) + [HBM_SPEC] * n_extra_in,
        out_specs=list(out_specs) + [HBM_SPEC] * len(phases),
        out_shape=all_out_shape,
        scratch_shapes=list(scratch_shapes) + sems,
        input_output_aliases=aliases,
        compiler_params=pltpu.CompilerParams(dimension_semantics=("arbitrary",) * len(grid),
                                             vmem_limit_bytes=min(VMEM_CEILING, max(32 * 1024 * 1024, 2 * est))),
        name=name,
    )(*all_args)
    return list(results[:n_out]), list(results[n_out:])


def _all_gather(shards, name):
    n = len(shards)

    def body(*refs):
        ins, outs = refs[:n], refs[n:2 * n]
        send_sems, recv_sems, local_sems = refs[2 * n:]
        x, y, c = _place()
        me, sibling = (x, y, c), (x, y, 1 - c)
        chips = _chips(x, y)

        def copy(i, k, block, to, src=None):
            dst = outs[i].at[_index_of(*block)]
            return _rcopy(dst if src is None else src, dst, send_sems.at[i * 7 + k], recv_sems.at[i * 7 + k], to)

        own = [pltpu.make_async_copy(ins[i], outs[i].at[_index_of(*me)], local_sems.at[i]) for i in range(n)]
        for cp in own:
            cp.start()
        first = []
        for i in range(n):
            first.append(copy(i, 0, me, sibling, src=ins[i]))
            first += [copy(i, 1 + j, me, (*chip, c), src=ins[i]) for j, chip in enumerate(chips)]
        for cp in first:
            cp.start()
        passed = []
        for i in range(n):
            for j, chip in enumerate(chips):
                copy(i, 1 + j, (*chip, c), me).wait_recv()
                cp = copy(i, 4 + j, (*chip, c), sibling)
                cp.start()
                passed.append(cp)
        for i in range(n):
            copy(i, 0, sibling, me).wait_recv()
            for j, chip in enumerate(chips):
                copy(i, 4 + j, (*chip, 1 - c), me).wait_recv()
        for cp in first + passed:
            cp.wait_send()
        for cp in own:
            cp.wait()

    return pl.pallas_call(
        body,
        in_specs=[HBM_SPEC] * n, out_specs=[HBM_SPEC] * n,
        out_shape=[_sds((N_DEV,) + s.shape, s.dtype) for s in shards],
        scratch_shapes=[pltpu.SemaphoreType.DMA((7 * n,)), pltpu.SemaphoreType.DMA((7 * n,)),
                        pltpu.SemaphoreType.DMA((n,))],
        name=name,
    )(*shards)


def _mm(a, b, mode, out_dtype, *, tm, tn, tk, name, b_blocked=False, out_blocked=False, m_tiles=None, pair=None, phases=()):
    if mode == "nn":
        (m, k), dims = a.shape, NN
        a_blk, a_map = (tm, tk), (lambda i, j, kk: (i, kk))
        if b_blocked:
            assert b.shape[1] == k and b.shape[2] == tn and tk == k
            n = b.shape[0] * tn
            b_spec = pl.BlockSpec((None, tk, tn), lambda i, j, kk: (j, kk, 0))
        else:
            assert b.shape[0] == k
            n = b.shape[1]
            b_spec = pl.BlockSpec((tk, tn), lambda i, j, kk: (kk, j))
    elif mode == "tn":
        (k, m), n, dims = a.shape, b.shape[1], TN
        assert b.shape[0] == k
        first = 0 if m_tiles is None else m_tiles[0]
        a_blk, a_map = (tk, tm), (lambda i, j, kk: (kk, i + first))
        b_spec = pl.BlockSpec((tk, tn), lambda i, j, kk: (kk, j))
    else:
        (m, k), dims = a.shape, NT
        a_blk, a_map = (tm, tk), (lambda i, j, kk: (i, kk))
        if b_blocked:
            assert b.shape[0] * b.shape[2] == k and b.shape[2] == tk
            n = b.shape[1]
            b_spec = pl.BlockSpec((None, tn, tk), lambda i, j, kk: (kk, j, 0))
        else:
            assert b.shape[1] == k
            n = b.shape[0]
            b_spec = pl.BlockSpec((tn, tk), lambda i, j, kk: (j, kk))
    assert m % tm == 0 and n % tn == 0 and k % tk == 0, (a.shape, b.shape, mode)
    nk = k // tk
    n_row_tiles = m // tm if m_tiles is None else m_tiles[1]
    if out_blocked:
        out_shape, out_spec = _sds((n // tn, n_row_tiles * tm, tn), out_dtype), pl.BlockSpec((None, tm, tn), lambda i, j, kk: (j, i, 0))
    else:
        out_shape, out_spec = _sds((n_row_tiles * tm, n), out_dtype), pl.BlockSpec((tm, tn), lambda i, j, kk: (i, j))

    grid = (n_row_tiles, n // tn, nk)

    def body(a_ref, b_ref, o_ref, *rest):
        p = _dot(a_ref[...], b_ref[...], dims)
        if nk == 1:
            o_ref[...] = p.astype(out_dtype)
            if pair is not None:
                _send_to_sibling(p.astype(out_dtype), *rest)
        else:
            acc_ref, = rest
            kk = pl.program_id(2)

            @pl.when(kk == 0)
            def _():
                acc_ref[...] = p

            @pl.when(kk > 0)
            def _():
                acc_ref[...] += p

            @pl.when(kk == nk - 1)
            def _():
                o_ref[...] = acc_ref[...].astype(out_dtype)

    def _send_to_sibling(tile, pair_ref, stage_ref, send_sems, recv_sem):
        i, j = pl.program_id(0), pl.program_id(1)
        x, y, c = _place()
        blk = pair["block"](i, j)
        k = ((blk >> 2) ^ x) + 2 * (((blk >> 1) & 1) ^ y)
        ordinal = pair["ordinal"](i, j)

        def send(slot):
            return _rcopy(stage_ref.at[slot], pair["dst"](pair_ref, k, i, j), send_sems.at[slot], recv_sem.at[0], (x, y, 1 - c))

        @pl.when((blk & 1) != c)
        def _():
            slot = ordinal & 1

            @pl.when(ordinal >= 2)
            def _():
                send(slot).wait_send()

            stage_ref[slot] = tile
            send(slot).start()

        @pl.when((i == grid[0] - 1) & (j == grid[1] - 1))
        def _():
            send(0).wait_send()
            send(1).wait_send()
            _rcopy(pair_ref, pair_ref, send_sems.at[0], recv_sem.at[0], (x, y, 1 - c)).wait_recv()

    blocks = _nbytes(a_blk, a.dtype) + tk * tn * jnp.dtype(b.dtype).itemsize + _nbytes((tm, tn), out_dtype)
    out_specs, out_shapes, scratch = [out_spec], [out_shape], [] if nk == 1 else [pltpu.VMEM((tm, tn), F32)]
    scratch_bytes = _nbytes((tm, tn), F32) * (nk > 1)
    if pair is not None:
        assert nk == 1
        out_specs, out_shapes = out_specs + [HBM_SPEC], out_shapes + [pair["like"]]
        scratch = [pltpu.VMEM((2, tm, tn), out_dtype), pltpu.SemaphoreType.DMA((2,)), pltpu.SemaphoreType.DMA((1,))]
        scratch_bytes = 2 * _nbytes((tm, tn), out_dtype)
    outs, bufs = _hosted(
        body, name=name, grid=grid,
        in_specs=[pl.BlockSpec(a_blk, a_map), b_spec], out_specs=out_specs, out_shape=out_shapes, args=[a, b],
        scratch_shapes=scratch, block_bytes=blocks, scratch_bytes=scratch_bytes, phases=phases)
    return outs[0], outs[1:] + bufs


RB = 256


def _row_spec(width):
    return pl.BlockSpec((RB, width), lambda i: (i, 0))


def _vec_spec(width):
    return pl.BlockSpec((1, width), lambda i: (0, 0))


def _rinv(x):
    return lax.rsqrt(jnp.mean(x * x, axis=-1, keepdims=True) + EPS)


def _norm_bwd(dyn, xhat, r):
    return r * (dyn - xhat * jnp.mean(dyn * xhat, axis=-1, keepdims=True))


def _colsum(x):
    return jnp.sum(x, axis=0, keepdims=True)


def _accumulate(ref, value):
    @pl.when(pl.program_id(0) == 0)
    def _():
        ref[...] = value

    @pl.when(pl.program_id(0) > 0)
    def _():
        ref[...] += value


def _prenorm(x, gain):
    def body(x_ref, g_ref, h_ref):
        xv = x_ref[...]
        h_ref[...] = (xv * _rinv(xv) * g_ref[...]).astype(BF16)

    outs, _ = _hosted(body, name="prenorm", grid=(S // RB,), in_specs=[_row_spec(D), _vec_spec(D)], out_specs=[_row_spec(D)],
                      out_shape=[_sds((S, D), BF16)], args=[x, gain], block_bytes=RB * D * 6)
    return outs[0]


def _mid_fwd(x, y, npost, npre, phases=()):
    def body(x_ref, y_ref, po_ref, pr_ref, x1_ref, h1_ref):
        yv = y_ref[...]
        x1 = x_ref[...] + yv * _rinv(yv) * po_ref[...]
        x1_ref[...] = x1
        h1_ref[...] = (x1 * _rinv(x1) * pr_ref[...]).astype(BF16)

    return _hosted(body, name="mid_fwd", grid=(S // RB,), in_specs=[_row_spec(D), _row_spec(D), _vec_spec(D), _vec_spec(D)],
                   out_specs=[_row_spec(D), _row_spec(D)], out_shape=[_sds((S, D), F32), _sds((S, D), BF16)],
                   args=[x, y, npost, npre], block_bytes=RB * D * 14, phases=phases)


def _final(x1, y1, tgt, npost):
    def body(x_ref, y_ref, t_ref, po_ref, loss_ref, dx_ref, dy_ref, dpo_ref):
        yv = y_ref[...]
        r = _rinv(yv)
        yhat = yv * r
        err = x_ref[...] + yhat * po_ref[...] - t_ref[...]
        dx = err * (1.0 / D)
        dx_ref[...] = dx
        dy_ref[...] = _norm_bwd(dx * po_ref[...], yhat, r).astype(BF16)
        _accumulate(loss_ref, _colsum(err * err))
        _accumulate(dpo_ref, _colsum(dx * yhat))

    outs, _ = _hosted(body, name="final", grid=(S // RB,), in_specs=[_row_spec(D), _row_spec(D), _row_spec(D), _vec_spec(D)],
                      out_specs=[_vec_spec(D), _row_spec(D), _row_spec(D), _vec_spec(D)],
                      out_shape=[_sds((1, D), F32), _sds((S, D), F32), _sds((S, D), BF16), _sds((1, D), F32)],
                      args=[x1, y1, tgt, npost], block_bytes=RB * D * 18)
    return outs


def _mid_bwd(dx2, dh1, x1, y0, npre, npost, phases=()):
    def body(dx2_ref, dh_ref, x_ref, y_ref, pr_ref, po_ref, dx1_ref, dy_ref, dpr_ref, dpo_ref):
        xv = x_ref[...]
        r = _rinv(xv)
        xhat = xv * r
        dh = dh_ref[...]
        dx1 = dx2_ref[...] + _norm_bwd(dh * pr_ref[...], xhat, r)
        dx1_ref[...] = dx1
        yv = y_ref[...]
        ry = _rinv(yv)
        yhat = yv * ry
        dy_ref[...] = _norm_bwd(dx1 * po_ref[...], yhat, ry).astype(BF16)
        _accumulate(dpr_ref, _colsum(dh * xhat))
        _accumulate(dpo_ref, _colsum(dx1 * yhat))

    return _hosted(body, name="mid_bwd", grid=(S // RB,), in_specs=[_row_spec(D)] * 4 + [_vec_spec(D)] * 2,
                   out_specs=[_row_spec(D), _row_spec(D), _vec_spec(D), _vec_spec(D)],
                   out_shape=[_sds((S, D), F32), _sds((S, D), BF16), _sds((1, D), F32), _sds((1, D), F32)],
                   args=[dx2, dh1, x1, y0, npre, npost], block_bytes=RB * D * 22, phases=phases)


def _first_bwd(dx1, dh0, x0, npre, phases=()):
    def body(dx1_ref, dh_ref, x_ref, pr_ref, gx_ref, dpr_ref):
        xv = x_ref[...]
        r = _rinv(xv)
        xhat = xv * r
        dh = dh_ref[...]
        gx_ref[...] = dx1_ref[...] + _norm_bwd(dh * pr_ref[...], xhat, r)
        _accumulate(dpr_ref, _colsum(dh * xhat))

    return _hosted(body, name="first_bwd", grid=(S // RB,), in_specs=[_row_spec(D)] * 3 + [_vec_spec(D)],
                   out_specs=[_row_spec(D), _vec_spec(D)], out_shape=[_sds((S, D), F32), _sds((1, D), F32)],
                   args=[dx1, dh0, x0, npre], block_bytes=RB * D * 16, phases=phases)


GLA_RB = 256
GLA_CPB = GLA_RB // C


def _sigmoid(x):
    return 1.0 / (1.0 + jnp.exp(-x))


def _tri(strict):
    r = lax.broadcasted_iota(jnp.int32, (C, C), 0)
    c = lax.broadcasted_iota(jnp.int32, (C, C), 1)
    return jnp.where(c < r if strict else c <= r, 1.0, 0.0).astype(F32)


def _gla_gate(glr_b, w2_h, b_h, tri):
    z = _dot(glr_b, w2_h) + b_h
    log_a = (jnp.minimum(z, 0.0) - jnp.log(1.0 + jnp.exp(-jnp.abs(z)))) * (1.0 / GLA_TAU)
    bcum = _dot(tri, log_a, precision=HIGHEST)
    b_end = jnp.sum(log_a, axis=0, keepdims=True)
    return z, jnp.exp(b_end - bcum), jnp.exp(b_end)


def _gla_fwd(proj, w2p, bgate, ogain, phases=()):
    def body(p_ref, w2_ref, b_ref, og_ref, y_ref, st_out_ref, st_ref):
        @pl.when(pl.program_id(0) == 0)
        def _():
            st_ref[...] = jnp.zeros_like(st_ref)

        tri = _tri(False)

        def chunk(ci, carry):
            rows = pl.ds(pl.multiple_of(ci * C, C), C)
            glr_b = p_ref[rows, LR0:LR0 + LRP].astype(BF16)
            for h in range(H):
                _, ea, dec = _gla_gate(glr_b, w2_ref[:, h * DK:(h + 1) * DK], b_ref[:, h * DK:(h + 1) * DK], tri)
                k_dec = (p_ref[rows, K0 + h * DK:K0 + (h + 1) * DK] * ea).astype(BF16)
                v_b = p_ref[rows, V0 + h * DV:V0 + (h + 1) * DV].astype(BF16)
                st = st_ref[h] * dec + _dot(v_b, k_dec, TN)
                st_ref[h] = st
                st_b = st.astype(BF16)
                st_out_ref[ci, h] = st_b
                q_b = (p_ref[rows, Q0 + h * DK:Q0 + (h + 1) * DK] * (DK ** -0.5)).astype(BF16)
                o = _dot(q_b, st_b, NT)
                on = o * _rinv(o)
                g = p_ref[rows, G0 + h * DV:G0 + (h + 1) * DV]
                y_ref[rows, h * DV:(h + 1) * DV] = (on * og_ref[:, h * DV:(h + 1) * DV] * (g * _sigmoid(g))).astype(BF16)
            return carry

        lax.fori_loop(0, GLA_CPB, chunk, 0)

    blocks = GLA_RB * GLA_PAD * 4 + GLA_RB * D * 2 + GLA_CPB * H * DV * DK * 2
    return _hosted(
        body, name="gla_fwd", grid=(S // GLA_RB,),
        in_specs=[pl.BlockSpec((GLA_RB, GLA_PAD), lambda i: (i, 0)),
                  pl.BlockSpec((LRP, H * DK), lambda i: (0, 0)),
                  pl.BlockSpec((1, H * DK), lambda i: (0, 0)),
                  pl.BlockSpec((1, H * DV), lambda i: (0, 0))],
        out_specs=[pl.BlockSpec((GLA_RB, H * DV), lambda i: (i, 0)),
                   pl.BlockSpec((GLA_CPB, H, DV, DK), lambda i: (i, 0, 0, 0))],
        out_shape=[_sds((S, H * DV), BF16), _sds((NC, H, DV, DK), BF16)],
        args=[proj, w2p, bgate, ogain], scratch_shapes=[pltpu.VMEM((H, DV, DK), F32)],
        block_bytes=blocks, scratch_bytes=H * DV * DK * 4, phases=phases)


def _gla_bwd(proj, dypre, states, w2p, bgate, ogain, phases=()):
    nb = S // GLA_RB

    def body(p_ref, dy_ref, st_blk_ref, st_prev_ref, w2_ref, b_ref, og_ref,
             dp_ref, dog_ref, dbg_ref, dw2_ref, r_ref):
        step = pl.program_id(0)

        @pl.when(step == 0)
        def _():
            r_ref[...] = jnp.zeros_like(r_ref)
            dog_ref[...] = jnp.zeros_like(dog_ref)
            dbg_ref[...] = jnp.zeros_like(dbg_ref)
            dw2_ref[...] = jnp.zeros_like(dw2_ref)

        tri = _tri(False)
        tri_strict = _tri(True)
        has_prev = jnp.where(step < nb - 1, 1.0, 0.0).astype(F32)

        def chunk(ci, st_prev_of):
            rows = pl.ds(ci * C if isinstance(ci, int) else pl.multiple_of(ci * C, C), C)
            glr_b = p_ref[rows, LR0:LR0 + LRP].astype(BF16)
            dglr = jnp.zeros((C, LRP), F32)
            for h in range(H):
                kcol = slice(h * DK, (h + 1) * DK)
                vcol = slice(h * DV, (h + 1) * DV)
                w2_h = w2_ref[:, kcol]
                z, ea, dec = _gla_gate(glr_b, w2_h, b_ref[:, kcol], tri)
                k_dec = p_ref[rows, K0 + h * DK:K0 + (h + 1) * DK] * ea
                k_dec_b = k_dec.astype(BF16)
                v_b = p_ref[rows, V0 + h * DV:V0 + (h + 1) * DV].astype(BF16)
                q_b = (p_ref[rows, Q0 + h * DK:Q0 + (h + 1) * DK] * (DK ** -0.5)).astype(BF16)
                st_b = st_blk_ref[ci, h]
                o = _dot(q_b, st_b, NT)
                rinv = _rinv(o)
                on = o * rinv
                g = p_ref[rows, G0 + h * DV:G0 + (h + 1) * DV]
                sg = _sigmoid(g)
                og = og_ref[:, vcol]
                dyp = dy_ref[rows, vcol]
                dp_ref[rows, G0 + h * DV:G0 + (h + 1) * DV] = (dyp * (on * og) * (sg * (1.0 + g * (1.0 - sg)))).astype(BF16)
                dpn = dyp * (g * sg)
                dog_ref[:, vcol] += _colsum(dpn * on)
                do_b = _norm_bwd(dpn * og, on, rinv).astype(BF16)
                gt = _dot(do_b, q_b, TN) + r_ref[h]
                gt_b = gt.astype(BF16)
                dp_ref[rows, Q0 + h * DK:Q0 + (h + 1) * DK] = (_dot(do_b, st_b) * (DK ** -0.5)).astype(BF16)
                dkd = _dot(v_b, gt_b)
                dp_ref[rows, V0 + h * DV:V0 + (h + 1) * DV] = _dot(k_dec_b, gt_b, NT).astype(BF16)
                dp_ref[rows, K0 + h * DK:K0 + (h + 1) * DK] = (dkd * ea).astype(BF16)
                ddec = _colsum(gt * st_prev_of(h))
                dla = _dot(tri_strict, dkd * k_dec, precision=HIGHEST) + ddec * dec
                dz = dla * (1.0 / GLA_TAU) * (1.0 - _sigmoid(z))
                dz_b = dz.astype(BF16)
                r_ref[h] = gt * dec
                dbg_ref[:, kcol] += _colsum(dz)
                dw2_ref[:, kcol] += _dot(glr_b, dz_b, TN)
                dglr = dglr + _dot(dz_b, w2_h, NT)
            dp_ref[rows, LR0:LR0 + LRP] = dglr.astype(BF16)

        def later_chunk(t, carry):
            ci = GLA_CPB - 1 - t
            chunk(ci, lambda h: st_blk_ref[ci - 1, h].astype(F32))
            return carry

        lax.fori_loop(0, GLA_CPB - 1, later_chunk, 0)
        chunk(0, lambda h: st_prev_ref[0, h].astype(F32) * has_prev)

    blocks = (GLA_RB * GLA_PAD * 4 + GLA_RB * D * 4 + (GLA_CPB + 1) * H * DV * DK * 2 + GLA_RB * GLA_PAD * 2)
    rev = lambda i: nb - 1 - i
    return _hosted(
        body, name="gla_bwd", grid=(nb,),
        in_specs=[pl.BlockSpec((GLA_RB, GLA_PAD), lambda i: (rev(i), 0)),
                  pl.BlockSpec((GLA_RB, H * DV), lambda i: (rev(i), 0)),
                  pl.BlockSpec((GLA_CPB, H, DV, DK), lambda i: (rev(i), 0, 0, 0)),
                  pl.BlockSpec((1, H, DV, DK), lambda i: (jnp.maximum(rev(i) * GLA_CPB - 1, 0), 0, 0, 0)),
                  pl.BlockSpec((LRP, H * DK), lambda i: (0, 0)),
                  pl.BlockSpec((1, H * DK), lambda i: (0, 0)),
                  pl.BlockSpec((1, H * DV), lambda i: (0, 0))],
        out_specs=[pl.BlockSpec((GLA_RB, GLA_PAD), lambda i: (rev(i), 0)),
                   pl.BlockSpec((1, H * DV), lambda i: (0, 0)),
                   pl.BlockSpec((1, H * DK), lambda i: (0, 0)),
                   pl.BlockSpec((LRP, H * DK), lambda i: (0, 0))],
        out_shape=[_sds((S, GLA_PAD), BF16), _sds((1, H * DV), F32), _sds((1, H * DK), F32), _sds((LRP, H * DK), F32)],
        args=[proj, dypre, states, states, w2p, bgate, ogain], scratch_shapes=[pltpu.VMEM((H, DV, DK), F32)],
        block_bytes=blocks, scratch_bytes=H * DV * DK * 4, phases=phases)


SGU_RB = 256
GELU_C = 0.7978845608028654
GELU_A = 0.044715


def _gelu(x):
    return 0.5 * x * (1.0 + jnp.tanh(GELU_C * (x + GELU_A * x * x * x)))


def _gelu_grad(x):
    t = jnp.tanh(GELU_C * (x + GELU_A * x * x * x))
    return 0.5 * (1.0 + t) + 0.5 * x * (1.0 - t * t) * (GELU_C * (1.0 + 3.0 * GELU_A * x * x))


def _causal_mask(transposed=False):
    i = lax.broadcasted_iota(jnp.int32, (SGU_BLOCK, SGU_BLOCK), 1 if transposed else 0)
    j = lax.broadcasted_iota(jnp.int32, (SGU_BLOCK, SGU_BLOCK), 0 if transposed else 1)
    return (i >= C) | (j < C)


def _layer_norm(vf, gain, bias):
    mu = jnp.mean(vf, axis=-1, keepdims=True)
    cen = vf - mu
    rstd = lax.rsqrt(jnp.mean(cen * cen, axis=-1, keepdims=True) + EPS)
    xhat = cen * rstd
    return xhat, rstd, xhat * gain + bias


def _sgu_fwd(proj, lng, lnb, ws, bsb, phases=()):
    def body(p_ref, g_ref, b_ref, ws_ref, bs_ref, o_ref):
        mask = _causal_mask()
        for n in range(SGU_RB // SGU_BLOCK):
            rows = slice(n * SGU_BLOCK, (n + 1) * SGU_BLOCK)
            _, _, vn = _layer_norm(_gelu(p_ref[rows, D:2 * D]), g_ref[...], b_ref[...])
            vn_b = vn.astype(BF16)
            for gi in range(SGU_G):
                cols = slice(gi * SGU_GD, (gi + 1) * SGU_GD)
                w = jnp.where(mask, ws_ref[gi], 0.0).astype(BF16)
                vs = _dot(w, vn_b[:, cols]) + bs_ref[gi]
                gate = p_ref[rows, 2 * D + gi * SGU_GD:2 * D + (gi + 1) * SGU_GD]
                o_ref[rows, cols] = (_gelu(p_ref[rows, cols]) * vs * (gate * _sigmoid(gate))).astype(BF16)

    blocks = SGU_RB * SGU_COLS * 4 + SGU_RB * D * 2 + SGU_G * SGU_BLOCK * (SGU_BLOCK + SGU_GD) * 4
    return _hosted(
        body, name="sgu_fwd", grid=(S // SGU_RB,),
        in_specs=[pl.BlockSpec((SGU_RB, SGU_COLS), lambda i: (i, 0)),
                  pl.BlockSpec((1, D), lambda i: (0, 0)), pl.BlockSpec((1, D), lambda i: (0, 0)),
                  pl.BlockSpec((SGU_G, SGU_BLOCK, SGU_BLOCK), lambda i: (0, 0, 0)),
                  pl.BlockSpec((SGU_G, SGU_BLOCK, SGU_GD), lambda i: (0, 0, 0))],
        out_specs=[pl.BlockSpec((SGU_RB, D), lambda i: (i, 0))], out_shape=[_sds((S, D), BF16)],
        args=[proj, lng, lnb, ws, bsb], block_bytes=blocks, phases=phases)


def _sgu_bwd(proj, dpre, lng, lnb, ws, wst, bsb, phases=()):
    nsteps = S // SGU_RB

    def body(p_ref, d_ref, g_ref, b_ref, ws_ref, wst_ref, bs_ref,
             dp_ref, dg_ref, db_ref, dws_ref, dbs_ref, dvn_ref, dvs_acc_ref):
        step = pl.program_id(0)

        @pl.when(step == 0)
        def _():
            dg_ref[...] = jnp.zeros_like(dg_ref)
            db_ref[...] = jnp.zeros_like(db_ref)
            dws_ref[...] = jnp.zeros_like(dws_ref)
            dvs_acc_ref[...] = jnp.zeros_like(dvs_acc_ref)

        mask = _causal_mask()
        maskt = _causal_mask(transposed=True)
        for n in range(SGU_RB // SGU_BLOCK):
            rows = slice(n * SGU_BLOCK, (n + 1) * SGU_BLOCK)
            v = p_ref[rows, D:2 * D]
            xhat, rstd, vn = _layer_norm(_gelu(v), g_ref[...], b_ref[...])
            vn_b = vn.astype(BF16)
            for gi in range(SGU_G):
                cols = slice(gi * SGU_GD, (gi + 1) * SGU_GD)
                w = jnp.where(mask, ws_ref[gi], 0.0).astype(BF16)
                wt = jnp.where(maskt, wst_ref[gi], 0.0).astype(BF16)
                vs = _dot(w, vn_b[:, cols]) + bs_ref[gi]
                u = p_ref[rows, cols]
                gate = p_ref[rows, 2 * D + gi * SGU_GD:2 * D + (gi + 1) * SGU_GD]
                sg = _sigmoid(gate)
                gu = _gelu(u)
                dpre_g = d_ref[rows, cols]
                t = dpre_g * (gate * sg)
                dp_ref[rows, cols] = (t * vs * _gelu_grad(u)).astype(BF16)
                dp_ref[rows, 2 * D + gi * SGU_GD:2 * D + (gi + 1) * SGU_GD] = (
                    dpre_g * gu * vs * (sg * (1.0 + gate * (1.0 - sg)))).astype(BF16)
                dvs = t * gu
                dvs_b = dvs.astype(BF16)
                dvs_acc_ref[:, cols] += dvs
                dws_ref[gi] += _dot(dvs_b, vn_b[:, cols], NT)
                dvn_ref[:, cols] = _dot(wt, dvs_b)
            dvn = dvn_ref[...]
            dg_ref[...] += _colsum(dvn * xhat)
            db_ref[...] += _colsum(dvn)
            dxh = dvn * g_ref[...]
            dvf = rstd * (dxh - jnp.mean(dxh, axis=-1, keepdims=True) - xhat * jnp.mean(dxh * xhat, axis=-1, keepdims=True))
            dp_ref[rows, D:2 * D] = (dvf * _gelu_grad(v)).astype(BF16)

        @pl.when(step == nsteps - 1)
        def _():
            lane = lax.broadcasted_iota(jnp.int32, (SGU_BLOCK, SGU_BLOCK), 1)
            out = jnp.zeros((SGU_BLOCK, SGU_BLOCK), F32)
            for gi in range(SGU_G):
                out = out + jnp.where(lane == gi, jnp.sum(dvs_acc_ref[:, gi * SGU_GD:(gi + 1) * SGU_GD], axis=1, keepdims=True), 0.0)
                dws_ref[gi] = jnp.where(mask, dws_ref[gi], 0.0)
            dbs_ref[...] = out

    blocks = SGU_RB * SGU_COLS * 6 + SGU_RB * D * 4 + SGU_G * SGU_BLOCK * (3 * SGU_BLOCK + SGU_GD) * 4
    const3 = lambda i: (0, 0, 0)
    return _hosted(
        body, name="sgu_bwd", grid=(nsteps,),
        in_specs=[pl.BlockSpec((SGU_RB, SGU_COLS), lambda i: (i, 0)),
                  pl.BlockSpec((SGU_RB, D), lambda i: (i, 0)),
                  pl.BlockSpec((1, D), lambda i: (0, 0)), pl.BlockSpec((1, D), lambda i: (0, 0)),
                  pl.BlockSpec((SGU_G, SGU_BLOCK, SGU_BLOCK), const3),
                  pl.BlockSpec((SGU_G, SGU_BLOCK, SGU_BLOCK), const3),
                  pl.BlockSpec((SGU_G, SGU_BLOCK, SGU_GD), const3)],
        out_specs=[pl.BlockSpec((SGU_RB, SGU_COLS), lambda i: (i, 0)),
                   pl.BlockSpec((1, D), lambda i: (0, 0)), pl.BlockSpec((1, D), lambda i: (0, 0)),
                   pl.BlockSpec((SGU_G, SGU_BLOCK, SGU_BLOCK), const3),
                   pl.BlockSpec((SGU_BLOCK, SGU_BLOCK), lambda i: (0, 0))],
        out_shape=[_sds((S, SGU_COLS), BF16), _sds((1, D), F32), _sds((1, D), F32),
                   _sds((SGU_G, SGU_BLOCK, SGU_BLOCK), F32), _sds((SGU_BLOCK, SGU_BLOCK), F32)],
        args=[proj, dpre, lng, lnb, ws, wst, bsb],
        scratch_shapes=[pltpu.VMEM((SGU_BLOCK, D), F32), pltpu.VMEM((SGU_BLOCK, D), F32)],
        block_bytes=blocks, scratch_bytes=2 * SGU_BLOCK * D * 4, phases=phases)


def _pair_sum(own3, a, r0, name):
    _, nr, c = own3.shape
    tr = 256
    assert r0 % tr == 0 and nr % tr == 0

    def body(own_ref, sib_ref, o_ref):
        o_ref[...] = (own_ref[...].astype(F32) + sib_ref[...].astype(F32)).astype(BF16)

    cpad = -(-c // 128) * 128
    outs, _ = _hosted(
        body, name=name, grid=(3, nr // tr),
        in_specs=[pl.BlockSpec((None, tr, c), lambda j, i: (j, i, 0)),
                  pl.BlockSpec((None, tr, c), lambda j, i: (1 + j, r0 // tr + i, 0))],
        out_specs=[pl.BlockSpec((None, tr, c), lambda j, i: (j, i, 0))], out_shape=[_sds((3, nr, c), BF16)],
        args=[own3, a], block_bytes=3 * tr * cpad * 2)
    return outs[0]


def _adamw_math(w, g, m, v):
    m = ADAM_B1 * m + (1.0 - ADAM_B1) * g
    v = ADAM_B2 * v + (1.0 - ADAM_B2) * (g * g)
    m_hat = m / (1.0 - ADAM_B1 ** ADAM_STEP)
    v_hat = v / (1.0 - ADAM_B2 ** ADAM_STEP)
    delta = -ADAM_LR * (m_hat / (jnp.sqrt(v_hat) + ADAM_EPS) + ADAM_WD * w)
    return delta, m, v


def _sum_adamw(own, a, b, w, m, v, *, name, phases=()):
    r, c = w.shape
    tr = 256

    def body(own_ref, sib_ref, far_ref, w_ref, m_ref, v_ref, g_ref, d_ref, nm_ref, nv_ref):
        g = own_ref[...].astype(F32) + sib_ref[...].astype(F32)
        for j in range(3):
            g = g + far_ref[j].astype(F32)
        g_ref[...] = g
        d_ref[...], nm_ref[...], nv_ref[...] = _adamw_math(w_ref[...], g, m_ref[...], v_ref[...])

    spec = pl.BlockSpec((tr, c), lambda i: (i, 0))
    cpad = -(-c // 128) * 128
    return _hosted(
        body, name=name, grid=(r // tr,),
        in_specs=[spec, pl.BlockSpec((None, tr, c), lambda i: (0, i, 0)),
                  pl.BlockSpec((3, tr, c), lambda i: (0, i, 0)), spec, spec, spec],
        out_specs=[spec] * 4, out_shape=[_sds((r, c), F32)] * 4, args=[own, a, b, w, m, v],
        block_bytes=5 * tr * cpad * 2 + 7 * tr * cpad * 4, phases=phases)


def _sum_parts(parts, name):
    n, r, c = parts.shape

    def body(p_ref, o_ref):
        g = p_ref[0]
        for j in range(1, n):
            g = g + p_ref[j]
        o_ref[...] = g

    outs, _ = _hosted(body, name=name, grid=(1,), in_specs=[pl.BlockSpec((n, r, c), lambda i: (0, 0, 0))],
                      out_specs=[pl.BlockSpec((r, c), lambda i: (0, 0))], out_shape=[_sds((r, c), F32)], args=[parts],
                      block_bytes=(n + 1) * r * c * 4)
    return outs[0]


def _adamw(w, g, m, v, name):
    def body(w_ref, g_ref, m_ref, v_ref, d_ref, nm_ref, nv_ref):
        d_ref[...], nm_ref[...], nv_ref[...] = _adamw_math(w_ref[...], g_ref[...], m_ref[...], v_ref[...])

    spec = pl.BlockSpec(w.shape, lambda i: (0, 0))
    outs, _ = _hosted(body, name=name, grid=(1,), in_specs=[spec] * 4, out_specs=[spec] * 3, out_shape=[_sds(w.shape, F32)] * 3,
                      args=[w, g, m, v], block_bytes=7 * _nbytes(w.shape, F32))
    return outs


def _blocks_to_columns(g):
    n, r, c = g.shape
    return jnp.transpose(g, (1, 0, 2)).reshape(r, n * c)


def _columns_to_blocks(w, c):
    r = w.shape[0]
    return jnp.transpose(w.reshape(r, N_DEV, c), (1, 0, 2))


def _pack(parts):
    return jnp.concatenate([p.reshape(-1) for p in parts]).reshape(-1, 128)


def _unpack(packed, like):
    flat, outs, off = packed.reshape(-1), [], 0
    for p in like:
        outs.append(flat[off:off + p.size].reshape(p.shape))
        off += p.size
    return outs


def kernel(x, norm_pre, norm_post, gla_w_in, gla_w_gate2, gla_b_gate, gla_o_gain, gla_w_out, sgu_w_in, sgu_ln_gain, sgu_ln_bias, sgu_w_spatial, sgu_b_spatial, sgu_w_out, loss_target, m_norm_pre, m_norm_post, m_gla_w_in, m_gla_w_gate2, m_gla_b_gate, m_gla_o_gain, m_gla_w_out, m_sgu_w_in, m_sgu_ln_gain, m_sgu_ln_bias, m_sgu_w_spatial, m_sgu_b_spatial, m_sgu_w_out, v_norm_pre, v_norm_post, v_gla_w_in, v_gla_w_gate2, v_gla_b_gate, v_gla_o_gain, v_gla_w_out, v_sgu_w_in, v_sgu_ln_gain, v_sgu_ln_bias, v_sgu_w_spatial, v_sgu_b_spatial, v_sgu_w_out):
    me = _index_of(*_place())
    x0 = x.reshape(S, D)
    tgt = loss_target.reshape(S, D)
    npre0, npre1 = norm_pre[0:1], norm_pre[1:2]
    npost0, npost1 = norm_post[0:1], norm_post[1:2]
    ws = sgu_w_spatial[0]
    wst = jnp.transpose(ws, (0, 2, 1))
    bsb = jnp.broadcast_to(sgu_b_spatial[0][:, :, None], (SGU_G, SGU_BLOCK, SGU_GD))
    W_ROWS = D // N_DEV
    IN_COLS_G, IN_COLS_S = GLA_COLS // N_DEV, SGU_COLS // N_DEV

    s_gwi, s_gwo = gla_w_in[0].astype(BF16), gla_w_out[0].astype(BF16)
    s_swi, s_swo = sgu_w_in[0].astype(BF16), sgu_w_out[0].astype(BF16)
    small = jnp.concatenate([jnp.pad(gla_w_gate2[0].reshape(4, 512), ((0, 4), (0, 0))),
                             jnp.pad(jnp.concatenate([sgu_ln_gain, sgu_ln_bias], axis=1), ((0, 7), (0, 0)))], axis=0)

    g_gwi, g_small = _all_gather([s_gwi, small], "gather_first")
    wg_in = jnp.pad(_blocks_to_columns(g_gwi), ((0, 0), (0, GLA_PAD - GLA_COLS)))
    w2 = _blocks_to_columns(g_small[:, :4, :].reshape(N_DEV, LR, 128))
    w2p = jnp.pad(w2, ((0, LRP - LR), (0, 0))).astype(BF16)
    lng = g_small[:, 8, :256].reshape(1, D)
    lnb = g_small[:, 8, 256:].reshape(1, D)
    like_gwo, like_swi = _sds((N_DEV, W_ROWS, D), BF16), _sds((N_DEV, D, IN_COLS_S), BF16)

    h0 = _prenorm(x0, npre0)
    proj0, (g_gwo, g_swi) = _mm(h0, wg_in, "nn", F32, tm=1024, tn=896, tk=D, name="gla_in", phases=[
        _Phase(like_gwo, None, [_gather_send(s_gwo, 0, W_ROWS)]),
        _Phase(like_swi, None, [_gather_send(s_swi, 0, 256)])])
    (ypre0, states), (g_gwo, g_swi) = _gla_fwd(proj0, w2p, gla_b_gate, gla_o_gain, phases=[
        _Phase(like_gwo, g_gwo, [_gather_pass(0, W_ROWS)]),
        _Phase(like_swi, g_swi, [_gather_send(s_swi, 256, 1280), _gather_pass(0, 256)])])
    wg_out = g_gwo.reshape(D, D)
    y0, (g_swi,) = _mm(ypre0, wg_out, "nn", F32, tm=1024, tn=1024, tk=D, name="gla_out", phases=[
        _Phase(like_swi, g_swi, [_gather_send(s_swi, 1536, 512), _gather_pass(256, 1280)])])
    (x1, h1), (g_swi,) = _mid_fwd(x0, y0, npost0, npre1, phases=[_Phase(like_swi, g_swi, [_gather_pass(1536, 512)])])
    proj1, (g_swo,) = _mm(h1, g_swi, "nn", F32, tm=1024, tn=IN_COLS_S, tk=D, name="sgu_in", b_blocked=True, phases=[
        _Phase(like_gwo, None, [_gather_send(s_swo, 0, W_ROWS)])])
    (pre1,), (g_swo,) = _sgu_fwd(proj1, lng, lnb, ws, bsb, phases=[_Phase(like_gwo, g_swo, [_gather_pass(0, W_ROWS)])])
    ws_out = g_swo.reshape(D, D)
    y1, _ = _mm(pre1, ws_out, "nn", F32, tm=1024, tn=1024, tk=D, name="sgu_out")
    loss_cols, dx2, dy1, dnpost1 = _final(x1, y1, tgt, npost1)
    loss = lax.psum(0.5 * jnp.sum(loss_cols) / D, ("x", "y", "c"))

    like_b_out, like_b_swi = _sds((3, W_ROWS, D), BF16), _sds((3, D, IN_COLS_S), BF16)
    like_a_gwi, like_b_gwi = _sds((4, D, IN_COLS_G), BF16), _sds((3, D, IN_COLS_G), BF16)
    row_pair = dict(like=_sds((4, W_ROWS, D), BF16), block=lambda i, j: i, ordinal=lambda i, j: i >> 1,
                    dst=lambda ref, k, i, j: ref.at[k])
    col_pair = dict(like=_sds((4, D, IN_COLS_S), BF16), block=lambda i, j: j, ordinal=lambda i, j: 4 * i + (j >> 1),
                    dst=lambda ref, k, i, j: ref.at[k, pl.ds(pl.multiple_of(i * 1024, 1024), 1024)])

    dws_out, (a_swo,) = _mm(pre1, dy1, "tn", BF16, tm=W_ROWS, tn=D, tk=S, name="sgu_out_dw", pair=row_pair)
    own_swo, own3_swo = _own_blocks(dws_out.reshape(N_DEV, W_ROWS, D))
    t_swo = _pair_sum(own3_swo, a_swo, 0, "pair_sum_sgu_w_out")
    dpre1, _ = _mm(dy1, ws_out, "nt", F32, tm=1024, tn=1024, tk=D, name="sgu_out_dx")
    (dproj1, dlng, dlnb, dwsp, dbsp), (b_swo,) = _sgu_bwd(proj1, dpre1, lng, lnb, ws, wst, bsb, phases=[
        _Phase(like_b_out, None, [_reduce_cross(t_swo, 0, 0, W_ROWS)])])
    p_swi, (a_swi,) = _mm(h1, dproj1, "tn", BF16, tm=1024, tn=IN_COLS_S, tk=S, name="sgu_in_dw", out_blocked=True, pair=col_pair)
    own_swi, own3_swi = _own_blocks(p_swi)
    t_swi = _pair_sum(own3_swi, a_swi, 0, "pair_sum_sgu_w_in")
    dh1, (b_swi,) = _mm(dproj1, g_swi, "nt", F32, tm=1024, tn=1024, tk=IN_COLS_S, name="sgu_in_dx", b_blocked=True, phases=[
        _Phase(like_b_swi, None, [_reduce_cross(t_swi, 0, 0, 1024)])])
    (dx1, dy0, dnpre1, dnpost0), (b_swi,) = _mid_bwd(dx2, dh1, x1, y0, npre1, npost0, phases=[
        _Phase(like_b_swi, b_swi, [_reduce_cross(t_swi, 1024, 1024, 512)])])
    dwg_out, (a_gwo, b_swi) = _mm(ypre0, dy0, "tn", BF16, tm=W_ROWS, tn=D, tk=S, name="gla_out_dw", pair=row_pair, phases=[
        _Phase(like_b_swi, b_swi, [_reduce_cross(t_swi, 1536, 1536, 512)])])
    own_gwo, own3_gwo = _own_blocks(dwg_out.reshape(N_DEV, W_ROWS, D))
    t_gwo = _pair_sum(own3_gwo, a_gwo, 0, "pair_sum_gla_w_out")
    dypre0, _ = _mm(dy0, wg_out, "nt", F32, tm=1024, tn=1024, tk=D, name="gla_out_dx")
    late = [dnpre1, dnpost1, dlng, dlnb, dwsp, jnp.transpose(dbsp[:, :SGU_G])]
    late_pack = _pack(late)
    (dproj0, dogain, dbgate, dw2), (b_gwo, g_late) = _gla_bwd(proj0, dypre0, states, w2p, gla_b_gate, gla_o_gain, phases=[
        _Phase(like_b_out, None, [_reduce_cross(t_gwo, 0, 0, W_ROWS)]),
        _Phase(_sds((N_DEV,) + late_pack.shape, F32), None, [_gather_send(late_pack, 0, late_pack.shape[0])])])
    half = D // 2
    dwg_in_a, (g_late,) = _mm(h0, dproj0, "tn", BF16, tm=half, tn=896, tk=S, name="gla_in_dw_a", m_tiles=(0, 1), phases=[
        _Phase(_sds((N_DEV,) + late_pack.shape, F32), g_late, [_gather_pass(0, late_pack.shape[0])])])
    p_gwi_a = _columns_to_blocks(dwg_in_a[:, :GLA_COLS], IN_COLS_G)
    own_gwi_a, own3_gwi_a = _own_blocks(p_gwi_a)
    dwg_in_b, (a_gwi,) = _mm(h0, dproj0, "tn", BF16, tm=half, tn=896, tk=S, name="gla_in_dw_b", m_tiles=(1, 1), phases=[
        _Phase(like_a_gwi, None, [_reduce_pair(p_gwi_a, 0, 0, half)])])
    p_gwi_b = _columns_to_blocks(dwg_in_b[:, :GLA_COLS], IN_COLS_G)
    own_gwi_b, own3_gwi_b = _own_blocks(p_gwi_b)
    t_gwi_a = _pair_sum(own3_gwi_a, a_gwi, 0, "pair_sum_gla_w_in_a")
    dh0, (b_gwi, a_gwi) = _mm(dproj0, wg_in, "nt", F32, tm=1024, tn=1024, tk=896, name="gla_in_dx", phases=[
        _Phase(like_b_gwi, None, [_reduce_cross(t_gwi_a, 0, 0, half)]),
        _Phase(like_a_gwi, a_gwi, [_reduce_pair(p_gwi_b, 0, half, half)])])
    t_gwi_b = _pair_sum(own3_gwi_b, a_gwi, half, "pair_sum_gla_w_in_b")
    (grad_x, dnpre0), (b_gwi,) = _first_bwd(dx1, dh0, x0, npre0, phases=[
        _Phase(like_b_gwi, b_gwi, [_reduce_cross(t_gwi_b, 0, half, 256)])])

    early = [dnpre0, dnpost0, dbgate, dogain, dw2[:LR]]
    early_pack = _pack(early)
    like_early = _sds((N_DEV,) + early_pack.shape, F32)
    (g_swo, d_swo, nm_swo, nv_swo), (b_gwi, g_early) = _sum_adamw(
        own_swo, a_swo, b_swo, sgu_w_out[0], m_sgu_w_out[0], v_sgu_w_out[0], name="adamw_sgu_w_out", phases=[
            _Phase(like_b_gwi, b_gwi, [_reduce_cross(t_gwi_b, 256, half + 256, 256)]),
            _Phase(like_early, None, [_gather_send(early_pack, 0, early_pack.shape[0])])])
    (g_swi_, d_swi, nm_swi, nv_swi), (b_gwi, g_early) = _sum_adamw(
        own_swi, a_swi, b_swi, sgu_w_in[0], m_sgu_w_in[0], v_sgu_w_in[0], name="adamw_sgu_w_in", phases=[
            _Phase(like_b_gwi, b_gwi, [_reduce_cross(t_gwi_b, 512, half + 512, 256)]),
            _Phase(like_early, g_early, [_gather_pass(0, early_pack.shape[0])])])
    (g_gwo_, d_gwo, nm_gwo, nv_gwo), (b_gwi,) = _sum_adamw(
        own_gwo, a_gwo, b_gwo, gla_w_out[0], m_gla_w_out[0], v_gla_w_out[0], name="adamw_gla_w_out", phases=[
            _Phase(like_b_gwi, b_gwi, [_reduce_cross(t_gwi_b, 768, half + 768, 256)])])
    (g_gwi_, d_gwi, nm_gwi, nv_gwi), _ = _sum_adamw(
        jnp.concatenate([own_gwi_a, own_gwi_b], axis=0), a_gwi, b_gwi, gla_w_in[0], m_gla_w_in[0], v_gla_w_in[0],
        name="adamw_gla_w_in")

    g_npre1, g_npost1, g_lng_full, g_lnb_full, g_wsp, g_bsp = _unpack(_sum_parts(g_late, "sum_late_small_grads"), late)
    g_npre0, g_npost0, g_bgate, g_ogain, g_w2_full = _unpack(_sum_parts(g_early, "sum_early_small_grads"), early)
    g_w2 = lax.dynamic_slice(g_w2_full, (0, me * 128), (LR, 128))
    g_lng = lax.dynamic_slice(g_lng_full, (0, me * 256), (1, 256))
    g_lnb = lax.dynamic_slice(g_lnb_full, (0, me * 256), (1, 256))
    small_g = [jnp.concatenate([g_npre0, g_npre1], 0), jnp.concatenate([g_npost0, g_npost1], 0), g_w2, g_bgate, g_ogain,
               g_lng, g_lnb, g_wsp, g_bsp]
    small_w = [norm_pre, norm_post, gla_w_gate2[0], gla_b_gate, gla_o_gain, sgu_ln_gain, sgu_ln_bias, sgu_w_spatial[0], sgu_b_spatial[0]]
    small_m = [m_norm_pre, m_norm_post, m_gla_w_gate2[0], m_gla_b_gate, m_gla_o_gain, m_sgu_ln_gain, m_sgu_ln_bias, m_sgu_w_spatial[0], m_sgu_b_spatial[0]]
    small_v = [v_norm_pre, v_norm_post, v_gla_w_gate2[0], v_gla_b_gate, v_gla_o_gain, v_sgu_ln_gain, v_sgu_ln_bias, v_sgu_w_spatial[0], v_sgu_b_spatial[0]]
    d_pack, nm_pack, nv_pack = _adamw(_pack(small_w), _pack(small_g), _pack(small_m), _pack(small_v), "adamw_small")

    out_like = [norm_pre, norm_post, gla_w_gate2, gla_b_gate, gla_o_gain, sgu_ln_gain, sgu_ln_bias, sgu_w_spatial, sgu_b_spatial]
    sg_ = [g.reshape(s.shape) for g, s in zip(small_g, out_like)]
    sd_, sm_, sv_ = (_unpack(pk, out_like) for pk in (d_pack, nm_pack, nv_pack))

    def assemble(small_list, w_in_g, w_out_g, w_in_s, w_out_s):
        npre_, npost_, w2_, bg_, og_, lg_, lb_, wsp_, bsp_ = small_list
        return [npre_, npost_, w_in_g[None], w2_, bg_, og_, w_out_g[None], w_in_s[None], lg_, lb_, wsp_, bsp_, w_out_s[None]]

    return (loss, grad_x.reshape(1, S, D),
            *assemble(sg_, g_gwi_, g_gwo_, g_swi_, g_swo),
            *assemble(sd_, d_gwi, d_gwo, d_swi, d_swo),
            *assemble(sm_, nm_gwi, nm_gwo, nm_swi, nm_swo),
            *assemble(sv_, nv_gwi, nv_gwo, nv_swi, nv_swo))
```

```python
import functools

import jax
import jax.numpy as jnp
from jax import lax
from jax.experimental import pallas as pl
from jax.experimental.pallas import tpu as pltpu

F32 = jnp.float32
BF16 = jnp.bfloat16

N_DEV = 8
S = 2048
D = 2048
H = 4
DK = 256
DV = 512
C = 64
NC = S // C
GLA_COLS = 6160
GLA_PAD = 6272
Q0, K0, V0, G0, LR0 = 0, 1024, 2048, 4096, 6144
LR = 16
LRP = 128
SGU_COLS = 6144
SGU_BLOCK = 128
SGU_G = 8
SGU_GD = 256
EPS = 1e-6
GLA_TAU = 16.0

ADAM_LR, ADAM_B1, ADAM_B2, ADAM_EPS, ADAM_WD, ADAM_STEP = 0.001, 0.9, 0.999, 1e-08, 0.01, 10

V7X_VMEM_BYTES = 64 * 1024 * 1024
VMEM_CEILING = V7X_VMEM_BYTES - 6 * 1024 * 1024
MESH = pl.DeviceIdType.MESH
HIGHEST = lax.Precision.HIGHEST
HBM_SPEC = pl.BlockSpec(memory_space=pl.ANY)


def _sds(shape, dtype):
    return jax.ShapeDtypeStruct(tuple(shape), dtype)


def _nbytes(shape, dtype):
    n = 1
    for s in shape:
        n *= s
    return n * jnp.dtype(dtype).itemsize


def _dot(a, b, dims=(((1,), (0,)), ((), ())), precision=None):
    return lax.dot_general(a, b, dims, precision=precision, preferred_element_type=F32)


NN = (((1,), (0,)), ((), ()))
TN = (((0,), (0,)), ((), ()))
NT = (((1,), (1,)), ((), ()))


def _place():
    return lax.axis_index("x"), lax.axis_index("y"), lax.axis_index("c")


def _index_of(px, py, pc):
    return 4 * px + 2 * py + pc


def _chips(x, y):
    return [(1 - x, y), (x, 1 - y), (1 - x, 1 - y)]


def _rcopy(src, dst, send_sem, recv_sem, to):
    return pltpu.make_async_remote_copy(src_ref=src, dst_ref=dst, send_sem=send_sem, recv_sem=recv_sem,
                                        device_id=to, device_id_type=MESH)


class _Move:
    def __init__(self, ins, n_remote, n_local, make):
        self.ins, self.n_remote, self.n_local, self.make = list(ins), n_remote, n_local, make

    def start(self, in_refs, buf, sems):
        sends, _, local = self.make(in_refs, buf, *sems)
        for cp in local + sends:
            cp.start()

    def finish(self, in_refs, buf, sems):
        sends, arrivals, local = self.make(in_refs, buf, *sems)
        for cp in arrivals:
            cp.wait_recv()
        for cp in sends:
            cp.wait_send()
        for cp in local:
            cp.wait()


class _Phase:
    def __init__(self, like, so_far, moves):
        self.like, self.so_far, self.moves = like, so_far, list(moves)


def _gather_send(shard, r0, nr):
    def make(in_refs, g, ss, rs, ls):
        sh, = in_refs
        x, y, c = _place()
        me = _index_of(x, y, c)
        rows = pl.ds(r0, nr)
        peers = [(x, y, 1 - c)] + [(px, py, c) for px, py in _chips(x, y)]
        sends = [_rcopy(sh.at[rows], g.at[me, rows], ss.at[k], rs.at[k], p) for k, p in enumerate(peers)]
        arrivals = [_rcopy(sh.at[rows], g.at[_index_of(*p), rows], ss.at[k], rs.at[k], p) for k, p in enumerate(peers)]
        return sends, arrivals, [pltpu.make_async_copy(sh.at[rows], g.at[me, rows], ls.at[0])]

    return _Move([shard], 4, 1, make)


def _gather_pass(r0, nr):
    def make(in_refs, g, ss, rs, ls):
        x, y, c = _place()
        rows = pl.ds(r0, nr)
        sends = [_rcopy(g.at[_index_of(px, py, c), rows], g.at[_index_of(px, py, c), rows], ss.at[j], rs.at[j], (x, y, 1 - c))
                 for j, (px, py) in enumerate(_chips(x, y))]
        arrivals = [_rcopy(g.at[_index_of(px, py, c), rows], g.at[_index_of(px, py, 1 - c), rows], ss.at[j], rs.at[j], (x, y, 1 - c))
                    for j, (px, py) in enumerate(_chips(x, y))]
        return sends, arrivals, []

    return _Move([], 3, 0, make)


def _own_table():
    x, y, c = _place()
    return jnp.stack([_index_of(px, py, c) for px, py in [(x, y)] + _chips(x, y)]).astype(jnp.int32)


def _blockify_pair(dw, own_so_far, a_so_far, dst_r0, name):
    rows, tr, cw, win = dw.shape[0], 256, GLA_COLS // N_DEV, 896
    n_steps = rows // tr

    def body(*refs):
        x_ref, own_ref, a_ref, stage_ref, send_sems, recv_sem = refs[0], *refs[-5:]
        i = pl.program_id(0)
        x, y, c = _place()

        def send(slot, k):
            dst = a_ref.at[k, pl.ds(pl.multiple_of(dst_r0 + i * tr, tr), tr)]
            return _rcopy(stage_ref.at[slot], dst, send_sems.at[slot], recv_sem.at[0], (x, y, 1 - c))

        for j in range(N_DEV):
            window = x_ref[:, 768 * j:768 * j + win].astype(F32)
            tile = (pltpu.roll(window, win - 2 * j, 1) if j else window)[:, :cw].astype(BF16)
            k = ((j >> 2) ^ x) + 2 * (((j >> 1) & 1) ^ y)

            @pl.when((j & 1) == c)
            def _():
                own_ref[k] = tile

            @pl.when((j & 1) != c)
            def _():
                slot = (j >> 1) & 1
                if j >> 1 >= 2:
                    send(slot, k).wait_send()
                else:
                    pl.when(i > 0)(lambda: send(slot, k).wait_send())
                stage_ref[slot] = tile
                send(slot, k).start()

        @pl.when(i == n_steps - 1)
        def _():
            send(0, 0).wait_send()
            send(1, 0).wait_send()
            arrived = a_ref.at[:, pl.ds(dst_r0, rows)]
            _rcopy(arrived, arrived, send_sems.at[0], recv_sem.at[0], (x, y, 1 - c)).wait_recv()

    continues = a_so_far is not None
    own, a = pl.pallas_call(
        body, grid=(n_steps,),
        in_specs=[pl.BlockSpec((tr, GLA_PAD), lambda i: (i, 0))] + [HBM_SPEC] * (2 * continues),
        out_specs=[pl.BlockSpec((4, tr, cw), lambda i: (0, dst_r0 // tr + i, 0)), HBM_SPEC],
        out_shape=[_sds((4, D, cw), BF16), _sds((4, D, cw), BF16)],
        scratch_shapes=[pltpu.VMEM((2, tr, cw), BF16), pltpu.SemaphoreType.DMA((2,)), pltpu.SemaphoreType.DMA((1,))],
        input_output_aliases={1: 0, 2: 1} if continues else {},
        compiler_params=pltpu.CompilerParams(dimension_semantics=("arbitrary",), vmem_limit_bytes=48 * 1024 * 1024),
        name=name,
    )(*([dw] + [own_so_far, a_so_far] * continues))
    return own, a


def _reduce_cross(sums, src_r0, dst_r0, nr):
    def make(in_refs, b, ss, rs, ls):
        t, = in_refs
        x, y, c = _place()
        src, dst = pl.ds(src_r0, nr), pl.ds(dst_r0, nr)
        sends = [_rcopy(t.at[j, src], b.at[j, dst], ss.at[j], rs.at[j], (px, py, c)) for j, (px, py) in enumerate(_chips(x, y))]
        return sends, sends, []

    return _Move([sums], 3, 0, make)


def _hosted(body, *, name, grid, in_specs, out_specs, out_shape, args, scratch_shapes=(), block_bytes, scratch_bytes=0,
            phases=(), table=None):
    n_in, n_out, n_scr = len(args), len(out_shape), len(scratch_shapes)
    all_args, all_out_shape, sems, aliases, layout = list(args), list(out_shape), [], {}, []
    for j, ph in enumerate(phases):
        counts = []
        for mv in ph.moves:
            all_args += mv.ins
            counts.append(len(mv.ins))
            sems += [pltpu.SemaphoreType.DMA((mv.n_remote,)), pltpu.SemaphoreType.DMA((mv.n_remote,)),
                     pltpu.SemaphoreType.DMA((max(mv.n_local, 1),))]
        if ph.so_far is not None:
            aliases[len(all_args)] = n_out + j
            all_args.append(ph.so_far)
        layout.append((counts, ph.so_far is not None))
        all_out_shape.append(ph.like)
    n_extra_in = len(all_args) - n_in

    def wrapped(*refs):
        ins, pos = refs[:n_in], n_in
        move_ins = []
        for counts, continues in layout:
            per_move = []
            for cnt in counts:
                per_move.append(refs[pos:pos + cnt])
                pos += cnt
            pos += continues
            move_ins.append(per_move)
        outs = refs[pos:pos + n_out]
        bufs = refs[pos + n_out:pos + n_out + len(phases)]
        pos += n_out + len(phases)
        scratch = refs[pos:pos + n_scr]
        pos += n_scr
        move_sems = []
        for ph in phases:
            per_move = []
            for _ in ph.moves:
                per_move.append(refs[pos:pos + 3])
                pos += 3
            move_sems.append(per_move)

        def each_move(fn_name):
            for ph, buf, per_in, per_sem in zip(phases, bufs, move_ins, move_sems):
                for mv, mv_in, mv_sem in zip(ph.moves, per_in, per_sem):
                    getattr(mv, fn_name)(mv_in, buf, mv_sem)

        if phases:
            first = functools.reduce(jnp.logical_and, [pl.program_id(a) == 0 for a in range(len(grid))])
            last = functools.reduce(jnp.logical_and, [pl.program_id(a) == grid[a] - 1 for a in range(len(grid))])
            pl.when(first)(lambda: each_move("start"))
        body(*ins, *outs, *scratch)
        if phases:
            pl.when(last)(lambda: each_move("finish"))

    est = 2 * block_bytes + scratch_bytes
    params = pltpu.CompilerParams(dimension_semantics=("arbitrary",) * len(grid),
                                  vmem_limit_bytes=min(VMEM_CEILING, max(32 * 1024 * 1024, 2 * est)))
    all_in_specs, all_out_specs = list(in_specs) + [HBM_SPEC] * n_extra_in, list(out_specs) + [HBM_SPEC] * len(phases)
    if table is None:
        results = pl.pallas_call(
            wrapped, grid=grid, in_specs=all_in_specs, out_specs=all_out_specs, out_shape=all_out_shape,
            scratch_shapes=list(scratch_shapes) + sems, input_output_aliases=aliases, compiler_params=params, name=name,
        )(*all_args)
    else:
        results = pl.pallas_call(
            lambda table_ref, *refs: wrapped(*refs),
            grid_spec=pltpu.PrefetchScalarGridSpec(num_scalar_prefetch=1, grid=grid, in_specs=all_in_specs, out_specs=all_out_specs,
                                                   scratch_shapes=list(scratch_shapes) + sems),
            out_shape=all_out_shape, input_output_aliases={k + 1: v for k, v in aliases.items()}, compiler_params=params, name=name,
        )(table, *all_args)
    return list(results[:n_out]), list(results[n_out:])


def _all_gather(shards, name):
    n = len(shards)

    def body(*refs):
        ins, outs = refs[:n], refs[n:2 * n]
        send_sems, recv_sems, local_sems = refs[2 * n:]
        x, y, c = _place()
        me, sibling = (x, y, c), (x, y, 1 - c)
        chips = _chips(x, y)

        def copy(i, k, block, to, src=None):
            dst = outs[i].at[_index_of(*block)]
            return _rcopy(dst if src is None else src, dst, send_sems.at[i * 7 + k], recv_sems.at[i * 7 + k], to)

        own = [pltpu.make_async_copy(ins[i], outs[i].at[_index_of(*me)], local_sems.at[i]) for i in range(n)]
        for cp in own:
            cp.start()
        first = []
        for i in range(n):
            first.append(copy(i, 0, me, sibling, src=ins[i]))
            first += [copy(i, 1 + j, me, (*chip, c), src=ins[i]) for j, chip in enumerate(chips)]
        for cp in first:
            cp.start()
        passed = []
        for i in range(n):
            for j, chip in enumerate(chips):
                copy(i, 1 + j, (*chip, c), me).wait_recv()
                cp = copy(i, 4 + j, (*chip, c), sibling)
                cp.start()
                passed.append(cp)
        for i in range(n):
            copy(i, 0, sibling, me).wait_recv()
            for j, chip in enumerate(chips):
                copy(i, 4 + j, (*chip, 1 - c), me).wait_recv()
        for cp in first + passed:
            cp.wait_send()
        for cp in own:
            cp.wait()

    return pl.pallas_call(
        body,
        in_specs=[HBM_SPEC] * n, out_specs=[HBM_SPEC] * n,
        out_shape=[_sds((N_DEV,) + s.shape, s.dtype) for s in shards],
        scratch_shapes=[pltpu.SemaphoreType.DMA((7 * n,)), pltpu.SemaphoreType.DMA((7 * n,)),
                        pltpu.SemaphoreType.DMA((n,))],
        name=name,
    )(*shards)


def _mm(a, b, mode, out_dtype, *, tm, tn, tk, name, b_blocked=False, out_blocked=False, m_tiles=None, pair=None, phases=()):
    if mode == "nn":
        (m, k), dims = a.shape, NN
        a_blk, a_map = (tm, tk), (lambda i, j, kk: (i, kk))
        if b_blocked:
            assert b.shape[1] == k and b.shape[2] == tn and tk == k
            n = b.shape[0] * tn
            b_spec = pl.BlockSpec((None, tk, tn), lambda i, j, kk: (j, kk, 0))
        else:
            assert b.shape[0] == k
            n = b.shape[1]
            b_spec = pl.BlockSpec((tk, tn), lambda i, j, kk: (kk, j))
    elif mode == "tn":
        (k, m), n, dims = a.shape, b.shape[1], TN
        assert b.shape[0] == k
        first = 0 if m_tiles is None else m_tiles[0]
        a_blk, a_map = (tk, tm), (lambda i, j, kk: (kk, i + first))
        b_spec = pl.BlockSpec((tk, tn), lambda i, j, kk: (kk, j))
    else:
        (m, k), dims = a.shape, NT
        a_blk, a_map = (tm, tk), (lambda i, j, kk: (i, kk))
        if b_blocked:
            assert b.shape[0] * b.shape[2] == k and b.shape[2] == tk
            n = b.shape[1]
            b_spec = pl.BlockSpec((None, tn, tk), lambda i, j, kk: (kk, j, 0))
        else:
            assert b.shape[1] == k
            n = b.shape[0]
            b_spec = pl.BlockSpec((tn, tk), lambda i, j, kk: (j, kk))
    assert m % tm == 0 and n % tn == 0 and k % tk == 0, (a.shape, b.shape, mode)
    nk = k // tk
    n_row_tiles = m // tm if m_tiles is None else m_tiles[1]
    if out_blocked:
        out_shape, out_spec = _sds((n // tn, n_row_tiles * tm, tn), out_dtype), pl.BlockSpec((None, tm, tn), lambda i, j, kk: (j, i, 0))
    else:
        out_shape, out_spec = _sds((n_row_tiles * tm, n), out_dtype), pl.BlockSpec((tm, tn), lambda i, j, kk: (i, j))

    grid = (n_row_tiles, n // tn, nk)

    def body(a_ref, b_ref, o_ref, *rest):
        p = _dot(a_ref[...], b_ref[...], dims)
        if nk == 1:
            o_ref[...] = p.astype(out_dtype)
            if pair is not None:
                _send_to_sibling(p.astype(out_dtype), *rest)
        else:
            acc_ref, = rest
            kk = pl.program_id(2)

            @pl.when(kk == 0)
            def _():
                acc_ref[...] = p

            @pl.when(kk > 0)
            def _():
                acc_ref[...] += p

            @pl.when(kk == nk - 1)
            def _():
                o_ref[...] = acc_ref[...].astype(out_dtype)

    def _send_to_sibling(tile, pair_ref, stage_ref, send_sems, recv_sem):
        i, j = pl.program_id(0), pl.program_id(1)
        x, y, c = _place()
        blk = pair["block"](i, j)
        k = ((blk >> 2) ^ x) + 2 * (((blk >> 1) & 1) ^ y)
        ordinal = pair["ordinal"](i, j)

        def send(slot):
            return _rcopy(stage_ref.at[slot], pair["dst"](pair_ref, k, i, j), send_sems.at[slot], recv_sem.at[0], (x, y, 1 - c))

        @pl.when((blk & 1) != c)
        def _():
            slot = ordinal & 1

            @pl.when(ordinal >= 2)
            def _():
                send(slot).wait_send()

            stage_ref[slot] = tile
            send(slot).start()

        @pl.when((i == grid[0] - 1) & (j == grid[1] - 1))
        def _():
            send(0).wait_send()
            send(1).wait_send()
            _rcopy(pair_ref, pair_ref, send_sems.at[0], recv_sem.at[0], (x, y, 1 - c)).wait_recv()

    blocks = _nbytes(a_blk, a.dtype) + tk * tn * jnp.dtype(b.dtype).itemsize + _nbytes((tm, tn), out_dtype)
    out_specs, out_shapes, scratch = [out_spec], [out_shape], [] if nk == 1 else [pltpu.VMEM((tm, tn), F32)]
    scratch_bytes = _nbytes((tm, tn), F32) * (nk > 1)
    if pair is not None:
        assert nk == 1
        out_specs, out_shapes = out_specs + [HBM_SPEC], out_shapes + [pair["like"]]
        scratch = [pltpu.VMEM((2, tm, tn), out_dtype), pltpu.SemaphoreType.DMA((2,)), pltpu.SemaphoreType.DMA((1,))]
        scratch_bytes = 2 * _nbytes((tm, tn), out_dtype)
    outs, bufs = _hosted(
        body, name=name, grid=grid,
        in_specs=[pl.BlockSpec(a_blk, a_map), b_spec], out_specs=out_specs, out_shape=out_shapes, args=[a, b],
        scratch_shapes=scratch, block_bytes=blocks, scratch_bytes=scratch_bytes, phases=phases)
    return outs[0], outs[1:] + bufs


RB = 256


def _row_spec(width):
    return pl.BlockSpec((RB, width), lambda i: (i, 0))


def _vec_spec(width):
    return pl.BlockSpec((1, width), lambda i: (0, 0))


def _rinv(x):
    return lax.rsqrt(jnp.mean(x * x, axis=-1, keepdims=True) + EPS)


def _norm_bwd(dyn, xhat, r):
    return r * (dyn - xhat * jnp.mean(dyn * xhat, axis=-1, keepdims=True))


def _colsum(x):
    return jnp.sum(x, axis=0, keepdims=True)


def _accumulate(ref, value):
    @pl.when(pl.program_id(0) == 0)
    def _():
        ref[...] = value

    @pl.when(pl.program_id(0) > 0)
    def _():
        ref[...] += value


def _prenorm(x, gain):
    def body(x_ref, g_ref, h_ref):
        xv = x_ref[...]
        h_ref[...] = (xv * _rinv(xv) * g_ref[...]).astype(BF16)

    outs, _ = _hosted(body, name="prenorm", grid=(S // RB,), in_specs=[_row_spec(D), _vec_spec(D)], out_specs=[_row_spec(D)],
                      out_shape=[_sds((S, D), BF16)], args=[x, gain], block_bytes=RB * D * 6)
    return outs[0]


def _mid_fwd(x, y, npost, npre, phases=()):
    def body(x_ref, y_ref, po_ref, pr_ref, x1_ref, h1_ref):
        yv = y_ref[...]
        x1 = x_ref[...] + yv * _rinv(yv) * po_ref[...]
        x1_ref[...] = x1
        h1_ref[...] = (x1 * _rinv(x1) * pr_ref[...]).astype(BF16)

    return _hosted(body, name="mid_fwd", grid=(S // RB,), in_specs=[_row_spec(D), _row_spec(D), _vec_spec(D), _vec_spec(D)],
                   out_specs=[_row_spec(D), _row_spec(D)], out_shape=[_sds((S, D), F32), _sds((S, D), BF16)],
                   args=[x, y, npost, npre], block_bytes=RB * D * 14, phases=phases)


def _final(x1, y1, tgt, npost):
    def body(x_ref, y_ref, t_ref, po_ref, loss_ref, dx_ref, dy_ref, dpo_ref):
        yv = y_ref[...]
        r = _rinv(yv)
        yhat = yv * r
        err = x_ref[...] + yhat * po_ref[...] - t_ref[...]
        dx = err * (1.0 / D)
        dx_ref[...] = dx
        dy_ref[...] = _norm_bwd(dx * po_ref[...], yhat, r).astype(BF16)
        _accumulate(loss_ref, _colsum(err * err))
        _accumulate(dpo_ref, _colsum(dx * yhat))

    outs, _ = _hosted(body, name="final", grid=(S // RB,), in_specs=[_row_spec(D), _row_spec(D), _row_spec(D), _vec_spec(D)],
                      out_specs=[_vec_spec(D), _row_spec(D), _row_spec(D), _vec_spec(D)],
                      out_shape=[_sds((1, D), F32), _sds((S, D), F32), _sds((S, D), BF16), _sds((1, D), F32)],
                      args=[x1, y1, tgt, npost], block_bytes=RB * D * 18)
    return outs


def _mid_bwd(dx2, dh1, x1, y0, npre, npost, phases=()):
    def body(dx2_ref, dh_ref, x_ref, y_ref, pr_ref, po_ref, dx1_ref, dy_ref, dpr_ref, dpo_ref):
        xv = x_ref[...]
        r = _rinv(xv)
        xhat = xv * r
        dh = dh_ref[...]
        dx1 = dx2_ref[...] + _norm_bwd(dh * pr_ref[...], xhat, r)
        dx1_ref[...] = dx1
        yv = y_ref[...]
        ry = _rinv(yv)
        yhat = yv * ry
        dy_ref[...] = _norm_bwd(dx1 * po_ref[...], yhat, ry).astype(BF16)
        _accumulate(dpr_ref, _colsum(dh * xhat))
        _accumulate(dpo_ref, _colsum(dx1 * yhat))

    return _hosted(body, name="mid_bwd", grid=(S // RB,), in_specs=[_row_spec(D)] * 4 + [_vec_spec(D)] * 2,
                   out_specs=[_row_spec(D), _row_spec(D), _vec_spec(D), _vec_spec(D)],
                   out_shape=[_sds((S, D), F32), _sds((S, D), BF16), _sds((1, D), F32), _sds((1, D), F32)],
                   args=[dx2, dh1, x1, y0, npre, npost], block_bytes=RB * D * 22, phases=phases)


def _first_bwd(dx1, dh0, x0, npre, phases=()):
    def body(dx1_ref, dh_ref, x_ref, pr_ref, gx_ref, dpr_ref):
        xv = x_ref[...]
        r = _rinv(xv)
        xhat = xv * r
        dh = dh_ref[...]
        gx_ref[...] = dx1_ref[...] + _norm_bwd(dh * pr_ref[...], xhat, r)
        _accumulate(dpr_ref, _colsum(dh * xhat))

    return _hosted(body, name="first_bwd", grid=(S // RB,), in_specs=[_row_spec(D)] * 3 + [_vec_spec(D)],
                   out_specs=[_row_spec(D), _vec_spec(D)], out_shape=[_sds((S, D), F32), _sds((1, D), F32)],
                   args=[dx1, dh0, x0, npre], block_bytes=RB * D * 16, phases=phases)


GLA_RB = 256
GLA_CPB = GLA_RB // C


def _sigmoid(x):
    return 1.0 / (1.0 + jnp.exp(-x))


def _tri(strict):
    r = lax.broadcasted_iota(jnp.int32, (C, C), 0)
    c = lax.broadcasted_iota(jnp.int32, (C, C), 1)
    return jnp.where(c < r if strict else c <= r, 1.0, 0.0).astype(F32)


def _gla_gate(glr_b, w2_h, b_h, tri):
    z = _dot(glr_b, w2_h) + b_h
    log_a = (jnp.minimum(z, 0.0) - jnp.log(1.0 + jnp.exp(-jnp.abs(z)))) * (1.0 / GLA_TAU)
    bcum = _dot(tri, log_a, precision=HIGHEST)
    b_end = jnp.sum(log_a, axis=0, keepdims=True)
    return z, jnp.exp(b_end - bcum), jnp.exp(b_end)


def _gla_fwd(proj, w2p, bgate, ogain, phases=()):
    def body(p_ref, w2_ref, b_ref, og_ref, y_ref, st_out_ref, st_ref):
        @pl.when(pl.program_id(0) == 0)
        def _():
            st_ref[...] = jnp.zeros_like(st_ref)

        tri = _tri(False)

        def chunk(ci, carry):
            rows = pl.ds(pl.multiple_of(ci * C, C), C)
            glr_b = p_ref[rows, LR0:LR0 + LRP].astype(BF16)
            for h in range(H):
                _, ea, dec = _gla_gate(glr_b, w2_ref[:, h * DK:(h + 1) * DK], b_ref[:, h * DK:(h + 1) * DK], tri)
                k_dec = (p_ref[rows, K0 + h * DK:K0 + (h + 1) * DK] * ea).astype(BF16)
                v_b = p_ref[rows, V0 + h * DV:V0 + (h + 1) * DV].astype(BF16)
                st = st_ref[h] * dec + _dot(v_b, k_dec, TN)
                st_ref[h] = st
                st_b = st.astype(BF16)
                st_out_ref[ci, h] = st_b
                q_b = (p_ref[rows, Q0 + h * DK:Q0 + (h + 1) * DK] * (DK ** -0.5)).astype(BF16)
                o = _dot(q_b, st_b, NT)
                on = o * _rinv(o)
                g = p_ref[rows, G0 + h * DV:G0 + (h + 1) * DV]
                y_ref[rows, h * DV:(h + 1) * DV] = (on * og_ref[:, h * DV:(h + 1) * DV] * (g * _sigmoid(g))).astype(BF16)
            return carry

        lax.fori_loop(0, GLA_CPB, chunk, 0)

    blocks = GLA_RB * GLA_PAD * 4 + GLA_RB * D * 2 + GLA_CPB * H * DV * DK * 2
    return _hosted(
        body, name="gla_fwd", grid=(S // GLA_RB,),
        in_specs=[pl.BlockSpec((GLA_RB, GLA_PAD), lambda i: (i, 0)),
                  pl.BlockSpec((LRP, H * DK), lambda i: (0, 0)),
                  pl.BlockSpec((1, H * DK), lambda i: (0, 0)),
                  pl.BlockSpec((1, H * DV), lambda i: (0, 0))],
        out_specs=[pl.BlockSpec((GLA_RB, H * DV), lambda i: (i, 0)),
                   pl.BlockSpec((GLA_CPB, H, DV, DK), lambda i: (i, 0, 0, 0))],
        out_shape=[_sds((S, H * DV), BF16), _sds((NC, H, DV, DK), BF16)],
        args=[proj, w2p, bgate, ogain], scratch_shapes=[pltpu.VMEM((H, DV, DK), F32)],
        block_bytes=blocks, scratch_bytes=H * DV * DK * 4, phases=phases)


def _gla_bwd(proj, dypre, states, w2p, bgate, ogain, phases=()):
    nb = S // GLA_RB

    def body(p_ref, dy_ref, st_blk_ref, st_prev_ref, w2_ref, b_ref, og_ref,
             dp_ref, dog_ref, dbg_ref, dw2_ref, r_ref):
        step = pl.program_id(0)

        @pl.when(step == 0)
        def _():
            r_ref[...] = jnp.zeros_like(r_ref)
            dog_ref[...] = jnp.zeros_like(dog_ref)
            dbg_ref[...] = jnp.zeros_like(dbg_ref)
            dw2_ref[...] = jnp.zeros_like(dw2_ref)

        tri = _tri(False)
        tri_strict = _tri(True)
        has_prev = jnp.where(step < nb - 1, 1.0, 0.0).astype(F32)

        def chunk(ci, st_prev_of):
            rows = pl.ds(ci * C if isinstance(ci, int) else pl.multiple_of(ci * C, C), C)
            glr_b = p_ref[rows, LR0:LR0 + LRP].astype(BF16)
            dglr = jnp.zeros((C, LRP), F32)
            for h in range(H):
                kcol = slice(h * DK, (h + 1) * DK)
                vcol = slice(h * DV, (h + 1) * DV)
                w2_h = w2_ref[:, kcol]
                z, ea, dec = _gla_gate(glr_b, w2_h, b_ref[:, kcol], tri)
                k_dec = p_ref[rows, K0 + h * DK:K0 + (h + 1) * DK] * ea
                k_dec_b = k_dec.astype(BF16)
                v_b = p_ref[rows, V0 + h * DV:V0 + (h + 1) * DV].astype(BF16)
                q_b = (p_ref[rows, Q0 + h * DK:Q0 + (h + 1) * DK] * (DK ** -0.5)).astype(BF16)
                st_b = st_blk_ref[ci, h]
                o = _dot(q_b, st_b, NT)
                rinv = _rinv(o)
                on = o * rinv
                g = p_ref[rows, G0 + h * DV:G0 + (h + 1) * DV]
                sg = _sigmoid(g)
                og = og_ref[:, vcol]
                dyp = dy_ref[rows, vcol]
                dp_ref[rows, G0 + h * DV:G0 + (h + 1) * DV] = (dyp * (on * og) * (sg * (1.0 + g * (1.0 - sg)))).astype(BF16)
                dpn = dyp * (g * sg)
                dog_ref[:, vcol] += _colsum(dpn * on)
                do_b = _norm_bwd(dpn * og, on, rinv).astype(BF16)
                gt = _dot(do_b, q_b, TN) + r_ref[h]
                gt_b = gt.astype(BF16)
                dp_ref[rows, Q0 + h * DK:Q0 + (h + 1) * DK] = (_dot(do_b, st_b) * (DK ** -0.5)).astype(BF16)
                dkd = _dot(v_b, gt_b)
                dp_ref[rows, V0 + h * DV:V0 + (h + 1) * DV] = _dot(k_dec_b, gt_b, NT).astype(BF16)
                dp_ref[rows, K0 + h * DK:K0 + (h + 1) * DK] = (dkd * ea).astype(BF16)
                ddec = _colsum(gt * st_prev_of(h))
                dla = _dot(tri_strict, dkd * k_dec, precision=HIGHEST) + ddec * dec
                dz = dla * (1.0 / GLA_TAU) * (1.0 - _sigmoid(z))
                dz_b = dz.astype(BF16)
                r_ref[h] = gt * dec
                dbg_ref[:, kcol] += _colsum(dz)
                dw2_ref[:, kcol] += _dot(glr_b, dz_b, TN)
                dglr = dglr + _dot(dz_b, w2_h, NT)
            dp_ref[rows, LR0:LR0 + LRP] = dglr.astype(BF16)

        def later_chunk(t, carry):
            ci = GLA_CPB - 1 - t
            chunk(ci, lambda h: st_blk_ref[ci - 1, h].astype(F32))
            return carry

        lax.fori_loop(0, GLA_CPB - 1, later_chunk, 0)
        chunk(0, lambda h: st_prev_ref[0, h].astype(F32) * has_prev)

    blocks = (GLA_RB * GLA_PAD * 4 + GLA_RB * D * 4 + (GLA_CPB + 1) * H * DV * DK * 2 + GLA_RB * GLA_PAD * 2)
    rev = lambda i: nb - 1 - i
    return _hosted(
        body, name="gla_bwd", grid=(nb,),
        in_specs=[pl.BlockSpec((GLA_RB, GLA_PAD), lambda i: (rev(i), 0)),
                  pl.BlockSpec((GLA_RB, H * DV), lambda i: (rev(i), 0)),
                  pl.BlockSpec((GLA_CPB, H, DV, DK), lambda i: (rev(i), 0, 0, 0)),
                  pl.BlockSpec((1, H, DV, DK), lambda i: (jnp.maximum(rev(i) * GLA_CPB - 1, 0), 0, 0, 0)),
                  pl.BlockSpec((LRP, H * DK), lambda i: (0, 0)),
                  pl.BlockSpec((1, H * DK), lambda i: (0, 0)),
                  pl.BlockSpec((1, H * DV), lambda i: (0, 0))],
        out_specs=[pl.BlockSpec((GLA_RB, GLA_PAD), lambda i: (rev(i), 0)),
                   pl.BlockSpec((1, H * DV), lambda i: (0, 0)),
                   pl.BlockSpec((1, H * DK), lambda i: (0, 0)),
                   pl.BlockSpec((LRP, H * DK), lambda i: (0, 0))],
        out_shape=[_sds((S, GLA_PAD), BF16), _sds((1, H * DV), F32), _sds((1, H * DK), F32), _sds((LRP, H * DK), F32)],
        args=[proj, dypre, states, states, w2p, bgate, ogain], scratch_shapes=[pltpu.VMEM((H, DV, DK), F32)],
        block_bytes=blocks, scratch_bytes=H * DV * DK * 4, phases=phases)


SGU_RB = 256
GELU_C = 0.7978845608028654
GELU_A = 0.044715


def _gelu(x):
    return 0.5 * x * (1.0 + jnp.tanh(GELU_C * (x + GELU_A * x * x * x)))


def _gelu_grad(x):
    t = jnp.tanh(GELU_C * (x + GELU_A * x * x * x))
    return 0.5 * (1.0 + t) + 0.5 * x * (1.0 - t * t) * (GELU_C * (1.0 + 3.0 * GELU_A * x * x))


def _causal_mask(transposed=False):
    i = lax.broadcasted_iota(jnp.int32, (SGU_BLOCK, SGU_BLOCK), 1 if transposed else 0)
    j = lax.broadcasted_iota(jnp.int32, (SGU_BLOCK, SGU_BLOCK), 0 if transposed else 1)
    return (i >= C) | (j < C)


def _layer_norm(vf, gain, bias):
    mu = jnp.mean(vf, axis=-1, keepdims=True)
    cen = vf - mu
    rstd = lax.rsqrt(jnp.mean(cen * cen, axis=-1, keepdims=True) + EPS)
    xhat = cen * rstd
    return xhat, rstd, xhat * gain + bias


def _sgu_fwd(proj, lng, lnb, ws, bsb, phases=()):
    def body(p_ref, g_ref, b_ref, ws_ref, bs_ref, o_ref):
        mask = _causal_mask()
        for n in range(SGU_RB // SGU_BLOCK):
            rows = slice(n * SGU_BLOCK, (n + 1) * SGU_BLOCK)
            _, _, vn = _layer_norm(_gelu(p_ref[rows, D:2 * D]), g_ref[...], b_ref[...])
            vn_b = vn.astype(BF16)
            for gi in range(SGU_G):
                cols = slice(gi * SGU_GD, (gi + 1) * SGU_GD)
                w = jnp.where(mask, ws_ref[gi], 0.0).astype(BF16)
                vs = _dot(w, vn_b[:, cols]) + bs_ref[gi]
                gate = p_ref[rows, 2 * D + gi * SGU_GD:2 * D + (gi + 1) * SGU_GD]
                o_ref[rows, cols] = (_gelu(p_ref[rows, cols]) * vs * (gate * _sigmoid(gate))).astype(BF16)

    blocks = SGU_RB * SGU_COLS * 4 + SGU_RB * D * 2 + SGU_G * SGU_BLOCK * (SGU_BLOCK + SGU_GD) * 4
    return _hosted(
        body, name="sgu_fwd", grid=(S // SGU_RB,),
        in_specs=[pl.BlockSpec((SGU_RB, SGU_COLS), lambda i: (i, 0)),
                  pl.BlockSpec((1, D), lambda i: (0, 0)), pl.BlockSpec((1, D), lambda i: (0, 0)),
                  pl.BlockSpec((SGU_G, SGU_BLOCK, SGU_BLOCK), lambda i: (0, 0, 0)),
                  pl.BlockSpec((SGU_G, SGU_BLOCK, SGU_GD), lambda i: (0, 0, 0))],
        out_specs=[pl.BlockSpec((SGU_RB, D), lambda i: (i, 0))], out_shape=[_sds((S, D), BF16)],
        args=[proj, lng, lnb, ws, bsb], block_bytes=blocks, phases=phases)


def _sgu_bwd(proj, dpre, lng, lnb, ws, wst, bsb, phases=()):
    nsteps = S // SGU_RB

    def body(p_ref, d_ref, g_ref, b_ref, ws_ref, wst_ref, bs_ref,
             dp_ref, dg_ref, db_ref, dws_ref, dbs_ref, dvn_ref, dvs_acc_ref):
        step = pl.program_id(0)

        @pl.when(step == 0)
        def _():
            dg_ref[...] = jnp.zeros_like(dg_ref)
            db_ref[...] = jnp.zeros_like(db_ref)
            dws_ref[...] = jnp.zeros_like(dws_ref)
            dvs_acc_ref[...] = jnp.zeros_like(dvs_acc_ref)

        mask = _causal_mask()
        maskt = _causal_mask(transposed=True)
        for n in range(SGU_RB // SGU_BLOCK):
            rows = slice(n * SGU_BLOCK, (n + 1) * SGU_BLOCK)
            v = p_ref[rows, D:2 * D]
            xhat, rstd, vn = _layer_norm(_gelu(v), g_ref[...], b_ref[...])
            vn_b = vn.astype(BF16)
            for gi in range(SGU_G):
                cols = slice(gi * SGU_GD, (gi + 1) * SGU_GD)
                w = jnp.where(mask, ws_ref[gi], 0.0).astype(BF16)
                wt = jnp.where(maskt, wst_ref[gi], 0.0).astype(BF16)
                vs = _dot(w, vn_b[:, cols]) + bs_ref[gi]
                u = p_ref[rows, cols]
                gate = p_ref[rows, 2 * D + gi * SGU_GD:2 * D + (gi + 1) * SGU_GD]
                sg = _sigmoid(gate)
                gu = _gelu(u)
                dpre_g = d_ref[rows, cols]
                t = dpre_g * (gate * sg)
                dp_ref[rows, cols] = (t * vs * _gelu_grad(u)).astype(BF16)
                dp_ref[rows, 2 * D + gi * SGU_GD:2 * D + (gi + 1) * SGU_GD] = (
                    dpre_g * gu * vs * (sg * (1.0 + gate * (1.0 - sg)))).astype(BF16)
                dvs = t * gu
                dvs_b = dvs.astype(BF16)
                dvs_acc_ref[:, cols] += dvs
                dws_ref[gi] += _dot(dvs_b, vn_b[:, cols], NT)
                dvn_ref[:, cols] = _dot(wt, dvs_b)
            dvn = dvn_ref[...]
            dg_ref[...] += _colsum(dvn * xhat)
            db_ref[...] += _colsum(dvn)
            dxh = dvn * g_ref[...]
            dvf = rstd * (dxh - jnp.mean(dxh, axis=-1, keepdims=True) - xhat * jnp.mean(dxh * xhat, axis=-1, keepdims=True))
            dp_ref[rows, D:2 * D] = (dvf * _gelu_grad(v)).astype(BF16)

        @pl.when(step == nsteps - 1)
        def _():
            lane = lax.broadcasted_iota(jnp.int32, (SGU_BLOCK, SGU_BLOCK), 1)
            out = jnp.zeros((SGU_BLOCK, SGU_BLOCK), F32)
            for gi in range(SGU_G):
                out = out + jnp.where(lane == gi, jnp.sum(dvs_acc_ref[:, gi * SGU_GD:(gi + 1) * SGU_GD], axis=1, keepdims=True), 0.0)
                dws_ref[gi] = jnp.where(mask, dws_ref[gi], 0.0)
            dbs_ref[...] = out

    blocks = SGU_RB * SGU_COLS * 6 + SGU_RB * D * 4 + SGU_G * SGU_BLOCK * (3 * SGU_BLOCK + SGU_GD) * 4
    const3 = lambda i: (0, 0, 0)
    return _hosted(
        body, name="sgu_bwd", grid=(nsteps,),
        in_specs=[pl.BlockSpec((SGU_RB, SGU_COLS), lambda i: (i, 0)),
                  pl.BlockSpec((SGU_RB, D), lambda i: (i, 0)),
                  pl.BlockSpec((1, D), lambda i: (0, 0)), pl.BlockSpec((1, D), lambda i: (0, 0)),
                  pl.BlockSpec((SGU_G, SGU_BLOCK, SGU_BLOCK), const3),
                  pl.BlockSpec((SGU_G, SGU_BLOCK, SGU_BLOCK), const3),
                  pl.BlockSpec((SGU_G, SGU_BLOCK, SGU_GD), const3)],
        out_specs=[pl.BlockSpec((SGU_RB, SGU_COLS), lambda i: (i, 0)),
                   pl.BlockSpec((1, D), lambda i: (0, 0)), pl.BlockSpec((1, D), lambda i: (0, 0)),
                   pl.BlockSpec((SGU_G, SGU_BLOCK, SGU_BLOCK), const3),
                   pl.BlockSpec((SGU_BLOCK, SGU_BLOCK), lambda i: (0, 0))],
        out_shape=[_sds((S, SGU_COLS), BF16), _sds((1, D), F32), _sds((1, D), F32),
                   _sds((SGU_G, SGU_BLOCK, SGU_BLOCK), F32), _sds((SGU_BLOCK, SGU_BLOCK), F32)],
        args=[proj, dpre, lng, lnb, ws, wst, bsb],
        scratch_shapes=[pltpu.VMEM((SGU_BLOCK, D), F32), pltpu.VMEM((SGU_BLOCK, D), F32)],
        block_bytes=blocks, scratch_bytes=2 * SGU_BLOCK * D * 4, phases=phases)


def _pair_sum(own, a, r0, nr, name, table=None):
    c = own.shape[2]
    tr = 256
    assert r0 % tr == 0 and nr % tr == 0

    def body(own_ref, sib_ref, o_ref):
        o_ref[...] = (own_ref[...].astype(F32) + sib_ref[...].astype(F32)).astype(BF16)

    own_map = ((lambda j, i: (1 + j, r0 // tr + i, 0)) if table is None else
               (lambda j, i, t: (t[1 + j], r0 // tr + i, 0)))
    cpad = -(-c // 128) * 128
    outs, _ = _hosted(
        body, name=name, grid=(3, nr // tr),
        in_specs=[pl.BlockSpec((None, tr, c), own_map),
                  pl.BlockSpec((None, tr, c), lambda j, i, *t: (1 + j, r0 // tr + i, 0))],
        out_specs=[pl.BlockSpec((None, tr, c), lambda j, i, *t: (j, i, 0))], out_shape=[_sds((3, nr, c), BF16)],
        args=[own, a], block_bytes=3 * tr * cpad * 2, table=table)
    return outs[0]


def _adamw_math(w, g, m, v):
    m = ADAM_B1 * m + (1.0 - ADAM_B1) * g
    v = ADAM_B2 * v + (1.0 - ADAM_B2) * (g * g)
    m_hat = m / (1.0 - ADAM_B1 ** ADAM_STEP)
    v_hat = v / (1.0 - ADAM_B2 ** ADAM_STEP)
    delta = -ADAM_LR * (m_hat / (jnp.sqrt(v_hat) + ADAM_EPS) + ADAM_WD * w)
    return delta, m, v


def _sum_adamw(own, a, b, w, m, v, *, name, phases=(), table=None):
    r, c = w.shape
    tr = 256

    def body(own_ref, sib_ref, far_ref, w_ref, m_ref, v_ref, g_ref, d_ref, nm_ref, nv_ref):
        g = own_ref[...].astype(F32) + sib_ref[...].astype(F32)
        for j in range(3):
            g = g + far_ref[j].astype(F32)
        g_ref[...] = g
        d_ref[...], nm_ref[...], nv_ref[...] = _adamw_math(w_ref[...], g, m_ref[...], v_ref[...])

    spec = pl.BlockSpec((tr, c), lambda i, *t: (i, 0))
    own_map = (lambda i: (0, i, 0)) if table is None else (lambda i, t: (t[0], i, 0))
    cpad = -(-c // 128) * 128
    return _hosted(
        body, name=name, grid=(r // tr,),
        in_specs=[pl.BlockSpec((None, tr, c), own_map), pl.BlockSpec((None, tr, c), lambda i, *t: (0, i, 0)),
                  pl.BlockSpec((3, tr, c), lambda i, *t: (0, i, 0)), spec, spec, spec],
        out_specs=[spec] * 4, out_shape=[_sds((r, c), F32)] * 4, args=[own, a, b, w, m, v],
        block_bytes=5 * tr * cpad * 2 + 7 * tr * cpad * 4, phases=phases, table=table)


def _sum_parts(parts, name):
    n, r, c = parts.shape

    def body(p_ref, o_ref):
        g = p_ref[0]
        for j in range(1, n):
            g = g + p_ref[j]
        o_ref[...] = g

    outs, _ = _hosted(body, name=name, grid=(1,), in_specs=[pl.BlockSpec((n, r, c), lambda i: (0, 0, 0))],
                      out_specs=[pl.BlockSpec((r, c), lambda i: (0, 0))], out_shape=[_sds((r, c), F32)], args=[parts],
                      block_bytes=(n + 1) * r * c * 4)
    return outs[0]


def _adamw(w, g, m, v, name):
    def body(w_ref, g_ref, m_ref, v_ref, d_ref, nm_ref, nv_ref):
        d_ref[...], nm_ref[...], nv_ref[...] = _adamw_math(w_ref[...], g_ref[...], m_ref[...], v_ref[...])

    spec = pl.BlockSpec(w.shape, lambda i: (0, 0))
    outs, _ = _hosted(body, name=name, grid=(1,), in_specs=[spec] * 4, out_specs=[spec] * 3, out_shape=[_sds(w.shape, F32)] * 3,
                      args=[w, g, m, v], block_bytes=7 * _nbytes(w.shape, F32))
    return outs


def _blocks_to_columns(g):
    n, r, c = g.shape
    return jnp.transpose(g, (1, 0, 2)).reshape(r, n * c)


def _pack(parts):
    return jnp.concatenate([p.reshape(-1) for p in parts]).reshape(-1, 128)


def _unpack(packed, like):
    flat, outs, off = packed.reshape(-1), [], 0
    for p in like:
        outs.append(flat[off:off + p.size].reshape(p.shape))
        off += p.size
    return outs


def kernel(x, norm_pre, norm_post, gla_w_in, gla_w_gate2, gla_b_gate, gla_o_gain, gla_w_out, sgu_w_in, sgu_ln_gain, sgu_ln_bias, sgu_w_spatial, sgu_b_spatial, sgu_w_out, loss_target, m_norm_pre, m_norm_post, m_gla_w_in, m_gla_w_gate2, m_gla_b_gate, m_gla_o_gain, m_gla_w_out, m_sgu_w_in, m_sgu_ln_gain, m_sgu_ln_bias, m_sgu_w_spatial, m_sgu_b_spatial, m_sgu_w_out, v_norm_pre, v_norm_post, v_gla_w_in, v_gla_w_gate2, v_gla_b_gate, v_gla_o_gain, v_gla_w_out, v_sgu_w_in, v_sgu_ln_gain, v_sgu_ln_bias, v_sgu_w_spatial, v_sgu_b_spatial, v_sgu_w_out):
    me = _index_of(*_place())
    x0 = x.reshape(S, D)
    tgt = loss_target.reshape(S, D)
    npre0, npre1 = norm_pre[0:1], norm_pre[1:2]
    npost0, npost1 = norm_post[0:1], norm_post[1:2]
    ws = sgu_w_spatial[0]
    wst = jnp.transpose(ws, (0, 2, 1))
    bsb = jnp.broadcast_to(sgu_b_spatial[0][:, :, None], (SGU_G, SGU_BLOCK, SGU_GD))
    W_ROWS = D // N_DEV
    IN_COLS_G, IN_COLS_S = GLA_COLS // N_DEV, SGU_COLS // N_DEV

    s_gwi, s_gwo = gla_w_in[0].astype(BF16), gla_w_out[0].astype(BF16)
    s_swi, s_swo = sgu_w_in[0].astype(BF16), sgu_w_out[0].astype(BF16)
    small = jnp.concatenate([jnp.pad(gla_w_gate2[0].reshape(4, 512), ((0, 4), (0, 0))),
                             jnp.pad(jnp.concatenate([sgu_ln_gain, sgu_ln_bias], axis=1), ((0, 7), (0, 0)))], axis=0)

    g_gwi, g_small = _all_gather([s_gwi, small], "gather_first")
    wg_in = jnp.pad(_blocks_to_columns(g_gwi), ((0, 0), (0, GLA_PAD - GLA_COLS)))
    w2 = _blocks_to_columns(g_small[:, :4, :].reshape(N_DEV, LR, 128))
    w2p = jnp.pad(w2, ((0, LRP - LR), (0, 0))).astype(BF16)
    lng = g_small[:, 8, :256].reshape(1, D)
    lnb = g_small[:, 8, 256:].reshape(1, D)
    like_gwo, like_swi = _sds((N_DEV, W_ROWS, D), BF16), _sds((N_DEV, D, IN_COLS_S), BF16)

    h0 = _prenorm(x0, npre0)
    proj0, (g_gwo, g_swi) = _mm(h0, wg_in, "nn", F32, tm=1024, tn=896, tk=D, name="gla_in", phases=[
        _Phase(like_gwo, None, [_gather_send(s_gwo, 0, W_ROWS)]),
        _Phase(like_swi, None, [_gather_send(s_swi, 0, 256)])])
    (ypre0, states), (g_gwo, g_swi) = _gla_fwd(proj0, w2p, gla_b_gate, gla_o_gain, phases=[
        _Phase(like_gwo, g_gwo, [_gather_pass(0, W_ROWS)]),
        _Phase(like_swi, g_swi, [_gather_send(s_swi, 256, 1280), _gather_pass(0, 256)])])
    wg_out = g_gwo.reshape(D, D)
    y0, (g_swi,) = _mm(ypre0, wg_out, "nn", F32, tm=1024, tn=1024, tk=D, name="gla_out", phases=[
        _Phase(like_swi, g_swi, [_gather_send(s_swi, 1536, 512), _gather_pass(256, 1280)])])
    (x1, h1), (g_swi,) = _mid_fwd(x0, y0, npost0, npre1, phases=[_Phase(like_swi, g_swi, [_gather_pass(1536, 512)])])
    proj1, (g_swo,) = _mm(h1, g_swi, "nn", F32, tm=1024, tn=IN_COLS_S, tk=D, name="sgu_in", b_blocked=True, phases=[
        _Phase(like_gwo, None, [_gather_send(s_swo, 0, W_ROWS)])])
    (pre1,), (g_swo,) = _sgu_fwd(proj1, lng, lnb, ws, bsb, phases=[_Phase(like_gwo, g_swo, [_gather_pass(0, W_ROWS)])])
    ws_out = g_swo.reshape(D, D)
    y1, _ = _mm(pre1, ws_out, "nn", F32, tm=1024, tn=1024, tk=D, name="sgu_out")
    loss_cols, dx2, dy1, dnpost1 = _final(x1, y1, tgt, npost1)
    loss = lax.psum(0.5 * jnp.sum(loss_cols) / D, ("x", "y", "c"))

    like_b_out, like_b_swi = _sds((3, W_ROWS, D), BF16), _sds((3, D, IN_COLS_S), BF16)
    like_b_gwi = _sds((3, D, IN_COLS_G), BF16)
    row_pair = dict(like=_sds((4, W_ROWS, D), BF16), block=lambda i, j: i, ordinal=lambda i, j: i >> 1,
                    dst=lambda ref, k, i, j: ref.at[k])
    col_pair = dict(like=_sds((4, D, IN_COLS_S), BF16), block=lambda i, j: j, ordinal=lambda i, j: 4 * i + (j >> 1),
                    dst=lambda ref, k, i, j: ref.at[k, pl.ds(pl.multiple_of(i * 1024, 1024), 1024)])

    mine = _own_table()
    dws_out, (a_swo,) = _mm(pre1, dy1, "tn", BF16, tm=W_ROWS, tn=D, tk=S, name="sgu_out_dw", pair=row_pair)
    p_swo = dws_out.reshape(N_DEV, W_ROWS, D)
    t_swo = _pair_sum(p_swo, a_swo, 0, W_ROWS, "pair_sum_sgu_w_out", table=mine)
    dpre1, _ = _mm(dy1, ws_out, "nt", F32, tm=1024, tn=1024, tk=D, name="sgu_out_dx")
    (dproj1, dlng, dlnb, dwsp, dbsp), (b_swo,) = _sgu_bwd(proj1, dpre1, lng, lnb, ws, wst, bsb, phases=[
        _Phase(like_b_out, None, [_reduce_cross(t_swo, 0, 0, W_ROWS)])])
    p_swi, (a_swi,) = _mm(h1, dproj1, "tn", BF16, tm=1024, tn=IN_COLS_S, tk=S, name="sgu_in_dw", out_blocked=True, pair=col_pair)
    t_swi = _pair_sum(p_swi, a_swi, 0, D, "pair_sum_sgu_w_in", table=mine)
    dh1, (b_swi,) = _mm(dproj1, g_swi, "nt", F32, tm=1024, tn=1024, tk=IN_COLS_S, name="sgu_in_dx", b_blocked=True, phases=[
        _Phase(like_b_swi, None, [_reduce_cross(t_swi, 0, 0, 1024)])])
    (dx1, dy0, dnpre1, dnpost0), (b_swi,) = _mid_bwd(dx2, dh1, x1, y0, npre1, npost0, phases=[
        _Phase(like_b_swi, b_swi, [_reduce_cross(t_swi, 1024, 1024, 512)])])
    dwg_out, (a_gwo, b_swi) = _mm(ypre0, dy0, "tn", BF16, tm=W_ROWS, tn=D, tk=S, name="gla_out_dw", pair=row_pair, phases=[
        _Phase(like_b_swi, b_swi, [_reduce_cross(t_swi, 1536, 1536, 512)])])
    p_gwo = dwg_out.reshape(N_DEV, W_ROWS, D)
    t_gwo = _pair_sum(p_gwo, a_gwo, 0, W_ROWS, "pair_sum_gla_w_out", table=mine)
    dypre0, _ = _mm(dy0, wg_out, "nt", F32, tm=1024, tn=1024, tk=D, name="gla_out_dx")
    late = [dnpre1, dnpost1, dlng, dlnb, dwsp, jnp.transpose(dbsp[:, :SGU_G])]
    late_pack = _pack(late)
    (dproj0, dogain, dbgate, dw2), (b_gwo, g_late) = _gla_bwd(proj0, dypre0, states, w2p, gla_b_gate, gla_o_gain, phases=[
        _Phase(like_b_out, None, [_reduce_cross(t_gwo, 0, 0, W_ROWS)]),
        _Phase(_sds((N_DEV,) + late_pack.shape, F32), None, [_gather_send(late_pack, 0, late_pack.shape[0])])])
    half = D // 2
    dwg_in_a, (g_late,) = _mm(h0, dproj0, "tn", BF16, tm=half, tn=896, tk=S, name="gla_in_dw_a", m_tiles=(0, 1), phases=[
        _Phase(_sds((N_DEV,) + late_pack.shape, F32), g_late, [_gather_pass(0, late_pack.shape[0])])])
    own_gwi, a_gwi = _blockify_pair(dwg_in_a, None, None, 0, "blockify_gla_w_in_a")
    t_gwi_a = _pair_sum(own_gwi, a_gwi, 0, half, "pair_sum_gla_w_in_a")
    dwg_in_b, (b_gwi,) = _mm(h0, dproj0, "tn", BF16, tm=half, tn=896, tk=S, name="gla_in_dw_b", m_tiles=(1, 1), phases=[
        _Phase(like_b_gwi, None, [_reduce_cross(t_gwi_a, 0, 0, 512)])])
    own_gwi, a_gwi = _blockify_pair(dwg_in_b, own_gwi, a_gwi, half, "blockify_gla_w_in_b")
    t_gwi_b = _pair_sum(own_gwi, a_gwi, half, half, "pair_sum_gla_w_in_b")
    dh0, (b_gwi,) = _mm(dproj0, wg_in, "nt", F32, tm=1024, tn=1024, tk=896, name="gla_in_dx", phases=[
        _Phase(like_b_gwi, b_gwi, [_reduce_cross(t_gwi_a, 512, 512, 512), _reduce_cross(t_gwi_b, 0, half, 512)])])
    (grad_x, dnpre0), (b_gwi,) = _first_bwd(dx1, dh0, x0, npre0, phases=[
        _Phase(like_b_gwi, b_gwi, [_reduce_cross(t_gwi_b, 512, half + 512, 256)])])

    early = [dnpre0, dnpost0, dbgate, dogain, dw2[:LR]]
    early_pack = _pack(early)
    like_early = _sds((N_DEV,) + early_pack.shape, F32)
    (g_swo, d_swo, nm_swo, nv_swo), (b_gwi, g_early) = _sum_adamw(
        p_swo, a_swo, b_swo, sgu_w_out[0], m_sgu_w_out[0], v_sgu_w_out[0], name="adamw_sgu_w_out", table=mine, phases=[
            _Phase(like_b_gwi, b_gwi, [_reduce_cross(t_gwi_b, 768, half + 768, 256)]),
            _Phase(like_early, None, [_gather_send(early_pack, 0, early_pack.shape[0])])])
    (g_swi_, d_swi, nm_swi, nv_swi), (g_early,) = _sum_adamw(
        p_swi, a_swi, b_swi, sgu_w_in[0], m_sgu_w_in[0], v_sgu_w_in[0], name="adamw_sgu_w_in", table=mine, phases=[
            _Phase(like_early, g_early, [_gather_pass(0, early_pack.shape[0])])])
    (g_gwo_, d_gwo, nm_gwo, nv_gwo), _ = _sum_adamw(
        p_gwo, a_gwo, b_gwo, gla_w_out[0], m_gla_w_out[0], v_gla_w_out[0], name="adamw_gla_w_out", table=mine)
    (g_gwi_, d_gwi, nm_gwi, nv_gwi), _ = _sum_adamw(
        own_gwi, a_gwi, b_gwi, gla_w_in[0], m_gla_w_in[0], v_gla_w_in[0], name="adamw_gla_w_in")

    g_npre1, g_npost1, g_lng_full, g_lnb_full, g_wsp, g_bsp = _unpack(_sum_parts(g_late, "sum_late_small_grads"), late)
    g_npre0, g_npost0, g_bgate, g_ogain, g_w2_full = _unpack(_sum_parts(g_early, "sum_early_small_grads"), early)
    g_w2 = lax.dynamic_slice(g_w2_full, (0, me * 128), (LR, 128))
    g_lng = lax.dynamic_slice(g_lng_full, (0, me * 256), (1, 256))
    g_lnb = lax.dynamic_slice(g_lnb_full, (0, me * 256), (1, 256))
    small_g = [jnp.concatenate([g_npre0, g_npre1], 0), jnp.concatenate([g_npost0, g_npost1], 0), g_w2, g_bgate, g_ogain,
               g_lng, g_lnb, g_wsp, g_bsp]
    small_w = [norm_pre, norm_post, gla_w_gate2[0], gla_b_gate, gla_o_gain, sgu_ln_gain, sgu_ln_bias, sgu_w_spatial[0], sgu_b_spatial[0]]
    small_m = [m_norm_pre, m_norm_post, m_gla_w_gate2[0], m_gla_b_gate, m_gla_o_gain, m_sgu_ln_gain, m_sgu_ln_bias, m_sgu_w_spatial[0], m_sgu_b_spatial[0]]
    small_v = [v_norm_pre, v_norm_post, v_gla_w_gate2[0], v_gla_b_gate, v_gla_o_gain, v_sgu_ln_gain, v_sgu_ln_bias, v_sgu_w_spatial[0], v_sgu_b_spatial[0]]
    d_pack, nm_pack, nv_pack = _adamw(_pack(small_w), _pack(small_g), _pack(small_m), _pack(small_v), "adamw_small")

    out_like = [norm_pre, norm_post, gla_w_gate2, gla_b_gate, gla_o_gain, sgu_ln_gain, sgu_ln_bias, sgu_w_spatial, sgu_b_spatial]
    sg_ = [g.reshape(s.shape) for g, s in zip(small_g, out_like)]
    sd_, sm_, sv_ = (_unpack(pk, out_like) for pk in (d_pack, nm_pack, nv_pack))

    def assemble(small_list, w_in_g, w_out_g, w_in_s, w_out_s):
        npre_, npost_, w2_, bg_, og_, lg_, lb_, wsp_, bsp_ = small_list
        return [npre_, npost_, w_in_g[None], w2_, bg_, og_, w_out_g[None], w_in_s[None], lg_, lb_, wsp_, bsp_, w_out_s[None]]

    return (loss, grad_x.reshape(1, S, D),
            *assemble(sg_, g_gwi_, g_gwo_, g_swi_, g_swo),
            *assemble(sd_, d_gwi, d_gwo, d_swi, d_swo),
            *assemble(sm_, nm_gwi, nm_gwo, nm_swi, nm_swo),
            *assemble(sv_, nv_gwi, nv_gwo, nv_swi, nv_swo))
```

```python
import functools

import jax
import jax.numpy as jnp
from jax import lax
from jax.experimental import pallas as pl
from jax.experimental.pallas import tpu as pltpu

F32 = jnp.float32
BF16 = jnp.bfloat16

N_DEV = 8
S = 2048
D = 2048
H = 4
DK = 256
DV = 512
C = 64
NC = S // C
GLA_COLS = 6160
GLA_PAD = 6272
Q0, K0, V0, G0, LR0 = 0, 1024, 2048, 4096, 6144
LR = 16
LRP = 128
SGU_COLS = 6144
SGU_BLOCK = 128
SGU_G = 8
SGU_GD = 256
EPS = 1e-6
GLA_TAU = 16.0

ADAM_LR, ADAM_B1, ADAM_B2, ADAM_EPS, ADAM_WD, ADAM_STEP = 0.001, 0.9, 0.999, 1e-08, 0.01, 10

V7X_VMEM_BYTES = 64 * 1024 * 1024
VMEM_CEILING = V7X_VMEM_BYTES - 6 * 1024 * 1024
MESH = pl.DeviceIdType.MESH
HIGHEST = lax.Precision.HIGHEST
HBM_SPEC = pl.BlockSpec(memory_space=pl.ANY)


def _sds(shape, dtype):
    return jax.ShapeDtypeStruct(tuple(shape), dtype)


def _nbytes(shape, dtype):
    n = 1
    for s in shape:
        n *= s
    return n * jnp.dtype(dtype).itemsize


def _dot(a, b, dims=(((1,), (0,)), ((), ())), precision=None):
    return lax.dot_general(a, b, dims, precision=precision, preferred_element_type=F32)


NN = (((1,), (0,)), ((), ()))
TN = (((0,), (0,)), ((), ()))
NT = (((1,), (1,)), ((), ()))


def _place():
    return lax.axis_index("x"), lax.axis_index("y"), lax.axis_index("c")


def _index_of(px, py, pc):
    return 4 * px + 2 * py + pc


def _chips(x, y):
    return [(1 - x, y), (x, 1 - y), (1 - x, 1 - y)]


def _rcopy(src, dst, send_sem, recv_sem, to):
    return pltpu.make_async_remote_copy(src_ref=src, dst_ref=dst, send_sem=send_sem, recv_sem=recv_sem,
                                        device_id=to, device_id_type=MESH)


class _Move:
    def __init__(self, ins, n_remote, n_local, make):
        self.ins, self.n_remote, self.n_local, self.make = list(ins), n_remote, n_local, make

    def start(self, in_refs, buf, sems):
        sends, _, local = self.make(in_refs, buf, *sems)
        for cp in local + sends:
            cp.start()

    def finish(self, in_refs, buf, sems):
        sends, arrivals, local = self.make(in_refs, buf, *sems)
        for cp in arrivals:
            cp.wait_recv()
        for cp in sends:
            cp.wait_send()
        for cp in local:
            cp.wait()


class _Phase:
    def __init__(self, like, so_far, moves):
        self.like, self.so_far, self.moves = like, so_far, list(moves)


def _gather_send(shard, r0, nr):
    def make(in_refs, g, ss, rs, ls):
        sh, = in_refs
        x, y, c = _place()
        me = _index_of(x, y, c)
        rows = pl.ds(r0, nr)
        peers = [(x, y, 1 - c)] + [(px, py, c) for px, py in _chips(x, y)]
        sends = [_rcopy(sh.at[rows], g.at[me, rows], ss.at[k], rs.at[k], p) for k, p in enumerate(peers)]
        arrivals = [_rcopy(sh.at[rows], g.at[_index_of(*p), rows], ss.at[k], rs.at[k], p) for k, p in enumerate(peers)]
        return sends, arrivals, [pltpu.make_async_copy(sh.at[rows], g.at[me, rows], ls.at[0])]

    return _Move([shard], 4, 1, make)


def _gather_pass(r0, nr):
    def make(in_refs, g, ss, rs, ls):
        x, y, c = _place()
        rows = pl.ds(r0, nr)
        sends = [_rcopy(g.at[_index_of(px, py, c), rows], g.at[_index_of(px, py, c), rows], ss.at[j], rs.at[j], (x, y, 1 - c))
                 for j, (px, py) in enumerate(_chips(x, y))]
        arrivals = [_rcopy(g.at[_index_of(px, py, c), rows], g.at[_index_of(px, py, 1 - c), rows], ss.at[j], rs.at[j], (x, y, 1 - c))
                    for j, (px, py) in enumerate(_chips(x, y))]
        return sends, arrivals, []

    return _Move([], 3, 0, make)


def _own_table():
    x, y, c = _place()
    return jnp.stack([_index_of(px, py, c) for px, py in [(x, y)] + _chips(x, y)]).astype(jnp.int32)


def _blockify_pair(dw, own_so_far, a_so_far, dst_r0, name):
    rows, tr, cw, win = dw.shape[0], 256, GLA_COLS // N_DEV, 896
    n_steps = rows // tr

    def body(*refs):
        x_ref, own_ref, a_ref, stage_ref, send_sems, recv_sem = refs[0], *refs[-5:]
        i = pl.program_id(0)
        x, y, c = _place()

        def send(slot, k):
            dst = a_ref.at[k, pl.ds(pl.multiple_of(dst_r0 + i * tr, tr), tr)]
            return _rcopy(stage_ref.at[slot], dst, send_sems.at[slot], recv_sem.at[0], (x, y, 1 - c))

        for j in range(N_DEV):
            window = x_ref[:, 768 * j:768 * j + win].astype(F32)
            tile = (pltpu.roll(window, win - 2 * j, 1) if j else window)[:, :cw].astype(BF16)
            k = ((j >> 2) ^ x) + 2 * (((j >> 1) & 1) ^ y)

            @pl.when((j & 1) == c)
            def _():
                own_ref[k] = tile

            @pl.when((j & 1) != c)
            def _():
                slot = (j >> 1) & 1
                if j >> 1 >= 2:
                    send(slot, k).wait_send()
                else:
                    pl.when(i > 0)(lambda: send(slot, k).wait_send())
                stage_ref[slot] = tile
                send(slot, k).start()

        @pl.when(i == n_steps - 1)
        def _():
            send(0, 0).wait_send()
            send(1, 0).wait_send()
            arrived = a_ref.at[:, pl.ds(dst_r0, rows)]
            _rcopy(arrived, arrived, send_sems.at[0], recv_sem.at[0], (x, y, 1 - c)).wait_recv()

    continues = a_so_far is not None
    own, a = pl.pallas_call(
        body, grid=(n_steps,),
        in_specs=[pl.BlockSpec((tr, GLA_PAD), lambda i: (i, 0))] + [HBM_SPEC] * (2 * continues),
        out_specs=[pl.BlockSpec((4, tr, cw), lambda i: (0, dst_r0 // tr + i, 0)), HBM_SPEC],
        out_shape=[_sds((4, D, cw), BF16), _sds((4, D, cw), BF16)],
        scratch_shapes=[pltpu.VMEM((2, tr, cw), BF16), pltpu.SemaphoreType.DMA((2,)), pltpu.SemaphoreType.DMA((1,))],
        input_output_aliases={1: 0, 2: 1} if continues else {},
        compiler_params=pltpu.CompilerParams(dimension_semantics=("arbitrary",), vmem_limit_bytes=48 * 1024 * 1024),
        name=name,
    )(*([dw] + [own_so_far, a_so_far] * continues))
    return own, a


def _reduce_cross(sums, src_r0, dst_r0, nr):
    def make(in_refs, b, ss, rs, ls):
        t, = in_refs
        x, y, c = _place()
        src, dst = pl.ds(src_r0, nr), pl.ds(dst_r0, nr)
        sends = [_rcopy(t.at[j, src], b.at[j, dst], ss.at[j], rs.at[j], (px, py, c)) for j, (px, py) in enumerate(_chips(x, y))]
        return sends, sends, []

    return _Move([sums], 3, 0, make)


def _hosted(body, *, name, grid, in_specs, out_specs, out_shape, args, scratch_shapes=(), block_bytes, scratch_bytes=0,
            phases=(), table=None):
    n_in, n_out, n_scr = len(args), len(out_shape), len(scratch_shapes)
    all_args, all_out_shape, sems, aliases, layout = list(args), list(out_shape), [], {}, []
    for j, ph in enumerate(phases):
        counts = []
        for mv in ph.moves:
            all_args += mv.ins
            counts.append(len(mv.ins))
            sems += [pltpu.SemaphoreType.DMA((mv.n_remote,)), pltpu.SemaphoreType.DMA((mv.n_remote,)),
                     pltpu.SemaphoreType.DMA((max(mv.n_local, 1),))]
        if ph.so_far is not None:
            aliases[len(all_args)] = n_out + j
            all_args.append(ph.so_far)
        layout.append((counts, ph.so_far is not None))
        all_out_shape.append(ph.like)
    n_extra_in = len(all_args) - n_in

    def wrapped(*refs):
        ins, pos = refs[:n_in], n_in
        move_ins = []
        for counts, continues in layout:
            per_move = []
            for cnt in counts:
                per_move.append(refs[pos:pos + cnt])
                pos += cnt
            pos += continues
            move_ins.append(per_move)
        outs = refs[pos:pos + n_out]
        bufs = refs[pos + n_out:pos + n_out + len(phases)]
        pos += n_out + len(phases)
        scratch = refs[pos:pos + n_scr]
        pos += n_scr
        move_sems = []
        for ph in phases:
            per_move = []
            for _ in ph.moves:
                per_move.append(refs[pos:pos + 3])
                pos += 3
            move_sems.append(per_move)

        def each_move(fn_name):
            for ph, buf, per_in, per_sem in zip(phases, bufs, move_ins, move_sems):
                for mv, mv_in, mv_sem in zip(ph.moves, per_in, per_sem):
                    getattr(mv, fn_name)(mv_in, buf, mv_sem)

        if phases:
            first = functools.reduce(jnp.logical_and, [pl.program_id(a) == 0 for a in range(len(grid))])
            last = functools.reduce(jnp.logical_and, [pl.program_id(a) == grid[a] - 1 for a in range(len(grid))])
            pl.when(first)(lambda: each_move("start"))
        body(*ins, *outs, *scratch)
        if phases:
            pl.when(last)(lambda: each_move("finish"))

    est = 2 * block_bytes + scratch_bytes
    params = pltpu.CompilerParams(dimension_semantics=("arbitrary",) * len(grid),
                                  vmem_limit_bytes=min(VMEM_CEILING, max(32 * 1024 * 1024, 2 * est)))
    all_in_specs, all_out_specs = list(in_specs) + [HBM_SPEC] * n_extra_in, list(out_specs) + [HBM_SPEC] * len(phases)
    if table is None:
        results = pl.pallas_call(
            wrapped, grid=grid, in_specs=all_in_specs, out_specs=all_out_specs, out_shape=all_out_shape,
            scratch_shapes=list(scratch_shapes) + sems, input_output_aliases=aliases, compiler_params=params, name=name,
        )(*all_args)
    else:
        results = pl.pallas_call(
            lambda table_ref, *refs: wrapped(*refs),
            grid_spec=pltpu.PrefetchScalarGridSpec(num_scalar_prefetch=1, grid=grid, in_specs=all_in_specs, out_specs=all_out_specs,
                                                   scratch_shapes=list(scratch_shapes) + sems),
            out_shape=all_out_shape, input_output_aliases={k + 1: v for k, v in aliases.items()}, compiler_params=params, name=name,
        )(table, *all_args)
    return list(results[:n_out]), list(results[n_out:])


def _gather_first(shard, small, name):
    cw, tr, n_tiles = shard.shape[1], 256, GLA_PAD // 128

    def body(sh_ref, sm_ref, wn_ref, g_ref, gs_ref, wt_ref, win_ref, tmp_ref, send_sems, recv_sems, local_sems):
        x, y, c = _place()
        me, sibling = (x, y, c), (x, y, 1 - c)
        chips = _chips(x, y)

        def copy(base, out_ref, k, block, to, src=None):
            dst = out_ref.at[_index_of(*block)]
            return _rcopy(dst if src is None else src, dst, send_sems.at[base + k], recv_sems.at[base + k], to)

        wcopy = functools.partial(copy, 0, g_ref)
        scopy = functools.partial(copy, 7, gs_ref)

        def load(src_ref, slot):
            cp = pltpu.make_async_copy(src_ref, win_ref.at[slot], local_sems.at[0])
            cp.start()
            cp.wait()

        def place(slot, block):
            b = _index_of(*block)

            def rows_chunk(r, carry):
                rows = pl.ds(pl.multiple_of(r * tr, tr), tr)
                tmp_ref[:, :cw] = win_ref[slot, rows, :].astype(F32)
                shifted = pltpu.roll(tmp_ref[...], 2 * b, 1)
                for u in range(7):
                    wt_ref[6 * b + u, rows, :] = (wt_ref[6 * b + u, rows, :].astype(F32) + shifted[:, 128 * u:128 * (u + 1)]).astype(BF16)
                return carry

            lax.fori_loop(0, D // tr, rows_chunk, 0)

        small_own = pltpu.make_async_copy(sm_ref, gs_ref.at[_index_of(*me)], local_sems.at[1])
        small_own.start()
        first = [wcopy(1 + j, me, (*chip, c), src=sh_ref) for j, chip in enumerate(chips)]
        first += [scopy(0, me, sibling, src=sm_ref)] + [scopy(1 + j, me, (*chip, c), src=sm_ref) for j, chip in enumerate(chips)]
        for cp in first:
            cp.start()

        def clear(t, carry):
            wt_ref[t] = jnp.zeros((D, 128), BF16)
            return carry

        lax.fori_loop(0, n_tiles, clear, 0)
        tmp_ref[...] = jnp.zeros_like(tmp_ref)

        load(sh_ref, 0)
        to_sibling = wcopy(0, me, sibling, src=win_ref.at[0])
        to_sibling.start()
        place(0, me)
        passed = []
        for j, chip in enumerate(chips):
            slot = (j + 1) % 2
            wcopy(1 + j, (*chip, c), me).wait_recv()
            if j == 1:
                to_sibling.wait_send()
            if j == 2:
                passed[0].wait_send()
            load(g_ref.at[_index_of(*chip, c)], slot)
            cp = wcopy(4 + j, (*chip, c), sibling, src=win_ref.at[slot])
            cp.start()
            passed.append(cp)
            place(slot, (*chip, c))
        small_passed = []
        for j, chip in enumerate(chips):
            scopy(1 + j, (*chip, c), me).wait_recv()
            cp = scopy(4 + j, (*chip, c), sibling)
            cp.start()
            small_passed.append(cp)
        passed[1].wait_send()
        wcopy(0, sibling, me).wait_recv()
        load(g_ref.at[_index_of(*sibling)], 0)
        place(0, sibling)
        passed[2].wait_send()
        for j, chip in enumerate(chips):
            slot = (j + 1) % 2
            wcopy(4 + j, (*chip, 1 - c), me).wait_recv()
            load(g_ref.at[_index_of(*chip, 1 - c)], slot)
            place(slot, (*chip, 1 - c))
        scopy(0, sibling, me).wait_recv()
        for j, chip in enumerate(chips):
            scopy(4 + j, (*chip, 1 - c), me).wait_recv()
        for cp in first + small_passed:
            cp.wait_send()
        small_own.wait()
        for t in range(n_tiles):
            pltpu.make_async_copy(wt_ref.at[t], wn_ref.at[:, pl.ds(128 * t, 128)], local_sems.at[2]).start()
        pltpu.make_async_copy(wn_ref, wn_ref, local_sems.at[2]).wait()

    wn, _, gs = pl.pallas_call(
        body,
        in_specs=[HBM_SPEC] * 2, out_specs=[HBM_SPEC] * 3,
        out_shape=[_sds((D, GLA_PAD), BF16), _sds((N_DEV,) + shard.shape, BF16), _sds((N_DEV,) + small.shape, small.dtype)],
        scratch_shapes=[pltpu.VMEM((n_tiles, D, 128), BF16), pltpu.VMEM((2, D, cw), BF16), pltpu.VMEM((tr, 7 * 128), F32),
                        pltpu.SemaphoreType.DMA((14,)), pltpu.SemaphoreType.DMA((14,)), pltpu.SemaphoreType.DMA((3,))],
        compiler_params=pltpu.CompilerParams(vmem_limit_bytes=48 * 1024 * 1024),
        name=name,
    )(shard, small)
    return wn, gs


def _mm(a, b, mode, out_dtype, *, tm, tn, tk, name, b_blocked=False, out_blocked=False, m_tiles=None, pair=None, phases=()):
    if mode == "nn":
        (m, k), dims = a.shape, NN
        a_blk, a_map = (tm, tk), (lambda i, j, kk: (i, kk))
        if b_blocked:
            assert b.shape[1] == k and b.shape[2] == tn and tk == k
            n = b.shape[0] * tn
            b_spec = pl.BlockSpec((None, tk, tn), lambda i, j, kk: (j, kk, 0))
        else:
            assert b.shape[0] == k
            n = b.shape[1]
            b_spec = pl.BlockSpec((tk, tn), lambda i, j, kk: (kk, j))
    elif mode == "tn":
        (k, m), n, dims = a.shape, b.shape[1], TN
        assert b.shape[0] == k
        first = 0 if m_tiles is None else m_tiles[0]
        a_blk, a_map = (tk, tm), (lambda i, j, kk: (kk, i + first))
        b_spec = pl.BlockSpec((tk, tn), lambda i, j, kk: (kk, j))
    else:
        (m, k), dims = a.shape, NT
        a_blk, a_map = (tm, tk), (lambda i, j, kk: (i, kk))
        if b_blocked:
            assert b.shape[0] * b.shape[2] == k and b.shape[2] == tk
            n = b.shape[1]
            b_spec = pl.BlockSpec((None, tn, tk), lambda i, j, kk: (kk, j, 0))
        else:
            assert b.shape[1] == k
            n = b.shape[0]
            b_spec = pl.BlockSpec((tn, tk), lambda i, j, kk: (j, kk))
    assert m % tm == 0 and n % tn == 0 and k % tk == 0, (a.shape, b.shape, mode)
    nk = k // tk
    n_row_tiles = m // tm if m_tiles is None else m_tiles[1]
    if out_blocked:
        out_shape, out_spec = _sds((n // tn, n_row_tiles * tm, tn), out_dtype), pl.BlockSpec((None, tm, tn), lambda i, j, kk: (j, i, 0))
    else:
        out_shape, out_spec = _sds((n_row_tiles * tm, n), out_dtype), pl.BlockSpec((tm, tn), lambda i, j, kk: (i, j))

    grid = (n_row_tiles, n // tn, nk)

    def body(a_ref, b_ref, o_ref, *rest):
        p = _dot(a_ref[...], b_ref[...], dims)
        if nk == 1:
            o_ref[...] = p.astype(out_dtype)
            if pair is not None:
                _send_to_sibling(p.astype(out_dtype), *rest)
        else:
            acc_ref, = rest
            kk = pl.program_id(2)

            @pl.when(kk == 0)
            def _():
                acc_ref[...] = p

            @pl.when(kk > 0)
            def _():
                acc_ref[...] += p

            @pl.when(kk == nk - 1)
            def _():
                o_ref[...] = acc_ref[...].astype(out_dtype)

    def _send_to_sibling(tile, pair_ref, stage_ref, send_sems, recv_sem):
        i, j = pl.program_id(0), pl.program_id(1)
        x, y, c = _place()
        blk = pair["block"](i, j)
        k = ((blk >> 2) ^ x) + 2 * (((blk >> 1) & 1) ^ y)
        ordinal = pair["ordinal"](i, j)

        def send(slot):
            return _rcopy(stage_ref.at[slot], pair["dst"](pair_ref, k, i, j), send_sems.at[slot], recv_sem.at[0], (x, y, 1 - c))

        @pl.when((blk & 1) != c)
        def _():
            slot = ordinal & 1

            @pl.when(ordinal >= 2)
            def _():
                send(slot).wait_send()

            stage_ref[slot] = tile
            send(slot).start()

        @pl.when((i == grid[0] - 1) & (j == grid[1] - 1))
        def _():
            send(0).wait_send()
            send(1).wait_send()
            _rcopy(pair_ref, pair_ref, send_sems.at[0], recv_sem.at[0], (x, y, 1 - c)).wait_recv()

    blocks = _nbytes(a_blk, a.dtype) + tk * tn * jnp.dtype(b.dtype).itemsize + _nbytes((tm, tn), out_dtype)
    out_specs, out_shapes, scratch = [out_spec], [out_shape], [] if nk == 1 else [pltpu.VMEM((tm, tn), F32)]
    scratch_bytes = _nbytes((tm, tn), F32) * (nk > 1)
    if pair is not None:
        assert nk == 1
        out_specs, out_shapes = out_specs + [HBM_SPEC], out_shapes + [pair["like"]]
        scratch = [pltpu.VMEM((2, tm, tn), out_dtype), pltpu.SemaphoreType.DMA((2,)), pltpu.SemaphoreType.DMA((1,))]
        scratch_bytes = 2 * _nbytes((tm, tn), out_dtype)
    outs, bufs = _hosted(
        body, name=name, grid=grid,
        in_specs=[pl.BlockSpec(a_blk, a_map), b_spec], out_specs=out_specs, out_shape=out_shapes, args=[a, b],
        scratch_shapes=scratch, block_bytes=blocks, scratch_bytes=scratch_bytes, phases=phases)
    return outs[0], outs[1:] + bufs


RB = 256


def _row_spec(width):
    return pl.BlockSpec((RB, width), lambda i: (i, 0))


def _vec_spec(width):
    return pl.BlockSpec((1, width), lambda i: (0, 0))


def _rinv(x):
    return lax.rsqrt(jnp.mean(x * x, axis=-1, keepdims=True) + EPS)


def _norm_bwd(dyn, xhat, r):
    return r * (dyn - xhat * jnp.mean(dyn * xhat, axis=-1, keepdims=True))


def _colsum(x):
    return jnp.sum(x, axis=0, keepdims=True)


def _accumulate(ref, value):
    @pl.when(pl.program_id(0) == 0)
    def _():
        ref[...] = value

    @pl.when(pl.program_id(0) > 0)
    def _():
        ref[...] += value


def _prenorm(x, gain):
    def body(x_ref, g_ref, h_ref):
        xv = x_ref[...]
        h_ref[...] = (xv * _rinv(xv) * g_ref[...]).astype(BF16)

    outs, _ = _hosted(body, name="prenorm", grid=(S // RB,), in_specs=[_row_spec(D), _vec_spec(D)], out_specs=[_row_spec(D)],
                      out_shape=[_sds((S, D), BF16)], args=[x, gain], block_bytes=RB * D * 6)
    return outs[0]


def _mid_fwd(x, y, npost, npre, phases=()):
    def body(x_ref, y_ref, po_ref, pr_ref, x1_ref, h1_ref):
        yv = y_ref[...]
        x1 = x_ref[...] + yv * _rinv(yv) * po_ref[...]
        x1_ref[...] = x1
        h1_ref[...] = (x1 * _rinv(x1) * pr_ref[...]).astype(BF16)

    return _hosted(body, name="mid_fwd", grid=(S // RB,), in_specs=[_row_spec(D), _row_spec(D), _vec_spec(D), _vec_spec(D)],
                   out_specs=[_row_spec(D), _row_spec(D)], out_shape=[_sds((S, D), F32), _sds((S, D), BF16)],
                   args=[x, y, npost, npre], block_bytes=RB * D * 14, phases=phases)


def _final(x1, y1, tgt, npost):
    def body(x_ref, y_ref, t_ref, po_ref, loss_ref, dx_ref, dy_ref, dpo_ref):
        yv = y_ref[...]
        r = _rinv(yv)
        yhat = yv * r
        err = x_ref[...] + yhat * po_ref[...] - t_ref[...]
        dx = err * (1.0 / D)
        dx_ref[...] = dx
        dy_ref[...] = _norm_bwd(dx * po_ref[...], yhat, r).astype(BF16)
        _accumulate(loss_ref, _colsum(err * err))
        _accumulate(dpo_ref, _colsum(dx * yhat))

    outs, _ = _hosted(body, name="final", grid=(S // RB,), in_specs=[_row_spec(D), _row_spec(D), _row_spec(D), _vec_spec(D)],
                      out_specs=[_vec_spec(D), _row_spec(D), _row_spec(D), _vec_spec(D)],
                      out_shape=[_sds((1, D), F32), _sds((S, D), F32), _sds((S, D), BF16), _sds((1, D), F32)],
                      args=[x1, y1, tgt, npost], block_bytes=RB * D * 18)
    return outs


def _mid_bwd(dx2, dh1, x1, y0, npre, npost, phases=()):
    def body(dx2_ref, dh_ref, x_ref, y_ref, pr_ref, po_ref, dx1_ref, dy_ref, dpr_ref, dpo_ref):
        xv = x_ref[...]
        r = _rinv(xv)
        xhat = xv * r
        dh = dh_ref[...]
        dx1 = dx2_ref[...] + _norm_bwd(dh * pr_ref[...], xhat, r)
        dx1_ref[...] = dx1
        yv = y_ref[...]
        ry = _rinv(yv)
        yhat = yv * ry
        dy_ref[...] = _norm_bwd(dx1 * po_ref[...], yhat, ry).astype(BF16)
        _accumulate(dpr_ref, _colsum(dh * xhat))
        _accumulate(dpo_ref, _colsum(dx1 * yhat))

    return _hosted(body, name="mid_bwd", grid=(S // RB,), in_specs=[_row_spec(D)] * 4 + [_vec_spec(D)] * 2,
                   out_specs=[_row_spec(D), _row_spec(D), _vec_spec(D), _vec_spec(D)],
                   out_shape=[_sds((S, D), F32), _sds((S, D), BF16), _sds((1, D), F32), _sds((1, D), F32)],
                   args=[dx2, dh1, x1, y0, npre, npost], block_bytes=RB * D * 22, phases=phases)


def _first_bwd(dx1, dh0, x0, npre, phases=()):
    def body(dx1_ref, dh_ref, x_ref, pr_ref, gx_ref, dpr_ref):
        xv = x_ref[...]
        r = _rinv(xv)
        xhat = xv * r
        dh = dh_ref[...]
        gx_ref[...] = dx1_ref[...] + _norm_bwd(dh * pr_ref[...], xhat, r)
        _accumulate(dpr_ref, _colsum(dh * xhat))

    return _hosted(body, name="first_bwd", grid=(S // RB,), in_specs=[_row_spec(D)] * 3 + [_vec_spec(D)],
                   out_specs=[_row_spec(D), _vec_spec(D)], out_shape=[_sds((S, D), F32), _sds((1, D), F32)],
                   args=[dx1, dh0, x0, npre], block_bytes=RB * D * 16, phases=phases)


GLA_RB = 256
GLA_CPB = GLA_RB // C


def _sigmoid(x):
    return 1.0 / (1.0 + jnp.exp(-x))


def _tri(strict):
    r = lax.broadcasted_iota(jnp.int32, (C, C), 0)
    c = lax.broadcasted_iota(jnp.int32, (C, C), 1)
    return jnp.where(c < r if strict else c <= r, 1.0, 0.0).astype(F32)


def _gla_gate(glr_b, w2_h, b_h, tri):
    z = _dot(glr_b, w2_h) + b_h
    log_a = (jnp.minimum(z, 0.0) - jnp.log(1.0 + jnp.exp(-jnp.abs(z)))) * (1.0 / GLA_TAU)
    bcum = _dot(tri, log_a, precision=HIGHEST)
    b_end = jnp.sum(log_a, axis=0, keepdims=True)
    return z, jnp.exp(b_end - bcum), jnp.exp(b_end)


def _gla_fwd(proj, w2p, bgate, ogain, phases=()):
    def body(p_ref, w2_ref, b_ref, og_ref, y_ref, st_out_ref, st_ref):
        @pl.when(pl.program_id(0) == 0)
        def _():
            st_ref[...] = jnp.zeros_like(st_ref)

        tri = _tri(False)

        def chunk(ci, carry):
            rows = pl.ds(pl.multiple_of(ci * C, C), C)
            glr_b = p_ref[rows, LR0:LR0 + LRP].astype(BF16)
            for h in range(H):
                _, ea, dec = _gla_gate(glr_b, w2_ref[:, h * DK:(h + 1) * DK], b_ref[:, h * DK:(h + 1) * DK], tri)
                k_dec = (p_ref[rows, K0 + h * DK:K0 + (h + 1) * DK] * ea).astype(BF16)
                v_b = p_ref[rows, V0 + h * DV:V0 + (h + 1) * DV].astype(BF16)
                st = st_ref[h] * dec + _dot(v_b, k_dec, TN)
                st_ref[h] = st
                st_b = st.astype(BF16)
                st_out_ref[ci, h] = st_b
                q_b = (p_ref[rows, Q0 + h * DK:Q0 + (h + 1) * DK] * (DK ** -0.5)).astype(BF16)
                o = _dot(q_b, st_b, NT)
                on = o * _rinv(o)
                g = p_ref[rows, G0 + h * DV:G0 + (h + 1) * DV]
                y_ref[rows, h * DV:(h + 1) * DV] = (on * og_ref[:, h * DV:(h + 1) * DV] * (g * _sigmoid(g))).astype(BF16)
            return carry

        lax.fori_loop(0, GLA_CPB, chunk, 0)

    blocks = GLA_RB * GLA_PAD * 4 + GLA_RB * D * 2 + GLA_CPB * H * DV * DK * 2
    return _hosted(
        body, name="gla_fwd", grid=(S // GLA_RB,),
        in_specs=[pl.BlockSpec((GLA_RB, GLA_PAD), lambda i: (i, 0)),
                  pl.BlockSpec((LRP, H * DK), lambda i: (0, 0)),
                  pl.BlockSpec((1, H * DK), lambda i: (0, 0)),
                  pl.BlockSpec((1, H * DV), lambda i: (0, 0))],
        out_specs=[pl.BlockSpec((GLA_RB, H * DV), lambda i: (i, 0)),
                   pl.BlockSpec((GLA_CPB, H, DV, DK), lambda i: (i, 0, 0, 0))],
        out_shape=[_sds((S, H * DV), BF16), _sds((NC, H, DV, DK), BF16)],
        args=[proj, w2p, bgate, ogain], scratch_shapes=[pltpu.VMEM((H, DV, DK), F32)],
        block_bytes=blocks, scratch_bytes=H * DV * DK * 4, phases=phases)


def _gla_bwd(proj, dypre, states, w2p, bgate, ogain, phases=()):
    nb = S // GLA_RB

    def body(p_ref, dy_ref, st_blk_ref, st_prev_ref, w2_ref, b_ref, og_ref,
             dp_ref, dog_ref, dbg_ref, dw2_ref, r_ref):
        step = pl.program_id(0)

        @pl.when(step == 0)
        def _():
            r_ref[...] = jnp.zeros_like(r_ref)
            dog_ref[...] = jnp.zeros_like(dog_ref)
            dbg_ref[...] = jnp.zeros_like(dbg_ref)
            dw2_ref[...] = jnp.zeros_like(dw2_ref)

        tri = _tri(False)
        tri_strict = _tri(True)
        has_prev = jnp.where(step < nb - 1, 1.0, 0.0).astype(F32)

        def chunk(ci, st_prev_of):
            rows = pl.ds(ci * C if isinstance(ci, int) else pl.multiple_of(ci * C, C), C)
            glr_b = p_ref[rows, LR0:LR0 + LRP].astype(BF16)
            dglr = jnp.zeros((C, LRP), F32)
            for h in range(H):
                kcol = slice(h * DK, (h + 1) * DK)
                vcol = slice(h * DV, (h + 1) * DV)
                w2_h = w2_ref[:, kcol]
                z, ea, dec = _gla_gate(glr_b, w2_h, b_ref[:, kcol], tri)
                k_dec = p_ref[rows, K0 + h * DK:K0 + (h + 1) * DK] * ea
                k_dec_b = k_dec.astype(BF16)
                v_b = p_ref[rows, V0 + h * DV:V0 + (h + 1) * DV].astype(BF16)
                q_b = (p_ref[rows, Q0 + h * DK:Q0 + (h + 1) * DK] * (DK ** -0.5)).astype(BF16)
                st_b = st_blk_ref[ci, h]
                o = _dot(q_b, st_b, NT)
                rinv = _rinv(o)
                on = o * rinv
                g = p_ref[rows, G0 + h * DV:G0 + (h + 1) * DV]
                sg = _sigmoid(g)
                og = og_ref[:, vcol]
                dyp = dy_ref[rows, vcol]
                dp_ref[rows, G0 + h * DV:G0 + (h + 1) * DV] = (dyp * (on * og) * (sg * (1.0 + g * (1.0 - sg)))).astype(BF16)
                dpn = dyp * (g * sg)
                dog_ref[:, vcol] += _colsum(dpn * on)
                do_b = _norm_bwd(dpn * og, on, rinv).astype(BF16)
                gt = _dot(do_b, q_b, TN) + r_ref[h]
                gt_b = gt.astype(BF16)
                dp_ref[rows, Q0 + h * DK:Q0 + (h + 1) * DK] = (_dot(do_b, st_b) * (DK ** -0.5)).astype(BF16)
                dkd = _dot(v_b, gt_b)
                dp_ref[rows, V0 + h * DV:V0 + (h + 1) * DV] = _dot(k_dec_b, gt_b, NT).astype(BF16)
                dp_ref[rows, K0 + h * DK:K0 + (h + 1) * DK] = (dkd * ea).astype(BF16)
                ddec = _colsum(gt * st_prev_of(h))
                dla = _dot(tri_strict, dkd * k_dec, precision=HIGHEST) + ddec * dec
                dz = dla * (1.0 / GLA_TAU) * (1.0 - _sigmoid(z))
                dz_b = dz.astype(BF16)
                r_ref[h] = gt * dec
                dbg_ref[:, kcol] += _colsum(dz)
                dw2_ref[:, kcol] += _dot(glr_b, dz_b, TN)
                dglr = dglr + _dot(dz_b, w2_h, NT)
            dp_ref[rows, LR0:LR0 + LRP] = dglr.astype(BF16)

        def later_chunk(t, carry):
            ci = GLA_CPB - 1 - t
            chunk(ci, lambda h: st_blk_ref[ci - 1, h].astype(F32))
            return carry

        lax.fori_loop(0, GLA_CPB - 1, later_chunk, 0)
        chunk(0, lambda h: st_prev_ref[0, h].astype(F32) * has_prev)

    blocks = (GLA_RB * GLA_PAD * 4 + GLA_RB * D * 4 + (GLA_CPB + 1) * H * DV * DK * 2 + GLA_RB * GLA_PAD * 2)
    rev = lambda i: nb - 1 - i
    return _hosted(
        body, name="gla_bwd", grid=(nb,),
        in_specs=[pl.BlockSpec((GLA_RB, GLA_PAD), lambda i: (rev(i), 0)),
                  pl.BlockSpec((GLA_RB, H * DV), lambda i: (rev(i), 0)),
                  pl.BlockSpec((GLA_CPB, H, DV, DK), lambda i: (rev(i), 0, 0, 0)),
                  pl.BlockSpec((1, H, DV, DK), lambda i: (jnp.maximum(rev(i) * GLA_CPB - 1, 0), 0, 0, 0)),
                  pl.BlockSpec((LRP, H * DK), lambda i: (0, 0)),
                  pl.BlockSpec((1, H * DK), lambda i: (0, 0)),
                  pl.BlockSpec((1, H * DV), lambda i: (0, 0))],
        out_specs=[pl.BlockSpec((GLA_RB, GLA_PAD), lambda i: (rev(i), 0)),
                   pl.BlockSpec((1, H * DV), lambda i: (0, 0)),
                   pl.BlockSpec((1, H * DK), lambda i: (0, 0)),
                   pl.BlockSpec((LRP, H * DK), lambda i: (0, 0))],
        out_shape=[_sds((S, GLA_PAD), BF16), _sds((1, H * DV), F32), _sds((1, H * DK), F32), _sds((LRP, H * DK), F32)],
        args=[proj, dypre, states, states, w2p, bgate, ogain], scratch_shapes=[pltpu.VMEM((H, DV, DK), F32)],
        block_bytes=blocks, scratch_bytes=H * DV * DK * 4, phases=phases)


SGU_RB = 256
GELU_C = 0.7978845608028654
GELU_A = 0.044715


def _gelu(x):
    return 0.5 * x * (1.0 + jnp.tanh(GELU_C * (x + GELU_A * x * x * x)))


def _gelu_grad(x):
    t = jnp.tanh(GELU_C * (x + GELU_A * x * x * x))
    return 0.5 * (1.0 + t) + 0.5 * x * (1.0 - t * t) * (GELU_C * (1.0 + 3.0 * GELU_A * x * x))


def _causal_mask(transposed=False):
    i = lax.broadcasted_iota(jnp.int32, (SGU_BLOCK, SGU_BLOCK), 1 if transposed else 0)
    j = lax.broadcasted_iota(jnp.int32, (SGU_BLOCK, SGU_BLOCK), 0 if transposed else 1)
    return (i >= C) | (j < C)


def _layer_norm(vf, gain, bias):
    mu = jnp.mean(vf, axis=-1, keepdims=True)
    cen = vf - mu
    rstd = lax.rsqrt(jnp.mean(cen * cen, axis=-1, keepdims=True) + EPS)
    xhat = cen * rstd
    return xhat, rstd, xhat * gain + bias


def _sgu_fwd(proj, lng, lnb, ws, bsb, phases=()):
    def body(p_ref, g_ref, b_ref, ws_ref, bs_ref, o_ref):
        mask = _causal_mask()
        for n in range(SGU_RB // SGU_BLOCK):
            rows = slice(n * SGU_BLOCK, (n + 1) * SGU_BLOCK)
            _, _, vn = _layer_norm(_gelu(p_ref[rows, D:2 * D]), g_ref[...], b_ref[...])
            vn_b = vn.astype(BF16)
            for gi in range(SGU_G):
                cols = slice(gi * SGU_GD, (gi + 1) * SGU_GD)
                w = jnp.where(mask, ws_ref[gi], 0.0).astype(BF16)
                vs = _dot(w, vn_b[:, cols]) + bs_ref[gi]
                gate = p_ref[rows, 2 * D + gi * SGU_GD:2 * D + (gi + 1) * SGU_GD]
                o_ref[rows, cols] = (_gelu(p_ref[rows, cols]) * vs * (gate * _sigmoid(gate))).astype(BF16)

    blocks = SGU_RB * SGU_COLS * 4 + SGU_RB * D * 2 + SGU_G * SGU_BLOCK * (SGU_BLOCK + SGU_GD) * 4
    return _hosted(
        body, name="sgu_fwd", grid=(S // SGU_RB,),
        in_specs=[pl.BlockSpec((SGU_RB, SGU_COLS), lambda i: (i, 0)),
                  pl.BlockSpec((1, D), lambda i: (0, 0)), pl.BlockSpec((1, D), lambda i: (0, 0)),
                  pl.BlockSpec((SGU_G, SGU_BLOCK, SGU_BLOCK), lambda i: (0, 0, 0)),
                  pl.BlockSpec((SGU_G, SGU_BLOCK, SGU_GD), lambda i: (0, 0, 0))],
        out_specs=[pl.BlockSpec((SGU_RB, D), lambda i: (i, 0))], out_shape=[_sds((S, D), BF16)],
        args=[proj, lng, lnb, ws, bsb], block_bytes=blocks, phases=phases)


def _sgu_bwd(proj, dpre, lng, lnb, ws, wst, bsb, phases=()):
    nsteps = S // SGU_RB

    def body(p_ref, d_ref, g_ref, b_ref, ws_ref, wst_ref, bs_ref,
             dp_ref, dg_ref, db_ref, dws_ref, dbs_ref, dvn_ref, dvs_acc_ref):
        step = pl.program_id(0)

        @pl.when(step == 0)
        def _():
            dg_ref[...] = jnp.zeros_like(dg_ref)
            db_ref[...] = jnp.zeros_like(db_ref)
            dws_ref[...] = jnp.zeros_like(dws_ref)
            dvs_acc_ref[...] = jnp.zeros_like(dvs_acc_ref)

        mask = _causal_mask()
        maskt = _causal_mask(transposed=True)
        for n in range(SGU_RB // SGU_BLOCK):
            rows = slice(n * SGU_BLOCK, (n + 1) * SGU_BLOCK)
            v = p_ref[rows, D:2 * D]
            xhat, rstd, vn = _layer_norm(_gelu(v), g_ref[...], b_ref[...])
            vn_b = vn.astype(BF16)
            for gi in range(SGU_G):
                cols = slice(gi * SGU_GD, (gi + 1) * SGU_GD)
                w = jnp.where(mask, ws_ref[gi], 0.0).astype(BF16)
                wt = jnp.where(maskt, wst_ref[gi], 0.0).astype(BF16)
                vs = _dot(w, vn_b[:, cols]) + bs_ref[gi]
                u = p_ref[rows, cols]
                gate = p_ref[rows, 2 * D + gi * SGU_GD:2 * D + (gi + 1) * SGU_GD]
                sg = _sigmoid(gate)
                gu = _gelu(u)
                dpre_g = d_ref[rows, cols]
                t = dpre_g * (gate * sg)
                dp_ref[rows, cols] = (t * vs * _gelu_grad(u)).astype(BF16)
                dp_ref[rows, 2 * D + gi * SGU_GD:2 * D + (gi + 1) * SGU_GD] = (
                    dpre_g * gu * vs * (sg * (1.0 + gate * (1.0 - sg)))).astype(BF16)
                dvs = t * gu
                dvs_b = dvs.astype(BF16)
                dvs_acc_ref[:, cols] += dvs
                dws_ref[gi] += _dot(dvs_b, vn_b[:, cols], NT)
                dvn_ref[:, cols] = _dot(wt, dvs_b)
            dvn = dvn_ref[...]
            dg_ref[...] += _colsum(dvn * xhat)
            db_ref[...] += _colsum(dvn)
            dxh = dvn * g_ref[...]
            dvf = rstd * (dxh - jnp.mean(dxh, axis=-1, keepdims=True) - xhat * jnp.mean(dxh * xhat, axis=-1, keepdims=True))
            dp_ref[rows, D:2 * D] = (dvf * _gelu_grad(v)).astype(BF16)

        @pl.when(step == nsteps - 1)
        def _():
            lane = lax.broadcasted_iota(jnp.int32, (SGU_BLOCK, SGU_BLOCK), 1)
            out = jnp.zeros((SGU_BLOCK, SGU_BLOCK), F32)
            for gi in range(SGU_G):
                out = out + jnp.where(lane == gi, jnp.sum(dvs_acc_ref[:, gi * SGU_GD:(gi + 1) * SGU_GD], axis=1, keepdims=True), 0.0)
                dws_ref[gi] = jnp.where(mask, dws_ref[gi], 0.0)
            dbs_ref[...] = out

    blocks = SGU_RB * SGU_COLS * 6 + SGU_RB * D * 4 + SGU_G * SGU_BLOCK * (3 * SGU_BLOCK + SGU_GD) * 4
    const3 = lambda i: (0, 0, 0)
    return _hosted(
        body, name="sgu_bwd", grid=(nsteps,),
        in_specs=[pl.BlockSpec((SGU_RB, SGU_COLS), lambda i: (i, 0)),
                  pl.BlockSpec((SGU_RB, D), lambda i: (i, 0)),
                  pl.BlockSpec((1, D), lambda i: (0, 0)), pl.BlockSpec((1, D), lambda i: (0, 0)),
                  pl.BlockSpec((SGU_G, SGU_BLOCK, SGU_BLOCK), const3),
                  pl.BlockSpec((SGU_G, SGU_BLOCK, SGU_BLOCK), const3),
                  pl.BlockSpec((SGU_G, SGU_BLOCK, SGU_GD), const3)],
        out_specs=[pl.BlockSpec((SGU_RB, SGU_COLS), lambda i: (i, 0)),
                   pl.BlockSpec((1, D), lambda i: (0, 0)), pl.BlockSpec((1, D), lambda i: (0, 0)),
                   pl.BlockSpec((SGU_G, SGU_BLOCK, SGU_BLOCK), const3),
                   pl.BlockSpec((SGU_BLOCK, SGU_BLOCK), lambda i: (0, 0))],
        out_shape=[_sds((S, SGU_COLS), BF16), _sds((1, D), F32), _sds((1, D), F32),
                   _sds((SGU_G, SGU_BLOCK, SGU_BLOCK), F32), _sds((SGU_BLOCK, SGU_BLOCK), F32)],
        args=[proj, dpre, lng, lnb, ws, wst, bsb],
        scratch_shapes=[pltpu.VMEM((SGU_BLOCK, D), F32), pltpu.VMEM((SGU_BLOCK, D), F32)],
        block_bytes=blocks, scratch_bytes=2 * SGU_BLOCK * D * 4, phases=phases)


def _pair_sum(own, a, r0, nr, name, table=None):
    c = own.shape[2]
    tr = 256
    assert r0 % tr == 0 and nr % tr == 0

    def body(own_ref, sib_ref, o_ref):
        o_ref[...] = (own_ref[...].astype(F32) + sib_ref[...].astype(F32)).astype(BF16)

    own_map = ((lambda j, i: (1 + j, r0 // tr + i, 0)) if table is None else
               (lambda j, i, t: (t[1 + j], r0 // tr + i, 0)))
    cpad = -(-c // 128) * 128
    outs, _ = _hosted(
        body, name=name, grid=(3, nr // tr),
        in_specs=[pl.BlockSpec((None, tr, c), own_map),
                  pl.BlockSpec((None, tr, c), lambda j, i, *t: (1 + j, r0 // tr + i, 0))],
        out_specs=[pl.BlockSpec((None, tr, c), lambda j, i, *t: (j, i, 0))], out_shape=[_sds((3, nr, c), BF16)],
        args=[own, a], block_bytes=3 * tr * cpad * 2, table=table)
    return outs[0]


def _adamw_math(w, g, m, v):
    m = ADAM_B1 * m + (1.0 - ADAM_B1) * g
    v = ADAM_B2 * v + (1.0 - ADAM_B2) * (g * g)
    m_hat = m / (1.0 - ADAM_B1 ** ADAM_STEP)
    v_hat = v / (1.0 - ADAM_B2 ** ADAM_STEP)
    delta = -ADAM_LR * (m_hat / (jnp.sqrt(v_hat) + ADAM_EPS) + ADAM_WD * w)
    return delta, m, v


def _sum_adamw(own, a, b, w, m, v, *, name, phases=(), table=None):
    r, c = w.shape
    tr = 256

    def body(own_ref, sib_ref, far_ref, w_ref, m_ref, v_ref, g_ref, d_ref, nm_ref, nv_ref):
        g = own_ref[...].astype(F32) + sib_ref[...].astype(F32)
        for j in range(3):
            g = g + far_ref[j].astype(F32)
        g_ref[...] = g
        d_ref[...], nm_ref[...], nv_ref[...] = _adamw_math(w_ref[...], g, m_ref[...], v_ref[...])

    spec = pl.BlockSpec((tr, c), lambda i, *t: (i, 0))
    own_map = (lambda i: (0, i, 0)) if table is None else (lambda i, t: (t[0], i, 0))
    cpad = -(-c // 128) * 128
    return _hosted(
        body, name=name, grid=(r // tr,),
        in_specs=[pl.BlockSpec((None, tr, c), own_map), pl.BlockSpec((None, tr, c), lambda i, *t: (0, i, 0)),
                  pl.BlockSpec((3, tr, c), lambda i, *t: (0, i, 0)), spec, spec, spec],
        out_specs=[spec] * 4, out_shape=[_sds((r, c), F32)] * 4, args=[own, a, b, w, m, v],
        block_bytes=5 * tr * cpad * 2 + 7 * tr * cpad * 4, phases=phases, table=table)


def _sum_parts(parts, name):
    n, r, c = parts.shape

    def body(p_ref, o_ref):
        g = p_ref[0]
        for j in range(1, n):
            g = g + p_ref[j]
        o_ref[...] = g

    outs, _ = _hosted(body, name=name, grid=(1,), in_specs=[pl.BlockSpec((n, r, c), lambda i: (0, 0, 0))],
                      out_specs=[pl.BlockSpec((r, c), lambda i: (0, 0))], out_shape=[_sds((r, c), F32)], args=[parts],
                      block_bytes=(n + 1) * r * c * 4)
    return outs[0]


def _adamw(w, g, m, v, name):
    def body(w_ref, g_ref, m_ref, v_ref, d_ref, nm_ref, nv_ref):
        d_ref[...], nm_ref[...], nv_ref[...] = _adamw_math(w_ref[...], g_ref[...], m_ref[...], v_ref[...])

    spec = pl.BlockSpec(w.shape, lambda i: (0, 0))
    outs, _ = _hosted(body, name=name, grid=(1,), in_specs=[spec] * 4, out_specs=[spec] * 3, out_shape=[_sds(w.shape, F32)] * 3,
                      args=[w, g, m, v], block_bytes=7 * _nbytes(w.shape, F32))
    return outs


def _blocks_to_columns(g):
    n, r, c = g.shape
    return jnp.transpose(g, (1, 0, 2)).reshape(r, n * c)


def _pack(parts):
    return jnp.concatenate([p.reshape(-1) for p in parts]).reshape(-1, 128)


def _unpack(packed, like):
    flat, outs, off = packed.reshape(-1), [], 0
    for p in like:
        outs.append(flat[off:off + p.size].reshape(p.shape))
        off += p.size
    return outs


def kernel(x, norm_pre, norm_post, gla_w_in, gla_w_gate2, gla_b_gate, gla_o_gain, gla_w_out, sgu_w_in, sgu_ln_gain, sgu_ln_bias, sgu_w_spatial, sgu_b_spatial, sgu_w_out, loss_target, m_norm_pre, m_norm_post, m_gla_w_in, m_gla_w_gate2, m_gla_b_gate, m_gla_o_gain, m_gla_w_out, m_sgu_w_in, m_sgu_ln_gain, m_sgu_ln_bias, m_sgu_w_spatial, m_sgu_b_spatial, m_sgu_w_out, v_norm_pre, v_norm_post, v_gla_w_in, v_gla_w_gate2, v_gla_b_gate, v_gla_o_gain, v_gla_w_out, v_sgu_w_in, v_sgu_ln_gain, v_sgu_ln_bias, v_sgu_w_spatial, v_sgu_b_spatial, v_sgu_w_out):
    me = _index_of(*_place())
    x0 = x.reshape(S, D)
    tgt = loss_target.reshape(S, D)
    npre0, npre1 = norm_pre[0:1], norm_pre[1:2]
    npost0, npost1 = norm_post[0:1], norm_post[1:2]
    ws = sgu_w_spatial[0]
    wst = jnp.transpose(ws, (0, 2, 1))
    bsb = jnp.broadcast_to(sgu_b_spatial[0][:, :, None], (SGU_G, SGU_BLOCK, SGU_GD))
    W_ROWS = D // N_DEV
    IN_COLS_G, IN_COLS_S = GLA_COLS // N_DEV, SGU_COLS // N_DEV

    s_gwi, s_gwo = gla_w_in[0].astype(BF16), gla_w_out[0].astype(BF16)
    s_swi, s_swo = sgu_w_in[0].astype(BF16), sgu_w_out[0].astype(BF16)
    small = jnp.concatenate([jnp.pad(gla_w_gate2[0].reshape(4, 512), ((0, 4), (0, 0))),
                             jnp.pad(jnp.concatenate([sgu_ln_gain, sgu_ln_bias], axis=1), ((0, 7), (0, 0)))], axis=0)

    wg_in, g_small = _gather_first(s_gwi, small, "gather_first")
    w2 =_blocks_to_columns(g_small[:, :4, :].reshape(N_DEV, LR, 128))
    w2p = jnp.pad(w2, ((0, LRP - LR), (0, 0))).astype(BF16)
    lng = g_small[:, 8, :256].reshape(1, D)
    lnb = g_small[:, 8, 256:].reshape(1, D)
    like_gwo, like_swi = _sds((N_DEV, W_ROWS, D), BF16), _sds((N_DEV, D, IN_COLS_S), BF16)

    h0 = _prenorm(x0, npre0)
    proj0, (g_gwo, g_swi) = _mm(h0, wg_in, "nn", F32, tm=1024, tn=896, tk=D, name="gla_in", phases=[
        _Phase(like_gwo, None, [_gather_send(s_gwo, 0, W_ROWS)]),
        _Phase(like_swi, None, [_gather_send(s_swi, 0, 256)])])
    (ypre0, states), (g_gwo, g_swi) = _gla_fwd(proj0, w2p, gla_b_gate, gla_o_gain, phases=[
        _Phase(like_gwo, g_gwo, [_gather_pass(0, W_ROWS)]),
        _Phase(like_swi, g_swi, [_gather_send(s_swi, 256, 1280), _gather_pass(0, 256)])])
    wg_out = g_gwo.reshape(D, D)
    y0, (g_swi,) = _mm(ypre0, wg_out, "nn", F32, tm=1024, tn=1024, tk=D, name="gla_out", phases=[
        _Phase(like_swi, g_swi, [_gather_send(s_swi, 1536, 512), _gather_pass(256, 1280)])])
    (x1, h1), (g_swi,) = _mid_fwd(x0, y0, npost0, npre1, phases=[_Phase(like_swi, g_swi, [_gather_pass(1536, 512)])])
    proj1, (g_swo,) = _mm(h1, g_swi, "nn", F32, tm=1024, tn=IN_COLS_S, tk=D, name="sgu_in", b_blocked=True, phases=[
        _Phase(like_gwo, None, [_gather_send(s_swo, 0, W_ROWS)])])
    (pre1,), (g_swo,) = _sgu_fwd(proj1, lng, lnb, ws, bsb, phases=[_Phase(like_gwo, g_swo, [_gather_pass(0, W_ROWS)])])
    ws_out = g_swo.reshape(D, D)
    y1, _ = _mm(pre1, ws_out, "nn", F32, tm=1024, tn=1024, tk=D, name="sgu_out")
    loss_cols, dx2, dy1, dnpost1 = _final(x1, y1, tgt, npost1)
    loss = lax.psum(0.5 * jnp.sum(loss_cols) / D, ("x", "y", "c"))

    like_b_out, like_b_swi = _sds((3, W_ROWS, D), BF16), _sds((3, D, IN_COLS_S), BF16)
    like_b_gwi = _sds((3, D, IN_COLS_G), BF16)
    row_pair = dict(like=_sds((4, W_ROWS, D), BF16), block=lambda i, j: i, ordinal=lambda i, j: i >> 1,
                    dst=lambda ref, k, i, j: ref.at[k])
    col_pair = dict(like=_sds((4, D, IN_COLS_S), BF16), block=lambda i, j: j, ordinal=lambda i, j: 4 * i + (j >> 1),
                    dst=lambda ref, k, i, j: ref.at[k, pl.ds(pl.multiple_of(i * 1024, 1024), 1024)])

    mine = _own_table()
    dws_out, (a_swo,) = _mm(pre1, dy1, "tn", BF16, tm=W_ROWS, tn=D, tk=S, name="sgu_out_dw", pair=row_pair)
    p_swo = dws_out.reshape(N_DEV, W_ROWS, D)
    t_swo = _pair_sum(p_swo, a_swo, 0, W_ROWS, "pair_sum_sgu_w_out", table=mine)
    dpre1, _ = _mm(dy1, ws_out, "nt", F32, tm=1024, tn=1024, tk=D, name="sgu_out_dx")
    (dproj1, dlng, dlnb, dwsp, dbsp), (b_swo,) = _sgu_bwd(proj1, dpre1, lng, lnb, ws, wst, bsb, phases=[
        _Phase(like_b_out, None, [_reduce_cross(t_swo, 0, 0, W_ROWS)])])
    p_swi, (a_swi,) = _mm(h1, dproj1, "tn", BF16, tm=1024, tn=IN_COLS_S, tk=S, name="sgu_in_dw", out_blocked=True, pair=col_pair)
    t_swi = _pair_sum(p_swi, a_swi, 0, D, "pair_sum_sgu_w_in", table=mine)
    dh1, (b_swi,) = _mm(dproj1, g_swi, "nt", F32, tm=1024, tn=1024, tk=IN_COLS_S, name="sgu_in_dx", b_blocked=True, phases=[
        _Phase(like_b_swi, None, [_reduce_cross(t_swi, 0, 0, 1024)])])
    (dx1, dy0, dnpre1, dnpost0), (b_swi,) = _mid_bwd(dx2, dh1, x1, y0, npre1, npost0, phases=[
        _Phase(like_b_swi, b_swi, [_reduce_cross(t_swi, 1024, 1024, 512)])])
    dwg_out, (a_gwo, b_swi) = _mm(ypre0, dy0, "tn", BF16, tm=W_ROWS, tn=D, tk=S, name="gla_out_dw", pair=row_pair, phases=[
        _Phase(like_b_swi, b_swi, [_reduce_cross(t_swi, 1536, 1536, 512)])])
    p_gwo = dwg_out.reshape(N_DEV, W_ROWS, D)
    t_gwo = _pair_sum(p_gwo, a_gwo, 0, W_ROWS, "pair_sum_gla_w_out", table=mine)
    dypre0, _ = _mm(dy0, wg_out, "nt", F32, tm=1024, tn=1024, tk=D, name="gla_out_dx")
    late = [dnpre1, dnpost1, dlng, dlnb, dwsp, jnp.transpose(dbsp[:, :SGU_G])]
    late_pack = _pack(late)
    (dproj0, dogain, dbgate, dw2), (b_gwo, g_late) = _gla_bwd(proj0, dypre0, states, w2p, gla_b_gate, gla_o_gain, phases=[
        _Phase(like_b_out, None, [_reduce_cross(t_gwo, 0, 0, W_ROWS)]),
        _Phase(_sds((N_DEV,) + late_pack.shape, F32), None, [_gather_send(late_pack, 0, late_pack.shape[0])])])
    half = D // 2
    dwg_in_a, (g_late,) = _mm(h0, dproj0, "tn", BF16, tm=half, tn=896, tk=S, name="gla_in_dw_a", m_tiles=(0, 1), phases=[
        _Phase(_sds((N_DEV,) + late_pack.shape, F32), g_late, [_gather_pass(0, late_pack.shape[0])])])
    own_gwi, a_gwi = _blockify_pair(dwg_in_a, None, None, 0, "blockify_gla_w_in_a")
    t_gwi_a = _pair_sum(own_gwi, a_gwi, 0, half, "pair_sum_gla_w_in_a")
    dwg_in_b, (b_gwi,) = _mm(h0, dproj0, "tn", BF16, tm=half, tn=896, tk=S, name="gla_in_dw_b", m_tiles=(1, 1), phases=[
        _Phase(like_b_gwi, None, [_reduce_cross(t_gwi_a, 0, 0, 512)])])
    own_gwi, a_gwi = _blockify_pair(dwg_in_b, own_gwi, a_gwi, half, "blockify_gla_w_in_b")
    t_gwi_b = _pair_sum(own_gwi, a_gwi, half, half, "pair_sum_gla_w_in_b")
    dh0, (b_gwi,) = _mm(dproj0, wg_in, "nt", F32, tm=1024, tn=1024, tk=896, name="gla_in_dx", phases=[
        _Phase(like_b_gwi, b_gwi, [_reduce_cross(t_gwi_a, 512, 512, 512), _reduce_cross(t_gwi_b, 0, half, 512)])])
    (grad_x, dnpre0), (b_gwi,) = _first_bwd(dx1, dh0, x0, npre0, phases=[
        _Phase(like_b_gwi, b_gwi, [_reduce_cross(t_gwi_b, 512, half + 512, 256)])])

    early = [dnpre0, dnpost0, dbgate, dogain, dw2[:LR]]
    early_pack = _pack(early)
    like_early = _sds((N_DEV,) + early_pack.shape, F32)
    (g_swo, d_swo, nm_swo, nv_swo), (b_gwi, g_early) = _sum_adamw(
        p_swo, a_swo, b_swo, sgu_w_out[0], m_sgu_w_out[0], v_sgu_w_out[0], name="adamw_sgu_w_out", table=mine, phases=[
            _Phase(like_b_gwi, b_gwi, [_reduce_cross(t_gwi_b, 768, half + 768, 256)]),
            _Phase(like_early, None, [_gather_send(early_pack, 0, early_pack.shape[0])])])
    (g_swi_, d_swi, nm_swi, nv_swi), (g_early,) = _sum_adamw(
        p_swi, a_swi, b_swi, sgu_w_in[0], m_sgu_w_in[0], v_sgu_w_in[0], name="adamw_sgu_w_in", table=mine, phases=[
            _Phase(like_early, g_early, [_gather_pass(0, early_pack.shape[0])])])
    (g_gwo_, d_gwo, nm_gwo, nv_gwo), _ = _sum_adamw(
        p_gwo, a_gwo, b_gwo, gla_w_out[0], m_gla_w_out[0], v_gla_w_out[0], name="adamw_gla_w_out", table=mine)
    (g_gwi_, d_gwi, nm_gwi, nv_gwi), _ = _sum_adamw(
        own_gwi, a_gwi, b_gwi, gla_w_in[0], m_gla_w_in[0], v_gla_w_in[0], name="adamw_gla_w_in")

    g_npre1, g_npost1, g_lng_full, g_lnb_full, g_wsp, g_bsp = _unpack(_sum_parts(g_late, "sum_late_small_grads"), late)
    g_npre0, g_npost0, g_bgate, g_ogain, g_w2_full = _unpack(_sum_parts(g_early, "sum_early_small_grads"), early)
    g_w2 = lax.dynamic_slice(g_w2_full, (0, me * 128), (LR, 128))
    g_lng = lax.dynamic_slice(g_lng_full, (0, me * 256), (1, 256))
    g_lnb = lax.dynamic_slice(g_lnb_full, (0, me * 256), (1, 256))
    small_g = [jnp.concatenate([g_npre0, g_npre1], 0), jnp.concatenate([g_npost0, g_npost1], 0), g_w2, g_bgate, g_ogain,
               g_lng, g_lnb, g_wsp, g_bsp]
    small_w = [norm_pre, norm_post, gla_w_gate2[0], gla_b_gate, gla_o_gain, sgu_ln_gain, sgu_ln_bias, sgu_w_spatial[0], sgu_b_spatial[0]]
    small_m = [m_norm_pre, m_norm_post, m_gla_w_gate2[0], m_gla_b_gate, m_gla_o_gain, m_sgu_ln_gain, m_sgu_ln_bias, m_sgu_w_spatial[0], m_sgu_b_spatial[0]]
    small_v = [v_norm_pre, v_norm_post, v_gla_w_gate2[0], v_gla_b_gate, v_gla_o_gain, v_sgu_ln_gain, v_sgu_ln_bias, v_sgu_w_spatial[0], v_sgu_b_spatial[0]]
    d_pack, nm_pack, nv_pack = _adamw(_pack(small_w), _pack(small_g), _pack(small_m), _pack(small_v), "adamw_small")

    out_like = [norm_pre, norm_post, gla_w_gate2, gla_b_gate, gla_o_gain, sgu_ln_gain, sgu_ln_bias, sgu_w_spatial, sgu_b_spatial]
    sg_ = [g.reshape(s.shape) for g, s in zip(small_g, out_like)]
    sd_, sm_, sv_ = (_unpack(pk, out_like) for pk in (d_pack, nm_pack, nv_pack))

    def assemble(small_list, w_in_g, w_out_g, w_in_s, w_out_s):
        npre_, npost_, w2_, bg_, og_, lg_, lb_, wsp_, bsp_ = small_list
        return [npre_, npost_, w_in_g[None], w2_, bg_, og_, w_out_g[None], w_in_s[None], lg_, lb_, wsp_, bsp_, w_out_s[None]]

    return (loss, grad_x.reshape(1, S, D),
            *assemble(sg_, g_gwi_, g_gwo_, g_swi_, g_swo),
            *assemble(sd_, d_gwi, d_gwo, d_swi, d_swo),
            *assemble(sm_, nm_gwi, nm_gwo, nm_swi, nm_swo),
            *assemble(sv_, nv_gwi, nv_gwo, nv_swi, nv_swo))
```

```python
import functools

import jax
import jax.numpy as jnp
from jax import lax
from jax.experimental import pallas as pl
from jax.experimental.pallas import tpu as pltpu

F32 = jnp.float32
BF16 = jnp.bfloat16

N_DEV = 8
S = 2048
D = 2048
H = 4
DK = 256
DV = 512
C = 64
NC = S // C
GLA_COLS = 6160
GLA_PAD = 6272
Q0, K0, V0, G0, LR0 = 0, 1024, 2048, 4096, 6144
LR = 16
LRP = 128
SGU_COLS = 6144
SGU_BLOCK = 128
SGU_G = 8
SGU_GD = 256
EPS = 1e-6
GLA_TAU = 16.0

ADAM_LR, ADAM_B1, ADAM_B2, ADAM_EPS, ADAM_WD, ADAM_STEP = 0.001, 0.9, 0.999, 1e-08, 0.01, 10

V7X_VMEM_BYTES = 64 * 1024 * 1024
VMEM_CEILING = V7X_VMEM_BYTES - 6 * 1024 * 1024
MESH = pl.DeviceIdType.MESH
HIGHEST = lax.Precision.HIGHEST
HBM_SPEC = pl.BlockSpec(memory_space=pl.ANY)


def _sds(shape, dtype):
    return jax.ShapeDtypeStruct(tuple(shape), dtype)


def _nbytes(shape, dtype):
    n = 1
    for s in shape:
        n *= s
    return n * jnp.dtype(dtype).itemsize


def _dot(a, b, dims=(((1,), (0,)), ((), ())), precision=None):
    return lax.dot_general(a, b, dims, precision=precision, preferred_element_type=F32)


NN = (((1,), (0,)), ((), ()))
TN = (((0,), (0,)), ((), ()))
NT = (((1,), (1,)), ((), ()))


def _place():
    return lax.axis_index("x"), lax.axis_index("y"), lax.axis_index("c")


def _index_of(px, py, pc):
    return 4 * px + 2 * py + pc


def _chips(x, y):
    return [(1 - x, y), (x, 1 - y), (1 - x, 1 - y)]


def _rcopy(src, dst, send_sem, recv_sem, to):
    return pltpu.make_async_remote_copy(src_ref=src, dst_ref=dst, send_sem=send_sem, recv_sem=recv_sem,
                                        device_id=to, device_id_type=MESH)


class _Move:
    def __init__(self, ins, n_remote, n_local, make):
        self.ins, self.n_remote, self.n_local, self.make = list(ins), n_remote, n_local, make

    def start(self, in_refs, buf, sems):
        sends, _, local = self.make(in_refs, buf, *sems)
        for cp in local + sends:
            cp.start()

    def finish(self, in_refs, buf, sems):
        sends, arrivals, local = self.make(in_refs, buf, *sems)
        for cp in arrivals:
            cp.wait_recv()
        for cp in sends:
            cp.wait_send()
        for cp in local:
            cp.wait()


class _Phase:
    def __init__(self, like, so_far, moves):
        self.like, self.so_far, self.moves = like, so_far, list(moves)


def _gather_send(shard, r0, nr):
    def make(in_refs, g, ss, rs, ls):
        sh, = in_refs
        x, y, c = _place()
        me = _index_of(x, y, c)
        rows = pl.ds(r0, nr)
        peers = [(x, y, 1 - c)] + [(px, py, c) for px, py in _chips(x, y)]
        sends = [_rcopy(sh.at[rows], g.at[me, rows], ss.at[k], rs.at[k], p) for k, p in enumerate(peers)]
        arrivals = [_rcopy(sh.at[rows], g.at[_index_of(*p), rows], ss.at[k], rs.at[k], p) for k, p in enumerate(peers)]
        return sends, arrivals, [pltpu.make_async_copy(sh.at[rows], g.at[me, rows], ls.at[0])]

    return _Move([shard], 4, 1, make)


def _gather_pass(r0, nr):
    def make(in_refs, g, ss, rs, ls):
        x, y, c = _place()
        rows = pl.ds(r0, nr)
        sends = [_rcopy(g.at[_index_of(px, py, c), rows], g.at[_index_of(px, py, c), rows], ss.at[j], rs.at[j], (x, y, 1 - c))
                 for j, (px, py) in enumerate(_chips(x, y))]
        arrivals = [_rcopy(g.at[_index_of(px, py, c), rows], g.at[_index_of(px, py, 1 - c), rows], ss.at[j], rs.at[j], (x, y, 1 - c))
                    for j, (px, py) in enumerate(_chips(x, y))]
        return sends, arrivals, []

    return _Move([], 3, 0, make)


def _own_table():
    x, y, c = _place()
    return jnp.stack([_index_of(px, py, c) for px, py in [(x, y)] + _chips(x, y)]).astype(jnp.int32)


def _blockify_pair(dw, own_so_far, a_so_far, dst_r0, name):
    rows, tr, cw, win = dw.shape[0], 256, GLA_COLS // N_DEV, 896
    n_steps = rows // tr

    def body(*refs):
        x_ref, own_ref, a_ref, stage_ref, send_sems, recv_sem = refs[0], *refs[-5:]
        i = pl.program_id(0)
        x, y, c = _place()

        def send(slot, k):
            dst = a_ref.at[k, pl.ds(pl.multiple_of(dst_r0 + i * tr, tr), tr)]
            return _rcopy(stage_ref.at[slot], dst, send_sems.at[slot], recv_sem.at[0], (x, y, 1 - c))

        for j in range(N_DEV):
            window = x_ref[:, 768 * j:768 * j + win].astype(F32)
            tile = (pltpu.roll(window, win - 2 * j, 1) if j else window)[:, :cw].astype(BF16)
            k = ((j >> 2) ^ x) + 2 * (((j >> 1) & 1) ^ y)

            @pl.when((j & 1) == c)
            def _():
                own_ref[k] = tile

            @pl.when((j & 1) != c)
            def _():
                slot = (j >> 1) & 1
                if j >> 1 >= 2:
                    send(slot, k).wait_send()
                else:
                    pl.when(i > 0)(lambda: send(slot, k).wait_send())
                stage_ref[slot] = tile
                send(slot, k).start()

        @pl.when(i == n_steps - 1)
        def _():
            send(0, 0).wait_send()
            send(1, 0).wait_send()
            arrived = a_ref.at[:, pl.ds(dst_r0, rows)]
            _rcopy(arrived, arrived, send_sems.at[0], recv_sem.at[0], (x, y, 1 - c)).wait_recv()

    continues = a_so_far is not None
    own, a = pl.pallas_call(
        body, grid=(n_steps,),
        in_specs=[pl.BlockSpec((tr, GLA_PAD), lambda i: (i, 0))] + [HBM_SPEC] * (2 * continues),
        out_specs=[pl.BlockSpec((4, tr, cw), lambda i: (0, dst_r0 // tr + i, 0)), HBM_SPEC],
        out_shape=[_sds((4, D, cw), BF16), _sds((4, D, cw), BF16)],
        scratch_shapes=[pltpu.VMEM((2, tr, cw), BF16), pltpu.SemaphoreType.DMA((2,)), pltpu.SemaphoreType.DMA((1,))],
        input_output_aliases={1: 0, 2: 1} if continues else {},
        compiler_params=pltpu.CompilerParams(dimension_semantics=("arbitrary",), vmem_limit_bytes=48 * 1024 * 1024),
        name=name,
    )(*([dw] + [own_so_far, a_so_far] * continues))
    return own, a


def _reduce_cross(sums, src_r0, dst_r0, nr):
    def make(in_refs, b, ss, rs, ls):
        t, = in_refs
        x, y, c = _place()
        src, dst = pl.ds(src_r0, nr), pl.ds(dst_r0, nr)
        sends = [_rcopy(t.at[j, src], b.at[j, dst], ss.at[j], rs.at[j], (px, py, c)) for j, (px, py) in enumerate(_chips(x, y))]
        return sends, sends, []

    return _Move([sums], 3, 0, make)


def _hosted(body, *, name, grid, in_specs, out_specs, out_shape, args, scratch_shapes=(), block_bytes, scratch_bytes=0,
            phases=(), table=None):
    n_in, n_out, n_scr = len(args), len(out_shape), len(scratch_shapes)
    all_args, all_out_shape, sems, aliases, layout = list(args), list(out_shape), [], {}, []
    for j, ph in enumerate(phases):
        counts = []
        for mv in ph.moves:
            all_args += mv.ins
            counts.append(len(mv.ins))
            sems += [pltpu.SemaphoreType.DMA((mv.n_remote,)), pltpu.SemaphoreType.DMA((mv.n_remote,)),
                     pltpu.SemaphoreType.DMA((max(mv.n_local, 1),))]
        if ph.so_far is not None:
            aliases[len(all_args)] = n_out + j
            all_args.append(ph.so_far)
        layout.append((counts, ph.so_far is not None))
        all_out_shape.append(ph.like)
    n_extra_in = len(all_args) - n_in

    def wrapped(*refs):
        ins, pos = refs[:n_in], n_in
        move_ins = []
        for counts, continues in layout:
            per_move = []
            for cnt in counts:
                per_move.append(refs[pos:pos + cnt])
                pos += cnt
            pos += continues
            move_ins.append(per_move)
        outs = refs[pos:pos + n_out]
        bufs = refs[pos + n_out:pos + n_out + len(phases)]
        pos += n_out + len(phases)
        scratch = refs[pos:pos + n_scr]
        pos += n_scr
        move_sems = []
        for ph in phases:
            per_move = []
            for _ in ph.moves:
                per_move.append(refs[pos:pos + 3])
                pos += 3
            move_sems.append(per_move)

        def each_move(fn_name):
            for ph, buf, per_in, per_sem in zip(phases, bufs, move_ins, move_sems):
                for mv, mv_in, mv_sem in zip(ph.moves, per_in, per_sem):
                    getattr(mv, fn_name)(mv_in, buf, mv_sem)

        if phases:
            first = functools.reduce(jnp.logical_and, [pl.program_id(a) == 0 for a in range(len(grid))])
            last = functools.reduce(jnp.logical_and, [pl.program_id(a) == grid[a] - 1 for a in range(len(grid))])
            pl.when(first)(lambda: each_move("start"))
        body(*ins, *outs, *scratch)
        if phases:
            pl.when(last)(lambda: each_move("finish"))

    est = 2 * block_bytes + scratch_bytes
    params = pltpu.CompilerParams(dimension_semantics=("arbitrary",) * len(grid),
                                  vmem_limit_bytes=min(VMEM_CEILING, max(32 * 1024 * 1024, 2 * est)))
    all_in_specs, all_out_specs = list(in_specs) + [HBM_SPEC] * n_extra_in, list(out_specs) + [HBM_SPEC] * len(phases)
    if table is None:
        results = pl.pallas_call(
            wrapped, grid=grid, in_specs=all_in_specs, out_specs=all_out_specs, out_shape=all_out_shape,
            scratch_shapes=list(scratch_shapes) + sems, input_output_aliases=aliases, compiler_params=params, name=name,
        )(*all_args)
    else:
        results = pl.pallas_call(
            lambda table_ref, *refs: wrapped(*refs),
            grid_spec=pltpu.PrefetchScalarGridSpec(num_scalar_prefetch=1, grid=grid, in_specs=all_in_specs, out_specs=all_out_specs,
                                                   scratch_shapes=list(scratch_shapes) + sems),
            out_shape=all_out_shape, input_output_aliases={k + 1: v for k, v in aliases.items()}, compiler_params=params, name=name,
        )(table, *all_args)
    return list(results[:n_out]), list(results[n_out:])


def _gather_first(shard, small, name):
    cw, tr, n_tiles = shard.shape[1], 256, GLA_PAD // 128

    def body(sh_ref, sm_ref, wn_ref, g_ref, gs_ref, wt_ref, win_ref, tmp_ref, send_sems, recv_sems, local_sems):
        x, y, c = _place()
        me, sibling = (x, y, c), (x, y, 1 - c)
        chips = _chips(x, y)

        def copy(base, out_ref, k, block, to, src=None):
            dst = out_ref.at[_index_of(*block)]
            return _rcopy(dst if src is None else src, dst, send_sems.at[base + k], recv_sems.at[base + k], to)

        wcopy = functools.partial(copy, 0, g_ref)
        scopy = functools.partial(copy, 7, gs_ref)

        def load(src_ref, slot):
            cp = pltpu.make_async_copy(src_ref, win_ref.at[slot], local_sems.at[0])
            cp.start()
            cp.wait()

        def place(slot, block):
            b = _index_of(*block)

            def rows_chunk(r, carry):
                rows = pl.ds(pl.multiple_of(r * tr, tr), tr)
                tmp_ref[:, :cw] = win_ref[slot, rows, :].astype(F32)
                shifted = pltpu.roll(tmp_ref[...], 2 * b, 1)
                for u in range(7):
                    wt_ref[6 * b + u, rows, :] = (wt_ref[6 * b + u, rows, :].astype(F32) + shifted[:, 128 * u:128 * (u + 1)]).astype(BF16)
                return carry

            lax.fori_loop(0, D // tr, rows_chunk, 0)

        small_own = pltpu.make_async_copy(sm_ref, gs_ref.at[_index_of(*me)], local_sems.at[1])
        small_own.start()
        first = [wcopy(1 + j, me, (*chip, c), src=sh_ref) for j, chip in enumerate(chips)]
        first += [scopy(0, me, sibling, src=sm_ref)] + [scopy(1 + j, me, (*chip, c), src=sm_ref) for j, chip in enumerate(chips)]
        for cp in first:
            cp.start()

        def clear(t, carry):
            wt_ref[t] = jnp.zeros((D, 128), BF16)
            return carry

        lax.fori_loop(0, n_tiles, clear, 0)
        tmp_ref[...] = jnp.zeros_like(tmp_ref)

        def emit(t):
            pltpu.make_async_copy(wt_ref.at[t], wn_ref.at[t], local_sems.at[2]).start()

        def take(block, slot, arrival=None, pass_on=None):
            if arrival is not None:
                arrival.wait_recv()
            load(sh_ref if arrival is None else g_ref.at[_index_of(*block)], slot)
            if pass_on is not None:
                pass_on.start()
            place(slot, block)
            for u in range(1, 6):
                emit(6 * _index_of(*block) + u)

        to_sibling = wcopy(0, me, sibling, src=win_ref.at[0])
        pass_x = wcopy(4, (*chips[0], c), sibling, src=win_ref.at[1])
        pass_y = wcopy(5, (*chips[1], c), sibling, src=win_ref.at[0])
        pass_d = wcopy(6, (*chips[2], c), sibling, src=win_ref.at[0])
        take(me, 0, pass_on=to_sibling)
        take((*chips[0], c), 1, wcopy(1, (*chips[0], c), me), pass_x)
        to_sibling.wait_send()
        take((*chips[1], c), 0, wcopy(2, (*chips[1], c), me), pass_y)
        small_passed = []
        for j, chip in enumerate(chips):
            scopy(1 + j, (*chip, c), me).wait_recv()
            cp = scopy(4 + j, (*chip, c), sibling)
            cp.start()
            small_passed.append(cp)
        pass_x.wait_send()
        take(sibling, 1, wcopy(0, sibling, me))
        pass_y.wait_send()
        take((*chips[0], 1 - c), 0, wcopy(4, (*chips[0], 1 - c), me))
        take((*chips[1], 1 - c), 1, wcopy(5, (*chips[1], 1 - c), me))
        take((*chips[2], c), 0, wcopy(3, (*chips[2], c), me), pass_d)
        take((*chips[2], 1 - c), 1, wcopy(6, (*chips[2], 1 - c), me))
        for t in range(0, n_tiles, 6):
            emit(t)
        scopy(0, sibling, me).wait_recv()
        for j, chip in enumerate(chips):
            scopy(4 + j, (*chip, 1 - c), me).wait_recv()
        for cp in first + small_passed + [pass_d]:
            cp.wait_send()
        small_own.wait()
        pltpu.make_async_copy(wn_ref, wn_ref, local_sems.at[2]).wait()

    wn, _, gs = pl.pallas_call(
        body,
        in_specs=[HBM_SPEC] * 2, out_specs=[HBM_SPEC] * 3,
        out_shape=[_sds((n_tiles, D, 128), BF16), _sds((N_DEV,) + shard.shape, BF16), _sds((N_DEV,) + small.shape, small.dtype)],
        scratch_shapes=[pltpu.VMEM((n_tiles, D, 128), BF16), pltpu.VMEM((2, D, cw), BF16), pltpu.VMEM((tr, 7 * 128), F32),
                        pltpu.SemaphoreType.DMA((14,)), pltpu.SemaphoreType.DMA((14,)), pltpu.SemaphoreType.DMA((3,))],
        compiler_params=pltpu.CompilerParams(vmem_limit_bytes=48 * 1024 * 1024),
        name=name,
    )(shard, small)
    return wn, gs


def _mm(a, b, mode, out_dtype, *, tm, tn, tk, name, b_blocked=False, b_tiled=False, out_blocked=False, m_tiles=None, pair=None,
        phases=()):
    if mode == "nn":
        (m, k), dims = a.shape, NN
        a_blk, a_map = (tm, tk), (lambda i, j, kk: (i, kk))
        if b_blocked:
            assert b.shape[1] == k and b.shape[2] == tn and tk == k
            n = b.shape[0] * tn
            b_spec = pl.BlockSpec((None, tk, tn), lambda i, j, kk: (j, kk, 0))
        elif b_tiled:
            assert b.shape[1] == k and b.shape[2] == 128 and tn % 128 == 0
            n = b.shape[0] * 128
            b_spec = pl.BlockSpec((tn // 128, tk, 128), lambda i, j, kk: (j, kk, 0))
        else:
            assert b.shape[0] == k
            n = b.shape[1]
            b_spec = pl.BlockSpec((tk, tn), lambda i, j, kk: (kk, j))
    elif mode == "tn":
        (k, m), n, dims = a.shape, b.shape[1], TN
        assert b.shape[0] == k
        first = 0 if m_tiles is None else m_tiles[0]
        a_blk, a_map = (tk, tm), (lambda i, j, kk: (kk, i + first))
        b_spec = pl.BlockSpec((tk, tn), lambda i, j, kk: (kk, j))
    else:
        (m, k), dims = a.shape, NT
        a_blk, a_map = (tm, tk), (lambda i, j, kk: (i, kk))
        if b_blocked:
            assert b.shape[0] * b.shape[2] == k and b.shape[2] == tk
            n = b.shape[1]
            b_spec = pl.BlockSpec((None, tn, tk), lambda i, j, kk: (kk, j, 0))
        elif b_tiled:
            assert b.shape[0] * 128 == k and b.shape[2] == 128 and tk % 128 == 0
            n = b.shape[1]
            b_spec = pl.BlockSpec((tk // 128, tn, 128), lambda i, j, kk: (kk, j, 0))
        else:
            assert b.shape[1] == k
            n = b.shape[0]
            b_spec = pl.BlockSpec((tn, tk), lambda i, j, kk: (j, kk))
    assert m % tm == 0 and n % tn == 0 and k % tk == 0, (a.shape, b.shape, mode)
    nk = k // tk
    n_row_tiles = m // tm if m_tiles is None else m_tiles[1]
    if out_blocked:
        out_shape, out_spec = _sds((n // tn, n_row_tiles * tm, tn), out_dtype), pl.BlockSpec((None, tm, tn), lambda i, j, kk: (j, i, 0))
    else:
        out_shape, out_spec = _sds((n_row_tiles * tm, n), out_dtype), pl.BlockSpec((tm, tn), lambda i, j, kk: (i, j))

    grid = (n_row_tiles, n // tn, nk)

    def body(a_ref, b_ref, o_ref, *rest):
        rhs = jnp.concatenate([b_ref[u] for u in range(b_ref.shape[0])], axis=1) if b_tiled else b_ref[...]
        p = _dot(a_ref[...], rhs, dims)
        if nk == 1:
            o_ref[...] = p.astype(out_dtype)
            if pair is not None:
                _send_to_sibling(p.astype(out_dtype), *rest)
        else:
            acc_ref, = rest
            kk = pl.program_id(2)

            @pl.when(kk == 0)
            def _():
                acc_ref[...] = p

            @pl.when(kk > 0)
            def _():
                acc_ref[...] += p

            @pl.when(kk == nk - 1)
            def _():
                o_ref[...] = acc_ref[...].astype(out_dtype)

    def _send_to_sibling(tile, pair_ref, stage_ref, send_sems, recv_sem):
        i, j = pl.program_id(0), pl.program_id(1)
        x, y, c = _place()
        blk = pair["block"](i, j)
        k = ((blk >> 2) ^ x) + 2 * (((blk >> 1) & 1) ^ y)
        ordinal = pair["ordinal"](i, j)

        def send(slot):
            return _rcopy(stage_ref.at[slot], pair["dst"](pair_ref, k, i, j), send_sems.at[slot], recv_sem.at[0], (x, y, 1 - c))

        @pl.when((blk & 1) != c)
        def _():
            slot = ordinal & 1

            @pl.when(ordinal >= 2)
            def _():
                send(slot).wait_send()

            stage_ref[slot] = tile
            send(slot).start()

        @pl.when((i == grid[0] - 1) & (j == grid[1] - 1))
        def _():
            send(0).wait_send()
            send(1).wait_send()
            _rcopy(pair_ref, pair_ref, send_sems.at[0], recv_sem.at[0], (x, y, 1 - c)).wait_recv()

    blocks = _nbytes(a_blk, a.dtype) + tk * tn * jnp.dtype(b.dtype).itemsize + _nbytes((tm, tn), out_dtype)
    out_specs, out_shapes, scratch = [out_spec], [out_shape], [] if nk == 1 else [pltpu.VMEM((tm, tn), F32)]
    scratch_bytes = _nbytes((tm, tn), F32) * (nk > 1)
    if pair is not None:
        assert nk == 1
        out_specs, out_shapes = out_specs + [HBM_SPEC], out_shapes + [pair["like"]]
        scratch = [pltpu.VMEM((2, tm, tn), out_dtype), pltpu.SemaphoreType.DMA((2,)), pltpu.SemaphoreType.DMA((1,))]
        scratch_bytes = 2 * _nbytes((tm, tn), out_dtype)
    outs, bufs = _hosted(
        body, name=name, grid=grid,
        in_specs=[pl.BlockSpec(a_blk, a_map), b_spec], out_specs=out_specs, out_shape=out_shapes, args=[a, b],
        scratch_shapes=scratch, block_bytes=blocks, scratch_bytes=scratch_bytes, phases=phases)
    return outs[0], outs[1:] + bufs


RB = 256


def _row_spec(width):
    return pl.BlockSpec((RB, width), lambda i: (i, 0))


def _vec_spec(width):
    return pl.BlockSpec((1, width), lambda i: (0, 0))


def _rinv(x):
    return lax.rsqrt(jnp.mean(x * x, axis=-1, keepdims=True) + EPS)


def _norm_bwd(dyn, xhat, r):
    return r * (dyn - xhat * jnp.mean(dyn * xhat, axis=-1, keepdims=True))


def _colsum(x):
    return jnp.sum(x, axis=0, keepdims=True)


def _accumulate(ref, value):
    @pl.when(pl.program_id(0) == 0)
    def _():
        ref[...] = value

    @pl.when(pl.program_id(0) > 0)
    def _():
        ref[...] += value


def _prenorm(x, gain):
    def body(x_ref, g_ref, h_ref):
        xv = x_ref[...]
        h_ref[...] = (xv * _rinv(xv) * g_ref[...]).astype(BF16)

    outs, _ = _hosted(body, name="prenorm", grid=(S // RB,), in_specs=[_row_spec(D), _vec_spec(D)], out_specs=[_row_spec(D)],
                      out_shape=[_sds((S, D), BF16)], args=[x, gain], block_bytes=RB * D * 6)
    return outs[0]


def _mid_fwd(x, y, npost, npre, phases=()):
    def body(x_ref, y_ref, po_ref, pr_ref, x1_ref, h1_ref):
        yv = y_ref[...]
        x1 = x_ref[...] + yv * _rinv(yv) * po_ref[...]
        x1_ref[...] = x1
        h1_ref[...] = (x1 * _rinv(x1) * pr_ref[...]).astype(BF16)

    return _hosted(body, name="mid_fwd", grid=(S // RB,), in_specs=[_row_spec(D), _row_spec(D), _vec_spec(D), _vec_spec(D)],
                   out_specs=[_row_spec(D), _row_spec(D)], out_shape=[_sds((S, D), F32), _sds((S, D), BF16)],
                   args=[x, y, npost, npre], block_bytes=RB * D * 14, phases=phases)


def _final(x1, y1, tgt, npost):
    def body(x_ref, y_ref, t_ref, po_ref, loss_ref, dx_ref, dy_ref, dpo_ref):
        yv = y_ref[...]
        r = _rinv(yv)
        yhat = yv * r
        err = x_ref[...] + yhat * po_ref[...] - t_ref[...]
        dx = err * (1.0 / D)
        dx_ref[...] = dx
        dy_ref[...] = _norm_bwd(dx * po_ref[...], yhat, r).astype(BF16)
        _accumulate(loss_ref, _colsum(err * err))
        _accumulate(dpo_ref, _colsum(dx * yhat))

    outs, _ = _hosted(body, name="final", grid=(S // RB,), in_specs=[_row_spec(D), _row_spec(D), _row_spec(D), _vec_spec(D)],
                      out_specs=[_vec_spec(D), _row_spec(D), _row_spec(D), _vec_spec(D)],
                      out_shape=[_sds((1, D), F32), _sds((S, D), F32), _sds((S, D), BF16), _sds((1, D), F32)],
                      args=[x1, y1, tgt, npost], block_bytes=RB * D * 18)
    return outs


def _mid_bwd(dx2, dh1, x1, y0, npre, npost, phases=()):
    def body(dx2_ref, dh_ref, x_ref, y_ref, pr_ref, po_ref, dx1_ref, dy_ref, dpr_ref, dpo_ref):
        xv = x_ref[...]
        r = _rinv(xv)
        xhat = xv * r
        dh = dh_ref[...]
        dx1 = dx2_ref[...] + _norm_bwd(dh * pr_ref[...], xhat, r)
        dx1_ref[...] = dx1
        yv = y_ref[...]
        ry = _rinv(yv)
        yhat = yv * ry
        dy_ref[...] = _norm_bwd(dx1 * po_ref[...], yhat, ry).astype(BF16)
        _accumulate(dpr_ref, _colsum(dh * xhat))
        _accumulate(dpo_ref, _colsum(dx1 * yhat))

    return _hosted(body, name="mid_bwd", grid=(S // RB,), in_specs=[_row_spec(D)] * 4 + [_vec_spec(D)] * 2,
                   out_specs=[_row_spec(D), _row_spec(D), _vec_spec(D), _vec_spec(D)],
                   out_shape=[_sds((S, D), F32), _sds((S, D), BF16), _sds((1, D), F32), _sds((1, D), F32)],
                   args=[dx2, dh1, x1, y0, npre, npost], block_bytes=RB * D * 22, phases=phases)


def _first_bwd(dx1, dh0, x0, npre, phases=()):
    def body(dx1_ref, dh_ref, x_ref, pr_ref, gx_ref, dpr_ref):
        xv = x_ref[...]
        r = _rinv(xv)
        xhat = xv * r
        dh = dh_ref[...]
        gx_ref[...] = dx1_ref[...] + _norm_bwd(dh * pr_ref[...], xhat, r)
        _accumulate(dpr_ref, _colsum(dh * xhat))

    return _hosted(body, name="first_bwd", grid=(S // RB,), in_specs=[_row_spec(D)] * 3 + [_vec_spec(D)],
                   out_specs=[_row_spec(D), _vec_spec(D)], out_shape=[_sds((S, D), F32), _sds((1, D), F32)],
                   args=[dx1, dh0, x0, npre], block_bytes=RB * D * 16, phases=phases)


GLA_RB = 256
GLA_CPB = GLA_RB // C


def _sigmoid(x):
    return 1.0 / (1.0 + jnp.exp(-x))


def _tri(strict):
    r = lax.broadcasted_iota(jnp.int32, (C, C), 0)
    c = lax.broadcasted_iota(jnp.int32, (C, C), 1)
    return jnp.where(c < r if strict else c <= r, 1.0, 0.0).astype(F32)


def _gla_gate(glr_b, w2_h, b_h, tri):
    z = _dot(glr_b, w2_h) + b_h
    log_a = (jnp.minimum(z, 0.0) - jnp.log(1.0 + jnp.exp(-jnp.abs(z)))) * (1.0 / GLA_TAU)
    bcum = _dot(tri, log_a, precision=HIGHEST)
    b_end = jnp.sum(log_a, axis=0, keepdims=True)
    return z, jnp.exp(b_end - bcum), jnp.exp(b_end)


def _gla_fwd(proj, w2p, bgate, ogain, phases=()):
    def body(p_ref, w2_ref, b_ref, og_ref, y_ref, st_out_ref, st_ref):
        @pl.when(pl.program_id(0) == 0)
        def _():
            st_ref[...] = jnp.zeros_like(st_ref)

        tri = _tri(False)

        def chunk(ci, carry):
            rows = pl.ds(pl.multiple_of(ci * C, C), C)
            glr_b = p_ref[rows, LR0:LR0 + LRP].astype(BF16)
            for h in range(H):
                _, ea, dec = _gla_gate(glr_b, w2_ref[:, h * DK:(h + 1) * DK], b_ref[:, h * DK:(h + 1) * DK], tri)
                k_dec = (p_ref[rows, K0 + h * DK:K0 + (h + 1) * DK] * ea).astype(BF16)
                v_b = p_ref[rows, V0 + h * DV:V0 + (h + 1) * DV].astype(BF16)
                st = st_ref[h] * dec + _dot(v_b, k_dec, TN)
                st_ref[h] = st
                st_b = st.astype(BF16)
                st_out_ref[ci, h] = st_b
                q_b = (p_ref[rows, Q0 + h * DK:Q0 + (h + 1) * DK] * (DK ** -0.5)).astype(BF16)
                o = _dot(q_b, st_b, NT)
                on = o * _rinv(o)
                g = p_ref[rows, G0 + h * DV:G0 + (h + 1) * DV]
                y_ref[rows, h * DV:(h + 1) * DV] = (on * og_ref[:, h * DV:(h + 1) * DV] * (g * _sigmoid(g))).astype(BF16)
            return carry

        lax.fori_loop(0, GLA_CPB, chunk, 0)

    blocks = GLA_RB * GLA_PAD * 4 + GLA_RB * D * 2 + GLA_CPB * H * DV * DK * 2
    return _hosted(
        body, name="gla_fwd", grid=(S // GLA_RB,),
        in_specs=[pl.BlockSpec((GLA_RB, GLA_PAD), lambda i: (i, 0)),
                  pl.BlockSpec((LRP, H * DK), lambda i: (0, 0)),
                  pl.BlockSpec((1, H * DK), lambda i: (0, 0)),
                  pl.BlockSpec((1, H * DV), lambda i: (0, 0))],
        out_specs=[pl.BlockSpec((GLA_RB, H * DV), lambda i: (i, 0)),
                   pl.BlockSpec((GLA_CPB, H, DV, DK), lambda i: (i, 0, 0, 0))],
        out_shape=[_sds((S, H * DV), BF16), _sds((NC, H, DV, DK), BF16)],
        args=[proj, w2p, bgate, ogain], scratch_shapes=[pltpu.VMEM((H, DV, DK), F32)],
        block_bytes=blocks, scratch_bytes=H * DV * DK * 4, phases=phases)


def _gla_bwd(proj, dypre, states, w2p, bgate, ogain, phases=()):
    nb = S // GLA_RB

    def body(p_ref, dy_ref, st_blk_ref, st_prev_ref, w2_ref, b_ref, og_ref,
             dp_ref, dog_ref, dbg_ref, dw2_ref, r_ref):
        step = pl.program_id(0)

        @pl.when(step == 0)
        def _():
            r_ref[...] = jnp.zeros_like(r_ref)
            dog_ref[...] = jnp.zeros_like(dog_ref)
            dbg_ref[...] = jnp.zeros_like(dbg_ref)
            dw2_ref[...] = jnp.zeros_like(dw2_ref)

        tri = _tri(False)
        tri_strict = _tri(True)
        has_prev = jnp.where(step < nb - 1, 1.0, 0.0).astype(F32)

        def chunk(ci, st_prev_of):
            rows = pl.ds(ci * C if isinstance(ci, int) else pl.multiple_of(ci * C, C), C)
            glr_b = p_ref[rows, LR0:LR0 + LRP].astype(BF16)
            dglr = jnp.zeros((C, LRP), F32)
            for h in range(H):
                kcol = slice(h * DK, (h + 1) * DK)
                vcol = slice(h * DV, (h + 1) * DV)
                w2_h = w2_ref[:, kcol]
                z, ea, dec = _gla_gate(glr_b, w2_h, b_ref[:, kcol], tri)
                k_dec = p_ref[rows, K0 + h * DK:K0 + (h + 1) * DK] * ea
                k_dec_b = k_dec.astype(BF16)
                v_b = p_ref[rows, V0 + h * DV:V0 + (h + 1) * DV].astype(BF16)
                q_b = (p_ref[rows, Q0 + h * DK:Q0 + (h + 1) * DK] * (DK ** -0.5)).astype(BF16)
                st_b = st_blk_ref[ci, h]
                o = _dot(q_b, st_b, NT)
                rinv = _rinv(o)
                on = o * rinv
                g = p_ref[rows, G0 + h * DV:G0 + (h + 1) * DV]
                sg = _sigmoid(g)
                og = og_ref[:, vcol]
                dyp = dy_ref[rows, vcol]
                dp_ref[rows, G0 + h * DV:G0 + (h + 1) * DV] = (dyp * (on * og) * (sg * (1.0 + g * (1.0 - sg)))).astype(BF16)
                dpn = dyp * (g * sg)
                dog_ref[:, vcol] += _colsum(dpn * on)
                do_b = _norm_bwd(dpn * og, on, rinv).astype(BF16)
                gt = _dot(do_b, q_b, TN) + r_ref[h]
                gt_b = gt.astype(BF16)
                dp_ref[rows, Q0 + h * DK:Q0 + (h + 1) * DK] = (_dot(do_b, st_b) * (DK ** -0.5)).astype(BF16)
                dkd = _dot(v_b, gt_b)
                dp_ref[rows, V0 + h * DV:V0 + (h + 1) * DV] = _dot(k_dec_b, gt_b, NT).astype(BF16)
                dp_ref[rows, K0 + h * DK:K0 + (h + 1) * DK] = (dkd * ea).astype(BF16)
                ddec = _colsum(gt * st_prev_of(h))
                dla = _dot(tri_strict, dkd * k_dec, precision=HIGHEST) + ddec * dec
                dz = dla * (1.0 / GLA_TAU) * (1.0 - _sigmoid(z))
                dz_b = dz.astype(BF16)
                r_ref[h] = gt * dec
                dbg_ref[:, kcol] += _colsum(dz)
                dw2_ref[:, kcol] += _dot(glr_b, dz_b, TN)
                dglr = dglr + _dot(dz_b, w2_h, NT)
            dp_ref[rows, LR0:LR0 + LRP] = dglr.astype(BF16)

        def later_chunk(t, carry):
            ci = GLA_CPB - 1 - t
            chunk(ci, lambda h: st_blk_ref[ci - 1, h].astype(F32))
            return carry

        lax.fori_loop(0, GLA_CPB - 1, later_chunk, 0)
        chunk(0, lambda h: st_prev_ref[0, h].astype(F32) * has_prev)

    blocks = (GLA_RB * GLA_PAD * 4 + GLA_RB * D * 4 + (GLA_CPB + 1) * H * DV * DK * 2 + GLA_RB * GLA_PAD * 2)
    rev = lambda i: nb - 1 - i
    return _hosted(
        body, name="gla_bwd", grid=(nb,),
        in_specs=[pl.BlockSpec((GLA_RB, GLA_PAD), lambda i: (rev(i), 0)),
                  pl.BlockSpec((GLA_RB, H * DV), lambda i: (rev(i), 0)),
                  pl.BlockSpec((GLA_CPB, H, DV, DK), lambda i: (rev(i), 0, 0, 0)),
                  pl.BlockSpec((1, H, DV, DK), lambda i: (jnp.maximum(rev(i) * GLA_CPB - 1, 0), 0, 0, 0)),
                  pl.BlockSpec((LRP, H * DK), lambda i: (0, 0)),
                  pl.BlockSpec((1, H * DK), lambda i: (0, 0)),
                  pl.BlockSpec((1, H * DV), lambda i: (0, 0))],
        out_specs=[pl.BlockSpec((GLA_RB, GLA_PAD), lambda i: (rev(i), 0)),
                   pl.BlockSpec((1, H * DV), lambda i: (0, 0)),
                   pl.BlockSpec((1, H * DK), lambda i: (0, 0)),
                   pl.BlockSpec((LRP, H * DK), lambda i: (0, 0))],
        out_shape=[_sds((S, GLA_PAD), BF16), _sds((1, H * DV), F32), _sds((1, H * DK), F32), _sds((LRP, H * DK), F32)],
        args=[proj, dypre, states, states, w2p, bgate, ogain], scratch_shapes=[pltpu.VMEM((H, DV, DK), F32)],
        block_bytes=blocks, scratch_bytes=H * DV * DK * 4, phases=phases)


SGU_RB = 256
GELU_C = 0.7978845608028654
GELU_A = 0.044715


def _gelu(x):
    return 0.5 * x * (1.0 + jnp.tanh(GELU_C * (x + GELU_A * x * x * x)))


def _gelu_grad(x):
    t = jnp.tanh(GELU_C * (x + GELU_A * x * x * x))
    return 0.5 * (1.0 + t) + 0.5 * x * (1.0 - t * t) * (GELU_C * (1.0 + 3.0 * GELU_A * x * x))


def _causal_mask(transposed=False):
    i = lax.broadcasted_iota(jnp.int32, (SGU_BLOCK, SGU_BLOCK), 1 if transposed else 0)
    j = lax.broadcasted_iota(jnp.int32, (SGU_BLOCK, SGU_BLOCK), 0 if transposed else 1)
    return (i >= C) | (j < C)


def _layer_norm(vf, gain, bias):
    mu = jnp.mean(vf, axis=-1, keepdims=True)
    cen = vf - mu
    rstd = lax.rsqrt(jnp.mean(cen * cen, axis=-1, keepdims=True) + EPS)
    xhat = cen * rstd
    return xhat, rstd, xhat * gain + bias


def _sgu_fwd(proj, lng, lnb, ws, bsb, phases=()):
    def body(p_ref, g_ref, b_ref, ws_ref, bs_ref, o_ref):
        mask = _causal_mask()
        for n in range(SGU_RB // SGU_BLOCK):
            rows = slice(n * SGU_BLOCK, (n + 1) * SGU_BLOCK)
            _, _, vn = _layer_norm(_gelu(p_ref[rows, D:2 * D]), g_ref[...], b_ref[...])
            vn_b = vn.astype(BF16)
            for gi in range(SGU_G):
                cols = slice(gi * SGU_GD, (gi + 1) * SGU_GD)
                w = jnp.where(mask, ws_ref[gi], 0.0).astype(BF16)
                vs = _dot(w, vn_b[:, cols]) + bs_ref[gi]
                gate = p_ref[rows, 2 * D + gi * SGU_GD:2 * D + (gi + 1) * SGU_GD]
                o_ref[rows, cols] = (_gelu(p_ref[rows, cols]) * vs * (gate * _sigmoid(gate))).astype(BF16)

    blocks = SGU_RB * SGU_COLS * 4 + SGU_RB * D * 2 + SGU_G * SGU_BLOCK * (SGU_BLOCK + SGU_GD) * 4
    return _hosted(
        body, name="sgu_fwd", grid=(S // SGU_RB,),
        in_specs=[pl.BlockSpec((SGU_RB, SGU_COLS), lambda i: (i, 0)),
                  pl.BlockSpec((1, D), lambda i: (0, 0)), pl.BlockSpec((1, D), lambda i: (0, 0)),
                  pl.BlockSpec((SGU_G, SGU_BLOCK, SGU_BLOCK), lambda i: (0, 0, 0)),
                  pl.BlockSpec((SGU_G, SGU_BLOCK, SGU_GD), lambda i: (0, 0, 0))],
        out_specs=[pl.BlockSpec((SGU_RB, D), lambda i: (i, 0))], out_shape=[_sds((S, D), BF16)],
        args=[proj, lng, lnb, ws, bsb], block_bytes=blocks, phases=phases)


def _sgu_bwd(proj, dpre, lng, lnb, ws, wst, bsb, phases=()):
    nsteps = S // SGU_RB

    def body(p_ref, d_ref, g_ref, b_ref, ws_ref, wst_ref, bs_ref,
             dp_ref, dg_ref, db_ref, dws_ref, dbs_ref, dvn_ref, dvs_acc_ref):
        step = pl.program_id(0)

        @pl.when(step == 0)
        def _():
            dg_ref[...] = jnp.zeros_like(dg_ref)
            db_ref[...] = jnp.zeros_like(db_ref)
            dws_ref[...] = jnp.zeros_like(dws_ref)
            dvs_acc_ref[...] = jnp.zeros_like(dvs_acc_ref)

        mask = _causal_mask()
        maskt = _causal_mask(transposed=True)
        for n in range(SGU_RB // SGU_BLOCK):
            rows = slice(n * SGU_BLOCK, (n + 1) * SGU_BLOCK)
            v = p_ref[rows, D:2 * D]
            xhat, rstd, vn = _layer_norm(_gelu(v), g_ref[...], b_ref[...])
            vn_b = vn.astype(BF16)
            for gi in range(SGU_G):
                cols = slice(gi * SGU_GD, (gi + 1) * SGU_GD)
                w = jnp.where(mask, ws_ref[gi], 0.0).astype(BF16)
                wt = jnp.where(maskt, wst_ref[gi], 0.0).astype(BF16)
                vs = _dot(w, vn_b[:, cols]) + bs_ref[gi]
                u = p_ref[rows, cols]
                gate = p_ref[rows, 2 * D + gi * SGU_GD:2 * D + (gi + 1) * SGU_GD]
                sg = _sigmoid(gate)
                gu = _gelu(u)
                dpre_g = d_ref[rows, cols]
                t = dpre_g * (gate * sg)
                dp_ref[rows, cols] = (t * vs * _gelu_grad(u)).astype(BF16)
                dp_ref[rows, 2 * D + gi * SGU_GD:2 * D + (gi + 1) * SGU_GD] = (
                    dpre_g * gu * vs * (sg * (1.0 + gate * (1.0 - sg)))).astype(BF16)
                dvs = t * gu
                dvs_b = dvs.astype(BF16)
                dvs_acc_ref[:, cols] += dvs
                dws_ref[gi] += _dot(dvs_b, vn_b[:, cols], NT)
                dvn_ref[:, cols] = _dot(wt, dvs_b)
            dvn = dvn_ref[...]
            dg_ref[...] += _colsum(dvn * xhat)
            db_ref[...] += _colsum(dvn)
            dxh = dvn * g_ref[...]
            dvf = rstd * (dxh - jnp.mean(dxh, axis=-1, keepdims=True) - xhat * jnp.mean(dxh * xhat, axis=-1, keepdims=True))
            dp_ref[rows, D:2 * D] = (dvf * _gelu_grad(v)).astype(BF16)

        @pl.when(step == nsteps - 1)
        def _():
            lane = lax.broadcasted_iota(jnp.int32, (SGU_BLOCK, SGU_BLOCK), 1)
            out = jnp.zeros((SGU_BLOCK, SGU_BLOCK), F32)
            for gi in range(SGU_G):
                out = out + jnp.where(lane == gi, jnp.sum(dvs_acc_ref[:, gi * SGU_GD:(gi + 1) * SGU_GD], axis=1, keepdims=True), 0.0)
                dws_ref[gi] = jnp.where(mask, dws_ref[gi], 0.0)
            dbs_ref[...] = out

    blocks = SGU_RB * SGU_COLS * 6 + SGU_RB * D * 4 + SGU_G * SGU_BLOCK * (3 * SGU_BLOCK + SGU_GD) * 4
    const3 = lambda i: (0, 0, 0)
    return _hosted(
        body, name="sgu_bwd", grid=(nsteps,),
        in_specs=[pl.BlockSpec((SGU_RB, SGU_COLS), lambda i: (i, 0)),
                  pl.BlockSpec((SGU_RB, D), lambda i: (i, 0)),
                  pl.BlockSpec((1, D), lambda i: (0, 0)), pl.BlockSpec((1, D), lambda i: (0, 0)),
                  pl.BlockSpec((SGU_G, SGU_BLOCK, SGU_BLOCK), const3),
                  pl.BlockSpec((SGU_G, SGU_BLOCK, SGU_BLOCK), const3),
                  pl.BlockSpec((SGU_G, SGU_BLOCK, SGU_GD), const3)],
        out_specs=[pl.BlockSpec((SGU_RB, SGU_COLS), lambda i: (i, 0)),
                   pl.BlockSpec((1, D), lambda i: (0, 0)), pl.BlockSpec((1, D), lambda i: (0, 0)),
                   pl.BlockSpec((SGU_G, SGU_BLOCK, SGU_BLOCK), const3),
                   pl.BlockSpec((SGU_BLOCK, SGU_BLOCK), lambda i: (0, 0))],
        out_shape=[_sds((S, SGU_COLS), BF16), _sds((1, D), F32), _sds((1, D), F32),
                   _sds((SGU_G, SGU_BLOCK, SGU_BLOCK), F32), _sds((SGU_BLOCK, SGU_BLOCK), F32)],
        args=[proj, dpre, lng, lnb, ws, wst, bsb],
        scratch_shapes=[pltpu.VMEM((SGU_BLOCK, D), F32), pltpu.VMEM((SGU_BLOCK, D), F32)],
        block_bytes=blocks, scratch_bytes=2 * SGU_BLOCK * D * 4, phases=phases)


def _pair_sum(own, a, r0, nr, name, table=None):
    c = own.shape[2]
    tr = 256
    assert r0 % tr == 0 and nr % tr == 0

    def body(own_ref, sib_ref, o_ref):
        o_ref[...] = (own_ref[...].astype(F32) + sib_ref[...].astype(F32)).astype(BF16)

    own_map = ((lambda j, i: (1 + j, r0 // tr + i, 0)) if table is None else
               (lambda j, i, t: (t[1 + j], r0 // tr + i, 0)))
    cpad = -(-c // 128) * 128
    outs, _ = _hosted(
        body, name=name, grid=(3, nr // tr),
        in_specs=[pl.BlockSpec((None, tr, c), own_map),
                  pl.BlockSpec((None, tr, c), lambda j, i, *t: (1 + j, r0 // tr + i, 0))],
        out_specs=[pl.BlockSpec((None, tr, c), lambda j, i, *t: (j, i, 0))], out_shape=[_sds((3, nr, c), BF16)],
        args=[own, a], block_bytes=3 * tr * cpad * 2, table=table)
    return outs[0]


def _adamw_math(w, g, m, v):
    m = ADAM_B1 * m + (1.0 - ADAM_B1) * g
    v = ADAM_B2 * v + (1.0 - ADAM_B2) * (g * g)
    m_hat = m / (1.0 - ADAM_B1 ** ADAM_STEP)
    v_hat = v / (1.0 - ADAM_B2 ** ADAM_STEP)
    delta = -ADAM_LR * (m_hat / (jnp.sqrt(v_hat) + ADAM_EPS) + ADAM_WD * w)
    return delta, m, v


def _sum_adamw(own, a, b, w, m, v, *, name, phases=(), table=None):
    r, c = w.shape
    tr = 256

    def body(own_ref, sib_ref, far_ref, w_ref, m_ref, v_ref, g_ref, d_ref, nm_ref, nv_ref):
        g = own_ref[...].astype(F32) + sib_ref[...].astype(F32)
        for j in range(3):
            g = g + far_ref[j].astype(F32)
        g_ref[...] = g
        d_ref[...], nm_ref[...], nv_ref[...] = _adamw_math(w_ref[...], g, m_ref[...], v_ref[...])

    spec = pl.BlockSpec((tr, c), lambda i, *t: (i, 0))
    own_map = (lambda i: (0, i, 0)) if table is None else (lambda i, t: (t[0], i, 0))
    cpad = -(-c // 128) * 128
    return _hosted(
        body, name=name, grid=(r // tr,),
        in_specs=[pl.BlockSpec((None, tr, c), own_map), pl.BlockSpec((None, tr, c), lambda i, *t: (0, i, 0)),
                  pl.BlockSpec((3, tr, c), lambda i, *t: (0, i, 0)), spec, spec, spec],
        out_specs=[spec] * 4, out_shape=[_sds((r, c), F32)] * 4, args=[own, a, b, w, m, v],
        block_bytes=5 * tr * cpad * 2 + 7 * tr * cpad * 4, phases=phases, table=table)


def _sum_parts(parts, name):
    n, r, c = parts.shape

    def body(p_ref, o_ref):
        g = p_ref[0]
        for j in range(1, n):
            g = g + p_ref[j]
        o_ref[...] = g

    outs, _ = _hosted(body, name=name, grid=(1,), in_specs=[pl.BlockSpec((n, r, c), lambda i: (0, 0, 0))],
                      out_specs=[pl.BlockSpec((r, c), lambda i: (0, 0))], out_shape=[_sds((r, c), F32)], args=[parts],
                      block_bytes=(n + 1) * r * c * 4)
    return outs[0]


def _adamw(w, g, m, v, name):
    def body(w_ref, g_ref, m_ref, v_ref, d_ref, nm_ref, nv_ref):
        d_ref[...], nm_ref[...], nv_ref[...] = _adamw_math(w_ref[...], g_ref[...], m_ref[...], v_ref[...])

    spec = pl.BlockSpec(w.shape, lambda i: (0, 0))
    outs, _ = _hosted(body, name=name, grid=(1,), in_specs=[spec] * 4, out_specs=[spec] * 3, out_shape=[_sds(w.shape, F32)] * 3,
                      args=[w, g, m, v], block_bytes=7 * _nbytes(w.shape, F32))
    return outs


def _blocks_to_columns(g):
    n, r, c = g.shape
    return jnp.transpose(g, (1, 0, 2)).reshape(r, n * c)


def _pack(parts):
    return jnp.concatenate([p.reshape(-1) for p in parts]).reshape(-1, 128)


def _unpack(packed, like):
    flat, outs, off = packed.reshape(-1), [], 0
    for p in like:
        outs.append(flat[off:off + p.size].reshape(p.shape))
        off += p.size
    return outs


def kernel(x, norm_pre, norm_post, gla_w_in, gla_w_gate2, gla_b_gate, gla_o_gain, gla_w_out, sgu_w_in, sgu_ln_gain, sgu_ln_bias, sgu_w_spatial, sgu_b_spatial, sgu_w_out, loss_target, m_norm_pre, m_norm_post, m_gla_w_in, m_gla_w_gate2, m_gla_b_gate, m_gla_o_gain, m_gla_w_out, m_sgu_w_in, m_sgu_ln_gain, m_sgu_ln_bias, m_sgu_w_spatial, m_sgu_b_spatial, m_sgu_w_out, v_norm_pre, v_norm_post, v_gla_w_in, v_gla_w_gate2, v_gla_b_gate, v_gla_o_gain, v_gla_w_out, v_sgu_w_in, v_sgu_ln_gain, v_sgu_ln_bias, v_sgu_w_spatial, v_sgu_b_spatial, v_sgu_w_out):
    me = _index_of(*_place())
    x0 = x.reshape(S, D)
    tgt = loss_target.reshape(S, D)
    npre0, npre1 = norm_pre[0:1], norm_pre[1:2]
    npost0, npost1 = norm_post[0:1], norm_post[1:2]
    ws = sgu_w_spatial[0]
    wst = jnp.transpose(ws, (0, 2, 1))
    bsb = jnp.broadcast_to(sgu_b_spatial[0][:, :, None], (SGU_G, SGU_BLOCK, SGU_GD))
    W_ROWS = D // N_DEV
    IN_COLS_G, IN_COLS_S = GLA_COLS // N_DEV, SGU_COLS // N_DEV

    s_gwi, s_gwo = gla_w_in[0].astype(BF16), gla_w_out[0].astype(BF16)
    s_swi, s_swo = sgu_w_in[0].astype(BF16), sgu_w_out[0].astype(BF16)
    small = jnp.concatenate([jnp.pad(gla_w_gate2[0].reshape(4, 512), ((0, 4), (0, 0))),
                             jnp.pad(jnp.concatenate([sgu_ln_gain, sgu_ln_bias], axis=1), ((0, 7), (0, 0)))], axis=0)

    wg_in, g_small = _gather_first(s_gwi, small, "gather_first")
    w2 =_blocks_to_columns(g_small[:, :4, :].reshape(N_DEV, LR, 128))
    w2p = jnp.pad(w2, ((0, LRP - LR), (0, 0))).astype(BF16)
    lng = g_small[:, 8, :256].reshape(1, D)
    lnb = g_small[:, 8, 256:].reshape(1, D)
    like_gwo, like_swi = _sds((N_DEV, W_ROWS, D), BF16), _sds((N_DEV, D, IN_COLS_S), BF16)

    h0 = _prenorm(x0, npre0)
    proj0, (g_gwo, g_swi) = _mm(h0, wg_in, "nn", F32, tm=1024, tn=896, tk=D, name="gla_in", b_tiled=True, phases=[
        _Phase(like_gwo, None, [_gather_send(s_gwo, 0, W_ROWS)]),
        _Phase(like_swi, None, [_gather_send(s_swi, 0, 256)])])
    (ypre0, states), (g_gwo, g_swi) = _gla_fwd(proj0, w2p, gla_b_gate, gla_o_gain, phases=[
        _Phase(like_gwo, g_gwo, [_gather_pass(0, W_ROWS)]),
        _Phase(like_swi, g_swi, [_gather_send(s_swi, 256, 1280), _gather_pass(0, 256)])])
    wg_out = g_gwo.reshape(D, D)
    y0, (g_swi,) = _mm(ypre0, wg_out, "nn", F32, tm=1024, tn=1024, tk=D, name="gla_out", phases=[
        _Phase(like_swi, g_swi, [_gather_send(s_swi, 1536, 512), _gather_pass(256, 1280)])])
    (x1, h1), (g_swi,) = _mid_fwd(x0, y0, npost0, npre1, phases=[_Phase(like_swi, g_swi, [_gather_pass(1536, 512)])])
    proj1, (g_swo,) = _mm(h1, g_swi, "nn", F32, tm=1024, tn=IN_COLS_S, tk=D, name="sgu_in", b_blocked=True, phases=[
        _Phase(like_gwo, None, [_gather_send(s_swo, 0, W_ROWS)])])
    (pre1,), (g_swo,) = _sgu_fwd(proj1, lng, lnb, ws, bsb, phases=[_Phase(like_gwo, g_swo, [_gather_pass(0, W_ROWS)])])
    ws_out = g_swo.reshape(D, D)
    y1, _ = _mm(pre1, ws_out, "nn", F32, tm=1024, tn=1024, tk=D, name="sgu_out")
    loss_cols, dx2, dy1, dnpost1 = _final(x1, y1, tgt, npost1)
    loss = lax.psum(0.5 * jnp.sum(loss_cols) / D, ("x", "y", "c"))

    like_b_out, like_b_swi = _sds((3, W_ROWS, D), BF16), _sds((3, D, IN_COLS_S), BF16)
    like_b_gwi = _sds((3, D, IN_COLS_G), BF16)
    row_pair = dict(like=_sds((4, W_ROWS, D), BF16), block=lambda i, j: i, ordinal=lambda i, j: i >> 1,
                    dst=lambda ref, k, i, j: ref.at[k])
    col_pair = dict(like=_sds((4, D, IN_COLS_S), BF16), block=lambda i, j: j, ordinal=lambda i, j: 4 * i + (j >> 1),
                    dst=lambda ref, k, i, j: ref.at[k, pl.ds(pl.multiple_of(i * 1024, 1024), 1024)])

    mine = _own_table()
    dws_out, (a_swo,) = _mm(pre1, dy1, "tn", BF16, tm=W_ROWS, tn=D, tk=S, name="sgu_out_dw", pair=row_pair)
    p_swo = dws_out.reshape(N_DEV, W_ROWS, D)
    t_swo = _pair_sum(p_swo, a_swo, 0, W_ROWS, "pair_sum_sgu_w_out", table=mine)
    dpre1, _ = _mm(dy1, ws_out, "nt", F32, tm=1024, tn=1024, tk=D, name="sgu_out_dx")
    (dproj1, dlng, dlnb, dwsp, dbsp), (b_swo,) = _sgu_bwd(proj1, dpre1, lng, lnb, ws, wst, bsb, phases=[
        _Phase(like_b_out, None, [_reduce_cross(t_swo, 0, 0, W_ROWS)])])
    p_swi, (a_swi,) = _mm(h1, dproj1, "tn", BF16, tm=1024, tn=IN_COLS_S, tk=S, name="sgu_in_dw", out_blocked=True, pair=col_pair)
    t_swi = _pair_sum(p_swi, a_swi, 0, D, "pair_sum_sgu_w_in", table=mine)
    dh1, (b_swi,) = _mm(dproj1, g_swi, "nt", F32, tm=1024, tn=1024, tk=IN_COLS_S, name="sgu_in_dx", b_blocked=True, phases=[
        _Phase(like_b_swi, None, [_reduce_cross(t_swi, 0, 0, 1024)])])
    (dx1, dy0, dnpre1, dnpost0), (b_swi,) = _mid_bwd(dx2, dh1, x1, y0, npre1, npost0, phases=[
        _Phase(like_b_swi, b_swi, [_reduce_cross(t_swi, 1024, 1024, 512)])])
    dwg_out, (a_gwo, b_swi) = _mm(ypre0, dy0, "tn", BF16, tm=W_ROWS, tn=D, tk=S, name="gla_out_dw", pair=row_pair, phases=[
        _Phase(like_b_swi, b_swi, [_reduce_cross(t_swi, 1536, 1536, 512)])])
    p_gwo = dwg_out.reshape(N_DEV, W_ROWS, D)
    t_gwo = _pair_sum(p_gwo, a_gwo, 0, W_ROWS, "pair_sum_gla_w_out", table=mine)
    dypre0, _ = _mm(dy0, wg_out, "nt", F32, tm=1024, tn=1024, tk=D, name="gla_out_dx")
    late = [dnpre1, dnpost1, dlng, dlnb, dwsp, jnp.transpose(dbsp[:, :SGU_G])]
    late_pack = _pack(late)
    (dproj0, dogain, dbgate, dw2), (b_gwo, g_late) = _gla_bwd(proj0, dypre0, states, w2p, gla_b_gate, gla_o_gain, phases=[
        _Phase(like_b_out, None, [_reduce_cross(t_gwo, 0, 0, W_ROWS)]),
        _Phase(_sds((N_DEV,) + late_pack.shape, F32), None, [_gather_send(late_pack, 0, late_pack.shape[0])])])
    half = D // 2
    dwg_in_a, (g_late,) = _mm(h0, dproj0, "tn", BF16, tm=half, tn=896, tk=S, name="gla_in_dw_a", m_tiles=(0, 1), phases=[
        _Phase(_sds((N_DEV,) + late_pack.shape, F32), g_late, [_gather_pass(0, late_pack.shape[0])])])
    own_gwi, a_gwi = _blockify_pair(dwg_in_a, None, None, 0, "blockify_gla_w_in_a")
    t_gwi_a = _pair_sum(own_gwi, a_gwi, 0, half, "pair_sum_gla_w_in_a")
    dwg_in_b, (b_gwi,) = _mm(h0, dproj0, "tn", BF16, tm=half, tn=896, tk=S, name="gla_in_dw_b", m_tiles=(1, 1), phases=[
        _Phase(like_b_gwi, None, [_reduce_cross(t_gwi_a, 0, 0, 512)])])
    own_gwi, a_gwi = _blockify_pair(dwg_in_b, own_gwi, a_gwi, half, "blockify_gla_w_in_b")
    t_gwi_b = _pair_sum(own_gwi, a_gwi, half, half, "pair_sum_gla_w_in_b")
    dh0, (b_gwi,) = _mm(dproj0, wg_in, "nt", F32, tm=1024, tn=1024, tk=896, name="gla_in_dx", b_tiled=True, phases=[
        _Phase(like_b_gwi, b_gwi, [_reduce_cross(t_gwi_a, 512, 512, 512), _reduce_cross(t_gwi_b, 0, half, 512)])])
    (grad_x, dnpre0), (b_gwi,) = _first_bwd(dx1, dh0, x0, npre0, phases=[
        _Phase(like_b_gwi, b_gwi, [_reduce_cross(t_gwi_b, 512, half + 512, 256)])])

    early = [dnpre0, dnpost0, dbgate, dogain, dw2[:LR]]
    early_pack = _pack(early)
    like_early = _sds((N_DEV,) + early_pack.shape, F32)
    (g_swo, d_swo, nm_swo, nv_swo), (b_gwi, g_early) = _sum_adamw(
        p_swo, a_swo, b_swo, sgu_w_out[0], m_sgu_w_out[0], v_sgu_w_out[0], name="adamw_sgu_w_out", table=mine, phases=[
            _Phase(like_b_gwi, b_gwi, [_reduce_cross(t_gwi_b, 768, half + 768, 256)]),
            _Phase(like_early, None, [_gather_send(early_pack, 0, early_pack.shape[0])])])
    (g_swi_, d_swi, nm_swi, nv_swi), (g_early,) = _sum_adamw(
        p_swi, a_swi, b_swi, sgu_w_in[0], m_sgu_w_in[0], v_sgu_w_in[0], name="adamw_sgu_w_in", table=mine, phases=[
            _Phase(like_early, g_early, [_gather_pass(0, early_pack.shape[0])])])
    (g_gwo_, d_gwo, nm_gwo, nv_gwo), _ = _sum_adamw(
        p_gwo, a_gwo, b_gwo, gla_w_out[0], m_gla_w_out[0], v_gla_w_out[0], name="adamw_gla_w_out", table=mine)
    (g_gwi_, d_gwi, nm_gwi, nv_gwi), _ = _sum_adamw(
        own_gwi, a_gwi, b_gwi, gla_w_in[0], m_gla_w_in[0], v_gla_w_in[0], name="adamw_gla_w_in")

    g_npre1, g_npost1, g_lng_full, g_lnb_full, g_wsp, g_bsp = _unpack(_sum_parts(g_late, "sum_late_small_grads"), late)
    g_npre0, g_npost0, g_bgate, g_ogain, g_w2_full = _unpack(_sum_parts(g_early, "sum_early_small_grads"), early)
    g_w2 = lax.dynamic_slice(g_w2_full, (0, me * 128), (LR, 128))
    g_lng = lax.dynamic_slice(g_lng_full, (0, me * 256), (1, 256))
    g_lnb = lax.dynamic_slice(g_lnb_full, (0, me * 256), (1, 256))
    small_g = [jnp.concatenate([g_npre0, g_npre1], 0), jnp.concatenate([g_npost0, g_npost1], 0), g_w2, g_bgate, g_ogain,
               g_lng, g_lnb, g_wsp, g_bsp]
    small_w = [norm_pre, norm_post, gla_w_gate2[0], gla_b_gate, gla_o_gain, sgu_ln_gain, sgu_ln_bias, sgu_w_spatial[0], sgu_b_spatial[0]]
    small_m = [m_norm_pre, m_norm_post, m_gla_w_gate2[0], m_gla_b_gate, m_gla_o_gain, m_sgu_ln_gain, m_sgu_ln_bias, m_sgu_w_spatial[0], m_sgu_b_spatial[0]]
    small_v = [v_norm_pre, v_norm_post, v_gla_w_gate2[0], v_gla_b_gate, v_gla_o_gain, v_sgu_ln_gain, v_sgu_ln_bias, v_sgu_w_spatial[0], v_sgu_b_spatial[0]]
    d_pack, nm_pack, nv_pack = _adamw(_pack(small_w), _pack(small_g), _pack(small_m), _pack(small_v), "adamw_small")

    out_like = [norm_pre, norm_post, gla_w_gate2, gla_b_gate, gla_o_gain, sgu_ln_gain, sgu_ln_bias, sgu_w_spatial, sgu_b_spatial]
    sg_ = [g.reshape(s.shape) for g, s in zip(small_g, out_like)]
    sd_, sm_, sv_ = (_unpack(pk, out_like) for pk in (d_pack, nm_pack, nv_pack))

    def assemble(small_list, w_in_g, w_out_g, w_in_s, w_out_s):
        npre_, npost_, w2_, bg_, og_, lg_, lb_, wsp_, bsp_ = small_list
        return [npre_, npost_, w_in_g[None], w2_, bg_, og_, w_out_g[None], w_in_s[None], lg_, lb_, wsp_, bsp_, w_out_s[None]]

    return (loss, grad_x.reshape(1, S, D),
            *assemble(sg_, g_gwi_, g_gwo_, g_swi_, g_swo),
            *assemble(sd_, d_gwi, d_gwo, d_swi, d_swo),
            *assemble(sm_, nm_gwi, nm_gwo, nm_swi, nm_swo),
            *assemble(sv_, nv_gwi, nv_gwo, nv_swi, nv_swo))
```

```python
import functools

import jax
import jax.numpy as jnp
from jax import lax
from jax.experimental import pallas as pl
from jax.experimental.pallas import tpu as pltpu

F32 = jnp.float32
BF16 = jnp.bfloat16

N_DEV = 8
S = 2048
D = 2048
H = 4
DK = 256
DV = 512
C = 64
NC = S // C
GLA_COLS = 6160
GLA_PAD = 6272
Q0, K0, V0, G0, LR0 = 0, 1024, 2048, 4096, 6144
LR = 16
LRP = 128
SGU_COLS = 6144
SGU_BLOCK = 128
SGU_G = 8
SGU_GD = 256
EPS = 1e-6
GLA_TAU = 16.0

ADAM_LR, ADAM_B1, ADAM_B2, ADAM_EPS, ADAM_WD, ADAM_STEP = 0.001, 0.9, 0.999, 1e-08, 0.01, 10

V7X_VMEM_BYTES = 64 * 1024 * 1024
VMEM_CEILING = V7X_VMEM_BYTES - 6 * 1024 * 1024
MESH = pl.DeviceIdType.MESH
HIGHEST = lax.Precision.HIGHEST
HBM_SPEC = pl.BlockSpec(memory_space=pl.ANY)


def _sds(shape, dtype):
    return jax.ShapeDtypeStruct(tuple(shape), dtype)


def _nbytes(shape, dtype):
    n = 1
    for s in shape:
        n *= s
    return n * jnp.dtype(dtype).itemsize


def _dot(a, b, dims=(((1,), (0,)), ((), ())), precision=None):
    return lax.dot_general(a, b, dims, precision=precision, preferred_element_type=F32)


NN = (((1,), (0,)), ((), ()))
TN = (((0,), (0,)), ((), ()))
NT = (((1,), (1,)), ((), ()))


def _place():
    return lax.axis_index("x"), lax.axis_index("y"), lax.axis_index("c")


def _index_of(px, py, pc):
    return 4 * px + 2 * py + pc


def _chips(x, y):
    return [(1 - x, y), (x, 1 - y), (1 - x, 1 - y)]


def _rcopy(src, dst, send_sem, recv_sem, to):
    return pltpu.make_async_remote_copy(src_ref=src, dst_ref=dst, send_sem=send_sem, recv_sem=recv_sem,
                                        device_id=to, device_id_type=MESH)


class _Move:
    def __init__(self, ins, n_remote, n_local, make):
        self.ins, self.n_remote, self.n_local, self.make = list(ins), n_remote, n_local, make

    def start(self, in_refs, buf, sems):
        sends, _, local = self.make(in_refs, buf, *sems)
        for cp in local + sends:
            cp.start()

    def finish(self, in_refs, buf, sems):
        sends, arrivals, local = self.make(in_refs, buf, *sems)
        for cp in arrivals:
            cp.wait_recv()
        for cp in sends:
            cp.wait_send()
        for cp in local:
            cp.wait()


class _Phase:
    def __init__(self, like, so_far, moves):
        self.like, self.so_far, self.moves = like, so_far, list(moves)


def _gather_send(shard, r0, nr):
    def make(in_refs, g, ss, rs, ls):
        sh, = in_refs
        x, y, c = _place()
        me = _index_of(x, y, c)
        rows = pl.ds(r0, nr)
        peers = [(x, y, 1 - c)] + [(px, py, c) for px, py in _chips(x, y)]
        sends = [_rcopy(sh.at[rows], g.at[me, rows], ss.at[k], rs.at[k], p) for k, p in enumerate(peers)]
        arrivals = [_rcopy(sh.at[rows], g.at[_index_of(*p), rows], ss.at[k], rs.at[k], p) for k, p in enumerate(peers)]
        return sends, arrivals, [pltpu.make_async_copy(sh.at[rows], g.at[me, rows], ls.at[0])]

    return _Move([shard], 4, 1, make)


def _gather_pass(r0, nr):
    def make(in_refs, g, ss, rs, ls):
        x, y, c = _place()
        rows = pl.ds(r0, nr)
        sends = [_rcopy(g.at[_index_of(px, py, c), rows], g.at[_index_of(px, py, c), rows], ss.at[j], rs.at[j], (x, y, 1 - c))
                 for j, (px, py) in enumerate(_chips(x, y))]
        arrivals = [_rcopy(g.at[_index_of(px, py, c), rows], g.at[_index_of(px, py, 1 - c), rows], ss.at[j], rs.at[j], (x, y, 1 - c))
                    for j, (px, py) in enumerate(_chips(x, y))]
        return sends, arrivals, []

    return _Move([], 3, 0, make)


def _own_table():
    x, y, c = _place()
    return jnp.stack([_index_of(px, py, c) for px, py in [(x, y)] + _chips(x, y)]).astype(jnp.int32)


def _blockify_pair(dw, own_so_far, a_so_far, dst_r0, name):
    rows, tr, cw, win = dw.shape[0], 256, GLA_COLS // N_DEV, 896
    n_steps = rows // tr

    def body(*refs):
        x_ref, own_ref, a_ref, stage_ref, send_sems, recv_sem = refs[0], *refs[-5:]
        i = pl.program_id(0)
        x, y, c = _place()

        def send(slot, k):
            dst = a_ref.at[k, pl.ds(pl.multiple_of(dst_r0 + i * tr, tr), tr)]
            return _rcopy(stage_ref.at[slot], dst, send_sems.at[slot], recv_sem.at[0], (x, y, 1 - c))

        for j in range(N_DEV):
            window = x_ref[:, 768 * j:768 * j + win].astype(F32)
            tile = (pltpu.roll(window, win - 2 * j, 1) if j else window)[:, :cw].astype(BF16)
            k = ((j >> 2) ^ x) + 2 * (((j >> 1) & 1) ^ y)

            @pl.when((j & 1) == c)
            def _():
                own_ref[k] = tile

            @pl.when((j & 1) != c)
            def _():
                slot = (j >> 1) & 1
                if j >> 1 >= 2:
                    send(slot, k).wait_send()
                else:
                    pl.when(i > 0)(lambda: send(slot, k).wait_send())
                stage_ref[slot] = tile
                send(slot, k).start()

        @pl.when(i == n_steps - 1)
        def _():
            send(0, 0).wait_send()
            send(1, 0).wait_send()
            arrived = a_ref.at[:, pl.ds(dst_r0, rows)]
            _rcopy(arrived, arrived, send_sems.at[0], recv_sem.at[0], (x, y, 1 - c)).wait_recv()

    continues = a_so_far is not None
    own, a = pl.pallas_call(
        body, grid=(n_steps,),
        in_specs=[pl.BlockSpec((tr, GLA_PAD), lambda i: (i, 0))] + [HBM_SPEC] * (2 * continues),
        out_specs=[pl.BlockSpec((4, tr, cw), lambda i: (0, dst_r0 // tr + i, 0)), HBM_SPEC],
        out_shape=[_sds((4, D, cw), BF16), _sds((4, D, cw), BF16)],
        scratch_shapes=[pltpu.VMEM((2, tr, cw), BF16), pltpu.SemaphoreType.DMA((2,)), pltpu.SemaphoreType.DMA((1,))],
        input_output_aliases={1: 0, 2: 1} if continues else {},
        compiler_params=pltpu.CompilerParams(dimension_semantics=("arbitrary",), vmem_limit_bytes=48 * 1024 * 1024),
        name=name,
    )(*([dw] + [own_so_far, a_so_far] * continues))
    return own, a


def _reduce_cross(sums, src_r0, dst_r0, nr):
    def make(in_refs, b, ss, rs, ls):
        t, = in_refs
        x, y, c = _place()
        src, dst = pl.ds(src_r0, nr), pl.ds(dst_r0, nr)
        sends = [_rcopy(t.at[j, src], b.at[j, dst], ss.at[j], rs.at[j], (px, py, c)) for j, (px, py) in enumerate(_chips(x, y))]
        return sends, sends, []

    return _Move([sums], 3, 0, make)


def _hosted(body, *, name, grid, in_specs, out_specs, out_shape, args, scratch_shapes=(), block_bytes, scratch_bytes=0,
            phases=(), table=None):
    n_in, n_out, n_scr = len(args), len(out_shape), len(scratch_shapes)
    all_args, all_out_shape, sems, aliases, layout = list(args), list(out_shape), [], {}, []
    for j, ph in enumerate(phases):
        counts = []
        for mv in ph.moves:
            all_args += mv.ins
            counts.append(len(mv.ins))
            sems += [pltpu.SemaphoreType.DMA((mv.n_remote,)), pltpu.SemaphoreType.DMA((mv.n_remote,)),
                     pltpu.SemaphoreType.DMA((max(mv.n_local, 1),))]
        if ph.so_far is not None:
            aliases[len(all_args)] = n_out + j
            all_args.append(ph.so_far)
        layout.append((counts, ph.so_far is not None))
        all_out_shape.append(ph.like)
    n_extra_in = len(all_args) - n_in

    def wrapped(*refs):
        ins, pos = refs[:n_in], n_in
        move_ins = []
        for counts, continues in layout:
            per_move = []
            for cnt in counts:
                per_move.append(refs[pos:pos + cnt])
                pos += cnt
            pos += continues
            move_ins.append(per_move)
        outs = refs[pos:pos + n_out]
        bufs = refs[pos + n_out:pos + n_out + len(phases)]
        pos += n_out + len(phases)
        scratch = refs[pos:pos + n_scr]
        pos += n_scr
        move_sems = []
        for ph in phases:
            per_move = []
            for _ in ph.moves:
                per_move.append(refs[pos:pos + 3])
                pos += 3
            move_sems.append(per_move)

        def each_move(fn_name):
            for ph, buf, per_in, per_sem in zip(phases, bufs, move_ins, move_sems):
                for mv, mv_in, mv_sem in zip(ph.moves, per_in, per_sem):
                    getattr(mv, fn_name)(mv_in, buf, mv_sem)

        if phases:
            first = functools.reduce(jnp.logical_and, [pl.program_id(a) == 0 for a in range(len(grid))])
            last = functools.reduce(jnp.logical_and, [pl.program_id(a) == grid[a] - 1 for a in range(len(grid))])
            pl.when(first)(lambda: each_move("start"))
        body(*ins, *outs, *scratch)
        if phases:
            pl.when(last)(lambda: each_move("finish"))

    est = 2 * block_bytes + scratch_bytes
    params = pltpu.CompilerParams(dimension_semantics=("arbitrary",) * len(grid),
                                  vmem_limit_bytes=min(VMEM_CEILING, max(32 * 1024 * 1024, 2 * est)))
    all_in_specs, all_out_specs = list(in_specs) + [HBM_SPEC] * n_extra_in, list(out_specs) + [HBM_SPEC] * len(phases)
    if table is None:
        results = pl.pallas_call(
            wrapped, grid=grid, in_specs=all_in_specs, out_specs=all_out_specs, out_shape=all_out_shape,
            scratch_shapes=list(scratch_shapes) + sems, input_output_aliases=aliases, compiler_params=params, name=name,
        )(*all_args)
    else:
        results = pl.pallas_call(
            lambda table_ref, *refs: wrapped(*refs),
            grid_spec=pltpu.PrefetchScalarGridSpec(num_scalar_prefetch=1, grid=grid, in_specs=all_in_specs, out_specs=all_out_specs,
                                                   scratch_shapes=list(scratch_shapes) + sems),
            out_shape=all_out_shape, input_output_aliases={k + 1: v for k, v in aliases.items()}, compiler_params=params, name=name,
        )(table, *all_args)
    return list(results[:n_out]), list(results[n_out:])


def _gather_first(shard, small, name):
    cw, tr, n_tiles = shard.shape[1], 256, GLA_PAD // 128
    n_chunks = D // tr

    def body(sh_ref, sm_ref, wn_ref, g_ref, gs_ref, wt_ref, win_ref, tmp_ref, send_sems, recv_sems, local_sems):
        x, y, c = _place()
        me, sibling = (x, y, c), (x, y, 1 - c)
        chips = _chips(x, y)

        def rows_of(ch):
            return pl.ds(ch * tr if isinstance(ch, int) else pl.multiple_of(ch * tr, tr), tr)

        def wcopy(k, ch, block, to, src=None):
            dst = g_ref.at[_index_of(*block), rows_of(ch)]
            return _rcopy(dst if src is None else src, dst, send_sems.at[k * n_chunks + ch], recv_sems.at[k * n_chunks + ch], to)

        def scopy(k, block, to, src=None):
            dst = gs_ref.at[_index_of(*block)]
            return _rcopy(dst if src is None else src, dst, send_sems.at[7 * n_chunks + k], recv_sems.at[7 * n_chunks + k], to)

        small_own = pltpu.make_async_copy(sm_ref, gs_ref.at[_index_of(*me)], local_sems.at[1])
        small_own.start()
        small_first = [scopy(0, me, sibling, src=sm_ref)] + [scopy(1 + j, me, (*chip, c), src=sm_ref) for j, chip in enumerate(chips)]
        for cp in small_first:
            cp.start()
        for ch in range(n_chunks):
            for j, chip in enumerate(chips):
                wcopy(1 + j, ch, me, (*chip, c), src=sh_ref.at[rows_of(ch)]).start()

        def clear(t, carry):
            wt_ref[t] = jnp.zeros((D, 128), BF16)
            return carry

        lax.fori_loop(0, n_tiles, clear, 0)
        tmp_ref[...] = jnp.zeros_like(tmp_ref)

        def emit(t):
            pltpu.make_async_copy(wt_ref.at[t], wn_ref.at[t], local_sems.at[2]).start()

        def take(block, slot, arrive_k=None, pass_k=None):
            b = _index_of(*block)

            def chunk(ch, carry):
                rows = rows_of(ch)
                if arrive_k is not None:
                    wcopy(arrive_k, ch, block, me).wait_recv()
                cp = pltpu.make_async_copy(sh_ref.at[rows] if arrive_k is None else g_ref.at[b, rows], win_ref.at[slot, rows],
                                           local_sems.at[0])
                cp.start()
                cp.wait()
                if pass_k is not None:
                    wcopy(pass_k, ch, block, sibling, src=win_ref.at[slot, rows]).start()
                tmp_ref[:, :cw] = win_ref[slot, rows, :].astype(F32)
                shifted = pltpu.roll(tmp_ref[...], 2 * b, 1)
                for u in range(7):
                    wt_ref[6 * b + u, rows, :] = (wt_ref[6 * b + u, rows, :].astype(F32) + shifted[:, 128 * u:128 * (u + 1)]).astype(BF16)
                return carry

            lax.fori_loop(0, n_chunks, chunk, 0)
            for u in range(1, 6):
                emit(6 * b + u)

        def drained(k, block, slot):
            for ch in range(n_chunks):
                wcopy(k, ch, block, sibling, src=win_ref.at[slot, rows_of(ch)]).wait_send()

        take(me, 0, pass_k=0)
        take((*chips[0], c), 1, arrive_k=1, pass_k=4)
        drained(0, me, 0)
        take((*chips[1], c), 0, arrive_k=2, pass_k=5)
        small_passed = []
        for j, chip in enumerate(chips):
            scopy(1 + j, (*chip, c), me).wait_recv()
            cp = scopy(4 + j, (*chip, c), sibling)
            cp.start()
            small_passed.append(cp)
        drained(4, (*chips[0], c), 1)
        take(sibling, 1, arrive_k=0)
        drained(5, (*chips[1], c), 0)
        take((*chips[0], 1 - c), 0, arrive_k=4)
        take((*chips[1], 1 - c), 1, arrive_k=5)
        take((*chips[2], c), 0, arrive_k=3, pass_k=6)
        take((*chips[2], 1 - c), 1, arrive_k=6)
        for t in range(0, n_tiles, 6):
            emit(t)
        scopy(0, sibling, me).wait_recv()
        for j, chip in enumerate(chips):
            scopy(4 + j, (*chip, 1 - c), me).wait_recv()
        drained(6, (*chips[2], c), 0)
        for ch in range(n_chunks):
            for j, chip in enumerate(chips):
                wcopy(1 + j, ch, me, (*chip, c), src=sh_ref.at[rows_of(ch)]).wait_send()
        for cp in small_first + small_passed:
            cp.wait_send()
        small_own.wait()
        pltpu.make_async_copy(wn_ref, wn_ref, local_sems.at[2]).wait()

    wn, _, gs = pl.pallas_call(
        body,
        in_specs=[HBM_SPEC] * 2, out_specs=[HBM_SPEC] * 3,
        out_shape=[_sds((n_tiles, D, 128), BF16), _sds((N_DEV,) + shard.shape, BF16), _sds((N_DEV,) + small.shape, small.dtype)],
        scratch_shapes=[pltpu.VMEM((n_tiles, D, 128), BF16), pltpu.VMEM((2, D, cw), BF16), pltpu.VMEM((tr, 7 * 128), F32),
                        pltpu.SemaphoreType.DMA((7 * n_chunks + 7,)), pltpu.SemaphoreType.DMA((7 * n_chunks + 7,)),
                        pltpu.SemaphoreType.DMA((3,))],
        compiler_params=pltpu.CompilerParams(vmem_limit_bytes=48 * 1024 * 1024),
        name=name,
    )(shard, small)
    return wn, gs


def _mm(a, b, mode, out_dtype, *, tm, tn, tk, name, b_blocked=False, b_tiled=False, out_blocked=False, m_tiles=None, pair=None,
        phases=()):
    if mode == "nn":
        (m, k), dims = a.shape, NN
        a_blk, a_map = (tm, tk), (lambda i, j, kk: (i, kk))
        if b_blocked:
            assert b.shape[1] == k and b.shape[2] == tn and tk == k
            n = b.shape[0] * tn
            b_spec = pl.BlockSpec((None, tk, tn), lambda i, j, kk: (j, kk, 0))
        elif b_tiled:
            assert b.shape[1] == k and b.shape[2] == 128 and tn % 128 == 0
            n = b.shape[0] * 128
            b_spec = pl.BlockSpec((tn // 128, tk, 128), lambda i, j, kk: (j, kk, 0))
        else:
            assert b.shape[0] == k
            n = b.shape[1]
            b_spec = pl.BlockSpec((tk, tn), lambda i, j, kk: (kk, j))
    elif mode == "tn":
        (k, m), n, dims = a.shape, b.shape[1], TN
        assert b.shape[0] == k
        first = 0 if m_tiles is None else m_tiles[0]
        a_blk, a_map = (tk, tm), (lambda i, j, kk: (kk, i + first))
        b_spec = pl.BlockSpec((tk, tn), lambda i, j, kk: (kk, j))
    else:
        (m, k), dims = a.shape, NT
        a_blk, a_map = (tm, tk), (lambda i, j, kk: (i, kk))
        if b_blocked:
            assert b.shape[0] * b.shape[2] == k and b.shape[2] == tk
            n = b.shape[1]
            b_spec = pl.BlockSpec((None, tn, tk), lambda i, j, kk: (kk, j, 0))
        elif b_tiled:
            assert b.shape[0] * 128 == k and b.shape[2] == 128 and tk % 128 == 0
            n = b.shape[1]
            b_spec = pl.BlockSpec((tk // 128, tn, 128), lambda i, j, kk: (kk, j, 0))
        else:
            assert b.shape[1] == k
            n = b.shape[0]
            b_spec = pl.BlockSpec((tn, tk), lambda i, j, kk: (j, kk))
    assert m % tm == 0 and n % tn == 0 and k % tk == 0, (a.shape, b.shape, mode)
    nk = k // tk
    n_row_tiles = m // tm if m_tiles is None else m_tiles[1]
    if out_blocked:
        out_shape, out_spec = _sds((n // tn, n_row_tiles * tm, tn), out_dtype), pl.BlockSpec((None, tm, tn), lambda i, j, kk: (j, i, 0))
    else:
        out_shape, out_spec = _sds((n_row_tiles * tm, n), out_dtype), pl.BlockSpec((tm, tn), lambda i, j, kk: (i, j))

    grid = (n_row_tiles, n // tn, nk)

    def body(a_ref, b_ref, o_ref, *rest):
        rhs = jnp.concatenate([b_ref[u] for u in range(b_ref.shape[0])], axis=1) if b_tiled else b_ref[...]
        p = _dot(a_ref[...], rhs, dims)
        if nk == 1:
            o_ref[...] = p.astype(out_dtype)
            if pair is not None:
                _send_to_sibling(p.astype(out_dtype), *rest)
        else:
            acc_ref, = rest
            kk = pl.program_id(2)

            @pl.when(kk == 0)
            def _():
                acc_ref[...] = p

            @pl.when(kk > 0)
            def _():
                acc_ref[...] += p

            @pl.when(kk == nk - 1)
            def _():
                o_ref[...] = acc_ref[...].astype(out_dtype)

    def _send_to_sibling(tile, pair_ref, stage_ref, send_sems, recv_sem):
        i, j = pl.program_id(0), pl.program_id(1)
        x, y, c = _place()
        blk = pair["block"](i, j)
        k = ((blk >> 2) ^ x) + 2 * (((blk >> 1) & 1) ^ y)
        ordinal = pair["ordinal"](i, j)

        def send(slot):
            return _rcopy(stage_ref.at[slot], pair["dst"](pair_ref, k, i, j), send_sems.at[slot], recv_sem.at[0], (x, y, 1 - c))

        @pl.when((blk & 1) != c)
        def _():
            slot = ordinal & 1

            @pl.when(ordinal >= 2)
            def _():
                send(slot).wait_send()

            stage_ref[slot] = tile
            send(slot).start()

        @pl.when((i == grid[0] - 1) & (j == grid[1] - 1))
        def _():
            send(0).wait_send()
            send(1).wait_send()
            _rcopy(pair_ref, pair_ref, send_sems.at[0], recv_sem.at[0], (x, y, 1 - c)).wait_recv()

    blocks = _nbytes(a_blk, a.dtype) + tk * tn * jnp.dtype(b.dtype).itemsize + _nbytes((tm, tn), out_dtype)
    out_specs, out_shapes, scratch = [out_spec], [out_shape], [] if nk == 1 else [pltpu.VMEM((tm, tn), F32)]
    scratch_bytes = _nbytes((tm, tn), F32) * (nk > 1)
    if pair is not None:
        assert nk == 1
        out_specs, out_shapes = out_specs + [HBM_SPEC], out_shapes + [pair["like"]]
        scratch = [pltpu.VMEM((2, tm, tn), out_dtype), pltpu.SemaphoreType.DMA((2,)), pltpu.SemaphoreType.DMA((1,))]
        scratch_bytes = 2 * _nbytes((tm, tn), out_dtype)
    outs, bufs = _hosted(
        body, name=name, grid=grid,
        in_specs=[pl.BlockSpec(a_blk, a_map), b_spec], out_specs=out_specs, out_shape=out_shapes, args=[a, b],
        scratch_shapes=scratch, block_bytes=blocks, scratch_bytes=scratch_bytes, phases=phases)
    return outs[0], outs[1:] + bufs


RB = 256


def _row_spec(width):
    return pl.BlockSpec((RB, width), lambda i: (i, 0))


def _vec_spec(width):
    return pl.BlockSpec((1, width), lambda i: (0, 0))


def _rinv(x):
    return lax.rsqrt(jnp.mean(x * x, axis=-1, keepdims=True) + EPS)


def _norm_bwd(dyn, xhat, r):
    return r * (dyn - xhat * jnp.mean(dyn * xhat, axis=-1, keepdims=True))


def _colsum(x):
    return jnp.sum(x, axis=0, keepdims=True)


def _accumulate(ref, value):
    @pl.when(pl.program_id(0) == 0)
    def _():
        ref[...] = value

    @pl.when(pl.program_id(0) > 0)
    def _():
        ref[...] += value


def _prenorm(x, gain):
    def body(x_ref, g_ref, h_ref):
        xv = x_ref[...]
        h_ref[...] = (xv * _rinv(xv) * g_ref[...]).astype(BF16)

    outs, _ = _hosted(body, name="prenorm", grid=(S // RB,), in_specs=[_row_spec(D), _vec_spec(D)], out_specs=[_row_spec(D)],
                      out_shape=[_sds((S, D), BF16)], args=[x, gain], block_bytes=RB * D * 6)
    return outs[0]


def _mid_fwd(x, y, npost, npre, phases=()):
    def body(x_ref, y_ref, po_ref, pr_ref, x1_ref, h1_ref):
        yv = y_ref[...]
        x1 = x_ref[...] + yv * _rinv(yv) * po_ref[...]
        x1_ref[...] = x1
        h1_ref[...] = (x1 * _rinv(x1) * pr_ref[...]).astype(BF16)

    return _hosted(body, name="mid_fwd", grid=(S // RB,), in_specs=[_row_spec(D), _row_spec(D), _vec_spec(D), _vec_spec(D)],
                   out_specs=[_row_spec(D), _row_spec(D)], out_shape=[_sds((S, D), F32), _sds((S, D), BF16)],
                   args=[x, y, npost, npre], block_bytes=RB * D * 14, phases=phases)


def _final(x1, y1, tgt, npost):
    def body(x_ref, y_ref, t_ref, po_ref, loss_ref, dx_ref, dy_ref, dpo_ref):
        yv = y_ref[...]
        r = _rinv(yv)
        yhat = yv * r
        err = x_ref[...] + yhat * po_ref[...] - t_ref[...]
        dx = err * (1.0 / D)
        dx_ref[...] = dx
        dy_ref[...] = _norm_bwd(dx * po_ref[...], yhat, r).astype(BF16)
        _accumulate(loss_ref, _colsum(err * err))
        _accumulate(dpo_ref, _colsum(dx * yhat))

    outs, _ = _hosted(body, name="final", grid=(S // RB,), in_specs=[_row_spec(D), _row_spec(D), _row_spec(D), _vec_spec(D)],
                      out_specs=[_vec_spec(D), _row_spec(D), _row_spec(D), _vec_spec(D)],
                      out_shape=[_sds((1, D), F32), _sds((S, D), F32), _sds((S, D), BF16), _sds((1, D), F32)],
                      args=[x1, y1, tgt, npost], block_bytes=RB * D * 18)
    return outs


def _mid_bwd(dx2, dh1, x1, y0, npre, npost, phases=()):
    def body(dx2_ref, dh_ref, x_ref, y_ref, pr_ref, po_ref, dx1_ref, dy_ref, dpr_ref, dpo_ref):
        xv = x_ref[...]
        r = _rinv(xv)
        xhat = xv * r
        dh = dh_ref[...]
        dx1 = dx2_ref[...] + _norm_bwd(dh * pr_ref[...], xhat, r)
        dx1_ref[...] = dx1
        yv = y_ref[...]
        ry = _rinv(yv)
        yhat = yv * ry
        dy_ref[...] = _norm_bwd(dx1 * po_ref[...], yhat, ry).astype(BF16)
        _accumulate(dpr_ref, _colsum(dh * xhat))
        _accumulate(dpo_ref, _colsum(dx1 * yhat))

    return _hosted(body, name="mid_bwd", grid=(S // RB,), in_specs=[_row_spec(D)] * 4 + [_vec_spec(D)] * 2,
                   out_specs=[_row_spec(D), _row_spec(D), _vec_spec(D), _vec_spec(D)],
                   out_shape=[_sds((S, D), F32), _sds((S, D), BF16), _sds((1, D), F32), _sds((1, D), F32)],
                   args=[dx2, dh1, x1, y0, npre, npost], block_bytes=RB * D * 22, phases=phases)


def _first_bwd(dx1, dh0, x0, npre, phases=()):
    def body(dx1_ref, dh_ref, x_ref, pr_ref, gx_ref, dpr_ref):
        xv = x_ref[...]
        r = _rinv(xv)
        xhat = xv * r
        dh = dh_ref[...]
        gx_ref[...] = dx1_ref[...] + _norm_bwd(dh * pr_ref[...], xhat, r)
        _accumulate(dpr_ref, _colsum(dh * xhat))

    return _hosted(body, name="first_bwd", grid=(S // RB,), in_specs=[_row_spec(D)] * 3 + [_vec_spec(D)],
                   out_specs=[_row_spec(D), _vec_spec(D)], out_shape=[_sds((S, D), F32), _sds((1, D), F32)],
                   args=[dx1, dh0, x0, npre], block_bytes=RB * D * 16, phases=phases)


GLA_RB = 256
GLA_CPB = GLA_RB // C


def _sigmoid(x):
    return 1.0 / (1.0 + jnp.exp(-x))


def _tri(strict):
    r = lax.broadcasted_iota(jnp.int32, (C, C), 0)
    c = lax.broadcasted_iota(jnp.int32, (C, C), 1)
    return jnp.where(c < r if strict else c <= r, 1.0, 0.0).astype(F32)


def _gla_gate(glr_b, w2_h, b_h, tri):
    z = _dot(glr_b, w2_h) + b_h
    log_a = (jnp.minimum(z, 0.0) - jnp.log(1.0 + jnp.exp(-jnp.abs(z)))) * (1.0 / GLA_TAU)
    bcum = _dot(tri, log_a, precision=HIGHEST)
    b_end = jnp.sum(log_a, axis=0, keepdims=True)
    return z, jnp.exp(b_end - bcum), jnp.exp(b_end)


def _gla_fwd(proj, w2p, bgate, ogain, phases=()):
    def body(p_ref, w2_ref, b_ref, og_ref, y_ref, st_out_ref, st_ref):
        @pl.when(pl.program_id(0) == 0)
        def _():
            st_ref[...] = jnp.zeros_like(st_ref)

        tri = _tri(False)

        def chunk(ci, carry):
            rows = pl.ds(pl.multiple_of(ci * C, C), C)
            glr_b = p_ref[rows, LR0:LR0 + LRP].astype(BF16)
            for h in range(H):
                _, ea, dec = _gla_gate(glr_b, w2_ref[:, h * DK:(h + 1) * DK], b_ref[:, h * DK:(h + 1) * DK], tri)
                k_dec = (p_ref[rows, K0 + h * DK:K0 + (h + 1) * DK] * ea).astype(BF16)
                v_b = p_ref[rows, V0 + h * DV:V0 + (h + 1) * DV].astype(BF16)
                st = st_ref[h] * dec + _dot(v_b, k_dec, TN)
                st_ref[h] = st
                st_b = st.astype(BF16)
                st_out_ref[ci, h] = st_b
                q_b = (p_ref[rows, Q0 + h * DK:Q0 + (h + 1) * DK] * (DK ** -0.5)).astype(BF16)
                o = _dot(q_b, st_b, NT)
                on = o * _rinv(o)
                g = p_ref[rows, G0 + h * DV:G0 + (h + 1) * DV]
                y_ref[rows, h * DV:(h + 1) * DV] = (on * og_ref[:, h * DV:(h + 1) * DV] * (g * _sigmoid(g))).astype(BF16)
            return carry

        lax.fori_loop(0, GLA_CPB, chunk, 0)

    blocks = GLA_RB * GLA_PAD * 4 + GLA_RB * D * 2 + GLA_CPB * H * DV * DK * 2
    return _hosted(
        body, name="gla_fwd", grid=(S // GLA_RB,),
        in_specs=[pl.BlockSpec((GLA_RB, GLA_PAD), lambda i: (i, 0)),
                  pl.BlockSpec((LRP, H * DK), lambda i: (0, 0)),
                  pl.BlockSpec((1, H * DK), lambda i: (0, 0)),
                  pl.BlockSpec((1, H * DV), lambda i: (0, 0))],
        out_specs=[pl.BlockSpec((GLA_RB, H * DV), lambda i: (i, 0)),
                   pl.BlockSpec((GLA_CPB, H, DV, DK), lambda i: (i, 0, 0, 0))],
        out_shape=[_sds((S, H * DV), BF16), _sds((NC, H, DV, DK), BF16)],
        args=[proj, w2p, bgate, ogain], scratch_shapes=[pltpu.VMEM((H, DV, DK), F32)],
        block_bytes=blocks, scratch_bytes=H * DV * DK * 4, phases=phases)


def _gla_bwd(proj, dypre, states, w2p, bgate, ogain, phases=()):
    nb = S // GLA_RB

    def body(p_ref, dy_ref, st_blk_ref, st_prev_ref, w2_ref, b_ref, og_ref,
             dp_ref, dog_ref, dbg_ref, dw2_ref, r_ref):
        step = pl.program_id(0)

        @pl.when(step == 0)
        def _():
            r_ref[...] = jnp.zeros_like(r_ref)
            dog_ref[...] = jnp.zeros_like(dog_ref)
            dbg_ref[...] = jnp.zeros_like(dbg_ref)
            dw2_ref[...] = jnp.zeros_like(dw2_ref)

        tri = _tri(False)
        tri_strict = _tri(True)
        has_prev = jnp.where(step < nb - 1, 1.0, 0.0).astype(F32)

        def chunk(ci, st_prev_of):
            rows = pl.ds(ci * C if isinstance(ci, int) else pl.multiple_of(ci * C, C), C)
            glr_b = p_ref[rows, LR0:LR0 + LRP].astype(BF16)
            dglr = jnp.zeros((C, LRP), F32)
            for h in range(H):
                kcol = slice(h * DK, (h + 1) * DK)
                vcol = slice(h * DV, (h + 1) * DV)
                w2_h = w2_ref[:, kcol]
                z, ea, dec = _gla_gate(glr_b, w2_h, b_ref[:, kcol], tri)
                k_dec = p_ref[rows, K0 + h * DK:K0 + (h + 1) * DK] * ea
                k_dec_b = k_dec.astype(BF16)
                v_b = p_ref[rows, V0 + h * DV:V0 + (h + 1) * DV].astype(BF16)
                q_b = (p_ref[rows, Q0 + h * DK:Q0 + (h + 1) * DK] * (DK ** -0.5)).astype(BF16)
                st_b = st_blk_ref[ci, h]
                o = _dot(q_b, st_b, NT)
                rinv = _rinv(o)
                on = o * rinv
                g = p_ref[rows, G0 + h * DV:G0 + (h + 1) * DV]
                sg = _sigmoid(g)
                og = og_ref[:, vcol]
                dyp = dy_ref[rows, vcol]
                dp_ref[rows, G0 + h * DV:G0 + (h + 1) * DV] = (dyp * (on * og) * (sg * (1.0 + g * (1.0 - sg)))).astype(BF16)
                dpn = dyp * (g * sg)
                dog_ref[:, vcol] += _colsum(dpn * on)
                do_b = _norm_bwd(dpn * og, on, rinv).astype(BF16)
                gt = _dot(do_b, q_b, TN) + r_ref[h]
                gt_b = gt.astype(BF16)
                dp_ref[rows, Q0 + h * DK:Q0 + (h + 1) * DK] = (_dot(do_b, st_b) * (DK ** -0.5)).astype(BF16)
                dkd = _dot(v_b, gt_b)
                dp_ref[rows, V0 + h * DV:V0 + (h + 1) * DV] = _dot(k_dec_b, gt_b, NT).astype(BF16)
                dp_ref[rows, K0 + h * DK:K0 + (h + 1) * DK] = (dkd * ea).astype(BF16)
                ddec = _colsum(gt * st_prev_of(h))
                dla = _dot(tri_strict, dkd * k_dec, precision=HIGHEST) + ddec * dec
                dz = dla * (1.0 / GLA_TAU) * (1.0 - _sigmoid(z))
                dz_b = dz.astype(BF16)
                r_ref[h] = gt * dec
                dbg_ref[:, kcol] += _colsum(dz)
                dw2_ref[:, kcol] += _dot(glr_b, dz_b, TN)
                dglr = dglr + _dot(dz_b, w2_h, NT)
            dp_ref[rows, LR0:LR0 + LRP] = dglr.astype(BF16)

        def later_chunk(t, carry):
            ci = GLA_CPB - 1 - t
            chunk(ci, lambda h: st_blk_ref[ci - 1, h].astype(F32))
            return carry

        lax.fori_loop(0, GLA_CPB - 1, later_chunk, 0)
        chunk(0, lambda h: st_prev_ref[0, h].astype(F32) * has_prev)

    blocks = (GLA_RB * GLA_PAD * 4 + GLA_RB * D * 4 + (GLA_CPB + 1) * H * DV * DK * 2 + GLA_RB * GLA_PAD * 2)
    rev = lambda i: nb - 1 - i
    return _hosted(
        body, name="gla_bwd", grid=(nb,),
        in_specs=[pl.BlockSpec((GLA_RB, GLA_PAD), lambda i: (rev(i), 0)),
                  pl.BlockSpec((GLA_RB, H * DV), lambda i: (rev(i), 0)),
                  pl.BlockSpec((GLA_CPB, H, DV, DK), lambda i: (rev(i), 0, 0, 0)),
                  pl.BlockSpec((1, H, DV, DK), lambda i: (jnp.maximum(rev(i) * GLA_CPB - 1, 0), 0, 0, 0)),
                  pl.BlockSpec((LRP, H * DK), lambda i: (0, 0)),
                  pl.BlockSpec((1, H * DK), lambda i: (0, 0)),
                  pl.BlockSpec((1, H * DV), lambda i: (0, 0))],
        out_specs=[pl.BlockSpec((GLA_RB, GLA_PAD), lambda i: (rev(i), 0)),
                   pl.BlockSpec((1, H * DV), lambda i: (0, 0)),
                   pl.BlockSpec((1, H * DK), lambda i: (0, 0)),
                   pl.BlockSpec((LRP, H * DK), lambda i: (0, 0))],
        out_shape=[_sds((S, GLA_PAD), BF16), _sds((1, H * DV), F32), _sds((1, H * DK), F32), _sds((LRP, H * DK), F32)],
        args=[proj, dypre, states, states, w2p, bgate, ogain], scratch_shapes=[pltpu.VMEM((H, DV, DK), F32)],
        block_bytes=blocks, scratch_bytes=H * DV * DK * 4, phases=phases)


SGU_RB = 256
GELU_C = 0.7978845608028654
GELU_A = 0.044715


def _gelu(x):
    return 0.5 * x * (1.0 + jnp.tanh(GELU_C * (x + GELU_A * x * x * x)))


def _gelu_grad(x):
    t = jnp.tanh(GELU_C * (x + GELU_A * x * x * x))
    return 0.5 * (1.0 + t) + 0.5 * x * (1.0 - t * t) * (GELU_C * (1.0 + 3.0 * GELU_A * x * x))


def _causal_mask(transposed=False):
    i = lax.broadcasted_iota(jnp.int32, (SGU_BLOCK, SGU_BLOCK), 1 if transposed else 0)
    j = lax.broadcasted_iota(jnp.int32, (SGU_BLOCK, SGU_BLOCK), 0 if transposed else 1)
    return (i >= C) | (j < C)


def _layer_norm(vf, gain, bias):
    mu = jnp.mean(vf, axis=-1, keepdims=True)
    cen = vf - mu
    rstd = lax.rsqrt(jnp.mean(cen * cen, axis=-1, keepdims=True) + EPS)
    xhat = cen * rstd
    return xhat, rstd, xhat * gain + bias


def _sgu_fwd(proj, lng, lnb, ws, bsb, phases=()):
    def body(p_ref, g_ref, b_ref, ws_ref, bs_ref, o_ref):
        mask = _causal_mask()
        for n in range(SGU_RB // SGU_BLOCK):
            rows = slice(n * SGU_BLOCK, (n + 1) * SGU_BLOCK)
            _, _, vn = _layer_norm(_gelu(p_ref[rows, D:2 * D]), g_ref[...], b_ref[...])
            vn_b = vn.astype(BF16)
            for gi in range(SGU_G):
                cols = slice(gi * SGU_GD, (gi + 1) * SGU_GD)
                w = jnp.where(mask, ws_ref[gi], 0.0).astype(BF16)
                vs = _dot(w, vn_b[:, cols]) + bs_ref[gi]
                gate = p_ref[rows, 2 * D + gi * SGU_GD:2 * D + (gi + 1) * SGU_GD]
                o_ref[rows, cols] = (_gelu(p_ref[rows, cols]) * vs * (gate * _sigmoid(gate))).astype(BF16)

    blocks = SGU_RB * SGU_COLS * 4 + SGU_RB * D * 2 + SGU_G * SGU_BLOCK * (SGU_BLOCK + SGU_GD) * 4
    return _hosted(
        body, name="sgu_fwd", grid=(S // SGU_RB,),
        in_specs=[pl.BlockSpec((SGU_RB, SGU_COLS), lambda i: (i, 0)),
                  pl.BlockSpec((1, D), lambda i: (0, 0)), pl.BlockSpec((1, D), lambda i: (0, 0)),
                  pl.BlockSpec((SGU_G, SGU_BLOCK, SGU_BLOCK), lambda i: (0, 0, 0)),
                  pl.BlockSpec((SGU_G, SGU_BLOCK, SGU_GD), lambda i: (0, 0, 0))],
        out_specs=[pl.BlockSpec((SGU_RB, D), lambda i: (i, 0))], out_shape=[_sds((S, D), BF16)],
        args=[proj, lng, lnb, ws, bsb], block_bytes=blocks, phases=phases)


def _sgu_bwd(proj, dpre, lng, lnb, ws, wst, bsb, phases=()):
    nsteps = S // SGU_RB

    def body(p_ref, d_ref, g_ref, b_ref, ws_ref, wst_ref, bs_ref,
             dp_ref, dg_ref, db_ref, dws_ref, dbs_ref, dvn_ref, dvs_acc_ref):
        step = pl.program_id(0)

        @pl.when(step == 0)
        def _():
            dg_ref[...] = jnp.zeros_like(dg_ref)
            db_ref[...] = jnp.zeros_like(db_ref)
            dws_ref[...] = jnp.zeros_like(dws_ref)
            dvs_acc_ref[...] = jnp.zeros_like(dvs_acc_ref)

        mask = _causal_mask()
        maskt = _causal_mask(transposed=True)
        for n in range(SGU_RB // SGU_BLOCK):
            rows = slice(n * SGU_BLOCK, (n + 1) * SGU_BLOCK)
            v = p_ref[rows, D:2 * D]
            xhat, rstd, vn = _layer_norm(_gelu(v), g_ref[...], b_ref[...])
            vn_b = vn.astype(BF16)
            for gi in range(SGU_G):
                cols = slice(gi * SGU_GD, (gi + 1) * SGU_GD)
                w = jnp.where(mask, ws_ref[gi], 0.0).astype(BF16)
                wt = jnp.where(maskt, wst_ref[gi], 0.0).astype(BF16)
                vs = _dot(w, vn_b[:, cols]) + bs_ref[gi]
                u = p_ref[rows, cols]
                gate = p_ref[rows, 2 * D + gi * SGU_GD:2 * D + (gi + 1) * SGU_GD]
                sg = _sigmoid(gate)
                gu = _gelu(u)
                dpre_g = d_ref[rows, cols]
                t = dpre_g * (gate * sg)
                dp_ref[rows, cols] = (t * vs * _gelu_grad(u)).astype(BF16)
                dp_ref[rows, 2 * D + gi * SGU_GD:2 * D + (gi + 1) * SGU_GD] = (
                    dpre_g * gu * vs * (sg * (1.0 + gate * (1.0 - sg)))).astype(BF16)
                dvs = t * gu
                dvs_b = dvs.astype(BF16)
                dvs_acc_ref[:, cols] += dvs
                dws_ref[gi] += _dot(dvs_b, vn_b[:, cols], NT)
                dvn_ref[:, cols] = _dot(wt, dvs_b)
            dvn = dvn_ref[...]
            dg_ref[...] += _colsum(dvn * xhat)
            db_ref[...] += _colsum(dvn)
            dxh = dvn * g_ref[...]
            dvf = rstd * (dxh - jnp.mean(dxh, axis=-1, keepdims=True) - xhat * jnp.mean(dxh * xhat, axis=-1, keepdims=True))
            dp_ref[rows, D:2 * D] = (dvf * _gelu_grad(v)).astype(BF16)

        @pl.when(step == nsteps - 1)
        def _():
            lane = lax.broadcasted_iota(jnp.int32, (SGU_BLOCK, SGU_BLOCK), 1)
            out = jnp.zeros((SGU_BLOCK, SGU_BLOCK), F32)
            for gi in range(SGU_G):
                out = out + jnp.where(lane == gi, jnp.sum(dvs_acc_ref[:, gi * SGU_GD:(gi + 1) * SGU_GD], axis=1, keepdims=True), 0.0)
                dws_ref[gi] = jnp.where(mask, dws_ref[gi], 0.0)
            dbs_ref[...] = out

    blocks = SGU_RB * SGU_COLS * 6 + SGU_RB * D * 4 + SGU_G * SGU_BLOCK * (3 * SGU_BLOCK + SGU_GD) * 4
    const3 = lambda i: (0, 0, 0)
    return _hosted(
        body, name="sgu_bwd", grid=(nsteps,),
        in_specs=[pl.BlockSpec((SGU_RB, SGU_COLS), lambda i: (i, 0)),
                  pl.BlockSpec((SGU_RB, D), lambda i: (i, 0)),
                  pl.BlockSpec((1, D), lambda i: (0, 0)), pl.BlockSpec((1, D), lambda i: (0, 0)),
                  pl.BlockSpec((SGU_G, SGU_BLOCK, SGU_BLOCK), const3),
                  pl.BlockSpec((SGU_G, SGU_BLOCK, SGU_BLOCK), const3),
                  pl.BlockSpec((SGU_G, SGU_BLOCK, SGU_GD), const3)],
        out_specs=[pl.BlockSpec((SGU_RB, SGU_COLS), lambda i: (i, 0)),
                   pl.BlockSpec((1, D), lambda i: (0, 0)), pl.BlockSpec((1, D), lambda i: (0, 0)),
                   pl.BlockSpec((SGU_G, SGU_BLOCK, SGU_BLOCK), const3),
                   pl.BlockSpec((SGU_BLOCK, SGU_BLOCK), lambda i: (0, 0))],
        out_shape=[_sds((S, SGU_COLS), BF16), _sds((1, D), F32), _sds((1, D), F32),
                   _sds((SGU_G, SGU_BLOCK, SGU_BLOCK), F32), _sds((SGU_BLOCK, SGU_BLOCK), F32)],
        args=[proj, dpre, lng, lnb, ws, wst, bsb],
        scratch_shapes=[pltpu.VMEM((SGU_BLOCK, D), F32), pltpu.VMEM((SGU_BLOCK, D), F32)],
        block_bytes=blocks, scratch_bytes=2 * SGU_BLOCK * D * 4, phases=phases)


def _pair_sum(own, a, r0, nr, name, table=None):
    c = own.shape[2]
    tr = 256
    assert r0 % tr == 0 and nr % tr == 0

    def body(own_ref, sib_ref, o_ref):
        o_ref[...] = (own_ref[...].astype(F32) + sib_ref[...].astype(F32)).astype(BF16)

    own_map = ((lambda j, i: (1 + j, r0 // tr + i, 0)) if table is None else
               (lambda j, i, t: (t[1 + j], r0 // tr + i, 0)))
    cpad = -(-c // 128) * 128
    outs, _ = _hosted(
        body, name=name, grid=(3, nr // tr),
        in_specs=[pl.BlockSpec((None, tr, c), own_map),
                  pl.BlockSpec((None, tr, c), lambda j, i, *t: (1 + j, r0 // tr + i, 0))],
        out_specs=[pl.BlockSpec((None, tr, c), lambda j, i, *t: (j, i, 0))], out_shape=[_sds((3, nr, c), BF16)],
        args=[own, a], block_bytes=3 * tr * cpad * 2, table=table)
    return outs[0]


def _adamw_math(w, g, m, v):
    m = ADAM_B1 * m + (1.0 - ADAM_B1) * g
    v = ADAM_B2 * v + (1.0 - ADAM_B2) * (g * g)
    m_hat = m / (1.0 - ADAM_B1 ** ADAM_STEP)
    v_hat = v / (1.0 - ADAM_B2 ** ADAM_STEP)
    delta = -ADAM_LR * (m_hat / (jnp.sqrt(v_hat) + ADAM_EPS) + ADAM_WD * w)
    return delta, m, v


def _sum_adamw(own, a, b, w, m, v, *, name, phases=(), table=None):
    r, c = w.shape
    tr = 256

    def body(own_ref, sib_ref, far_ref, w_ref, m_ref, v_ref, g_ref, d_ref, nm_ref, nv_ref):
        g = own_ref[...].astype(F32) + sib_ref[...].astype(F32)
        for j in range(3):
            g = g + far_ref[j].astype(F32)
        g_ref[...] = g
        d_ref[...], nm_ref[...], nv_ref[...] = _adamw_math(w_ref[...], g, m_ref[...], v_ref[...])

    spec = pl.BlockSpec((tr, c), lambda i, *t: (i, 0))
    own_map = (lambda i: (0, i, 0)) if table is None else (lambda i, t: (t[0], i, 0))
    cpad = -(-c // 128) * 128
    return _hosted(
        body, name=name, grid=(r // tr,),
        in_specs=[pl.BlockSpec((None, tr, c), own_map), pl.BlockSpec((None, tr, c), lambda i, *t: (0, i, 0)),
                  pl.BlockSpec((3, tr, c), lambda i, *t: (0, i, 0)), spec, spec, spec],
        out_specs=[spec] * 4, out_shape=[_sds((r, c), F32)] * 4, args=[own, a, b, w, m, v],
        block_bytes=5 * tr * cpad * 2 + 7 * tr * cpad * 4, phases=phases, table=table)


def _sum_parts(parts, name):
    n, r, c = parts.shape

    def body(p_ref, o_ref):
        g = p_ref[0]
        for j in range(1, n):
            g = g + p_ref[j]
        o_ref[...] = g

    outs, _ = _hosted(body, name=name, grid=(1,), in_specs=[pl.BlockSpec((n, r, c), lambda i: (0, 0, 0))],
                      out_specs=[pl.BlockSpec((r, c), lambda i: (0, 0))], out_shape=[_sds((r, c), F32)], args=[parts],
                      block_bytes=(n + 1) * r * c * 4)
    return outs[0]


def _adamw(w, g, m, v, name):
    def body(w_ref, g_ref, m_ref, v_ref, d_ref, nm_ref, nv_ref):
        d_ref[...], nm_ref[...], nv_ref[...] = _adamw_math(w_ref[...], g_ref[...], m_ref[...], v_ref[...])

    spec = pl.BlockSpec(w.shape, lambda i: (0, 0))
    outs, _ = _hosted(body, name=name, grid=(1,), in_specs=[spec] * 4, out_specs=[spec] * 3, out_shape=[_sds(w.shape, F32)] * 3,
                      args=[w, g, m, v], block_bytes=7 * _nbytes(w.shape, F32))
    return outs


def _blocks_to_columns(g):
    n, r, c = g.shape
    return jnp.transpose(g, (1, 0, 2)).reshape(r, n * c)


def _pack(parts):
    return jnp.concatenate([p.reshape(-1) for p in parts]).reshape(-1, 128)


def _unpack(packed, like):
    flat, outs, off = packed.reshape(-1), [], 0
    for p in like:
        outs.append(flat[off:off + p.size].reshape(p.shape))
        off += p.size
    return outs


def kernel(x, norm_pre, norm_post, gla_w_in, gla_w_gate2, gla_b_gate, gla_o_gain, gla_w_out, sgu_w_in, sgu_ln_gain, sgu_ln_bias, sgu_w_spatial, sgu_b_spatial, sgu_w_out, loss_target, m_norm_pre, m_norm_post, m_gla_w_in, m_gla_w_gate2, m_gla_b_gate, m_gla_o_gain, m_gla_w_out, m_sgu_w_in, m_sgu_ln_gain, m_sgu_ln_bias, m_sgu_w_spatial, m_sgu_b_spatial, m_sgu_w_out, v_norm_pre, v_norm_post, v_gla_w_in, v_gla_w_gate2, v_gla_b_gate, v_gla_o_gain, v_gla_w_out, v_sgu_w_in, v_sgu_ln_gain, v_sgu_ln_bias, v_sgu_w_spatial, v_sgu_b_spatial, v_sgu_w_out):
    me = _index_of(*_place())
    x0 = x.reshape(S, D)
    tgt = loss_target.reshape(S, D)
    npre0, npre1 = norm_pre[0:1], norm_pre[1:2]
    npost0, npost1 = norm_post[0:1], norm_post[1:2]
    ws = sgu_w_spatial[0]
    wst = jnp.transpose(ws, (0, 2, 1))
    bsb = jnp.broadcast_to(sgu_b_spatial[0][:, :, None], (SGU_G, SGU_BLOCK, SGU_GD))
    W_ROWS = D // N_DEV
    IN_COLS_G, IN_COLS_S = GLA_COLS // N_DEV, SGU_COLS // N_DEV

    s_gwi, s_gwo = gla_w_in[0].astype(BF16), gla_w_out[0].astype(BF16)
    s_swi, s_swo = sgu_w_in[0].astype(BF16), sgu_w_out[0].astype(BF16)
    small = jnp.concatenate([jnp.pad(gla_w_gate2[0].reshape(4, 512), ((0, 4), (0, 0))),
                             jnp.pad(jnp.concatenate([sgu_ln_gain, sgu_ln_bias], axis=1), ((0, 7), (0, 0)))], axis=0)

    wg_in, g_small = _gather_first(s_gwi, small, "gather_first")
    w2 =_blocks_to_columns(g_small[:, :4, :].reshape(N_DEV, LR, 128))
    w2p = jnp.pad(w2, ((0, LRP - LR), (0, 0))).astype(BF16)
    lng = g_small[:, 8, :256].reshape(1, D)
    lnb = g_small[:, 8, 256:].reshape(1, D)
    like_gwo, like_swi = _sds((N_DEV, W_ROWS, D), BF16), _sds((N_DEV, D, IN_COLS_S), BF16)

    h0 = _prenorm(x0, npre0)
    proj0, (g_gwo, g_swi) = _mm(h0, wg_in, "nn", F32, tm=1024, tn=896, tk=D, name="gla_in", b_tiled=True, phases=[
        _Phase(like_gwo, None, [_gather_send(s_gwo, 0, W_ROWS)]),
        _Phase(like_swi, None, [_gather_send(s_swi, 0, 256)])])
    (ypre0, states), (g_gwo, g_swi) = _gla_fwd(proj0, w2p, gla_b_gate, gla_o_gain, phases=[
        _Phase(like_gwo, g_gwo, [_gather_pass(0, W_ROWS)]),
        _Phase(like_swi, g_swi, [_gather_send(s_swi, 256, 1280), _gather_pass(0, 256)])])
    wg_out = g_gwo.reshape(D, D)
    y0, (g_swi,) = _mm(ypre0, wg_out, "nn", F32, tm=1024, tn=1024, tk=D, name="gla_out", phases=[
        _Phase(like_swi, g_swi, [_gather_send(s_swi, 1536, 512), _gather_pass(256, 1280)])])
    (x1, h1), (g_swi,) = _mid_fwd(x0, y0, npost0, npre1, phases=[_Phase(like_swi, g_swi, [_gather_pass(1536, 512)])])
    proj1, (g_swo,) = _mm(h1, g_swi, "nn", F32, tm=1024, tn=IN_COLS_S, tk=D, name="sgu_in", b_blocked=True, phases=[
        _Phase(like_gwo, None, [_gather_send(s_swo, 0, W_ROWS)])])
    (pre1,), (g_swo,) = _sgu_fwd(proj1, lng, lnb, ws, bsb, phases=[_Phase(like_gwo, g_swo, [_gather_pass(0, W_ROWS)])])
    ws_out = g_swo.reshape(D, D)
    y1, _ = _mm(pre1, ws_out, "nn", F32, tm=1024, tn=1024, tk=D, name="sgu_out")
    loss_cols, dx2, dy1, dnpost1 = _final(x1, y1, tgt, npost1)
    loss = lax.psum(0.5 * jnp.sum(loss_cols) / D, ("x", "y", "c"))

    like_b_out, like_b_swi = _sds((3, W_ROWS, D), BF16), _sds((3, D, IN_COLS_S), BF16)
    like_b_gwi = _sds((3, D, IN_COLS_G), BF16)
    row_pair = dict(like=_sds((4, W_ROWS, D), BF16), block=lambda i, j: i, ordinal=lambda i, j: i >> 1,
                    dst=lambda ref, k, i, j: ref.at[k])
    col_pair = dict(like=_sds((4, D, IN_COLS_S), BF16), block=lambda i, j: j, ordinal=lambda i, j: 4 * i + (j >> 1),
                    dst=lambda ref, k, i, j: ref.at[k, pl.ds(pl.multiple_of(i * 1024, 1024), 1024)])

    mine = _own_table()
    dws_out, (a_swo,) = _mm(pre1, dy1, "tn", BF16, tm=W_ROWS, tn=D, tk=S, name="sgu_out_dw", pair=row_pair)
    p_swo = dws_out.reshape(N_DEV, W_ROWS, D)
    t_swo = _pair_sum(p_swo, a_swo, 0, W_ROWS, "pair_sum_sgu_w_out", table=mine)
    dpre1, _ = _mm(dy1, ws_out, "nt", F32, tm=1024, tn=1024, tk=D, name="sgu_out_dx")
    (dproj1, dlng, dlnb, dwsp, dbsp), (b_swo,) = _sgu_bwd(proj1, dpre1, lng, lnb, ws, wst, bsb, phases=[
        _Phase(like_b_out, None, [_reduce_cross(t_swo, 0, 0, W_ROWS)])])
    p_swi, (a_swi,) = _mm(h1, dproj1, "tn", BF16, tm=1024, tn=IN_COLS_S, tk=S, name="sgu_in_dw", out_blocked=True, pair=col_pair)
    t_swi = _pair_sum(p_swi, a_swi, 0, D, "pair_sum_sgu_w_in", table=mine)
    dh1, (b_swi,) = _mm(dproj1, g_swi, "nt", F32, tm=1024, tn=1024, tk=IN_COLS_S, name="sgu_in_dx", b_blocked=True, phases=[
        _Phase(like_b_swi, None, [_reduce_cross(t_swi, 0, 0, 1024)])])
    (dx1, dy0, dnpre1, dnpost0), (b_swi,) = _mid_bwd(dx2, dh1, x1, y0, npre1, npost0, phases=[
        _Phase(like_b_swi, b_swi, [_reduce_cross(t_swi, 1024, 1024, 512)])])
    dwg_out, (a_gwo, b_swi) = _mm(ypre0, dy0, "tn", BF16, tm=W_ROWS, tn=D, tk=S, name="gla_out_dw", pair=row_pair, phases=[
        _Phase(like_b_swi, b_swi, [_reduce_cross(t_swi, 1536, 1536, 512)])])
    p_gwo = dwg_out.reshape(N_DEV, W_ROWS, D)
    t_gwo = _pair_sum(p_gwo, a_gwo, 0, W_ROWS, "pair_sum_gla_w_out", table=mine)
    dypre0, _ = _mm(dy0, wg_out, "nt", F32, tm=1024, tn=1024, tk=D, name="gla_out_dx")
    late = [dnpre1, dnpost1, dlng, dlnb, dwsp, jnp.transpose(dbsp[:, :SGU_G])]
    late_pack = _pack(late)
    (dproj0, dogain, dbgate, dw2), (b_gwo, g_late) = _gla_bwd(proj0, dypre0, states, w2p, gla_b_gate, gla_o_gain, phases=[
        _Phase(like_b_out, None, [_reduce_cross(t_gwo, 0, 0, W_ROWS)]),
        _Phase(_sds((N_DEV,) + late_pack.shape, F32), None, [_gather_send(late_pack, 0, late_pack.shape[0])])])
    half = D // 2
    dwg_in_a, (g_late,) = _mm(h0, dproj0, "tn", BF16, tm=half, tn=896, tk=S, name="gla_in_dw_a", m_tiles=(0, 1), phases=[
        _Phase(_sds((N_DEV,) + late_pack.shape, F32), g_late, [_gather_pass(0, late_pack.shape[0])])])
    own_gwi, a_gwi = _blockify_pair(dwg_in_a, None, None, 0, "blockify_gla_w_in_a")
    t_gwi_a = _pair_sum(own_gwi, a_gwi, 0, half, "pair_sum_gla_w_in_a")
    dwg_in_b, (b_gwi,) = _mm(h0, dproj0, "tn", BF16, tm=half, tn=896, tk=S, name="gla_in_dw_b", m_tiles=(1, 1), phases=[
        _Phase(like_b_gwi, None, [_reduce_cross(t_gwi_a, 0, 0, 512)])])
    own_gwi, a_gwi = _blockify_pair(dwg_in_b, own_gwi, a_gwi, half, "blockify_gla_w_in_b")
    t_gwi_b = _pair_sum(own_gwi, a_gwi, half, half, "pair_sum_gla_w_in_b")
    dh0, (b_gwi,) = _mm(dproj0, wg_in, "nt", F32, tm=1024, tn=1024, tk=896, name="gla_in_dx", b_tiled=True, phases=[
        _Phase(like_b_gwi, b_gwi, [_reduce_cross(t_gwi_a, 512, 512, 512), _reduce_cross(t_gwi_b, 0, half, 512)])])
    (grad_x, dnpre0), (b_gwi,) = _first_bwd(dx1, dh0, x0, npre0, phases=[
        _Phase(like_b_gwi, b_gwi, [_reduce_cross(t_gwi_b, 512, half + 512, 256)])])

    early = [dnpre0, dnpost0, dbgate, dogain, dw2[:LR]]
    early_pack = _pack(early)
    like_early = _sds((N_DEV,) + early_pack.shape, F32)
    (g_swo, d_swo, nm_swo, nv_swo), (b_gwi, g_early) = _sum_adamw(
        p_swo, a_swo, b_swo, sgu_w_out[0], m_sgu_w_out[0], v_sgu_w_out[0], name="adamw_sgu_w_out", table=mine, phases=[
            _Phase(like_b_gwi, b_gwi, [_reduce_cross(t_gwi_b, 768, half + 768, 256)]),
            _Phase(like_early, None, [_gather_send(early_pack, 0, early_pack.shape[0])])])
    (g_swi_, d_swi, nm_swi, nv_swi), (g_early,) = _sum_adamw(
        p_swi, a_swi, b_swi, sgu_w_in[0], m_sgu_w_in[0], v_sgu_w_in[0], name="adamw_sgu_w_in", table=mine, phases=[
            _Phase(like_early, g_early, [_gather_pass(0, early_pack.shape[0])])])
    (g_gwo_, d_gwo, nm_gwo, nv_gwo), _ = _sum_adamw(
        p_gwo, a_gwo, b_gwo, gla_w_out[0], m_gla_w_out[0], v_gla_w_out[0], name="adamw_gla_w_out", table=mine)
    (g_gwi_, d_gwi, nm_gwi, nv_gwi), _ = _sum_adamw(
        own_gwi, a_gwi, b_gwi, gla_w_in[0], m_gla_w_in[0], v_gla_w_in[0], name="adamw_gla_w_in")

    g_npre1, g_npost1, g_lng_full, g_lnb_full, g_wsp, g_bsp = _unpack(_sum_parts(g_late, "sum_late_small_grads"), late)
    g_npre0, g_npost0, g_bgate, g_ogain, g_w2_full = _unpack(_sum_parts(g_early, "sum_early_small_grads"), early)
    g_w2 = lax.dynamic_slice(g_w2_full, (0, me * 128), (LR, 128))
    g_lng = lax.dynamic_slice(g_lng_full, (0, me * 256), (1, 256))
    g_lnb = lax.dynamic_slice(g_lnb_full, (0, me * 256), (1, 256))
    small_g = [jnp.concatenate([g_npre0, g_npre1], 0), jnp.concatenate([g_npost0, g_npost1], 0), g_w2, g_bgate, g_ogain,
               g_lng, g_lnb, g_wsp, g_bsp]
    small_w = [norm_pre, norm_post, gla_w_gate2[0], gla_b_gate, gla_o_gain, sgu_ln_gain, sgu_ln_bias, sgu_w_spatial[0], sgu_b_spatial[0]]
    small_m = [m_norm_pre, m_norm_post, m_gla_w_gate2[0], m_gla_b_gate, m_gla_o_gain, m_sgu_ln_gain, m_sgu_ln_bias, m_sgu_w_spatial[0], m_sgu_b_spatial[0]]
    small_v = [v_norm_pre, v_norm_post, v_gla_w_gate2[0], v_gla_b_gate, v_gla_o_gain, v_sgu_ln_gain, v_sgu_ln_bias, v_sgu_w_spatial[0], v_sgu_b_spatial[0]]
    d_pack, nm_pack, nv_pack = _adamw(_pack(small_w), _pack(small_g), _pack(small_m), _pack(small_v), "adamw_small")

    out_like = [norm_pre, norm_post, gla_w_gate2, gla_b_gate, gla_o_gain, sgu_ln_gain, sgu_ln_bias, sgu_w_spatial, sgu_b_spatial]
    sg_ = [g.reshape(s.shape) for g, s in zip(small_g, out_like)]
    sd_, sm_, sv_ = (_unpack(pk, out_like) for pk in (d_pack, nm_pack, nv_pack))

    def assemble(small_list, w_in_g, w_out_g, w_in_s, w_out_s):
        npre_, npost_, w2_, bg_, og_, lg_, lb_, wsp_, bsp_ = small_list
        return [npre_, npost_, w_in_g[None], w2_, bg_, og_, w_out_g[None], w_in_s[None], lg_, lb_, wsp_, bsp_, w_out_s[None]]

    return (loss, grad_x.reshape(1, S, D),
            *assemble(sg_, g_gwi_, g_gwo_, g_swi_, g_swo),
            *assemble(sd_, d_gwi, d_gwo, d_swi, d_swo),
            *assemble(sm_, nm_gwi, nm_gwo, nm_swi, nm_swo),
            *assemble(sv_, nv_gwi, nv_gwo, nv_swi, nv_swo))
```

```python
import functools

import jax
import jax.numpy as jnp
from jax import lax
from jax.experimental import pallas as pl
from jax.experimental.pallas import tpu as pltpu

F32 = jnp.float32
BF16 = jnp.bfloat16

N_DEV = 8
S = 2048
D = 2048
H = 4
DK = 256
DV = 512
C = 64
NC = S // C
GLA_COLS = 6160
GLA_PAD = 6272
Q0, K0, V0, G0, LR0 = 0, 1024, 2048, 4096, 6144
LR = 16
LRP = 128
SGU_COLS = 6144
SGU_BLOCK = 128
SGU_G = 8
SGU_GD = 256
EPS = 1e-6
GLA_TAU = 16.0

ADAM_LR, ADAM_B1, ADAM_B2, ADAM_EPS, ADAM_WD, ADAM_STEP = 0.001, 0.9, 0.999, 1e-08, 0.01, 10

V7X_VMEM_BYTES = 64 * 1024 * 1024
VMEM_CEILING = V7X_VMEM_BYTES - 6 * 1024 * 1024
MESH = pl.DeviceIdType.MESH
HBM_SPEC = pl.BlockSpec(memory_space=pl.ANY)


def _sds(shape, dtype):
    return jax.ShapeDtypeStruct(tuple(shape), dtype)


def _nbytes(shape, dtype):
    n = 1
    for s in shape:
        n *= s
    return n * jnp.dtype(dtype).itemsize


def _dot(a, b, dims=(((1,), (0,)), ((), ())), precision=None):
    return lax.dot_general(a, b, dims, precision=precision, preferred_element_type=F32)


NN = (((1,), (0,)), ((), ()))
TN = (((0,), (0,)), ((), ()))
NT = (((1,), (1,)), ((), ()))


def _place():
    return lax.axis_index("x"), lax.axis_index("y"), lax.axis_index("c")


def _index_of(px, py, pc):
    return 4 * px + 2 * py + pc


def _chips(x, y):
    return [(1 - x, y), (x, 1 - y), (1 - x, 1 - y)]


def _rcopy(src, dst, send_sem, recv_sem, to):
    return pltpu.make_async_remote_copy(src_ref=src, dst_ref=dst, send_sem=send_sem, recv_sem=recv_sem,
                                        device_id=to, device_id_type=MESH)


class _Move:
    def __init__(self, ins, n_remote, make, stage=None):
        self.ins, self.n_remote, self.make, self.stage = list(ins), n_remote, make, stage

    def scratch(self):
        sems = [pltpu.SemaphoreType.DMA((self.n_remote,)), pltpu.SemaphoreType.DMA((self.n_remote,))]
        return sems if self.stage is None else sems + [pltpu.SemaphoreType.DMA((1,)), pltpu.VMEM(*self.stage)]

    def start(self, in_refs, buf, scratch):
        sends, _, local = self.make(in_refs, buf, scratch[0], scratch[1])
        if local is not None:
            pltpu.make_async_copy(local[0], scratch[3], scratch[2].at[0]).start()
        for cp in sends:
            cp.start()

    def finish(self, in_refs, buf, scratch):
        sends, arrivals, local = self.make(in_refs, buf, scratch[0], scratch[1])
        if local is not None:
            pltpu.make_async_copy(local[0], scratch[3], scratch[2].at[0]).wait()
            out = pltpu.make_async_copy(scratch[3], local[1], scratch[2].at[0])
            out.start()
        for cp in arrivals:
            cp.wait_recv()
        for cp in sends:
            cp.wait_send()
        if local is not None:
            out.wait()


class _Phase:
    def __init__(self, like, so_far, moves):
        self.like, self.so_far, self.moves = like, so_far, list(moves)


def _gather_send(shard, r0, nr):
    def make(in_refs, g, ss, rs):
        sh, = in_refs
        x, y, c = _place()
        me = _index_of(x, y, c)
        rows = pl.ds(r0, nr)
        peers = [(x, y, 1 - c)] + [(px, py, c) for px, py in _chips(x, y)]
        sends = [_rcopy(sh.at[rows], g.at[me, rows], ss.at[k], rs.at[k], p) for k, p in enumerate(peers)]
        arrivals = [_rcopy(sh.at[rows], g.at[_index_of(*p), rows], ss.at[k], rs.at[k], p) for k, p in enumerate(peers)]
        return sends, arrivals, (sh.at[rows], g.at[me, rows])

    return _Move([shard], 4, make, stage=((nr, shard.shape[1]), shard.dtype))


def _gather_pass(r0, nr):
    def make(in_refs, g, ss, rs):
        x, y, c = _place()
        rows = pl.ds(r0, nr)
        sends = [_rcopy(g.at[_index_of(px, py, c), rows], g.at[_index_of(px, py, c), rows], ss.at[j], rs.at[j], (x, y, 1 - c))
                 for j, (px, py) in enumerate(_chips(x, y))]
        arrivals = [_rcopy(g.at[_index_of(px, py, c), rows], g.at[_index_of(px, py, 1 - c), rows], ss.at[j], rs.at[j], (x, y, 1 - c))
                    for j, (px, py) in enumerate(_chips(x, y))]
        return sends, arrivals, None

    return _Move([], 3, make)


def _own_table():
    x, y, c = _place()
    return jnp.stack([_index_of(px, py, c) for px, py in [(x, y)] + _chips(x, y)]).astype(jnp.int32)


def _blockify_pair(dw, own_so_far, a_so_far, dst_r0, name):
    rows, tr, cw, win = dw.shape[0], 256, GLA_COLS // N_DEV, 896
    n_steps = rows // tr

    def body(*refs):
        x_ref, own_ref, a_ref, stage_ref, send_sems, recv_sem = refs[0], *refs[-5:]
        i = pl.program_id(0)
        x, y, c = _place()

        def send(slot, k):
            dst = a_ref.at[k, pl.ds(pl.multiple_of(dst_r0 + i * tr, tr), tr)]
            return _rcopy(stage_ref.at[slot], dst, send_sems.at[slot], recv_sem.at[0], (x, y, 1 - c))

        for j in range(N_DEV):
            window = x_ref[:, 768 * j:768 * j + win].astype(F32)
            tile = (pltpu.roll(window, win - 2 * j, 1) if j else window)[:, :cw].astype(BF16)
            k = ((j >> 2) ^ x) + 2 * (((j >> 1) & 1) ^ y)

            @pl.when((j & 1) == c)
            def _():
                own_ref[k] = tile

            @pl.when((j & 1) != c)
            def _():
                slot = (j >> 1) & 1
                if j >> 1 >= 2:
                    send(slot, k).wait_send()
                else:
                    pl.when(i > 0)(lambda: send(slot, k).wait_send())
                stage_ref[slot] = tile
                send(slot, k).start()

        @pl.when(i == n_steps - 1)
        def _():
            send(0, 0).wait_send()
            send(1, 0).wait_send()
            arrived = a_ref.at[:, pl.ds(dst_r0, rows)]
            _rcopy(arrived, arrived, send_sems.at[0], recv_sem.at[0], (x, y, 1 - c)).wait_recv()

    continues = a_so_far is not None
    own, a = pl.pallas_call(
        body, grid=(n_steps,),
        in_specs=[pl.BlockSpec((tr, GLA_PAD), lambda i: (i, 0))] + [HBM_SPEC] * (2 * continues),
        out_specs=[pl.BlockSpec((4, tr, cw), lambda i: (0, dst_r0 // tr + i, 0)), HBM_SPEC],
        out_shape=[_sds((4, D, cw), BF16), _sds((4, D, cw), BF16)],
        scratch_shapes=[pltpu.VMEM((2, tr, cw), BF16), pltpu.SemaphoreType.DMA((2,)), pltpu.SemaphoreType.DMA((1,))],
        input_output_aliases={1: 0, 2: 1} if continues else {},
        compiler_params=pltpu.CompilerParams(dimension_semantics=("arbitrary",), vmem_limit_bytes=48 * 1024 * 1024),
        name=name,
    )(*([dw] + [own_so_far, a_so_far] * continues))
    return own, a


def _reduce_cross(sums, src_r0, dst_r0, nr):
    def make(in_refs, b, ss, rs):
        t, = in_refs
        x, y, c = _place()
        src, dst = pl.ds(src_r0, nr), pl.ds(dst_r0, nr)
        sends = [_rcopy(t.at[j, src], b.at[j, dst], ss.at[j], rs.at[j], (px, py, c)) for j, (px, py) in enumerate(_chips(x, y))]
        return sends, sends, None

    return _Move([sums], 3, make)


def _hosted(body, *, name, grid, in_specs, out_specs, out_shape, args, scratch_shapes=(), block_bytes, scratch_bytes=0,
            phases=(), table=None):
    n_in, n_out, n_scr = len(args), len(out_shape), len(scratch_shapes)
    all_args, all_out_shape, sems, aliases, layout = list(args), list(out_shape), [], {}, []
    for j, ph in enumerate(phases):
        counts = []
        for mv in ph.moves:
            all_args += mv.ins
            counts.append(len(mv.ins))
            sems += mv.scratch()
        if ph.so_far is not None:
            aliases[len(all_args)] = n_out + j
            all_args.append(ph.so_far)
        layout.append((counts, ph.so_far is not None))
        all_out_shape.append(ph.like)
    n_extra_in = len(all_args) - n_in

    def wrapped(*refs):
        ins, pos = refs[:n_in], n_in
        move_ins = []
        for counts, continues in layout:
            per_move = []
            for cnt in counts:
                per_move.append(refs[pos:pos + cnt])
                pos += cnt
            pos += continues
            move_ins.append(per_move)
        outs = refs[pos:pos + n_out]
        bufs = refs[pos + n_out:pos + n_out + len(phases)]
        pos += n_out + len(phases)
        scratch = refs[pos:pos + n_scr]
        pos += n_scr
        move_sems = []
        for ph in phases:
            per_move = []
            for mv in ph.moves:
                count = len(mv.scratch())
                per_move.append(refs[pos:pos + count])
                pos += count
            move_sems.append(per_move)

        def each_move(fn_name):
            for ph, buf, per_in, per_sem in zip(phases, bufs, move_ins, move_sems):
                for mv, mv_in, mv_sem in zip(ph.moves, per_in, per_sem):
                    getattr(mv, fn_name)(mv_in, buf, mv_sem)

        if phases:
            first = functools.reduce(jnp.logical_and, [pl.program_id(a) == 0 for a in range(len(grid))])
            last = functools.reduce(jnp.logical_and, [pl.program_id(a) == grid[a] - 1 for a in range(len(grid))])
            pl.when(first)(lambda: each_move("start"))
        body(*ins, *outs, *scratch)
        if phases:
            pl.when(last)(lambda: each_move("finish"))

    est = 2 * block_bytes + scratch_bytes
    params = pltpu.CompilerParams(dimension_semantics=("arbitrary",) * len(grid),
                                  vmem_limit_bytes=min(VMEM_CEILING, max(32 * 1024 * 1024, 2 * est)))
    all_in_specs, all_out_specs = list(in_specs) + [HBM_SPEC] * n_extra_in, list(out_specs) + [HBM_SPEC] * len(phases)
    if table is None:
        results = pl.pallas_call(
            wrapped, grid=grid, in_specs=all_in_specs, out_specs=all_out_specs, out_shape=all_out_shape,
            scratch_shapes=list(scratch_shapes) + sems, input_output_aliases=aliases, compiler_params=params, name=name,
        )(*all_args)
    else:
        results = pl.pallas_call(
            lambda table_ref, *refs: wrapped(*refs),
            grid_spec=pltpu.PrefetchScalarGridSpec(num_scalar_prefetch=1, grid=grid, in_specs=all_in_specs, out_specs=all_out_specs,
                                                   scratch_shapes=list(scratch_shapes) + sems),
            out_shape=all_out_shape, input_output_aliases={k + 1: v for k, v in aliases.items()}, compiler_params=params, name=name,
        )(table, *all_args)
    return list(results[:n_out]), list(results[n_out:])


def _carry(phases, name):
    def body(o_ref):
        o_ref[...] = jnp.zeros_like(o_ref)

    _, bufs = _hosted(body, name=name, grid=(1,), in_specs=[], out_specs=[pl.BlockSpec((8, 128), lambda i: (0, 0))],
                      out_shape=[_sds((8, 128), F32)], args=[], block_bytes=8 * 128 * 4, phases=phases)
    return bufs


def _gather_first(shard, small, name):
    cw, tr, n_tiles = shard.shape[1], 256, GLA_PAD // 128

    def body(sh_ref, sm_ref, wn_ref, g_ref, gs_ref, wt_ref, win_ref, tmp_ref, send_sems, recv_sems, local_sems):
        x, y, c = _place()
        me, sibling = (x, y, c), (x, y, 1 - c)
        chips = _chips(x, y)

        def copy(base, out_ref, k, block, to, src=None):
            dst = out_ref.at[_index_of(*block)]
            return _rcopy(dst if src is None else src, dst, send_sems.at[base + k], recv_sems.at[base + k], to)

        wcopy = functools.partial(copy, 0, g_ref)
        scopy = functools.partial(copy, 7, gs_ref)

        def load(src_ref, slot):
            cp = pltpu.make_async_copy(src_ref, win_ref.at[slot], local_sems.at[0])
            cp.start()
            cp.wait()

        def place(slot, block):
            b = _index_of(*block)

            def rows_chunk(r, carry):
                rows = pl.ds(pl.multiple_of(r * tr, tr), tr)
                tmp_ref[:, :cw] = win_ref[slot, rows, :].astype(F32)
                shifted = pltpu.roll(tmp_ref[...], 2 * b, 1)
                for u in range(7):
                    wt_ref[6 * b + u, rows, :] = (wt_ref[6 * b + u, rows, :].astype(F32) + shifted[:, 128 * u:128 * (u + 1)]).astype(BF16)
                return carry

            lax.fori_loop(0, D // tr, rows_chunk, 0)

        small_own = pltpu.make_async_copy(sm_ref, gs_ref.at[_index_of(*me)], local_sems.at[1])
        small_own.start()
        first = [wcopy(1 + j, me, (*chip, c), src=sh_ref) for j, chip in enumerate(chips)]
        first += [scopy(0, me, sibling, src=sm_ref)] + [scopy(1 + j, me, (*chip, c), src=sm_ref) for j, chip in enumerate(chips)]
        for cp in first:
            cp.start()

        def clear(t, carry):
            wt_ref[t] = jnp.zeros((D, 128), BF16)
            return carry

        lax.fori_loop(0, n_tiles, clear, 0)
        tmp_ref[...] = jnp.zeros_like(tmp_ref)

        def emit(t):
            pltpu.make_async_copy(wt_ref.at[t], wn_ref.at[t], local_sems.at[2]).start()

        def take(block, slot, arrival=None, pass_on=None):
            if arrival is not None:
                arrival.wait_recv()
            load(sh_ref if arrival is None else g_ref.at[_index_of(*block)], slot)
            if pass_on is not None:
                pass_on.start()
            place(slot, block)
            for u in range(1, 6):
                emit(6 * _index_of(*block) + u)

        to_sibling = wcopy(0, me, sibling, src=win_ref.at[0])
        pass_x = wcopy(4, (*chips[0], c), sibling, src=win_ref.at[1])
        pass_y = wcopy(5, (*chips[1], c), sibling, src=win_ref.at[0])
        pass_d = wcopy(6, (*chips[2], c), sibling, src=win_ref.at[0])
        take(me, 0, pass_on=to_sibling)
        take((*chips[0], c), 1, wcopy(1, (*chips[0], c), me), pass_x)
        to_sibling.wait_send()
        take((*chips[1], c), 0, wcopy(2, (*chips[1], c), me), pass_y)
        small_passed = []
        for j, chip in enumerate(chips):
            scopy(1 + j, (*chip, c), me).wait_recv()
            cp = scopy(4 + j, (*chip, c), sibling)
            cp.start()
            small_passed.append(cp)
        pass_x.wait_send()
        take(sibling, 1, wcopy(0, sibling, me))
        pass_y.wait_send()
        take((*chips[0], 1 - c), 0, wcopy(4, (*chips[0], 1 - c), me))
        take((*chips[1], 1 - c), 1, wcopy(5, (*chips[1], 1 - c), me))
        take((*chips[2], c), 0, wcopy(3, (*chips[2], c), me), pass_d)
        take((*chips[2], 1 - c), 1, wcopy(6, (*chips[2], 1 - c), me))
        for t in range(0, n_tiles, 6):
            emit(t)
        scopy(0, sibling, me).wait_recv()
        for j, chip in enumerate(chips):
            scopy(4 + j, (*chip, 1 - c), me).wait_recv()
        for cp in first + small_passed + [pass_d]:
            cp.wait_send()
        small_own.wait()
        pltpu.make_async_copy(wn_ref, wn_ref, local_sems.at[2]).wait()

    wn, _, gs = pl.pallas_call(
        body,
        in_specs=[HBM_SPEC] * 2, out_specs=[HBM_SPEC] * 3,
        out_shape=[_sds((n_tiles, D, 128), BF16), _sds((N_DEV,) + shard.shape, BF16), _sds((N_DEV,) + small.shape, small.dtype)],
        scratch_shapes=[pltpu.VMEM((n_tiles, D, 128), BF16), pltpu.VMEM((2, D, cw), BF16), pltpu.VMEM((tr, 7 * 128), F32),
                        pltpu.SemaphoreType.DMA((14,)), pltpu.SemaphoreType.DMA((14,)), pltpu.SemaphoreType.DMA((3,))],
        compiler_params=pltpu.CompilerParams(vmem_limit_bytes=48 * 1024 * 1024),
        name=name,
    )(shard, small)
    return wn, gs


def _mm(a, b, mode, out_dtype, *, tm, tn, tk, name, b_blocked=False, b_tiled=False, out_blocked=False, m_tiles=None, pair=None,
        phases=()):
    if mode == "nn":
        (m, k), dims = a.shape, NN
        a_blk, a_map = (tm, tk), (lambda i, j, kk: (i, kk))
        if b_blocked:
            assert b.shape[1] == k and b.shape[2] == tn and tk == k
            n = b.shape[0] * tn
            b_spec = pl.BlockSpec((None, tk, tn), lambda i, j, kk: (j, kk, 0))
        elif b_tiled:
            assert b.shape[1] == k and b.shape[2] == 128 and tn % 128 == 0
            n = b.shape[0] * 128
            b_spec = pl.BlockSpec((tn // 128, tk, 128), lambda i, j, kk: (j, kk, 0))
        else:
            assert b.shape[0] == k
            n = b.shape[1]
            b_spec = pl.BlockSpec((tk, tn), lambda i, j, kk: (kk, j))
    elif mode == "tn":
        (k, m), n, dims = a.shape, b.shape[1], TN
        assert b.shape[0] == k
        first = 0 if m_tiles is None else m_tiles[0]
        a_blk, a_map = (tk, tm), (lambda i, j, kk: (kk, i + first))
        b_spec = pl.BlockSpec((tk, tn), lambda i, j, kk: (kk, j))
    else:
        (m, k), dims = a.shape, NT
        a_blk, a_map = (tm, tk), (lambda i, j, kk: (i, kk))
        if b_blocked:
            assert b.shape[0] * b.shape[2] == k and b.shape[2] == tk
            n = b.shape[1]
            b_spec = pl.BlockSpec((None, tn, tk), lambda i, j, kk: (kk, j, 0))
        elif b_tiled:
            assert b.shape[0] * 128 == k and b.shape[2] == 128 and tk % 128 == 0
            n = b.shape[1]
            b_spec = pl.BlockSpec((tk // 128, tn, 128), lambda i, j, kk: (kk, j, 0))
        else:
            assert b.shape[1] == k
            n = b.shape[0]
            b_spec = pl.BlockSpec((tn, tk), lambda i, j, kk: (j, kk))
    assert m % tm == 0 and n % tn == 0 and k % tk == 0, (a.shape, b.shape, mode)
    nk = k // tk
    n_row_tiles = m // tm if m_tiles is None else m_tiles[1]
    if out_blocked:
        out_shape, out_spec = _sds((n // tn, n_row_tiles * tm, tn), out_dtype), pl.BlockSpec((None, tm, tn), lambda i, j, kk: (j, i, 0))
    else:
        out_shape, out_spec = _sds((n_row_tiles * tm, n), out_dtype), pl.BlockSpec((tm, tn), lambda i, j, kk: (i, j))

    grid = (n_row_tiles, n // tn, nk)

    def body(a_ref, b_ref, o_ref, *rest):
        rhs = jnp.concatenate([b_ref[u] for u in range(b_ref.shape[0])], axis=1) if b_tiled else b_ref[...]
        p = _dot(a_ref[...], rhs, dims)
        if nk == 1:
            o_ref[...] = p.astype(out_dtype)
            if pair is not None:
                _send_to_sibling(p.astype(out_dtype), *rest)
        else:
            acc_ref, = rest
            kk = pl.program_id(2)

            @pl.when(kk == 0)
            def _():
                acc_ref[...] = p

            @pl.when(kk > 0)
            def _():
                acc_ref[...] += p

            @pl.when(kk == nk - 1)
            def _():
                o_ref[...] = acc_ref[...].astype(out_dtype)

    def _send_to_sibling(tile, pair_ref, stage_ref, send_sems, recv_sem):
        i, j = pl.program_id(0), pl.program_id(1)
        x, y, c = _place()
        blk = pair["block"](i, j)
        k = ((blk >> 2) ^ x) + 2 * (((blk >> 1) & 1) ^ y)
        ordinal = pair["ordinal"](i, j)

        def send(slot):
            return _rcopy(stage_ref.at[slot], pair["dst"](pair_ref, k, i, j), send_sems.at[slot], recv_sem.at[0], (x, y, 1 - c))

        @pl.when((blk & 1) != c)
        def _():
            slot = ordinal & 1

            @pl.when(ordinal >= 2)
            def _():
                send(slot).wait_send()

            stage_ref[slot] = tile
            send(slot).start()

        @pl.when((i == grid[0] - 1) & (j == grid[1] - 1))
        def _():
            send(0).wait_send()
            send(1).wait_send()
            _rcopy(pair_ref, pair_ref, send_sems.at[0], recv_sem.at[0], (x, y, 1 - c)).wait_recv()

    blocks = _nbytes(a_blk, a.dtype) + tk * tn * jnp.dtype(b.dtype).itemsize + _nbytes((tm, tn), out_dtype)
    out_specs, out_shapes, scratch = [out_spec], [out_shape], [] if nk == 1 else [pltpu.VMEM((tm, tn), F32)]
    scratch_bytes = _nbytes((tm, tn), F32) * (nk > 1)
    if pair is not None:
        assert nk == 1
        out_specs, out_shapes = out_specs + [HBM_SPEC], out_shapes + [pair["like"]]
        scratch = [pltpu.VMEM((2, tm, tn), out_dtype), pltpu.SemaphoreType.DMA((2,)), pltpu.SemaphoreType.DMA((1,))]
        scratch_bytes = 2 * _nbytes((tm, tn), out_dtype)
    outs, bufs = _hosted(
        body, name=name, grid=grid,
        in_specs=[pl.BlockSpec(a_blk, a_map), b_spec], out_specs=out_specs, out_shape=out_shapes, args=[a, b],
        scratch_shapes=scratch, block_bytes=blocks, scratch_bytes=scratch_bytes, phases=phases)
    return outs[0], outs[1:] + bufs


RB = 256


def _row_spec(width):
    return pl.BlockSpec((RB, width), lambda i: (i, 0))


def _vec_spec(width):
    return pl.BlockSpec((1, width), lambda i: (0, 0))


def _rinv(x):
    return lax.rsqrt(jnp.mean(x * x, axis=-1, keepdims=True) + EPS)


def _norm_bwd(dyn, xhat, r):
    return r * (dyn - xhat * jnp.mean(dyn * xhat, axis=-1, keepdims=True))


def _colsum(x):
    return jnp.sum(x, axis=0, keepdims=True)


def _accumulate(ref, value):
    @pl.when(pl.program_id(0) == 0)
    def _():
        ref[...] = value

    @pl.when(pl.program_id(0) > 0)
    def _():
        ref[...] += value


def _prenorm(x, gain):
    def body(x_ref, g_ref, h_ref):
        xv = x_ref[...]
        h_ref[...] = (xv * _rinv(xv) * g_ref[...]).astype(BF16)

    outs, _ = _hosted(body, name="prenorm", grid=(S // RB,), in_specs=[_row_spec(D), _vec_spec(D)], out_specs=[_row_spec(D)],
                      out_shape=[_sds((S, D), BF16)], args=[x, gain], block_bytes=RB * D * 6)
    return outs[0]


def _mid_fwd(x, y, npost, npre, phases=()):
    def body(x_ref, y_ref, po_ref, pr_ref, x1_ref, h1_ref):
        yv = y_ref[...]
        x1 = x_ref[...] + yv * _rinv(yv) * po_ref[...]
        x1_ref[...] = x1
        h1_ref[...] = (x1 * _rinv(x1) * pr_ref[...]).astype(BF16)

    return _hosted(body, name="mid_fwd", grid=(S // RB,), in_specs=[_row_spec(D), _row_spec(D), _vec_spec(D), _vec_spec(D)],
                   out_specs=[_row_spec(D), _row_spec(D)], out_shape=[_sds((S, D), F32), _sds((S, D), BF16)],
                   args=[x, y, npost, npre], block_bytes=RB * D * 14, phases=phases)


def _final(x1, y1, tgt, npost):
    def body(x_ref, y_ref, t_ref, po_ref, loss_ref, dx_ref, dy_ref, dpo_ref):
        yv = y_ref[...]
        r = _rinv(yv)
        yhat = yv * r
        err = x_ref[...] + yhat * po_ref[...] - t_ref[...]
        dx = err * (1.0 / D)
        dx_ref[...] = dx
        dy_ref[...] = _norm_bwd(dx * po_ref[...], yhat, r).astype(BF16)
        _accumulate(loss_ref, _colsum(err * err))
        _accumulate(dpo_ref, _colsum(dx * yhat))

    outs, _ = _hosted(body, name="final", grid=(S // RB,), in_specs=[_row_spec(D), _row_spec(D), _row_spec(D), _vec_spec(D)],
                      out_specs=[_vec_spec(D), _row_spec(D), _row_spec(D), _vec_spec(D)],
                      out_shape=[_sds((1, D), F32), _sds((S, D), F32), _sds((S, D), BF16), _sds((1, D), F32)],
                      args=[x1, y1, tgt, npost], block_bytes=RB * D * 18)
    return outs


def _mid_bwd(dx2, dh1, x1, y0, npre, npost, phases=()):
    def body(dx2_ref, dh_ref, x_ref, y_ref, pr_ref, po_ref, dx1_ref, dy_ref, dpr_ref, dpo_ref):
        xv = x_ref[...]
        r = _rinv(xv)
        xhat = xv * r
        dh = dh_ref[...]
        dx1 = dx2_ref[...] + _norm_bwd(dh * pr_ref[...], xhat, r)
        dx1_ref[...] = dx1
        yv = y_ref[...]
        ry = _rinv(yv)
        yhat = yv * ry
        dy_ref[...] = _norm_bwd(dx1 * po_ref[...], yhat, ry).astype(BF16)
        _accumulate(dpr_ref, _colsum(dh * xhat))
        _accumulate(dpo_ref, _colsum(dx1 * yhat))

    return _hosted(body, name="mid_bwd", grid=(S // RB,), in_specs=[_row_spec(D)] * 4 + [_vec_spec(D)] * 2,
                   out_specs=[_row_spec(D), _row_spec(D), _vec_spec(D), _vec_spec(D)],
                   out_shape=[_sds((S, D), F32), _sds((S, D), BF16), _sds((1, D), F32), _sds((1, D), F32)],
                   args=[dx2, dh1, x1, y0, npre, npost], block_bytes=RB * D * 22, phases=phases)


def _first_bwd(dx1, dh0, x0, npre, phases=()):
    def body(dx1_ref, dh_ref, x_ref, pr_ref, gx_ref, dpr_ref):
        xv = x_ref[...]
        r = _rinv(xv)
        xhat = xv * r
        dh = dh_ref[...]
        gx_ref[...] = dx1_ref[...] + _norm_bwd(dh * pr_ref[...], xhat, r)
        _accumulate(dpr_ref, _colsum(dh * xhat))

    return _hosted(body, name="first_bwd", grid=(S // RB,), in_specs=[_row_spec(D)] * 3 + [_vec_spec(D)],
                   out_specs=[_row_spec(D), _vec_spec(D)], out_shape=[_sds((S, D), F32), _sds((1, D), F32)],
                   args=[dx1, dh0, x0, npre], block_bytes=RB * D * 16, phases=phases)


GLA_RB = 256
GLA_CPB = GLA_RB // C


def _sigmoid(x):
    return 1.0 / (1.0 + jnp.exp(-x))


def _tri(strict):
    r = lax.broadcasted_iota(jnp.int32, (C, C), 0)
    c = lax.broadcasted_iota(jnp.int32, (C, C), 1)
    return jnp.where(c < r if strict else c <= r, 1.0, 0.0).astype(BF16)


def _tri_dot(tri, x):
    hi = x.astype(BF16)
    lo = (x - hi.astype(F32)).astype(BF16)
    return _dot(tri, hi) + _dot(tri, lo)


def _gla_gates(glr_b, w2, b, tri):
    z = _dot(glr_b, w2) + b
    log_a = (jnp.minimum(z, 0.0) - jnp.log(1.0 + jnp.exp(-jnp.abs(z)))) * (1.0 / GLA_TAU)
    bcum = _tri_dot(tri, log_a)
    b_end = jnp.sum(log_a, axis=0, keepdims=True)
    return z, jnp.exp(b_end - bcum), jnp.exp(b_end)


def _gla_fwd(proj, w2p, bgate, ogain, phases=()):
    def body(p_ref, w2_ref, b_ref, og_ref, y_ref, st_out_ref, st_ref):
        @pl.when(pl.program_id(0) == 0)
        def _():
            st_ref[...] = jnp.zeros_like(st_ref)

        tri = _tri(False)

        def chunk(ci, carry):
            rows = pl.ds(pl.multiple_of(ci * C, C), C)
            glr_b = p_ref[rows, LR0:LR0 + LRP].astype(BF16)
            _, ea_all, dec_all = _gla_gates(glr_b, w2_ref[...], b_ref[...], tri)
            for h in range(H):
                ea, dec = ea_all[:, h * DK:(h + 1) * DK], dec_all[:, h * DK:(h + 1) * DK]
                k_dec = (p_ref[rows, K0 + h * DK:K0 + (h + 1) * DK] * ea).astype(BF16)
                v_b = p_ref[rows, V0 + h * DV:V0 + (h + 1) * DV].astype(BF16)
                st = st_ref[h] * dec + _dot(v_b, k_dec, TN)
                st_ref[h] = st
                st_b = st.astype(BF16)
                st_out_ref[ci, h] = st_b
                q_b = (p_ref[rows, Q0 + h * DK:Q0 + (h + 1) * DK] * (DK ** -0.5)).astype(BF16)
                o = _dot(q_b, st_b, NT)
                on = o * _rinv(o)
                g = p_ref[rows, G0 + h * DV:G0 + (h + 1) * DV]
                y_ref[rows, h * DV:(h + 1) * DV] = (on * og_ref[:, h * DV:(h + 1) * DV] * (g * _sigmoid(g))).astype(BF16)
            return carry

        lax.fori_loop(0, GLA_CPB, chunk, 0, unroll=True)

    blocks = GLA_RB * GLA_PAD * 4 + GLA_RB * D * 2 + GLA_CPB * H * DV * DK * 2
    return _hosted(
        body, name="gla_fwd", grid=(S // GLA_RB,),
        in_specs=[pl.BlockSpec((GLA_RB, GLA_PAD), lambda i: (i, 0)),
                  pl.BlockSpec((LRP, H * DK), lambda i: (0, 0)),
                  pl.BlockSpec((1, H * DK), lambda i: (0, 0)),
                  pl.BlockSpec((1, H * DV), lambda i: (0, 0))],
        out_specs=[pl.BlockSpec((GLA_RB, H * DV), lambda i: (i, 0)),
                   pl.BlockSpec((GLA_CPB, H, DV, DK), lambda i: (i, 0, 0, 0))],
        out_shape=[_sds((S, H * DV), BF16), _sds((NC, H, DV, DK), BF16)],
        args=[proj, w2p, bgate, ogain], scratch_shapes=[pltpu.VMEM((H, DV, DK), F32)],
        block_bytes=blocks, scratch_bytes=H * DV * DK * 4, phases=phases)


def _gla_bwd(proj, dypre, states, w2p, bgate, ogain, phases=()):
    nb = S // GLA_RB

    def body(p_ref, dy_ref, st_blk_ref, st_prev_ref, w2_ref, b_ref, og_ref,
             dp_ref, dog_ref, dbg_ref, dw2_ref, r_ref):
        step = pl.program_id(0)

        @pl.when(step == 0)
        def _():
            r_ref[...] = jnp.zeros_like(r_ref)
            dog_ref[...] = jnp.zeros_like(dog_ref)
            dbg_ref[...] = jnp.zeros_like(dbg_ref)
            dw2_ref[...] = jnp.zeros_like(dw2_ref)

        tri = _tri(False)
        tri_strict = _tri(True)
        has_prev = jnp.where(step < nb - 1, 1.0, 0.0).astype(F32)

        def chunk(ci, st_prev_of):
            rows = pl.ds(ci * C if isinstance(ci, int) else pl.multiple_of(ci * C, C), C)
            glr_b = p_ref[rows, LR0:LR0 + LRP].astype(BF16)
            z, ea_all, dec_all = _gla_gates(glr_b, w2_ref[...], b_ref[...], tri)
            d_a, d_end = [], []
            for h in range(H):
                kcol = slice(h * DK, (h + 1) * DK)
                vcol = slice(h * DV, (h + 1) * DV)
                ea, dec = ea_all[:, kcol], dec_all[:, kcol]
                k_dec = p_ref[rows, K0 + h * DK:K0 + (h + 1) * DK] * ea
                k_dec_b = k_dec.astype(BF16)
                v_b = p_ref[rows, V0 + h * DV:V0 + (h + 1) * DV].astype(BF16)
                q_b = (p_ref[rows, Q0 + h * DK:Q0 + (h + 1) * DK] * (DK ** -0.5)).astype(BF16)
                st_b = st_blk_ref[ci, h]
                o = _dot(q_b, st_b, NT)
                rinv = _rinv(o)
                on = o * rinv
                g = p_ref[rows, G0 + h * DV:G0 + (h + 1) * DV]
                sg = _sigmoid(g)
                og = og_ref[:, vcol]
                dyp = dy_ref[rows, vcol]
                dp_ref[rows, G0 + h * DV:G0 + (h + 1) * DV] = (dyp * (on * og) * (sg * (1.0 + g * (1.0 - sg)))).astype(BF16)
                dpn = dyp * (g * sg)
                dog_ref[:, vcol] += _colsum(dpn * on)
                do_b = _norm_bwd(dpn * og, on, rinv).astype(BF16)
                gt = _dot(do_b, q_b, TN) + r_ref[h]
                gt_b = gt.astype(BF16)
                dp_ref[rows, Q0 + h * DK:Q0 + (h + 1) * DK] = (_dot(do_b, st_b) * (DK ** -0.5)).astype(BF16)
                dkd = _dot(v_b, gt_b)
                dp_ref[rows, V0 + h * DV:V0 + (h + 1) * DV] = _dot(k_dec_b, gt_b, NT).astype(BF16)
                dp_ref[rows, K0 + h * DK:K0 + (h + 1) * DK] = (dkd * ea).astype(BF16)
                d_a.append(dkd * k_dec)
                d_end.append(_colsum(gt * st_prev_of(h)) * dec)
                r_ref[h] = gt * dec
            dla = _tri_dot(tri_strict, jnp.concatenate(d_a, axis=1)) + jnp.concatenate(d_end, axis=1)
            dz = dla * (1.0 / GLA_TAU) * (1.0 - _sigmoid(z))
            dz_b = dz.astype(BF16)
            dbg_ref[...] += _colsum(dz)
            dw2_ref[...] += _dot(glr_b, dz_b, TN)
            dp_ref[rows, LR0:LR0 + LRP] = _dot(dz_b, w2_ref[...], NT).astype(BF16)

        def later_chunk(t, carry):
            ci = GLA_CPB - 1 - t
            chunk(ci, lambda h: st_blk_ref[ci - 1, h].astype(F32))
            return carry

        lax.fori_loop(0, GLA_CPB - 1, later_chunk, 0, unroll=True)
        chunk(0, lambda h: st_prev_ref[0, h].astype(F32) * has_prev)

    blocks = (GLA_RB * GLA_PAD * 4 + GLA_RB * D * 4 + (GLA_CPB + 1) * H * DV * DK * 2 + GLA_RB * GLA_PAD * 2)
    rev = lambda i: nb - 1 - i
    return _hosted(
        body, name="gla_bwd", grid=(nb,),
        in_specs=[pl.BlockSpec((GLA_RB, GLA_PAD), lambda i: (rev(i), 0)),
                  pl.BlockSpec((GLA_RB, H * DV), lambda i: (rev(i), 0)),
                  pl.BlockSpec((GLA_CPB, H, DV, DK), lambda i: (rev(i), 0, 0, 0)),
                  pl.BlockSpec((1, H, DV, DK), lambda i: (jnp.maximum(rev(i) * GLA_CPB - 1, 0), 0, 0, 0)),
                  pl.BlockSpec((LRP, H * DK), lambda i: (0, 0)),
                  pl.BlockSpec((1, H * DK), lambda i: (0, 0)),
                  pl.BlockSpec((1, H * DV), lambda i: (0, 0))],
        out_specs=[pl.BlockSpec((GLA_RB, GLA_PAD), lambda i: (rev(i), 0)),
                   pl.BlockSpec((1, H * DV), lambda i: (0, 0)),
                   pl.BlockSpec((1, H * DK), lambda i: (0, 0)),
                   pl.BlockSpec((LRP, H * DK), lambda i: (0, 0))],
        out_shape=[_sds((S, GLA_PAD), BF16), _sds((1, H * DV), F32), _sds((1, H * DK), F32), _sds((LRP, H * DK), F32)],
        args=[proj, dypre, states, states, w2p, bgate, ogain], scratch_shapes=[pltpu.VMEM((H, DV, DK), F32)],
        block_bytes=blocks, scratch_bytes=H * DV * DK * 4, phases=phases)


SGU_RB = 256
GELU_C = 0.7978845608028654
GELU_A = 0.044715


def _gelu(x):
    return 0.5 * x * (1.0 + jnp.tanh(GELU_C * (x + GELU_A * x * x * x)))


def _gelu_grad(x):
    t = jnp.tanh(GELU_C * (x + GELU_A * x * x * x))
    return 0.5 * (1.0 + t) + 0.5 * x * (1.0 - t * t) * (GELU_C * (1.0 + 3.0 * GELU_A * x * x))


def _causal_mask(transposed=False):
    i = lax.broadcasted_iota(jnp.int32, (SGU_BLOCK, SGU_BLOCK), 1 if transposed else 0)
    j = lax.broadcasted_iota(jnp.int32, (SGU_BLOCK, SGU_BLOCK), 0 if transposed else 1)
    return (i >= C) | (j < C)


def _layer_norm(vf, gain, bias):
    mu = jnp.mean(vf, axis=-1, keepdims=True)
    cen = vf - mu
    rstd = lax.rsqrt(jnp.mean(cen * cen, axis=-1, keepdims=True) + EPS)
    xhat = cen * rstd
    return xhat, rstd, xhat * gain + bias


def _sgu_fwd(proj, lng, lnb, ws, bsb, phases=()):
    def body(p_ref, g_ref, b_ref, ws_ref, bs_ref, o_ref):
        mask = _causal_mask()
        for n in range(SGU_RB // SGU_BLOCK):
            rows = slice(n * SGU_BLOCK, (n + 1) * SGU_BLOCK)
            _, _, vn = _layer_norm(_gelu(p_ref[rows, D:2 * D]), g_ref[...], b_ref[...])
            vn_b = vn.astype(BF16)
            for gi in range(SGU_G):
                cols = slice(gi * SGU_GD, (gi + 1) * SGU_GD)
                w = jnp.where(mask, ws_ref[gi], 0.0).astype(BF16)
                vs = _dot(w, vn_b[:, cols]) + bs_ref[gi]
                gate = p_ref[rows, 2 * D + gi * SGU_GD:2 * D + (gi + 1) * SGU_GD]
                o_ref[rows, cols] = (_gelu(p_ref[rows, cols]) * vs * (gate * _sigmoid(gate))).astype(BF16)

    blocks = SGU_RB * SGU_COLS * 4 + SGU_RB * D * 2 + SGU_G * SGU_BLOCK * (SGU_BLOCK + SGU_GD) * 4
    return _hosted(
        body, name="sgu_fwd", grid=(S // SGU_RB,),
        in_specs=[pl.BlockSpec((SGU_RB, SGU_COLS), lambda i: (i, 0)),
                  pl.BlockSpec((1, D), lambda i: (0, 0)), pl.BlockSpec((1, D), lambda i: (0, 0)),
                  pl.BlockSpec((SGU_G, SGU_BLOCK, SGU_BLOCK), lambda i: (0, 0, 0)),
                  pl.BlockSpec((SGU_G, SGU_BLOCK, SGU_GD), lambda i: (0, 0, 0))],
        out_specs=[pl.BlockSpec((SGU_RB, D), lambda i: (i, 0))], out_shape=[_sds((S, D), BF16)],
        args=[proj, lng, lnb, ws, bsb], block_bytes=blocks, phases=phases)


def _sgu_bwd(proj, dpre, lng, lnb, ws, wst, bsb, phases=()):
    nsteps = S // SGU_RB

    def body(p_ref, d_ref, g_ref, b_ref, ws_ref, wst_ref, bs_ref,
             dp_ref, dg_ref, db_ref, dws_ref, dbs_ref, dvn_ref, dvs_acc_ref):
        step = pl.program_id(0)

        @pl.when(step == 0)
        def _():
            dg_ref[...] = jnp.zeros_like(dg_ref)
            db_ref[...] = jnp.zeros_like(db_ref)
            dws_ref[...] = jnp.zeros_like(dws_ref)
            dvs_acc_ref[...] = jnp.zeros_like(dvs_acc_ref)

        mask = _causal_mask()
        maskt = _causal_mask(transposed=True)
        for n in range(SGU_RB // SGU_BLOCK):
            rows = slice(n * SGU_BLOCK, (n + 1) * SGU_BLOCK)
            v = p_ref[rows, D:2 * D]
            xhat, rstd, vn = _layer_norm(_gelu(v), g_ref[...], b_ref[...])
            vn_b = vn.astype(BF16)
            for gi in range(SGU_G):
                cols = slice(gi * SGU_GD, (gi + 1) * SGU_GD)
                w = jnp.where(mask, ws_ref[gi], 0.0).astype(BF16)
                wt = jnp.where(maskt, wst_ref[gi], 0.0).astype(BF16)
                vs = _dot(w, vn_b[:, cols]) + bs_ref[gi]
                u = p_ref[rows, cols]
                gate = p_ref[rows, 2 * D + gi * SGU_GD:2 * D + (gi + 1) * SGU_GD]
                sg = _sigmoid(gate)
                gu = _gelu(u)
                dpre_g = d_ref[rows, cols]
                t = dpre_g * (gate * sg)
                dp_ref[rows, cols] = (t * vs * _gelu_grad(u)).astype(BF16)
                dp_ref[rows, 2 * D + gi * SGU_GD:2 * D + (gi + 1) * SGU_GD] = (
                    dpre_g * gu * vs * (sg * (1.0 + gate * (1.0 - sg)))).astype(BF16)
                dvs = t * gu
                dvs_b = dvs.astype(BF16)
                dvs_acc_ref[:, cols] += dvs
                dws_ref[gi] += _dot(dvs_b, vn_b[:, cols], NT)
                dvn_ref[:, cols] = _dot(wt, dvs_b)
            dvn = dvn_ref[...]
            dg_ref[...] += _colsum(dvn * xhat)
            db_ref[...] += _colsum(dvn)
            dxh = dvn * g_ref[...]
            dvf = rstd * (dxh - jnp.mean(dxh, axis=-1, keepdims=True) - xhat * jnp.mean(dxh * xhat, axis=-1, keepdims=True))
            dp_ref[rows, D:2 * D] = (dvf * _gelu_grad(v)).astype(BF16)

        @pl.when(step == nsteps - 1)
        def _():
            lane = lax.broadcasted_iota(jnp.int32, (SGU_BLOCK, SGU_BLOCK), 1)
            out = jnp.zeros((SGU_BLOCK, SGU_BLOCK), F32)
            for gi in range(SGU_G):
                out = out + jnp.where(lane == gi, jnp.sum(dvs_acc_ref[:, gi * SGU_GD:(gi + 1) * SGU_GD], axis=1, keepdims=True), 0.0)
                dws_ref[gi] = jnp.where(mask, dws_ref[gi], 0.0)
            dbs_ref[...] = out

    blocks = SGU_RB * SGU_COLS * 6 + SGU_RB * D * 4 + SGU_G * SGU_BLOCK * (3 * SGU_BLOCK + SGU_GD) * 4
    const3 = lambda i: (0, 0, 0)
    return _hosted(
        body, name="sgu_bwd", grid=(nsteps,),
        in_specs=[pl.BlockSpec((SGU_RB, SGU_COLS), lambda i: (i, 0)),
                  pl.BlockSpec((SGU_RB, D), lambda i: (i, 0)),
                  pl.BlockSpec((1, D), lambda i: (0, 0)), pl.BlockSpec((1, D), lambda i: (0, 0)),
                  pl.BlockSpec((SGU_G, SGU_BLOCK, SGU_BLOCK), const3),
                  pl.BlockSpec((SGU_G, SGU_BLOCK, SGU_BLOCK), const3),
                  pl.BlockSpec((SGU_G, SGU_BLOCK, SGU_GD), const3)],
        out_specs=[pl.BlockSpec((SGU_RB, SGU_COLS), lambda i: (i, 0)),
                   pl.BlockSpec((1, D), lambda i: (0, 0)), pl.BlockSpec((1, D), lambda i: (0, 0)),
                   pl.BlockSpec((SGU_G, SGU_BLOCK, SGU_BLOCK), const3),
                   pl.BlockSpec((SGU_BLOCK, SGU_BLOCK), lambda i: (0, 0))],
        out_shape=[_sds((S, SGU_COLS), BF16), _sds((1, D), F32), _sds((1, D), F32),
                   _sds((SGU_G, SGU_BLOCK, SGU_BLOCK), F32), _sds((SGU_BLOCK, SGU_BLOCK), F32)],
        args=[proj, dpre, lng, lnb, ws, wst, bsb],
        scratch_shapes=[pltpu.VMEM((SGU_BLOCK, D), F32), pltpu.VMEM((SGU_BLOCK, D), F32)],
        block_bytes=blocks, scratch_bytes=2 * SGU_BLOCK * D * 4, phases=phases)


def _pair_sum(own, a, r0, nr, name, table=None):
    c = own.shape[2]
    tr = 256
    assert r0 % tr == 0 and nr % tr == 0

    def body(own_ref, sib_ref, o_ref):
        o_ref[...] = (own_ref[...].astype(F32) + sib_ref[...].astype(F32)).astype(BF16)

    own_map = ((lambda j, i: (1 + j, r0 // tr + i, 0)) if table is None else
               (lambda j, i, t: (t[1 + j], r0 // tr + i, 0)))
    cpad = -(-c // 128) * 128
    outs, _ = _hosted(
        body, name=name, grid=(3, nr // tr),
        in_specs=[pl.BlockSpec((None, tr, c), own_map),
                  pl.BlockSpec((None, tr, c), lambda j, i, *t: (1 + j, r0 // tr + i, 0))],
        out_specs=[pl.BlockSpec((None, tr, c), lambda j, i, *t: (j, i, 0))], out_shape=[_sds((3, nr, c), BF16)],
        args=[own, a], block_bytes=3 * tr * cpad * 2, table=table)
    return outs[0]


def _adamw_math(w, g, m, v):
    m = ADAM_B1 * m + (1.0 - ADAM_B1) * g
    v = ADAM_B2 * v + (1.0 - ADAM_B2) * (g * g)
    m_hat = m / (1.0 - ADAM_B1 ** ADAM_STEP)
    v_hat = v / (1.0 - ADAM_B2 ** ADAM_STEP)
    delta = -ADAM_LR * (m_hat / (jnp.sqrt(v_hat) + ADAM_EPS) + ADAM_WD * w)
    return delta, m, v


def _sum_adamw(own, a, b, w, m, v, *, name, phases=(), table=None):
    r, c = w.shape
    tr = 256

    def body(own_ref, sib_ref, far_ref, w_ref, m_ref, v_ref, g_ref, d_ref, nm_ref, nv_ref):
        g = own_ref[...].astype(F32) + sib_ref[...].astype(F32)
        for j in range(3):
            g = g + far_ref[j].astype(F32)
        g_ref[...] = g
        d_ref[...], nm_ref[...], nv_ref[...] = _adamw_math(w_ref[...], g, m_ref[...], v_ref[...])

    spec = pl.BlockSpec((tr, c), lambda i, *t: (i, 0))
    own_map = (lambda i: (0, i, 0)) if table is None else (lambda i, t: (t[0], i, 0))
    cpad = -(-c // 128) * 128
    return _hosted(
        body, name=name, grid=(r // tr,),
        in_specs=[pl.BlockSpec((None, tr, c), own_map), pl.BlockSpec((None, tr, c), lambda i, *t: (0, i, 0)),
                  pl.BlockSpec((3, tr, c), lambda i, *t: (0, i, 0)), spec, spec, spec],
        out_specs=[spec] * 4, out_shape=[_sds((r, c), F32)] * 4, args=[own, a, b, w, m, v],
        block_bytes=5 * tr * cpad * 2 + 7 * tr * cpad * 4, phases=phases, table=table)


def _sum_parts(parts, name):
    n, r, c = parts.shape

    def body(p_ref, o_ref):
        g = p_ref[0]
        for j in range(1, n):
            g = g + p_ref[j]
        o_ref[...] = g

    outs, _ = _hosted(body, name=name, grid=(1,), in_specs=[pl.BlockSpec((n, r, c), lambda i: (0, 0, 0))],
                      out_specs=[pl.BlockSpec((r, c), lambda i: (0, 0))], out_shape=[_sds((r, c), F32)], args=[parts],
                      block_bytes=(n + 1) * r * c * 4)
    return outs[0]


def _adamw(w, g, m, v, name):
    def body(w_ref, g_ref, m_ref, v_ref, d_ref, nm_ref, nv_ref):
        d_ref[...], nm_ref[...], nv_ref[...] = _adamw_math(w_ref[...], g_ref[...], m_ref[...], v_ref[...])

    spec = pl.BlockSpec(w.shape, lambda i: (0, 0))
    outs, _ = _hosted(body, name=name, grid=(1,), in_specs=[spec] * 4, out_specs=[spec] * 3, out_shape=[_sds(w.shape, F32)] * 3,
                      args=[w, g, m, v], block_bytes=7 * _nbytes(w.shape, F32))
    return outs


def _blocks_to_columns(g):
    n, r, c = g.shape
    return jnp.transpose(g, (1, 0, 2)).reshape(r, n * c)


def _pack(parts):
    return jnp.concatenate([p.reshape(-1) for p in parts]).reshape(-1, 128)


def _unpack(packed, like):
    flat, outs, off = packed.reshape(-1), [], 0
    for p in like:
        outs.append(flat[off:off + p.size].reshape(p.shape))
        off += p.size
    return outs


def kernel(x, norm_pre, norm_post, gla_w_in, gla_w_gate2, gla_b_gate, gla_o_gain, gla_w_out, sgu_w_in, sgu_ln_gain, sgu_ln_bias, sgu_w_spatial, sgu_b_spatial, sgu_w_out, loss_target, m_norm_pre, m_norm_post, m_gla_w_in, m_gla_w_gate2, m_gla_b_gate, m_gla_o_gain, m_gla_w_out, m_sgu_w_in, m_sgu_ln_gain, m_sgu_ln_bias, m_sgu_w_spatial, m_sgu_b_spatial, m_sgu_w_out, v_norm_pre, v_norm_post, v_gla_w_in, v_gla_w_gate2, v_gla_b_gate, v_gla_o_gain, v_gla_w_out, v_sgu_w_in, v_sgu_ln_gain, v_sgu_ln_bias, v_sgu_w_spatial, v_sgu_b_spatial, v_sgu_w_out):
    me = _index_of(*_place())
    x0 = x.reshape(S, D)
    tgt = loss_target.reshape(S, D)
    npre0, npre1 = norm_pre[0:1], norm_pre[1:2]
    npost0, npost1 = norm_post[0:1], norm_post[1:2]
    ws = sgu_w_spatial[0]
    wst = jnp.transpose(ws, (0, 2, 1))
    bsb = jnp.broadcast_to(sgu_b_spatial[0][:, :, None], (SGU_G, SGU_BLOCK, SGU_GD))
    W_ROWS = D // N_DEV
    IN_COLS_G, IN_COLS_S = GLA_COLS // N_DEV, SGU_COLS // N_DEV

    s_gwi, s_gwo = gla_w_in[0].astype(BF16), gla_w_out[0].astype(BF16)
    s_swi, s_swo = sgu_w_in[0].astype(BF16), sgu_w_out[0].astype(BF16)
    small = jnp.concatenate([jnp.pad(gla_w_gate2[0].reshape(4, 512), ((0, 4), (0, 0))),
                             jnp.pad(jnp.concatenate([sgu_ln_gain, sgu_ln_bias], axis=1), ((0, 7), (0, 0)))], axis=0)

    wg_in, g_small = _gather_first(s_gwi, small, "gather_first")
    w2 =_blocks_to_columns(g_small[:, :4, :].reshape(N_DEV, LR, 128))
    w2p = jnp.pad(w2, ((0, LRP - LR), (0, 0))).astype(BF16)
    lng = g_small[:, 8, :256].reshape(1, D)
    lnb = g_small[:, 8, 256:].reshape(1, D)
    like_gwo, like_swi = _sds((N_DEV, W_ROWS, D), BF16), _sds((N_DEV, D, IN_COLS_S), BF16)

    h0 = _prenorm(x0, npre0)
    proj0, (g_gwo, g_swi) = _mm(h0, wg_in, "nn", F32, tm=1024, tn=896, tk=D, name="gla_in", b_tiled=True, phases=[
        _Phase(like_gwo, None, [_gather_send(s_gwo, 0, W_ROWS)]),
        _Phase(like_swi, None, [_gather_send(s_swi, 0, 512)])])
    (ypre0, states), (g_gwo, g_swi) = _gla_fwd(proj0, w2p, gla_b_gate, gla_o_gain, phases=[
        _Phase(like_gwo, g_gwo, [_gather_pass(0, W_ROWS)]),
        _Phase(like_swi, g_swi, [_gather_send(s_swi, 512, 512), _gather_pass(0, 512)])])
    wg_out = g_gwo.reshape(D, D)
    y0, (g_swi,) = _mm(ypre0, wg_out, "nn", F32, tm=1024, tn=1024, tk=D, name="gla_out", phases=[
        _Phase(like_swi, g_swi, [_gather_send(s_swi, 1024, 512), _gather_pass(512, 512)])])
    (x1, h1), (g_swi,) = _mid_fwd(x0, y0, npost0, npre1, phases=[
        _Phase(like_swi, g_swi, [_gather_send(s_swi, 1536, 512), _gather_pass(1024, 512)])])
    g_swi, = _carry([_Phase(like_swi, g_swi, [_gather_pass(1536, 512)])], "pass_sgu_w_in")
    proj1, (g_swo,) = _mm(h1, g_swi, "nn", F32, tm=1024, tn=IN_COLS_S, tk=D, name="sgu_in", b_blocked=True, phases=[
        _Phase(like_gwo, None, [_gather_send(s_swo, 0, W_ROWS)])])
    (pre1,), (g_swo,) = _sgu_fwd(proj1, lng, lnb, ws, bsb, phases=[_Phase(like_gwo, g_swo, [_gather_pass(0, W_ROWS)])])
    ws_out = g_swo.reshape(D, D)
    y1, _ = _mm(pre1, ws_out, "nn", F32, tm=1024, tn=1024, tk=D, name="sgu_out")
    loss_cols, dx2, dy1, dnpost1 = _final(x1, y1, tgt, npost1)
    loss = lax.psum(0.5 * jnp.sum(loss_cols) / D, ("x", "y", "c"))

    like_b_out, like_b_swi = _sds((3, W_ROWS, D), BF16), _sds((3, D, IN_COLS_S), BF16)
    like_b_gwi = _sds((3, D, IN_COLS_G), BF16)
    row_pair = dict(like=_sds((4, W_ROWS, D), BF16), block=lambda i, j: i, ordinal=lambda i, j: i >> 1,
                    dst=lambda ref, k, i, j: ref.at[k])
    col_pair = dict(like=_sds((4, D, IN_COLS_S), BF16), block=lambda i, j: j, ordinal=lambda i, j: 4 * i + (j >> 1),
                    dst=lambda ref, k, i, j: ref.at[k, pl.ds(pl.multiple_of(i * 1024, 1024), 1024)])

    mine = _own_table()
    dws_out, (a_swo,) = _mm(pre1, dy1, "tn", BF16, tm=W_ROWS, tn=D, tk=S, name="sgu_out_dw", pair=row_pair)
    p_swo = dws_out.reshape(N_DEV, W_ROWS, D)
    t_swo = _pair_sum(p_swo, a_swo, 0, W_ROWS, "pair_sum_sgu_w_out", table=mine)
    dpre1, _ = _mm(dy1, ws_out, "nt", F32, tm=1024, tn=1024, tk=D, name="sgu_out_dx")
    (dproj1, dlng, dlnb, dwsp, dbsp), (b_swo,) = _sgu_bwd(proj1, dpre1, lng, lnb, ws, wst, bsb, phases=[
        _Phase(like_b_out, None, [_reduce_cross(t_swo, 0, 0, W_ROWS)])])
    p_swi, (a_swi,) = _mm(h1, dproj1, "tn", BF16, tm=1024, tn=IN_COLS_S, tk=S, name="sgu_in_dw", out_blocked=True, pair=col_pair)
    t_swi = _pair_sum(p_swi, a_swi, 0, D, "pair_sum_sgu_w_in", table=mine)
    dh1, (b_swi,) = _mm(dproj1, g_swi, "nt", F32, tm=1024, tn=1024, tk=IN_COLS_S, name="sgu_in_dx", b_blocked=True, phases=[
        _Phase(like_b_swi, None, [_reduce_cross(t_swi, 0, 0, 1024)])])
    (dx1, dy0, dnpre1, dnpost0), (b_swi,) = _mid_bwd(dx2, dh1, x1, y0, npre1, npost0, phases=[
        _Phase(like_b_swi, b_swi, [_reduce_cross(t_swi, 1024, 1024, 512)])])
    dwg_out, (a_gwo, b_swi) = _mm(ypre0, dy0, "tn", BF16, tm=W_ROWS, tn=D, tk=S, name="gla_out_dw", pair=row_pair, phases=[
        _Phase(like_b_swi, b_swi, [_reduce_cross(t_swi, 1536, 1536, 512)])])
    p_gwo = dwg_out.reshape(N_DEV, W_ROWS, D)
    t_gwo = _pair_sum(p_gwo, a_gwo, 0, W_ROWS, "pair_sum_gla_w_out", table=mine)
    dypre0, _ = _mm(dy0, wg_out, "nt", F32, tm=1024, tn=1024, tk=D, name="gla_out_dx")
    late = [dnpre1, dnpost1, dlng, dlnb, dwsp, jnp.transpose(dbsp[:, :SGU_G])]
    late_pack = _pack(late)
    (dproj0, dogain, dbgate, dw2), (b_gwo, g_late) = _gla_bwd(proj0, dypre0, states, w2p, gla_b_gate, gla_o_gain, phases=[
        _Phase(like_b_out, None, [_reduce_cross(t_gwo, 0, 0, W_ROWS)]),
        _Phase(_sds((N_DEV,) + late_pack.shape, F32), None, [_gather_send(late_pack, 0, late_pack.shape[0])])])
    half = D // 2
    dwg_in_a, (g_late,) = _mm(h0, dproj0, "tn", BF16, tm=half, tn=896, tk=S, name="gla_in_dw_a", m_tiles=(0, 1), phases=[
        _Phase(_sds((N_DEV,) + late_pack.shape, F32), g_late, [_gather_pass(0, late_pack.shape[0])])])
    own_gwi, a_gwi = _blockify_pair(dwg_in_a, None, None, 0, "blockify_gla_w_in_a")
    t_gwi_a = _pair_sum(own_gwi, a_gwi, 0, half, "pair_sum_gla_w_in_a")
    dwg_in_b, (b_gwi,) = _mm(h0, dproj0, "tn", BF16, tm=half, tn=896, tk=S, name="gla_in_dw_b", m_tiles=(1, 1), phases=[
        _Phase(like_b_gwi, None, [_reduce_cross(t_gwi_a, 0, 0, 512)])])
    own_gwi, a_gwi = _blockify_pair(dwg_in_b, own_gwi, a_gwi, half, "blockify_gla_w_in_b")
    t_gwi_b = _pair_sum(own_gwi, a_gwi, half, half, "pair_sum_gla_w_in_b")
    dh0, (b_gwi,) = _mm(dproj0, wg_in, "nt", F32, tm=1024, tn=1024, tk=896, name="gla_in_dx", b_tiled=True, phases=[
        _Phase(like_b_gwi, b_gwi, [_reduce_cross(t_gwi_a, 512, 512, 512), _reduce_cross(t_gwi_b, 0, half, 512)])])
    (grad_x, dnpre0), (b_gwi,) = _first_bwd(dx1, dh0, x0, npre0, phases=[
        _Phase(like_b_gwi, b_gwi, [_reduce_cross(t_gwi_b, 512, half + 512, 256)])])

    early = [dnpre0, dnpost0, dbgate, dogain, dw2[:LR]]
    early_pack = _pack(early)
    like_early = _sds((N_DEV,) + early_pack.shape, F32)
    (g_swo, d_swo, nm_swo, nv_swo), (b_gwi, g_early) = _sum_adamw(
        p_swo, a_swo, b_swo, sgu_w_out[0], m_sgu_w_out[0], v_sgu_w_out[0], name="adamw_sgu_w_out", table=mine, phases=[
            _Phase(like_b_gwi, b_gwi, [_reduce_cross(t_gwi_b, 768, half + 768, 256)]),
            _Phase(like_early, None, [_gather_send(early_pack, 0, early_pack.shape[0])])])
    (g_swi_, d_swi, nm_swi, nv_swi), (g_early,) = _sum_adamw(
        p_swi, a_swi, b_swi, sgu_w_in[0], m_sgu_w_in[0], v_sgu_w_in[0], name="adamw_sgu_w_in", table=mine, phases=[
            _Phase(like_early, g_early, [_gather_pass(0, early_pack.shape[0])])])
    (g_gwo_, d_gwo, nm_gwo, nv_gwo), _ = _sum_adamw(
        p_gwo, a_gwo, b_gwo, gla_w_out[0], m_gla_w_out[0], v_gla_w_out[0], name="adamw_gla_w_out", table=mine)
    (g_gwi_, d_gwi, nm_gwi, nv_gwi), _ = _sum_adamw(
        own_gwi, a_gwi, b_gwi, gla_w_in[0], m_gla_w_in[0], v_gla_w_in[0], name="adamw_gla_w_in")

    g_npre1, g_npost1, g_lng_full, g_lnb_full, g_wsp, g_bsp = _unpack(_sum_parts(g_late, "sum_late_small_grads"), late)
    g_npre0, g_npost0, g_bgate, g_ogain, g_w2_full = _unpack(_sum_parts(g_early, "sum_early_small_grads"), early)
    g_w2 = lax.dynamic_slice(g_w2_full, (0, me * 128), (LR, 128))
    g_lng = lax.dynamic_slice(g_lng_full, (0, me * 256), (1, 256))
    g_lnb = lax.dynamic_slice(g_lnb_full, (0, me * 256), (1, 256))
    small_g = [jnp.concatenate([g_npre0, g_npre1], 0), jnp.concatenate([g_npost0, g_npost1], 0), g_w2, g_bgate, g_ogain,
               g_lng, g_lnb, g_wsp, g_bsp]
    small_w = [norm_pre, norm_post, gla_w_gate2[0], gla_b_gate, gla_o_gain, sgu_ln_gain, sgu_ln_bias, sgu_w_spatial[0], sgu_b_spatial[0]]
    small_m = [m_norm_pre, m_norm_post, m_gla_w_gate2[0], m_gla_b_gate, m_gla_o_gain, m_sgu_ln_gain, m_sgu_ln_bias, m_sgu_w_spatial[0], m_sgu_b_spatial[0]]
    small_v = [v_norm_pre, v_norm_post, v_gla_w_gate2[0], v_gla_b_gate, v_gla_o_gain, v_sgu_ln_gain, v_sgu_ln_bias, v_sgu_w_spatial[0], v_sgu_b_spatial[0]]
    d_pack, nm_pack, nv_pack = _adamw(_pack(small_w), _pack(small_g), _pack(small_m), _pack(small_v), "adamw_small")

    out_like = [norm_pre, norm_post, gla_w_gate2, gla_b_gate, gla_o_gain, sgu_ln_gain, sgu_ln_bias, sgu_w_spatial, sgu_b_spatial]
    sg_ = [g.reshape(s.shape) for g, s in zip(small_g, out_like)]
    sd_, sm_, sv_ = (_unpack(pk, out_like) for pk in (d_pack, nm_pack, nv_pack))

    def assemble(small_list, w_in_g, w_out_g, w_in_s, w_out_s):
        npre_, npost_, w2_, bg_, og_, lg_, lb_, wsp_, bsp_ = small_list
        return [npre_, npost_, w_in_g[None], w2_, bg_, og_, w_out_g[None], w_in_s[None], lg_, lb_, wsp_, bsp_, w_out_s[None]]

    return (loss, grad_x.reshape(1, S, D),
            *assemble(sg_, g_gwi_, g_gwo_, g_swi_, g_swo),
            *assemble(sd_, d_gwi, d_gwo, d_swi, d_swo),
            *assemble(sm_, nm_gwi, nm_gwo, nm_swi, nm_swo),
            *assemble(sv_, nv_gwi, nv_gwo, nv_swi, nv_swo))
```

```python
import functools

import jax
import jax.numpy as jnp
from jax import lax
from jax.experimental import pallas as pl
from jax.experimental.pallas import tpu as pltpu

F32 = jnp.float32
BF16 = jnp.bfloat16

N_DEV = 8
S = 2048
D = 2048
H = 4
DK = 256
DV = 512
C = 64
NC = S // C
GLA_COLS = 6160
GLA_PAD = 6272
Q0, K0, V0, G0, LR0 = 0, 1024, 2048, 4096, 6144
LR = 16
LRP = 128
SGU_COLS = 6144
SGU_BLOCK = 128
SGU_G = 8
SGU_GD = 256
EPS = 1e-6
GLA_TAU = 16.0

ADAM_LR, ADAM_B1, ADAM_B2, ADAM_EPS, ADAM_WD, ADAM_STEP = 0.001, 0.9, 0.999, 1e-08, 0.01, 10

V7X_VMEM_BYTES = 64 * 1024 * 1024
VMEM_CEILING = V7X_VMEM_BYTES - 6 * 1024 * 1024
MESH = pl.DeviceIdType.MESH
HBM_SPEC = pl.BlockSpec(memory_space=pl.ANY)


def _sds(shape, dtype):
    return jax.ShapeDtypeStruct(tuple(shape), dtype)


def _nbytes(shape, dtype):
    n = 1
    for s in shape:
        n *= s
    return n * jnp.dtype(dtype).itemsize


def _dot(a, b, dims=(((1,), (0,)), ((), ())), precision=None):
    return lax.dot_general(a, b, dims, precision=precision, preferred_element_type=F32)


NN = (((1,), (0,)), ((), ()))
TN = (((0,), (0,)), ((), ()))
NT = (((1,), (1,)), ((), ()))


def _place():
    return lax.axis_index("x"), lax.axis_index("y"), lax.axis_index("c")


def _index_of(px, py, pc):
    return 4 * px + 2 * py + pc


def _chips(x, y):
    return [(1 - x, y), (x, 1 - y), (1 - x, 1 - y)]


def _rcopy(src, dst, send_sem, recv_sem, to):
    return pltpu.make_async_remote_copy(src_ref=src, dst_ref=dst, send_sem=send_sem, recv_sem=recv_sem,
                                        device_id=to, device_id_type=MESH)


class _Move:
    def __init__(self, ins, n_remote, make, stage=None):
        self.ins, self.n_remote, self.make, self.stage = list(ins), n_remote, make, stage

    def scratch(self):
        sems = [pltpu.SemaphoreType.DMA((self.n_remote,)), pltpu.SemaphoreType.DMA((self.n_remote,))]
        return sems if self.stage is None else sems + [pltpu.SemaphoreType.DMA((1,)), pltpu.VMEM(*self.stage)]

    def start(self, in_refs, buf, scratch):
        sends, _, local = self.make(in_refs, buf, scratch[0], scratch[1])
        if local is not None:
            pltpu.make_async_copy(local[0], scratch[3], scratch[2].at[0]).start()
        for cp in sends:
            cp.start()

    def finish(self, in_refs, buf, scratch):
        sends, arrivals, local = self.make(in_refs, buf, scratch[0], scratch[1])
        if local is not None:
            pltpu.make_async_copy(local[0], scratch[3], scratch[2].at[0]).wait()
            out = pltpu.make_async_copy(scratch[3], local[1], scratch[2].at[0])
            out.start()
        for cp in arrivals:
            cp.wait_recv()
        for cp in sends:
            cp.wait_send()
        if local is not None:
            out.wait()


class _Phase:
    def __init__(self, like, so_far, moves):
        self.like, self.so_far, self.moves = like, so_far, list(moves)


def _gather_send(shard, r0, nr, diagonal=True):
    def make(in_refs, g, ss, rs):
        sh, = in_refs
        x, y, c = _place()
        me = _index_of(x, y, c)
        rows = pl.ds(r0, nr)
        peers = [(x, y, 1 - c)] + [(px, py, c) for px, py in _chips(x, y)[:3 if diagonal else 2]]
        sends = [_rcopy(sh.at[rows], g.at[me, rows], ss.at[k], rs.at[k], p) for k, p in enumerate(peers)]
        arrivals = [_rcopy(sh.at[rows], g.at[_index_of(*p), rows], ss.at[k], rs.at[k], p) for k, p in enumerate(peers)]
        return sends, arrivals, (sh.at[rows], g.at[me, rows])

    return _Move([shard], 4 if diagonal else 3, make, stage=((nr, shard.shape[1]), shard.dtype))


def _gather_relay(r0, nr):
    def make(in_refs, g, ss, rs):
        x, y, c = _place()
        nx, ny, nd = [(px, py, c) for px, py in _chips(x, y)]
        first, second = pl.ds(r0, nr // 2), pl.ds(r0 + nr // 2, nr // 2)
        sends = [_rcopy(g.at[_index_of(*nx), first], g.at[_index_of(*nx), first], ss.at[0], rs.at[0], ny),
                 _rcopy(g.at[_index_of(*ny), second], g.at[_index_of(*ny), second], ss.at[1], rs.at[1], nx)]
        arrivals = [_rcopy(g.at[_index_of(*nx), first], g.at[_index_of(*nd), first], ss.at[0], rs.at[0], ny),
                    _rcopy(g.at[_index_of(*ny), second], g.at[_index_of(*nd), second], ss.at[1], rs.at[1], nx)]
        return sends, arrivals, None

    return _Move([], 2, make)


def _gather_pass(r0, nr):
    def make(in_refs, g, ss, rs):
        x, y, c = _place()
        rows = pl.ds(r0, nr)
        sends = [_rcopy(g.at[_index_of(px, py, c), rows], g.at[_index_of(px, py, c), rows], ss.at[j], rs.at[j], (x, y, 1 - c))
                 for j, (px, py) in enumerate(_chips(x, y))]
        arrivals = [_rcopy(g.at[_index_of(px, py, c), rows], g.at[_index_of(px, py, 1 - c), rows], ss.at[j], rs.at[j], (x, y, 1 - c))
                    for j, (px, py) in enumerate(_chips(x, y))]
        return sends, arrivals, None

    return _Move([], 3, make)


def _own_table():
    x, y, c = _place()
    return jnp.stack([_index_of(px, py, c) for px, py in [(x, y)] + _chips(x, y)]).astype(jnp.int32)


def _blockify_pair(dw, own_so_far, a_so_far, dst_r0, name):
    rows, tr, cw, win = dw.shape[0], 256, GLA_COLS // N_DEV, 896
    n_steps = rows // tr

    def body(*refs):
        x_ref, own_ref, a_ref, stage_ref, send_sems, recv_sem = refs[0], *refs[-5:]
        i = pl.program_id(0)
        x, y, c = _place()

        def send(slot, k):
            dst = a_ref.at[k, pl.ds(pl.multiple_of(dst_r0 + i * tr, tr), tr)]
            return _rcopy(stage_ref.at[slot], dst, send_sems.at[slot], recv_sem.at[0], (x, y, 1 - c))

        for j in range(N_DEV):
            window = x_ref[:, 768 * j:768 * j + win].astype(F32)
            tile = (pltpu.roll(window, win - 2 * j, 1) if j else window)[:, :cw].astype(BF16)
            k = ((j >> 2) ^ x) + 2 * (((j >> 1) & 1) ^ y)

            @pl.when((j & 1) == c)
            def _():
                own_ref[k] = tile

            @pl.when((j & 1) != c)
            def _():
                slot = (j >> 1) & 1
                if j >> 1 >= 2:
                    send(slot, k).wait_send()
                else:
                    pl.when(i > 0)(lambda: send(slot, k).wait_send())
                stage_ref[slot] = tile
                send(slot, k).start()

        @pl.when(i == n_steps - 1)
        def _():
            send(0, 0).wait_send()
            send(1, 0).wait_send()
            arrived = a_ref.at[:, pl.ds(dst_r0, rows)]
            _rcopy(arrived, arrived, send_sems.at[0], recv_sem.at[0], (x, y, 1 - c)).wait_recv()

    continues = a_so_far is not None
    own, a = pl.pallas_call(
        body, grid=(n_steps,),
        in_specs=[pl.BlockSpec((tr, GLA_PAD), lambda i: (i, 0))] + [HBM_SPEC] * (2 * continues),
        out_specs=[pl.BlockSpec((4, tr, cw), lambda i: (0, dst_r0 // tr + i, 0)), HBM_SPEC],
        out_shape=[_sds((4, D, cw), BF16), _sds((4, D, cw), BF16)],
        scratch_shapes=[pltpu.VMEM((2, tr, cw), BF16), pltpu.SemaphoreType.DMA((2,)), pltpu.SemaphoreType.DMA((1,))],
        input_output_aliases={1: 0, 2: 1} if continues else {},
        compiler_params=pltpu.CompilerParams(dimension_semantics=("arbitrary",), vmem_limit_bytes=48 * 1024 * 1024),
        name=name,
    )(*([dw] + [own_so_far, a_so_far] * continues))
    return own, a


def _reduce_cross(sums, src_r0, dst_r0, nr):
    def make(in_refs, b, ss, rs):
        t, = in_refs
        x, y, c = _place()
        src, dst = pl.ds(src_r0, nr), pl.ds(dst_r0, nr)
        sends = [_rcopy(t.at[j, src], b.at[j, dst], ss.at[j], rs.at[j], (px, py, c)) for j, (px, py) in enumerate(_chips(x, y))]
        return sends, sends, None

    return _Move([sums], 3, make)


def _hosted(body, *, name, grid, in_specs, out_specs, out_shape, args, scratch_shapes=(), block_bytes, scratch_bytes=0,
            phases=(), table=None):
    n_in, n_out, n_scr = len(args), len(out_shape), len(scratch_shapes)
    all_args, all_out_shape, sems, aliases, layout = list(args), list(out_shape), [], {}, []
    for j, ph in enumerate(phases):
        counts = []
        for mv in ph.moves:
            all_args += mv.ins
            counts.append(len(mv.ins))
            sems += mv.scratch()
        if ph.so_far is not None:
            aliases[len(all_args)] = n_out + j
            all_args.append(ph.so_far)
        layout.append((counts, ph.so_far is not None))
        all_out_shape.append(ph.like)
    n_extra_in = len(all_args) - n_in

    def wrapped(*refs):
        ins, pos = refs[:n_in], n_in
        move_ins = []
        for counts, continues in layout:
            per_move = []
            for cnt in counts:
                per_move.append(refs[pos:pos + cnt])
                pos += cnt
            pos += continues
            move_ins.append(per_move)
        outs = refs[pos:pos + n_out]
        bufs = refs[pos + n_out:pos + n_out + len(phases)]
        pos += n_out + len(phases)
        scratch = refs[pos:pos + n_scr]
        pos += n_scr
        move_sems = []
        for ph in phases:
            per_move = []
            for mv in ph.moves:
                count = len(mv.scratch())
                per_move.append(refs[pos:pos + count])
                pos += count
            move_sems.append(per_move)

        def each_move(fn_name):
            for ph, buf, per_in, per_sem in zip(phases, bufs, move_ins, move_sems):
                for mv, mv_in, mv_sem in zip(ph.moves, per_in, per_sem):
                    getattr(mv, fn_name)(mv_in, buf, mv_sem)

        if phases:
            first = functools.reduce(jnp.logical_and, [pl.program_id(a) == 0 for a in range(len(grid))])
            last = functools.reduce(jnp.logical_and, [pl.program_id(a) == grid[a] - 1 for a in range(len(grid))])
            pl.when(first)(lambda: each_move("start"))
        body(*ins, *outs, *scratch)
        if phases:
            pl.when(last)(lambda: each_move("finish"))

    est = 2 * block_bytes + scratch_bytes
    params = pltpu.CompilerParams(dimension_semantics=("arbitrary",) * len(grid),
                                  vmem_limit_bytes=min(VMEM_CEILING, max(32 * 1024 * 1024, 2 * est)))
    all_in_specs, all_out_specs = list(in_specs) + [HBM_SPEC] * n_extra_in, list(out_specs) + [HBM_SPEC] * len(phases)
    if table is None:
        results = pl.pallas_call(
            wrapped, grid=grid, in_specs=all_in_specs, out_specs=all_out_specs, out_shape=all_out_shape,
            scratch_shapes=list(scratch_shapes) + sems, input_output_aliases=aliases, compiler_params=params, name=name,
        )(*all_args)
    else:
        results = pl.pallas_call(
            lambda table_ref, *refs: wrapped(*refs),
            grid_spec=pltpu.PrefetchScalarGridSpec(num_scalar_prefetch=1, grid=grid, in_specs=all_in_specs, out_specs=all_out_specs,
                                                   scratch_shapes=list(scratch_shapes) + sems),
            out_shape=all_out_shape, input_output_aliases={k + 1: v for k, v in aliases.items()}, compiler_params=params, name=name,
        )(table, *all_args)
    return list(results[:n_out]), list(results[n_out:])


def _carry(phases, name):
    def body(o_ref):
        o_ref[...] = jnp.zeros_like(o_ref)

    _, bufs = _hosted(body, name=name, grid=(1,), in_specs=[], out_specs=[pl.BlockSpec((8, 128), lambda i: (0, 0))],
                      out_shape=[_sds((8, 128), F32)], args=[], block_bytes=8 * 128 * 4, phases=phases)
    return bufs


def _gather_first(shard, small, name):
    cw, tr, n_tiles = shard.shape[1], 256, GLA_PAD // 128

    def body(sh_ref, sm_ref, wn_ref, g_ref, gs_ref, wt_ref, win_ref, tmp_ref, send_sems, recv_sems, local_sems):
        x, y, c = _place()
        me, sibling = (x, y, c), (x, y, 1 - c)
        chips = _chips(x, y)

        def copy(base, out_ref, k, block, to, src=None):
            dst = out_ref.at[_index_of(*block)]
            return _rcopy(dst if src is None else src, dst, send_sems.at[base + k], recv_sems.at[base + k], to)

        wcopy = functools.partial(copy, 0, g_ref)
        scopy = functools.partial(copy, 8, gs_ref)

        def relay(k, block, half, to):
            rows = pl.ds(half * (D // 2), D // 2)
            ref = g_ref.at[_index_of(*block), rows]
            return _rcopy(ref, ref, send_sems.at[k], recv_sems.at[k], to)

        def load(src_ref, slot):
            cp = pltpu.make_async_copy(src_ref, win_ref.at[slot], local_sems.at[0])
            cp.start()
            cp.wait()

        def place(slot, block):
            b = _index_of(*block)

            def rows_chunk(r, carry):
                rows = pl.ds(pl.multiple_of(r * tr, tr), tr)
                tmp_ref[:, :cw] = win_ref[slot, rows, :].astype(F32)
                shifted = pltpu.roll(tmp_ref[...], 2 * b, 1)
                for u in range(7):
                    wt_ref[6 * b + u, rows, :] = (wt_ref[6 * b + u, rows, :].astype(F32) + shifted[:, 128 * u:128 * (u + 1)]).astype(BF16)
                return carry

            lax.fori_loop(0, D // tr, rows_chunk, 0)

        small_own = pltpu.make_async_copy(sm_ref, gs_ref.at[_index_of(*me)], local_sems.at[1])
        small_own.start()
        first = [wcopy(1 + j, me, (*chip, c), src=sh_ref) for j, chip in enumerate(chips[:2])]
        first += [scopy(0, me, sibling, src=sm_ref)] + [scopy(1 + j, me, (*chip, c), src=sm_ref) for j, chip in enumerate(chips)]
        for cp in first:
            cp.start()

        def clear(t, carry):
            wt_ref[t] = jnp.zeros((D, 128), BF16)
            return carry

        lax.fori_loop(0, n_tiles, clear, 0)
        tmp_ref[...] = jnp.zeros_like(tmp_ref)

        def emit(t):
            pltpu.make_async_copy(wt_ref.at[t], wn_ref.at[t], local_sems.at[2]).start()

        def take(block, slot, arrivals=None, pass_on=None):
            for cp in arrivals or ():
                cp.wait_recv()
            load(sh_ref if arrivals is None else g_ref.at[_index_of(*block)], slot)
            if pass_on is not None:
                pass_on.start()
            place(slot, block)
            for u in range(1, 6):
                emit(6 * _index_of(*block) + u)

        near_x, near_y, far = [(*chip, c) for chip in chips]
        to_sibling = wcopy(0, me, sibling, src=win_ref.at[0])
        pass_x = wcopy(4, near_x, sibling, src=win_ref.at[1])
        pass_y = wcopy(5, near_y, sibling, src=win_ref.at[0])
        pass_d = wcopy(6, far, sibling, src=win_ref.at[0])
        relays = [relay(3, near_x, 0, near_y), relay(7, near_y, 1, near_x)]
        take(me, 0, pass_on=to_sibling)
        take(near_x, 1, [wcopy(1, near_x, me)], pass_x)
        relays[0].start()
        to_sibling.wait_send()
        take(near_y, 0, [wcopy(2, near_y, me)], pass_y)
        relays[1].start()
        small_passed = []
        for j, chip in enumerate(chips):
            scopy(1 + j, (*chip, c), me).wait_recv()
            cp = scopy(4 + j, (*chip, c), sibling)
            cp.start()
            small_passed.append(cp)
        pass_x.wait_send()
        take(sibling, 1, [wcopy(0, sibling, me)])
        pass_y.wait_send()
        take((*chips[0], 1 - c), 0, [wcopy(4, (*chips[0], 1 - c), me)])
        take((*chips[1], 1 - c), 1, [wcopy(5, (*chips[1], 1 - c), me)])
        take(far, 0, [relay(3, far, 0, near_y), relay(7, far, 1, near_x)], pass_d)
        take((*chips[2], 1 - c), 1, [wcopy(6, (*chips[2], 1 - c), me)])
        for t in range(0, n_tiles, 6):
            emit(t)
        scopy(0, sibling, me).wait_recv()
        for j, chip in enumerate(chips):
            scopy(4 + j, (*chip, 1 - c), me).wait_recv()
        for cp in first + small_passed + relays + [pass_d]:
            cp.wait_send()
        small_own.wait()
        pltpu.make_async_copy(wn_ref, wn_ref, local_sems.at[2]).wait()

    wn, _, gs = pl.pallas_call(
        body,
        in_specs=[HBM_SPEC] * 2, out_specs=[HBM_SPEC] * 3,
        out_shape=[_sds((n_tiles, D, 128), BF16), _sds((N_DEV,) + shard.shape, BF16), _sds((N_DEV,) + small.shape, small.dtype)],
        scratch_shapes=[pltpu.VMEM((n_tiles, D, 128), BF16), pltpu.VMEM((2, D, cw), BF16), pltpu.VMEM((tr, 7 * 128), F32),
                        pltpu.SemaphoreType.DMA((15,)), pltpu.SemaphoreType.DMA((15,)), pltpu.SemaphoreType.DMA((3,))],
        compiler_params=pltpu.CompilerParams(vmem_limit_bytes=48 * 1024 * 1024),
        name=name,
    )(shard, small)
    return wn, gs


def _mm(a, b, mode, out_dtype, *, tm, tn, tk, name, b_blocked=False, b_tiled=False, out_blocked=False, m_tiles=None, pair=None,
        phases=()):
    if mode == "nn":
        (m, k), dims = a.shape, NN
        a_blk, a_map = (tm, tk), (lambda i, j, kk: (i, kk))
        if b_blocked:
            assert b.shape[1] == k and b.shape[2] == tn and tk == k
            n = b.shape[0] * tn
            b_spec = pl.BlockSpec((None, tk, tn), lambda i, j, kk: (j, kk, 0))
        elif b_tiled:
            assert b.shape[1] == k and b.shape[2] == 128 and tn % 128 == 0
            n = b.shape[0] * 128
            b_spec = pl.BlockSpec((tn // 128, tk, 128), lambda i, j, kk: (j, kk, 0))
        else:
            assert b.shape[0] == k
            n = b.shape[1]
            b_spec = pl.BlockSpec((tk, tn), lambda i, j, kk: (kk, j))
    elif mode == "tn":
        (k, m), n, dims = a.shape, b.shape[1], TN
        assert b.shape[0] == k
        first = 0 if m_tiles is None else m_tiles[0]
        a_blk, a_map = (tk, tm), (lambda i, j, kk: (kk, i + first))
        b_spec = pl.BlockSpec((tk, tn), lambda i, j, kk: (kk, j))
    else:
        (m, k), dims = a.shape, NT
        a_blk, a_map = (tm, tk), (lambda i, j, kk: (i, kk))
        if b_blocked:
            assert b.shape[0] * b.shape[2] == k and b.shape[2] == tk
            n = b.shape[1]
            b_spec = pl.BlockSpec((None, tn, tk), lambda i, j, kk: (kk, j, 0))
        elif b_tiled:
            assert b.shape[0] * 128 == k and b.shape[2] == 128 and tk % 128 == 0
            n = b.shape[1]
            b_spec = pl.BlockSpec((tk // 128, tn, 128), lambda i, j, kk: (kk, j, 0))
        else:
            assert b.shape[1] == k
            n = b.shape[0]
            b_spec = pl.BlockSpec((tn, tk), lambda i, j, kk: (j, kk))
    assert m % tm == 0 and n % tn == 0 and k % tk == 0, (a.shape, b.shape, mode)
    nk = k // tk
    n_row_tiles = m // tm if m_tiles is None else m_tiles[1]
    if out_blocked:
        out_shape, out_spec = _sds((n // tn, n_row_tiles * tm, tn), out_dtype), pl.BlockSpec((None, tm, tn), lambda i, j, kk: (j, i, 0))
    else:
        out_shape, out_spec = _sds((n_row_tiles * tm, n), out_dtype), pl.BlockSpec((tm, tn), lambda i, j, kk: (i, j))

    grid = (n_row_tiles, n // tn, nk)

    def body(a_ref, b_ref, o_ref, *rest):
        rhs = jnp.concatenate([b_ref[u] for u in range(b_ref.shape[0])], axis=1) if b_tiled else b_ref[...]
        p = _dot(a_ref[...], rhs, dims)
        if nk == 1:
            o_ref[...] = p.astype(out_dtype)
            if pair is not None:
                _send_to_sibling(p.astype(out_dtype), *rest)
        else:
            acc_ref, = rest
            kk = pl.program_id(2)

            @pl.when(kk == 0)
            def _():
                acc_ref[...] = p

            @pl.when(kk > 0)
            def _():
                acc_ref[...] += p

            @pl.when(kk == nk - 1)
            def _():
                o_ref[...] = acc_ref[...].astype(out_dtype)

    def _send_to_sibling(tile, pair_ref, stage_ref, send_sems, recv_sem):
        i, j = pl.program_id(0), pl.program_id(1)
        x, y, c = _place()
        blk = pair["block"](i, j)
        k = ((blk >> 2) ^ x) + 2 * (((blk >> 1) & 1) ^ y)
        ordinal = pair["ordinal"](i, j)

        def send(slot):
            return _rcopy(stage_ref.at[slot], pair["dst"](pair_ref, k, i, j), send_sems.at[slot], recv_sem.at[0], (x, y, 1 - c))

        @pl.when((blk & 1) != c)
        def _():
            slot = ordinal & 1

            @pl.when(ordinal >= 2)
            def _():
                send(slot).wait_send()

            stage_ref[slot] = tile
            send(slot).start()

        @pl.when((i == grid[0] - 1) & (j == grid[1] - 1))
        def _():
            send(0).wait_send()
            send(1).wait_send()
            _rcopy(pair_ref, pair_ref, send_sems.at[0], recv_sem.at[0], (x, y, 1 - c)).wait_recv()

    blocks = _nbytes(a_blk, a.dtype) + tk * tn * jnp.dtype(b.dtype).itemsize + _nbytes((tm, tn), out_dtype)
    out_specs, out_shapes, scratch = [out_spec], [out_shape], [] if nk == 1 else [pltpu.VMEM((tm, tn), F32)]
    scratch_bytes = _nbytes((tm, tn), F32) * (nk > 1)
    if pair is not None:
        assert nk == 1
        out_specs, out_shapes = out_specs + [HBM_SPEC], out_shapes + [pair["like"]]
        scratch = [pltpu.VMEM((2, tm, tn), out_dtype), pltpu.SemaphoreType.DMA((2,)), pltpu.SemaphoreType.DMA((1,))]
        scratch_bytes = 2 * _nbytes((tm, tn), out_dtype)
    outs, bufs = _hosted(
        body, name=name, grid=grid,
        in_specs=[pl.BlockSpec(a_blk, a_map), b_spec], out_specs=out_specs, out_shape=out_shapes, args=[a, b],
        scratch_shapes=scratch, block_bytes=blocks, scratch_bytes=scratch_bytes, phases=phases)
    return outs[0], outs[1:] + bufs


RB = 256


def _row_spec(width):
    return pl.BlockSpec((RB, width), lambda i: (i, 0))


def _vec_spec(width):
    return pl.BlockSpec((1, width), lambda i: (0, 0))


def _rinv(x):
    return lax.rsqrt(jnp.mean(x * x, axis=-1, keepdims=True) + EPS)


def _norm_bwd(dyn, xhat, r):
    return r * (dyn - xhat * jnp.mean(dyn * xhat, axis=-1, keepdims=True))


def _colsum(x):
    return jnp.sum(x, axis=0, keepdims=True)


def _accumulate(ref, value):
    @pl.when(pl.program_id(0) == 0)
    def _():
        ref[...] = value

    @pl.when(pl.program_id(0) > 0)
    def _():
        ref[...] += value


def _prenorm(x, gain):
    def body(x_ref, g_ref, h_ref):
        xv = x_ref[...]
        h_ref[...] = (xv * _rinv(xv) * g_ref[...]).astype(BF16)

    outs, _ = _hosted(body, name="prenorm", grid=(S // RB,), in_specs=[_row_spec(D), _vec_spec(D)], out_specs=[_row_spec(D)],
                      out_shape=[_sds((S, D), BF16)], args=[x, gain], block_bytes=RB * D * 6)
    return outs[0]


def _mid_fwd(x, y, npost, npre, phases=()):
    def body(x_ref, y_ref, po_ref, pr_ref, x1_ref, h1_ref):
        yv = y_ref[...]
        x1 = x_ref[...] + yv * _rinv(yv) * po_ref[...]
        x1_ref[...] = x1
        h1_ref[...] = (x1 * _rinv(x1) * pr_ref[...]).astype(BF16)

    return _hosted(body, name="mid_fwd", grid=(S // RB,), in_specs=[_row_spec(D), _row_spec(D), _vec_spec(D), _vec_spec(D)],
                   out_specs=[_row_spec(D), _row_spec(D)], out_shape=[_sds((S, D), F32), _sds((S, D), BF16)],
                   args=[x, y, npost, npre], block_bytes=RB * D * 14, phases=phases)


def _final(x1, y1, tgt, npost):
    def body(x_ref, y_ref, t_ref, po_ref, loss_ref, dx_ref, dy_ref, dpo_ref):
        yv = y_ref[...]
        r = _rinv(yv)
        yhat = yv * r
        err = x_ref[...] + yhat * po_ref[...] - t_ref[...]
        dx = err * (1.0 / D)
        dx_ref[...] = dx
        dy_ref[...] = _norm_bwd(dx * po_ref[...], yhat, r).astype(BF16)
        _accumulate(loss_ref, _colsum(err * err))
        _accumulate(dpo_ref, _colsum(dx * yhat))

    outs, _ = _hosted(body, name="final", grid=(S // RB,), in_specs=[_row_spec(D), _row_spec(D), _row_spec(D), _vec_spec(D)],
                      out_specs=[_vec_spec(D), _row_spec(D), _row_spec(D), _vec_spec(D)],
                      out_shape=[_sds((1, D), F32), _sds((S, D), F32), _sds((S, D), BF16), _sds((1, D), F32)],
                      args=[x1, y1, tgt, npost], block_bytes=RB * D * 18)
    return outs


def _mid_bwd(dx2, dh1, x1, y0, npre, npost, phases=()):
    def body(dx2_ref, dh_ref, x_ref, y_ref, pr_ref, po_ref, dx1_ref, dy_ref, dpr_ref, dpo_ref):
        xv = x_ref[...]
        r = _rinv(xv)
        xhat = xv * r
        dh = dh_ref[...]
        dx1 = dx2_ref[...] + _norm_bwd(dh * pr_ref[...], xhat, r)
        dx1_ref[...] = dx1
        yv = y_ref[...]
        ry = _rinv(yv)
        yhat = yv * ry
        dy_ref[...] = _norm_bwd(dx1 * po_ref[...], yhat, ry).astype(BF16)
        _accumulate(dpr_ref, _colsum(dh * xhat))
        _accumulate(dpo_ref, _colsum(dx1 * yhat))

    return _hosted(body, name="mid_bwd", grid=(S // RB,), in_specs=[_row_spec(D)] * 4 + [_vec_spec(D)] * 2,
                   out_specs=[_row_spec(D), _row_spec(D), _vec_spec(D), _vec_spec(D)],
                   out_shape=[_sds((S, D), F32), _sds((S, D), BF16), _sds((1, D), F32), _sds((1, D), F32)],
                   args=[dx2, dh1, x1, y0, npre, npost], block_bytes=RB * D * 22, phases=phases)


def _first_bwd(dx1, dh0, x0, npre, phases=()):
    def body(dx1_ref, dh_ref, x_ref, pr_ref, gx_ref, dpr_ref):
        xv = x_ref[...]
        r = _rinv(xv)
        xhat = xv * r
        dh = dh_ref[...]
        gx_ref[...] = dx1_ref[...] + _norm_bwd(dh * pr_ref[...], xhat, r)
        _accumulate(dpr_ref, _colsum(dh * xhat))

    return _hosted(body, name="first_bwd", grid=(S // RB,), in_specs=[_row_spec(D)] * 3 + [_vec_spec(D)],
                   out_specs=[_row_spec(D), _vec_spec(D)], out_shape=[_sds((S, D), F32), _sds((1, D), F32)],
                   args=[dx1, dh0, x0, npre], block_bytes=RB * D * 16, phases=phases)


GLA_RB = 256
GLA_CPB = GLA_RB // C


def _sigmoid(x):
    return 1.0 / (1.0 + jnp.exp(-x))


def _tri(strict):
    r = lax.broadcasted_iota(jnp.int32, (C, C), 0)
    c = lax.broadcasted_iota(jnp.int32, (C, C), 1)
    return jnp.where(c < r if strict else c <= r, 1.0, 0.0).astype(BF16)


def _tri_dot(tri, x):
    hi = x.astype(BF16)
    lo = (x - hi.astype(F32)).astype(BF16)
    return _dot(tri, hi) + _dot(tri, lo)


def _gla_gates(glr_b, w2, b, tri):
    z = _dot(glr_b, w2) + b
    log_a = (jnp.minimum(z, 0.0) - jnp.log(1.0 + jnp.exp(-jnp.abs(z)))) * (1.0 / GLA_TAU)
    bcum = _tri_dot(tri, log_a)
    b_end = jnp.sum(log_a, axis=0, keepdims=True)
    return z, jnp.exp(b_end - bcum), jnp.exp(b_end)


def _gla_fwd(proj, w2p, bgate, ogain, phases=()):
    def body(p_ref, w2_ref, b_ref, og_ref, y_ref, st_out_ref, st_ref):
        @pl.when(pl.program_id(0) == 0)
        def _():
            st_ref[...] = jnp.zeros_like(st_ref)

        tri = _tri(False)

        def chunk(ci, carry):
            rows = pl.ds(pl.multiple_of(ci * C, C), C)
            glr_b = p_ref[rows, LR0:LR0 + LRP].astype(BF16)
            _, ea_all, dec_all = _gla_gates(glr_b, w2_ref[...], b_ref[...], tri)
            for h in range(H):
                ea, dec = ea_all[:, h * DK:(h + 1) * DK], dec_all[:, h * DK:(h + 1) * DK]
                k_dec = (p_ref[rows, K0 + h * DK:K0 + (h + 1) * DK] * ea).astype(BF16)
                v_b = p_ref[rows, V0 + h * DV:V0 + (h + 1) * DV].astype(BF16)
                st = st_ref[h] * dec + _dot(v_b, k_dec, TN)
                st_ref[h] = st
                st_b = st.astype(BF16)
                st_out_ref[ci, h] = st_b
                q_b = (p_ref[rows, Q0 + h * DK:Q0 + (h + 1) * DK] * (DK ** -0.5)).astype(BF16)
                o = _dot(q_b, st_b, NT)
                on = o * _rinv(o)
                g = p_ref[rows, G0 + h * DV:G0 + (h + 1) * DV]
                y_ref[rows, h * DV:(h + 1) * DV] = (on * og_ref[:, h * DV:(h + 1) * DV] * (g * _sigmoid(g))).astype(BF16)
            return carry

        lax.fori_loop(0, GLA_CPB, chunk, 0, unroll=True)

    blocks = GLA_RB * GLA_PAD * 4 + GLA_RB * D * 2 + GLA_CPB * H * DV * DK * 2
    return _hosted(
        body, name="gla_fwd", grid=(S // GLA_RB,),
        in_specs=[pl.BlockSpec((GLA_RB, GLA_PAD), lambda i: (i, 0)),
                  pl.BlockSpec((LRP, H * DK), lambda i: (0, 0)),
                  pl.BlockSpec((1, H * DK), lambda i: (0, 0)),
                  pl.BlockSpec((1, H * DV), lambda i: (0, 0))],
        out_specs=[pl.BlockSpec((GLA_RB, H * DV), lambda i: (i, 0)),
                   pl.BlockSpec((GLA_CPB, H, DV, DK), lambda i: (i, 0, 0, 0))],
        out_shape=[_sds((S, H * DV), BF16), _sds((NC, H, DV, DK), BF16)],
        args=[proj, w2p, bgate, ogain], scratch_shapes=[pltpu.VMEM((H, DV, DK), F32)],
        block_bytes=blocks, scratch_bytes=H * DV * DK * 4, phases=phases)


def _gla_bwd(proj, dypre, states, w2p, bgate, ogain, phases=()):
    nb = S // GLA_RB

    def body(p_ref, dy_ref, st_blk_ref, st_prev_ref, w2_ref, b_ref, og_ref,
             dp_ref, dog_ref, dbg_ref, dw2_ref, r_ref):
        step = pl.program_id(0)

        @pl.when(step == 0)
        def _():
            r_ref[...] = jnp.zeros_like(r_ref)
            dog_ref[...] = jnp.zeros_like(dog_ref)
            dbg_ref[...] = jnp.zeros_like(dbg_ref)
            dw2_ref[...] = jnp.zeros_like(dw2_ref)

        tri = _tri(False)
        tri_strict = _tri(True)
        has_prev = jnp.where(step < nb - 1, 1.0, 0.0).astype(F32)

        def chunk(ci, st_prev_of):
            rows = pl.ds(ci * C if isinstance(ci, int) else pl.multiple_of(ci * C, C), C)
            glr_b = p_ref[rows, LR0:LR0 + LRP].astype(BF16)
            z, ea_all, dec_all = _gla_gates(glr_b, w2_ref[...], b_ref[...], tri)
            d_a, d_end = [], []
            for h in range(H):
                kcol = slice(h * DK, (h + 1) * DK)
                vcol = slice(h * DV, (h + 1) * DV)
                ea, dec = ea_all[:, kcol], dec_all[:, kcol]
                k_dec = p_ref[rows, K0 + h * DK:K0 + (h + 1) * DK] * ea
                k_dec_b = k_dec.astype(BF16)
                v_b = p_ref[rows, V0 + h * DV:V0 + (h + 1) * DV].astype(BF16)
                q_b = (p_ref[rows, Q0 + h * DK:Q0 + (h + 1) * DK] * (DK ** -0.5)).astype(BF16)
                st_b = st_blk_ref[ci, h]
                o = _dot(q_b, st_b, NT)
                rinv = _rinv(o)
                on = o * rinv
                g = p_ref[rows, G0 + h * DV:G0 + (h + 1) * DV]
                sg = _sigmoid(g)
                og = og_ref[:, vcol]
                dyp = dy_ref[rows, vcol]
                dp_ref[rows, G0 + h * DV:G0 + (h + 1) * DV] = (dyp * (on * og) * (sg * (1.0 + g * (1.0 - sg)))).astype(BF16)
                dpn = dyp * (g * sg)
                dog_ref[:, vcol] += _colsum(dpn * on)
                do_b = _norm_bwd(dpn * og, on, rinv).astype(BF16)
                gt = _dot(do_b, q_b, TN) + r_ref[h]
                gt_b = gt.astype(BF16)
                dp_ref[rows, Q0 + h * DK:Q0 + (h + 1) * DK] = (_dot(do_b, st_b) * (DK ** -0.5)).astype(BF16)
                dkd = _dot(v_b, gt_b)
                dp_ref[rows, V0 + h * DV:V0 + (h + 1) * DV] = _dot(k_dec_b, gt_b, NT).astype(BF16)
                dp_ref[rows, K0 + h * DK:K0 + (h + 1) * DK] = (dkd * ea).astype(BF16)
                d_a.append(dkd * k_dec)
                d_end.append(_colsum(gt * st_prev_of(h)) * dec)
                r_ref[h] = gt * dec
            dla = _tri_dot(tri_strict, jnp.concatenate(d_a, axis=1)) + jnp.concatenate(d_end, axis=1)
            dz = dla * (1.0 / GLA_TAU) * (1.0 - _sigmoid(z))
            dz_b = dz.astype(BF16)
            dbg_ref[...] += _colsum(dz)
            dw2_ref[...] += _dot(glr_b, dz_b, TN)
            dp_ref[rows, LR0:LR0 + LRP] = _dot(dz_b, w2_ref[...], NT).astype(BF16)

        def later_chunk(t, carry):
            ci = GLA_CPB - 1 - t
            chunk(ci, lambda h: st_blk_ref[ci - 1, h].astype(F32))
            return carry

        lax.fori_loop(0, GLA_CPB - 1, later_chunk, 0, unroll=True)
        chunk(0, lambda h: st_prev_ref[0, h].astype(F32) * has_prev)

    blocks = (GLA_RB * GLA_PAD * 4 + GLA_RB * D * 4 + (GLA_CPB + 1) * H * DV * DK * 2 + GLA_RB * GLA_PAD * 2)
    rev = lambda i: nb - 1 - i
    return _hosted(
        body, name="gla_bwd", grid=(nb,),
        in_specs=[pl.BlockSpec((GLA_RB, GLA_PAD), lambda i: (rev(i), 0)),
                  pl.BlockSpec((GLA_RB, H * DV), lambda i: (rev(i), 0)),
                  pl.BlockSpec((GLA_CPB, H, DV, DK), lambda i: (rev(i), 0, 0, 0)),
                  pl.BlockSpec((1, H, DV, DK), lambda i: (jnp.maximum(rev(i) * GLA_CPB - 1, 0), 0, 0, 0)),
                  pl.BlockSpec((LRP, H * DK), lambda i: (0, 0)),
                  pl.BlockSpec((1, H * DK), lambda i: (0, 0)),
                  pl.BlockSpec((1, H * DV), lambda i: (0, 0))],
        out_specs=[pl.BlockSpec((GLA_RB, GLA_PAD), lambda i: (rev(i), 0)),
                   pl.BlockSpec((1, H * DV), lambda i: (0, 0)),
                   pl.BlockSpec((1, H * DK), lambda i: (0, 0)),
                   pl.BlockSpec((LRP, H * DK), lambda i: (0, 0))],
        out_shape=[_sds((S, GLA_PAD), BF16), _sds((1, H * DV), F32), _sds((1, H * DK), F32), _sds((LRP, H * DK), F32)],
        args=[proj, dypre, states, states, w2p, bgate, ogain], scratch_shapes=[pltpu.VMEM((H, DV, DK), F32)],
        block_bytes=blocks, scratch_bytes=H * DV * DK * 4, phases=phases)


SGU_RB = 256
GELU_C = 0.7978845608028654
GELU_A = 0.044715


def _gelu(x):
    return 0.5 * x * (1.0 + jnp.tanh(GELU_C * (x + GELU_A * x * x * x)))


def _gelu_grad(x):
    t = jnp.tanh(GELU_C * (x + GELU_A * x * x * x))
    return 0.5 * (1.0 + t) + 0.5 * x * (1.0 - t * t) * (GELU_C * (1.0 + 3.0 * GELU_A * x * x))


def _causal_mask(transposed=False):
    i = lax.broadcasted_iota(jnp.int32, (SGU_BLOCK, SGU_BLOCK), 1 if transposed else 0)
    j = lax.broadcasted_iota(jnp.int32, (SGU_BLOCK, SGU_BLOCK), 0 if transposed else 1)
    return (i >= C) | (j < C)


def _layer_norm(vf, gain, bias):
    mu = jnp.mean(vf, axis=-1, keepdims=True)
    cen = vf - mu
    rstd = lax.rsqrt(jnp.mean(cen * cen, axis=-1, keepdims=True) + EPS)
    xhat = cen * rstd
    return xhat, rstd, xhat * gain + bias


def _sgu_fwd(proj, lng, lnb, ws, bsb, phases=()):
    def body(p_ref, g_ref, b_ref, ws_ref, bs_ref, o_ref):
        mask = _causal_mask()
        for n in range(SGU_RB // SGU_BLOCK):
            rows = slice(n * SGU_BLOCK, (n + 1) * SGU_BLOCK)
            _, _, vn = _layer_norm(_gelu(p_ref[rows, D:2 * D]), g_ref[...], b_ref[...])
            vn_b = vn.astype(BF16)
            for gi in range(SGU_G):
                cols = slice(gi * SGU_GD, (gi + 1) * SGU_GD)
                w = jnp.where(mask, ws_ref[gi], 0.0).astype(BF16)
                vs = _dot(w, vn_b[:, cols]) + bs_ref[gi]
                gate = p_ref[rows, 2 * D + gi * SGU_GD:2 * D + (gi + 1) * SGU_GD]
                o_ref[rows, cols] = (_gelu(p_ref[rows, cols]) * vs * (gate * _sigmoid(gate))).astype(BF16)

    blocks = SGU_RB * SGU_COLS * 4 + SGU_RB * D * 2 + SGU_G * SGU_BLOCK * (SGU_BLOCK + SGU_GD) * 4
    return _hosted(
        body, name="sgu_fwd", grid=(S // SGU_RB,),
        in_specs=[pl.BlockSpec((SGU_RB, SGU_COLS), lambda i: (i, 0)),
                  pl.BlockSpec((1, D), lambda i: (0, 0)), pl.BlockSpec((1, D), lambda i: (0, 0)),
                  pl.BlockSpec((SGU_G, SGU_BLOCK, SGU_BLOCK), lambda i: (0, 0, 0)),
                  pl.BlockSpec((SGU_G, SGU_BLOCK, SGU_GD), lambda i: (0, 0, 0))],
        out_specs=[pl.BlockSpec((SGU_RB, D), lambda i: (i, 0))], out_shape=[_sds((S, D), BF16)],
        args=[proj, lng, lnb, ws, bsb], block_bytes=blocks, phases=phases)


def _sgu_bwd(proj, dpre, lng, lnb, ws, wst, bsb, phases=()):
    nsteps = S // SGU_RB

    def body(p_ref, d_ref, g_ref, b_ref, ws_ref, wst_ref, bs_ref,
             dp_ref, dg_ref, db_ref, dws_ref, dbs_ref, dvn_ref, dvs_acc_ref):
        step = pl.program_id(0)

        @pl.when(step == 0)
        def _():
            dg_ref[...] = jnp.zeros_like(dg_ref)
            db_ref[...] = jnp.zeros_like(db_ref)
            dws_ref[...] = jnp.zeros_like(dws_ref)
            dvs_acc_ref[...] = jnp.zeros_like(dvs_acc_ref)

        mask = _causal_mask()
        maskt = _causal_mask(transposed=True)
        for n in range(SGU_RB // SGU_BLOCK):
            rows = slice(n * SGU_BLOCK, (n + 1) * SGU_BLOCK)
            v = p_ref[rows, D:2 * D]
            xhat, rstd, vn = _layer_norm(_gelu(v), g_ref[...], b_ref[...])
            vn_b = vn.astype(BF16)
            for gi in range(SGU_G):
                cols = slice(gi * SGU_GD, (gi + 1) * SGU_GD)
                w = jnp.where(mask, ws_ref[gi], 0.0).astype(BF16)
                wt = jnp.where(maskt, wst_ref[gi], 0.0).astype(BF16)
                vs = _dot(w, vn_b[:, cols]) + bs_ref[gi]
                u = p_ref[rows, cols]
                gate = p_ref[rows, 2 * D + gi * SGU_GD:2 * D + (gi + 1) * SGU_GD]
                sg = _sigmoid(gate)
                gu = _gelu(u)
                dpre_g = d_ref[rows, cols]
                t = dpre_g * (gate * sg)
                dp_ref[rows, cols] = (t * vs * _gelu_grad(u)).astype(BF16)
                dp_ref[rows, 2 * D + gi * SGU_GD:2 * D + (gi + 1) * SGU_GD] = (
                    dpre_g * gu * vs * (sg * (1.0 + gate * (1.0 - sg)))).astype(BF16)
                dvs = t * gu
                dvs_b = dvs.astype(BF16)
                dvs_acc_ref[:, cols] += dvs
                dws_ref[gi] += _dot(dvs_b, vn_b[:, cols], NT)
                dvn_ref[:, cols] = _dot(wt, dvs_b)
            dvn = dvn_ref[...]
            dg_ref[...] += _colsum(dvn * xhat)
            db_ref[...] += _colsum(dvn)
            dxh = dvn * g_ref[...]
            dvf = rstd * (dxh - jnp.mean(dxh, axis=-1, keepdims=True) - xhat * jnp.mean(dxh * xhat, axis=-1, keepdims=True))
            dp_ref[rows, D:2 * D] = (dvf * _gelu_grad(v)).astype(BF16)

        @pl.when(step == nsteps - 1)
        def _():
            lane = lax.broadcasted_iota(jnp.int32, (SGU_BLOCK, SGU_BLOCK), 1)
            out = jnp.zeros((SGU_BLOCK, SGU_BLOCK), F32)
            for gi in range(SGU_G):
                out = out + jnp.where(lane == gi, jnp.sum(dvs_acc_ref[:, gi * SGU_GD:(gi + 1) * SGU_GD], axis=1, keepdims=True), 0.0)
                dws_ref[gi] = jnp.where(mask, dws_ref[gi], 0.0)
            dbs_ref[...] = out

    blocks = SGU_RB * SGU_COLS * 6 + SGU_RB * D * 4 + SGU_G * SGU_BLOCK * (3 * SGU_BLOCK + SGU_GD) * 4
    const3 = lambda i: (0, 0, 0)
    return _hosted(
        body, name="sgu_bwd", grid=(nsteps,),
        in_specs=[pl.BlockSpec((SGU_RB, SGU_COLS), lambda i: (i, 0)),
                  pl.BlockSpec((SGU_RB, D), lambda i: (i, 0)),
                  pl.BlockSpec((1, D), lambda i: (0, 0)), pl.BlockSpec((1, D), lambda i: (0, 0)),
                  pl.BlockSpec((SGU_G, SGU_BLOCK, SGU_BLOCK), const3),
                  pl.BlockSpec((SGU_G, SGU_BLOCK, SGU_BLOCK), const3),
                  pl.BlockSpec((SGU_G, SGU_BLOCK, SGU_GD), const3)],
        out_specs=[pl.BlockSpec((SGU_RB, SGU_COLS), lambda i: (i, 0)),
                   pl.BlockSpec((1, D), lambda i: (0, 0)), pl.BlockSpec((1, D), lambda i: (0, 0)),
                   pl.BlockSpec((SGU_G, SGU_BLOCK, SGU_BLOCK), const3),
                   pl.BlockSpec((SGU_BLOCK, SGU_BLOCK), lambda i: (0, 0))],
        out_shape=[_sds((S, SGU_COLS), BF16), _sds((1, D), F32), _sds((1, D), F32),
                   _sds((SGU_G, SGU_BLOCK, SGU_BLOCK), F32), _sds((SGU_BLOCK, SGU_BLOCK), F32)],
        args=[proj, dpre, lng, lnb, ws, wst, bsb],
        scratch_shapes=[pltpu.VMEM((SGU_BLOCK, D), F32), pltpu.VMEM((SGU_BLOCK, D), F32)],
        block_bytes=blocks, scratch_bytes=2 * SGU_BLOCK * D * 4, phases=phases)


def _pair_sum(own, a, r0, nr, name, table=None):
    c = own.shape[2]
    tr = 256
    assert r0 % tr == 0 and nr % tr == 0

    def body(own_ref, sib_ref, o_ref):
        o_ref[...] = (own_ref[...].astype(F32) + sib_ref[...].astype(F32)).astype(BF16)

    own_map = ((lambda j, i: (1 + j, r0 // tr + i, 0)) if table is None else
               (lambda j, i, t: (t[1 + j], r0 // tr + i, 0)))
    cpad = -(-c // 128) * 128
    outs, _ = _hosted(
        body, name=name, grid=(3, nr // tr),
        in_specs=[pl.BlockSpec((None, tr, c), own_map),
                  pl.BlockSpec((None, tr, c), lambda j, i, *t: (1 + j, r0 // tr + i, 0))],
        out_specs=[pl.BlockSpec((None, tr, c), lambda j, i, *t: (j, i, 0))], out_shape=[_sds((3, nr, c), BF16)],
        args=[own, a], block_bytes=3 * tr * cpad * 2, table=table)
    return outs[0]


def _adamw_math(w, g, m, v):
    m = ADAM_B1 * m + (1.0 - ADAM_B1) * g
    v = ADAM_B2 * v + (1.0 - ADAM_B2) * (g * g)
    m_hat = m / (1.0 - ADAM_B1 ** ADAM_STEP)
    v_hat = v / (1.0 - ADAM_B2 ** ADAM_STEP)
    delta = -ADAM_LR * (m_hat / (jnp.sqrt(v_hat) + ADAM_EPS) + ADAM_WD * w)
    return delta, m, v


def _sum_adamw(own, a, b, w, m, v, *, name, phases=(), table=None):
    r, c = w.shape
    tr = 256

    def body(own_ref, sib_ref, far_ref, w_ref, m_ref, v_ref, g_ref, d_ref, nm_ref, nv_ref):
        g = own_ref[...].astype(F32) + sib_ref[...].astype(F32)
        for j in range(3):
            g = g + far_ref[j].astype(F32)
        g_ref[...] = g
        d_ref[...], nm_ref[...], nv_ref[...] = _adamw_math(w_ref[...], g, m_ref[...], v_ref[...])

    spec = pl.BlockSpec((tr, c), lambda i, *t: (i, 0))
    own_map = (lambda i: (0, i, 0)) if table is None else (lambda i, t: (t[0], i, 0))
    cpad = -(-c // 128) * 128
    return _hosted(
        body, name=name, grid=(r // tr,),
        in_specs=[pl.BlockSpec((None, tr, c), own_map), pl.BlockSpec((None, tr, c), lambda i, *t: (0, i, 0)),
                  pl.BlockSpec((3, tr, c), lambda i, *t: (0, i, 0)), spec, spec, spec],
        out_specs=[spec] * 4, out_shape=[_sds((r, c), F32)] * 4, args=[own, a, b, w, m, v],
        block_bytes=5 * tr * cpad * 2 + 7 * tr * cpad * 4, phases=phases, table=table)


def _sum_parts(parts, name):
    n, r, c = parts.shape

    def body(p_ref, o_ref):
        g = p_ref[0]
        for j in range(1, n):
            g = g + p_ref[j]
        o_ref[...] = g

    outs, _ = _hosted(body, name=name, grid=(1,), in_specs=[pl.BlockSpec((n, r, c), lambda i: (0, 0, 0))],
                      out_specs=[pl.BlockSpec((r, c), lambda i: (0, 0))], out_shape=[_sds((r, c), F32)], args=[parts],
                      block_bytes=(n + 1) * r * c * 4)
    return outs[0]


def _adamw(w, g, m, v, name):
    def body(w_ref, g_ref, m_ref, v_ref, d_ref, nm_ref, nv_ref):
        d_ref[...], nm_ref[...], nv_ref[...] = _adamw_math(w_ref[...], g_ref[...], m_ref[...], v_ref[...])

    spec = pl.BlockSpec(w.shape, lambda i: (0, 0))
    outs, _ = _hosted(body, name=name, grid=(1,), in_specs=[spec] * 4, out_specs=[spec] * 3, out_shape=[_sds(w.shape, F32)] * 3,
                      args=[w, g, m, v], block_bytes=7 * _nbytes(w.shape, F32))
    return outs


def _blocks_to_columns(g):
    n, r, c = g.shape
    return jnp.transpose(g, (1, 0, 2)).reshape(r, n * c)


def _pack(parts):
    return jnp.concatenate([p.reshape(-1) for p in parts]).reshape(-1, 128)


def _unpack(packed, like):
    flat, outs, off = packed.reshape(-1), [], 0
    for p in like:
        outs.append(flat[off:off + p.size].reshape(p.shape))
        off += p.size
    return outs


def kernel(x, norm_pre, norm_post, gla_w_in, gla_w_gate2, gla_b_gate, gla_o_gain, gla_w_out, sgu_w_in, sgu_ln_gain, sgu_ln_bias, sgu_w_spatial, sgu_b_spatial, sgu_w_out, loss_target, m_norm_pre, m_norm_post, m_gla_w_in, m_gla_w_gate2, m_gla_b_gate, m_gla_o_gain, m_gla_w_out, m_sgu_w_in, m_sgu_ln_gain, m_sgu_ln_bias, m_sgu_w_spatial, m_sgu_b_spatial, m_sgu_w_out, v_norm_pre, v_norm_post, v_gla_w_in, v_gla_w_gate2, v_gla_b_gate, v_gla_o_gain, v_gla_w_out, v_sgu_w_in, v_sgu_ln_gain, v_sgu_ln_bias, v_sgu_w_spatial, v_sgu_b_spatial, v_sgu_w_out):
    me = _index_of(*_place())
    x0 = x.reshape(S, D)
    tgt = loss_target.reshape(S, D)
    npre0, npre1 = norm_pre[0:1], norm_pre[1:2]
    npost0, npost1 = norm_post[0:1], norm_post[1:2]
    ws = sgu_w_spatial[0]
    wst = jnp.transpose(ws, (0, 2, 1))
    bsb = jnp.broadcast_to(sgu_b_spatial[0][:, :, None], (SGU_G, SGU_BLOCK, SGU_GD))
    W_ROWS = D // N_DEV
    IN_COLS_G, IN_COLS_S = GLA_COLS // N_DEV, SGU_COLS // N_DEV

    s_gwi, s_gwo = gla_w_in[0].astype(BF16), gla_w_out[0].astype(BF16)
    s_swi, s_swo = sgu_w_in[0].astype(BF16), sgu_w_out[0].astype(BF16)
    small = jnp.concatenate([jnp.pad(gla_w_gate2[0].reshape(4, 512), ((0, 4), (0, 0))),
                             jnp.pad(jnp.concatenate([sgu_ln_gain, sgu_ln_bias], axis=1), ((0, 7), (0, 0)))], axis=0)

    wg_in, g_small = _gather_first(s_gwi, small, "gather_first")
    w2 =_blocks_to_columns(g_small[:, :4, :].reshape(N_DEV, LR, 128))
    w2p = jnp.pad(w2, ((0, LRP - LR), (0, 0))).astype(BF16)
    lng = g_small[:, 8, :256].reshape(1, D)
    lnb = g_small[:, 8, 256:].reshape(1, D)
    like_gwo, like_swi = _sds((N_DEV, W_ROWS, D), BF16), _sds((N_DEV, D, IN_COLS_S), BF16)

    h0 = _prenorm(x0, npre0)
    proj0, (g_gwo, g_swi) = _mm(h0, wg_in, "nn", F32, tm=1024, tn=896, tk=D, name="gla_in", b_tiled=True, phases=[
        _Phase(like_gwo, None, [_gather_send(s_gwo, 0, W_ROWS)]),
        _Phase(like_swi, None, [_gather_send(s_swi, 0, 768, diagonal=False)])])
    (ypre0, states), (g_gwo, g_swi) = _gla_fwd(proj0, w2p, gla_b_gate, gla_o_gain, phases=[
        _Phase(like_gwo, g_gwo, [_gather_pass(0, W_ROWS)]),
        _Phase(like_swi, g_swi, [_gather_relay(0, 768), _gather_send(s_swi, 768, 512, diagonal=False)])])
    wg_out = g_gwo.reshape(D, D)
    y0, (g_swi,) = _mm(ypre0, wg_out, "nn", F32, tm=1024, tn=1024, tk=D, name="gla_out", phases=[
        _Phase(like_swi, g_swi, [_gather_pass(0, 768), _gather_relay(768, 512), _gather_send(s_swi, 1280, 512, diagonal=False)])])
    (x1, h1), (g_swi,) = _mid_fwd(x0, y0, npost0, npre1, phases=[
        _Phase(like_swi, g_swi, [_gather_pass(768, 512), _gather_relay(1280, 512), _gather_send(s_swi, 1792, 256, diagonal=False)])])
    g_swi, = _carry([_Phase(like_swi, g_swi, [_gather_pass(1280, 512), _gather_relay(1792, 256)])], "relay_sgu_w_in")
    g_swi, = _carry([_Phase(like_swi, g_swi, [_gather_pass(1792, 256)])], "pass_sgu_w_in")
    proj1, (g_swo,) = _mm(h1, g_swi, "nn", F32, tm=1024, tn=IN_COLS_S, tk=D, name="sgu_in", b_blocked=True, phases=[
        _Phase(like_gwo, None, [_gather_send(s_swo, 0, W_ROWS)])])
    (pre1,), (g_swo,) = _sgu_fwd(proj1, lng, lnb, ws, bsb, phases=[_Phase(like_gwo, g_swo, [_gather_pass(0, W_ROWS)])])
    ws_out = g_swo.reshape(D, D)
    y1, _ = _mm(pre1, ws_out, "nn", F32, tm=1024, tn=1024, tk=D, name="sgu_out")
    loss_cols, dx2, dy1, dnpost1 = _final(x1, y1, tgt, npost1)
    loss = lax.psum(0.5 * jnp.sum(loss_cols) / D, ("x", "y", "c"))

    like_b_out, like_b_swi = _sds((3, W_ROWS, D), BF16), _sds((3, D, IN_COLS_S), BF16)
    like_b_gwi = _sds((3, D, IN_COLS_G), BF16)
    row_pair = dict(like=_sds((4, W_ROWS, D), BF16), block=lambda i, j: i, ordinal=lambda i, j: i >> 1,
                    dst=lambda ref, k, i, j: ref.at[k])
    col_pair = dict(like=_sds((4, D, IN_COLS_S), BF16), block=lambda i, j: j, ordinal=lambda i, j: 4 * i + (j >> 1),
                    dst=lambda ref, k, i, j: ref.at[k, pl.ds(pl.multiple_of(i * 1024, 1024), 1024)])

    mine = _own_table()
    dws_out, (a_swo,) = _mm(pre1, dy1, "tn", BF16, tm=W_ROWS, tn=D, tk=S, name="sgu_out_dw", pair=row_pair)
    p_swo = dws_out.reshape(N_DEV, W_ROWS, D)
    t_swo = _pair_sum(p_swo, a_swo, 0, W_ROWS, "pair_sum_sgu_w_out", table=mine)
    dpre1, _ = _mm(dy1, ws_out, "nt", F32, tm=1024, tn=1024, tk=D, name="sgu_out_dx")
    (dproj1, dlng, dlnb, dwsp, dbsp), (b_swo,) = _sgu_bwd(proj1, dpre1, lng, lnb, ws, wst, bsb, phases=[
        _Phase(like_b_out, None, [_reduce_cross(t_swo, 0, 0, W_ROWS)])])
    p_swi, (a_swi,) = _mm(h1, dproj1, "tn", BF16, tm=1024, tn=IN_COLS_S, tk=S, name="sgu_in_dw", out_blocked=True, pair=col_pair)
    t_swi = _pair_sum(p_swi, a_swi, 0, D, "pair_sum_sgu_w_in", table=mine)
    dh1, (b_swi,) = _mm(dproj1, g_swi, "nt", F32, tm=1024, tn=1024, tk=IN_COLS_S, name="sgu_in_dx", b_blocked=True, phases=[
        _Phase(like_b_swi, None, [_reduce_cross(t_swi, 0, 0, 1024)])])
    (dx1, dy0, dnpre1, dnpost0), (b_swi,) = _mid_bwd(dx2, dh1, x1, y0, npre1, npost0, phases=[
        _Phase(like_b_swi, b_swi, [_reduce_cross(t_swi, 1024, 1024, 512)])])
    dwg_out, (a_gwo, b_swi) = _mm(ypre0, dy0, "tn", BF16, tm=W_ROWS, tn=D, tk=S, name="gla_out_dw", pair=row_pair, phases=[
        _Phase(like_b_swi, b_swi, [_reduce_cross(t_swi, 1536, 1536, 512)])])
    p_gwo = dwg_out.reshape(N_DEV, W_ROWS, D)
    t_gwo = _pair_sum(p_gwo, a_gwo, 0, W_ROWS, "pair_sum_gla_w_out", table=mine)
    dypre0, _ = _mm(dy0, wg_out, "nt", F32, tm=1024, tn=1024, tk=D, name="gla_out_dx")
    late = [dnpre1, dnpost1, dlng, dlnb, dwsp, jnp.transpose(dbsp[:, :SGU_G])]
    late_pack = _pack(late)
    (dproj0, dogain, dbgate, dw2), (b_gwo, g_late) = _gla_bwd(proj0, dypre0, states, w2p, gla_b_gate, gla_o_gain, phases=[
        _Phase(like_b_out, None, [_reduce_cross(t_gwo, 0, 0, W_ROWS)]),
        _Phase(_sds((N_DEV,) + late_pack.shape, F32), None, [_gather_send(late_pack, 0, late_pack.shape[0])])])
    half = D // 2
    dwg_in_a, (g_late,) = _mm(h0, dproj0, "tn", BF16, tm=half, tn=896, tk=S, name="gla_in_dw_a", m_tiles=(0, 1), phases=[
        _Phase(_sds((N_DEV,) + late_pack.shape, F32), g_late, [_gather_pass(0, late_pack.shape[0])])])
    own_gwi, a_gwi = _blockify_pair(dwg_in_a, None, None, 0, "blockify_gla_w_in_a")
    t_gwi_a = _pair_sum(own_gwi, a_gwi, 0, half, "pair_sum_gla_w_in_a")
    dwg_in_b, (b_gwi,) = _mm(h0, dproj0, "tn", BF16, tm=half, tn=896, tk=S, name="gla_in_dw_b", m_tiles=(1, 1), phases=[
        _Phase(like_b_gwi, None, [_reduce_cross(t_gwi_a, 0, 0, 512)])])
    own_gwi, a_gwi = _blockify_pair(dwg_in_b, own_gwi, a_gwi, half, "blockify_gla_w_in_b")
    t_gwi_b = _pair_sum(own_gwi, a_gwi, half, half, "pair_sum_gla_w_in_b")
    dh0, (b_gwi,) = _mm(dproj0, wg_in, "nt", F32, tm=1024, tn=1024, tk=896, name="gla_in_dx", b_tiled=True, phases=[
        _Phase(like_b_gwi, b_gwi, [_reduce_cross(t_gwi_a, 512, 512, 512), _reduce_cross(t_gwi_b, 0, half, 512)])])
    (grad_x, dnpre0), (b_gwi,) = _first_bwd(dx1, dh0, x0, npre0, phases=[
        _Phase(like_b_gwi, b_gwi, [_reduce_cross(t_gwi_b, 512, half + 512, 256)])])

    early = [dnpre0, dnpost0, dbgate, dogain, dw2[:LR]]
    early_pack = _pack(early)
    like_early = _sds((N_DEV,) + early_pack.shape, F32)
    (g_swo, d_swo, nm_swo, nv_swo), (b_gwi, g_early) = _sum_adamw(
        p_swo, a_swo, b_swo, sgu_w_out[0], m_sgu_w_out[0], v_sgu_w_out[0], name="adamw_sgu_w_out", table=mine, phases=[
            _Phase(like_b_gwi, b_gwi, [_reduce_cross(t_gwi_b, 768, half + 768, 256)]),
            _Phase(like_early, None, [_gather_send(early_pack, 0, early_pack.shape[0])])])
    (g_swi_, d_swi, nm_swi, nv_swi), (g_early,) = _sum_adamw(
        p_swi, a_swi, b_swi, sgu_w_in[0], m_sgu_w_in[0], v_sgu_w_in[0], name="adamw_sgu_w_in", table=mine, phases=[
            _Phase(like_early, g_early, [_gather_pass(0, early_pack.shape[0])])])
    (g_gwo_, d_gwo, nm_gwo, nv_gwo), _ = _sum_adamw(
        p_gwo, a_gwo, b_gwo, gla_w_out[0], m_gla_w_out[0], v_gla_w_out[0], name="adamw_gla_w_out", table=mine)
    (g_gwi_, d_gwi, nm_gwi, nv_gwi), _ = _sum_adamw(
        own_gwi, a_gwi, b_gwi, gla_w_in[0], m_gla_w_in[0], v_gla_w_in[0], name="adamw_gla_w_in")

    g_npre1, g_npost1, g_lng_full, g_lnb_full, g_wsp, g_bsp = _unpack(_sum_parts(g_late, "sum_late_small_grads"), late)
    g_npre0, g_npost0, g_bgate, g_ogain, g_w2_full = _unpack(_sum_parts(g_early, "sum_early_small_grads"), early)
    g_w2 = lax.dynamic_slice(g_w2_full, (0, me * 128), (LR, 128))
    g_lng = lax.dynamic_slice(g_lng_full, (0, me * 256), (1, 256))
    g_lnb = lax.dynamic_slice(g_lnb_full, (0, me * 256), (1, 256))
    small_g = [jnp.concatenate([g_npre0, g_npre1], 0), jnp.concatenate([g_npost0, g_npost1], 0), g_w2, g_bgate, g_ogain,
               g_lng, g_lnb, g_wsp, g_bsp]
    small_w = [norm_pre, norm_post, gla_w_gate2[0], gla_b_gate, gla_o_gain, sgu_ln_gain, sgu_ln_bias, sgu_w_spatial[0], sgu_b_spatial[0]]
    small_m = [m_norm_pre, m_norm_post, m_gla_w_gate2[0], m_gla_b_gate, m_gla_o_gain, m_sgu_ln_gain, m_sgu_ln_bias, m_sgu_w_spatial[0], m_sgu_b_spatial[0]]
    small_v = [v_norm_pre, v_norm_post, v_gla_w_gate2[0], v_gla_b_gate, v_gla_o_gain, v_sgu_ln_gain, v_sgu_ln_bias, v_sgu_w_spatial[0], v_sgu_b_spatial[0]]
    d_pack, nm_pack, nv_pack = _adamw(_pack(small_w), _pack(small_g), _pack(small_m), _pack(small_v), "adamw_small")

    out_like = [norm_pre, norm_post, gla_w_gate2, gla_b_gate, gla_o_gain, sgu_ln_gain, sgu_ln_bias, sgu_w_spatial, sgu_b_spatial]
    sg_ = [g.reshape(s.shape) for g, s in zip(small_g, out_like)]
    sd_, sm_, sv_ = (_unpack(pk, out_like) for pk in (d_pack, nm_pack, nv_pack))

    def assemble(small_list, w_in_g, w_out_g, w_in_s, w_out_s):
        npre_, npost_, w2_, bg_, og_, lg_, lb_, wsp_, bsp_ = small_list
        return [npre_, npost_, w_in_g[None], w2_, bg_, og_, w_out_g[None], w_in_s[None], lg_, lb_, wsp_, bsp_, w_out_s[None]]

    return (loss, grad_x.reshape(1, S, D),
            *assemble(sg_, g_gwi_, g_gwo_, g_swi_, g_swo),
            *assemble(sd_, d_gwi, d_gwo, d_swi, d_swo),
            *assemble(sm_, nm_gwi, nm_gwo, nm_swi, nm_swo),
            *assemble(sv_, nv_gwi, nv_gwo, nv_swi, nv_swo))
```

```python
import functools

import jax
import jax.numpy as jnp
from jax import lax
from jax.experimental import pallas as pl
from jax.experimental.pallas import tpu as pltpu

F32 = jnp.float32
BF16 = jnp.bfloat16

N_DEV = 8
S = 2048
D = 2048
H = 4
DK = 256
DV = 512
C = 64
NC = S // C
GLA_COLS = 6160
GLA_PAD = 6272
Q0, K0, V0, G0, LR0 = 0, 1024, 2048, 4096, 6144
LR = 16
LRP = 128
SGU_COLS = 6144
SGU_BLOCK = 128
SGU_G = 8
SGU_GD = 256
EPS = 1e-6
GLA_TAU = 16.0

ADAM_LR, ADAM_B1, ADAM_B2, ADAM_EPS, ADAM_WD, ADAM_STEP = 0.001, 0.9, 0.999, 1e-08, 0.01, 10

V7X_VMEM_BYTES = 64 * 1024 * 1024
VMEM_CEILING = V7X_VMEM_BYTES - 6 * 1024 * 1024
MESH = pl.DeviceIdType.MESH
HBM_SPEC = pl.BlockSpec(memory_space=pl.ANY)


def _sds(shape, dtype):
    return jax.ShapeDtypeStruct(tuple(shape), dtype)


def _nbytes(shape, dtype):
    n = 1
    for s in shape:
        n *= s
    return n * jnp.dtype(dtype).itemsize


def _dot(a, b, dims=(((1,), (0,)), ((), ())), precision=None):
    return lax.dot_general(a, b, dims, precision=precision, preferred_element_type=F32)


NN = (((1,), (0,)), ((), ()))
TN = (((0,), (0,)), ((), ()))
NT = (((1,), (1,)), ((), ()))


def _place():
    return lax.axis_index("x"), lax.axis_index("y"), lax.axis_index("c")


def _index_of(px, py, pc):
    return 4 * px + 2 * py + pc


def _chips(x, y):
    return [(1 - x, y), (x, 1 - y), (1 - x, 1 - y)]


def _rcopy(src, dst, send_sem, recv_sem, to):
    return pltpu.make_async_remote_copy(src_ref=src, dst_ref=dst, send_sem=send_sem, recv_sem=recv_sem,
                                        device_id=to, device_id_type=MESH)


class _Move:
    def __init__(self, ins, n_remote, make, stage=None):
        self.ins, self.n_remote, self.make, self.stage = list(ins), n_remote, make, stage

    def scratch(self):
        sems = [pltpu.SemaphoreType.DMA((self.n_remote,)), pltpu.SemaphoreType.DMA((self.n_remote,))]
        return sems if self.stage is None else sems + [pltpu.SemaphoreType.DMA((1,)), pltpu.VMEM(*self.stage)]

    def start(self, in_refs, buf, scratch):
        sends, _, local = self.make(in_refs, buf, scratch[0], scratch[1])
        if local is not None:
            pltpu.make_async_copy(local[0], scratch[3], scratch[2].at[0]).start()
        for cp in sends:
            cp.start()

    def finish(self, in_refs, buf, scratch):
        sends, arrivals, local = self.make(in_refs, buf, scratch[0], scratch[1])
        if local is not None:
            pltpu.make_async_copy(local[0], scratch[3], scratch[2].at[0]).wait()
            out = pltpu.make_async_copy(scratch[3], local[1], scratch[2].at[0])
            out.start()
        for cp in arrivals:
            cp.wait_recv()
        for cp in sends:
            cp.wait_send()
        if local is not None:
            out.wait()


class _Phase:
    def __init__(self, like, so_far, moves):
        self.like, self.so_far, self.moves = like, so_far, list(moves)


def _gather_send(shard, r0, nr, diagonal=True):
    def make(in_refs, g, ss, rs):
        sh, = in_refs
        x, y, c = _place()
        me = _index_of(x, y, c)
        rows = pl.ds(r0, nr)
        peers = [(x, y, 1 - c)] + [(px, py, c) for px, py in _chips(x, y)[:3 if diagonal else 2]]
        sends = [_rcopy(sh.at[rows], g.at[me, rows], ss.at[k], rs.at[k], p) for k, p in enumerate(peers)]
        arrivals = [_rcopy(sh.at[rows], g.at[_index_of(*p), rows], ss.at[k], rs.at[k], p) for k, p in enumerate(peers)]
        return sends, arrivals, (sh.at[rows], g.at[me, rows])

    return _Move([shard], 4 if diagonal else 3, make, stage=((nr, shard.shape[1]), shard.dtype))


def _gather_relay(r0, nr):
    def make(in_refs, g, ss, rs):
        x, y, c = _place()
        nx, ny, nd = [(px, py, c) for px, py in _chips(x, y)]
        first, second = pl.ds(r0, nr // 2), pl.ds(r0 + nr // 2, nr // 2)
        sends = [_rcopy(g.at[_index_of(*nx), first], g.at[_index_of(*nx), first], ss.at[0], rs.at[0], ny),
                 _rcopy(g.at[_index_of(*ny), second], g.at[_index_of(*ny), second], ss.at[1], rs.at[1], nx)]
        arrivals = [_rcopy(g.at[_index_of(*nx), first], g.at[_index_of(*nd), first], ss.at[0], rs.at[0], ny),
                    _rcopy(g.at[_index_of(*ny), second], g.at[_index_of(*nd), second], ss.at[1], rs.at[1], nx)]
        return sends, arrivals, None

    return _Move([], 2, make)


def _gather_pass(r0, nr):
    def make(in_refs, g, ss, rs):
        x, y, c = _place()
        rows = pl.ds(r0, nr)
        sends = [_rcopy(g.at[_index_of(px, py, c), rows], g.at[_index_of(px, py, c), rows], ss.at[j], rs.at[j], (x, y, 1 - c))
                 for j, (px, py) in enumerate(_chips(x, y))]
        arrivals = [_rcopy(g.at[_index_of(px, py, c), rows], g.at[_index_of(px, py, 1 - c), rows], ss.at[j], rs.at[j], (x, y, 1 - c))
                    for j, (px, py) in enumerate(_chips(x, y))]
        return sends, arrivals, None

    return _Move([], 3, make)


def _own_table():
    x, y, c = _place()
    return jnp.stack([_index_of(px, py, c) for px, py in [(x, y)] + _chips(x, y)]).astype(jnp.int32)


def _blockify_pair(dw, own_so_far, a_so_far, dst_r0, name):
    rows, tr, cw, win = dw.shape[0], 256, GLA_COLS // N_DEV, 896
    n_steps = rows // tr

    def body(*refs):
        x_ref, own_ref, a_ref, stage_ref, send_sems, recv_sem = refs[0], *refs[-5:]
        i = pl.program_id(0)
        x, y, c = _place()

        def send(slot, k):
            dst = a_ref.at[k, pl.ds(pl.multiple_of(dst_r0 + i * tr, tr), tr)]
            return _rcopy(stage_ref.at[slot], dst, send_sems.at[slot], recv_sem.at[0], (x, y, 1 - c))

        for j in range(N_DEV):
            window = x_ref[:, 768 * j:768 * j + win].astype(F32)
            tile = (pltpu.roll(window, win - 2 * j, 1) if j else window)[:, :cw].astype(BF16)
            k = ((j >> 2) ^ x) + 2 * (((j >> 1) & 1) ^ y)

            @pl.when((j & 1) == c)
            def _():
                own_ref[k] = tile

            @pl.when((j & 1) != c)
            def _():
                slot = (j >> 1) & 1
                if j >> 1 >= 2:
                    send(slot, k).wait_send()
                else:
                    pl.when(i > 0)(lambda: send(slot, k).wait_send())
                stage_ref[slot] = tile
                send(slot, k).start()

        @pl.when(i == n_steps - 1)
        def _():
            send(0, 0).wait_send()
            send(1, 0).wait_send()
            arrived = a_ref.at[:, pl.ds(dst_r0, rows)]
            _rcopy(arrived, arrived, send_sems.at[0], recv_sem.at[0], (x, y, 1 - c)).wait_recv()

    continues = a_so_far is not None
    own, a = pl.pallas_call(
        body, grid=(n_steps,),
        in_specs=[pl.BlockSpec((tr, GLA_PAD), lambda i: (i, 0))] + [HBM_SPEC] * (2 * continues),
        out_specs=[pl.BlockSpec((4, tr, cw), lambda i: (0, dst_r0 // tr + i, 0)), HBM_SPEC],
        out_shape=[_sds((4, D, cw), BF16), _sds((4, D, cw), BF16)],
        scratch_shapes=[pltpu.VMEM((2, tr, cw), BF16), pltpu.SemaphoreType.DMA((2,)), pltpu.SemaphoreType.DMA((1,))],
        input_output_aliases={1: 0, 2: 1} if continues else {},
        compiler_params=pltpu.CompilerParams(dimension_semantics=("arbitrary",), vmem_limit_bytes=48 * 1024 * 1024),
        name=name,
    )(*([dw] + [own_so_far, a_so_far] * continues))
    return own, a


def _reduce_cross(sums, src_r0, dst_r0, nr):
    def make(in_refs, b, ss, rs):
        t, = in_refs
        x, y, c = _place()
        src, dst = pl.ds(src_r0, nr), pl.ds(dst_r0, nr)
        sends = [_rcopy(t.at[j, src], b.at[j, dst], ss.at[j], rs.at[j], (px, py, c)) for j, (px, py) in enumerate(_chips(x, y))]
        return sends, sends, None

    return _Move([sums], 3, make)


def _hosted(body, *, name, grid, in_specs, out_specs, out_shape, args, scratch_shapes=(), block_bytes, scratch_bytes=0,
            phases=(), table=None):
    n_in, n_out, n_scr = len(args), len(out_shape), len(scratch_shapes)
    all_args, all_out_shape, sems, aliases, layout = list(args), list(out_shape), [], {}, []
    for j, ph in enumerate(phases):
        counts = []
        for mv in ph.moves:
            all_args += mv.ins
            counts.append(len(mv.ins))
            sems += mv.scratch()
        if ph.so_far is not None:
            aliases[len(all_args)] = n_out + j
            all_args.append(ph.so_far)
        layout.append((counts, ph.so_far is not None))
        all_out_shape.append(ph.like)
    n_extra_in = len(all_args) - n_in

    def wrapped(*refs):
        ins, pos = refs[:n_in], n_in
        move_ins = []
        for counts, continues in layout:
            per_move = []
            for cnt in counts:
                per_move.append(refs[pos:pos + cnt])
                pos += cnt
            pos += continues
            move_ins.append(per_move)
        outs = refs[pos:pos + n_out]
        bufs = refs[pos + n_out:pos + n_out + len(phases)]
        pos += n_out + len(phases)
        scratch = refs[pos:pos + n_scr]
        pos += n_scr
        move_sems = []
        for ph in phases:
            per_move = []
            for mv in ph.moves:
                count = len(mv.scratch())
                per_move.append(refs[pos:pos + count])
                pos += count
            move_sems.append(per_move)

        def each_move(fn_name):
            for ph, buf, per_in, per_sem in zip(phases, bufs, move_ins, move_sems):
                for mv, mv_in, mv_sem in zip(ph.moves, per_in, per_sem):
                    getattr(mv, fn_name)(mv_in, buf, mv_sem)

        if phases:
            first = functools.reduce(jnp.logical_and, [pl.program_id(a) == 0 for a in range(len(grid))])
            last = functools.reduce(jnp.logical_and, [pl.program_id(a) == grid[a] - 1 for a in range(len(grid))])
            pl.when(first)(lambda: each_move("start"))
        body(*ins, *outs, *scratch)
        if phases:
            pl.when(last)(lambda: each_move("finish"))

    est = 2 * block_bytes + scratch_bytes
    params = pltpu.CompilerParams(dimension_semantics=("arbitrary",) * len(grid),
                                  vmem_limit_bytes=min(VMEM_CEILING, max(32 * 1024 * 1024, 2 * est)))
    all_in_specs, all_out_specs = list(in_specs) + [HBM_SPEC] * n_extra_in, list(out_specs) + [HBM_SPEC] * len(phases)
    if table is None:
        results = pl.pallas_call(
            wrapped, grid=grid, in_specs=all_in_specs, out_specs=all_out_specs, out_shape=all_out_shape,
            scratch_shapes=list(scratch_shapes) + sems, input_output_aliases=aliases, compiler_params=params, name=name,
        )(*all_args)
    else:
        results = pl.pallas_call(
            lambda table_ref, *refs: wrapped(*refs),
            grid_spec=pltpu.PrefetchScalarGridSpec(num_scalar_prefetch=1, grid=grid, in_specs=all_in_specs, out_specs=all_out_specs,
                                                   scratch_shapes=list(scratch_shapes) + sems),
            out_shape=all_out_shape, input_output_aliases={k + 1: v for k, v in aliases.items()}, compiler_params=params, name=name,
        )(table, *all_args)
    return list(results[:n_out]), list(results[n_out:])


def _carry(phases, name):
    def body(o_ref):
        o_ref[...] = jnp.zeros_like(o_ref)

    _, bufs = _hosted(body, name=name, grid=(1,), in_specs=[], out_specs=[pl.BlockSpec((8, 128), lambda i: (0, 0))],
                      out_shape=[_sds((8, 128), F32)], args=[], block_bytes=8 * 128 * 4, phases=phases)
    return bufs


def _gather_first(shard, small, name):
    cw, tr, n_tiles = shard.shape[1], 256, GLA_PAD // 128

    def body(sh_ref, sm_ref, wn_ref, g_ref, gs_ref, wt_ref, win_ref, tmp_ref, send_sems, recv_sems, local_sems):
        x, y, c = _place()
        me, sibling = (x, y, c), (x, y, 1 - c)
        chips = _chips(x, y)

        def copy(base, out_ref, k, block, to, src=None):
            dst = out_ref.at[_index_of(*block)]
            return _rcopy(dst if src is None else src, dst, send_sems.at[base + k], recv_sems.at[base + k], to)

        wcopy = functools.partial(copy, 0, g_ref)
        scopy = functools.partial(copy, 8, gs_ref)

        def relay(k, block, half, to):
            rows = pl.ds(half * (D // 2), D // 2)
            ref = g_ref.at[_index_of(*block), rows]
            return _rcopy(ref, ref, send_sems.at[k], recv_sems.at[k], to)

        def load(src_ref, slot):
            cp = pltpu.make_async_copy(src_ref, win_ref.at[slot], local_sems.at[0])
            cp.start()
            cp.wait()

        def place(slot, block):
            b = _index_of(*block)

            def rows_chunk(r, carry):
                rows = pl.ds(pl.multiple_of(r * tr, tr), tr)
                tmp_ref[:, :cw] = win_ref[slot, rows, :].astype(F32)
                shifted = pltpu.roll(tmp_ref[...], 2 * b, 1)
                for u in range(7):
                    wt_ref[6 * b + u, rows, :] = (wt_ref[6 * b + u, rows, :].astype(F32) + shifted[:, 128 * u:128 * (u + 1)]).astype(BF16)
                return carry

            lax.fori_loop(0, D // tr, rows_chunk, 0)

        small_own = pltpu.make_async_copy(sm_ref, gs_ref.at[_index_of(*me)], local_sems.at[1])
        small_own.start()
        first = [wcopy(1 + j, me, (*chip, c), src=sh_ref) for j, chip in enumerate(chips[:2])]
        first += [scopy(0, me, sibling, src=sm_ref)] + [scopy(1 + j, me, (*chip, c), src=sm_ref) for j, chip in enumerate(chips)]
        for cp in first:
            cp.start()

        def clear(t, carry):
            wt_ref[t] = jnp.zeros((D, 128), BF16)
            return carry

        lax.fori_loop(0, n_tiles, clear, 0)
        tmp_ref[...] = jnp.zeros_like(tmp_ref)

        def emit(t):
            pltpu.make_async_copy(wt_ref.at[t], wn_ref.at[t], local_sems.at[2]).start()

        def take(block, slot, arrivals=None, pass_on=None):
            for cp in arrivals or ():
                cp.wait_recv()
            load(sh_ref if arrivals is None else g_ref.at[_index_of(*block)], slot)
            if pass_on is not None:
                pass_on.start()
            place(slot, block)
            for u in range(1, 6):
                emit(6 * _index_of(*block) + u)

        near_x, near_y, far = [(*chip, c) for chip in chips]
        to_sibling = wcopy(0, me, sibling, src=win_ref.at[0])
        pass_x = wcopy(4, near_x, sibling, src=win_ref.at[1])
        pass_y = wcopy(5, near_y, sibling, src=win_ref.at[0])
        pass_d = wcopy(6, far, sibling, src=win_ref.at[0])
        relays = [relay(3, near_x, 0, near_y), relay(7, near_y, 1, near_x)]
        take(me, 0, pass_on=to_sibling)
        take(near_x, 1, [wcopy(1, near_x, me)], pass_x)
        relays[0].start()
        to_sibling.wait_send()
        take(near_y, 0, [wcopy(2, near_y, me)], pass_y)
        relays[1].start()
        small_passed = []
        for j, chip in enumerate(chips):
            scopy(1 + j, (*chip, c), me).wait_recv()
            cp = scopy(4 + j, (*chip, c), sibling)
            cp.start()
            small_passed.append(cp)
        pass_x.wait_send()
        take(sibling, 1, [wcopy(0, sibling, me)])
        pass_y.wait_send()
        take((*chips[0], 1 - c), 0, [wcopy(4, (*chips[0], 1 - c), me)])
        take((*chips[1], 1 - c), 1, [wcopy(5, (*chips[1], 1 - c), me)])
        take(far, 0, [relay(3, far, 0, near_y), relay(7, far, 1, near_x)], pass_d)
        take((*chips[2], 1 - c), 1, [wcopy(6, (*chips[2], 1 - c), me)])
        for t in range(0, n_tiles, 6):
            emit(t)
        scopy(0, sibling, me).wait_recv()
        for j, chip in enumerate(chips):
            scopy(4 + j, (*chip, 1 - c), me).wait_recv()
        for cp in first + small_passed + relays + [pass_d]:
            cp.wait_send()
        small_own.wait()
        pltpu.make_async_copy(wn_ref, wn_ref, local_sems.at[2]).wait()

    wn, _, gs = pl.pallas_call(
        body,
        in_specs=[HBM_SPEC] * 2, out_specs=[HBM_SPEC] * 3,
        out_shape=[_sds((n_tiles, D, 128), BF16), _sds((N_DEV,) + shard.shape, BF16), _sds((N_DEV,) + small.shape, small.dtype)],
        scratch_shapes=[pltpu.VMEM((n_tiles, D, 128), BF16), pltpu.VMEM((2, D, cw), BF16), pltpu.VMEM((tr, 7 * 128), F32),
                        pltpu.SemaphoreType.DMA((15,)), pltpu.SemaphoreType.DMA((15,)), pltpu.SemaphoreType.DMA((3,))],
        compiler_params=pltpu.CompilerParams(vmem_limit_bytes=48 * 1024 * 1024),
        name=name,
    )(shard, small)
    return wn, gs


def _mm(a, b, mode, out_dtype, *, tm, tn, tk, name, b_blocked=False, b_tiled=False, out_blocked=False, m_tiles=None, pair=None,
        phases=()):
    if mode == "nn":
        (m, k), dims = a.shape, NN
        a_blk, a_map = (tm, tk), (lambda i, j, kk: (i, kk))
        if b_blocked:
            assert b.shape[1] == k and b.shape[2] == tn and tk == k
            n = b.shape[0] * tn
            b_spec = pl.BlockSpec((None, tk, tn), lambda i, j, kk: (j, kk, 0))
        elif b_tiled:
            assert b.shape[1] == k and b.shape[2] == 128 and tn % 128 == 0
            n = b.shape[0] * 128
            b_spec = pl.BlockSpec((tn // 128, tk, 128), lambda i, j, kk: (j, kk, 0))
        else:
            assert b.shape[0] == k
            n = b.shape[1]
            b_spec = pl.BlockSpec((tk, tn), lambda i, j, kk: (kk, j))
    elif mode == "tn":
        (k, m), n, dims = a.shape, b.shape[1], TN
        assert b.shape[0] == k
        first = 0 if m_tiles is None else m_tiles[0]
        a_blk, a_map = (tk, tm), (lambda i, j, kk: (kk, i + first))
        b_spec = pl.BlockSpec((tk, tn), lambda i, j, kk: (kk, j))
    else:
        (m, k), dims = a.shape, NT
        a_blk, a_map = (tm, tk), (lambda i, j, kk: (i, kk))
        if b_blocked:
            assert b.shape[0] * b.shape[2] == k and b.shape[2] == tk
            n = b.shape[1]
            b_spec = pl.BlockSpec((None, tn, tk), lambda i, j, kk: (kk, j, 0))
        elif b_tiled:
            assert b.shape[0] * 128 == k and b.shape[2] == 128 and tk % 128 == 0
            n = b.shape[1]
            b_spec = pl.BlockSpec((tk // 128, tn, 128), lambda i, j, kk: (kk, j, 0))
        else:
            assert b.shape[1] == k
            n = b.shape[0]
            b_spec = pl.BlockSpec((tn, tk), lambda i, j, kk: (j, kk))
    assert m % tm == 0 and n % tn == 0 and k % tk == 0, (a.shape, b.shape, mode)
    nk = k // tk
    n_row_tiles = m // tm if m_tiles is None else m_tiles[1]
    if out_blocked:
        out_shape, out_spec = _sds((n // tn, n_row_tiles * tm, tn), out_dtype), pl.BlockSpec((None, tm, tn), lambda i, j, kk: (j, i, 0))
    else:
        out_shape, out_spec = _sds((n_row_tiles * tm, n), out_dtype), pl.BlockSpec((tm, tn), lambda i, j, kk: (i, j))

    grid = (n_row_tiles, n // tn, nk)

    def body(a_ref, b_ref, o_ref, *rest):
        rhs = jnp.concatenate([b_ref[u] for u in range(b_ref.shape[0])], axis=1) if b_tiled else b_ref[...]
        p = _dot(a_ref[...], rhs, dims)
        if nk == 1:
            o_ref[...] = p.astype(out_dtype)
            if pair is not None:
                _send_to_sibling(p.astype(out_dtype), *rest)
        else:
            acc_ref, = rest
            kk = pl.program_id(2)

            @pl.when(kk == 0)
            def _():
                acc_ref[...] = p

            @pl.when(kk > 0)
            def _():
                acc_ref[...] += p

            @pl.when(kk == nk - 1)
            def _():
                o_ref[...] = acc_ref[...].astype(out_dtype)

    def _send_to_sibling(tile, pair_ref, stage_ref, send_sems, recv_sem):
        i, j = pl.program_id(0), pl.program_id(1)
        x, y, c = _place()
        blk = pair["block"](i, j)
        k = ((blk >> 2) ^ x) + 2 * (((blk >> 1) & 1) ^ y)
        ordinal = pair["ordinal"](i, j)

        def send(slot):
            return _rcopy(stage_ref.at[slot], pair["dst"](pair_ref, k, i, j), send_sems.at[slot], recv_sem.at[0], (x, y, 1 - c))

        @pl.when((blk & 1) != c)
        def _():
            slot = ordinal & 1

            @pl.when(ordinal >= 2)
            def _():
                send(slot).wait_send()

            stage_ref[slot] = tile
            send(slot).start()

        @pl.when((i == grid[0] - 1) & (j == grid[1] - 1))
        def _():
            send(0).wait_send()
            send(1).wait_send()
            _rcopy(pair_ref, pair_ref, send_sems.at[0], recv_sem.at[0], (x, y, 1 - c)).wait_recv()

    blocks = _nbytes(a_blk, a.dtype) + tk * tn * jnp.dtype(b.dtype).itemsize + _nbytes((tm, tn), out_dtype)
    out_specs, out_shapes, scratch = [out_spec], [out_shape], [] if nk == 1 else [pltpu.VMEM((tm, tn), F32)]
    scratch_bytes = _nbytes((tm, tn), F32) * (nk > 1)
    if pair is not None:
        assert nk == 1
        out_specs, out_shapes = out_specs + [HBM_SPEC], out_shapes + [pair["like"]]
        scratch = [pltpu.VMEM((2, tm, tn), out_dtype), pltpu.SemaphoreType.DMA((2,)), pltpu.SemaphoreType.DMA((1,))]
        scratch_bytes = 2 * _nbytes((tm, tn), out_dtype)
    outs, bufs = _hosted(
        body, name=name, grid=grid,
        in_specs=[pl.BlockSpec(a_blk, a_map), b_spec], out_specs=out_specs, out_shape=out_shapes, args=[a, b],
        scratch_shapes=scratch, block_bytes=blocks, scratch_bytes=scratch_bytes, phases=phases)
    return outs[0], outs[1:] + bufs


RB = 256


def _row_spec(width):
    return pl.BlockSpec((RB, width), lambda i: (i, 0))


def _vec_spec(width):
    return pl.BlockSpec((1, width), lambda i: (0, 0))


def _rinv(x):
    return lax.rsqrt(jnp.mean(x * x, axis=-1, keepdims=True) + EPS)


def _norm_bwd(dyn, xhat, r):
    return r * (dyn - xhat * jnp.mean(dyn * xhat, axis=-1, keepdims=True))


def _colsum(x):
    return jnp.sum(x, axis=0, keepdims=True)


def _accumulate(ref, value):
    @pl.when(pl.program_id(0) == 0)
    def _():
        ref[...] = value

    @pl.when(pl.program_id(0) > 0)
    def _():
        ref[...] += value


def _prenorm(x, gain):
    def body(x_ref, g_ref, h_ref):
        xv = x_ref[...]
        h_ref[...] = (xv * _rinv(xv) * g_ref[...]).astype(BF16)

    outs, _ = _hosted(body, name="prenorm", grid=(S // RB,), in_specs=[_row_spec(D), _vec_spec(D)], out_specs=[_row_spec(D)],
                      out_shape=[_sds((S, D), BF16)], args=[x, gain], block_bytes=RB * D * 6)
    return outs[0]


def _mid_fwd(x, y, npost, npre, phases=()):
    def body(x_ref, y_ref, po_ref, pr_ref, x1_ref, h1_ref):
        yv = y_ref[...]
        x1 = x_ref[...] + yv * _rinv(yv) * po_ref[...]
        x1_ref[...] = x1
        h1_ref[...] = (x1 * _rinv(x1) * pr_ref[...]).astype(BF16)

    return _hosted(body, name="mid_fwd", grid=(S // RB,), in_specs=[_row_spec(D), _row_spec(D), _vec_spec(D), _vec_spec(D)],
                   out_specs=[_row_spec(D), _row_spec(D)], out_shape=[_sds((S, D), F32), _sds((S, D), BF16)],
                   args=[x, y, npost, npre], block_bytes=RB * D * 14, phases=phases)


def _final(x1, y1, tgt, npost):
    def body(x_ref, y_ref, t_ref, po_ref, loss_ref, dx_ref, dy_ref, dpo_ref):
        yv = y_ref[...]
        r = _rinv(yv)
        yhat = yv * r
        err = x_ref[...] + yhat * po_ref[...] - t_ref[...]
        dx = err * (1.0 / D)
        dx_ref[...] = dx
        dy_ref[...] = _norm_bwd(dx * po_ref[...], yhat, r).astype(BF16)
        _accumulate(loss_ref, _colsum(err * err))
        _accumulate(dpo_ref, _colsum(dx * yhat))

    outs, _ = _hosted(body, name="final", grid=(S // RB,), in_specs=[_row_spec(D), _row_spec(D), _row_spec(D), _vec_spec(D)],
                      out_specs=[_vec_spec(D), _row_spec(D), _row_spec(D), _vec_spec(D)],
                      out_shape=[_sds((1, D), F32), _sds((S, D), F32), _sds((S, D), BF16), _sds((1, D), F32)],
                      args=[x1, y1, tgt, npost], block_bytes=RB * D * 18)
    return outs


def _mid_bwd(dx2, dh1, x1, y0, npre, npost, phases=()):
    def body(dx2_ref, dh_ref, x_ref, y_ref, pr_ref, po_ref, dx1_ref, dy_ref, dpr_ref, dpo_ref):
        xv = x_ref[...]
        r = _rinv(xv)
        xhat = xv * r
        dh = dh_ref[...]
        dx1 = dx2_ref[...] + _norm_bwd(dh * pr_ref[...], xhat, r)
        dx1_ref[...] = dx1
        yv = y_ref[...]
        ry = _rinv(yv)
        yhat = yv * ry
        dy_ref[...] = _norm_bwd(dx1 * po_ref[...], yhat, ry).astype(BF16)
        _accumulate(dpr_ref, _colsum(dh * xhat))
        _accumulate(dpo_ref, _colsum(dx1 * yhat))

    return _hosted(body, name="mid_bwd", grid=(S // RB,), in_specs=[_row_spec(D)] * 4 + [_vec_spec(D)] * 2,
                   out_specs=[_row_spec(D), _row_spec(D), _vec_spec(D), _vec_spec(D)],
                   out_shape=[_sds((S, D), F32), _sds((S, D), BF16), _sds((1, D), F32), _sds((1, D), F32)],
                   args=[dx2, dh1, x1, y0, npre, npost], block_bytes=RB * D * 22, phases=phases)


def _first_bwd(dx1, dh0, x0, npre, phases=()):
    def body(dx1_ref, dh_ref, x_ref, pr_ref, gx_ref, dpr_ref):
        xv = x_ref[...]
        r = _rinv(xv)
        xhat = xv * r
        dh = dh_ref[...]
        gx_ref[...] = dx1_ref[...] + _norm_bwd(dh * pr_ref[...], xhat, r)
        _accumulate(dpr_ref, _colsum(dh * xhat))

    return _hosted(body, name="first_bwd", grid=(S // RB,), in_specs=[_row_spec(D)] * 3 + [_vec_spec(D)],
                   out_specs=[_row_spec(D), _vec_spec(D)], out_shape=[_sds((S, D), F32), _sds((1, D), F32)],
                   args=[dx1, dh0, x0, npre], block_bytes=RB * D * 16, phases=phases)


GLA_RB = 256
GLA_CPB = GLA_RB // C


def _sigmoid(x):
    return 1.0 / (1.0 + jnp.exp(-x))


def _tri(strict):
    r = lax.broadcasted_iota(jnp.int32, (C, C), 0)
    c = lax.broadcasted_iota(jnp.int32, (C, C), 1)
    return jnp.where(c < r if strict else c <= r, 1.0, 0.0).astype(BF16)


def _tri_dot(tri, x):
    hi = x.astype(BF16)
    lo = (x - hi.astype(F32)).astype(BF16)
    return _dot(tri, hi) + _dot(tri, lo)


def _gla_gates(glr_b, w2, b, tri):
    z = _dot(glr_b, w2) + b
    log_a = (jnp.minimum(z, 0.0) - jnp.log(1.0 + jnp.exp(-jnp.abs(z)))) * (1.0 / GLA_TAU)
    bcum = _tri_dot(tri, log_a)
    b_end = jnp.sum(log_a, axis=0, keepdims=True)
    return z, jnp.exp(b_end - bcum), jnp.exp(b_end)


def _gla_fwd(proj, w2p, bgate, ogain, phases=()):
    def body(p_ref, w2_ref, b_ref, og_ref, y_ref, st_out_ref, st_ref):
        @pl.when(pl.program_id(0) == 0)
        def _():
            st_ref[...] = jnp.zeros_like(st_ref)

        tri = _tri(False)

        def chunk(ci, carry):
            rows = pl.ds(pl.multiple_of(ci * C, C), C)
            glr_b = p_ref[rows, LR0:LR0 + LRP].astype(BF16)
            _, ea_all, dec_all = _gla_gates(glr_b, w2_ref[...], b_ref[...], tri)
            for h in range(H):
                ea, dec = ea_all[:, h * DK:(h + 1) * DK], dec_all[:, h * DK:(h + 1) * DK]
                k_dec = (p_ref[rows, K0 + h * DK:K0 + (h + 1) * DK] * ea).astype(BF16)
                v_b = p_ref[rows, V0 + h * DV:V0 + (h + 1) * DV].astype(BF16)
                st = st_ref[h] * dec + _dot(v_b, k_dec, TN)
                st_ref[h] = st
                st_b = st.astype(BF16)
                st_out_ref[ci, h] = st_b
                q_b = (p_ref[rows, Q0 + h * DK:Q0 + (h + 1) * DK] * (DK ** -0.5)).astype(BF16)
                o = _dot(q_b, st_b, NT)
                on = o * _rinv(o)
                g = p_ref[rows, G0 + h * DV:G0 + (h + 1) * DV]
                y_ref[rows, h * DV:(h + 1) * DV] = (on * og_ref[:, h * DV:(h + 1) * DV] * (g * _sigmoid(g))).astype(BF16)
            return carry

        lax.fori_loop(0, GLA_CPB, chunk, 0, unroll=True)

    blocks = GLA_RB * GLA_PAD * 4 + GLA_RB * D * 2 + GLA_CPB * H * DV * DK * 2
    return _hosted(
        body, name="gla_fwd", grid=(S // GLA_RB,),
        in_specs=[pl.BlockSpec((GLA_RB, GLA_PAD), lambda i: (i, 0)),
                  pl.BlockSpec((LRP, H * DK), lambda i: (0, 0)),
                  pl.BlockSpec((1, H * DK), lambda i: (0, 0)),
                  pl.BlockSpec((1, H * DV), lambda i: (0, 0))],
        out_specs=[pl.BlockSpec((GLA_RB, H * DV), lambda i: (i, 0)),
                   pl.BlockSpec((GLA_CPB, H, DV, DK), lambda i: (i, 0, 0, 0))],
        out_shape=[_sds((S, H * DV), BF16), _sds((NC, H, DV, DK), BF16)],
        args=[proj, w2p, bgate, ogain], scratch_shapes=[pltpu.VMEM((H, DV, DK), F32)],
        block_bytes=blocks, scratch_bytes=H * DV * DK * 4, phases=phases)


def _gla_bwd(proj, dypre, states, w2p, bgate, ogain, phases=()):
    nb = S // GLA_RB

    def body(p_ref, dy_ref, st_blk_ref, st_prev_ref, w2_ref, b_ref, og_ref,
             dp_ref, dog_ref, dbg_ref, dw2_ref, r_ref):
        step = pl.program_id(0)

        @pl.when(step == 0)
        def _():
            r_ref[...] = jnp.zeros_like(r_ref)
            dog_ref[...] = jnp.zeros_like(dog_ref)
            dbg_ref[...] = jnp.zeros_like(dbg_ref)
            dw2_ref[...] = jnp.zeros_like(dw2_ref)

        tri = _tri(False)
        tri_strict = _tri(True)
        has_prev = jnp.where(step < nb - 1, 1.0, 0.0).astype(F32)

        def chunk(ci, st_prev_of):
            rows = pl.ds(ci * C if isinstance(ci, int) else pl.multiple_of(ci * C, C), C)
            glr_b = p_ref[rows, LR0:LR0 + LRP].astype(BF16)
            z, ea_all, dec_all = _gla_gates(glr_b, w2_ref[...], b_ref[...], tri)
            d_a, d_end = [], []
            for h in range(H):
                kcol = slice(h * DK, (h + 1) * DK)
                vcol = slice(h * DV, (h + 1) * DV)
                ea, dec = ea_all[:, kcol], dec_all[:, kcol]
                k_dec = p_ref[rows, K0 + h * DK:K0 + (h + 1) * DK] * ea
                k_dec_b = k_dec.astype(BF16)
                v_b = p_ref[rows, V0 + h * DV:V0 + (h + 1) * DV].astype(BF16)
                q_b = (p_ref[rows, Q0 + h * DK:Q0 + (h + 1) * DK] * (DK ** -0.5)).astype(BF16)
                st_b = st_blk_ref[ci, h]
                o = _dot(q_b, st_b, NT)
                rinv = _rinv(o)
                on = o * rinv
                g = p_ref[rows, G0 + h * DV:G0 + (h + 1) * DV]
                sg = _sigmoid(g)
                og = og_ref[:, vcol]
                dyp = dy_ref[rows, vcol]
                dp_ref[rows, G0 + h * DV:G0 + (h + 1) * DV] = (dyp * (on * og) * (sg * (1.0 + g * (1.0 - sg)))).astype(BF16)
                dpn = dyp * (g * sg)
                dog_ref[:, vcol] += _colsum(dpn * on)
                do_b = _norm_bwd(dpn * og, on, rinv).astype(BF16)
                gt = _dot(do_b, q_b, TN) + r_ref[h]
                gt_b = gt.astype(BF16)
                dp_ref[rows, Q0 + h * DK:Q0 + (h + 1) * DK] = (_dot(do_b, st_b) * (DK ** -0.5)).astype(BF16)
                dkd = _dot(v_b, gt_b)
                dp_ref[rows, V0 + h * DV:V0 + (h + 1) * DV] = _dot(k_dec_b, gt_b, NT).astype(BF16)
                dp_ref[rows, K0 + h * DK:K0 + (h + 1) * DK] = (dkd * ea).astype(BF16)
                d_a.append(dkd * k_dec)
                d_end.append(_colsum(gt * st_prev_of(h)) * dec)
                r_ref[h] = gt * dec
            dla = _tri_dot(tri_strict, jnp.concatenate(d_a, axis=1)) + jnp.concatenate(d_end, axis=1)
            dz = dla * (1.0 / GLA_TAU) * (1.0 - _sigmoid(z))
            dz_b = dz.astype(BF16)
            dbg_ref[...] += _colsum(dz)
            dw2_ref[...] += _dot(glr_b, dz_b, TN)
            dp_ref[rows, LR0:LR0 + LRP] = _dot(dz_b, w2_ref[...], NT).astype(BF16)

        def later_chunk(t, carry):
            ci = GLA_CPB - 1 - t
            chunk(ci, lambda h: st_blk_ref[ci - 1, h].astype(F32))
            return carry

        lax.fori_loop(0, GLA_CPB - 1, later_chunk, 0, unroll=True)
        chunk(0, lambda h: st_prev_ref[0, h].astype(F32) * has_prev)

    blocks = (GLA_RB * GLA_PAD * 4 + GLA_RB * D * 4 + (GLA_CPB + 1) * H * DV * DK * 2 + GLA_RB * GLA_PAD * 2)
    rev = lambda i: nb - 1 - i
    return _hosted(
        body, name="gla_bwd", grid=(nb,),
        in_specs=[pl.BlockSpec((GLA_RB, GLA_PAD), lambda i: (rev(i), 0)),
                  pl.BlockSpec((GLA_RB, H * DV), lambda i: (rev(i), 0)),
                  pl.BlockSpec((GLA_CPB, H, DV, DK), lambda i: (rev(i), 0, 0, 0)),
                  pl.BlockSpec((1, H, DV, DK), lambda i: (jnp.maximum(rev(i) * GLA_CPB - 1, 0), 0, 0, 0)),
                  pl.BlockSpec((LRP, H * DK), lambda i: (0, 0)),
                  pl.BlockSpec((1, H * DK), lambda i: (0, 0)),
                  pl.BlockSpec((1, H * DV), lambda i: (0, 0))],
        out_specs=[pl.BlockSpec((GLA_RB, GLA_PAD), lambda i: (rev(i), 0)),
                   pl.BlockSpec((1, H * DV), lambda i: (0, 0)),
                   pl.BlockSpec((1, H * DK), lambda i: (0, 0)),
                   pl.BlockSpec((LRP, H * DK), lambda i: (0, 0))],
        out_shape=[_sds((S, GLA_PAD), BF16), _sds((1, H * DV), F32), _sds((1, H * DK), F32), _sds((LRP, H * DK), F32)],
        args=[proj, dypre, states, states, w2p, bgate, ogain], scratch_shapes=[pltpu.VMEM((H, DV, DK), F32)],
        block_bytes=blocks, scratch_bytes=H * DV * DK * 4, phases=phases)


SGU_RB = 256
GELU_C = 0.7978845608028654
GELU_A = 0.044715


def _gelu(x):
    return 0.5 * x * (1.0 + jnp.tanh(GELU_C * (x + GELU_A * x * x * x)))


def _gelu_grad(x):
    t = jnp.tanh(GELU_C * (x + GELU_A * x * x * x))
    return 0.5 * (1.0 + t) + 0.5 * x * (1.0 - t * t) * (GELU_C * (1.0 + 3.0 * GELU_A * x * x))


def _causal_mask(transposed=False):
    i = lax.broadcasted_iota(jnp.int32, (SGU_BLOCK, SGU_BLOCK), 1 if transposed else 0)
    j = lax.broadcasted_iota(jnp.int32, (SGU_BLOCK, SGU_BLOCK), 0 if transposed else 1)
    return (i >= C) | (j < C)


def _layer_norm(vf, gain, bias):
    mu = jnp.mean(vf, axis=-1, keepdims=True)
    cen = vf - mu
    rstd = lax.rsqrt(jnp.mean(cen * cen, axis=-1, keepdims=True) + EPS)
    xhat = cen * rstd
    return xhat, rstd, xhat * gain + bias


def _sgu_fwd(proj, lng, lnb, ws, bsb, phases=()):
    def body(p_ref, g_ref, b_ref, ws_ref, bs_ref, o_ref):
        mask = _causal_mask()
        for n in range(SGU_RB // SGU_BLOCK):
            rows = slice(n * SGU_BLOCK, (n + 1) * SGU_BLOCK)
            _, _, vn = _layer_norm(_gelu(p_ref[rows, D:2 * D]), g_ref[...], b_ref[...])
            vn_b = vn.astype(BF16)
            for gi in range(SGU_G):
                cols = slice(gi * SGU_GD, (gi + 1) * SGU_GD)
                w = jnp.where(mask, ws_ref[gi], 0.0).astype(BF16)
                vs = _dot(w, vn_b[:, cols]) + bs_ref[gi]
                gate = p_ref[rows, 2 * D + gi * SGU_GD:2 * D + (gi + 1) * SGU_GD]
                o_ref[rows, cols] = (_gelu(p_ref[rows, cols]) * vs * (gate * _sigmoid(gate))).astype(BF16)

    blocks = SGU_RB * SGU_COLS * 4 + SGU_RB * D * 2 + SGU_G * SGU_BLOCK * (SGU_BLOCK + SGU_GD) * 4
    return _hosted(
        body, name="sgu_fwd", grid=(S // SGU_RB,),
        in_specs=[pl.BlockSpec((SGU_RB, SGU_COLS), lambda i: (i, 0)),
                  pl.BlockSpec((1, D), lambda i: (0, 0)), pl.BlockSpec((1, D), lambda i: (0, 0)),
                  pl.BlockSpec((SGU_G, SGU_BLOCK, SGU_BLOCK), lambda i: (0, 0, 0)),
                  pl.BlockSpec((SGU_G, SGU_BLOCK, SGU_GD), lambda i: (0, 0, 0))],
        out_specs=[pl.BlockSpec((SGU_RB, D), lambda i: (i, 0))], out_shape=[_sds((S, D), BF16)],
        args=[proj, lng, lnb, ws, bsb], block_bytes=blocks, phases=phases)


def _sgu_bwd(proj, dpre, lng, lnb, ws, wst, bsb, phases=()):
    nsteps = S // SGU_RB

    def body(p_ref, d_ref, g_ref, b_ref, ws_ref, wst_ref, bs_ref,
             dp_ref, dg_ref, db_ref, dws_ref, dbs_ref, dvn_ref, dvs_acc_ref):
        step = pl.program_id(0)

        @pl.when(step == 0)
        def _():
            dg_ref[...] = jnp.zeros_like(dg_ref)
            db_ref[...] = jnp.zeros_like(db_ref)
            dws_ref[...] = jnp.zeros_like(dws_ref)
            dvs_acc_ref[...] = jnp.zeros_like(dvs_acc_ref)

        mask = _causal_mask()
        maskt = _causal_mask(transposed=True)
        for n in range(SGU_RB // SGU_BLOCK):
            rows = slice(n * SGU_BLOCK, (n + 1) * SGU_BLOCK)
            v = p_ref[rows, D:2 * D]
            xhat, rstd, vn = _layer_norm(_gelu(v), g_ref[...], b_ref[...])
            vn_b = vn.astype(BF16)
            for gi in range(SGU_G):
                cols = slice(gi * SGU_GD, (gi + 1) * SGU_GD)
                w = jnp.where(mask, ws_ref[gi], 0.0).astype(BF16)
                wt = jnp.where(maskt, wst_ref[gi], 0.0).astype(BF16)
                vs = _dot(w, vn_b[:, cols]) + bs_ref[gi]
                u = p_ref[rows, cols]
                gate = p_ref[rows, 2 * D + gi * SGU_GD:2 * D + (gi + 1) * SGU_GD]
                sg = _sigmoid(gate)
                gu = _gelu(u)
                dpre_g = d_ref[rows, cols]
                t = dpre_g * (gate * sg)
                dp_ref[rows, cols] = (t * vs * _gelu_grad(u)).astype(BF16)
                dp_ref[rows, 2 * D + gi * SGU_GD:2 * D + (gi + 1) * SGU_GD] = (
                    dpre_g * gu * vs * (sg * (1.0 + gate * (1.0 - sg)))).astype(BF16)
                dvs = t * gu
                dvs_b = dvs.astype(BF16)
                dvs_acc_ref[:, cols] += dvs
                dws_ref[gi] += _dot(dvs_b, vn_b[:, cols], NT)
                dvn_ref[:, cols] = _dot(wt, dvs_b)
            dvn = dvn_ref[...]
            dg_ref[...] += _colsum(dvn * xhat)
            db_ref[...] += _colsum(dvn)
            dxh = dvn * g_ref[...]
            dvf = rstd * (dxh - jnp.mean(dxh, axis=-1, keepdims=True) - xhat * jnp.mean(dxh * xhat, axis=-1, keepdims=True))
            dp_ref[rows, D:2 * D] = (dvf * _gelu_grad(v)).astype(BF16)

        @pl.when(step == nsteps - 1)
        def _():
            lane = lax.broadcasted_iota(jnp.int32, (SGU_BLOCK, SGU_BLOCK), 1)
            out = jnp.zeros((SGU_BLOCK, SGU_BLOCK), F32)
            for gi in range(SGU_G):
                out = out + jnp.where(lane == gi, jnp.sum(dvs_acc_ref[:, gi * SGU_GD:(gi + 1) * SGU_GD], axis=1, keepdims=True), 0.0)
                dws_ref[gi] = jnp.where(mask, dws_ref[gi], 0.0)
            dbs_ref[...] = out

    blocks = SGU_RB * SGU_COLS * 6 + SGU_RB * D * 4 + SGU_G * SGU_BLOCK * (3 * SGU_BLOCK + SGU_GD) * 4
    const3 = lambda i: (0, 0, 0)
    return _hosted(
        body, name="sgu_bwd", grid=(nsteps,),
        in_specs=[pl.BlockSpec((SGU_RB, SGU_COLS), lambda i: (i, 0)),
                  pl.BlockSpec((SGU_RB, D), lambda i: (i, 0)),
                  pl.BlockSpec((1, D), lambda i: (0, 0)), pl.BlockSpec((1, D), lambda i: (0, 0)),
                  pl.BlockSpec((SGU_G, SGU_BLOCK, SGU_BLOCK), const3),
                  pl.BlockSpec((SGU_G, SGU_BLOCK, SGU_BLOCK), const3),
                  pl.BlockSpec((SGU_G, SGU_BLOCK, SGU_GD), const3)],
        out_specs=[pl.BlockSpec((SGU_RB, SGU_COLS), lambda i: (i, 0)),
                   pl.BlockSpec((1, D), lambda i: (0, 0)), pl.BlockSpec((1, D), lambda i: (0, 0)),
                   pl.BlockSpec((SGU_G, SGU_BLOCK, SGU_BLOCK), const3),
                   pl.BlockSpec((SGU_BLOCK, SGU_BLOCK), lambda i: (0, 0))],
        out_shape=[_sds((S, SGU_COLS), BF16), _sds((1, D), F32), _sds((1, D), F32),
                   _sds((SGU_G, SGU_BLOCK, SGU_BLOCK), F32), _sds((SGU_BLOCK, SGU_BLOCK), F32)],
        args=[proj, dpre, lng, lnb, ws, wst, bsb],
        scratch_shapes=[pltpu.VMEM((SGU_BLOCK, D), F32), pltpu.VMEM((SGU_BLOCK, D), F32)],
        block_bytes=blocks, scratch_bytes=2 * SGU_BLOCK * D * 4, phases=phases)


def _pair_sum(own, a, r0, nr, name, table=None):
    c = own.shape[2]
    tr = 256
    assert r0 % tr == 0 and nr % tr == 0

    def body(own_ref, sib_ref, o_ref):
        o_ref[...] = (own_ref[...].astype(F32) + sib_ref[...].astype(F32)).astype(BF16)

    own_map = ((lambda j, i: (1 + j, r0 // tr + i, 0)) if table is None else
               (lambda j, i, t: (t[1 + j], r0 // tr + i, 0)))
    cpad = -(-c // 128) * 128
    outs, _ = _hosted(
        body, name=name, grid=(3, nr // tr),
        in_specs=[pl.BlockSpec((None, tr, c), own_map),
                  pl.BlockSpec((None, tr, c), lambda j, i, *t: (1 + j, r0 // tr + i, 0))],
        out_specs=[pl.BlockSpec((None, tr, c), lambda j, i, *t: (j, i, 0))], out_shape=[_sds((3, nr, c), BF16)],
        args=[own, a], block_bytes=3 * tr * cpad * 2, table=table)
    return outs[0]


def _adamw_math(w, g, m, v):
    m = ADAM_B1 * m + (1.0 - ADAM_B1) * g
    v = ADAM_B2 * v + (1.0 - ADAM_B2) * (g * g)
    m_hat = m / (1.0 - ADAM_B1 ** ADAM_STEP)
    v_hat = v / (1.0 - ADAM_B2 ** ADAM_STEP)
    delta = -ADAM_LR * (m_hat / (jnp.sqrt(v_hat) + ADAM_EPS) + ADAM_WD * w)
    return delta, m, v


def _sum_adamw(own, a, b, w, m, v, *, name, phases=(), table=None):
    r, c = w.shape
    tr = 256

    def body(own_ref, sib_ref, far_ref, w_ref, m_ref, v_ref, g_ref, d_ref, nm_ref, nv_ref):
        g = own_ref[...].astype(F32) + sib_ref[...].astype(F32)
        for j in range(3):
            g = g + far_ref[j].astype(F32)
        g_ref[...] = g
        d_ref[...], nm_ref[...], nv_ref[...] = _adamw_math(w_ref[...], g, m_ref[...], v_ref[...])

    spec = pl.BlockSpec((tr, c), lambda i, *t: (i, 0))
    own_map = (lambda i: (0, i, 0)) if table is None else (lambda i, t: (t[0], i, 0))
    cpad = -(-c // 128) * 128
    return _hosted(
        body, name=name, grid=(r // tr,),
        in_specs=[pl.BlockSpec((None, tr, c), own_map), pl.BlockSpec((None, tr, c), lambda i, *t: (0, i, 0)),
                  pl.BlockSpec((3, tr, c), lambda i, *t: (0, i, 0)), spec, spec, spec],
        out_specs=[spec] * 4, out_shape=[_sds((r, c), F32)] * 4, args=[own, a, b, w, m, v],
        block_bytes=5 * tr * cpad * 2 + 7 * tr * cpad * 4, phases=phases, table=table)


def _sum_parts(parts, name):
    n, r, c = parts.shape

    def body(p_ref, o_ref):
        g = p_ref[0]
        for j in range(1, n):
            g = g + p_ref[j]
        o_ref[...] = g

    outs, _ = _hosted(body, name=name, grid=(1,), in_specs=[pl.BlockSpec((n, r, c), lambda i: (0, 0, 0))],
                      out_specs=[pl.BlockSpec((r, c), lambda i: (0, 0))], out_shape=[_sds((r, c), F32)], args=[parts],
                      block_bytes=(n + 1) * r * c * 4)
    return outs[0]


def _adamw(w, g, m, v, name):
    def body(w_ref, g_ref, m_ref, v_ref, d_ref, nm_ref, nv_ref):
        d_ref[...], nm_ref[...], nv_ref[...] = _adamw_math(w_ref[...], g_ref[...], m_ref[...], v_ref[...])

    spec = pl.BlockSpec(w.shape, lambda i: (0, 0))
    outs, _ = _hosted(body, name=name, grid=(1,), in_specs=[spec] * 4, out_specs=[spec] * 3, out_shape=[_sds(w.shape, F32)] * 3,
                      args=[w, g, m, v], block_bytes=7 * _nbytes(w.shape, F32))
    return outs


def _blocks_to_columns(g):
    n, r, c = g.shape
    return jnp.transpose(g, (1, 0, 2)).reshape(r, n * c)


def _pack(parts):
    return jnp.concatenate([p.reshape(-1) for p in parts]).reshape(-1, 128)


def _unpack(packed, like):
    flat, outs, off = packed.reshape(-1), [], 0
    for p in like:
        outs.append(flat[off:off + p.size].reshape(p.shape))
        off += p.size
    return outs


def kernel(x, norm_pre, norm_post, gla_w_in, gla_w_gate2, gla_b_gate, gla_o_gain, gla_w_out, sgu_w_in, sgu_ln_gain, sgu_ln_bias, sgu_w_spatial, sgu_b_spatial, sgu_w_out, loss_target, m_norm_pre, m_norm_post, m_gla_w_in, m_gla_w_gate2, m_gla_b_gate, m_gla_o_gain, m_gla_w_out, m_sgu_w_in, m_sgu_ln_gain, m_sgu_ln_bias, m_sgu_w_spatial, m_sgu_b_spatial, m_sgu_w_out, v_norm_pre, v_norm_post, v_gla_w_in, v_gla_w_gate2, v_gla_b_gate, v_gla_o_gain, v_gla_w_out, v_sgu_w_in, v_sgu_ln_gain, v_sgu_ln_bias, v_sgu_w_spatial, v_sgu_b_spatial, v_sgu_w_out):
    me = _index_of(*_place())
    x0 = x.reshape(S, D)
    tgt = loss_target.reshape(S, D)
    npre0, npre1 = norm_pre[0:1], norm_pre[1:2]
    npost0, npost1 = norm_post[0:1], norm_post[1:2]
    ws = sgu_w_spatial[0]
    wst = jnp.transpose(ws, (0, 2, 1))
    bsb = jnp.broadcast_to(sgu_b_spatial[0][:, :, None], (SGU_G, SGU_BLOCK, SGU_GD))
    W_ROWS = D // N_DEV
    IN_COLS_G, IN_COLS_S = GLA_COLS // N_DEV, SGU_COLS // N_DEV

    s_gwi, s_gwo = gla_w_in[0].astype(BF16), gla_w_out[0].astype(BF16)
    s_swi, s_swo = sgu_w_in[0].astype(BF16), sgu_w_out[0].astype(BF16)
    small = jnp.concatenate([jnp.pad(gla_w_gate2[0].reshape(4, 512), ((0, 4), (0, 0))),
                             jnp.pad(jnp.concatenate([sgu_ln_gain, sgu_ln_bias], axis=1), ((0, 7), (0, 0)))], axis=0)

    wg_in, g_small = _gather_first(s_gwi, small, "gather_first")
    w2 =_blocks_to_columns(g_small[:, :4, :].reshape(N_DEV, LR, 128))
    w2p = jnp.pad(w2, ((0, LRP - LR), (0, 0))).astype(BF16)
    lng = g_small[:, 8, :256].reshape(1, D)
    lnb = g_small[:, 8, 256:].reshape(1, D)
    like_gwo, like_swi = _sds((N_DEV, W_ROWS, D), BF16), _sds((N_DEV, D, IN_COLS_S), BF16)

    h0 = _prenorm(x0, npre0)
    proj0, (g_gwo, g_swi) = _mm(h0, wg_in, "nn", F32, tm=1024, tn=896, tk=D, name="gla_in", b_tiled=True, phases=[
        _Phase(like_gwo, None, [_gather_send(s_gwo, 0, W_ROWS)]),
        _Phase(like_swi, None, [_gather_send(s_swi, 0, 768, diagonal=False)])])
    (ypre0, states), (g_gwo, g_swi) = _gla_fwd(proj0, w2p, gla_b_gate, gla_o_gain, phases=[
        _Phase(like_gwo, g_gwo, [_gather_pass(0, W_ROWS)]),
        _Phase(like_swi, g_swi, [_gather_relay(0, 768), _gather_send(s_swi, 768, 512, diagonal=False)])])
    wg_out = g_gwo.reshape(D, D)
    y0, (g_swi,) = _mm(ypre0, wg_out, "nn", F32, tm=1024, tn=1024, tk=D, name="gla_out", phases=[
        _Phase(like_swi, g_swi, [_gather_pass(0, 768), _gather_relay(768, 512), _gather_send(s_swi, 1280, 512, diagonal=False)])])
    (x1, h1), (g_swi,) = _mid_fwd(x0, y0, npost0, npre1, phases=[
        _Phase(like_swi, g_swi, [_gather_pass(768, 512), _gather_relay(1280, 512), _gather_send(s_swi, 1792, 256, diagonal=False)])])
    g_swi, = _carry([_Phase(like_swi, g_swi, [_gather_pass(1280, 512), _gather_relay(1792, 256)])], "relay_sgu_w_in")
    g_swi, = _carry([_Phase(like_swi, g_swi, [_gather_pass(1792, 256)])], "pass_sgu_w_in")
    proj1, (g_swo,) = _mm(h1, g_swi, "nn", F32, tm=1024, tn=IN_COLS_S, tk=D, name="sgu_in", b_blocked=True, phases=[
        _Phase(like_gwo, None, [_gather_send(s_swo, 0, W_ROWS)])])
    (pre1,), (g_swo,) = _sgu_fwd(proj1, lng, lnb, ws, bsb, phases=[_Phase(like_gwo, g_swo, [_gather_pass(0, W_ROWS)])])
    ws_out = g_swo.reshape(D, D)
    y1, _ = _mm(pre1, ws_out, "nn", F32, tm=1024, tn=1024, tk=D, name="sgu_out")
    loss_cols, dx2, dy1, dnpost1 = _final(x1, y1, tgt, npost1)
    loss = lax.psum(0.5 * jnp.sum(loss_cols) / D, ("x", "y", "c"))

    like_b_out, like_b_swi = _sds((3, W_ROWS, D), BF16), _sds((3, D, IN_COLS_S), BF16)
    like_b_gwi = _sds((3, D, IN_COLS_G), BF16)
    row_pair = dict(like=_sds((4, W_ROWS, D), BF16), block=lambda i, j: i, ordinal=lambda i, j: i >> 1,
                    dst=lambda ref, k, i, j: ref.at[k])
    col_pair = dict(like=_sds((4, D, IN_COLS_S), BF16), block=lambda i, j: j, ordinal=lambda i, j: 4 * i + (j >> 1),
                    dst=lambda ref, k, i, j: ref.at[k, pl.ds(pl.multiple_of(i * 1024, 1024), 1024)])

    mine = _own_table()
    dws_out, (a_swo,) = _mm(pre1, dy1, "tn", BF16, tm=W_ROWS, tn=D, tk=S, name="sgu_out_dw", pair=row_pair)
    p_swo = dws_out.reshape(N_DEV, W_ROWS, D)
    t_swo = _pair_sum(p_swo, a_swo, 0, W_ROWS, "pair_sum_sgu_w_out", table=mine)
    dpre1, _ = _mm(dy1, ws_out, "nt", F32, tm=1024, tn=1024, tk=D, name="sgu_out_dx")
    (dproj1, dlng, dlnb, dwsp, dbsp), (b_swo,) = _sgu_bwd(proj1, dpre1, lng, lnb, ws, wst, bsb, phases=[
        _Phase(like_b_out, None, [_reduce_cross(t_swo, 0, 0, W_ROWS)])])
    p_swi, (a_swi,) = _mm(h1, dproj1, "tn", BF16, tm=1024, tn=IN_COLS_S, tk=S, name="sgu_in_dw", out_blocked=True, pair=col_pair)
    t_swi = _pair_sum(p_swi, a_swi, 0, D, "pair_sum_sgu_w_in", table=mine)
    dh1, (b_swi,) = _mm(dproj1, g_swi, "nt", F32, tm=1024, tn=1024, tk=IN_COLS_S, name="sgu_in_dx", b_blocked=True, phases=[
        _Phase(like_b_swi, None, [_reduce_cross(t_swi, 0, 0, 1024)])])
    (dx1, dy0, dnpre1, dnpost0), _ = _mid_bwd(dx2, dh1, x1, y0, npre1, npost0)
    dwg_out, (a_gwo,) = _mm(ypre0, dy0, "tn", BF16, tm=W_ROWS, tn=D, tk=S, name="gla_out_dw", pair=row_pair)
    p_gwo = dwg_out.reshape(N_DEV, W_ROWS, D)
    t_gwo = _pair_sum(p_gwo, a_gwo, 0, W_ROWS, "pair_sum_gla_w_out", table=mine)
    dypre0, _ = _mm(dy0, wg_out, "nt", F32, tm=1024, tn=1024, tk=D, name="gla_out_dx")
    late = [dnpre1, dnpost1, dlng, dlnb, dwsp, jnp.transpose(dbsp[:, :SGU_G])]
    late_pack = _pack(late)
    (dproj0, dogain, dbgate, dw2), (b_swi, b_gwo, g_late) = _gla_bwd(proj0, dypre0, states, w2p, gla_b_gate, gla_o_gain, phases=[
        _Phase(like_b_swi, b_swi, [_reduce_cross(t_swi, 1024, 1024, 1024)]),
        _Phase(like_b_out, None, [_reduce_cross(t_gwo, 0, 0, W_ROWS)]),
        _Phase(_sds((N_DEV,) + late_pack.shape, F32), None, [_gather_send(late_pack, 0, late_pack.shape[0])])])
    half = D // 2
    dwg_in_a, (g_late,) = _mm(h0, dproj0, "tn", BF16, tm=half, tn=896, tk=S, name="gla_in_dw_a", m_tiles=(0, 1), phases=[
        _Phase(_sds((N_DEV,) + late_pack.shape, F32), g_late, [_gather_pass(0, late_pack.shape[0])])])
    own_gwi, a_gwi = _blockify_pair(dwg_in_a, None, None, 0, "blockify_gla_w_in_a")
    t_gwi_a = _pair_sum(own_gwi, a_gwi, 0, half, "pair_sum_gla_w_in_a")
    dwg_in_b, (b_gwi,) = _mm(h0, dproj0, "tn", BF16, tm=half, tn=896, tk=S, name="gla_in_dw_b", m_tiles=(1, 1), phases=[
        _Phase(like_b_gwi, None, [_reduce_cross(t_gwi_a, 0, 0, 512)])])
    own_gwi, a_gwi = _blockify_pair(dwg_in_b, own_gwi, a_gwi, half, "blockify_gla_w_in_b")
    t_gwi_b = _pair_sum(own_gwi, a_gwi, half, half, "pair_sum_gla_w_in_b")
    dh0, (b_gwi,) = _mm(dproj0, wg_in, "nt", F32, tm=1024, tn=1024, tk=896, name="gla_in_dx", b_tiled=True, phases=[
        _Phase(like_b_gwi, b_gwi, [_reduce_cross(t_gwi_a, 512, 512, 512), _reduce_cross(t_gwi_b, 0, half, half)])])
    (grad_x, dnpre0), _ = _first_bwd(dx1, dh0, x0, npre0)

    early = [dnpre0, dnpost0, dbgate, dogain, dw2[:LR]]
    early_pack = _pack(early)
    like_early = _sds((N_DEV,) + early_pack.shape, F32)
    (g_swo, d_swo, nm_swo, nv_swo), (g_early,) = _sum_adamw(
        p_swo, a_swo, b_swo, sgu_w_out[0], m_sgu_w_out[0], v_sgu_w_out[0], name="adamw_sgu_w_out", table=mine, phases=[
            _Phase(like_early, None, [_gather_send(early_pack, 0, early_pack.shape[0])])])
    (g_gwo_, d_gwo, nm_gwo, nv_gwo), (g_early,) = _sum_adamw(
        p_gwo, a_gwo, b_gwo, gla_w_out[0], m_gla_w_out[0], v_gla_w_out[0], name="adamw_gla_w_out", table=mine, phases=[
            _Phase(like_early, g_early, [_gather_pass(0, early_pack.shape[0])])])
    (g_swi_, d_swi, nm_swi, nv_swi), _ = _sum_adamw(
        p_swi, a_swi, b_swi, sgu_w_in[0], m_sgu_w_in[0], v_sgu_w_in[0], name="adamw_sgu_w_in", table=mine)
    (g_gwi_, d_gwi, nm_gwi, nv_gwi), _ = _sum_adamw(
        own_gwi, a_gwi, b_gwi, gla_w_in[0], m_gla_w_in[0], v_gla_w_in[0], name="adamw_gla_w_in")

    g_npre1, g_npost1, g_lng_full, g_lnb_full, g_wsp, g_bsp = _unpack(_sum_parts(g_late, "sum_late_small_grads"), late)
    g_npre0, g_npost0, g_bgate, g_ogain, g_w2_full = _unpack(_sum_parts(g_early, "sum_early_small_grads"), early)
    g_w2 = lax.dynamic_slice(g_w2_full, (0, me * 128), (LR, 128))
    g_lng = lax.dynamic_slice(g_lng_full, (0, me * 256), (1, 256))
    g_lnb = lax.dynamic_slice(g_lnb_full, (0, me * 256), (1, 256))
    small_g = [jnp.concatenate([g_npre0, g_npre1], 0), jnp.concatenate([g_npost0, g_npost1], 0), g_w2, g_bgate, g_ogain,
               g_lng, g_lnb, g_wsp, g_bsp]
    small_w = [norm_pre, norm_post, gla_w_gate2[0], gla_b_gate, gla_o_gain, sgu_ln_gain, sgu_ln_bias, sgu_w_spatial[0], sgu_b_spatial[0]]
    small_m = [m_norm_pre, m_norm_post, m_gla_w_gate2[0], m_gla_b_gate, m_gla_o_gain, m_sgu_ln_gain, m_sgu_ln_bias, m_sgu_w_spatial[0], m_sgu_b_spatial[0]]
    small_v = [v_norm_pre, v_norm_post, v_gla_w_gate2[0], v_gla_b_gate, v_gla_o_gain, v_sgu_ln_gain, v_sgu_ln_bias, v_sgu_w_spatial[0], v_sgu_b_spatial[0]]
    d_pack, nm_pack, nv_pack = _adamw(_pack(small_w), _pack(small_g), _pack(small_m), _pack(small_v), "adamw_small")

    out_like = [norm_pre, norm_post, gla_w_gate2, gla_b_gate, gla_o_gain, sgu_ln_gain, sgu_ln_bias, sgu_w_spatial, sgu_b_spatial]
    sg_ = [g.reshape(s.shape) for g, s in zip(small_g, out_like)]
    sd_, sm_, sv_ = (_unpack(pk, out_like) for pk in (d_pack, nm_pack, nv_pack))

    def assemble(small_list, w_in_g, w_out_g, w_in_s, w_out_s):
        npre_, npost_, w2_, bg_, og_, lg_, lb_, wsp_, bsp_ = small_list
        return [npre_, npost_, w_in_g[None], w2_, bg_, og_, w_out_g[None], w_in_s[None], lg_, lb_, wsp_, bsp_, w_out_s[None]]

    return (loss, grad_x.reshape(1, S, D),
            *assemble(sg_, g_gwi_, g_gwo_, g_swi_, g_swo),
            *assemble(sd_, d_gwi, d_gwo, d_swi, d_swo),
            *assemble(sm_, nm_gwi, nm_gwo, nm_swi, nm_swo),
            *assemble(sv_, nv_gwi, nv_gwo, nv_swi, nv_swo))
```

```python
import functools

import jax
import jax.numpy as jnp
from jax import lax
from jax.experimental import pallas as pl
from jax.experimental.pallas import tpu as pltpu

F32 = jnp.float32
BF16 = jnp.bfloat16

N_DEV = 8
S = 2048
D = 2048
H = 4
DK = 256
DV = 512
C = 64
NC = S // C
GLA_COLS = 6160
GLA_PAD = 6272
Q0, K0, V0, G0, LR0 = 0, 1024, 2048, 4096, 6144
LR = 16
LRP = 128
SGU_COLS = 6144
SGU_BLOCK = 128
SGU_G = 8
SGU_GD = 256
EPS = 1e-6
GLA_TAU = 16.0

ADAM_LR, ADAM_B1, ADAM_B2, ADAM_EPS, ADAM_WD, ADAM_STEP = 0.001, 0.9, 0.999, 1e-08, 0.01, 10

V7X_VMEM_BYTES = 64 * 1024 * 1024
VMEM_CEILING = V7X_VMEM_BYTES - 6 * 1024 * 1024
HBM_PIN_BYTES = 1024 * 1024
MESH = pl.DeviceIdType.MESH
HBM_SPEC = pl.BlockSpec(memory_space=pl.ANY)


def _sds(shape, dtype):
    return jax.ShapeDtypeStruct(tuple(shape), dtype)


def _nbytes(shape, dtype):
    n = 1
    for s in shape:
        n *= s
    return n * jnp.dtype(dtype).itemsize


def _dot(a, b, dims=(((1,), (0,)), ((), ())), precision=None):
    return lax.dot_general(a, b, dims, precision=precision, preferred_element_type=F32)


NN = (((1,), (0,)), ((), ()))
TN = (((0,), (0,)), ((), ()))
NT = (((1,), (1,)), ((), ()))


def _place():
    return lax.axis_index("x"), lax.axis_index("y"), lax.axis_index("c")


def _index_of(px, py, pc):
    return 4 * px + 2 * py + pc


def _chips(x, y):
    return [(1 - x, y), (x, 1 - y), (1 - x, 1 - y)]


def _rcopy(src, dst, send_sem, recv_sem, to):
    return pltpu.make_async_remote_copy(src_ref=src, dst_ref=dst, send_sem=send_sem, recv_sem=recv_sem,
                                        device_id=to, device_id_type=MESH)


class _Move:
    def __init__(self, ins, n_remote, make, stage=None):
        self.ins, self.n_remote, self.make, self.stage = list(ins), n_remote, make, stage

    def scratch(self):
        sems = [pltpu.SemaphoreType.DMA((self.n_remote,)), pltpu.SemaphoreType.DMA((self.n_remote,))]
        return sems if self.stage is None else sems + [pltpu.SemaphoreType.DMA((1,)), pltpu.VMEM(*self.stage)]

    def start(self, in_refs, buf, scratch):
        sends, _, local = self.make(in_refs, buf, scratch[0], scratch[1])
        if local is not None:
            pltpu.make_async_copy(local[0], scratch[3], scratch[2].at[0]).start()
        for cp in sends:
            cp.start()

    def finish(self, in_refs, buf, scratch):
        sends, arrivals, local = self.make(in_refs, buf, scratch[0], scratch[1])
        if local is not None:
            pltpu.make_async_copy(local[0], scratch[3], scratch[2].at[0]).wait()
            out = pltpu.make_async_copy(scratch[3], local[1], scratch[2].at[0])
            out.start()
        for cp in arrivals:
            cp.wait_recv()
        for cp in sends:
            cp.wait_send()
        if local is not None:
            out.wait()


class _Phase:
    def __init__(self, like, so_far, moves):
        self.like, self.so_far, self.moves = like, so_far, list(moves)


def _gather_send(shard, r0, nr, diagonal=True):
    def make(in_refs, g, ss, rs):
        sh, = in_refs
        x, y, c = _place()
        me = _index_of(x, y, c)
        rows = pl.ds(r0, nr)
        peers = [(x, y, 1 - c)] + [(px, py, c) for px, py in _chips(x, y)[:3 if diagonal else 2]]
        sends = [_rcopy(sh.at[rows], g.at[me, rows], ss.at[k], rs.at[k], p) for k, p in enumerate(peers)]
        arrivals = [_rcopy(sh.at[rows], g.at[_index_of(*p), rows], ss.at[k], rs.at[k], p) for k, p in enumerate(peers)]
        return sends, arrivals, (sh.at[rows], g.at[me, rows])

    return _Move([shard], 4 if diagonal else 3, make, stage=((nr, shard.shape[1]), shard.dtype))


def _gather_relay(r0, nr):
    def make(in_refs, g, ss, rs):
        x, y, c = _place()
        nx, ny, nd = [(px, py, c) for px, py in _chips(x, y)]
        first, second = pl.ds(r0, nr // 2), pl.ds(r0 + nr // 2, nr // 2)
        sends = [_rcopy(g.at[_index_of(*nx), first], g.at[_index_of(*nx), first], ss.at[0], rs.at[0], ny),
                 _rcopy(g.at[_index_of(*ny), second], g.at[_index_of(*ny), second], ss.at[1], rs.at[1], nx)]
        arrivals = [_rcopy(g.at[_index_of(*nx), first], g.at[_index_of(*nd), first], ss.at[0], rs.at[0], ny),
                    _rcopy(g.at[_index_of(*ny), second], g.at[_index_of(*nd), second], ss.at[1], rs.at[1], nx)]
        return sends, arrivals, None

    return _Move([], 2, make)


def _gather_pass(r0, nr):
    def make(in_refs, g, ss, rs):
        x, y, c = _place()
        rows = pl.ds(r0, nr)
        sends = [_rcopy(g.at[_index_of(px, py, c), rows], g.at[_index_of(px, py, c), rows], ss.at[j], rs.at[j], (x, y, 1 - c))
                 for j, (px, py) in enumerate(_chips(x, y))]
        arrivals = [_rcopy(g.at[_index_of(px, py, c), rows], g.at[_index_of(px, py, 1 - c), rows], ss.at[j], rs.at[j], (x, y, 1 - c))
                    for j, (px, py) in enumerate(_chips(x, y))]
        return sends, arrivals, None

    return _Move([], 3, make)


def _own_table():
    x, y, c = _place()
    return jnp.stack([_index_of(px, py, c) for px, py in [(x, y)] + _chips(x, y)]).astype(jnp.int32)


def _blockify_pair(dw, own_so_far, a_so_far, dst_r0, name):
    rows, tr, cw, win = dw.shape[0], 256, GLA_COLS // N_DEV, 896
    n_steps = rows // tr

    def body(*refs):
        x_ref, own_ref, a_ref, stage_ref, send_sems, recv_sem = refs[0], *refs[-5:]
        i = pl.program_id(0)
        x, y, c = _place()

        def send(slot, k):
            dst = a_ref.at[k, pl.ds(pl.multiple_of(dst_r0 + i * tr, tr), tr)]
            return _rcopy(stage_ref.at[slot], dst, send_sems.at[slot], recv_sem.at[0], (x, y, 1 - c))

        for j in range(N_DEV):
            window = x_ref[:, 768 * j:768 * j + win].astype(F32)
            tile = (pltpu.roll(window, win - 2 * j, 1) if j else window)[:, :cw].astype(BF16)
            k = ((j >> 2) ^ x) + 2 * (((j >> 1) & 1) ^ y)

            @pl.when((j & 1) == c)
            def _():
                own_ref[k] = tile

            @pl.when((j & 1) != c)
            def _():
                slot = (j >> 1) & 1
                if j >> 1 >= 2:
                    send(slot, k).wait_send()
                else:
                    pl.when(i > 0)(lambda: send(slot, k).wait_send())
                stage_ref[slot] = tile
                send(slot, k).start()

        @pl.when(i == n_steps - 1)
        def _():
            send(0, 0).wait_send()
            send(1, 0).wait_send()
            arrived = a_ref.at[:, pl.ds(dst_r0, rows)]
            _rcopy(arrived, arrived, send_sems.at[0], recv_sem.at[0], (x, y, 1 - c)).wait_recv()

    continues = a_so_far is not None
    own, a = pl.pallas_call(
        body, grid=(n_steps,),
        in_specs=[pl.BlockSpec((tr, GLA_PAD), lambda i: (i, 0))] + [HBM_SPEC] * (2 * continues),
        out_specs=[pl.BlockSpec((4, tr, cw), lambda i: (0, dst_r0 // tr + i, 0)), HBM_SPEC],
        out_shape=[_sds((4, D, cw), BF16), _sds((4, D, cw), BF16)],
        scratch_shapes=[pltpu.VMEM((2, tr, cw), BF16), pltpu.SemaphoreType.DMA((2,)), pltpu.SemaphoreType.DMA((1,))],
        input_output_aliases={1: 0, 2: 1} if continues else {},
        compiler_params=pltpu.CompilerParams(dimension_semantics=("arbitrary",), vmem_limit_bytes=48 * 1024 * 1024),
        name=name,
    )(*([dw] + [own_so_far, a_so_far] * continues))
    return own, a


def _reduce_cross(sums, src_r0, dst_r0, nr):
    def make(in_refs, b, ss, rs):
        t, = in_refs
        x, y, c = _place()
        src, dst = pl.ds(src_r0, nr), pl.ds(dst_r0, nr)
        sends = [_rcopy(t.at[j, src], b.at[j, dst], ss.at[j], rs.at[j], (px, py, c)) for j, (px, py) in enumerate(_chips(x, y))]
        return sends, sends, None

    return _Move([sums], 3, make)


def _hosted(body, *, name, grid, in_specs, out_specs, out_shape, args, scratch_shapes=(), block_bytes, scratch_bytes=0,
            phases=(), table=None):
    n_in, n_out, n_scr = len(args), len(out_shape), len(scratch_shapes)
    all_args, all_out_shape, sems, aliases, layout = list(args), list(out_shape), [], {}, []
    for j, ph in enumerate(phases):
        counts = []
        for mv in ph.moves:
            all_args += mv.ins
            counts.append(len(mv.ins))
            sems += mv.scratch()
        if ph.so_far is not None:
            aliases[len(all_args)] = n_out + j
            all_args.append(ph.so_far)
        layout.append((counts, ph.so_far is not None))
        all_out_shape.append(ph.like)
    n_extra_in = len(all_args) - n_in

    def wrapped(*refs):
        ins, pos = refs[:n_in], n_in
        move_ins = []
        for counts, continues in layout:
            per_move = []
            for cnt in counts:
                per_move.append(refs[pos:pos + cnt])
                pos += cnt
            pos += continues
            move_ins.append(per_move)
        outs = refs[pos:pos + n_out]
        bufs = refs[pos + n_out:pos + n_out + len(phases)]
        pos += n_out + len(phases)
        scratch = refs[pos:pos + n_scr]
        pos += n_scr
        move_sems = []
        for ph in phases:
            per_move = []
            for mv in ph.moves:
                count = len(mv.scratch())
                per_move.append(refs[pos:pos + count])
                pos += count
            move_sems.append(per_move)

        def each_move(fn_name):
            for ph, buf, per_in, per_sem in zip(phases, bufs, move_ins, move_sems):
                for mv, mv_in, mv_sem in zip(ph.moves, per_in, per_sem):
                    getattr(mv, fn_name)(mv_in, buf, mv_sem)

        if phases:
            first = functools.reduce(jnp.logical_and, [pl.program_id(a) == 0 for a in range(len(grid))])
            last = functools.reduce(jnp.logical_and, [pl.program_id(a) == grid[a] - 1 for a in range(len(grid))])
            pl.when(first)(lambda: each_move("start"))
        body(*ins, *outs, *scratch)
        if phases:
            pl.when(last)(lambda: each_move("finish"))

    all_args = [pltpu.with_memory_space_constraint(a, pltpu.HBM) if a.size * a.dtype.itemsize >= HBM_PIN_BYTES else a
                for a in all_args]
    est = 2 * block_bytes + scratch_bytes
    params = pltpu.CompilerParams(dimension_semantics=("arbitrary",) * len(grid),
                                  vmem_limit_bytes=min(VMEM_CEILING, max(32 * 1024 * 1024, 2 * est)))
    all_in_specs, all_out_specs = list(in_specs) + [HBM_SPEC] * n_extra_in, list(out_specs) + [HBM_SPEC] * len(phases)
    if table is None:
        results = pl.pallas_call(
            wrapped, grid=grid, in_specs=all_in_specs, out_specs=all_out_specs, out_shape=all_out_shape,
            scratch_shapes=list(scratch_shapes) + sems, input_output_aliases=aliases, compiler_params=params, name=name,
        )(*all_args)
    else:
        results = pl.pallas_call(
            lambda table_ref, *refs: wrapped(*refs),
            grid_spec=pltpu.PrefetchScalarGridSpec(num_scalar_prefetch=1, grid=grid, in_specs=all_in_specs, out_specs=all_out_specs,
                                                   scratch_shapes=list(scratch_shapes) + sems),
            out_shape=all_out_shape, input_output_aliases={k + 1: v for k, v in aliases.items()}, compiler_params=params, name=name,
        )(table, *all_args)
    return list(results[:n_out]), list(results[n_out:])


def _carry(phases, name):
    def body(o_ref):
        o_ref[...] = jnp.zeros_like(o_ref)

    _, bufs = _hosted(body, name=name, grid=(1,), in_specs=[], out_specs=[pl.BlockSpec((8, 128), lambda i: (0, 0))],
                      out_shape=[_sds((8, 128), F32)], args=[], block_bytes=8 * 128 * 4, phases=phases)
    return bufs


def _gather_first(shard, small, name):
    cw, tr, n_tiles = shard.shape[1], 256, GLA_PAD // 128

    def body(sh_ref, sm_ref, wn_ref, g_ref, gs_ref, wt_ref, win_ref, tmp_ref, send_sems, recv_sems, local_sems):
        x, y, c = _place()
        me, sibling = (x, y, c), (x, y, 1 - c)
        chips = _chips(x, y)

        def copy(base, out_ref, k, block, to, src=None):
            dst = out_ref.at[_index_of(*block)]
            return _rcopy(dst if src is None else src, dst, send_sems.at[base + k], recv_sems.at[base + k], to)

        wcopy = functools.partial(copy, 0, g_ref)
        scopy = functools.partial(copy, 8, gs_ref)

        def relay(k, block, half, to):
            rows = pl.ds(half * (D // 2), D // 2)
            ref = g_ref.at[_index_of(*block), rows]
            return _rcopy(ref, ref, send_sems.at[k], recv_sems.at[k], to)

        def load(src_ref, slot):
            cp = pltpu.make_async_copy(src_ref, win_ref.at[slot], local_sems.at[0])
            cp.start()
            cp.wait()

        def place(slot, block):
            b = _index_of(*block)

            def rows_chunk(r, carry):
                rows = pl.ds(pl.multiple_of(r * tr, tr), tr)
                tmp_ref[:, :cw] = win_ref[slot, rows, :].astype(F32)
                shifted = pltpu.roll(tmp_ref[...], 2 * b, 1)
                for u in range(7):
                    wt_ref[6 * b + u, rows, :] = (wt_ref[6 * b + u, rows, :].astype(F32) + shifted[:, 128 * u:128 * (u + 1)]).astype(BF16)
                return carry

            lax.fori_loop(0, D // tr, rows_chunk, 0)

        small_own = pltpu.make_async_copy(sm_ref, gs_ref.at[_index_of(*me)], local_sems.at[1])
        small_own.start()
        first = [wcopy(1 + j, me, (*chip, c), src=sh_ref) for j, chip in enumerate(chips[:2])]
        first += [scopy(0, me, sibling, src=sm_ref)] + [scopy(1 + j, me, (*chip, c), src=sm_ref) for j, chip in enumerate(chips)]
        for cp in first:
            cp.start()

        def clear(t, carry):
            wt_ref[t] = jnp.zeros((D, 128), BF16)
            return carry

        lax.fori_loop(0, n_tiles, clear, 0)
        tmp_ref[...] = jnp.zeros_like(tmp_ref)

        def emit(t):
            pltpu.make_async_copy(wt_ref.at[t], wn_ref.at[t], local_sems.at[2]).start()

        def take(block, slot, arrivals=None, pass_on=None):
            for cp in arrivals or ():
                cp.wait_recv()
            load(sh_ref if arrivals is None else g_ref.at[_index_of(*block)], slot)
            if pass_on is not None:
                pass_on.start()
            place(slot, block)
            for u in range(1, 6):
                emit(6 * _index_of(*block) + u)

        near_x, near_y, far = [(*chip, c) for chip in chips]
        to_sibling = wcopy(0, me, sibling, src=win_ref.at[0])
        pass_x = wcopy(4, near_x, sibling, src=win_ref.at[1])
        pass_y = wcopy(5, near_y, sibling, src=win_ref.at[0])
        pass_d = wcopy(6, far, sibling, src=win_ref.at[0])
        relays = [relay(3, near_x, 0, near_y), relay(7, near_y, 1, near_x)]
        take(me, 0, pass_on=to_sibling)
        take(near_x, 1, [wcopy(1, near_x, me)], pass_x)
        relays[0].start()
        to_sibling.wait_send()
        take(near_y, 0, [wcopy(2, near_y, me)], pass_y)
        relays[1].start()
        small_passed = []
        for j, chip in enumerate(chips):
            scopy(1 + j, (*chip, c), me).wait_recv()
            cp = scopy(4 + j, (*chip, c), sibling)
            cp.start()
            small_passed.append(cp)
        pass_x.wait_send()
        take(sibling, 1, [wcopy(0, sibling, me)])
        pass_y.wait_send()
        take((*chips[0], 1 - c), 0, [wcopy(4, (*chips[0], 1 - c), me)])
        take((*chips[1], 1 - c), 1, [wcopy(5, (*chips[1], 1 - c), me)])
        take(far, 0, [relay(3, far, 0, near_y), relay(7, far, 1, near_x)], pass_d)
        take((*chips[2], 1 - c), 1, [wcopy(6, (*chips[2], 1 - c), me)])
        for t in range(0, n_tiles, 6):
            emit(t)
        scopy(0, sibling, me).wait_recv()
        for j, chip in enumerate(chips):
            scopy(4 + j, (*chip, 1 - c), me).wait_recv()
        for cp in first + small_passed + relays + [pass_d]:
            cp.wait_send()
        small_own.wait()
        pltpu.make_async_copy(wn_ref, wn_ref, local_sems.at[2]).wait()

    wn, _, gs = pl.pallas_call(
        body,
        in_specs=[HBM_SPEC] * 2, out_specs=[HBM_SPEC] * 3,
        out_shape=[_sds((n_tiles, D, 128), BF16), _sds((N_DEV,) + shard.shape, BF16), _sds((N_DEV,) + small.shape, small.dtype)],
        scratch_shapes=[pltpu.VMEM((n_tiles, D, 128), BF16), pltpu.VMEM((2, D, cw), BF16), pltpu.VMEM((tr, 7 * 128), F32),
                        pltpu.SemaphoreType.DMA((15,)), pltpu.SemaphoreType.DMA((15,)), pltpu.SemaphoreType.DMA((3,))],
        compiler_params=pltpu.CompilerParams(vmem_limit_bytes=48 * 1024 * 1024),
        name=name,
    )(shard, small)
    return wn, gs


def _mm(a, b, mode, out_dtype, *, tm, tn, tk, name, b_blocked=False, b_tiled=False, out_blocked=False, m_tiles=None, pair=None,
        phases=()):
    if mode == "nn":
        (m, k), dims = a.shape, NN
        a_blk, a_map = (tm, tk), (lambda i, j, kk: (i, kk))
        if b_blocked:
            assert b.shape[1] == k and b.shape[2] == tn and tk == k
            n = b.shape[0] * tn
            b_spec = pl.BlockSpec((None, tk, tn), lambda i, j, kk: (j, kk, 0))
        elif b_tiled:
            assert b.shape[1] == k and b.shape[2] == 128 and tn % 128 == 0
            n = b.shape[0] * 128
            b_spec = pl.BlockSpec((tn // 128, tk, 128), lambda i, j, kk: (j, kk, 0))
        else:
            assert b.shape[0] == k
            n = b.shape[1]
            b_spec = pl.BlockSpec((tk, tn), lambda i, j, kk: (kk, j))
    elif mode == "tn":
        (k, m), n, dims = a.shape, b.shape[1], TN
        assert b.shape[0] == k
        first = 0 if m_tiles is None else m_tiles[0]
        a_blk, a_map = (tk, tm), (lambda i, j, kk: (kk, i + first))
        b_spec = pl.BlockSpec((tk, tn), lambda i, j, kk: (kk, j))
    else:
        (m, k), dims = a.shape, NT
        a_blk, a_map = (tm, tk), (lambda i, j, kk: (i, kk))
        if b_blocked:
            assert b.shape[0] * b.shape[2] == k and b.shape[2] == tk
            n = b.shape[1]
            b_spec = pl.BlockSpec((None, tn, tk), lambda i, j, kk: (kk, j, 0))
        elif b_tiled:
            assert b.shape[0] * 128 == k and b.shape[2] == 128 and tk % 128 == 0
            n = b.shape[1]
            b_spec = pl.BlockSpec((tk // 128, tn, 128), lambda i, j, kk: (kk, j, 0))
        else:
            assert b.shape[1] == k
            n = b.shape[0]
            b_spec = pl.BlockSpec((tn, tk), lambda i, j, kk: (j, kk))
    assert m % tm == 0 and n % tn == 0 and k % tk == 0, (a.shape, b.shape, mode)
    nk = k // tk
    n_row_tiles = m // tm if m_tiles is None else m_tiles[1]
    if out_blocked:
        out_shape, out_spec = _sds((n // tn, n_row_tiles * tm, tn), out_dtype), pl.BlockSpec((None, tm, tn), lambda i, j, kk: (j, i, 0))
    else:
        out_shape, out_spec = _sds((n_row_tiles * tm, n), out_dtype), pl.BlockSpec((tm, tn), lambda i, j, kk: (i, j))

    grid = (n_row_tiles, n // tn, nk)

    def body(a_ref, b_ref, o_ref, *rest):
        rhs = jnp.concatenate([b_ref[u] for u in range(b_ref.shape[0])], axis=1) if b_tiled else b_ref[...]
        p = _dot(a_ref[...], rhs, dims)
        if nk == 1:
            o_ref[...] = p.astype(out_dtype)
            if pair is not None:
                _send_to_sibling(p.astype(out_dtype), *rest)
        else:
            acc_ref, = rest
            kk = pl.program_id(2)

            @pl.when(kk == 0)
            def _():
                acc_ref[...] = p

            @pl.when(kk > 0)
            def _():
                acc_ref[...] += p

            @pl.when(kk == nk - 1)
            def _():
                o_ref[...] = acc_ref[...].astype(out_dtype)

    def _send_to_sibling(tile, pair_ref, stage_ref, send_sems, recv_sem):
        i, j = pl.program_id(0), pl.program_id(1)
        x, y, c = _place()
        blk = pair["block"](i, j)
        k = ((blk >> 2) ^ x) + 2 * (((blk >> 1) & 1) ^ y)
        ordinal = pair["ordinal"](i, j)

        def send(slot):
            return _rcopy(stage_ref.at[slot], pair["dst"](pair_ref, k, i, j), send_sems.at[slot], recv_sem.at[0], (x, y, 1 - c))

        @pl.when((blk & 1) != c)
        def _():
            slot = ordinal & 1

            @pl.when(ordinal >= 2)
            def _():
                send(slot).wait_send()

            stage_ref[slot] = tile
            send(slot).start()

        @pl.when((i == grid[0] - 1) & (j == grid[1] - 1))
        def _():
            send(0).wait_send()
            send(1).wait_send()
            _rcopy(pair_ref, pair_ref, send_sems.at[0], recv_sem.at[0], (x, y, 1 - c)).wait_recv()

    blocks = _nbytes(a_blk, a.dtype) + tk * tn * jnp.dtype(b.dtype).itemsize + _nbytes((tm, tn), out_dtype)
    out_specs, out_shapes, scratch = [out_spec], [out_shape], [] if nk == 1 else [pltpu.VMEM((tm, tn), F32)]
    scratch_bytes = _nbytes((tm, tn), F32) * (nk > 1)
    if pair is not None:
        assert nk == 1
        out_specs, out_shapes = out_specs + [HBM_SPEC], out_shapes + [pair["like"]]
        scratch = [pltpu.VMEM((2, tm, tn), out_dtype), pltpu.SemaphoreType.DMA((2,)), pltpu.SemaphoreType.DMA((1,))]
        scratch_bytes = 2 * _nbytes((tm, tn), out_dtype)
    outs, bufs = _hosted(
        body, name=name, grid=grid,
        in_specs=[pl.BlockSpec(a_blk, a_map), b_spec], out_specs=out_specs, out_shape=out_shapes, args=[a, b],
        scratch_shapes=scratch, block_bytes=blocks, scratch_bytes=scratch_bytes, phases=phases)
    return outs[0], outs[1:] + bufs


RB = 256


def _row_spec(width):
    return pl.BlockSpec((RB, width), lambda i: (i, 0))


def _vec_spec(width):
    return pl.BlockSpec((1, width), lambda i: (0, 0))


def _rinv(x):
    return lax.rsqrt(jnp.mean(x * x, axis=-1, keepdims=True) + EPS)


def _norm_bwd(dyn, xhat, r):
    return r * (dyn - xhat * jnp.mean(dyn * xhat, axis=-1, keepdims=True))


def _colsum(x):
    return jnp.sum(x, axis=0, keepdims=True)


def _accumulate(ref, value):
    @pl.when(pl.program_id(0) == 0)
    def _():
        ref[...] = value

    @pl.when(pl.program_id(0) > 0)
    def _():
        ref[...] += value


def _prenorm(x, gain):
    def body(x_ref, g_ref, h_ref):
        xv = x_ref[...]
        h_ref[...] = (xv * _rinv(xv) * g_ref[...]).astype(BF16)

    outs, _ = _hosted(body, name="prenorm", grid=(S // RB,), in_specs=[_row_spec(D), _vec_spec(D)], out_specs=[_row_spec(D)],
                      out_shape=[_sds((S, D), BF16)], args=[x, gain], block_bytes=RB * D * 6)
    return outs[0]


def _mid_fwd(x, y, npost, npre, phases=()):
    def body(x_ref, y_ref, po_ref, pr_ref, x1_ref, h1_ref):
        yv = y_ref[...]
        x1 = x_ref[...] + yv * _rinv(yv) * po_ref[...]
        x1_ref[...] = x1
        h1_ref[...] = (x1 * _rinv(x1) * pr_ref[...]).astype(BF16)

    return _hosted(body, name="mid_fwd", grid=(S // RB,), in_specs=[_row_spec(D), _row_spec(D), _vec_spec(D), _vec_spec(D)],
                   out_specs=[_row_spec(D), _row_spec(D)], out_shape=[_sds((S, D), F32), _sds((S, D), BF16)],
                   args=[x, y, npost, npre], block_bytes=RB * D * 14, phases=phases)


def _final(x1, y1, tgt, npost):
    def body(x_ref, y_ref, t_ref, po_ref, loss_ref, dx_ref, dy_ref, dpo_ref):
        yv = y_ref[...]
        r = _rinv(yv)
        yhat = yv * r
        err = x_ref[...] + yhat * po_ref[...] - t_ref[...]
        dx = err * (1.0 / D)
        dx_ref[...] = dx
        dy_ref[...] = _norm_bwd(dx * po_ref[...], yhat, r).astype(BF16)
        _accumulate(loss_ref, _colsum(err * err))
        _accumulate(dpo_ref, _colsum(dx * yhat))

    outs, _ = _hosted(body, name="final", grid=(S // RB,), in_specs=[_row_spec(D), _row_spec(D), _row_spec(D), _vec_spec(D)],
                      out_specs=[_vec_spec(D), _row_spec(D), _row_spec(D), _vec_spec(D)],
                      out_shape=[_sds((1, D), F32), _sds((S, D), F32), _sds((S, D), BF16), _sds((1, D), F32)],
                      args=[x1, y1, tgt, npost], block_bytes=RB * D * 18)
    return outs


def _mid_bwd(dx2, dh1, x1, y0, npre, npost, phases=()):
    def body(dx2_ref, dh_ref, x_ref, y_ref, pr_ref, po_ref, dx1_ref, dy_ref, dpr_ref, dpo_ref):
        xv = x_ref[...]
        r = _rinv(xv)
        xhat = xv * r
        dh = dh_ref[...]
        dx1 = dx2_ref[...] + _norm_bwd(dh * pr_ref[...], xhat, r)
        dx1_ref[...] = dx1
        yv = y_ref[...]
        ry = _rinv(yv)
        yhat = yv * ry
        dy_ref[...] = _norm_bwd(dx1 * po_ref[...], yhat, ry).astype(BF16)
        _accumulate(dpr_ref, _colsum(dh * xhat))
        _accumulate(dpo_ref, _colsum(dx1 * yhat))

    return _hosted(body, name="mid_bwd", grid=(S // RB,), in_specs=[_row_spec(D)] * 4 + [_vec_spec(D)] * 2,
                   out_specs=[_row_spec(D), _row_spec(D), _vec_spec(D), _vec_spec(D)],
                   out_shape=[_sds((S, D), F32), _sds((S, D), BF16), _sds((1, D), F32), _sds((1, D), F32)],
                   args=[dx2, dh1, x1, y0, npre, npost], block_bytes=RB * D * 22, phases=phases)


def _first_bwd(dx1, dh0, x0, npre, phases=()):
    def body(dx1_ref, dh_ref, x_ref, pr_ref, gx_ref, dpr_ref):
        xv = x_ref[...]
        r = _rinv(xv)
        xhat = xv * r
        dh = dh_ref[...]
        gx_ref[...] = dx1_ref[...] + _norm_bwd(dh * pr_ref[...], xhat, r)
        _accumulate(dpr_ref, _colsum(dh * xhat))

    return _hosted(body, name="first_bwd", grid=(S // RB,), in_specs=[_row_spec(D)] * 3 + [_vec_spec(D)],
                   out_specs=[_row_spec(D), _vec_spec(D)], out_shape=[_sds((S, D), F32), _sds((1, D), F32)],
                   args=[dx1, dh0, x0, npre], block_bytes=RB * D * 16, phases=phases)


GLA_RB = 256
GLA_CPB = GLA_RB // C


def _sigmoid(x):
    return 1.0 / (1.0 + jnp.exp(-x))


def _tri(strict):
    r = lax.broadcasted_iota(jnp.int32, (C, C), 0)
    c = lax.broadcasted_iota(jnp.int32, (C, C), 1)
    return jnp.where(c < r if strict else c <= r, 1.0, 0.0).astype(BF16)


def _tri_dot(tri, x):
    hi = x.astype(BF16)
    lo = (x - hi.astype(F32)).astype(BF16)
    return _dot(tri, hi) + _dot(tri, lo)


def _gla_gates(glr_b, w2, b, tri):
    z = _dot(glr_b, w2) + b
    log_a = (jnp.minimum(z, 0.0) - jnp.log(1.0 + jnp.exp(-jnp.abs(z)))) * (1.0 / GLA_TAU)
    bcum = _tri_dot(tri, log_a)
    b_end = jnp.sum(log_a, axis=0, keepdims=True)
    return z, jnp.exp(b_end - bcum), jnp.exp(b_end)


def _gla_fwd(proj, w2p, bgate, ogain, phases=()):
    def body(p_ref, w2_ref, b_ref, og_ref, y_ref, st_out_ref, st_ref):
        @pl.when(pl.program_id(0) == 0)
        def _():
            st_ref[...] = jnp.zeros_like(st_ref)

        tri = _tri(False)

        def chunk(ci, carry):
            rows = pl.ds(pl.multiple_of(ci * C, C), C)
            glr_b = p_ref[rows, LR0:LR0 + LRP].astype(BF16)
            _, ea_all, dec_all = _gla_gates(glr_b, w2_ref[...], b_ref[...], tri)
            for h in range(H):
                ea, dec = ea_all[:, h * DK:(h + 1) * DK], dec_all[:, h * DK:(h + 1) * DK]
                k_dec = (p_ref[rows, K0 + h * DK:K0 + (h + 1) * DK] * ea).astype(BF16)
                v_b = p_ref[rows, V0 + h * DV:V0 + (h + 1) * DV].astype(BF16)
                st = st_ref[h] * dec + _dot(v_b, k_dec, TN)
                st_ref[h] = st
                st_b = st.astype(BF16)
                st_out_ref[ci, h] = st_b
                q_b = (p_ref[rows, Q0 + h * DK:Q0 + (h + 1) * DK] * (DK ** -0.5)).astype(BF16)
                o = _dot(q_b, st_b, NT)
                on = o * _rinv(o)
                g = p_ref[rows, G0 + h * DV:G0 + (h + 1) * DV]
                y_ref[rows, h * DV:(h + 1) * DV] = (on * og_ref[:, h * DV:(h + 1) * DV] * (g * _sigmoid(g))).astype(BF16)
            return carry

        lax.fori_loop(0, GLA_CPB, chunk, 0, unroll=True)

    blocks = GLA_RB * GLA_PAD * 4 + GLA_RB * D * 2 + GLA_CPB * H * DV * DK * 2
    return _hosted(
        body, name="gla_fwd", grid=(S // GLA_RB,),
        in_specs=[pl.BlockSpec((GLA_RB, GLA_PAD), lambda i: (i, 0)),
                  pl.BlockSpec((LRP, H * DK), lambda i: (0, 0)),
                  pl.BlockSpec((1, H * DK), lambda i: (0, 0)),
                  pl.BlockSpec((1, H * DV), lambda i: (0, 0))],
        out_specs=[pl.BlockSpec((GLA_RB, H * DV), lambda i: (i, 0)),
                   pl.BlockSpec((GLA_CPB, H, DV, DK), lambda i: (i, 0, 0, 0))],
        out_shape=[_sds((S, H * DV), BF16), _sds((NC, H, DV, DK), BF16)],
        args=[proj, w2p, bgate, ogain], scratch_shapes=[pltpu.VMEM((H, DV, DK), F32)],
        block_bytes=blocks, scratch_bytes=H * DV * DK * 4, phases=phases)


def _gla_bwd(proj, dypre, states, w2p, bgate, ogain, phases=()):
    nb = S // GLA_RB

    def body(p_ref, dy_ref, st_blk_ref, st_prev_ref, w2_ref, b_ref, og_ref,
             dp_ref, dog_ref, dbg_ref, dw2_ref, r_ref):
        step = pl.program_id(0)

        @pl.when(step == 0)
        def _():
            r_ref[...] = jnp.zeros_like(r_ref)
            dog_ref[...] = jnp.zeros_like(dog_ref)
            dbg_ref[...] = jnp.zeros_like(dbg_ref)
            dw2_ref[...] = jnp.zeros_like(dw2_ref)

        tri = _tri(False)
        tri_strict = _tri(True)
        has_prev = jnp.where(step < nb - 1, 1.0, 0.0).astype(F32)

        def chunk(ci, st_prev_of):
            rows = pl.ds(ci * C if isinstance(ci, int) else pl.multiple_of(ci * C, C), C)
            glr_b = p_ref[rows, LR0:LR0 + LRP].astype(BF16)
            z, ea_all, dec_all = _gla_gates(glr_b, w2_ref[...], b_ref[...], tri)
            d_a, d_end = [], []
            for h in range(H):
                kcol = slice(h * DK, (h + 1) * DK)
                vcol = slice(h * DV, (h + 1) * DV)
                ea, dec = ea_all[:, kcol], dec_all[:, kcol]
                k_dec = p_ref[rows, K0 + h * DK:K0 + (h + 1) * DK] * ea
                k_dec_b = k_dec.astype(BF16)
                v_b = p_ref[rows, V0 + h * DV:V0 + (h + 1) * DV].astype(BF16)
                q_b = (p_ref[rows, Q0 + h * DK:Q0 + (h + 1) * DK] * (DK ** -0.5)).astype(BF16)
                st_b = st_blk_ref[ci, h]
                o = _dot(q_b, st_b, NT)
                rinv = _rinv(o)
                on = o * rinv
                g = p_ref[rows, G0 + h * DV:G0 + (h + 1) * DV]
                sg = _sigmoid(g)
                og = og_ref[:, vcol]
                dyp = dy_ref[rows, vcol]
                dp_ref[rows, G0 + h * DV:G0 + (h + 1) * DV] = (dyp * (on * og) * (sg * (1.0 + g * (1.0 - sg)))).astype(BF16)
                dpn = dyp * (g * sg)
                dog_ref[:, vcol] += _colsum(dpn * on)
                do_b = _norm_bwd(dpn * og, on, rinv).astype(BF16)
                gt = _dot(do_b, q_b, TN) + r_ref[h]
                gt_b = gt.astype(BF16)
                dp_ref[rows, Q0 + h * DK:Q0 + (h + 1) * DK] = (_dot(do_b, st_b) * (DK ** -0.5)).astype(BF16)
                dkd = _dot(v_b, gt_b)
                dp_ref[rows, V0 + h * DV:V0 + (h + 1) * DV] = _dot(k_dec_b, gt_b, NT).astype(BF16)
                dp_ref[rows, K0 + h * DK:K0 + (h + 1) * DK] = (dkd * ea).astype(BF16)
                d_a.append(dkd * k_dec)
                d_end.append(_colsum(gt * st_prev_of(h)) * dec)
                r_ref[h] = gt * dec
            dla = _tri_dot(tri_strict, jnp.concatenate(d_a, axis=1)) + jnp.concatenate(d_end, axis=1)
            dz = dla * (1.0 / GLA_TAU) * (1.0 - _sigmoid(z))
            dz_b = dz.astype(BF16)
            dbg_ref[...] += _colsum(dz)
            dw2_ref[...] += _dot(glr_b, dz_b, TN)
            dp_ref[rows, LR0:LR0 + LRP] = _dot(dz_b, w2_ref[...], NT).astype(BF16)

        def later_chunk(t, carry):
            ci = GLA_CPB - 1 - t
            chunk(ci, lambda h: st_blk_ref[ci - 1, h].astype(F32))
            return carry

        lax.fori_loop(0, GLA_CPB - 1, later_chunk, 0, unroll=True)
        chunk(0, lambda h: st_prev_ref[0, h].astype(F32) * has_prev)

    blocks = (GLA_RB * GLA_PAD * 4 + GLA_RB * D * 4 + (GLA_CPB + 1) * H * DV * DK * 2 + GLA_RB * GLA_PAD * 2)
    rev = lambda i: nb - 1 - i
    return _hosted(
        body, name="gla_bwd", grid=(nb,),
        in_specs=[pl.BlockSpec((GLA_RB, GLA_PAD), lambda i: (rev(i), 0)),
                  pl.BlockSpec((GLA_RB, H * DV), lambda i: (rev(i), 0)),
                  pl.BlockSpec((GLA_CPB, H, DV, DK), lambda i: (rev(i), 0, 0, 0)),
                  pl.BlockSpec((1, H, DV, DK), lambda i: (jnp.maximum(rev(i) * GLA_CPB - 1, 0), 0, 0, 0)),
                  pl.BlockSpec((LRP, H * DK), lambda i: (0, 0)),
                  pl.BlockSpec((1, H * DK), lambda i: (0, 0)),
                  pl.BlockSpec((1, H * DV), lambda i: (0, 0))],
        out_specs=[pl.BlockSpec((GLA_RB, GLA_PAD), lambda i: (rev(i), 0)),
                   pl.BlockSpec((1, H * DV), lambda i: (0, 0)),
                   pl.BlockSpec((1, H * DK), lambda i: (0, 0)),
                   pl.BlockSpec((LRP, H * DK), lambda i: (0, 0))],
        out_shape=[_sds((S, GLA_PAD), BF16), _sds((1, H * DV), F32), _sds((1, H * DK), F32), _sds((LRP, H * DK), F32)],
        args=[proj, dypre, states, states, w2p, bgate, ogain], scratch_shapes=[pltpu.VMEM((H, DV, DK), F32)],
        block_bytes=blocks, scratch_bytes=H * DV * DK * 4, phases=phases)


SGU_RB = 256
GELU_C = 0.7978845608028654
GELU_A = 0.044715


def _gelu(x):
    return 0.5 * x * (1.0 + jnp.tanh(GELU_C * (x + GELU_A * x * x * x)))


def _gelu_grad(x):
    t = jnp.tanh(GELU_C * (x + GELU_A * x * x * x))
    return 0.5 * (1.0 + t) + 0.5 * x * (1.0 - t * t) * (GELU_C * (1.0 + 3.0 * GELU_A * x * x))


def _causal_mask(transposed=False):
    i = lax.broadcasted_iota(jnp.int32, (SGU_BLOCK, SGU_BLOCK), 1 if transposed else 0)
    j = lax.broadcasted_iota(jnp.int32, (SGU_BLOCK, SGU_BLOCK), 0 if transposed else 1)
    return (i >= C) | (j < C)


def _layer_norm(vf, gain, bias):
    mu = jnp.mean(vf, axis=-1, keepdims=True)
    cen = vf - mu
    rstd = lax.rsqrt(jnp.mean(cen * cen, axis=-1, keepdims=True) + EPS)
    xhat = cen * rstd
    return xhat, rstd, xhat * gain + bias


def _sgu_fwd(proj, lng, lnb, ws, bsb, phases=()):
    def body(p_ref, g_ref, b_ref, ws_ref, bs_ref, o_ref):
        mask = _causal_mask()
        for n in range(SGU_RB // SGU_BLOCK):
            rows = slice(n * SGU_BLOCK, (n + 1) * SGU_BLOCK)
            _, _, vn = _layer_norm(_gelu(p_ref[rows, D:2 * D]), g_ref[...], b_ref[...])
            vn_b = vn.astype(BF16)
            for gi in range(SGU_G):
                cols = slice(gi * SGU_GD, (gi + 1) * SGU_GD)
                w = jnp.where(mask, ws_ref[gi], 0.0).astype(BF16)
                vs = _dot(w, vn_b[:, cols]) + bs_ref[gi]
                gate = p_ref[rows, 2 * D + gi * SGU_GD:2 * D + (gi + 1) * SGU_GD]
                o_ref[rows, cols] = (_gelu(p_ref[rows, cols]) * vs * (gate * _sigmoid(gate))).astype(BF16)

    blocks = SGU_RB * SGU_COLS * 4 + SGU_RB * D * 2 + SGU_G * SGU_BLOCK * (SGU_BLOCK + SGU_GD) * 4
    return _hosted(
        body, name="sgu_fwd", grid=(S // SGU_RB,),
        in_specs=[pl.BlockSpec((SGU_RB, SGU_COLS), lambda i: (i, 0)),
                  pl.BlockSpec((1, D), lambda i: (0, 0)), pl.BlockSpec((1, D), lambda i: (0, 0)),
                  pl.BlockSpec((SGU_G, SGU_BLOCK, SGU_BLOCK), lambda i: (0, 0, 0)),
                  pl.BlockSpec((SGU_G, SGU_BLOCK, SGU_GD), lambda i: (0, 0, 0))],
        out_specs=[pl.BlockSpec((SGU_RB, D), lambda i: (i, 0))], out_shape=[_sds((S, D), BF16)],
        args=[proj, lng, lnb, ws, bsb], block_bytes=blocks, phases=phases)


def _sgu_bwd(proj, dpre, lng, lnb, ws, wst, bsb, phases=()):
    nsteps = S // SGU_RB

    def body(p_ref, d_ref, g_ref, b_ref, ws_ref, wst_ref, bs_ref,
             dp_ref, dg_ref, db_ref, dws_ref, dbs_ref, dvn_ref, dvs_acc_ref):
        step = pl.program_id(0)

        @pl.when(step == 0)
        def _():
            dg_ref[...] = jnp.zeros_like(dg_ref)
            db_ref[...] = jnp.zeros_like(db_ref)
            dws_ref[...] = jnp.zeros_like(dws_ref)
            dvs_acc_ref[...] = jnp.zeros_like(dvs_acc_ref)

        mask = _causal_mask()
        maskt = _causal_mask(transposed=True)
        for n in range(SGU_RB // SGU_BLOCK):
            rows = slice(n * SGU_BLOCK, (n + 1) * SGU_BLOCK)
            v = p_ref[rows, D:2 * D]
            xhat, rstd, vn = _layer_norm(_gelu(v), g_ref[...], b_ref[...])
            vn_b = vn.astype(BF16)
            for gi in range(SGU_G):
                cols = slice(gi * SGU_GD, (gi + 1) * SGU_GD)
                w = jnp.where(mask, ws_ref[gi], 0.0).astype(BF16)
                wt = jnp.where(maskt, wst_ref[gi], 0.0).astype(BF16)
                vs = _dot(w, vn_b[:, cols]) + bs_ref[gi]
                u = p_ref[rows, cols]
                gate = p_ref[rows, 2 * D + gi * SGU_GD:2 * D + (gi + 1) * SGU_GD]
                sg = _sigmoid(gate)
                gu = _gelu(u)
                dpre_g = d_ref[rows, cols]
                t = dpre_g * (gate * sg)
                dp_ref[rows, cols] = (t * vs * _gelu_grad(u)).astype(BF16)
                dp_ref[rows, 2 * D + gi * SGU_GD:2 * D + (gi + 1) * SGU_GD] = (
                    dpre_g * gu * vs * (sg * (1.0 + gate * (1.0 - sg)))).astype(BF16)
                dvs = t * gu
                dvs_b = dvs.astype(BF16)
                dvs_acc_ref[:, cols] += dvs
                dws_ref[gi] += _dot(dvs_b, vn_b[:, cols], NT)
                dvn_ref[:, cols] = _dot(wt, dvs_b)
            dvn = dvn_ref[...]
            dg_ref[...] += _colsum(dvn * xhat)
            db_ref[...] += _colsum(dvn)
            dxh = dvn * g_ref[...]
            dvf = rstd * (dxh - jnp.mean(dxh, axis=-1, keepdims=True) - xhat * jnp.mean(dxh * xhat, axis=-1, keepdims=True))
            dp_ref[rows, D:2 * D] = (dvf * _gelu_grad(v)).astype(BF16)

        @pl.when(step == nsteps - 1)
        def _():
            lane = lax.broadcasted_iota(jnp.int32, (SGU_BLOCK, SGU_BLOCK), 1)
            out = jnp.zeros((SGU_BLOCK, SGU_BLOCK), F32)
            for gi in range(SGU_G):
                out = out + jnp.where(lane == gi, jnp.sum(dvs_acc_ref[:, gi * SGU_GD:(gi + 1) * SGU_GD], axis=1, keepdims=True), 0.0)
                dws_ref[gi] = jnp.where(mask, dws_ref[gi], 0.0)
            dbs_ref[...] = out

    blocks = SGU_RB * SGU_COLS * 6 + SGU_RB * D * 4 + SGU_G * SGU_BLOCK * (3 * SGU_BLOCK + SGU_GD) * 4
    const3 = lambda i: (0, 0, 0)
    return _hosted(
        body, name="sgu_bwd", grid=(nsteps,),
        in_specs=[pl.BlockSpec((SGU_RB, SGU_COLS), lambda i: (i, 0)),
                  pl.BlockSpec((SGU_RB, D), lambda i: (i, 0)),
                  pl.BlockSpec((1, D), lambda i: (0, 0)), pl.BlockSpec((1, D), lambda i: (0, 0)),
                  pl.BlockSpec((SGU_G, SGU_BLOCK, SGU_BLOCK), const3),
                  pl.BlockSpec((SGU_G, SGU_BLOCK, SGU_BLOCK), const3),
                  pl.BlockSpec((SGU_G, SGU_BLOCK, SGU_GD), const3)],
        out_specs=[pl.BlockSpec((SGU_RB, SGU_COLS), lambda i: (i, 0)),
                   pl.BlockSpec((1, D), lambda i: (0, 0)), pl.BlockSpec((1, D), lambda i: (0, 0)),
                   pl.BlockSpec((SGU_G, SGU_BLOCK, SGU_BLOCK), const3),
                   pl.BlockSpec((SGU_BLOCK, SGU_BLOCK), lambda i: (0, 0))],
        out_shape=[_sds((S, SGU_COLS), BF16), _sds((1, D), F32), _sds((1, D), F32),
                   _sds((SGU_G, SGU_BLOCK, SGU_BLOCK), F32), _sds((SGU_BLOCK, SGU_BLOCK), F32)],
        args=[proj, dpre, lng, lnb, ws, wst, bsb],
        scratch_shapes=[pltpu.VMEM((SGU_BLOCK, D), F32), pltpu.VMEM((SGU_BLOCK, D), F32)],
        block_bytes=blocks, scratch_bytes=2 * SGU_BLOCK * D * 4, phases=phases)


def _pair_sum(own, a, r0, nr, name, table=None):
    c = own.shape[2]
    tr = 256
    assert r0 % tr == 0 and nr % tr == 0

    def body(own_ref, sib_ref, o_ref):
        o_ref[...] = (own_ref[...].astype(F32) + sib_ref[...].astype(F32)).astype(BF16)

    own_map = ((lambda j, i: (1 + j, r0 // tr + i, 0)) if table is None else
               (lambda j, i, t: (t[1 + j], r0 // tr + i, 0)))
    cpad = -(-c // 128) * 128
    outs, _ = _hosted(
        body, name=name, grid=(3, nr // tr),
        in_specs=[pl.BlockSpec((None, tr, c), own_map),
                  pl.BlockSpec((None, tr, c), lambda j, i, *t: (1 + j, r0 // tr + i, 0))],
        out_specs=[pl.BlockSpec((None, tr, c), lambda j, i, *t: (j, i, 0))], out_shape=[_sds((3, nr, c), BF16)],
        args=[own, a], block_bytes=3 * tr * cpad * 2, table=table)
    return outs[0]


def _adamw_math(w, g, m, v):
    m = ADAM_B1 * m + (1.0 - ADAM_B1) * g
    v = ADAM_B2 * v + (1.0 - ADAM_B2) * (g * g)
    m_hat = m / (1.0 - ADAM_B1 ** ADAM_STEP)
    v_hat = v / (1.0 - ADAM_B2 ** ADAM_STEP)
    delta = -ADAM_LR * (m_hat / (jnp.sqrt(v_hat) + ADAM_EPS) + ADAM_WD * w)
    return delta, m, v


def _sum_adamw(own, a, b, w, m, v, *, name, phases=(), table=None):
    r, c = w.shape
    tr = 256

    def body(own_ref, sib_ref, far_ref, w_ref, m_ref, v_ref, g_ref, d_ref, nm_ref, nv_ref):
        g = own_ref[...].astype(F32) + sib_ref[...].astype(F32)
        for j in range(3):
            g = g + far_ref[j].astype(F32)
        g_ref[...] = g
        d_ref[...], nm_ref[...], nv_ref[...] = _adamw_math(w_ref[...], g, m_ref[...], v_ref[...])

    spec = pl.BlockSpec((tr, c), lambda i, *t: (i, 0))
    own_map = (lambda i: (0, i, 0)) if table is None else (lambda i, t: (t[0], i, 0))
    cpad = -(-c // 128) * 128
    return _hosted(
        body, name=name, grid=(r // tr,),
        in_specs=[pl.BlockSpec((None, tr, c), own_map), pl.BlockSpec((None, tr, c), lambda i, *t: (0, i, 0)),
                  pl.BlockSpec((3, tr, c), lambda i, *t: (0, i, 0)), spec, spec, spec],
        out_specs=[spec] * 4, out_shape=[_sds((r, c), F32)] * 4, args=[own, a, b, w, m, v],
        block_bytes=5 * tr * cpad * 2 + 7 * tr * cpad * 4, phases=phases, table=table)


def _sum_parts(parts, name):
    n, r, c = parts.shape

    def body(p_ref, o_ref):
        g = p_ref[0]
        for j in range(1, n):
            g = g + p_ref[j]
        o_ref[...] = g

    outs, _ = _hosted(body, name=name, grid=(1,), in_specs=[pl.BlockSpec((n, r, c), lambda i: (0, 0, 0))],
                      out_specs=[pl.BlockSpec((r, c), lambda i: (0, 0))], out_shape=[_sds((r, c), F32)], args=[parts],
                      block_bytes=(n + 1) * r * c * 4)
    return outs[0]


def _adamw(w, g, m, v, name):
    def body(w_ref, g_ref, m_ref, v_ref, d_ref, nm_ref, nv_ref):
        d_ref[...], nm_ref[...], nv_ref[...] = _adamw_math(w_ref[...], g_ref[...], m_ref[...], v_ref[...])

    spec = pl.BlockSpec(w.shape, lambda i: (0, 0))
    outs, _ = _hosted(body, name=name, grid=(1,), in_specs=[spec] * 4, out_specs=[spec] * 3, out_shape=[_sds(w.shape, F32)] * 3,
                      args=[w, g, m, v], block_bytes=7 * _nbytes(w.shape, F32))
    return outs


def _blocks_to_columns(g):
    n, r, c = g.shape
    return jnp.transpose(g, (1, 0, 2)).reshape(r, n * c)


def _pack(parts):
    return jnp.concatenate([p.reshape(-1) for p in parts]).reshape(-1, 128)


def _unpack(packed, like):
    flat, outs, off = packed.reshape(-1), [], 0
    for p in like:
        outs.append(flat[off:off + p.size].reshape(p.shape))
        off += p.size
    return outs


def kernel(x, norm_pre, norm_post, gla_w_in, gla_w_gate2, gla_b_gate, gla_o_gain, gla_w_out, sgu_w_in, sgu_ln_gain, sgu_ln_bias, sgu_w_spatial, sgu_b_spatial, sgu_w_out, loss_target, m_norm_pre, m_norm_post, m_gla_w_in, m_gla_w_gate2, m_gla_b_gate, m_gla_o_gain, m_gla_w_out, m_sgu_w_in, m_sgu_ln_gain, m_sgu_ln_bias, m_sgu_w_spatial, m_sgu_b_spatial, m_sgu_w_out, v_norm_pre, v_norm_post, v_gla_w_in, v_gla_w_gate2, v_gla_b_gate, v_gla_o_gain, v_gla_w_out, v_sgu_w_in, v_sgu_ln_gain, v_sgu_ln_bias, v_sgu_w_spatial, v_sgu_b_spatial, v_sgu_w_out):
    me = _index_of(*_place())
    x0 = x.reshape(S, D)
    tgt = loss_target.reshape(S, D)
    npre0, npre1 = norm_pre[0:1], norm_pre[1:2]
    npost0, npost1 = norm_post[0:1], norm_post[1:2]
    ws = sgu_w_spatial[0]
    wst = jnp.transpose(ws, (0, 2, 1))
    bsb = jnp.broadcast_to(sgu_b_spatial[0][:, :, None], (SGU_G, SGU_BLOCK, SGU_GD))
    W_ROWS = D // N_DEV
    IN_COLS_G, IN_COLS_S = GLA_COLS // N_DEV, SGU_COLS // N_DEV

    s_gwi, s_gwo = gla_w_in[0].astype(BF16), gla_w_out[0].astype(BF16)
    s_swi, s_swo = sgu_w_in[0].astype(BF16), sgu_w_out[0].astype(BF16)
    small = jnp.concatenate([jnp.pad(gla_w_gate2[0].reshape(4, 512), ((0, 4), (0, 0))),
                             jnp.pad(jnp.concatenate([sgu_ln_gain, sgu_ln_bias], axis=1), ((0, 7), (0, 0)))], axis=0)

    wg_in, g_small = _gather_first(s_gwi, small, "gather_first")
    w2 =_blocks_to_columns(g_small[:, :4, :].reshape(N_DEV, LR, 128))
    w2p = jnp.pad(w2, ((0, LRP - LR), (0, 0))).astype(BF16)
    lng = g_small[:, 8, :256].reshape(1, D)
    lnb = g_small[:, 8, 256:].reshape(1, D)
    like_gwo, like_swi = _sds((N_DEV, W_ROWS, D), BF16), _sds((N_DEV, D, IN_COLS_S), BF16)

    h0 = _prenorm(x0, npre0)
    proj0, (g_gwo, g_swi) = _mm(h0, wg_in, "nn", F32, tm=1024, tn=896, tk=D, name="gla_in", b_tiled=True, phases=[
        _Phase(like_gwo, None, [_gather_send(s_gwo, 0, W_ROWS)]),
        _Phase(like_swi, None, [_gather_send(s_swi, 0, 768, diagonal=False)])])
    (ypre0, states), (g_gwo, g_swi) = _gla_fwd(proj0, w2p, gla_b_gate, gla_o_gain, phases=[
        _Phase(like_gwo, g_gwo, [_gather_pass(0, W_ROWS)]),
        _Phase(like_swi, g_swi, [_gather_relay(0, 768), _gather_send(s_swi, 768, 512, diagonal=False)])])
    wg_out = g_gwo.reshape(D, D)
    y0, (g_swi,) = _mm(ypre0, wg_out, "nn", F32, tm=1024, tn=1024, tk=D, name="gla_out", phases=[
        _Phase(like_swi, g_swi, [_gather_pass(0, 768), _gather_relay(768, 512), _gather_send(s_swi, 1280, 512, diagonal=False)])])
    (x1, h1), (g_swi,) = _mid_fwd(x0, y0, npost0, npre1, phases=[
        _Phase(like_swi, g_swi, [_gather_pass(768, 512), _gather_relay(1280, 512), _gather_send(s_swi, 1792, 256, diagonal=False)])])
    g_swi, = _carry([_Phase(like_swi, g_swi, [_gather_pass(1280, 512), _gather_relay(1792, 256)])], "relay_sgu_w_in")
    g_swi, = _carry([_Phase(like_swi, g_swi, [_gather_pass(1792, 256)])], "pass_sgu_w_in")
    proj1, (g_swo,) = _mm(h1, g_swi, "nn", F32, tm=1024, tn=IN_COLS_S, tk=D, name="sgu_in", b_blocked=True, phases=[
        _Phase(like_gwo, None, [_gather_send(s_swo, 0, W_ROWS)])])
    (pre1,), (g_swo,) = _sgu_fwd(proj1, lng, lnb, ws, bsb, phases=[_Phase(like_gwo, g_swo, [_gather_pass(0, W_ROWS)])])
    ws_out = g_swo.reshape(D, D)
    y1, _ = _mm(pre1, ws_out, "nn", F32, tm=1024, tn=1024, tk=D, name="sgu_out")
    loss_cols, dx2, dy1, dnpost1 = _final(x1, y1, tgt, npost1)
    loss = lax.psum(0.5 * jnp.sum(loss_cols) / D, ("x", "y", "c"))

    like_b_out, like_b_swi = _sds((3, W_ROWS, D), BF16), _sds((3, D, IN_COLS_S), BF16)
    like_b_gwi = _sds((3, D, IN_COLS_G), BF16)
    row_pair = dict(like=_sds((4, W_ROWS, D), BF16), block=lambda i, j: i, ordinal=lambda i, j: i >> 1,
                    dst=lambda ref, k, i, j: ref.at[k])
    col_pair = dict(like=_sds((4, D, IN_COLS_S), BF16), block=lambda i, j: j, ordinal=lambda i, j: 4 * i + (j >> 1),
                    dst=lambda ref, k, i, j: ref.at[k, pl.ds(pl.multiple_of(i * 1024, 1024), 1024)])

    mine = _own_table()
    dws_out, (a_swo,) = _mm(pre1, dy1, "tn", BF16, tm=W_ROWS, tn=D, tk=S, name="sgu_out_dw", pair=row_pair)
    p_swo = dws_out.reshape(N_DEV, W_ROWS, D)
    t_swo = _pair_sum(p_swo, a_swo, 0, W_ROWS, "pair_sum_sgu_w_out", table=mine)
    dpre1, _ = _mm(dy1, ws_out, "nt", F32, tm=1024, tn=1024, tk=D, name="sgu_out_dx")
    (dproj1, dlng, dlnb, dwsp, dbsp), (b_swo,) = _sgu_bwd(proj1, dpre1, lng, lnb, ws, wst, bsb, phases=[
        _Phase(like_b_out, None, [_reduce_cross(t_swo, 0, 0, W_ROWS)])])
    p_swi, (a_swi,) = _mm(h1, dproj1, "tn", BF16, tm=1024, tn=IN_COLS_S, tk=S, name="sgu_in_dw", out_blocked=True, pair=col_pair)
    t_swi = _pair_sum(p_swi, a_swi, 0, D, "pair_sum_sgu_w_in", table=mine)
    dh1, (b_swi,) = _mm(dproj1, g_swi, "nt", F32, tm=1024, tn=1024, tk=IN_COLS_S, name="sgu_in_dx", b_blocked=True, phases=[
        _Phase(like_b_swi, None, [_reduce_cross(t_swi, 0, 0, 1024)])])
    (dx1, dy0, dnpre1, dnpost0), _ = _mid_bwd(dx2, dh1, x1, y0, npre1, npost0)
    dwg_out, (a_gwo,) = _mm(ypre0, dy0, "tn", BF16, tm=W_ROWS, tn=D, tk=S, name="gla_out_dw", pair=row_pair)
    p_gwo = dwg_out.reshape(N_DEV, W_ROWS, D)
    t_gwo = _pair_sum(p_gwo, a_gwo, 0, W_ROWS, "pair_sum_gla_w_out", table=mine)
    dypre0, _ = _mm(dy0, wg_out, "nt", F32, tm=1024, tn=1024, tk=D, name="gla_out_dx")
    late = [dnpre1, dnpost1, dlng, dlnb, dwsp, jnp.transpose(dbsp[:, :SGU_G])]
    late_pack = _pack(late)
    (dproj0, dogain, dbgate, dw2), (b_swi, b_gwo, g_late) = _gla_bwd(proj0, dypre0, states, w2p, gla_b_gate, gla_o_gain, phases=[
        _Phase(like_b_swi, b_swi, [_reduce_cross(t_swi, 1024, 1024, 1024)]),
        _Phase(like_b_out, None, [_reduce_cross(t_gwo, 0, 0, W_ROWS)]),
        _Phase(_sds((N_DEV,) + late_pack.shape, F32), None, [_gather_send(late_pack, 0, late_pack.shape[0])])])
    half = D // 2
    dwg_in_a, (g_late,) = _mm(h0, dproj0, "tn", BF16, tm=half, tn=896, tk=S, name="gla_in_dw_a", m_tiles=(0, 1), phases=[
        _Phase(_sds((N_DEV,) + late_pack.shape, F32), g_late, [_gather_pass(0, late_pack.shape[0])])])
    own_gwi, a_gwi = _blockify_pair(dwg_in_a, None, None, 0, "blockify_gla_w_in_a")
    t_gwi_a = _pair_sum(own_gwi, a_gwi, 0, half, "pair_sum_gla_w_in_a")
    dwg_in_b, (b_gwi,) = _mm(h0, dproj0, "tn", BF16, tm=half, tn=896, tk=S, name="gla_in_dw_b", m_tiles=(1, 1), phases=[
        _Phase(like_b_gwi, None, [_reduce_cross(t_gwi_a, 0, 0, 512)])])
    own_gwi, a_gwi = _blockify_pair(dwg_in_b, own_gwi, a_gwi, half, "blockify_gla_w_in_b")
    t_gwi_b = _pair_sum(own_gwi, a_gwi, half, half, "pair_sum_gla_w_in_b")
    dh0, (b_gwi,) = _mm(dproj0, wg_in, "nt", F32, tm=1024, tn=1024, tk=896, name="gla_in_dx", b_tiled=True, phases=[
        _Phase(like_b_gwi, b_gwi, [_reduce_cross(t_gwi_a, 512, 512, 512), _reduce_cross(t_gwi_b, 0, half, half)])])
    (grad_x, dnpre0), _ = _first_bwd(dx1, dh0, x0, npre0)

    early = [dnpre0, dnpost0, dbgate, dogain, dw2[:LR]]
    early_pack = _pack(early)
    like_early = _sds((N_DEV,) + early_pack.shape, F32)
    (g_swo, d_swo, nm_swo, nv_swo), (g_early,) = _sum_adamw(
        p_swo, a_swo, b_swo, sgu_w_out[0], m_sgu_w_out[0], v_sgu_w_out[0], name="adamw_sgu_w_out", table=mine, phases=[
            _Phase(like_early, None, [_gather_send(early_pack, 0, early_pack.shape[0])])])
    (g_gwo_, d_gwo, nm_gwo, nv_gwo), (g_early,) = _sum_adamw(
        p_gwo, a_gwo, b_gwo, gla_w_out[0], m_gla_w_out[0], v_gla_w_out[0], name="adamw_gla_w_out", table=mine, phases=[
            _Phase(like_early, g_early, [_gather_pass(0, early_pack.shape[0])])])
    (g_swi_, d_swi, nm_swi, nv_swi), _ = _sum_adamw(
        p_swi, a_swi, b_swi, sgu_w_in[0], m_sgu_w_in[0], v_sgu_w_in[0], name="adamw_sgu_w_in", table=mine)
    (g_gwi_, d_gwi, nm_gwi, nv_gwi), _ = _sum_adamw(
        own_gwi, a_gwi, b_gwi, gla_w_in[0], m_gla_w_in[0], v_gla_w_in[0], name="adamw_gla_w_in")

    g_npre1, g_npost1, g_lng_full, g_lnb_full, g_wsp, g_bsp = _unpack(_sum_parts(g_late, "sum_late_small_grads"), late)
    g_npre0, g_npost0, g_bgate, g_ogain, g_w2_full = _unpack(_sum_parts(g_early, "sum_early_small_grads"), early)
    g_w2 = lax.dynamic_slice(g_w2_full, (0, me * 128), (LR, 128))
    g_lng = lax.dynamic_slice(g_lng_full, (0, me * 256), (1, 256))
    g_lnb = lax.dynamic_slice(g_lnb_full, (0, me * 256), (1, 256))
    small_g = [jnp.concatenate([g_npre0, g_npre1], 0), jnp.concatenate([g_npost0, g_npost1], 0), g_w2, g_bgate, g_ogain,
               g_lng, g_lnb, g_wsp, g_bsp]
    small_w = [norm_pre, norm_post, gla_w_gate2[0], gla_b_gate, gla_o_gain, sgu_ln_gain, sgu_ln_bias, sgu_w_spatial[0], sgu_b_spatial[0]]
    small_m = [m_norm_pre, m_norm_post, m_gla_w_gate2[0], m_gla_b_gate, m_gla_o_gain, m_sgu_ln_gain, m_sgu_ln_bias, m_sgu_w_spatial[0], m_sgu_b_spatial[0]]
    small_v = [v_norm_pre, v_norm_post, v_gla_w_gate2[0], v_gla_b_gate, v_gla_o_gain, v_sgu_ln_gain, v_sgu_ln_bias, v_sgu_w_spatial[0], v_sgu_b_spatial[0]]
    d_pack, nm_pack, nv_pack = _adamw(_pack(small_w), _pack(small_g), _pack(small_m), _pack(small_v), "adamw_small")

    out_like = [norm_pre, norm_post, gla_w_gate2, gla_b_gate, gla_o_gain, sgu_ln_gain, sgu_ln_bias, sgu_w_spatial, sgu_b_spatial]
    sg_ = [g.reshape(s.shape) for g, s in zip(small_g, out_like)]
    sd_, sm_, sv_ = (_unpack(pk, out_like) for pk in (d_pack, nm_pack, nv_pack))

    def assemble(small_list, w_in_g, w_out_g, w_in_s, w_out_s):
        npre_, npost_, w2_, bg_, og_, lg_, lb_, wsp_, bsp_ = small_list
        return [npre_, npost_, w_in_g[None], w2_, bg_, og_, w_out_g[None], w_in_s[None], lg_, lb_, wsp_, bsp_, w_out_s[None]]

    return (loss, grad_x.reshape(1, S, D),
            *assemble(sg_, g_gwi_, g_gwo_, g_swi_, g_swo),
            *assemble(sd_, d_gwi, d_gwo, d_swi, d_swo),
            *assemble(sm_, nm_gwi, nm_gwo, nm_swi, nm_swo),
            *assemble(sv_, nv_gwi, nv_gwo, nv_swi, nv_swo))
```

```python
import functools

import jax
import jax.numpy as jnp
from jax import lax
from jax.experimental import pallas as pl
from jax.experimental.pallas import tpu as pltpu

F32 = jnp.float32
BF16 = jnp.bfloat16

N_DEV = 8
S = 2048
D = 2048
H = 4
DK = 256
DV = 512
C = 64
NC = S // C
GLA_COLS = 6160
GLA_PAD = 6272
Q0, K0, V0, G0, LR0 = 0, 1024, 2048, 4096, 6144
LR = 16
LRP = 128
SGU_COLS = 6144
SGU_BLOCK = 128
SGU_G = 8
SGU_GD = 256
EPS = 1e-6
GLA_TAU = 16.0

ADAM_LR, ADAM_B1, ADAM_B2, ADAM_EPS, ADAM_WD, ADAM_STEP = 0.001, 0.9, 0.999, 1e-08, 0.01, 10

V7X_VMEM_BYTES = 64 * 1024 * 1024
VMEM_CEILING = V7X_VMEM_BYTES - 6 * 1024 * 1024
HBM_PIN_BYTES = 1024 * 1024
MESH = pl.DeviceIdType.MESH
HBM_SPEC = pl.BlockSpec(memory_space=pl.ANY)


def _sds(shape, dtype):
    return jax.ShapeDtypeStruct(tuple(shape), dtype)


def _nbytes(shape, dtype):
    n = 1
    for s in shape:
        n *= s
    return n * jnp.dtype(dtype).itemsize


def _dot(a, b, dims=(((1,), (0,)), ((), ())), precision=None):
    return lax.dot_general(a, b, dims, precision=precision, preferred_element_type=F32)


NN = (((1,), (0,)), ((), ()))
TN = (((0,), (0,)), ((), ()))
NT = (((1,), (1,)), ((), ()))


def _place():
    return lax.axis_index("x"), lax.axis_index("y"), lax.axis_index("c")


def _index_of(px, py, pc):
    return 4 * px + 2 * py + pc


def _chips(x, y):
    return [(1 - x, y), (x, 1 - y), (1 - x, 1 - y)]


def _rcopy(src, dst, send_sem, recv_sem, to):
    return pltpu.make_async_remote_copy(src_ref=src, dst_ref=dst, send_sem=send_sem, recv_sem=recv_sem,
                                        device_id=to, device_id_type=MESH)


class _Move:
    def __init__(self, ins, n_remote, make, stage=None):
        self.ins, self.n_remote, self.make, self.stage = list(ins), n_remote, make, stage

    def scratch(self):
        sems = [pltpu.SemaphoreType.DMA((self.n_remote,)), pltpu.SemaphoreType.DMA((self.n_remote,))]
        return sems if self.stage is None else sems + [pltpu.SemaphoreType.DMA((1,)), pltpu.VMEM(*self.stage)]

    def start(self, in_refs, buf, scratch):
        sends, _, local = self.make(in_refs, buf, scratch[0], scratch[1])
        if local is not None:
            pltpu.make_async_copy(local[0], scratch[3], scratch[2].at[0]).start()
        for cp in sends:
            cp.start()

    def finish(self, in_refs, buf, scratch):
        sends, arrivals, local = self.make(in_refs, buf, scratch[0], scratch[1])
        if local is not None:
            pltpu.make_async_copy(local[0], scratch[3], scratch[2].at[0]).wait()
            out = pltpu.make_async_copy(scratch[3], local[1], scratch[2].at[0])
            out.start()
        for cp in arrivals:
            cp.wait_recv()
        for cp in sends:
            cp.wait_send()
        if local is not None:
            out.wait()


class _Phase:
    def __init__(self, like, so_far, moves):
        self.like, self.so_far, self.moves = like, so_far, list(moves)


def _gather_send(shard, r0, nr, diagonal=True):
    def make(in_refs, g, ss, rs):
        sh, = in_refs
        x, y, c = _place()
        me = _index_of(x, y, c)
        rows = pl.ds(r0, nr)
        peers = [(x, y, 1 - c)] + [(px, py, c) for px, py in _chips(x, y)[:3 if diagonal else 2]]
        sends = [_rcopy(sh.at[rows], g.at[me, rows], ss.at[k], rs.at[k], p) for k, p in enumerate(peers)]
        arrivals = [_rcopy(sh.at[rows], g.at[_index_of(*p), rows], ss.at[k], rs.at[k], p) for k, p in enumerate(peers)]
        return sends, arrivals, (sh.at[rows], g.at[me, rows])

    return _Move([shard], 4 if diagonal else 3, make, stage=((nr, shard.shape[1]), shard.dtype))


def _gather_relay(r0, nr):
    def make(in_refs, g, ss, rs):
        x, y, c = _place()
        nx, ny, nd = [(px, py, c) for px, py in _chips(x, y)]
        first, second = pl.ds(r0, nr // 2), pl.ds(r0 + nr // 2, nr // 2)
        sends = [_rcopy(g.at[_index_of(*nx), first], g.at[_index_of(*nx), first], ss.at[0], rs.at[0], ny),
                 _rcopy(g.at[_index_of(*ny), second], g.at[_index_of(*ny), second], ss.at[1], rs.at[1], nx)]
        arrivals = [_rcopy(g.at[_index_of(*nx), first], g.at[_index_of(*nd), first], ss.at[0], rs.at[0], ny),
                    _rcopy(g.at[_index_of(*ny), second], g.at[_index_of(*nd), second], ss.at[1], rs.at[1], nx)]
        return sends, arrivals, None

    return _Move([], 2, make)


def _gather_pass(r0, nr):
    def make(in_refs, g, ss, rs):
        x, y, c = _place()
        rows = pl.ds(r0, nr)
        sends = [_rcopy(g.at[_index_of(px, py, c), rows], g.at[_index_of(px, py, c), rows], ss.at[j], rs.at[j], (x, y, 1 - c))
                 for j, (px, py) in enumerate(_chips(x, y))]
        arrivals = [_rcopy(g.at[_index_of(px, py, c), rows], g.at[_index_of(px, py, 1 - c), rows], ss.at[j], rs.at[j], (x, y, 1 - c))
                    for j, (px, py) in enumerate(_chips(x, y))]
        return sends, arrivals, None

    return _Move([], 3, make)


def _own_table():
    x, y, c = _place()
    return jnp.stack([_index_of(px, py, c) for px, py in [(x, y)] + _chips(x, y)]).astype(jnp.int32)


def _blockify_pair(dw, own_so_far, a_so_far, dst_r0, name):
    rows, tr, cw, win = dw.shape[0], 256, GLA_COLS // N_DEV, 896
    n_steps = rows // tr

    def body(*refs):
        x_ref, own_ref, a_ref, stage_ref, send_sems, recv_sem = refs[0], *refs[-5:]
        i = pl.program_id(0)
        x, y, c = _place()

        def send(slot, k):
            dst = a_ref.at[k, pl.ds(pl.multiple_of(dst_r0 + i * tr, tr), tr)]
            return _rcopy(stage_ref.at[slot], dst, send_sems.at[slot], recv_sem.at[0], (x, y, 1 - c))

        for j in range(N_DEV):
            window = x_ref[:, 768 * j:768 * j + win].astype(F32)
            tile = (pltpu.roll(window, win - 2 * j, 1) if j else window)[:, :cw].astype(BF16)
            k = ((j >> 2) ^ x) + 2 * (((j >> 1) & 1) ^ y)

            @pl.when((j & 1) == c)
            def _():
                own_ref[k] = tile

            @pl.when((j & 1) != c)
            def _():
                slot = (j >> 1) & 1
                if j >> 1 >= 2:
                    send(slot, k).wait_send()
                else:
                    pl.when(i > 0)(lambda: send(slot, k).wait_send())
                stage_ref[slot] = tile
                send(slot, k).start()

        @pl.when(i == n_steps - 1)
        def _():
            send(0, 0).wait_send()
            send(1, 0).wait_send()
            arrived = a_ref.at[:, pl.ds(dst_r0, rows)]
            _rcopy(arrived, arrived, send_sems.at[0], recv_sem.at[0], (x, y, 1 - c)).wait_recv()

    continues = a_so_far is not None
    own, a = pl.pallas_call(
        body, grid=(n_steps,),
        in_specs=[pl.BlockSpec((tr, GLA_PAD), lambda i: (i, 0))] + [HBM_SPEC] * (2 * continues),
        out_specs=[pl.BlockSpec((4, tr, cw), lambda i: (0, dst_r0 // tr + i, 0)), HBM_SPEC],
        out_shape=[_sds((4, D, cw), BF16), _sds((4, D, cw), BF16)],
        scratch_shapes=[pltpu.VMEM((2, tr, cw), BF16), pltpu.SemaphoreType.DMA((2,)), pltpu.SemaphoreType.DMA((1,))],
        input_output_aliases={1: 0, 2: 1} if continues else {},
        compiler_params=pltpu.CompilerParams(dimension_semantics=("arbitrary",), vmem_limit_bytes=48 * 1024 * 1024),
        name=name,
    )(*([dw] + [own_so_far, a_so_far] * continues))
    return own, a


def _reduce_cross(sums, src_r0, dst_r0, nr):
    def make(in_refs, b, ss, rs):
        t, = in_refs
        x, y, c = _place()
        src, dst = pl.ds(src_r0, nr), pl.ds(dst_r0, nr)
        sends = [_rcopy(t.at[j, src], b.at[j, dst], ss.at[j], rs.at[j], (px, py, c)) for j, (px, py) in enumerate(_chips(x, y))]
        return sends, sends, None

    return _Move([sums], 3, make)


def _hosted(body, *, name, grid, in_specs, out_specs, out_shape, args, scratch_shapes=(), block_bytes, scratch_bytes=0,
            phases=(), table=None):
    n_in, n_out, n_scr = len(args), len(out_shape), len(scratch_shapes)
    all_args, all_out_shape, sems, aliases, layout = list(args), list(out_shape), [], {}, []
    for j, ph in enumerate(phases):
        counts = []
        for mv in ph.moves:
            all_args += mv.ins
            counts.append(len(mv.ins))
            sems += mv.scratch()
        if ph.so_far is not None:
            aliases[len(all_args)] = n_out + j
            all_args.append(ph.so_far)
        layout.append((counts, ph.so_far is not None))
        all_out_shape.append(ph.like)
    n_extra_in = len(all_args) - n_in

    def wrapped(*refs):
        ins, pos = refs[:n_in], n_in
        move_ins = []
        for counts, continues in layout:
            per_move = []
            for cnt in counts:
                per_move.append(refs[pos:pos + cnt])
                pos += cnt
            pos += continues
            move_ins.append(per_move)
        outs = refs[pos:pos + n_out]
        bufs = refs[pos + n_out:pos + n_out + len(phases)]
        pos += n_out + len(phases)
        scratch = refs[pos:pos + n_scr]
        pos += n_scr
        move_sems = []
        for ph in phases:
            per_move = []
            for mv in ph.moves:
                count = len(mv.scratch())
                per_move.append(refs[pos:pos + count])
                pos += count
            move_sems.append(per_move)

        def each_move(fn_name):
            for ph, buf, per_in, per_sem in zip(phases, bufs, move_ins, move_sems):
                for mv, mv_in, mv_sem in zip(ph.moves, per_in, per_sem):
                    getattr(mv, fn_name)(mv_in, buf, mv_sem)

        if phases:
            first = functools.reduce(jnp.logical_and, [pl.program_id(a) == 0 for a in range(len(grid))])
            last = functools.reduce(jnp.logical_and, [pl.program_id(a) == grid[a] - 1 for a in range(len(grid))])
            pl.when(first)(lambda: each_move("start"))
        body(*ins, *outs, *scratch)
        if phases:
            pl.when(last)(lambda: each_move("finish"))

    all_args = [pltpu.with_memory_space_constraint(a, pltpu.HBM) if a.size * a.dtype.itemsize >= HBM_PIN_BYTES else a
                for a in all_args]
    est = 2 * block_bytes + scratch_bytes
    params = pltpu.CompilerParams(dimension_semantics=("arbitrary",) * len(grid),
                                  vmem_limit_bytes=min(VMEM_CEILING, max(32 * 1024 * 1024, 2 * est)))
    all_in_specs, all_out_specs = list(in_specs) + [HBM_SPEC] * n_extra_in, list(out_specs) + [HBM_SPEC] * len(phases)
    if table is None:
        results = pl.pallas_call(
            wrapped, grid=grid, in_specs=all_in_specs, out_specs=all_out_specs, out_shape=all_out_shape,
            scratch_shapes=list(scratch_shapes) + sems, input_output_aliases=aliases, compiler_params=params, name=name,
        )(*all_args)
    else:
        results = pl.pallas_call(
            lambda table_ref, *refs: wrapped(*refs),
            grid_spec=pltpu.PrefetchScalarGridSpec(num_scalar_prefetch=1, grid=grid, in_specs=all_in_specs, out_specs=all_out_specs,
                                                   scratch_shapes=list(scratch_shapes) + sems),
            out_shape=all_out_shape, input_output_aliases={k + 1: v for k, v in aliases.items()}, compiler_params=params, name=name,
        )(table, *all_args)
    return list(results[:n_out]), list(results[n_out:])


def _carry(phases, name):
    def body(o_ref):
        o_ref[...] = jnp.zeros_like(o_ref)

    _, bufs = _hosted(body, name=name, grid=(1,), in_specs=[], out_specs=[pl.BlockSpec((8, 128), lambda i: (0, 0))],
                      out_shape=[_sds((8, 128), F32)], args=[], block_bytes=8 * 128 * 4, phases=phases)
    return bufs


def _gather_first(shard, small, name):
    cw, tr, n_tiles = shard.shape[1], 256, GLA_PAD // 128

    def body(sh_ref, sm_ref, wn_ref, g_ref, gs_ref, wt_ref, win_ref, tmp_ref, send_sems, recv_sems, local_sems):
        x, y, c = _place()
        me, sibling = (x, y, c), (x, y, 1 - c)
        chips = _chips(x, y)

        def copy(base, out_ref, k, block, to, src=None):
            dst = out_ref.at[_index_of(*block)]
            return _rcopy(dst if src is None else src, dst, send_sems.at[base + k], recv_sems.at[base + k], to)

        wcopy = functools.partial(copy, 0, g_ref)
        scopy = functools.partial(copy, 8, gs_ref)

        def relay(k, block, half, to):
            rows = pl.ds(half * (D // 2), D // 2)
            ref = g_ref.at[_index_of(*block), rows]
            return _rcopy(ref, ref, send_sems.at[k], recv_sems.at[k], to)

        def load(src_ref, slot):
            cp = pltpu.make_async_copy(src_ref, win_ref.at[slot], local_sems.at[0])
            cp.start()
            cp.wait()

        def place(slot, block):
            b = _index_of(*block)

            def rows_chunk(r, carry):
                rows = pl.ds(pl.multiple_of(r * tr, tr), tr)
                tmp_ref[:, :cw] = win_ref[slot, rows, :].astype(F32)
                shifted = pltpu.roll(tmp_ref[...], 2 * b, 1)
                for u in range(7):
                    wt_ref[6 * b + u, rows, :] = (wt_ref[6 * b + u, rows, :].astype(F32) + shifted[:, 128 * u:128 * (u + 1)]).astype(BF16)
                return carry

            lax.fori_loop(0, D // tr, rows_chunk, 0)

        small_own = pltpu.make_async_copy(sm_ref, gs_ref.at[_index_of(*me)], local_sems.at[1])
        small_own.start()
        first = [wcopy(1 + j, me, (*chip, c), src=sh_ref) for j, chip in enumerate(chips[:2])]
        first += [scopy(0, me, sibling, src=sm_ref)] + [scopy(1 + j, me, (*chip, c), src=sm_ref) for j, chip in enumerate(chips)]
        for cp in first:
            cp.start()

        def clear(t, carry):
            wt_ref[t] = jnp.zeros((D, 128), BF16)
            return carry

        lax.fori_loop(0, n_tiles, clear, 0)
        tmp_ref[...] = jnp.zeros_like(tmp_ref)

        def emit(t):
            pltpu.make_async_copy(wt_ref.at[t], wn_ref.at[t], local_sems.at[2]).start()

        def take(block, slot, arrivals=None, pass_on=None):
            for cp in arrivals or ():
                cp.wait_recv()
            load(sh_ref if arrivals is None else g_ref.at[_index_of(*block)], slot)
            if pass_on is not None:
                pass_on.start()
            place(slot, block)
            for u in range(1, 6):
                emit(6 * _index_of(*block) + u)

        near_x, near_y, far = [(*chip, c) for chip in chips]
        to_sibling = wcopy(0, me, sibling, src=win_ref.at[0])
        pass_x = wcopy(4, near_x, sibling, src=win_ref.at[1])
        pass_y = wcopy(5, near_y, sibling, src=win_ref.at[0])
        pass_d = wcopy(6, far, sibling, src=win_ref.at[0])
        relays = [relay(3, near_x, 0, near_y), relay(7, near_y, 1, near_x)]
        take(me, 0, pass_on=to_sibling)
        take(near_x, 1, [wcopy(1, near_x, me)], pass_x)
        relays[0].start()
        to_sibling.wait_send()
        take(near_y, 0, [wcopy(2, near_y, me)], pass_y)
        relays[1].start()
        small_passed = []
        for j, chip in enumerate(chips):
            scopy(1 + j, (*chip, c), me).wait_recv()
            cp = scopy(4 + j, (*chip, c), sibling)
            cp.start()
            small_passed.append(cp)
        pass_x.wait_send()
        take(sibling, 1, [wcopy(0, sibling, me)])
        pass_y.wait_send()
        take((*chips[0], 1 - c), 0, [wcopy(4, (*chips[0], 1 - c), me)])
        take((*chips[1], 1 - c), 1, [wcopy(5, (*chips[1], 1 - c), me)])
        take(far, 0, [relay(3, far, 0, near_y), relay(7, far, 1, near_x)], pass_d)
        take((*chips[2], 1 - c), 1, [wcopy(6, (*chips[2], 1 - c), me)])
        for t in range(0, n_tiles, 6):
            emit(t)
        scopy(0, sibling, me).wait_recv()
        for j, chip in enumerate(chips):
            scopy(4 + j, (*chip, 1 - c), me).wait_recv()
        for cp in first + small_passed + relays + [pass_d]:
            cp.wait_send()
        small_own.wait()
        pltpu.make_async_copy(wn_ref, wn_ref, local_sems.at[2]).wait()

    wn, _, gs = pl.pallas_call(
        body,
        in_specs=[HBM_SPEC] * 2, out_specs=[HBM_SPEC] * 3,
        out_shape=[_sds((n_tiles, D, 128), BF16), _sds((N_DEV,) + shard.shape, BF16), _sds((N_DEV,) + small.shape, small.dtype)],
        scratch_shapes=[pltpu.VMEM((n_tiles, D, 128), BF16), pltpu.VMEM((2, D, cw), BF16), pltpu.VMEM((tr, 7 * 128), F32),
                        pltpu.SemaphoreType.DMA((15,)), pltpu.SemaphoreType.DMA((15,)), pltpu.SemaphoreType.DMA((3,))],
        compiler_params=pltpu.CompilerParams(vmem_limit_bytes=48 * 1024 * 1024),
        name=name,
    )(shard, small)
    return wn, gs


def _mm(a, b, mode, out_dtype, *, tm, tn, tk, name, b_blocked=False, b_tiled=False, out_blocked=False, m_tiles=None, pair=None,
        phases=()):
    if mode == "nn":
        (m, k), dims = a.shape, NN
        a_blk, a_map = (tm, tk), (lambda i, j, kk: (i, kk))
        if b_blocked:
            assert b.shape[1] == k and b.shape[2] == tn and tk == k
            n = b.shape[0] * tn
            b_spec = pl.BlockSpec((None, tk, tn), lambda i, j, kk: (j, kk, 0))
        elif b_tiled:
            assert b.shape[1] == k and b.shape[2] == 128 and tn % 128 == 0
            n = b.shape[0] * 128
            b_spec = pl.BlockSpec((tn // 128, tk, 128), lambda i, j, kk: (j, kk, 0))
        else:
            assert b.shape[0] == k
            n = b.shape[1]
            b_spec = pl.BlockSpec((tk, tn), lambda i, j, kk: (kk, j))
    elif mode == "tn":
        (k, m), n, dims = a.shape, b.shape[1], TN
        assert b.shape[0] == k
        first = 0 if m_tiles is None else m_tiles[0]
        a_blk, a_map = (tk, tm), (lambda i, j, kk: (kk, i + first))
        b_spec = pl.BlockSpec((tk, tn), lambda i, j, kk: (kk, j))
    else:
        (m, k), dims = a.shape, NT
        a_blk, a_map = (tm, tk), (lambda i, j, kk: (i, kk))
        if b_blocked:
            assert b.shape[0] * b.shape[2] == k and b.shape[2] == tk
            n = b.shape[1]
            b_spec = pl.BlockSpec((None, tn, tk), lambda i, j, kk: (kk, j, 0))
        elif b_tiled:
            assert b.shape[0] * 128 == k and b.shape[2] == 128 and tk % 128 == 0
            n = b.shape[1]
            b_spec = pl.BlockSpec((tk // 128, tn, 128), lambda i, j, kk: (kk, j, 0))
        else:
            assert b.shape[1] == k
            n = b.shape[0]
            b_spec = pl.BlockSpec((tn, tk), lambda i, j, kk: (j, kk))
    assert m % tm == 0 and n % tn == 0 and k % tk == 0, (a.shape, b.shape, mode)
    nk = k // tk
    n_row_tiles = m // tm if m_tiles is None else m_tiles[1]
    if out_blocked:
        out_shape, out_spec = _sds((n // tn, n_row_tiles * tm, tn), out_dtype), pl.BlockSpec((None, tm, tn), lambda i, j, kk: (j, i, 0))
    else:
        out_shape, out_spec = _sds((n_row_tiles * tm, n), out_dtype), pl.BlockSpec((tm, tn), lambda i, j, kk: (i, j))

    grid = (n_row_tiles, n // tn, nk)

    def body(a_ref, b_ref, o_ref, *rest):
        rhs = jnp.concatenate([b_ref[u] for u in range(b_ref.shape[0])], axis=1) if b_tiled else b_ref[...]
        p = _dot(a_ref[...], rhs, dims)
        if nk == 1:
            o_ref[...] = p.astype(out_dtype)
            if pair is not None:
                _send_to_sibling(p.astype(out_dtype), *rest)
        else:
            acc_ref, = rest
            kk = pl.program_id(2)

            @pl.when(kk == 0)
            def _():
                acc_ref[...] = p

            @pl.when(kk > 0)
            def _():
                acc_ref[...] += p

            @pl.when(kk == nk - 1)
            def _():
                o_ref[...] = acc_ref[...].astype(out_dtype)

    def _send_to_sibling(tile, pair_ref, stage_ref, send_sems, recv_sem):
        i, j = pl.program_id(0), pl.program_id(1)
        x, y, c = _place()
        blk = pair["block"](i, j)
        k = ((blk >> 2) ^ x) + 2 * (((blk >> 1) & 1) ^ y)
        ordinal = pair["ordinal"](i, j)

        def send(slot):
            return _rcopy(stage_ref.at[slot], pair["dst"](pair_ref, k, i, j), send_sems.at[slot], recv_sem.at[0], (x, y, 1 - c))

        @pl.when((blk & 1) != c)
        def _():
            slot = ordinal & 1

            @pl.when(ordinal >= 2)
            def _():
                send(slot).wait_send()

            stage_ref[slot] = tile
            send(slot).start()

        @pl.when((i == grid[0] - 1) & (j == grid[1] - 1))
        def _():
            send(0).wait_send()
            send(1).wait_send()
            _rcopy(pair_ref, pair_ref, send_sems.at[0], recv_sem.at[0], (x, y, 1 - c)).wait_recv()

    blocks = _nbytes(a_blk, a.dtype) + tk * tn * jnp.dtype(b.dtype).itemsize + _nbytes((tm, tn), out_dtype)
    out_specs, out_shapes, scratch = [out_spec], [out_shape], [] if nk == 1 else [pltpu.VMEM((tm, tn), F32)]
    scratch_bytes = _nbytes((tm, tn), F32) * (nk > 1)
    if pair is not None:
        assert nk == 1
        out_specs, out_shapes = out_specs + [HBM_SPEC], out_shapes + [pair["like"]]
        scratch = [pltpu.VMEM((2, tm, tn), out_dtype), pltpu.SemaphoreType.DMA((2,)), pltpu.SemaphoreType.DMA((1,))]
        scratch_bytes = 2 * _nbytes((tm, tn), out_dtype)
    outs, bufs = _hosted(
        body, name=name, grid=grid,
        in_specs=[pl.BlockSpec(a_blk, a_map), b_spec], out_specs=out_specs, out_shape=out_shapes, args=[a, b],
        scratch_shapes=scratch, block_bytes=blocks, scratch_bytes=scratch_bytes, phases=phases)
    return outs[0], outs[1:] + bufs


NT_ROWS_TM = 512
NT_ROWS_SUB = 128


def _nt_rows(a, b, *, tk, name, row_ins, vec_ins, outs, tail, b_blocked=False, b_tiled=False, phases=()):
    m, k = a.shape
    tm, nk = NT_ROWS_TM, k // tk
    if b_blocked:
        assert b.shape[0] * b.shape[2] == k and b.shape[2] == tk and b.shape[1] == D
        b_spec = pl.BlockSpec((None, D, tk), lambda i, kk: (kk, 0, 0))
    else:
        assert b_tiled and b.shape[0] * 128 == k and tk % 128 == 0 and b.shape[1] == D
        b_spec = pl.BlockSpec((tk // 128, D, 128), lambda i, kk: (kk, 0, 0))
    row_spec, vec_spec = pl.BlockSpec((tm, D), lambda i, kk: (i, 0)), pl.BlockSpec((1, D), lambda i, kk: (0, 0))
    n_row, n_vec, n_out = len(row_ins), len(vec_ins), len(outs)
    assert nk >= 2

    def body(a_ref, b_ref, *rest):
        row_hbm, vec_refs = rest[:n_row], rest[n_row:n_row + n_vec]
        out_refs = rest[n_row + n_vec:n_row + n_vec + n_out]
        acc_ref, row_sems = rest[n_row + n_vec + n_out], rest[-1]
        row_refs = rest[n_row + n_vec + n_out + 1:-1]
        rhs = jnp.concatenate([b_ref[u] for u in range(b_ref.shape[0])], axis=1) if b_tiled else b_ref[...]
        p = _dot(a_ref[...], rhs, NT)
        i, kk = pl.program_id(0), pl.program_id(1)

        def fetch(r):
            return pltpu.make_async_copy(row_hbm[r].at[pl.ds(pl.multiple_of(i * tm, tm), tm)], row_refs[r], row_sems.at[r])

        @pl.when(kk == 0)
        def _():
            for r in range(n_row):
                fetch(r).start()
            acc_ref[...] = p

        @pl.when(kk > 0)
        def _():
            acc_ref[...] += p

        @pl.when(kk == nk - 1)
        def _():
            for r in range(n_row):
                fetch(r).wait()
            for s in range(tm // NT_ROWS_SUB):
                rows = slice(s * NT_ROWS_SUB, (s + 1) * NT_ROWS_SUB)
                tail(acc_ref[rows, :], rows, (i == 0) if s == 0 else None, row_refs, vec_refs, out_refs)

    out_specs = [row_spec if kind == "row" else vec_spec for kind, _ in outs]
    out_shape = [_sds((m, D) if kind == "row" else (1, D), dt) for kind, dt in outs]
    blocks = tm * tk * 2 + D * tk * 2 + sum(tm * D * jnp.dtype(dt).itemsize for kind, dt in outs if kind == "row")
    scratch = [pltpu.VMEM((tm, D), F32)] + [pltpu.VMEM((tm, D), x.dtype) for x in row_ins] + [pltpu.SemaphoreType.DMA((n_row,))]
    return _hosted(body, name=name, grid=(m // tm, nk), in_specs=[pl.BlockSpec((tm, tk), lambda i, kk: (i, kk)), b_spec]
                   + [HBM_SPEC] * n_row + [vec_spec] * n_vec, out_specs=out_specs, out_shape=out_shape,
                   args=[a, b] + list(row_ins) + list(vec_ins), scratch_shapes=scratch,
                   block_bytes=blocks, scratch_bytes=(1 + n_row) * tm * D * 4, phases=phases)


def _vec_add(ref, value, first):
    if first is None:
        ref[...] += value
    else:
        pl.when(first)(lambda: ref.__setitem__(Ellipsis, value))
        pl.when(jnp.logical_not(first))(lambda: ref.__setitem__(Ellipsis, ref[...] + value))


RB = 256


def _row_spec(width):
    return pl.BlockSpec((RB, width), lambda i: (i, 0))


def _vec_spec(width):
    return pl.BlockSpec((1, width), lambda i: (0, 0))


def _rinv(x):
    return lax.rsqrt(jnp.mean(x * x, axis=-1, keepdims=True) + EPS)


def _norm_bwd(dyn, xhat, r):
    return r * (dyn - xhat * jnp.mean(dyn * xhat, axis=-1, keepdims=True))


def _colsum(x):
    return jnp.sum(x, axis=0, keepdims=True)


def _accumulate(ref, value):
    @pl.when(pl.program_id(0) == 0)
    def _():
        ref[...] = value

    @pl.when(pl.program_id(0) > 0)
    def _():
        ref[...] += value


def _prenorm(x, gain):
    def body(x_ref, g_ref, h_ref):
        xv = x_ref[...]
        h_ref[...] = (xv * _rinv(xv) * g_ref[...]).astype(BF16)

    outs, _ = _hosted(body, name="prenorm", grid=(S // RB,), in_specs=[_row_spec(D), _vec_spec(D)], out_specs=[_row_spec(D)],
                      out_shape=[_sds((S, D), BF16)], args=[x, gain], block_bytes=RB * D * 6)
    return outs[0]


def _mid_fwd(x, y, npost, npre, phases=()):
    def body(x_ref, y_ref, po_ref, pr_ref, x1_ref, h1_ref):
        yv = y_ref[...]
        x1 = x_ref[...] + yv * _rinv(yv) * po_ref[...]
        x1_ref[...] = x1
        h1_ref[...] = (x1 * _rinv(x1) * pr_ref[...]).astype(BF16)

    return _hosted(body, name="mid_fwd", grid=(S // RB,), in_specs=[_row_spec(D), _row_spec(D), _vec_spec(D), _vec_spec(D)],
                   out_specs=[_row_spec(D), _row_spec(D)], out_shape=[_sds((S, D), F32), _sds((S, D), BF16)],
                   args=[x, y, npost, npre], block_bytes=RB * D * 14, phases=phases)


def _final(x1, y1, tgt, npost):
    def body(x_ref, y_ref, t_ref, po_ref, loss_ref, dx_ref, dy_ref, dpo_ref):
        yv = y_ref[...]
        r = _rinv(yv)
        yhat = yv * r
        err = x_ref[...] + yhat * po_ref[...] - t_ref[...]
        dx = err * (1.0 / D)
        dx_ref[...] = dx
        dy_ref[...] = _norm_bwd(dx * po_ref[...], yhat, r).astype(BF16)
        _accumulate(loss_ref, _colsum(err * err))
        _accumulate(dpo_ref, _colsum(dx * yhat))

    outs, _ = _hosted(body, name="final", grid=(S // RB,), in_specs=[_row_spec(D), _row_spec(D), _row_spec(D), _vec_spec(D)],
                      out_specs=[_vec_spec(D), _row_spec(D), _row_spec(D), _vec_spec(D)],
                      out_shape=[_sds((1, D), F32), _sds((S, D), F32), _sds((S, D), BF16), _sds((1, D), F32)],
                      args=[x1, y1, tgt, npost], block_bytes=RB * D * 18)
    return outs


def _mid_bwd_tail(dh, rows, first, row_refs, vec_refs, out_refs):
    (dx2_ref, x_ref, y_ref), (pr_ref, po_ref), (dx1_ref, dy_ref, dpr_ref, dpo_ref) = row_refs, vec_refs, out_refs
    xv = x_ref[rows, :]
    r = _rinv(xv)
    xhat = xv * r
    dx1 = dx2_ref[rows, :] + _norm_bwd(dh * pr_ref[...], xhat, r)
    dx1_ref[rows, :] = dx1
    yv = y_ref[rows, :]
    ry = _rinv(yv)
    yhat = yv * ry
    dy_ref[rows, :] = _norm_bwd(dx1 * po_ref[...], yhat, ry).astype(BF16)
    _vec_add(dpr_ref, _colsum(dh * xhat), first)
    _vec_add(dpo_ref, _colsum(dx1 * yhat), first)


def _first_bwd_tail(dh, rows, first, row_refs, vec_refs, out_refs):
    (dx1_ref, x_ref), (pr_ref,), (gx_ref, dpr_ref) = row_refs, vec_refs, out_refs
    xv = x_ref[rows, :]
    r = _rinv(xv)
    xhat = xv * r
    gx_ref[rows, :] = dx1_ref[rows, :] + _norm_bwd(dh * pr_ref[...], xhat, r)
    _vec_add(dpr_ref, _colsum(dh * xhat), first)


GLA_RB = 256
GLA_CPB = GLA_RB // C


def _sigmoid(x):
    return 1.0 / (1.0 + jnp.exp(-x))


def _tri(strict):
    r = lax.broadcasted_iota(jnp.int32, (C, C), 0)
    c = lax.broadcasted_iota(jnp.int32, (C, C), 1)
    return jnp.where(c < r if strict else c <= r, 1.0, 0.0).astype(BF16)


def _tri_dot(tri, x):
    hi = x.astype(BF16)
    lo = (x - hi.astype(F32)).astype(BF16)
    return _dot(tri, hi) + _dot(tri, lo)


def _gla_gates(glr_b, w2, b, tri):
    z = _dot(glr_b, w2) + b
    log_a = (jnp.minimum(z, 0.0) - jnp.log(1.0 + jnp.exp(-jnp.abs(z)))) * (1.0 / GLA_TAU)
    bcum = _tri_dot(tri, log_a)
    b_end = jnp.sum(log_a, axis=0, keepdims=True)
    return z, jnp.exp(b_end - bcum), jnp.exp(b_end)


def _gla_fwd(proj, w2p, bgate, ogain, phases=()):
    def body(p_ref, w2_ref, b_ref, og_ref, y_ref, st_out_ref, st_ref):
        @pl.when(pl.program_id(0) == 0)
        def _():
            st_ref[...] = jnp.zeros_like(st_ref)

        tri = _tri(False)

        def chunk(ci, carry):
            rows = pl.ds(pl.multiple_of(ci * C, C), C)
            glr_b = p_ref[rows, LR0:LR0 + LRP].astype(BF16)
            _, ea_all, dec_all = _gla_gates(glr_b, w2_ref[...], b_ref[...], tri)
            for h in range(H):
                ea, dec = ea_all[:, h * DK:(h + 1) * DK], dec_all[:, h * DK:(h + 1) * DK]
                k_dec = (p_ref[rows, K0 + h * DK:K0 + (h + 1) * DK] * ea).astype(BF16)
                v_b = p_ref[rows, V0 + h * DV:V0 + (h + 1) * DV].astype(BF16)
                st = st_ref[h] * dec + _dot(v_b, k_dec, TN)
                st_ref[h] = st
                st_b = st.astype(BF16)
                st_out_ref[ci, h] = st_b
                q_b = (p_ref[rows, Q0 + h * DK:Q0 + (h + 1) * DK] * (DK ** -0.5)).astype(BF16)
                o = _dot(q_b, st_b, NT)
                on = o * _rinv(o)
                g = p_ref[rows, G0 + h * DV:G0 + (h + 1) * DV]
                y_ref[rows, h * DV:(h + 1) * DV] = (on * og_ref[:, h * DV:(h + 1) * DV] * (g * _sigmoid(g))).astype(BF16)
            return carry

        lax.fori_loop(0, GLA_CPB, chunk, 0, unroll=True)

    blocks = GLA_RB * GLA_PAD * 4 + GLA_RB * D * 2 + GLA_CPB * H * DV * DK * 2
    return _hosted(
        body, name="gla_fwd", grid=(S // GLA_RB,),
        in_specs=[pl.BlockSpec((GLA_RB, GLA_PAD), lambda i: (i, 0)),
                  pl.BlockSpec((LRP, H * DK), lambda i: (0, 0)),
                  pl.BlockSpec((1, H * DK), lambda i: (0, 0)),
                  pl.BlockSpec((1, H * DV), lambda i: (0, 0))],
        out_specs=[pl.BlockSpec((GLA_RB, H * DV), lambda i: (i, 0)),
                   pl.BlockSpec((GLA_CPB, H, DV, DK), lambda i: (i, 0, 0, 0))],
        out_shape=[_sds((S, H * DV), BF16), _sds((NC, H, DV, DK), BF16)],
        args=[proj, w2p, bgate, ogain], scratch_shapes=[pltpu.VMEM((H, DV, DK), F32)],
        block_bytes=blocks, scratch_bytes=H * DV * DK * 4, phases=phases)


def _gla_bwd(proj, dypre, states, w2p, bgate, ogain, phases=()):
    nb = S // GLA_RB

    def body(p_ref, dy_ref, st_blk_ref, st_prev_ref, w2_ref, b_ref, og_ref,
             dp_ref, dog_ref, dbg_ref, dw2_ref, r_ref):
        step = pl.program_id(0)

        @pl.when(step == 0)
        def _():
            r_ref[...] = jnp.zeros_like(r_ref)
            dog_ref[...] = jnp.zeros_like(dog_ref)
            dbg_ref[...] = jnp.zeros_like(dbg_ref)
            dw2_ref[...] = jnp.zeros_like(dw2_ref)

        tri = _tri(False)
        tri_strict = _tri(True)
        has_prev = jnp.where(step < nb - 1, 1.0, 0.0).astype(F32)

        def chunk(ci, st_prev_of):
            rows = pl.ds(ci * C if isinstance(ci, int) else pl.multiple_of(ci * C, C), C)
            glr_b = p_ref[rows, LR0:LR0 + LRP].astype(BF16)
            z, ea_all, dec_all = _gla_gates(glr_b, w2_ref[...], b_ref[...], tri)
            d_a, d_end = [], []
            for h in range(H):
                kcol = slice(h * DK, (h + 1) * DK)
                vcol = slice(h * DV, (h + 1) * DV)
                ea, dec = ea_all[:, kcol], dec_all[:, kcol]
                k_dec = p_ref[rows, K0 + h * DK:K0 + (h + 1) * DK] * ea
                k_dec_b = k_dec.astype(BF16)
                v_b = p_ref[rows, V0 + h * DV:V0 + (h + 1) * DV].astype(BF16)
                q_b = (p_ref[rows, Q0 + h * DK:Q0 + (h + 1) * DK] * (DK ** -0.5)).astype(BF16)
                st_b = st_blk_ref[ci, h]
                o = _dot(q_b, st_b, NT)
                rinv = _rinv(o)
                on = o * rinv
                g = p_ref[rows, G0 + h * DV:G0 + (h + 1) * DV]
                sg = _sigmoid(g)
                og = og_ref[:, vcol]
                dyp = dy_ref[rows, vcol]
                dp_ref[rows, G0 + h * DV:G0 + (h + 1) * DV] = (dyp * (on * og) * (sg * (1.0 + g * (1.0 - sg)))).astype(BF16)
                dpn = dyp * (g * sg)
                dog_ref[:, vcol] += _colsum(dpn * on)
                do_b = _norm_bwd(dpn * og, on, rinv).astype(BF16)
                gt = _dot(do_b, q_b, TN) + r_ref[h]
                gt_b = gt.astype(BF16)
                dp_ref[rows, Q0 + h * DK:Q0 + (h + 1) * DK] = (_dot(do_b, st_b) * (DK ** -0.5)).astype(BF16)
                dkd = _dot(v_b, gt_b)
                dp_ref[rows, V0 + h * DV:V0 + (h + 1) * DV] = _dot(k_dec_b, gt_b, NT).astype(BF16)
                dp_ref[rows, K0 + h * DK:K0 + (h + 1) * DK] = (dkd * ea).astype(BF16)
                d_a.append(dkd * k_dec)
                d_end.append(_colsum(gt * st_prev_of(h)) * dec)
                r_ref[h] = gt * dec
            dla = _tri_dot(tri_strict, jnp.concatenate(d_a, axis=1)) + jnp.concatenate(d_end, axis=1)
            dz = dla * (1.0 / GLA_TAU) * (1.0 - _sigmoid(z))
            dz_b = dz.astype(BF16)
            dbg_ref[...] += _colsum(dz)
            dw2_ref[...] += _dot(glr_b, dz_b, TN)
            dp_ref[rows, LR0:LR0 + LRP] = _dot(dz_b, w2_ref[...], NT).astype(BF16)

        def later_chunk(t, carry):
            ci = GLA_CPB - 1 - t
            chunk(ci, lambda h: st_blk_ref[ci - 1, h].astype(F32))
            return carry

        lax.fori_loop(0, GLA_CPB - 1, later_chunk, 0, unroll=True)
        chunk(0, lambda h: st_prev_ref[0, h].astype(F32) * has_prev)

    blocks = (GLA_RB * GLA_PAD * 4 + GLA_RB * D * 4 + (GLA_CPB + 1) * H * DV * DK * 2 + GLA_RB * GLA_PAD * 2)
    rev = lambda i: nb - 1 - i
    return _hosted(
        body, name="gla_bwd", grid=(nb,),
        in_specs=[pl.BlockSpec((GLA_RB, GLA_PAD), lambda i: (rev(i), 0)),
                  pl.BlockSpec((GLA_RB, H * DV), lambda i: (rev(i), 0)),
                  pl.BlockSpec((GLA_CPB, H, DV, DK), lambda i: (rev(i), 0, 0, 0)),
                  pl.BlockSpec((1, H, DV, DK), lambda i: (jnp.maximum(rev(i) * GLA_CPB - 1, 0), 0, 0, 0)),
                  pl.BlockSpec((LRP, H * DK), lambda i: (0, 0)),
                  pl.BlockSpec((1, H * DK), lambda i: (0, 0)),
                  pl.BlockSpec((1, H * DV), lambda i: (0, 0))],
        out_specs=[pl.BlockSpec((GLA_RB, GLA_PAD), lambda i: (rev(i), 0)),
                   pl.BlockSpec((1, H * DV), lambda i: (0, 0)),
                   pl.BlockSpec((1, H * DK), lambda i: (0, 0)),
                   pl.BlockSpec((LRP, H * DK), lambda i: (0, 0))],
        out_shape=[_sds((S, GLA_PAD), BF16), _sds((1, H * DV), F32), _sds((1, H * DK), F32), _sds((LRP, H * DK), F32)],
        args=[proj, dypre, states, states, w2p, bgate, ogain], scratch_shapes=[pltpu.VMEM((H, DV, DK), F32)],
        block_bytes=blocks, scratch_bytes=H * DV * DK * 4, phases=phases)


SGU_RB = 256
GELU_C = 0.7978845608028654
GELU_A = 0.044715


def _gelu(x):
    return 0.5 * x * (1.0 + jnp.tanh(GELU_C * (x + GELU_A * x * x * x)))


def _gelu_grad(x):
    t = jnp.tanh(GELU_C * (x + GELU_A * x * x * x))
    return 0.5 * (1.0 + t) + 0.5 * x * (1.0 - t * t) * (GELU_C * (1.0 + 3.0 * GELU_A * x * x))


def _causal_mask(transposed=False):
    i = lax.broadcasted_iota(jnp.int32, (SGU_BLOCK, SGU_BLOCK), 1 if transposed else 0)
    j = lax.broadcasted_iota(jnp.int32, (SGU_BLOCK, SGU_BLOCK), 0 if transposed else 1)
    return (i >= C) | (j < C)


def _layer_norm(vf, gain, bias):
    mu = jnp.mean(vf, axis=-1, keepdims=True)
    cen = vf - mu
    rstd = lax.rsqrt(jnp.mean(cen * cen, axis=-1, keepdims=True) + EPS)
    xhat = cen * rstd
    return xhat, rstd, xhat * gain + bias


def _sgu_fwd(proj, lng, lnb, ws, bsb, phases=()):
    def body(p_ref, g_ref, b_ref, ws_ref, bs_ref, o_ref):
        mask = _causal_mask()
        for n in range(SGU_RB // SGU_BLOCK):
            rows = slice(n * SGU_BLOCK, (n + 1) * SGU_BLOCK)
            _, _, vn = _layer_norm(_gelu(p_ref[rows, D:2 * D]), g_ref[...], b_ref[...])
            vn_b = vn.astype(BF16)
            for gi in range(SGU_G):
                cols = slice(gi * SGU_GD, (gi + 1) * SGU_GD)
                w = jnp.where(mask, ws_ref[gi], 0.0).astype(BF16)
                vs = _dot(w, vn_b[:, cols]) + bs_ref[gi]
                gate = p_ref[rows, 2 * D + gi * SGU_GD:2 * D + (gi + 1) * SGU_GD]
                o_ref[rows, cols] = (_gelu(p_ref[rows, cols]) * vs * (gate * _sigmoid(gate))).astype(BF16)

    blocks = SGU_RB * SGU_COLS * 4 + SGU_RB * D * 2 + SGU_G * SGU_BLOCK * (SGU_BLOCK + SGU_GD) * 4
    return _hosted(
        body, name="sgu_fwd", grid=(S // SGU_RB,),
        in_specs=[pl.BlockSpec((SGU_RB, SGU_COLS), lambda i: (i, 0)),
                  pl.BlockSpec((1, D), lambda i: (0, 0)), pl.BlockSpec((1, D), lambda i: (0, 0)),
                  pl.BlockSpec((SGU_G, SGU_BLOCK, SGU_BLOCK), lambda i: (0, 0, 0)),
                  pl.BlockSpec((SGU_G, SGU_BLOCK, SGU_GD), lambda i: (0, 0, 0))],
        out_specs=[pl.BlockSpec((SGU_RB, D), lambda i: (i, 0))], out_shape=[_sds((S, D), BF16)],
        args=[proj, lng, lnb, ws, bsb], block_bytes=blocks, phases=phases)


def _sgu_bwd(proj, dpre, lng, lnb, ws, wst, bsb, phases=()):
    nsteps = S // SGU_RB

    def body(p_ref, d_ref, g_ref, b_ref, ws_ref, wst_ref, bs_ref,
             dp_ref, dg_ref, db_ref, dws_ref, dbs_ref, dvn_ref, dvs_acc_ref):
        step = pl.program_id(0)

        @pl.when(step == 0)
        def _():
            dg_ref[...] = jnp.zeros_like(dg_ref)
            db_ref[...] = jnp.zeros_like(db_ref)
            dws_ref[...] = jnp.zeros_like(dws_ref)
            dvs_acc_ref[...] = jnp.zeros_like(dvs_acc_ref)

        mask = _causal_mask()
        maskt = _causal_mask(transposed=True)
        for n in range(SGU_RB // SGU_BLOCK):
            rows = slice(n * SGU_BLOCK, (n + 1) * SGU_BLOCK)
            v = p_ref[rows, D:2 * D]
            xhat, rstd, vn = _layer_norm(_gelu(v), g_ref[...], b_ref[...])
            vn_b = vn.astype(BF16)
            for gi in range(SGU_G):
                cols = slice(gi * SGU_GD, (gi + 1) * SGU_GD)
                w = jnp.where(mask, ws_ref[gi], 0.0).astype(BF16)
                wt = jnp.where(maskt, wst_ref[gi], 0.0).astype(BF16)
                vs = _dot(w, vn_b[:, cols]) + bs_ref[gi]
                u = p_ref[rows, cols]
                gate = p_ref[rows, 2 * D + gi * SGU_GD:2 * D + (gi + 1) * SGU_GD]
                sg = _sigmoid(gate)
                gu = _gelu(u)
                dpre_g = d_ref[rows, cols]
                t = dpre_g * (gate * sg)
                dp_ref[rows, cols] = (t * vs * _gelu_grad(u)).astype(BF16)
                dp_ref[rows, 2 * D + gi * SGU_GD:2 * D + (gi + 1) * SGU_GD] = (
                    dpre_g * gu * vs * (sg * (1.0 + gate * (1.0 - sg)))).astype(BF16)
                dvs = t * gu
                dvs_b = dvs.astype(BF16)
                dvs_acc_ref[:, cols] += dvs
                dws_ref[gi] += _dot(dvs_b, vn_b[:, cols], NT)
                dvn_ref[:, cols] = _dot(wt, dvs_b)
            dvn = dvn_ref[...]
            dg_ref[...] += _colsum(dvn * xhat)
            db_ref[...] += _colsum(dvn)
            dxh = dvn * g_ref[...]
            dvf = rstd * (dxh - jnp.mean(dxh, axis=-1, keepdims=True) - xhat * jnp.mean(dxh * xhat, axis=-1, keepdims=True))
            dp_ref[rows, D:2 * D] = (dvf * _gelu_grad(v)).astype(BF16)

        @pl.when(step == nsteps - 1)
        def _():
            lane = lax.broadcasted_iota(jnp.int32, (SGU_BLOCK, SGU_BLOCK), 1)
            out = jnp.zeros((SGU_BLOCK, SGU_BLOCK), F32)
            for gi in range(SGU_G):
                out = out + jnp.where(lane == gi, jnp.sum(dvs_acc_ref[:, gi * SGU_GD:(gi + 1) * SGU_GD], axis=1, keepdims=True), 0.0)
                dws_ref[gi] = jnp.where(mask, dws_ref[gi], 0.0)
            dbs_ref[...] = out

    blocks = SGU_RB * SGU_COLS * 6 + SGU_RB * D * 4 + SGU_G * SGU_BLOCK * (3 * SGU_BLOCK + SGU_GD) * 4
    const3 = lambda i: (0, 0, 0)
    return _hosted(
        body, name="sgu_bwd", grid=(nsteps,),
        in_specs=[pl.BlockSpec((SGU_RB, SGU_COLS), lambda i: (i, 0)),
                  pl.BlockSpec((SGU_RB, D), lambda i: (i, 0)),
                  pl.BlockSpec((1, D), lambda i: (0, 0)), pl.BlockSpec((1, D), lambda i: (0, 0)),
                  pl.BlockSpec((SGU_G, SGU_BLOCK, SGU_BLOCK), const3),
                  pl.BlockSpec((SGU_G, SGU_BLOCK, SGU_BLOCK), const3),
                  pl.BlockSpec((SGU_G, SGU_BLOCK, SGU_GD), const3)],
        out_specs=[pl.BlockSpec((SGU_RB, SGU_COLS), lambda i: (i, 0)),
                   pl.BlockSpec((1, D), lambda i: (0, 0)), pl.BlockSpec((1, D), lambda i: (0, 0)),
                   pl.BlockSpec((SGU_G, SGU_BLOCK, SGU_BLOCK), const3),
                   pl.BlockSpec((SGU_BLOCK, SGU_BLOCK), lambda i: (0, 0))],
        out_shape=[_sds((S, SGU_COLS), BF16), _sds((1, D), F32), _sds((1, D), F32),
                   _sds((SGU_G, SGU_BLOCK, SGU_BLOCK), F32), _sds((SGU_BLOCK, SGU_BLOCK), F32)],
        args=[proj, dpre, lng, lnb, ws, wst, bsb],
        scratch_shapes=[pltpu.VMEM((SGU_BLOCK, D), F32), pltpu.VMEM((SGU_BLOCK, D), F32)],
        block_bytes=blocks, scratch_bytes=2 * SGU_BLOCK * D * 4, phases=phases)


def _pair_sum(own, a, r0, nr, name, table=None):
    c = own.shape[2]
    tr = 256
    assert r0 % tr == 0 and nr % tr == 0

    def body(own_ref, sib_ref, o_ref):
        o_ref[...] = (own_ref[...].astype(F32) + sib_ref[...].astype(F32)).astype(BF16)

    own_map = ((lambda j, i: (1 + j, r0 // tr + i, 0)) if table is None else
               (lambda j, i, t: (t[1 + j], r0 // tr + i, 0)))
    cpad = -(-c // 128) * 128
    outs, _ = _hosted(
        body, name=name, grid=(3, nr // tr),
        in_specs=[pl.BlockSpec((None, tr, c), own_map),
                  pl.BlockSpec((None, tr, c), lambda j, i, *t: (1 + j, r0 // tr + i, 0))],
        out_specs=[pl.BlockSpec((None, tr, c), lambda j, i, *t: (j, i, 0))], out_shape=[_sds((3, nr, c), BF16)],
        args=[own, a], block_bytes=3 * tr * cpad * 2, table=table)
    return outs[0]


def _adamw_math(w, g, m, v):
    m = ADAM_B1 * m + (1.0 - ADAM_B1) * g
    v = ADAM_B2 * v + (1.0 - ADAM_B2) * (g * g)
    m_hat = m / (1.0 - ADAM_B1 ** ADAM_STEP)
    v_hat = v / (1.0 - ADAM_B2 ** ADAM_STEP)
    delta = -ADAM_LR * (m_hat / (jnp.sqrt(v_hat) + ADAM_EPS) + ADAM_WD * w)
    return delta, m, v


def _sum_adamw(own, a, b, w, m, v, *, name, phases=(), table=None):
    r, c = w.shape
    tr = 256

    def body(own_ref, sib_ref, far_ref, w_ref, m_ref, v_ref, g_ref, d_ref, nm_ref, nv_ref):
        g = own_ref[...].astype(F32) + sib_ref[...].astype(F32)
        for j in range(3):
            g = g + far_ref[j].astype(F32)
        g_ref[...] = g
        d_ref[...], nm_ref[...], nv_ref[...] = _adamw_math(w_ref[...], g, m_ref[...], v_ref[...])

    spec = pl.BlockSpec((tr, c), lambda i, *t: (i, 0))
    own_map = (lambda i: (0, i, 0)) if table is None else (lambda i, t: (t[0], i, 0))
    cpad = -(-c // 128) * 128
    return _hosted(
        body, name=name, grid=(r // tr,),
        in_specs=[pl.BlockSpec((None, tr, c), own_map), pl.BlockSpec((None, tr, c), lambda i, *t: (0, i, 0)),
                  pl.BlockSpec((3, tr, c), lambda i, *t: (0, i, 0)), spec, spec, spec],
        out_specs=[spec] * 4, out_shape=[_sds((r, c), F32)] * 4, args=[own, a, b, w, m, v],
        block_bytes=5 * tr * cpad * 2 + 7 * tr * cpad * 4, phases=phases, table=table)


def _sum_parts(parts, name):
    n, r, c = parts.shape

    def body(p_ref, o_ref):
        g = p_ref[0]
        for j in range(1, n):
            g = g + p_ref[j]
        o_ref[...] = g

    outs, _ = _hosted(body, name=name, grid=(1,), in_specs=[pl.BlockSpec((n, r, c), lambda i: (0, 0, 0))],
                      out_specs=[pl.BlockSpec((r, c), lambda i: (0, 0))], out_shape=[_sds((r, c), F32)], args=[parts],
                      block_bytes=(n + 1) * r * c * 4)
    return outs[0]


def _adamw(w, g, m, v, name):
    def body(w_ref, g_ref, m_ref, v_ref, d_ref, nm_ref, nv_ref):
        d_ref[...], nm_ref[...], nv_ref[...] = _adamw_math(w_ref[...], g_ref[...], m_ref[...], v_ref[...])

    spec = pl.BlockSpec(w.shape, lambda i: (0, 0))
    outs, _ = _hosted(body, name=name, grid=(1,), in_specs=[spec] * 4, out_specs=[spec] * 3, out_shape=[_sds(w.shape, F32)] * 3,
                      args=[w, g, m, v], block_bytes=7 * _nbytes(w.shape, F32))
    return outs


def _blocks_to_columns(g):
    n, r, c = g.shape
    return jnp.transpose(g, (1, 0, 2)).reshape(r, n * c)


def _pack(parts):
    return jnp.concatenate([p.reshape(-1) for p in parts]).reshape(-1, 128)


def _unpack(packed, like):
    flat, outs, off = packed.reshape(-1), [], 0
    for p in like:
        outs.append(flat[off:off + p.size].reshape(p.shape))
        off += p.size
    return outs


def kernel(x, norm_pre, norm_post, gla_w_in, gla_w_gate2, gla_b_gate, gla_o_gain, gla_w_out, sgu_w_in, sgu_ln_gain, sgu_ln_bias, sgu_w_spatial, sgu_b_spatial, sgu_w_out, loss_target, m_norm_pre, m_norm_post, m_gla_w_in, m_gla_w_gate2, m_gla_b_gate, m_gla_o_gain, m_gla_w_out, m_sgu_w_in, m_sgu_ln_gain, m_sgu_ln_bias, m_sgu_w_spatial, m_sgu_b_spatial, m_sgu_w_out, v_norm_pre, v_norm_post, v_gla_w_in, v_gla_w_gate2, v_gla_b_gate, v_gla_o_gain, v_gla_w_out, v_sgu_w_in, v_sgu_ln_gain, v_sgu_ln_bias, v_sgu_w_spatial, v_sgu_b_spatial, v_sgu_w_out):
    me = _index_of(*_place())
    x0 = x.reshape(S, D)
    tgt = loss_target.reshape(S, D)
    npre0, npre1 = norm_pre[0:1], norm_pre[1:2]
    npost0, npost1 = norm_post[0:1], norm_post[1:2]
    ws = sgu_w_spatial[0]
    wst = jnp.transpose(ws, (0, 2, 1))
    bsb = jnp.broadcast_to(sgu_b_spatial[0][:, :, None], (SGU_G, SGU_BLOCK, SGU_GD))
    W_ROWS = D // N_DEV
    IN_COLS_G, IN_COLS_S = GLA_COLS // N_DEV, SGU_COLS // N_DEV

    s_gwi, s_gwo = gla_w_in[0].astype(BF16), gla_w_out[0].astype(BF16)
    s_swi, s_swo = sgu_w_in[0].astype(BF16), sgu_w_out[0].astype(BF16)
    small = jnp.concatenate([jnp.pad(gla_w_gate2[0].reshape(4, 512), ((0, 4), (0, 0))),
                             jnp.pad(jnp.concatenate([sgu_ln_gain, sgu_ln_bias], axis=1), ((0, 7), (0, 0)))], axis=0)

    wg_in, g_small = _gather_first(s_gwi, small, "gather_first")
    w2 =_blocks_to_columns(g_small[:, :4, :].reshape(N_DEV, LR, 128))
    w2p = jnp.pad(w2, ((0, LRP - LR), (0, 0))).astype(BF16)
    lng = g_small[:, 8, :256].reshape(1, D)
    lnb = g_small[:, 8, 256:].reshape(1, D)
    like_gwo, like_swi = _sds((N_DEV, W_ROWS, D), BF16), _sds((N_DEV, D, IN_COLS_S), BF16)

    h0 = _prenorm(x0, npre0)
    proj0, (g_gwo, g_swi) = _mm(h0, wg_in, "nn", F32, tm=1024, tn=896, tk=D, name="gla_in", b_tiled=True, phases=[
        _Phase(like_gwo, None, [_gather_send(s_gwo, 0, W_ROWS)]),
        _Phase(like_swi, None, [_gather_send(s_swi, 0, 768, diagonal=False)])])
    (ypre0, states), (g_gwo, g_swi) = _gla_fwd(proj0, w2p, gla_b_gate, gla_o_gain, phases=[
        _Phase(like_gwo, g_gwo, [_gather_pass(0, W_ROWS)]),
        _Phase(like_swi, g_swi, [_gather_relay(0, 768), _gather_send(s_swi, 768, 512, diagonal=False)])])
    wg_out = g_gwo.reshape(D, D)
    y0, (g_swi,) = _mm(ypre0, wg_out, "nn", F32, tm=1024, tn=1024, tk=D, name="gla_out", phases=[
        _Phase(like_swi, g_swi, [_gather_pass(0, 768), _gather_relay(768, 512), _gather_send(s_swi, 1280, 512, diagonal=False)])])
    (x1, h1), (g_swi,) = _mid_fwd(x0, y0, npost0, npre1, phases=[
        _Phase(like_swi, g_swi, [_gather_pass(768, 512), _gather_relay(1280, 512), _gather_send(s_swi, 1792, 256, diagonal=False)])])
    g_swi, = _carry([_Phase(like_swi, g_swi, [_gather_pass(1280, 512), _gather_relay(1792, 256)])], "relay_sgu_w_in")
    g_swi, = _carry([_Phase(like_swi, g_swi, [_gather_pass(1792, 256)])], "pass_sgu_w_in")
    proj1, (g_swo,) = _mm(h1, g_swi, "nn", F32, tm=1024, tn=IN_COLS_S, tk=D, name="sgu_in", b_blocked=True, phases=[
        _Phase(like_gwo, None, [_gather_send(s_swo, 0, W_ROWS)])])
    (pre1,), (g_swo,) = _sgu_fwd(proj1, lng, lnb, ws, bsb, phases=[_Phase(like_gwo, g_swo, [_gather_pass(0, W_ROWS)])])
    ws_out = g_swo.reshape(D, D)
    y1, _ = _mm(pre1, ws_out, "nn", F32, tm=1024, tn=1024, tk=D, name="sgu_out")
    loss_cols, dx2, dy1, dnpost1 = _final(x1, y1, tgt, npost1)
    loss = lax.psum(0.5 * jnp.sum(loss_cols) / D, ("x", "y", "c"))

    like_b_out, like_b_swi = _sds((3, W_ROWS, D), BF16), _sds((3, D, IN_COLS_S), BF16)
    like_b_gwi = _sds((3, D, IN_COLS_G), BF16)
    row_pair = dict(like=_sds((4, W_ROWS, D), BF16), block=lambda i, j: i, ordinal=lambda i, j: i >> 1,
                    dst=lambda ref, k, i, j: ref.at[k])
    col_pair = dict(like=_sds((4, D, IN_COLS_S), BF16), block=lambda i, j: j, ordinal=lambda i, j: 4 * i + (j >> 1),
                    dst=lambda ref, k, i, j: ref.at[k, pl.ds(pl.multiple_of(i * 1024, 1024), 1024)])

    mine = _own_table()
    dws_out, (a_swo,) = _mm(pre1, dy1, "tn", BF16, tm=W_ROWS, tn=D, tk=S, name="sgu_out_dw", pair=row_pair)
    p_swo = dws_out.reshape(N_DEV, W_ROWS, D)
    t_swo = _pair_sum(p_swo, a_swo, 0, W_ROWS, "pair_sum_sgu_w_out", table=mine)
    dpre1, _ = _mm(dy1, ws_out, "nt", F32, tm=1024, tn=1024, tk=D, name="sgu_out_dx")
    (dproj1, dlng, dlnb, dwsp, dbsp), (b_swo,) = _sgu_bwd(proj1, dpre1, lng, lnb, ws, wst, bsb, phases=[
        _Phase(like_b_out, None, [_reduce_cross(t_swo, 0, 0, W_ROWS)])])
    p_swi, (a_swi,) = _mm(h1, dproj1, "tn", BF16, tm=1024, tn=IN_COLS_S, tk=S, name="sgu_in_dw", out_blocked=True, pair=col_pair)
    t_swi = _pair_sum(p_swi, a_swi, 0, D, "pair_sum_sgu_w_in", table=mine)
    (dx1, dy0, dnpre1, dnpost0), (b_swi,) = _nt_rows(
        dproj1, g_swi, tk=IN_COLS_S, name="sgu_in_dx", b_blocked=True, row_ins=[dx2, x1, y0], vec_ins=[npre1, npost0],
        outs=[("row", F32), ("row", BF16), ("vec", F32), ("vec", F32)], tail=_mid_bwd_tail, phases=[
            _Phase(like_b_swi, None, [_reduce_cross(t_swi, 0, 0, 1024)])])
    dwg_out, (a_gwo,) = _mm(ypre0, dy0, "tn", BF16, tm=W_ROWS, tn=D, tk=S, name="gla_out_dw", pair=row_pair)
    p_gwo = dwg_out.reshape(N_DEV, W_ROWS, D)
    t_gwo = _pair_sum(p_gwo, a_gwo, 0, W_ROWS, "pair_sum_gla_w_out", table=mine)
    dypre0, _ = _mm(dy0, wg_out, "nt", F32, tm=1024, tn=1024, tk=D, name="gla_out_dx")
    late = [dnpre1, dnpost1, dlng, dlnb, dwsp, jnp.transpose(dbsp[:, :SGU_G])]
    late_pack = _pack(late)
    (dproj0, dogain, dbgate, dw2), (b_swi, b_gwo, g_late) = _gla_bwd(proj0, dypre0, states, w2p, gla_b_gate, gla_o_gain, phases=[
        _Phase(like_b_swi, b_swi, [_reduce_cross(t_swi, 1024, 1024, 1024)]),
        _Phase(like_b_out, None, [_reduce_cross(t_gwo, 0, 0, W_ROWS)]),
        _Phase(_sds((N_DEV,) + late_pack.shape, F32), None, [_gather_send(late_pack, 0, late_pack.shape[0])])])
    half = D // 2
    dwg_in_a, (g_late,) = _mm(h0, dproj0, "tn", BF16, tm=half, tn=896, tk=S, name="gla_in_dw_a", m_tiles=(0, 1), phases=[
        _Phase(_sds((N_DEV,) + late_pack.shape, F32), g_late, [_gather_pass(0, late_pack.shape[0])])])
    own_gwi, a_gwi = _blockify_pair(dwg_in_a, None, None, 0, "blockify_gla_w_in_a")
    t_gwi_a = _pair_sum(own_gwi, a_gwi, 0, half, "pair_sum_gla_w_in_a")
    dwg_in_b, (b_gwi,) = _mm(h0, dproj0, "tn", BF16, tm=half, tn=896, tk=S, name="gla_in_dw_b", m_tiles=(1, 1), phases=[
        _Phase(like_b_gwi, None, [_reduce_cross(t_gwi_a, 0, 0, 512)])])
    own_gwi, a_gwi = _blockify_pair(dwg_in_b, own_gwi, a_gwi, half, "blockify_gla_w_in_b")
    t_gwi_b = _pair_sum(own_gwi, a_gwi, half, half, "pair_sum_gla_w_in_b")
    (grad_x, dnpre0), (b_gwi,) = _nt_rows(
        dproj0, wg_in, tk=896, name="gla_in_dx", b_tiled=True, row_ins=[dx1, x0], vec_ins=[npre0],
        outs=[("row", F32), ("vec", F32)], tail=_first_bwd_tail, phases=[
            _Phase(like_b_gwi, b_gwi, [_reduce_cross(t_gwi_a, 512, 512, 512), _reduce_cross(t_gwi_b, 0, half, half)])])

    early = [dnpre0, dnpost0, dbgate, dogain, dw2[:LR]]
    early_pack = _pack(early)
    like_early = _sds((N_DEV,) + early_pack.shape, F32)
    (g_swo, d_swo, nm_swo, nv_swo), (g_early,) = _sum_adamw(
        p_swo, a_swo, b_swo, sgu_w_out[0], m_sgu_w_out[0], v_sgu_w_out[0], name="adamw_sgu_w_out", table=mine, phases=[
            _Phase(like_early, None, [_gather_send(early_pack, 0, early_pack.shape[0])])])
    (g_gwo_, d_gwo, nm_gwo, nv_gwo), (g_early,) = _sum_adamw(
        p_gwo, a_gwo, b_gwo, gla_w_out[0], m_gla_w_out[0], v_gla_w_out[0], name="adamw_gla_w_out", table=mine, phases=[
            _Phase(like_early, g_early, [_gather_pass(0, early_pack.shape[0])])])
    (g_swi_, d_swi, nm_swi, nv_swi), _ = _sum_adamw(
        p_swi, a_swi, b_swi, sgu_w_in[0], m_sgu_w_in[0], v_sgu_w_in[0], name="adamw_sgu_w_in", table=mine)
    (g_gwi_, d_gwi, nm_gwi, nv_gwi), _ = _sum_adamw(
        own_gwi, a_gwi, b_gwi, gla_w_in[0], m_gla_w_in[0], v_gla_w_in[0], name="adamw_gla_w_in")

    g_npre1, g_npost1, g_lng_full, g_lnb_full, g_wsp, g_bsp = _unpack(_sum_parts(g_late, "sum_late_small_grads"), late)
    g_npre0, g_npost0, g_bgate, g_ogain, g_w2_full = _unpack(_sum_parts(g_early, "sum_early_small_grads"), early)
    g_w2 = lax.dynamic_slice(g_w2_full, (0, me * 128), (LR, 128))
    g_lng = lax.dynamic_slice(g_lng_full, (0, me * 256), (1, 256))
    g_lnb = lax.dynamic_slice(g_lnb_full, (0, me * 256), (1, 256))
    small_g = [jnp.concatenate([g_npre0, g_npre1], 0), jnp.concatenate([g_npost0, g_npost1], 0), g_w2, g_bgate, g_ogain,
               g_lng, g_lnb, g_wsp, g_bsp]
    small_w = [norm_pre, norm_post, gla_w_gate2[0], gla_b_gate, gla_o_gain, sgu_ln_gain, sgu_ln_bias, sgu_w_spatial[0], sgu_b_spatial[0]]
    small_m = [m_norm_pre, m_norm_post, m_gla_w_gate2[0], m_gla_b_gate, m_gla_o_gain, m_sgu_ln_gain, m_sgu_ln_bias, m_sgu_w_spatial[0], m_sgu_b_spatial[0]]
    small_v = [v_norm_pre, v_norm_post, v_gla_w_gate2[0], v_gla_b_gate, v_gla_o_gain, v_sgu_ln_gain, v_sgu_ln_bias, v_sgu_w_spatial[0], v_sgu_b_spatial[0]]
    d_pack, nm_pack, nv_pack = _adamw(_pack(small_w), _pack(small_g), _pack(small_m), _pack(small_v), "adamw_small")

    out_like = [norm_pre, norm_post, gla_w_gate2, gla_b_gate, gla_o_gain, sgu_ln_gain, sgu_ln_bias, sgu_w_spatial, sgu_b_spatial]
    sg_ = [g.reshape(s.shape) for g, s in zip(small_g, out_like)]
    sd_, sm_, sv_ = (_unpack(pk, out_like) for pk in (d_pack, nm_pack, nv_pack))

    def assemble(small_list, w_in_g, w_out_g, w_in_s, w_out_s):
        npre_, npost_, w2_, bg_, og_, lg_, lb_, wsp_, bsp_ = small_list
        return [npre_, npost_, w_in_g[None], w2_, bg_, og_, w_out_g[None], w_in_s[None], lg_, lb_, wsp_, bsp_, w_out_s[None]]

    return (loss, grad_x.reshape(1, S, D),
            *assemble(sg_, g_gwi_, g_gwo_, g_swi_, g_swo),
            *assemble(sd_, d_gwi, d_gwo, d_swi, d_swo),
            *assemble(sm_, nm_gwi, nm_gwo, nm_swi, nm_swo),
            *assemble(sv_, nv_gwi, nv_gwo, nv_swi, nv_swo))
```

```python
import functools

import jax
import jax.numpy as jnp
from jax import lax
from jax.experimental import pallas as pl
from jax.experimental.pallas import tpu as pltpu

F32 = jnp.float32
BF16 = jnp.bfloat16

N_DEV = 8
S = 2048
D = 2048
H = 4
DK = 256
DV = 512
C = 64
NC = S // C
GLA_COLS = 6160
GLA_PAD = 6272
Q0, K0, V0, G0, LR0 = 0, 1024, 2048, 4096, 6144
LR = 16
LRP = 128
SGU_COLS = 6144
SGU_BLOCK = 128
SGU_G = 8
SGU_GD = 256
EPS = 1e-6
GLA_TAU = 16.0

ADAM_LR, ADAM_B1, ADAM_B2, ADAM_EPS, ADAM_WD, ADAM_STEP = 0.001, 0.9, 0.999, 1e-08, 0.01, 10

V7X_VMEM_BYTES = 64 * 1024 * 1024
VMEM_CEILING = V7X_VMEM_BYTES - 6 * 1024 * 1024
MESH = pl.DeviceIdType.MESH
HBM_SPEC = pl.BlockSpec(memory_space=pl.ANY)


def _sds(shape, dtype):
    return jax.ShapeDtypeStruct(tuple(shape), dtype)


def _nbytes(shape, dtype):
    n = 1
    for s in shape:
        n *= s
    return n * jnp.dtype(dtype).itemsize


def _dot(a, b, dims=(((1,), (0,)), ((), ())), precision=None):
    return lax.dot_general(a, b, dims, precision=precision, preferred_element_type=F32)


NN = (((1,), (0,)), ((), ()))
TN = (((0,), (0,)), ((), ()))
NT = (((1,), (1,)), ((), ()))


def _place():
    return lax.axis_index("x"), lax.axis_index("y"), lax.axis_index("c")


def _index_of(px, py, pc):
    return 4 * px + 2 * py + pc


def _chips(x, y):
    return [(1 - x, y), (x, 1 - y), (1 - x, 1 - y)]


def _rcopy(src, dst, send_sem, recv_sem, to):
    return pltpu.make_async_remote_copy(src_ref=src, dst_ref=dst, send_sem=send_sem, recv_sem=recv_sem,
                                        device_id=to, device_id_type=MESH)


class _Move:
    def __init__(self, ins, n_remote, make, stage=None):
        self.ins, self.n_remote, self.make, self.stage = list(ins), n_remote, make, stage

    def scratch(self):
        sems = [pltpu.SemaphoreType.DMA((self.n_remote,)), pltpu.SemaphoreType.DMA((self.n_remote,))]
        return sems if self.stage is None else sems + [pltpu.SemaphoreType.DMA((1,)), pltpu.VMEM(*self.stage)]

    def start(self, in_refs, buf, scratch):
        sends, _, local = self.make(in_refs, buf, scratch[0], scratch[1])
        if local is not None:
            pltpu.make_async_copy(local[0], scratch[3], scratch[2].at[0]).start()
        for cp in sends:
            cp.start()

    def finish(self, in_refs, buf, scratch):
        sends, arrivals, local = self.make(in_refs, buf, scratch[0], scratch[1])
        if local is not None:
            pltpu.make_async_copy(local[0], scratch[3], scratch[2].at[0]).wait()
            out = pltpu.make_async_copy(scratch[3], local[1], scratch[2].at[0])
            out.start()
        for cp in arrivals:
            cp.wait_recv()
        for cp in sends:
            cp.wait_send()
        if local is not None:
            out.wait()


class _Phase:
    def __init__(self, like, so_far, moves):
        self.like, self.so_far, self.moves = like, so_far, list(moves)


def _gather_send(shard, r0, nr, diagonal=True):
    def make(in_refs, g, ss, rs):
        sh, = in_refs
        x, y, c = _place()
        me = _index_of(x, y, c)
        rows = pl.ds(r0, nr)
        peers = [(x, y, 1 - c)] + [(px, py, c) for px, py in _chips(x, y)[:3 if diagonal else 2]]
        sends = [_rcopy(sh.at[rows], g.at[me, rows], ss.at[k], rs.at[k], p) for k, p in enumerate(peers)]
        arrivals = [_rcopy(sh.at[rows], g.at[_index_of(*p), rows], ss.at[k], rs.at[k], p) for k, p in enumerate(peers)]
        return sends, arrivals, (sh.at[rows], g.at[me, rows])

    return _Move([shard], 4 if diagonal else 3, make, stage=((nr, shard.shape[1]), shard.dtype))


def _gather_relay(r0, nr):
    def make(in_refs, g, ss, rs):
        x, y, c = _place()
        nx, ny, nd = [(px, py, c) for px, py in _chips(x, y)]
        first, second = pl.ds(r0, nr // 2), pl.ds(r0 + nr // 2, nr // 2)
        sends = [_rcopy(g.at[_index_of(*nx), first], g.at[_index_of(*nx), first], ss.at[0], rs.at[0], ny),
                 _rcopy(g.at[_index_of(*ny), second], g.at[_index_of(*ny), second], ss.at[1], rs.at[1], nx)]
        arrivals = [_rcopy(g.at[_index_of(*nx), first], g.at[_index_of(*nd), first], ss.at[0], rs.at[0], ny),
                    _rcopy(g.at[_index_of(*ny), second], g.at[_index_of(*nd), second], ss.at[1], rs.at[1], nx)]
        return sends, arrivals, None

    return _Move([], 2, make)


def _gather_pass(r0, nr):
    def make(in_refs, g, ss, rs):
        x, y, c = _place()
        rows = pl.ds(r0, nr)
        sends = [_rcopy(g.at[_index_of(px, py, c), rows], g.at[_index_of(px, py, c), rows], ss.at[j], rs.at[j], (x, y, 1 - c))
                 for j, (px, py) in enumerate(_chips(x, y))]
        arrivals = [_rcopy(g.at[_index_of(px, py, c), rows], g.at[_index_of(px, py, 1 - c), rows], ss.at[j], rs.at[j], (x, y, 1 - c))
                    for j, (px, py) in enumerate(_chips(x, y))]
        return sends, arrivals, None

    return _Move([], 3, make)


def _own_table():
    x, y, c = _place()
    return jnp.stack([_index_of(px, py, c) for px, py in [(x, y)] + _chips(x, y)]).astype(jnp.int32)


def _blockify_pair(dw, own_so_far, a_so_far, dst_r0, name):
    rows, tr, cw, win = dw.shape[0], 256, GLA_COLS // N_DEV, 896
    n_steps = rows // tr

    def body(*refs):
        x_ref, own_ref, a_ref, stage_ref, send_sems, recv_sem = refs[0], *refs[-5:]
        i = pl.program_id(0)
        x, y, c = _place()

        def send(slot, k):
            dst = a_ref.at[k, pl.ds(pl.multiple_of(dst_r0 + i * tr, tr), tr)]
            return _rcopy(stage_ref.at[slot], dst, send_sems.at[slot], recv_sem.at[0], (x, y, 1 - c))

        for j in range(N_DEV):
            window = x_ref[:, 768 * j:768 * j + win].astype(F32)
            tile = (pltpu.roll(window, win - 2 * j, 1) if j else window)[:, :cw].astype(BF16)
            k = ((j >> 2) ^ x) + 2 * (((j >> 1) & 1) ^ y)

            @pl.when((j & 1) == c)
            def _():
                own_ref[k] = tile

            @pl.when((j & 1) != c)
            def _():
                slot = (j >> 1) & 1
                if j >> 1 >= 2:
                    send(slot, k).wait_send()
                else:
                    pl.when(i > 0)(lambda: send(slot, k).wait_send())
                stage_ref[slot] = tile
                send(slot, k).start()

        @pl.when(i == n_steps - 1)
        def _():
            send(0, 0).wait_send()
            send(1, 0).wait_send()
            arrived = a_ref.at[:, pl.ds(dst_r0, rows)]
            _rcopy(arrived, arrived, send_sems.at[0], recv_sem.at[0], (x, y, 1 - c)).wait_recv()

    continues = a_so_far is not None
    own, a = pl.pallas_call(
        body, grid=(n_steps,),
        in_specs=[pl.BlockSpec((tr, GLA_PAD), lambda i: (i, 0))] + [HBM_SPEC] * (2 * continues),
        out_specs=[pl.BlockSpec((4, tr, cw), lambda i: (0, dst_r0 // tr + i, 0)), HBM_SPEC],
        out_shape=[_sds((4, D, cw), BF16), _sds((4, D, cw), BF16)],
        scratch_shapes=[pltpu.VMEM((2, tr, cw), BF16), pltpu.SemaphoreType.DMA((2,)), pltpu.SemaphoreType.DMA((1,))],
        input_output_aliases={1: 0, 2: 1} if continues else {},
        compiler_params=pltpu.CompilerParams(dimension_semantics=("arbitrary",), vmem_limit_bytes=48 * 1024 * 1024),
        name=name,
    )(*([dw] + [own_so_far, a_so_far] * continues))
    return own, a


def _reduce_cross(sums, src_r0, dst_r0, nr):
    def make(in_refs, b, ss, rs):
        t, = in_refs
        x, y, c = _place()
        src, dst = pl.ds(src_r0, nr), pl.ds(dst_r0, nr)
        sends = [_rcopy(t.at[j, src], b.at[j, dst], ss.at[j], rs.at[j], (px, py, c)) for j, (px, py) in enumerate(_chips(x, y))]
        return sends, sends, None

    return _Move([sums], 3, make)


def _hosted(body, *, name, grid, in_specs, out_specs, out_shape, args, scratch_shapes=(), block_bytes, scratch_bytes=0,
            phases=(), table=None):
    n_in, n_out, n_scr = len(args), len(out_shape), len(scratch_shapes)
    all_args, all_out_shape, sems, aliases, layout = list(args), list(out_shape), [], {}, []
    for j, ph in enumerate(phases):
        counts = []
        for mv in ph.moves:
            all_args += mv.ins
            counts.append(len(mv.ins))
            sems += mv.scratch()
        if ph.so_far is not None:
            aliases[len(all_args)] = n_out + j
            all_args.append(ph.so_far)
        layout.append((counts, ph.so_far is not None))
        all_out_shape.append(ph.like)
    n_extra_in = len(all_args) - n_in

    def wrapped(*refs):
        ins, pos = refs[:n_in], n_in
        move_ins = []
        for counts, continues in layout:
            per_move = []
            for cnt in counts:
                per_move.append(refs[pos:pos + cnt])
                pos += cnt
            pos += continues
            move_ins.append(per_move)
        outs = refs[pos:pos + n_out]
        bufs = refs[pos + n_out:pos + n_out + len(phases)]
        pos += n_out + len(phases)
        scratch = refs[pos:pos + n_scr]
        pos += n_scr
        move_sems = []
        for ph in phases:
            per_move = []
            for mv in ph.moves:
                count = len(mv.scratch())
                per_move.append(refs[pos:pos + count])
                pos += count
            move_sems.append(per_move)

        def each_move(fn_name):
            for ph, buf, per_in, per_sem in zip(phases, bufs, move_ins, move_sems):
                for mv, mv_in, mv_sem in zip(ph.moves, per_in, per_sem):
                    getattr(mv, fn_name)(mv_in, buf, mv_sem)

        if phases:
            first = functools.reduce(jnp.logical_and, [pl.program_id(a) == 0 for a in range(len(grid))])
            last = functools.reduce(jnp.logical_and, [pl.program_id(a) == grid[a] - 1 for a in range(len(grid))])
            pl.when(first)(lambda: each_move("start"))
        body(*ins, *outs, *scratch)
        if phases:
            pl.when(last)(lambda: each_move("finish"))

    all_args = [pltpu.with_memory_space_constraint(a, pltpu.HBM) for a in all_args]
    est = 2 * block_bytes + scratch_bytes
    params = pltpu.CompilerParams(dimension_semantics=("arbitrary",) * len(grid),
                                  vmem_limit_bytes=min(VMEM_CEILING, max(32 * 1024 * 1024, 2 * est)))
    all_in_specs, all_out_specs = list(in_specs) + [HBM_SPEC] * n_extra_in, list(out_specs) + [HBM_SPEC] * len(phases)
    if table is None:
        results = pl.pallas_call(
            wrapped, grid=grid, in_specs=all_in_specs, out_specs=all_out_specs, out_shape=all_out_shape,
            scratch_shapes=list(scratch_shapes) + sems, input_output_aliases=aliases, compiler_params=params, name=name,
        )(*all_args)
    else:
        results = pl.pallas_call(
            lambda table_ref, *refs: wrapped(*refs),
            grid_spec=pltpu.PrefetchScalarGridSpec(num_scalar_prefetch=1, grid=grid, in_specs=all_in_specs, out_specs=all_out_specs,
                                                   scratch_shapes=list(scratch_shapes) + sems),
            out_shape=all_out_shape, input_output_aliases={k + 1: v for k, v in aliases.items()}, compiler_params=params, name=name,
        )(table, *all_args)
    return list(results[:n_out]), list(results[n_out:])


def _carry(phases, name):
    def body(o_ref):
        o_ref[...] = jnp.zeros_like(o_ref)

    _, bufs = _hosted(body, name=name, grid=(1,), in_specs=[], out_specs=[pl.BlockSpec((8, 128), lambda i: (0, 0))],
                      out_shape=[_sds((8, 128), F32)], args=[], block_bytes=8 * 128 * 4, phases=phases)
    return bufs


def _gather_first(shard, small, name):
    cw, tr, n_tiles = shard.shape[1], 256, GLA_PAD // 128

    def body(sh_ref, sm_ref, wn_ref, g_ref, gs_ref, wt_ref, win_ref, tmp_ref, send_sems, recv_sems, local_sems):
        x, y, c = _place()
        me, sibling = (x, y, c), (x, y, 1 - c)
        chips = _chips(x, y)

        def copy(base, out_ref, k, block, to, src=None):
            dst = out_ref.at[_index_of(*block)]
            return _rcopy(dst if src is None else src, dst, send_sems.at[base + k], recv_sems.at[base + k], to)

        wcopy = functools.partial(copy, 0, g_ref)
        scopy = functools.partial(copy, 8, gs_ref)

        def relay(k, block, half, to):
            rows = pl.ds(half * (D // 2), D // 2)
            ref = g_ref.at[_index_of(*block), rows]
            return _rcopy(ref, ref, send_sems.at[k], recv_sems.at[k], to)

        def load(src_ref, slot):
            cp = pltpu.make_async_copy(src_ref, win_ref.at[slot], local_sems.at[0])
            cp.start()
            cp.wait()

        def place(slot, block):
            b = _index_of(*block)

            def rows_chunk(r, carry):
                rows = pl.ds(pl.multiple_of(r * tr, tr), tr)
                tmp_ref[:, :cw] = win_ref[slot, rows, :].astype(F32)
                shifted = pltpu.roll(tmp_ref[...], 2 * b, 1)
                for u in range(7):
                    wt_ref[6 * b + u, rows, :] = (wt_ref[6 * b + u, rows, :].astype(F32) + shifted[:, 128 * u:128 * (u + 1)]).astype(BF16)
                return carry

            lax.fori_loop(0, D // tr, rows_chunk, 0)

        small_own = pltpu.make_async_copy(sm_ref, gs_ref.at[_index_of(*me)], local_sems.at[1])
        small_own.start()
        first = [wcopy(1 + j, me, (*chip, c), src=sh_ref) for j, chip in enumerate(chips[:2])]
        first += [scopy(0, me, sibling, src=sm_ref)] + [scopy(1 + j, me, (*chip, c), src=sm_ref) for j, chip in enumerate(chips)]
        for cp in first:
            cp.start()

        def clear(t, carry):
            wt_ref[t] = jnp.zeros((D, 128), BF16)
            return carry

        lax.fori_loop(0, n_tiles, clear, 0)
        tmp_ref[...] = jnp.zeros_like(tmp_ref)

        def emit(t):
            pltpu.make_async_copy(wt_ref.at[t], wn_ref.at[t], local_sems.at[2]).start()

        def take(block, slot, arrivals=None, pass_on=None):
            for cp in arrivals or ():
                cp.wait_recv()
            load(sh_ref if arrivals is None else g_ref.at[_index_of(*block)], slot)
            if pass_on is not None:
                pass_on.start()
            place(slot, block)
            for u in range(1, 6):
                emit(6 * _index_of(*block) + u)

        near_x, near_y, far = [(*chip, c) for chip in chips]
        to_sibling = wcopy(0, me, sibling, src=win_ref.at[0])
        pass_x = wcopy(4, near_x, sibling, src=win_ref.at[1])
        pass_y = wcopy(5, near_y, sibling, src=win_ref.at[0])
        pass_d = wcopy(6, far, sibling, src=win_ref.at[0])
        relays = [relay(3, near_x, 0, near_y), relay(7, near_y, 1, near_x)]
        take(me, 0, pass_on=to_sibling)
        take(near_x, 1, [wcopy(1, near_x, me)], pass_x)
        relays[0].start()
        to_sibling.wait_send()
        take(near_y, 0, [wcopy(2, near_y, me)], pass_y)
        relays[1].start()
        small_passed = []
        for j, chip in enumerate(chips):
            scopy(1 + j, (*chip, c), me).wait_recv()
            cp = scopy(4 + j, (*chip, c), sibling)
            cp.start()
            small_passed.append(cp)
        pass_x.wait_send()
        take(sibling, 1, [wcopy(0, sibling, me)])
        pass_y.wait_send()
        take((*chips[0], 1 - c), 0, [wcopy(4, (*chips[0], 1 - c), me)])
        take((*chips[1], 1 - c), 1, [wcopy(5, (*chips[1], 1 - c), me)])
        take(far, 0, [relay(3, far, 0, near_y), relay(7, far, 1, near_x)], pass_d)
        take((*chips[2], 1 - c), 1, [wcopy(6, (*chips[2], 1 - c), me)])
        for t in range(0, n_tiles, 6):
            emit(t)
        scopy(0, sibling, me).wait_recv()
        for j, chip in enumerate(chips):
            scopy(4 + j, (*chip, 1 - c), me).wait_recv()
        for cp in first + small_passed + relays + [pass_d]:
            cp.wait_send()
        small_own.wait()
        pltpu.make_async_copy(wn_ref, wn_ref, local_sems.at[2]).wait()

    wn, _, gs = pl.pallas_call(
        body,
        in_specs=[HBM_SPEC] * 2, out_specs=[HBM_SPEC] * 3,
        out_shape=[_sds((n_tiles, D, 128), BF16), _sds((N_DEV,) + shard.shape, BF16), _sds((N_DEV,) + small.shape, small.dtype)],
        scratch_shapes=[pltpu.VMEM((n_tiles, D, 128), BF16), pltpu.VMEM((2, D, cw), BF16), pltpu.VMEM((tr, 7 * 128), F32),
                        pltpu.SemaphoreType.DMA((15,)), pltpu.SemaphoreType.DMA((15,)), pltpu.SemaphoreType.DMA((3,))],
        compiler_params=pltpu.CompilerParams(vmem_limit_bytes=48 * 1024 * 1024),
        name=name,
    )(shard, small)
    return wn, gs


def _mm(a, b, mode, out_dtype, *, tm, tn, tk, name, b_blocked=False, b_tiled=False, out_blocked=False, m_tiles=None, pair=None,
        phases=()):
    if mode == "nn":
        (m, k), dims = a.shape, NN
        a_blk, a_map = (tm, tk), (lambda i, j, kk: (i, kk))
        if b_blocked:
            assert b.shape[1] == k and b.shape[2] == tn and tk == k
            n = b.shape[0] * tn
            b_spec = pl.BlockSpec((None, tk, tn), lambda i, j, kk: (j, kk, 0))
        elif b_tiled:
            assert b.shape[1] == k and b.shape[2] == 128 and tn % 128 == 0
            n = b.shape[0] * 128
            b_spec = pl.BlockSpec((tn // 128, tk, 128), lambda i, j, kk: (j, kk, 0))
        else:
            assert b.shape[0] == k
            n = b.shape[1]
            b_spec = pl.BlockSpec((tk, tn), lambda i, j, kk: (kk, j))
    elif mode == "tn":
        (k, m), n, dims = a.shape, b.shape[1], TN
        assert b.shape[0] == k
        first = 0 if m_tiles is None else m_tiles[0]
        a_blk, a_map = (tk, tm), (lambda i, j, kk: (kk, i + first))
        b_spec = pl.BlockSpec((tk, tn), lambda i, j, kk: (kk, j))
    else:
        (m, k), dims = a.shape, NT
        a_blk, a_map = (tm, tk), (lambda i, j, kk: (i, kk))
        if b_blocked:
            assert b.shape[0] * b.shape[2] == k and b.shape[2] == tk
            n = b.shape[1]
            b_spec = pl.BlockSpec((None, tn, tk), lambda i, j, kk: (kk, j, 0))
        elif b_tiled:
            assert b.shape[0] * 128 == k and b.shape[2] == 128 and tk % 128 == 0
            n = b.shape[1]
            b_spec = pl.BlockSpec((tk // 128, tn, 128), lambda i, j, kk: (kk, j, 0))
        else:
            assert b.shape[1] == k
            n = b.shape[0]
            b_spec = pl.BlockSpec((tn, tk), lambda i, j, kk: (j, kk))
    assert m % tm == 0 and n % tn == 0 and k % tk == 0, (a.shape, b.shape, mode)
    nk = k // tk
    n_row_tiles = m // tm if m_tiles is None else m_tiles[1]
    if out_blocked:
        out_shape, out_spec = _sds((n // tn, n_row_tiles * tm, tn), out_dtype), pl.BlockSpec((None, tm, tn), lambda i, j, kk: (j, i, 0))
    else:
        out_shape, out_spec = _sds((n_row_tiles * tm, n), out_dtype), pl.BlockSpec((tm, tn), lambda i, j, kk: (i, j))

    grid = (n_row_tiles, n // tn, nk)

    def body(a_ref, b_ref, o_ref, *rest):
        rhs = jnp.concatenate([b_ref[u] for u in range(b_ref.shape[0])], axis=1) if b_tiled else b_ref[...]
        p = _dot(a_ref[...], rhs, dims)
        if nk == 1:
            o_ref[...] = p.astype(out_dtype)
            if pair is not None:
                _send_to_sibling(p.astype(out_dtype), *rest)
        else:
            acc_ref, = rest
            kk = pl.program_id(2)

            @pl.when(kk == 0)
            def _():
                acc_ref[...] = p

            @pl.when(kk > 0)
            def _():
                acc_ref[...] += p

            @pl.when(kk == nk - 1)
            def _():
                o_ref[...] = acc_ref[...].astype(out_dtype)

    def _send_to_sibling(tile, pair_ref, stage_ref, send_sems, recv_sem):
        i, j = pl.program_id(0), pl.program_id(1)
        x, y, c = _place()
        blk = pair["block"](i, j)
        k = ((blk >> 2) ^ x) + 2 * (((blk >> 1) & 1) ^ y)
        ordinal = pair["ordinal"](i, j)

        def send(slot):
            return _rcopy(stage_ref.at[slot], pair["dst"](pair_ref, k, i, j), send_sems.at[slot], recv_sem.at[0], (x, y, 1 - c))

        @pl.when((blk & 1) != c)
        def _():
            slot = ordinal & 1

            @pl.when(ordinal >= 2)
            def _():
                send(slot).wait_send()

            stage_ref[slot] = tile
            send(slot).start()

        @pl.when((i == grid[0] - 1) & (j == grid[1] - 1))
        def _():
            send(0).wait_send()
            send(1).wait_send()
            _rcopy(pair_ref, pair_ref, send_sems.at[0], recv_sem.at[0], (x, y, 1 - c)).wait_recv()

    blocks = _nbytes(a_blk, a.dtype) + tk * tn * jnp.dtype(b.dtype).itemsize + _nbytes((tm, tn), out_dtype)
    out_specs, out_shapes, scratch = [out_spec], [out_shape], [] if nk == 1 else [pltpu.VMEM((tm, tn), F32)]
    scratch_bytes = _nbytes((tm, tn), F32) * (nk > 1)
    if pair is not None:
        assert nk == 1
        out_specs, out_shapes = out_specs + [HBM_SPEC], out_shapes + [pair["like"]]
        scratch = [pltpu.VMEM((2, tm, tn), out_dtype), pltpu.SemaphoreType.DMA((2,)), pltpu.SemaphoreType.DMA((1,))]
        scratch_bytes = 2 * _nbytes((tm, tn), out_dtype)
    outs, bufs = _hosted(
        body, name=name, grid=grid,
        in_specs=[pl.BlockSpec(a_blk, a_map), b_spec], out_specs=out_specs, out_shape=out_shapes, args=[a, b],
        scratch_shapes=scratch, block_bytes=blocks, scratch_bytes=scratch_bytes, phases=phases)
    return outs[0], outs[1:] + bufs


NT_ROWS_TM = 512
NT_ROWS_SUB = 128


def _nt_rows(a, b, *, tk, name, row_ins, vec_ins, outs, tail, b_blocked=False, b_tiled=False, phases=()):
    m, k = a.shape
    tm, nk = NT_ROWS_TM, k // tk
    if b_blocked:
        assert b.shape[0] * b.shape[2] == k and b.shape[2] == tk and b.shape[1] == D
        b_spec = pl.BlockSpec((None, D, tk), lambda i, kk: (kk, 0, 0))
    else:
        assert b_tiled and b.shape[0] * 128 == k and tk % 128 == 0 and b.shape[1] == D
        b_spec = pl.BlockSpec((tk // 128, D, 128), lambda i, kk: (kk, 0, 0))
    row_spec, vec_spec = pl.BlockSpec((tm, D), lambda i, kk: (i, 0)), pl.BlockSpec((1, D), lambda i, kk: (0, 0))
    n_row, n_vec, n_out = len(row_ins), len(vec_ins), len(outs)
    assert nk >= 2

    def body(a_ref, b_ref, *rest):
        row_hbm, vec_refs = rest[:n_row], rest[n_row:n_row + n_vec]
        out_refs = rest[n_row + n_vec:n_row + n_vec + n_out]
        acc_ref, row_sems = rest[n_row + n_vec + n_out], rest[-1]
        row_refs = rest[n_row + n_vec + n_out + 1:-1]
        rhs = jnp.concatenate([b_ref[u] for u in range(b_ref.shape[0])], axis=1) if b_tiled else b_ref[...]
        p = _dot(a_ref[...], rhs, NT)
        i, kk = pl.program_id(0), pl.program_id(1)

        def fetch(r):
            return pltpu.make_async_copy(row_hbm[r].at[pl.ds(pl.multiple_of(i * tm, tm), tm)], row_refs[r], row_sems.at[r])

        @pl.when(kk == 0)
        def _():
            for r in range(n_row):
                fetch(r).start()
            acc_ref[...] = p

        @pl.when(kk > 0)
        def _():
            acc_ref[...] += p

        @pl.when(kk == nk - 1)
        def _():
            for r in range(n_row):
                fetch(r).wait()
            for s in range(tm // NT_ROWS_SUB):
                rows = slice(s * NT_ROWS_SUB, (s + 1) * NT_ROWS_SUB)
                tail(acc_ref[rows, :], rows, (i == 0) if s == 0 else None, row_refs, vec_refs, out_refs)

    out_specs = [row_spec if kind == "row" else vec_spec for kind, _ in outs]
    out_shape = [_sds((m, D) if kind == "row" else (1, D), dt) for kind, dt in outs]
    blocks = tm * tk * 2 + D * tk * 2 + sum(tm * D * jnp.dtype(dt).itemsize for kind, dt in outs if kind == "row")
    scratch = [pltpu.VMEM((tm, D), F32)] + [pltpu.VMEM((tm, D), x.dtype) for x in row_ins] + [pltpu.SemaphoreType.DMA((n_row,))]
    return _hosted(body, name=name, grid=(m // tm, nk), in_specs=[pl.BlockSpec((tm, tk), lambda i, kk: (i, kk)), b_spec]
                   + [HBM_SPEC] * n_row + [vec_spec] * n_vec, out_specs=out_specs, out_shape=out_shape,
                   args=[a, b] + list(row_ins) + list(vec_ins), scratch_shapes=scratch,
                   block_bytes=blocks, scratch_bytes=(1 + n_row) * tm * D * 4, phases=phases)


def _nn_rows(a, b, *, name, row_ins, vec_ins, outs, tail):
    m, k = a.shape
    tm = NT_ROWS_TM
    assert b.shape == (k, D)
    row_spec, vec_spec = pl.BlockSpec((tm, D), lambda i: (i, 0)), pl.BlockSpec((1, D), lambda i: (0, 0))
    n_row, n_vec, n_out = len(row_ins), len(vec_ins), len(outs)

    def body(a_ref, b_ref, *rest):
        row_hbm, vec_refs = rest[:n_row], rest[n_row:n_row + n_vec]
        out_refs = rest[n_row + n_vec:n_row + n_vec + n_out]
        d_ref, row_sems = rest[n_row + n_vec + n_out], rest[-1]
        row_refs = rest[n_row + n_vec + n_out + 1:-1]
        i = pl.program_id(0)
        fetches = [pltpu.make_async_copy(row_hbm[r].at[pl.ds(pl.multiple_of(i * tm, tm), tm)], row_refs[r], row_sems.at[r])
                   for r in range(n_row)]
        for cp in fetches:
            cp.start()
        d_ref[...] = _dot(a_ref[...], b_ref[...])
        for cp in fetches:
            cp.wait()
        for s in range(tm // NT_ROWS_SUB):
            rows = slice(s * NT_ROWS_SUB, (s + 1) * NT_ROWS_SUB)
            tail(d_ref[rows, :], rows, (i == 0) if s == 0 else None, row_refs, vec_refs, out_refs)

    out_specs = [row_spec if kind == "row" else vec_spec for kind, _ in outs]
    out_shape = [_sds((m, D) if kind == "row" else (1, D), dt) for kind, dt in outs]
    blocks = tm * k * 2 + k * D * 2 + sum(tm * D * jnp.dtype(dt).itemsize for kind, dt in outs if kind == "row")
    scratch = [pltpu.VMEM((tm, D), F32)] + [pltpu.VMEM((tm, D), x.dtype) for x in row_ins] + [pltpu.SemaphoreType.DMA((n_row,))]
    outs_, _ = _hosted(body, name=name, grid=(m // tm,), in_specs=[pl.BlockSpec((tm, k), lambda i: (i, 0)),
                                                                 pl.BlockSpec((k, D), lambda i: (0, 0))]
                       + [HBM_SPEC] * n_row + [vec_spec] * n_vec, out_specs=out_specs, out_shape=out_shape,
                       args=[a, b] + list(row_ins) + list(vec_ins), scratch_shapes=scratch,
                       block_bytes=blocks, scratch_bytes=(1 + n_row) * tm * D * 4)
    return outs_


def _vec_add(ref, value, first):
    if first is None:
        ref[...] += value
    else:
        pl.when(first)(lambda: ref.__setitem__(Ellipsis, value))
        pl.when(jnp.logical_not(first))(lambda: ref.__setitem__(Ellipsis, ref[...] + value))


RB = 256


def _row_spec(width):
    return pl.BlockSpec((RB, width), lambda i: (i, 0))


def _vec_spec(width):
    return pl.BlockSpec((1, width), lambda i: (0, 0))


def _rinv(x):
    return lax.rsqrt(jnp.mean(x * x, axis=-1, keepdims=True) + EPS)


def _norm_bwd(dyn, xhat, r):
    return r * (dyn - xhat * jnp.mean(dyn * xhat, axis=-1, keepdims=True))


def _colsum(x):
    return jnp.sum(x, axis=0, keepdims=True)


def _prenorm(x, gain):
    def body(x_ref, g_ref, h_ref):
        xv = x_ref[...]
        h_ref[...] = (xv * _rinv(xv) * g_ref[...]).astype(BF16)

    outs, _ = _hosted(body, name="prenorm", grid=(S // RB,), in_specs=[_row_spec(D), _vec_spec(D)], out_specs=[_row_spec(D)],
                      out_shape=[_sds((S, D), BF16)], args=[x, gain], block_bytes=RB * D * 6)
    return outs[0]


def _mid_fwd(x, y, npost, npre, phases=()):
    def body(x_ref, y_ref, po_ref, pr_ref, x1_ref, h1_ref):
        yv = y_ref[...]
        x1 = x_ref[...] + yv * _rinv(yv) * po_ref[...]
        x1_ref[...] = x1
        h1_ref[...] = (x1 * _rinv(x1) * pr_ref[...]).astype(BF16)

    return _hosted(body, name="mid_fwd", grid=(S // RB,), in_specs=[_row_spec(D), _row_spec(D), _vec_spec(D), _vec_spec(D)],
                   out_specs=[_row_spec(D), _row_spec(D)], out_shape=[_sds((S, D), F32), _sds((S, D), BF16)],
                   args=[x, y, npost, npre], block_bytes=RB * D * 14, phases=phases)


def _final_tail(yv, rows, first, row_refs, vec_refs, out_refs):
    (x_ref, t_ref), (po_ref,), (loss_ref, dx_ref, dy_ref, dpo_ref) = row_refs, vec_refs, out_refs
    r = _rinv(yv)
    yhat = yv * r
    err = x_ref[rows, :] + yhat * po_ref[...] - t_ref[rows, :]
    dx = err * (1.0 / D)
    dx_ref[rows, :] = dx
    dy_ref[rows, :] = _norm_bwd(dx * po_ref[...], yhat, r).astype(BF16)
    _vec_add(loss_ref, _colsum(err * err), first)
    _vec_add(dpo_ref, _colsum(dx * yhat), first)


def _mid_bwd_tail(dh, rows, first, row_refs, vec_refs, out_refs):
    (dx2_ref, x_ref, y_ref), (pr_ref, po_ref), (dx1_ref, dy_ref, dpr_ref, dpo_ref) = row_refs, vec_refs, out_refs
    xv = x_ref[rows, :]
    r = _rinv(xv)
    xhat = xv * r
    dx1 = dx2_ref[rows, :] + _norm_bwd(dh * pr_ref[...], xhat, r)
    dx1_ref[rows, :] = dx1
    yv = y_ref[rows, :]
    ry = _rinv(yv)
    yhat = yv * ry
    dy_ref[rows, :] = _norm_bwd(dx1 * po_ref[...], yhat, ry).astype(BF16)
    _vec_add(dpr_ref, _colsum(dh * xhat), first)
    _vec_add(dpo_ref, _colsum(dx1 * yhat), first)


def _first_bwd_tail(dh, rows, first, row_refs, vec_refs, out_refs):
    (dx1_ref, x_ref), (pr_ref,), (gx_ref, dpr_ref) = row_refs, vec_refs, out_refs
    xv = x_ref[rows, :]
    r = _rinv(xv)
    xhat = xv * r
    gx_ref[rows, :] = dx1_ref[rows, :] + _norm_bwd(dh * pr_ref[...], xhat, r)
    _vec_add(dpr_ref, _colsum(dh * xhat), first)


GLA_RB = 256
GLA_CPB = GLA_RB // C


def _sigmoid(x):
    return 1.0 / (1.0 + jnp.exp(-x))


def _tri(strict):
    r = lax.broadcasted_iota(jnp.int32, (C, C), 0)
    c = lax.broadcasted_iota(jnp.int32, (C, C), 1)
    return jnp.where(c < r if strict else c <= r, 1.0, 0.0).astype(BF16)


def _tri_dot(tri, x):
    hi = x.astype(BF16)
    lo = (x - hi.astype(F32)).astype(BF16)
    return _dot(tri, hi) + _dot(tri, lo)


def _gla_gates(glr_b, w2, b, tri):
    z = _dot(glr_b, w2) + b
    log_a = (jnp.minimum(z, 0.0) - jnp.log(1.0 + jnp.exp(-jnp.abs(z)))) * (1.0 / GLA_TAU)
    bcum = _tri_dot(tri, log_a)
    b_end = jnp.sum(log_a, axis=0, keepdims=True)
    return z, jnp.exp(b_end - bcum), jnp.exp(b_end)


def _gla_fwd(proj, w2p, bgate, ogain, phases=()):
    def body(p_ref, w2_ref, b_ref, og_ref, y_ref, st_out_ref, st_ref):
        @pl.when(pl.program_id(0) == 0)
        def _():
            st_ref[...] = jnp.zeros_like(st_ref)

        tri = _tri(False)

        def chunk(ci, carry):
            rows = pl.ds(pl.multiple_of(ci * C, C), C)
            glr_b = p_ref[rows, LR0:LR0 + LRP].astype(BF16)
            _, ea_all, dec_all = _gla_gates(glr_b, w2_ref[...], b_ref[...], tri)
            for h in range(H):
                ea, dec = ea_all[:, h * DK:(h + 1) * DK], dec_all[:, h * DK:(h + 1) * DK]
                k_dec = (p_ref[rows, K0 + h * DK:K0 + (h + 1) * DK] * ea).astype(BF16)
                v_b = p_ref[rows, V0 + h * DV:V0 + (h + 1) * DV].astype(BF16)
                st = st_ref[h] * dec + _dot(v_b, k_dec, TN)
                st_ref[h] = st
                st_b = st.astype(BF16)
                st_out_ref[ci, h] = st_b
                q_b = (p_ref[rows, Q0 + h * DK:Q0 + (h + 1) * DK] * (DK ** -0.5)).astype(BF16)
                o = _dot(q_b, st_b, NT)
                on = o * _rinv(o)
                g = p_ref[rows, G0 + h * DV:G0 + (h + 1) * DV]
                y_ref[rows, h * DV:(h + 1) * DV] = (on * og_ref[:, h * DV:(h + 1) * DV] * (g * _sigmoid(g))).astype(BF16)
            return carry

        lax.fori_loop(0, GLA_CPB, chunk, 0, unroll=True)

    blocks = GLA_RB * GLA_PAD * 4 + GLA_RB * D * 2 + GLA_CPB * H * DV * DK * 2
    return _hosted(
        body, name="gla_fwd", grid=(S // GLA_RB,),
        in_specs=[pl.BlockSpec((GLA_RB, GLA_PAD), lambda i: (i, 0)),
                  pl.BlockSpec((LRP, H * DK), lambda i: (0, 0)),
                  pl.BlockSpec((1, H * DK), lambda i: (0, 0)),
                  pl.BlockSpec((1, H * DV), lambda i: (0, 0))],
        out_specs=[pl.BlockSpec((GLA_RB, H * DV), lambda i: (i, 0)),
                   pl.BlockSpec((GLA_CPB, H, DV, DK), lambda i: (i, 0, 0, 0))],
        out_shape=[_sds((S, H * DV), BF16), _sds((NC, H, DV, DK), BF16)],
        args=[proj, w2p, bgate, ogain], scratch_shapes=[pltpu.VMEM((H, DV, DK), F32)],
        block_bytes=blocks, scratch_bytes=H * DV * DK * 4, phases=phases)


def _gla_bwd(proj, dypre, states, w2p, bgate, ogain, phases=()):
    nb = S // GLA_RB

    def body(p_ref, dy_ref, st_blk_ref, st_prev_ref, w2_ref, b_ref, og_ref,
             dp_ref, dog_ref, dbg_ref, dw2_ref, r_ref):
        step = pl.program_id(0)

        @pl.when(step == 0)
        def _():
            r_ref[...] = jnp.zeros_like(r_ref)
            dog_ref[...] = jnp.zeros_like(dog_ref)
            dbg_ref[...] = jnp.zeros_like(dbg_ref)
            dw2_ref[...] = jnp.zeros_like(dw2_ref)

        tri = _tri(False)
        tri_strict = _tri(True)
        has_prev = jnp.where(step < nb - 1, 1.0, 0.0).astype(F32)

        def chunk(ci, st_prev_of):
            rows = pl.ds(ci * C if isinstance(ci, int) else pl.multiple_of(ci * C, C), C)
            glr_b = p_ref[rows, LR0:LR0 + LRP].astype(BF16)
            z, ea_all, dec_all = _gla_gates(glr_b, w2_ref[...], b_ref[...], tri)
            d_a, d_end = [], []
            for h in range(H):
                kcol = slice(h * DK, (h + 1) * DK)
                vcol = slice(h * DV, (h + 1) * DV)
                ea, dec = ea_all[:, kcol], dec_all[:, kcol]
                k_dec = p_ref[rows, K0 + h * DK:K0 + (h + 1) * DK] * ea
                k_dec_b = k_dec.astype(BF16)
                v_b = p_ref[rows, V0 + h * DV:V0 + (h + 1) * DV].astype(BF16)
                q_b = (p_ref[rows, Q0 + h * DK:Q0 + (h + 1) * DK] * (DK ** -0.5)).astype(BF16)
                st_b = st_blk_ref[ci, h]
                o = _dot(q_b, st_b, NT)
                rinv = _rinv(o)
                on = o * rinv
                g = p_ref[rows, G0 + h * DV:G0 + (h + 1) * DV]
                sg = _sigmoid(g)
                og = og_ref[:, vcol]
                dyp = dy_ref[rows, vcol]
                dp_ref[rows, G0 + h * DV:G0 + (h + 1) * DV] = (dyp * (on * og) * (sg * (1.0 + g * (1.0 - sg)))).astype(BF16)
                dpn = dyp * (g * sg)
                dog_ref[:, vcol] += _colsum(dpn * on)
                do_b = _norm_bwd(dpn * og, on, rinv).astype(BF16)
                gt = _dot(do_b, q_b, TN) + r_ref[h]
                gt_b = gt.astype(BF16)
                dp_ref[rows, Q0 + h * DK:Q0 + (h + 1) * DK] = (_dot(do_b, st_b) * (DK ** -0.5)).astype(BF16)
                dkd = _dot(v_b, gt_b)
                dp_ref[rows, V0 + h * DV:V0 + (h + 1) * DV] = _dot(k_dec_b, gt_b, NT).astype(BF16)
                dp_ref[rows, K0 + h * DK:K0 + (h + 1) * DK] = (dkd * ea).astype(BF16)
                d_a.append(dkd * k_dec)
                d_end.append(_colsum(gt * st_prev_of(h)) * dec)
                r_ref[h] = gt * dec
            dla = _tri_dot(tri_strict, jnp.concatenate(d_a, axis=1)) + jnp.concatenate(d_end, axis=1)
            dz = dla * (1.0 / GLA_TAU) * (1.0 - _sigmoid(z))
            dz_b = dz.astype(BF16)
            dbg_ref[...] += _colsum(dz)
            dw2_ref[...] += _dot(glr_b, dz_b, TN)
            dp_ref[rows, LR0:LR0 + LRP] = _dot(dz_b, w2_ref[...], NT).astype(BF16)

        def later_chunk(t, carry):
            ci = GLA_CPB - 1 - t
            chunk(ci, lambda h: st_blk_ref[ci - 1, h].astype(F32))
            return carry

        lax.fori_loop(0, GLA_CPB - 1, later_chunk, 0, unroll=True)
        chunk(0, lambda h: st_prev_ref[0, h].astype(F32) * has_prev)

    blocks = (GLA_RB * GLA_PAD * 4 + GLA_RB * D * 4 + (GLA_CPB + 1) * H * DV * DK * 2 + GLA_RB * GLA_PAD * 2)
    rev = lambda i: nb - 1 - i
    return _hosted(
        body, name="gla_bwd", grid=(nb,),
        in_specs=[pl.BlockSpec((GLA_RB, GLA_PAD), lambda i: (rev(i), 0)),
                  pl.BlockSpec((GLA_RB, H * DV), lambda i: (rev(i), 0)),
                  pl.BlockSpec((GLA_CPB, H, DV, DK), lambda i: (rev(i), 0, 0, 0)),
                  pl.BlockSpec((1, H, DV, DK), lambda i: (jnp.maximum(rev(i) * GLA_CPB - 1, 0), 0, 0, 0)),
                  pl.BlockSpec((LRP, H * DK), lambda i: (0, 0)),
                  pl.BlockSpec((1, H * DK), lambda i: (0, 0)),
                  pl.BlockSpec((1, H * DV), lambda i: (0, 0))],
        out_specs=[pl.BlockSpec((GLA_RB, GLA_PAD), lambda i: (rev(i), 0)),
                   pl.BlockSpec((1, H * DV), lambda i: (0, 0)),
                   pl.BlockSpec((1, H * DK), lambda i: (0, 0)),
                   pl.BlockSpec((LRP, H * DK), lambda i: (0, 0))],
        out_shape=[_sds((S, GLA_PAD), BF16), _sds((1, H * DV), F32), _sds((1, H * DK), F32), _sds((LRP, H * DK), F32)],
        args=[proj, dypre, states, states, w2p, bgate, ogain], scratch_shapes=[pltpu.VMEM((H, DV, DK), F32)],
        block_bytes=blocks, scratch_bytes=H * DV * DK * 4, phases=phases)


SGU_RB = 256
GELU_C = 0.7978845608028654
GELU_A = 0.044715


def _gelu(x):
    return 0.5 * x * (1.0 + jnp.tanh(GELU_C * (x + GELU_A * x * x * x)))


def _gelu_grad(x):
    t = jnp.tanh(GELU_C * (x + GELU_A * x * x * x))
    return 0.5 * (1.0 + t) + 0.5 * x * (1.0 - t * t) * (GELU_C * (1.0 + 3.0 * GELU_A * x * x))


def _causal_mask(transposed=False):
    i = lax.broadcasted_iota(jnp.int32, (SGU_BLOCK, SGU_BLOCK), 1 if transposed else 0)
    j = lax.broadcasted_iota(jnp.int32, (SGU_BLOCK, SGU_BLOCK), 0 if transposed else 1)
    return (i >= C) | (j < C)


def _layer_norm(vf, gain, bias):
    mu = jnp.mean(vf, axis=-1, keepdims=True)
    cen = vf - mu
    rstd = lax.rsqrt(jnp.mean(cen * cen, axis=-1, keepdims=True) + EPS)
    xhat = cen * rstd
    return xhat, rstd, xhat * gain + bias


def _sgu_fwd(proj, lng, lnb, ws, bsb, phases=()):
    def body(p_ref, g_ref, b_ref, ws_ref, bs_ref, o_ref):
        mask = _causal_mask()
        for n in range(SGU_RB // SGU_BLOCK):
            rows = slice(n * SGU_BLOCK, (n + 1) * SGU_BLOCK)
            _, _, vn = _layer_norm(_gelu(p_ref[rows, D:2 * D]), g_ref[...], b_ref[...])
            vn_b = vn.astype(BF16)
            for gi in range(SGU_G):
                cols = slice(gi * SGU_GD, (gi + 1) * SGU_GD)
                w = jnp.where(mask, ws_ref[gi], 0.0).astype(BF16)
                vs = _dot(w, vn_b[:, cols]) + bs_ref[gi]
                gate = p_ref[rows, 2 * D + gi * SGU_GD:2 * D + (gi + 1) * SGU_GD]
                o_ref[rows, cols] = (_gelu(p_ref[rows, cols]) * vs * (gate * _sigmoid(gate))).astype(BF16)

    blocks = SGU_RB * SGU_COLS * 4 + SGU_RB * D * 2 + SGU_G * SGU_BLOCK * (SGU_BLOCK + SGU_GD) * 4
    return _hosted(
        body, name="sgu_fwd", grid=(S // SGU_RB,),
        in_specs=[pl.BlockSpec((SGU_RB, SGU_COLS), lambda i: (i, 0)),
                  pl.BlockSpec((1, D), lambda i: (0, 0)), pl.BlockSpec((1, D), lambda i: (0, 0)),
                  pl.BlockSpec((SGU_G, SGU_BLOCK, SGU_BLOCK), lambda i: (0, 0, 0)),
                  pl.BlockSpec((SGU_G, SGU_BLOCK, SGU_GD), lambda i: (0, 0, 0))],
        out_specs=[pl.BlockSpec((SGU_RB, D), lambda i: (i, 0))], out_shape=[_sds((S, D), BF16)],
        args=[proj, lng, lnb, ws, bsb], block_bytes=blocks, phases=phases)


def _sgu_bwd(proj, dpre, lng, lnb, ws, wst, bsb, phases=()):
    nsteps = S // SGU_RB

    def body(p_ref, d_ref, g_ref, b_ref, ws_ref, wst_ref, bs_ref,
             dp_ref, dg_ref, db_ref, dws_ref, dbs_ref, dvn_ref, dvs_acc_ref):
        step = pl.program_id(0)

        @pl.when(step == 0)
        def _():
            dg_ref[...] = jnp.zeros_like(dg_ref)
            db_ref[...] = jnp.zeros_like(db_ref)
            dws_ref[...] = jnp.zeros_like(dws_ref)
            dvs_acc_ref[...] = jnp.zeros_like(dvs_acc_ref)

        mask = _causal_mask()
        maskt = _causal_mask(transposed=True)
        for n in range(SGU_RB // SGU_BLOCK):
            rows = slice(n * SGU_BLOCK, (n + 1) * SGU_BLOCK)
            v = p_ref[rows, D:2 * D]
            xhat, rstd, vn = _layer_norm(_gelu(v), g_ref[...], b_ref[...])
            vn_b = vn.astype(BF16)
            for gi in range(SGU_G):
                cols = slice(gi * SGU_GD, (gi + 1) * SGU_GD)
                w = jnp.where(mask, ws_ref[gi], 0.0).astype(BF16)
                wt = jnp.where(maskt, wst_ref[gi], 0.0).astype(BF16)
                vs = _dot(w, vn_b[:, cols]) + bs_ref[gi]
                u = p_ref[rows, cols]
                gate = p_ref[rows, 2 * D + gi * SGU_GD:2 * D + (gi + 1) * SGU_GD]
                sg = _sigmoid(gate)
                gu = _gelu(u)
                dpre_g = d_ref[rows, cols]
                t = dpre_g * (gate * sg)
                dp_ref[rows, cols] = (t * vs * _gelu_grad(u)).astype(BF16)
                dp_ref[rows, 2 * D + gi * SGU_GD:2 * D + (gi + 1) * SGU_GD] = (
                    dpre_g * gu * vs * (sg * (1.0 + gate * (1.0 - sg)))).astype(BF16)
                dvs = t * gu
                dvs_b = dvs.astype(BF16)
                dvs_acc_ref[:, cols] += dvs
                dws_ref[gi] += _dot(dvs_b, vn_b[:, cols], NT)
                dvn_ref[:, cols] = _dot(wt, dvs_b)
            dvn = dvn_ref[...]
            dg_ref[...] += _colsum(dvn * xhat)
            db_ref[...] += _colsum(dvn)
            dxh = dvn * g_ref[...]
            dvf = rstd * (dxh - jnp.mean(dxh, axis=-1, keepdims=True) - xhat * jnp.mean(dxh * xhat, axis=-1, keepdims=True))
            dp_ref[rows, D:2 * D] = (dvf * _gelu_grad(v)).astype(BF16)

        @pl.when(step == nsteps - 1)
        def _():
            lane = lax.broadcasted_iota(jnp.int32, (SGU_BLOCK, SGU_BLOCK), 1)
            out = jnp.zeros((SGU_BLOCK, SGU_BLOCK), F32)
            for gi in range(SGU_G):
                out = out + jnp.where(lane == gi, jnp.sum(dvs_acc_ref[:, gi * SGU_GD:(gi + 1) * SGU_GD], axis=1, keepdims=True), 0.0)
                dws_ref[gi] = jnp.where(mask, dws_ref[gi], 0.0)
            dbs_ref[...] = out

    blocks = SGU_RB * SGU_COLS * 6 + SGU_RB * D * 4 + SGU_G * SGU_BLOCK * (3 * SGU_BLOCK + SGU_GD) * 4
    const3 = lambda i: (0, 0, 0)
    return _hosted(
        body, name="sgu_bwd", grid=(nsteps,),
        in_specs=[pl.BlockSpec((SGU_RB, SGU_COLS), lambda i: (i, 0)),
                  pl.BlockSpec((SGU_RB, D), lambda i: (i, 0)),
                  pl.BlockSpec((1, D), lambda i: (0, 0)), pl.BlockSpec((1, D), lambda i: (0, 0)),
                  pl.BlockSpec((SGU_G, SGU_BLOCK, SGU_BLOCK), const3),
                  pl.BlockSpec((SGU_G, SGU_BLOCK, SGU_BLOCK), const3),
                  pl.BlockSpec((SGU_G, SGU_BLOCK, SGU_GD), const3)],
        out_specs=[pl.BlockSpec((SGU_RB, SGU_COLS), lambda i: (i, 0)),
                   pl.BlockSpec((1, D), lambda i: (0, 0)), pl.BlockSpec((1, D), lambda i: (0, 0)),
                   pl.BlockSpec((SGU_G, SGU_BLOCK, SGU_BLOCK), const3),
                   pl.BlockSpec((SGU_BLOCK, SGU_BLOCK), lambda i: (0, 0))],
        out_shape=[_sds((S, SGU_COLS), BF16), _sds((1, D), F32), _sds((1, D), F32),
                   _sds((SGU_G, SGU_BLOCK, SGU_BLOCK), F32), _sds((SGU_BLOCK, SGU_BLOCK), F32)],
        args=[proj, dpre, lng, lnb, ws, wst, bsb],
        scratch_shapes=[pltpu.VMEM((SGU_BLOCK, D), F32), pltpu.VMEM((SGU_BLOCK, D), F32)],
        block_bytes=blocks, scratch_bytes=2 * SGU_BLOCK * D * 4, phases=phases)


def _pair_sum(own, a, r0, nr, name, table=None):
    c = own.shape[2]
    tr = 256
    assert r0 % tr == 0 and nr % tr == 0

    def body(own_ref, sib_ref, o_ref):
        o_ref[...] = (own_ref[...].astype(F32) + sib_ref[...].astype(F32)).astype(BF16)

    own_map = ((lambda j, i: (1 + j, r0 // tr + i, 0)) if table is None else
               (lambda j, i, t: (t[1 + j], r0 // tr + i, 0)))
    cpad = -(-c // 128) * 128
    outs, _ = _hosted(
        body, name=name, grid=(3, nr // tr),
        in_specs=[pl.BlockSpec((None, tr, c), own_map),
                  pl.BlockSpec((None, tr, c), lambda j, i, *t: (1 + j, r0 // tr + i, 0))],
        out_specs=[pl.BlockSpec((None, tr, c), lambda j, i, *t: (j, i, 0))], out_shape=[_sds((3, nr, c), BF16)],
        args=[own, a], block_bytes=3 * tr * cpad * 2, table=table)
    return outs[0]


def _adamw_math(w, g, m, v):
    m = ADAM_B1 * m + (1.0 - ADAM_B1) * g
    v = ADAM_B2 * v + (1.0 - ADAM_B2) * (g * g)
    m_hat = m / (1.0 - ADAM_B1 ** ADAM_STEP)
    v_hat = v / (1.0 - ADAM_B2 ** ADAM_STEP)
    delta = -ADAM_LR * (m_hat / (jnp.sqrt(v_hat) + ADAM_EPS) + ADAM_WD * w)
    return delta, m, v


def _sum_adamw(own, a, b, w, m, v, *, name, phases=(), table=None):
    r, c = w.shape
    tr = 256

    def body(own_ref, sib_ref, far_ref, w_ref, m_ref, v_ref, g_ref, d_ref, nm_ref, nv_ref):
        g = own_ref[...].astype(F32) + sib_ref[...].astype(F32)
        for j in range(3):
            g = g + far_ref[j].astype(F32)
        g_ref[...] = g
        d_ref[...], nm_ref[...], nv_ref[...] = _adamw_math(w_ref[...], g, m_ref[...], v_ref[...])

    spec = pl.BlockSpec((tr, c), lambda i, *t: (i, 0))
    own_map = (lambda i: (0, i, 0)) if table is None else (lambda i, t: (t[0], i, 0))
    cpad = -(-c // 128) * 128
    return _hosted(
        body, name=name, grid=(r // tr,),
        in_specs=[pl.BlockSpec((None, tr, c), own_map), pl.BlockSpec((None, tr, c), lambda i, *t: (0, i, 0)),
                  pl.BlockSpec((3, tr, c), lambda i, *t: (0, i, 0)), spec, spec, spec],
        out_specs=[spec] * 4, out_shape=[_sds((r, c), F32)] * 4, args=[own, a, b, w, m, v],
        block_bytes=5 * tr * cpad * 2 + 7 * tr * cpad * 4, phases=phases, table=table)


def _sum_parts(parts, name):
    n, r, c = parts.shape

    def body(p_ref, o_ref):
        g = p_ref[0]
        for j in range(1, n):
            g = g + p_ref[j]
        o_ref[...] = g

    outs, _ = _hosted(body, name=name, grid=(1,), in_specs=[pl.BlockSpec((n, r, c), lambda i: (0, 0, 0))],
                      out_specs=[pl.BlockSpec((r, c), lambda i: (0, 0))], out_shape=[_sds((r, c), F32)], args=[parts],
                      block_bytes=(n + 1) * r * c * 4)
    return outs[0]


def _adamw(w, g, m, v, name):
    def body(w_ref, g_ref, m_ref, v_ref, d_ref, nm_ref, nv_ref):
        d_ref[...], nm_ref[...], nv_ref[...] = _adamw_math(w_ref[...], g_ref[...], m_ref[...], v_ref[...])

    spec = pl.BlockSpec(w.shape, lambda i: (0, 0))
    outs, _ = _hosted(body, name=name, grid=(1,), in_specs=[spec] * 4, out_specs=[spec] * 3, out_shape=[_sds(w.shape, F32)] * 3,
                      args=[w, g, m, v], block_bytes=7 * _nbytes(w.shape, F32))
    return outs


def _blocks_to_columns(g):
    n, r, c = g.shape
    return jnp.transpose(g, (1, 0, 2)).reshape(r, n * c)


def _pack(parts):
    return jnp.concatenate([p.reshape(-1) for p in parts]).reshape(-1, 128)


def _unpack(packed, like):
    flat, outs, off = packed.reshape(-1), [], 0
    for p in like:
        outs.append(flat[off:off + p.size].reshape(p.shape))
        off += p.size
    return outs


def kernel(x, norm_pre, norm_post, gla_w_in, gla_w_gate2, gla_b_gate, gla_o_gain, gla_w_out, sgu_w_in, sgu_ln_gain, sgu_ln_bias, sgu_w_spatial, sgu_b_spatial, sgu_w_out, loss_target, m_norm_pre, m_norm_post, m_gla_w_in, m_gla_w_gate2, m_gla_b_gate, m_gla_o_gain, m_gla_w_out, m_sgu_w_in, m_sgu_ln_gain, m_sgu_ln_bias, m_sgu_w_spatial, m_sgu_b_spatial, m_sgu_w_out, v_norm_pre, v_norm_post, v_gla_w_in, v_gla_w_gate2, v_gla_b_gate, v_gla_o_gain, v_gla_w_out, v_sgu_w_in, v_sgu_ln_gain, v_sgu_ln_bias, v_sgu_w_spatial, v_sgu_b_spatial, v_sgu_w_out):
    me = _index_of(*_place())
    x0 = x.reshape(S, D)
    tgt = loss_target.reshape(S, D)
    npre0, npre1 = norm_pre[0:1], norm_pre[1:2]
    npost0, npost1 = norm_post[0:1], norm_post[1:2]
    ws = sgu_w_spatial[0]
    wst = jnp.transpose(ws, (0, 2, 1))
    bsb = jnp.broadcast_to(sgu_b_spatial[0][:, :, None], (SGU_G, SGU_BLOCK, SGU_GD))
    W_ROWS = D // N_DEV
    IN_COLS_G, IN_COLS_S = GLA_COLS // N_DEV, SGU_COLS // N_DEV

    s_gwi, s_gwo = gla_w_in[0].astype(BF16), gla_w_out[0].astype(BF16)
    s_swi, s_swo = sgu_w_in[0].astype(BF16), sgu_w_out[0].astype(BF16)
    small = jnp.concatenate([jnp.pad(gla_w_gate2[0].reshape(4, 512), ((0, 4), (0, 0))),
                             jnp.pad(jnp.concatenate([sgu_ln_gain, sgu_ln_bias], axis=1), ((0, 7), (0, 0)))], axis=0)

    wg_in, g_small = _gather_first(s_gwi, small, "gather_first")
    w2 =_blocks_to_columns(g_small[:, :4, :].reshape(N_DEV, LR, 128))
    w2p = jnp.pad(w2, ((0, LRP - LR), (0, 0))).astype(BF16)
    lng = g_small[:, 8, :256].reshape(1, D)
    lnb = g_small[:, 8, 256:].reshape(1, D)
    like_gwo, like_swi = _sds((N_DEV, W_ROWS, D), BF16), _sds((N_DEV, D, IN_COLS_S), BF16)

    h0 = _prenorm(x0, npre0)
    proj0, (g_gwo, g_swi) = _mm(h0, wg_in, "nn", F32, tm=1024, tn=896, tk=D, name="gla_in", b_tiled=True, phases=[
        _Phase(like_gwo, None, [_gather_send(s_gwo, 0, W_ROWS)]),
        _Phase(like_swi, None, [_gather_send(s_swi, 0, 768, diagonal=False)])])
    (ypre0, states), (g_gwo, g_swi) = _gla_fwd(proj0, w2p, gla_b_gate, gla_o_gain, phases=[
        _Phase(like_gwo, g_gwo, [_gather_pass(0, W_ROWS)]),
        _Phase(like_swi, g_swi, [_gather_relay(0, 768), _gather_send(s_swi, 768, 512, diagonal=False)])])
    wg_out = g_gwo.reshape(D, D)
    y0, (g_swi,) = _mm(ypre0, wg_out, "nn", F32, tm=1024, tn=1024, tk=D, name="gla_out", phases=[
        _Phase(like_swi, g_swi, [_gather_pass(0, 768), _gather_relay(768, 512), _gather_send(s_swi, 1280, 512, diagonal=False)])])
    (x1, h1), (g_swi,) = _mid_fwd(x0, y0, npost0, npre1, phases=[
        _Phase(like_swi, g_swi, [_gather_pass(768, 512), _gather_relay(1280, 512), _gather_send(s_swi, 1792, 256, diagonal=False)])])
    g_swi, = _carry([_Phase(like_swi, g_swi, [_gather_pass(1280, 512), _gather_relay(1792, 256)])], "relay_sgu_w_in")
    g_swi, = _carry([_Phase(like_swi, g_swi, [_gather_pass(1792, 256)])], "pass_sgu_w_in")
    proj1, (g_swo,) = _mm(h1, g_swi, "nn", F32, tm=1024, tn=IN_COLS_S, tk=D, name="sgu_in", b_blocked=True, phases=[
        _Phase(like_gwo, None, [_gather_send(s_swo, 0, W_ROWS)])])
    (pre1,), (g_swo,) = _sgu_fwd(proj1, lng, lnb, ws, bsb, phases=[_Phase(like_gwo, g_swo, [_gather_pass(0, W_ROWS)])])
    ws_out = g_swo.reshape(D, D)
    loss_cols, dx2, dy1, dnpost1 = _nn_rows(pre1, ws_out, name="sgu_out", row_ins=[x1, tgt], vec_ins=[npost1],
                                            outs=[("vec", F32), ("row", F32), ("row", BF16), ("vec", F32)], tail=_final_tail)
    loss = lax.psum(0.5 * jnp.sum(loss_cols) / D, ("x", "y", "c"))

    like_b_out, like_b_swi = _sds((3, W_ROWS, D), BF16), _sds((3, D, IN_COLS_S), BF16)
    like_b_gwi = _sds((3, D, IN_COLS_G), BF16)
    row_pair = dict(like=_sds((4, W_ROWS, D), BF16), block=lambda i, j: i, ordinal=lambda i, j: i >> 1,
                    dst=lambda ref, k, i, j: ref.at[k])
    col_pair = dict(like=_sds((4, D, IN_COLS_S), BF16), block=lambda i, j: j, ordinal=lambda i, j: 4 * i + (j >> 1),
                    dst=lambda ref, k, i, j: ref.at[k, pl.ds(pl.multiple_of(i * 1024, 1024), 1024)])

    mine = _own_table()
    dws_out, (a_swo,) = _mm(pre1, dy1, "tn", BF16, tm=W_ROWS, tn=D, tk=S, name="sgu_out_dw", pair=row_pair)
    p_swo = dws_out.reshape(N_DEV, W_ROWS, D)
    t_swo = _pair_sum(p_swo, a_swo, 0, W_ROWS, "pair_sum_sgu_w_out", table=mine)
    dpre1, _ = _mm(dy1, ws_out, "nt", F32, tm=1024, tn=1024, tk=D, name="sgu_out_dx")
    (dproj1, dlng, dlnb, dwsp, dbsp), (b_swo,) = _sgu_bwd(proj1, dpre1, lng, lnb, ws, wst, bsb, phases=[
        _Phase(like_b_out, None, [_reduce_cross(t_swo, 0, 0, W_ROWS)])])
    p_swi, (a_swi,) = _mm(h1, dproj1, "tn", BF16, tm=1024, tn=IN_COLS_S, tk=S, name="sgu_in_dw", out_blocked=True, pair=col_pair)
    t_swi = _pair_sum(p_swi, a_swi, 0, D, "pair_sum_sgu_w_in", table=mine)
    (dx1, dy0, dnpre1, dnpost0), (b_swi,) = _nt_rows(
        dproj1, g_swi, tk=IN_COLS_S, name="sgu_in_dx", b_blocked=True, row_ins=[dx2, x1, y0], vec_ins=[npre1, npost0],
        outs=[("row", F32), ("row", BF16), ("vec", F32), ("vec", F32)], tail=_mid_bwd_tail, phases=[
            _Phase(like_b_swi, None, [_reduce_cross(t_swi, 0, 0, 1024)])])
    dwg_out, (a_gwo,) = _mm(ypre0, dy0, "tn", BF16, tm=W_ROWS, tn=D, tk=S, name="gla_out_dw", pair=row_pair)
    p_gwo = dwg_out.reshape(N_DEV, W_ROWS, D)
    t_gwo = _pair_sum(p_gwo, a_gwo, 0, W_ROWS, "pair_sum_gla_w_out", table=mine)
    dypre0, _ = _mm(dy0, wg_out, "nt", F32, tm=1024, tn=1024, tk=D, name="gla_out_dx")
    late = [dnpre1, dnpost1, dlng, dlnb, dwsp, jnp.transpose(dbsp[:, :SGU_G])]
    late_pack = _pack(late)
    (dproj0, dogain, dbgate, dw2), (b_swi, b_gwo, g_late) = _gla_bwd(proj0, dypre0, states, w2p, gla_b_gate, gla_o_gain, phases=[
        _Phase(like_b_swi, b_swi, [_reduce_cross(t_swi, 1024, 1024, 1024)]),
        _Phase(like_b_out, None, [_reduce_cross(t_gwo, 0, 0, W_ROWS)]),
        _Phase(_sds((N_DEV,) + late_pack.shape, F32), None, [_gather_send(late_pack, 0, late_pack.shape[0])])])
    half = D // 2
    dwg_in_a, (g_late,) = _mm(h0, dproj0, "tn", BF16, tm=half, tn=896, tk=S, name="gla_in_dw_a", m_tiles=(0, 1), phases=[
        _Phase(_sds((N_DEV,) + late_pack.shape, F32), g_late, [_gather_pass(0, late_pack.shape[0])])])
    own_gwi, a_gwi = _blockify_pair(dwg_in_a, None, None, 0, "blockify_gla_w_in_a")
    t_gwi_a = _pair_sum(own_gwi, a_gwi, 0, half, "pair_sum_gla_w_in_a")
    dwg_in_b, (b_gwi,) = _mm(h0, dproj0, "tn", BF16, tm=half, tn=896, tk=S, name="gla_in_dw_b", m_tiles=(1, 1), phases=[
        _Phase(like_b_gwi, None, [_reduce_cross(t_gwi_a, 0, 0, 512)])])
    own_gwi, a_gwi = _blockify_pair(dwg_in_b, own_gwi, a_gwi, half, "blockify_gla_w_in_b")
    t_gwi_b = _pair_sum(own_gwi, a_gwi, half, half, "pair_sum_gla_w_in_b")
    (grad_x, dnpre0), (b_gwi,) = _nt_rows(
        dproj0, wg_in, tk=896, name="gla_in_dx", b_tiled=True, row_ins=[dx1, x0], vec_ins=[npre0],
        outs=[("row", F32), ("vec", F32)], tail=_first_bwd_tail, phases=[
            _Phase(like_b_gwi, b_gwi, [_reduce_cross(t_gwi_a, 512, 512, 512), _reduce_cross(t_gwi_b, 0, half, half)])])

    early = [dnpre0, dnpost0, dbgate, dogain, dw2[:LR]]
    early_pack = _pack(early)
    like_early = _sds((N_DEV,) + early_pack.shape, F32)
    (g_swo, d_swo, nm_swo, nv_swo), (g_early,) = _sum_adamw(
        p_swo, a_swo, b_swo, sgu_w_out[0], m_sgu_w_out[0], v_sgu_w_out[0], name="adamw_sgu_w_out", table=mine, phases=[
            _Phase(like_early, None, [_gather_send(early_pack, 0, early_pack.shape[0])])])
    (g_gwo_, d_gwo, nm_gwo, nv_gwo), (g_early,) = _sum_adamw(
        p_gwo, a_gwo, b_gwo, gla_w_out[0], m_gla_w_out[0], v_gla_w_out[0], name="adamw_gla_w_out", table=mine, phases=[
            _Phase(like_early, g_early, [_gather_pass(0, early_pack.shape[0])])])
    (g_swi_, d_swi, nm_swi, nv_swi), _ = _sum_adamw(
        p_swi, a_swi, b_swi, sgu_w_in[0], m_sgu_w_in[0], v_sgu_w_in[0], name="adamw_sgu_w_in", table=mine)
    (g_gwi_, d_gwi, nm_gwi, nv_gwi), _ = _sum_adamw(
        own_gwi, a_gwi, b_gwi, gla_w_in[0], m_gla_w_in[0], v_gla_w_in[0], name="adamw_gla_w_in")

    g_npre1, g_npost1, g_lng_full, g_lnb_full, g_wsp, g_bsp = _unpack(_sum_parts(g_late, "sum_late_small_grads"), late)
    g_npre0, g_npost0, g_bgate, g_ogain, g_w2_full = _unpack(_sum_parts(g_early, "sum_early_small_grads"), early)
    g_w2 = lax.dynamic_slice(g_w2_full, (0, me * 128), (LR, 128))
    g_lng = lax.dynamic_slice(g_lng_full, (0, me * 256), (1, 256))
    g_lnb = lax.dynamic_slice(g_lnb_full, (0, me * 256), (1, 256))
    small_g = [jnp.concatenate([g_npre0, g_npre1], 0), jnp.concatenate([g_npost0, g_npost1], 0), g_w2, g_bgate, g_ogain,
               g_lng, g_lnb, g_wsp, g_bsp]
    small_w = [norm_pre, norm_post, gla_w_gate2[0], gla_b_gate, gla_o_gain, sgu_ln_gain, sgu_ln_bias, sgu_w_spatial[0], sgu_b_spatial[0]]
    small_m = [m_norm_pre, m_norm_post, m_gla_w_gate2[0], m_gla_b_gate, m_gla_o_gain, m_sgu_ln_gain, m_sgu_ln_bias, m_sgu_w_spatial[0], m_sgu_b_spatial[0]]
    small_v = [v_norm_pre, v_norm_post, v_gla_w_gate2[0], v_gla_b_gate, v_gla_o_gain, v_sgu_ln_gain, v_sgu_ln_bias, v_sgu_w_spatial[0], v_sgu_b_spatial[0]]
    d_pack, nm_pack, nv_pack = _adamw(_pack(small_w), _pack(small_g), _pack(small_m), _pack(small_v), "adamw_small")

    out_like = [norm_pre, norm_post, gla_w_gate2, gla_b_gate, gla_o_gain, sgu_ln_gain, sgu_ln_bias, sgu_w_spatial, sgu_b_spatial]
    sg_ = [g.reshape(s.shape) for g, s in zip(small_g, out_like)]
    sd_, sm_, sv_ = (_unpack(pk, out_like) for pk in (d_pack, nm_pack, nv_pack))

    def assemble(small_list, w_in_g, w_out_g, w_in_s, w_out_s):
        npre_, npost_, w2_, bg_, og_, lg_, lb_, wsp_, bsp_ = small_list
        return [npre_, npost_, w_in_g[None], w2_, bg_, og_, w_out_g[None], w_in_s[None], lg_, lb_, wsp_, bsp_, w_out_s[None]]

    return (loss, grad_x.reshape(1, S, D),
            *assemble(sg_, g_gwi_, g_gwo_, g_swi_, g_swo),
            *assemble(sd_, d_gwi, d_gwo, d_swi, d_swo),
            *assemble(sm_, nm_gwi, nm_gwo, nm_swi, nm_swo),
            *assemble(sv_, nv_gwi, nv_gwo, nv_swi, nv_swo))
```

```python
import functools

import jax
import jax.numpy as jnp
from jax import lax
from jax.experimental import pallas as pl
from jax.experimental.pallas import tpu as pltpu

F32 = jnp.float32
BF16 = jnp.bfloat16

N_DEV = 8
S = 2048
D = 2048
H = 4
DK = 256
DV = 512
C = 64
NC = S // C
GLA_COLS = 6160
GLA_PAD = 6272
Q0, K0, V0, G0, LR0 = 0, 1024, 2048, 4096, 6144
LR = 16
LRP = 128
SGU_COLS = 6144
SGU_BLOCK = 128
SGU_G = 8
SGU_GD = 256
EPS = 1e-6
GLA_TAU = 16.0

ADAM_LR, ADAM_B1, ADAM_B2, ADAM_EPS, ADAM_WD, ADAM_STEP = 0.001, 0.9, 0.999, 1e-08, 0.01, 10

V7X_VMEM_BYTES = 64 * 1024 * 1024
VMEM_CEILING = V7X_VMEM_BYTES - 6 * 1024 * 1024
MESH = pl.DeviceIdType.MESH
HBM_SPEC = pl.BlockSpec(memory_space=pl.ANY)


def _sds(shape, dtype):
    return jax.ShapeDtypeStruct(tuple(shape), dtype)


def _nbytes(shape, dtype):
    n = 1
    for s in shape:
        n *= s
    return n * jnp.dtype(dtype).itemsize


def _dot(a, b, dims=(((1,), (0,)), ((), ())), precision=None):
    return lax.dot_general(a, b, dims, precision=precision, preferred_element_type=F32)


NN = (((1,), (0,)), ((), ()))
TN = (((0,), (0,)), ((), ()))
NT = (((1,), (1,)), ((), ()))


def _place():
    return lax.axis_index("x"), lax.axis_index("y"), lax.axis_index("c")


def _index_of(px, py, pc):
    return 4 * px + 2 * py + pc


def _chips(x, y):
    return [(1 - x, y), (x, 1 - y), (1 - x, 1 - y)]


def _rcopy(src, dst, send_sem, recv_sem, to):
    return pltpu.make_async_remote_copy(src_ref=src, dst_ref=dst, send_sem=send_sem, recv_sem=recv_sem,
                                        device_id=to, device_id_type=MESH)


class _Move:
    def __init__(self, ins, n_remote, make, stage=None):
        self.ins, self.n_remote, self.make, self.stage = list(ins), n_remote, make, stage

    def scratch(self):
        sems = [pltpu.SemaphoreType.DMA((self.n_remote,)), pltpu.SemaphoreType.DMA((self.n_remote,))]
        return sems if self.stage is None else sems + [pltpu.SemaphoreType.DMA((1,)), pltpu.VMEM(*self.stage)]

    def start(self, in_refs, buf, scratch):
        sends, _, local = self.make(in_refs, buf, scratch[0], scratch[1])
        if local is not None:
            pltpu.make_async_copy(local[0], scratch[3], scratch[2].at[0]).start()
        for cp in sends:
            cp.start()

    def finish(self, in_refs, buf, scratch):
        sends, arrivals, local = self.make(in_refs, buf, scratch[0], scratch[1])
        if local is not None:
            pltpu.make_async_copy(local[0], scratch[3], scratch[2].at[0]).wait()
            out = pltpu.make_async_copy(scratch[3], local[1], scratch[2].at[0])
            out.start()
        for cp in arrivals:
            cp.wait_recv()
        for cp in sends:
            cp.wait_send()
        if local is not None:
            out.wait()


class _Phase:
    def __init__(self, like, so_far, moves):
        self.like, self.so_far, self.moves = like, so_far, list(moves)


def _gather_send(shard, r0, nr, diagonal=True):
    def make(in_refs, g, ss, rs):
        sh, = in_refs
        x, y, c = _place()
        me = _index_of(x, y, c)
        rows = pl.ds(r0, nr)
        peers = [(x, y, 1 - c)] + [(px, py, c) for px, py in _chips(x, y)[:3 if diagonal else 2]]
        sends = [_rcopy(sh.at[rows], g.at[me, rows], ss.at[k], rs.at[k], p) for k, p in enumerate(peers)]
        arrivals = [_rcopy(sh.at[rows], g.at[_index_of(*p), rows], ss.at[k], rs.at[k], p) for k, p in enumerate(peers)]
        return sends, arrivals, (sh.at[rows], g.at[me, rows])

    return _Move([shard], 4 if diagonal else 3, make, stage=((nr, shard.shape[1]), shard.dtype))


def _gather_relay(r0, nr):
    def make(in_refs, g, ss, rs):
        x, y, c = _place()
        nx, ny, nd = [(px, py, c) for px, py in _chips(x, y)]
        first, second = pl.ds(r0, nr // 2), pl.ds(r0 + nr // 2, nr // 2)
        sends = [_rcopy(g.at[_index_of(*nx), first], g.at[_index_of(*nx), first], ss.at[0], rs.at[0], ny),
                 _rcopy(g.at[_index_of(*ny), second], g.at[_index_of(*ny), second], ss.at[1], rs.at[1], nx)]
        arrivals = [_rcopy(g.at[_index_of(*nx), first], g.at[_index_of(*nd), first], ss.at[0], rs.at[0], ny),
                    _rcopy(g.at[_index_of(*ny), second], g.at[_index_of(*nd), second], ss.at[1], rs.at[1], nx)]
        return sends, arrivals, None

    return _Move([], 2, make)


def _gather_pass(r0, nr):
    def make(in_refs, g, ss, rs):
        x, y, c = _place()
        rows = pl.ds(r0, nr)
        sends = [_rcopy(g.at[_index_of(px, py, c), rows], g.at[_index_of(px, py, c), rows], ss.at[j], rs.at[j], (x, y, 1 - c))
                 for j, (px, py) in enumerate(_chips(x, y))]
        arrivals = [_rcopy(g.at[_index_of(px, py, c), rows], g.at[_index_of(px, py, 1 - c), rows], ss.at[j], rs.at[j], (x, y, 1 - c))
                    for j, (px, py) in enumerate(_chips(x, y))]
        return sends, arrivals, None

    return _Move([], 3, make)


def _own_table():
    x, y, c = _place()
    return jnp.stack([_index_of(px, py, c) for px, py in [(x, y)] + _chips(x, y)]).astype(jnp.int32)


def _blockify_pair(dw, own_so_far, a_so_far, dst_r0, name):
    rows, tr, cw, win = dw.shape[0], 256, GLA_COLS // N_DEV, 896
    n_steps = rows // tr

    def body(*refs):
        x_ref, own_ref, a_ref, stage_ref, send_sems, recv_sem = refs[0], *refs[-5:]
        i = pl.program_id(0)
        x, y, c = _place()

        def send(slot, k):
            dst = a_ref.at[k, pl.ds(pl.multiple_of(dst_r0 + i * tr, tr), tr)]
            return _rcopy(stage_ref.at[slot], dst, send_sems.at[slot], recv_sem.at[0], (x, y, 1 - c))

        for j in range(N_DEV):
            window = x_ref[:, 768 * j:768 * j + win].astype(F32)
            tile = (pltpu.roll(window, win - 2 * j, 1) if j else window)[:, :cw].astype(BF16)
            k = ((j >> 2) ^ x) + 2 * (((j >> 1) & 1) ^ y)

            @pl.when((j & 1) == c)
            def _():
                own_ref[k] = tile

            @pl.when((j & 1) != c)
            def _():
                slot = (j >> 1) & 1
                if j >> 1 >= 2:
                    send(slot, k).wait_send()
                else:
                    pl.when(i > 0)(lambda: send(slot, k).wait_send())
                stage_ref[slot] = tile
                send(slot, k).start()

        @pl.when(i == n_steps - 1)
        def _():
            send(0, 0).wait_send()
            send(1, 0).wait_send()
            arrived = a_ref.at[:, pl.ds(dst_r0, rows)]
            _rcopy(arrived, arrived, send_sems.at[0], recv_sem.at[0], (x, y, 1 - c)).wait_recv()

    continues = a_so_far is not None
    own, a = pl.pallas_call(
        body, grid=(n_steps,),
        in_specs=[pl.BlockSpec((tr, GLA_PAD), lambda i: (i, 0))] + [HBM_SPEC] * (2 * continues),
        out_specs=[pl.BlockSpec((4, tr, cw), lambda i: (0, dst_r0 // tr + i, 0)), HBM_SPEC],
        out_shape=[_sds((4, D, cw), BF16), _sds((4, D, cw), BF16)],
        scratch_shapes=[pltpu.VMEM((2, tr, cw), BF16), pltpu.SemaphoreType.DMA((2,)), pltpu.SemaphoreType.DMA((1,))],
        input_output_aliases={1: 0, 2: 1} if continues else {},
        compiler_params=pltpu.CompilerParams(dimension_semantics=("arbitrary",), vmem_limit_bytes=48 * 1024 * 1024),
        name=name,
    )(*([dw] + [own_so_far, a_so_far] * continues))
    return own, a


def _reduce_cross(sums, src_r0, dst_r0, nr):
    def make(in_refs, b, ss, rs):
        t, = in_refs
        x, y, c = _place()
        src, dst = pl.ds(src_r0, nr), pl.ds(dst_r0, nr)
        sends = [_rcopy(t.at[j, src], b.at[j, dst], ss.at[j], rs.at[j], (px, py, c)) for j, (px, py) in enumerate(_chips(x, y))]
        return sends, sends, None

    return _Move([sums], 3, make)


def _hosted(body, *, name, grid, in_specs, out_specs, out_shape, args, scratch_shapes=(), block_bytes, scratch_bytes=0,
            phases=(), table=None):
    n_in, n_out, n_scr = len(args), len(out_shape), len(scratch_shapes)
    all_args, all_out_shape, sems, aliases, layout = list(args), list(out_shape), [], {}, []
    for j, ph in enumerate(phases):
        counts = []
        for mv in ph.moves:
            all_args += mv.ins
            counts.append(len(mv.ins))
            sems += mv.scratch()
        if ph.so_far is not None:
            aliases[len(all_args)] = n_out + j
            all_args.append(ph.so_far)
        layout.append((counts, ph.so_far is not None))
        all_out_shape.append(ph.like)
    n_extra_in = len(all_args) - n_in

    def wrapped(*refs):
        ins, pos = refs[:n_in], n_in
        move_ins = []
        for counts, continues in layout:
            per_move = []
            for cnt in counts:
                per_move.append(refs[pos:pos + cnt])
                pos += cnt
            pos += continues
            move_ins.append(per_move)
        outs = refs[pos:pos + n_out]
        bufs = refs[pos + n_out:pos + n_out + len(phases)]
        pos += n_out + len(phases)
        scratch = refs[pos:pos + n_scr]
        pos += n_scr
        move_sems = []
        for ph in phases:
            per_move = []
            for mv in ph.moves:
                count = len(mv.scratch())
                per_move.append(refs[pos:pos + count])
                pos += count
            move_sems.append(per_move)

        def each_move(fn_name):
            for ph, buf, per_in, per_sem in zip(phases, bufs, move_ins, move_sems):
                for mv, mv_in, mv_sem in zip(ph.moves, per_in, per_sem):
                    getattr(mv, fn_name)(mv_in, buf, mv_sem)

        if phases:
            first = functools.reduce(jnp.logical_and, [pl.program_id(a) == 0 for a in range(len(grid))])
            last = functools.reduce(jnp.logical_and, [pl.program_id(a) == grid[a] - 1 for a in range(len(grid))])
            pl.when(first)(lambda: each_move("start"))
        body(*ins, *outs, *scratch)
        if phases:
            pl.when(last)(lambda: each_move("finish"))

    all_args = [pltpu.with_memory_space_constraint(a, pltpu.HBM) for a in all_args]
    est = 2 * block_bytes + scratch_bytes
    params = pltpu.CompilerParams(dimension_semantics=("arbitrary",) * len(grid),
                                  vmem_limit_bytes=min(VMEM_CEILING, max(32 * 1024 * 1024, 2 * est)))
    all_in_specs, all_out_specs = list(in_specs) + [HBM_SPEC] * n_extra_in, list(out_specs) + [HBM_SPEC] * len(phases)
    if table is None:
        results = pl.pallas_call(
            wrapped, grid=grid, in_specs=all_in_specs, out_specs=all_out_specs, out_shape=all_out_shape,
            scratch_shapes=list(scratch_shapes) + sems, input_output_aliases=aliases, compiler_params=params, name=name,
        )(*all_args)
    else:
        results = pl.pallas_call(
            lambda table_ref, *refs: wrapped(*refs),
            grid_spec=pltpu.PrefetchScalarGridSpec(num_scalar_prefetch=1, grid=grid, in_specs=all_in_specs, out_specs=all_out_specs,
                                                   scratch_shapes=list(scratch_shapes) + sems),
            out_shape=all_out_shape, input_output_aliases={k + 1: v for k, v in aliases.items()}, compiler_params=params, name=name,
        )(table, *all_args)
    return list(results[:n_out]), list(results[n_out:])


class _Both:
    def __init__(self, copies):
        self.copies = copies

    def start(self):
        for cp in self.copies:
            cp.start()

    def wait_send(self):
        for cp in self.copies:
            cp.wait_send()

    def wait_recv(self):
        for cp in self.copies:
            cp.wait_recv()


def _carry(phases, name):
    def body(o_ref):
        o_ref[...] = jnp.zeros_like(o_ref)

    _, bufs = _hosted(body, name=name, grid=(1,), in_specs=[], out_specs=[pl.BlockSpec((8, 128), lambda i: (0, 0))],
                      out_shape=[_sds((8, 128), F32)], args=[], block_bytes=8 * 128 * 4, phases=phases)
    return bufs


def _gather_first(shard, small, name):
    cw, tr, n_tiles = shard.shape[1], 256, GLA_PAD // 128

    def body(sh_ref, sm_ref, wn_ref, g_ref, gs_ref, wt_ref, win_ref, tmp_ref, send_sems, recv_sems, local_sems):
        x, y, c = _place()
        me, sibling = (x, y, c), (x, y, 1 - c)
        chips = _chips(x, y)

        def copy(base, out_ref, k, block, to, src=None):
            dst = out_ref.at[_index_of(*block)]
            return _rcopy(dst if src is None else src, dst, send_sems.at[base + k], recv_sems.at[base + k], to)

        icopy = functools.partial(copy, 0, g_ref)
        scopy = functools.partial(copy, 8, gs_ref)

        def wcopy(k, block, to, src=None):
            if k in (1, 2):
                return icopy(k, block, to, src)
            halves = []
            for part, sem in enumerate((k, {0: 15, 4: 16, 5: 17, 6: 18}[k])):
                rows = pl.ds(part * (D // 2), D // 2)
                dst = g_ref.at[_index_of(*block), rows]
                halves.append(_rcopy(dst if src is None else src.at[rows], dst, send_sems.at[sem], recv_sems.at[sem], to))
            return _Both(halves)

        def relay(k, block, half, to):
            rows = pl.ds(half * (D // 2), D // 2)
            ref = g_ref.at[_index_of(*block), rows]
            return _rcopy(ref, ref, send_sems.at[k], recv_sems.at[k], to)

        def load(src_ref, slot):
            cp = pltpu.make_async_copy(src_ref, win_ref.at[slot], local_sems.at[0])
            cp.start()
            cp.wait()

        def place(slot, block):
            b = _index_of(*block)

            def rows_chunk(r, carry):
                rows = pl.ds(pl.multiple_of(r * tr, tr), tr)
                tmp_ref[:, :cw] = win_ref[slot, rows, :].astype(F32)
                shifted = pltpu.roll(tmp_ref[...], 2 * b, 1)
                for u in range(7):
                    wt_ref[6 * b + u, rows, :] = (wt_ref[6 * b + u, rows, :].astype(F32) + shifted[:, 128 * u:128 * (u + 1)]).astype(BF16)
                return carry

            lax.fori_loop(0, D // tr, rows_chunk, 0)

        small_own = pltpu.make_async_copy(sm_ref, gs_ref.at[_index_of(*me)], local_sems.at[1])
        small_own.start()
        first = [wcopy(1 + j, me, (*chip, c), src=sh_ref) for j, chip in enumerate(chips[:2])]
        first += [scopy(0, me, sibling, src=sm_ref)] + [scopy(1 + j, me, (*chip, c), src=sm_ref) for j, chip in enumerate(chips)]
        for cp in first:
            cp.start()

        def clear(t, carry):
            wt_ref[t] = jnp.zeros((D, 128), BF16)
            return carry

        lax.fori_loop(0, n_tiles, clear, 0)
        tmp_ref[...] = jnp.zeros_like(tmp_ref)

        def emit(t):
            pltpu.make_async_copy(wt_ref.at[t], wn_ref.at[t], local_sems.at[2]).start()

        def take(block, slot, arrivals=None, pass_on=None):
            for cp in arrivals or ():
                cp.wait_recv()
            load(sh_ref if arrivals is None else g_ref.at[_index_of(*block)], slot)
            if pass_on is not None:
                pass_on.start()
            place(slot, block)
            for u in range(1, 6):
                emit(6 * _index_of(*block) + u)

        near_x, near_y, far = [(*chip, c) for chip in chips]
        to_sibling = wcopy(0, me, sibling, src=win_ref.at[0])
        pass_x = wcopy(4, near_x, sibling, src=win_ref.at[1])
        pass_y = wcopy(5, near_y, sibling, src=win_ref.at[0])
        pass_d = wcopy(6, far, sibling, src=win_ref.at[0])
        relays = [relay(3, near_x, 0, near_y), relay(7, near_y, 1, near_x)]
        take(me, 0, pass_on=to_sibling)
        take(near_x, 1, [wcopy(1, near_x, me)], pass_x)
        relays[0].start()
        to_sibling.wait_send()
        take(near_y, 0, [wcopy(2, near_y, me)], pass_y)
        relays[1].start()
        small_passed = []
        for j, chip in enumerate(chips):
            scopy(1 + j, (*chip, c), me).wait_recv()
            cp = scopy(4 + j, (*chip, c), sibling)
            cp.start()
            small_passed.append(cp)
        pass_x.wait_send()
        take(sibling, 1, [wcopy(0, sibling, me)])
        pass_y.wait_send()
        take((*chips[0], 1 - c), 0, [wcopy(4, (*chips[0], 1 - c), me)])
        take((*chips[1], 1 - c), 1, [wcopy(5, (*chips[1], 1 - c), me)])
        take(far, 0, [relay(3, far, 0, near_y), relay(7, far, 1, near_x)], pass_d)
        take((*chips[2], 1 - c), 1, [wcopy(6, (*chips[2], 1 - c), me)])
        for t in range(0, n_tiles, 6):
            emit(t)
        scopy(0, sibling, me).wait_recv()
        for j, chip in enumerate(chips):
            scopy(4 + j, (*chip, 1 - c), me).wait_recv()
        for cp in first + small_passed + relays + [pass_d]:
            cp.wait_send()
        small_own.wait()
        pltpu.make_async_copy(wn_ref, wn_ref, local_sems.at[2]).wait()

    wn, _, gs = pl.pallas_call(
        body,
        in_specs=[HBM_SPEC] * 2, out_specs=[HBM_SPEC] * 3,
        out_shape=[_sds((n_tiles, D, 128), BF16), _sds((N_DEV,) + shard.shape, BF16), _sds((N_DEV,) + small.shape, small.dtype)],
        scratch_shapes=[pltpu.VMEM((n_tiles, D, 128), BF16), pltpu.VMEM((2, D, cw), BF16), pltpu.VMEM((tr, 7 * 128), F32),
                        pltpu.SemaphoreType.DMA((19,)), pltpu.SemaphoreType.DMA((19,)), pltpu.SemaphoreType.DMA((3,))],
        compiler_params=pltpu.CompilerParams(vmem_limit_bytes=48 * 1024 * 1024),
        name=name,
    )(shard, small)
    return wn, gs


def _mm(a, b, mode, out_dtype, *, tm, tn, tk, name, b_blocked=False, b_tiled=False, out_blocked=False, m_tiles=None, pair=None,
        phases=()):
    if mode == "nn":
        (m, k), dims = a.shape, NN
        a_blk, a_map = (tm, tk), (lambda i, j, kk: (i, kk))
        if b_blocked:
            assert b.shape[1] == k and b.shape[2] == tn and tk == k
            n = b.shape[0] * tn
            b_spec = pl.BlockSpec((None, tk, tn), lambda i, j, kk: (j, kk, 0))
        elif b_tiled:
            assert b.shape[1] == k and b.shape[2] == 128 and tn % 128 == 0
            n = b.shape[0] * 128
            b_spec = pl.BlockSpec((tn // 128, tk, 128), lambda i, j, kk: (j, kk, 0))
        else:
            assert b.shape[0] == k
            n = b.shape[1]
            b_spec = pl.BlockSpec((tk, tn), lambda i, j, kk: (kk, j))
    elif mode == "tn":
        (k, m), n, dims = a.shape, b.shape[1], TN
        assert b.shape[0] == k
        first = 0 if m_tiles is None else m_tiles[0]
        a_blk, a_map = (tk, tm), (lambda i, j, kk: (kk, i + first))
        b_spec = pl.BlockSpec((tk, tn), lambda i, j, kk: (kk, j))
    else:
        (m, k), dims = a.shape, NT
        a_blk, a_map = (tm, tk), (lambda i, j, kk: (i, kk))
        if b_blocked:
            assert b.shape[0] * b.shape[2] == k and b.shape[2] == tk
            n = b.shape[1]
            b_spec = pl.BlockSpec((None, tn, tk), lambda i, j, kk: (kk, j, 0))
        elif b_tiled:
            assert b.shape[0] * 128 == k and b.shape[2] == 128 and tk % 128 == 0
            n = b.shape[1]
            b_spec = pl.BlockSpec((tk // 128, tn, 128), lambda i, j, kk: (kk, j, 0))
        else:
            assert b.shape[1] == k
            n = b.shape[0]
            b_spec = pl.BlockSpec((tn, tk), lambda i, j, kk: (j, kk))
    assert m % tm == 0 and n % tn == 0 and k % tk == 0, (a.shape, b.shape, mode)
    nk = k // tk
    n_row_tiles = m // tm if m_tiles is None else m_tiles[1]
    if out_blocked:
        out_shape, out_spec = _sds((n // tn, n_row_tiles * tm, tn), out_dtype), pl.BlockSpec((None, tm, tn), lambda i, j, kk: (j, i, 0))
    else:
        out_shape, out_spec = _sds((n_row_tiles * tm, n), out_dtype), pl.BlockSpec((tm, tn), lambda i, j, kk: (i, j))

    grid = (n_row_tiles, n // tn, nk)

    def body(a_ref, b_ref, o_ref, *rest):
        rhs = jnp.concatenate([b_ref[u] for u in range(b_ref.shape[0])], axis=1) if b_tiled else b_ref[...]
        p = _dot(a_ref[...], rhs, dims)
        if nk == 1:
            o_ref[...] = p.astype(out_dtype)
            if pair is not None:
                _send_to_sibling(p.astype(out_dtype), *rest)
        else:
            acc_ref, = rest
            kk = pl.program_id(2)

            @pl.when(kk == 0)
            def _():
                acc_ref[...] = p

            @pl.when(kk > 0)
            def _():
                acc_ref[...] += p

            @pl.when(kk == nk - 1)
            def _():
                o_ref[...] = acc_ref[...].astype(out_dtype)

    def _send_to_sibling(tile, pair_ref, stage_ref, send_sems, recv_sem):
        i, j = pl.program_id(0), pl.program_id(1)
        x, y, c = _place()
        blk = pair["block"](i, j)
        k = ((blk >> 2) ^ x) + 2 * (((blk >> 1) & 1) ^ y)
        ordinal = pair["ordinal"](i, j)

        def send(slot):
            return _rcopy(stage_ref.at[slot], pair["dst"](pair_ref, k, i, j), send_sems.at[slot], recv_sem.at[0], (x, y, 1 - c))

        @pl.when((blk & 1) != c)
        def _():
            slot = ordinal & 1

            @pl.when(ordinal >= 2)
            def _():
                send(slot).wait_send()

            stage_ref[slot] = tile
            send(slot).start()

        @pl.when((i == grid[0] - 1) & (j == grid[1] - 1))
        def _():
            send(0).wait_send()
            send(1).wait_send()
            _rcopy(pair_ref, pair_ref, send_sems.at[0], recv_sem.at[0], (x, y, 1 - c)).wait_recv()

    blocks = _nbytes(a_blk, a.dtype) + tk * tn * jnp.dtype(b.dtype).itemsize + _nbytes((tm, tn), out_dtype)
    out_specs, out_shapes, scratch = [out_spec], [out_shape], [] if nk == 1 else [pltpu.VMEM((tm, tn), F32)]
    scratch_bytes = _nbytes((tm, tn), F32) * (nk > 1)
    if pair is not None:
        assert nk == 1
        out_specs, out_shapes = out_specs + [HBM_SPEC], out_shapes + [pair["like"]]
        scratch = [pltpu.VMEM((2, tm, tn), out_dtype), pltpu.SemaphoreType.DMA((2,)), pltpu.SemaphoreType.DMA((1,))]
        scratch_bytes = 2 * _nbytes((tm, tn), out_dtype)
    outs, bufs = _hosted(
        body, name=name, grid=grid,
        in_specs=[pl.BlockSpec(a_blk, a_map), b_spec], out_specs=out_specs, out_shape=out_shapes, args=[a, b],
        scratch_shapes=scratch, block_bytes=blocks, scratch_bytes=scratch_bytes, phases=phases)
    return outs[0], outs[1:] + bufs


NT_ROWS_TM = 512
NT_ROWS_SUB = 128


def _nt_rows(a, b, *, tk, name, row_ins, vec_ins, outs, tail, b_blocked=False, b_tiled=False, phases=()):
    m, k = a.shape
    tm, nk = NT_ROWS_TM, k // tk
    if b_blocked:
        assert b.shape[0] * b.shape[2] == k and b.shape[2] == tk and b.shape[1] == D
        b_spec = pl.BlockSpec((None, D, tk), lambda i, kk: (kk, 0, 0))
    else:
        assert b_tiled and b.shape[0] * 128 == k and tk % 128 == 0 and b.shape[1] == D
        b_spec = pl.BlockSpec((tk // 128, D, 128), lambda i, kk: (kk, 0, 0))
    row_spec, vec_spec = pl.BlockSpec((tm, D), lambda i, kk: (i, 0)), pl.BlockSpec((1, D), lambda i, kk: (0, 0))
    n_row, n_vec, n_out = len(row_ins), len(vec_ins), len(outs)
    assert nk >= 2

    def body(a_ref, b_ref, *rest):
        row_hbm, vec_refs = rest[:n_row], rest[n_row:n_row + n_vec]
        out_refs = rest[n_row + n_vec:n_row + n_vec + n_out]
        acc_ref, row_sems = rest[n_row + n_vec + n_out], rest[-1]
        row_refs = rest[n_row + n_vec + n_out + 1:-1]
        rhs = jnp.concatenate([b_ref[u] for u in range(b_ref.shape[0])], axis=1) if b_tiled else b_ref[...]
        p = _dot(a_ref[...], rhs, NT)
        i, kk = pl.program_id(0), pl.program_id(1)

        def fetch(r):
            return pltpu.make_async_copy(row_hbm[r].at[pl.ds(pl.multiple_of(i * tm, tm), tm)], row_refs[r], row_sems.at[r])

        @pl.when(kk == 0)
        def _():
            for r in range(n_row):
                fetch(r).start()
            acc_ref[...] = p

        @pl.when(kk > 0)
        def _():
            acc_ref[...] += p

        @pl.when(kk == nk - 1)
        def _():
            for r in range(n_row):
                fetch(r).wait()
            for s in range(tm // NT_ROWS_SUB):
                rows = slice(s * NT_ROWS_SUB, (s + 1) * NT_ROWS_SUB)
                tail(acc_ref[rows, :], rows, (i == 0) if s == 0 else None, row_refs, vec_refs, out_refs)

    out_specs = [row_spec if kind == "row" else vec_spec for kind, _ in outs]
    out_shape = [_sds((m, D) if kind == "row" else (1, D), dt) for kind, dt in outs]
    blocks = tm * tk * 2 + D * tk * 2 + sum(tm * D * jnp.dtype(dt).itemsize for kind, dt in outs if kind == "row")
    scratch = [pltpu.VMEM((tm, D), F32)] + [pltpu.VMEM((tm, D), x.dtype) for x in row_ins] + [pltpu.SemaphoreType.DMA((n_row,))]
    return _hosted(body, name=name, grid=(m // tm, nk), in_specs=[pl.BlockSpec((tm, tk), lambda i, kk: (i, kk)), b_spec]
                   + [HBM_SPEC] * n_row + [vec_spec] * n_vec, out_specs=out_specs, out_shape=out_shape,
                   args=[a, b] + list(row_ins) + list(vec_ins), scratch_shapes=scratch,
                   block_bytes=blocks, scratch_bytes=(1 + n_row) * tm * D * 4, phases=phases)


def _nn_rows(a, b, *, name, row_ins, vec_ins, outs, tail):
    m, k = a.shape
    tm = NT_ROWS_TM
    assert b.shape == (k, D)
    row_spec, vec_spec = pl.BlockSpec((tm, D), lambda i: (i, 0)), pl.BlockSpec((1, D), lambda i: (0, 0))
    n_row, n_vec, n_out = len(row_ins), len(vec_ins), len(outs)

    def body(a_ref, b_ref, *rest):
        row_hbm, vec_refs = rest[:n_row], rest[n_row:n_row + n_vec]
        out_refs = rest[n_row + n_vec:n_row + n_vec + n_out]
        d_ref, row_sems = rest[n_row + n_vec + n_out], rest[-1]
        row_refs = rest[n_row + n_vec + n_out + 1:-1]
        i = pl.program_id(0)
        fetches = [pltpu.make_async_copy(row_hbm[r].at[pl.ds(pl.multiple_of(i * tm, tm), tm)], row_refs[r], row_sems.at[r])
                   for r in range(n_row)]
        for cp in fetches:
            cp.start()
        d_ref[...] = _dot(a_ref[...], b_ref[...])
        for cp in fetches:
            cp.wait()
        for s in range(tm // NT_ROWS_SUB):
            rows = slice(s * NT_ROWS_SUB, (s + 1) * NT_ROWS_SUB)
            tail(d_ref[rows, :], rows, (i == 0) if s == 0 else None, row_refs, vec_refs, out_refs)

    out_specs = [row_spec if kind == "row" else vec_spec for kind, _ in outs]
    out_shape = [_sds((m, D) if kind == "row" else (1, D), dt) for kind, dt in outs]
    blocks = tm * k * 2 + k * D * 2 + sum(tm * D * jnp.dtype(dt).itemsize for kind, dt in outs if kind == "row")
    scratch = [pltpu.VMEM((tm, D), F32)] + [pltpu.VMEM((tm, D), x.dtype) for x in row_ins] + [pltpu.SemaphoreType.DMA((n_row,))]
    outs_, _ = _hosted(body, name=name, grid=(m // tm,), in_specs=[pl.BlockSpec((tm, k), lambda i: (i, 0)),
                                                                 pl.BlockSpec((k, D), lambda i: (0, 0))]
                       + [HBM_SPEC] * n_row + [vec_spec] * n_vec, out_specs=out_specs, out_shape=out_shape,
                       args=[a, b] + list(row_ins) + list(vec_ins), scratch_shapes=scratch,
                       block_bytes=blocks, scratch_bytes=(1 + n_row) * tm * D * 4)
    return outs_


def _vec_add(ref, value, first):
    if first is None:
        ref[...] += value
    else:
        pl.when(first)(lambda: ref.__setitem__(Ellipsis, value))
        pl.when(jnp.logical_not(first))(lambda: ref.__setitem__(Ellipsis, ref[...] + value))


RB = 256


def _row_spec(width):
    return pl.BlockSpec((RB, width), lambda i: (i, 0))


def _vec_spec(width):
    return pl.BlockSpec((1, width), lambda i: (0, 0))


def _rinv(x):
    return lax.rsqrt(jnp.mean(x * x, axis=-1, keepdims=True) + EPS)


def _norm_bwd(dyn, xhat, r):
    return r * (dyn - xhat * jnp.mean(dyn * xhat, axis=-1, keepdims=True))


def _colsum(x):
    return jnp.sum(x, axis=0, keepdims=True)


def _prenorm(x, gain):
    def body(x_ref, g_ref, h_ref):
        xv = x_ref[...]
        h_ref[...] = (xv * _rinv(xv) * g_ref[...]).astype(BF16)

    outs, _ = _hosted(body, name="prenorm", grid=(S // RB,), in_specs=[_row_spec(D), _vec_spec(D)], out_specs=[_row_spec(D)],
                      out_shape=[_sds((S, D), BF16)], args=[x, gain], block_bytes=RB * D * 6)
    return outs[0]


def _mid_fwd(x, y, npost, npre, phases=()):
    def body(x_ref, y_ref, po_ref, pr_ref, x1_ref, h1_ref):
        yv = y_ref[...]
        x1 = x_ref[...] + yv * _rinv(yv) * po_ref[...]
        x1_ref[...] = x1
        h1_ref[...] = (x1 * _rinv(x1) * pr_ref[...]).astype(BF16)

    return _hosted(body, name="mid_fwd", grid=(S // RB,), in_specs=[_row_spec(D), _row_spec(D), _vec_spec(D), _vec_spec(D)],
                   out_specs=[_row_spec(D), _row_spec(D)], out_shape=[_sds((S, D), F32), _sds((S, D), BF16)],
                   args=[x, y, npost, npre], block_bytes=RB * D * 14, phases=phases)


def _final_tail(yv, rows, first, row_refs, vec_refs, out_refs):
    (x_ref, t_ref), (po_ref,), (loss_ref, dx_ref, dy_ref, dpo_ref) = row_refs, vec_refs, out_refs
    r = _rinv(yv)
    yhat = yv * r
    err = x_ref[rows, :] + yhat * po_ref[...] - t_ref[rows, :]
    dx = err * (1.0 / D)
    dx_ref[rows, :] = dx
    dy_ref[rows, :] = _norm_bwd(dx * po_ref[...], yhat, r).astype(BF16)
    _vec_add(loss_ref, _colsum(err * err), first)
    _vec_add(dpo_ref, _colsum(dx * yhat), first)


def _mid_bwd_tail(dh, rows, first, row_refs, vec_refs, out_refs):
    (dx2_ref, x_ref, y_ref), (pr_ref, po_ref), (dx1_ref, dy_ref, dpr_ref, dpo_ref) = row_refs, vec_refs, out_refs
    xv = x_ref[rows, :]
    r = _rinv(xv)
    xhat = xv * r
    dx1 = dx2_ref[rows, :] + _norm_bwd(dh * pr_ref[...], xhat, r)
    dx1_ref[rows, :] = dx1
    yv = y_ref[rows, :]
    ry = _rinv(yv)
    yhat = yv * ry
    dy_ref[rows, :] = _norm_bwd(dx1 * po_ref[...], yhat, ry).astype(BF16)
    _vec_add(dpr_ref, _colsum(dh * xhat), first)
    _vec_add(dpo_ref, _colsum(dx1 * yhat), first)


def _first_bwd_tail(dh, rows, first, row_refs, vec_refs, out_refs):
    (dx1_ref, x_ref), (pr_ref,), (gx_ref, dpr_ref) = row_refs, vec_refs, out_refs
    xv = x_ref[rows, :]
    r = _rinv(xv)
    xhat = xv * r
    gx_ref[rows, :] = dx1_ref[rows, :] + _norm_bwd(dh * pr_ref[...], xhat, r)
    _vec_add(dpr_ref, _colsum(dh * xhat), first)


GLA_RB = 256
GLA_CPB = GLA_RB // C


def _sigmoid(x):
    return 1.0 / (1.0 + jnp.exp(-x))


def _tri(strict):
    r = lax.broadcasted_iota(jnp.int32, (C, C), 0)
    c = lax.broadcasted_iota(jnp.int32, (C, C), 1)
    return jnp.where(c < r if strict else c <= r, 1.0, 0.0).astype(BF16)


def _tri_dot(tri, x):
    hi = x.astype(BF16)
    lo = (x - hi.astype(F32)).astype(BF16)
    return _dot(tri, hi) + _dot(tri, lo)


def _gla_gates(glr_b, w2, b, tri):
    z = _dot(glr_b, w2) + b
    log_a = (jnp.minimum(z, 0.0) - jnp.log(1.0 + jnp.exp(-jnp.abs(z)))) * (1.0 / GLA_TAU)
    bcum = _tri_dot(tri, log_a)
    b_end = jnp.sum(log_a, axis=0, keepdims=True)
    return z, jnp.exp(b_end - bcum), jnp.exp(b_end)


def _gla_fwd(proj, w2p, bgate, ogain, phases=()):
    def body(p_ref, w2_ref, b_ref, og_ref, y_ref, st_out_ref, st_ref):
        @pl.when(pl.program_id(0) == 0)
        def _():
            st_ref[...] = jnp.zeros_like(st_ref)

        tri = _tri(False)

        def chunk(ci, carry):
            rows = pl.ds(pl.multiple_of(ci * C, C), C)
            glr_b = p_ref[rows, LR0:LR0 + LRP].astype(BF16)
            _, ea_all, dec_all = _gla_gates(glr_b, w2_ref[...], b_ref[...], tri)
            for h in range(H):
                ea, dec = ea_all[:, h * DK:(h + 1) * DK], dec_all[:, h * DK:(h + 1) * DK]
                k_dec = (p_ref[rows, K0 + h * DK:K0 + (h + 1) * DK] * ea).astype(BF16)
                v_b = p_ref[rows, V0 + h * DV:V0 + (h + 1) * DV].astype(BF16)
                st = st_ref[h] * dec + _dot(v_b, k_dec, TN)
                st_ref[h] = st
                st_b = st.astype(BF16)
                st_out_ref[ci, h] = st_b
                q_b = (p_ref[rows, Q0 + h * DK:Q0 + (h + 1) * DK] * (DK ** -0.5)).astype(BF16)
                o = _dot(q_b, st_b, NT)
                on = o * _rinv(o)
                g = p_ref[rows, G0 + h * DV:G0 + (h + 1) * DV]
                y_ref[rows, h * DV:(h + 1) * DV] = (on * og_ref[:, h * DV:(h + 1) * DV] * (g * _sigmoid(g))).astype(BF16)
            return carry

        lax.fori_loop(0, GLA_CPB, chunk, 0, unroll=True)

    blocks = GLA_RB * GLA_PAD * 4 + GLA_RB * D * 2 + GLA_CPB * H * DV * DK * 2
    return _hosted(
        body, name="gla_fwd", grid=(S // GLA_RB,),
        in_specs=[pl.BlockSpec((GLA_RB, GLA_PAD), lambda i: (i, 0)),
                  pl.BlockSpec((LRP, H * DK), lambda i: (0, 0)),
                  pl.BlockSpec((1, H * DK), lambda i: (0, 0)),
                  pl.BlockSpec((1, H * DV), lambda i: (0, 0))],
        out_specs=[pl.BlockSpec((GLA_RB, H * DV), lambda i: (i, 0)),
                   pl.BlockSpec((GLA_CPB, H, DV, DK), lambda i: (i, 0, 0, 0))],
        out_shape=[_sds((S, H * DV), BF16), _sds((NC, H, DV, DK), BF16)],
        args=[proj, w2p, bgate, ogain], scratch_shapes=[pltpu.VMEM((H, DV, DK), F32)],
        block_bytes=blocks, scratch_bytes=H * DV * DK * 4, phases=phases)


def _gla_bwd(proj, dypre, states, w2p, bgate, ogain, phases=()):
    nb = S // GLA_RB

    def body(p_ref, dy_ref, st_blk_ref, st_prev_ref, w2_ref, b_ref, og_ref,
             dp_ref, dog_ref, dbg_ref, dw2_ref, r_ref):
        step = pl.program_id(0)

        @pl.when(step == 0)
        def _():
            r_ref[...] = jnp.zeros_like(r_ref)
            dog_ref[...] = jnp.zeros_like(dog_ref)
            dbg_ref[...] = jnp.zeros_like(dbg_ref)
            dw2_ref[...] = jnp.zeros_like(dw2_ref)

        tri = _tri(False)
        tri_strict = _tri(True)
        has_prev = jnp.where(step < nb - 1, 1.0, 0.0).astype(F32)

        def chunk(ci, st_prev_of):
            rows = pl.ds(ci * C if isinstance(ci, int) else pl.multiple_of(ci * C, C), C)
            glr_b = p_ref[rows, LR0:LR0 + LRP].astype(BF16)
            z, ea_all, dec_all = _gla_gates(glr_b, w2_ref[...], b_ref[...], tri)
            d_a, d_end = [], []
            for h in range(H):
                kcol = slice(h * DK, (h + 1) * DK)
                vcol = slice(h * DV, (h + 1) * DV)
                ea, dec = ea_all[:, kcol], dec_all[:, kcol]
                k_dec = p_ref[rows, K0 + h * DK:K0 + (h + 1) * DK] * ea
                k_dec_b = k_dec.astype(BF16)
                v_b = p_ref[rows, V0 + h * DV:V0 + (h + 1) * DV].astype(BF16)
                q_b = (p_ref[rows, Q0 + h * DK:Q0 + (h + 1) * DK] * (DK ** -0.5)).astype(BF16)
                st_b = st_blk_ref[ci, h]
                o = _dot(q_b, st_b, NT)
                rinv = _rinv(o)
                on = o * rinv
                g = p_ref[rows, G0 + h * DV:G0 + (h + 1) * DV]
                sg = _sigmoid(g)
                og = og_ref[:, vcol]
                dyp = dy_ref[rows, vcol]
                dp_ref[rows, G0 + h * DV:G0 + (h + 1) * DV] = (dyp * (on * og) * (sg * (1.0 + g * (1.0 - sg)))).astype(BF16)
                dpn = dyp * (g * sg)
                dog_ref[:, vcol] += _colsum(dpn * on)
                do_b = _norm_bwd(dpn * og, on, rinv).astype(BF16)
                gt = _dot(do_b, q_b, TN) + r_ref[h]
                gt_b = gt.astype(BF16)
                dp_ref[rows, Q0 + h * DK:Q0 + (h + 1) * DK] = (_dot(do_b, st_b) * (DK ** -0.5)).astype(BF16)
                dkd = _dot(v_b, gt_b)
                dp_ref[rows, V0 + h * DV:V0 + (h + 1) * DV] = _dot(k_dec_b, gt_b, NT).astype(BF16)
                dp_ref[rows, K0 + h * DK:K0 + (h + 1) * DK] = (dkd * ea).astype(BF16)
                d_a.append(dkd * k_dec)
                d_end.append(_colsum(gt * st_prev_of(h)) * dec)
                r_ref[h] = gt * dec
            dla = _tri_dot(tri_strict, jnp.concatenate(d_a, axis=1)) + jnp.concatenate(d_end, axis=1)
            dz = dla * (1.0 / GLA_TAU) * (1.0 - _sigmoid(z))
            dz_b = dz.astype(BF16)
            dbg_ref[...] += _colsum(dz)
            dw2_ref[...] += _dot(glr_b, dz_b, TN)
            dp_ref[rows, LR0:LR0 + LRP] = _dot(dz_b, w2_ref[...], NT).astype(BF16)

        def later_chunk(t, carry):
            ci = GLA_CPB - 1 - t
            chunk(ci, lambda h: st_blk_ref[ci - 1, h].astype(F32))
            return carry

        lax.fori_loop(0, GLA_CPB - 1, later_chunk, 0, unroll=True)
        chunk(0, lambda h: st_prev_ref[0, h].astype(F32) * has_prev)

    blocks = (GLA_RB * GLA_PAD * 4 + GLA_RB * D * 4 + (GLA_CPB + 1) * H * DV * DK * 2 + GLA_RB * GLA_PAD * 2)
    rev = lambda i: nb - 1 - i
    return _hosted(
        body, name="gla_bwd", grid=(nb,),
        in_specs=[pl.BlockSpec((GLA_RB, GLA_PAD), lambda i: (rev(i), 0)),
                  pl.BlockSpec((GLA_RB, H * DV), lambda i: (rev(i), 0)),
                  pl.BlockSpec((GLA_CPB, H, DV, DK), lambda i: (rev(i), 0, 0, 0)),
                  pl.BlockSpec((1, H, DV, DK), lambda i: (jnp.maximum(rev(i) * GLA_CPB - 1, 0), 0, 0, 0)),
                  pl.BlockSpec((LRP, H * DK), lambda i: (0, 0)),
                  pl.BlockSpec((1, H * DK), lambda i: (0, 0)),
                  pl.BlockSpec((1, H * DV), lambda i: (0, 0))],
        out_specs=[pl.BlockSpec((GLA_RB, GLA_PAD), lambda i: (rev(i), 0)),
                   pl.BlockSpec((1, H * DV), lambda i: (0, 0)),
                   pl.BlockSpec((1, H * DK), lambda i: (0, 0)),
                   pl.BlockSpec((LRP, H * DK), lambda i: (0, 0))],
        out_shape=[_sds((S, GLA_PAD), BF16), _sds((1, H * DV), F32), _sds((1, H * DK), F32), _sds((LRP, H * DK), F32)],
        args=[proj, dypre, states, states, w2p, bgate, ogain], scratch_shapes=[pltpu.VMEM((H, DV, DK), F32)],
        block_bytes=blocks, scratch_bytes=H * DV * DK * 4, phases=phases)


SGU_RB = 256
GELU_C = 0.7978845608028654
GELU_A = 0.044715


def _gelu(x):
    return 0.5 * x * (1.0 + jnp.tanh(GELU_C * (x + GELU_A * x * x * x)))


def _gelu_grad(x):
    t = jnp.tanh(GELU_C * (x + GELU_A * x * x * x))
    return 0.5 * (1.0 + t) + 0.5 * x * (1.0 - t * t) * (GELU_C * (1.0 + 3.0 * GELU_A * x * x))


def _causal_mask(transposed=False):
    i = lax.broadcasted_iota(jnp.int32, (SGU_BLOCK, SGU_BLOCK), 1 if transposed else 0)
    j = lax.broadcasted_iota(jnp.int32, (SGU_BLOCK, SGU_BLOCK), 0 if transposed else 1)
    return (i >= C) | (j < C)


def _layer_norm(vf, gain, bias):
    mu = jnp.mean(vf, axis=-1, keepdims=True)
    cen = vf - mu
    rstd = lax.rsqrt(jnp.mean(cen * cen, axis=-1, keepdims=True) + EPS)
    xhat = cen * rstd
    return xhat, rstd, xhat * gain + bias


def _sgu_fwd(proj, lng, lnb, ws, bsb, phases=()):
    def body(p_ref, g_ref, b_ref, ws_ref, bs_ref, o_ref):
        mask = _causal_mask()
        for n in range(SGU_RB // SGU_BLOCK):
            rows = slice(n * SGU_BLOCK, (n + 1) * SGU_BLOCK)
            _, _, vn = _layer_norm(_gelu(p_ref[rows, D:2 * D]), g_ref[...], b_ref[...])
            vn_b = vn.astype(BF16)
            for gi in range(SGU_G):
                cols = slice(gi * SGU_GD, (gi + 1) * SGU_GD)
                w = jnp.where(mask, ws_ref[gi], 0.0).astype(BF16)
                vs = _dot(w, vn_b[:, cols]) + bs_ref[gi]
                gate = p_ref[rows, 2 * D + gi * SGU_GD:2 * D + (gi + 1) * SGU_GD]
                o_ref[rows, cols] = (_gelu(p_ref[rows, cols]) * vs * (gate * _sigmoid(gate))).astype(BF16)

    blocks = SGU_RB * SGU_COLS * 4 + SGU_RB * D * 2 + SGU_G * SGU_BLOCK * (SGU_BLOCK + SGU_GD) * 4
    return _hosted(
        body, name="sgu_fwd", grid=(S // SGU_RB,),
        in_specs=[pl.BlockSpec((SGU_RB, SGU_COLS), lambda i: (i, 0)),
                  pl.BlockSpec((1, D), lambda i: (0, 0)), pl.BlockSpec((1, D), lambda i: (0, 0)),
                  pl.BlockSpec((SGU_G, SGU_BLOCK, SGU_BLOCK), lambda i: (0, 0, 0)),
                  pl.BlockSpec((SGU_G, SGU_BLOCK, SGU_GD), lambda i: (0, 0, 0))],
        out_specs=[pl.BlockSpec((SGU_RB, D), lambda i: (i, 0))], out_shape=[_sds((S, D), BF16)],
        args=[proj, lng, lnb, ws, bsb], block_bytes=blocks, phases=phases)


def _sgu_bwd(proj, dpre, lng, lnb, ws, wst, bsb, phases=()):
    nsteps = S // SGU_RB

    def body(p_ref, d_ref, g_ref, b_ref, ws_ref, wst_ref, bs_ref,
             dp_ref, dg_ref, db_ref, dws_ref, dbs_ref, dvn_ref, dvs_acc_ref):
        step = pl.program_id(0)

        @pl.when(step == 0)
        def _():
            dg_ref[...] = jnp.zeros_like(dg_ref)
            db_ref[...] = jnp.zeros_like(db_ref)
            dws_ref[...] = jnp.zeros_like(dws_ref)
            dvs_acc_ref[...] = jnp.zeros_like(dvs_acc_ref)

        mask = _causal_mask()
        maskt = _causal_mask(transposed=True)
        for n in range(SGU_RB // SGU_BLOCK):
            rows = slice(n * SGU_BLOCK, (n + 1) * SGU_BLOCK)
            v = p_ref[rows, D:2 * D]
            xhat, rstd, vn = _layer_norm(_gelu(v), g_ref[...], b_ref[...])
            vn_b = vn.astype(BF16)
            for gi in range(SGU_G):
                cols = slice(gi * SGU_GD, (gi + 1) * SGU_GD)
                w = jnp.where(mask, ws_ref[gi], 0.0).astype(BF16)
                wt = jnp.where(maskt, wst_ref[gi], 0.0).astype(BF16)
                vs = _dot(w, vn_b[:, cols]) + bs_ref[gi]
                u = p_ref[rows, cols]
                gate = p_ref[rows, 2 * D + gi * SGU_GD:2 * D + (gi + 1) * SGU_GD]
                sg = _sigmoid(gate)
                gu = _gelu(u)
                dpre_g = d_ref[rows, cols]
                t = dpre_g * (gate * sg)
                dp_ref[rows, cols] = (t * vs * _gelu_grad(u)).astype(BF16)
                dp_ref[rows, 2 * D + gi * SGU_GD:2 * D + (gi + 1) * SGU_GD] = (
                    dpre_g * gu * vs * (sg * (1.0 + gate * (1.0 - sg)))).astype(BF16)
                dvs = t * gu
                dvs_b = dvs.astype(BF16)
                dvs_acc_ref[:, cols] += dvs
                dws_ref[gi] += _dot(dvs_b, vn_b[:, cols], NT)
                dvn_ref[:, cols] = _dot(wt, dvs_b)
            dvn = dvn_ref[...]
            dg_ref[...] += _colsum(dvn * xhat)
            db_ref[...] += _colsum(dvn)
            dxh = dvn * g_ref[...]
            dvf = rstd * (dxh - jnp.mean(dxh, axis=-1, keepdims=True) - xhat * jnp.mean(dxh * xhat, axis=-1, keepdims=True))
            dp_ref[rows, D:2 * D] = (dvf * _gelu_grad(v)).astype(BF16)

        @pl.when(step == nsteps - 1)
        def _():
            lane = lax.broadcasted_iota(jnp.int32, (SGU_BLOCK, SGU_BLOCK), 1)
            out = jnp.zeros((SGU_BLOCK, SGU_BLOCK), F32)
            for gi in range(SGU_G):
                out = out + jnp.where(lane == gi, jnp.sum(dvs_acc_ref[:, gi * SGU_GD:(gi + 1) * SGU_GD], axis=1, keepdims=True), 0.0)
                dws_ref[gi] = jnp.where(mask, dws_ref[gi], 0.0)
            dbs_ref[...] = out

    blocks = SGU_RB * SGU_COLS * 6 + SGU_RB * D * 4 + SGU_G * SGU_BLOCK * (3 * SGU_BLOCK + SGU_GD) * 4
    const3 = lambda i: (0, 0, 0)
    return _hosted(
        body, name="sgu_bwd", grid=(nsteps,),
        in_specs=[pl.BlockSpec((SGU_RB, SGU_COLS), lambda i: (i, 0)),
                  pl.BlockSpec((SGU_RB, D), lambda i: (i, 0)),
                  pl.BlockSpec((1, D), lambda i: (0, 0)), pl.BlockSpec((1, D), lambda i: (0, 0)),
                  pl.BlockSpec((SGU_G, SGU_BLOCK, SGU_BLOCK), const3),
                  pl.BlockSpec((SGU_G, SGU_BLOCK, SGU_BLOCK), const3),
                  pl.BlockSpec((SGU_G, SGU_BLOCK, SGU_GD), const3)],
        out_specs=[pl.BlockSpec((SGU_RB, SGU_COLS), lambda i: (i, 0)),
                   pl.BlockSpec((1, D), lambda i: (0, 0)), pl.BlockSpec((1, D), lambda i: (0, 0)),
                   pl.BlockSpec((SGU_G, SGU_BLOCK, SGU_BLOCK), const3),
                   pl.BlockSpec((SGU_BLOCK, SGU_BLOCK), lambda i: (0, 0))],
        out_shape=[_sds((S, SGU_COLS), BF16), _sds((1, D), F32), _sds((1, D), F32),
                   _sds((SGU_G, SGU_BLOCK, SGU_BLOCK), F32), _sds((SGU_BLOCK, SGU_BLOCK), F32)],
        args=[proj, dpre, lng, lnb, ws, wst, bsb],
        scratch_shapes=[pltpu.VMEM((SGU_BLOCK, D), F32), pltpu.VMEM((SGU_BLOCK, D), F32)],
        block_bytes=blocks, scratch_bytes=2 * SGU_BLOCK * D * 4, phases=phases)


def _pair_sum(own, a, r0, nr, name, table=None):
    c = own.shape[2]
    tr = 256
    assert r0 % tr == 0 and nr % tr == 0

    def body(own_ref, sib_ref, o_ref):
        o_ref[...] = (own_ref[...].astype(F32) + sib_ref[...].astype(F32)).astype(BF16)

    own_map = ((lambda j, i: (1 + j, r0 // tr + i, 0)) if table is None else
               (lambda j, i, t: (t[1 + j], r0 // tr + i, 0)))
    cpad = -(-c // 128) * 128
    outs, _ = _hosted(
        body, name=name, grid=(3, nr // tr),
        in_specs=[pl.BlockSpec((None, tr, c), own_map),
                  pl.BlockSpec((None, tr, c), lambda j, i, *t: (1 + j, r0 // tr + i, 0))],
        out_specs=[pl.BlockSpec((None, tr, c), lambda j, i, *t: (j, i, 0))], out_shape=[_sds((3, nr, c), BF16)],
        args=[own, a], block_bytes=3 * tr * cpad * 2, table=table)
    return outs[0]


def _adamw_math(w, g, m, v):
    m = ADAM_B1 * m + (1.0 - ADAM_B1) * g
    v = ADAM_B2 * v + (1.0 - ADAM_B2) * (g * g)
    m_hat = m / (1.0 - ADAM_B1 ** ADAM_STEP)
    v_hat = v / (1.0 - ADAM_B2 ** ADAM_STEP)
    delta = -ADAM_LR * (m_hat / (jnp.sqrt(v_hat) + ADAM_EPS) + ADAM_WD * w)
    return delta, m, v


def _sum_adamw(own, a, b, w, m, v, *, name, phases=(), table=None):
    r, c = w.shape
    tr = 256

    def body(own_ref, sib_ref, far_ref, w_ref, m_ref, v_ref, g_ref, d_ref, nm_ref, nv_ref):
        g = own_ref[...].astype(F32) + sib_ref[...].astype(F32)
        for j in range(3):
            g = g + far_ref[j].astype(F32)
        g_ref[...] = g
        d_ref[...], nm_ref[...], nv_ref[...] = _adamw_math(w_ref[...], g, m_ref[...], v_ref[...])

    spec = pl.BlockSpec((tr, c), lambda i, *t: (i, 0))
    own_map = (lambda i: (0, i, 0)) if table is None else (lambda i, t: (t[0], i, 0))
    cpad = -(-c // 128) * 128
    return _hosted(
        body, name=name, grid=(r // tr,),
        in_specs=[pl.BlockSpec((None, tr, c), own_map), pl.BlockSpec((None, tr, c), lambda i, *t: (0, i, 0)),
                  pl.BlockSpec((3, tr, c), lambda i, *t: (0, i, 0)), spec, spec, spec],
        out_specs=[spec] * 4, out_shape=[_sds((r, c), F32)] * 4, args=[own, a, b, w, m, v],
        block_bytes=5 * tr * cpad * 2 + 7 * tr * cpad * 4, phases=phases, table=table)


def _sum_parts(parts, name):
    n, r, c = parts.shape

    def body(p_ref, o_ref):
        g = p_ref[0]
        for j in range(1, n):
            g = g + p_ref[j]
        o_ref[...] = g

    outs, _ = _hosted(body, name=name, grid=(1,), in_specs=[pl.BlockSpec((n, r, c), lambda i: (0, 0, 0))],
                      out_specs=[pl.BlockSpec((r, c), lambda i: (0, 0))], out_shape=[_sds((r, c), F32)], args=[parts],
                      block_bytes=(n + 1) * r * c * 4)
    return outs[0]


def _adamw(w, g, m, v, name):
    def body(w_ref, g_ref, m_ref, v_ref, d_ref, nm_ref, nv_ref):
        d_ref[...], nm_ref[...], nv_ref[...] = _adamw_math(w_ref[...], g_ref[...], m_ref[...], v_ref[...])

    spec = pl.BlockSpec(w.shape, lambda i: (0, 0))
    outs, _ = _hosted(body, name=name, grid=(1,), in_specs=[spec] * 4, out_specs=[spec] * 3, out_shape=[_sds(w.shape, F32)] * 3,
                      args=[w, g, m, v], block_bytes=7 * _nbytes(w.shape, F32))
    return outs


def _blocks_to_columns(g):
    n, r, c = g.shape
    return jnp.transpose(g, (1, 0, 2)).reshape(r, n * c)


def _pack(parts):
    return jnp.concatenate([p.reshape(-1) for p in parts]).reshape(-1, 128)


def _unpack(packed, like):
    flat, outs, off = packed.reshape(-1), [], 0
    for p in like:
        outs.append(flat[off:off + p.size].reshape(p.shape))
        off += p.size
    return outs


def kernel(x, norm_pre, norm_post, gla_w_in, gla_w_gate2, gla_b_gate, gla_o_gain, gla_w_out, sgu_w_in, sgu_ln_gain, sgu_ln_bias, sgu_w_spatial, sgu_b_spatial, sgu_w_out, loss_target, m_norm_pre, m_norm_post, m_gla_w_in, m_gla_w_gate2, m_gla_b_gate, m_gla_o_gain, m_gla_w_out, m_sgu_w_in, m_sgu_ln_gain, m_sgu_ln_bias, m_sgu_w_spatial, m_sgu_b_spatial, m_sgu_w_out, v_norm_pre, v_norm_post, v_gla_w_in, v_gla_w_gate2, v_gla_b_gate, v_gla_o_gain, v_gla_w_out, v_sgu_w_in, v_sgu_ln_gain, v_sgu_ln_bias, v_sgu_w_spatial, v_sgu_b_spatial, v_sgu_w_out):
    me = _index_of(*_place())
    x0 = x.reshape(S, D)
    tgt = loss_target.reshape(S, D)
    npre0, npre1 = norm_pre[0:1], norm_pre[1:2]
    npost0, npost1 = norm_post[0:1], norm_post[1:2]
    ws = sgu_w_spatial[0]
    wst = jnp.transpose(ws, (0, 2, 1))
    bsb = jnp.broadcast_to(sgu_b_spatial[0][:, :, None], (SGU_G, SGU_BLOCK, SGU_GD))
    W_ROWS = D // N_DEV
    IN_COLS_G, IN_COLS_S = GLA_COLS // N_DEV, SGU_COLS // N_DEV

    s_gwi, s_gwo = gla_w_in[0].astype(BF16), gla_w_out[0].astype(BF16)
    s_swi, s_swo = sgu_w_in[0].astype(BF16), sgu_w_out[0].astype(BF16)
    small = jnp.concatenate([jnp.pad(gla_w_gate2[0].reshape(4, 512), ((0, 4), (0, 0))),
                             jnp.pad(jnp.concatenate([sgu_ln_gain, sgu_ln_bias], axis=1), ((0, 7), (0, 0)))], axis=0)

    wg_in, g_small = _gather_first(s_gwi, small, "gather_first")
    w2 =_blocks_to_columns(g_small[:, :4, :].reshape(N_DEV, LR, 128))
    w2p = jnp.pad(w2, ((0, LRP - LR), (0, 0))).astype(BF16)
    lng = g_small[:, 8, :256].reshape(1, D)
    lnb = g_small[:, 8, 256:].reshape(1, D)
    like_gwo, like_swi = _sds((N_DEV, W_ROWS, D), BF16), _sds((N_DEV, D, IN_COLS_S), BF16)

    h0 = _prenorm(x0, npre0)
    proj0, (g_gwo, g_swi) = _mm(h0, wg_in, "nn", F32, tm=1024, tn=896, tk=D, name="gla_in", b_tiled=True, phases=[
        _Phase(like_gwo, None, [_gather_send(s_gwo, 0, W_ROWS)]),
        _Phase(like_swi, None, [_gather_send(s_swi, 0, 768, diagonal=False)])])
    (ypre0, states), (g_gwo, g_swi) = _gla_fwd(proj0, w2p, gla_b_gate, gla_o_gain, phases=[
        _Phase(like_gwo, g_gwo, [_gather_pass(0, W_ROWS)]),
        _Phase(like_swi, g_swi, [_gather_relay(0, 768), _gather_send(s_swi, 768, 512, diagonal=False)])])
    wg_out = g_gwo.reshape(D, D)
    y0, (g_swi,) = _mm(ypre0, wg_out, "nn", F32, tm=1024, tn=1024, tk=D, name="gla_out", phases=[
        _Phase(like_swi, g_swi, [_gather_pass(0, 768), _gather_relay(768, 512), _gather_send(s_swi, 1280, 512, diagonal=False)])])
    (x1, h1), (g_swi,) = _mid_fwd(x0, y0, npost0, npre1, phases=[
        _Phase(like_swi, g_swi, [_gather_pass(768, 512), _gather_relay(1280, 512), _gather_send(s_swi, 1792, 256, diagonal=False)])])
    g_swi, = _carry([_Phase(like_swi, g_swi, [_gather_pass(1280, 512), _gather_relay(1792, 256)])], "relay_sgu_w_in")
    g_swi, = _carry([_Phase(like_swi, g_swi, [_gather_pass(1792, 256)])], "pass_sgu_w_in")
    proj1, (g_swo,) = _mm(h1, g_swi, "nn", F32, tm=1024, tn=IN_COLS_S, tk=D, name="sgu_in", b_blocked=True, phases=[
        _Phase(like_gwo, None, [_gather_send(s_swo, 0, W_ROWS)])])
    (pre1,), (g_swo,) = _sgu_fwd(proj1, lng, lnb, ws, bsb, phases=[_Phase(like_gwo, g_swo, [_gather_pass(0, W_ROWS)])])
    ws_out = g_swo.reshape(D, D)
    loss_cols, dx2, dy1, dnpost1 = _nn_rows(pre1, ws_out, name="sgu_out", row_ins=[x1, tgt], vec_ins=[npost1],
                                            outs=[("vec", F32), ("row", F32), ("row", BF16), ("vec", F32)], tail=_final_tail)
    loss_here = jnp.pad((0.5 * jnp.sum(loss_cols) / D).reshape(1, 1), ((0, 7), (0, 127)))

    like_b_out, like_b_swi = _sds((3, W_ROWS, D), BF16), _sds((3, D, IN_COLS_S), BF16)
    like_b_gwi = _sds((3, D, IN_COLS_G), BF16)
    row_pair = dict(like=_sds((4, W_ROWS, D), BF16), block=lambda i, j: i, ordinal=lambda i, j: i >> 1,
                    dst=lambda ref, k, i, j: ref.at[k])
    col_pair = dict(like=_sds((4, D, IN_COLS_S), BF16), block=lambda i, j: j, ordinal=lambda i, j: 4 * i + (j >> 1),
                    dst=lambda ref, k, i, j: ref.at[k, pl.ds(pl.multiple_of(i * 1024, 1024), 1024)])

    mine = _own_table()
    dws_out, (a_swo,) = _mm(pre1, dy1, "tn", BF16, tm=W_ROWS, tn=D, tk=S, name="sgu_out_dw", pair=row_pair)
    p_swo = dws_out.reshape(N_DEV, W_ROWS, D)
    t_swo = _pair_sum(p_swo, a_swo, 0, W_ROWS, "pair_sum_sgu_w_out", table=mine)
    dpre1, _ = _mm(dy1, ws_out, "nt", F32, tm=1024, tn=1024, tk=D, name="sgu_out_dx")
    (dproj1, dlng, dlnb, dwsp, dbsp), (b_swo,) = _sgu_bwd(proj1, dpre1, lng, lnb, ws, wst, bsb, phases=[
        _Phase(like_b_out, None, [_reduce_cross(t_swo, 0, 0, W_ROWS)])])
    p_swi, (a_swi,) = _mm(h1, dproj1, "tn", BF16, tm=1024, tn=IN_COLS_S, tk=S, name="sgu_in_dw", out_blocked=True, pair=col_pair)
    t_swi = _pair_sum(p_swi, a_swi, 0, D, "pair_sum_sgu_w_in", table=mine)
    (dx1, dy0, dnpre1, dnpost0), (b_swi,) = _nt_rows(
        dproj1, g_swi, tk=IN_COLS_S, name="sgu_in_dx", b_blocked=True, row_ins=[dx2, x1, y0], vec_ins=[npre1, npost0],
        outs=[("row", F32), ("row", BF16), ("vec", F32), ("vec", F32)], tail=_mid_bwd_tail, phases=[
            _Phase(like_b_swi, None, [_reduce_cross(t_swi, 0, 0, 1280)])])
    dwg_out, (a_gwo,) = _mm(ypre0, dy0, "tn", BF16, tm=W_ROWS, tn=D, tk=S, name="gla_out_dw", pair=row_pair)
    p_gwo = dwg_out.reshape(N_DEV, W_ROWS, D)
    t_gwo = _pair_sum(p_gwo, a_gwo, 0, W_ROWS, "pair_sum_gla_w_out", table=mine)
    dypre0, _ = _mm(dy0, wg_out, "nt", F32, tm=1024, tn=1024, tk=D, name="gla_out_dx")
    late = [dnpre1, dnpost1, dlng, dlnb, dwsp, jnp.transpose(dbsp[:, :SGU_G])]
    late_pack = _pack(late)
    (dproj0, dogain, dbgate, dw2), (b_swi, b_gwo, g_late) = _gla_bwd(proj0, dypre0, states, w2p, gla_b_gate, gla_o_gain, phases=[
        _Phase(like_b_swi, b_swi, [_reduce_cross(t_swi, 1280, 1280, 768)]),
        _Phase(like_b_out, None, [_reduce_cross(t_gwo, 0, 0, W_ROWS)]),
        _Phase(_sds((N_DEV,) + late_pack.shape, F32), None, [_gather_send(late_pack, 0, late_pack.shape[0])])])
    half = D // 2
    dwg_in_a, (g_late,) = _mm(h0, dproj0, "tn", BF16, tm=half, tn=896, tk=S, name="gla_in_dw_a", m_tiles=(0, 1), phases=[
        _Phase(_sds((N_DEV,) + late_pack.shape, F32), g_late, [_gather_pass(0, late_pack.shape[0])])])
    own_gwi, a_gwi = _blockify_pair(dwg_in_a, None, None, 0, "blockify_gla_w_in_a")
    t_gwi_a = _pair_sum(own_gwi, a_gwi, 0, half, "pair_sum_gla_w_in_a")
    dwg_in_b, (b_gwi,) = _mm(h0, dproj0, "tn", BF16, tm=half, tn=896, tk=S, name="gla_in_dw_b", m_tiles=(1, 1), phases=[
        _Phase(like_b_gwi, None, [_reduce_cross(t_gwi_a, 0, 0, 512)])])
    own_gwi, a_gwi = _blockify_pair(dwg_in_b, own_gwi, a_gwi, half, "blockify_gla_w_in_b")
    t_gwi_b = _pair_sum(own_gwi, a_gwi, half, half, "pair_sum_gla_w_in_b")
    (grad_x, dnpre0), (b_gwi,) = _nt_rows(
        dproj0, wg_in, tk=896, name="gla_in_dx", b_tiled=True, row_ins=[dx1, x0], vec_ins=[npre0],
        outs=[("row", F32), ("vec", F32)], tail=_first_bwd_tail, phases=[
            _Phase(like_b_gwi, b_gwi, [_reduce_cross(t_gwi_a, 512, 512, 512), _reduce_cross(t_gwi_b, 0, half, half)])])

    early = [dnpre0, dnpost0, dbgate, dogain, dw2[:LR], loss_here]
    early_pack = _pack(early)
    like_early = _sds((N_DEV,) + early_pack.shape, F32)
    (g_swo, d_swo, nm_swo, nv_swo), (g_early,) = _sum_adamw(
        p_swo, a_swo, b_swo, sgu_w_out[0], m_sgu_w_out[0], v_sgu_w_out[0], name="adamw_sgu_w_out", table=mine, phases=[
            _Phase(like_early, None, [_gather_send(early_pack, 0, early_pack.shape[0])])])
    (g_gwo_, d_gwo, nm_gwo, nv_gwo), (g_early,) = _sum_adamw(
        p_gwo, a_gwo, b_gwo, gla_w_out[0], m_gla_w_out[0], v_gla_w_out[0], name="adamw_gla_w_out", table=mine, phases=[
            _Phase(like_early, g_early, [_gather_pass(0, early_pack.shape[0])])])
    (g_swi_, d_swi, nm_swi, nv_swi), _ = _sum_adamw(
        p_swi, a_swi, b_swi, sgu_w_in[0], m_sgu_w_in[0], v_sgu_w_in[0], name="adamw_sgu_w_in", table=mine)
    (g_gwi_, d_gwi, nm_gwi, nv_gwi), _ = _sum_adamw(
        own_gwi, a_gwi, b_gwi, gla_w_in[0], m_gla_w_in[0], v_gla_w_in[0], name="adamw_gla_w_in")

    g_npre1, g_npost1, g_lng_full, g_lnb_full, g_wsp, g_bsp = _unpack(_sum_parts(g_late, "sum_late_small_grads"), late)
    g_npre0, g_npost0, g_bgate, g_ogain, g_w2_full, loss_all = _unpack(_sum_parts(g_early, "sum_early_small_grads"), early)
    loss = loss_all[0, 0]
    g_w2 = lax.dynamic_slice(g_w2_full, (0, me * 128), (LR, 128))
    g_lng = lax.dynamic_slice(g_lng_full, (0, me * 256), (1, 256))
    g_lnb = lax.dynamic_slice(g_lnb_full, (0, me * 256), (1, 256))
    small_g = [jnp.concatenate([g_npre0, g_npre1], 0), jnp.concatenate([g_npost0, g_npost1], 0), g_w2, g_bgate, g_ogain,
               g_lng, g_lnb, g_wsp, g_bsp]
    small_w = [norm_pre, norm_post, gla_w_gate2[0], gla_b_gate, gla_o_gain, sgu_ln_gain, sgu_ln_bias, sgu_w_spatial[0], sgu_b_spatial[0]]
    small_m = [m_norm_pre, m_norm_post, m_gla_w_gate2[0], m_gla_b_gate, m_gla_o_gain, m_sgu_ln_gain, m_sgu_ln_bias, m_sgu_w_spatial[0], m_sgu_b_spatial[0]]
    small_v = [v_norm_pre, v_norm_post, v_gla_w_gate2[0], v_gla_b_gate, v_gla_o_gain, v_sgu_ln_gain, v_sgu_ln_bias, v_sgu_w_spatial[0], v_sgu_b_spatial[0]]
    d_pack, nm_pack, nv_pack = _adamw(_pack(small_w), _pack(small_g), _pack(small_m), _pack(small_v), "adamw_small")

    out_like = [norm_pre, norm_post, gla_w_gate2, gla_b_gate, gla_o_gain, sgu_ln_gain, sgu_ln_bias, sgu_w_spatial, sgu_b_spatial]
    sg_ = [g.reshape(s.shape) for g, s in zip(small_g, out_like)]
    sd_, sm_, sv_ = (_unpack(pk, out_like) for pk in (d_pack, nm_pack, nv_pack))

    def assemble(small_list, w_in_g, w_out_g, w_in_s, w_out_s):
        npre_, npost_, w2_, bg_, og_, lg_, lb_, wsp_, bsp_ = small_list
        return [npre_, npost_, w_in_g[None], w2_, bg_, og_, w_out_g[None], w_in_s[None], lg_, lb_, wsp_, bsp_, w_out_s[None]]

    return (loss, grad_x.reshape(1, S, D),
            *assemble(sg_, g_gwi_, g_gwo_, g_swi_, g_swo),
            *assemble(sd_, d_gwi, d_gwo, d_swi, d_swo),
            *assemble(sm_, nm_gwi, nm_gwo, nm_swi, nm_swo),
            *assemble(sv_, nv_gwi, nv_gwo, nv_swi, nv_swo))
```

```python
import functools

import jax
import jax.numpy as jnp
from jax import lax
from jax.experimental import pallas as pl
from jax.experimental.pallas import tpu as pltpu

F32 = jnp.float32
BF16 = jnp.bfloat16

N_DEV = 8
S = 2048
D = 2048
H = 4
DK = 256
DV = 512
C = 64
NC = S // C
GLA_COLS = 6160
GLA_PAD = 6272
Q0, K0, V0, G0, LR0 = 0, 1024, 2048, 4096, 6144
LR = 16
LRP = 128
SGU_COLS = 6144
SGU_BLOCK = 128
SGU_G = 8
SGU_GD = 256
EPS = 1e-6
GLA_TAU = 16.0

ADAM_LR, ADAM_B1, ADAM_B2, ADAM_EPS, ADAM_WD, ADAM_STEP = 0.001, 0.9, 0.999, 1e-08, 0.01, 10

V7X_VMEM_BYTES = 64 * 1024 * 1024
VMEM_CEILING = V7X_VMEM_BYTES - 6 * 1024 * 1024
MESH = pl.DeviceIdType.MESH
HBM_SPEC = pl.BlockSpec(memory_space=pl.ANY)


def _sds(shape, dtype):
    return jax.ShapeDtypeStruct(tuple(shape), dtype)


def _nbytes(shape, dtype):
    n = 1
    for s in shape:
        n *= s
    return n * jnp.dtype(dtype).itemsize


def _dot(a, b, dims=(((1,), (0,)), ((), ())), precision=None):
    return lax.dot_general(a, b, dims, precision=precision, preferred_element_type=F32)


NN = (((1,), (0,)), ((), ()))
TN = (((0,), (0,)), ((), ()))
NT = (((1,), (1,)), ((), ()))


def _place():
    return lax.axis_index("x"), lax.axis_index("y"), lax.axis_index("c")


def _index_of(px, py, pc):
    return 4 * px + 2 * py + pc


def _chips(x, y):
    return [(1 - x, y), (x, 1 - y), (1 - x, 1 - y)]


def _rcopy(src, dst, send_sem, recv_sem, to):
    return pltpu.make_async_remote_copy(src_ref=src, dst_ref=dst, send_sem=send_sem, recv_sem=recv_sem,
                                        device_id=to, device_id_type=MESH)


class _Move:
    def __init__(self, ins, n_remote, make, stage=None):
        self.ins, self.n_remote, self.make, self.stage = list(ins), n_remote, make, stage

    def scratch(self):
        sems = [pltpu.SemaphoreType.DMA((self.n_remote,)), pltpu.SemaphoreType.DMA((self.n_remote,))]
        return sems if self.stage is None else sems + [pltpu.SemaphoreType.DMA((1,)), pltpu.VMEM(*self.stage)]

    def start(self, in_refs, buf, scratch):
        sends, _, local = self.make(in_refs, buf, scratch[0], scratch[1])
        if local is not None:
            pltpu.make_async_copy(local[0], scratch[3], scratch[2].at[0]).start()
        for cp in sends:
            cp.start()

    def finish(self, in_refs, buf, scratch):
        sends, arrivals, local = self.make(in_refs, buf, scratch[0], scratch[1])
        if local is not None:
            pltpu.make_async_copy(local[0], scratch[3], scratch[2].at[0]).wait()
            out = pltpu.make_async_copy(scratch[3], local[1], scratch[2].at[0])
            out.start()
        for cp in arrivals:
            cp.wait_recv()
        for cp in sends:
            cp.wait_send()
        if local is not None:
            out.wait()


class _Phase:
    def __init__(self, like, so_far, moves):
        self.like, self.so_far, self.moves = like, so_far, list(moves)


def _gather_send(shard, r0, nr, diagonal=True):
    def make(in_refs, g, ss, rs):
        sh, = in_refs
        x, y, c = _place()
        me = _index_of(x, y, c)
        rows = pl.ds(r0, nr)
        peers = [(x, y, 1 - c)] + [(px, py, c) for px, py in _chips(x, y)[:3 if diagonal else 2]]
        sends = [_rcopy(sh.at[rows], g.at[me, rows], ss.at[k], rs.at[k], p) for k, p in enumerate(peers)]
        arrivals = [_rcopy(sh.at[rows], g.at[_index_of(*p), rows], ss.at[k], rs.at[k], p) for k, p in enumerate(peers)]
        return sends, arrivals, (sh.at[rows], g.at[me, rows])

    return _Move([shard], 4 if diagonal else 3, make, stage=((nr, shard.shape[1]), shard.dtype))


def _gather_relay(r0, nr):
    def make(in_refs, g, ss, rs):
        x, y, c = _place()
        nx, ny, nd = [(px, py, c) for px, py in _chips(x, y)]
        first, second = pl.ds(r0, nr // 2), pl.ds(r0 + nr // 2, nr // 2)
        sends = [_rcopy(g.at[_index_of(*nx), first], g.at[_index_of(*nx), first], ss.at[0], rs.at[0], ny),
                 _rcopy(g.at[_index_of(*ny), second], g.at[_index_of(*ny), second], ss.at[1], rs.at[1], nx)]
        arrivals = [_rcopy(g.at[_index_of(*nx), first], g.at[_index_of(*nd), first], ss.at[0], rs.at[0], ny),
                    _rcopy(g.at[_index_of(*ny), second], g.at[_index_of(*nd), second], ss.at[1], rs.at[1], nx)]
        return sends, arrivals, None

    return _Move([], 2, make)


def _gather_pass(r0, nr):
    def make(in_refs, g, ss, rs):
        x, y, c = _place()
        rows = pl.ds(r0, nr)
        sends = [_rcopy(g.at[_index_of(px, py, c), rows], g.at[_index_of(px, py, c), rows], ss.at[j], rs.at[j], (x, y, 1 - c))
                 for j, (px, py) in enumerate(_chips(x, y))]
        arrivals = [_rcopy(g.at[_index_of(px, py, c), rows], g.at[_index_of(px, py, 1 - c), rows], ss.at[j], rs.at[j], (x, y, 1 - c))
                    for j, (px, py) in enumerate(_chips(x, y))]
        return sends, arrivals, None

    return _Move([], 3, make)


def _own_table():
    x, y, c = _place()
    return jnp.stack([_index_of(px, py, c) for px, py in [(x, y)] + _chips(x, y)]).astype(jnp.int32)


def _blockify_pair(dw, own_so_far, a_so_far, dst_r0, name):
    rows, tr, cw, win = dw.shape[0], 256, GLA_COLS // N_DEV, 896
    n_steps = rows // tr

    def body(*refs):
        x_ref, own_ref, a_ref, stage_ref, send_sems, recv_sem = refs[0], *refs[-5:]
        i = pl.program_id(0)
        x, y, c = _place()

        def send(slot, k):
            dst = a_ref.at[k, pl.ds(pl.multiple_of(dst_r0 + i * tr, tr), tr)]
            return _rcopy(stage_ref.at[slot], dst, send_sems.at[slot], recv_sem.at[0], (x, y, 1 - c))

        for j in range(N_DEV):
            window = x_ref[:, 768 * j:768 * j + win].astype(F32)
            tile = (pltpu.roll(window, win - 2 * j, 1) if j else window)[:, :cw].astype(BF16)
            k = ((j >> 2) ^ x) + 2 * (((j >> 1) & 1) ^ y)

            @pl.when((j & 1) == c)
            def _():
                own_ref[k] = tile

            @pl.when((j & 1) != c)
            def _():
                slot = (j >> 1) & 1
                if j >> 1 >= 2:
                    send(slot, k).wait_send()
                else:
                    pl.when(i > 0)(lambda: send(slot, k).wait_send())
                stage_ref[slot] = tile
                send(slot, k).start()

        @pl.when(i == n_steps - 1)
        def _():
            send(0, 0).wait_send()
            send(1, 0).wait_send()
            arrived = a_ref.at[:, pl.ds(dst_r0, rows)]
            _rcopy(arrived, arrived, send_sems.at[0], recv_sem.at[0], (x, y, 1 - c)).wait_recv()

    continues = a_so_far is not None
    own, a = pl.pallas_call(
        body, grid=(n_steps,),
        in_specs=[pl.BlockSpec((tr, GLA_PAD), lambda i: (i, 0))] + [HBM_SPEC] * (2 * continues),
        out_specs=[pl.BlockSpec((4, tr, cw), lambda i: (0, dst_r0 // tr + i, 0)), HBM_SPEC],
        out_shape=[_sds((4, D, cw), BF16), _sds((4, D, cw), BF16)],
        scratch_shapes=[pltpu.VMEM((2, tr, cw), BF16), pltpu.SemaphoreType.DMA((2,)), pltpu.SemaphoreType.DMA((1,))],
        input_output_aliases={1: 0, 2: 1} if continues else {},
        compiler_params=pltpu.CompilerParams(dimension_semantics=("arbitrary",), vmem_limit_bytes=48 * 1024 * 1024),
        name=name,
    )(*([dw] + [own_so_far, a_so_far] * continues))
    return own, a


def _reduce_cross(sums, src_r0, dst_r0, nr):
    def make(in_refs, b, ss, rs):
        t, = in_refs
        x, y, c = _place()
        src, dst = pl.ds(src_r0, nr), pl.ds(dst_r0, nr)
        sends = [_rcopy(t.at[j, src], b.at[j, dst], ss.at[j], rs.at[j], (px, py, c)) for j, (px, py) in enumerate(_chips(x, y))]
        return sends, sends, None

    return _Move([sums], 3, make)


def _hosted(body, *, name, grid, in_specs, out_specs, out_shape, args, scratch_shapes=(), block_bytes, scratch_bytes=0,
            phases=(), table=None):
    n_in, n_out, n_scr = len(args), len(out_shape), len(scratch_shapes)
    all_args, all_out_shape, sems, aliases, layout = list(args), list(out_shape), [], {}, []
    for j, ph in enumerate(phases):
        counts = []
        for mv in ph.moves:
            all_args += mv.ins
            counts.append(len(mv.ins))
            sems += mv.scratch()
        if ph.so_far is not None:
            aliases[len(all_args)] = n_out + j
            all_args.append(ph.so_far)
        layout.append((counts, ph.so_far is not None))
        all_out_shape.append(ph.like)
    n_extra_in = len(all_args) - n_in

    def wrapped(*refs):
        ins, pos = refs[:n_in], n_in
        move_ins = []
        for counts, continues in layout:
            per_move = []
            for cnt in counts:
                per_move.append(refs[pos:pos + cnt])
                pos += cnt
            pos += continues
            move_ins.append(per_move)
        outs = refs[pos:pos + n_out]
        bufs = refs[pos + n_out:pos + n_out + len(phases)]
        pos += n_out + len(phases)
        scratch = refs[pos:pos + n_scr]
        pos += n_scr
        move_sems = []
        for ph in phases:
            per_move = []
            for mv in ph.moves:
                count = len(mv.scratch())
                per_move.append(refs[pos:pos + count])
                pos += count
            move_sems.append(per_move)

        def each_move(fn_name):
            for ph, buf, per_in, per_sem in zip(phases, bufs, move_ins, move_sems):
                for mv, mv_in, mv_sem in zip(ph.moves, per_in, per_sem):
                    getattr(mv, fn_name)(mv_in, buf, mv_sem)

        if phases:
            first = functools.reduce(jnp.logical_and, [pl.program_id(a) == 0 for a in range(len(grid))])
            last = functools.reduce(jnp.logical_and, [pl.program_id(a) == grid[a] - 1 for a in range(len(grid))])
            pl.when(first)(lambda: each_move("start"))
        body(*ins, *outs, *scratch)
        if phases:
            pl.when(last)(lambda: each_move("finish"))

    all_args = [pltpu.with_memory_space_constraint(a, pltpu.HBM) for a in all_args]
    est = 2 * block_bytes + scratch_bytes
    params = pltpu.CompilerParams(dimension_semantics=("arbitrary",) * len(grid),
                                  vmem_limit_bytes=min(VMEM_CEILING, max(32 * 1024 * 1024, 2 * est)))
    all_in_specs, all_out_specs = list(in_specs) + [HBM_SPEC] * n_extra_in, list(out_specs) + [HBM_SPEC] * len(phases)
    if table is None:
        results = pl.pallas_call(
            wrapped, grid=grid, in_specs=all_in_specs, out_specs=all_out_specs, out_shape=all_out_shape,
            scratch_shapes=list(scratch_shapes) + sems, input_output_aliases=aliases, compiler_params=params, name=name,
        )(*all_args)
    else:
        results = pl.pallas_call(
            lambda table_ref, *refs: wrapped(*refs),
            grid_spec=pltpu.PrefetchScalarGridSpec(num_scalar_prefetch=1, grid=grid, in_specs=all_in_specs, out_specs=all_out_specs,
                                                   scratch_shapes=list(scratch_shapes) + sems),
            out_shape=all_out_shape, input_output_aliases={k + 1: v for k, v in aliases.items()}, compiler_params=params, name=name,
        )(table, *all_args)
    return list(results[:n_out]), list(results[n_out:])


class _Both:
    def __init__(self, copies):
        self.copies = copies

    def start(self):
        for cp in self.copies:
            cp.start()

    def wait_send(self):
        for cp in self.copies:
            cp.wait_send()

    def wait_recv(self):
        for cp in self.copies:
            cp.wait_recv()


def _carry(phases, name):
    def body(o_ref):
        o_ref[...] = jnp.zeros_like(o_ref)

    _, bufs = _hosted(body, name=name, grid=(1,), in_specs=[], out_specs=[pl.BlockSpec((8, 128), lambda i: (0, 0))],
                      out_shape=[_sds((8, 128), F32)], args=[], block_bytes=8 * 128 * 4, phases=phases)
    return bufs


def _gather_first(shard, small, name):
    cw, tr, n_tiles = shard.shape[1], 256, GLA_PAD // 128

    def body(sh_ref, sm_ref, wn_ref, g_ref, gs_ref, wt_ref, win_ref, tmp_ref, send_sems, recv_sems, local_sems):
        x, y, c = _place()
        me, sibling = (x, y, c), (x, y, 1 - c)
        chips = _chips(x, y)

        def copy(base, out_ref, k, block, to, src=None):
            dst = out_ref.at[_index_of(*block)]
            return _rcopy(dst if src is None else src, dst, send_sems.at[base + k], recv_sems.at[base + k], to)

        icopy = functools.partial(copy, 0, g_ref)
        scopy = functools.partial(copy, 8, gs_ref)

        def wcopy(k, block, to, src=None):
            if k in (1, 2):
                return icopy(k, block, to, src)
            halves = []
            for part, sem in enumerate((k, {0: 15, 4: 16, 5: 17, 6: 18}[k])):
                rows = pl.ds(part * (D // 2), D // 2)
                dst = g_ref.at[_index_of(*block), rows]
                halves.append(_rcopy(dst if src is None else src.at[rows], dst, send_sems.at[sem], recv_sems.at[sem], to))
            return _Both(halves)

        def relay(k, block, half, to):
            rows = pl.ds(half * (D // 2), D // 2)
            ref = g_ref.at[_index_of(*block), rows]
            return _rcopy(ref, ref, send_sems.at[k], recv_sems.at[k], to)

        def load(src_ref, slot):
            cp = pltpu.make_async_copy(src_ref, win_ref.at[slot], local_sems.at[0])
            cp.start()
            cp.wait()

        def place(slot, block):
            b = _index_of(*block)

            def rows_chunk(r, carry):
                rows = pl.ds(pl.multiple_of(r * tr, tr), tr)
                tmp_ref[:, :cw] = win_ref[slot, rows, :].astype(F32)
                shifted = pltpu.roll(tmp_ref[...], 2 * b, 1)
                for u in range(7):
                    wt_ref[6 * b + u, rows, :] = (wt_ref[6 * b + u, rows, :].astype(F32) + shifted[:, 128 * u:128 * (u + 1)]).astype(BF16)
                return carry

            lax.fori_loop(0, D // tr, rows_chunk, 0)

        small_own = pltpu.make_async_copy(sm_ref, gs_ref.at[_index_of(*me)], local_sems.at[1])
        small_own.start()
        first = [wcopy(1 + j, me, (*chip, c), src=sh_ref) for j, chip in enumerate(chips[:2])]
        first += [scopy(0, me, sibling, src=sm_ref)] + [scopy(1 + j, me, (*chip, c), src=sm_ref) for j, chip in enumerate(chips)]
        for cp in first:
            cp.start()

        def clear(t, carry):
            wt_ref[t] = jnp.zeros((D, 128), BF16)
            return carry

        lax.fori_loop(0, n_tiles, clear, 0)
        tmp_ref[...] = jnp.zeros_like(tmp_ref)

        def emit(t):
            pltpu.make_async_copy(wt_ref.at[t], wn_ref.at[t], local_sems.at[2]).start()

        def take(block, slot, arrivals=None, pass_on=None):
            for cp in arrivals or ():
                cp.wait_recv()
            load(sh_ref if arrivals is None else g_ref.at[_index_of(*block)], slot)
            if pass_on is not None:
                pass_on.start()
            place(slot, block)
            for u in range(1, 6):
                emit(6 * _index_of(*block) + u)

        near_x, near_y, far = [(*chip, c) for chip in chips]
        to_sibling = wcopy(0, me, sibling, src=win_ref.at[0])
        pass_x = wcopy(4, near_x, sibling, src=win_ref.at[1])
        pass_y = wcopy(5, near_y, sibling, src=win_ref.at[0])
        pass_d = wcopy(6, far, sibling, src=win_ref.at[0])
        relays = [relay(3, near_x, 0, near_y), relay(7, near_y, 1, near_x)]
        take(me, 0, pass_on=to_sibling)
        take(near_x, 1, [wcopy(1, near_x, me)], pass_x)
        relays[0].start()
        to_sibling.wait_send()
        take(near_y, 0, [wcopy(2, near_y, me)], pass_y)
        relays[1].start()
        small_passed = []
        for j, chip in enumerate(chips):
            scopy(1 + j, (*chip, c), me).wait_recv()
            cp = scopy(4 + j, (*chip, c), sibling)
            cp.start()
            small_passed.append(cp)
        pass_x.wait_send()
        take(sibling, 1, [wcopy(0, sibling, me)])
        pass_y.wait_send()
        take((*chips[0], 1 - c), 0, [wcopy(4, (*chips[0], 1 - c), me)])
        take((*chips[1], 1 - c), 1, [wcopy(5, (*chips[1], 1 - c), me)])
        take(far, 0, [relay(3, far, 0, near_y), relay(7, far, 1, near_x)], pass_d)
        take((*chips[2], 1 - c), 1, [wcopy(6, (*chips[2], 1 - c), me)])
        for t in range(0, n_tiles, 6):
            emit(t)
        scopy(0, sibling, me).wait_recv()
        for j, chip in enumerate(chips):
            scopy(4 + j, (*chip, 1 - c), me).wait_recv()
        for cp in first + small_passed + relays + [pass_d]:
            cp.wait_send()
        small_own.wait()
        pltpu.make_async_copy(wn_ref, wn_ref, local_sems.at[2]).wait()

    wn, _, gs = pl.pallas_call(
        body,
        in_specs=[HBM_SPEC] * 2, out_specs=[HBM_SPEC] * 3,
        out_shape=[_sds((n_tiles, D, 128), BF16), _sds((N_DEV,) + shard.shape, BF16), _sds((N_DEV,) + small.shape, small.dtype)],
        scratch_shapes=[pltpu.VMEM((n_tiles, D, 128), BF16), pltpu.VMEM((2, D, cw), BF16), pltpu.VMEM((tr, 7 * 128), F32),
                        pltpu.SemaphoreType.DMA((19,)), pltpu.SemaphoreType.DMA((19,)), pltpu.SemaphoreType.DMA((3,))],
        compiler_params=pltpu.CompilerParams(vmem_limit_bytes=48 * 1024 * 1024),
        name=name,
    )(shard, small)
    return wn, gs


def _mm(a, b, mode, out_dtype, *, tm, tn, tk, name, b_blocked=False, b_tiled=False, out_blocked=False, m_tiles=None, pair=None,
        phases=()):
    if mode == "nn":
        (m, k), dims = a.shape, NN
        a_blk, a_map = (tm, tk), (lambda i, j, kk: (i, kk))
        if b_blocked:
            assert b.shape[1] == k and b.shape[2] == tn and tk == k
            n = b.shape[0] * tn
            b_spec = pl.BlockSpec((None, tk, tn), lambda i, j, kk: (j, kk, 0))
        elif b_tiled:
            assert b.shape[1] == k and b.shape[2] == 128 and tn % 128 == 0
            n = b.shape[0] * 128
            b_spec = pl.BlockSpec((tn // 128, tk, 128), lambda i, j, kk: (j, kk, 0))
        else:
            assert b.shape[0] == k
            n = b.shape[1]
            b_spec = pl.BlockSpec((tk, tn), lambda i, j, kk: (kk, j))
    elif mode == "tn":
        (k, m), n, dims = a.shape, b.shape[1], TN
        assert b.shape[0] == k
        first = 0 if m_tiles is None else m_tiles[0]
        a_blk, a_map = (tk, tm), (lambda i, j, kk: (kk, i + first))
        b_spec = pl.BlockSpec((tk, tn), lambda i, j, kk: (kk, j))
    else:
        (m, k), dims = a.shape, NT
        a_blk, a_map = (tm, tk), (lambda i, j, kk: (i, kk))
        if b_blocked:
            assert b.shape[0] * b.shape[2] == k and b.shape[2] == tk
            n = b.shape[1]
            b_spec = pl.BlockSpec((None, tn, tk), lambda i, j, kk: (kk, j, 0))
        elif b_tiled:
            assert b.shape[0] * 128 == k and b.shape[2] == 128 and tk % 128 == 0
            n = b.shape[1]
            b_spec = pl.BlockSpec((tk // 128, tn, 128), lambda i, j, kk: (kk, j, 0))
        else:
            assert b.shape[1] == k
            n = b.shape[0]
            b_spec = pl.BlockSpec((tn, tk), lambda i, j, kk: (j, kk))
    assert m % tm == 0 and n % tn == 0 and k % tk == 0, (a.shape, b.shape, mode)
    nk = k // tk
    n_row_tiles = m // tm if m_tiles is None else m_tiles[1]
    if out_blocked:
        out_shape, out_spec = _sds((n // tn, n_row_tiles * tm, tn), out_dtype), pl.BlockSpec((None, tm, tn), lambda i, j, kk: (j, i, 0))
    else:
        out_shape, out_spec = _sds((n_row_tiles * tm, n), out_dtype), pl.BlockSpec((tm, tn), lambda i, j, kk: (i, j))

    grid = (n_row_tiles, n // tn, nk)

    def body(a_ref, b_ref, o_ref, *rest):
        rhs = jnp.concatenate([b_ref[u] for u in range(b_ref.shape[0])], axis=1) if b_tiled else b_ref[...]
        p = _dot(a_ref[...], rhs, dims)
        if nk == 1:
            o_ref[...] = p.astype(out_dtype)
            if pair is not None:
                _send_to_sibling(p.astype(out_dtype), *rest)
        else:
            acc_ref, = rest
            kk = pl.program_id(2)

            @pl.when(kk == 0)
            def _():
                acc_ref[...] = p

            @pl.when(kk > 0)
            def _():
                acc_ref[...] += p

            @pl.when(kk == nk - 1)
            def _():
                o_ref[...] = acc_ref[...].astype(out_dtype)

    def _send_to_sibling(tile, pair_ref, stage_ref, send_sems, recv_sem):
        i, j = pl.program_id(0), pl.program_id(1)
        x, y, c = _place()
        blk = pair["block"](i, j)
        k = ((blk >> 2) ^ x) + 2 * (((blk >> 1) & 1) ^ y)
        ordinal = pair["ordinal"](i, j)

        def send(slot):
            return _rcopy(stage_ref.at[slot], pair["dst"](pair_ref, k, i, j), send_sems.at[slot], recv_sem.at[0], (x, y, 1 - c))

        @pl.when((blk & 1) != c)
        def _():
            slot = ordinal & 1

            @pl.when(ordinal >= 2)
            def _():
                send(slot).wait_send()

            stage_ref[slot] = tile
            send(slot).start()

        @pl.when((i == grid[0] - 1) & (j == grid[1] - 1))
        def _():
            send(0).wait_send()
            send(1).wait_send()
            _rcopy(pair_ref, pair_ref, send_sems.at[0], recv_sem.at[0], (x, y, 1 - c)).wait_recv()

    blocks = _nbytes(a_blk, a.dtype) + tk * tn * jnp.dtype(b.dtype).itemsize + _nbytes((tm, tn), out_dtype)
    out_specs, out_shapes, scratch = [out_spec], [out_shape], [] if nk == 1 else [pltpu.VMEM((tm, tn), F32)]
    scratch_bytes = _nbytes((tm, tn), F32) * (nk > 1)
    if pair is not None:
        assert nk == 1
        out_specs, out_shapes = out_specs + [HBM_SPEC], out_shapes + [pair["like"]]
        scratch = [pltpu.VMEM((2, tm, tn), out_dtype), pltpu.SemaphoreType.DMA((2,)), pltpu.SemaphoreType.DMA((1,))]
        scratch_bytes = 2 * _nbytes((tm, tn), out_dtype)
    outs, bufs = _hosted(
        body, name=name, grid=grid,
        in_specs=[pl.BlockSpec(a_blk, a_map), b_spec], out_specs=out_specs, out_shape=out_shapes, args=[a, b],
        scratch_shapes=scratch, block_bytes=blocks, scratch_bytes=scratch_bytes, phases=phases)
    return outs[0], outs[1:] + bufs


NT_ROWS_TM = 512
NT_ROWS_SUB = 128


def _nt_rows(a, b, *, tk, name, row_ins, vec_ins, outs, tail, b_blocked=False, b_tiled=False, phases=()):
    m, k = a.shape
    tm, nk = NT_ROWS_TM, k // tk
    if b_blocked:
        assert b.shape[0] * b.shape[2] == k and b.shape[2] == tk and b.shape[1] == D
        b_spec = pl.BlockSpec((None, D, tk), lambda i, kk: (kk, 0, 0))
    else:
        assert b_tiled and b.shape[0] * 128 == k and tk % 128 == 0 and b.shape[1] == D
        b_spec = pl.BlockSpec((tk // 128, D, 128), lambda i, kk: (kk, 0, 0))
    row_spec, vec_spec = pl.BlockSpec((tm, D), lambda i, kk: (i, 0)), pl.BlockSpec((1, D), lambda i, kk: (0, 0))
    n_row, n_vec, n_out = len(row_ins), len(vec_ins), len(outs)
    assert nk >= 2

    def body(a_ref, b_ref, *rest):
        row_hbm, vec_refs = rest[:n_row], rest[n_row:n_row + n_vec]
        out_refs = rest[n_row + n_vec:n_row + n_vec + n_out]
        acc_ref, row_sems = rest[n_row + n_vec + n_out], rest[-1]
        row_refs = rest[n_row + n_vec + n_out + 1:-1]
        rhs = jnp.concatenate([b_ref[u] for u in range(b_ref.shape[0])], axis=1) if b_tiled else b_ref[...]
        p = _dot(a_ref[...], rhs, NT)
        i, kk = pl.program_id(0), pl.program_id(1)

        def fetch(r):
            return pltpu.make_async_copy(row_hbm[r].at[pl.ds(pl.multiple_of(i * tm, tm), tm)], row_refs[r], row_sems.at[r])

        @pl.when(kk == 0)
        def _():
            for r in range(n_row):
                fetch(r).start()
            acc_ref[...] = p

        @pl.when(kk > 0)
        def _():
            acc_ref[...] += p

        @pl.when(kk == nk - 1)
        def _():
            for r in range(n_row):
                fetch(r).wait()
            for s in range(tm // NT_ROWS_SUB):
                rows = slice(s * NT_ROWS_SUB, (s + 1) * NT_ROWS_SUB)
                tail(acc_ref[rows, :], rows, (i == 0) if s == 0 else None, row_refs, vec_refs, out_refs)

    out_specs = [row_spec if kind == "row" else vec_spec for kind, _ in outs]
    out_shape = [_sds((m, D) if kind == "row" else (1, D), dt) for kind, dt in outs]
    blocks = tm * tk * 2 + D * tk * 2 + sum(tm * D * jnp.dtype(dt).itemsize for kind, dt in outs if kind == "row")
    scratch = [pltpu.VMEM((tm, D), F32)] + [pltpu.VMEM((tm, D), x.dtype) for x in row_ins] + [pltpu.SemaphoreType.DMA((n_row,))]
    return _hosted(body, name=name, grid=(m // tm, nk), in_specs=[pl.BlockSpec((tm, tk), lambda i, kk: (i, kk)), b_spec]
                   + [HBM_SPEC] * n_row + [vec_spec] * n_vec, out_specs=out_specs, out_shape=out_shape,
                   args=[a, b] + list(row_ins) + list(vec_ins), scratch_shapes=scratch,
                   block_bytes=blocks, scratch_bytes=(1 + n_row) * tm * D * 4, phases=phases)


def _nn_rows(a, b, *, name, row_ins, vec_ins, outs, tail):
    m, k = a.shape
    tm = NT_ROWS_TM
    assert b.shape == (k, D)
    row_spec, vec_spec = pl.BlockSpec((tm, D), lambda i: (i, 0)), pl.BlockSpec((1, D), lambda i: (0, 0))
    n_row, n_vec, n_out = len(row_ins), len(vec_ins), len(outs)

    def body(a_ref, b_ref, *rest):
        row_hbm, vec_refs = rest[:n_row], rest[n_row:n_row + n_vec]
        out_refs = rest[n_row + n_vec:n_row + n_vec + n_out]
        d_ref, row_sems = rest[n_row + n_vec + n_out], rest[-1]
        row_refs = rest[n_row + n_vec + n_out + 1:-1]
        i = pl.program_id(0)
        fetches = [pltpu.make_async_copy(row_hbm[r].at[pl.ds(pl.multiple_of(i * tm, tm), tm)], row_refs[r], row_sems.at[r])
                   for r in range(n_row)]
        for cp in fetches:
            cp.start()
        d_ref[...] = _dot(a_ref[...], b_ref[...])
        for cp in fetches:
            cp.wait()
        for s in range(tm // NT_ROWS_SUB):
            rows = slice(s * NT_ROWS_SUB, (s + 1) * NT_ROWS_SUB)
            tail(d_ref[rows, :], rows, (i == 0) if s == 0 else None, row_refs, vec_refs, out_refs)

    out_specs = [row_spec if kind == "row" else vec_spec for kind, _ in outs]
    out_shape = [_sds((m, D) if kind == "row" else (1, D), dt) for kind, dt in outs]
    blocks = tm * k * 2 + k * D * 2 + sum(tm * D * jnp.dtype(dt).itemsize for kind, dt in outs if kind == "row")
    scratch = [pltpu.VMEM((tm, D), F32)] + [pltpu.VMEM((tm, D), x.dtype) for x in row_ins] + [pltpu.SemaphoreType.DMA((n_row,))]
    outs_, _ = _hosted(body, name=name, grid=(m // tm,), in_specs=[pl.BlockSpec((tm, k), lambda i: (i, 0)),
                                                                 pl.BlockSpec((k, D), lambda i: (0, 0))]
                       + [HBM_SPEC] * n_row + [vec_spec] * n_vec, out_specs=out_specs, out_shape=out_shape,
                       args=[a, b] + list(row_ins) + list(vec_ins), scratch_shapes=scratch,
                       block_bytes=blocks, scratch_bytes=(1 + n_row) * tm * D * 4)
    return outs_


def _vec_add(ref, value, first):
    if first is None:
        ref[...] += value
    else:
        pl.when(first)(lambda: ref.__setitem__(Ellipsis, value))
        pl.when(jnp.logical_not(first))(lambda: ref.__setitem__(Ellipsis, ref[...] + value))


RB = 256


def _row_spec(width):
    return pl.BlockSpec((RB, width), lambda i: (i, 0))


def _vec_spec(width):
    return pl.BlockSpec((1, width), lambda i: (0, 0))


def _rinv(x):
    return lax.rsqrt(jnp.mean(x * x, axis=-1, keepdims=True) + EPS)


def _norm_bwd(dyn, xhat, r):
    return r * (dyn - xhat * jnp.mean(dyn * xhat, axis=-1, keepdims=True))


def _colsum(x):
    return jnp.sum(x, axis=0, keepdims=True)


def _prenorm(x, gain):
    def body(x_ref, g_ref, h_ref):
        xv = x_ref[...]
        h_ref[...] = (xv * _rinv(xv) * g_ref[...]).astype(BF16)

    outs, _ = _hosted(body, name="prenorm", grid=(S // RB,), in_specs=[_row_spec(D), _vec_spec(D)], out_specs=[_row_spec(D)],
                      out_shape=[_sds((S, D), BF16)], args=[x, gain], block_bytes=RB * D * 6)
    return outs[0]


def _mid_fwd(x, y, npost, npre, phases=()):
    def body(x_ref, y_ref, po_ref, pr_ref, x1_ref, h1_ref):
        yv = y_ref[...]
        x1 = x_ref[...] + yv * _rinv(yv) * po_ref[...]
        x1_ref[...] = x1
        h1_ref[...] = (x1 * _rinv(x1) * pr_ref[...]).astype(BF16)

    return _hosted(body, name="mid_fwd", grid=(S // RB,), in_specs=[_row_spec(D), _row_spec(D), _vec_spec(D), _vec_spec(D)],
                   out_specs=[_row_spec(D), _row_spec(D)], out_shape=[_sds((S, D), F32), _sds((S, D), BF16)],
                   args=[x, y, npost, npre], block_bytes=RB * D * 14, phases=phases)


def _final_tail(yv, rows, first, row_refs, vec_refs, out_refs):
    (x_ref, t_ref), (po_ref,), (loss_ref, dx_ref, dy_ref, dpo_ref) = row_refs, vec_refs, out_refs
    r = _rinv(yv)
    yhat = yv * r
    err = x_ref[rows, :] + yhat * po_ref[...] - t_ref[rows, :]
    dx = err * (1.0 / D)
    dx_ref[rows, :] = dx
    dy_ref[rows, :] = _norm_bwd(dx * po_ref[...], yhat, r).astype(BF16)
    _vec_add(loss_ref, _colsum(err * err), first)
    _vec_add(dpo_ref, _colsum(dx * yhat), first)


def _mid_bwd_tail(dh, rows, first, row_refs, vec_refs, out_refs):
    (dx2_ref, x_ref, y_ref), (pr_ref, po_ref), (dx1_ref, dy_ref, dpr_ref, dpo_ref) = row_refs, vec_refs, out_refs
    xv = x_ref[rows, :]
    r = _rinv(xv)
    xhat = xv * r
    dx1 = dx2_ref[rows, :] + _norm_bwd(dh * pr_ref[...], xhat, r)
    dx1_ref[rows, :] = dx1
    yv = y_ref[rows, :]
    ry = _rinv(yv)
    yhat = yv * ry
    dy_ref[rows, :] = _norm_bwd(dx1 * po_ref[...], yhat, ry).astype(BF16)
    _vec_add(dpr_ref, _colsum(dh * xhat), first)
    _vec_add(dpo_ref, _colsum(dx1 * yhat), first)


def _first_bwd_tail(dh, rows, first, row_refs, vec_refs, out_refs):
    (dx1_ref, x_ref), (pr_ref,), (gx_ref, dpr_ref) = row_refs, vec_refs, out_refs
    xv = x_ref[rows, :]
    r = _rinv(xv)
    xhat = xv * r
    gx_ref[rows, :] = dx1_ref[rows, :] + _norm_bwd(dh * pr_ref[...], xhat, r)
    _vec_add(dpr_ref, _colsum(dh * xhat), first)


GLA_RB = 256
GLA_CPB = GLA_RB // C


def _sigmoid(x):
    return 1.0 / (1.0 + jnp.exp(-x))


def _tri(strict):
    r = lax.broadcasted_iota(jnp.int32, (C, C), 0)
    c = lax.broadcasted_iota(jnp.int32, (C, C), 1)
    return jnp.where(c < r if strict else c <= r, 1.0, 0.0).astype(BF16)


def _tri_dot(tri, x):
    hi = x.astype(BF16)
    lo = (x - hi.astype(F32)).astype(BF16)
    return _dot(tri, hi) + _dot(tri, lo)


def _gla_gates(glr_b, w2, b, tri):
    z = _dot(glr_b, w2) + b
    log_a = (jnp.minimum(z, 0.0) - jnp.log(1.0 + jnp.exp(-jnp.abs(z)))) * (1.0 / GLA_TAU)
    bcum = _tri_dot(tri, log_a)
    b_end = jnp.sum(log_a, axis=0, keepdims=True)
    return z, jnp.exp(b_end - bcum), jnp.exp(b_end)


def _gla_fwd(proj, w2p, bgate, ogain, phases=()):
    def body(p_ref, w2_ref, b_ref, og_ref, y_ref, st_out_ref, st_ref):
        @pl.when(pl.program_id(0) == 0)
        def _():
            st_ref[...] = jnp.zeros_like(st_ref)

        tri = _tri(False)

        def chunk(ci, carry):
            rows = pl.ds(pl.multiple_of(ci * C, C), C)
            glr_b = p_ref[rows, LR0:LR0 + LRP].astype(BF16)
            _, ea_all, dec_all = _gla_gates(glr_b, w2_ref[...], b_ref[...], tri)
            for h in range(H):
                ea, dec = ea_all[:, h * DK:(h + 1) * DK], dec_all[:, h * DK:(h + 1) * DK]
                k_dec = (p_ref[rows, K0 + h * DK:K0 + (h + 1) * DK] * ea).astype(BF16)
                v_b = p_ref[rows, V0 + h * DV:V0 + (h + 1) * DV].astype(BF16)
                st = st_ref[h] * dec + _dot(v_b, k_dec, TN)
                st_ref[h] = st
                st_b = st.astype(BF16)
                st_out_ref[ci, h] = st_b
                q_b = (p_ref[rows, Q0 + h * DK:Q0 + (h + 1) * DK] * (DK ** -0.5)).astype(BF16)
                o = _dot(q_b, st_b, NT)
                on = o * _rinv(o)
                g = p_ref[rows, G0 + h * DV:G0 + (h + 1) * DV]
                y_ref[rows, h * DV:(h + 1) * DV] = (on * og_ref[:, h * DV:(h + 1) * DV] * (g * _sigmoid(g))).astype(BF16)
            return carry

        lax.fori_loop(0, GLA_CPB, chunk, 0, unroll=True)

    blocks = GLA_RB * GLA_PAD * 4 + GLA_RB * D * 2 + GLA_CPB * H * DV * DK * 2
    return _hosted(
        body, name="gla_fwd", grid=(S // GLA_RB,),
        in_specs=[pl.BlockSpec((GLA_RB, GLA_PAD), lambda i: (i, 0)),
                  pl.BlockSpec((LRP, H * DK), lambda i: (0, 0)),
                  pl.BlockSpec((1, H * DK), lambda i: (0, 0)),
                  pl.BlockSpec((1, H * DV), lambda i: (0, 0))],
        out_specs=[pl.BlockSpec((GLA_RB, H * DV), lambda i: (i, 0)),
                   pl.BlockSpec((GLA_CPB, H, DV, DK), lambda i: (i, 0, 0, 0))],
        out_shape=[_sds((S, H * DV), BF16), _sds((NC, H, DV, DK), BF16)],
        args=[proj, w2p, bgate, ogain], scratch_shapes=[pltpu.VMEM((H, DV, DK), F32)],
        block_bytes=blocks, scratch_bytes=H * DV * DK * 4, phases=phases)


def _gla_bwd(proj, dypre, states, w2p, bgate, ogain, phases=()):
    nb = S // GLA_RB

    def body(p_ref, dy_ref, st_blk_ref, st_prev_ref, w2_ref, b_ref, og_ref,
             dp_ref, dog_ref, dbg_ref, dw2_ref, r_ref):
        step = pl.program_id(0)

        @pl.when(step == 0)
        def _():
            r_ref[...] = jnp.zeros_like(r_ref)
            dog_ref[...] = jnp.zeros_like(dog_ref)
            dbg_ref[...] = jnp.zeros_like(dbg_ref)
            dw2_ref[...] = jnp.zeros_like(dw2_ref)

        tri = _tri(False)
        tri_strict = _tri(True)
        has_prev = jnp.where(step < nb - 1, 1.0, 0.0).astype(F32)

        def chunk(ci, st_prev_of):
            rows = pl.ds(ci * C if isinstance(ci, int) else pl.multiple_of(ci * C, C), C)
            glr_b = p_ref[rows, LR0:LR0 + LRP].astype(BF16)
            z, ea_all, dec_all = _gla_gates(glr_b, w2_ref[...], b_ref[...], tri)
            d_a, d_end = [], []
            for h in range(H):
                kcol = slice(h * DK, (h + 1) * DK)
                vcol = slice(h * DV, (h + 1) * DV)
                ea, dec = ea_all[:, kcol], dec_all[:, kcol]
                k_dec = p_ref[rows, K0 + h * DK:K0 + (h + 1) * DK] * ea
                k_dec_b = k_dec.astype(BF16)
                v_b = p_ref[rows, V0 + h * DV:V0 + (h + 1) * DV].astype(BF16)
                q_b = (p_ref[rows, Q0 + h * DK:Q0 + (h + 1) * DK] * (DK ** -0.5)).astype(BF16)
                st_b = st_blk_ref[ci, h]
                o = _dot(q_b, st_b, NT)
                rinv = _rinv(o)
                on = o * rinv
                g = p_ref[rows, G0 + h * DV:G0 + (h + 1) * DV]
                sg = _sigmoid(g)
                og = og_ref[:, vcol]
                dyp = dy_ref[rows, vcol]
                dp_ref[rows, G0 + h * DV:G0 + (h + 1) * DV] = (dyp * (on * og) * (sg * (1.0 + g * (1.0 - sg)))).astype(BF16)
                dpn = dyp * (g * sg)
                dog_ref[:, vcol] += _colsum(dpn * on)
                do_b = _norm_bwd(dpn * og, on, rinv).astype(BF16)
                gt = _dot(do_b, q_b, TN) + r_ref[h]
                gt_b = gt.astype(BF16)
                dp_ref[rows, Q0 + h * DK:Q0 + (h + 1) * DK] = (_dot(do_b, st_b) * (DK ** -0.5)).astype(BF16)
                dkd = _dot(v_b, gt_b)
                dp_ref[rows, V0 + h * DV:V0 + (h + 1) * DV] = _dot(k_dec_b, gt_b, NT).astype(BF16)
                dp_ref[rows, K0 + h * DK:K0 + (h + 1) * DK] = (dkd * ea).astype(BF16)
                d_a.append(dkd * k_dec)
                d_end.append(_colsum(gt * st_prev_of(h)) * dec)
                r_ref[h] = gt * dec
            dla = _tri_dot(tri_strict, jnp.concatenate(d_a, axis=1)) + jnp.concatenate(d_end, axis=1)
            dz = dla * (1.0 / GLA_TAU) * (1.0 - _sigmoid(z))
            dz_b = dz.astype(BF16)
            dbg_ref[...] += _colsum(dz)
            dw2_ref[...] += _dot(glr_b, dz_b, TN)
            dp_ref[rows, LR0:LR0 + LRP] = _dot(dz_b, w2_ref[...], NT).astype(BF16)

        def later_chunk(t, carry):
            ci = GLA_CPB - 1 - t
            chunk(ci, lambda h: st_blk_ref[ci - 1, h].astype(F32))
            return carry

        lax.fori_loop(0, GLA_CPB - 1, later_chunk, 0, unroll=True)
        chunk(0, lambda h: st_prev_ref[0, h].astype(F32) * has_prev)

    blocks = (GLA_RB * GLA_PAD * 4 + GLA_RB * D * 4 + (GLA_CPB + 1) * H * DV * DK * 2 + GLA_RB * GLA_PAD * 2)
    rev = lambda i: nb - 1 - i
    return _hosted(
        body, name="gla_bwd", grid=(nb,),
        in_specs=[pl.BlockSpec((GLA_RB, GLA_PAD), lambda i: (rev(i), 0)),
                  pl.BlockSpec((GLA_RB, H * DV), lambda i: (rev(i), 0)),
                  pl.BlockSpec((GLA_CPB, H, DV, DK), lambda i: (rev(i), 0, 0, 0)),
                  pl.BlockSpec((1, H, DV, DK), lambda i: (jnp.maximum(rev(i) * GLA_CPB - 1, 0), 0, 0, 0)),
                  pl.BlockSpec((LRP, H * DK), lambda i: (0, 0)),
                  pl.BlockSpec((1, H * DK), lambda i: (0, 0)),
                  pl.BlockSpec((1, H * DV), lambda i: (0, 0))],
        out_specs=[pl.BlockSpec((GLA_RB, GLA_PAD), lambda i: (rev(i), 0)),
                   pl.BlockSpec((1, H * DV), lambda i: (0, 0)),
                   pl.BlockSpec((1, H * DK), lambda i: (0, 0)),
                   pl.BlockSpec((LRP, H * DK), lambda i: (0, 0))],
        out_shape=[_sds((S, GLA_PAD), BF16), _sds((1, H * DV), F32), _sds((1, H * DK), F32), _sds((LRP, H * DK), F32)],
        args=[proj, dypre, states, states, w2p, bgate, ogain], scratch_shapes=[pltpu.VMEM((H, DV, DK), F32)],
        block_bytes=blocks, scratch_bytes=H * DV * DK * 4, phases=phases)


SGU_RB = 256
GELU_C = 0.7978845608028654
GELU_A = 0.044715


def _gelu(x):
    return 0.5 * x * (1.0 + jnp.tanh(GELU_C * (x + GELU_A * x * x * x)))


def _gelu_grad(x):
    t = jnp.tanh(GELU_C * (x + GELU_A * x * x * x))
    return 0.5 * (1.0 + t) + 0.5 * x * (1.0 - t * t) * (GELU_C * (1.0 + 3.0 * GELU_A * x * x))


def _causal_mask(transposed=False):
    i = lax.broadcasted_iota(jnp.int32, (SGU_BLOCK, SGU_BLOCK), 1 if transposed else 0)
    j = lax.broadcasted_iota(jnp.int32, (SGU_BLOCK, SGU_BLOCK), 0 if transposed else 1)
    return (i >= C) | (j < C)


def _layer_norm(vf, gain, bias):
    mu = jnp.mean(vf, axis=-1, keepdims=True)
    cen = vf - mu
    rstd = lax.rsqrt(jnp.mean(cen * cen, axis=-1, keepdims=True) + EPS)
    xhat = cen * rstd
    return xhat, rstd, xhat * gain + bias


def _sgu_fwd(proj, lng, lnb, ws, bsb, phases=()):
    def body(p_ref, g_ref, b_ref, ws_ref, bs_ref, o_ref):
        mask = _causal_mask()
        for n in range(SGU_RB // SGU_BLOCK):
            rows = slice(n * SGU_BLOCK, (n + 1) * SGU_BLOCK)
            _, _, vn = _layer_norm(_gelu(p_ref[rows, D:2 * D]), g_ref[...], b_ref[...])
            vn_b = vn.astype(BF16)
            for gi in range(SGU_G):
                cols = slice(gi * SGU_GD, (gi + 1) * SGU_GD)
                w = jnp.where(mask, ws_ref[gi], 0.0).astype(BF16)
                vs = _dot(w, vn_b[:, cols]) + bs_ref[gi]
                gate = p_ref[rows, 2 * D + gi * SGU_GD:2 * D + (gi + 1) * SGU_GD]
                o_ref[rows, cols] = (_gelu(p_ref[rows, cols]) * vs * (gate * _sigmoid(gate))).astype(BF16)

    blocks = SGU_RB * SGU_COLS * 4 + SGU_RB * D * 2 + SGU_G * SGU_BLOCK * (SGU_BLOCK + SGU_GD) * 4
    return _hosted(
        body, name="sgu_fwd", grid=(S // SGU_RB,),
        in_specs=[pl.BlockSpec((SGU_RB, SGU_COLS), lambda i: (i, 0)),
                  pl.BlockSpec((1, D), lambda i: (0, 0)), pl.BlockSpec((1, D), lambda i: (0, 0)),
                  pl.BlockSpec((SGU_G, SGU_BLOCK, SGU_BLOCK), lambda i: (0, 0, 0)),
                  pl.BlockSpec((SGU_G, SGU_BLOCK, SGU_GD), lambda i: (0, 0, 0))],
        out_specs=[pl.BlockSpec((SGU_RB, D), lambda i: (i, 0))], out_shape=[_sds((S, D), BF16)],
        args=[proj, lng, lnb, ws, bsb], block_bytes=blocks, phases=phases)


def _sgu_bwd(proj, dpre, lng, lnb, ws, wst, bsb, phases=()):
    nsteps = S // SGU_RB

    def body(p_ref, d_ref, g_ref, b_ref, ws_ref, wst_ref, bs_ref,
             dp_ref, dg_ref, db_ref, dws_ref, dbs_ref, dvn_ref, dvs_acc_ref):
        step = pl.program_id(0)

        @pl.when(step == 0)
        def _():
            dg_ref[...] = jnp.zeros_like(dg_ref)
            db_ref[...] = jnp.zeros_like(db_ref)
            dws_ref[...] = jnp.zeros_like(dws_ref)
            dvs_acc_ref[...] = jnp.zeros_like(dvs_acc_ref)

        mask = _causal_mask()
        maskt = _causal_mask(transposed=True)
        for n in range(SGU_RB // SGU_BLOCK):
            rows = slice(n * SGU_BLOCK, (n + 1) * SGU_BLOCK)
            v = p_ref[rows, D:2 * D]
            xhat, rstd, vn = _layer_norm(_gelu(v), g_ref[...], b_ref[...])
            vn_b = vn.astype(BF16)
            for gi in range(SGU_G):
                cols = slice(gi * SGU_GD, (gi + 1) * SGU_GD)
                w = jnp.where(mask, ws_ref[gi], 0.0).astype(BF16)
                wt = jnp.where(maskt, wst_ref[gi], 0.0).astype(BF16)
                vs = _dot(w, vn_b[:, cols]) + bs_ref[gi]
                u = p_ref[rows, cols]
                gate = p_ref[rows, 2 * D + gi * SGU_GD:2 * D + (gi + 1) * SGU_GD]
                sg = _sigmoid(gate)
                gu = _gelu(u)
                dpre_g = d_ref[rows, cols]
                t = dpre_g * (gate * sg)
                dp_ref[rows, cols] = (t * vs * _gelu_grad(u)).astype(BF16)
                dp_ref[rows, 2 * D + gi * SGU_GD:2 * D + (gi + 1) * SGU_GD] = (
                    dpre_g * gu * vs * (sg * (1.0 + gate * (1.0 - sg)))).astype(BF16)
                dvs = t * gu
                dvs_b = dvs.astype(BF16)
                dvs_acc_ref[:, cols] += dvs
                dws_ref[gi] += _dot(dvs_b, vn_b[:, cols], NT)
                dvn_ref[:, cols] = _dot(wt, dvs_b)
            dvn = dvn_ref[...]
            dg_ref[...] += _colsum(dvn * xhat)
            db_ref[...] += _colsum(dvn)
            dxh = dvn * g_ref[...]
            dvf = rstd * (dxh - jnp.mean(dxh, axis=-1, keepdims=True) - xhat * jnp.mean(dxh * xhat, axis=-1, keepdims=True))
            dp_ref[rows, D:2 * D] = (dvf * _gelu_grad(v)).astype(BF16)

        @pl.when(step == nsteps - 1)
        def _():
            lane = lax.broadcasted_iota(jnp.int32, (SGU_BLOCK, SGU_BLOCK), 1)
            out = jnp.zeros((SGU_BLOCK, SGU_BLOCK), F32)
            for gi in range(SGU_G):
                out = out + jnp.where(lane == gi, jnp.sum(dvs_acc_ref[:, gi * SGU_GD:(gi + 1) * SGU_GD], axis=1, keepdims=True), 0.0)
                dws_ref[gi] = jnp.where(mask, dws_ref[gi], 0.0)
            dbs_ref[...] = out

    blocks = SGU_RB * SGU_COLS * 6 + SGU_RB * D * 4 + SGU_G * SGU_BLOCK * (3 * SGU_BLOCK + SGU_GD) * 4
    const3 = lambda i: (0, 0, 0)
    return _hosted(
        body, name="sgu_bwd", grid=(nsteps,),
        in_specs=[pl.BlockSpec((SGU_RB, SGU_COLS), lambda i: (i, 0)),
                  pl.BlockSpec((SGU_RB, D), lambda i: (i, 0)),
                  pl.BlockSpec((1, D), lambda i: (0, 0)), pl.BlockSpec((1, D), lambda i: (0, 0)),
                  pl.BlockSpec((SGU_G, SGU_BLOCK, SGU_BLOCK), const3),
                  pl.BlockSpec((SGU_G, SGU_BLOCK, SGU_BLOCK), const3),
                  pl.BlockSpec((SGU_G, SGU_BLOCK, SGU_GD), const3)],
        out_specs=[pl.BlockSpec((SGU_RB, SGU_COLS), lambda i: (i, 0)),
                   pl.BlockSpec((1, D), lambda i: (0, 0)), pl.BlockSpec((1, D), lambda i: (0, 0)),
                   pl.BlockSpec((SGU_G, SGU_BLOCK, SGU_BLOCK), const3),
                   pl.BlockSpec((SGU_BLOCK, SGU_BLOCK), lambda i: (0, 0))],
        out_shape=[_sds((S, SGU_COLS), BF16), _sds((1, D), F32), _sds((1, D), F32),
                   _sds((SGU_G, SGU_BLOCK, SGU_BLOCK), F32), _sds((SGU_BLOCK, SGU_BLOCK), F32)],
        args=[proj, dpre, lng, lnb, ws, wst, bsb],
        scratch_shapes=[pltpu.VMEM((SGU_BLOCK, D), F32), pltpu.VMEM((SGU_BLOCK, D), F32)],
        block_bytes=blocks, scratch_bytes=2 * SGU_BLOCK * D * 4, phases=phases)


def _pair_sum(own, a, r0, nr, name, table=None, phases=()):
    c = own.shape[2]
    tr = 256
    assert r0 % tr == 0 and nr % tr == 0

    def body(own_ref, sib_ref, o_ref):
        o_ref[...] = (own_ref[...].astype(F32) + sib_ref[...].astype(F32)).astype(BF16)

    own_map = ((lambda j, i: (1 + j, r0 // tr + i, 0)) if table is None else
               (lambda j, i, t: (t[1 + j], r0 // tr + i, 0)))
    cpad = -(-c // 128) * 128
    outs, bufs = _hosted(
        body, name=name, grid=(3, nr // tr),
        in_specs=[pl.BlockSpec((None, tr, c), own_map),
                  pl.BlockSpec((None, tr, c), lambda j, i, *t: (1 + j, r0 // tr + i, 0))],
        out_specs=[pl.BlockSpec((None, tr, c), lambda j, i, *t: (j, i, 0))], out_shape=[_sds((3, nr, c), BF16)],
        args=[own, a], block_bytes=3 * tr * cpad * 2, phases=phases, table=table)
    return outs[0], bufs


def _adamw_math(w, g, m, v):
    m = ADAM_B1 * m + (1.0 - ADAM_B1) * g
    v = ADAM_B2 * v + (1.0 - ADAM_B2) * (g * g)
    m_hat = m / (1.0 - ADAM_B1 ** ADAM_STEP)
    v_hat = v / (1.0 - ADAM_B2 ** ADAM_STEP)
    delta = -ADAM_LR * (m_hat / (jnp.sqrt(v_hat) + ADAM_EPS) + ADAM_WD * w)
    return delta, m, v


def _sum_adamw(own, a, b, w, m, v, *, name, phases=(), table=None):
    r, c = w.shape
    tr = 256

    def body(own_ref, sib_ref, far_ref, w_ref, m_ref, v_ref, g_ref, d_ref, nm_ref, nv_ref):
        g = own_ref[...].astype(F32) + sib_ref[...].astype(F32)
        for j in range(3):
            g = g + far_ref[j].astype(F32)
        g_ref[...] = g
        d_ref[...], nm_ref[...], nv_ref[...] = _adamw_math(w_ref[...], g, m_ref[...], v_ref[...])

    spec = pl.BlockSpec((tr, c), lambda i, *t: (i, 0))
    own_map = (lambda i: (0, i, 0)) if table is None else (lambda i, t: (t[0], i, 0))
    cpad = -(-c // 128) * 128
    return _hosted(
        body, name=name, grid=(r // tr,),
        in_specs=[pl.BlockSpec((None, tr, c), own_map), pl.BlockSpec((None, tr, c), lambda i, *t: (0, i, 0)),
                  pl.BlockSpec((3, tr, c), lambda i, *t: (0, i, 0)), spec, spec, spec],
        out_specs=[spec] * 4, out_shape=[_sds((r, c), F32)] * 4, args=[own, a, b, w, m, v],
        block_bytes=5 * tr * cpad * 2 + 7 * tr * cpad * 4, phases=phases, table=table)


def _sum_parts(parts, name):
    n, r, c = parts.shape

    def body(p_ref, o_ref):
        g = p_ref[0]
        for j in range(1, n):
            g = g + p_ref[j]
        o_ref[...] = g

    outs, _ = _hosted(body, name=name, grid=(1,), in_specs=[pl.BlockSpec((n, r, c), lambda i: (0, 0, 0))],
                      out_specs=[pl.BlockSpec((r, c), lambda i: (0, 0))], out_shape=[_sds((r, c), F32)], args=[parts],
                      block_bytes=(n + 1) * r * c * 4)
    return outs[0]


def _adamw(w, g, m, v, name):
    def body(w_ref, g_ref, m_ref, v_ref, d_ref, nm_ref, nv_ref):
        d_ref[...], nm_ref[...], nv_ref[...] = _adamw_math(w_ref[...], g_ref[...], m_ref[...], v_ref[...])

    spec = pl.BlockSpec(w.shape, lambda i: (0, 0))
    outs, _ = _hosted(body, name=name, grid=(1,), in_specs=[spec] * 4, out_specs=[spec] * 3, out_shape=[_sds(w.shape, F32)] * 3,
                      args=[w, g, m, v], block_bytes=7 * _nbytes(w.shape, F32))
    return outs


def _blocks_to_columns(g):
    n, r, c = g.shape
    return jnp.transpose(g, (1, 0, 2)).reshape(r, n * c)


def _pack(parts):
    return jnp.concatenate([p.reshape(-1) for p in parts]).reshape(-1, 128)


def _unpack(packed, like):
    flat, outs, off = packed.reshape(-1), [], 0
    for p in like:
        outs.append(flat[off:off + p.size].reshape(p.shape))
        off += p.size
    return outs


def kernel(x, norm_pre, norm_post, gla_w_in, gla_w_gate2, gla_b_gate, gla_o_gain, gla_w_out, sgu_w_in, sgu_ln_gain, sgu_ln_bias, sgu_w_spatial, sgu_b_spatial, sgu_w_out, loss_target, m_norm_pre, m_norm_post, m_gla_w_in, m_gla_w_gate2, m_gla_b_gate, m_gla_o_gain, m_gla_w_out, m_sgu_w_in, m_sgu_ln_gain, m_sgu_ln_bias, m_sgu_w_spatial, m_sgu_b_spatial, m_sgu_w_out, v_norm_pre, v_norm_post, v_gla_w_in, v_gla_w_gate2, v_gla_b_gate, v_gla_o_gain, v_gla_w_out, v_sgu_w_in, v_sgu_ln_gain, v_sgu_ln_bias, v_sgu_w_spatial, v_sgu_b_spatial, v_sgu_w_out):
    me = _index_of(*_place())
    x0 = x.reshape(S, D)
    tgt = loss_target.reshape(S, D)
    npre0, npre1 = norm_pre[0:1], norm_pre[1:2]
    npost0, npost1 = norm_post[0:1], norm_post[1:2]
    ws = sgu_w_spatial[0]
    wst = jnp.transpose(ws, (0, 2, 1))
    bsb = jnp.broadcast_to(sgu_b_spatial[0][:, :, None], (SGU_G, SGU_BLOCK, SGU_GD))
    W_ROWS = D // N_DEV
    IN_COLS_G, IN_COLS_S = GLA_COLS // N_DEV, SGU_COLS // N_DEV

    s_gwi, s_gwo = gla_w_in[0].astype(BF16), gla_w_out[0].astype(BF16)
    s_swi, s_swo = sgu_w_in[0].astype(BF16), sgu_w_out[0].astype(BF16)
    small = jnp.concatenate([jnp.pad(gla_w_gate2[0].reshape(4, 512), ((0, 4), (0, 0))),
                             jnp.pad(jnp.concatenate([sgu_ln_gain, sgu_ln_bias], axis=1), ((0, 7), (0, 0)))], axis=0)

    wg_in, g_small = _gather_first(s_gwi, small, "gather_first")
    w2 =_blocks_to_columns(g_small[:, :4, :].reshape(N_DEV, LR, 128))
    w2p = jnp.pad(w2, ((0, LRP - LR), (0, 0))).astype(BF16)
    lng = g_small[:, 8, :256].reshape(1, D)
    lnb = g_small[:, 8, 256:].reshape(1, D)
    like_gwo, like_swi = _sds((N_DEV, W_ROWS, D), BF16), _sds((N_DEV, D, IN_COLS_S), BF16)

    h0 = _prenorm(x0, npre0)
    proj0, (g_gwo, g_swi) = _mm(h0, wg_in, "nn", F32, tm=1024, tn=896, tk=D, name="gla_in", b_tiled=True, phases=[
        _Phase(like_gwo, None, [_gather_send(s_gwo, 0, W_ROWS)]),
        _Phase(like_swi, None, [_gather_send(s_swi, 0, 768, diagonal=False)])])
    (ypre0, states), (g_gwo, g_swi) = _gla_fwd(proj0, w2p, gla_b_gate, gla_o_gain, phases=[
        _Phase(like_gwo, g_gwo, [_gather_pass(0, W_ROWS)]),
        _Phase(like_swi, g_swi, [_gather_relay(0, 768), _gather_send(s_swi, 768, 512, diagonal=False)])])
    wg_out = g_gwo.reshape(D, D)
    y0, (g_swi,) = _mm(ypre0, wg_out, "nn", F32, tm=1024, tn=1024, tk=D, name="gla_out", phases=[
        _Phase(like_swi, g_swi, [_gather_pass(0, 768), _gather_relay(768, 512), _gather_send(s_swi, 1280, 512, diagonal=False)])])
    (x1, h1), (g_swi,) = _mid_fwd(x0, y0, npost0, npre1, phases=[
        _Phase(like_swi, g_swi, [_gather_pass(768, 512), _gather_relay(1280, 512), _gather_send(s_swi, 1792, 256, diagonal=False)])])
    g_swi, = _carry([_Phase(like_swi, g_swi, [_gather_pass(1280, 512), _gather_relay(1792, 256)])], "relay_sgu_w_in")
    g_swi, = _carry([_Phase(like_swi, g_swi, [_gather_pass(1792, 256)])], "pass_sgu_w_in")
    proj1, (g_swo,) = _mm(h1, g_swi, "nn", F32, tm=1024, tn=IN_COLS_S, tk=D, name="sgu_in", b_blocked=True, phases=[
        _Phase(like_gwo, None, [_gather_send(s_swo, 0, W_ROWS)])])
    (pre1,), (g_swo,) = _sgu_fwd(proj1, lng, lnb, ws, bsb, phases=[_Phase(like_gwo, g_swo, [_gather_pass(0, W_ROWS)])])
    ws_out = g_swo.reshape(D, D)
    loss_cols, dx2, dy1, dnpost1 = _nn_rows(pre1, ws_out, name="sgu_out", row_ins=[x1, tgt], vec_ins=[npost1],
                                            outs=[("vec", F32), ("row", F32), ("row", BF16), ("vec", F32)], tail=_final_tail)
    loss_here = jnp.pad((0.5 * jnp.sum(loss_cols) / D).reshape(1, 1), ((0, 7), (0, 127)))

    like_b_out, like_b_swi = _sds((3, W_ROWS, D), BF16), _sds((3, D, IN_COLS_S), BF16)
    like_b_gwi = _sds((3, D, IN_COLS_G), BF16)
    row_pair = dict(like=_sds((4, W_ROWS, D), BF16), block=lambda i, j: i, ordinal=lambda i, j: i >> 1,
                    dst=lambda ref, k, i, j: ref.at[k])
    col_pair = dict(like=_sds((4, D, IN_COLS_S), BF16), block=lambda i, j: j, ordinal=lambda i, j: 4 * i + (j >> 1),
                    dst=lambda ref, k, i, j: ref.at[k, pl.ds(pl.multiple_of(i * 1024, 1024), 1024)])

    mine = _own_table()
    dws_out, (a_swo,) = _mm(pre1, dy1, "tn", BF16, tm=W_ROWS, tn=D, tk=S, name="sgu_out_dw", pair=row_pair)
    p_swo = dws_out.reshape(N_DEV, W_ROWS, D)
    t_swo, _ = _pair_sum(p_swo, a_swo, 0, W_ROWS, "pair_sum_sgu_w_out", table=mine)
    dpre1, _ = _mm(dy1, ws_out, "nt", F32, tm=1024, tn=1024, tk=D, name="sgu_out_dx")
    (dproj1, dlng, dlnb, dwsp, dbsp), (b_swo,) = _sgu_bwd(proj1, dpre1, lng, lnb, ws, wst, bsb, phases=[
        _Phase(like_b_out, None, [_reduce_cross(t_swo, 0, 0, W_ROWS)])])
    p_swi, (a_swi,) = _mm(h1, dproj1, "tn", BF16, tm=1024, tn=IN_COLS_S, tk=S, name="sgu_in_dw", out_blocked=True, pair=col_pair)
    t_swi, _ = _pair_sum(p_swi, a_swi, 0, D, "pair_sum_sgu_w_in", table=mine)
    (dx1, dy0, dnpre1, dnpost0), (b_swi,) = _nt_rows(
        dproj1, g_swi, tk=IN_COLS_S, name="sgu_in_dx", b_blocked=True, row_ins=[dx2, x1, y0], vec_ins=[npre1, npost0],
        outs=[("row", F32), ("row", BF16), ("vec", F32), ("vec", F32)], tail=_mid_bwd_tail, phases=[
            _Phase(like_b_swi, None, [_reduce_cross(t_swi, 0, 0, 896)])])
    dwg_out, (a_gwo, b_swi) = _mm(ypre0, dy0, "tn", BF16, tm=W_ROWS, tn=D, tk=S, name="gla_out_dw", pair=row_pair, phases=[
        _Phase(like_b_swi, b_swi, [_reduce_cross(t_swi, 896, 896, 256)])])
    p_gwo = dwg_out.reshape(N_DEV, W_ROWS, D)
    t_gwo, _ = _pair_sum(p_gwo, a_gwo, 0, W_ROWS, "pair_sum_gla_w_out", table=mine)
    dypre0, (b_swi,) = _mm(dy0, wg_out, "nt", F32, tm=1024, tn=1024, tk=D, name="gla_out_dx", phases=[
        _Phase(like_b_swi, b_swi, [_reduce_cross(t_swi, 1152, 1152, 256)])])
    late = [dnpre1, dnpost1, dlng, dlnb, dwsp, jnp.transpose(dbsp[:, :SGU_G])]
    late_pack = _pack(late)
    (dproj0, dogain, dbgate, dw2), (b_swi, b_gwo, g_late) = _gla_bwd(proj0, dypre0, states, w2p, gla_b_gate, gla_o_gain, phases=[
        _Phase(like_b_swi, b_swi, [_reduce_cross(t_swi, 1408, 1408, 640)]),
        _Phase(like_b_out, None, [_reduce_cross(t_gwo, 0, 0, W_ROWS)]),
        _Phase(_sds((N_DEV,) + late_pack.shape, F32), None, [_gather_send(late_pack, 0, late_pack.shape[0])])])
    half = D // 2
    dwg_in_a, (g_late,) = _mm(h0, dproj0, "tn", BF16, tm=half, tn=896, tk=S, name="gla_in_dw_a", m_tiles=(0, 1), phases=[
        _Phase(_sds((N_DEV,) + late_pack.shape, F32), g_late, [_gather_pass(0, late_pack.shape[0])])])
    own_gwi, a_gwi = _blockify_pair(dwg_in_a, None, None, 0, "blockify_gla_w_in_a")
    t_gwi_a, _ = _pair_sum(own_gwi, a_gwi, 0, half, "pair_sum_gla_w_in_a")
    dwg_in_b, (b_gwi,) = _mm(h0, dproj0, "tn", BF16, tm=half, tn=896, tk=S, name="gla_in_dw_b", m_tiles=(1, 1), phases=[
        _Phase(like_b_gwi, None, [_reduce_cross(t_gwi_a, 0, 0, 640)])])
    own_gwi, a_gwi = _blockify_pair(dwg_in_b, own_gwi, a_gwi, half, "blockify_gla_w_in_b")
    t_gwi_b, (b_gwi,) = _pair_sum(own_gwi, a_gwi, half, half, "pair_sum_gla_w_in_b", phases=[
        _Phase(like_b_gwi, b_gwi, [_reduce_cross(t_gwi_a, 640, 640, 128)])])
    (grad_x, dnpre0), (b_gwi,) = _nt_rows(
        dproj0, wg_in, tk=896, name="gla_in_dx", b_tiled=True, row_ins=[dx1, x0], vec_ins=[npre0],
        outs=[("row", F32), ("vec", F32)], tail=_first_bwd_tail, phases=[
            _Phase(like_b_gwi, b_gwi, [_reduce_cross(t_gwi_a, 768, 768, 256), _reduce_cross(t_gwi_b, 0, half, half)])])

    early = [dnpre0, dnpost0, dbgate, dogain, dw2[:LR], loss_here]
    early_pack = _pack(early)
    like_early = _sds((N_DEV,) + early_pack.shape, F32)
    (g_swo, d_swo, nm_swo, nv_swo), (g_early,) = _sum_adamw(
        p_swo, a_swo, b_swo, sgu_w_out[0], m_sgu_w_out[0], v_sgu_w_out[0], name="adamw_sgu_w_out", table=mine, phases=[
            _Phase(like_early, None, [_gather_send(early_pack, 0, early_pack.shape[0])])])
    (g_gwo_, d_gwo, nm_gwo, nv_gwo), (g_early,) = _sum_adamw(
        p_gwo, a_gwo, b_gwo, gla_w_out[0], m_gla_w_out[0], v_gla_w_out[0], name="adamw_gla_w_out", table=mine, phases=[
            _Phase(like_early, g_early, [_gather_pass(0, early_pack.shape[0])])])
    (g_swi_, d_swi, nm_swi, nv_swi), _ = _sum_adamw(
        p_swi, a_swi, b_swi, sgu_w_in[0], m_sgu_w_in[0], v_sgu_w_in[0], name="adamw_sgu_w_in", table=mine)
    (g_gwi_, d_gwi, nm_gwi, nv_gwi), _ = _sum_adamw(
        own_gwi, a_gwi, b_gwi, gla_w_in[0], m_gla_w_in[0], v_gla_w_in[0], name="adamw_gla_w_in")

    g_npre1, g_npost1, g_lng_full, g_lnb_full, g_wsp, g_bsp = _unpack(_sum_parts(g_late, "sum_late_small_grads"), late)
    g_npre0, g_npost0, g_bgate, g_ogain, g_w2_full, loss_all = _unpack(_sum_parts(g_early, "sum_early_small_grads"), early)
    loss = loss_all[0, 0]
    g_w2 = lax.dynamic_slice(g_w2_full, (0, me * 128), (LR, 128))
    g_lng = lax.dynamic_slice(g_lng_full, (0, me * 256), (1, 256))
    g_lnb = lax.dynamic_slice(g_lnb_full, (0, me * 256), (1, 256))
    small_g = [jnp.concatenate([g_npre0, g_npre1], 0), jnp.concatenate([g_npost0, g_npost1], 0), g_w2, g_bgate, g_ogain,
               g_lng, g_lnb, g_wsp, g_bsp]
    small_w = [norm_pre, norm_post, gla_w_gate2[0], gla_b_gate, gla_o_gain, sgu_ln_gain, sgu_ln_bias, sgu_w_spatial[0], sgu_b_spatial[0]]
    small_m = [m_norm_pre, m_norm_post, m_gla_w_gate2[0], m_gla_b_gate, m_gla_o_gain, m_sgu_ln_gain, m_sgu_ln_bias, m_sgu_w_spatial[0], m_sgu_b_spatial[0]]
    small_v = [v_norm_pre, v_norm_post, v_gla_w_gate2[0], v_gla_b_gate, v_gla_o_gain, v_sgu_ln_gain, v_sgu_ln_bias, v_sgu_w_spatial[0], v_sgu_b_spatial[0]]
    d_pack, nm_pack, nv_pack = _adamw(_pack(small_w), _pack(small_g), _pack(small_m), _pack(small_v), "adamw_small")

    out_like = [norm_pre, norm_post, gla_w_gate2, gla_b_gate, gla_o_gain, sgu_ln_gain, sgu_ln_bias, sgu_w_spatial, sgu_b_spatial]
    sg_ = [g.reshape(s.shape) for g, s in zip(small_g, out_like)]
    sd_, sm_, sv_ = (_unpack(pk, out_like) for pk in (d_pack, nm_pack, nv_pack))

    def assemble(small_list, w_in_g, w_out_g, w_in_s, w_out_s):
        npre_, npost_, w2_, bg_, og_, lg_, lb_, wsp_, bsp_ = small_list
        return [npre_, npost_, w_in_g[None], w2_, bg_, og_, w_out_g[None], w_in_s[None], lg_, lb_, wsp_, bsp_, w_out_s[None]]

    return (loss, grad_x.reshape(1, S, D),
            *assemble(sg_, g_gwi_, g_gwo_, g_swi_, g_swo),
            *assemble(sd_, d_gwi, d_gwo, d_swi, d_swo),
            *assemble(sm_, nm_gwi, nm_gwo, nm_swi, nm_swo),
            *assemble(sv_, nv_gwi, nv_gwo, nv_swi, nv_swo))
```

```python
import functools

import jax
import jax.numpy as jnp
from jax import lax
from jax.experimental import pallas as pl
from jax.experimental.pallas import tpu as pltpu

F32 = jnp.float32
BF16 = jnp.bfloat16

N_DEV = 8
S = 2048
D = 2048
H = 4
DK = 256
DV = 512
C = 64
NC = S // C
GLA_COLS = 6160
GLA_PAD = 6272
Q0, K0, V0, G0, LR0 = 0, 1024, 2048, 4096, 6144
LR = 16
LRP = 128
SGU_COLS = 6144
SGU_BLOCK = 128
SGU_G = 8
SGU_GD = 256
EPS = 1e-6
GLA_TAU = 16.0

ADAM_LR, ADAM_B1, ADAM_B2, ADAM_EPS, ADAM_WD, ADAM_STEP = 0.001, 0.9, 0.999, 1e-08, 0.01, 10

V7X_VMEM_BYTES = 64 * 1024 * 1024
VMEM_CEILING = V7X_VMEM_BYTES - 6 * 1024 * 1024
MESH = pl.DeviceIdType.MESH
HBM_SPEC = pl.BlockSpec(memory_space=pl.ANY)


def _sds(shape, dtype):
    return jax.ShapeDtypeStruct(tuple(shape), dtype)


def _nbytes(shape, dtype):
    n = 1
    for s in shape:
        n *= s
    return n * jnp.dtype(dtype).itemsize


def _dot(a, b, dims=(((1,), (0,)), ((), ())), precision=None):
    return lax.dot_general(a, b, dims, precision=precision, preferred_element_type=F32)


NN = (((1,), (0,)), ((), ()))
TN = (((0,), (0,)), ((), ()))
NT = (((1,), (1,)), ((), ()))


def _place():
    return lax.axis_index("x"), lax.axis_index("y"), lax.axis_index("c")


def _index_of(px, py, pc):
    return 4 * px + 2 * py + pc


def _chips(x, y):
    return [(1 - x, y), (x, 1 - y), (1 - x, 1 - y)]


def _rcopy(src, dst, send_sem, recv_sem, to):
    return pltpu.make_async_remote_copy(src_ref=src, dst_ref=dst, send_sem=send_sem, recv_sem=recv_sem,
                                        device_id=to, device_id_type=MESH)


class _Move:
    def __init__(self, ins, n_remote, make, stage=None):
        self.ins, self.n_remote, self.make, self.stage = list(ins), n_remote, make, stage

    def scratch(self):
        sems = [pltpu.SemaphoreType.DMA((self.n_remote,)), pltpu.SemaphoreType.DMA((self.n_remote,))]
        return sems if self.stage is None else sems + [pltpu.SemaphoreType.DMA((1,)), pltpu.VMEM(*self.stage)]

    def start(self, in_refs, buf, scratch):
        sends, _, local = self.make(in_refs, buf, scratch[0], scratch[1])
        if local is not None:
            pltpu.make_async_copy(local[0], scratch[3], scratch[2].at[0]).start()
        for cp in sends:
            cp.start()

    def finish(self, in_refs, buf, scratch):
        sends, arrivals, local = self.make(in_refs, buf, scratch[0], scratch[1])
        if local is not None:
            pltpu.make_async_copy(local[0], scratch[3], scratch[2].at[0]).wait()
            out = pltpu.make_async_copy(scratch[3], local[1], scratch[2].at[0])
            out.start()
        for cp in arrivals:
            cp.wait_recv()
        for cp in sends:
            cp.wait_send()
        if local is not None:
            out.wait()


class _Phase:
    def __init__(self, like, so_far, moves):
        self.like, self.so_far, self.moves = like, so_far, list(moves)


def _gather_send(shard, r0, nr, diagonal=True):
    def make(in_refs, g, ss, rs):
        sh, = in_refs
        x, y, c = _place()
        me = _index_of(x, y, c)
        rows = pl.ds(r0, nr)
        peers = [(x, y, 1 - c)] + [(px, py, c) for px, py in _chips(x, y)[:3 if diagonal else 2]]
        sends = [_rcopy(sh.at[rows], g.at[me, rows], ss.at[k], rs.at[k], p) for k, p in enumerate(peers)]
        arrivals = [_rcopy(sh.at[rows], g.at[_index_of(*p), rows], ss.at[k], rs.at[k], p) for k, p in enumerate(peers)]
        return sends, arrivals, (sh.at[rows], g.at[me, rows])

    return _Move([shard], 4 if diagonal else 3, make, stage=((nr, shard.shape[1]), shard.dtype))


def _gather_relay(r0, nr):
    def make(in_refs, g, ss, rs):
        x, y, c = _place()
        nx, ny, nd = [(px, py, c) for px, py in _chips(x, y)]
        first, second = pl.ds(r0, nr // 2), pl.ds(r0 + nr // 2, nr // 2)
        sends = [_rcopy(g.at[_index_of(*nx), first], g.at[_index_of(*nx), first], ss.at[0], rs.at[0], ny),
                 _rcopy(g.at[_index_of(*ny), second], g.at[_index_of(*ny), second], ss.at[1], rs.at[1], nx)]
        arrivals = [_rcopy(g.at[_index_of(*nx), first], g.at[_index_of(*nd), first], ss.at[0], rs.at[0], ny),
                    _rcopy(g.at[_index_of(*ny), second], g.at[_index_of(*nd), second], ss.at[1], rs.at[1], nx)]
        return sends, arrivals, None

    return _Move([], 2, make)


def _gather_pass(r0, nr):
    def make(in_refs, g, ss, rs):
        x, y, c = _place()
        rows = pl.ds(r0, nr)
        sends = [_rcopy(g.at[_index_of(px, py, c), rows], g.at[_index_of(px, py, c), rows], ss.at[j], rs.at[j], (x, y, 1 - c))
                 for j, (px, py) in enumerate(_chips(x, y))]
        arrivals = [_rcopy(g.at[_index_of(px, py, c), rows], g.at[_index_of(px, py, 1 - c), rows], ss.at[j], rs.at[j], (x, y, 1 - c))
                    for j, (px, py) in enumerate(_chips(x, y))]
        return sends, arrivals, None

    return _Move([], 3, make)


def _own_table():
    x, y, c = _place()
    return jnp.stack([_index_of(px, py, c) for px, py in [(x, y)] + _chips(x, y)]).astype(jnp.int32)


def _blockify_pair(dw, own_so_far, a_so_far, dst_r0, name):
    rows, tr, cw, win = dw.shape[0], 256, GLA_COLS // N_DEV, 896
    n_steps = rows // tr

    def body(*refs):
        x_ref, own_ref, a_ref, stage_ref, send_sems, recv_sem = refs[0], *refs[-5:]
        i = pl.program_id(0)
        x, y, c = _place()

        def send(slot, k):
            dst = a_ref.at[k, pl.ds(pl.multiple_of(dst_r0 + i * tr, tr), tr)]
            return _rcopy(stage_ref.at[slot], dst, send_sems.at[slot], recv_sem.at[0], (x, y, 1 - c))

        for j in range(N_DEV):
            window = x_ref[:, 768 * j:768 * j + win].astype(F32)
            tile = (pltpu.roll(window, win - 2 * j, 1) if j else window)[:, :cw].astype(BF16)
            k = ((j >> 2) ^ x) + 2 * (((j >> 1) & 1) ^ y)

            @pl.when((j & 1) == c)
            def _():
                own_ref[k] = tile

            @pl.when((j & 1) != c)
            def _():
                slot = (j >> 1) & 1
                if j >> 1 >= 2:
                    send(slot, k).wait_send()
                else:
                    pl.when(i > 0)(lambda: send(slot, k).wait_send())
                stage_ref[slot] = tile
                send(slot, k).start()

        @pl.when(i == n_steps - 1)
        def _():
            send(0, 0).wait_send()
            send(1, 0).wait_send()
            arrived = a_ref.at[:, pl.ds(dst_r0, rows)]
            _rcopy(arrived, arrived, send_sems.at[0], recv_sem.at[0], (x, y, 1 - c)).wait_recv()

    continues = a_so_far is not None
    own, a = pl.pallas_call(
        body, grid=(n_steps,),
        in_specs=[pl.BlockSpec((tr, GLA_PAD), lambda i: (i, 0))] + [HBM_SPEC] * (2 * continues),
        out_specs=[pl.BlockSpec((4, tr, cw), lambda i: (0, dst_r0 // tr + i, 0)), HBM_SPEC],
        out_shape=[_sds((4, D, cw), BF16), _sds((4, D, cw), BF16)],
        scratch_shapes=[pltpu.VMEM((2, tr, cw), BF16), pltpu.SemaphoreType.DMA((2,)), pltpu.SemaphoreType.DMA((1,))],
        input_output_aliases={1: 0, 2: 1} if continues else {},
        compiler_params=pltpu.CompilerParams(dimension_semantics=("arbitrary",), vmem_limit_bytes=48 * 1024 * 1024),
        name=name,
    )(*([dw] + [own_so_far, a_so_far] * continues))
    return own, a


def _reduce_cross(sums, src_r0, dst_r0, nr):
    def make(in_refs, b, ss, rs):
        t, = in_refs
        x, y, c = _place()
        src, dst = pl.ds(src_r0, nr), pl.ds(dst_r0, nr)
        sends = [_rcopy(t.at[j, src], b.at[j, dst], ss.at[j], rs.at[j], (px, py, c)) for j, (px, py) in enumerate(_chips(x, y))]
        return sends, sends, None

    return _Move([sums], 3, make)


def _hosted(body, *, name, grid, in_specs, out_specs, out_shape, args, scratch_shapes=(), block_bytes, scratch_bytes=0,
            phases=(), table=None):
    n_in, n_out, n_scr = len(args), len(out_shape), len(scratch_shapes)
    all_args, all_out_shape, sems, aliases, layout = list(args), list(out_shape), [], {}, []
    for j, ph in enumerate(phases):
        counts = []
        for mv in ph.moves:
            all_args += mv.ins
            counts.append(len(mv.ins))
            sems += mv.scratch()
        if ph.so_far is not None:
            aliases[len(all_args)] = n_out + j
            all_args.append(ph.so_far)
        layout.append((counts, ph.so_far is not None))
        all_out_shape.append(ph.like)
    n_extra_in = len(all_args) - n_in

    def wrapped(*refs):
        ins, pos = refs[:n_in], n_in
        move_ins = []
        for counts, continues in layout:
            per_move = []
            for cnt in counts:
                per_move.append(refs[pos:pos + cnt])
                pos += cnt
            pos += continues
            move_ins.append(per_move)
        outs = refs[pos:pos + n_out]
        bufs = refs[pos + n_out:pos + n_out + len(phases)]
        pos += n_out + len(phases)
        scratch = refs[pos:pos + n_scr]
        pos += n_scr
        move_sems = []
        for ph in phases:
            per_move = []
            for mv in ph.moves:
                count = len(mv.scratch())
                per_move.append(refs[pos:pos + count])
                pos += count
            move_sems.append(per_move)

        def each_move(fn_name):
            for ph, buf, per_in, per_sem in zip(phases, bufs, move_ins, move_sems):
                for mv, mv_in, mv_sem in zip(ph.moves, per_in, per_sem):
                    getattr(mv, fn_name)(mv_in, buf, mv_sem)

        if phases:
            first = functools.reduce(jnp.logical_and, [pl.program_id(a) == 0 for a in range(len(grid))])
            last = functools.reduce(jnp.logical_and, [pl.program_id(a) == grid[a] - 1 for a in range(len(grid))])
            pl.when(first)(lambda: each_move("start"))
        body(*ins, *outs, *scratch)
        if phases:
            pl.when(last)(lambda: each_move("finish"))

    all_args = [pltpu.with_memory_space_constraint(a, pltpu.HBM) for a in all_args]
    est = 2 * block_bytes + scratch_bytes
    params = pltpu.CompilerParams(dimension_semantics=("arbitrary",) * len(grid),
                                  vmem_limit_bytes=min(VMEM_CEILING, max(32 * 1024 * 1024, 2 * est)))
    all_in_specs, all_out_specs = list(in_specs) + [HBM_SPEC] * n_extra_in, list(out_specs) + [HBM_SPEC] * len(phases)
    if table is None:
        results = pl.pallas_call(
            wrapped, grid=grid, in_specs=all_in_specs, out_specs=all_out_specs, out_shape=all_out_shape,
            scratch_shapes=list(scratch_shapes) + sems, input_output_aliases=aliases, compiler_params=params, name=name,
        )(*all_args)
    else:
        results = pl.pallas_call(
            lambda table_ref, *refs: wrapped(*refs),
            grid_spec=pltpu.PrefetchScalarGridSpec(num_scalar_prefetch=1, grid=grid, in_specs=all_in_specs, out_specs=all_out_specs,
                                                   scratch_shapes=list(scratch_shapes) + sems),
            out_shape=all_out_shape, input_output_aliases={k + 1: v for k, v in aliases.items()}, compiler_params=params, name=name,
        )(table, *all_args)
    return list(results[:n_out]), list(results[n_out:])


class _Both:
    def __init__(self, copies):
        self.copies = copies

    def start(self):
        for cp in self.copies:
            cp.start()

    def wait_send(self):
        for cp in self.copies:
            cp.wait_send()

    def wait_recv(self):
        for cp in self.copies:
            cp.wait_recv()


def _carry(phases, name):
    def body(o_ref):
        o_ref[...] = jnp.zeros_like(o_ref)

    _, bufs = _hosted(body, name=name, grid=(1,), in_specs=[], out_specs=[pl.BlockSpec((8, 128), lambda i: (0, 0))],
                      out_shape=[_sds((8, 128), F32)], args=[], block_bytes=8 * 128 * 4, phases=phases)
    return bufs


def _gather_first(shard, small, name):
    cw, tr, n_tiles = shard.shape[1], 256, GLA_PAD // 128

    def body(sh_ref, sm_ref, wn_ref, g_ref, gs_ref, wt_ref, win_ref, tmp_ref, send_sems, recv_sems, local_sems):
        x, y, c = _place()
        me, sibling = (x, y, c), (x, y, 1 - c)
        chips = _chips(x, y)

        def copy(base, out_ref, k, block, to, src=None):
            dst = out_ref.at[_index_of(*block)]
            return _rcopy(dst if src is None else src, dst, send_sems.at[base + k], recv_sems.at[base + k], to)

        icopy = functools.partial(copy, 0, g_ref)
        scopy = functools.partial(copy, 8, gs_ref)

        def wcopy(k, block, to, src=None):
            if k in (1, 2):
                return icopy(k, block, to, src)
            halves = []
            for part, sem in enumerate((k, {0: 15, 4: 16, 5: 17, 6: 18}[k])):
                rows = pl.ds(part * (D // 2), D // 2)
                dst = g_ref.at[_index_of(*block), rows]
                halves.append(_rcopy(dst if src is None else src.at[rows], dst, send_sems.at[sem], recv_sems.at[sem], to))
            return _Both(halves)

        def relay(k, block, half, to):
            rows = pl.ds(half * (D // 2), D // 2)
            ref = g_ref.at[_index_of(*block), rows]
            return _rcopy(ref, ref, send_sems.at[k], recv_sems.at[k], to)

        def load(src_ref, slot):
            cp = pltpu.make_async_copy(src_ref, win_ref.at[slot], local_sems.at[0])
            cp.start()
            cp.wait()

        def place(slot, block):
            b = _index_of(*block)

            def rows_chunk(r, carry):
                rows = pl.ds(pl.multiple_of(r * tr, tr), tr)
                tmp_ref[:, :cw] = win_ref[slot, rows, :].astype(F32)
                shifted = pltpu.roll(tmp_ref[...], 2 * b, 1)
                for u in range(7):
                    wt_ref[6 * b + u, rows, :] = (wt_ref[6 * b + u, rows, :].astype(F32) + shifted[:, 128 * u:128 * (u + 1)]).astype(BF16)
                return carry

            lax.fori_loop(0, D // tr, rows_chunk, 0)

        small_own = pltpu.make_async_copy(sm_ref, gs_ref.at[_index_of(*me)], local_sems.at[1])
        small_own.start()
        first = [wcopy(1 + j, me, (*chip, c), src=sh_ref) for j, chip in enumerate(chips[:2])]
        first += [scopy(0, me, sibling, src=sm_ref)] + [scopy(1 + j, me, (*chip, c), src=sm_ref) for j, chip in enumerate(chips)]
        for cp in first:
            cp.start()

        def clear(t, carry):
            wt_ref[t] = jnp.zeros((D, 128), BF16)
            return carry

        lax.fori_loop(0, n_tiles, clear, 0)
        tmp_ref[...] = jnp.zeros_like(tmp_ref)

        def emit(t):
            pltpu.make_async_copy(wt_ref.at[t], wn_ref.at[t], local_sems.at[2]).start()

        def take(block, slot, arrivals=None, pass_on=None):
            for cp in arrivals or ():
                cp.wait_recv()
            load(sh_ref if arrivals is None else g_ref.at[_index_of(*block)], slot)
            if pass_on is not None:
                pass_on.start()
            place(slot, block)
            for u in range(1, 6):
                emit(6 * _index_of(*block) + u)

        near_x, near_y, far = [(*chip, c) for chip in chips]
        to_sibling = wcopy(0, me, sibling, src=win_ref.at[0])
        pass_x = wcopy(4, near_x, sibling, src=win_ref.at[1])
        pass_y = wcopy(5, near_y, sibling, src=win_ref.at[0])
        pass_d = wcopy(6, far, sibling, src=win_ref.at[0])
        relays = [relay(3, near_x, 0, near_y), relay(7, near_y, 1, near_x)]
        take(me, 0, pass_on=to_sibling)
        take(near_x, 1, [wcopy(1, near_x, me)], pass_x)
        relays[0].start()
        to_sibling.wait_send()
        take(near_y, 0, [wcopy(2, near_y, me)], pass_y)
        relays[1].start()
        small_passed = []
        for j, chip in enumerate(chips):
            scopy(1 + j, (*chip, c), me).wait_recv()
            cp = scopy(4 + j, (*chip, c), sibling)
            cp.start()
            small_passed.append(cp)
        pass_x.wait_send()
        take(sibling, 1, [wcopy(0, sibling, me)])
        pass_y.wait_send()
        take((*chips[0], 1 - c), 0, [wcopy(4, (*chips[0], 1 - c), me)])
        take((*chips[1], 1 - c), 1, [wcopy(5, (*chips[1], 1 - c), me)])
        take(far, 0, [relay(3, far, 0, near_y), relay(7, far, 1, near_x)], pass_d)
        take((*chips[2], 1 - c), 1, [wcopy(6, (*chips[2], 1 - c), me)])
        for t in range(0, n_tiles, 6):
            emit(t)
        scopy(0, sibling, me).wait_recv()
        for j, chip in enumerate(chips):
            scopy(4 + j, (*chip, 1 - c), me).wait_recv()
        for cp in first + small_passed + relays + [pass_d]:
            cp.wait_send()
        small_own.wait()
        pltpu.make_async_copy(wn_ref, wn_ref, local_sems.at[2]).wait()

    wn, _, gs = pl.pallas_call(
        body,
        in_specs=[HBM_SPEC] * 2, out_specs=[HBM_SPEC] * 3,
        out_shape=[_sds((n_tiles, D, 128), BF16), _sds((N_DEV,) + shard.shape, BF16), _sds((N_DEV,) + small.shape, small.dtype)],
        scratch_shapes=[pltpu.VMEM((n_tiles, D, 128), BF16), pltpu.VMEM((2, D, cw), BF16), pltpu.VMEM((tr, 7 * 128), F32),
                        pltpu.SemaphoreType.DMA((19,)), pltpu.SemaphoreType.DMA((19,)), pltpu.SemaphoreType.DMA((3,))],
        compiler_params=pltpu.CompilerParams(vmem_limit_bytes=48 * 1024 * 1024),
        name=name,
    )(shard, small)
    return wn, gs


def _mm(a, b, mode, out_dtype, *, tm, tn, tk, name, b_blocked=False, b_tiled=False, out_blocked=False, m_tiles=None, pair=None,
        phases=()):
    if mode == "nn":
        (m, k), dims = a.shape, NN
        a_blk, a_map = (tm, tk), (lambda i, j, kk: (i, kk))
        if b_blocked:
            assert b.shape[1] == k and b.shape[2] == tn and tk == k
            n = b.shape[0] * tn
            b_spec = pl.BlockSpec((None, tk, tn), lambda i, j, kk: (j, kk, 0))
        elif b_tiled:
            assert b.shape[1] == k and b.shape[2] == 128 and tn % 128 == 0
            n = b.shape[0] * 128
            b_spec = pl.BlockSpec((tn // 128, tk, 128), lambda i, j, kk: (j, kk, 0))
        else:
            assert b.shape[0] == k
            n = b.shape[1]
            b_spec = pl.BlockSpec((tk, tn), lambda i, j, kk: (kk, j))
    elif mode == "tn":
        (k, m), n, dims = a.shape, b.shape[1], TN
        assert b.shape[0] == k
        first = 0 if m_tiles is None else m_tiles[0]
        a_blk, a_map = (tk, tm), (lambda i, j, kk: (kk, i + first))
        b_spec = pl.BlockSpec((tk, tn), lambda i, j, kk: (kk, j))
    else:
        (m, k), dims = a.shape, NT
        a_blk, a_map = (tm, tk), (lambda i, j, kk: (i, kk))
        if b_blocked:
            assert b.shape[0] * b.shape[2] == k and b.shape[2] == tk
            n = b.shape[1]
            b_spec = pl.BlockSpec((None, tn, tk), lambda i, j, kk: (kk, j, 0))
        elif b_tiled:
            assert b.shape[0] * 128 == k and b.shape[2] == 128 and tk % 128 == 0
            n = b.shape[1]
            b_spec = pl.BlockSpec((tk // 128, tn, 128), lambda i, j, kk: (kk, j, 0))
        else:
            assert b.shape[1] == k
            n = b.shape[0]
            b_spec = pl.BlockSpec((tn, tk), lambda i, j, kk: (j, kk))
    assert m % tm == 0 and n % tn == 0 and k % tk == 0, (a.shape, b.shape, mode)
    nk = k // tk
    n_row_tiles = m // tm if m_tiles is None else m_tiles[1]
    if out_blocked:
        out_shape, out_spec = _sds((n // tn, n_row_tiles * tm, tn), out_dtype), pl.BlockSpec((None, tm, tn), lambda i, j, kk: (j, i, 0))
    else:
        out_shape, out_spec = _sds((n_row_tiles * tm, n), out_dtype), pl.BlockSpec((tm, tn), lambda i, j, kk: (i, j))

    grid = (n_row_tiles, n // tn, nk)

    def body(a_ref, b_ref, o_ref, *rest):
        rhs = jnp.concatenate([b_ref[u] for u in range(b_ref.shape[0])], axis=1) if b_tiled else b_ref[...]
        p = _dot(a_ref[...], rhs, dims)
        if nk == 1:
            o_ref[...] = p.astype(out_dtype)
            if pair is not None:
                _send_to_sibling(p.astype(out_dtype), *rest)
        else:
            acc_ref, = rest
            kk = pl.program_id(2)

            @pl.when(kk == 0)
            def _():
                acc_ref[...] = p

            @pl.when(kk > 0)
            def _():
                acc_ref[...] += p

            @pl.when(kk == nk - 1)
            def _():
                o_ref[...] = acc_ref[...].astype(out_dtype)

    def _send_to_sibling(tile, pair_ref, stage_ref, send_sems, recv_sem):
        i, j = pl.program_id(0), pl.program_id(1)
        x, y, c = _place()
        blk = pair["block"](i, j)
        k = ((blk >> 2) ^ x) + 2 * (((blk >> 1) & 1) ^ y)
        ordinal = pair["ordinal"](i, j)

        def send(slot):
            return _rcopy(stage_ref.at[slot], pair["dst"](pair_ref, k, i, j), send_sems.at[slot], recv_sem.at[0], (x, y, 1 - c))

        @pl.when((blk & 1) != c)
        def _():
            slot = ordinal & 1

            @pl.when(ordinal >= 2)
            def _():
                send(slot).wait_send()

            stage_ref[slot] = tile
            send(slot).start()

        @pl.when((i == grid[0] - 1) & (j == grid[1] - 1))
        def _():
            send(0).wait_send()
            send(1).wait_send()
            _rcopy(pair_ref, pair_ref, send_sems.at[0], recv_sem.at[0], (x, y, 1 - c)).wait_recv()

    blocks = _nbytes(a_blk, a.dtype) + tk * tn * jnp.dtype(b.dtype).itemsize + _nbytes((tm, tn), out_dtype)
    out_specs, out_shapes, scratch = [out_spec], [out_shape], [] if nk == 1 else [pltpu.VMEM((tm, tn), F32)]
    scratch_bytes = _nbytes((tm, tn), F32) * (nk > 1)
    if pair is not None:
        assert nk == 1
        out_specs, out_shapes = out_specs + [HBM_SPEC], out_shapes + [pair["like"]]
        scratch = [pltpu.VMEM((2, tm, tn), out_dtype), pltpu.SemaphoreType.DMA((2,)), pltpu.SemaphoreType.DMA((1,))]
        scratch_bytes = 2 * _nbytes((tm, tn), out_dtype)
    outs, bufs = _hosted(
        body, name=name, grid=grid,
        in_specs=[pl.BlockSpec(a_blk, a_map), b_spec], out_specs=out_specs, out_shape=out_shapes, args=[a, b],
        scratch_shapes=scratch, block_bytes=blocks, scratch_bytes=scratch_bytes, phases=phases)
    return outs[0], outs[1:] + bufs


NT_ROWS_TM = 512
NT_ROWS_SUB = 128


def _nt_rows(a, b, *, tk, name, row_ins, vec_ins, outs, tail, b_blocked=False, b_tiled=False, phases=()):
    m, k = a.shape
    tm, nk = NT_ROWS_TM, k // tk
    if b_blocked:
        assert b.shape[0] * b.shape[2] == k and b.shape[2] == tk and b.shape[1] == D
        b_spec = pl.BlockSpec((None, D, tk), lambda i, kk: (kk, 0, 0))
    else:
        assert b_tiled and b.shape[0] * 128 == k and tk % 128 == 0 and b.shape[1] == D
        b_spec = pl.BlockSpec((tk // 128, D, 128), lambda i, kk: (kk, 0, 0))
    row_spec, vec_spec = pl.BlockSpec((tm, D), lambda i, kk: (i, 0)), pl.BlockSpec((1, D), lambda i, kk: (0, 0))
    n_row, n_vec, n_out = len(row_ins), len(vec_ins), len(outs)
    assert nk >= 2

    def body(a_ref, b_ref, *rest):
        row_hbm, vec_refs = rest[:n_row], rest[n_row:n_row + n_vec]
        out_refs = rest[n_row + n_vec:n_row + n_vec + n_out]
        acc_ref, row_sems = rest[n_row + n_vec + n_out], rest[-1]
        row_refs = rest[n_row + n_vec + n_out + 1:-1]
        rhs = jnp.concatenate([b_ref[u] for u in range(b_ref.shape[0])], axis=1) if b_tiled else b_ref[...]
        p = _dot(a_ref[...], rhs, NT)
        i, kk = pl.program_id(0), pl.program_id(1)

        def fetch(r):
            return pltpu.make_async_copy(row_hbm[r].at[pl.ds(pl.multiple_of(i * tm, tm), tm)], row_refs[r], row_sems.at[r])

        @pl.when(kk == 0)
        def _():
            for r in range(n_row):
                fetch(r).start()
            acc_ref[...] = p

        @pl.when(kk > 0)
        def _():
            acc_ref[...] += p

        @pl.when(kk == nk - 1)
        def _():
            for r in range(n_row):
                fetch(r).wait()
            for s in range(tm // NT_ROWS_SUB):
                rows = slice(s * NT_ROWS_SUB, (s + 1) * NT_ROWS_SUB)
                tail(acc_ref[rows, :], rows, (i == 0) if s == 0 else None, row_refs, vec_refs, out_refs)

    out_specs = [row_spec if kind == "row" else vec_spec for kind, _ in outs]
    out_shape = [_sds((m, D) if kind == "row" else (1, D), dt) for kind, dt in outs]
    blocks = tm * tk * 2 + D * tk * 2 + sum(tm * D * jnp.dtype(dt).itemsize for kind, dt in outs if kind == "row")
    scratch = [pltpu.VMEM((tm, D), F32)] + [pltpu.VMEM((tm, D), x.dtype) for x in row_ins] + [pltpu.SemaphoreType.DMA((n_row,))]
    return _hosted(body, name=name, grid=(m // tm, nk), in_specs=[pl.BlockSpec((tm, tk), lambda i, kk: (i, kk)), b_spec]
                   + [HBM_SPEC] * n_row + [vec_spec] * n_vec, out_specs=out_specs, out_shape=out_shape,
                   args=[a, b] + list(row_ins) + list(vec_ins), scratch_shapes=scratch,
                   block_bytes=blocks, scratch_bytes=(1 + n_row) * tm * D * 4, phases=phases)


def _nn_rows(a, b, *, name, row_ins, vec_ins, outs, tail):
    m, k = a.shape
    tm = NT_ROWS_TM
    assert b.shape == (k, D)
    row_spec, vec_spec = pl.BlockSpec((tm, D), lambda i: (i, 0)), pl.BlockSpec((1, D), lambda i: (0, 0))
    n_row, n_vec, n_out = len(row_ins), len(vec_ins), len(outs)

    def body(a_ref, b_ref, *rest):
        row_hbm, vec_refs = rest[:n_row], rest[n_row:n_row + n_vec]
        out_refs = rest[n_row + n_vec:n_row + n_vec + n_out]
        d_ref, row_sems = rest[n_row + n_vec + n_out], rest[-1]
        row_refs = rest[n_row + n_vec + n_out + 1:-1]
        i = pl.program_id(0)
        fetches = [pltpu.make_async_copy(row_hbm[r].at[pl.ds(pl.multiple_of(i * tm, tm), tm)], row_refs[r], row_sems.at[r])
                   for r in range(n_row)]
        for cp in fetches:
            cp.start()
        d_ref[...] = _dot(a_ref[...], b_ref[...])
        for cp in fetches:
            cp.wait()
        for s in range(tm // NT_ROWS_SUB):
            rows = slice(s * NT_ROWS_SUB, (s + 1) * NT_ROWS_SUB)
            tail(d_ref[rows, :], rows, (i == 0) if s == 0 else None, row_refs, vec_refs, out_refs)

    out_specs = [row_spec if kind == "row" else vec_spec for kind, _ in outs]
    out_shape = [_sds((m, D) if kind == "row" else (1, D), dt) for kind, dt in outs]
    blocks = tm * k * 2 + k * D * 2 + sum(tm * D * jnp.dtype(dt).itemsize for kind, dt in outs if kind == "row")
    scratch = [pltpu.VMEM((tm, D), F32)] + [pltpu.VMEM((tm, D), x.dtype) for x in row_ins] + [pltpu.SemaphoreType.DMA((n_row,))]
    outs_, _ = _hosted(body, name=name, grid=(m // tm,), in_specs=[pl.BlockSpec((tm, k), lambda i: (i, 0)),
                                                                 pl.BlockSpec((k, D), lambda i: (0, 0))]
                       + [HBM_SPEC] * n_row + [vec_spec] * n_vec, out_specs=out_specs, out_shape=out_shape,
                       args=[a, b] + list(row_ins) + list(vec_ins), scratch_shapes=scratch,
                       block_bytes=blocks, scratch_bytes=(1 + n_row) * tm * D * 4)
    return outs_


def _vec_add(ref, value, first):
    if first is None:
        ref[...] += value
    else:
        pl.when(first)(lambda: ref.__setitem__(Ellipsis, value))
        pl.when(jnp.logical_not(first))(lambda: ref.__setitem__(Ellipsis, ref[...] + value))


RB = 256


def _row_spec(width):
    return pl.BlockSpec((RB, width), lambda i: (i, 0))


def _vec_spec(width):
    return pl.BlockSpec((1, width), lambda i: (0, 0))


def _rinv(x):
    return lax.rsqrt(jnp.mean(x * x, axis=-1, keepdims=True) + EPS)


def _norm_bwd(dyn, xhat, r):
    return r * (dyn - xhat * jnp.mean(dyn * xhat, axis=-1, keepdims=True))


def _colsum(x):
    return jnp.sum(x, axis=0, keepdims=True)


def _prenorm(x, gain):
    def body(x_ref, g_ref, h_ref):
        xv = x_ref[...]
        h_ref[...] = (xv * _rinv(xv) * g_ref[...]).astype(BF16)

    outs, _ = _hosted(body, name="prenorm", grid=(S // RB,), in_specs=[_row_spec(D), _vec_spec(D)], out_specs=[_row_spec(D)],
                      out_shape=[_sds((S, D), BF16)], args=[x, gain], block_bytes=RB * D * 6)
    return outs[0]


def _mid_fwd(x, y, npost, npre, phases=()):
    def body(x_ref, y_ref, po_ref, pr_ref, x1_ref, h1_ref):
        yv = y_ref[...]
        x1 = x_ref[...] + yv * _rinv(yv) * po_ref[...]
        x1_ref[...] = x1
        h1_ref[...] = (x1 * _rinv(x1) * pr_ref[...]).astype(BF16)

    return _hosted(body, name="mid_fwd", grid=(S // RB,), in_specs=[_row_spec(D), _row_spec(D), _vec_spec(D), _vec_spec(D)],
                   out_specs=[_row_spec(D), _row_spec(D)], out_shape=[_sds((S, D), F32), _sds((S, D), BF16)],
                   args=[x, y, npost, npre], block_bytes=RB * D * 14, phases=phases)


def _final_tail(yv, rows, first, row_refs, vec_refs, out_refs):
    (x_ref, t_ref), (po_ref,), (loss_ref, dx_ref, dy_ref, dpo_ref) = row_refs, vec_refs, out_refs
    r = _rinv(yv)
    yhat = yv * r
    err = x_ref[rows, :] + yhat * po_ref[...] - t_ref[rows, :]
    dx = err * (1.0 / D)
    dx_ref[rows, :] = dx
    dy_ref[rows, :] = _norm_bwd(dx * po_ref[...], yhat, r).astype(BF16)
    _vec_add(loss_ref, _colsum(err * err), first)
    _vec_add(dpo_ref, _colsum(dx * yhat), first)


def _mid_bwd_tail(dh, rows, first, row_refs, vec_refs, out_refs):
    (dx2_ref, x_ref, y_ref), (pr_ref, po_ref), (dx1_ref, dy_ref, dpr_ref, dpo_ref) = row_refs, vec_refs, out_refs
    xv = x_ref[rows, :]
    r = _rinv(xv)
    xhat = xv * r
    dx1 = dx2_ref[rows, :] + _norm_bwd(dh * pr_ref[...], xhat, r)
    dx1_ref[rows, :] = dx1
    yv = y_ref[rows, :]
    ry = _rinv(yv)
    yhat = yv * ry
    dy_ref[rows, :] = _norm_bwd(dx1 * po_ref[...], yhat, ry).astype(BF16)
    _vec_add(dpr_ref, _colsum(dh * xhat), first)
    _vec_add(dpo_ref, _colsum(dx1 * yhat), first)


def _first_bwd_tail(dh, rows, first, row_refs, vec_refs, out_refs):
    (dx1_ref, x_ref), (pr_ref,), (gx_ref, dpr_ref) = row_refs, vec_refs, out_refs
    xv = x_ref[rows, :]
    r = _rinv(xv)
    xhat = xv * r
    gx_ref[rows, :] = dx1_ref[rows, :] + _norm_bwd(dh * pr_ref[...], xhat, r)
    _vec_add(dpr_ref, _colsum(dh * xhat), first)


GLA_RB = 256
GLA_CPB = GLA_RB // C


def _sigmoid(x):
    return 1.0 / (1.0 + jnp.exp(-x))


def _tri(strict):
    r = lax.broadcasted_iota(jnp.int32, (C, C), 0)
    c = lax.broadcasted_iota(jnp.int32, (C, C), 1)
    return jnp.where(c < r if strict else c <= r, 1.0, 0.0).astype(BF16)


def _tri_dot(tri, x):
    hi = x.astype(BF16)
    lo = (x - hi.astype(F32)).astype(BF16)
    return _dot(tri, hi) + _dot(tri, lo)


def _gla_gates(glr_b, w2, b, tri):
    z = _dot(glr_b, w2) + b
    log_a = (jnp.minimum(z, 0.0) - jnp.log(1.0 + jnp.exp(-jnp.abs(z)))) * (1.0 / GLA_TAU)
    bcum = _tri_dot(tri, log_a)
    b_end = jnp.sum(log_a, axis=0, keepdims=True)
    return z, jnp.exp(b_end - bcum), jnp.exp(b_end)


def _gla_fwd(proj, w2p, bgate, ogain, phases=()):
    def body(p_ref, w2_ref, b_ref, og_ref, y_ref, st_out_ref, st_ref):
        @pl.when(pl.program_id(0) == 0)
        def _():
            st_ref[...] = jnp.zeros_like(st_ref)

        tri = _tri(False)

        def chunk(ci, carry):
            rows = pl.ds(pl.multiple_of(ci * C, C), C)
            glr_b = p_ref[rows, LR0:LR0 + LRP].astype(BF16)
            _, ea_all, dec_all = _gla_gates(glr_b, w2_ref[...], b_ref[...], tri)
            for h in range(H):
                ea, dec = ea_all[:, h * DK:(h + 1) * DK], dec_all[:, h * DK:(h + 1) * DK]
                k_dec = (p_ref[rows, K0 + h * DK:K0 + (h + 1) * DK] * ea).astype(BF16)
                v_b = p_ref[rows, V0 + h * DV:V0 + (h + 1) * DV].astype(BF16)
                st = st_ref[h] * dec + _dot(v_b, k_dec, TN)
                st_ref[h] = st
                st_b = st.astype(BF16)
                st_out_ref[ci, h] = st_b
                q_b = (p_ref[rows, Q0 + h * DK:Q0 + (h + 1) * DK] * (DK ** -0.5)).astype(BF16)
                o = _dot(q_b, st_b, NT)
                on = o * _rinv(o)
                g = p_ref[rows, G0 + h * DV:G0 + (h + 1) * DV]
                y_ref[rows, h * DV:(h + 1) * DV] = (on * og_ref[:, h * DV:(h + 1) * DV] * (g * _sigmoid(g))).astype(BF16)
            return carry

        lax.fori_loop(0, GLA_CPB, chunk, 0, unroll=True)

    blocks = GLA_RB * GLA_PAD * 4 + GLA_RB * D * 2 + GLA_CPB * H * DV * DK * 2
    return _hosted(
        body, name="gla_fwd", grid=(S // GLA_RB,),
        in_specs=[pl.BlockSpec((GLA_RB, GLA_PAD), lambda i: (i, 0)),
                  pl.BlockSpec((LRP, H * DK), lambda i: (0, 0)),
                  pl.BlockSpec((1, H * DK), lambda i: (0, 0)),
                  pl.BlockSpec((1, H * DV), lambda i: (0, 0))],
        out_specs=[pl.BlockSpec((GLA_RB, H * DV), lambda i: (i, 0)),
                   pl.BlockSpec((GLA_CPB, H, DV, DK), lambda i: (i, 0, 0, 0))],
        out_shape=[_sds((S, H * DV), BF16), _sds((NC, H, DV, DK), BF16)],
        args=[proj, w2p, bgate, ogain], scratch_shapes=[pltpu.VMEM((H, DV, DK), F32)],
        block_bytes=blocks, scratch_bytes=H * DV * DK * 4, phases=phases)


def _gla_bwd(proj, dypre, states, w2p, bgate, ogain, phases=()):
    nb = S // GLA_RB

    def body(p_ref, dy_ref, st_blk_ref, st_prev_ref, w2_ref, b_ref, og_ref,
             dp_ref, dog_ref, dbg_ref, dw2_ref, r_ref):
        step = pl.program_id(0)

        @pl.when(step == 0)
        def _():
            r_ref[...] = jnp.zeros_like(r_ref)
            dog_ref[...] = jnp.zeros_like(dog_ref)
            dbg_ref[...] = jnp.zeros_like(dbg_ref)
            dw2_ref[...] = jnp.zeros_like(dw2_ref)

        tri = _tri(False)
        tri_strict = _tri(True)
        has_prev = jnp.where(step < nb - 1, 1.0, 0.0).astype(F32)

        def chunk(ci, st_prev_of):
            rows = pl.ds(ci * C if isinstance(ci, int) else pl.multiple_of(ci * C, C), C)
            glr_b = p_ref[rows, LR0:LR0 + LRP].astype(BF16)
            z, ea_all, dec_all = _gla_gates(glr_b, w2_ref[...], b_ref[...], tri)
            d_a, d_end = [], []
            for h in range(H):
                kcol = slice(h * DK, (h + 1) * DK)
                vcol = slice(h * DV, (h + 1) * DV)
                ea, dec = ea_all[:, kcol], dec_all[:, kcol]
                k_dec = p_ref[rows, K0 + h * DK:K0 + (h + 1) * DK] * ea
                k_dec_b = k_dec.astype(BF16)
                v_b = p_ref[rows, V0 + h * DV:V0 + (h + 1) * DV].astype(BF16)
                q_b = (p_ref[rows, Q0 + h * DK:Q0 + (h + 1) * DK] * (DK ** -0.5)).astype(BF16)
                st_b = st_blk_ref[ci, h]
                o = _dot(q_b, st_b, NT)
                rinv = _rinv(o)
                on = o * rinv
                g = p_ref[rows, G0 + h * DV:G0 + (h + 1) * DV]
                sg = _sigmoid(g)
                og = og_ref[:, vcol]
                dyp = dy_ref[rows, vcol]
                dp_ref[rows, G0 + h * DV:G0 + (h + 1) * DV] = (dyp * (on * og) * (sg * (1.0 + g * (1.0 - sg)))).astype(BF16)
                dpn = dyp * (g * sg)
                dog_ref[:, vcol] += _colsum(dpn * on)
                do_b = _norm_bwd(dpn * og, on, rinv).astype(BF16)
                gt = _dot(do_b, q_b, TN) + r_ref[h]
                gt_b = gt.astype(BF16)
                dp_ref[rows, Q0 + h * DK:Q0 + (h + 1) * DK] = (_dot(do_b, st_b) * (DK ** -0.5)).astype(BF16)
                dkd = _dot(v_b, gt_b)
                dp_ref[rows, V0 + h * DV:V0 + (h + 1) * DV] = _dot(k_dec_b, gt_b, NT).astype(BF16)
                dp_ref[rows, K0 + h * DK:K0 + (h + 1) * DK] = (dkd * ea).astype(BF16)
                d_a.append(dkd * k_dec)
                d_end.append(_colsum(gt * st_prev_of(h)) * dec)
                r_ref[h] = gt * dec
            dla = _tri_dot(tri_strict, jnp.concatenate(d_a, axis=1)) + jnp.concatenate(d_end, axis=1)
            dz = dla * (1.0 / GLA_TAU) * (1.0 - _sigmoid(z))
            dz_b = dz.astype(BF16)
            dbg_ref[...] += _colsum(dz)
            dw2_ref[...] += _dot(glr_b, dz_b, TN)
            dp_ref[rows, LR0:LR0 + LRP] = _dot(dz_b, w2_ref[...], NT).astype(BF16)

        def later_chunk(t, carry):
            ci = GLA_CPB - 1 - t
            chunk(ci, lambda h: st_blk_ref[ci - 1, h].astype(F32))
            return carry

        lax.fori_loop(0, GLA_CPB - 1, later_chunk, 0, unroll=True)
        chunk(0, lambda h: st_prev_ref[0, h].astype(F32) * has_prev)

    blocks = (GLA_RB * GLA_PAD * 4 + GLA_RB * D * 4 + (GLA_CPB + 1) * H * DV * DK * 2 + GLA_RB * GLA_PAD * 2)
    rev = lambda i: nb - 1 - i
    return _hosted(
        body, name="gla_bwd", grid=(nb,),
        in_specs=[pl.BlockSpec((GLA_RB, GLA_PAD), lambda i: (rev(i), 0)),
                  pl.BlockSpec((GLA_RB, H * DV), lambda i: (rev(i), 0)),
                  pl.BlockSpec((GLA_CPB, H, DV, DK), lambda i: (rev(i), 0, 0, 0)),
                  pl.BlockSpec((1, H, DV, DK), lambda i: (jnp.maximum(rev(i) * GLA_CPB - 1, 0), 0, 0, 0)),
                  pl.BlockSpec((LRP, H * DK), lambda i: (0, 0)),
                  pl.BlockSpec((1, H * DK), lambda i: (0, 0)),
                  pl.BlockSpec((1, H * DV), lambda i: (0, 0))],
        out_specs=[pl.BlockSpec((GLA_RB, GLA_PAD), lambda i: (rev(i), 0)),
                   pl.BlockSpec((1, H * DV), lambda i: (0, 0)),
                   pl.BlockSpec((1, H * DK), lambda i: (0, 0)),
                   pl.BlockSpec((LRP, H * DK), lambda i: (0, 0))],
        out_shape=[_sds((S, GLA_PAD), BF16), _sds((1, H * DV), F32), _sds((1, H * DK), F32), _sds((LRP, H * DK), F32)],
        args=[proj, dypre, states, states, w2p, bgate, ogain], scratch_shapes=[pltpu.VMEM((H, DV, DK), F32)],
        block_bytes=blocks, scratch_bytes=H * DV * DK * 4, phases=phases)


SGU_RB = 256
GELU_C = 0.7978845608028654
GELU_A = 0.044715


def _gelu(x):
    return 0.5 * x * (1.0 + jnp.tanh(GELU_C * (x + GELU_A * x * x * x)))


def _gelu_grad(x):
    t = jnp.tanh(GELU_C * (x + GELU_A * x * x * x))
    return 0.5 * (1.0 + t) + 0.5 * x * (1.0 - t * t) * (GELU_C * (1.0 + 3.0 * GELU_A * x * x))


def _causal_mask(transposed=False):
    i = lax.broadcasted_iota(jnp.int32, (SGU_BLOCK, SGU_BLOCK), 1 if transposed else 0)
    j = lax.broadcasted_iota(jnp.int32, (SGU_BLOCK, SGU_BLOCK), 0 if transposed else 1)
    return (i >= C) | (j < C)


def _layer_norm(vf, gain, bias):
    mu = jnp.mean(vf, axis=-1, keepdims=True)
    cen = vf - mu
    rstd = lax.rsqrt(jnp.mean(cen * cen, axis=-1, keepdims=True) + EPS)
    xhat = cen * rstd
    return xhat, rstd, xhat * gain + bias


def _sgu_fwd(proj, lng, lnb, ws, bsb, phases=()):
    def body(p_ref, g_ref, b_ref, ws_ref, bs_ref, o_ref):
        mask = _causal_mask()
        for n in range(SGU_RB // SGU_BLOCK):
            rows = slice(n * SGU_BLOCK, (n + 1) * SGU_BLOCK)
            _, _, vn = _layer_norm(_gelu(p_ref[rows, D:2 * D]), g_ref[...], b_ref[...])
            vn_b = vn.astype(BF16)
            for gi in range(SGU_G):
                cols = slice(gi * SGU_GD, (gi + 1) * SGU_GD)
                w = jnp.where(mask, ws_ref[gi], 0.0).astype(BF16)
                vs = _dot(w, vn_b[:, cols]) + bs_ref[gi]
                gate = p_ref[rows, 2 * D + gi * SGU_GD:2 * D + (gi + 1) * SGU_GD]
                o_ref[rows, cols] = (_gelu(p_ref[rows, cols]) * vs * (gate * _sigmoid(gate))).astype(BF16)

    blocks = SGU_RB * SGU_COLS * 4 + SGU_RB * D * 2 + SGU_G * SGU_BLOCK * (SGU_BLOCK + SGU_GD) * 4
    return _hosted(
        body, name="sgu_fwd", grid=(S // SGU_RB,),
        in_specs=[pl.BlockSpec((SGU_RB, SGU_COLS), lambda i: (i, 0)),
                  pl.BlockSpec((1, D), lambda i: (0, 0)), pl.BlockSpec((1, D), lambda i: (0, 0)),
                  pl.BlockSpec((SGU_G, SGU_BLOCK, SGU_BLOCK), lambda i: (0, 0, 0)),
                  pl.BlockSpec((SGU_G, SGU_BLOCK, SGU_GD), lambda i: (0, 0, 0))],
        out_specs=[pl.BlockSpec((SGU_RB, D), lambda i: (i, 0))], out_shape=[_sds((S, D), BF16)],
        args=[proj, lng, lnb, ws, bsb], block_bytes=blocks, phases=phases)


def _sgu_bwd(proj, dpre, lng, lnb, ws, wst, bsb, phases=()):
    nsteps = S // SGU_RB

    def body(p_ref, d_ref, g_ref, b_ref, ws_ref, wst_ref, bs_ref,
             dp_ref, dg_ref, db_ref, dws_ref, dbs_ref, dvn_ref, dvs_acc_ref):
        step = pl.program_id(0)

        @pl.when(step == 0)
        def _():
            dg_ref[...] = jnp.zeros_like(dg_ref)
            db_ref[...] = jnp.zeros_like(db_ref)
            dws_ref[...] = jnp.zeros_like(dws_ref)
            dvs_acc_ref[...] = jnp.zeros_like(dvs_acc_ref)

        mask = _causal_mask()
        maskt = _causal_mask(transposed=True)
        for n in range(SGU_RB // SGU_BLOCK):
            rows = slice(n * SGU_BLOCK, (n + 1) * SGU_BLOCK)
            v = p_ref[rows, D:2 * D]
            xhat, rstd, vn = _layer_norm(_gelu(v), g_ref[...], b_ref[...])
            vn_b = vn.astype(BF16)
            for gi in range(SGU_G):
                cols = slice(gi * SGU_GD, (gi + 1) * SGU_GD)
                w = jnp.where(mask, ws_ref[gi], 0.0).astype(BF16)
                wt = jnp.where(maskt, wst_ref[gi], 0.0).astype(BF16)
                vs = _dot(w, vn_b[:, cols]) + bs_ref[gi]
                u = p_ref[rows, cols]
                gate = p_ref[rows, 2 * D + gi * SGU_GD:2 * D + (gi + 1) * SGU_GD]
                sg = _sigmoid(gate)
                gu = _gelu(u)
                dpre_g = d_ref[rows, cols]
                t = dpre_g * (gate * sg)
                dp_ref[rows, cols] = (t * vs * _gelu_grad(u)).astype(BF16)
                dp_ref[rows, 2 * D + gi * SGU_GD:2 * D + (gi + 1) * SGU_GD] = (
                    dpre_g * gu * vs * (sg * (1.0 + gate * (1.0 - sg)))).astype(BF16)
                dvs = t * gu
                dvs_b = dvs.astype(BF16)
                dvs_acc_ref[:, cols] += dvs
                dws_ref[gi] += _dot(dvs_b, vn_b[:, cols], NT)
                dvn_ref[:, cols] = _dot(wt, dvs_b)
            dvn = dvn_ref[...]
            dg_ref[...] += _colsum(dvn * xhat)
            db_ref[...] += _colsum(dvn)
            dxh = dvn * g_ref[...]
            dvf = rstd * (dxh - jnp.mean(dxh, axis=-1, keepdims=True) - xhat * jnp.mean(dxh * xhat, axis=-1, keepdims=True))
            dp_ref[rows, D:2 * D] = (dvf * _gelu_grad(v)).astype(BF16)

        @pl.when(step == nsteps - 1)
        def _():
            lane = lax.broadcasted_iota(jnp.int32, (SGU_BLOCK, SGU_BLOCK), 1)
            out = jnp.zeros((SGU_BLOCK, SGU_BLOCK), F32)
            for gi in range(SGU_G):
                out = out + jnp.where(lane == gi, jnp.sum(dvs_acc_ref[:, gi * SGU_GD:(gi + 1) * SGU_GD], axis=1, keepdims=True), 0.0)
                dws_ref[gi] = jnp.where(mask, dws_ref[gi], 0.0)
            dbs_ref[...] = out

    blocks = SGU_RB * SGU_COLS * 6 + SGU_RB * D * 4 + SGU_G * SGU_BLOCK * (3 * SGU_BLOCK + SGU_GD) * 4
    const3 = lambda i: (0, 0, 0)
    return _hosted(
        body, name="sgu_bwd", grid=(nsteps,),
        in_specs=[pl.BlockSpec((SGU_RB, SGU_COLS), lambda i: (i, 0)),
                  pl.BlockSpec((SGU_RB, D), lambda i: (i, 0)),
                  pl.BlockSpec((1, D), lambda i: (0, 0)), pl.BlockSpec((1, D), lambda i: (0, 0)),
                  pl.BlockSpec((SGU_G, SGU_BLOCK, SGU_BLOCK), const3),
                  pl.BlockSpec((SGU_G, SGU_BLOCK, SGU_BLOCK), const3),
                  pl.BlockSpec((SGU_G, SGU_BLOCK, SGU_GD), const3)],
        out_specs=[pl.BlockSpec((SGU_RB, SGU_COLS), lambda i: (i, 0)),
                   pl.BlockSpec((1, D), lambda i: (0, 0)), pl.BlockSpec((1, D), lambda i: (0, 0)),
                   pl.BlockSpec((SGU_G, SGU_BLOCK, SGU_BLOCK), const3),
                   pl.BlockSpec((SGU_BLOCK, SGU_BLOCK), lambda i: (0, 0))],
        out_shape=[_sds((S, SGU_COLS), BF16), _sds((1, D), F32), _sds((1, D), F32),
                   _sds((SGU_G, SGU_BLOCK, SGU_BLOCK), F32), _sds((SGU_BLOCK, SGU_BLOCK), F32)],
        args=[proj, dpre, lng, lnb, ws, wst, bsb],
        scratch_shapes=[pltpu.VMEM((SGU_BLOCK, D), F32), pltpu.VMEM((SGU_BLOCK, D), F32)],
        block_bytes=blocks, scratch_bytes=2 * SGU_BLOCK * D * 4, phases=phases)


def _pair_sum(own, a, r0, nr, name, table=None, phases=()):
    c = own.shape[2]
    tr = 256
    assert r0 % tr == 0 and nr % tr == 0

    def body(own_ref, sib_ref, o_ref):
        o_ref[...] = (own_ref[...].astype(F32) + sib_ref[...].astype(F32)).astype(BF16)

    own_map = ((lambda j, i: (1 + j, r0 // tr + i, 0)) if table is None else
               (lambda j, i, t: (t[1 + j], r0 // tr + i, 0)))
    cpad = -(-c // 128) * 128
    outs, bufs = _hosted(
        body, name=name, grid=(3, nr // tr),
        in_specs=[pl.BlockSpec((None, tr, c), own_map),
                  pl.BlockSpec((None, tr, c), lambda j, i, *t: (1 + j, r0 // tr + i, 0))],
        out_specs=[pl.BlockSpec((None, tr, c), lambda j, i, *t: (j, i, 0))], out_shape=[_sds((3, nr, c), BF16)],
        args=[own, a], block_bytes=3 * tr * cpad * 2, phases=phases, table=table)
    return outs[0], bufs


def _adamw_math(w, g, m, v):
    m = ADAM_B1 * m + (1.0 - ADAM_B1) * g
    v = ADAM_B2 * v + (1.0 - ADAM_B2) * (g * g)
    m_hat = m / (1.0 - ADAM_B1 ** ADAM_STEP)
    v_hat = v / (1.0 - ADAM_B2 ** ADAM_STEP)
    delta = -ADAM_LR * (m_hat / (jnp.sqrt(v_hat) + ADAM_EPS) + ADAM_WD * w)
    return delta, m, v


def _sum_adamw(own, a, b, w, m, v, *, name, phases=(), table=None):
    r, c = w.shape
    tr = 256

    def body(own_ref, sib_ref, far_ref, w_ref, m_ref, v_ref, g_ref, d_ref, nm_ref, nv_ref):
        g = own_ref[...].astype(F32) + sib_ref[...].astype(F32)
        for j in range(3):
            g = g + far_ref[j].astype(F32)
        g_ref[...] = g
        d_ref[...], nm_ref[...], nv_ref[...] = _adamw_math(w_ref[...], g, m_ref[...], v_ref[...])

    spec = pl.BlockSpec((tr, c), lambda i, *t: (i, 0))
    own_map = (lambda i: (0, i, 0)) if table is None else (lambda i, t: (t[0], i, 0))
    cpad = -(-c // 128) * 128
    return _hosted(
        body, name=name, grid=(r // tr,),
        in_specs=[pl.BlockSpec((None, tr, c), own_map), pl.BlockSpec((None, tr, c), lambda i, *t: (0, i, 0)),
                  pl.BlockSpec((3, tr, c), lambda i, *t: (0, i, 0)), spec, spec, spec],
        out_specs=[spec] * 4, out_shape=[_sds((r, c), F32)] * 4, args=[own, a, b, w, m, v],
        block_bytes=5 * tr * cpad * 2 + 7 * tr * cpad * 4, phases=phases, table=table)


def _sum_parts(parts, name):
    n, r, c = parts.shape

    def body(p_ref, o_ref):
        g = p_ref[0]
        for j in range(1, n):
            g = g + p_ref[j]
        o_ref[...] = g

    outs, _ = _hosted(body, name=name, grid=(1,), in_specs=[pl.BlockSpec((n, r, c), lambda i: (0, 0, 0))],
                      out_specs=[pl.BlockSpec((r, c), lambda i: (0, 0))], out_shape=[_sds((r, c), F32)], args=[parts],
                      block_bytes=(n + 1) * r * c * 4)
    return outs[0]


def _adamw(w, g, m, v, name):
    def body(w_ref, g_ref, m_ref, v_ref, d_ref, nm_ref, nv_ref):
        d_ref[...], nm_ref[...], nv_ref[...] = _adamw_math(w_ref[...], g_ref[...], m_ref[...], v_ref[...])

    spec = pl.BlockSpec(w.shape, lambda i: (0, 0))
    outs, _ = _hosted(body, name=name, grid=(1,), in_specs=[spec] * 4, out_specs=[spec] * 3, out_shape=[_sds(w.shape, F32)] * 3,
                      args=[w, g, m, v], block_bytes=7 * _nbytes(w.shape, F32))
    return outs


def _blocks_to_columns(g):
    n, r, c = g.shape
    return jnp.transpose(g, (1, 0, 2)).reshape(r, n * c)


def _pack(parts):
    return jnp.concatenate([p.reshape(-1) for p in parts]).reshape(-1, 128)


def _unpack(packed, like):
    flat, outs, off = packed.reshape(-1), [], 0
    for p in like:
        outs.append(flat[off:off + p.size].reshape(p.shape))
        off += p.size
    return outs


def kernel(x, norm_pre, norm_post, gla_w_in, gla_w_gate2, gla_b_gate, gla_o_gain, gla_w_out, sgu_w_in, sgu_ln_gain, sgu_ln_bias, sgu_w_spatial, sgu_b_spatial, sgu_w_out, loss_target, m_norm_pre, m_norm_post, m_gla_w_in, m_gla_w_gate2, m_gla_b_gate, m_gla_o_gain, m_gla_w_out, m_sgu_w_in, m_sgu_ln_gain, m_sgu_ln_bias, m_sgu_w_spatial, m_sgu_b_spatial, m_sgu_w_out, v_norm_pre, v_norm_post, v_gla_w_in, v_gla_w_gate2, v_gla_b_gate, v_gla_o_gain, v_gla_w_out, v_sgu_w_in, v_sgu_ln_gain, v_sgu_ln_bias, v_sgu_w_spatial, v_sgu_b_spatial, v_sgu_w_out):
    me = _index_of(*_place())
    x0 = x.reshape(S, D)
    tgt = loss_target.reshape(S, D)
    npre0, npre1 = norm_pre[0:1], norm_pre[1:2]
    npost0, npost1 = norm_post[0:1], norm_post[1:2]
    ws = sgu_w_spatial[0]
    wst = jnp.transpose(ws, (0, 2, 1))
    bsb = jnp.broadcast_to(sgu_b_spatial[0][:, :, None], (SGU_G, SGU_BLOCK, SGU_GD))
    W_ROWS = D // N_DEV
    IN_COLS_G, IN_COLS_S = GLA_COLS // N_DEV, SGU_COLS // N_DEV

    s_gwi, s_gwo = gla_w_in[0].astype(BF16), gla_w_out[0].astype(BF16)
    s_swi, s_swo = sgu_w_in[0].astype(BF16), sgu_w_out[0].astype(BF16)
    small = jnp.concatenate([jnp.pad(gla_w_gate2[0].reshape(4, 512), ((0, 4), (0, 0))),
                             jnp.pad(jnp.concatenate([sgu_ln_gain, sgu_ln_bias], axis=1), ((0, 7), (0, 0)))], axis=0)

    wg_in, g_small = _gather_first(s_gwi, small, "gather_first")
    w2 =_blocks_to_columns(g_small[:, :4, :].reshape(N_DEV, LR, 128))
    w2p = jnp.pad(w2, ((0, LRP - LR), (0, 0))).astype(BF16)
    lng = g_small[:, 8, :256].reshape(1, D)
    lnb = g_small[:, 8, 256:].reshape(1, D)
    like_gwo, like_swi = _sds((N_DEV, W_ROWS, D), BF16), _sds((N_DEV, D, IN_COLS_S), BF16)

    h0 = _prenorm(x0, npre0)
    proj0, (g_gwo, g_swi) = _mm(h0, wg_in, "nn", F32, tm=1024, tn=896, tk=D, name="gla_in", b_tiled=True, phases=[
        _Phase(like_gwo, None, [_gather_send(s_gwo, 0, W_ROWS)]),
        _Phase(like_swi, None, [_gather_send(s_swi, 0, 768, diagonal=False)])])
    (ypre0, states), (g_gwo, g_swi) = _gla_fwd(proj0, w2p, gla_b_gate, gla_o_gain, phases=[
        _Phase(like_gwo, g_gwo, [_gather_pass(0, W_ROWS)]),
        _Phase(like_swi, g_swi, [_gather_relay(0, 768), _gather_send(s_swi, 768, 512, diagonal=False)])])
    wg_out = g_gwo.reshape(D, D)
    y0, (g_swi,) = _mm(ypre0, wg_out, "nn", F32, tm=1024, tn=1024, tk=D, name="gla_out", phases=[
        _Phase(like_swi, g_swi, [_gather_pass(0, 768), _gather_relay(768, 512), _gather_send(s_swi, 1280, 512, diagonal=False)])])
    (x1, h1), (g_swi,) = _mid_fwd(x0, y0, npost0, npre1, phases=[
        _Phase(like_swi, g_swi, [_gather_pass(768, 512), _gather_relay(1280, 512), _gather_send(s_swi, 1792, 256, diagonal=False)])])
    g_swi, = _carry([_Phase(like_swi, g_swi, [_gather_pass(1280, 512), _gather_relay(1792, 256)])], "relay_sgu_w_in")
    g_swi, = _carry([_Phase(like_swi, g_swi, [_gather_pass(1792, 256)])], "pass_sgu_w_in")
    proj1, (g_swo,) = _mm(h1, g_swi, "nn", F32, tm=1024, tn=IN_COLS_S, tk=D, name="sgu_in", b_blocked=True, phases=[
        _Phase(like_gwo, None, [_gather_send(s_swo, 0, W_ROWS)])])
    (pre1,), (g_swo,) = _sgu_fwd(proj1, lng, lnb, ws, bsb, phases=[_Phase(like_gwo, g_swo, [_gather_pass(0, W_ROWS)])])
    ws_out = g_swo.reshape(D, D)
    loss_cols, dx2, dy1, dnpost1 = _nn_rows(pre1, ws_out, name="sgu_out", row_ins=[x1, tgt], vec_ins=[npost1],
                                            outs=[("vec", F32), ("row", F32), ("row", BF16), ("vec", F32)], tail=_final_tail)
    loss_here = jnp.pad((0.5 * jnp.sum(loss_cols) / D).reshape(1, 1), ((0, 7), (0, 127)))

    like_b_out, like_b_swi = _sds((3, W_ROWS, D), BF16), _sds((3, D, IN_COLS_S), BF16)
    like_b_gwi = _sds((3, D, IN_COLS_G), BF16)
    row_pair = dict(like=_sds((4, W_ROWS, D), BF16), block=lambda i, j: i, ordinal=lambda i, j: i >> 1,
                    dst=lambda ref, k, i, j: ref.at[k])
    col_pair = dict(like=_sds((4, D, IN_COLS_S), BF16), block=lambda i, j: j, ordinal=lambda i, j: 4 * i + (j >> 1),
                    dst=lambda ref, k, i, j: ref.at[k, pl.ds(pl.multiple_of(i * 1024, 1024), 1024)])

    mine = _own_table()
    dws_out, (a_swo,) = _mm(pre1, dy1, "tn", BF16, tm=W_ROWS, tn=D, tk=S, name="sgu_out_dw", pair=row_pair)
    p_swo = dws_out.reshape(N_DEV, W_ROWS, D)
    t_swo, _ = _pair_sum(p_swo, a_swo, 0, W_ROWS, "pair_sum_sgu_w_out", table=mine)
    dpre1, _ = _mm(dy1, ws_out, "nt", F32, tm=1024, tn=1024, tk=D, name="sgu_out_dx")
    (dproj1, dlng, dlnb, dwsp, dbsp), (b_swo,) = _sgu_bwd(proj1, dpre1, lng, lnb, ws, wst, bsb, phases=[
        _Phase(like_b_out, None, [_reduce_cross(t_swo, 0, 0, W_ROWS)])])
    p_swi, (a_swi,) = _mm(h1, dproj1, "tn", BF16, tm=1024, tn=IN_COLS_S, tk=S, name="sgu_in_dw", out_blocked=True, pair=col_pair)
    t_swi, _ = _pair_sum(p_swi, a_swi, 0, D, "pair_sum_sgu_w_in", table=mine)
    (dx1, dy0, dnpre1, dnpost0), (b_swi,) = _nt_rows(
        dproj1, g_swi, tk=IN_COLS_S, name="sgu_in_dx", b_blocked=True, row_ins=[dx2, x1, y0], vec_ins=[npre1, npost0],
        outs=[("row", F32), ("row", BF16), ("vec", F32), ("vec", F32)], tail=_mid_bwd_tail, phases=[
            _Phase(like_b_swi, None, [_reduce_cross(t_swi, 0, 0, 1152)])])
    dwg_out, (a_gwo, b_swi) = _mm(ypre0, dy0, "tn", BF16, tm=W_ROWS, tn=D, tk=S, name="gla_out_dw", pair=row_pair, phases=[
        _Phase(like_b_swi, b_swi, [_reduce_cross(t_swi, 1152, 1152, 256)])])
    p_gwo = dwg_out.reshape(N_DEV, W_ROWS, D)
    t_gwo, _ = _pair_sum(p_gwo, a_gwo, 0, W_ROWS, "pair_sum_gla_w_out", table=mine)
    dypre0, _ = _mm(dy0, wg_out, "nt", F32, tm=1024, tn=1024, tk=D, name="gla_out_dx")
    late = [dnpre1, dnpost1, dlng, dlnb, dwsp, jnp.transpose(dbsp[:, :SGU_G])]
    late_pack = _pack(late)
    (dproj0, dogain, dbgate, dw2), (b_swi, b_gwo, g_late) = _gla_bwd(proj0, dypre0, states, w2p, gla_b_gate, gla_o_gain, phases=[
        _Phase(like_b_swi, b_swi, [_reduce_cross(t_swi, 1408, 1408, 640)]),
        _Phase(like_b_out, None, [_reduce_cross(t_gwo, 0, 0, W_ROWS)]),
        _Phase(_sds((N_DEV,) + late_pack.shape, F32), None, [_gather_send(late_pack, 0, late_pack.shape[0])])])
    half = D // 2
    dwg_in_a, (g_late,) = _mm(h0, dproj0, "tn", BF16, tm=half, tn=896, tk=S, name="gla_in_dw_a", m_tiles=(0, 1), phases=[
        _Phase(_sds((N_DEV,) + late_pack.shape, F32), g_late, [_gather_pass(0, late_pack.shape[0])])])
    own_gwi, a_gwi = _blockify_pair(dwg_in_a, None, None, 0, "blockify_gla_w_in_a")
    t_gwi_a, _ = _pair_sum(own_gwi, a_gwi, 0, half, "pair_sum_gla_w_in_a")
    dwg_in_b, (b_gwi,) = _mm(h0, dproj0, "tn", BF16, tm=half, tn=896, tk=S, name="gla_in_dw_b", m_tiles=(1, 1), phases=[
        _Phase(like_b_gwi, None, [_reduce_cross(t_gwi_a, 0, 0, 512)])])
    own_gwi, a_gwi = _blockify_pair(dwg_in_b, own_gwi, a_gwi, half, "blockify_gla_w_in_b")
    t_gwi_b, _ = _pair_sum(own_gwi, a_gwi, half, half, "pair_sum_gla_w_in_b")
    (grad_x, dnpre0), (b_gwi,) = _nt_rows(
        dproj0, wg_in, tk=896, name="gla_in_dx", b_tiled=True, row_ins=[dx1, x0], vec_ins=[npre0],
        outs=[("row", F32), ("vec", F32)], tail=_first_bwd_tail, phases=[
            _Phase(like_b_gwi, b_gwi, [_reduce_cross(t_gwi_a, 512, 512, 512), _reduce_cross(t_gwi_b, 0, half, 384)])])

    early = [dnpre0, dnpost0, dbgate, dogain, dw2[:LR], loss_here]
    early_pack = _pack(early)
    like_early = _sds((N_DEV,) + early_pack.shape, F32)
    (g_swo, d_swo, nm_swo, nv_swo), (g_early, b_gwi) = _sum_adamw(
        p_swo, a_swo, b_swo, sgu_w_out[0], m_sgu_w_out[0], v_sgu_w_out[0], name="adamw_sgu_w_out", table=mine, phases=[
            _Phase(like_early, None, [_gather_send(early_pack, 0, early_pack.shape[0])]),
            _Phase(like_b_gwi, b_gwi, [_reduce_cross(t_gwi_b, 384, half + 384, 256)])])
    (g_gwo_, d_gwo, nm_gwo, nv_gwo), (g_early, b_gwi) = _sum_adamw(
        p_gwo, a_gwo, b_gwo, gla_w_out[0], m_gla_w_out[0], v_gla_w_out[0], name="adamw_gla_w_out", table=mine, phases=[
            _Phase(like_early, g_early, [_gather_pass(0, early_pack.shape[0])]),
            _Phase(like_b_gwi, b_gwi, [_reduce_cross(t_gwi_b, 640, half + 640, 128)])])
    (g_swi_, d_swi, nm_swi, nv_swi), (b_gwi,) = _sum_adamw(
        p_swi, a_swi, b_swi, sgu_w_in[0], m_sgu_w_in[0], v_sgu_w_in[0], name="adamw_sgu_w_in", table=mine, phases=[
            _Phase(like_b_gwi, b_gwi, [_reduce_cross(t_gwi_b, 768, half + 768, 256)])])
    (g_gwi_, d_gwi, nm_gwi, nv_gwi), _ = _sum_adamw(
        own_gwi, a_gwi, b_gwi, gla_w_in[0], m_gla_w_in[0], v_gla_w_in[0], name="adamw_gla_w_in")

    g_npre1, g_npost1, g_lng_full, g_lnb_full, g_wsp, g_bsp = _unpack(_sum_parts(g_late, "sum_late_small_grads"), late)
    g_npre0, g_npost0, g_bgate, g_ogain, g_w2_full, loss_all = _unpack(_sum_parts(g_early, "sum_early_small_grads"), early)
    loss = loss_all[0, 0]
    g_w2 = lax.dynamic_slice(g_w2_full, (0, me * 128), (LR, 128))
    g_lng = lax.dynamic_slice(g_lng_full, (0, me * 256), (1, 256))
    g_lnb = lax.dynamic_slice(g_lnb_full, (0, me * 256), (1, 256))
    small_g = [jnp.concatenate([g_npre0, g_npre1], 0), jnp.concatenate([g_npost0, g_npost1], 0), g_w2, g_bgate, g_ogain,
               g_lng, g_lnb, g_wsp, g_bsp]
    small_w = [norm_pre, norm_post, gla_w_gate2[0], gla_b_gate, gla_o_gain, sgu_ln_gain, sgu_ln_bias, sgu_w_spatial[0], sgu_b_spatial[0]]
    small_m = [m_norm_pre, m_norm_post, m_gla_w_gate2[0], m_gla_b_gate, m_gla_o_gain, m_sgu_ln_gain, m_sgu_ln_bias, m_sgu_w_spatial[0], m_sgu_b_spatial[0]]
    small_v = [v_norm_pre, v_norm_post, v_gla_w_gate2[0], v_gla_b_gate, v_gla_o_gain, v_sgu_ln_gain, v_sgu_ln_bias, v_sgu_w_spatial[0], v_sgu_b_spatial[0]]
    d_pack, nm_pack, nv_pack = _adamw(_pack(small_w), _pack(small_g), _pack(small_m), _pack(small_v), "adamw_small")

    out_like = [norm_pre, norm_post, gla_w_gate2, gla_b_gate, gla_o_gain, sgu_ln_gain, sgu_ln_bias, sgu_w_spatial, sgu_b_spatial]
    sg_ = [g.reshape(s.shape) for g, s in zip(small_g, out_like)]
    sd_, sm_, sv_ = (_unpack(pk, out_like) for pk in (d_pack, nm_pack, nv_pack))

    def assemble(small_list, w_in_g, w_out_g, w_in_s, w_out_s):
        npre_, npost_, w2_, bg_, og_, lg_, lb_, wsp_, bsp_ = small_list
        return [npre_, npost_, w_in_g[None], w2_, bg_, og_, w_out_g[None], w_in_s[None], lg_, lb_, wsp_, bsp_, w_out_s[None]]

    return (loss, grad_x.reshape(1, S, D),
            *assemble(sg_, g_gwi_, g_gwo_, g_swi_, g_swo),
            *assemble(sd_, d_gwi, d_gwo, d_swi, d_swo),
            *assemble(sm_, nm_gwi, nm_gwo, nm_swi, nm_swo),
            *assemble(sv_, nv_gwi, nv_gwo, nv_swi, nv_swo))
```

```python
import functools

import jax
import jax.numpy as jnp
from jax import lax
from jax.experimental import pallas as pl
from jax.experimental.pallas import tpu as pltpu

F32 = jnp.float32
BF16 = jnp.bfloat16

N_DEV = 8
S = 2048
D = 2048
H = 4
DK = 256
DV = 512
C = 64
NC = S // C
GLA_COLS = 6160
GLA_PAD = 6272
Q0, K0, V0, G0, LR0 = 0, 1024, 2048, 4096, 6144
LR = 16
LRP = 128
SGU_COLS = 6144
SGU_BLOCK = 128
SGU_G = 8
SGU_GD = 256
EPS = 1e-6
GLA_TAU = 16.0

ADAM_LR, ADAM_B1, ADAM_B2, ADAM_EPS, ADAM_WD, ADAM_STEP = 0.001, 0.9, 0.999, 1e-08, 0.01, 10

V7X_VMEM_BYTES = 64 * 1024 * 1024
VMEM_CEILING = V7X_VMEM_BYTES - 6 * 1024 * 1024
MESH = pl.DeviceIdType.MESH
HBM_SPEC = pl.BlockSpec(memory_space=pl.ANY)


def _sds(shape, dtype):
    return jax.ShapeDtypeStruct(tuple(shape), dtype)


def _nbytes(shape, dtype):
    n = 1
    for s in shape:
        n *= s
    return n * jnp.dtype(dtype).itemsize


def _dot(a, b, dims=(((1,), (0,)), ((), ())), precision=None):
    return lax.dot_general(a, b, dims, precision=precision, preferred_element_type=F32)


NN = (((1,), (0,)), ((), ()))
TN = (((0,), (0,)), ((), ()))
NT = (((1,), (1,)), ((), ()))


def _place():
    return lax.axis_index("x"), lax.axis_index("y"), lax.axis_index("c")


def _index_of(px, py, pc):
    return 4 * px + 2 * py + pc


def _chips(x, y):
    return [(1 - x, y), (x, 1 - y), (1 - x, 1 - y)]


def _rcopy(src, dst, send_sem, recv_sem, to):
    return pltpu.make_async_remote_copy(src_ref=src, dst_ref=dst, send_sem=send_sem, recv_sem=recv_sem,
                                        device_id=to, device_id_type=MESH)


class _Move:
    def __init__(self, ins, n_remote, make, stage=None):
        self.ins, self.n_remote, self.make, self.stage = list(ins), n_remote, make, stage

    def scratch(self):
        sems = [pltpu.SemaphoreType.DMA((self.n_remote,)), pltpu.SemaphoreType.DMA((self.n_remote,))]
        return sems if self.stage is None else sems + [pltpu.SemaphoreType.DMA((1,)), pltpu.VMEM(*self.stage)]

    def start(self, in_refs, buf, scratch):
        sends, _, local = self.make(in_refs, buf, scratch[0], scratch[1])
        if local is not None:
            pltpu.make_async_copy(local[0], scratch[3], scratch[2].at[0]).start()
        for cp in sends:
            cp.start()

    def finish(self, in_refs, buf, scratch):
        sends, arrivals, local = self.make(in_refs, buf, scratch[0], scratch[1])
        if local is not None:
            pltpu.make_async_copy(local[0], scratch[3], scratch[2].at[0]).wait()
            out = pltpu.make_async_copy(scratch[3], local[1], scratch[2].at[0])
            out.start()
        for cp in arrivals:
            cp.wait_recv()
        for cp in sends:
            cp.wait_send()
        if local is not None:
            out.wait()


class _Phase:
    def __init__(self, like, so_far, moves):
        self.like, self.so_far, self.moves = like, so_far, list(moves)


def _gather_send(shard, r0, nr, diagonal=True):
    def make(in_refs, g, ss, rs):
        sh, = in_refs
        x, y, c = _place()
        me = _index_of(x, y, c)
        rows = pl.ds(r0, nr)
        peers = [(x, y, 1 - c)] + [(px, py, c) for px, py in _chips(x, y)[:3 if diagonal else 2]]
        sends = [_rcopy(sh.at[rows], g.at[me, rows], ss.at[k], rs.at[k], p) for k, p in enumerate(peers)]
        arrivals = [_rcopy(sh.at[rows], g.at[_index_of(*p), rows], ss.at[k], rs.at[k], p) for k, p in enumerate(peers)]
        return sends, arrivals, (sh.at[rows], g.at[me, rows])

    return _Move([shard], 4 if diagonal else 3, make, stage=((nr, shard.shape[1]), shard.dtype))


def _gather_relay(r0, nr):
    def make(in_refs, g, ss, rs):
        x, y, c = _place()
        nx, ny, nd = [(px, py, c) for px, py in _chips(x, y)]
        first, second = pl.ds(r0, nr // 2), pl.ds(r0 + nr // 2, nr // 2)
        sends = [_rcopy(g.at[_index_of(*nx), first], g.at[_index_of(*nx), first], ss.at[0], rs.at[0], ny),
                 _rcopy(g.at[_index_of(*ny), second], g.at[_index_of(*ny), second], ss.at[1], rs.at[1], nx)]
        arrivals = [_rcopy(g.at[_index_of(*nx), first], g.at[_index_of(*nd), first], ss.at[0], rs.at[0], ny),
                    _rcopy(g.at[_index_of(*ny), second], g.at[_index_of(*nd), second], ss.at[1], rs.at[1], nx)]
        return sends, arrivals, None

    return _Move([], 2, make)


def _gather_pass(r0, nr):
    def make(in_refs, g, ss, rs):
        x, y, c = _place()
        rows = pl.ds(r0, nr)
        sends = [_rcopy(g.at[_index_of(px, py, c), rows], g.at[_index_of(px, py, c), rows], ss.at[j], rs.at[j], (x, y, 1 - c))
                 for j, (px, py) in enumerate(_chips(x, y))]
        arrivals = [_rcopy(g.at[_index_of(px, py, c), rows], g.at[_index_of(px, py, 1 - c), rows], ss.at[j], rs.at[j], (x, y, 1 - c))
                    for j, (px, py) in enumerate(_chips(x, y))]
        return sends, arrivals, None

    return _Move([], 3, make)


def _own_table():
    x, y, c = _place()
    return jnp.stack([_index_of(px, py, c) for px, py in [(x, y)] + _chips(x, y)]).astype(jnp.int32)


def _blockify_pair(dw, own_so_far, a_so_far, dst_r0, name):
    rows, tr, cw, win = dw.shape[0], 256, GLA_COLS // N_DEV, 896
    n_steps = rows // tr

    def body(*refs):
        x_ref, own_ref, a_ref, stage_ref, send_sems, recv_sem = refs[0], *refs[-5:]
        i = pl.program_id(0)
        x, y, c = _place()

        def send(slot, k):
            dst = a_ref.at[k, pl.ds(pl.multiple_of(dst_r0 + i * tr, tr), tr)]
            return _rcopy(stage_ref.at[slot], dst, send_sems.at[slot], recv_sem.at[0], (x, y, 1 - c))

        for j in range(N_DEV):
            window = x_ref[:, 768 * j:768 * j + win].astype(F32)
            tile = (pltpu.roll(window, win - 2 * j, 1) if j else window)[:, :cw].astype(BF16)
            k = ((j >> 2) ^ x) + 2 * (((j >> 1) & 1) ^ y)

            @pl.when((j & 1) == c)
            def _():
                own_ref[k] = tile

            @pl.when((j & 1) != c)
            def _():
                slot = (j >> 1) & 1
                if j >> 1 >= 2:
                    send(slot, k).wait_send()
                else:
                    pl.when(i > 0)(lambda: send(slot, k).wait_send())
                stage_ref[slot] = tile
                send(slot, k).start()

        @pl.when(i == n_steps - 1)
        def _():
            send(0, 0).wait_send()
            send(1, 0).wait_send()
            arrived = a_ref.at[:, pl.ds(dst_r0, rows)]
            _rcopy(arrived, arrived, send_sems.at[0], recv_sem.at[0], (x, y, 1 - c)).wait_recv()

    continues = a_so_far is not None
    own, a = pl.pallas_call(
        body, grid=(n_steps,),
        in_specs=[pl.BlockSpec((tr, GLA_PAD), lambda i: (i, 0))] + [HBM_SPEC] * (2 * continues),
        out_specs=[pl.BlockSpec((4, tr, cw), lambda i: (0, dst_r0 // tr + i, 0)), HBM_SPEC],
        out_shape=[_sds((4, D, cw), BF16), _sds((4, D, cw), BF16)],
        scratch_shapes=[pltpu.VMEM((2, tr, cw), BF16), pltpu.SemaphoreType.DMA((2,)), pltpu.SemaphoreType.DMA((1,))],
        input_output_aliases={1: 0, 2: 1} if continues else {},
        compiler_params=pltpu.CompilerParams(dimension_semantics=("arbitrary",), vmem_limit_bytes=VMEM_CEILING),
        name=name,
    )(*([dw] + [own_so_far, a_so_far] * continues))
    return own, a


def _reduce_cross(sums, src_r0, dst_r0, nr):
    def make(in_refs, b, ss, rs):
        t, = in_refs
        x, y, c = _place()
        src, dst = pl.ds(src_r0, nr), pl.ds(dst_r0, nr)
        sends = [_rcopy(t.at[j, src], b.at[j, dst], ss.at[j], rs.at[j], (px, py, c)) for j, (px, py) in enumerate(_chips(x, y))]
        return sends, sends, None

    return _Move([sums], 3, make)


def _hosted(body, *, name, grid, in_specs, out_specs, out_shape, args, scratch_shapes=(), block_bytes, scratch_bytes=0,
            phases=(), table=None):
    n_in, n_out, n_scr = len(args), len(out_shape), len(scratch_shapes)
    all_args, all_out_shape, sems, aliases, layout = list(args), list(out_shape), [], {}, []
    for j, ph in enumerate(phases):
        counts = []
        for mv in ph.moves:
            all_args += mv.ins
            counts.append(len(mv.ins))
            sems += mv.scratch()
        if ph.so_far is not None:
            aliases[len(all_args)] = n_out + j
            all_args.append(ph.so_far)
        layout.append((counts, ph.so_far is not None))
        all_out_shape.append(ph.like)
    n_extra_in = len(all_args) - n_in

    def wrapped(*refs):
        ins, pos = refs[:n_in], n_in
        move_ins = []
        for counts, continues in layout:
            per_move = []
            for cnt in counts:
                per_move.append(refs[pos:pos + cnt])
                pos += cnt
            pos += continues
            move_ins.append(per_move)
        outs = refs[pos:pos + n_out]
        bufs = refs[pos + n_out:pos + n_out + len(phases)]
        pos += n_out + len(phases)
        scratch = refs[pos:pos + n_scr]
        pos += n_scr
        move_sems = []
        for ph in phases:
            per_move = []
            for mv in ph.moves:
                count = len(mv.scratch())
                per_move.append(refs[pos:pos + count])
                pos += count
            move_sems.append(per_move)

        def each_move(fn_name):
            for ph, buf, per_in, per_sem in zip(phases, bufs, move_ins, move_sems):
                for mv, mv_in, mv_sem in zip(ph.moves, per_in, per_sem):
                    getattr(mv, fn_name)(mv_in, buf, mv_sem)

        if phases:
            first = functools.reduce(jnp.logical_and, [pl.program_id(a) == 0 for a in range(len(grid))])
            last = functools.reduce(jnp.logical_and, [pl.program_id(a) == grid[a] - 1 for a in range(len(grid))])
            pl.when(first)(lambda: each_move("start"))
        body(*ins, *outs, *scratch)
        if phases:
            pl.when(last)(lambda: each_move("finish"))

    all_args = [pltpu.with_memory_space_constraint(a, pltpu.HBM) for a in all_args]
    assert 2 * block_bytes + scratch_bytes <= VMEM_CEILING, name
    params = pltpu.CompilerParams(dimension_semantics=("arbitrary",) * len(grid), vmem_limit_bytes=VMEM_CEILING)
    all_in_specs, all_out_specs = list(in_specs) + [HBM_SPEC] * n_extra_in, list(out_specs) + [HBM_SPEC] * len(phases)
    if table is None:
        results = pl.pallas_call(
            wrapped, grid=grid, in_specs=all_in_specs, out_specs=all_out_specs, out_shape=all_out_shape,
            scratch_shapes=list(scratch_shapes) + sems, input_output_aliases=aliases, compiler_params=params, name=name,
        )(*all_args)
    else:
        results = pl.pallas_call(
            lambda table_ref, *refs: wrapped(*refs),
            grid_spec=pltpu.PrefetchScalarGridSpec(num_scalar_prefetch=1, grid=grid, in_specs=all_in_specs, out_specs=all_out_specs,
                                                   scratch_shapes=list(scratch_shapes) + sems),
            out_shape=all_out_shape, input_output_aliases={k + 1: v for k, v in aliases.items()}, compiler_params=params, name=name,
        )(table, *all_args)
    return list(results[:n_out]), list(results[n_out:])


class _Both:
    def __init__(self, copies):
        self.copies = copies

    def start(self):
        for cp in self.copies:
            cp.start()

    def wait_send(self):
        for cp in self.copies:
            cp.wait_send()

    def wait_recv(self):
        for cp in self.copies:
            cp.wait_recv()


def _carry(phases, name):
    def body(o_ref):
        o_ref[...] = jnp.zeros_like(o_ref)

    _, bufs = _hosted(body, name=name, grid=(1,), in_specs=[], out_specs=[pl.BlockSpec((8, 128), lambda i: (0, 0))],
                      out_shape=[_sds((8, 128), F32)], args=[], block_bytes=8 * 128 * 4, phases=phases)
    return bufs


def _gather_first(shard, small, name):
    cw, tr, n_tiles = shard.shape[1], 256, GLA_PAD // 128

    def body(sh_ref, sm_ref, wn_ref, g_ref, gs_ref, wt_ref, win_ref, tmp_ref, send_sems, recv_sems, local_sems):
        x, y, c = _place()
        me, sibling = (x, y, c), (x, y, 1 - c)
        chips = _chips(x, y)

        def copy(base, out_ref, k, block, to, src=None):
            dst = out_ref.at[_index_of(*block)]
            return _rcopy(dst if src is None else src, dst, send_sems.at[base + k], recv_sems.at[base + k], to)

        icopy = functools.partial(copy, 0, g_ref)
        scopy = functools.partial(copy, 8, gs_ref)

        def wcopy(k, block, to, src=None):
            if k in (1, 2):
                return icopy(k, block, to, src)
            halves = []
            for part, sem in enumerate((k, {0: 15, 4: 16, 5: 17, 6: 18}[k])):
                rows = pl.ds(part * (D // 2), D // 2)
                dst = g_ref.at[_index_of(*block), rows]
                halves.append(_rcopy(dst if src is None else src.at[rows], dst, send_sems.at[sem], recv_sems.at[sem], to))
            return _Both(halves)

        def relay(k, block, half, to):
            rows = pl.ds(half * (D // 2), D // 2)
            ref = g_ref.at[_index_of(*block), rows]
            return _rcopy(ref, ref, send_sems.at[k], recv_sems.at[k], to)

        def load(src_ref, slot):
            cp = pltpu.make_async_copy(src_ref, win_ref.at[slot], local_sems.at[0])
            cp.start()
            cp.wait()

        def place(slot, block):
            b = _index_of(*block)

            def rows_chunk(r, carry):
                rows = pl.ds(pl.multiple_of(r * tr, tr), tr)
                tmp_ref[:, :cw] = win_ref[slot, rows, :].astype(F32)
                shifted = pltpu.roll(tmp_ref[...], 2 * b, 1)
                for u in range(7):
                    wt_ref[6 * b + u, rows, :] = (wt_ref[6 * b + u, rows, :].astype(F32) + shifted[:, 128 * u:128 * (u + 1)]).astype(BF16)
                return carry

            lax.fori_loop(0, D // tr, rows_chunk, 0)

        small_own = pltpu.make_async_copy(sm_ref, gs_ref.at[_index_of(*me)], local_sems.at[1])
        small_own.start()
        first = [wcopy(1 + j, me, (*chip, c), src=sh_ref) for j, chip in enumerate(chips[:2])]
        first += [scopy(0, me, sibling, src=sm_ref)] + [scopy(1 + j, me, (*chip, c), src=sm_ref) for j, chip in enumerate(chips)]
        for cp in first:
            cp.start()

        def clear(t, carry):
            wt_ref[t] = jnp.zeros((D, 128), BF16)
            return carry

        lax.fori_loop(0, n_tiles, clear, 0)
        tmp_ref[...] = jnp.zeros_like(tmp_ref)

        def emit(t):
            pltpu.make_async_copy(wt_ref.at[t], wn_ref.at[t], local_sems.at[2]).start()

        def take(block, slot, arrivals=None, pass_on=None):
            for cp in arrivals or ():
                cp.wait_recv()
            load(sh_ref if arrivals is None else g_ref.at[_index_of(*block)], slot)
            if pass_on is not None:
                pass_on.start()
            place(slot, block)
            for u in range(1, 6):
                emit(6 * _index_of(*block) + u)

        near_x, near_y, far = [(*chip, c) for chip in chips]
        to_sibling = wcopy(0, me, sibling, src=win_ref.at[0])
        pass_x = wcopy(4, near_x, sibling, src=win_ref.at[1])
        pass_y = wcopy(5, near_y, sibling, src=win_ref.at[0])
        pass_d = wcopy(6, far, sibling, src=win_ref.at[0])
        relays = [relay(3, near_x, 0, near_y), relay(7, near_y, 1, near_x)]
        take(me, 0, pass_on=to_sibling)
        take(near_x, 1, [wcopy(1, near_x, me)], pass_x)
        relays[0].start()
        to_sibling.wait_send()
        take(near_y, 0, [wcopy(2, near_y, me)], pass_y)
        relays[1].start()
        small_passed = []
        for j, chip in enumerate(chips):
            scopy(1 + j, (*chip, c), me).wait_recv()
            cp = scopy(4 + j, (*chip, c), sibling)
            cp.start()
            small_passed.append(cp)
        pass_x.wait_send()
        take(sibling, 1, [wcopy(0, sibling, me)])
        pass_y.wait_send()
        take((*chips[0], 1 - c), 0, [wcopy(4, (*chips[0], 1 - c), me)])
        take((*chips[1], 1 - c), 1, [wcopy(5, (*chips[1], 1 - c), me)])
        take(far, 0, [relay(3, far, 0, near_y), relay(7, far, 1, near_x)], pass_d)
        take((*chips[2], 1 - c), 1, [wcopy(6, (*chips[2], 1 - c), me)])
        for t in range(0, n_tiles, 6):
            emit(t)
        scopy(0, sibling, me).wait_recv()
        for j, chip in enumerate(chips):
            scopy(4 + j, (*chip, 1 - c), me).wait_recv()
        for cp in first + small_passed + relays + [pass_d]:
            cp.wait_send()
        small_own.wait()
        pltpu.make_async_copy(wn_ref, wn_ref, local_sems.at[2]).wait()

    wn, _, gs = pl.pallas_call(
        body,
        in_specs=[HBM_SPEC] * 2, out_specs=[HBM_SPEC] * 3,
        out_shape=[_sds((n_tiles, D, 128), BF16), _sds((N_DEV,) + shard.shape, BF16), _sds((N_DEV,) + small.shape, small.dtype)],
        scratch_shapes=[pltpu.VMEM((n_tiles, D, 128), BF16), pltpu.VMEM((2, D, cw), BF16), pltpu.VMEM((tr, 7 * 128), F32),
                        pltpu.SemaphoreType.DMA((19,)), pltpu.SemaphoreType.DMA((19,)), pltpu.SemaphoreType.DMA((3,))],
        compiler_params=pltpu.CompilerParams(vmem_limit_bytes=VMEM_CEILING),
        name=name,
    )(shard, small)
    return wn, gs


def _mm(a, b, mode, out_dtype, *, tm, tn, tk, name, b_blocked=False, b_tiled=False, out_blocked=False, m_tiles=None, pair=None,
        phases=()):
    if mode == "nn":
        (m, k), dims = a.shape, NN
        a_blk, a_map = (tm, tk), (lambda i, j, kk: (i, kk))
        if b_blocked:
            assert b.shape[1] == k and b.shape[2] == tn and tk == k
            n = b.shape[0] * tn
            b_spec = pl.BlockSpec((None, tk, tn), lambda i, j, kk: (j, kk, 0))
        elif b_tiled:
            assert b.shape[1] == k and b.shape[2] == 128 and tn % 128 == 0
            n = b.shape[0] * 128
            b_spec = pl.BlockSpec((tn // 128, tk, 128), lambda i, j, kk: (j, kk, 0))
        else:
            assert b.shape[0] == k
            n = b.shape[1]
            b_spec = pl.BlockSpec((tk, tn), lambda i, j, kk: (kk, j))
    elif mode == "tn":
        (k, m), n, dims = a.shape, b.shape[1], TN
        assert b.shape[0] == k
        first = 0 if m_tiles is None else m_tiles[0]
        a_blk, a_map = (tk, tm), (lambda i, j, kk: (kk, i + first))
        b_spec = pl.BlockSpec((tk, tn), lambda i, j, kk: (kk, j))
    else:
        (m, k), dims = a.shape, NT
        a_blk, a_map = (tm, tk), (lambda i, j, kk: (i, kk))
        if b_blocked:
            assert b.shape[0] * b.shape[2] == k and b.shape[2] == tk
            n = b.shape[1]
            b_spec = pl.BlockSpec((None, tn, tk), lambda i, j, kk: (kk, j, 0))
        elif b_tiled:
            assert b.shape[0] * 128 == k and b.shape[2] == 128 and tk % 128 == 0
            n = b.shape[1]
            b_spec = pl.BlockSpec((tk // 128, tn, 128), lambda i, j, kk: (kk, j, 0))
        else:
            assert b.shape[1] == k
            n = b.shape[0]
            b_spec = pl.BlockSpec((tn, tk), lambda i, j, kk: (j, kk))
    assert m % tm == 0 and n % tn == 0 and k % tk == 0, (a.shape, b.shape, mode)
    nk = k // tk
    n_row_tiles = m // tm if m_tiles is None else m_tiles[1]
    if out_blocked:
        out_shape, out_spec = _sds((n // tn, n_row_tiles * tm, tn), out_dtype), pl.BlockSpec((None, tm, tn), lambda i, j, kk: (j, i, 0))
    else:
        out_shape, out_spec = _sds((n_row_tiles * tm, n), out_dtype), pl.BlockSpec((tm, tn), lambda i, j, kk: (i, j))

    grid = (n_row_tiles, n // tn, nk)

    def body(a_ref, b_ref, o_ref, *rest):
        rhs = jnp.concatenate([b_ref[u] for u in range(b_ref.shape[0])], axis=1) if b_tiled else b_ref[...]
        p = _dot(a_ref[...], rhs, dims)
        if nk == 1:
            o_ref[...] = p.astype(out_dtype)
            if pair is not None:
                _send_to_sibling(p.astype(out_dtype), *rest)
        else:
            acc_ref, = rest
            kk = pl.program_id(2)

            @pl.when(kk == 0)
            def _():
                acc_ref[...] = p

            @pl.when(kk > 0)
            def _():
                acc_ref[...] += p

            @pl.when(kk == nk - 1)
            def _():
                o_ref[...] = acc_ref[...].astype(out_dtype)

    def _send_to_sibling(tile, pair_ref, stage_ref, send_sems, recv_sem):
        i, j = pl.program_id(0), pl.program_id(1)
        x, y, c = _place()
        blk = pair["block"](i, j)
        k = ((blk >> 2) ^ x) + 2 * (((blk >> 1) & 1) ^ y)
        ordinal = pair["ordinal"](i, j)

        def send(slot):
            return _rcopy(stage_ref.at[slot], pair["dst"](pair_ref, k, i, j), send_sems.at[slot], recv_sem.at[0], (x, y, 1 - c))

        @pl.when((blk & 1) != c)
        def _():
            slot = ordinal & 1

            @pl.when(ordinal >= 2)
            def _():
                send(slot).wait_send()

            stage_ref[slot] = tile
            send(slot).start()

        @pl.when((i == grid[0] - 1) & (j == grid[1] - 1))
        def _():
            send(0).wait_send()
            send(1).wait_send()
            _rcopy(pair_ref, pair_ref, send_sems.at[0], recv_sem.at[0], (x, y, 1 - c)).wait_recv()

    blocks = _nbytes(a_blk, a.dtype) + tk * tn * jnp.dtype(b.dtype).itemsize + _nbytes((tm, tn), out_dtype)
    out_specs, out_shapes, scratch = [out_spec], [out_shape], [] if nk == 1 else [pltpu.VMEM((tm, tn), F32)]
    scratch_bytes = _nbytes((tm, tn), F32) * (nk > 1)
    if pair is not None:
        assert nk == 1
        out_specs, out_shapes = out_specs + [HBM_SPEC], out_shapes + [pair["like"]]
        scratch = [pltpu.VMEM((2, tm, tn), out_dtype), pltpu.SemaphoreType.DMA((2,)), pltpu.SemaphoreType.DMA((1,))]
        scratch_bytes = 2 * _nbytes((tm, tn), out_dtype)
    outs, bufs = _hosted(
        body, name=name, grid=grid,
        in_specs=[pl.BlockSpec(a_blk, a_map), b_spec], out_specs=out_specs, out_shape=out_shapes, args=[a, b],
        scratch_shapes=scratch, block_bytes=blocks, scratch_bytes=scratch_bytes, phases=phases)
    return outs[0], outs[1:] + bufs


NT_ROWS_TM = 512
NT_ROWS_SUB = 128


def _nt_rows(a, b, *, tk, name, row_ins, vec_ins, outs, tail, b_blocked=False, b_tiled=False, phases=()):
    m, k = a.shape
    tm, nk = NT_ROWS_TM, k // tk
    if b_blocked:
        assert b.shape[0] * b.shape[2] == k and b.shape[2] == tk and b.shape[1] == D
        b_spec = pl.BlockSpec((None, D, tk), lambda i, kk: (kk, 0, 0))
    else:
        assert b_tiled and b.shape[0] * 128 == k and tk % 128 == 0 and b.shape[1] == D
        b_spec = pl.BlockSpec((tk // 128, D, 128), lambda i, kk: (kk, 0, 0))
    row_spec, vec_spec = pl.BlockSpec((tm, D), lambda i, kk: (i, 0)), pl.BlockSpec((1, D), lambda i, kk: (0, 0))
    n_row, n_vec, n_out = len(row_ins), len(vec_ins), len(outs)
    assert nk >= 2

    def body(a_ref, b_ref, *rest):
        row_hbm, vec_refs = rest[:n_row], rest[n_row:n_row + n_vec]
        out_refs = rest[n_row + n_vec:n_row + n_vec + n_out]
        acc_ref, row_sems = rest[n_row + n_vec + n_out], rest[-1]
        row_refs = rest[n_row + n_vec + n_out + 1:-1]
        rhs = jnp.concatenate([b_ref[u] for u in range(b_ref.shape[0])], axis=1) if b_tiled else b_ref[...]
        p = _dot(a_ref[...], rhs, NT)
        i, kk = pl.program_id(0), pl.program_id(1)

        def fetch(r):
            return pltpu.make_async_copy(row_hbm[r].at[pl.ds(pl.multiple_of(i * tm, tm), tm)], row_refs[r], row_sems.at[r])

        @pl.when(kk == 0)
        def _():
            for r in range(n_row):
                fetch(r).start()
            acc_ref[...] = p

        @pl.when(kk > 0)
        def _():
            acc_ref[...] += p

        @pl.when(kk == nk - 1)
        def _():
            for r in range(n_row):
                fetch(r).wait()
            for s in range(tm // NT_ROWS_SUB):
                rows = slice(s * NT_ROWS_SUB, (s + 1) * NT_ROWS_SUB)
                tail(acc_ref[rows, :], rows, (i == 0) if s == 0 else None, row_refs, vec_refs, out_refs)

    out_specs = [row_spec if kind == "row" else vec_spec for kind, _ in outs]
    out_shape = [_sds((m, D) if kind == "row" else (1, D), dt) for kind, dt in outs]
    blocks = tm * tk * 2 + D * tk * 2 + sum(tm * D * jnp.dtype(dt).itemsize for kind, dt in outs if kind == "row")
    scratch = [pltpu.VMEM((tm, D), F32)] + [pltpu.VMEM((tm, D), x.dtype) for x in row_ins] + [pltpu.SemaphoreType.DMA((n_row,))]
    return _hosted(body, name=name, grid=(m // tm, nk), in_specs=[pl.BlockSpec((tm, tk), lambda i, kk: (i, kk)), b_spec]
                   + [HBM_SPEC] * n_row + [vec_spec] * n_vec, out_specs=out_specs, out_shape=out_shape,
                   args=[a, b] + list(row_ins) + list(vec_ins), scratch_shapes=scratch,
                   block_bytes=blocks, scratch_bytes=(1 + n_row) * tm * D * 4, phases=phases)


def _nn_rows(a, b, *, name, row_ins, vec_ins, outs, tail):
    m, k = a.shape
    tm = NT_ROWS_TM
    assert b.shape == (k, D)
    row_spec, vec_spec = pl.BlockSpec((tm, D), lambda i: (i, 0)), pl.BlockSpec((1, D), lambda i: (0, 0))
    n_row, n_vec, n_out = len(row_ins), len(vec_ins), len(outs)

    def body(a_ref, b_ref, *rest):
        row_hbm, vec_refs = rest[:n_row], rest[n_row:n_row + n_vec]
        out_refs = rest[n_row + n_vec:n_row + n_vec + n_out]
        d_ref, row_sems = rest[n_row + n_vec + n_out], rest[-1]
        row_refs = rest[n_row + n_vec + n_out + 1:-1]
        i = pl.program_id(0)
        fetches = [pltpu.make_async_copy(row_hbm[r].at[pl.ds(pl.multiple_of(i * tm, tm), tm)], row_refs[r], row_sems.at[r])
                   for r in range(n_row)]
        for cp in fetches:
            cp.start()
        d_ref[...] = _dot(a_ref[...], b_ref[...])
        for cp in fetches:
            cp.wait()
        for s in range(tm // NT_ROWS_SUB):
            rows = slice(s * NT_ROWS_SUB, (s + 1) * NT_ROWS_SUB)
            tail(d_ref[rows, :], rows, (i == 0) if s == 0 else None, row_refs, vec_refs, out_refs)

    out_specs = [row_spec if kind == "row" else vec_spec for kind, _ in outs]
    out_shape = [_sds((m, D) if kind == "row" else (1, D), dt) for kind, dt in outs]
    blocks = tm * k * 2 + k * D * 2 + sum(tm * D * jnp.dtype(dt).itemsize for kind, dt in outs if kind == "row")
    scratch = [pltpu.VMEM((tm, D), F32)] + [pltpu.VMEM((tm, D), x.dtype) for x in row_ins] + [pltpu.SemaphoreType.DMA((n_row,))]
    outs_, _ = _hosted(body, name=name, grid=(m // tm,), in_specs=[pl.BlockSpec((tm, k), lambda i: (i, 0)),
                                                                 pl.BlockSpec((k, D), lambda i: (0, 0))]
                       + [HBM_SPEC] * n_row + [vec_spec] * n_vec, out_specs=out_specs, out_shape=out_shape,
                       args=[a, b] + list(row_ins) + list(vec_ins), scratch_shapes=scratch,
                       block_bytes=blocks, scratch_bytes=(1 + n_row) * tm * D * 4)
    return outs_


def _vec_add(ref, value, first):
    if first is None:
        ref[...] += value
    else:
        pl.when(first)(lambda: ref.__setitem__(Ellipsis, value))
        pl.when(jnp.logical_not(first))(lambda: ref.__setitem__(Ellipsis, ref[...] + value))


RB = 256


def _row_spec(width):
    return pl.BlockSpec((RB, width), lambda i: (i, 0))


def _vec_spec(width):
    return pl.BlockSpec((1, width), lambda i: (0, 0))


def _rinv(x):
    return lax.rsqrt(jnp.mean(x * x, axis=-1, keepdims=True) + EPS)


def _norm_bwd(dyn, xhat, r):
    return r * (dyn - xhat * jnp.mean(dyn * xhat, axis=-1, keepdims=True))


def _colsum(x):
    return jnp.sum(x, axis=0, keepdims=True)


def _prenorm(x, gain):
    def body(x_ref, g_ref, h_ref):
        xv = x_ref[...]
        h_ref[...] = (xv * _rinv(xv) * g_ref[...]).astype(BF16)

    outs, _ = _hosted(body, name="prenorm", grid=(S // RB,), in_specs=[_row_spec(D), _vec_spec(D)], out_specs=[_row_spec(D)],
                      out_shape=[_sds((S, D), BF16)], args=[x, gain], block_bytes=RB * D * 6)
    return outs[0]


def _mid_fwd(x, y, npost, npre, phases=()):
    def body(x_ref, y_ref, po_ref, pr_ref, x1_ref, h1_ref):
        yv = y_ref[...]
        x1 = x_ref[...] + yv * _rinv(yv) * po_ref[...]
        x1_ref[...] = x1
        h1_ref[...] = (x1 * _rinv(x1) * pr_ref[...]).astype(BF16)

    return _hosted(body, name="mid_fwd", grid=(S // RB,), in_specs=[_row_spec(D), _row_spec(D), _vec_spec(D), _vec_spec(D)],
                   out_specs=[_row_spec(D), _row_spec(D)], out_shape=[_sds((S, D), F32), _sds((S, D), BF16)],
                   args=[x, y, npost, npre], block_bytes=RB * D * 14, phases=phases)


def _final_tail(yv, rows, first, row_refs, vec_refs, out_refs):
    (x_ref, t_ref), (po_ref,), (loss_ref, dx_ref, dy_ref, dpo_ref) = row_refs, vec_refs, out_refs
    r = _rinv(yv)
    yhat = yv * r
    err = x_ref[rows, :] + yhat * po_ref[...] - t_ref[rows, :]
    dx = err * (1.0 / D)
    dx_ref[rows, :] = dx
    dy_ref[rows, :] = _norm_bwd(dx * po_ref[...], yhat, r).astype(BF16)
    _vec_add(loss_ref, _colsum(err * err), first)
    _vec_add(dpo_ref, _colsum(dx * yhat), first)


def _mid_bwd_tail(dh, rows, first, row_refs, vec_refs, out_refs):
    (dx2_ref, x_ref, y_ref), (pr_ref, po_ref), (dx1_ref, dy_ref, dpr_ref, dpo_ref) = row_refs, vec_refs, out_refs
    xv = x_ref[rows, :]
    r = _rinv(xv)
    xhat = xv * r
    dx1 = dx2_ref[rows, :] + _norm_bwd(dh * pr_ref[...], xhat, r)
    dx1_ref[rows, :] = dx1
    yv = y_ref[rows, :]
    ry = _rinv(yv)
    yhat = yv * ry
    dy_ref[rows, :] = _norm_bwd(dx1 * po_ref[...], yhat, ry).astype(BF16)
    _vec_add(dpr_ref, _colsum(dh * xhat), first)
    _vec_add(dpo_ref, _colsum(dx1 * yhat), first)


def _first_bwd_tail(dh, rows, first, row_refs, vec_refs, out_refs):
    (dx1_ref, x_ref), (pr_ref,), (gx_ref, dpr_ref) = row_refs, vec_refs, out_refs
    xv = x_ref[rows, :]
    r = _rinv(xv)
    xhat = xv * r
    gx_ref[rows, :] = dx1_ref[rows, :] + _norm_bwd(dh * pr_ref[...], xhat, r)
    _vec_add(dpr_ref, _colsum(dh * xhat), first)


GLA_RB = 256
GLA_CPB = GLA_RB // C


def _sigmoid(x):
    return 1.0 / (1.0 + jnp.exp(-x))


def _tri(strict):
    r = lax.broadcasted_iota(jnp.int32, (C, C), 0)
    c = lax.broadcasted_iota(jnp.int32, (C, C), 1)
    return jnp.where(c < r if strict else c <= r, 1.0, 0.0).astype(BF16)


def _tri_dot(tri, x):
    hi = x.astype(BF16)
    lo = (x - hi.astype(F32)).astype(BF16)
    return _dot(tri, hi) + _dot(tri, lo)


def _gla_gates(glr_b, w2, b, tri):
    z = _dot(glr_b, w2) + b
    log_a = (jnp.minimum(z, 0.0) - jnp.log(1.0 + jnp.exp(-jnp.abs(z)))) * (1.0 / GLA_TAU)
    bcum = _tri_dot(tri, log_a)
    b_end = jnp.sum(log_a, axis=0, keepdims=True)
    return z, jnp.exp(b_end - bcum), jnp.exp(b_end)


def _gla_fwd(proj, w2p, bgate, ogain, phases=()):
    def body(p_ref, w2_ref, b_ref, og_ref, y_ref, st_out_ref, st_ref):
        @pl.when(pl.program_id(0) == 0)
        def _():
            st_ref[...] = jnp.zeros_like(st_ref)

        tri = _tri(False)

        def chunk(ci, carry):
            rows = pl.ds(pl.multiple_of(ci * C, C), C)
            glr_b = p_ref[rows, LR0:LR0 + LRP].astype(BF16)
            _, ea_all, dec_all = _gla_gates(glr_b, w2_ref[...], b_ref[...], tri)
            for h in range(H):
                ea, dec = ea_all[:, h * DK:(h + 1) * DK], dec_all[:, h * DK:(h + 1) * DK]
                k_dec = (p_ref[rows, K0 + h * DK:K0 + (h + 1) * DK] * ea).astype(BF16)
                v_b = p_ref[rows, V0 + h * DV:V0 + (h + 1) * DV].astype(BF16)
                st = st_ref[h] * dec + _dot(v_b, k_dec, TN)
                st_ref[h] = st
                st_b = st.astype(BF16)
                st_out_ref[ci, h] = st_b
                q_b = (p_ref[rows, Q0 + h * DK:Q0 + (h + 1) * DK] * (DK ** -0.5)).astype(BF16)
                o = _dot(q_b, st_b, NT)
                on = o * _rinv(o)
                g = p_ref[rows, G0 + h * DV:G0 + (h + 1) * DV]
                y_ref[rows, h * DV:(h + 1) * DV] = (on * og_ref[:, h * DV:(h + 1) * DV] * (g * _sigmoid(g))).astype(BF16)
            return carry

        lax.fori_loop(0, GLA_CPB, chunk, 0, unroll=True)

    blocks = GLA_RB * GLA_PAD * 4 + GLA_RB * D * 2 + GLA_CPB * H * DV * DK * 2
    return _hosted(
        body, name="gla_fwd", grid=(S // GLA_RB,),
        in_specs=[pl.BlockSpec((GLA_RB, GLA_PAD), lambda i: (i, 0)),
                  pl.BlockSpec((LRP, H * DK), lambda i: (0, 0)),
                  pl.BlockSpec((1, H * DK), lambda i: (0, 0)),
                  pl.BlockSpec((1, H * DV), lambda i: (0, 0))],
        out_specs=[pl.BlockSpec((GLA_RB, H * DV), lambda i: (i, 0)),
                   pl.BlockSpec((GLA_CPB, H, DV, DK), lambda i: (i, 0, 0, 0))],
        out_shape=[_sds((S, H * DV), BF16), _sds((NC, H, DV, DK), BF16)],
        args=[proj, w2p, bgate, ogain], scratch_shapes=[pltpu.VMEM((H, DV, DK), F32)],
        block_bytes=blocks, scratch_bytes=H * DV * DK * 4, phases=phases)


def _gla_bwd(proj, dypre, states, w2p, bgate, ogain, phases=()):
    nb = S // GLA_RB

    def body(p_ref, dy_ref, st_blk_ref, st_prev_ref, w2_ref, b_ref, og_ref,
             dp_ref, dog_ref, dbg_ref, dw2_ref, r_ref):
        step = pl.program_id(0)

        @pl.when(step == 0)
        def _():
            r_ref[...] = jnp.zeros_like(r_ref)
            dog_ref[...] = jnp.zeros_like(dog_ref)
            dbg_ref[...] = jnp.zeros_like(dbg_ref)
            dw2_ref[...] = jnp.zeros_like(dw2_ref)

        tri = _tri(False)
        tri_strict = _tri(True)
        has_prev = jnp.where(step < nb - 1, 1.0, 0.0).astype(F32)

        def chunk(ci, st_prev_of):
            rows = pl.ds(ci * C if isinstance(ci, int) else pl.multiple_of(ci * C, C), C)
            glr_b = p_ref[rows, LR0:LR0 + LRP].astype(BF16)
            z, ea_all, dec_all = _gla_gates(glr_b, w2_ref[...], b_ref[...], tri)
            d_a, d_end = [], []
            for h in range(H):
                kcol = slice(h * DK, (h + 1) * DK)
                vcol = slice(h * DV, (h + 1) * DV)
                ea, dec = ea_all[:, kcol], dec_all[:, kcol]
                k_dec = p_ref[rows, K0 + h * DK:K0 + (h + 1) * DK] * ea
                k_dec_b = k_dec.astype(BF16)
                v_b = p_ref[rows, V0 + h * DV:V0 + (h + 1) * DV].astype(BF16)
                q_b = (p_ref[rows, Q0 + h * DK:Q0 + (h + 1) * DK] * (DK ** -0.5)).astype(BF16)
                st_b = st_blk_ref[ci, h]
                o = _dot(q_b, st_b, NT)
                rinv = _rinv(o)
                on = o * rinv
                g = p_ref[rows, G0 + h * DV:G0 + (h + 1) * DV]
                sg = _sigmoid(g)
                og = og_ref[:, vcol]
                dyp = dy_ref[rows, vcol]
                dp_ref[rows, G0 + h * DV:G0 + (h + 1) * DV] = (dyp * (on * og) * (sg * (1.0 + g * (1.0 - sg)))).astype(BF16)
                dpn = dyp * (g * sg)
                dog_ref[:, vcol] += _colsum(dpn * on)
                do_b = _norm_bwd(dpn * og, on, rinv).astype(BF16)
                gt = _dot(do_b, q_b, TN) + r_ref[h]
                gt_b = gt.astype(BF16)
                dp_ref[rows, Q0 + h * DK:Q0 + (h + 1) * DK] = (_dot(do_b, st_b) * (DK ** -0.5)).astype(BF16)
                dkd = _dot(v_b, gt_b)
                dp_ref[rows, V0 + h * DV:V0 + (h + 1) * DV] = _dot(k_dec_b, gt_b, NT).astype(BF16)
                dp_ref[rows, K0 + h * DK:K0 + (h + 1) * DK] = (dkd * ea).astype(BF16)
                d_a.append(dkd * k_dec)
                d_end.append(_colsum(gt * st_prev_of(h)) * dec)
                r_ref[h] = gt * dec
            dla = _tri_dot(tri_strict, jnp.concatenate(d_a, axis=1)) + jnp.concatenate(d_end, axis=1)
            dz = dla * (1.0 / GLA_TAU) * (1.0 - _sigmoid(z))
            dz_b = dz.astype(BF16)
            dbg_ref[...] += _colsum(dz)
            dw2_ref[...] += _dot(glr_b, dz_b, TN)
            dp_ref[rows, LR0:LR0 + LRP] = _dot(dz_b, w2_ref[...], NT).astype(BF16)

        def later_chunk(t, carry):
            ci = GLA_CPB - 1 - t
            chunk(ci, lambda h: st_blk_ref[ci - 1, h].astype(F32))
            return carry

        lax.fori_loop(0, GLA_CPB - 1, later_chunk, 0, unroll=True)
        chunk(0, lambda h: st_prev_ref[0, h].astype(F32) * has_prev)

    blocks = (GLA_RB * GLA_PAD * 4 + GLA_RB * D * 4 + (GLA_CPB + 1) * H * DV * DK * 2 + GLA_RB * GLA_PAD * 2)
    rev = lambda i: nb - 1 - i
    return _hosted(
        body, name="gla_bwd", grid=(nb,),
        in_specs=[pl.BlockSpec((GLA_RB, GLA_PAD), lambda i: (rev(i), 0)),
                  pl.BlockSpec((GLA_RB, H * DV), lambda i: (rev(i), 0)),
                  pl.BlockSpec((GLA_CPB, H, DV, DK), lambda i: (rev(i), 0, 0, 0)),
                  pl.BlockSpec((1, H, DV, DK), lambda i: (jnp.maximum(rev(i) * GLA_CPB - 1, 0), 0, 0, 0)),
                  pl.BlockSpec((LRP, H * DK), lambda i: (0, 0)),
                  pl.BlockSpec((1, H * DK), lambda i: (0, 0)),
                  pl.BlockSpec((1, H * DV), lambda i: (0, 0))],
        out_specs=[pl.BlockSpec((GLA_RB, GLA_PAD), lambda i: (rev(i), 0)),
                   pl.BlockSpec((1, H * DV), lambda i: (0, 0)),
                   pl.BlockSpec((1, H * DK), lambda i: (0, 0)),
                   pl.BlockSpec((LRP, H * DK), lambda i: (0, 0))],
        out_shape=[_sds((S, GLA_PAD), BF16), _sds((1, H * DV), F32), _sds((1, H * DK), F32), _sds((LRP, H * DK), F32)],
        args=[proj, dypre, states, states, w2p, bgate, ogain], scratch_shapes=[pltpu.VMEM((H, DV, DK), F32)],
        block_bytes=blocks, scratch_bytes=H * DV * DK * 4, phases=phases)


SGU_RB = 256
GELU_C = 0.7978845608028654
GELU_A = 0.044715


def _gelu(x):
    return 0.5 * x * (1.0 + jnp.tanh(GELU_C * (x + GELU_A * x * x * x)))


def _gelu_grad(x):
    t = jnp.tanh(GELU_C * (x + GELU_A * x * x * x))
    return 0.5 * (1.0 + t) + 0.5 * x * (1.0 - t * t) * (GELU_C * (1.0 + 3.0 * GELU_A * x * x))


def _causal_mask(transposed=False):
    i = lax.broadcasted_iota(jnp.int32, (SGU_BLOCK, SGU_BLOCK), 1 if transposed else 0)
    j = lax.broadcasted_iota(jnp.int32, (SGU_BLOCK, SGU_BLOCK), 0 if transposed else 1)
    return (i >= C) | (j < C)


def _layer_norm(vf, gain, bias):
    mu = jnp.mean(vf, axis=-1, keepdims=True)
    cen = vf - mu
    rstd = lax.rsqrt(jnp.mean(cen * cen, axis=-1, keepdims=True) + EPS)
    xhat = cen * rstd
    return xhat, rstd, xhat * gain + bias


def _sgu_fwd(proj, lng, lnb, ws, bsb, phases=()):
    def body(p_ref, g_ref, b_ref, ws_ref, bs_ref, o_ref):
        mask = _causal_mask()
        for n in range(SGU_RB // SGU_BLOCK):
            rows = slice(n * SGU_BLOCK, (n + 1) * SGU_BLOCK)
            _, _, vn = _layer_norm(_gelu(p_ref[rows, D:2 * D]), g_ref[...], b_ref[...])
            vn_b = vn.astype(BF16)
            for gi in range(SGU_G):
                cols = slice(gi * SGU_GD, (gi + 1) * SGU_GD)
                w = jnp.where(mask, ws_ref[gi], 0.0).astype(BF16)
                vs = _dot(w, vn_b[:, cols]) + bs_ref[gi]
                gate = p_ref[rows, 2 * D + gi * SGU_GD:2 * D + (gi + 1) * SGU_GD]
                o_ref[rows, cols] = (_gelu(p_ref[rows, cols]) * vs * (gate * _sigmoid(gate))).astype(BF16)

    blocks = SGU_RB * SGU_COLS * 4 + SGU_RB * D * 2 + SGU_G * SGU_BLOCK * (SGU_BLOCK + SGU_GD) * 4
    return _hosted(
        body, name="sgu_fwd", grid=(S // SGU_RB,),
        in_specs=[pl.BlockSpec((SGU_RB, SGU_COLS), lambda i: (i, 0)),
                  pl.BlockSpec((1, D), lambda i: (0, 0)), pl.BlockSpec((1, D), lambda i: (0, 0)),
                  pl.BlockSpec((SGU_G, SGU_BLOCK, SGU_BLOCK), lambda i: (0, 0, 0)),
                  pl.BlockSpec((SGU_G, SGU_BLOCK, SGU_GD), lambda i: (0, 0, 0))],
        out_specs=[pl.BlockSpec((SGU_RB, D), lambda i: (i, 0))], out_shape=[_sds((S, D), BF16)],
        args=[proj, lng, lnb, ws, bsb], block_bytes=blocks, phases=phases)


def _sgu_bwd(proj, dpre, lng, lnb, ws, wst, bsb, phases=()):
    nsteps = S // SGU_RB

    def body(p_ref, d_ref, g_ref, b_ref, ws_ref, wst_ref, bs_ref,
             dp_ref, dg_ref, db_ref, dws_ref, dbs_ref, dvn_ref, dvs_acc_ref):
        step = pl.program_id(0)

        @pl.when(step == 0)
        def _():
            dg_ref[...] = jnp.zeros_like(dg_ref)
            db_ref[...] = jnp.zeros_like(db_ref)
            dws_ref[...] = jnp.zeros_like(dws_ref)
            dvs_acc_ref[...] = jnp.zeros_like(dvs_acc_ref)

        mask = _causal_mask()
        maskt = _causal_mask(transposed=True)
        for n in range(SGU_RB // SGU_BLOCK):
            rows = slice(n * SGU_BLOCK, (n + 1) * SGU_BLOCK)
            v = p_ref[rows, D:2 * D]
            xhat, rstd, vn = _layer_norm(_gelu(v), g_ref[...], b_ref[...])
            vn_b = vn.astype(BF16)
            for gi in range(SGU_G):
                cols = slice(gi * SGU_GD, (gi + 1) * SGU_GD)
                w = jnp.where(mask, ws_ref[gi], 0.0).astype(BF16)
                wt = jnp.where(maskt, wst_ref[gi], 0.0).astype(BF16)
                vs = _dot(w, vn_b[:, cols]) + bs_ref[gi]
                u = p_ref[rows, cols]
                gate = p_ref[rows, 2 * D + gi * SGU_GD:2 * D + (gi + 1) * SGU_GD]
                sg = _sigmoid(gate)
                gu = _gelu(u)
                dpre_g = d_ref[rows, cols]
                t = dpre_g * (gate * sg)
                dp_ref[rows, cols] = (t * vs * _gelu_grad(u)).astype(BF16)
                dp_ref[rows, 2 * D + gi * SGU_GD:2 * D + (gi + 1) * SGU_GD] = (
                    dpre_g * gu * vs * (sg * (1.0 + gate * (1.0 - sg)))).astype(BF16)
                dvs = t * gu
                dvs_b = dvs.astype(BF16)
                dvs_acc_ref[:, cols] += dvs
                dws_ref[gi] += _dot(dvs_b, vn_b[:, cols], NT)
                dvn_ref[:, cols] = _dot(wt, dvs_b)
            dvn = dvn_ref[...]
            dg_ref[...] += _colsum(dvn * xhat)
            db_ref[...] += _colsum(dvn)
            dxh = dvn * g_ref[...]
            dvf = rstd * (dxh - jnp.mean(dxh, axis=-1, keepdims=True) - xhat * jnp.mean(dxh * xhat, axis=-1, keepdims=True))
            dp_ref[rows, D:2 * D] = (dvf * _gelu_grad(v)).astype(BF16)

        @pl.when(step == nsteps - 1)
        def _():
            lane = lax.broadcasted_iota(jnp.int32, (SGU_BLOCK, SGU_BLOCK), 1)
            out = jnp.zeros((SGU_BLOCK, SGU_BLOCK), F32)
            for gi in range(SGU_G):
                out = out + jnp.where(lane == gi, jnp.sum(dvs_acc_ref[:, gi * SGU_GD:(gi + 1) * SGU_GD], axis=1, keepdims=True), 0.0)
                dws_ref[gi] = jnp.where(mask, dws_ref[gi], 0.0)
            dbs_ref[...] = out

    blocks = SGU_RB * SGU_COLS * 6 + SGU_RB * D * 4 + SGU_G * SGU_BLOCK * (3 * SGU_BLOCK + SGU_GD) * 4
    const3 = lambda i: (0, 0, 0)
    return _hosted(
        body, name="sgu_bwd", grid=(nsteps,),
        in_specs=[pl.BlockSpec((SGU_RB, SGU_COLS), lambda i: (i, 0)),
                  pl.BlockSpec((SGU_RB, D), lambda i: (i, 0)),
                  pl.BlockSpec((1, D), lambda i: (0, 0)), pl.BlockSpec((1, D), lambda i: (0, 0)),
                  pl.BlockSpec((SGU_G, SGU_BLOCK, SGU_BLOCK), const3),
                  pl.BlockSpec((SGU_G, SGU_BLOCK, SGU_BLOCK), const3),
                  pl.BlockSpec((SGU_G, SGU_BLOCK, SGU_GD), const3)],
        out_specs=[pl.BlockSpec((SGU_RB, SGU_COLS), lambda i: (i, 0)),
                   pl.BlockSpec((1, D), lambda i: (0, 0)), pl.BlockSpec((1, D), lambda i: (0, 0)),
                   pl.BlockSpec((SGU_G, SGU_BLOCK, SGU_BLOCK), const3),
                   pl.BlockSpec((SGU_BLOCK, SGU_BLOCK), lambda i: (0, 0))],
        out_shape=[_sds((S, SGU_COLS), BF16), _sds((1, D), F32), _sds((1, D), F32),
                   _sds((SGU_G, SGU_BLOCK, SGU_BLOCK), F32), _sds((SGU_BLOCK, SGU_BLOCK), F32)],
        args=[proj, dpre, lng, lnb, ws, wst, bsb],
        scratch_shapes=[pltpu.VMEM((SGU_BLOCK, D), F32), pltpu.VMEM((SGU_BLOCK, D), F32)],
        block_bytes=blocks, scratch_bytes=2 * SGU_BLOCK * D * 4, phases=phases)


def _pair_sum(own, a, r0, nr, name, table=None):
    c = own.shape[2]
    tr = 256
    assert r0 % tr == 0 and nr % tr == 0

    def body(own_ref, sib_ref, o_ref):
        o_ref[...] = (own_ref[...].astype(F32) + sib_ref[...].astype(F32)).astype(BF16)

    own_map = ((lambda j, i: (1 + j, r0 // tr + i, 0)) if table is None else
               (lambda j, i, t: (t[1 + j], r0 // tr + i, 0)))
    cpad = -(-c // 128) * 128
    outs, _ = _hosted(
        body, name=name, grid=(3, nr // tr),
        in_specs=[pl.BlockSpec((None, tr, c), own_map),
                  pl.BlockSpec((None, tr, c), lambda j, i, *t: (1 + j, r0 // tr + i, 0))],
        out_specs=[pl.BlockSpec((None, tr, c), lambda j, i, *t: (j, i, 0))], out_shape=[_sds((3, nr, c), BF16)],
        args=[own, a], block_bytes=3 * tr * cpad * 2, table=table)
    return outs[0]


def _adamw_math(w, g, m, v):
    m = ADAM_B1 * m + (1.0 - ADAM_B1) * g
    v = ADAM_B2 * v + (1.0 - ADAM_B2) * (g * g)
    m_hat = m / (1.0 - ADAM_B1 ** ADAM_STEP)
    v_hat = v / (1.0 - ADAM_B2 ** ADAM_STEP)
    delta = -ADAM_LR * (m_hat / (jnp.sqrt(v_hat) + ADAM_EPS) + ADAM_WD * w)
    return delta, m, v


def _sum_adamw(own, a, b, w, m, v, *, name, phases=(), table=None):
    r, c = w.shape
    tr = 256

    def body(own_ref, sib_ref, far_ref, w_ref, m_ref, v_ref, g_ref, d_ref, nm_ref, nv_ref):
        g = own_ref[...].astype(F32) + sib_ref[...].astype(F32)
        for j in range(3):
            g = g + far_ref[j].astype(F32)
        g_ref[...] = g
        d_ref[...], nm_ref[...], nv_ref[...] = _adamw_math(w_ref[...], g, m_ref[...], v_ref[...])

    spec = pl.BlockSpec((tr, c), lambda i, *t: (i, 0))
    own_map = (lambda i: (0, i, 0)) if table is None else (lambda i, t: (t[0], i, 0))
    cpad = -(-c // 128) * 128
    return _hosted(
        body, name=name, grid=(r // tr,),
        in_specs=[pl.BlockSpec((None, tr, c), own_map), pl.BlockSpec((None, tr, c), lambda i, *t: (0, i, 0)),
                  pl.BlockSpec((3, tr, c), lambda i, *t: (0, i, 0)), spec, spec, spec],
        out_specs=[spec] * 4, out_shape=[_sds((r, c), F32)] * 4, args=[own, a, b, w, m, v],
        block_bytes=5 * tr * cpad * 2 + 7 * tr * cpad * 4, phases=phases, table=table)


def _sum_parts(parts, name):
    n, r, c = parts.shape

    def body(p_ref, o_ref):
        g = p_ref[0]
        for j in range(1, n):
            g = g + p_ref[j]
        o_ref[...] = g

    outs, _ = _hosted(body, name=name, grid=(1,), in_specs=[pl.BlockSpec((n, r, c), lambda i: (0, 0, 0))],
                      out_specs=[pl.BlockSpec((r, c), lambda i: (0, 0))], out_shape=[_sds((r, c), F32)], args=[parts],
                      block_bytes=(n + 1) * r * c * 4)
    return outs[0]


def _adamw(w, g, m, v, name):
    def body(w_ref, g_ref, m_ref, v_ref, d_ref, nm_ref, nv_ref):
        d_ref[...], nm_ref[...], nv_ref[...] = _adamw_math(w_ref[...], g_ref[...], m_ref[...], v_ref[...])

    spec = pl.BlockSpec(w.shape, lambda i: (0, 0))
    outs, _ = _hosted(body, name=name, grid=(1,), in_specs=[spec] * 4, out_specs=[spec] * 3, out_shape=[_sds(w.shape, F32)] * 3,
                      args=[w, g, m, v], block_bytes=7 * _nbytes(w.shape, F32))
    return outs


def _blocks_to_columns(g):
    n, r, c = g.shape
    return jnp.transpose(g, (1, 0, 2)).reshape(r, n * c)


def _pack(parts):
    return jnp.concatenate([p.reshape(-1) for p in parts]).reshape(-1, 128)


def _unpack(packed, like):
    flat, outs, off = packed.reshape(-1), [], 0
    for p in like:
        outs.append(flat[off:off + p.size].reshape(p.shape))
        off += p.size
    return outs


def kernel(x, norm_pre, norm_post, gla_w_in, gla_w_gate2, gla_b_gate, gla_o_gain, gla_w_out, sgu_w_in, sgu_ln_gain, sgu_ln_bias, sgu_w_spatial, sgu_b_spatial, sgu_w_out, loss_target, m_norm_pre, m_norm_post, m_gla_w_in, m_gla_w_gate2, m_gla_b_gate, m_gla_o_gain, m_gla_w_out, m_sgu_w_in, m_sgu_ln_gain, m_sgu_ln_bias, m_sgu_w_spatial, m_sgu_b_spatial, m_sgu_w_out, v_norm_pre, v_norm_post, v_gla_w_in, v_gla_w_gate2, v_gla_b_gate, v_gla_o_gain, v_gla_w_out, v_sgu_w_in, v_sgu_ln_gain, v_sgu_ln_bias, v_sgu_w_spatial, v_sgu_b_spatial, v_sgu_w_out):
    me = _index_of(*_place())
    x0 = x.reshape(S, D)
    tgt = loss_target.reshape(S, D)
    npre0, npre1 = norm_pre[0:1], norm_pre[1:2]
    npost0, npost1 = norm_post[0:1], norm_post[1:2]
    ws = sgu_w_spatial[0]
    wst = jnp.transpose(ws, (0, 2, 1))
    bsb = jnp.broadcast_to(sgu_b_spatial[0][:, :, None], (SGU_G, SGU_BLOCK, SGU_GD))
    W_ROWS = D // N_DEV
    IN_COLS_G, IN_COLS_S = GLA_COLS // N_DEV, SGU_COLS // N_DEV

    s_gwi, s_gwo = gla_w_in[0].astype(BF16), gla_w_out[0].astype(BF16)
    s_swi, s_swo = sgu_w_in[0].astype(BF16), sgu_w_out[0].astype(BF16)
    small = jnp.concatenate([jnp.pad(gla_w_gate2[0].reshape(4, 512), ((0, 4), (0, 0))),
                             jnp.pad(jnp.concatenate([sgu_ln_gain, sgu_ln_bias], axis=1), ((0, 7), (0, 0)))], axis=0)

    wg_in, g_small = _gather_first(s_gwi, small, "gather_first")
    w2 =_blocks_to_columns(g_small[:, :4, :].reshape(N_DEV, LR, 128))
    w2p = jnp.pad(w2, ((0, LRP - LR), (0, 0))).astype(BF16)
    lng = g_small[:, 8, :256].reshape(1, D)
    lnb = g_small[:, 8, 256:].reshape(1, D)
    like_gwo, like_swi = _sds((N_DEV, W_ROWS, D), BF16), _sds((N_DEV, D, IN_COLS_S), BF16)

    h0 = _prenorm(x0, npre0)
    proj0, (g_gwo, g_swi) = _mm(h0, wg_in, "nn", F32, tm=1024, tn=896, tk=D, name="gla_in", b_tiled=True, phases=[
        _Phase(like_gwo, None, [_gather_send(s_gwo, 0, W_ROWS)]),
        _Phase(like_swi, None, [_gather_send(s_swi, 0, 768, diagonal=False)])])
    (ypre0, states), (g_gwo, g_swi) = _gla_fwd(proj0, w2p, gla_b_gate, gla_o_gain, phases=[
        _Phase(like_gwo, g_gwo, [_gather_pass(0, W_ROWS)]),
        _Phase(like_swi, g_swi, [_gather_relay(0, 768), _gather_send(s_swi, 768, 512, diagonal=False)])])
    wg_out = g_gwo.reshape(D, D)
    y0, (g_swi,) = _mm(ypre0, wg_out, "nn", F32, tm=1024, tn=1024, tk=D, name="gla_out", phases=[
        _Phase(like_swi, g_swi, [_gather_pass(0, 768), _gather_relay(768, 512), _gather_send(s_swi, 1280, 512, diagonal=False)])])
    (x1, h1), (g_swi,) = _mid_fwd(x0, y0, npost0, npre1, phases=[
        _Phase(like_swi, g_swi, [_gather_pass(768, 512), _gather_relay(1280, 512), _gather_send(s_swi, 1792, 256, diagonal=False)])])
    g_swi, = _carry([_Phase(like_swi, g_swi, [_gather_pass(1280, 512), _gather_relay(1792, 256)])], "relay_sgu_w_in")
    g_swi, = _carry([_Phase(like_swi, g_swi, [_gather_pass(1792, 256)])], "pass_sgu_w_in")
    proj1, (g_swo,) = _mm(h1, g_swi, "nn", F32, tm=1024, tn=IN_COLS_S, tk=D, name="sgu_in", b_blocked=True, phases=[
        _Phase(like_gwo, None, [_gather_send(s_swo, 0, W_ROWS)])])
    (pre1,), (g_swo,) = _sgu_fwd(proj1, lng, lnb, ws, bsb, phases=[_Phase(like_gwo, g_swo, [_gather_pass(0, W_ROWS)])])
    ws_out = g_swo.reshape(D, D)
    loss_cols, dx2, dy1, dnpost1 = _nn_rows(pre1, ws_out, name="sgu_out", row_ins=[x1, tgt], vec_ins=[npost1],
                                            outs=[("vec", F32), ("row", F32), ("row", BF16), ("vec", F32)], tail=_final_tail)
    loss_here = jnp.pad((0.5 * jnp.sum(loss_cols) / D).reshape(1, 1), ((0, 7), (0, 127)))

    like_b_out, like_b_swi = _sds((3, W_ROWS, D), BF16), _sds((3, D, IN_COLS_S), BF16)
    like_b_gwi = _sds((3, D, IN_COLS_G), BF16)
    row_pair = dict(like=_sds((4, W_ROWS, D), BF16), block=lambda i, j: i, ordinal=lambda i, j: i >> 1,
                    dst=lambda ref, k, i, j: ref.at[k])
    col_pair = dict(like=_sds((4, D, IN_COLS_S), BF16), block=lambda i, j: j, ordinal=lambda i, j: 4 * i + (j >> 1),
                    dst=lambda ref, k, i, j: ref.at[k, pl.ds(pl.multiple_of(i * 1024, 1024), 1024)])

    mine = _own_table()
    dws_out, (a_swo,) = _mm(pre1, dy1, "tn", BF16, tm=W_ROWS, tn=D, tk=S, name="sgu_out_dw", pair=row_pair)
    p_swo = dws_out.reshape(N_DEV, W_ROWS, D)
    t_swo = _pair_sum(p_swo, a_swo, 0, W_ROWS, "pair_sum_sgu_w_out", table=mine)
    dpre1, _ = _mm(dy1, ws_out, "nt", F32, tm=1024, tn=1024, tk=D, name="sgu_out_dx")
    (dproj1, dlng, dlnb, dwsp, dbsp), (b_swo,) = _sgu_bwd(proj1, dpre1, lng, lnb, ws, wst, bsb, phases=[
        _Phase(like_b_out, None, [_reduce_cross(t_swo, 0, 0, W_ROWS)])])
    p_swi, (a_swi,) = _mm(h1, dproj1, "tn", BF16, tm=1024, tn=IN_COLS_S, tk=S, name="sgu_in_dw", out_blocked=True, pair=col_pair)
    t_swi = _pair_sum(p_swi, a_swi, 0, D, "pair_sum_sgu_w_in", table=mine)
    (dx1, dy0, dnpre1, dnpost0), (b_swi,) = _nt_rows(
        dproj1, g_swi, tk=IN_COLS_S, name="sgu_in_dx", b_blocked=True, row_ins=[dx2, x1, y0], vec_ins=[npre1, npost0],
        outs=[("row", F32), ("row", BF16), ("vec", F32), ("vec", F32)], tail=_mid_bwd_tail, phases=[
            _Phase(like_b_swi, None, [_reduce_cross(t_swi, 0, 0, 1280)])])
    dwg_out, (a_gwo,) = _mm(ypre0, dy0, "tn", BF16, tm=W_ROWS, tn=D, tk=S, name="gla_out_dw", pair=row_pair)
    p_gwo = dwg_out.reshape(N_DEV, W_ROWS, D)
    t_gwo = _pair_sum(p_gwo, a_gwo, 0, W_ROWS, "pair_sum_gla_w_out", table=mine)
    dypre0, _ = _mm(dy0, wg_out, "nt", F32, tm=1024, tn=1024, tk=D, name="gla_out_dx")
    late = [dnpre1, dnpost1, dlng, dlnb, dwsp, jnp.transpose(dbsp[:, :SGU_G])]
    late_pack = _pack(late)
    (dproj0, dogain, dbgate, dw2), (b_swi, b_gwo, g_late) = _gla_bwd(proj0, dypre0, states, w2p, gla_b_gate, gla_o_gain, phases=[
        _Phase(like_b_swi, b_swi, [_reduce_cross(t_swi, 1280, 1280, 768)]),
        _Phase(like_b_out, None, [_reduce_cross(t_gwo, 0, 0, W_ROWS)]),
        _Phase(_sds((N_DEV,) + late_pack.shape, F32), None, [_gather_send(late_pack, 0, late_pack.shape[0])])])
    half = D // 2
    dwg_in_a, (g_late,) = _mm(h0, dproj0, "tn", BF16, tm=half, tn=896, tk=S, name="gla_in_dw_a", m_tiles=(0, 1), phases=[
        _Phase(_sds((N_DEV,) + late_pack.shape, F32), g_late, [_gather_pass(0, late_pack.shape[0])])])
    own_gwi, a_gwi = _blockify_pair(dwg_in_a, None, None, 0, "blockify_gla_w_in_a")
    t_gwi_a = _pair_sum(own_gwi, a_gwi, 0, half, "pair_sum_gla_w_in_a")
    dwg_in_b, (b_gwi,) = _mm(h0, dproj0, "tn", BF16, tm=half, tn=896, tk=S, name="gla_in_dw_b", m_tiles=(1, 1), phases=[
        _Phase(like_b_gwi, None, [_reduce_cross(t_gwi_a, 0, 0, 512)])])
    own_gwi, a_gwi = _blockify_pair(dwg_in_b, own_gwi, a_gwi, half, "blockify_gla_w_in_b")
    t_gwi_b = _pair_sum(own_gwi, a_gwi, half, half, "pair_sum_gla_w_in_b")
    (grad_x, dnpre0), (b_gwi,) = _nt_rows(
        dproj0, wg_in, tk=896, name="gla_in_dx", b_tiled=True, row_ins=[dx1, x0], vec_ins=[npre0],
        outs=[("row", F32), ("vec", F32)], tail=_first_bwd_tail, phases=[
            _Phase(like_b_gwi, b_gwi, [_reduce_cross(t_gwi_a, 512, 512, 512), _reduce_cross(t_gwi_b, 0, half, half)])])

    early = [dnpre0, dnpost0, dbgate, dogain, dw2[:LR], loss_here]
    early_pack = _pack(early)
    like_early = _sds((N_DEV,) + early_pack.shape, F32)
    (g_swo, d_swo, nm_swo, nv_swo), (g_early,) = _sum_adamw(
        p_swo, a_swo, b_swo, sgu_w_out[0], m_sgu_w_out[0], v_sgu_w_out[0], name="adamw_sgu_w_out", table=mine, phases=[
            _Phase(like_early, None, [_gather_send(early_pack, 0, early_pack.shape[0])])])
    (g_gwo_, d_gwo, nm_gwo, nv_gwo), (g_early,) = _sum_adamw(
        p_gwo, a_gwo, b_gwo, gla_w_out[0], m_gla_w_out[0], v_gla_w_out[0], name="adamw_gla_w_out", table=mine, phases=[
            _Phase(like_early, g_early, [_gather_pass(0, early_pack.shape[0])])])
    (g_swi_, d_swi, nm_swi, nv_swi), _ = _sum_adamw(
        p_swi, a_swi, b_swi, sgu_w_in[0], m_sgu_w_in[0], v_sgu_w_in[0], name="adamw_sgu_w_in", table=mine)
    (g_gwi_, d_gwi, nm_gwi, nv_gwi), _ = _sum_adamw(
        own_gwi, a_gwi, b_gwi, gla_w_in[0], m_gla_w_in[0], v_gla_w_in[0], name="adamw_gla_w_in")

    g_npre1, g_npost1, g_lng_full, g_lnb_full, g_wsp, g_bsp = _unpack(_sum_parts(g_late, "sum_late_small_grads"), late)
    g_npre0, g_npost0, g_bgate, g_ogain, g_w2_full, loss_all = _unpack(_sum_parts(g_early, "sum_early_small_grads"), early)
    loss = loss_all[0, 0]
    g_w2 = lax.dynamic_slice(g_w2_full, (0, me * 128), (LR, 128))
    g_lng = lax.dynamic_slice(g_lng_full, (0, me * 256), (1, 256))
    g_lnb = lax.dynamic_slice(g_lnb_full, (0, me * 256), (1, 256))
    small_g = [jnp.concatenate([g_npre0, g_npre1], 0), jnp.concatenate([g_npost0, g_npost1], 0), g_w2, g_bgate, g_ogain,
               g_lng, g_lnb, g_wsp, g_bsp]
    small_w = [norm_pre, norm_post, gla_w_gate2[0], gla_b_gate, gla_o_gain, sgu_ln_gain, sgu_ln_bias, sgu_w_spatial[0], sgu_b_spatial[0]]
    small_m = [m_norm_pre, m_norm_post, m_gla_w_gate2[0], m_gla_b_gate, m_gla_o_gain, m_sgu_ln_gain, m_sgu_ln_bias, m_sgu_w_spatial[0], m_sgu_b_spatial[0]]
    small_v = [v_norm_pre, v_norm_post, v_gla_w_gate2[0], v_gla_b_gate, v_gla_o_gain, v_sgu_ln_gain, v_sgu_ln_bias, v_sgu_w_spatial[0], v_sgu_b_spatial[0]]
    d_pack, nm_pack, nv_pack = _adamw(_pack(small_w), _pack(small_g), _pack(small_m), _pack(small_v), "adamw_small")

    out_like = [norm_pre, norm_post, gla_w_gate2, gla_b_gate, gla_o_gain, sgu_ln_gain, sgu_ln_bias, sgu_w_spatial, sgu_b_spatial]
    sg_ = [g.reshape(s.shape) for g, s in zip(small_g, out_like)]
    sd_, sm_, sv_ = (_unpack(pk, out_like) for pk in (d_pack, nm_pack, nv_pack))

    def assemble(small_list, w_in_g, w_out_g, w_in_s, w_out_s):
        npre_, npost_, w2_, bg_, og_, lg_, lb_, wsp_, bsp_ = small_list
        return [npre_, npost_, w_in_g[None], w2_, bg_, og_, w_out_g[None], w_in_s[None], lg_, lb_, wsp_, bsp_, w_out_s[None]]

    return (loss, grad_x.reshape(1, S, D),
            *assemble(sg_, g_gwi_, g_gwo_, g_swi_, g_swo),
            *assemble(sd_, d_gwi, d_gwo, d_swi, d_swo),
            *assemble(sm_, nm_gwi, nm_gwo, nm_swi, nm_swo),
            *assemble(sv_, nv_gwi, nv_gwo, nv_swi, nv_swo))
```

```python
import functools

import jax
import jax.numpy as jnp
from jax import lax
from jax.experimental import pallas as pl
from jax.experimental.pallas import tpu as pltpu

F32 = jnp.float32
BF16 = jnp.bfloat16

N_DEV = 8
S = 2048
D = 2048
H = 4
DK = 256
DV = 512
C = 64
NC = S // C
GLA_COLS = 6160
GLA_PAD = 6272
Q0, K0, V0, G0, LR0 = 0, 1024, 2048, 4096, 6144
LR = 16
LRP = 128
SGU_COLS = 6144
SGU_BLOCK = 128
SGU_G = 8
SGU_GD = 256
EPS = 1e-6
GLA_TAU = 16.0

ADAM_LR, ADAM_B1, ADAM_B2, ADAM_EPS, ADAM_WD, ADAM_STEP = 0.001, 0.9, 0.999, 1e-08, 0.01, 10

V7X_VMEM_BYTES = 64 * 1024 * 1024
VMEM_CEILING = V7X_VMEM_BYTES - 6 * 1024 * 1024
MESH = pl.DeviceIdType.MESH
HBM_SPEC = pl.BlockSpec(memory_space=pl.ANY)


def _sds(shape, dtype):
    return jax.ShapeDtypeStruct(tuple(shape), dtype)


def _nbytes(shape, dtype):
    n = 1
    for s in shape:
        n *= s
    return n * jnp.dtype(dtype).itemsize


def _dot(a, b, dims=(((1,), (0,)), ((), ())), precision=None):
    return lax.dot_general(a, b, dims, precision=precision, preferred_element_type=F32)


NN = (((1,), (0,)), ((), ()))
TN = (((0,), (0,)), ((), ()))
NT = (((1,), (1,)), ((), ()))


def _place():
    return lax.axis_index("x"), lax.axis_index("y"), lax.axis_index("c")


def _index_of(px, py, pc):
    return 4 * px + 2 * py + pc


def _chips(x, y):
    return [(1 - x, y), (x, 1 - y), (1 - x, 1 - y)]


def _rcopy(src, dst, send_sem, recv_sem, to):
    return pltpu.make_async_remote_copy(src_ref=src, dst_ref=dst, send_sem=send_sem, recv_sem=recv_sem,
                                        device_id=to, device_id_type=MESH)


class _Move:
    def __init__(self, ins, n_remote, make, stage=None):
        self.ins, self.n_remote, self.make, self.stage = list(ins), n_remote, make, stage

    def scratch(self):
        sems = [pltpu.SemaphoreType.DMA((self.n_remote,)), pltpu.SemaphoreType.DMA((self.n_remote,))]
        return sems if self.stage is None else sems + [pltpu.SemaphoreType.DMA((1,)), pltpu.VMEM(*self.stage)]

    def start(self, in_refs, buf, scratch):
        sends, _, local = self.make(in_refs, buf, scratch[0], scratch[1])
        if local is not None:
            pltpu.make_async_copy(local[0], scratch[3], scratch[2].at[0]).start()
        for cp in sends:
            cp.start()

    def finish(self, in_refs, buf, scratch):
        sends, arrivals, local = self.make(in_refs, buf, scratch[0], scratch[1])
        if local is not None:
            pltpu.make_async_copy(local[0], scratch[3], scratch[2].at[0]).wait()
            out = pltpu.make_async_copy(scratch[3], local[1], scratch[2].at[0])
            out.start()
        for cp in arrivals:
            cp.wait_recv()
        for cp in sends:
            cp.wait_send()
        if local is not None:
            out.wait()


class _Phase:
    def __init__(self, like, so_far, moves):
        self.like, self.so_far, self.moves = like, so_far, list(moves)


def _gather_send(shard, r0, nr, diagonal=True):
    def make(in_refs, g, ss, rs):
        sh, = in_refs
        x, y, c = _place()
        me = _index_of(x, y, c)
        rows = pl.ds(r0, nr)
        peers = [(x, y, 1 - c)] + [(px, py, c) for px, py in _chips(x, y)[:3 if diagonal else 2]]
        sends = [_rcopy(sh.at[rows], g.at[me, rows], ss.at[k], rs.at[k], p) for k, p in enumerate(peers)]
        arrivals = [_rcopy(sh.at[rows], g.at[_index_of(*p), rows], ss.at[k], rs.at[k], p) for k, p in enumerate(peers)]
        return sends, arrivals, (sh.at[rows], g.at[me, rows])

    return _Move([shard], 4 if diagonal else 3, make, stage=((nr, shard.shape[1]), shard.dtype))


def _gather_relay(r0, nr):
    def make(in_refs, g, ss, rs):
        x, y, c = _place()
        nx, ny, nd = [(px, py, c) for px, py in _chips(x, y)]
        first, second = pl.ds(r0, nr // 2), pl.ds(r0 + nr // 2, nr // 2)
        sends = [_rcopy(g.at[_index_of(*nx), first], g.at[_index_of(*nx), first], ss.at[0], rs.at[0], ny),
                 _rcopy(g.at[_index_of(*ny), second], g.at[_index_of(*ny), second], ss.at[1], rs.at[1], nx)]
        arrivals = [_rcopy(g.at[_index_of(*nx), first], g.at[_index_of(*nd), first], ss.at[0], rs.at[0], ny),
                    _rcopy(g.at[_index_of(*ny), second], g.at[_index_of(*nd), second], ss.at[1], rs.at[1], nx)]
        return sends, arrivals, None

    return _Move([], 2, make)


def _gather_pass(r0, nr):
    def make(in_refs, g, ss, rs):
        x, y, c = _place()
        rows = pl.ds(r0, nr)
        sends = [_rcopy(g.at[_index_of(px, py, c), rows], g.at[_index_of(px, py, c), rows], ss.at[j], rs.at[j], (x, y, 1 - c))
                 for j, (px, py) in enumerate(_chips(x, y))]
        arrivals = [_rcopy(g.at[_index_of(px, py, c), rows], g.at[_index_of(px, py, 1 - c), rows], ss.at[j], rs.at[j], (x, y, 1 - c))
                    for j, (px, py) in enumerate(_chips(x, y))]
        return sends, arrivals, None

    return _Move([], 3, make)


def _own_table():
    x, y, c = _place()
    return jnp.stack([_index_of(px, py, c) for px, py in [(x, y)] + _chips(x, y)]).astype(jnp.int32)


def _blockify_pair(dw, own_so_far, a_so_far, dst_r0, name):
    rows, tr, cw, win = dw.shape[0], 256, GLA_COLS // N_DEV, 896
    n_steps = rows // tr

    def body(*refs):
        x_ref, own_ref, a_ref, stage_ref, send_sems, recv_sem = refs[0], *refs[-5:]
        i = pl.program_id(0)
        x, y, c = _place()

        def send(slot, k):
            dst = a_ref.at[k, pl.ds(pl.multiple_of(dst_r0 + i * tr, tr), tr)]
            return _rcopy(stage_ref.at[slot], dst, send_sems.at[slot], recv_sem.at[0], (x, y, 1 - c))

        for j in range(N_DEV):
            window = x_ref[:, 768 * j:768 * j + win].astype(F32)
            tile = (pltpu.roll(window, win - 2 * j, 1) if j else window)[:, :cw].astype(BF16)
            k = ((j >> 2) ^ x) + 2 * (((j >> 1) & 1) ^ y)

            @pl.when((j & 1) == c)
            def _():
                own_ref[k] = tile

            @pl.when((j & 1) != c)
            def _():
                slot = (j >> 1) & 1
                if j >> 1 >= 2:
                    send(slot, k).wait_send()
                else:
                    pl.when(i > 0)(lambda: send(slot, k).wait_send())
                stage_ref[slot] = tile
                send(slot, k).start()

        @pl.when(i == n_steps - 1)
        def _():
            send(0, 0).wait_send()
            send(1, 0).wait_send()
            arrived = a_ref.at[:, pl.ds(dst_r0, rows)]
            _rcopy(arrived, arrived, send_sems.at[0], recv_sem.at[0], (x, y, 1 - c)).wait_recv()

    continues = a_so_far is not None
    own, a = pl.pallas_call(
        body, grid=(n_steps,),
        in_specs=[pl.BlockSpec((tr, GLA_PAD), lambda i: (i, 0))] + [HBM_SPEC] * (2 * continues),
        out_specs=[pl.BlockSpec((4, tr, cw), lambda i: (0, dst_r0 // tr + i, 0)), HBM_SPEC],
        out_shape=[_sds((4, D, cw), BF16), _sds((4, D, cw), BF16)],
        scratch_shapes=[pltpu.VMEM((2, tr, cw), BF16), pltpu.SemaphoreType.DMA((2,)), pltpu.SemaphoreType.DMA((1,))],
        input_output_aliases={1: 0, 2: 1} if continues else {},
        compiler_params=pltpu.CompilerParams(dimension_semantics=("arbitrary",), vmem_limit_bytes=48 * 1024 * 1024),
        name=name,
    )(*([dw] + [own_so_far, a_so_far] * continues))
    return own, a


def _reduce_cross(sums, src_r0, dst_r0, nr):
    def make(in_refs, b, ss, rs):
        t, = in_refs
        x, y, c = _place()
        src, dst = pl.ds(src_r0, nr), pl.ds(dst_r0, nr)
        sends = [_rcopy(t.at[j, src], b.at[j, dst], ss.at[j], rs.at[j], (px, py, c)) for j, (px, py) in enumerate(_chips(x, y))]
        return sends, sends, None

    return _Move([sums], 3, make)


def _hosted(body, *, name, grid, in_specs, out_specs, out_shape, args, scratch_shapes=(), block_bytes, scratch_bytes=0,
            phases=(), table=None):
    n_in, n_out, n_scr = len(args), len(out_shape), len(scratch_shapes)
    all_args, all_out_shape, sems, aliases, layout = list(args), list(out_shape), [], {}, []
    for j, ph in enumerate(phases):
        counts = []
        for mv in ph.moves:
            all_args += mv.ins
            counts.append(len(mv.ins))
            sems += mv.scratch()
        if ph.so_far is not None:
            aliases[len(all_args)] = n_out + j
            all_args.append(ph.so_far)
        layout.append((counts, ph.so_far is not None))
        all_out_shape.append(ph.like)
    n_extra_in = len(all_args) - n_in

    def wrapped(*refs):
        ins, pos = refs[:n_in], n_in
        move_ins = []
        for counts, continues in layout:
            per_move = []
            for cnt in counts:
                per_move.append(refs[pos:pos + cnt])
                pos += cnt
            pos += continues
            move_ins.append(per_move)
        outs = refs[pos:pos + n_out]
        bufs = refs[pos + n_out:pos + n_out + len(phases)]
        pos += n_out + len(phases)
        scratch = refs[pos:pos + n_scr]
        pos += n_scr
        move_sems = []
        for ph in phases:
            per_move = []
            for mv in ph.moves:
                count = len(mv.scratch())
                per_move.append(refs[pos:pos + count])
                pos += count
            move_sems.append(per_move)

        def each_move(fn_name):
            for ph, buf, per_in, per_sem in zip(phases, bufs, move_ins, move_sems):
                for mv, mv_in, mv_sem in zip(ph.moves, per_in, per_sem):
                    getattr(mv, fn_name)(mv_in, buf, mv_sem)

        if phases:
            first = functools.reduce(jnp.logical_and, [pl.program_id(a) == 0 for a in range(len(grid))])
            last = functools.reduce(jnp.logical_and, [pl.program_id(a) == grid[a] - 1 for a in range(len(grid))])
            pl.when(first)(lambda: each_move("start"))
        body(*ins, *outs, *scratch)
        if phases:
            pl.when(last)(lambda: each_move("finish"))

    all_args = [pltpu.with_memory_space_constraint(a, pltpu.HBM) for a in all_args]
    est = 2 * block_bytes + scratch_bytes
    params = pltpu.CompilerParams(dimension_semantics=("arbitrary",) * len(grid),
                                  vmem_limit_bytes=min(VMEM_CEILING, max(32 * 1024 * 1024, 2 * est)))
    all_in_specs, all_out_specs = list(in_specs) + [HBM_SPEC] * n_extra_in, list(out_specs) + [HBM_SPEC] * len(phases)
    if table is None:
        results = pl.pallas_call(
            wrapped, grid=grid, in_specs=all_in_specs, out_specs=all_out_specs, out_shape=all_out_shape,
            scratch_shapes=list(scratch_shapes) + sems, input_output_aliases=aliases, compiler_params=params, name=name,
        )(*all_args)
    else:
        results = pl.pallas_call(
            lambda table_ref, *refs: wrapped(*refs),
            grid_spec=pltpu.PrefetchScalarGridSpec(num_scalar_prefetch=1, grid=grid, in_specs=all_in_specs, out_specs=all_out_specs,
                                                   scratch_shapes=list(scratch_shapes) + sems),
            out_shape=all_out_shape, input_output_aliases={k + 1: v for k, v in aliases.items()}, compiler_params=params, name=name,
        )(table, *all_args)
    return list(results[:n_out]), list(results[n_out:])


class _Both:
    def __init__(self, copies):
        self.copies = copies

    def start(self):
        for cp in self.copies:
            cp.start()

    def wait_send(self):
        for cp in self.copies:
            cp.wait_send()

    def wait_recv(self):
        for cp in self.copies:
            cp.wait_recv()


def _carry(phases, name):
    def body(o_ref):
        o_ref[...] = jnp.zeros_like(o_ref)

    _, bufs = _hosted(body, name=name, grid=(1,), in_specs=[], out_specs=[pl.BlockSpec((8, 128), lambda i: (0, 0))],
                      out_shape=[_sds((8, 128), F32)], args=[], block_bytes=8 * 128 * 4, phases=phases)
    return bufs


def _gather_first(shard, small, name):
    cw, tr, n_tiles = shard.shape[1], 256, GLA_PAD // 128

    def body(sh_ref, sm_ref, wn_ref, g_ref, gs_ref, wt_ref, win_ref, tmp_ref, send_sems, recv_sems, local_sems):
        x, y, c = _place()
        me, sibling = (x, y, c), (x, y, 1 - c)
        chips = _chips(x, y)

        def copy(base, out_ref, k, block, to, src=None):
            dst = out_ref.at[_index_of(*block)]
            return _rcopy(dst if src is None else src, dst, send_sems.at[base + k], recv_sems.at[base + k], to)

        icopy = functools.partial(copy, 0, g_ref)
        scopy = functools.partial(copy, 8, gs_ref)

        def wcopy(k, block, to, src=None):
            if k in (1, 2):
                return icopy(k, block, to, src)
            halves = []
            for part, sem in enumerate((k, {0: 15, 4: 16, 5: 17, 6: 18}[k])):
                rows = pl.ds(part * (D // 2), D // 2)
                dst = g_ref.at[_index_of(*block), rows]
                halves.append(_rcopy(dst if src is None else src.at[rows], dst, send_sems.at[sem], recv_sems.at[sem], to))
            return _Both(halves)

        def relay(k, block, half, to):
            rows = pl.ds(half * (D // 2), D // 2)
            ref = g_ref.at[_index_of(*block), rows]
            return _rcopy(ref, ref, send_sems.at[k], recv_sems.at[k], to)

        def load(src_ref, slot):
            cp = pltpu.make_async_copy(src_ref, win_ref.at[slot], local_sems.at[0])
            cp.start()
            cp.wait()

        def place(slot, block):
            b = _index_of(*block)

            def rows_chunk(r, carry):
                rows = pl.ds(pl.multiple_of(r * tr, tr), tr)
                tmp_ref[:, :cw] = win_ref[slot, rows, :].astype(F32)
                shifted = pltpu.roll(tmp_ref[...], 2 * b, 1)
                for u in range(7):
                    wt_ref[6 * b + u, rows, :] = (wt_ref[6 * b + u, rows, :].astype(F32) + shifted[:, 128 * u:128 * (u + 1)]).astype(BF16)
                return carry

            lax.fori_loop(0, D // tr, rows_chunk, 0)

        small_own = pltpu.make_async_copy(sm_ref, gs_ref.at[_index_of(*me)], local_sems.at[1])
        small_own.start()
        first = [wcopy(1 + j, me, (*chip, c), src=sh_ref) for j, chip in enumerate(chips[:2])]
        first += [scopy(0, me, sibling, src=sm_ref)] + [scopy(1 + j, me, (*chip, c), src=sm_ref) for j, chip in enumerate(chips)]
        for cp in first:
            cp.start()

        def clear(t, carry):
            wt_ref[t] = jnp.zeros((D, 128), BF16)
            return carry

        lax.fori_loop(0, n_tiles, clear, 0)
        tmp_ref[...] = jnp.zeros_like(tmp_ref)

        def emit(t):
            pltpu.make_async_copy(wt_ref.at[t], wn_ref.at[t], local_sems.at[2]).start()

        def take(block, slot, arrivals=None, pass_on=None):
            for cp in arrivals or ():
                cp.wait_recv()
            load(sh_ref if arrivals is None else g_ref.at[_index_of(*block)], slot)
            if pass_on is not None:
                pass_on.start()
            place(slot, block)
            for u in range(1, 6):
                emit(6 * _index_of(*block) + u)

        near_x, near_y, far = [(*chip, c) for chip in chips]
        to_sibling = wcopy(0, me, sibling, src=win_ref.at[0])
        pass_x = wcopy(4, near_x, sibling, src=win_ref.at[1])
        pass_y = wcopy(5, near_y, sibling, src=win_ref.at[0])
        pass_d = wcopy(6, far, sibling, src=win_ref.at[0])
        relays = [relay(3, near_x, 0, near_y), relay(7, near_y, 1, near_x)]
        take(me, 0, pass_on=to_sibling)
        take(near_x, 1, [wcopy(1, near_x, me)], pass_x)
        relays[0].start()
        to_sibling.wait_send()
        take(near_y, 0, [wcopy(2, near_y, me)], pass_y)
        relays[1].start()
        small_passed = []
        for j, chip in enumerate(chips):
            scopy(1 + j, (*chip, c), me).wait_recv()
            cp = scopy(4 + j, (*chip, c), sibling)
            cp.start()
            small_passed.append(cp)
        pass_x.wait_send()
        take(sibling, 1, [wcopy(0, sibling, me)])
        pass_y.wait_send()
        take((*chips[0], 1 - c), 0, [wcopy(4, (*chips[0], 1 - c), me)])
        take((*chips[1], 1 - c), 1, [wcopy(5, (*chips[1], 1 - c), me)])
        take(far, 0, [relay(3, far, 0, near_y), relay(7, far, 1, near_x)], pass_d)
        take((*chips[2], 1 - c), 1, [wcopy(6, (*chips[2], 1 - c), me)])
        for t in range(0, n_tiles, 6):
            emit(t)
        scopy(0, sibling, me).wait_recv()
        for j, chip in enumerate(chips):
            scopy(4 + j, (*chip, 1 - c), me).wait_recv()
        for cp in first + small_passed + relays + [pass_d]:
            cp.wait_send()
        small_own.wait()
        pltpu.make_async_copy(wn_ref, wn_ref, local_sems.at[2]).wait()

    wn, _, gs = pl.pallas_call(
        body,
        in_specs=[HBM_SPEC] * 2, out_specs=[HBM_SPEC] * 3,
        out_shape=[_sds((n_tiles, D, 128), BF16), _sds((N_DEV,) + shard.shape, BF16), _sds((N_DEV,) + small.shape, small.dtype)],
        scratch_shapes=[pltpu.VMEM((n_tiles, D, 128), BF16), pltpu.VMEM((2, D, cw), BF16), pltpu.VMEM((tr, 7 * 128), F32),
                        pltpu.SemaphoreType.DMA((19,)), pltpu.SemaphoreType.DMA((19,)), pltpu.SemaphoreType.DMA((3,))],
        compiler_params=pltpu.CompilerParams(vmem_limit_bytes=48 * 1024 * 1024),
        name=name,
    )(shard, small)
    return wn, gs


def _mm(a, b, mode, out_dtype, *, tm, tn, tk, name, b_blocked=False, b_tiled=False, out_blocked=False, m_tiles=None, pair=None,
        phases=()):
    if mode == "nn":
        (m, k), dims = a.shape, NN
        a_blk, a_map = (tm, tk), (lambda i, j, kk: (i, kk))
        if b_blocked:
            assert b.shape[1] == k and b.shape[2] == tn and tk == k
            n = b.shape[0] * tn
            b_spec = pl.BlockSpec((None, tk, tn), lambda i, j, kk: (j, kk, 0))
        elif b_tiled:
            assert b.shape[1] == k and b.shape[2] == 128 and tn % 128 == 0
            n = b.shape[0] * 128
            b_spec = pl.BlockSpec((tn // 128, tk, 128), lambda i, j, kk: (j, kk, 0))
        else:
            assert b.shape[0] == k
            n = b.shape[1]
            b_spec = pl.BlockSpec((tk, tn), lambda i, j, kk: (kk, j))
    elif mode == "tn":
        (k, m), n, dims = a.shape, b.shape[1], TN
        assert b.shape[0] == k
        first = 0 if m_tiles is None else m_tiles[0]
        a_blk, a_map = (tk, tm), (lambda i, j, kk: (kk, i + first))
        b_spec = pl.BlockSpec((tk, tn), lambda i, j, kk: (kk, j))
    else:
        (m, k), dims = a.shape, NT
        a_blk, a_map = (tm, tk), (lambda i, j, kk: (i, kk))
        if b_blocked:
            assert b.shape[0] * b.shape[2] == k and b.shape[2] == tk
            n = b.shape[1]
            b_spec = pl.BlockSpec((None, tn, tk), lambda i, j, kk: (kk, j, 0))
        elif b_tiled:
            assert b.shape[0] * 128 == k and b.shape[2] == 128 and tk % 128 == 0
            n = b.shape[1]
            b_spec = pl.BlockSpec((tk // 128, tn, 128), lambda i, j, kk: (kk, j, 0))
        else:
            assert b.shape[1] == k
            n = b.shape[0]
            b_spec = pl.BlockSpec((tn, tk), lambda i, j, kk: (j, kk))
    assert m % tm == 0 and n % tn == 0 and k % tk == 0, (a.shape, b.shape, mode)
    nk = k // tk
    n_row_tiles = m // tm if m_tiles is None else m_tiles[1]
    if out_blocked:
        out_shape, out_spec = _sds((n // tn, n_row_tiles * tm, tn), out_dtype), pl.BlockSpec((None, tm, tn), lambda i, j, kk: (j, i, 0))
    else:
        out_shape, out_spec = _sds((n_row_tiles * tm, n), out_dtype), pl.BlockSpec((tm, tn), lambda i, j, kk: (i, j))

    grid = (n_row_tiles, n // tn, nk)

    def body(a_ref, b_ref, o_ref, *rest):
        rhs = jnp.concatenate([b_ref[u] for u in range(b_ref.shape[0])], axis=1) if b_tiled else b_ref[...]
        p = _dot(a_ref[...], rhs, dims)
        if nk == 1:
            o_ref[...] = p.astype(out_dtype)
            if pair is not None:
                _send_to_sibling(p.astype(out_dtype), *rest)
        else:
            acc_ref, = rest
            kk = pl.program_id(2)

            @pl.when(kk == 0)
            def _():
                acc_ref[...] = p

            @pl.when(kk > 0)
            def _():
                acc_ref[...] += p

            @pl.when(kk == nk - 1)
            def _():
                o_ref[...] = acc_ref[...].astype(out_dtype)

    def _send_to_sibling(tile, pair_ref, stage_ref, send_sems, recv_sem):
        i, j = pl.program_id(0), pl.program_id(1)
        x, y, c = _place()
        blk = pair["block"](i, j)
        k = ((blk >> 2) ^ x) + 2 * (((blk >> 1) & 1) ^ y)
        ordinal = pair["ordinal"](i, j)

        def send(slot):
            return _rcopy(stage_ref.at[slot], pair["dst"](pair_ref, k, i, j), send_sems.at[slot], recv_sem.at[0], (x, y, 1 - c))

        @pl.when((blk & 1) != c)
        def _():
            slot = ordinal & 1

            @pl.when(ordinal >= 2)
            def _():
                send(slot).wait_send()

            stage_ref[slot] = tile
            send(slot).start()

        @pl.when((i == grid[0] - 1) & (j == grid[1] - 1))
        def _():
            send(0).wait_send()
            send(1).wait_send()
            _rcopy(pair_ref, pair_ref, send_sems.at[0], recv_sem.at[0], (x, y, 1 - c)).wait_recv()

    blocks = _nbytes(a_blk, a.dtype) + tk * tn * jnp.dtype(b.dtype).itemsize + _nbytes((tm, tn), out_dtype)
    out_specs, out_shapes, scratch = [out_spec], [out_shape], [] if nk == 1 else [pltpu.VMEM((tm, tn), F32)]
    scratch_bytes = _nbytes((tm, tn), F32) * (nk > 1)
    if pair is not None:
        assert nk == 1
        out_specs, out_shapes = out_specs + [HBM_SPEC], out_shapes + [pair["like"]]
        scratch = [pltpu.VMEM((2, tm, tn), out_dtype), pltpu.SemaphoreType.DMA((2,)), pltpu.SemaphoreType.DMA((1,))]
        scratch_bytes = 2 * _nbytes((tm, tn), out_dtype)
    outs, bufs = _hosted(
        body, name=name, grid=grid,
        in_specs=[pl.BlockSpec(a_blk, a_map), b_spec], out_specs=out_specs, out_shape=out_shapes, args=[a, b],
        scratch_shapes=scratch, block_bytes=blocks, scratch_bytes=scratch_bytes, phases=phases)
    return outs[0], outs[1:] + bufs


NT_ROWS_TM = 1024
NN_ROWS_TM = 512
NT_ROWS_SUB = 128


def _nt_rows(a, b, *, tk, name, row_ins, vec_ins, outs, tail, b_blocked=False, b_tiled=False, phases=()):
    m, k = a.shape
    tm, sub, nk = NT_ROWS_TM, NT_ROWS_SUB, k // tk
    n_sub = tm // sub
    if b_blocked:
        assert b.shape[0] * b.shape[2] == k and b.shape[2] == tk and b.shape[1] == D
        b_spec = pl.BlockSpec((None, D, tk), lambda i, kk: (kk, 0, 0))
    else:
        assert b_tiled and b.shape[0] * 128 == k and tk % 128 == 0 and b.shape[1] == D
        b_spec = pl.BlockSpec((tk // 128, D, 128), lambda i, kk: (kk, 0, 0))
    row_spec, vec_spec = pl.BlockSpec((tm, D), lambda i, kk: (i, 0)), pl.BlockSpec((1, D), lambda i, kk: (0, 0))
    n_row, n_vec, n_out = len(row_ins), len(vec_ins), len(outs)
    assert nk >= 2

    def body(a_ref, b_ref, *rest):
        row_hbm, vec_refs = rest[:n_row], rest[n_row:n_row + n_vec]
        out_refs = rest[n_row + n_vec:n_row + n_vec + n_out]
        acc_ref, row_sems = rest[n_row + n_vec + n_out], rest[-1]
        row_refs = rest[n_row + n_vec + n_out + 1:-1]
        rhs = jnp.concatenate([b_ref[u] for u in range(b_ref.shape[0])], axis=1) if b_tiled else b_ref[...]
        p = _dot(a_ref[...], rhs, NT)
        i, kk = pl.program_id(0), pl.program_id(1)

        def fetch(r, s):
            src = row_hbm[r].at[pl.ds(pl.multiple_of(i * tm + s * sub, sub), sub)]
            return pltpu.make_async_copy(src, row_refs[r].at[s % 2], row_sems.at[r, s % 2])

        @pl.when(kk == 0)
        def _():
            for r in range(n_row):
                fetch(r, 0).start()
            acc_ref[...] = p

        @pl.when(kk > 0)
        def _():
            acc_ref[...] += p

        @pl.when(kk == nk - 1)
        def _():
            for s in range(n_sub):
                for r in range(n_row):
                    if s + 1 < n_sub:
                        fetch(r, s + 1).start()
                    fetch(r, s).wait()
                rows = slice(s * sub, (s + 1) * sub)
                tail(acc_ref[rows, :], rows, (i == 0) if s == 0 else None, [ref[s % 2] for ref in row_refs], vec_refs, out_refs)

    out_specs = [row_spec if kind == "row" else vec_spec for kind, _ in outs]
    out_shape = [_sds((m, D) if kind == "row" else (1, D), dt) for kind, dt in outs]
    blocks = tm * tk * 2 + D * tk * 2 + sum(tm * D * jnp.dtype(dt).itemsize for kind, dt in outs if kind == "row")
    scratch = ([pltpu.VMEM((tm, D), F32)] + [pltpu.VMEM((2, sub, D), x.dtype) for x in row_ins]
               + [pltpu.SemaphoreType.DMA((n_row, 2))])
    return _hosted(body, name=name, grid=(m // tm, nk), in_specs=[pl.BlockSpec((tm, tk), lambda i, kk: (i, kk)), b_spec]
                   + [HBM_SPEC] * n_row + [vec_spec] * n_vec, out_specs=out_specs, out_shape=out_shape,
                   args=[a, b] + list(row_ins) + list(vec_ins), scratch_shapes=scratch,
                   block_bytes=blocks, scratch_bytes=tm * D * 4 + n_row * 2 * sub * D * 4, phases=phases)


def _nn_rows(a, b, *, name, row_ins, vec_ins, outs, tail):
    m, k = a.shape
    tm = NN_ROWS_TM
    assert b.shape == (k, D)
    row_spec, vec_spec = pl.BlockSpec((tm, D), lambda i: (i, 0)), pl.BlockSpec((1, D), lambda i: (0, 0))
    n_row, n_vec, n_out = len(row_ins), len(vec_ins), len(outs)

    def body(a_ref, b_ref, *rest):
        row_hbm, vec_refs = rest[:n_row], rest[n_row:n_row + n_vec]
        out_refs = rest[n_row + n_vec:n_row + n_vec + n_out]
        d_ref, row_sems = rest[n_row + n_vec + n_out], rest[-1]
        row_refs = rest[n_row + n_vec + n_out + 1:-1]
        i = pl.program_id(0)
        fetches = [pltpu.make_async_copy(row_hbm[r].at[pl.ds(pl.multiple_of(i * tm, tm), tm)], row_refs[r], row_sems.at[r])
                   for r in range(n_row)]
        for cp in fetches:
            cp.start()
        d_ref[...] = _dot(a_ref[...], b_ref[...])
        for cp in fetches:
            cp.wait()
        for s in range(tm // NT_ROWS_SUB):
            rows = slice(s * NT_ROWS_SUB, (s + 1) * NT_ROWS_SUB)
            tail(d_ref[rows, :], rows, (i == 0) if s == 0 else None, [ref[rows, :] for ref in row_refs], vec_refs, out_refs)

    out_specs = [row_spec if kind == "row" else vec_spec for kind, _ in outs]
    out_shape = [_sds((m, D) if kind == "row" else (1, D), dt) for kind, dt in outs]
    blocks = tm * k * 2 + k * D * 2 + sum(tm * D * jnp.dtype(dt).itemsize for kind, dt in outs if kind == "row")
    scratch = [pltpu.VMEM((tm, D), F32)] + [pltpu.VMEM((tm, D), x.dtype) for x in row_ins] + [pltpu.SemaphoreType.DMA((n_row,))]
    outs_, _ = _hosted(body, name=name, grid=(m // tm,), in_specs=[pl.BlockSpec((tm, k), lambda i: (i, 0)),
                                                                 pl.BlockSpec((k, D), lambda i: (0, 0))]
                       + [HBM_SPEC] * n_row + [vec_spec] * n_vec, out_specs=out_specs, out_shape=out_shape,
                       args=[a, b] + list(row_ins) + list(vec_ins), scratch_shapes=scratch,
                       block_bytes=blocks, scratch_bytes=(1 + n_row) * tm * D * 4)
    return outs_


def _vec_add(ref, value, first):
    if first is None:
        ref[...] += value
    else:
        pl.when(first)(lambda: ref.__setitem__(Ellipsis, value))
        pl.when(jnp.logical_not(first))(lambda: ref.__setitem__(Ellipsis, ref[...] + value))


RB = 256


def _row_spec(width):
    return pl.BlockSpec((RB, width), lambda i: (i, 0))


def _vec_spec(width):
    return pl.BlockSpec((1, width), lambda i: (0, 0))


def _rinv(x):
    return lax.rsqrt(jnp.mean(x * x, axis=-1, keepdims=True) + EPS)


def _norm_bwd(dyn, xhat, r):
    return r * (dyn - xhat * jnp.mean(dyn * xhat, axis=-1, keepdims=True))


def _colsum(x):
    return jnp.sum(x, axis=0, keepdims=True)


def _prenorm(x, gain):
    def body(x_ref, g_ref, h_ref):
        xv = x_ref[...]
        h_ref[...] = (xv * _rinv(xv) * g_ref[...]).astype(BF16)

    outs, _ = _hosted(body, name="prenorm", grid=(S // RB,), in_specs=[_row_spec(D), _vec_spec(D)], out_specs=[_row_spec(D)],
                      out_shape=[_sds((S, D), BF16)], args=[x, gain], block_bytes=RB * D * 6)
    return outs[0]


def _mid_fwd(x, y, npost, npre, phases=()):
    def body(x_ref, y_ref, po_ref, pr_ref, x1_ref, h1_ref):
        yv = y_ref[...]
        x1 = x_ref[...] + yv * _rinv(yv) * po_ref[...]
        x1_ref[...] = x1
        h1_ref[...] = (x1 * _rinv(x1) * pr_ref[...]).astype(BF16)

    return _hosted(body, name="mid_fwd", grid=(S // RB,), in_specs=[_row_spec(D), _row_spec(D), _vec_spec(D), _vec_spec(D)],
                   out_specs=[_row_spec(D), _row_spec(D)], out_shape=[_sds((S, D), F32), _sds((S, D), BF16)],
                   args=[x, y, npost, npre], block_bytes=RB * D * 14, phases=phases)


def _final_tail(yv, rows, first, row_vals, vec_refs, out_refs):
    (xv, tv), (po_ref,), (loss_ref, dx_ref, dy_ref, dpo_ref) = row_vals, vec_refs, out_refs
    r = _rinv(yv)
    yhat = yv * r
    err = xv + yhat * po_ref[...] - tv
    dx = err * (1.0 / D)
    dx_ref[rows, :] = dx
    dy_ref[rows, :] = _norm_bwd(dx * po_ref[...], yhat, r).astype(BF16)
    _vec_add(loss_ref, _colsum(err * err), first)
    _vec_add(dpo_ref, _colsum(dx * yhat), first)


def _mid_bwd_tail(dh, rows, first, row_vals, vec_refs, out_refs):
    (dx2, xv, yv), (pr_ref, po_ref), (dx1_ref, dy_ref, dpr_ref, dpo_ref) = row_vals, vec_refs, out_refs
    r = _rinv(xv)
    xhat = xv * r
    dx1 = dx2 + _norm_bwd(dh * pr_ref[...], xhat, r)
    dx1_ref[rows, :] = dx1
    ry = _rinv(yv)
    yhat = yv * ry
    dy_ref[rows, :] = _norm_bwd(dx1 * po_ref[...], yhat, ry).astype(BF16)
    _vec_add(dpr_ref, _colsum(dh * xhat), first)
    _vec_add(dpo_ref, _colsum(dx1 * yhat), first)


def _first_bwd_tail(dh, rows, first, row_vals, vec_refs, out_refs):
    (dx1, xv), (pr_ref,), (gx_ref, dpr_ref) = row_vals, vec_refs, out_refs
    r = _rinv(xv)
    xhat = xv * r
    gx_ref[rows, :] = dx1 + _norm_bwd(dh * pr_ref[...], xhat, r)
    _vec_add(dpr_ref, _colsum(dh * xhat), first)


GLA_RB = 256
GLA_CPB = GLA_RB // C


def _sigmoid(x):
    return 1.0 / (1.0 + jnp.exp(-x))


def _tri(strict):
    r = lax.broadcasted_iota(jnp.int32, (C, C), 0)
    c = lax.broadcasted_iota(jnp.int32, (C, C), 1)
    return jnp.where(c < r if strict else c <= r, 1.0, 0.0).astype(BF16)


def _tri_dot(tri, x):
    hi = x.astype(BF16)
    lo = (x - hi.astype(F32)).astype(BF16)
    return _dot(tri, hi) + _dot(tri, lo)


def _gla_gates(glr_b, w2, b, tri):
    z = _dot(glr_b, w2) + b
    log_a = (jnp.minimum(z, 0.0) - jnp.log(1.0 + jnp.exp(-jnp.abs(z)))) * (1.0 / GLA_TAU)
    bcum = _tri_dot(tri, log_a)
    b_end = jnp.sum(log_a, axis=0, keepdims=True)
    return z, jnp.exp(b_end - bcum), jnp.exp(b_end)


def _gla_fwd(proj, w2p, bgate, ogain, phases=()):
    def body(p_ref, w2_ref, b_ref, og_ref, y_ref, st_out_ref, st_ref):
        @pl.when(pl.program_id(0) == 0)
        def _():
            st_ref[...] = jnp.zeros_like(st_ref)

        tri = _tri(False)

        def chunk(ci, carry):
            rows = pl.ds(pl.multiple_of(ci * C, C), C)
            glr_b = p_ref[rows, LR0:LR0 + LRP].astype(BF16)
            _, ea_all, dec_all = _gla_gates(glr_b, w2_ref[...], b_ref[...], tri)
            for h in range(H):
                ea, dec = ea_all[:, h * DK:(h + 1) * DK], dec_all[:, h * DK:(h + 1) * DK]
                k_dec = (p_ref[rows, K0 + h * DK:K0 + (h + 1) * DK] * ea).astype(BF16)
                v_b = p_ref[rows, V0 + h * DV:V0 + (h + 1) * DV].astype(BF16)
                st = st_ref[h] * dec + _dot(v_b, k_dec, TN)
                st_ref[h] = st
                st_b = st.astype(BF16)
                st_out_ref[ci, h] = st_b
                q_b = (p_ref[rows, Q0 + h * DK:Q0 + (h + 1) * DK] * (DK ** -0.5)).astype(BF16)
                o = _dot(q_b, st_b, NT)
                on = o * _rinv(o)
                g = p_ref[rows, G0 + h * DV:G0 + (h + 1) * DV]
                y_ref[rows, h * DV:(h + 1) * DV] = (on * og_ref[:, h * DV:(h + 1) * DV] * (g * _sigmoid(g))).astype(BF16)
            return carry

        lax.fori_loop(0, GLA_CPB, chunk, 0, unroll=True)

    blocks = GLA_RB * GLA_PAD * 4 + GLA_RB * D * 2 + GLA_CPB * H * DV * DK * 2
    return _hosted(
        body, name="gla_fwd", grid=(S // GLA_RB,),
        in_specs=[pl.BlockSpec((GLA_RB, GLA_PAD), lambda i: (i, 0)),
                  pl.BlockSpec((LRP, H * DK), lambda i: (0, 0)),
                  pl.BlockSpec((1, H * DK), lambda i: (0, 0)),
                  pl.BlockSpec((1, H * DV), lambda i: (0, 0))],
        out_specs=[pl.BlockSpec((GLA_RB, H * DV), lambda i: (i, 0)),
                   pl.BlockSpec((GLA_CPB, H, DV, DK), lambda i: (i, 0, 0, 0))],
        out_shape=[_sds((S, H * DV), BF16), _sds((NC, H, DV, DK), BF16)],
        args=[proj, w2p, bgate, ogain], scratch_shapes=[pltpu.VMEM((H, DV, DK), F32)],
        block_bytes=blocks, scratch_bytes=H * DV * DK * 4, phases=phases)


def _gla_bwd(proj, dypre, states, w2p, bgate, ogain, phases=()):
    nb = S // GLA_RB

    def body(p_ref, dy_ref, st_blk_ref, st_prev_ref, w2_ref, b_ref, og_ref,
             dp_ref, dog_ref, dbg_ref, dw2_ref, r_ref):
        step = pl.program_id(0)

        @pl.when(step == 0)
        def _():
            r_ref[...] = jnp.zeros_like(r_ref)
            dog_ref[...] = jnp.zeros_like(dog_ref)
            dbg_ref[...] = jnp.zeros_like(dbg_ref)
            dw2_ref[...] = jnp.zeros_like(dw2_ref)

        tri = _tri(False)
        tri_strict = _tri(True)
        has_prev = jnp.where(step < nb - 1, 1.0, 0.0).astype(F32)

        def chunk(ci, st_prev_of):
            rows = pl.ds(ci * C if isinstance(ci, int) else pl.multiple_of(ci * C, C), C)
            glr_b = p_ref[rows, LR0:LR0 + LRP].astype(BF16)
            z, ea_all, dec_all = _gla_gates(glr_b, w2_ref[...], b_ref[...], tri)
            d_a, d_end = [], []
            for h in range(H):
                kcol = slice(h * DK, (h + 1) * DK)
                vcol = slice(h * DV, (h + 1) * DV)
                ea, dec = ea_all[:, kcol], dec_all[:, kcol]
                k_dec = p_ref[rows, K0 + h * DK:K0 + (h + 1) * DK] * ea
                k_dec_b = k_dec.astype(BF16)
                v_b = p_ref[rows, V0 + h * DV:V0 + (h + 1) * DV].astype(BF16)
                q_b = (p_ref[rows, Q0 + h * DK:Q0 + (h + 1) * DK] * (DK ** -0.5)).astype(BF16)
                st_b = st_blk_ref[ci, h]
                o = _dot(q_b, st_b, NT)
                rinv = _rinv(o)
                on = o * rinv
                g = p_ref[rows, G0 + h * DV:G0 + (h + 1) * DV]
                sg = _sigmoid(g)
                og = og_ref[:, vcol]
                dyp = dy_ref[rows, vcol]
                dp_ref[rows, G0 + h * DV:G0 + (h + 1) * DV] = (dyp * (on * og) * (sg * (1.0 + g * (1.0 - sg)))).astype(BF16)
                dpn = dyp * (g * sg)
                dog_ref[:, vcol] += _colsum(dpn * on)
                do_b = _norm_bwd(dpn * og, on, rinv).astype(BF16)
                gt = _dot(do_b, q_b, TN) + r_ref[h]
                gt_b = gt.astype(BF16)
                dp_ref[rows, Q0 + h * DK:Q0 + (h + 1) * DK] = (_dot(do_b, st_b) * (DK ** -0.5)).astype(BF16)
                dkd = _dot(v_b, gt_b)
                dp_ref[rows, V0 + h * DV:V0 + (h + 1) * DV] = _dot(k_dec_b, gt_b, NT).astype(BF16)
                dp_ref[rows, K0 + h * DK:K0 + (h + 1) * DK] = (dkd * ea).astype(BF16)
                d_a.append(dkd * k_dec)
                d_end.append(_colsum(gt * st_prev_of(h)) * dec)
                r_ref[h] = gt * dec
            dla = _tri_dot(tri_strict, jnp.concatenate(d_a, axis=1)) + jnp.concatenate(d_end, axis=1)
            dz = dla * (1.0 / GLA_TAU) * (1.0 - _sigmoid(z))
            dz_b = dz.astype(BF16)
            dbg_ref[...] += _colsum(dz)
            dw2_ref[...] += _dot(glr_b, dz_b, TN)
            dp_ref[rows, LR0:LR0 + LRP] = _dot(dz_b, w2_ref[...], NT).astype(BF16)

        def later_chunk(t, carry):
            ci = GLA_CPB - 1 - t
            chunk(ci, lambda h: st_blk_ref[ci - 1, h].astype(F32))
            return carry

        lax.fori_loop(0, GLA_CPB - 1, later_chunk, 0, unroll=True)
        chunk(0, lambda h: st_prev_ref[0, h].astype(F32) * has_prev)

    blocks = (GLA_RB * GLA_PAD * 4 + GLA_RB * D * 4 + (GLA_CPB + 1) * H * DV * DK * 2 + GLA_RB * GLA_PAD * 2)
    rev = lambda i: nb - 1 - i
    return _hosted(
        body, name="gla_bwd", grid=(nb,),
        in_specs=[pl.BlockSpec((GLA_RB, GLA_PAD), lambda i: (rev(i), 0)),
                  pl.BlockSpec((GLA_RB, H * DV), lambda i: (rev(i), 0)),
                  pl.BlockSpec((GLA_CPB, H, DV, DK), lambda i: (rev(i), 0, 0, 0)),
                  pl.BlockSpec((1, H, DV, DK), lambda i: (jnp.maximum(rev(i) * GLA_CPB - 1, 0), 0, 0, 0)),
                  pl.BlockSpec((LRP, H * DK), lambda i: (0, 0)),
                  pl.BlockSpec((1, H * DK), lambda i: (0, 0)),
                  pl.BlockSpec((1, H * DV), lambda i: (0, 0))],
        out_specs=[pl.BlockSpec((GLA_RB, GLA_PAD), lambda i: (rev(i), 0)),
                   pl.BlockSpec((1, H * DV), lambda i: (0, 0)),
                   pl.BlockSpec((1, H * DK), lambda i: (0, 0)),
                   pl.BlockSpec((LRP, H * DK), lambda i: (0, 0))],
        out_shape=[_sds((S, GLA_PAD), BF16), _sds((1, H * DV), F32), _sds((1, H * DK), F32), _sds((LRP, H * DK), F32)],
        args=[proj, dypre, states, states, w2p, bgate, ogain], scratch_shapes=[pltpu.VMEM((H, DV, DK), F32)],
        block_bytes=blocks, scratch_bytes=H * DV * DK * 4, phases=phases)


SGU_RB = 256
GELU_C = 0.7978845608028654
GELU_A = 0.044715


def _gelu(x):
    return 0.5 * x * (1.0 + jnp.tanh(GELU_C * (x + GELU_A * x * x * x)))


def _gelu_grad(x):
    t = jnp.tanh(GELU_C * (x + GELU_A * x * x * x))
    return 0.5 * (1.0 + t) + 0.5 * x * (1.0 - t * t) * (GELU_C * (1.0 + 3.0 * GELU_A * x * x))


def _causal_mask(transposed=False):
    i = lax.broadcasted_iota(jnp.int32, (SGU_BLOCK, SGU_BLOCK), 1 if transposed else 0)
    j = lax.broadcasted_iota(jnp.int32, (SGU_BLOCK, SGU_BLOCK), 0 if transposed else 1)
    return (i >= C) | (j < C)


def _layer_norm(vf, gain, bias):
    mu = jnp.mean(vf, axis=-1, keepdims=True)
    cen = vf - mu
    rstd = lax.rsqrt(jnp.mean(cen * cen, axis=-1, keepdims=True) + EPS)
    xhat = cen * rstd
    return xhat, rstd, xhat * gain + bias


def _sgu_fwd(proj, lng, lnb, ws, bsb, phases=()):
    def body(p_ref, g_ref, b_ref, ws_ref, bs_ref, o_ref):
        mask = _causal_mask()
        for n in range(SGU_RB // SGU_BLOCK):
            rows = slice(n * SGU_BLOCK, (n + 1) * SGU_BLOCK)
            _, _, vn = _layer_norm(_gelu(p_ref[rows, D:2 * D]), g_ref[...], b_ref[...])
            vn_b = vn.astype(BF16)
            for gi in range(SGU_G):
                cols = slice(gi * SGU_GD, (gi + 1) * SGU_GD)
                w = jnp.where(mask, ws_ref[gi], 0.0).astype(BF16)
                vs = _dot(w, vn_b[:, cols]) + bs_ref[gi]
                gate = p_ref[rows, 2 * D + gi * SGU_GD:2 * D + (gi + 1) * SGU_GD]
                o_ref[rows, cols] = (_gelu(p_ref[rows, cols]) * vs * (gate * _sigmoid(gate))).astype(BF16)

    blocks = SGU_RB * SGU_COLS * 4 + SGU_RB * D * 2 + SGU_G * SGU_BLOCK * (SGU_BLOCK + SGU_GD) * 4
    return _hosted(
        body, name="sgu_fwd", grid=(S // SGU_RB,),
        in_specs=[pl.BlockSpec((SGU_RB, SGU_COLS), lambda i: (i, 0)),
                  pl.BlockSpec((1, D), lambda i: (0, 0)), pl.BlockSpec((1, D), lambda i: (0, 0)),
                  pl.BlockSpec((SGU_G, SGU_BLOCK, SGU_BLOCK), lambda i: (0, 0, 0)),
                  pl.BlockSpec((SGU_G, SGU_BLOCK, SGU_GD), lambda i: (0, 0, 0))],
        out_specs=[pl.BlockSpec((SGU_RB, D), lambda i: (i, 0))], out_shape=[_sds((S, D), BF16)],
        args=[proj, lng, lnb, ws, bsb], block_bytes=blocks, phases=phases)


def _sgu_bwd(proj, dpre, lng, lnb, ws, wst, bsb, phases=()):
    nsteps = S // SGU_RB

    def body(p_ref, d_ref, g_ref, b_ref, ws_ref, wst_ref, bs_ref,
             dp_ref, dg_ref, db_ref, dws_ref, dbs_ref, dvn_ref, dvs_acc_ref):
        step = pl.program_id(0)

        @pl.when(step == 0)
        def _():
            dg_ref[...] = jnp.zeros_like(dg_ref)
            db_ref[...] = jnp.zeros_like(db_ref)
            dws_ref[...] = jnp.zeros_like(dws_ref)
            dvs_acc_ref[...] = jnp.zeros_like(dvs_acc_ref)

        mask = _causal_mask()
        maskt = _causal_mask(transposed=True)
        for n in range(SGU_RB // SGU_BLOCK):
            rows = slice(n * SGU_BLOCK, (n + 1) * SGU_BLOCK)
            v = p_ref[rows, D:2 * D]
            xhat, rstd, vn = _layer_norm(_gelu(v), g_ref[...], b_ref[...])
            vn_b = vn.astype(BF16)
            for gi in range(SGU_G):
                cols = slice(gi * SGU_GD, (gi + 1) * SGU_GD)
                w = jnp.where(mask, ws_ref[gi], 0.0).astype(BF16)
                wt = jnp.where(maskt, wst_ref[gi], 0.0).astype(BF16)
                vs = _dot(w, vn_b[:, cols]) + bs_ref[gi]
                u = p_ref[rows, cols]
                gate = p_ref[rows, 2 * D + gi * SGU_GD:2 * D + (gi + 1) * SGU_GD]
                sg = _sigmoid(gate)
                gu = _gelu(u)
                dpre_g = d_ref[rows, cols]
                t = dpre_g * (gate * sg)
                dp_ref[rows, cols] = (t * vs * _gelu_grad(u)).astype(BF16)
                dp_ref[rows, 2 * D + gi * SGU_GD:2 * D + (gi + 1) * SGU_GD] = (
                    dpre_g * gu * vs * (sg * (1.0 + gate * (1.0 - sg)))).astype(BF16)
                dvs = t * gu
                dvs_b = dvs.astype(BF16)
                dvs_acc_ref[:, cols] += dvs
                dws_ref[gi] += _dot(dvs_b, vn_b[:, cols], NT)
                dvn_ref[:, cols] = _dot(wt, dvs_b)
            dvn = dvn_ref[...]
            dg_ref[...] += _colsum(dvn * xhat)
            db_ref[...] += _colsum(dvn)
            dxh = dvn * g_ref[...]
            dvf = rstd * (dxh - jnp.mean(dxh, axis=-1, keepdims=True) - xhat * jnp.mean(dxh * xhat, axis=-1, keepdims=True))
            dp_ref[rows, D:2 * D] = (dvf * _gelu_grad(v)).astype(BF16)

        @pl.when(step == nsteps - 1)
        def _():
            lane = lax.broadcasted_iota(jnp.int32, (SGU_BLOCK, SGU_BLOCK), 1)
            out = jnp.zeros((SGU_BLOCK, SGU_BLOCK), F32)
            for gi in range(SGU_G):
                out = out + jnp.where(lane == gi, jnp.sum(dvs_acc_ref[:, gi * SGU_GD:(gi + 1) * SGU_GD], axis=1, keepdims=True), 0.0)
                dws_ref[gi] = jnp.where(mask, dws_ref[gi], 0.0)
            dbs_ref[...] = out

    blocks = SGU_RB * SGU_COLS * 6 + SGU_RB * D * 4 + SGU_G * SGU_BLOCK * (3 * SGU_BLOCK + SGU_GD) * 4
    const3 = lambda i: (0, 0, 0)
    return _hosted(
        body, name="sgu_bwd", grid=(nsteps,),
        in_specs=[pl.BlockSpec((SGU_RB, SGU_COLS), lambda i: (i, 0)),
                  pl.BlockSpec((SGU_RB, D), lambda i: (i, 0)),
                  pl.BlockSpec((1, D), lambda i: (0, 0)), pl.BlockSpec((1, D), lambda i: (0, 0)),
                  pl.BlockSpec((SGU_G, SGU_BLOCK, SGU_BLOCK), const3),
                  pl.BlockSpec((SGU_G, SGU_BLOCK, SGU_BLOCK), const3),
                  pl.BlockSpec((SGU_G, SGU_BLOCK, SGU_GD), const3)],
        out_specs=[pl.BlockSpec((SGU_RB, SGU_COLS), lambda i: (i, 0)),
                   pl.BlockSpec((1, D), lambda i: (0, 0)), pl.BlockSpec((1, D), lambda i: (0, 0)),
                   pl.BlockSpec((SGU_G, SGU_BLOCK, SGU_BLOCK), const3),
                   pl.BlockSpec((SGU_BLOCK, SGU_BLOCK), lambda i: (0, 0))],
        out_shape=[_sds((S, SGU_COLS), BF16), _sds((1, D), F32), _sds((1, D), F32),
                   _sds((SGU_G, SGU_BLOCK, SGU_BLOCK), F32), _sds((SGU_BLOCK, SGU_BLOCK), F32)],
        args=[proj, dpre, lng, lnb, ws, wst, bsb],
        scratch_shapes=[pltpu.VMEM((SGU_BLOCK, D), F32), pltpu.VMEM((SGU_BLOCK, D), F32)],
        block_bytes=blocks, scratch_bytes=2 * SGU_BLOCK * D * 4, phases=phases)


def _pair_sum(own, a, r0, nr, name, table=None):
    c = own.shape[2]
    tr = 256
    assert r0 % tr == 0 and nr % tr == 0

    def body(own_ref, sib_ref, o_ref):
        o_ref[...] = (own_ref[...].astype(F32) + sib_ref[...].astype(F32)).astype(BF16)

    own_map = ((lambda j, i: (1 + j, r0 // tr + i, 0)) if table is None else
               (lambda j, i, t: (t[1 + j], r0 // tr + i, 0)))
    cpad = -(-c // 128) * 128
    outs, _ = _hosted(
        body, name=name, grid=(3, nr // tr),
        in_specs=[pl.BlockSpec((None, tr, c), own_map),
                  pl.BlockSpec((None, tr, c), lambda j, i, *t: (1 + j, r0 // tr + i, 0))],
        out_specs=[pl.BlockSpec((None, tr, c), lambda j, i, *t: (j, i, 0))], out_shape=[_sds((3, nr, c), BF16)],
        args=[own, a], block_bytes=3 * tr * cpad * 2, table=table)
    return outs[0]


def _adamw_math(w, g, m, v):
    m = ADAM_B1 * m + (1.0 - ADAM_B1) * g
    v = ADAM_B2 * v + (1.0 - ADAM_B2) * (g * g)
    m_hat = m / (1.0 - ADAM_B1 ** ADAM_STEP)
    v_hat = v / (1.0 - ADAM_B2 ** ADAM_STEP)
    delta = -ADAM_LR * (m_hat / (jnp.sqrt(v_hat) + ADAM_EPS) + ADAM_WD * w)
    return delta, m, v


def _sum_adamw(own, a, b, w, m, v, *, name, phases=(), table=None):
    r, c = w.shape
    tr = 256

    def body(own_ref, sib_ref, far_ref, w_ref, m_ref, v_ref, g_ref, d_ref, nm_ref, nv_ref):
        g = own_ref[...].astype(F32) + sib_ref[...].astype(F32)
        for j in range(3):
            g = g + far_ref[j].astype(F32)
        g_ref[...] = g
        d_ref[...], nm_ref[...], nv_ref[...] = _adamw_math(w_ref[...], g, m_ref[...], v_ref[...])

    spec = pl.BlockSpec((tr, c), lambda i, *t: (i, 0))
    own_map = (lambda i: (0, i, 0)) if table is None else (lambda i, t: (t[0], i, 0))
    cpad = -(-c // 128) * 128
    return _hosted(
        body, name=name, grid=(r // tr,),
        in_specs=[pl.BlockSpec((None, tr, c), own_map), pl.BlockSpec((None, tr, c), lambda i, *t: (0, i, 0)),
                  pl.BlockSpec((3, tr, c), lambda i, *t: (0, i, 0)), spec, spec, spec],
        out_specs=[spec] * 4, out_shape=[_sds((r, c), F32)] * 4, args=[own, a, b, w, m, v],
        block_bytes=5 * tr * cpad * 2 + 7 * tr * cpad * 4, phases=phases, table=table)


def _sum_parts(parts, name):
    n, r, c = parts.shape

    def body(p_ref, o_ref):
        g = p_ref[0]
        for j in range(1, n):
            g = g + p_ref[j]
        o_ref[...] = g

    outs, _ = _hosted(body, name=name, grid=(1,), in_specs=[pl.BlockSpec((n, r, c), lambda i: (0, 0, 0))],
                      out_specs=[pl.BlockSpec((r, c), lambda i: (0, 0))], out_shape=[_sds((r, c), F32)], args=[parts],
                      block_bytes=(n + 1) * r * c * 4)
    return outs[0]


def _adamw(w, g, m, v, name):
    def body(w_ref, g_ref, m_ref, v_ref, d_ref, nm_ref, nv_ref):
        d_ref[...], nm_ref[...], nv_ref[...] = _adamw_math(w_ref[...], g_ref[...], m_ref[...], v_ref[...])

    spec = pl.BlockSpec(w.shape, lambda i: (0, 0))
    outs, _ = _hosted(body, name=name, grid=(1,), in_specs=[spec] * 4, out_specs=[spec] * 3, out_shape=[_sds(w.shape, F32)] * 3,
                      args=[w, g, m, v], block_bytes=7 * _nbytes(w.shape, F32))
    return outs


def _blocks_to_columns(g):
    n, r, c = g.shape
    return jnp.transpose(g, (1, 0, 2)).reshape(r, n * c)


def _pack(parts):
    return jnp.concatenate([p.reshape(-1) for p in parts]).reshape(-1, 128)


def _unpack(packed, like):
    flat, outs, off = packed.reshape(-1), [], 0
    for p in like:
        outs.append(flat[off:off + p.size].reshape(p.shape))
        off += p.size
    return outs


def kernel(x, norm_pre, norm_post, gla_w_in, gla_w_gate2, gla_b_gate, gla_o_gain, gla_w_out, sgu_w_in, sgu_ln_gain, sgu_ln_bias, sgu_w_spatial, sgu_b_spatial, sgu_w_out, loss_target, m_norm_pre, m_norm_post, m_gla_w_in, m_gla_w_gate2, m_gla_b_gate, m_gla_o_gain, m_gla_w_out, m_sgu_w_in, m_sgu_ln_gain, m_sgu_ln_bias, m_sgu_w_spatial, m_sgu_b_spatial, m_sgu_w_out, v_norm_pre, v_norm_post, v_gla_w_in, v_gla_w_gate2, v_gla_b_gate, v_gla_o_gain, v_gla_w_out, v_sgu_w_in, v_sgu_ln_gain, v_sgu_ln_bias, v_sgu_w_spatial, v_sgu_b_spatial, v_sgu_w_out):
    me = _index_of(*_place())
    x0 = x.reshape(S, D)
    tgt = loss_target.reshape(S, D)
    npre0, npre1 = norm_pre[0:1], norm_pre[1:2]
    npost0, npost1 = norm_post[0:1], norm_post[1:2]
    ws = sgu_w_spatial[0]
    wst = jnp.transpose(ws, (0, 2, 1))
    bsb = jnp.broadcast_to(sgu_b_spatial[0][:, :, None], (SGU_G, SGU_BLOCK, SGU_GD))
    W_ROWS = D // N_DEV
    IN_COLS_G, IN_COLS_S = GLA_COLS // N_DEV, SGU_COLS // N_DEV

    s_gwi, s_gwo = gla_w_in[0].astype(BF16), gla_w_out[0].astype(BF16)
    s_swi, s_swo = sgu_w_in[0].astype(BF16), sgu_w_out[0].astype(BF16)
    small = jnp.concatenate([jnp.pad(gla_w_gate2[0].reshape(4, 512), ((0, 4), (0, 0))),
                             jnp.pad(jnp.concatenate([sgu_ln_gain, sgu_ln_bias], axis=1), ((0, 7), (0, 0)))], axis=0)

    wg_in, g_small = _gather_first(s_gwi, small, "gather_first")
    w2 =_blocks_to_columns(g_small[:, :4, :].reshape(N_DEV, LR, 128))
    w2p = jnp.pad(w2, ((0, LRP - LR), (0, 0))).astype(BF16)
    lng = g_small[:, 8, :256].reshape(1, D)
    lnb = g_small[:, 8, 256:].reshape(1, D)
    like_gwo, like_swi = _sds((N_DEV, W_ROWS, D), BF16), _sds((N_DEV, D, IN_COLS_S), BF16)

    h0 = _prenorm(x0, npre0)
    proj0, (g_gwo, g_swi) = _mm(h0, wg_in, "nn", F32, tm=1024, tn=896, tk=D, name="gla_in", b_tiled=True, phases=[
        _Phase(like_gwo, None, [_gather_send(s_gwo, 0, W_ROWS)]),
        _Phase(like_swi, None, [_gather_send(s_swi, 0, 768, diagonal=False)])])
    (ypre0, states), (g_gwo, g_swi) = _gla_fwd(proj0, w2p, gla_b_gate, gla_o_gain, phases=[
        _Phase(like_gwo, g_gwo, [_gather_pass(0, W_ROWS)]),
        _Phase(like_swi, g_swi, [_gather_relay(0, 768), _gather_send(s_swi, 768, 512, diagonal=False)])])
    wg_out = g_gwo.reshape(D, D)
    y0, (g_swi,) = _mm(ypre0, wg_out, "nn", F32, tm=1024, tn=1024, tk=D, name="gla_out", phases=[
        _Phase(like_swi, g_swi, [_gather_pass(0, 768), _gather_relay(768, 512), _gather_send(s_swi, 1280, 512, diagonal=False)])])
    (x1, h1), (g_swi,) = _mid_fwd(x0, y0, npost0, npre1, phases=[
        _Phase(like_swi, g_swi, [_gather_pass(768, 512), _gather_relay(1280, 512), _gather_send(s_swi, 1792, 256, diagonal=False)])])
    g_swi, = _carry([_Phase(like_swi, g_swi, [_gather_pass(1280, 512), _gather_relay(1792, 256)])], "relay_sgu_w_in")
    g_swi, = _carry([_Phase(like_swi, g_swi, [_gather_pass(1792, 256)])], "pass_sgu_w_in")
    proj1, (g_swo,) = _mm(h1, g_swi, "nn", F32, tm=1024, tn=IN_COLS_S, tk=D, name="sgu_in", b_blocked=True, phases=[
        _Phase(like_gwo, None, [_gather_send(s_swo, 0, W_ROWS)])])
    (pre1,), (g_swo,) = _sgu_fwd(proj1, lng, lnb, ws, bsb, phases=[_Phase(like_gwo, g_swo, [_gather_pass(0, W_ROWS)])])
    ws_out = g_swo.reshape(D, D)
    loss_cols, dx2, dy1, dnpost1 = _nn_rows(pre1, ws_out, name="sgu_out", row_ins=[x1, tgt], vec_ins=[npost1],
                                            outs=[("vec", F32), ("row", F32), ("row", BF16), ("vec", F32)], tail=_final_tail)
    loss_here = jnp.pad((0.5 * jnp.sum(loss_cols) / D).reshape(1, 1), ((0, 7), (0, 127)))

    like_b_out, like_b_swi = _sds((3, W_ROWS, D), BF16), _sds((3, D, IN_COLS_S), BF16)
    like_b_gwi = _sds((3, D, IN_COLS_G), BF16)
    row_pair = dict(like=_sds((4, W_ROWS, D), BF16), block=lambda i, j: i, ordinal=lambda i, j: i >> 1,
                    dst=lambda ref, k, i, j: ref.at[k])
    col_pair = dict(like=_sds((4, D, IN_COLS_S), BF16), block=lambda i, j: j, ordinal=lambda i, j: 4 * i + (j >> 1),
                    dst=lambda ref, k, i, j: ref.at[k, pl.ds(pl.multiple_of(i * 1024, 1024), 1024)])

    mine = _own_table()
    dws_out, (a_swo,) = _mm(pre1, dy1, "tn", BF16, tm=W_ROWS, tn=D, tk=S, name="sgu_out_dw", pair=row_pair)
    p_swo = dws_out.reshape(N_DEV, W_ROWS, D)
    t_swo = _pair_sum(p_swo, a_swo, 0, W_ROWS, "pair_sum_sgu_w_out", table=mine)
    dpre1, _ = _mm(dy1, ws_out, "nt", F32, tm=1024, tn=1024, tk=D, name="sgu_out_dx")
    (dproj1, dlng, dlnb, dwsp, dbsp), (b_swo,) = _sgu_bwd(proj1, dpre1, lng, lnb, ws, wst, bsb, phases=[
        _Phase(like_b_out, None, [_reduce_cross(t_swo, 0, 0, W_ROWS)])])
    p_swi, (a_swi,) = _mm(h1, dproj1, "tn", BF16, tm=1024, tn=IN_COLS_S, tk=S, name="sgu_in_dw", out_blocked=True, pair=col_pair)
    t_swi = _pair_sum(p_swi, a_swi, 0, D, "pair_sum_sgu_w_in", table=mine)
    (dx1, dy0, dnpre1, dnpost0), (b_swi,) = _nt_rows(
        dproj1, g_swi, tk=IN_COLS_S, name="sgu_in_dx", b_blocked=True, row_ins=[dx2, x1, y0], vec_ins=[npre1, npost0],
        outs=[("row", F32), ("row", BF16), ("vec", F32), ("vec", F32)], tail=_mid_bwd_tail, phases=[
            _Phase(like_b_swi, None, [_reduce_cross(t_swi, 0, 0, 1280)])])
    dwg_out, (a_gwo,) = _mm(ypre0, dy0, "tn", BF16, tm=W_ROWS, tn=D, tk=S, name="gla_out_dw", pair=row_pair)
    p_gwo = dwg_out.reshape(N_DEV, W_ROWS, D)
    t_gwo = _pair_sum(p_gwo, a_gwo, 0, W_ROWS, "pair_sum_gla_w_out", table=mine)
    dypre0, _ = _mm(dy0, wg_out, "nt", F32, tm=1024, tn=1024, tk=D, name="gla_out_dx")
    late = [dnpre1, dnpost1, dlng, dlnb, dwsp, jnp.transpose(dbsp[:, :SGU_G])]
    late_pack = _pack(late)
    (dproj0, dogain, dbgate, dw2), (b_swi, b_gwo, g_late) = _gla_bwd(proj0, dypre0, states, w2p, gla_b_gate, gla_o_gain, phases=[
        _Phase(like_b_swi, b_swi, [_reduce_cross(t_swi, 1280, 1280, 768)]),
        _Phase(like_b_out, None, [_reduce_cross(t_gwo, 0, 0, W_ROWS)]),
        _Phase(_sds((N_DEV,) + late_pack.shape, F32), None, [_gather_send(late_pack, 0, late_pack.shape[0])])])
    half = D // 2
    dwg_in_a, (g_late,) = _mm(h0, dproj0, "tn", BF16, tm=half, tn=896, tk=S, name="gla_in_dw_a", m_tiles=(0, 1), phases=[
        _Phase(_sds((N_DEV,) + late_pack.shape, F32), g_late, [_gather_pass(0, late_pack.shape[0])])])
    own_gwi, a_gwi = _blockify_pair(dwg_in_a, None, None, 0, "blockify_gla_w_in_a")
    t_gwi_a = _pair_sum(own_gwi, a_gwi, 0, half, "pair_sum_gla_w_in_a")
    dwg_in_b, (b_gwi,) = _mm(h0, dproj0, "tn", BF16, tm=half, tn=896, tk=S, name="gla_in_dw_b", m_tiles=(1, 1), phases=[
        _Phase(like_b_gwi, None, [_reduce_cross(t_gwi_a, 0, 0, 512)])])
    own_gwi, a_gwi = _blockify_pair(dwg_in_b, own_gwi, a_gwi, half, "blockify_gla_w_in_b")
    t_gwi_b = _pair_sum(own_gwi, a_gwi, half, half, "pair_sum_gla_w_in_b")
    (grad_x, dnpre0), (b_gwi,) = _nt_rows(
        dproj0, wg_in, tk=896, name="gla_in_dx", b_tiled=True, row_ins=[dx1, x0], vec_ins=[npre0],
        outs=[("row", F32), ("vec", F32)], tail=_first_bwd_tail, phases=[
            _Phase(like_b_gwi, b_gwi, [_reduce_cross(t_gwi_a, 512, 512, 512), _reduce_cross(t_gwi_b, 0, half, half)])])

    early = [dnpre0, dnpost0, dbgate, dogain, dw2[:LR], loss_here]
    early_pack = _pack(early)
    like_early = _sds((N_DEV,) + early_pack.shape, F32)
    (g_swo, d_swo, nm_swo, nv_swo), (g_early,) = _sum_adamw(
        p_swo, a_swo, b_swo, sgu_w_out[0], m_sgu_w_out[0], v_sgu_w_out[0], name="adamw_sgu_w_out", table=mine, phases=[
            _Phase(like_early, None, [_gather_send(early_pack, 0, early_pack.shape[0])])])
    (g_gwo_, d_gwo, nm_gwo, nv_gwo), (g_early,) = _sum_adamw(
        p_gwo, a_gwo, b_gwo, gla_w_out[0], m_gla_w_out[0], v_gla_w_out[0], name="adamw_gla_w_out", table=mine, phases=[
            _Phase(like_early, g_early, [_gather_pass(0, early_pack.shape[0])])])
    (g_swi_, d_swi, nm_swi, nv_swi), _ = _sum_adamw(
        p_swi, a_swi, b_swi, sgu_w_in[0], m_sgu_w_in[0], v_sgu_w_in[0], name="adamw_sgu_w_in", table=mine)
    (g_gwi_, d_gwi, nm_gwi, nv_gwi), _ = _sum_adamw(
        own_gwi, a_gwi, b_gwi, gla_w_in[0], m_gla_w_in[0], v_gla_w_in[0], name="adamw_gla_w_in")

    g_npre1, g_npost1, g_lng_full, g_lnb_full, g_wsp, g_bsp = _unpack(_sum_parts(g_late, "sum_late_small_grads"), late)
    g_npre0, g_npost0, g_bgate, g_ogain, g_w2_full, loss_all = _unpack(_sum_parts(g_early, "sum_early_small_grads"), early)
    loss = loss_all[0, 0]
    g_w2 = lax.dynamic_slice(g_w2_full, (0, me * 128), (LR, 128))
    g_lng = lax.dynamic_slice(g_lng_full, (0, me * 256), (1, 256))
    g_lnb = lax.dynamic_slice(g_lnb_full, (0, me * 256), (1, 256))
    small_g = [jnp.concatenate([g_npre0, g_npre1], 0), jnp.concatenate([g_npost0, g_npost1], 0), g_w2, g_bgate, g_ogain,
               g_lng, g_lnb, g_wsp, g_bsp]
    small_w = [norm_pre, norm_post, gla_w_gate2[0], gla_b_gate, gla_o_gain, sgu_ln_gain, sgu_ln_bias, sgu_w_spatial[0], sgu_b_spatial[0]]
    small_m = [m_norm_pre, m_norm_post, m_gla_w_gate2[0], m_gla_b_gate, m_gla_o_gain, m_sgu_ln_gain, m_sgu_ln_bias, m_sgu_w_spatial[0], m_sgu_b_spatial[0]]
    small_v = [v_norm_pre, v_norm_post, v_gla_w_gate2[0], v_gla_b_gate, v_gla_o_gain, v_sgu_ln_gain, v_sgu_ln_bias, v_sgu_w_spatial[0], v_sgu_b_spatial[0]]
    d_pack, nm_pack, nv_pack = _adamw(_pack(small_w), _pack(small_g), _pack(small_m), _pack(small_v), "adamw_small")

    out_like = [norm_pre, norm_post, gla_w_gate2, gla_b_gate, gla_o_gain, sgu_ln_gain, sgu_ln_bias, sgu_w_spatial, sgu_b_spatial]
    sg_ = [g.reshape(s.shape) for g, s in zip(small_g, out_like)]
    sd_, sm_, sv_ = (_unpack(pk, out_like) for pk in (d_pack, nm_pack, nv_pack))

    def assemble(small_list, w_in_g, w_out_g, w_in_s, w_out_s):
        npre_, npost_, w2_, bg_, og_, lg_, lb_, wsp_, bsp_ = small_list
        return [npre_, npost_, w_in_g[None], w2_, bg_, og_, w_out_g[None], w_in_s[None], lg_, lb_, wsp_, bsp_, w_out_s[None]]

    return (loss, grad_x.reshape(1, S, D),
            *assemble(sg_, g_gwi_, g_gwo_, g_swi_, g_swo),
            *assemble(sd_, d_gwi, d_gwo, d_swi, d_swo),
            *assemble(sm_, nm_gwi, nm_gwo, nm_swi, nm_swo),
            *assemble(sv_, nv_gwi, nv_gwo, nv_swi, nv_swo))
```

```python
import functools

import jax
import jax.numpy as jnp
from jax import lax
from jax.experimental import pallas as pl
from jax.experimental.pallas import tpu as pltpu

F32 = jnp.float32
BF16 = jnp.bfloat16

N_DEV = 8
S = 2048
D = 2048
H = 4
DK = 256
DV = 512
C = 64
NC = S // C
GLA_COLS = 6160
GLA_PAD = 6272
Q0, K0, V0, G0, LR0 = 0, 1024, 2048, 4096, 6144
LR = 16
LRP = 128
SGU_COLS = 6144
SGU_BLOCK = 128
SGU_G = 8
SGU_GD = 256
EPS = 1e-6
GLA_TAU = 16.0

ADAM_LR, ADAM_B1, ADAM_B2, ADAM_EPS, ADAM_WD, ADAM_STEP = 0.001, 0.9, 0.999, 1e-08, 0.01, 10

V7X_VMEM_BYTES = 64 * 1024 * 1024
VMEM_CEILING = V7X_VMEM_BYTES - 6 * 1024 * 1024
MESH = pl.DeviceIdType.MESH
HBM_SPEC = pl.BlockSpec(memory_space=pl.ANY)


def _sds(shape, dtype):
    return jax.ShapeDtypeStruct(tuple(shape), dtype)


def _nbytes(shape, dtype):
    n = 1
    for s in shape:
        n *= s
    return n * jnp.dtype(dtype).itemsize


def _dot(a, b, dims=(((1,), (0,)), ((), ())), precision=None):
    return lax.dot_general(a, b, dims, precision=precision, preferred_element_type=F32)


NN = (((1,), (0,)), ((), ()))
TN = (((0,), (0,)), ((), ()))
NT = (((1,), (1,)), ((), ()))


def _place():
    return lax.axis_index("x"), lax.axis_index("y"), lax.axis_index("c")


def _index_of(px, py, pc):
    return 4 * px + 2 * py + pc


def _chips(x, y):
    return [(1 - x, y), (x, 1 - y), (1 - x, 1 - y)]


def _rcopy(src, dst, send_sem, recv_sem, to):
    return pltpu.make_async_remote_copy(src_ref=src, dst_ref=dst, send_sem=send_sem, recv_sem=recv_sem,
                                        device_id=to, device_id_type=MESH)


class _Move:
    def __init__(self, ins, n_remote, make, stage=None):
        self.ins, self.n_remote, self.make, self.stage = list(ins), n_remote, make, stage

    def scratch(self):
        sems = [pltpu.SemaphoreType.DMA((self.n_remote,)), pltpu.SemaphoreType.DMA((self.n_remote,))]
        return sems if self.stage is None else sems + [pltpu.SemaphoreType.DMA((1,)), pltpu.VMEM(*self.stage)]

    def start(self, in_refs, buf, scratch):
        sends, _, local = self.make(in_refs, buf, scratch[0], scratch[1])
        if local is not None:
            pltpu.make_async_copy(local[0], scratch[3], scratch[2].at[0]).start()
        for cp in sends:
            cp.start()

    def finish(self, in_refs, buf, scratch):
        sends, arrivals, local = self.make(in_refs, buf, scratch[0], scratch[1])
        if local is not None:
            pltpu.make_async_copy(local[0], scratch[3], scratch[2].at[0]).wait()
            out = pltpu.make_async_copy(scratch[3], local[1], scratch[2].at[0])
            out.start()
        for cp in arrivals:
            cp.wait_recv()
        for cp in sends:
            cp.wait_send()
        if local is not None:
            out.wait()


class _Phase:
    def __init__(self, like, so_far, moves):
        self.like, self.so_far, self.moves = like, so_far, list(moves)


def _gather_send(shard, r0, nr, diagonal=True):
    def make(in_refs, g, ss, rs):
        sh, = in_refs
        x, y, c = _place()
        me = _index_of(x, y, c)
        rows = pl.ds(r0, nr)
        peers = [(x, y, 1 - c)] + [(px, py, c) for px, py in _chips(x, y)[:3 if diagonal else 2]]
        sends = [_rcopy(sh.at[rows], g.at[me, rows], ss.at[k], rs.at[k], p) for k, p in enumerate(peers)]
        arrivals = [_rcopy(sh.at[rows], g.at[_index_of(*p), rows], ss.at[k], rs.at[k], p) for k, p in enumerate(peers)]
        return sends, arrivals, (sh.at[rows], g.at[me, rows])

    return _Move([shard], 4 if diagonal else 3, make, stage=((nr, shard.shape[1]), shard.dtype))


def _gather_relay(r0, nr):
    def make(in_refs, g, ss, rs):
        x, y, c = _place()
        nx, ny, nd = [(px, py, c) for px, py in _chips(x, y)]
        first, second = pl.ds(r0, nr // 2), pl.ds(r0 + nr // 2, nr // 2)
        sends = [_rcopy(g.at[_index_of(*nx), first], g.at[_index_of(*nx), first], ss.at[0], rs.at[0], ny),
                 _rcopy(g.at[_index_of(*ny), second], g.at[_index_of(*ny), second], ss.at[1], rs.at[1], nx)]
        arrivals = [_rcopy(g.at[_index_of(*nx), first], g.at[_index_of(*nd), first], ss.at[0], rs.at[0], ny),
                    _rcopy(g.at[_index_of(*ny), second], g.at[_index_of(*nd), second], ss.at[1], rs.at[1], nx)]
        return sends, arrivals, None

    return _Move([], 2, make)


def _gather_pass(r0, nr):
    def make(in_refs, g, ss, rs):
        x, y, c = _place()
        rows = pl.ds(r0, nr)
        sends = [_rcopy(g.at[_index_of(px, py, c), rows], g.at[_index_of(px, py, c), rows], ss.at[j], rs.at[j], (x, y, 1 - c))
                 for j, (px, py) in enumerate(_chips(x, y))]
        arrivals = [_rcopy(g.at[_index_of(px, py, c), rows], g.at[_index_of(px, py, 1 - c), rows], ss.at[j], rs.at[j], (x, y, 1 - c))
                    for j, (px, py) in enumerate(_chips(x, y))]
        return sends, arrivals, None

    return _Move([], 3, make)


def _own_table():
    x, y, c = _place()
    return jnp.stack([_index_of(px, py, c) for px, py in [(x, y)] + _chips(x, y)]).astype(jnp.int32)


def _blockify_pair(dw, own_so_far, a_so_far, dst_r0, name, phases=()):
    rows, tr, cw, win = dw.shape[0], 256, GLA_COLS // N_DEV, 896
    n_steps = rows // tr

    def body(*refs):
        x_ref, own_ref, a_ref, stage_ref, send_sems, recv_sem = refs[0], *refs[-5:]
        i = pl.program_id(0)
        x, y, c = _place()

        def send(slot, k):
            dst = a_ref.at[k, pl.ds(pl.multiple_of(dst_r0 + i * tr, tr), tr)]
            return _rcopy(stage_ref.at[slot], dst, send_sems.at[slot], recv_sem.at[0], (x, y, 1 - c))

        for j in range(N_DEV):
            window = x_ref[:, 768 * j:768 * j + win].astype(F32)
            tile = (pltpu.roll(window, win - 2 * j, 1) if j else window)[:, :cw].astype(BF16)
            k = ((j >> 2) ^ x) + 2 * (((j >> 1) & 1) ^ y)

            @pl.when((j & 1) == c)
            def _():
                own_ref[k] = tile

            @pl.when((j & 1) != c)
            def _():
                slot = (j >> 1) & 1
                if j >> 1 >= 2:
                    send(slot, k).wait_send()
                else:
                    pl.when(i > 0)(lambda: send(slot, k).wait_send())
                stage_ref[slot] = tile
                send(slot, k).start()

        @pl.when(i == n_steps - 1)
        def _():
            send(0, 0).wait_send()
            send(1, 0).wait_send()
            arrived = a_ref.at[:, pl.ds(dst_r0, rows)]
            _rcopy(arrived, arrived, send_sems.at[0], recv_sem.at[0], (x, y, 1 - c)).wait_recv()

    continues = a_so_far is not None
    (own, a), bufs = _hosted(
        body, name=name, grid=(n_steps,),
        in_specs=[pl.BlockSpec((tr, GLA_PAD), lambda i: (i, 0))] + [HBM_SPEC] * (2 * continues),
        out_specs=[pl.BlockSpec((4, tr, cw), lambda i: (0, dst_r0 // tr + i, 0)), HBM_SPEC],
        out_shape=[_sds((4, D, cw), BF16), _sds((4, D, cw), BF16)], args=[dw] + [own_so_far, a_so_far] * continues,
        scratch_shapes=[pltpu.VMEM((2, tr, cw), BF16), pltpu.SemaphoreType.DMA((2,)), pltpu.SemaphoreType.DMA((1,))],
        block_bytes=tr * GLA_PAD * 2 + 4 * tr * win * 2, scratch_bytes=2 * tr * win * 2, phases=phases,
        continued={1: 0, 2: 1} if continues else None)
    return own, a, bufs


def _reduce_cross(sums, src_r0, dst_r0, nr):
    def make(in_refs, b, ss, rs):
        t, = in_refs
        x, y, c = _place()
        src, dst = pl.ds(src_r0, nr), pl.ds(dst_r0, nr)
        sends = [_rcopy(t.at[j, src], b.at[j, dst], ss.at[j], rs.at[j], (px, py, c)) for j, (px, py) in enumerate(_chips(x, y))]
        return sends, sends, None

    return _Move([sums], 3, make)


def _hosted(body, *, name, grid, in_specs, out_specs, out_shape, args, scratch_shapes=(), block_bytes, scratch_bytes=0,
            phases=(), table=None, continued=None):
    n_in, n_out, n_scr = len(args), len(out_shape), len(scratch_shapes)
    all_args, all_out_shape, sems, aliases, layout = list(args), list(out_shape), [], dict(continued or {}), []
    for j, ph in enumerate(phases):
        counts = []
        for mv in ph.moves:
            all_args += mv.ins
            counts.append(len(mv.ins))
            sems += mv.scratch()
        if ph.so_far is not None:
            aliases[len(all_args)] = n_out + j
            all_args.append(ph.so_far)
        layout.append((counts, ph.so_far is not None))
        all_out_shape.append(ph.like)
    n_extra_in = len(all_args) - n_in

    def wrapped(*refs):
        ins, pos = refs[:n_in], n_in
        move_ins = []
        for counts, continues in layout:
            per_move = []
            for cnt in counts:
                per_move.append(refs[pos:pos + cnt])
                pos += cnt
            pos += continues
            move_ins.append(per_move)
        outs = refs[pos:pos + n_out]
        bufs = refs[pos + n_out:pos + n_out + len(phases)]
        pos += n_out + len(phases)
        scratch = refs[pos:pos + n_scr]
        pos += n_scr
        move_sems = []
        for ph in phases:
            per_move = []
            for mv in ph.moves:
                count = len(mv.scratch())
                per_move.append(refs[pos:pos + count])
                pos += count
            move_sems.append(per_move)

        def each_move(fn_name):
            for ph, buf, per_in, per_sem in zip(phases, bufs, move_ins, move_sems):
                for mv, mv_in, mv_sem in zip(ph.moves, per_in, per_sem):
                    getattr(mv, fn_name)(mv_in, buf, mv_sem)

        if phases:
            first = functools.reduce(jnp.logical_and, [pl.program_id(a) == 0 for a in range(len(grid))])
            last = functools.reduce(jnp.logical_and, [pl.program_id(a) == grid[a] - 1 for a in range(len(grid))])
            pl.when(first)(lambda: each_move("start"))
        body(*ins, *outs, *scratch)
        if phases:
            pl.when(last)(lambda: each_move("finish"))

    all_args = [pltpu.with_memory_space_constraint(a, pltpu.HBM) for a in all_args]
    est = 2 * block_bytes + scratch_bytes
    params = pltpu.CompilerParams(dimension_semantics=("arbitrary",) * len(grid),
                                  vmem_limit_bytes=min(VMEM_CEILING, max(32 * 1024 * 1024, 2 * est)))
    all_in_specs, all_out_specs = list(in_specs) + [HBM_SPEC] * n_extra_in, list(out_specs) + [HBM_SPEC] * len(phases)
    if table is None:
        results = pl.pallas_call(
            wrapped, grid=grid, in_specs=all_in_specs, out_specs=all_out_specs, out_shape=all_out_shape,
            scratch_shapes=list(scratch_shapes) + sems, input_output_aliases=aliases, compiler_params=params, name=name,
        )(*all_args)
    else:
        results = pl.pallas_call(
            lambda table_ref, *refs: wrapped(*refs),
            grid_spec=pltpu.PrefetchScalarGridSpec(num_scalar_prefetch=1, grid=grid, in_specs=all_in_specs, out_specs=all_out_specs,
                                                   scratch_shapes=list(scratch_shapes) + sems),
            out_shape=all_out_shape, input_output_aliases={k + 1: v for k, v in aliases.items()}, compiler_params=params, name=name,
        )(table, *all_args)
    return list(results[:n_out]), list(results[n_out:])


class _Both:
    def __init__(self, copies):
        self.copies = copies

    def start(self):
        for cp in self.copies:
            cp.start()

    def wait_send(self):
        for cp in self.copies:
            cp.wait_send()

    def wait_recv(self):
        for cp in self.copies:
            cp.wait_recv()


def _carry(phases, name):
    def body(o_ref):
        o_ref[...] = jnp.zeros_like(o_ref)

    _, bufs = _hosted(body, name=name, grid=(1,), in_specs=[], out_specs=[pl.BlockSpec((8, 128), lambda i: (0, 0))],
                      out_shape=[_sds((8, 128), F32)], args=[], block_bytes=8 * 128 * 4, phases=phases)
    return bufs


def _gather_first(shard, small, name):
    cw, tr, n_tiles = shard.shape[1], 256, GLA_PAD // 128

    def body(sh_ref, sm_ref, wn_ref, g_ref, gs_ref, wt_ref, win_ref, tmp_ref, send_sems, recv_sems, local_sems):
        x, y, c = _place()
        me, sibling = (x, y, c), (x, y, 1 - c)
        chips = _chips(x, y)

        def copy(base, out_ref, k, block, to, src=None):
            dst = out_ref.at[_index_of(*block)]
            return _rcopy(dst if src is None else src, dst, send_sems.at[base + k], recv_sems.at[base + k], to)

        icopy = functools.partial(copy, 0, g_ref)
        scopy = functools.partial(copy, 8, gs_ref)

        def wcopy(k, block, to, src=None):
            if k in (1, 2):
                return icopy(k, block, to, src)
            halves = []
            for part, sem in enumerate((k, {0: 15, 4: 16, 5: 17, 6: 18}[k])):
                rows = pl.ds(part * (D // 2), D // 2)
                dst = g_ref.at[_index_of(*block), rows]
                halves.append(_rcopy(dst if src is None else src.at[rows], dst, send_sems.at[sem], recv_sems.at[sem], to))
            return _Both(halves)

        def relay(k, block, half, to):
            rows = pl.ds(half * (D // 2), D // 2)
            ref = g_ref.at[_index_of(*block), rows]
            return _rcopy(ref, ref, send_sems.at[k], recv_sems.at[k], to)

        def load(src_ref, slot):
            cp = pltpu.make_async_copy(src_ref, win_ref.at[slot], local_sems.at[0])
            cp.start()
            cp.wait()

        def place(slot, block):
            b = _index_of(*block)

            def rows_chunk(r, carry):
                rows = pl.ds(pl.multiple_of(r * tr, tr), tr)
                tmp_ref[:, :cw] = win_ref[slot, rows, :].astype(F32)
                shifted = pltpu.roll(tmp_ref[...], 2 * b, 1)
                for u in range(7):
                    wt_ref[6 * b + u, rows, :] = (wt_ref[6 * b + u, rows, :].astype(F32) + shifted[:, 128 * u:128 * (u + 1)]).astype(BF16)
                return carry

            lax.fori_loop(0, D // tr, rows_chunk, 0)

        small_own = pltpu.make_async_copy(sm_ref, gs_ref.at[_index_of(*me)], local_sems.at[1])
        small_own.start()
        first = [wcopy(1 + j, me, (*chip, c), src=sh_ref) for j, chip in enumerate(chips[:2])]
        first += [scopy(0, me, sibling, src=sm_ref)] + [scopy(1 + j, me, (*chip, c), src=sm_ref) for j, chip in enumerate(chips)]
        for cp in first:
            cp.start()

        def clear(t, carry):
            wt_ref[t] = jnp.zeros((D, 128), BF16)
            return carry

        lax.fori_loop(0, n_tiles, clear, 0)
        tmp_ref[...] = jnp.zeros_like(tmp_ref)

        def emit(t):
            pltpu.make_async_copy(wt_ref.at[t], wn_ref.at[t], local_sems.at[2]).start()

        def take(block, slot, arrivals=None, pass_on=None):
            for cp in arrivals or ():
                cp.wait_recv()
            load(sh_ref if arrivals is None else g_ref.at[_index_of(*block)], slot)
            if pass_on is not None:
                pass_on.start()
            place(slot, block)
            for u in range(1, 6):
                emit(6 * _index_of(*block) + u)

        near_x, near_y, far = [(*chip, c) for chip in chips]
        to_sibling = wcopy(0, me, sibling, src=win_ref.at[0])
        pass_x = wcopy(4, near_x, sibling, src=win_ref.at[1])
        pass_y = wcopy(5, near_y, sibling, src=win_ref.at[0])
        pass_d = wcopy(6, far, sibling, src=win_ref.at[0])
        relays = [relay(3, near_x, 0, near_y), relay(7, near_y, 1, near_x)]
        take(me, 0, pass_on=to_sibling)
        take(near_x, 1, [wcopy(1, near_x, me)], pass_x)
        relays[0].start()
        to_sibling.wait_send()
        take(near_y, 0, [wcopy(2, near_y, me)], pass_y)
        relays[1].start()
        small_passed = []
        for j, chip in enumerate(chips):
            scopy(1 + j, (*chip, c), me).wait_recv()
            cp = scopy(4 + j, (*chip, c), sibling)
            cp.start()
            small_passed.append(cp)
        pass_x.wait_send()
        take(sibling, 1, [wcopy(0, sibling, me)])
        pass_y.wait_send()
        take((*chips[0], 1 - c), 0, [wcopy(4, (*chips[0], 1 - c), me)])
        take((*chips[1], 1 - c), 1, [wcopy(5, (*chips[1], 1 - c), me)])
        take(far, 0, [relay(3, far, 0, near_y), relay(7, far, 1, near_x)], pass_d)
        take((*chips[2], 1 - c), 1, [wcopy(6, (*chips[2], 1 - c), me)])
        for t in range(0, n_tiles, 6):
            emit(t)
        scopy(0, sibling, me).wait_recv()
        for j, chip in enumerate(chips):
            scopy(4 + j, (*chip, 1 - c), me).wait_recv()
        for cp in first + small_passed + relays + [pass_d]:
            cp.wait_send()
        small_own.wait()
        pltpu.make_async_copy(wn_ref, wn_ref, local_sems.at[2]).wait()

    wn, _, gs = pl.pallas_call(
        body,
        in_specs=[HBM_SPEC] * 2, out_specs=[HBM_SPEC] * 3,
        out_shape=[_sds((n_tiles, D, 128), BF16), _sds((N_DEV,) + shard.shape, BF16), _sds((N_DEV,) + small.shape, small.dtype)],
        scratch_shapes=[pltpu.VMEM((n_tiles, D, 128), BF16), pltpu.VMEM((2, D, cw), BF16), pltpu.VMEM((tr, 7 * 128), F32),
                        pltpu.SemaphoreType.DMA((19,)), pltpu.SemaphoreType.DMA((19,)), pltpu.SemaphoreType.DMA((3,))],
        compiler_params=pltpu.CompilerParams(vmem_limit_bytes=48 * 1024 * 1024),
        name=name,
    )(shard, small)
    return wn, gs


def _mm(a, b, mode, out_dtype, *, tm, tn, tk, name, b_blocked=False, b_tiled=False, out_blocked=False, m_tiles=None, pair=None,
        phases=()):
    if mode == "nn":
        (m, k), dims = a.shape, NN
        a_blk, a_map = (tm, tk), (lambda i, j, kk: (i, kk))
        if b_blocked:
            assert b.shape[1] == k and b.shape[2] == tn and tk == k
            n = b.shape[0] * tn
            b_spec = pl.BlockSpec((None, tk, tn), lambda i, j, kk: (j, kk, 0))
        elif b_tiled:
            assert b.shape[1] == k and b.shape[2] == 128 and tn % 128 == 0
            n = b.shape[0] * 128
            b_spec = pl.BlockSpec((tn // 128, tk, 128), lambda i, j, kk: (j, kk, 0))
        else:
            assert b.shape[0] == k
            n = b.shape[1]
            b_spec = pl.BlockSpec((tk, tn), lambda i, j, kk: (kk, j))
    elif mode == "tn":
        (k, m), n, dims = a.shape, b.shape[1], TN
        assert b.shape[0] == k
        first = 0 if m_tiles is None else m_tiles[0]
        a_blk, a_map = (tk, tm), (lambda i, j, kk: (kk, i + first))
        b_spec = pl.BlockSpec((tk, tn), lambda i, j, kk: (kk, j))
    else:
        (m, k), dims = a.shape, NT
        a_blk, a_map = (tm, tk), (lambda i, j, kk: (i, kk))
        if b_blocked:
            assert b.shape[0] * b.shape[2] == k and b.shape[2] == tk
            n = b.shape[1]
            b_spec = pl.BlockSpec((None, tn, tk), lambda i, j, kk: (kk, j, 0))
        elif b_tiled:
            assert b.shape[0] * 128 == k and b.shape[2] == 128 and tk % 128 == 0
            n = b.shape[1]
            b_spec = pl.BlockSpec((tk // 128, tn, 128), lambda i, j, kk: (kk, j, 0))
        else:
            assert b.shape[1] == k
            n = b.shape[0]
            b_spec = pl.BlockSpec((tn, tk), lambda i, j, kk: (j, kk))
    assert m % tm == 0 and n % tn == 0 and k % tk == 0, (a.shape, b.shape, mode)
    nk = k // tk
    n_row_tiles = m // tm if m_tiles is None else m_tiles[1]
    if out_blocked:
        out_shape, out_spec = _sds((n // tn, n_row_tiles * tm, tn), out_dtype), pl.BlockSpec((None, tm, tn), lambda i, j, kk: (j, i, 0))
    else:
        out_shape, out_spec = _sds((n_row_tiles * tm, n), out_dtype), pl.BlockSpec((tm, tn), lambda i, j, kk: (i, j))

    grid = (n_row_tiles, n // tn, nk)

    def body(a_ref, b_ref, o_ref, *rest):
        rhs = jnp.concatenate([b_ref[u] for u in range(b_ref.shape[0])], axis=1) if b_tiled else b_ref[...]
        p = _dot(a_ref[...], rhs, dims)
        if nk == 1:
            o_ref[...] = p.astype(out_dtype)
            if pair is not None:
                _send_to_sibling(p.astype(out_dtype), *rest)
        else:
            acc_ref, = rest
            kk = pl.program_id(2)

            @pl.when(kk == 0)
            def _():
                acc_ref[...] = p

            @pl.when(kk > 0)
            def _():
                acc_ref[...] += p

            @pl.when(kk == nk - 1)
            def _():
                o_ref[...] = acc_ref[...].astype(out_dtype)

    def _send_to_sibling(tile, pair_ref, stage_ref, send_sems, recv_sem):
        i, j = pl.program_id(0), pl.program_id(1)
        x, y, c = _place()
        blk = pair["block"](i, j)
        k = ((blk >> 2) ^ x) + 2 * (((blk >> 1) & 1) ^ y)
        ordinal = pair["ordinal"](i, j)

        def send(slot):
            return _rcopy(stage_ref.at[slot], pair["dst"](pair_ref, k, i, j), send_sems.at[slot], recv_sem.at[0], (x, y, 1 - c))

        @pl.when((blk & 1) != c)
        def _():
            slot = ordinal & 1

            @pl.when(ordinal >= 2)
            def _():
                send(slot).wait_send()

            stage_ref[slot] = tile
            send(slot).start()

        @pl.when((i == grid[0] - 1) & (j == grid[1] - 1))
        def _():
            send(0).wait_send()
            send(1).wait_send()
            _rcopy(pair_ref, pair_ref, send_sems.at[0], recv_sem.at[0], (x, y, 1 - c)).wait_recv()

    blocks = _nbytes(a_blk, a.dtype) + tk * tn * jnp.dtype(b.dtype).itemsize + _nbytes((tm, tn), out_dtype)
    out_specs, out_shapes, scratch = [out_spec], [out_shape], [] if nk == 1 else [pltpu.VMEM((tm, tn), F32)]
    scratch_bytes = _nbytes((tm, tn), F32) * (nk > 1)
    if pair is not None:
        assert nk == 1
        out_specs, out_shapes = out_specs + [HBM_SPEC], out_shapes + [pair["like"]]
        scratch = [pltpu.VMEM((2, tm, tn), out_dtype), pltpu.SemaphoreType.DMA((2,)), pltpu.SemaphoreType.DMA((1,))]
        scratch_bytes = 2 * _nbytes((tm, tn), out_dtype)
    outs, bufs = _hosted(
        body, name=name, grid=grid,
        in_specs=[pl.BlockSpec(a_blk, a_map), b_spec], out_specs=out_specs, out_shape=out_shapes, args=[a, b],
        scratch_shapes=scratch, block_bytes=blocks, scratch_bytes=scratch_bytes, phases=phases)
    return outs[0], outs[1:] + bufs


NT_ROWS_TM = 1024
NN_ROWS_TM = 512
NT_ROWS_SUB = 128


def _nt_rows(a, b, *, tk, name, row_ins, vec_ins, outs, tail, b_blocked=False, b_tiled=False, phases=()):
    m, k = a.shape
    tm, sub, nk = NT_ROWS_TM, NT_ROWS_SUB, k // tk
    n_sub = tm // sub
    if b_blocked:
        assert b.shape[0] * b.shape[2] == k and b.shape[2] == tk and b.shape[1] == D
        b_spec = pl.BlockSpec((None, D, tk), lambda i, kk: (kk, 0, 0))
    else:
        assert b_tiled and b.shape[0] * 128 == k and tk % 128 == 0 and b.shape[1] == D
        b_spec = pl.BlockSpec((tk // 128, D, 128), lambda i, kk: (kk, 0, 0))
    row_spec, vec_spec = pl.BlockSpec((tm, D), lambda i, kk: (i, 0)), pl.BlockSpec((1, D), lambda i, kk: (0, 0))
    n_row, n_vec, n_out = len(row_ins), len(vec_ins), len(outs)
    assert nk >= 2

    def body(a_ref, b_ref, *rest):
        row_hbm, vec_refs = rest[:n_row], rest[n_row:n_row + n_vec]
        out_refs = rest[n_row + n_vec:n_row + n_vec + n_out]
        acc_ref, row_sems = rest[n_row + n_vec + n_out], rest[-1]
        row_refs = rest[n_row + n_vec + n_out + 1:-1]
        rhs = jnp.concatenate([b_ref[u] for u in range(b_ref.shape[0])], axis=1) if b_tiled else b_ref[...]
        p = _dot(a_ref[...], rhs, NT)
        i, kk = pl.program_id(0), pl.program_id(1)

        def fetch(r, s):
            src = row_hbm[r].at[pl.ds(pl.multiple_of(i * tm + s * sub, sub), sub)]
            return pltpu.make_async_copy(src, row_refs[r].at[s % 2], row_sems.at[r, s % 2])

        @pl.when(kk == 0)
        def _():
            for r in range(n_row):
                fetch(r, 0).start()
            acc_ref[...] = p

        @pl.when(kk > 0)
        def _():
            acc_ref[...] += p

        @pl.when(kk == nk - 1)
        def _():
            for s in range(n_sub):
                for r in range(n_row):
                    if s + 1 < n_sub:
                        fetch(r, s + 1).start()
                    fetch(r, s).wait()
                rows = slice(s * sub, (s + 1) * sub)
                tail(acc_ref[rows, :], rows, (i == 0) if s == 0 else None, [ref[s % 2] for ref in row_refs], vec_refs, out_refs)

    out_specs = [row_spec if kind == "row" else vec_spec for kind, _ in outs]
    out_shape = [_sds((m, D) if kind == "row" else (1, D), dt) for kind, dt in outs]
    blocks = tm * tk * 2 + D * tk * 2 + sum(tm * D * jnp.dtype(dt).itemsize for kind, dt in outs if kind == "row")
    scratch = ([pltpu.VMEM((tm, D), F32)] + [pltpu.VMEM((2, sub, D), x.dtype) for x in row_ins]
               + [pltpu.SemaphoreType.DMA((n_row, 2))])
    return _hosted(body, name=name, grid=(m // tm, nk), in_specs=[pl.BlockSpec((tm, tk), lambda i, kk: (i, kk)), b_spec]
                   + [HBM_SPEC] * n_row + [vec_spec] * n_vec, out_specs=out_specs, out_shape=out_shape,
                   args=[a, b] + list(row_ins) + list(vec_ins), scratch_shapes=scratch,
                   block_bytes=blocks, scratch_bytes=tm * D * 4 + n_row * 2 * sub * D * 4, phases=phases)


def _nn_rows(a, b, *, name, row_ins, vec_ins, outs, tail):
    m, k = a.shape
    tm = NN_ROWS_TM
    assert b.shape == (k, D)
    row_spec, vec_spec = pl.BlockSpec((tm, D), lambda i: (i, 0)), pl.BlockSpec((1, D), lambda i: (0, 0))
    n_row, n_vec, n_out = len(row_ins), len(vec_ins), len(outs)

    def body(a_ref, b_ref, *rest):
        row_hbm, vec_refs = rest[:n_row], rest[n_row:n_row + n_vec]
        out_refs = rest[n_row + n_vec:n_row + n_vec + n_out]
        d_ref, row_sems = rest[n_row + n_vec + n_out], rest[-1]
        row_refs = rest[n_row + n_vec + n_out + 1:-1]
        i = pl.program_id(0)
        fetches = [pltpu.make_async_copy(row_hbm[r].at[pl.ds(pl.multiple_of(i * tm, tm), tm)], row_refs[r], row_sems.at[r])
                   for r in range(n_row)]
        for cp in fetches:
            cp.start()
        d_ref[...] = _dot(a_ref[...], b_ref[...])
        for cp in fetches:
            cp.wait()
        for s in range(tm // NT_ROWS_SUB):
            rows = slice(s * NT_ROWS_SUB, (s + 1) * NT_ROWS_SUB)
            tail(d_ref[rows, :], rows, (i == 0) if s == 0 else None, [ref[rows, :] for ref in row_refs], vec_refs, out_refs)

    out_specs = [row_spec if kind == "row" else vec_spec for kind, _ in outs]
    out_shape = [_sds((m, D) if kind == "row" else (1, D), dt) for kind, dt in outs]
    blocks = tm * k * 2 + k * D * 2 + sum(tm * D * jnp.dtype(dt).itemsize for kind, dt in outs if kind == "row")
    scratch = [pltpu.VMEM((tm, D), F32)] + [pltpu.VMEM((tm, D), x.dtype) for x in row_ins] + [pltpu.SemaphoreType.DMA((n_row,))]
    outs_, _ = _hosted(body, name=name, grid=(m // tm,), in_specs=[pl.BlockSpec((tm, k), lambda i: (i, 0)),
                                                                 pl.BlockSpec((k, D), lambda i: (0, 0))]
                       + [HBM_SPEC] * n_row + [vec_spec] * n_vec, out_specs=out_specs, out_shape=out_shape,
                       args=[a, b] + list(row_ins) + list(vec_ins), scratch_shapes=scratch,
                       block_bytes=blocks, scratch_bytes=(1 + n_row) * tm * D * 4)
    return outs_


def _vec_add(ref, value, first):
    if first is None:
        ref[...] += value
    else:
        pl.when(first)(lambda: ref.__setitem__(Ellipsis, value))
        pl.when(jnp.logical_not(first))(lambda: ref.__setitem__(Ellipsis, ref[...] + value))


RB = 256


def _row_spec(width):
    return pl.BlockSpec((RB, width), lambda i: (i, 0))


def _vec_spec(width):
    return pl.BlockSpec((1, width), lambda i: (0, 0))


def _rinv(x):
    return lax.rsqrt(jnp.mean(x * x, axis=-1, keepdims=True) + EPS)


def _norm_bwd(dyn, xhat, r):
    return r * (dyn - xhat * jnp.mean(dyn * xhat, axis=-1, keepdims=True))


def _colsum(x):
    return jnp.sum(x, axis=0, keepdims=True)


def _prenorm(x, gain):
    def body(x_ref, g_ref, h_ref):
        xv = x_ref[...]
        h_ref[...] = (xv * _rinv(xv) * g_ref[...]).astype(BF16)

    outs, _ = _hosted(body, name="prenorm", grid=(S // RB,), in_specs=[_row_spec(D), _vec_spec(D)], out_specs=[_row_spec(D)],
                      out_shape=[_sds((S, D), BF16)], args=[x, gain], block_bytes=RB * D * 6)
    return outs[0]


def _mid_fwd(x, y, npost, npre, phases=()):
    def body(x_ref, y_ref, po_ref, pr_ref, x1_ref, h1_ref):
        yv = y_ref[...]
        x1 = x_ref[...] + yv * _rinv(yv) * po_ref[...]
        x1_ref[...] = x1
        h1_ref[...] = (x1 * _rinv(x1) * pr_ref[...]).astype(BF16)

    return _hosted(body, name="mid_fwd", grid=(S // RB,), in_specs=[_row_spec(D), _row_spec(D), _vec_spec(D), _vec_spec(D)],
                   out_specs=[_row_spec(D), _row_spec(D)], out_shape=[_sds((S, D), F32), _sds((S, D), BF16)],
                   args=[x, y, npost, npre], block_bytes=RB * D * 14, phases=phases)


def _final_tail(yv, rows, first, row_vals, vec_refs, out_refs):
    (xv, tv), (po_ref,), (loss_ref, dx_ref, dy_ref, dpo_ref) = row_vals, vec_refs, out_refs
    r = _rinv(yv)
    yhat = yv * r
    err = xv + yhat * po_ref[...] - tv
    dx = err * (1.0 / D)
    dx_ref[rows, :] = dx
    dy_ref[rows, :] = _norm_bwd(dx * po_ref[...], yhat, r).astype(BF16)
    _vec_add(loss_ref, _colsum(err * err), first)
    _vec_add(dpo_ref, _colsum(dx * yhat), first)


def _mid_bwd_tail(dh, rows, first, row_vals, vec_refs, out_refs):
    (dx2, xv, yv), (pr_ref, po_ref), (dx1_ref, dy_ref, dpr_ref, dpo_ref) = row_vals, vec_refs, out_refs
    r = _rinv(xv)
    xhat = xv * r
    dx1 = dx2 + _norm_bwd(dh * pr_ref[...], xhat, r)
    dx1_ref[rows, :] = dx1
    ry = _rinv(yv)
    yhat = yv * ry
    dy_ref[rows, :] = _norm_bwd(dx1 * po_ref[...], yhat, ry).astype(BF16)
    _vec_add(dpr_ref, _colsum(dh * xhat), first)
    _vec_add(dpo_ref, _colsum(dx1 * yhat), first)


def _first_bwd_tail(dh, rows, first, row_vals, vec_refs, out_refs):
    (dx1, xv), (pr_ref,), (gx_ref, dpr_ref) = row_vals, vec_refs, out_refs
    r = _rinv(xv)
    xhat = xv * r
    gx_ref[rows, :] = dx1 + _norm_bwd(dh * pr_ref[...], xhat, r)
    _vec_add(dpr_ref, _colsum(dh * xhat), first)


GLA_RB = 256
GLA_CPB = GLA_RB // C


def _sigmoid(x):
    return 1.0 / (1.0 + jnp.exp(-x))


def _tri(strict):
    r = lax.broadcasted_iota(jnp.int32, (C, C), 0)
    c = lax.broadcasted_iota(jnp.int32, (C, C), 1)
    return jnp.where(c < r if strict else c <= r, 1.0, 0.0).astype(BF16)


def _tri_dot(tri, x):
    hi = x.astype(BF16)
    lo = (x - hi.astype(F32)).astype(BF16)
    return _dot(tri, hi) + _dot(tri, lo)


def _gla_gates(glr_b, w2, b, tri):
    z = _dot(glr_b, w2) + b
    log_a = (jnp.minimum(z, 0.0) - jnp.log(1.0 + jnp.exp(-jnp.abs(z)))) * (1.0 / GLA_TAU)
    bcum = _tri_dot(tri, log_a)
    b_end = jnp.sum(log_a, axis=0, keepdims=True)
    return z, jnp.exp(b_end - bcum), jnp.exp(b_end)


def _gla_fwd(proj, w2p, bgate, ogain, phases=()):
    def body(p_ref, w2_ref, b_ref, og_ref, y_ref, st_out_ref, st_ref):
        @pl.when(pl.program_id(0) == 0)
        def _():
            st_ref[...] = jnp.zeros_like(st_ref)

        tri = _tri(False)

        def chunk(ci, carry):
            rows = pl.ds(pl.multiple_of(ci * C, C), C)
            glr_b = p_ref[rows, LR0:LR0 + LRP].astype(BF16)
            _, ea_all, dec_all = _gla_gates(glr_b, w2_ref[...], b_ref[...], tri)
            for h in range(H):
                ea, dec = ea_all[:, h * DK:(h + 1) * DK], dec_all[:, h * DK:(h + 1) * DK]
                k_dec = (p_ref[rows, K0 + h * DK:K0 + (h + 1) * DK] * ea).astype(BF16)
                v_b = p_ref[rows, V0 + h * DV:V0 + (h + 1) * DV].astype(BF16)
                st = st_ref[h] * dec + _dot(v_b, k_dec, TN)
                st_ref[h] = st
                st_b = st.astype(BF16)
                st_out_ref[ci, h] = st_b
                q_b = (p_ref[rows, Q0 + h * DK:Q0 + (h + 1) * DK] * (DK ** -0.5)).astype(BF16)
                o = _dot(q_b, st_b, NT)
                on = o * _rinv(o)
                g = p_ref[rows, G0 + h * DV:G0 + (h + 1) * DV]
                y_ref[rows, h * DV:(h + 1) * DV] = (on * og_ref[:, h * DV:(h + 1) * DV] * (g * _sigmoid(g))).astype(BF16)
            return carry

        lax.fori_loop(0, GLA_CPB, chunk, 0, unroll=True)

    blocks = GLA_RB * GLA_PAD * 4 + GLA_RB * D * 2 + GLA_CPB * H * DV * DK * 2
    return _hosted(
        body, name="gla_fwd", grid=(S // GLA_RB,),
        in_specs=[pl.BlockSpec((GLA_RB, GLA_PAD), lambda i: (i, 0)),
                  pl.BlockSpec((LRP, H * DK), lambda i: (0, 0)),
                  pl.BlockSpec((1, H * DK), lambda i: (0, 0)),
                  pl.BlockSpec((1, H * DV), lambda i: (0, 0))],
        out_specs=[pl.BlockSpec((GLA_RB, H * DV), lambda i: (i, 0)),
                   pl.BlockSpec((GLA_CPB, H, DV, DK), lambda i: (i, 0, 0, 0))],
        out_shape=[_sds((S, H * DV), BF16), _sds((NC, H, DV, DK), BF16)],
        args=[proj, w2p, bgate, ogain], scratch_shapes=[pltpu.VMEM((H, DV, DK), F32)],
        block_bytes=blocks, scratch_bytes=H * DV * DK * 4, phases=phases)


def _gla_bwd(proj, dypre, states, w2p, bgate, ogain, phases=()):
    nb = S // GLA_RB

    def body(p_ref, dy_ref, st_blk_ref, st_prev_ref, w2_ref, b_ref, og_ref,
             dp_ref, dog_ref, dbg_ref, dw2_ref, r_ref):
        step = pl.program_id(0)

        @pl.when(step == 0)
        def _():
            r_ref[...] = jnp.zeros_like(r_ref)
            dog_ref[...] = jnp.zeros_like(dog_ref)
            dbg_ref[...] = jnp.zeros_like(dbg_ref)
            dw2_ref[...] = jnp.zeros_like(dw2_ref)

        tri = _tri(False)
        tri_strict = _tri(True)
        has_prev = jnp.where(step < nb - 1, 1.0, 0.0).astype(F32)

        def chunk(ci, st_prev_of):
            rows = pl.ds(ci * C if isinstance(ci, int) else pl.multiple_of(ci * C, C), C)
            glr_b = p_ref[rows, LR0:LR0 + LRP].astype(BF16)
            z, ea_all, dec_all = _gla_gates(glr_b, w2_ref[...], b_ref[...], tri)
            d_a, d_end = [], []
            for h in range(H):
                kcol = slice(h * DK, (h + 1) * DK)
                vcol = slice(h * DV, (h + 1) * DV)
                ea, dec = ea_all[:, kcol], dec_all[:, kcol]
                k_dec = p_ref[rows, K0 + h * DK:K0 + (h + 1) * DK] * ea
                k_dec_b = k_dec.astype(BF16)
                v_b = p_ref[rows, V0 + h * DV:V0 + (h + 1) * DV].astype(BF16)
                q_b = (p_ref[rows, Q0 + h * DK:Q0 + (h + 1) * DK] * (DK ** -0.5)).astype(BF16)
                st_b = st_blk_ref[ci, h]
                o = _dot(q_b, st_b, NT)
                rinv = _rinv(o)
                on = o * rinv
                g = p_ref[rows, G0 + h * DV:G0 + (h + 1) * DV]
                sg = _sigmoid(g)
                og = og_ref[:, vcol]
                dyp = dy_ref[rows, vcol]
                dp_ref[rows, G0 + h * DV:G0 + (h + 1) * DV] = (dyp * (on * og) * (sg * (1.0 + g * (1.0 - sg)))).astype(BF16)
                dpn = dyp * (g * sg)
                dog_ref[:, vcol] += _colsum(dpn * on)
                do_b = _norm_bwd(dpn * og, on, rinv).astype(BF16)
                gt = _dot(do_b, q_b, TN) + r_ref[h]
                gt_b = gt.astype(BF16)
                dp_ref[rows, Q0 + h * DK:Q0 + (h + 1) * DK] = (_dot(do_b, st_b) * (DK ** -0.5)).astype(BF16)
                dkd = _dot(v_b, gt_b)
                dp_ref[rows, V0 + h * DV:V0 + (h + 1) * DV] = _dot(k_dec_b, gt_b, NT).astype(BF16)
                dp_ref[rows, K0 + h * DK:K0 + (h + 1) * DK] = (dkd * ea).astype(BF16)
                d_a.append(dkd * k_dec)
                d_end.append(_colsum(gt * st_prev_of(h)) * dec)
                r_ref[h] = gt * dec
            dla = _tri_dot(tri_strict, jnp.concatenate(d_a, axis=1)) + jnp.concatenate(d_end, axis=1)
            dz = dla * (1.0 / GLA_TAU) * (1.0 - _sigmoid(z))
            dz_b = dz.astype(BF16)
            dbg_ref[...] += _colsum(dz)
            dw2_ref[...] += _dot(glr_b, dz_b, TN)
            dp_ref[rows, LR0:LR0 + LRP] = _dot(dz_b, w2_ref[...], NT).astype(BF16)

        def later_chunk(t, carry):
            ci = GLA_CPB - 1 - t
            chunk(ci, lambda h: st_blk_ref[ci - 1, h].astype(F32))
            return carry

        lax.fori_loop(0, GLA_CPB - 1, later_chunk, 0, unroll=True)
        chunk(0, lambda h: st_prev_ref[0, h].astype(F32) * has_prev)

    blocks = (GLA_RB * GLA_PAD * 4 + GLA_RB * D * 4 + (GLA_CPB + 1) * H * DV * DK * 2 + GLA_RB * GLA_PAD * 2)
    rev = lambda i: nb - 1 - i
    return _hosted(
        body, name="gla_bwd", grid=(nb,),
        in_specs=[pl.BlockSpec((GLA_RB, GLA_PAD), lambda i: (rev(i), 0)),
                  pl.BlockSpec((GLA_RB, H * DV), lambda i: (rev(i), 0)),
                  pl.BlockSpec((GLA_CPB, H, DV, DK), lambda i: (rev(i), 0, 0, 0)),
                  pl.BlockSpec((1, H, DV, DK), lambda i: (jnp.maximum(rev(i) * GLA_CPB - 1, 0), 0, 0, 0)),
                  pl.BlockSpec((LRP, H * DK), lambda i: (0, 0)),
                  pl.BlockSpec((1, H * DK), lambda i: (0, 0)),
                  pl.BlockSpec((1, H * DV), lambda i: (0, 0))],
        out_specs=[pl.BlockSpec((GLA_RB, GLA_PAD), lambda i: (rev(i), 0)),
                   pl.BlockSpec((1, H * DV), lambda i: (0, 0)),
                   pl.BlockSpec((1, H * DK), lambda i: (0, 0)),
                   pl.BlockSpec((LRP, H * DK), lambda i: (0, 0))],
        out_shape=[_sds((S, GLA_PAD), BF16), _sds((1, H * DV), F32), _sds((1, H * DK), F32), _sds((LRP, H * DK), F32)],
        args=[proj, dypre, states, states, w2p, bgate, ogain], scratch_shapes=[pltpu.VMEM((H, DV, DK), F32)],
        block_bytes=blocks, scratch_bytes=H * DV * DK * 4, phases=phases)


SGU_RB = 256
GELU_C = 0.7978845608028654
GELU_A = 0.044715


def _gelu(x):
    return 0.5 * x * (1.0 + jnp.tanh(GELU_C * (x + GELU_A * x * x * x)))


def _gelu_grad(x):
    t = jnp.tanh(GELU_C * (x + GELU_A * x * x * x))
    return 0.5 * (1.0 + t) + 0.5 * x * (1.0 - t * t) * (GELU_C * (1.0 + 3.0 * GELU_A * x * x))


def _causal_mask(transposed=False):
    i = lax.broadcasted_iota(jnp.int32, (SGU_BLOCK, SGU_BLOCK), 1 if transposed else 0)
    j = lax.broadcasted_iota(jnp.int32, (SGU_BLOCK, SGU_BLOCK), 0 if transposed else 1)
    return (i >= C) | (j < C)


def _layer_norm(vf, gain, bias):
    mu = jnp.mean(vf, axis=-1, keepdims=True)
    cen = vf - mu
    rstd = lax.rsqrt(jnp.mean(cen * cen, axis=-1, keepdims=True) + EPS)
    xhat = cen * rstd
    return xhat, rstd, xhat * gain + bias


def _sgu_fwd(proj, lng, lnb, ws, bsb, phases=()):
    def body(p_ref, g_ref, b_ref, ws_ref, bs_ref, o_ref):
        mask = _causal_mask()
        for n in range(SGU_RB // SGU_BLOCK):
            rows = slice(n * SGU_BLOCK, (n + 1) * SGU_BLOCK)
            _, _, vn = _layer_norm(_gelu(p_ref[rows, D:2 * D]), g_ref[...], b_ref[...])
            vn_b = vn.astype(BF16)
            for gi in range(SGU_G):
                cols = slice(gi * SGU_GD, (gi + 1) * SGU_GD)
                w = jnp.where(mask, ws_ref[gi], 0.0).astype(BF16)
                vs = _dot(w, vn_b[:, cols]) + bs_ref[gi]
                gate = p_ref[rows, 2 * D + gi * SGU_GD:2 * D + (gi + 1) * SGU_GD]
                o_ref[rows, cols] = (_gelu(p_ref[rows, cols]) * vs * (gate * _sigmoid(gate))).astype(BF16)

    blocks = SGU_RB * SGU_COLS * 4 + SGU_RB * D * 2 + SGU_G * SGU_BLOCK * (SGU_BLOCK + SGU_GD) * 4
    return _hosted(
        body, name="sgu_fwd", grid=(S // SGU_RB,),
        in_specs=[pl.BlockSpec((SGU_RB, SGU_COLS), lambda i: (i, 0)),
                  pl.BlockSpec((1, D), lambda i: (0, 0)), pl.BlockSpec((1, D), lambda i: (0, 0)),
                  pl.BlockSpec((SGU_G, SGU_BLOCK, SGU_BLOCK), lambda i: (0, 0, 0)),
                  pl.BlockSpec((SGU_G, SGU_BLOCK, SGU_GD), lambda i: (0, 0, 0))],
        out_specs=[pl.BlockSpec((SGU_RB, D), lambda i: (i, 0))], out_shape=[_sds((S, D), BF16)],
        args=[proj, lng, lnb, ws, bsb], block_bytes=blocks, phases=phases)


def _sgu_bwd(proj, dpre, lng, lnb, ws, wst, bsb, phases=()):
    nsteps = S // SGU_RB

    def body(p_ref, d_ref, g_ref, b_ref, ws_ref, wst_ref, bs_ref,
             dp_ref, dg_ref, db_ref, dws_ref, dbs_ref, dvn_ref, dvs_acc_ref):
        step = pl.program_id(0)

        @pl.when(step == 0)
        def _():
            dg_ref[...] = jnp.zeros_like(dg_ref)
            db_ref[...] = jnp.zeros_like(db_ref)
            dws_ref[...] = jnp.zeros_like(dws_ref)
            dvs_acc_ref[...] = jnp.zeros_like(dvs_acc_ref)

        mask = _causal_mask()
        maskt = _causal_mask(transposed=True)
        for n in range(SGU_RB // SGU_BLOCK):
            rows = slice(n * SGU_BLOCK, (n + 1) * SGU_BLOCK)
            v = p_ref[rows, D:2 * D]
            xhat, rstd, vn = _layer_norm(_gelu(v), g_ref[...], b_ref[...])
            vn_b = vn.astype(BF16)
            for gi in range(SGU_G):
                cols = slice(gi * SGU_GD, (gi + 1) * SGU_GD)
                w = jnp.where(mask, ws_ref[gi], 0.0).astype(BF16)
                wt = jnp.where(maskt, wst_ref[gi], 0.0).astype(BF16)
                vs = _dot(w, vn_b[:, cols]) + bs_ref[gi]
                u = p_ref[rows, cols]
                gate = p_ref[rows, 2 * D + gi * SGU_GD:2 * D + (gi + 1) * SGU_GD]
                sg = _sigmoid(gate)
                gu = _gelu(u)
                dpre_g = d_ref[rows, cols]
                t = dpre_g * (gate * sg)
                dp_ref[rows, cols] = (t * vs * _gelu_grad(u)).astype(BF16)
                dp_ref[rows, 2 * D + gi * SGU_GD:2 * D + (gi + 1) * SGU_GD] = (
                    dpre_g * gu * vs * (sg * (1.0 + gate * (1.0 - sg)))).astype(BF16)
                dvs = t * gu
                dvs_b = dvs.astype(BF16)
                dvs_acc_ref[:, cols] += dvs
                dws_ref[gi] += _dot(dvs_b, vn_b[:, cols], NT)
                dvn_ref[:, cols] = _dot(wt, dvs_b)
            dvn = dvn_ref[...]
            dg_ref[...] += _colsum(dvn * xhat)
            db_ref[...] += _colsum(dvn)
            dxh = dvn * g_ref[...]
            dvf = rstd * (dxh - jnp.mean(dxh, axis=-1, keepdims=True) - xhat * jnp.mean(dxh * xhat, axis=-1, keepdims=True))
            dp_ref[rows, D:2 * D] = (dvf * _gelu_grad(v)).astype(BF16)

        @pl.when(step == nsteps - 1)
        def _():
            lane = lax.broadcasted_iota(jnp.int32, (SGU_BLOCK, SGU_BLOCK), 1)
            out = jnp.zeros((SGU_BLOCK, SGU_BLOCK), F32)
            for gi in range(SGU_G):
                out = out + jnp.where(lane == gi, jnp.sum(dvs_acc_ref[:, gi * SGU_GD:(gi + 1) * SGU_GD], axis=1, keepdims=True), 0.0)
                dws_ref[gi] = jnp.where(mask, dws_ref[gi], 0.0)
            dbs_ref[...] = out

    blocks = SGU_RB * SGU_COLS * 6 + SGU_RB * D * 4 + SGU_G * SGU_BLOCK * (3 * SGU_BLOCK + SGU_GD) * 4
    const3 = lambda i: (0, 0, 0)
    return _hosted(
        body, name="sgu_bwd", grid=(nsteps,),
        in_specs=[pl.BlockSpec((SGU_RB, SGU_COLS), lambda i: (i, 0)),
                  pl.BlockSpec((SGU_RB, D), lambda i: (i, 0)),
                  pl.BlockSpec((1, D), lambda i: (0, 0)), pl.BlockSpec((1, D), lambda i: (0, 0)),
                  pl.BlockSpec((SGU_G, SGU_BLOCK, SGU_BLOCK), const3),
                  pl.BlockSpec((SGU_G, SGU_BLOCK, SGU_BLOCK), const3),
                  pl.BlockSpec((SGU_G, SGU_BLOCK, SGU_GD), const3)],
        out_specs=[pl.BlockSpec((SGU_RB, SGU_COLS), lambda i: (i, 0)),
                   pl.BlockSpec((1, D), lambda i: (0, 0)), pl.BlockSpec((1, D), lambda i: (0, 0)),
                   pl.BlockSpec((SGU_G, SGU_BLOCK, SGU_BLOCK), const3),
                   pl.BlockSpec((SGU_BLOCK, SGU_BLOCK), lambda i: (0, 0))],
        out_shape=[_sds((S, SGU_COLS), BF16), _sds((1, D), F32), _sds((1, D), F32),
                   _sds((SGU_G, SGU_BLOCK, SGU_BLOCK), F32), _sds((SGU_BLOCK, SGU_BLOCK), F32)],
        args=[proj, dpre, lng, lnb, ws, wst, bsb],
        scratch_shapes=[pltpu.VMEM((SGU_BLOCK, D), F32), pltpu.VMEM((SGU_BLOCK, D), F32)],
        block_bytes=blocks, scratch_bytes=2 * SGU_BLOCK * D * 4, phases=phases)


def _pair_sum(own, a, r0, nr, name, table=None):
    c = own.shape[2]
    tr = 256
    assert r0 % tr == 0 and nr % tr == 0

    def body(own_ref, sib_ref, o_ref):
        o_ref[...] = (own_ref[...].astype(F32) + sib_ref[...].astype(F32)).astype(BF16)

    own_map = ((lambda j, i: (1 + j, r0 // tr + i, 0)) if table is None else
               (lambda j, i, t: (t[1 + j], r0 // tr + i, 0)))
    cpad = -(-c // 128) * 128
    outs, _ = _hosted(
        body, name=name, grid=(3, nr // tr),
        in_specs=[pl.BlockSpec((None, tr, c), own_map),
                  pl.BlockSpec((None, tr, c), lambda j, i, *t: (1 + j, r0 // tr + i, 0))],
        out_specs=[pl.BlockSpec((None, tr, c), lambda j, i, *t: (j, i, 0))], out_shape=[_sds((3, nr, c), BF16)],
        args=[own, a], block_bytes=3 * tr * cpad * 2, table=table)
    return outs[0]


def _adamw_math(w, g, m, v):
    m = ADAM_B1 * m + (1.0 - ADAM_B1) * g
    v = ADAM_B2 * v + (1.0 - ADAM_B2) * (g * g)
    m_hat = m / (1.0 - ADAM_B1 ** ADAM_STEP)
    v_hat = v / (1.0 - ADAM_B2 ** ADAM_STEP)
    delta = -ADAM_LR * (m_hat / (jnp.sqrt(v_hat) + ADAM_EPS) + ADAM_WD * w)
    return delta, m, v


def _sum_adamw(own, a, b, w, m, v, *, name, phases=(), table=None):
    r, c = w.shape
    tr = 256

    def body(own_ref, sib_ref, far_ref, w_ref, m_ref, v_ref, g_ref, d_ref, nm_ref, nv_ref):
        g = own_ref[...].astype(F32) + sib_ref[...].astype(F32)
        for j in range(3):
            g = g + far_ref[j].astype(F32)
        g_ref[...] = g
        d_ref[...], nm_ref[...], nv_ref[...] = _adamw_math(w_ref[...], g, m_ref[...], v_ref[...])

    spec = pl.BlockSpec((tr, c), lambda i, *t: (i, 0))
    own_map = (lambda i: (0, i, 0)) if table is None else (lambda i, t: (t[0], i, 0))
    cpad = -(-c // 128) * 128
    return _hosted(
        body, name=name, grid=(r // tr,),
        in_specs=[pl.BlockSpec((None, tr, c), own_map), pl.BlockSpec((None, tr, c), lambda i, *t: (0, i, 0)),
                  pl.BlockSpec((3, tr, c), lambda i, *t: (0, i, 0)), spec, spec, spec],
        out_specs=[spec] * 4, out_shape=[_sds((r, c), F32)] * 4, args=[own, a, b, w, m, v],
        block_bytes=5 * tr * cpad * 2 + 7 * tr * cpad * 4, phases=phases, table=table)


def _sum_parts(parts, name):
    n, r, c = parts.shape

    def body(p_ref, o_ref):
        g = p_ref[0]
        for j in range(1, n):
            g = g + p_ref[j]
        o_ref[...] = g

    outs, _ = _hosted(body, name=name, grid=(1,), in_specs=[pl.BlockSpec((n, r, c), lambda i: (0, 0, 0))],
                      out_specs=[pl.BlockSpec((r, c), lambda i: (0, 0))], out_shape=[_sds((r, c), F32)], args=[parts],
                      block_bytes=(n + 1) * r * c * 4)
    return outs[0]


def _adamw(w, g, m, v, name):
    def body(w_ref, g_ref, m_ref, v_ref, d_ref, nm_ref, nv_ref):
        d_ref[...], nm_ref[...], nv_ref[...] = _adamw_math(w_ref[...], g_ref[...], m_ref[...], v_ref[...])

    spec = pl.BlockSpec(w.shape, lambda i: (0, 0))
    outs, _ = _hosted(body, name=name, grid=(1,), in_specs=[spec] * 4, out_specs=[spec] * 3, out_shape=[_sds(w.shape, F32)] * 3,
                      args=[w, g, m, v], block_bytes=7 * _nbytes(w.shape, F32))
    return outs


def _blocks_to_columns(g):
    n, r, c = g.shape
    return jnp.transpose(g, (1, 0, 2)).reshape(r, n * c)


def _pack(parts):
    return jnp.concatenate([p.reshape(-1) for p in parts]).reshape(-1, 128)


def _unpack(packed, like):
    flat, outs, off = packed.reshape(-1), [], 0
    for p in like:
        outs.append(flat[off:off + p.size].reshape(p.shape))
        off += p.size
    return outs


def kernel(x, norm_pre, norm_post, gla_w_in, gla_w_gate2, gla_b_gate, gla_o_gain, gla_w_out, sgu_w_in, sgu_ln_gain, sgu_ln_bias, sgu_w_spatial, sgu_b_spatial, sgu_w_out, loss_target, m_norm_pre, m_norm_post, m_gla_w_in, m_gla_w_gate2, m_gla_b_gate, m_gla_o_gain, m_gla_w_out, m_sgu_w_in, m_sgu_ln_gain, m_sgu_ln_bias, m_sgu_w_spatial, m_sgu_b_spatial, m_sgu_w_out, v_norm_pre, v_norm_post, v_gla_w_in, v_gla_w_gate2, v_gla_b_gate, v_gla_o_gain, v_gla_w_out, v_sgu_w_in, v_sgu_ln_gain, v_sgu_ln_bias, v_sgu_w_spatial, v_sgu_b_spatial, v_sgu_w_out):
    me = _index_of(*_place())
    x0 = x.reshape(S, D)
    tgt = loss_target.reshape(S, D)
    npre0, npre1 = norm_pre[0:1], norm_pre[1:2]
    npost0, npost1 = norm_post[0:1], norm_post[1:2]
    ws = sgu_w_spatial[0]
    wst = jnp.transpose(ws, (0, 2, 1))
    bsb = jnp.broadcast_to(sgu_b_spatial[0][:, :, None], (SGU_G, SGU_BLOCK, SGU_GD))
    W_ROWS = D // N_DEV
    IN_COLS_G, IN_COLS_S = GLA_COLS // N_DEV, SGU_COLS // N_DEV

    s_gwi, s_gwo = gla_w_in[0].astype(BF16), gla_w_out[0].astype(BF16)
    s_swi, s_swo = sgu_w_in[0].astype(BF16), sgu_w_out[0].astype(BF16)
    small = jnp.concatenate([jnp.pad(gla_w_gate2[0].reshape(4, 512), ((0, 4), (0, 0))),
                             jnp.pad(jnp.concatenate([sgu_ln_gain, sgu_ln_bias], axis=1), ((0, 7), (0, 0)))], axis=0)

    wg_in, g_small = _gather_first(s_gwi, small, "gather_first")
    w2 =_blocks_to_columns(g_small[:, :4, :].reshape(N_DEV, LR, 128))
    w2p = jnp.pad(w2, ((0, LRP - LR), (0, 0))).astype(BF16)
    lng = g_small[:, 8, :256].reshape(1, D)
    lnb = g_small[:, 8, 256:].reshape(1, D)
    like_gwo, like_swi = _sds((N_DEV, W_ROWS, D), BF16), _sds((N_DEV, D, IN_COLS_S), BF16)

    h0 = _prenorm(x0, npre0)
    proj0, (g_gwo, g_swi) = _mm(h0, wg_in, "nn", F32, tm=1024, tn=896, tk=D, name="gla_in", b_tiled=True, phases=[
        _Phase(like_gwo, None, [_gather_send(s_gwo, 0, W_ROWS)]),
        _Phase(like_swi, None, [_gather_send(s_swi, 0, 768, diagonal=False)])])
    (ypre0, states), (g_gwo, g_swi) = _gla_fwd(proj0, w2p, gla_b_gate, gla_o_gain, phases=[
        _Phase(like_gwo, g_gwo, [_gather_pass(0, W_ROWS)]),
        _Phase(like_swi, g_swi, [_gather_relay(0, 768), _gather_send(s_swi, 768, 512, diagonal=False)])])
    wg_out = g_gwo.reshape(D, D)
    y0, (g_swi,) = _mm(ypre0, wg_out, "nn", F32, tm=1024, tn=1024, tk=D, name="gla_out", phases=[
        _Phase(like_swi, g_swi, [_gather_pass(0, 768), _gather_relay(768, 512), _gather_send(s_swi, 1280, 512, diagonal=False)])])
    (x1, h1), (g_swi,) = _mid_fwd(x0, y0, npost0, npre1, phases=[
        _Phase(like_swi, g_swi, [_gather_pass(768, 512), _gather_relay(1280, 512), _gather_send(s_swi, 1792, 256, diagonal=False)])])
    g_swi, = _carry([_Phase(like_swi, g_swi, [_gather_pass(1280, 512), _gather_relay(1792, 256)])], "relay_sgu_w_in")
    g_swi, = _carry([_Phase(like_swi, g_swi, [_gather_pass(1792, 256)])], "pass_sgu_w_in")
    proj1, (g_swo,) = _mm(h1, g_swi, "nn", F32, tm=1024, tn=IN_COLS_S, tk=D, name="sgu_in", b_blocked=True, phases=[
        _Phase(like_gwo, None, [_gather_send(s_swo, 0, W_ROWS)])])
    (pre1,), (g_swo,) = _sgu_fwd(proj1, lng, lnb, ws, bsb, phases=[_Phase(like_gwo, g_swo, [_gather_pass(0, W_ROWS)])])
    ws_out = g_swo.reshape(D, D)
    loss_cols, dx2, dy1, dnpost1 = _nn_rows(pre1, ws_out, name="sgu_out", row_ins=[x1, tgt], vec_ins=[npost1],
                                            outs=[("vec", F32), ("row", F32), ("row", BF16), ("vec", F32)], tail=_final_tail)
    loss_here = jnp.pad((0.5 * jnp.sum(loss_cols) / D).reshape(1, 1), ((0, 7), (0, 127)))

    like_b_out, like_b_swi = _sds((3, W_ROWS, D), BF16), _sds((3, D, IN_COLS_S), BF16)
    like_b_gwi = _sds((3, D, IN_COLS_G), BF16)
    row_pair = dict(like=_sds((4, W_ROWS, D), BF16), block=lambda i, j: i, ordinal=lambda i, j: i >> 1,
                    dst=lambda ref, k, i, j: ref.at[k])
    col_pair = dict(like=_sds((4, D, IN_COLS_S), BF16), block=lambda i, j: j, ordinal=lambda i, j: 4 * i + (j >> 1),
                    dst=lambda ref, k, i, j: ref.at[k, pl.ds(pl.multiple_of(i * 1024, 1024), 1024)])

    mine = _own_table()
    dws_out, (a_swo,) = _mm(pre1, dy1, "tn", BF16, tm=W_ROWS, tn=D, tk=S, name="sgu_out_dw", pair=row_pair)
    p_swo = dws_out.reshape(N_DEV, W_ROWS, D)
    t_swo = _pair_sum(p_swo, a_swo, 0, W_ROWS, "pair_sum_sgu_w_out", table=mine)
    dpre1, _ = _mm(dy1, ws_out, "nt", F32, tm=1024, tn=1024, tk=D, name="sgu_out_dx")
    (dproj1, dlng, dlnb, dwsp, dbsp), (b_swo,) = _sgu_bwd(proj1, dpre1, lng, lnb, ws, wst, bsb, phases=[
        _Phase(like_b_out, None, [_reduce_cross(t_swo, 0, 0, W_ROWS)])])
    p_swi, (a_swi,) = _mm(h1, dproj1, "tn", BF16, tm=1024, tn=IN_COLS_S, tk=S, name="sgu_in_dw", out_blocked=True, pair=col_pair)
    t_swi = _pair_sum(p_swi, a_swi, 0, D, "pair_sum_sgu_w_in", table=mine)
    (dx1, dy0, dnpre1, dnpost0), (b_swi,) = _nt_rows(
        dproj1, g_swi, tk=IN_COLS_S, name="sgu_in_dx", b_blocked=True, row_ins=[dx2, x1, y0], vec_ins=[npre1, npost0],
        outs=[("row", F32), ("row", BF16), ("vec", F32), ("vec", F32)], tail=_mid_bwd_tail, phases=[
            _Phase(like_b_swi, None, [_reduce_cross(t_swi, 0, 0, 1024)])])
    dwg_out, (a_gwo, b_swi) = _mm(ypre0, dy0, "tn", BF16, tm=W_ROWS, tn=D, tk=S, name="gla_out_dw", pair=row_pair, phases=[
        _Phase(like_b_swi, b_swi, [_reduce_cross(t_swi, 1024, 1024, 256)])])
    p_gwo = dwg_out.reshape(N_DEV, W_ROWS, D)
    t_gwo = _pair_sum(p_gwo, a_gwo, 0, W_ROWS, "pair_sum_gla_w_out", table=mine)
    dypre0, _ = _mm(dy0, wg_out, "nt", F32, tm=1024, tn=1024, tk=D, name="gla_out_dx")
    late = [dnpre1, dnpost1, dlng, dlnb, dwsp, jnp.transpose(dbsp[:, :SGU_G])]
    late_pack = _pack(late)
    (dproj0, dogain, dbgate, dw2), (b_swi, b_gwo, g_late) = _gla_bwd(proj0, dypre0, states, w2p, gla_b_gate, gla_o_gain, phases=[
        _Phase(like_b_swi, b_swi, [_reduce_cross(t_swi, 1280, 1280, 768)]),
        _Phase(like_b_out, None, [_reduce_cross(t_gwo, 0, 0, W_ROWS)]),
        _Phase(_sds((N_DEV,) + late_pack.shape, F32), None, [_gather_send(late_pack, 0, late_pack.shape[0])])])
    half = D // 2
    dwg_in_a, (g_late,) = _mm(h0, dproj0, "tn", BF16, tm=half, tn=896, tk=S, name="gla_in_dw_a", m_tiles=(0, 1), phases=[
        _Phase(_sds((N_DEV,) + late_pack.shape, F32), g_late, [_gather_pass(0, late_pack.shape[0])])])
    own_gwi, a_gwi, _ = _blockify_pair(dwg_in_a, None, None, 0, "blockify_gla_w_in_a")
    t_gwi_a = _pair_sum(own_gwi, a_gwi, 0, half, "pair_sum_gla_w_in_a")
    dwg_in_b, (b_gwi,) = _mm(h0, dproj0, "tn", BF16, tm=half, tn=896, tk=S, name="gla_in_dw_b", m_tiles=(1, 1), phases=[
        _Phase(like_b_gwi, None, [_reduce_cross(t_gwi_a, 0, 0, 512)])])
    own_gwi, a_gwi, (b_gwi,) = _blockify_pair(dwg_in_b, own_gwi, a_gwi, half, "blockify_gla_w_in_b", phases=[
        _Phase(like_b_gwi, b_gwi, [_reduce_cross(t_gwi_a, 512, 512, 256)])])
    t_gwi_b = _pair_sum(own_gwi, a_gwi, half, half, "pair_sum_gla_w_in_b")
    (grad_x, dnpre0), (b_gwi,) = _nt_rows(
        dproj0, wg_in, tk=896, name="gla_in_dx", b_tiled=True, row_ins=[dx1, x0], vec_ins=[npre0],
        outs=[("row", F32), ("vec", F32)], tail=_first_bwd_tail, phases=[
            _Phase(like_b_gwi, b_gwi, [_reduce_cross(t_gwi_a, 768, 768, 256), _reduce_cross(t_gwi_b, 0, half, half)])])

    early = [dnpre0, dnpost0, dbgate, dogain, dw2[:LR], loss_here]
    early_pack = _pack(early)
    like_early = _sds((N_DEV,) + early_pack.shape, F32)
    (g_swo, d_swo, nm_swo, nv_swo), (g_early,) = _sum_adamw(
        p_swo, a_swo, b_swo, sgu_w_out[0], m_sgu_w_out[0], v_sgu_w_out[0], name="adamw_sgu_w_out", table=mine, phases=[
            _Phase(like_early, None, [_gather_send(early_pack, 0, early_pack.shape[0])])])
    (g_gwo_, d_gwo, nm_gwo, nv_gwo), (g_early,) = _sum_adamw(
        p_gwo, a_gwo, b_gwo, gla_w_out[0], m_gla_w_out[0], v_gla_w_out[0], name="adamw_gla_w_out", table=mine, phases=[
            _Phase(like_early, g_early, [_gather_pass(0, early_pack.shape[0])])])
    (g_swi_, d_swi, nm_swi, nv_swi), _ = _sum_adamw(
        p_swi, a_swi, b_swi, sgu_w_in[0], m_sgu_w_in[0], v_sgu_w_in[0], name="adamw_sgu_w_in", table=mine)
    (g_gwi_, d_gwi, nm_gwi, nv_gwi), _ = _sum_adamw(
        own_gwi, a_gwi, b_gwi, gla_w_in[0], m_gla_w_in[0], v_gla_w_in[0], name="adamw_gla_w_in")

    g_npre1, g_npost1, g_lng_full, g_lnb_full, g_wsp, g_bsp = _unpack(_sum_parts(g_late, "sum_late_small_grads"), late)
    g_npre0, g_npost0, g_bgate, g_ogain, g_w2_full, loss_all = _unpack(_sum_parts(g_early, "sum_early_small_grads"), early)
    loss = loss_all[0, 0]
    g_w2 = lax.dynamic_slice(g_w2_full, (0, me * 128), (LR, 128))
    g_lng = lax.dynamic_slice(g_lng_full, (0, me * 256), (1, 256))
    g_lnb = lax.dynamic_slice(g_lnb_full, (0, me * 256), (1, 256))
    small_g = [jnp.concatenate([g_npre0, g_npre1], 0), jnp.concatenate([g_npost0, g_npost1], 0), g_w2, g_bgate, g_ogain,
               g_lng, g_lnb, g_wsp, g_bsp]
    small_w = [norm_pre, norm_post, gla_w_gate2[0], gla_b_gate, gla_o_gain, sgu_ln_gain, sgu_ln_bias, sgu_w_spatial[0], sgu_b_spatial[0]]
    small_m = [m_norm_pre, m_norm_post, m_gla_w_gate2[0], m_gla_b_gate, m_gla_o_gain, m_sgu_ln_gain, m_sgu_ln_bias, m_sgu_w_spatial[0], m_sgu_b_spatial[0]]
    small_v = [v_norm_pre, v_norm_post, v_gla_w_gate2[0], v_gla_b_gate, v_gla_o_gain, v_sgu_ln_gain, v_sgu_ln_bias, v_sgu_w_spatial[0], v_sgu_b_spatial[0]]
    d_pack, nm_pack, nv_pack = _adamw(_pack(small_w), _pack(small_g), _pack(small_m), _pack(small_v), "adamw_small")

    out_like = [norm_pre, norm_post, gla_w_gate2, gla_b_gate, gla_o_gain, sgu_ln_gain, sgu_ln_bias, sgu_w_spatial, sgu_b_spatial]
    sg_ = [g.reshape(s.shape) for g, s in zip(small_g, out_like)]
    sd_, sm_, sv_ = (_unpack(pk, out_like) for pk in (d_pack, nm_pack, nv_pack))

    def assemble(small_list, w_in_g, w_out_g, w_in_s, w_out_s):
        npre_, npost_, w2_, bg_, og_, lg_, lb_, wsp_, bsp_ = small_list
        return [npre_, npost_, w_in_g[None], w2_, bg_, og_, w_out_g[None], w_in_s[None], lg_, lb_, wsp_, bsp_, w_out_s[None]]

    return (loss, grad_x.reshape(1, S, D),
            *assemble(sg_, g_gwi_, g_gwo_, g_swi_, g_swo),
            *assemble(sd_, d_gwi, d_gwo, d_swi, d_swo),
            *assemble(sm_, nm_gwi, nm_gwo, nm_swi, nm_swo),
            *assemble(sv_, nv_gwi, nv_gwo, nv_swi, nv_swo))
```

```python
import functools

import jax
import jax.numpy as jnp
from jax import lax
from jax.experimental import pallas as pl
from jax.experimental.pallas import tpu as pltpu

F32 = jnp.float32
BF16 = jnp.bfloat16

N_DEV = 8
S = 2048
D = 2048
H = 4
DK = 256
DV = 512
C = 64
NC = S // C
GLA_COLS = 6160
GLA_PAD = 6272
Q0, K0, V0, G0, LR0 = 0, 1024, 2048, 4096, 6144
LR = 16
LRP = 128
SGU_COLS = 6144
SGU_BLOCK = 128
SGU_G = 8
SGU_GD = 256
EPS = 1e-6
GLA_TAU = 16.0

ADAM_LR, ADAM_B1, ADAM_B2, ADAM_EPS, ADAM_WD, ADAM_STEP = 0.001, 0.9, 0.999, 1e-08, 0.01, 10

V7X_VMEM_BYTES = 64 * 1024 * 1024
VMEM_CEILING = V7X_VMEM_BYTES - 6 * 1024 * 1024
MESH = pl.DeviceIdType.MESH
HBM_SPEC = pl.BlockSpec(memory_space=pl.ANY)


def _sds(shape, dtype):
    return jax.ShapeDtypeStruct(tuple(shape), dtype)


def _nbytes(shape, dtype):
    n = 1
    for s in shape:
        n *= s
    return n * jnp.dtype(dtype).itemsize


def _dot(a, b, dims=(((1,), (0,)), ((), ())), precision=None):
    return lax.dot_general(a, b, dims, precision=precision, preferred_element_type=F32)


NN = (((1,), (0,)), ((), ()))
TN = (((0,), (0,)), ((), ()))
NT = (((1,), (1,)), ((), ()))


def _place():
    return lax.axis_index("x"), lax.axis_index("y"), lax.axis_index("c")


def _index_of(px, py, pc):
    return 4 * px + 2 * py + pc


def _chips(x, y):
    return [(1 - x, y), (x, 1 - y), (1 - x, 1 - y)]


def _rcopy(src, dst, send_sem, recv_sem, to):
    return pltpu.make_async_remote_copy(src_ref=src, dst_ref=dst, send_sem=send_sem, recv_sem=recv_sem,
                                        device_id=to, device_id_type=MESH)


class _Move:
    def __init__(self, ins, n_remote, make, stage=None):
        self.ins, self.n_remote, self.make, self.stage = list(ins), n_remote, make, stage

    def scratch(self):
        sems = [pltpu.SemaphoreType.DMA((self.n_remote,)), pltpu.SemaphoreType.DMA((self.n_remote,))]
        return sems if self.stage is None else sems + [pltpu.SemaphoreType.DMA((1,)), pltpu.VMEM(*self.stage)]

    def start(self, in_refs, buf, scratch):
        sends, _, local = self.make(in_refs, buf, scratch[0], scratch[1])
        if local is not None:
            pltpu.make_async_copy(local[0], scratch[3], scratch[2].at[0]).start()
        for cp in sends:
            cp.start()

    def finish(self, in_refs, buf, scratch):
        sends, arrivals, local = self.make(in_refs, buf, scratch[0], scratch[1])
        if local is not None:
            pltpu.make_async_copy(local[0], scratch[3], scratch[2].at[0]).wait()
            out = pltpu.make_async_copy(scratch[3], local[1], scratch[2].at[0])
            out.start()
        for cp in arrivals:
            cp.wait_recv()
        for cp in sends:
            cp.wait_send()
        if local is not None:
            out.wait()


class _Phase:
    def __init__(self, like, so_far, moves):
        self.like, self.so_far, self.moves = like, so_far, list(moves)


def _gather_send(shard, r0, nr, diagonal=True):
    def make(in_refs, g, ss, rs):
        sh, = in_refs
        x, y, c = _place()
        me = _index_of(x, y, c)
        rows = pl.ds(r0, nr)
        peers = [(x, y, 1 - c)] + [(px, py, c) for px, py in _chips(x, y)[:3 if diagonal else 2]]
        sends = [_rcopy(sh.at[rows], g.at[me, rows], ss.at[k], rs.at[k], p) for k, p in enumerate(peers)]
        arrivals = [_rcopy(sh.at[rows], g.at[_index_of(*p), rows], ss.at[k], rs.at[k], p) for k, p in enumerate(peers)]
        return sends, arrivals, (sh.at[rows], g.at[me, rows])

    return _Move([shard], 4 if diagonal else 3, make, stage=((nr, shard.shape[1]), shard.dtype))


def _gather_relay(r0, nr):
    def make(in_refs, g, ss, rs):
        x, y, c = _place()
        nx, ny, nd = [(px, py, c) for px, py in _chips(x, y)]
        first, second = pl.ds(r0, nr // 2), pl.ds(r0 + nr // 2, nr // 2)
        sends = [_rcopy(g.at[_index_of(*nx), first], g.at[_index_of(*nx), first], ss.at[0], rs.at[0], ny),
                 _rcopy(g.at[_index_of(*ny), second], g.at[_index_of(*ny), second], ss.at[1], rs.at[1], nx)]
        arrivals = [_rcopy(g.at[_index_of(*nx), first], g.at[_index_of(*nd), first], ss.at[0], rs.at[0], ny),
                    _rcopy(g.at[_index_of(*ny), second], g.at[_index_of(*nd), second], ss.at[1], rs.at[1], nx)]
        return sends, arrivals, None

    return _Move([], 2, make)


def _gather_pass(r0, nr):
    def make(in_refs, g, ss, rs):
        x, y, c = _place()
        rows = pl.ds(r0, nr)
        sends = [_rcopy(g.at[_index_of(px, py, c), rows], g.at[_index_of(px, py, c), rows], ss.at[j], rs.at[j], (x, y, 1 - c))
                 for j, (px, py) in enumerate(_chips(x, y))]
        arrivals = [_rcopy(g.at[_index_of(px, py, c), rows], g.at[_index_of(px, py, 1 - c), rows], ss.at[j], rs.at[j], (x, y, 1 - c))
                    for j, (px, py) in enumerate(_chips(x, y))]
        return sends, arrivals, None

    return _Move([], 3, make)


def _own_table():
    x, y, c = _place()
    return jnp.stack([_index_of(px, py, c) for px, py in [(x, y)] + _chips(x, y)]).astype(jnp.int32)


def _blockify_pair(dw, own_so_far, a_so_far, dst_r0, name, phases=()):
    rows, tr, cw, win = dw.shape[0], 256, GLA_COLS // N_DEV, 896
    n_steps = rows // tr

    def body(*refs):
        x_ref, own_ref, a_ref, stage_ref, send_sems, recv_sem = refs[0], *refs[-5:]
        i = pl.program_id(0)
        x, y, c = _place()

        def send(slot, k):
            dst = a_ref.at[k, pl.ds(pl.multiple_of(dst_r0 + i * tr, tr), tr)]
            return _rcopy(stage_ref.at[slot], dst, send_sems.at[slot], recv_sem.at[0], (x, y, 1 - c))

        for j in range(N_DEV):
            window = x_ref[:, 768 * j:768 * j + win].astype(F32)
            tile = (pltpu.roll(window, win - 2 * j, 1) if j else window)[:, :cw].astype(BF16)
            k = ((j >> 2) ^ x) + 2 * (((j >> 1) & 1) ^ y)

            @pl.when((j & 1) == c)
            def _():
                own_ref[k] = tile

            @pl.when((j & 1) != c)
            def _():
                slot = (j >> 1) & 1
                if j >> 1 >= 2:
                    send(slot, k).wait_send()
                else:
                    pl.when(i > 0)(lambda: send(slot, k).wait_send())
                stage_ref[slot] = tile
                send(slot, k).start()

        @pl.when(i == n_steps - 1)
        def _():
            send(0, 0).wait_send()
            send(1, 0).wait_send()
            arrived = a_ref.at[:, pl.ds(dst_r0, rows)]
            _rcopy(arrived, arrived, send_sems.at[0], recv_sem.at[0], (x, y, 1 - c)).wait_recv()

    continues = a_so_far is not None
    (own, a), bufs = _hosted(
        body, name=name, grid=(n_steps,),
        in_specs=[pl.BlockSpec((tr, GLA_PAD), lambda i: (i, 0))] + [HBM_SPEC] * (2 * continues),
        out_specs=[pl.BlockSpec((4, tr, cw), lambda i: (0, dst_r0 // tr + i, 0)), HBM_SPEC],
        out_shape=[_sds((4, D, cw), BF16), _sds((4, D, cw), BF16)], args=[dw] + [own_so_far, a_so_far] * continues,
        scratch_shapes=[pltpu.VMEM((2, tr, cw), BF16), pltpu.SemaphoreType.DMA((2,)), pltpu.SemaphoreType.DMA((1,))],
        block_bytes=tr * GLA_PAD * 2 + 4 * tr * win * 2, scratch_bytes=2 * tr * win * 2, phases=phases,
        continued={1: 0, 2: 1} if continues else None)
    return own, a, bufs


def _reduce_cross(sums, src_r0, dst_r0, nr):
    def make(in_refs, b, ss, rs):
        t, = in_refs
        x, y, c = _place()
        src, dst = pl.ds(src_r0, nr), pl.ds(dst_r0, nr)
        sends = [_rcopy(t.at[j, src], b.at[j, dst], ss.at[j], rs.at[j], (px, py, c)) for j, (px, py) in enumerate(_chips(x, y))]
        return sends, sends, None

    return _Move([sums], 3, make)


def _hosted(body, *, name, grid, in_specs, out_specs, out_shape, args, scratch_shapes=(), block_bytes, scratch_bytes=0,
            phases=(), table=None, continued=None):
    n_in, n_out, n_scr = len(args), len(out_shape), len(scratch_shapes)
    all_args, all_out_shape, sems, aliases, layout = list(args), list(out_shape), [], dict(continued or {}), []
    for j, ph in enumerate(phases):
        counts = []
        for mv in ph.moves:
            all_args += mv.ins
            counts.append(len(mv.ins))
            sems += mv.scratch()
        if ph.so_far is not None:
            aliases[len(all_args)] = n_out + j
            all_args.append(ph.so_far)
        layout.append((counts, ph.so_far is not None))
        all_out_shape.append(ph.like)
    n_extra_in = len(all_args) - n_in

    def wrapped(*refs):
        ins, pos = refs[:n_in], n_in
        move_ins = []
        for counts, continues in layout:
            per_move = []
            for cnt in counts:
                per_move.append(refs[pos:pos + cnt])
                pos += cnt
            pos += continues
            move_ins.append(per_move)
        outs = refs[pos:pos + n_out]
        bufs = refs[pos + n_out:pos + n_out + len(phases)]
        pos += n_out + len(phases)
        scratch = refs[pos:pos + n_scr]
        pos += n_scr
        move_sems = []
        for ph in phases:
            per_move = []
            for mv in ph.moves:
                count = len(mv.scratch())
                per_move.append(refs[pos:pos + count])
                pos += count
            move_sems.append(per_move)

        def each_move(fn_name):
            for ph, buf, per_in, per_sem in zip(phases, bufs, move_ins, move_sems):
                for mv, mv_in, mv_sem in zip(ph.moves, per_in, per_sem):
                    getattr(mv, fn_name)(mv_in, buf, mv_sem)

        if phases:
            first = functools.reduce(jnp.logical_and, [pl.program_id(a) == 0 for a in range(len(grid))])
            last = functools.reduce(jnp.logical_and, [pl.program_id(a) == grid[a] - 1 for a in range(len(grid))])
            pl.when(first)(lambda: each_move("start"))
        body(*ins, *outs, *scratch)
        if phases:
            pl.when(last)(lambda: each_move("finish"))

    all_args = [pltpu.with_memory_space_constraint(a, pltpu.HBM) for a in all_args]
    est = 2 * block_bytes + scratch_bytes
    params = pltpu.CompilerParams(dimension_semantics=("arbitrary",) * len(grid),
                                  vmem_limit_bytes=min(VMEM_CEILING, max(32 * 1024 * 1024, 2 * est)))
    all_in_specs, all_out_specs = list(in_specs) + [HBM_SPEC] * n_extra_in, list(out_specs) + [HBM_SPEC] * len(phases)
    if table is None:
        results = pl.pallas_call(
            wrapped, grid=grid, in_specs=all_in_specs, out_specs=all_out_specs, out_shape=all_out_shape,
            scratch_shapes=list(scratch_shapes) + sems, input_output_aliases=aliases, compiler_params=params, name=name,
        )(*all_args)
    else:
        results = pl.pallas_call(
            lambda table_ref, *refs: wrapped(*refs),
            grid_spec=pltpu.PrefetchScalarGridSpec(num_scalar_prefetch=1, grid=grid, in_specs=all_in_specs, out_specs=all_out_specs,
                                                   scratch_shapes=list(scratch_shapes) + sems),
            out_shape=all_out_shape, input_output_aliases={k + 1: v for k, v in aliases.items()}, compiler_params=params, name=name,
        )(table, *all_args)
    return list(results[:n_out]), list(results[n_out:])


class _Both:
    def __init__(self, copies):
        self.copies = copies

    def start(self):
        for cp in self.copies:
            cp.start()

    def wait_send(self):
        for cp in self.copies:
            cp.wait_send()

    def wait_recv(self):
        for cp in self.copies:
            cp.wait_recv()


def _carry(phases, name):
    def body(o_ref):
        o_ref[...] = jnp.zeros_like(o_ref)

    _, bufs = _hosted(body, name=name, grid=(1,), in_specs=[], out_specs=[pl.BlockSpec((8, 128), lambda i: (0, 0))],
                      out_shape=[_sds((8, 128), F32)], args=[], block_bytes=8 * 128 * 4, phases=phases)
    return bufs


def _gather_first(shard, small, name):
    cw, tr, n_tiles = shard.shape[1], 256, GLA_PAD // 128

    def body(sh_ref, sm_ref, wn_ref, g_ref, gs_ref, wt_ref, win_ref, tmp_ref, send_sems, recv_sems, local_sems):
        x, y, c = _place()
        me, sibling = (x, y, c), (x, y, 1 - c)
        chips = _chips(x, y)

        def copy(base, out_ref, k, block, to, src=None):
            dst = out_ref.at[_index_of(*block)]
            return _rcopy(dst if src is None else src, dst, send_sems.at[base + k], recv_sems.at[base + k], to)

        icopy = functools.partial(copy, 0, g_ref)
        scopy = functools.partial(copy, 8, gs_ref)

        def wcopy(k, block, to, src=None):
            if k in (1, 2):
                return icopy(k, block, to, src)
            halves = []
            for part, sem in enumerate((k, {0: 15, 4: 16, 5: 17, 6: 18}[k])):
                rows = pl.ds(part * (D // 2), D // 2)
                dst = g_ref.at[_index_of(*block), rows]
                halves.append(_rcopy(dst if src is None else src.at[rows], dst, send_sems.at[sem], recv_sems.at[sem], to))
            return _Both(halves)

        def relay(k, block, half, to):
            rows = pl.ds(half * (D // 2), D // 2)
            ref = g_ref.at[_index_of(*block), rows]
            return _rcopy(ref, ref, send_sems.at[k], recv_sems.at[k], to)

        def load(src_ref, slot):
            cp = pltpu.make_async_copy(src_ref, win_ref.at[slot], local_sems.at[0])
            cp.start()
            cp.wait()

        def place(slot, block):
            b = _index_of(*block)

            def rows_chunk(r, carry):
                rows = pl.ds(pl.multiple_of(r * tr, tr), tr)
                tmp_ref[:, :cw] = win_ref[slot, rows, :].astype(F32)
                shifted = pltpu.roll(tmp_ref[...], 2 * b, 1)
                for u in range(7):
                    wt_ref[6 * b + u, rows, :] = (wt_ref[6 * b + u, rows, :].astype(F32) + shifted[:, 128 * u:128 * (u + 1)]).astype(BF16)
                return carry

            lax.fori_loop(0, D // tr, rows_chunk, 0)

        small_own = pltpu.make_async_copy(sm_ref, gs_ref.at[_index_of(*me)], local_sems.at[1])
        small_own.start()
        first = [wcopy(1 + j, me, (*chip, c), src=sh_ref) for j, chip in enumerate(chips[:2])]
        first += [scopy(0, me, sibling, src=sm_ref)] + [scopy(1 + j, me, (*chip, c), src=sm_ref) for j, chip in enumerate(chips)]
        for cp in first:
            cp.start()

        def clear(t, carry):
            wt_ref[t] = jnp.zeros((D, 128), BF16)
            return carry

        lax.fori_loop(0, n_tiles, clear, 0)
        tmp_ref[...] = jnp.zeros_like(tmp_ref)

        def emit(t):
            pltpu.make_async_copy(wt_ref.at[t], wn_ref.at[t], local_sems.at[2]).start()

        def take(block, slot, arrivals=None, pass_on=None):
            for cp in arrivals or ():
                cp.wait_recv()
            load(sh_ref if arrivals is None else g_ref.at[_index_of(*block)], slot)
            if pass_on is not None:
                pass_on.start()
            place(slot, block)
            for u in range(1, 6):
                emit(6 * _index_of(*block) + u)

        near_x, near_y, far = [(*chip, c) for chip in chips]
        to_sibling = wcopy(0, me, sibling, src=win_ref.at[0])
        pass_x = wcopy(4, near_x, sibling, src=win_ref.at[1])
        pass_y = wcopy(5, near_y, sibling, src=win_ref.at[0])
        pass_d = wcopy(6, far, sibling, src=win_ref.at[0])
        relays = [relay(3, near_x, 0, near_y), relay(7, near_y, 1, near_x)]
        take(me, 0, pass_on=to_sibling)
        take(near_x, 1, [wcopy(1, near_x, me)], pass_x)
        relays[0].start()
        to_sibling.wait_send()
        take(near_y, 0, [wcopy(2, near_y, me)], pass_y)
        relays[1].start()
        small_passed = []
        for j, chip in enumerate(chips):
            scopy(1 + j, (*chip, c), me).wait_recv()
            cp = scopy(4 + j, (*chip, c), sibling)
            cp.start()
            small_passed.append(cp)
        pass_x.wait_send()
        take(sibling, 1, [wcopy(0, sibling, me)])
        pass_y.wait_send()
        take((*chips[0], 1 - c), 0, [wcopy(4, (*chips[0], 1 - c), me)])
        take((*chips[1], 1 - c), 1, [wcopy(5, (*chips[1], 1 - c), me)])
        take(far, 0, [relay(3, far, 0, near_y), relay(7, far, 1, near_x)], pass_d)
        take((*chips[2], 1 - c), 1, [wcopy(6, (*chips[2], 1 - c), me)])
        for t in range(0, n_tiles, 6):
            emit(t)
        scopy(0, sibling, me).wait_recv()
        for j, chip in enumerate(chips):
            scopy(4 + j, (*chip, 1 - c), me).wait_recv()
        for cp in first + small_passed + relays + [pass_d]:
            cp.wait_send()
        small_own.wait()
        pltpu.make_async_copy(wn_ref, wn_ref, local_sems.at[2]).wait()

    wn, _, gs = pl.pallas_call(
        body,
        in_specs=[HBM_SPEC] * 2, out_specs=[HBM_SPEC] * 3,
        out_shape=[_sds((n_tiles, D, 128), BF16), _sds((N_DEV,) + shard.shape, BF16), _sds((N_DEV,) + small.shape, small.dtype)],
        scratch_shapes=[pltpu.VMEM((n_tiles, D, 128), BF16), pltpu.VMEM((2, D, cw), BF16), pltpu.VMEM((tr, 7 * 128), F32),
                        pltpu.SemaphoreType.DMA((19,)), pltpu.SemaphoreType.DMA((19,)), pltpu.SemaphoreType.DMA((3,))],
        compiler_params=pltpu.CompilerParams(vmem_limit_bytes=48 * 1024 * 1024),
        name=name,
    )(shard, small)
    return wn, gs


def _mm(a, b, mode, out_dtype, *, tm, tn, tk, name, b_blocked=False, b_tiled=False, out_blocked=False, m_tiles=None, pair=None,
        phases=()):
    if mode == "nn":
        (m, k), dims = a.shape, NN
        a_blk, a_map = (tm, tk), (lambda i, j, kk: (i, kk))
        if b_blocked:
            assert b.shape[1] == k and b.shape[2] == tn and tk == k
            n = b.shape[0] * tn
            b_spec = pl.BlockSpec((None, tk, tn), lambda i, j, kk: (j, kk, 0))
        elif b_tiled:
            assert b.shape[1] == k and b.shape[2] == 128 and tn % 128 == 0
            n = b.shape[0] * 128
            b_spec = pl.BlockSpec((tn // 128, tk, 128), lambda i, j, kk: (j, kk, 0))
        else:
            assert b.shape[0] == k
            n = b.shape[1]
            b_spec = pl.BlockSpec((tk, tn), lambda i, j, kk: (kk, j))
    elif mode == "tn":
        (k, m), n, dims = a.shape, b.shape[1], TN
        assert b.shape[0] == k
        first = 0 if m_tiles is None else m_tiles[0]
        a_blk, a_map = (tk, tm), (lambda i, j, kk: (kk, i + first))
        b_spec = pl.BlockSpec((tk, tn), lambda i, j, kk: (kk, j))
    else:
        (m, k), dims = a.shape, NT
        a_blk, a_map = (tm, tk), (lambda i, j, kk: (i, kk))
        if b_blocked:
            assert b.shape[0] * b.shape[2] == k and b.shape[2] == tk
            n = b.shape[1]
            b_spec = pl.BlockSpec((None, tn, tk), lambda i, j, kk: (kk, j, 0))
        elif b_tiled:
            assert b.shape[0] * 128 == k and b.shape[2] == 128 and tk % 128 == 0
            n = b.shape[1]
            b_spec = pl.BlockSpec((tk // 128, tn, 128), lambda i, j, kk: (kk, j, 0))
        else:
            assert b.shape[1] == k
            n = b.shape[0]
            b_spec = pl.BlockSpec((tn, tk), lambda i, j, kk: (j, kk))
    assert m % tm == 0 and n % tn == 0 and k % tk == 0, (a.shape, b.shape, mode)
    nk = k // tk
    n_row_tiles = m // tm if m_tiles is None else m_tiles[1]
    if out_blocked:
        out_shape, out_spec = _sds((n // tn, n_row_tiles * tm, tn), out_dtype), pl.BlockSpec((None, tm, tn), lambda i, j, kk: (j, i, 0))
    else:
        out_shape, out_spec = _sds((n_row_tiles * tm, n), out_dtype), pl.BlockSpec((tm, tn), lambda i, j, kk: (i, j))

    grid = (n_row_tiles, n // tn, nk)

    def body(a_ref, b_ref, o_ref, *rest):
        rhs = jnp.concatenate([b_ref[u] for u in range(b_ref.shape[0])], axis=1) if b_tiled else b_ref[...]
        p = _dot(a_ref[...], rhs, dims)
        if nk == 1:
            o_ref[...] = p.astype(out_dtype)
            if pair is not None:
                _send_to_sibling(p.astype(out_dtype), *rest)
        else:
            acc_ref, = rest
            kk = pl.program_id(2)

            @pl.when(kk == 0)
            def _():
                acc_ref[...] = p

            @pl.when(kk > 0)
            def _():
                acc_ref[...] += p

            @pl.when(kk == nk - 1)
            def _():
                o_ref[...] = acc_ref[...].astype(out_dtype)

    def _send_to_sibling(tile, pair_ref, stage_ref, send_sems, recv_sem):
        i, j = pl.program_id(0), pl.program_id(1)
        x, y, c = _place()
        blk = pair["block"](i, j)
        k = ((blk >> 2) ^ x) + 2 * (((blk >> 1) & 1) ^ y)
        ordinal = pair["ordinal"](i, j)

        def send(slot):
            return _rcopy(stage_ref.at[slot], pair["dst"](pair_ref, k, i, j), send_sems.at[slot], recv_sem.at[0], (x, y, 1 - c))

        @pl.when((blk & 1) != c)
        def _():
            slot = ordinal & 1

            @pl.when(ordinal >= 2)
            def _():
                send(slot).wait_send()

            stage_ref[slot] = tile
            send(slot).start()

        @pl.when((i == grid[0] - 1) & (j == grid[1] - 1))
        def _():
            send(0).wait_send()
            send(1).wait_send()
            _rcopy(pair_ref, pair_ref, send_sems.at[0], recv_sem.at[0], (x, y, 1 - c)).wait_recv()

    blocks = _nbytes(a_blk, a.dtype) + tk * tn * jnp.dtype(b.dtype).itemsize + _nbytes((tm, tn), out_dtype)
    out_specs, out_shapes, scratch = [out_spec], [out_shape], [] if nk == 1 else [pltpu.VMEM((tm, tn), F32)]
    scratch_bytes = _nbytes((tm, tn), F32) * (nk > 1)
    if pair is not None:
        assert nk == 1
        out_specs, out_shapes = out_specs + [HBM_SPEC], out_shapes + [pair["like"]]
        scratch = [pltpu.VMEM((2, tm, tn), out_dtype), pltpu.SemaphoreType.DMA((2,)), pltpu.SemaphoreType.DMA((1,))]
        scratch_bytes = 2 * _nbytes((tm, tn), out_dtype)
    outs, bufs = _hosted(
        body, name=name, grid=grid,
        in_specs=[pl.BlockSpec(a_blk, a_map), b_spec], out_specs=out_specs, out_shape=out_shapes, args=[a, b],
        scratch_shapes=scratch, block_bytes=blocks, scratch_bytes=scratch_bytes, phases=phases)
    return outs[0], outs[1:] + bufs


def _dw_blocks(a, g, own_so_far, pair_so_far, m0, rows, name, phases=()):
    cw, win = GLA_COLS // N_DEV, 896
    continues = own_so_far is not None

    def body(*refs):
        a_ref, g_hbm = refs[:2]
        own_ref, pair_ref, win_ref, stage_ref, win_sems, local_sems, send_sems, recv_sem = refs[-8:]
        j = pl.program_id(0)
        x, y, c = _place()
        dst_rows = pl.ds(m0, rows)

        def window(jj):
            src = g_hbm.at[:, pl.ds(pl.multiple_of(768 * jj, 128), win)]
            return pltpu.make_async_copy(src, win_ref.at[jj % 2], win_sems.at[jj % 2])

        pl.when(j == 0)(lambda: window(j).start())
        pl.when(j + 1 < N_DEV)(lambda: window(j + 1).start())
        window(j).wait()
        p = _dot(a_ref[...], win_ref[j % 2], TN)
        tile = pltpu.roll(p, lax.rem(win - 2 * j, win), 1)[:, :cw].astype(BF16)
        k = ((j >> 2) ^ x) + 2 * (((j >> 1) & 1) ^ y)

        def local(slot, kk):
            return pltpu.make_async_copy(stage_ref.at[slot], own_ref.at[kk, dst_rows], local_sems.at[slot])

        def send(slot, kk):
            return _rcopy(stage_ref.at[slot], pair_ref.at[kk, dst_rows], send_sems.at[slot], recv_sem.at[0], (x, y, 1 - c))

        slot = j & 1

        @pl.when(slot == c)
        def _():
            pl.when(j >= 2)(lambda: local(slot, k).wait())
            stage_ref[slot] = tile
            local(slot, k).start()

        @pl.when(slot != c)
        def _():
            pl.when(j >= 2)(lambda: send(slot, k).wait_send())
            stage_ref[slot] = tile
            send(slot, k).start()

        @pl.when(j == N_DEV - 1)
        def _():
            local(c, 0).wait()
            send(1 - c, 0).wait_send()
            arrived = pair_ref.at[:, dst_rows]
            _rcopy(arrived, arrived, send_sems.at[0], recv_sem.at[0], (x, y, 1 - c)).wait_recv()

    like = _sds((4, D, cw), BF16)
    return _hosted(
        body, name=name, grid=(N_DEV,),
        in_specs=[pl.BlockSpec((S, rows), lambda j: (0, m0 // rows)), HBM_SPEC] + [HBM_SPEC] * (2 * continues),
        out_specs=[HBM_SPEC, HBM_SPEC], out_shape=[like, like], args=[a, g] + [own_so_far, pair_so_far] * continues,
        scratch_shapes=[pltpu.VMEM((2, S, win), BF16), pltpu.VMEM((2, rows, cw), BF16), pltpu.SemaphoreType.DMA((2,)),
                        pltpu.SemaphoreType.DMA((2,)), pltpu.SemaphoreType.DMA((2,)), pltpu.SemaphoreType.DMA((1,))],
        block_bytes=S * rows * 2, scratch_bytes=2 * S * win * 2 + 2 * rows * win * 2 + 2 * rows * win * 4, phases=phases,
        continued={2: 0, 3: 1} if continues else None)


NT_ROWS_TM = 1024
NN_ROWS_TM = 512
NT_ROWS_SUB = 128


def _nt_rows(a, b, *, tk, name, row_ins, vec_ins, outs, tail, b_blocked=False, b_tiled=False, phases=()):
    m, k = a.shape
    tm, sub, nk = NT_ROWS_TM, NT_ROWS_SUB, k // tk
    n_sub = tm // sub
    if b_blocked:
        assert b.shape[0] * b.shape[2] == k and b.shape[2] == tk and b.shape[1] == D
        b_spec = pl.BlockSpec((None, D, tk), lambda i, kk: (kk, 0, 0))
    else:
        assert b_tiled and b.shape[0] * 128 == k and tk % 128 == 0 and b.shape[1] == D
        b_spec = pl.BlockSpec((tk // 128, D, 128), lambda i, kk: (kk, 0, 0))
    row_spec, vec_spec = pl.BlockSpec((tm, D), lambda i, kk: (i, 0)), pl.BlockSpec((1, D), lambda i, kk: (0, 0))
    n_row, n_vec, n_out = len(row_ins), len(vec_ins), len(outs)
    assert nk >= 2

    def body(a_ref, b_ref, *rest):
        row_hbm, vec_refs = rest[:n_row], rest[n_row:n_row + n_vec]
        out_refs = rest[n_row + n_vec:n_row + n_vec + n_out]
        acc_ref, row_sems = rest[n_row + n_vec + n_out], rest[-1]
        row_refs = rest[n_row + n_vec + n_out + 1:-1]
        rhs = jnp.concatenate([b_ref[u] for u in range(b_ref.shape[0])], axis=1) if b_tiled else b_ref[...]
        p = _dot(a_ref[...], rhs, NT)
        i, kk = pl.program_id(0), pl.program_id(1)

        def fetch(r, s):
            src = row_hbm[r].at[pl.ds(pl.multiple_of(i * tm + s * sub, sub), sub)]
            return pltpu.make_async_copy(src, row_refs[r].at[s % 2], row_sems.at[r, s % 2])

        @pl.when(kk == 0)
        def _():
            for r in range(n_row):
                fetch(r, 0).start()
            acc_ref[...] = p

        @pl.when(kk > 0)
        def _():
            acc_ref[...] += p

        @pl.when(kk == nk - 1)
        def _():
            for s in range(n_sub):
                for r in range(n_row):
                    if s + 1 < n_sub:
                        fetch(r, s + 1).start()
                    fetch(r, s).wait()
                rows = slice(s * sub, (s + 1) * sub)
                tail(acc_ref[rows, :], rows, (i == 0) if s == 0 else None, [ref[s % 2] for ref in row_refs], vec_refs, out_refs)

    out_specs = [row_spec if kind == "row" else vec_spec for kind, _ in outs]
    out_shape = [_sds((m, D) if kind == "row" else (1, D), dt) for kind, dt in outs]
    blocks = tm * tk * 2 + D * tk * 2 + sum(tm * D * jnp.dtype(dt).itemsize for kind, dt in outs if kind == "row")
    scratch = ([pltpu.VMEM((tm, D), F32)] + [pltpu.VMEM((2, sub, D), x.dtype) for x in row_ins]
               + [pltpu.SemaphoreType.DMA((n_row, 2))])
    return _hosted(body, name=name, grid=(m // tm, nk), in_specs=[pl.BlockSpec((tm, tk), lambda i, kk: (i, kk)), b_spec]
                   + [HBM_SPEC] * n_row + [vec_spec] * n_vec, out_specs=out_specs, out_shape=out_shape,
                   args=[a, b] + list(row_ins) + list(vec_ins), scratch_shapes=scratch,
                   block_bytes=blocks, scratch_bytes=tm * D * 4 + n_row * 2 * sub * D * 4, phases=phases)


def _nn_rows(a, b, *, name, row_ins, vec_ins, outs, tail):
    m, k = a.shape
    tm = NN_ROWS_TM
    assert b.shape == (k, D)
    row_spec, vec_spec = pl.BlockSpec((tm, D), lambda i: (i, 0)), pl.BlockSpec((1, D), lambda i: (0, 0))
    n_row, n_vec, n_out = len(row_ins), len(vec_ins), len(outs)

    def body(a_ref, b_ref, *rest):
        row_hbm, vec_refs = rest[:n_row], rest[n_row:n_row + n_vec]
        out_refs = rest[n_row + n_vec:n_row + n_vec + n_out]
        d_ref, row_sems = rest[n_row + n_vec + n_out], rest[-1]
        row_refs = rest[n_row + n_vec + n_out + 1:-1]
        i = pl.program_id(0)
        fetches = [pltpu.make_async_copy(row_hbm[r].at[pl.ds(pl.multiple_of(i * tm, tm), tm)], row_refs[r], row_sems.at[r])
                   for r in range(n_row)]
        for cp in fetches:
            cp.start()
        d_ref[...] = _dot(a_ref[...], b_ref[...])
        for cp in fetches:
            cp.wait()
        for s in range(tm // NT_ROWS_SUB):
            rows = slice(s * NT_ROWS_SUB, (s + 1) * NT_ROWS_SUB)
            tail(d_ref[rows, :], rows, (i == 0) if s == 0 else None, [ref[rows, :] for ref in row_refs], vec_refs, out_refs)

    out_specs = [row_spec if kind == "row" else vec_spec for kind, _ in outs]
    out_shape = [_sds((m, D) if kind == "row" else (1, D), dt) for kind, dt in outs]
    blocks = tm * k * 2 + k * D * 2 + sum(tm * D * jnp.dtype(dt).itemsize for kind, dt in outs if kind == "row")
    scratch = [pltpu.VMEM((tm, D), F32)] + [pltpu.VMEM((tm, D), x.dtype) for x in row_ins] + [pltpu.SemaphoreType.DMA((n_row,))]
    outs_, _ = _hosted(body, name=name, grid=(m // tm,), in_specs=[pl.BlockSpec((tm, k), lambda i: (i, 0)),
                                                                 pl.BlockSpec((k, D), lambda i: (0, 0))]
                       + [HBM_SPEC] * n_row + [vec_spec] * n_vec, out_specs=out_specs, out_shape=out_shape,
                       args=[a, b] + list(row_ins) + list(vec_ins), scratch_shapes=scratch,
                       block_bytes=blocks, scratch_bytes=(1 + n_row) * tm * D * 4)
    return outs_


def _vec_add(ref, value, first):
    if first is None:
        ref[...] += value
    else:
        pl.when(first)(lambda: ref.__setitem__(Ellipsis, value))
        pl.when(jnp.logical_not(first))(lambda: ref.__setitem__(Ellipsis, ref[...] + value))


RB = 256


def _row_spec(width):
    return pl.BlockSpec((RB, width), lambda i: (i, 0))


def _vec_spec(width):
    return pl.BlockSpec((1, width), lambda i: (0, 0))


def _rinv(x):
    return lax.rsqrt(jnp.mean(x * x, axis=-1, keepdims=True) + EPS)


def _norm_bwd(dyn, xhat, r):
    return r * (dyn - xhat * jnp.mean(dyn * xhat, axis=-1, keepdims=True))


def _colsum(x):
    return jnp.sum(x, axis=0, keepdims=True)


def _prenorm(x, gain):
    def body(x_ref, g_ref, h_ref):
        xv = x_ref[...]
        h_ref[...] = (xv * _rinv(xv) * g_ref[...]).astype(BF16)

    outs, _ = _hosted(body, name="prenorm", grid=(S // RB,), in_specs=[_row_spec(D), _vec_spec(D)], out_specs=[_row_spec(D)],
                      out_shape=[_sds((S, D), BF16)], args=[x, gain], block_bytes=RB * D * 6)
    return outs[0]


def _mid_fwd(x, y, npost, npre, phases=()):
    def body(x_ref, y_ref, po_ref, pr_ref, x1_ref, h1_ref):
        yv = y_ref[...]
        x1 = x_ref[...] + yv * _rinv(yv) * po_ref[...]
        x1_ref[...] = x1
        h1_ref[...] = (x1 * _rinv(x1) * pr_ref[...]).astype(BF16)

    return _hosted(body, name="mid_fwd", grid=(S // RB,), in_specs=[_row_spec(D), _row_spec(D), _vec_spec(D), _vec_spec(D)],
                   out_specs=[_row_spec(D), _row_spec(D)], out_shape=[_sds((S, D), F32), _sds((S, D), BF16)],
                   args=[x, y, npost, npre], block_bytes=RB * D * 14, phases=phases)


def _final_tail(yv, rows, first, row_vals, vec_refs, out_refs):
    (xv, tv), (po_ref,), (loss_ref, dx_ref, dy_ref, dpo_ref) = row_vals, vec_refs, out_refs
    r = _rinv(yv)
    yhat = yv * r
    err = xv + yhat * po_ref[...] - tv
    dx = err * (1.0 / D)
    dx_ref[rows, :] = dx
    dy_ref[rows, :] = _norm_bwd(dx * po_ref[...], yhat, r).astype(BF16)
    _vec_add(loss_ref, _colsum(err * err), first)
    _vec_add(dpo_ref, _colsum(dx * yhat), first)


def _mid_bwd_tail(dh, rows, first, row_vals, vec_refs, out_refs):
    (dx2, xv, yv), (pr_ref, po_ref), (dx1_ref, dy_ref, dpr_ref, dpo_ref) = row_vals, vec_refs, out_refs
    r = _rinv(xv)
    xhat = xv * r
    dx1 = dx2 + _norm_bwd(dh * pr_ref[...], xhat, r)
    dx1_ref[rows, :] = dx1
    ry = _rinv(yv)
    yhat = yv * ry
    dy_ref[rows, :] = _norm_bwd(dx1 * po_ref[...], yhat, ry).astype(BF16)
    _vec_add(dpr_ref, _colsum(dh * xhat), first)
    _vec_add(dpo_ref, _colsum(dx1 * yhat), first)


def _first_bwd_tail(dh, rows, first, row_vals, vec_refs, out_refs):
    (dx1, xv), (pr_ref,), (gx_ref, dpr_ref) = row_vals, vec_refs, out_refs
    r = _rinv(xv)
    xhat = xv * r
    gx_ref[rows, :] = dx1 + _norm_bwd(dh * pr_ref[...], xhat, r)
    _vec_add(dpr_ref, _colsum(dh * xhat), first)


GLA_RB = 256
GLA_CPB = GLA_RB // C


def _sigmoid(x):
    return 1.0 / (1.0 + jnp.exp(-x))


def _tri(strict):
    r = lax.broadcasted_iota(jnp.int32, (C, C), 0)
    c = lax.broadcasted_iota(jnp.int32, (C, C), 1)
    return jnp.where(c < r if strict else c <= r, 1.0, 0.0).astype(BF16)


def _tri_dot(tri, x):
    hi = x.astype(BF16)
    lo = (x - hi.astype(F32)).astype(BF16)
    return _dot(tri, hi) + _dot(tri, lo)


def _gla_gates(glr_b, w2, b, tri):
    z = _dot(glr_b, w2) + b
    log_a = (jnp.minimum(z, 0.0) - jnp.log(1.0 + jnp.exp(-jnp.abs(z)))) * (1.0 / GLA_TAU)
    bcum = _tri_dot(tri, log_a)
    b_end = jnp.sum(log_a, axis=0, keepdims=True)
    return z, jnp.exp(b_end - bcum), jnp.exp(b_end)


def _gla_fwd(proj, w2p, bgate, ogain, phases=()):
    def body(p_ref, w2_ref, b_ref, og_ref, y_ref, st_out_ref, st_ref):
        @pl.when(pl.program_id(0) == 0)
        def _():
            st_ref[...] = jnp.zeros_like(st_ref)

        tri = _tri(False)

        def chunk(ci, carry):
            rows = pl.ds(pl.multiple_of(ci * C, C), C)
            glr_b = p_ref[rows, LR0:LR0 + LRP].astype(BF16)
            _, ea_all, dec_all = _gla_gates(glr_b, w2_ref[...], b_ref[...], tri)
            for h in range(H):
                ea, dec = ea_all[:, h * DK:(h + 1) * DK], dec_all[:, h * DK:(h + 1) * DK]
                k_dec = (p_ref[rows, K0 + h * DK:K0 + (h + 1) * DK] * ea).astype(BF16)
                v_b = p_ref[rows, V0 + h * DV:V0 + (h + 1) * DV].astype(BF16)
                st = st_ref[h] * dec + _dot(v_b, k_dec, TN)
                st_ref[h] = st
                st_b = st.astype(BF16)
                st_out_ref[ci, h] = st_b
                q_b = (p_ref[rows, Q0 + h * DK:Q0 + (h + 1) * DK] * (DK ** -0.5)).astype(BF16)
                o = _dot(q_b, st_b, NT)
                on = o * _rinv(o)
                g = p_ref[rows, G0 + h * DV:G0 + (h + 1) * DV]
                y_ref[rows, h * DV:(h + 1) * DV] = (on * og_ref[:, h * DV:(h + 1) * DV] * (g * _sigmoid(g))).astype(BF16)
            return carry

        lax.fori_loop(0, GLA_CPB, chunk, 0, unroll=True)

    blocks = GLA_RB * GLA_PAD * 4 + GLA_RB * D * 2 + GLA_CPB * H * DV * DK * 2
    return _hosted(
        body, name="gla_fwd", grid=(S // GLA_RB,),
        in_specs=[pl.BlockSpec((GLA_RB, GLA_PAD), lambda i: (i, 0)),
                  pl.BlockSpec((LRP, H * DK), lambda i: (0, 0)),
                  pl.BlockSpec((1, H * DK), lambda i: (0, 0)),
                  pl.BlockSpec((1, H * DV), lambda i: (0, 0))],
        out_specs=[pl.BlockSpec((GLA_RB, H * DV), lambda i: (i, 0)),
                   pl.BlockSpec((GLA_CPB, H, DV, DK), lambda i: (i, 0, 0, 0))],
        out_shape=[_sds((S, H * DV), BF16), _sds((NC, H, DV, DK), BF16)],
        args=[proj, w2p, bgate, ogain], scratch_shapes=[pltpu.VMEM((H, DV, DK), F32)],
        block_bytes=blocks, scratch_bytes=H * DV * DK * 4, phases=phases)


def _gla_bwd(proj, dypre, states, w2p, bgate, ogain, phases=()):
    nb = S // GLA_RB

    def body(p_ref, dy_ref, st_blk_ref, st_prev_ref, w2_ref, b_ref, og_ref,
             dp_ref, dog_ref, dbg_ref, dw2_ref, r_ref):
        step = pl.program_id(0)

        @pl.when(step == 0)
        def _():
            r_ref[...] = jnp.zeros_like(r_ref)
            dog_ref[...] = jnp.zeros_like(dog_ref)
            dbg_ref[...] = jnp.zeros_like(dbg_ref)
            dw2_ref[...] = jnp.zeros_like(dw2_ref)

        tri = _tri(False)
        tri_strict = _tri(True)
        has_prev = jnp.where(step < nb - 1, 1.0, 0.0).astype(F32)

        def chunk(ci, st_prev_of):
            rows = pl.ds(ci * C if isinstance(ci, int) else pl.multiple_of(ci * C, C), C)
            glr_b = p_ref[rows, LR0:LR0 + LRP].astype(BF16)
            z, ea_all, dec_all = _gla_gates(glr_b, w2_ref[...], b_ref[...], tri)
            d_a, d_end = [], []
            for h in range(H):
                kcol = slice(h * DK, (h + 1) * DK)
                vcol = slice(h * DV, (h + 1) * DV)
                ea, dec = ea_all[:, kcol], dec_all[:, kcol]
                k_dec = p_ref[rows, K0 + h * DK:K0 + (h + 1) * DK] * ea
                k_dec_b = k_dec.astype(BF16)
                v_b = p_ref[rows, V0 + h * DV:V0 + (h + 1) * DV].astype(BF16)
                q_b = (p_ref[rows, Q0 + h * DK:Q0 + (h + 1) * DK] * (DK ** -0.5)).astype(BF16)
                st_b = st_blk_ref[ci, h]
                o = _dot(q_b, st_b, NT)
                rinv = _rinv(o)
                on = o * rinv
                g = p_ref[rows, G0 + h * DV:G0 + (h + 1) * DV]
                sg = _sigmoid(g)
                og = og_ref[:, vcol]
                dyp = dy_ref[rows, vcol]
                dp_ref[rows, G0 + h * DV:G0 + (h + 1) * DV] = (dyp * (on * og) * (sg * (1.0 + g * (1.0 - sg)))).astype(BF16)
                dpn = dyp * (g * sg)
                dog_ref[:, vcol] += _colsum(dpn * on)
                do_b = _norm_bwd(dpn * og, on, rinv).astype(BF16)
                gt = _dot(do_b, q_b, TN) + r_ref[h]
                gt_b = gt.astype(BF16)
                dp_ref[rows, Q0 + h * DK:Q0 + (h + 1) * DK] = (_dot(do_b, st_b) * (DK ** -0.5)).astype(BF16)
                dkd = _dot(v_b, gt_b)
                dp_ref[rows, V0 + h * DV:V0 + (h + 1) * DV] = _dot(k_dec_b, gt_b, NT).astype(BF16)
                dp_ref[rows, K0 + h * DK:K0 + (h + 1) * DK] = (dkd * ea).astype(BF16)
                d_a.append(dkd * k_dec)
                d_end.append(_colsum(gt * st_prev_of(h)) * dec)
                r_ref[h] = gt * dec
            dla = _tri_dot(tri_strict, jnp.concatenate(d_a, axis=1)) + jnp.concatenate(d_end, axis=1)
            dz = dla * (1.0 / GLA_TAU) * (1.0 - _sigmoid(z))
            dz_b = dz.astype(BF16)
            dbg_ref[...] += _colsum(dz)
            dw2_ref[...] += _dot(glr_b, dz_b, TN)
            dp_ref[rows, LR0:LR0 + LRP] = _dot(dz_b, w2_ref[...], NT).astype(BF16)

        def later_chunk(t, carry):
            ci = GLA_CPB - 1 - t
            chunk(ci, lambda h: st_blk_ref[ci - 1, h].astype(F32))
            return carry

        lax.fori_loop(0, GLA_CPB - 1, later_chunk, 0, unroll=True)
        chunk(0, lambda h: st_prev_ref[0, h].astype(F32) * has_prev)

    blocks = (GLA_RB * GLA_PAD * 4 + GLA_RB * D * 4 + (GLA_CPB + 1) * H * DV * DK * 2 + GLA_RB * GLA_PAD * 2)
    rev = lambda i: nb - 1 - i
    return _hosted(
        body, name="gla_bwd", grid=(nb,),
        in_specs=[pl.BlockSpec((GLA_RB, GLA_PAD), lambda i: (rev(i), 0)),
                  pl.BlockSpec((GLA_RB, H * DV), lambda i: (rev(i), 0)),
                  pl.BlockSpec((GLA_CPB, H, DV, DK), lambda i: (rev(i), 0, 0, 0)),
                  pl.BlockSpec((1, H, DV, DK), lambda i: (jnp.maximum(rev(i) * GLA_CPB - 1, 0), 0, 0, 0)),
                  pl.BlockSpec((LRP, H * DK), lambda i: (0, 0)),
                  pl.BlockSpec((1, H * DK), lambda i: (0, 0)),
                  pl.BlockSpec((1, H * DV), lambda i: (0, 0))],
        out_specs=[pl.BlockSpec((GLA_RB, GLA_PAD), lambda i: (rev(i), 0)),
                   pl.BlockSpec((1, H * DV), lambda i: (0, 0)),
                   pl.BlockSpec((1, H * DK), lambda i: (0, 0)),
                   pl.BlockSpec((LRP, H * DK), lambda i: (0, 0))],
        out_shape=[_sds((S, GLA_PAD), BF16), _sds((1, H * DV), F32), _sds((1, H * DK), F32), _sds((LRP, H * DK), F32)],
        args=[proj, dypre, states, states, w2p, bgate, ogain], scratch_shapes=[pltpu.VMEM((H, DV, DK), F32)],
        block_bytes=blocks, scratch_bytes=H * DV * DK * 4, phases=phases)


SGU_RB = 256
GELU_C = 0.7978845608028654
GELU_A = 0.044715


def _gelu(x):
    return 0.5 * x * (1.0 + jnp.tanh(GELU_C * (x + GELU_A * x * x * x)))


def _gelu_grad(x):
    t = jnp.tanh(GELU_C * (x + GELU_A * x * x * x))
    return 0.5 * (1.0 + t) + 0.5 * x * (1.0 - t * t) * (GELU_C * (1.0 + 3.0 * GELU_A * x * x))


def _causal_mask(transposed=False):
    i = lax.broadcasted_iota(jnp.int32, (SGU_BLOCK, SGU_BLOCK), 1 if transposed else 0)
    j = lax.broadcasted_iota(jnp.int32, (SGU_BLOCK, SGU_BLOCK), 0 if transposed else 1)
    return (i >= C) | (j < C)


def _layer_norm(vf, gain, bias):
    mu = jnp.mean(vf, axis=-1, keepdims=True)
    cen = vf - mu
    rstd = lax.rsqrt(jnp.mean(cen * cen, axis=-1, keepdims=True) + EPS)
    xhat = cen * rstd
    return xhat, rstd, xhat * gain + bias


def _sgu_fwd(proj, lng, lnb, ws, bsb, phases=()):
    def body(p_ref, g_ref, b_ref, ws_ref, bs_ref, o_ref):
        mask = _causal_mask()
        for n in range(SGU_RB // SGU_BLOCK):
            rows = slice(n * SGU_BLOCK, (n + 1) * SGU_BLOCK)
            _, _, vn = _layer_norm(_gelu(p_ref[rows, D:2 * D]), g_ref[...], b_ref[...])
            vn_b = vn.astype(BF16)
            for gi in range(SGU_G):
                cols = slice(gi * SGU_GD, (gi + 1) * SGU_GD)
                w = jnp.where(mask, ws_ref[gi], 0.0).astype(BF16)
                vs = _dot(w, vn_b[:, cols]) + bs_ref[gi]
                gate = p_ref[rows, 2 * D + gi * SGU_GD:2 * D + (gi + 1) * SGU_GD]
                o_ref[rows, cols] = (_gelu(p_ref[rows, cols]) * vs * (gate * _sigmoid(gate))).astype(BF16)

    blocks = SGU_RB * SGU_COLS * 4 + SGU_RB * D * 2 + SGU_G * SGU_BLOCK * (SGU_BLOCK + SGU_GD) * 4
    return _hosted(
        body, name="sgu_fwd", grid=(S // SGU_RB,),
        in_specs=[pl.BlockSpec((SGU_RB, SGU_COLS), lambda i: (i, 0)),
                  pl.BlockSpec((1, D), lambda i: (0, 0)), pl.BlockSpec((1, D), lambda i: (0, 0)),
                  pl.BlockSpec((SGU_G, SGU_BLOCK, SGU_BLOCK), lambda i: (0, 0, 0)),
                  pl.BlockSpec((SGU_G, SGU_BLOCK, SGU_GD), lambda i: (0, 0, 0))],
        out_specs=[pl.BlockSpec((SGU_RB, D), lambda i: (i, 0))], out_shape=[_sds((S, D), BF16)],
        args=[proj, lng, lnb, ws, bsb], block_bytes=blocks, phases=phases)


def _sgu_bwd(proj, dpre, lng, lnb, ws, wst, bsb, phases=()):
    nsteps = S // SGU_RB

    def body(p_ref, d_ref, g_ref, b_ref, ws_ref, wst_ref, bs_ref,
             dp_ref, dg_ref, db_ref, dws_ref, dbs_ref, dvn_ref, dvs_acc_ref):
        step = pl.program_id(0)

        @pl.when(step == 0)
        def _():
            dg_ref[...] = jnp.zeros_like(dg_ref)
            db_ref[...] = jnp.zeros_like(db_ref)
            dws_ref[...] = jnp.zeros_like(dws_ref)
            dvs_acc_ref[...] = jnp.zeros_like(dvs_acc_ref)

        mask = _causal_mask()
        maskt = _causal_mask(transposed=True)
        for n in range(SGU_RB // SGU_BLOCK):
            rows = slice(n * SGU_BLOCK, (n + 1) * SGU_BLOCK)
            v = p_ref[rows, D:2 * D]
            xhat, rstd, vn = _layer_norm(_gelu(v), g_ref[...], b_ref[...])
            vn_b = vn.astype(BF16)
            for gi in range(SGU_G):
                cols = slice(gi * SGU_GD, (gi + 1) * SGU_GD)
                w = jnp.where(mask, ws_ref[gi], 0.0).astype(BF16)
                wt = jnp.where(maskt, wst_ref[gi], 0.0).astype(BF16)
                vs = _dot(w, vn_b[:, cols]) + bs_ref[gi]
                u = p_ref[rows, cols]
                gate = p_ref[rows, 2 * D + gi * SGU_GD:2 * D + (gi + 1) * SGU_GD]
                sg = _sigmoid(gate)
                gu = _gelu(u)
                dpre_g = d_ref[rows, cols]
                t = dpre_g * (gate * sg)
                dp_ref[rows, cols] = (t * vs * _gelu_grad(u)).astype(BF16)
                dp_ref[rows, 2 * D + gi * SGU_GD:2 * D + (gi + 1) * SGU_GD] = (
                    dpre_g * gu * vs * (sg * (1.0 + gate * (1.0 - sg)))).astype(BF16)
                dvs = t * gu
                dvs_b = dvs.astype(BF16)
                dvs_acc_ref[:, cols] += dvs
                dws_ref[gi] += _dot(dvs_b, vn_b[:, cols], NT)
                dvn_ref[:, cols] = _dot(wt, dvs_b)
            dvn = dvn_ref[...]
            dg_ref[...] += _colsum(dvn * xhat)
            db_ref[...] += _colsum(dvn)
            dxh = dvn * g_ref[...]
            dvf = rstd * (dxh - jnp.mean(dxh, axis=-1, keepdims=True) - xhat * jnp.mean(dxh * xhat, axis=-1, keepdims=True))
            dp_ref[rows, D:2 * D] = (dvf * _gelu_grad(v)).astype(BF16)

        @pl.when(step == nsteps - 1)
        def _():
            lane = lax.broadcasted_iota(jnp.int32, (SGU_BLOCK, SGU_BLOCK), 1)
            out = jnp.zeros((SGU_BLOCK, SGU_BLOCK), F32)
            for gi in range(SGU_G):
                out = out + jnp.where(lane == gi, jnp.sum(dvs_acc_ref[:, gi * SGU_GD:(gi + 1) * SGU_GD], axis=1, keepdims=True), 0.0)
                dws_ref[gi] = jnp.where(mask, dws_ref[gi], 0.0)
            dbs_ref[...] = out

    blocks = SGU_RB * SGU_COLS * 6 + SGU_RB * D * 4 + SGU_G * SGU_BLOCK * (3 * SGU_BLOCK + SGU_GD) * 4
    const3 = lambda i: (0, 0, 0)
    return _hosted(
        body, name="sgu_bwd", grid=(nsteps,),
        in_specs=[pl.BlockSpec((SGU_RB, SGU_COLS), lambda i: (i, 0)),
                  pl.BlockSpec((SGU_RB, D), lambda i: (i, 0)),
                  pl.BlockSpec((1, D), lambda i: (0, 0)), pl.BlockSpec((1, D), lambda i: (0, 0)),
                  pl.BlockSpec((SGU_G, SGU_BLOCK, SGU_BLOCK), const3),
                  pl.BlockSpec((SGU_G, SGU_BLOCK, SGU_BLOCK), const3),
                  pl.BlockSpec((SGU_G, SGU_BLOCK, SGU_GD), const3)],
        out_specs=[pl.BlockSpec((SGU_RB, SGU_COLS), lambda i: (i, 0)),
                   pl.BlockSpec((1, D), lambda i: (0, 0)), pl.BlockSpec((1, D), lambda i: (0, 0)),
                   pl.BlockSpec((SGU_G, SGU_BLOCK, SGU_BLOCK), const3),
                   pl.BlockSpec((SGU_BLOCK, SGU_BLOCK), lambda i: (0, 0))],
        out_shape=[_sds((S, SGU_COLS), BF16), _sds((1, D), F32), _sds((1, D), F32),
                   _sds((SGU_G, SGU_BLOCK, SGU_BLOCK), F32), _sds((SGU_BLOCK, SGU_BLOCK), F32)],
        args=[proj, dpre, lng, lnb, ws, wst, bsb],
        scratch_shapes=[pltpu.VMEM((SGU_BLOCK, D), F32), pltpu.VMEM((SGU_BLOCK, D), F32)],
        block_bytes=blocks, scratch_bytes=2 * SGU_BLOCK * D * 4, phases=phases)


def _pair_sum(own, a, r0, nr, name, table=None):
    c = own.shape[2]
    tr = 256
    assert r0 % tr == 0 and nr % tr == 0

    def body(own_ref, sib_ref, o_ref):
        o_ref[...] = (own_ref[...].astype(F32) + sib_ref[...].astype(F32)).astype(BF16)

    own_map = ((lambda j, i: (1 + j, r0 // tr + i, 0)) if table is None else
               (lambda j, i, t: (t[1 + j], r0 // tr + i, 0)))
    cpad = -(-c // 128) * 128
    outs, _ = _hosted(
        body, name=name, grid=(3, nr // tr),
        in_specs=[pl.BlockSpec((None, tr, c), own_map),
                  pl.BlockSpec((None, tr, c), lambda j, i, *t: (1 + j, r0 // tr + i, 0))],
        out_specs=[pl.BlockSpec((None, tr, c), lambda j, i, *t: (j, i, 0))], out_shape=[_sds((3, nr, c), BF16)],
        args=[own, a], block_bytes=3 * tr * cpad * 2, table=table)
    return outs[0]


def _adamw_math(w, g, m, v):
    m = ADAM_B1 * m + (1.0 - ADAM_B1) * g
    v = ADAM_B2 * v + (1.0 - ADAM_B2) * (g * g)
    m_hat = m / (1.0 - ADAM_B1 ** ADAM_STEP)
    v_hat = v / (1.0 - ADAM_B2 ** ADAM_STEP)
    delta = -ADAM_LR * (m_hat / (jnp.sqrt(v_hat) + ADAM_EPS) + ADAM_WD * w)
    return delta, m, v


def _sum_adamw(own, a, b, w, m, v, *, name, phases=(), table=None):
    r, c = w.shape
    tr = 256

    def body(own_ref, sib_ref, far_ref, w_ref, m_ref, v_ref, g_ref, d_ref, nm_ref, nv_ref):
        g = own_ref[...].astype(F32) + sib_ref[...].astype(F32)
        for j in range(3):
            g = g + far_ref[j].astype(F32)
        g_ref[...] = g
        d_ref[...], nm_ref[...], nv_ref[...] = _adamw_math(w_ref[...], g, m_ref[...], v_ref[...])

    spec = pl.BlockSpec((tr, c), lambda i, *t: (i, 0))
    own_map = (lambda i: (0, i, 0)) if table is None else (lambda i, t: (t[0], i, 0))
    cpad = -(-c // 128) * 128
    return _hosted(
        body, name=name, grid=(r // tr,),
        in_specs=[pl.BlockSpec((None, tr, c), own_map), pl.BlockSpec((None, tr, c), lambda i, *t: (0, i, 0)),
                  pl.BlockSpec((3, tr, c), lambda i, *t: (0, i, 0)), spec, spec, spec],
        out_specs=[spec] * 4, out_shape=[_sds((r, c), F32)] * 4, args=[own, a, b, w, m, v],
        block_bytes=5 * tr * cpad * 2 + 7 * tr * cpad * 4, phases=phases, table=table)


def _sum_parts(parts, name):
    n, r, c = parts.shape

    def body(p_ref, o_ref):
        g = p_ref[0]
        for j in range(1, n):
            g = g + p_ref[j]
        o_ref[...] = g

    outs, _ = _hosted(body, name=name, grid=(1,), in_specs=[pl.BlockSpec((n, r, c), lambda i: (0, 0, 0))],
                      out_specs=[pl.BlockSpec((r, c), lambda i: (0, 0))], out_shape=[_sds((r, c), F32)], args=[parts],
                      block_bytes=(n + 1) * r * c * 4)
    return outs[0]


def _adamw(w, g, m, v, name):
    def body(w_ref, g_ref, m_ref, v_ref, d_ref, nm_ref, nv_ref):
        d_ref[...], nm_ref[...], nv_ref[...] = _adamw_math(w_ref[...], g_ref[...], m_ref[...], v_ref[...])

    spec = pl.BlockSpec(w.shape, lambda i: (0, 0))
    outs, _ = _hosted(body, name=name, grid=(1,), in_specs=[spec] * 4, out_specs=[spec] * 3, out_shape=[_sds(w.shape, F32)] * 3,
                      args=[w, g, m, v], block_bytes=7 * _nbytes(w.shape, F32))
    return outs


def _blocks_to_columns(g):
    n, r, c = g.shape
    return jnp.transpose(g, (1, 0, 2)).reshape(r, n * c)


def _pack(parts):
    return jnp.concatenate([p.reshape(-1) for p in parts]).reshape(-1, 128)


def _unpack(packed, like):
    flat, outs, off = packed.reshape(-1), [], 0
    for p in like:
        outs.append(flat[off:off + p.size].reshape(p.shape))
        off += p.size
    return outs


def kernel(x, norm_pre, norm_post, gla_w_in, gla_w_gate2, gla_b_gate, gla_o_gain, gla_w_out, sgu_w_in, sgu_ln_gain, sgu_ln_bias, sgu_w_spatial, sgu_b_spatial, sgu_w_out, loss_target, m_norm_pre, m_norm_post, m_gla_w_in, m_gla_w_gate2, m_gla_b_gate, m_gla_o_gain, m_gla_w_out, m_sgu_w_in, m_sgu_ln_gain, m_sgu_ln_bias, m_sgu_w_spatial, m_sgu_b_spatial, m_sgu_w_out, v_norm_pre, v_norm_post, v_gla_w_in, v_gla_w_gate2, v_gla_b_gate, v_gla_o_gain, v_gla_w_out, v_sgu_w_in, v_sgu_ln_gain, v_sgu_ln_bias, v_sgu_w_spatial, v_sgu_b_spatial, v_sgu_w_out):
    me = _index_of(*_place())
    x0 = x.reshape(S, D)
    tgt = loss_target.reshape(S, D)
    npre0, npre1 = norm_pre[0:1], norm_pre[1:2]
    npost0, npost1 = norm_post[0:1], norm_post[1:2]
    ws = sgu_w_spatial[0]
    wst = jnp.transpose(ws, (0, 2, 1))
    bsb = jnp.broadcast_to(sgu_b_spatial[0][:, :, None], (SGU_G, SGU_BLOCK, SGU_GD))
    W_ROWS = D // N_DEV
    IN_COLS_G, IN_COLS_S = GLA_COLS // N_DEV, SGU_COLS // N_DEV

    s_gwi, s_gwo = gla_w_in[0].astype(BF16), gla_w_out[0].astype(BF16)
    s_swi, s_swo = sgu_w_in[0].astype(BF16), sgu_w_out[0].astype(BF16)
    small = jnp.concatenate([jnp.pad(gla_w_gate2[0].reshape(4, 512), ((0, 4), (0, 0))),
                             jnp.pad(jnp.concatenate([sgu_ln_gain, sgu_ln_bias], axis=1), ((0, 7), (0, 0)))], axis=0)

    wg_in, g_small = _gather_first(s_gwi, small, "gather_first")
    w2 =_blocks_to_columns(g_small[:, :4, :].reshape(N_DEV, LR, 128))
    w2p = jnp.pad(w2, ((0, LRP - LR), (0, 0))).astype(BF16)
    lng = g_small[:, 8, :256].reshape(1, D)
    lnb = g_small[:, 8, 256:].reshape(1, D)
    like_gwo, like_swi = _sds((N_DEV, W_ROWS, D), BF16), _sds((N_DEV, D, IN_COLS_S), BF16)

    h0 = _prenorm(x0, npre0)
    proj0, (g_gwo, g_swi) = _mm(h0, wg_in, "nn", F32, tm=1024, tn=896, tk=D, name="gla_in", b_tiled=True, phases=[
        _Phase(like_gwo, None, [_gather_send(s_gwo, 0, W_ROWS)]),
        _Phase(like_swi, None, [_gather_send(s_swi, 0, 768, diagonal=False)])])
    (ypre0, states), (g_gwo, g_swi) = _gla_fwd(proj0, w2p, gla_b_gate, gla_o_gain, phases=[
        _Phase(like_gwo, g_gwo, [_gather_pass(0, W_ROWS)]),
        _Phase(like_swi, g_swi, [_gather_relay(0, 768), _gather_send(s_swi, 768, 512, diagonal=False)])])
    wg_out = g_gwo.reshape(D, D)
    y0, (g_swi,) = _mm(ypre0, wg_out, "nn", F32, tm=1024, tn=1024, tk=D, name="gla_out", phases=[
        _Phase(like_swi, g_swi, [_gather_pass(0, 768), _gather_relay(768, 512), _gather_send(s_swi, 1280, 512, diagonal=False)])])
    (x1, h1), (g_swi,) = _mid_fwd(x0, y0, npost0, npre1, phases=[
        _Phase(like_swi, g_swi, [_gather_pass(768, 512), _gather_relay(1280, 512), _gather_send(s_swi, 1792, 256, diagonal=False)])])
    g_swi, = _carry([_Phase(like_swi, g_swi, [_gather_pass(1280, 512), _gather_relay(1792, 256)])], "relay_sgu_w_in")
    g_swi, = _carry([_Phase(like_swi, g_swi, [_gather_pass(1792, 256)])], "pass_sgu_w_in")
    proj1, (g_swo,) = _mm(h1, g_swi, "nn", F32, tm=1024, tn=IN_COLS_S, tk=D, name="sgu_in", b_blocked=True, phases=[
        _Phase(like_gwo, None, [_gather_send(s_swo, 0, W_ROWS)])])
    (pre1,), (g_swo,) = _sgu_fwd(proj1, lng, lnb, ws, bsb, phases=[_Phase(like_gwo, g_swo, [_gather_pass(0, W_ROWS)])])
    ws_out = g_swo.reshape(D, D)
    loss_cols, dx2, dy1, dnpost1 = _nn_rows(pre1, ws_out, name="sgu_out", row_ins=[x1, tgt], vec_ins=[npost1],
                                            outs=[("vec", F32), ("row", F32), ("row", BF16), ("vec", F32)], tail=_final_tail)
    loss_here = jnp.pad((0.5 * jnp.sum(loss_cols) / D).reshape(1, 1), ((0, 7), (0, 127)))

    like_b_out, like_b_swi = _sds((3, W_ROWS, D), BF16), _sds((3, D, IN_COLS_S), BF16)
    like_b_gwi = _sds((3, D, IN_COLS_G), BF16)
    row_pair = dict(like=_sds((4, W_ROWS, D), BF16), block=lambda i, j: i, ordinal=lambda i, j: i >> 1,
                    dst=lambda ref, k, i, j: ref.at[k])
    col_pair = dict(like=_sds((4, D, IN_COLS_S), BF16), block=lambda i, j: j, ordinal=lambda i, j: 4 * i + (j >> 1),
                    dst=lambda ref, k, i, j: ref.at[k, pl.ds(pl.multiple_of(i * 1024, 1024), 1024)])

    mine = _own_table()
    dws_out, (a_swo,) = _mm(pre1, dy1, "tn", BF16, tm=W_ROWS, tn=D, tk=S, name="sgu_out_dw", pair=row_pair)
    p_swo = dws_out.reshape(N_DEV, W_ROWS, D)
    t_swo = _pair_sum(p_swo, a_swo, 0, W_ROWS, "pair_sum_sgu_w_out", table=mine)
    dpre1, _ = _mm(dy1, ws_out, "nt", F32, tm=1024, tn=1024, tk=D, name="sgu_out_dx")
    (dproj1, dlng, dlnb, dwsp, dbsp), (b_swo,) = _sgu_bwd(proj1, dpre1, lng, lnb, ws, wst, bsb, phases=[
        _Phase(like_b_out, None, [_reduce_cross(t_swo, 0, 0, W_ROWS)])])
    p_swi, (a_swi,) = _mm(h1, dproj1, "tn", BF16, tm=1024, tn=IN_COLS_S, tk=S, name="sgu_in_dw", out_blocked=True, pair=col_pair)
    t_swi = _pair_sum(p_swi, a_swi, 0, D, "pair_sum_sgu_w_in", table=mine)
    (dx1, dy0, dnpre1, dnpost0), (b_swi,) = _nt_rows(
        dproj1, g_swi, tk=IN_COLS_S, name="sgu_in_dx", b_blocked=True, row_ins=[dx2, x1, y0], vec_ins=[npre1, npost0],
        outs=[("row", F32), ("row", BF16), ("vec", F32), ("vec", F32)], tail=_mid_bwd_tail, phases=[
            _Phase(like_b_swi, None, [_reduce_cross(t_swi, 0, 0, 1024)])])
    dwg_out, (a_gwo, b_swi) = _mm(ypre0, dy0, "tn", BF16, tm=W_ROWS, tn=D, tk=S, name="gla_out_dw", pair=row_pair, phases=[
        _Phase(like_b_swi, b_swi, [_reduce_cross(t_swi, 1024, 1024, 256)])])
    p_gwo = dwg_out.reshape(N_DEV, W_ROWS, D)
    t_gwo = _pair_sum(p_gwo, a_gwo, 0, W_ROWS, "pair_sum_gla_w_out", table=mine)
    dypre0, _ = _mm(dy0, wg_out, "nt", F32, tm=1024, tn=1024, tk=D, name="gla_out_dx")
    late = [dnpre1, dnpost1, dlng, dlnb, dwsp, jnp.transpose(dbsp[:, :SGU_G])]
    late_pack = _pack(late)
    (dproj0, dogain, dbgate, dw2), (b_swi, b_gwo, g_late) = _gla_bwd(proj0, dypre0, states, w2p, gla_b_gate, gla_o_gain, phases=[
        _Phase(like_b_swi, b_swi, [_reduce_cross(t_swi, 1280, 1280, 768)]),
        _Phase(like_b_out, None, [_reduce_cross(t_gwo, 0, 0, W_ROWS)]),
        _Phase(_sds((N_DEV,) + late_pack.shape, F32), None, [_gather_send(late_pack, 0, late_pack.shape[0])])])
    half = D // 2
    (own_gwi, a_gwi), (g_late,) = _dw_blocks(h0, dproj0, None, None, 0, half, "gla_in_dw_a", phases=[
        _Phase(_sds((N_DEV,) + late_pack.shape, F32), g_late, [_gather_pass(0, late_pack.shape[0])])])
    t_gwi_a = _pair_sum(own_gwi, a_gwi, 0, half, "pair_sum_gla_w_in_a")
    (own_gwi, a_gwi), (b_gwi,) = _dw_blocks(h0, dproj0, own_gwi, a_gwi, half, half, "gla_in_dw_b", phases=[
        _Phase(like_b_gwi, None, [_reduce_cross(t_gwi_a, 0, 0, 640)])])
    t_gwi_b = _pair_sum(own_gwi, a_gwi, half, half, "pair_sum_gla_w_in_b")
    (grad_x, dnpre0), (b_gwi,) = _nt_rows(
        dproj0, wg_in, tk=896, name="gla_in_dx", b_tiled=True, row_ins=[dx1, x0], vec_ins=[npre0],
        outs=[("row", F32), ("vec", F32)], tail=_first_bwd_tail, phases=[
            _Phase(like_b_gwi, b_gwi, [_reduce_cross(t_gwi_a, 640, 640, 384), _reduce_cross(t_gwi_b, 0, half, half)])])

    early = [dnpre0, dnpost0, dbgate, dogain, dw2[:LR], loss_here]
    early_pack = _pack(early)
    like_early = _sds((N_DEV,) + early_pack.shape, F32)
    (g_swo, d_swo, nm_swo, nv_swo), (g_early,) = _sum_adamw(
        p_swo, a_swo, b_swo, sgu_w_out[0], m_sgu_w_out[0], v_sgu_w_out[0], name="adamw_sgu_w_out", table=mine, phases=[
            _Phase(like_early, None, [_gather_send(early_pack, 0, early_pack.shape[0])])])
    (g_gwo_, d_gwo, nm_gwo, nv_gwo), (g_early,) = _sum_adamw(
        p_gwo, a_gwo, b_gwo, gla_w_out[0], m_gla_w_out[0], v_gla_w_out[0], name="adamw_gla_w_out", table=mine, phases=[
            _Phase(like_early, g_early, [_gather_pass(0, early_pack.shape[0])])])
    (g_swi_, d_swi, nm_swi, nv_swi), _ = _sum_adamw(
        p_swi, a_swi, b_swi, sgu_w_in[0], m_sgu_w_in[0], v_sgu_w_in[0], name="adamw_sgu_w_in", table=mine)
    (g_gwi_, d_gwi, nm_gwi, nv_gwi), _ = _sum_adamw(
        own_gwi, a_gwi, b_gwi, gla_w_in[0], m_gla_w_in[0], v_gla_w_in[0], name="adamw_gla_w_in")

    g_npre1, g_npost1, g_lng_full, g_lnb_full, g_wsp, g_bsp = _unpack(_sum_parts(g_late, "sum_late_small_grads"), late)
    g_npre0, g_npost0, g_bgate, g_ogain, g_w2_full, loss_all = _unpack(_sum_parts(g_early, "sum_early_small_grads"), early)
    loss = loss_all[0, 0]
    g_w2 = lax.dynamic_slice(g_w2_full, (0, me * 128), (LR, 128))
    g_lng = lax.dynamic_slice(g_lng_full, (0, me * 256), (1, 256))
    g_lnb = lax.dynamic_slice(g_lnb_full, (0, me * 256), (1, 256))
    small_g = [jnp.concatenate([g_npre0, g_npre1], 0), jnp.concatenate([g_npost0, g_npost1], 0), g_w2, g_bgate, g_ogain,
               g_lng, g_lnb, g_wsp, g_bsp]
    small_w = [norm_pre, norm_post, gla_w_gate2[0], gla_b_gate, gla_o_gain, sgu_ln_gain, sgu_ln_bias, sgu_w_spatial[0], sgu_b_spatial[0]]
    small_m = [m_norm_pre, m_norm_post, m_gla_w_gate2[0], m_gla_b_gate, m_gla_o_gain, m_sgu_ln_gain, m_sgu_ln_bias, m_sgu_w_spatial[0], m_sgu_b_spatial[0]]
    small_v = [v_norm_pre, v_norm_post, v_gla_w_gate2[0], v_gla_b_gate, v_gla_o_gain, v_sgu_ln_gain, v_sgu_ln_bias, v_sgu_w_spatial[0], v_sgu_b_spatial[0]]
    d_pack, nm_pack, nv_pack = _adamw(_pack(small_w), _pack(small_g), _pack(small_m), _pack(small_v), "adamw_small")

    out_like = [norm_pre, norm_post, gla_w_gate2, gla_b_gate, gla_o_gain, sgu_ln_gain, sgu_ln_bias, sgu_w_spatial, sgu_b_spatial]
    sg_ = [g.reshape(s.shape) for g, s in zip(small_g, out_like)]
    sd_, sm_, sv_ = (_unpack(pk, out_like) for pk in (d_pack, nm_pack, nv_pack))

    def assemble(small_list, w_in_g, w_out_g, w_in_s, w_out_s):
        npre_, npost_, w2_, bg_, og_, lg_, lb_, wsp_, bsp_ = small_list
        return [npre_, npost_, w_in_g[None], w2_, bg_, og_, w_out_g[None], w_in_s[None], lg_, lb_, wsp_, bsp_, w_out_s[None]]

    return (loss, grad_x.reshape(1, S, D),
            *assemble(sg_, g_gwi_, g_gwo_, g_swi_, g_swo),
            *assemble(sd_, d_gwi, d_gwo, d_swi, d_swo),
            *assemble(sm_, nm_gwi, nm_gwo, nm_swi, nm_swo),
            *assemble(sv_, nv_gwi, nv_gwo, nv_swi, nv_swo))
```

```python
import functools

import jax
import jax.numpy as jnp
from jax import lax
from jax.experimental import pallas as pl
from jax.experimental.pallas import tpu as pltpu

F32 = jnp.float32
BF16 = jnp.bfloat16

N_DEV = 8
S = 2048
D = 2048
H = 4
DK = 256
DV = 512
C = 64
NC = S // C
GLA_COLS = 6160
GLA_PAD = 6272
Q0, K0, V0, G0, LR0 = 0, 1024, 2048, 4096, 6144
LR = 16
LRP = 128
SGU_COLS = 6144
SGU_BLOCK = 128
SGU_G = 8
SGU_GD = 256
EPS = 1e-6
GLA_TAU = 16.0

ADAM_LR, ADAM_B1, ADAM_B2, ADAM_EPS, ADAM_WD, ADAM_STEP = 0.001, 0.9, 0.999, 1e-08, 0.01, 10

V7X_VMEM_BYTES = 64 * 1024 * 1024
VMEM_CEILING = V7X_VMEM_BYTES - 6 * 1024 * 1024
MESH = pl.DeviceIdType.MESH
HBM_SPEC = pl.BlockSpec(memory_space=pl.ANY)


def _sds(shape, dtype):
    return jax.ShapeDtypeStruct(tuple(shape), dtype)


def _nbytes(shape, dtype):
    n = 1
    for s in shape:
        n *= s
    return n * jnp.dtype(dtype).itemsize


def _dot(a, b, dims=(((1,), (0,)), ((), ())), precision=None):
    return lax.dot_general(a, b, dims, precision=precision, preferred_element_type=F32)


NN = (((1,), (0,)), ((), ()))
TN = (((0,), (0,)), ((), ()))
NT = (((1,), (1,)), ((), ()))


def _place():
    return lax.axis_index("x"), lax.axis_index("y"), lax.axis_index("c")


def _index_of(px, py, pc):
    return 4 * px + 2 * py + pc


def _chips(x, y):
    return [(1 - x, y), (x, 1 - y), (1 - x, 1 - y)]


def _rcopy(src, dst, send_sem, recv_sem, to):
    return pltpu.make_async_remote_copy(src_ref=src, dst_ref=dst, send_sem=send_sem, recv_sem=recv_sem,
                                        device_id=to, device_id_type=MESH)


class _Move:
    def __init__(self, ins, n_remote, make, stage=None):
        self.ins, self.n_remote, self.make, self.stage = list(ins), n_remote, make, stage

    def scratch(self):
        sems = [pltpu.SemaphoreType.DMA((self.n_remote,)), pltpu.SemaphoreType.DMA((self.n_remote,))]
        return sems if self.stage is None else sems + [pltpu.SemaphoreType.DMA((1,)), pltpu.VMEM(*self.stage)]

    def start(self, in_refs, buf, scratch):
        sends, _, local = self.make(in_refs, buf, scratch[0], scratch[1])
        if local is not None:
            pltpu.make_async_copy(local[0], scratch[3], scratch[2].at[0]).start()
        for cp in sends:
            cp.start()

    def finish(self, in_refs, buf, scratch):
        sends, arrivals, local = self.make(in_refs, buf, scratch[0], scratch[1])
        if local is not None:
            pltpu.make_async_copy(local[0], scratch[3], scratch[2].at[0]).wait()
            out = pltpu.make_async_copy(scratch[3], local[1], scratch[2].at[0])
            out.start()
        for cp in arrivals:
            cp.wait_recv()
        for cp in sends:
            cp.wait_send()
        if local is not None:
            out.wait()


class _Phase:
    def __init__(self, like, so_far, moves):
        self.like, self.so_far, self.moves = like, so_far, list(moves)


def _gather_send(shard, r0, nr, diagonal=True):
    def make(in_refs, g, ss, rs):
        sh, = in_refs
        x, y, c = _place()
        me = _index_of(x, y, c)
        rows = pl.ds(r0, nr)
        peers = [(x, y, 1 - c)] + [(px, py, c) for px, py in _chips(x, y)[:3 if diagonal else 2]]
        sends = [_rcopy(sh.at[rows], g.at[me, rows], ss.at[k], rs.at[k], p) for k, p in enumerate(peers)]
        arrivals = [_rcopy(sh.at[rows], g.at[_index_of(*p), rows], ss.at[k], rs.at[k], p) for k, p in enumerate(peers)]
        return sends, arrivals, (sh.at[rows], g.at[me, rows])

    return _Move([shard], 4 if diagonal else 3, make, stage=((nr, shard.shape[1]), shard.dtype))


def _gather_relay(r0, nr):
    def make(in_refs, g, ss, rs):
        x, y, c = _place()
        nx, ny, nd = [(px, py, c) for px, py in _chips(x, y)]
        first, second = pl.ds(r0, nr // 2), pl.ds(r0 + nr // 2, nr // 2)
        sends = [_rcopy(g.at[_index_of(*nx), first], g.at[_index_of(*nx), first], ss.at[0], rs.at[0], ny),
                 _rcopy(g.at[_index_of(*ny), second], g.at[_index_of(*ny), second], ss.at[1], rs.at[1], nx)]
        arrivals = [_rcopy(g.at[_index_of(*nx), first], g.at[_index_of(*nd), first], ss.at[0], rs.at[0], ny),
                    _rcopy(g.at[_index_of(*ny), second], g.at[_index_of(*nd), second], ss.at[1], rs.at[1], nx)]
        return sends, arrivals, None

    return _Move([], 2, make)


def _gather_pass(r0, nr):
    def make(in_refs, g, ss, rs):
        x, y, c = _place()
        rows = pl.ds(r0, nr)
        sends = [_rcopy(g.at[_index_of(px, py, c), rows], g.at[_index_of(px, py, c), rows], ss.at[j], rs.at[j], (x, y, 1 - c))
                 for j, (px, py) in enumerate(_chips(x, y))]
        arrivals = [_rcopy(g.at[_index_of(px, py, c), rows], g.at[_index_of(px, py, 1 - c), rows], ss.at[j], rs.at[j], (x, y, 1 - c))
                    for j, (px, py) in enumerate(_chips(x, y))]
        return sends, arrivals, None

    return _Move([], 3, make)


def _own_table():
    x, y, c = _place()
    return jnp.stack([_index_of(px, py, c) for px, py in [(x, y)] + _chips(x, y)]).astype(jnp.int32)


def _reduce_cross(sums, src_r0, dst_r0, nr):
    def make(in_refs, b, ss, rs):
        t, = in_refs
        x, y, c = _place()
        src, dst = pl.ds(src_r0, nr), pl.ds(dst_r0, nr)
        sends = [_rcopy(t.at[j, src], b.at[j, dst], ss.at[j], rs.at[j], (px, py, c)) for j, (px, py) in enumerate(_chips(x, y))]
        return sends, sends, None

    return _Move([sums], 3, make)


def _hosted(body, *, name, grid, in_specs, out_specs, out_shape, args, scratch_shapes=(), block_bytes, scratch_bytes=0,
            phases=(), table=None, continued=None):
    n_in, n_out, n_scr = len(args), len(out_shape), len(scratch_shapes)
    all_args, all_out_shape, sems, aliases, layout = list(args), list(out_shape), [], dict(continued or {}), []
    for j, ph in enumerate(phases):
        counts = []
        for mv in ph.moves:
            all_args += mv.ins
            counts.append(len(mv.ins))
            sems += mv.scratch()
        if ph.so_far is not None:
            aliases[len(all_args)] = n_out + j
            all_args.append(ph.so_far)
        layout.append((counts, ph.so_far is not None))
        all_out_shape.append(ph.like)
    n_extra_in = len(all_args) - n_in

    def wrapped(*refs):
        ins, pos = refs[:n_in], n_in
        move_ins = []
        for counts, continues in layout:
            per_move = []
            for cnt in counts:
                per_move.append(refs[pos:pos + cnt])
                pos += cnt
            pos += continues
            move_ins.append(per_move)
        outs = refs[pos:pos + n_out]
        bufs = refs[pos + n_out:pos + n_out + len(phases)]
        pos += n_out + len(phases)
        scratch = refs[pos:pos + n_scr]
        pos += n_scr
        move_sems = []
        for ph in phases:
            per_move = []
            for mv in ph.moves:
                count = len(mv.scratch())
                per_move.append(refs[pos:pos + count])
                pos += count
            move_sems.append(per_move)

        def each_move(fn_name):
            for ph, buf, per_in, per_sem in zip(phases, bufs, move_ins, move_sems):
                for mv, mv_in, mv_sem in zip(ph.moves, per_in, per_sem):
                    getattr(mv, fn_name)(mv_in, buf, mv_sem)

        if phases:
            first = functools.reduce(jnp.logical_and, [pl.program_id(a) == 0 for a in range(len(grid))])
            last = functools.reduce(jnp.logical_and, [pl.program_id(a) == grid[a] - 1 for a in range(len(grid))])
            pl.when(first)(lambda: each_move("start"))
        body(*ins, *outs, *scratch)
        if phases:
            pl.when(last)(lambda: each_move("finish"))

    all_args = [pltpu.with_memory_space_constraint(a, pltpu.HBM) for a in all_args]
    est = 2 * block_bytes + scratch_bytes
    params = pltpu.CompilerParams(dimension_semantics=("arbitrary",) * len(grid),
                                  vmem_limit_bytes=min(VMEM_CEILING, max(32 * 1024 * 1024, 2 * est)))
    all_in_specs, all_out_specs = list(in_specs) + [HBM_SPEC] * n_extra_in, list(out_specs) + [HBM_SPEC] * len(phases)
    if table is None:
        results = pl.pallas_call(
            wrapped, grid=grid, in_specs=all_in_specs, out_specs=all_out_specs, out_shape=all_out_shape,
            scratch_shapes=list(scratch_shapes) + sems, input_output_aliases=aliases, compiler_params=params, name=name,
        )(*all_args)
    else:
        results = pl.pallas_call(
            lambda table_ref, *refs: wrapped(*refs),
            grid_spec=pltpu.PrefetchScalarGridSpec(num_scalar_prefetch=1, grid=grid, in_specs=all_in_specs, out_specs=all_out_specs,
                                                   scratch_shapes=list(scratch_shapes) + sems),
            out_shape=all_out_shape, input_output_aliases={k + 1: v for k, v in aliases.items()}, compiler_params=params, name=name,
        )(table, *all_args)
    return list(results[:n_out]), list(results[n_out:])


class _Both:
    def __init__(self, copies):
        self.copies = copies

    def start(self):
        for cp in self.copies:
            cp.start()

    def wait_send(self):
        for cp in self.copies:
            cp.wait_send()

    def wait_recv(self):
        for cp in self.copies:
            cp.wait_recv()


def _carry(phases, name):
    def body(o_ref):
        o_ref[...] = jnp.zeros_like(o_ref)

    _, bufs = _hosted(body, name=name, grid=(1,), in_specs=[], out_specs=[pl.BlockSpec((8, 128), lambda i: (0, 0))],
                      out_shape=[_sds((8, 128), F32)], args=[], block_bytes=8 * 128 * 4, phases=phases)
    return bufs


def _gather_first(shard, small, name):
    cw, tr, n_tiles = shard.shape[1], 256, GLA_PAD // 128

    def body(sh_ref, sm_ref, wn_ref, g_ref, gs_ref, wt_ref, win_ref, tmp_ref, send_sems, recv_sems, local_sems):
        x, y, c = _place()
        me, sibling = (x, y, c), (x, y, 1 - c)
        chips = _chips(x, y)

        def copy(base, out_ref, k, block, to, src=None):
            dst = out_ref.at[_index_of(*block)]
            return _rcopy(dst if src is None else src, dst, send_sems.at[base + k], recv_sems.at[base + k], to)

        icopy = functools.partial(copy, 0, g_ref)
        scopy = functools.partial(copy, 8, gs_ref)

        def wcopy(k, block, to, src=None):
            if k in (1, 2):
                return icopy(k, block, to, src)
            halves = []
            for part, sem in enumerate((k, {0: 15, 4: 16, 5: 17, 6: 18}[k])):
                rows = pl.ds(part * (D // 2), D // 2)
                dst = g_ref.at[_index_of(*block), rows]
                halves.append(_rcopy(dst if src is None else src.at[rows], dst, send_sems.at[sem], recv_sems.at[sem], to))
            return _Both(halves)

        def relay(k, block, half, to):
            rows = pl.ds(half * (D // 2), D // 2)
            ref = g_ref.at[_index_of(*block), rows]
            return _rcopy(ref, ref, send_sems.at[k], recv_sems.at[k], to)

        def load(src_ref, slot):
            cp = pltpu.make_async_copy(src_ref, win_ref.at[slot], local_sems.at[0])
            cp.start()
            cp.wait()

        def place(slot, block):
            b = _index_of(*block)

            def rows_chunk(r, carry):
                rows = pl.ds(pl.multiple_of(r * tr, tr), tr)
                tmp_ref[:, :cw] = win_ref[slot, rows, :].astype(F32)
                shifted = pltpu.roll(tmp_ref[...], 2 * b, 1)
                for u in range(7):
                    wt_ref[6 * b + u, rows, :] = (wt_ref[6 * b + u, rows, :].astype(F32) + shifted[:, 128 * u:128 * (u + 1)]).astype(BF16)
                return carry

            lax.fori_loop(0, D // tr, rows_chunk, 0)

        small_own = pltpu.make_async_copy(sm_ref, gs_ref.at[_index_of(*me)], local_sems.at[1])
        small_own.start()
        first = [wcopy(1 + j, me, (*chip, c), src=sh_ref) for j, chip in enumerate(chips[:2])]
        first += [scopy(0, me, sibling, src=sm_ref)] + [scopy(1 + j, me, (*chip, c), src=sm_ref) for j, chip in enumerate(chips)]
        for cp in first:
            cp.start()

        def clear(t, carry):
            wt_ref[t] = jnp.zeros((D, 128), BF16)
            return carry

        lax.fori_loop(0, n_tiles, clear, 0)
        tmp_ref[...] = jnp.zeros_like(tmp_ref)

        def emit(t):
            pltpu.make_async_copy(wt_ref.at[t], wn_ref.at[t], local_sems.at[2]).start()

        def take(block, slot, arrivals=None, pass_on=None):
            for cp in arrivals or ():
                cp.wait_recv()
            load(sh_ref if arrivals is None else g_ref.at[_index_of(*block)], slot)
            if pass_on is not None:
                pass_on.start()
            place(slot, block)
            for u in range(1, 6):
                emit(6 * _index_of(*block) + u)

        near_x, near_y, far = [(*chip, c) for chip in chips]
        to_sibling = wcopy(0, me, sibling, src=win_ref.at[0])
        pass_x = wcopy(4, near_x, sibling, src=win_ref.at[1])
        pass_y = wcopy(5, near_y, sibling, src=win_ref.at[0])
        pass_d = wcopy(6, far, sibling, src=win_ref.at[0])
        relays = [relay(3, near_x, 0, near_y), relay(7, near_y, 1, near_x)]
        take(me, 0, pass_on=to_sibling)
        take(near_x, 1, [wcopy(1, near_x, me)], pass_x)
        relays[0].start()
        to_sibling.wait_send()
        take(near_y, 0, [wcopy(2, near_y, me)], pass_y)
        relays[1].start()
        small_passed = []
        for j, chip in enumerate(chips):
            scopy(1 + j, (*chip, c), me).wait_recv()
            cp = scopy(4 + j, (*chip, c), sibling)
            cp.start()
            small_passed.append(cp)
        pass_x.wait_send()
        take(sibling, 1, [wcopy(0, sibling, me)])
        pass_y.wait_send()
        take((*chips[0], 1 - c), 0, [wcopy(4, (*chips[0], 1 - c), me)])
        take((*chips[1], 1 - c), 1, [wcopy(5, (*chips[1], 1 - c), me)])
        take(far, 0, [relay(3, far, 0, near_y), relay(7, far, 1, near_x)], pass_d)
        take((*chips[2], 1 - c), 1, [wcopy(6, (*chips[2], 1 - c), me)])
        for t in range(0, n_tiles, 6):
            emit(t)
        scopy(0, sibling, me).wait_recv()
        for j, chip in enumerate(chips):
            scopy(4 + j, (*chip, 1 - c), me).wait_recv()
        for cp in first + small_passed + relays + [pass_d]:
            cp.wait_send()
        small_own.wait()
        pltpu.make_async_copy(wn_ref, wn_ref, local_sems.at[2]).wait()

    wn, _, gs = pl.pallas_call(
        body,
        in_specs=[HBM_SPEC] * 2, out_specs=[HBM_SPEC] * 3,
        out_shape=[_sds((n_tiles, D, 128), BF16), _sds((N_DEV,) + shard.shape, BF16), _sds((N_DEV,) + small.shape, small.dtype)],
        scratch_shapes=[pltpu.VMEM((n_tiles, D, 128), BF16), pltpu.VMEM((2, D, cw), BF16), pltpu.VMEM((tr, 7 * 128), F32),
                        pltpu.SemaphoreType.DMA((19,)), pltpu.SemaphoreType.DMA((19,)), pltpu.SemaphoreType.DMA((3,))],
        compiler_params=pltpu.CompilerParams(vmem_limit_bytes=48 * 1024 * 1024),
        name=name,
    )(shard, small)
    return wn, gs


def _mm(a, b, mode, out_dtype, *, tm, tn, tk, name, b_blocked=False, b_tiled=False, out_blocked=False, pair=None, phases=()):
    if mode == "nn":
        (m, k), dims = a.shape, NN
        a_blk, a_map = (tm, tk), (lambda i, j, kk: (i, kk))
        if b_blocked:
            assert b.shape[1] == k and b.shape[2] == tn and tk == k
            n = b.shape[0] * tn
            b_spec = pl.BlockSpec((None, tk, tn), lambda i, j, kk: (j, kk, 0))
        elif b_tiled:
            assert b.shape[1] == k and b.shape[2] == 128 and tn % 128 == 0
            n = b.shape[0] * 128
            b_spec = pl.BlockSpec((tn // 128, tk, 128), lambda i, j, kk: (j, kk, 0))
        else:
            assert b.shape[0] == k
            n = b.shape[1]
            b_spec = pl.BlockSpec((tk, tn), lambda i, j, kk: (kk, j))
    elif mode == "tn":
        (k, m), n, dims = a.shape, b.shape[1], TN
        assert b.shape[0] == k
        a_blk, a_map = (tk, tm), (lambda i, j, kk: (kk, i))
        b_spec = pl.BlockSpec((tk, tn), lambda i, j, kk: (kk, j))
    else:
        (m, k), dims = a.shape, NT
        a_blk, a_map = (tm, tk), (lambda i, j, kk: (i, kk))
        if b_blocked:
            assert b.shape[0] * b.shape[2] == k and b.shape[2] == tk
            n = b.shape[1]
            b_spec = pl.BlockSpec((None, tn, tk), lambda i, j, kk: (kk, j, 0))
        elif b_tiled:
            assert b.shape[0] * 128 == k and b.shape[2] == 128 and tk % 128 == 0
            n = b.shape[1]
            b_spec = pl.BlockSpec((tk // 128, tn, 128), lambda i, j, kk: (kk, j, 0))
        else:
            assert b.shape[1] == k
            n = b.shape[0]
            b_spec = pl.BlockSpec((tn, tk), lambda i, j, kk: (j, kk))
    assert m % tm == 0 and n % tn == 0 and k % tk == 0, (a.shape, b.shape, mode)
    nk = k // tk
    n_row_tiles = m // tm
    if out_blocked:
        out_shape, out_spec = _sds((n // tn, n_row_tiles * tm, tn), out_dtype), pl.BlockSpec((None, tm, tn), lambda i, j, kk: (j, i, 0))
    else:
        out_shape, out_spec = _sds((n_row_tiles * tm, n), out_dtype), pl.BlockSpec((tm, tn), lambda i, j, kk: (i, j))

    grid = (n_row_tiles, n // tn, nk)

    def body(a_ref, b_ref, o_ref, *rest):
        rhs = jnp.concatenate([b_ref[u] for u in range(b_ref.shape[0])], axis=1) if b_tiled else b_ref[...]
        p = _dot(a_ref[...], rhs, dims)
        if nk == 1:
            o_ref[...] = p.astype(out_dtype)
            if pair is not None:
                _send_to_sibling(p.astype(out_dtype), *rest)
        else:
            acc_ref, = rest
            kk = pl.program_id(2)

            @pl.when(kk == 0)
            def _():
                acc_ref[...] = p

            @pl.when(kk > 0)
            def _():
                acc_ref[...] += p

            @pl.when(kk == nk - 1)
            def _():
                o_ref[...] = acc_ref[...].astype(out_dtype)

    def _send_to_sibling(tile, pair_ref, stage_ref, send_sems, recv_sem):
        i, j = pl.program_id(0), pl.program_id(1)
        x, y, c = _place()
        blk = pair["block"](i, j)
        k = ((blk >> 2) ^ x) + 2 * (((blk >> 1) & 1) ^ y)
        ordinal = pair["ordinal"](i, j)

        def send(slot):
            return _rcopy(stage_ref.at[slot], pair["dst"](pair_ref, k, i, j), send_sems.at[slot], recv_sem.at[0], (x, y, 1 - c))

        @pl.when((blk & 1) != c)
        def _():
            slot = ordinal & 1

            @pl.when(ordinal >= 2)
            def _():
                send(slot).wait_send()

            stage_ref[slot] = tile
            send(slot).start()

        @pl.when((i == grid[0] - 1) & (j == grid[1] - 1))
        def _():
            send(0).wait_send()
            send(1).wait_send()
            _rcopy(pair_ref, pair_ref, send_sems.at[0], recv_sem.at[0], (x, y, 1 - c)).wait_recv()

    blocks = _nbytes(a_blk, a.dtype) + tk * tn * jnp.dtype(b.dtype).itemsize + _nbytes((tm, tn), out_dtype)
    out_specs, out_shapes, scratch = [out_spec], [out_shape], [] if nk == 1 else [pltpu.VMEM((tm, tn), F32)]
    scratch_bytes = _nbytes((tm, tn), F32) * (nk > 1)
    if pair is not None:
        assert nk == 1
        out_specs, out_shapes = out_specs + [HBM_SPEC], out_shapes + [pair["like"]]
        scratch = [pltpu.VMEM((2, tm, tn), out_dtype), pltpu.SemaphoreType.DMA((2,)), pltpu.SemaphoreType.DMA((1,))]
        scratch_bytes = 2 * _nbytes((tm, tn), out_dtype)
    outs, bufs = _hosted(
        body, name=name, grid=grid,
        in_specs=[pl.BlockSpec(a_blk, a_map), b_spec], out_specs=out_specs, out_shape=out_shapes, args=[a, b],
        scratch_shapes=scratch, block_bytes=blocks, scratch_bytes=scratch_bytes, phases=phases)
    return outs[0], outs[1:] + bufs


def _dw_blocks(a, g, own_so_far, pair_so_far, m0, rows, name, phases=()):
    cw, win = GLA_COLS // N_DEV, 896
    continues = own_so_far is not None

    def body(*refs):
        a_ref, g_hbm = refs[:2]
        own_ref, pair_ref, win_ref, stage_ref, win_sems, local_sems, send_sems, recv_sem = refs[-8:]
        j = pl.program_id(0)
        x, y, c = _place()
        dst_rows = pl.ds(m0, rows)

        def window(jj):
            src = g_hbm.at[:, pl.ds(pl.multiple_of(768 * jj, 128), win)]
            return pltpu.make_async_copy(src, win_ref.at[jj % 2], win_sems.at[jj % 2])

        pl.when(j == 0)(lambda: window(j).start())
        pl.when(j + 1 < N_DEV)(lambda: window(j + 1).start())
        window(j).wait()
        p = _dot(a_ref[...], win_ref[j % 2], TN)
        tile = pltpu.roll(p, lax.rem(win - 2 * j, win), 1)[:, :cw].astype(BF16)
        k = ((j >> 2) ^ x) + 2 * (((j >> 1) & 1) ^ y)

        def local(slot, kk):
            return pltpu.make_async_copy(stage_ref.at[slot], own_ref.at[kk, dst_rows], local_sems.at[slot])

        def send(slot, kk):
            return _rcopy(stage_ref.at[slot], pair_ref.at[kk, dst_rows], send_sems.at[slot], recv_sem.at[0], (x, y, 1 - c))

        slot = j & 1

        @pl.when(slot == c)
        def _():
            pl.when(j >= 2)(lambda: local(slot, k).wait())
            stage_ref[slot] = tile
            local(slot, k).start()

        @pl.when(slot != c)
        def _():
            pl.when(j >= 2)(lambda: send(slot, k).wait_send())
            stage_ref[slot] = tile
            send(slot, k).start()

        @pl.when(j == N_DEV - 1)
        def _():
            local(c, 0).wait()
            send(1 - c, 0).wait_send()
            arrived = pair_ref.at[:, dst_rows]
            _rcopy(arrived, arrived, send_sems.at[0], recv_sem.at[0], (x, y, 1 - c)).wait_recv()

    like = _sds((4, D, cw), BF16)
    return _hosted(
        body, name=name, grid=(N_DEV,),
        in_specs=[pl.BlockSpec((S, rows), lambda j: (0, m0 // rows)), HBM_SPEC] + [HBM_SPEC] * (2 * continues),
        out_specs=[HBM_SPEC, HBM_SPEC], out_shape=[like, like], args=[a, g] + [own_so_far, pair_so_far] * continues,
        scratch_shapes=[pltpu.VMEM((2, S, win), BF16), pltpu.VMEM((2, rows, cw), BF16), pltpu.SemaphoreType.DMA((2,)),
                        pltpu.SemaphoreType.DMA((2,)), pltpu.SemaphoreType.DMA((2,)), pltpu.SemaphoreType.DMA((1,))],
        block_bytes=S * rows * 2, scratch_bytes=2 * S * win * 2 + 2 * rows * win * 2 + 2 * rows * win * 4, phases=phases,
        continued={2: 0, 3: 1} if continues else None)


NT_ROWS_TM = 1024
NN_ROWS_TM = 512
NT_ROWS_SUB = 128


def _nt_rows(a, b, *, tk, name, row_ins, vec_ins, outs, tail, b_blocked=False, b_tiled=False, phases=()):
    m, k = a.shape
    tm, sub, nk = NT_ROWS_TM, NT_ROWS_SUB, k // tk
    n_sub = tm // sub
    if b_blocked:
        assert b.shape[0] * b.shape[2] == k and b.shape[2] == tk and b.shape[1] == D
        b_spec = pl.BlockSpec((None, D, tk), lambda i, kk: (kk, 0, 0))
    else:
        assert b_tiled and b.shape[0] * 128 == k and tk % 128 == 0 and b.shape[1] == D
        b_spec = pl.BlockSpec((tk // 128, D, 128), lambda i, kk: (kk, 0, 0))
    row_spec, vec_spec = pl.BlockSpec((tm, D), lambda i, kk: (i, 0)), pl.BlockSpec((1, D), lambda i, kk: (0, 0))
    n_row, n_vec, n_out = len(row_ins), len(vec_ins), len(outs)
    assert nk >= 2

    def body(a_ref, b_ref, *rest):
        row_hbm, vec_refs = rest[:n_row], rest[n_row:n_row + n_vec]
        out_refs = rest[n_row + n_vec:n_row + n_vec + n_out]
        acc_ref, row_sems = rest[n_row + n_vec + n_out], rest[-1]
        row_refs = rest[n_row + n_vec + n_out + 1:-1]
        rhs = jnp.concatenate([b_ref[u] for u in range(b_ref.shape[0])], axis=1) if b_tiled else b_ref[...]
        p = _dot(a_ref[...], rhs, NT)
        i, kk = pl.program_id(0), pl.program_id(1)

        def fetch(r, s):
            src = row_hbm[r].at[pl.ds(pl.multiple_of(i * tm + s * sub, sub), sub)]
            return pltpu.make_async_copy(src, row_refs[r].at[s % 2], row_sems.at[r, s % 2])

        @pl.when(kk == 0)
        def _():
            for r in range(n_row):
                fetch(r, 0).start()
            acc_ref[...] = p

        @pl.when(kk > 0)
        def _():
            acc_ref[...] += p

        @pl.when(kk == nk - 1)
        def _():
            for s in range(n_sub):
                for r in range(n_row):
                    if s + 1 < n_sub:
                        fetch(r, s + 1).start()
                    fetch(r, s).wait()
                rows = slice(s * sub, (s + 1) * sub)
                tail(acc_ref[rows, :], rows, (i == 0) if s == 0 else None, [ref[s % 2] for ref in row_refs], vec_refs, out_refs)

    out_specs = [row_spec if kind == "row" else vec_spec for kind, _ in outs]
    out_shape = [_sds((m, D) if kind == "row" else (1, D), dt) for kind, dt in outs]
    blocks = tm * tk * 2 + D * tk * 2 + sum(tm * D * jnp.dtype(dt).itemsize for kind, dt in outs if kind == "row")
    scratch = ([pltpu.VMEM((tm, D), F32)] + [pltpu.VMEM((2, sub, D), x.dtype) for x in row_ins]
               + [pltpu.SemaphoreType.DMA((n_row, 2))])
    return _hosted(body, name=name, grid=(m // tm, nk), in_specs=[pl.BlockSpec((tm, tk), lambda i, kk: (i, kk)), b_spec]
                   + [HBM_SPEC] * n_row + [vec_spec] * n_vec, out_specs=out_specs, out_shape=out_shape,
                   args=[a, b] + list(row_ins) + list(vec_ins), scratch_shapes=scratch,
                   block_bytes=blocks, scratch_bytes=tm * D * 4 + n_row * 2 * sub * D * 4, phases=phases)


def _nn_rows(a, b, *, name, row_ins, vec_ins, outs, tail):
    m, k = a.shape
    tm = NN_ROWS_TM
    assert b.shape == (k, D)
    row_spec, vec_spec = pl.BlockSpec((tm, D), lambda i: (i, 0)), pl.BlockSpec((1, D), lambda i: (0, 0))
    n_row, n_vec, n_out = len(row_ins), len(vec_ins), len(outs)

    def body(a_ref, b_ref, *rest):
        row_hbm, vec_refs = rest[:n_row], rest[n_row:n_row + n_vec]
        out_refs = rest[n_row + n_vec:n_row + n_vec + n_out]
        d_ref, row_sems = rest[n_row + n_vec + n_out], rest[-1]
        row_refs = rest[n_row + n_vec + n_out + 1:-1]
        i = pl.program_id(0)
        fetches = [pltpu.make_async_copy(row_hbm[r].at[pl.ds(pl.multiple_of(i * tm, tm), tm)], row_refs[r], row_sems.at[r])
                   for r in range(n_row)]
        for cp in fetches:
            cp.start()
        d_ref[...] = _dot(a_ref[...], b_ref[...])
        for cp in fetches:
            cp.wait()
        for s in range(tm // NT_ROWS_SUB):
            rows = slice(s * NT_ROWS_SUB, (s + 1) * NT_ROWS_SUB)
            tail(d_ref[rows, :], rows, (i == 0) if s == 0 else None, [ref[rows, :] for ref in row_refs], vec_refs, out_refs)

    out_specs = [row_spec if kind == "row" else vec_spec for kind, _ in outs]
    out_shape = [_sds((m, D) if kind == "row" else (1, D), dt) for kind, dt in outs]
    blocks = tm * k * 2 + k * D * 2 + sum(tm * D * jnp.dtype(dt).itemsize for kind, dt in outs if kind == "row")
    scratch = [pltpu.VMEM((tm, D), F32)] + [pltpu.VMEM((tm, D), x.dtype) for x in row_ins] + [pltpu.SemaphoreType.DMA((n_row,))]
    outs_, _ = _hosted(body, name=name, grid=(m // tm,), in_specs=[pl.BlockSpec((tm, k), lambda i: (i, 0)),
                                                                 pl.BlockSpec((k, D), lambda i: (0, 0))]
                       + [HBM_SPEC] * n_row + [vec_spec] * n_vec, out_specs=out_specs, out_shape=out_shape,
                       args=[a, b] + list(row_ins) + list(vec_ins), scratch_shapes=scratch,
                       block_bytes=blocks, scratch_bytes=(1 + n_row) * tm * D * 4)
    return outs_


def _vec_add(ref, value, first):
    if first is None:
        ref[...] += value
    else:
        pl.when(first)(lambda: ref.__setitem__(Ellipsis, value))
        pl.when(jnp.logical_not(first))(lambda: ref.__setitem__(Ellipsis, ref[...] + value))


RB = 256


def _row_spec(width):
    return pl.BlockSpec((RB, width), lambda i: (i, 0))


def _vec_spec(width):
    return pl.BlockSpec((1, width), lambda i: (0, 0))


def _rinv(x):
    return lax.rsqrt(jnp.mean(x * x, axis=-1, keepdims=True) + EPS)


def _norm_bwd(dyn, xhat, r):
    return r * (dyn - xhat * jnp.mean(dyn * xhat, axis=-1, keepdims=True))


def _colsum(x):
    return jnp.sum(x, axis=0, keepdims=True)


def _prenorm(x, gain):
    def body(x_ref, g_ref, h_ref):
        xv = x_ref[...]
        h_ref[...] = (xv * _rinv(xv) * g_ref[...]).astype(BF16)

    outs, _ = _hosted(body, name="prenorm", grid=(S // RB,), in_specs=[_row_spec(D), _vec_spec(D)], out_specs=[_row_spec(D)],
                      out_shape=[_sds((S, D), BF16)], args=[x, gain], block_bytes=RB * D * 6)
    return outs[0]


def _mid_fwd(x, y, npost, npre, phases=()):
    def body(x_ref, y_ref, po_ref, pr_ref, x1_ref, h1_ref):
        yv = y_ref[...]
        x1 = x_ref[...] + yv * _rinv(yv) * po_ref[...]
        x1_ref[...] = x1
        h1_ref[...] = (x1 * _rinv(x1) * pr_ref[...]).astype(BF16)

    return _hosted(body, name="mid_fwd", grid=(S // RB,), in_specs=[_row_spec(D), _row_spec(D), _vec_spec(D), _vec_spec(D)],
                   out_specs=[_row_spec(D), _row_spec(D)], out_shape=[_sds((S, D), F32), _sds((S, D), BF16)],
                   args=[x, y, npost, npre], block_bytes=RB * D * 14, phases=phases)


def _final_tail(yv, rows, first, row_vals, vec_refs, out_refs):
    (xv, tv), (po_ref,), (loss_ref, dx_ref, dy_ref, dpo_ref) = row_vals, vec_refs, out_refs
    r = _rinv(yv)
    yhat = yv * r
    err = xv + yhat * po_ref[...] - tv
    dx = err * (1.0 / D)
    dx_ref[rows, :] = dx
    dy_ref[rows, :] = _norm_bwd(dx * po_ref[...], yhat, r).astype(BF16)
    _vec_add(loss_ref, _colsum(err * err), first)
    _vec_add(dpo_ref, _colsum(dx * yhat), first)


def _mid_bwd_tail(dh, rows, first, row_vals, vec_refs, out_refs):
    (dx2, xv, yv), (pr_ref, po_ref), (dx1_ref, dy_ref, dpr_ref, dpo_ref) = row_vals, vec_refs, out_refs
    r = _rinv(xv)
    xhat = xv * r
    dx1 = dx2 + _norm_bwd(dh * pr_ref[...], xhat, r)
    dx1_ref[rows, :] = dx1
    ry = _rinv(yv)
    yhat = yv * ry
    dy_ref[rows, :] = _norm_bwd(dx1 * po_ref[...], yhat, ry).astype(BF16)
    _vec_add(dpr_ref, _colsum(dh * xhat), first)
    _vec_add(dpo_ref, _colsum(dx1 * yhat), first)


def _first_bwd_tail(dh, rows, first, row_vals, vec_refs, out_refs):
    (dx1, xv), (pr_ref,), (gx_ref, dpr_ref) = row_vals, vec_refs, out_refs
    r = _rinv(xv)
    xhat = xv * r
    gx_ref[rows, :] = dx1 + _norm_bwd(dh * pr_ref[...], xhat, r)
    _vec_add(dpr_ref, _colsum(dh * xhat), first)


GLA_RB = 256
GLA_CPB = GLA_RB // C


def _sigmoid(x):
    return 1.0 / (1.0 + jnp.exp(-x))


def _tri(strict):
    r = lax.broadcasted_iota(jnp.int32, (C, C), 0)
    c = lax.broadcasted_iota(jnp.int32, (C, C), 1)
    return jnp.where(c < r if strict else c <= r, 1.0, 0.0).astype(BF16)


def _tri_dot(tri, x):
    hi = x.astype(BF16)
    lo = (x - hi.astype(F32)).astype(BF16)
    return _dot(tri, hi) + _dot(tri, lo)


def _gla_gates(glr_b, w2, b, tri):
    z = _dot(glr_b, w2) + b
    log_a = (jnp.minimum(z, 0.0) - jnp.log(1.0 + jnp.exp(-jnp.abs(z)))) * (1.0 / GLA_TAU)
    bcum = _tri_dot(tri, log_a)
    b_end = jnp.sum(log_a, axis=0, keepdims=True)
    return z, jnp.exp(b_end - bcum), jnp.exp(b_end)


def _gla_fwd(proj, w2p, bgate, ogain, phases=()):
    def body(p_ref, w2_ref, b_ref, og_ref, y_ref, st_out_ref, st_ref):
        @pl.when(pl.program_id(0) == 0)
        def _():
            st_ref[...] = jnp.zeros_like(st_ref)

        tri = _tri(False)

        def chunk(ci, carry):
            rows = pl.ds(pl.multiple_of(ci * C, C), C)
            glr_b = p_ref[rows, LR0:LR0 + LRP].astype(BF16)
            _, ea_all, dec_all = _gla_gates(glr_b, w2_ref[...], b_ref[...], tri)
            for h in range(H):
                ea, dec = ea_all[:, h * DK:(h + 1) * DK], dec_all[:, h * DK:(h + 1) * DK]
                k_dec = (p_ref[rows, K0 + h * DK:K0 + (h + 1) * DK] * ea).astype(BF16)
                v_b = p_ref[rows, V0 + h * DV:V0 + (h + 1) * DV].astype(BF16)
                st = st_ref[h] * dec + _dot(v_b, k_dec, TN)
                st_ref[h] = st
                st_b = st.astype(BF16)
                st_out_ref[ci, h] = st_b
                q_b = (p_ref[rows, Q0 + h * DK:Q0 + (h + 1) * DK] * (DK ** -0.5)).astype(BF16)
                o = _dot(q_b, st_b, NT)
                on = o * _rinv(o)
                g = p_ref[rows, G0 + h * DV:G0 + (h + 1) * DV]
                y_ref[rows, h * DV:(h + 1) * DV] = (on * og_ref[:, h * DV:(h + 1) * DV] * (g * _sigmoid(g))).astype(BF16)
            return carry

        lax.fori_loop(0, GLA_CPB, chunk, 0, unroll=True)

    blocks = GLA_RB * GLA_PAD * 4 + GLA_RB * D * 2 + GLA_CPB * H * DV * DK * 2
    return _hosted(
        body, name="gla_fwd", grid=(S // GLA_RB,),
        in_specs=[pl.BlockSpec((GLA_RB, GLA_PAD), lambda i: (i, 0)),
                  pl.BlockSpec((LRP, H * DK), lambda i: (0, 0)),
                  pl.BlockSpec((1, H * DK), lambda i: (0, 0)),
                  pl.BlockSpec((1, H * DV), lambda i: (0, 0))],
        out_specs=[pl.BlockSpec((GLA_RB, H * DV), lambda i: (i, 0)),
                   pl.BlockSpec((GLA_CPB, H, DV, DK), lambda i: (i, 0, 0, 0))],
        out_shape=[_sds((S, H * DV), BF16), _sds((NC, H, DV, DK), BF16)],
        args=[proj, w2p, bgate, ogain], scratch_shapes=[pltpu.VMEM((H, DV, DK), F32)],
        block_bytes=blocks, scratch_bytes=H * DV * DK * 4, phases=phases)


def _gla_bwd(proj, dypre, states, w2p, bgate, ogain, phases=()):
    nb = S // GLA_RB

    def body(p_ref, dy_ref, st_blk_ref, st_prev_ref, w2_ref, b_ref, og_ref,
             dp_ref, dog_ref, dbg_ref, dw2_ref, r_ref):
        step = pl.program_id(0)

        @pl.when(step == 0)
        def _():
            r_ref[...] = jnp.zeros_like(r_ref)
            dog_ref[...] = jnp.zeros_like(dog_ref)
            dbg_ref[...] = jnp.zeros_like(dbg_ref)
            dw2_ref[...] = jnp.zeros_like(dw2_ref)

        tri = _tri(False)
        tri_strict = _tri(True)
        has_prev = jnp.where(step < nb - 1, 1.0, 0.0).astype(F32)

        def chunk(ci, st_prev_of):
            rows = pl.ds(ci * C if isinstance(ci, int) else pl.multiple_of(ci * C, C), C)
            glr_b = p_ref[rows, LR0:LR0 + LRP].astype(BF16)
            z, ea_all, dec_all = _gla_gates(glr_b, w2_ref[...], b_ref[...], tri)
            d_a, d_end = [], []
            for h in range(H):
                kcol = slice(h * DK, (h + 1) * DK)
                vcol = slice(h * DV, (h + 1) * DV)
                ea, dec = ea_all[:, kcol], dec_all[:, kcol]
                k_dec = p_ref[rows, K0 + h * DK:K0 + (h + 1) * DK] * ea
                k_dec_b = k_dec.astype(BF16)
                v_b = p_ref[rows, V0 + h * DV:V0 + (h + 1) * DV].astype(BF16)
                q_b = (p_ref[rows, Q0 + h * DK:Q0 + (h + 1) * DK] * (DK ** -0.5)).astype(BF16)
                st_b = st_blk_ref[ci, h]
                o = _dot(q_b, st_b, NT)
                rinv = _rinv(o)
                on = o * rinv
                g = p_ref[rows, G0 + h * DV:G0 + (h + 1) * DV]
                sg = _sigmoid(g)
                og = og_ref[:, vcol]
                dyp = dy_ref[rows, vcol]
                dp_ref[rows, G0 + h * DV:G0 + (h + 1) * DV] = (dyp * (on * og) * (sg * (1.0 + g * (1.0 - sg)))).astype(BF16)
                dpn = dyp * (g * sg)
                dog_ref[:, vcol] += _colsum(dpn * on)
                do_b = _norm_bwd(dpn * og, on, rinv).astype(BF16)
                gt = _dot(do_b, q_b, TN) + r_ref[h]
                gt_b = gt.astype(BF16)
                dp_ref[rows, Q0 + h * DK:Q0 + (h + 1) * DK] = (_dot(do_b, st_b) * (DK ** -0.5)).astype(BF16)
                dkd = _dot(v_b, gt_b)
                dp_ref[rows, V0 + h * DV:V0 + (h + 1) * DV] = _dot(k_dec_b, gt_b, NT).astype(BF16)
                dp_ref[rows, K0 + h * DK:K0 + (h + 1) * DK] = (dkd * ea).astype(BF16)
                d_a.append(dkd * k_dec)
                d_end.append(_colsum(gt * st_prev_of(h)) * dec)
                r_ref[h] = gt * dec
            dla = _tri_dot(tri_strict, jnp.concatenate(d_a, axis=1)) + jnp.concatenate(d_end, axis=1)
            dz = dla * (1.0 / GLA_TAU) * (1.0 - _sigmoid(z))
            dz_b = dz.astype(BF16)
            dbg_ref[...] += _colsum(dz)
            dw2_ref[...] += _dot(glr_b, dz_b, TN)
            dp_ref[rows, LR0:LR0 + LRP] = _dot(dz_b, w2_ref[...], NT).astype(BF16)

        def later_chunk(t, carry):
            ci = GLA_CPB - 1 - t
            chunk(ci, lambda h: st_blk_ref[ci - 1, h].astype(F32))
            return carry

        lax.fori_loop(0, GLA_CPB - 1, later_chunk, 0, unroll=True)
        chunk(0, lambda h: st_prev_ref[0, h].astype(F32) * has_prev)

    blocks = (GLA_RB * GLA_PAD * 4 + GLA_RB * D * 4 + (GLA_CPB + 1) * H * DV * DK * 2 + GLA_RB * GLA_PAD * 2)
    rev = lambda i: nb - 1 - i
    return _hosted(
        body, name="gla_bwd", grid=(nb,),
        in_specs=[pl.BlockSpec((GLA_RB, GLA_PAD), lambda i: (rev(i), 0)),
                  pl.BlockSpec((GLA_RB, H * DV), lambda i: (rev(i), 0)),
                  pl.BlockSpec((GLA_CPB, H, DV, DK), lambda i: (rev(i), 0, 0, 0)),
                  pl.BlockSpec((1, H, DV, DK), lambda i: (jnp.maximum(rev(i) * GLA_CPB - 1, 0), 0, 0, 0)),
                  pl.BlockSpec((LRP, H * DK), lambda i: (0, 0)),
                  pl.BlockSpec((1, H * DK), lambda i: (0, 0)),
                  pl.BlockSpec((1, H * DV), lambda i: (0, 0))],
        out_specs=[pl.BlockSpec((GLA_RB, GLA_PAD), lambda i: (rev(i), 0)),
                   pl.BlockSpec((1, H * DV), lambda i: (0, 0)),
                   pl.BlockSpec((1, H * DK), lambda i: (0, 0)),
                   pl.BlockSpec((LRP, H * DK), lambda i: (0, 0))],
        out_shape=[_sds((S, GLA_PAD), BF16), _sds((1, H * DV), F32), _sds((1, H * DK), F32), _sds((LRP, H * DK), F32)],
        args=[proj, dypre, states, states, w2p, bgate, ogain], scratch_shapes=[pltpu.VMEM((H, DV, DK), F32)],
        block_bytes=blocks, scratch_bytes=H * DV * DK * 4, phases=phases)


SGU_RB = 256
GELU_C = 0.7978845608028654
GELU_A = 0.044715


def _gelu(x):
    return 0.5 * x * (1.0 + jnp.tanh(GELU_C * (x + GELU_A * x * x * x)))


def _gelu_grad(x):
    t = jnp.tanh(GELU_C * (x + GELU_A * x * x * x))
    return 0.5 * (1.0 + t) + 0.5 * x * (1.0 - t * t) * (GELU_C * (1.0 + 3.0 * GELU_A * x * x))


def _causal_mask(transposed=False):
    i = lax.broadcasted_iota(jnp.int32, (SGU_BLOCK, SGU_BLOCK), 1 if transposed else 0)
    j = lax.broadcasted_iota(jnp.int32, (SGU_BLOCK, SGU_BLOCK), 0 if transposed else 1)
    return (i >= C) | (j < C)


def _layer_norm(vf, gain, bias):
    mu = jnp.mean(vf, axis=-1, keepdims=True)
    cen = vf - mu
    rstd = lax.rsqrt(jnp.mean(cen * cen, axis=-1, keepdims=True) + EPS)
    xhat = cen * rstd
    return xhat, rstd, xhat * gain + bias


def _sgu_fwd(proj, lng, lnb, ws, bsb, phases=()):
    def body(p_ref, g_ref, b_ref, ws_ref, bs_ref, o_ref):
        mask = _causal_mask()
        for n in range(SGU_RB // SGU_BLOCK):
            rows = slice(n * SGU_BLOCK, (n + 1) * SGU_BLOCK)
            _, _, vn = _layer_norm(_gelu(p_ref[rows, D:2 * D]), g_ref[...], b_ref[...])
            vn_b = vn.astype(BF16)
            for gi in range(SGU_G):
                cols = slice(gi * SGU_GD, (gi + 1) * SGU_GD)
                w = jnp.where(mask, ws_ref[gi], 0.0).astype(BF16)
                vs = _dot(w, vn_b[:, cols]) + bs_ref[gi]
                gate = p_ref[rows, 2 * D + gi * SGU_GD:2 * D + (gi + 1) * SGU_GD]
                o_ref[rows, cols] = (_gelu(p_ref[rows, cols]) * vs * (gate * _sigmoid(gate))).astype(BF16)

    blocks = SGU_RB * SGU_COLS * 4 + SGU_RB * D * 2 + SGU_G * SGU_BLOCK * (SGU_BLOCK + SGU_GD) * 4
    return _hosted(
        body, name="sgu_fwd", grid=(S // SGU_RB,),
        in_specs=[pl.BlockSpec((SGU_RB, SGU_COLS), lambda i: (i, 0)),
                  pl.BlockSpec((1, D), lambda i: (0, 0)), pl.BlockSpec((1, D), lambda i: (0, 0)),
                  pl.BlockSpec((SGU_G, SGU_BLOCK, SGU_BLOCK), lambda i: (0, 0, 0)),
                  pl.BlockSpec((SGU_G, SGU_BLOCK, SGU_GD), lambda i: (0, 0, 0))],
        out_specs=[pl.BlockSpec((SGU_RB, D), lambda i: (i, 0))], out_shape=[_sds((S, D), BF16)],
        args=[proj, lng, lnb, ws, bsb], block_bytes=blocks, phases=phases)


def _sgu_bwd(proj, dpre, lng, lnb, ws, wst, bsb, phases=()):
    nsteps = S // SGU_RB

    def body(p_ref, d_ref, g_ref, b_ref, ws_ref, wst_ref, bs_ref,
             dp_ref, dg_ref, db_ref, dws_ref, dbs_ref, dvn_ref, dvs_acc_ref):
        step = pl.program_id(0)

        @pl.when(step == 0)
        def _():
            dg_ref[...] = jnp.zeros_like(dg_ref)
            db_ref[...] = jnp.zeros_like(db_ref)
            dws_ref[...] = jnp.zeros_like(dws_ref)
            dvs_acc_ref[...] = jnp.zeros_like(dvs_acc_ref)

        mask = _causal_mask()
        maskt = _causal_mask(transposed=True)
        for n in range(SGU_RB // SGU_BLOCK):
            rows = slice(n * SGU_BLOCK, (n + 1) * SGU_BLOCK)
            v = p_ref[rows, D:2 * D]
            xhat, rstd, vn = _layer_norm(_gelu(v), g_ref[...], b_ref[...])
            vn_b = vn.astype(BF16)
            for gi in range(SGU_G):
                cols = slice(gi * SGU_GD, (gi + 1) * SGU_GD)
                w = jnp.where(mask, ws_ref[gi], 0.0).astype(BF16)
                wt = jnp.where(maskt, wst_ref[gi], 0.0).astype(BF16)
                vs = _dot(w, vn_b[:, cols]) + bs_ref[gi]
                u = p_ref[rows, cols]
                gate = p_ref[rows, 2 * D + gi * SGU_GD:2 * D + (gi + 1) * SGU_GD]
                sg = _sigmoid(gate)
                gu = _gelu(u)
                dpre_g = d_ref[rows, cols]
                t = dpre_g * (gate * sg)
                dp_ref[rows, cols] = (t * vs * _gelu_grad(u)).astype(BF16)
                dp_ref[rows, 2 * D + gi * SGU_GD:2 * D + (gi + 1) * SGU_GD] = (
                    dpre_g * gu * vs * (sg * (1.0 + gate * (1.0 - sg)))).astype(BF16)
                dvs = t * gu
                dvs_b = dvs.astype(BF16)
                dvs_acc_ref[:, cols] += dvs
                dws_ref[gi] += _dot(dvs_b, vn_b[:, cols], NT)
                dvn_ref[:, cols] = _dot(wt, dvs_b)
            dvn = dvn_ref[...]
            dg_ref[...] += _colsum(dvn * xhat)
            db_ref[...] += _colsum(dvn)
            dxh = dvn * g_ref[...]
            dvf = rstd * (dxh - jnp.mean(dxh, axis=-1, keepdims=True) - xhat * jnp.mean(dxh * xhat, axis=-1, keepdims=True))
            dp_ref[rows, D:2 * D] = (dvf * _gelu_grad(v)).astype(BF16)

        @pl.when(step == nsteps - 1)
        def _():
            lane = lax.broadcasted_iota(jnp.int32, (SGU_BLOCK, SGU_BLOCK), 1)
            out = jnp.zeros((SGU_BLOCK, SGU_BLOCK), F32)
            for gi in range(SGU_G):
                out = out + jnp.where(lane == gi, jnp.sum(dvs_acc_ref[:, gi * SGU_GD:(gi + 1) * SGU_GD], axis=1, keepdims=True), 0.0)
                dws_ref[gi] = jnp.where(mask, dws_ref[gi], 0.0)
            dbs_ref[...] = out

    blocks = SGU_RB * SGU_COLS * 6 + SGU_RB * D * 4 + SGU_G * SGU_BLOCK * (3 * SGU_BLOCK + SGU_GD) * 4
    const3 = lambda i: (0, 0, 0)
    return _hosted(
        body, name="sgu_bwd", grid=(nsteps,),
        in_specs=[pl.BlockSpec((SGU_RB, SGU_COLS), lambda i: (i, 0)),
                  pl.BlockSpec((SGU_RB, D), lambda i: (i, 0)),
                  pl.BlockSpec((1, D), lambda i: (0, 0)), pl.BlockSpec((1, D), lambda i: (0, 0)),
                  pl.BlockSpec((SGU_G, SGU_BLOCK, SGU_BLOCK), const3),
                  pl.BlockSpec((SGU_G, SGU_BLOCK, SGU_BLOCK), const3),
                  pl.BlockSpec((SGU_G, SGU_BLOCK, SGU_GD), const3)],
        out_specs=[pl.BlockSpec((SGU_RB, SGU_COLS), lambda i: (i, 0)),
                   pl.BlockSpec((1, D), lambda i: (0, 0)), pl.BlockSpec((1, D), lambda i: (0, 0)),
                   pl.BlockSpec((SGU_G, SGU_BLOCK, SGU_BLOCK), const3),
                   pl.BlockSpec((SGU_BLOCK, SGU_BLOCK), lambda i: (0, 0))],
        out_shape=[_sds((S, SGU_COLS), BF16), _sds((1, D), F32), _sds((1, D), F32),
                   _sds((SGU_G, SGU_BLOCK, SGU_BLOCK), F32), _sds((SGU_BLOCK, SGU_BLOCK), F32)],
        args=[proj, dpre, lng, lnb, ws, wst, bsb],
        scratch_shapes=[pltpu.VMEM((SGU_BLOCK, D), F32), pltpu.VMEM((SGU_BLOCK, D), F32)],
        block_bytes=blocks, scratch_bytes=2 * SGU_BLOCK * D * 4, phases=phases)


def _pair_sum(own, a, r0, nr, name, table=None):
    c = own.shape[2]
    tr = 256
    assert r0 % tr == 0 and nr % tr == 0

    def body(own_ref, sib_ref, o_ref):
        o_ref[...] = (own_ref[...].astype(F32) + sib_ref[...].astype(F32)).astype(BF16)

    own_map = ((lambda j, i: (1 + j, r0 // tr + i, 0)) if table is None else
               (lambda j, i, t: (t[1 + j], r0 // tr + i, 0)))
    cpad = -(-c // 128) * 128
    outs, _ = _hosted(
        body, name=name, grid=(3, nr // tr),
        in_specs=[pl.BlockSpec((None, tr, c), own_map),
                  pl.BlockSpec((None, tr, c), lambda j, i, *t: (1 + j, r0 // tr + i, 0))],
        out_specs=[pl.BlockSpec((None, tr, c), lambda j, i, *t: (j, i, 0))], out_shape=[_sds((3, nr, c), BF16)],
        args=[own, a], block_bytes=3 * tr * cpad * 2, table=table)
    return outs[0]


def _adamw_math(w, g, m, v):
    m = ADAM_B1 * m + (1.0 - ADAM_B1) * g
    v = ADAM_B2 * v + (1.0 - ADAM_B2) * (g * g)
    m_hat = m / (1.0 - ADAM_B1 ** ADAM_STEP)
    v_hat = v / (1.0 - ADAM_B2 ** ADAM_STEP)
    delta = -ADAM_LR * (m_hat / (jnp.sqrt(v_hat) + ADAM_EPS) + ADAM_WD * w)
    return delta, m, v


def _sum_adamw(own, a, b, w, m, v, *, name, phases=(), table=None):
    r, c = w.shape
    tr = 256

    def body(own_ref, sib_ref, far_ref, w_ref, m_ref, v_ref, g_ref, d_ref, nm_ref, nv_ref):
        g = own_ref[...].astype(F32) + sib_ref[...].astype(F32)
        for j in range(3):
            g = g + far_ref[j].astype(F32)
        g_ref[...] = g
        d_ref[...], nm_ref[...], nv_ref[...] = _adamw_math(w_ref[...], g, m_ref[...], v_ref[...])

    spec = pl.BlockSpec((tr, c), lambda i, *t: (i, 0))
    own_map = (lambda i: (0, i, 0)) if table is None else (lambda i, t: (t[0], i, 0))
    cpad = -(-c // 128) * 128
    return _hosted(
        body, name=name, grid=(r // tr,),
        in_specs=[pl.BlockSpec((None, tr, c), own_map), pl.BlockSpec((None, tr, c), lambda i, *t: (0, i, 0)),
                  pl.BlockSpec((3, tr, c), lambda i, *t: (0, i, 0)), spec, spec, spec],
        out_specs=[spec] * 4, out_shape=[_sds((r, c), F32)] * 4, args=[own, a, b, w, m, v],
        block_bytes=5 * tr * cpad * 2 + 7 * tr * cpad * 4, phases=phases, table=table)


def _sum_parts(parts, name):
    n, r, c = parts.shape

    def body(p_ref, o_ref):
        g = p_ref[0]
        for j in range(1, n):
            g = g + p_ref[j]
        o_ref[...] = g

    outs, _ = _hosted(body, name=name, grid=(1,), in_specs=[pl.BlockSpec((n, r, c), lambda i: (0, 0, 0))],
                      out_specs=[pl.BlockSpec((r, c), lambda i: (0, 0))], out_shape=[_sds((r, c), F32)], args=[parts],
                      block_bytes=(n + 1) * r * c * 4)
    return outs[0]


def _adamw(w, g, m, v, name):
    def body(w_ref, g_ref, m_ref, v_ref, d_ref, nm_ref, nv_ref):
        d_ref[...], nm_ref[...], nv_ref[...] = _adamw_math(w_ref[...], g_ref[...], m_ref[...], v_ref[...])

    spec = pl.BlockSpec(w.shape, lambda i: (0, 0))
    outs, _ = _hosted(body, name=name, grid=(1,), in_specs=[spec] * 4, out_specs=[spec] * 3, out_shape=[_sds(w.shape, F32)] * 3,
                      args=[w, g, m, v], block_bytes=7 * _nbytes(w.shape, F32))
    return outs


def _blocks_to_columns(g):
    n, r, c = g.shape
    return jnp.transpose(g, (1, 0, 2)).reshape(r, n * c)


def _pack(parts):
    return jnp.concatenate([p.reshape(-1) for p in parts]).reshape(-1, 128)


def _unpack(packed, like):
    flat, outs, off = packed.reshape(-1), [], 0
    for p in like:
        outs.append(flat[off:off + p.size].reshape(p.shape))
        off += p.size
    return outs


def kernel(x, norm_pre, norm_post, gla_w_in, gla_w_gate2, gla_b_gate, gla_o_gain, gla_w_out, sgu_w_in, sgu_ln_gain, sgu_ln_bias, sgu_w_spatial, sgu_b_spatial, sgu_w_out, loss_target, m_norm_pre, m_norm_post, m_gla_w_in, m_gla_w_gate2, m_gla_b_gate, m_gla_o_gain, m_gla_w_out, m_sgu_w_in, m_sgu_ln_gain, m_sgu_ln_bias, m_sgu_w_spatial, m_sgu_b_spatial, m_sgu_w_out, v_norm_pre, v_norm_post, v_gla_w_in, v_gla_w_gate2, v_gla_b_gate, v_gla_o_gain, v_gla_w_out, v_sgu_w_in, v_sgu_ln_gain, v_sgu_ln_bias, v_sgu_w_spatial, v_sgu_b_spatial, v_sgu_w_out):
    me = _index_of(*_place())
    x0 = x.reshape(S, D)
    tgt = loss_target.reshape(S, D)
    npre0, npre1 = norm_pre[0:1], norm_pre[1:2]
    npost0, npost1 = norm_post[0:1], norm_post[1:2]
    ws = sgu_w_spatial[0]
    wst = jnp.transpose(ws, (0, 2, 1))
    bsb = jnp.broadcast_to(sgu_b_spatial[0][:, :, None], (SGU_G, SGU_BLOCK, SGU_GD))
    W_ROWS = D // N_DEV
    IN_COLS_G, IN_COLS_S = GLA_COLS // N_DEV, SGU_COLS // N_DEV

    s_gwi, s_gwo = gla_w_in[0].astype(BF16), gla_w_out[0].astype(BF16)
    s_swi, s_swo = sgu_w_in[0].astype(BF16), sgu_w_out[0].astype(BF16)
    small = jnp.concatenate([jnp.pad(gla_w_gate2[0].reshape(4, 512), ((0, 4), (0, 0))),
                             jnp.pad(jnp.concatenate([sgu_ln_gain, sgu_ln_bias], axis=1), ((0, 7), (0, 0)))], axis=0)

    wg_in, g_small = _gather_first(s_gwi, small, "gather_first")
    w2 =_blocks_to_columns(g_small[:, :4, :].reshape(N_DEV, LR, 128))
    w2p = jnp.pad(w2, ((0, LRP - LR), (0, 0))).astype(BF16)
    lng = g_small[:, 8, :256].reshape(1, D)
    lnb = g_small[:, 8, 256:].reshape(1, D)
    like_gwo, like_swi = _sds((N_DEV, W_ROWS, D), BF16), _sds((N_DEV, D, IN_COLS_S), BF16)

    h0 = _prenorm(x0, npre0)
    proj0, (g_gwo, g_swi) = _mm(h0, wg_in, "nn", F32, tm=1024, tn=896, tk=D, name="gla_in", b_tiled=True, phases=[
        _Phase(like_gwo, None, [_gather_send(s_gwo, 0, W_ROWS)]),
        _Phase(like_swi, None, [_gather_send(s_swi, 0, 768, diagonal=False)])])
    (ypre0, states), (g_gwo, g_swi) = _gla_fwd(proj0, w2p, gla_b_gate, gla_o_gain, phases=[
        _Phase(like_gwo, g_gwo, [_gather_pass(0, W_ROWS)]),
        _Phase(like_swi, g_swi, [_gather_relay(0, 768), _gather_send(s_swi, 768, 512, diagonal=False)])])
    wg_out = g_gwo.reshape(D, D)
    y0, (g_swi,) = _mm(ypre0, wg_out, "nn", F32, tm=1024, tn=1024, tk=D, name="gla_out", phases=[
        _Phase(like_swi, g_swi, [_gather_pass(0, 768), _gather_relay(768, 512), _gather_send(s_swi, 1280, 512, diagonal=False)])])
    (x1, h1), (g_swi,) = _mid_fwd(x0, y0, npost0, npre1, phases=[
        _Phase(like_swi, g_swi, [_gather_pass(768, 512), _gather_relay(1280, 512), _gather_send(s_swi, 1792, 256, diagonal=False)])])
    g_swi, = _carry([_Phase(like_swi, g_swi, [_gather_pass(1280, 512), _gather_relay(1792, 256)])], "relay_sgu_w_in")
    g_swi, = _carry([_Phase(like_swi, g_swi, [_gather_pass(1792, 256)])], "pass_sgu_w_in")
    proj1, (g_swo,) = _mm(h1, g_swi, "nn", F32, tm=S, tn=IN_COLS_S, tk=D, name="sgu_in", b_blocked=True, phases=[
        _Phase(like_gwo, None, [_gather_send(s_swo, 0, W_ROWS)])])
    (pre1,), (g_swo,) = _sgu_fwd(proj1, lng, lnb, ws, bsb, phases=[_Phase(like_gwo, g_swo, [_gather_pass(0, W_ROWS)])])
    ws_out = g_swo.reshape(D, D)
    loss_cols, dx2, dy1, dnpost1 = _nn_rows(pre1, ws_out, name="sgu_out", row_ins=[x1, tgt], vec_ins=[npost1],
                                            outs=[("vec", F32), ("row", F32), ("row", BF16), ("vec", F32)], tail=_final_tail)
    loss_here = jnp.pad((0.5 * jnp.sum(loss_cols) / D).reshape(1, 1), ((0, 7), (0, 127)))

    like_b_out, like_b_swi = _sds((3, W_ROWS, D), BF16), _sds((3, D, IN_COLS_S), BF16)
    like_b_gwi = _sds((3, D, IN_COLS_G), BF16)
    row_pair = dict(like=_sds((4, W_ROWS, D), BF16), block=lambda i, j: i, ordinal=lambda i, j: i >> 1,
                    dst=lambda ref, k, i, j: ref.at[k])
    col_pair = dict(like=_sds((4, D, IN_COLS_S), BF16), block=lambda i, j: j, ordinal=lambda i, j: 4 * i + (j >> 1),
                    dst=lambda ref, k, i, j: ref.at[k, pl.ds(pl.multiple_of(i * 1024, 1024), 1024)])

    mine = _own_table()
    dws_out, (a_swo,) = _mm(pre1, dy1, "tn", BF16, tm=W_ROWS, tn=D, tk=S, name="sgu_out_dw", pair=row_pair)
    p_swo = dws_out.reshape(N_DEV, W_ROWS, D)
    t_swo = _pair_sum(p_swo, a_swo, 0, W_ROWS, "pair_sum_sgu_w_out", table=mine)
    dpre1, _ = _mm(dy1, ws_out, "nt", F32, tm=1024, tn=1024, tk=D, name="sgu_out_dx")
    (dproj1, dlng, dlnb, dwsp, dbsp), (b_swo,) = _sgu_bwd(proj1, dpre1, lng, lnb, ws, wst, bsb, phases=[
        _Phase(like_b_out, None, [_reduce_cross(t_swo, 0, 0, W_ROWS)])])
    p_swi, (a_swi,) = _mm(h1, dproj1, "tn", BF16, tm=1024, tn=IN_COLS_S, tk=S, name="sgu_in_dw", out_blocked=True, pair=col_pair)
    t_swi = _pair_sum(p_swi, a_swi, 0, D, "pair_sum_sgu_w_in", table=mine)
    (dx1, dy0, dnpre1, dnpost0), (b_swi,) = _nt_rows(
        dproj1, g_swi, tk=IN_COLS_S, name="sgu_in_dx", b_blocked=True, row_ins=[dx2, x1, y0], vec_ins=[npre1, npost0],
        outs=[("row", F32), ("row", BF16), ("vec", F32), ("vec", F32)], tail=_mid_bwd_tail, phases=[
            _Phase(like_b_swi, None, [_reduce_cross(t_swi, 0, 0, 1024)])])
    dwg_out, (a_gwo, b_swi) = _mm(ypre0, dy0, "tn", BF16, tm=W_ROWS, tn=D, tk=S, name="gla_out_dw", pair=row_pair, phases=[
        _Phase(like_b_swi, b_swi, [_reduce_cross(t_swi, 1024, 1024, 256)])])
    p_gwo = dwg_out.reshape(N_DEV, W_ROWS, D)
    t_gwo = _pair_sum(p_gwo, a_gwo, 0, W_ROWS, "pair_sum_gla_w_out", table=mine)
    dypre0, _ = _mm(dy0, wg_out, "nt", F32, tm=1024, tn=1024, tk=D, name="gla_out_dx")
    late = [dnpre1, dnpost1, dlng, dlnb, dwsp, jnp.transpose(dbsp[:, :SGU_G])]
    late_pack = _pack(late)
    (dproj0, dogain, dbgate, dw2), (b_swi, b_gwo, g_late) = _gla_bwd(proj0, dypre0, states, w2p, gla_b_gate, gla_o_gain, phases=[
        _Phase(like_b_swi, b_swi, [_reduce_cross(t_swi, 1280, 1280, 768)]),
        _Phase(like_b_out, None, [_reduce_cross(t_gwo, 0, 0, W_ROWS)]),
        _Phase(_sds((N_DEV,) + late_pack.shape, F32), None, [_gather_send(late_pack, 0, late_pack.shape[0])])])
    half = D // 2
    (own_gwi, a_gwi), (g_late,) = _dw_blocks(h0, dproj0, None, None, 0, half, "gla_in_dw_a", phases=[
        _Phase(_sds((N_DEV,) + late_pack.shape, F32), g_late, [_gather_pass(0, late_pack.shape[0])])])
    t_gwi_a = _pair_sum(own_gwi, a_gwi, 0, half, "pair_sum_gla_w_in_a")
    (own_gwi, a_gwi), (b_gwi,) = _dw_blocks(h0, dproj0, own_gwi, a_gwi, half, half, "gla_in_dw_b", phases=[
        _Phase(like_b_gwi, None, [_reduce_cross(t_gwi_a, 0, 0, 640)])])
    t_gwi_b = _pair_sum(own_gwi, a_gwi, half, half, "pair_sum_gla_w_in_b")
    (grad_x, dnpre0), (b_gwi,) = _nt_rows(
        dproj0, wg_in, tk=896, name="gla_in_dx", b_tiled=True, row_ins=[dx1, x0], vec_ins=[npre0],
        outs=[("row", F32), ("vec", F32)], tail=_first_bwd_tail, phases=[
            _Phase(like_b_gwi, b_gwi, [_reduce_cross(t_gwi_a, 640, 640, 384), _reduce_cross(t_gwi_b, 0, half, half)])])

    early = [dnpre0, dnpost0, dbgate, dogain, dw2[:LR], loss_here]
    early_pack = _pack(early)
    like_early = _sds((N_DEV,) + early_pack.shape, F32)
    (g_swo, d_swo, nm_swo, nv_swo), (g_early,) = _sum_adamw(
        p_swo, a_swo, b_swo, sgu_w_out[0], m_sgu_w_out[0], v_sgu_w_out[0], name="adamw_sgu_w_out", table=mine, phases=[
            _Phase(like_early, None, [_gather_send(early_pack, 0, early_pack.shape[0])])])
    (g_gwo_, d_gwo, nm_gwo, nv_gwo), (g_early,) = _sum_adamw(
        p_gwo, a_gwo, b_gwo, gla_w_out[0], m_gla_w_out[0], v_gla_w_out[0], name="adamw_gla_w_out", table=mine, phases=[
            _Phase(like_early, g_early, [_gather_pass(0, early_pack.shape[0])])])
    (g_swi_, d_swi, nm_swi, nv_swi), _ = _sum_adamw(
        p_swi, a_swi, b_swi, sgu_w_in[0], m_sgu_w_in[0], v_sgu_w_in[0], name="adamw_sgu_w_in", table=mine)
    (g_gwi_, d_gwi, nm_gwi, nv_gwi), _ = _sum_adamw(
        own_gwi, a_gwi, b_gwi, gla_w_in[0], m_gla_w_in[0], v_gla_w_in[0], name="adamw_gla_w_in")

    g_npre1, g_npost1, g_lng_full, g_lnb_full, g_wsp, g_bsp = _unpack(_sum_parts(g_late, "sum_late_small_grads"), late)
    g_npre0, g_npost0, g_bgate, g_ogain, g_w2_full, loss_all = _unpack(_sum_parts(g_early, "sum_early_small_grads"), early)
    loss = loss_all[0, 0]
    g_w2 = lax.dynamic_slice(g_w2_full, (0, me * 128), (LR, 128))
    g_lng = lax.dynamic_slice(g_lng_full, (0, me * 256), (1, 256))
    g_lnb = lax.dynamic_slice(g_lnb_full, (0, me * 256), (1, 256))
    small_g = [jnp.concatenate([g_npre0, g_npre1], 0), jnp.concatenate([g_npost0, g_npost1], 0), g_w2, g_bgate, g_ogain,
               g_lng, g_lnb, g_wsp, g_bsp]
    small_w = [norm_pre, norm_post, gla_w_gate2[0], gla_b_gate, gla_o_gain, sgu_ln_gain, sgu_ln_bias, sgu_w_spatial[0], sgu_b_spatial[0]]
    small_m = [m_norm_pre, m_norm_post, m_gla_w_gate2[0], m_gla_b_gate, m_gla_o_gain, m_sgu_ln_gain, m_sgu_ln_bias, m_sgu_w_spatial[0], m_sgu_b_spatial[0]]
    small_v = [v_norm_pre, v_norm_post, v_gla_w_gate2[0], v_gla_b_gate, v_gla_o_gain, v_sgu_ln_gain, v_sgu_ln_bias, v_sgu_w_spatial[0], v_sgu_b_spatial[0]]
    d_pack, nm_pack, nv_pack = _adamw(_pack(small_w), _pack(small_g), _pack(small_m), _pack(small_v), "adamw_small")

    out_like = [norm_pre, norm_post, gla_w_gate2, gla_b_gate, gla_o_gain, sgu_ln_gain, sgu_ln_bias, sgu_w_spatial, sgu_b_spatial]
    sg_ = [g.reshape(s.shape) for g, s in zip(small_g, out_like)]
    sd_, sm_, sv_ = (_unpack(pk, out_like) for pk in (d_pack, nm_pack, nv_pack))

    def assemble(small_list, w_in_g, w_out_g, w_in_s, w_out_s):
        npre_, npost_, w2_, bg_, og_, lg_, lb_, wsp_, bsp_ = small_list
        return [npre_, npost_, w_in_g[None], w2_, bg_, og_, w_out_g[None], w_in_s[None], lg_, lb_, wsp_, bsp_, w_out_s[None]]

    return (loss, grad_x.reshape(1, S, D),
            *assemble(sg_, g_gwi_, g_gwo_, g_swi_, g_swo),
            *assemble(sd_, d_gwi, d_gwo, d_swi, d_swo),
            *assemble(sm_, nm_gwi, nm_gwo, nm_swi, nm_swo),
            *assemble(sv_, nv_gwi, nv_gwo, nv_swi, nv_swo))
```

```python
import functools

import jax
import jax.numpy as jnp
from jax import lax
from jax.experimental import pallas as pl
from jax.experimental.pallas import tpu as pltpu

F32 = jnp.float32
BF16 = jnp.bfloat16

N_DEV = 8
S = 2048
D = 2048
H = 4
DK = 256
DV = 512
C = 64
NC = S // C
GLA_COLS = 6160
GLA_PAD = 6272
Q0, K0, V0, G0, LR0 = 0, 1024, 2048, 4096, 6144
LR = 16
LRP = 128
SGU_COLS = 6144
SGU_BLOCK = 128
SGU_G = 8
SGU_GD = 256
EPS = 1e-6
GLA_TAU = 16.0

ADAM_LR, ADAM_B1, ADAM_B2, ADAM_EPS, ADAM_WD, ADAM_STEP = 0.001, 0.9, 0.999, 1e-08, 0.01, 10

V7X_VMEM_BYTES = 64 * 1024 * 1024
VMEM_CEILING = V7X_VMEM_BYTES - 6 * 1024 * 1024
MESH = pl.DeviceIdType.MESH
HBM_SPEC = pl.BlockSpec(memory_space=pl.ANY)


def _sds(shape, dtype):
    return jax.ShapeDtypeStruct(tuple(shape), dtype)


def _nbytes(shape, dtype):
    n = 1
    for s in shape:
        n *= s
    return n * jnp.dtype(dtype).itemsize


def _dot(a, b, dims=(((1,), (0,)), ((), ())), precision=None):
    return lax.dot_general(a, b, dims, precision=precision, preferred_element_type=F32)


NN = (((1,), (0,)), ((), ()))
TN = (((0,), (0,)), ((), ()))
NT = (((1,), (1,)), ((), ()))


def _place():
    return lax.axis_index("x"), lax.axis_index("y"), lax.axis_index("c")


def _index_of(px, py, pc):
    return 4 * px + 2 * py + pc


def _chips(x, y):
    return [(1 - x, y), (x, 1 - y), (1 - x, 1 - y)]


def _rcopy(src, dst, send_sem, recv_sem, to):
    return pltpu.make_async_remote_copy(src_ref=src, dst_ref=dst, send_sem=send_sem, recv_sem=recv_sem,
                                        device_id=to, device_id_type=MESH)


class _Move:
    def __init__(self, ins, n_remote, make, stage=None):
        self.ins, self.n_remote, self.make, self.stage = list(ins), n_remote, make, stage

    def scratch(self):
        sems = [pltpu.SemaphoreType.DMA((self.n_remote,)), pltpu.SemaphoreType.DMA((self.n_remote,))]
        return sems if self.stage is None else sems + [pltpu.SemaphoreType.DMA((1,)), pltpu.VMEM(*self.stage)]

    def start(self, in_refs, buf, scratch):
        sends, _, local = self.make(in_refs, buf, scratch[0], scratch[1])
        if local is not None:
            pltpu.make_async_copy(local[0], scratch[3], scratch[2].at[0]).start()
        for cp in sends:
            cp.start()

    def finish(self, in_refs, buf, scratch):
        sends, arrivals, local = self.make(in_refs, buf, scratch[0], scratch[1])
        if local is not None:
            pltpu.make_async_copy(local[0], scratch[3], scratch[2].at[0]).wait()
            out = pltpu.make_async_copy(scratch[3], local[1], scratch[2].at[0])
            out.start()
        for cp in arrivals:
            cp.wait_recv()
        for cp in sends:
            cp.wait_send()
        if local is not None:
            out.wait()


class _Phase:
    def __init__(self, like, so_far, moves):
        self.like, self.so_far, self.moves = like, so_far, list(moves)


def _gather_send(shard, r0, nr, diagonal=True):
    def make(in_refs, g, ss, rs):
        sh, = in_refs
        x, y, c = _place()
        me = _index_of(x, y, c)
        rows = pl.ds(r0, nr)
        peers = [(x, y, 1 - c)] + [(px, py, c) for px, py in _chips(x, y)[:3 if diagonal else 2]]
        sends = [_rcopy(sh.at[rows], g.at[me, rows], ss.at[k], rs.at[k], p) for k, p in enumerate(peers)]
        arrivals = [_rcopy(sh.at[rows], g.at[_index_of(*p), rows], ss.at[k], rs.at[k], p) for k, p in enumerate(peers)]
        return sends, arrivals, (sh.at[rows], g.at[me, rows])

    return _Move([shard], 4 if diagonal else 3, make, stage=((nr, shard.shape[1]), shard.dtype))


def _gather_relay(r0, nr):
    def make(in_refs, g, ss, rs):
        x, y, c = _place()
        nx, ny, nd = [(px, py, c) for px, py in _chips(x, y)]
        first, second = pl.ds(r0, nr // 2), pl.ds(r0 + nr // 2, nr // 2)
        sends = [_rcopy(g.at[_index_of(*nx), first], g.at[_index_of(*nx), first], ss.at[0], rs.at[0], ny),
                 _rcopy(g.at[_index_of(*ny), second], g.at[_index_of(*ny), second], ss.at[1], rs.at[1], nx)]
        arrivals = [_rcopy(g.at[_index_of(*nx), first], g.at[_index_of(*nd), first], ss.at[0], rs.at[0], ny),
                    _rcopy(g.at[_index_of(*ny), second], g.at[_index_of(*nd), second], ss.at[1], rs.at[1], nx)]
        return sends, arrivals, None

    return _Move([], 2, make)


def _gather_pass(r0, nr):
    def make(in_refs, g, ss, rs):
        x, y, c = _place()
        rows = pl.ds(r0, nr)
        sends = [_rcopy(g.at[_index_of(px, py, c), rows], g.at[_index_of(px, py, c), rows], ss.at[j], rs.at[j], (x, y, 1 - c))
                 for j, (px, py) in enumerate(_chips(x, y))]
        arrivals = [_rcopy(g.at[_index_of(px, py, c), rows], g.at[_index_of(px, py, 1 - c), rows], ss.at[j], rs.at[j], (x, y, 1 - c))
                    for j, (px, py) in enumerate(_chips(x, y))]
        return sends, arrivals, None

    return _Move([], 3, make)


def _own_table():
    x, y, c = _place()
    return jnp.stack([_index_of(px, py, c) for px, py in [(x, y)] + _chips(x, y)]).astype(jnp.int32)


def _reduce_cross(sums, src_r0, dst_r0, nr):
    def make(in_refs, b, ss, rs):
        t, = in_refs
        x, y, c = _place()
        src, dst = pl.ds(src_r0, nr), pl.ds(dst_r0, nr)
        sends = [_rcopy(t.at[j, src], b.at[j, dst], ss.at[j], rs.at[j], (px, py, c)) for j, (px, py) in enumerate(_chips(x, y))]
        return sends, sends, None

    return _Move([sums], 3, make)


def _hosted(body, *, name, grid, in_specs, out_specs, out_shape, args, scratch_shapes=(), block_bytes, scratch_bytes=0,
            phases=(), table=None, continued=None):
    n_in, n_out, n_scr = len(args), len(out_shape), len(scratch_shapes)
    all_args, all_out_shape, sems, aliases, layout = list(args), list(out_shape), [], dict(continued or {}), []
    for j, ph in enumerate(phases):
        counts = []
        for mv in ph.moves:
            all_args += mv.ins
            counts.append(len(mv.ins))
            sems += mv.scratch()
        if ph.so_far is not None:
            aliases[len(all_args)] = n_out + j
            all_args.append(ph.so_far)
        layout.append((counts, ph.so_far is not None))
        all_out_shape.append(ph.like)
    n_extra_in = len(all_args) - n_in

    def wrapped(*refs):
        ins, pos = refs[:n_in], n_in
        move_ins = []
        for counts, continues in layout:
            per_move = []
            for cnt in counts:
                per_move.append(refs[pos:pos + cnt])
                pos += cnt
            pos += continues
            move_ins.append(per_move)
        outs = refs[pos:pos + n_out]
        bufs = refs[pos + n_out:pos + n_out + len(phases)]
        pos += n_out + len(phases)
        scratch = refs[pos:pos + n_scr]
        pos += n_scr
        move_sems = []
        for ph in phases:
            per_move = []
            for mv in ph.moves:
                count = len(mv.scratch())
                per_move.append(refs[pos:pos + count])
                pos += count
            move_sems.append(per_move)

        def each_move(fn_name):
            for ph, buf, per_in, per_sem in zip(phases, bufs, move_ins, move_sems):
                for mv, mv_in, mv_sem in zip(ph.moves, per_in, per_sem):
                    getattr(mv, fn_name)(mv_in, buf, mv_sem)

        if phases:
            first = functools.reduce(jnp.logical_and, [pl.program_id(a) == 0 for a in range(len(grid))])
            last = functools.reduce(jnp.logical_and, [pl.program_id(a) == grid[a] - 1 for a in range(len(grid))])
            pl.when(first)(lambda: each_move("start"))
        body(*ins, *outs, *scratch)
        if phases:
            pl.when(last)(lambda: each_move("finish"))

    all_args = [pltpu.with_memory_space_constraint(a, pltpu.HBM) for a in all_args]
    est = 2 * block_bytes + scratch_bytes
    params = pltpu.CompilerParams(dimension_semantics=("arbitrary",) * len(grid),
                                  vmem_limit_bytes=min(VMEM_CEILING, max(32 * 1024 * 1024, 2 * est)))
    all_in_specs, all_out_specs = list(in_specs) + [HBM_SPEC] * n_extra_in, list(out_specs) + [HBM_SPEC] * len(phases)
    if table is None:
        results = pl.pallas_call(
            wrapped, grid=grid, in_specs=all_in_specs, out_specs=all_out_specs, out_shape=all_out_shape,
            scratch_shapes=list(scratch_shapes) + sems, input_output_aliases=aliases, compiler_params=params, name=name,
        )(*all_args)
    else:
        results = pl.pallas_call(
            lambda table_ref, *refs: wrapped(*refs),
            grid_spec=pltpu.PrefetchScalarGridSpec(num_scalar_prefetch=1, grid=grid, in_specs=all_in_specs, out_specs=all_out_specs,
                                                   scratch_shapes=list(scratch_shapes) + sems),
            out_shape=all_out_shape, input_output_aliases={k + 1: v for k, v in aliases.items()}, compiler_params=params, name=name,
        )(table, *all_args)
    return list(results[:n_out]), list(results[n_out:])


class _Both:
    def __init__(self, copies):
        self.copies = copies

    def start(self):
        for cp in self.copies:
            cp.start()

    def wait_send(self):
        for cp in self.copies:
            cp.wait_send()

    def wait_recv(self):
        for cp in self.copies:
            cp.wait_recv()


def _carry(phases, name):
    def body(o_ref):
        o_ref[...] = jnp.zeros_like(o_ref)

    _, bufs = _hosted(body, name=name, grid=(1,), in_specs=[], out_specs=[pl.BlockSpec((8, 128), lambda i: (0, 0))],
                      out_shape=[_sds((8, 128), F32)], args=[], block_bytes=8 * 128 * 4, phases=phases)
    return bufs


def _gather_first(shard, small, xs, gain, name):
    cw, tr, n_tiles = shard.shape[1], 256, GLA_PAD // 128

    def body(sh_ref, sm_ref, xs_hbm, gain_ref, wn_ref, g_ref, gs_ref, h_hbm, wt_ref, win_ref, tmp_ref, xin_ref, hout_ref,
             send_sems, recv_sems, local_sems, xin_sems, hout_sems):
        x, y, c = _place()
        me, sibling = (x, y, c), (x, y, 1 - c)
        chips = _chips(x, y)

        def copy(base, out_ref, k, block, to, src=None):
            dst = out_ref.at[_index_of(*block)]
            return _rcopy(dst if src is None else src, dst, send_sems.at[base + k], recv_sems.at[base + k], to)

        icopy = functools.partial(copy, 0, g_ref)
        scopy = functools.partial(copy, 8, gs_ref)

        def wcopy(k, block, to, src=None):
            if k in (1, 2):
                return icopy(k, block, to, src)
            halves = []
            for part, sem in enumerate((k, {0: 15, 4: 16, 5: 17, 6: 18}[k])):
                rows = pl.ds(part * (D // 2), D // 2)
                dst = g_ref.at[_index_of(*block), rows]
                halves.append(_rcopy(dst if src is None else src.at[rows], dst, send_sems.at[sem], recv_sems.at[sem], to))
            return _Both(halves)

        def relay(k, block, half, to):
            rows = pl.ds(half * (D // 2), D // 2)
            ref = g_ref.at[_index_of(*block), rows]
            return _rcopy(ref, ref, send_sems.at[k], recv_sems.at[k], to)

        def load(src_ref, slot):
            cp = pltpu.make_async_copy(src_ref, win_ref.at[slot], local_sems.at[0])
            cp.start()
            cp.wait()

        def place(slot, block):
            b = _index_of(*block)

            def rows_chunk(r, carry):
                rows = pl.ds(pl.multiple_of(r * tr, tr), tr)
                tmp_ref[:, :cw] = win_ref[slot, rows, :].astype(F32)
                shifted = pltpu.roll(tmp_ref[...], 2 * b, 1)
                for u in range(7):
                    wt_ref[6 * b + u, rows, :] = (wt_ref[6 * b + u, rows, :].astype(F32) + shifted[:, 128 * u:128 * (u + 1)]).astype(BF16)
                return carry

            lax.fori_loop(0, D // tr, rows_chunk, 0)

        small_own = pltpu.make_async_copy(sm_ref, gs_ref.at[_index_of(*me)], local_sems.at[1])
        small_own.start()
        first = [wcopy(1 + j, me, (*chip, c), src=sh_ref) for j, chip in enumerate(chips[:2])]
        first += [scopy(0, me, sibling, src=sm_ref)] + [scopy(1 + j, me, (*chip, c), src=sm_ref) for j, chip in enumerate(chips)]
        for cp in first:
            cp.start()

        def clear(t, carry):
            wt_ref[t] = jnp.zeros((D, 128), BF16)
            return carry

        lax.fori_loop(0, n_tiles, clear, 0)
        tmp_ref[...] = jnp.zeros_like(tmp_ref)

        def emit(t):
            pltpu.make_async_copy(wt_ref.at[t], wn_ref.at[t], local_sems.at[2]).start()

        def take(block, slot, arrivals=None, pass_on=None):
            for cp in arrivals or ():
                cp.wait_recv()
            load(sh_ref if arrivals is None else g_ref.at[_index_of(*block)], slot)
            if pass_on is not None:
                pass_on.start()
            place(slot, block)
            for u in range(1, 6):
                emit(6 * _index_of(*block) + u)

        near_x, near_y, far = [(*chip, c) for chip in chips]
        to_sibling = wcopy(0, me, sibling, src=win_ref.at[0])
        pass_x = wcopy(4, near_x, sibling, src=win_ref.at[1])
        pass_y = wcopy(5, near_y, sibling, src=win_ref.at[0])
        pass_d = wcopy(6, far, sibling, src=win_ref.at[0])
        relays = [relay(3, near_x, 0, near_y), relay(7, near_y, 1, near_x)]
        take(me, 0, pass_on=to_sibling)

        def piece_in(r):
            return pltpu.make_async_copy(xs_hbm.at[pl.ds(r * RB, RB)], xin_ref.at[r % 2], xin_sems.at[r % 2])

        def piece_out(r):
            return pltpu.make_async_copy(hout_ref.at[r % 2], h_hbm.at[pl.ds(r * RB, RB)], hout_sems.at[r % 2])

        piece_in(0).start()
        for r in range(S // RB):
            if r + 1 < S // RB:
                piece_in(r + 1).start()
            piece_in(r).wait()
            if r >= 2:
                piece_out(r - 2).wait()
            xv = xin_ref[r % 2]
            hout_ref[r % 2] = (xv * _rinv(xv) * gain_ref[...]).astype(BF16)
            piece_out(r).start()
        piece_out(S // RB - 2).wait()
        piece_out(S // RB - 1).wait()

        take(near_x, 1, [wcopy(1, near_x, me)], pass_x)
        relays[0].start()
        to_sibling.wait_send()
        take(near_y, 0, [wcopy(2, near_y, me)], pass_y)
        relays[1].start()
        small_passed = []
        for j, chip in enumerate(chips):
            scopy(1 + j, (*chip, c), me).wait_recv()
            cp = scopy(4 + j, (*chip, c), sibling)
            cp.start()
            small_passed.append(cp)
        pass_x.wait_send()
        take(sibling, 1, [wcopy(0, sibling, me)])
        pass_y.wait_send()
        take((*chips[0], 1 - c), 0, [wcopy(4, (*chips[0], 1 - c), me)])
        take((*chips[1], 1 - c), 1, [wcopy(5, (*chips[1], 1 - c), me)])
        take(far, 0, [relay(3, far, 0, near_y), relay(7, far, 1, near_x)], pass_d)
        take((*chips[2], 1 - c), 1, [wcopy(6, (*chips[2], 1 - c), me)])
        for t in range(0, n_tiles, 6):
            emit(t)
        scopy(0, sibling, me).wait_recv()
        for j, chip in enumerate(chips):
            scopy(4 + j, (*chip, 1 - c), me).wait_recv()
        for cp in first + small_passed + relays + [pass_d]:
            cp.wait_send()
        small_own.wait()
        pltpu.make_async_copy(wn_ref, wn_ref, local_sems.at[2]).wait()

    wn, _, gs, h = pl.pallas_call(
        body,
        in_specs=[HBM_SPEC] * 3 + [pl.BlockSpec(memory_space=pltpu.VMEM)], out_specs=[HBM_SPEC] * 4,
        out_shape=[_sds((n_tiles, D, 128), BF16), _sds((N_DEV,) + shard.shape, BF16), _sds((N_DEV,) + small.shape, small.dtype),
                   _sds(xs.shape, BF16)],
        scratch_shapes=[pltpu.VMEM((n_tiles, D, 128), BF16), pltpu.VMEM((2, D, cw), BF16), pltpu.VMEM((tr, 7 * 128), F32),
                        pltpu.VMEM((2, RB, D), F32), pltpu.VMEM((2, RB, D), BF16),
                        pltpu.SemaphoreType.DMA((19,)), pltpu.SemaphoreType.DMA((19,)), pltpu.SemaphoreType.DMA((3,)),
                        pltpu.SemaphoreType.DMA((2,)), pltpu.SemaphoreType.DMA((2,))],
        compiler_params=pltpu.CompilerParams(vmem_limit_bytes=48 * 1024 * 1024),
        name=name,
    )(shard, small, xs, gain)
    return wn, gs, h


def _mm(a, b, mode, out_dtype, *, tm, tn, tk, name, b_blocked=False, b_tiled=False, out_blocked=False, pair=None, phases=()):
    if mode == "nn":
        (m, k), dims = a.shape, NN
        a_blk, a_map = (tm, tk), (lambda i, j, kk: (i, kk))
        if b_blocked:
            assert b.shape[1] == k and b.shape[2] == tn and tk == k
            n = b.shape[0] * tn
            b_spec = pl.BlockSpec((None, tk, tn), lambda i, j, kk: (j, kk, 0))
        elif b_tiled:
            assert b.shape[1] == k and b.shape[2] == 128 and tn % 128 == 0
            n = b.shape[0] * 128
            b_spec = pl.BlockSpec((tn // 128, tk, 128), lambda i, j, kk: (j, kk, 0))
        else:
            assert b.shape[0] == k
            n = b.shape[1]
            b_spec = pl.BlockSpec((tk, tn), lambda i, j, kk: (kk, j))
    elif mode == "tn":
        (k, m), n, dims = a.shape, b.shape[1], TN
        assert b.shape[0] == k
        a_blk, a_map = (tk, tm), (lambda i, j, kk: (kk, i))
        b_spec = pl.BlockSpec((tk, tn), lambda i, j, kk: (kk, j))
    else:
        (m, k), dims = a.shape, NT
        a_blk, a_map = (tm, tk), (lambda i, j, kk: (i, kk))
        if b_blocked:
            assert b.shape[0] * b.shape[2] == k and b.shape[2] == tk
            n = b.shape[1]
            b_spec = pl.BlockSpec((None, tn, tk), lambda i, j, kk: (kk, j, 0))
        elif b_tiled:
            assert b.shape[0] * 128 == k and b.shape[2] == 128 and tk % 128 == 0
            n = b.shape[1]
            b_spec = pl.BlockSpec((tk // 128, tn, 128), lambda i, j, kk: (kk, j, 0))
        else:
            assert b.shape[1] == k
            n = b.shape[0]
            b_spec = pl.BlockSpec((tn, tk), lambda i, j, kk: (j, kk))
    assert m % tm == 0 and n % tn == 0 and k % tk == 0, (a.shape, b.shape, mode)
    nk = k // tk
    n_row_tiles = m // tm
    if out_blocked:
        out_shape, out_spec = _sds((n // tn, n_row_tiles * tm, tn), out_dtype), pl.BlockSpec((None, tm, tn), lambda i, j, kk: (j, i, 0))
    else:
        out_shape, out_spec = _sds((n_row_tiles * tm, n), out_dtype), pl.BlockSpec((tm, tn), lambda i, j, kk: (i, j))

    grid = (n_row_tiles, n // tn, nk)

    def body(a_ref, b_ref, o_ref, *rest):
        rhs = jnp.concatenate([b_ref[u] for u in range(b_ref.shape[0])], axis=1) if b_tiled else b_ref[...]
        p = _dot(a_ref[...], rhs, dims)
        if nk == 1:
            o_ref[...] = p.astype(out_dtype)
            if pair is not None:
                _send_to_sibling(p.astype(out_dtype), *rest)
        else:
            acc_ref, = rest
            kk = pl.program_id(2)

            @pl.when(kk == 0)
            def _():
                acc_ref[...] = p

            @pl.when(kk > 0)
            def _():
                acc_ref[...] += p

            @pl.when(kk == nk - 1)
            def _():
                o_ref[...] = acc_ref[...].astype(out_dtype)

    def _send_to_sibling(tile, pair_ref, stage_ref, send_sems, recv_sem):
        i, j = pl.program_id(0), pl.program_id(1)
        x, y, c = _place()
        blk = pair["block"](i, j)
        k = ((blk >> 2) ^ x) + 2 * (((blk >> 1) & 1) ^ y)
        ordinal = pair["ordinal"](i, j)

        def send(slot):
            return _rcopy(stage_ref.at[slot], pair["dst"](pair_ref, k, i, j), send_sems.at[slot], recv_sem.at[0], (x, y, 1 - c))

        @pl.when((blk & 1) != c)
        def _():
            slot = ordinal & 1

            @pl.when(ordinal >= 2)
            def _():
                send(slot).wait_send()

            stage_ref[slot] = tile
            send(slot).start()

        @pl.when((i == grid[0] - 1) & (j == grid[1] - 1))
        def _():
            send(0).wait_send()
            send(1).wait_send()
            _rcopy(pair_ref, pair_ref, send_sems.at[0], recv_sem.at[0], (x, y, 1 - c)).wait_recv()

    blocks = _nbytes(a_blk, a.dtype) + tk * tn * jnp.dtype(b.dtype).itemsize + _nbytes((tm, tn), out_dtype)
    out_specs, out_shapes, scratch = [out_spec], [out_shape], [] if nk == 1 else [pltpu.VMEM((tm, tn), F32)]
    scratch_bytes = _nbytes((tm, tn), F32) * (nk > 1)
    if pair is not None:
        assert nk == 1
        out_specs, out_shapes = out_specs + [HBM_SPEC], out_shapes + [pair["like"]]
        scratch = [pltpu.VMEM((2, tm, tn), out_dtype), pltpu.SemaphoreType.DMA((2,)), pltpu.SemaphoreType.DMA((1,))]
        scratch_bytes = 2 * _nbytes((tm, tn), out_dtype)
    outs, bufs = _hosted(
        body, name=name, grid=grid,
        in_specs=[pl.BlockSpec(a_blk, a_map), b_spec], out_specs=out_specs, out_shape=out_shapes, args=[a, b],
        scratch_shapes=scratch, block_bytes=blocks, scratch_bytes=scratch_bytes, phases=phases)
    return outs[0], outs[1:] + bufs


def _dw_blocks(a, g, own_so_far, pair_so_far, m0, rows, name, phases=()):
    cw, win = GLA_COLS // N_DEV, 896
    continues = own_so_far is not None

    def body(*refs):
        a_ref, g_hbm = refs[:2]
        own_ref, pair_ref, win_ref, stage_ref, win_sems, local_sems, send_sems, recv_sem = refs[-8:]
        j = pl.program_id(0)
        x, y, c = _place()
        dst_rows = pl.ds(m0, rows)

        def window(jj):
            src = g_hbm.at[:, pl.ds(pl.multiple_of(768 * jj, 128), win)]
            return pltpu.make_async_copy(src, win_ref.at[jj % 2], win_sems.at[jj % 2])

        pl.when(j == 0)(lambda: window(j).start())
        pl.when(j + 1 < N_DEV)(lambda: window(j + 1).start())
        window(j).wait()
        p = _dot(a_ref[...], win_ref[j % 2], TN)
        tile = pltpu.roll(p, lax.rem(win - 2 * j, win), 1)[:, :cw].astype(BF16)
        k = ((j >> 2) ^ x) + 2 * (((j >> 1) & 1) ^ y)

        def local(slot, kk):
            return pltpu.make_async_copy(stage_ref.at[slot], own_ref.at[kk, dst_rows], local_sems.at[slot])

        def send(slot, kk):
            return _rcopy(stage_ref.at[slot], pair_ref.at[kk, dst_rows], send_sems.at[slot], recv_sem.at[0], (x, y, 1 - c))

        slot = j & 1

        @pl.when(slot == c)
        def _():
            pl.when(j >= 2)(lambda: local(slot, k).wait())
            stage_ref[slot] = tile
            local(slot, k).start()

        @pl.when(slot != c)
        def _():
            pl.when(j >= 2)(lambda: send(slot, k).wait_send())
            stage_ref[slot] = tile
            send(slot, k).start()

        @pl.when(j == N_DEV - 1)
        def _():
            local(c, 0).wait()
            send(1 - c, 0).wait_send()
            arrived = pair_ref.at[:, dst_rows]
            _rcopy(arrived, arrived, send_sems.at[0], recv_sem.at[0], (x, y, 1 - c)).wait_recv()

    like = _sds((4, D, cw), BF16)
    return _hosted(
        body, name=name, grid=(N_DEV,),
        in_specs=[pl.BlockSpec((S, rows), lambda j: (0, m0 // rows)), HBM_SPEC] + [HBM_SPEC] * (2 * continues),
        out_specs=[HBM_SPEC, HBM_SPEC], out_shape=[like, like], args=[a, g] + [own_so_far, pair_so_far] * continues,
        scratch_shapes=[pltpu.VMEM((2, S, win), BF16), pltpu.VMEM((2, rows, cw), BF16), pltpu.SemaphoreType.DMA((2,)),
                        pltpu.SemaphoreType.DMA((2,)), pltpu.SemaphoreType.DMA((2,)), pltpu.SemaphoreType.DMA((1,))],
        block_bytes=S * rows * 2, scratch_bytes=2 * S * win * 2 + 2 * rows * win * 2 + 2 * rows * win * 4, phases=phases,
        continued={2: 0, 3: 1} if continues else None)


NT_ROWS_TM = 1024
NN_ROWS_TM = 512
NT_ROWS_SUB = 128


def _nt_rows(a, b, *, tk, name, row_ins, vec_ins, outs, tail, b_blocked=False, b_tiled=False, phases=()):
    m, k = a.shape
    tm, sub, nk = NT_ROWS_TM, NT_ROWS_SUB, k // tk
    n_sub = tm // sub
    if b_blocked:
        assert b.shape[0] * b.shape[2] == k and b.shape[2] == tk and b.shape[1] == D
        b_spec = pl.BlockSpec((None, D, tk), lambda i, kk: (kk, 0, 0))
    else:
        assert b_tiled and b.shape[0] * 128 == k and tk % 128 == 0 and b.shape[1] == D
        b_spec = pl.BlockSpec((tk // 128, D, 128), lambda i, kk: (kk, 0, 0))
    row_spec, vec_spec = pl.BlockSpec((tm, D), lambda i, kk: (i, 0)), pl.BlockSpec((1, D), lambda i, kk: (0, 0))
    n_row, n_vec, n_out = len(row_ins), len(vec_ins), len(outs)
    assert nk >= 2

    def body(a_ref, b_ref, *rest):
        row_hbm, vec_refs = rest[:n_row], rest[n_row:n_row + n_vec]
        out_refs = rest[n_row + n_vec:n_row + n_vec + n_out]
        acc_ref, row_sems = rest[n_row + n_vec + n_out], rest[-1]
        row_refs = rest[n_row + n_vec + n_out + 1:-1]
        rhs = jnp.concatenate([b_ref[u] for u in range(b_ref.shape[0])], axis=1) if b_tiled else b_ref[...]
        p = _dot(a_ref[...], rhs, NT)
        i, kk = pl.program_id(0), pl.program_id(1)

        def fetch(r, s):
            src = row_hbm[r].at[pl.ds(pl.multiple_of(i * tm + s * sub, sub), sub)]
            return pltpu.make_async_copy(src, row_refs[r].at[s % 2], row_sems.at[r, s % 2])

        @pl.when(kk == 0)
        def _():
            for r in range(n_row):
                fetch(r, 0).start()
            acc_ref[...] = p

        @pl.when(kk > 0)
        def _():
            acc_ref[...] += p

        @pl.when(kk == nk - 1)
        def _():
            for s in range(n_sub):
                for r in range(n_row):
                    if s + 1 < n_sub:
                        fetch(r, s + 1).start()
                    fetch(r, s).wait()
                rows = slice(s * sub, (s + 1) * sub)
                tail(acc_ref[rows, :], rows, (i == 0) if s == 0 else None, [ref[s % 2] for ref in row_refs], vec_refs, out_refs)

    out_specs = [row_spec if kind == "row" else vec_spec for kind, _ in outs]
    out_shape = [_sds((m, D) if kind == "row" else (1, D), dt) for kind, dt in outs]
    blocks = tm * tk * 2 + D * tk * 2 + sum(tm * D * jnp.dtype(dt).itemsize for kind, dt in outs if kind == "row")
    scratch = ([pltpu.VMEM((tm, D), F32)] + [pltpu.VMEM((2, sub, D), x.dtype) for x in row_ins]
               + [pltpu.SemaphoreType.DMA((n_row, 2))])
    return _hosted(body, name=name, grid=(m // tm, nk), in_specs=[pl.BlockSpec((tm, tk), lambda i, kk: (i, kk)), b_spec]
                   + [HBM_SPEC] * n_row + [vec_spec] * n_vec, out_specs=out_specs, out_shape=out_shape,
                   args=[a, b] + list(row_ins) + list(vec_ins), scratch_shapes=scratch,
                   block_bytes=blocks, scratch_bytes=tm * D * 4 + n_row * 2 * sub * D * 4, phases=phases)


def _nn_rows(a, b, *, name, row_ins, vec_ins, outs, tail):
    m, k = a.shape
    tm = NN_ROWS_TM
    assert b.shape == (k, D)
    row_spec, vec_spec = pl.BlockSpec((tm, D), lambda i: (i, 0)), pl.BlockSpec((1, D), lambda i: (0, 0))
    n_row, n_vec, n_out = len(row_ins), len(vec_ins), len(outs)

    def body(a_ref, b_ref, *rest):
        row_hbm, vec_refs = rest[:n_row], rest[n_row:n_row + n_vec]
        out_refs = rest[n_row + n_vec:n_row + n_vec + n_out]
        d_ref, row_sems = rest[n_row + n_vec + n_out], rest[-1]
        row_refs = rest[n_row + n_vec + n_out + 1:-1]
        i = pl.program_id(0)
        fetches = [pltpu.make_async_copy(row_hbm[r].at[pl.ds(pl.multiple_of(i * tm, tm), tm)], row_refs[r], row_sems.at[r])
                   for r in range(n_row)]
        for cp in fetches:
            cp.start()
        d_ref[...] = _dot(a_ref[...], b_ref[...])
        for cp in fetches:
            cp.wait()
        for s in range(tm // NT_ROWS_SUB):
            rows = slice(s * NT_ROWS_SUB, (s + 1) * NT_ROWS_SUB)
            tail(d_ref[rows, :], rows, (i == 0) if s == 0 else None, [ref[rows, :] for ref in row_refs], vec_refs, out_refs)

    out_specs = [row_spec if kind == "row" else vec_spec for kind, _ in outs]
    out_shape = [_sds((m, D) if kind == "row" else (1, D), dt) for kind, dt in outs]
    blocks = tm * k * 2 + k * D * 2 + sum(tm * D * jnp.dtype(dt).itemsize for kind, dt in outs if kind == "row")
    scratch = [pltpu.VMEM((tm, D), F32)] + [pltpu.VMEM((tm, D), x.dtype) for x in row_ins] + [pltpu.SemaphoreType.DMA((n_row,))]
    outs_, _ = _hosted(body, name=name, grid=(m // tm,), in_specs=[pl.BlockSpec((tm, k), lambda i: (i, 0)),
                                                                 pl.BlockSpec((k, D), lambda i: (0, 0))]
                       + [HBM_SPEC] * n_row + [vec_spec] * n_vec, out_specs=out_specs, out_shape=out_shape,
                       args=[a, b] + list(row_ins) + list(vec_ins), scratch_shapes=scratch,
                       block_bytes=blocks, scratch_bytes=(1 + n_row) * tm * D * 4)
    return outs_


def _vec_add(ref, value, first):
    if first is None:
        ref[...] += value
    else:
        pl.when(first)(lambda: ref.__setitem__(Ellipsis, value))
        pl.when(jnp.logical_not(first))(lambda: ref.__setitem__(Ellipsis, ref[...] + value))


RB = 256


def _row_spec(width):
    return pl.BlockSpec((RB, width), lambda i: (i, 0))


def _vec_spec(width):
    return pl.BlockSpec((1, width), lambda i: (0, 0))


def _rinv(x):
    return lax.rsqrt(jnp.mean(x * x, axis=-1, keepdims=True) + EPS)


def _norm_bwd(dyn, xhat, r):
    return r * (dyn - xhat * jnp.mean(dyn * xhat, axis=-1, keepdims=True))


def _colsum(x):
    return jnp.sum(x, axis=0, keepdims=True)


def _mid_fwd(x, y, npost, npre, phases=()):
    def body(x_ref, y_ref, po_ref, pr_ref, x1_ref, h1_ref):
        yv = y_ref[...]
        x1 = x_ref[...] + yv * _rinv(yv) * po_ref[...]
        x1_ref[...] = x1
        h1_ref[...] = (x1 * _rinv(x1) * pr_ref[...]).astype(BF16)

    return _hosted(body, name="mid_fwd", grid=(S // RB,), in_specs=[_row_spec(D), _row_spec(D), _vec_spec(D), _vec_spec(D)],
                   out_specs=[_row_spec(D), _row_spec(D)], out_shape=[_sds((S, D), F32), _sds((S, D), BF16)],
                   args=[x, y, npost, npre], block_bytes=RB * D * 14, phases=phases)


def _final_tail(yv, rows, first, row_vals, vec_refs, out_refs):
    (xv, tv), (po_ref,), (loss_ref, dx_ref, dy_ref, dpo_ref) = row_vals, vec_refs, out_refs
    r = _rinv(yv)
    yhat = yv * r
    err = xv + yhat * po_ref[...] - tv
    dx = err * (1.0 / D)
    dx_ref[rows, :] = dx
    dy_ref[rows, :] = _norm_bwd(dx * po_ref[...], yhat, r).astype(BF16)
    _vec_add(loss_ref, _colsum(err * err), first)
    _vec_add(dpo_ref, _colsum(dx * yhat), first)


def _mid_bwd_tail(dh, rows, first, row_vals, vec_refs, out_refs):
    (dx2, xv, yv), (pr_ref, po_ref), (dx1_ref, dy_ref, dpr_ref, dpo_ref) = row_vals, vec_refs, out_refs
    r = _rinv(xv)
    xhat = xv * r
    dx1 = dx2 + _norm_bwd(dh * pr_ref[...], xhat, r)
    dx1_ref[rows, :] = dx1
    ry = _rinv(yv)
    yhat = yv * ry
    dy_ref[rows, :] = _norm_bwd(dx1 * po_ref[...], yhat, ry).astype(BF16)
    _vec_add(dpr_ref, _colsum(dh * xhat), first)
    _vec_add(dpo_ref, _colsum(dx1 * yhat), first)


def _first_bwd_tail(dh, rows, first, row_vals, vec_refs, out_refs):
    (dx1, xv), (pr_ref,), (gx_ref, dpr_ref) = row_vals, vec_refs, out_refs
    r = _rinv(xv)
    xhat = xv * r
    gx_ref[rows, :] = dx1 + _norm_bwd(dh * pr_ref[...], xhat, r)
    _vec_add(dpr_ref, _colsum(dh * xhat), first)


GLA_RB = 256
GLA_CPB = GLA_RB // C


def _sigmoid(x):
    return 1.0 / (1.0 + jnp.exp(-x))


def _tri(strict):
    r = lax.broadcasted_iota(jnp.int32, (C, C), 0)
    c = lax.broadcasted_iota(jnp.int32, (C, C), 1)
    return jnp.where(c < r if strict else c <= r, 1.0, 0.0).astype(BF16)


def _tri_dot(tri, x):
    hi = x.astype(BF16)
    lo = (x - hi.astype(F32)).astype(BF16)
    return _dot(tri, hi) + _dot(tri, lo)


def _gla_gates(glr_b, w2, b, tri):
    z = _dot(glr_b, w2) + b
    log_a = (jnp.minimum(z, 0.0) - jnp.log(1.0 + jnp.exp(-jnp.abs(z)))) * (1.0 / GLA_TAU)
    bcum = _tri_dot(tri, log_a)
    b_end = jnp.sum(log_a, axis=0, keepdims=True)
    return z, jnp.exp(b_end - bcum), jnp.exp(b_end)


def _gla_fwd(proj, w2p, bgate, ogain, phases=()):
    def body(p_ref, w2_ref, b_ref, og_ref, y_ref, st_out_ref, st_ref):
        @pl.when(pl.program_id(0) == 0)
        def _():
            st_ref[...] = jnp.zeros_like(st_ref)

        tri = _tri(False)

        def chunk(ci, carry):
            rows = pl.ds(pl.multiple_of(ci * C, C), C)
            glr_b = p_ref[rows, LR0:LR0 + LRP].astype(BF16)
            _, ea_all, dec_all = _gla_gates(glr_b, w2_ref[...], b_ref[...], tri)
            for h in range(H):
                ea, dec = ea_all[:, h * DK:(h + 1) * DK], dec_all[:, h * DK:(h + 1) * DK]
                k_dec = (p_ref[rows, K0 + h * DK:K0 + (h + 1) * DK] * ea).astype(BF16)
                v_b = p_ref[rows, V0 + h * DV:V0 + (h + 1) * DV].astype(BF16)
                st = st_ref[h] * dec + _dot(v_b, k_dec, TN)
                st_ref[h] = st
                st_b = st.astype(BF16)
                st_out_ref[ci, h] = st_b
                q_b = (p_ref[rows, Q0 + h * DK:Q0 + (h + 1) * DK] * (DK ** -0.5)).astype(BF16)
                o = _dot(q_b, st_b, NT)
                on = o * _rinv(o)
                g = p_ref[rows, G0 + h * DV:G0 + (h + 1) * DV]
                y_ref[rows, h * DV:(h + 1) * DV] = (on * og_ref[:, h * DV:(h + 1) * DV] * (g * _sigmoid(g))).astype(BF16)
            return carry

        lax.fori_loop(0, GLA_CPB, chunk, 0, unroll=True)

    blocks = GLA_RB * GLA_PAD * 4 + GLA_RB * D * 2 + GLA_CPB * H * DV * DK * 2
    return _hosted(
        body, name="gla_fwd", grid=(S // GLA_RB,),
        in_specs=[pl.BlockSpec((GLA_RB, GLA_PAD), lambda i: (i, 0)),
                  pl.BlockSpec((LRP, H * DK), lambda i: (0, 0)),
                  pl.BlockSpec((1, H * DK), lambda i: (0, 0)),
                  pl.BlockSpec((1, H * DV), lambda i: (0, 0))],
        out_specs=[pl.BlockSpec((GLA_RB, H * DV), lambda i: (i, 0)),
                   pl.BlockSpec((GLA_CPB, H, DV, DK), lambda i: (i, 0, 0, 0))],
        out_shape=[_sds((S, H * DV), BF16), _sds((NC, H, DV, DK), BF16)],
        args=[proj, w2p, bgate, ogain], scratch_shapes=[pltpu.VMEM((H, DV, DK), F32)],
        block_bytes=blocks, scratch_bytes=H * DV * DK * 4, phases=phases)


def _gla_bwd(proj, dypre, states, w2p, bgate, ogain, phases=()):
    nb = S // GLA_RB

    def body(p_ref, dy_ref, st_blk_ref, st_prev_ref, w2_ref, b_ref, og_ref,
             dp_ref, dog_ref, dbg_ref, dw2_ref, r_ref):
        step = pl.program_id(0)

        @pl.when(step == 0)
        def _():
            r_ref[...] = jnp.zeros_like(r_ref)
            dog_ref[...] = jnp.zeros_like(dog_ref)
            dbg_ref[...] = jnp.zeros_like(dbg_ref)
            dw2_ref[...] = jnp.zeros_like(dw2_ref)

        tri = _tri(False)
        tri_strict = _tri(True)
        has_prev = jnp.where(step < nb - 1, 1.0, 0.0).astype(F32)

        def chunk(ci, st_prev_of):
            rows = pl.ds(ci * C if isinstance(ci, int) else pl.multiple_of(ci * C, C), C)
            glr_b = p_ref[rows, LR0:LR0 + LRP].astype(BF16)
            z, ea_all, dec_all = _gla_gates(glr_b, w2_ref[...], b_ref[...], tri)
            d_a, d_end = [], []
            for h in range(H):
                kcol = slice(h * DK, (h + 1) * DK)
                vcol = slice(h * DV, (h + 1) * DV)
                ea, dec = ea_all[:, kcol], dec_all[:, kcol]
                k_dec = p_ref[rows, K0 + h * DK:K0 + (h + 1) * DK] * ea
                k_dec_b = k_dec.astype(BF16)
                v_b = p_ref[rows, V0 + h * DV:V0 + (h + 1) * DV].astype(BF16)
                q_b = (p_ref[rows, Q0 + h * DK:Q0 + (h + 1) * DK] * (DK ** -0.5)).astype(BF16)
                st_b = st_blk_ref[ci, h]
                o = _dot(q_b, st_b, NT)
                rinv = _rinv(o)
                on = o * rinv
                g = p_ref[rows, G0 + h * DV:G0 + (h + 1) * DV]
                sg = _sigmoid(g)
                og = og_ref[:, vcol]
                dyp = dy_ref[rows, vcol]
                dp_ref[rows, G0 + h * DV:G0 + (h + 1) * DV] = (dyp * (on * og) * (sg * (1.0 + g * (1.0 - sg)))).astype(BF16)
                dpn = dyp * (g * sg)
                dog_ref[:, vcol] += _colsum(dpn * on)
                do_b = _norm_bwd(dpn * og, on, rinv).astype(BF16)
                gt = _dot(do_b, q_b, TN) + r_ref[h]
                gt_b = gt.astype(BF16)
                dp_ref[rows, Q0 + h * DK:Q0 + (h + 1) * DK] = (_dot(do_b, st_b) * (DK ** -0.5)).astype(BF16)
                dkd = _dot(v_b, gt_b)
                dp_ref[rows, V0 + h * DV:V0 + (h + 1) * DV] = _dot(k_dec_b, gt_b, NT).astype(BF16)
                dp_ref[rows, K0 + h * DK:K0 + (h + 1) * DK] = (dkd * ea).astype(BF16)
                d_a.append(dkd * k_dec)
                d_end.append(_colsum(gt * st_prev_of(h)) * dec)
                r_ref[h] = gt * dec
            dla = _tri_dot(tri_strict, jnp.concatenate(d_a, axis=1)) + jnp.concatenate(d_end, axis=1)
            dz = dla * (1.0 / GLA_TAU) * (1.0 - _sigmoid(z))
            dz_b = dz.astype(BF16)
            dbg_ref[...] += _colsum(dz)
            dw2_ref[...] += _dot(glr_b, dz_b, TN)
            dp_ref[rows, LR0:LR0 + LRP] = _dot(dz_b, w2_ref[...], NT).astype(BF16)

        def later_chunk(t, carry):
            ci = GLA_CPB - 1 - t
            chunk(ci, lambda h: st_blk_ref[ci - 1, h].astype(F32))
            return carry

        lax.fori_loop(0, GLA_CPB - 1, later_chunk, 0, unroll=True)
        chunk(0, lambda h: st_prev_ref[0, h].astype(F32) * has_prev)

    blocks = (GLA_RB * GLA_PAD * 4 + GLA_RB * D * 4 + (GLA_CPB + 1) * H * DV * DK * 2 + GLA_RB * GLA_PAD * 2)
    rev = lambda i: nb - 1 - i
    return _hosted(
        body, name="gla_bwd", grid=(nb,),
        in_specs=[pl.BlockSpec((GLA_RB, GLA_PAD), lambda i: (rev(i), 0)),
                  pl.BlockSpec((GLA_RB, H * DV), lambda i: (rev(i), 0)),
                  pl.BlockSpec((GLA_CPB, H, DV, DK), lambda i: (rev(i), 0, 0, 0)),
                  pl.BlockSpec((1, H, DV, DK), lambda i: (jnp.maximum(rev(i) * GLA_CPB - 1, 0), 0, 0, 0)),
                  pl.BlockSpec((LRP, H * DK), lambda i: (0, 0)),
                  pl.BlockSpec((1, H * DK), lambda i: (0, 0)),
                  pl.BlockSpec((1, H * DV), lambda i: (0, 0))],
        out_specs=[pl.BlockSpec((GLA_RB, GLA_PAD), lambda i: (rev(i), 0)),
                   pl.BlockSpec((1, H * DV), lambda i: (0, 0)),
                   pl.BlockSpec((1, H * DK), lambda i: (0, 0)),
                   pl.BlockSpec((LRP, H * DK), lambda i: (0, 0))],
        out_shape=[_sds((S, GLA_PAD), BF16), _sds((1, H * DV), F32), _sds((1, H * DK), F32), _sds((LRP, H * DK), F32)],
        args=[proj, dypre, states, states, w2p, bgate, ogain], scratch_shapes=[pltpu.VMEM((H, DV, DK), F32)],
        block_bytes=blocks, scratch_bytes=H * DV * DK * 4, phases=phases)


SGU_RB = 256
GELU_C = 0.7978845608028654
GELU_A = 0.044715


def _gelu(x):
    return 0.5 * x * (1.0 + jnp.tanh(GELU_C * (x + GELU_A * x * x * x)))


def _gelu_grad(x):
    t = jnp.tanh(GELU_C * (x + GELU_A * x * x * x))
    return 0.5 * (1.0 + t) + 0.5 * x * (1.0 - t * t) * (GELU_C * (1.0 + 3.0 * GELU_A * x * x))


def _causal_mask(transposed=False):
    i = lax.broadcasted_iota(jnp.int32, (SGU_BLOCK, SGU_BLOCK), 1 if transposed else 0)
    j = lax.broadcasted_iota(jnp.int32, (SGU_BLOCK, SGU_BLOCK), 0 if transposed else 1)
    return (i >= C) | (j < C)


def _layer_norm(vf, gain, bias):
    mu = jnp.mean(vf, axis=-1, keepdims=True)
    cen = vf - mu
    rstd = lax.rsqrt(jnp.mean(cen * cen, axis=-1, keepdims=True) + EPS)
    xhat = cen * rstd
    return xhat, rstd, xhat * gain + bias


def _sgu_fwd(proj, lng, lnb, ws, bsb, phases=()):
    def body(p_ref, g_ref, b_ref, ws_ref, bs_ref, o_ref):
        mask = _causal_mask()
        for n in range(SGU_RB // SGU_BLOCK):
            rows = slice(n * SGU_BLOCK, (n + 1) * SGU_BLOCK)
            _, _, vn = _layer_norm(_gelu(p_ref[rows, D:2 * D]), g_ref[...], b_ref[...])
            vn_b = vn.astype(BF16)
            for gi in range(SGU_G):
                cols = slice(gi * SGU_GD, (gi + 1) * SGU_GD)
                w = jnp.where(mask, ws_ref[gi], 0.0).astype(BF16)
                vs = _dot(w, vn_b[:, cols]) + bs_ref[gi]
                gate = p_ref[rows, 2 * D + gi * SGU_GD:2 * D + (gi + 1) * SGU_GD]
                o_ref[rows, cols] = (_gelu(p_ref[rows, cols]) * vs * (gate * _sigmoid(gate))).astype(BF16)

    blocks = SGU_RB * SGU_COLS * 4 + SGU_RB * D * 2 + SGU_G * SGU_BLOCK * (SGU_BLOCK + SGU_GD) * 4
    return _hosted(
        body, name="sgu_fwd", grid=(S // SGU_RB,),
        in_specs=[pl.BlockSpec((SGU_RB, SGU_COLS), lambda i: (i, 0)),
                  pl.BlockSpec((1, D), lambda i: (0, 0)), pl.BlockSpec((1, D), lambda i: (0, 0)),
                  pl.BlockSpec((SGU_G, SGU_BLOCK, SGU_BLOCK), lambda i: (0, 0, 0)),
                  pl.BlockSpec((SGU_G, SGU_BLOCK, SGU_GD), lambda i: (0, 0, 0))],
        out_specs=[pl.BlockSpec((SGU_RB, D), lambda i: (i, 0))], out_shape=[_sds((S, D), BF16)],
        args=[proj, lng, lnb, ws, bsb], block_bytes=blocks, phases=phases)


def _sgu_bwd(proj, dpre, lng, lnb, ws, wst, bsb, phases=()):
    nsteps = S // SGU_RB

    def body(p_ref, d_ref, g_ref, b_ref, ws_ref, wst_ref, bs_ref,
             dp_ref, dg_ref, db_ref, dws_ref, dbs_ref, dvn_ref, dvs_acc_ref):
        step = pl.program_id(0)

        @pl.when(step == 0)
        def _():
            dg_ref[...] = jnp.zeros_like(dg_ref)
            db_ref[...] = jnp.zeros_like(db_ref)
            dws_ref[...] = jnp.zeros_like(dws_ref)
            dvs_acc_ref[...] = jnp.zeros_like(dvs_acc_ref)

        mask = _causal_mask()
        maskt = _causal_mask(transposed=True)
        for n in range(SGU_RB // SGU_BLOCK):
            rows = slice(n * SGU_BLOCK, (n + 1) * SGU_BLOCK)
            v = p_ref[rows, D:2 * D]
            xhat, rstd, vn = _layer_norm(_gelu(v), g_ref[...], b_ref[...])
            vn_b = vn.astype(BF16)
            for gi in range(SGU_G):
                cols = slice(gi * SGU_GD, (gi + 1) * SGU_GD)
                w = jnp.where(mask, ws_ref[gi], 0.0).astype(BF16)
                wt = jnp.where(maskt, wst_ref[gi], 0.0).astype(BF16)
                vs = _dot(w, vn_b[:, cols]) + bs_ref[gi]
                u = p_ref[rows, cols]
                gate = p_ref[rows, 2 * D + gi * SGU_GD:2 * D + (gi + 1) * SGU_GD]
                sg = _sigmoid(gate)
                gu = _gelu(u)
                dpre_g = d_ref[rows, cols]
                t = dpre_g * (gate * sg)
                dp_ref[rows, cols] = (t * vs * _gelu_grad(u)).astype(BF16)
                dp_ref[rows, 2 * D + gi * SGU_GD:2 * D + (gi + 1) * SGU_GD] = (
                    dpre_g * gu * vs * (sg * (1.0 + gate * (1.0 - sg)))).astype(BF16)
                dvs = t * gu
                dvs_b = dvs.astype(BF16)
                dvs_acc_ref[:, cols] += dvs
                dws_ref[gi] += _dot(dvs_b, vn_b[:, cols], NT)
                dvn_ref[:, cols] = _dot(wt, dvs_b)
            dvn = dvn_ref[...]
            dg_ref[...] += _colsum(dvn * xhat)
            db_ref[...] += _colsum(dvn)
            dxh = dvn * g_ref[...]
            dvf = rstd * (dxh - jnp.mean(dxh, axis=-1, keepdims=True) - xhat * jnp.mean(dxh * xhat, axis=-1, keepdims=True))
            dp_ref[rows, D:2 * D] = (dvf * _gelu_grad(v)).astype(BF16)

        @pl.when(step == nsteps - 1)
        def _():
            lane = lax.broadcasted_iota(jnp.int32, (SGU_BLOCK, SGU_BLOCK), 1)
            out = jnp.zeros((SGU_BLOCK, SGU_BLOCK), F32)
            for gi in range(SGU_G):
                out = out + jnp.where(lane == gi, jnp.sum(dvs_acc_ref[:, gi * SGU_GD:(gi + 1) * SGU_GD], axis=1, keepdims=True), 0.0)
                dws_ref[gi] = jnp.where(mask, dws_ref[gi], 0.0)
            dbs_ref[...] = out

    blocks = SGU_RB * SGU_COLS * 6 + SGU_RB * D * 4 + SGU_G * SGU_BLOCK * (3 * SGU_BLOCK + SGU_GD) * 4
    const3 = lambda i: (0, 0, 0)
    return _hosted(
        body, name="sgu_bwd", grid=(nsteps,),
        in_specs=[pl.BlockSpec((SGU_RB, SGU_COLS), lambda i: (i, 0)),
                  pl.BlockSpec((SGU_RB, D), lambda i: (i, 0)),
                  pl.BlockSpec((1, D), lambda i: (0, 0)), pl.BlockSpec((1, D), lambda i: (0, 0)),
                  pl.BlockSpec((SGU_G, SGU_BLOCK, SGU_BLOCK), const3),
                  pl.BlockSpec((SGU_G, SGU_BLOCK, SGU_BLOCK), const3),
                  pl.BlockSpec((SGU_G, SGU_BLOCK, SGU_GD), const3)],
        out_specs=[pl.BlockSpec((SGU_RB, SGU_COLS), lambda i: (i, 0)),
                   pl.BlockSpec((1, D), lambda i: (0, 0)), pl.BlockSpec((1, D), lambda i: (0, 0)),
                   pl.BlockSpec((SGU_G, SGU_BLOCK, SGU_BLOCK), const3),
                   pl.BlockSpec((SGU_BLOCK, SGU_BLOCK), lambda i: (0, 0))],
        out_shape=[_sds((S, SGU_COLS), BF16), _sds((1, D), F32), _sds((1, D), F32),
                   _sds((SGU_G, SGU_BLOCK, SGU_BLOCK), F32), _sds((SGU_BLOCK, SGU_BLOCK), F32)],
        args=[proj, dpre, lng, lnb, ws, wst, bsb],
        scratch_shapes=[pltpu.VMEM((SGU_BLOCK, D), F32), pltpu.VMEM((SGU_BLOCK, D), F32)],
        block_bytes=blocks, scratch_bytes=2 * SGU_BLOCK * D * 4, phases=phases)


def _pair_sum(own, a, r0, nr, name, table=None):
    c = own.shape[2]
    tr = 256
    assert r0 % tr == 0 and nr % tr == 0

    def body(own_ref, sib_ref, o_ref):
        o_ref[...] = (own_ref[...].astype(F32) + sib_ref[...].astype(F32)).astype(BF16)

    own_map = ((lambda j, i: (1 + j, r0 // tr + i, 0)) if table is None else
               (lambda j, i, t: (t[1 + j], r0 // tr + i, 0)))
    cpad = -(-c // 128) * 128
    outs, _ = _hosted(
        body, name=name, grid=(3, nr // tr),
        in_specs=[pl.BlockSpec((None, tr, c), own_map),
                  pl.BlockSpec((None, tr, c), lambda j, i, *t: (1 + j, r0 // tr + i, 0))],
        out_specs=[pl.BlockSpec((None, tr, c), lambda j, i, *t: (j, i, 0))], out_shape=[_sds((3, nr, c), BF16)],
        args=[own, a], block_bytes=3 * tr * cpad * 2, table=table)
    return outs[0]


def _adamw_math(w, g, m, v):
    m = ADAM_B1 * m + (1.0 - ADAM_B1) * g
    v = ADAM_B2 * v + (1.0 - ADAM_B2) * (g * g)
    m_hat = m / (1.0 - ADAM_B1 ** ADAM_STEP)
    v_hat = v / (1.0 - ADAM_B2 ** ADAM_STEP)
    delta = -ADAM_LR * (m_hat / (jnp.sqrt(v_hat) + ADAM_EPS) + ADAM_WD * w)
    return delta, m, v


def _sum_adamw(own, a, b, w, m, v, *, name, phases=(), table=None):
    r, c = w.shape
    tr = 256

    def body(own_ref, sib_ref, far_ref, w_ref, m_ref, v_ref, g_ref, d_ref, nm_ref, nv_ref):
        g = own_ref[...].astype(F32) + sib_ref[...].astype(F32)
        for j in range(3):
            g = g + far_ref[j].astype(F32)
        g_ref[...] = g
        d_ref[...], nm_ref[...], nv_ref[...] = _adamw_math(w_ref[...], g, m_ref[...], v_ref[...])

    spec = pl.BlockSpec((tr, c), lambda i, *t: (i, 0))
    own_map = (lambda i: (0, i, 0)) if table is None else (lambda i, t: (t[0], i, 0))
    cpad = -(-c // 128) * 128
    return _hosted(
        body, name=name, grid=(r // tr,),
        in_specs=[pl.BlockSpec((None, tr, c), own_map), pl.BlockSpec((None, tr, c), lambda i, *t: (0, i, 0)),
                  pl.BlockSpec((3, tr, c), lambda i, *t: (0, i, 0)), spec, spec, spec],
        out_specs=[spec] * 4, out_shape=[_sds((r, c), F32)] * 4, args=[own, a, b, w, m, v],
        block_bytes=5 * tr * cpad * 2 + 7 * tr * cpad * 4, phases=phases, table=table)


def _sum_parts(parts, name):
    n, r, c = parts.shape

    def body(p_ref, o_ref):
        g = p_ref[0]
        for j in range(1, n):
            g = g + p_ref[j]
        o_ref[...] = g

    outs, _ = _hosted(body, name=name, grid=(1,), in_specs=[pl.BlockSpec((n, r, c), lambda i: (0, 0, 0))],
                      out_specs=[pl.BlockSpec((r, c), lambda i: (0, 0))], out_shape=[_sds((r, c), F32)], args=[parts],
                      block_bytes=(n + 1) * r * c * 4)
    return outs[0]


def _adamw(w, g, m, v, name):
    def body(w_ref, g_ref, m_ref, v_ref, d_ref, nm_ref, nv_ref):
        d_ref[...], nm_ref[...], nv_ref[...] = _adamw_math(w_ref[...], g_ref[...], m_ref[...], v_ref[...])

    spec = pl.BlockSpec(w.shape, lambda i: (0, 0))
    outs, _ = _hosted(body, name=name, grid=(1,), in_specs=[spec] * 4, out_specs=[spec] * 3, out_shape=[_sds(w.shape, F32)] * 3,
                      args=[w, g, m, v], block_bytes=7 * _nbytes(w.shape, F32))
    return outs


def _blocks_to_columns(g):
    n, r, c = g.shape
    return jnp.transpose(g, (1, 0, 2)).reshape(r, n * c)


def _pack(parts):
    return jnp.concatenate([p.reshape(-1) for p in parts]).reshape(-1, 128)


def _unpack(packed, like):
    flat, outs, off = packed.reshape(-1), [], 0
    for p in like:
        outs.append(flat[off:off + p.size].reshape(p.shape))
        off += p.size
    return outs


def kernel(x, norm_pre, norm_post, gla_w_in, gla_w_gate2, gla_b_gate, gla_o_gain, gla_w_out, sgu_w_in, sgu_ln_gain, sgu_ln_bias, sgu_w_spatial, sgu_b_spatial, sgu_w_out, loss_target, m_norm_pre, m_norm_post, m_gla_w_in, m_gla_w_gate2, m_gla_b_gate, m_gla_o_gain, m_gla_w_out, m_sgu_w_in, m_sgu_ln_gain, m_sgu_ln_bias, m_sgu_w_spatial, m_sgu_b_spatial, m_sgu_w_out, v_norm_pre, v_norm_post, v_gla_w_in, v_gla_w_gate2, v_gla_b_gate, v_gla_o_gain, v_gla_w_out, v_sgu_w_in, v_sgu_ln_gain, v_sgu_ln_bias, v_sgu_w_spatial, v_sgu_b_spatial, v_sgu_w_out):
    me = _index_of(*_place())
    x0 = x.reshape(S, D)
    tgt = loss_target.reshape(S, D)
    npre0, npre1 = norm_pre[0:1], norm_pre[1:2]
    npost0, npost1 = norm_post[0:1], norm_post[1:2]
    ws = sgu_w_spatial[0]
    wst = jnp.transpose(ws, (0, 2, 1))
    bsb = jnp.broadcast_to(sgu_b_spatial[0][:, :, None], (SGU_G, SGU_BLOCK, SGU_GD))
    W_ROWS = D // N_DEV
    IN_COLS_G, IN_COLS_S = GLA_COLS // N_DEV, SGU_COLS // N_DEV

    s_gwi, s_gwo = gla_w_in[0].astype(BF16), gla_w_out[0].astype(BF16)
    s_swi, s_swo = sgu_w_in[0].astype(BF16), sgu_w_out[0].astype(BF16)
    small = jnp.concatenate([jnp.pad(gla_w_gate2[0].reshape(4, 512), ((0, 4), (0, 0))),
                             jnp.pad(jnp.concatenate([sgu_ln_gain, sgu_ln_bias], axis=1), ((0, 7), (0, 0)))], axis=0)

    wg_in, g_small, h0 = _gather_first(s_gwi, small, x0, npre0, "gather_first")
    w2 =_blocks_to_columns(g_small[:, :4, :].reshape(N_DEV, LR, 128))
    w2p = jnp.pad(w2, ((0, LRP - LR), (0, 0))).astype(BF16)
    lng = g_small[:, 8, :256].reshape(1, D)
    lnb = g_small[:, 8, 256:].reshape(1, D)
    like_gwo, like_swi = _sds((N_DEV, W_ROWS, D), BF16), _sds((N_DEV, D, IN_COLS_S), BF16)

    proj0, (g_gwo, g_swi) = _mm(h0, wg_in, "nn", F32, tm=1024, tn=896, tk=D, name="gla_in", b_tiled=True, phases=[
        _Phase(like_gwo, None, [_gather_send(s_gwo, 0, W_ROWS)]),
        _Phase(like_swi, None, [_gather_send(s_swi, 0, 768, diagonal=False)])])
    (ypre0, states), (g_gwo, g_swi) = _gla_fwd(proj0, w2p, gla_b_gate, gla_o_gain, phases=[
        _Phase(like_gwo, g_gwo, [_gather_pass(0, W_ROWS)]),
        _Phase(like_swi, g_swi, [_gather_relay(0, 768), _gather_send(s_swi, 768, 512, diagonal=False)])])
    wg_out = g_gwo.reshape(D, D)
    y0, (g_swi,) = _mm(ypre0, wg_out, "nn", F32, tm=1024, tn=1024, tk=D, name="gla_out", phases=[
        _Phase(like_swi, g_swi, [_gather_pass(0, 768), _gather_relay(768, 512), _gather_send(s_swi, 1280, 512, diagonal=False)])])
    (x1, h1), (g_swi,) = _mid_fwd(x0, y0, npost0, npre1, phases=[
        _Phase(like_swi, g_swi, [_gather_pass(768, 512), _gather_relay(1280, 512), _gather_send(s_swi, 1792, 256, diagonal=False)])])
    g_swi, = _carry([_Phase(like_swi, g_swi, [_gather_pass(1280, 512), _gather_relay(1792, 256)])], "relay_sgu_w_in")
    g_swi, = _carry([_Phase(like_swi, g_swi, [_gather_pass(1792, 256)])], "pass_sgu_w_in")
    proj1, (g_swo,) = _mm(h1, g_swi, "nn", F32, tm=S, tn=IN_COLS_S, tk=D, name="sgu_in", b_blocked=True, phases=[
        _Phase(like_gwo, None, [_gather_send(s_swo, 0, W_ROWS)])])
    (pre1,), (g_swo,) = _sgu_fwd(proj1, lng, lnb, ws, bsb, phases=[_Phase(like_gwo, g_swo, [_gather_pass(0, W_ROWS)])])
    ws_out = g_swo.reshape(D, D)
    loss_cols, dx2, dy1, dnpost1 = _nn_rows(pre1, ws_out, name="sgu_out", row_ins=[x1, tgt], vec_ins=[npost1],
                                            outs=[("vec", F32), ("row", F32), ("row", BF16), ("vec", F32)], tail=_final_tail)
    loss_here = jnp.pad((0.5 * jnp.sum(loss_cols) / D).reshape(1, 1), ((0, 7), (0, 127)))

    like_b_out, like_b_swi = _sds((3, W_ROWS, D), BF16), _sds((3, D, IN_COLS_S), BF16)
    like_b_gwi = _sds((3, D, IN_COLS_G), BF16)
    row_pair = dict(like=_sds((4, W_ROWS, D), BF16), block=lambda i, j: i, ordinal=lambda i, j: i >> 1,
                    dst=lambda ref, k, i, j: ref.at[k])
    col_pair = dict(like=_sds((4, D, IN_COLS_S), BF16), block=lambda i, j: j, ordinal=lambda i, j: 4 * i + (j >> 1),
                    dst=lambda ref, k, i, j: ref.at[k, pl.ds(pl.multiple_of(i * 1024, 1024), 1024)])

    mine = _own_table()
    dws_out, (a_swo,) = _mm(pre1, dy1, "tn", BF16, tm=W_ROWS, tn=D, tk=S, name="sgu_out_dw", pair=row_pair)
    p_swo = dws_out.reshape(N_DEV, W_ROWS, D)
    t_swo = _pair_sum(p_swo, a_swo, 0, W_ROWS, "pair_sum_sgu_w_out", table=mine)
    dpre1, _ = _mm(dy1, ws_out, "nt", F32, tm=1024, tn=1024, tk=D, name="sgu_out_dx")
    (dproj1, dlng, dlnb, dwsp, dbsp), (b_swo,) = _sgu_bwd(proj1, dpre1, lng, lnb, ws, wst, bsb, phases=[
        _Phase(like_b_out, None, [_reduce_cross(t_swo, 0, 0, W_ROWS)])])
    p_swi, (a_swi,) = _mm(h1, dproj1, "tn", BF16, tm=1024, tn=IN_COLS_S, tk=S, name="sgu_in_dw", out_blocked=True, pair=col_pair)
    t_swi = _pair_sum(p_swi, a_swi, 0, D, "pair_sum_sgu_w_in", table=mine)
    (dx1, dy0, dnpre1, dnpost0), (b_swi,) = _nt_rows(
        dproj1, g_swi, tk=IN_COLS_S, name="sgu_in_dx", b_blocked=True, row_ins=[dx2, x1, y0], vec_ins=[npre1, npost0],
        outs=[("row", F32), ("row", BF16), ("vec", F32), ("vec", F32)], tail=_mid_bwd_tail, phases=[
            _Phase(like_b_swi, None, [_reduce_cross(t_swi, 0, 0, 1024)])])
    dwg_out, (a_gwo, b_swi) = _mm(ypre0, dy0, "tn", BF16, tm=W_ROWS, tn=D, tk=S, name="gla_out_dw", pair=row_pair, phases=[
        _Phase(like_b_swi, b_swi, [_reduce_cross(t_swi, 1024, 1024, 256)])])
    p_gwo = dwg_out.reshape(N_DEV, W_ROWS, D)
    t_gwo = _pair_sum(p_gwo, a_gwo, 0, W_ROWS, "pair_sum_gla_w_out", table=mine)
    dypre0, _ = _mm(dy0, wg_out, "nt", F32, tm=1024, tn=1024, tk=D, name="gla_out_dx")
    late = [dnpre1, dnpost1, dlng, dlnb, dwsp, jnp.transpose(dbsp[:, :SGU_G])]
    late_pack = _pack(late)
    (dproj0, dogain, dbgate, dw2), (b_swi, b_gwo, g_late) = _gla_bwd(proj0, dypre0, states, w2p, gla_b_gate, gla_o_gain, phases=[
        _Phase(like_b_swi, b_swi, [_reduce_cross(t_swi, 1280, 1280, 768)]),
        _Phase(like_b_out, None, [_reduce_cross(t_gwo, 0, 0, W_ROWS)]),
        _Phase(_sds((N_DEV,) + late_pack.shape, F32), None, [_gather_send(late_pack, 0, late_pack.shape[0])])])
    half = D // 2
    (own_gwi, a_gwi), (g_late,) = _dw_blocks(h0, dproj0, None, None, 0, half, "gla_in_dw_a", phases=[
        _Phase(_sds((N_DEV,) + late_pack.shape, F32), g_late, [_gather_pass(0, late_pack.shape[0])])])
    t_gwi_a = _pair_sum(own_gwi, a_gwi, 0, half, "pair_sum_gla_w_in_a")
    (own_gwi, a_gwi), (b_gwi,) = _dw_blocks(h0, dproj0, own_gwi, a_gwi, half, half, "gla_in_dw_b", phases=[
        _Phase(like_b_gwi, None, [_reduce_cross(t_gwi_a, 0, 0, 704)])])
    t_gwi_b = _pair_sum(own_gwi, a_gwi, half, half, "pair_sum_gla_w_in_b")
    (grad_x, dnpre0), (b_gwi,) = _nt_rows(
        dproj0, wg_in, tk=896, name="gla_in_dx", b_tiled=True, row_ins=[dx1, x0], vec_ins=[npre0],
        outs=[("row", F32), ("vec", F32)], tail=_first_bwd_tail, phases=[
            _Phase(like_b_gwi, b_gwi, [_reduce_cross(t_gwi_a, 704, 704, 320), _reduce_cross(t_gwi_b, 0, half, half)])])

    early = [dnpre0, dnpost0, dbgate, dogain, dw2[:LR], loss_here]
    early_pack = _pack(early)
    like_early = _sds((N_DEV,) + early_pack.shape, F32)
    (g_swo, d_swo, nm_swo, nv_swo), (g_early,) = _sum_adamw(
        p_swo, a_swo, b_swo, sgu_w_out[0], m_sgu_w_out[0], v_sgu_w_out[0], name="adamw_sgu_w_out", table=mine, phases=[
            _Phase(like_early, None, [_gather_send(early_pack, 0, early_pack.shape[0])])])
    (g_gwo_, d_gwo, nm_gwo, nv_gwo), (g_early,) = _sum_adamw(
        p_gwo, a_gwo, b_gwo, gla_w_out[0], m_gla_w_out[0], v_gla_w_out[0], name="adamw_gla_w_out", table=mine, phases=[
            _Phase(like_early, g_early, [_gather_pass(0, early_pack.shape[0])])])
    (g_swi_, d_swi, nm_swi, nv_swi), _ = _sum_adamw(
        p_swi, a_swi, b_swi, sgu_w_in[0], m_sgu_w_in[0], v_sgu_w_in[0], name="adamw_sgu_w_in", table=mine)
    (g_gwi_, d_gwi, nm_gwi, nv_gwi), _ = _sum_adamw(
        own_gwi, a_gwi, b_gwi, gla_w_in[0], m_gla_w_in[0], v_gla_w_in[0], name="adamw_gla_w_in")

    g_npre1, g_npost1, g_lng_full, g_lnb_full, g_wsp, g_bsp = _unpack(_sum_parts(g_late, "sum_late_small_grads"), late)
    g_npre0, g_npost0, g_bgate, g_ogain, g_w2_full, loss_all = _unpack(_sum_parts(g_early, "sum_early_small_grads"), early)
    loss = loss_all[0, 0]
    g_w2 = lax.dynamic_slice(g_w2_full, (0, me * 128), (LR, 128))
    g_lng = lax.dynamic_slice(g_lng_full, (0, me * 256), (1, 256))
    g_lnb = lax.dynamic_slice(g_lnb_full, (0, me * 256), (1, 256))
    small_g = [jnp.concatenate([g_npre0, g_npre1], 0), jnp.concatenate([g_npost0, g_npost1], 0), g_w2, g_bgate, g_ogain,
               g_lng, g_lnb, g_wsp, g_bsp]
    small_w = [norm_pre, norm_post, gla_w_gate2[0], gla_b_gate, gla_o_gain, sgu_ln_gain, sgu_ln_bias, sgu_w_spatial[0], sgu_b_spatial[0]]
    small_m = [m_norm_pre, m_norm_post, m_gla_w_gate2[0], m_gla_b_gate, m_gla_o_gain, m_sgu_ln_gain, m_sgu_ln_bias, m_sgu_w_spatial[0], m_sgu_b_spatial[0]]
    small_v = [v_norm_pre, v_norm_post, v_gla_w_gate2[0], v_gla_b_gate, v_gla_o_gain, v_sgu_ln_gain, v_sgu_ln_bias, v_sgu_w_spatial[0], v_sgu_b_spatial[0]]
    d_pack, nm_pack, nv_pack = _adamw(_pack(small_w), _pack(small_g), _pack(small_m), _pack(small_v), "adamw_small")

    out_like = [norm_pre, norm_post, gla_w_gate2, gla_b_gate, gla_o_gain, sgu_ln_gain, sgu_ln_bias, sgu_w_spatial, sgu_b_spatial]
    sg_ = [g.reshape(s.shape) for g, s in zip(small_g, out_like)]
    sd_, sm_, sv_ = (_unpack(pk, out_like) for pk in (d_pack, nm_pack, nv_pack))

    def assemble(small_list, w_in_g, w_out_g, w_in_s, w_out_s):
        npre_, npost_, w2_, bg_, og_, lg_, lb_, wsp_, bsp_ = small_list
        return [npre_, npost_, w_in_g[None], w2_, bg_, og_, w_out_g[None], w_in_s[None], lg_, lb_, wsp_, bsp_, w_out_s[None]]

    return (loss, grad_x.reshape(1, S, D),
            *assemble(sg_, g_gwi_, g_gwo_, g_swi_, g_swo),
            *assemble(sd_, d_gwi, d_gwo, d_swi, d_swo),
            *assemble(sm_, nm_gwi, nm_gwo, nm_swi, nm_swo),
            *assemble(sv_, nv_gwi, nv_gwo, nv_swi, nv_swo))
```

```python
import functools

import jax
import jax.numpy as jnp
from jax import lax
from jax.experimental import pallas as pl
from jax.experimental.pallas import tpu as pltpu

F32 = jnp.float32
BF16 = jnp.bfloat16

N_DEV = 8
S = 2048
D = 2048
H = 4
DK = 256
DV = 512
C = 64
NC = S // C
GLA_COLS = 6160
GLA_PAD = 6272
Q0, K0, V0, G0, LR0 = 0, 1024, 2048, 4096, 6144
LR = 16
LRP = 128
SGU_COLS = 6144
SGU_BLOCK = 128
SGU_G = 8
SGU_GD = 256
EPS = 1e-6
GLA_TAU = 16.0

ADAM_LR, ADAM_B1, ADAM_B2, ADAM_EPS, ADAM_WD, ADAM_STEP = 0.001, 0.9, 0.999, 1e-08, 0.01, 10

V7X_VMEM_BYTES = 64 * 1024 * 1024
VMEM_CEILING = V7X_VMEM_BYTES - 6 * 1024 * 1024
MESH = pl.DeviceIdType.MESH
HBM_SPEC = pl.BlockSpec(memory_space=pl.ANY)


def _sds(shape, dtype):
    return jax.ShapeDtypeStruct(tuple(shape), dtype)


def _nbytes(shape, dtype):
    n = 1
    for s in shape:
        n *= s
    return n * jnp.dtype(dtype).itemsize


def _dot(a, b, dims=(((1,), (0,)), ((), ())), precision=None):
    return lax.dot_general(a, b, dims, precision=precision, preferred_element_type=F32)


NN = (((1,), (0,)), ((), ()))
TN = (((0,), (0,)), ((), ()))
NT = (((1,), (1,)), ((), ()))


def _place():
    return lax.axis_index("x"), lax.axis_index("y"), lax.axis_index("c")


def _index_of(px, py, pc):
    return 4 * px + 2 * py + pc


def _chips(x, y):
    return [(1 - x, y), (x, 1 - y), (1 - x, 1 - y)]


def _rcopy(src, dst, send_sem, recv_sem, to):
    return pltpu.make_async_remote_copy(src_ref=src, dst_ref=dst, send_sem=send_sem, recv_sem=recv_sem,
                                        device_id=to, device_id_type=MESH)


class _Move:
    def __init__(self, ins, n_remote, make, stage=None):
        self.ins, self.n_remote, self.make, self.stage = list(ins), n_remote, make, stage

    def scratch(self):
        sems = [pltpu.SemaphoreType.DMA((self.n_remote,)), pltpu.SemaphoreType.DMA((self.n_remote,))]
        return sems if self.stage is None else sems + [pltpu.SemaphoreType.DMA((1,)), pltpu.VMEM(*self.stage)]

    def start(self, in_refs, buf, scratch):
        sends, _, local = self.make(in_refs, buf, scratch[0], scratch[1])
        if local is not None:
            pltpu.make_async_copy(local[0], scratch[3], scratch[2].at[0]).start()
        for cp in sends:
            cp.start()

    def finish(self, in_refs, buf, scratch):
        sends, arrivals, local = self.make(in_refs, buf, scratch[0], scratch[1])
        if local is not None:
            pltpu.make_async_copy(local[0], scratch[3], scratch[2].at[0]).wait()
            out = pltpu.make_async_copy(scratch[3], local[1], scratch[2].at[0])
            out.start()
        for cp in arrivals:
            cp.wait_recv()
        for cp in sends:
            cp.wait_send()
        if local is not None:
            out.wait()


class _Phase:
    def __init__(self, like, so_far, moves):
        self.like, self.so_far, self.moves = like, so_far, list(moves)


def _gather_send(shard, r0, nr, diagonal=True):
    def make(in_refs, g, ss, rs):
        sh, = in_refs
        x, y, c = _place()
        me = _index_of(x, y, c)
        rows = pl.ds(r0, nr)
        peers = [(x, y, 1 - c)] + [(px, py, c) for px, py in _chips(x, y)[:3 if diagonal else 2]]
        sends = [_rcopy(sh.at[rows], g.at[me, rows], ss.at[k], rs.at[k], p) for k, p in enumerate(peers)]
        arrivals = [_rcopy(sh.at[rows], g.at[_index_of(*p), rows], ss.at[k], rs.at[k], p) for k, p in enumerate(peers)]
        return sends, arrivals, (sh.at[rows], g.at[me, rows])

    return _Move([shard], 4 if diagonal else 3, make, stage=((nr, shard.shape[1]), shard.dtype))


def _gather_relay(r0, nr):
    def make(in_refs, g, ss, rs):
        x, y, c = _place()
        nx, ny, nd = [(px, py, c) for px, py in _chips(x, y)]
        first, second = pl.ds(r0, nr // 2), pl.ds(r0 + nr // 2, nr // 2)
        sends = [_rcopy(g.at[_index_of(*nx), first], g.at[_index_of(*nx), first], ss.at[0], rs.at[0], ny),
                 _rcopy(g.at[_index_of(*ny), second], g.at[_index_of(*ny), second], ss.at[1], rs.at[1], nx)]
        arrivals = [_rcopy(g.at[_index_of(*nx), first], g.at[_index_of(*nd), first], ss.at[0], rs.at[0], ny),
                    _rcopy(g.at[_index_of(*ny), second], g.at[_index_of(*nd), second], ss.at[1], rs.at[1], nx)]
        return sends, arrivals, None

    return _Move([], 2, make)


def _gather_pass(r0, nr):
    def make(in_refs, g, ss, rs):
        x, y, c = _place()
        rows = pl.ds(r0, nr)
        sends = [_rcopy(g.at[_index_of(px, py, c), rows], g.at[_index_of(px, py, c), rows], ss.at[j], rs.at[j], (x, y, 1 - c))
                 for j, (px, py) in enumerate(_chips(x, y))]
        arrivals = [_rcopy(g.at[_index_of(px, py, c), rows], g.at[_index_of(px, py, 1 - c), rows], ss.at[j], rs.at[j], (x, y, 1 - c))
                    for j, (px, py) in enumerate(_chips(x, y))]
        return sends, arrivals, None

    return _Move([], 3, make)


def _own_table():
    x, y, c = _place()
    return jnp.stack([_index_of(px, py, c) for px, py in [(x, y)] + _chips(x, y)]).astype(jnp.int32)


def _reduce_cross(sums, src_r0, dst_r0, nr):
    def make(in_refs, b, ss, rs):
        t, = in_refs
        x, y, c = _place()
        src, dst = pl.ds(src_r0, nr), pl.ds(dst_r0, nr)
        sends = [_rcopy(t.at[j, src], b.at[j, dst], ss.at[j], rs.at[j], (px, py, c)) for j, (px, py) in enumerate(_chips(x, y))]
        return sends, sends, None

    return _Move([sums], 3, make)


def _hosted(body, *, name, grid, in_specs, out_specs, out_shape, args, scratch_shapes=(), block_bytes, scratch_bytes=0,
            phases=(), table=None, continued=None):
    n_in, n_out, n_scr = len(args), len(out_shape), len(scratch_shapes)
    all_args, all_out_shape, sems, aliases, layout = list(args), list(out_shape), [], dict(continued or {}), []
    for j, ph in enumerate(phases):
        counts = []
        for mv in ph.moves:
            all_args += mv.ins
            counts.append(len(mv.ins))
            sems += mv.scratch()
        if ph.so_far is not None:
            aliases[len(all_args)] = n_out + j
            all_args.append(ph.so_far)
        layout.append((counts, ph.so_far is not None))
        all_out_shape.append(ph.like)
    n_extra_in = len(all_args) - n_in

    def wrapped(*refs):
        ins, pos = refs[:n_in], n_in
        move_ins = []
        for counts, continues in layout:
            per_move = []
            for cnt in counts:
                per_move.append(refs[pos:pos + cnt])
                pos += cnt
            pos += continues
            move_ins.append(per_move)
        outs = refs[pos:pos + n_out]
        bufs = refs[pos + n_out:pos + n_out + len(phases)]
        pos += n_out + len(phases)
        scratch = refs[pos:pos + n_scr]
        pos += n_scr
        move_sems = []
        for ph in phases:
            per_move = []
            for mv in ph.moves:
                count = len(mv.scratch())
                per_move.append(refs[pos:pos + count])
                pos += count
            move_sems.append(per_move)

        def each_move(fn_name):
            for ph, buf, per_in, per_sem in zip(phases, bufs, move_ins, move_sems):
                for mv, mv_in, mv_sem in zip(ph.moves, per_in, per_sem):
                    getattr(mv, fn_name)(mv_in, buf, mv_sem)

        if phases:
            first = functools.reduce(jnp.logical_and, [pl.program_id(a) == 0 for a in range(len(grid))])
            last = functools.reduce(jnp.logical_and, [pl.program_id(a) == grid[a] - 1 for a in range(len(grid))])
            pl.when(first)(lambda: each_move("start"))
        body(*ins, *outs, *scratch)
        if phases:
            pl.when(last)(lambda: each_move("finish"))

    all_args = [pltpu.with_memory_space_constraint(a, pltpu.HBM) for a in all_args]
    est = 2 * block_bytes + scratch_bytes
    params = pltpu.CompilerParams(dimension_semantics=("arbitrary",) * len(grid),
                                  vmem_limit_bytes=min(VMEM_CEILING, max(32 * 1024 * 1024, 2 * est)))
    all_in_specs, all_out_specs = list(in_specs) + [HBM_SPEC] * n_extra_in, list(out_specs) + [HBM_SPEC] * len(phases)
    if table is None:
        results = pl.pallas_call(
            wrapped, grid=grid, in_specs=all_in_specs, out_specs=all_out_specs, out_shape=all_out_shape,
            scratch_shapes=list(scratch_shapes) + sems, input_output_aliases=aliases, compiler_params=params, name=name,
        )(*all_args)
    else:
        results = pl.pallas_call(
            lambda table_ref, *refs: wrapped(*refs),
            grid_spec=pltpu.PrefetchScalarGridSpec(num_scalar_prefetch=1, grid=grid, in_specs=all_in_specs, out_specs=all_out_specs,
                                                   scratch_shapes=list(scratch_shapes) + sems),
            out_shape=all_out_shape, input_output_aliases={k + 1: v for k, v in aliases.items()}, compiler_params=params, name=name,
        )(table, *all_args)
    return list(results[:n_out]), list(results[n_out:])


class _Both:
    def __init__(self, copies):
        self.copies = copies

    def start(self):
        for cp in self.copies:
            cp.start()

    def wait_send(self):
        for cp in self.copies:
            cp.wait_send()

    def wait_recv(self):
        for cp in self.copies:
            cp.wait_recv()


def _carry(phases, name):
    def body(o_ref):
        o_ref[...] = jnp.zeros_like(o_ref)

    _, bufs = _hosted(body, name=name, grid=(1,), in_specs=[], out_specs=[pl.BlockSpec((8, 128), lambda i: (0, 0))],
                      out_shape=[_sds((8, 128), F32)], args=[], block_bytes=8 * 128 * 4, phases=phases)
    return bufs


def _gather_first(shard, small, xs, gain, name):
    cw, tr, n_tiles = shard.shape[1], 256, GLA_PAD // 128

    def body(sh_ref, sm_ref, xs_hbm, gain_ref, wn_ref, g_ref, gs_ref, h_hbm, wt_ref, win_ref, tmp_ref, xin_ref, hout_ref,
             send_sems, recv_sems, local_sems, xin_sems, hout_sems):
        x, y, c = _place()
        me, sibling = (x, y, c), (x, y, 1 - c)
        chips = _chips(x, y)

        def copy(base, out_ref, k, block, to, src=None):
            dst = out_ref.at[_index_of(*block)]
            return _rcopy(dst if src is None else src, dst, send_sems.at[base + k], recv_sems.at[base + k], to)

        icopy = functools.partial(copy, 0, g_ref)
        scopy = functools.partial(copy, 8, gs_ref)

        def wcopy(k, block, to, src=None):
            if k in (1, 2):
                return icopy(k, block, to, src)
            halves = []
            for part, sem in enumerate((k, {0: 15, 4: 16, 5: 17, 6: 18}[k])):
                rows = pl.ds(part * (D // 2), D // 2)
                dst = g_ref.at[_index_of(*block), rows]
                halves.append(_rcopy(dst if src is None else src.at[rows], dst, send_sems.at[sem], recv_sems.at[sem], to))
            return _Both(halves)

        def relay(k, block, half, to):
            rows = pl.ds(half * (D // 2), D // 2)
            ref = g_ref.at[_index_of(*block), rows]
            return _rcopy(ref, ref, send_sems.at[k], recv_sems.at[k], to)

        def load(src_ref, slot):
            cp = pltpu.make_async_copy(src_ref, win_ref.at[slot], local_sems.at[0])
            cp.start()
            cp.wait()

        def place(slot, block):
            b = _index_of(*block)

            def rows_chunk(r, carry):
                rows = pl.ds(pl.multiple_of(r * tr, tr), tr)
                tmp_ref[:, :cw] = win_ref[slot, rows, :].astype(F32)
                shifted = pltpu.roll(tmp_ref[...], 2 * b, 1)
                for u in range(7):
                    wt_ref[6 * b + u, rows, :] = (wt_ref[6 * b + u, rows, :].astype(F32) + shifted[:, 128 * u:128 * (u + 1)]).astype(BF16)
                return carry

            lax.fori_loop(0, D // tr, rows_chunk, 0)

        small_own = pltpu.make_async_copy(sm_ref, gs_ref.at[_index_of(*me)], local_sems.at[1])
        small_own.start()
        first = [wcopy(1 + j, me, (*chip, c), src=sh_ref) for j, chip in enumerate(chips[:2])]
        first += [scopy(0, me, sibling, src=sm_ref)] + [scopy(1 + j, me, (*chip, c), src=sm_ref) for j, chip in enumerate(chips)]
        for cp in first:
            cp.start()

        def clear(t, carry):
            wt_ref[t] = jnp.zeros((D, 128), BF16)
            return carry

        lax.fori_loop(0, n_tiles, clear, 0)
        tmp_ref[...] = jnp.zeros_like(tmp_ref)

        def emit(t):
            pltpu.make_async_copy(wt_ref.at[t], wn_ref.at[t], local_sems.at[2]).start()

        def take(block, slot, arrivals=None, pass_on=None):
            for cp in arrivals or ():
                cp.wait_recv()
            load(sh_ref if arrivals is None else g_ref.at[_index_of(*block)], slot)
            if pass_on is not None:
                pass_on.start()
            place(slot, block)
            for u in range(1, 6):
                emit(6 * _index_of(*block) + u)

        near_x, near_y, far = [(*chip, c) for chip in chips]
        to_sibling = wcopy(0, me, sibling, src=win_ref.at[0])
        pass_x = wcopy(4, near_x, sibling, src=win_ref.at[1])
        pass_y = wcopy(5, near_y, sibling, src=win_ref.at[0])
        pass_d = wcopy(6, far, sibling, src=win_ref.at[0])
        relays = [relay(3, near_x, 0, near_y), relay(7, near_y, 1, near_x)]
        take(me, 0, pass_on=to_sibling)

        def piece_in(r):
            return pltpu.make_async_copy(xs_hbm.at[pl.ds(r * RB, RB)], xin_ref.at[r % 2], xin_sems.at[r % 2])

        def piece_out(r):
            return pltpu.make_async_copy(hout_ref.at[r % 2], h_hbm.at[pl.ds(r * RB, RB)], hout_sems.at[r % 2])

        piece_in(0).start()
        for r in range(S // RB):
            if r + 1 < S // RB:
                piece_in(r + 1).start()
            piece_in(r).wait()
            if r >= 2:
                piece_out(r - 2).wait()
            xv = xin_ref[r % 2]
            hout_ref[r % 2] = (xv * _rinv(xv) * gain_ref[...]).astype(BF16)
            piece_out(r).start()
        piece_out(S // RB - 2).wait()
        piece_out(S // RB - 1).wait()

        take(near_x, 1, [wcopy(1, near_x, me)], pass_x)
        relays[0].start()
        to_sibling.wait_send()
        take(near_y, 0, [wcopy(2, near_y, me)], pass_y)
        relays[1].start()
        small_passed = []
        for j, chip in enumerate(chips):
            scopy(1 + j, (*chip, c), me).wait_recv()
            cp = scopy(4 + j, (*chip, c), sibling)
            cp.start()
            small_passed.append(cp)
        pass_x.wait_send()
        take(sibling, 1, [wcopy(0, sibling, me)])
        pass_y.wait_send()
        take((*chips[0], 1 - c), 0, [wcopy(4, (*chips[0], 1 - c), me)])
        take((*chips[1], 1 - c), 1, [wcopy(5, (*chips[1], 1 - c), me)])
        take(far, 0, [relay(3, far, 0, near_y), relay(7, far, 1, near_x)], pass_d)
        take((*chips[2], 1 - c), 1, [wcopy(6, (*chips[2], 1 - c), me)])
        for t in range(0, n_tiles, 6):
            emit(t)
        scopy(0, sibling, me).wait_recv()
        for j, chip in enumerate(chips):
            scopy(4 + j, (*chip, 1 - c), me).wait_recv()
        for cp in first + small_passed + relays + [pass_d]:
            cp.wait_send()
        small_own.wait()
        pltpu.make_async_copy(wn_ref, wn_ref, local_sems.at[2]).wait()

    wn, _, gs, h = pl.pallas_call(
        body,
        in_specs=[HBM_SPEC] * 3 + [pl.BlockSpec(memory_space=pltpu.VMEM)], out_specs=[HBM_SPEC] * 4,
        out_shape=[_sds((n_tiles, D, 128), BF16), _sds((N_DEV,) + shard.shape, BF16), _sds((N_DEV,) + small.shape, small.dtype),
                   _sds(xs.shape, BF16)],
        scratch_shapes=[pltpu.VMEM((n_tiles, D, 128), BF16), pltpu.VMEM((2, D, cw), BF16), pltpu.VMEM((tr, 7 * 128), F32),
                        pltpu.VMEM((2, RB, D), F32), pltpu.VMEM((2, RB, D), BF16),
                        pltpu.SemaphoreType.DMA((19,)), pltpu.SemaphoreType.DMA((19,)), pltpu.SemaphoreType.DMA((3,)),
                        pltpu.SemaphoreType.DMA((2,)), pltpu.SemaphoreType.DMA((2,))],
        compiler_params=pltpu.CompilerParams(vmem_limit_bytes=48 * 1024 * 1024),
        name=name,
    )(shard, small, xs, gain)
    return wn, gs, h


def _mm(a, b, mode, out_dtype, *, tm, tn, tk, name, b_blocked=False, b_tiled=False, out_blocked=False, pair=None, phases=()):
    if mode == "nn":
        (m, k), dims = a.shape, NN
        a_blk, a_map = (tm, tk), (lambda i, j, kk: (i, kk))
        if b_blocked:
            assert b.shape[1] == k and b.shape[2] == tn and tk == k
            n = b.shape[0] * tn
            b_spec = pl.BlockSpec((None, tk, tn), lambda i, j, kk: (j, kk, 0))
        elif b_tiled:
            assert b.shape[1] == k and b.shape[2] == 128 and tn % 128 == 0
            n = b.shape[0] * 128
            b_spec = pl.BlockSpec((tn // 128, tk, 128), lambda i, j, kk: (j, kk, 0))
        else:
            assert b.shape[0] == k
            n = b.shape[1]
            b_spec = pl.BlockSpec((tk, tn), lambda i, j, kk: (kk, j))
    elif mode == "tn":
        (k, m), n, dims = a.shape, b.shape[1], TN
        assert b.shape[0] == k
        a_blk, a_map = (tk, tm), (lambda i, j, kk: (kk, i))
        b_spec = pl.BlockSpec((tk, tn), lambda i, j, kk: (kk, j))
    else:
        (m, k), dims = a.shape, NT
        a_blk, a_map = (tm, tk), (lambda i, j, kk: (i, kk))
        if b_blocked:
            assert b.shape[0] * b.shape[2] == k and b.shape[2] == tk
            n = b.shape[1]
            b_spec = pl.BlockSpec((None, tn, tk), lambda i, j, kk: (kk, j, 0))
        elif b_tiled:
            assert b.shape[0] * 128 == k and b.shape[2] == 128 and tk % 128 == 0
            n = b.shape[1]
            b_spec = pl.BlockSpec((tk // 128, tn, 128), lambda i, j, kk: (kk, j, 0))
        else:
            assert b.shape[1] == k
            n = b.shape[0]
            b_spec = pl.BlockSpec((tn, tk), lambda i, j, kk: (j, kk))
    assert m % tm == 0 and n % tn == 0 and k % tk == 0, (a.shape, b.shape, mode)
    nk = k // tk
    n_row_tiles = m // tm
    if out_blocked:
        out_shape, out_spec = _sds((n // tn, n_row_tiles * tm, tn), out_dtype), pl.BlockSpec((None, tm, tn), lambda i, j, kk: (j, i, 0))
    else:
        out_shape, out_spec = _sds((n_row_tiles * tm, n), out_dtype), pl.BlockSpec((tm, tn), lambda i, j, kk: (i, j))

    grid = (n_row_tiles, n // tn, nk)

    def body(a_ref, b_ref, o_ref, *rest):
        rhs = jnp.concatenate([b_ref[u] for u in range(b_ref.shape[0])], axis=1) if b_tiled else b_ref[...]
        p = _dot(a_ref[...], rhs, dims)
        if nk == 1:
            o_ref[...] = p.astype(out_dtype)
            if pair is not None:
                _send_to_sibling(p.astype(out_dtype), *rest)
        else:
            acc_ref, = rest
            kk = pl.program_id(2)

            @pl.when(kk == 0)
            def _():
                acc_ref[...] = p

            @pl.when(kk > 0)
            def _():
                acc_ref[...] += p

            @pl.when(kk == nk - 1)
            def _():
                o_ref[...] = acc_ref[...].astype(out_dtype)

    def _send_to_sibling(tile, pair_ref, stage_ref, send_sems, recv_sem):
        i, j = pl.program_id(0), pl.program_id(1)
        x, y, c = _place()
        blk = pair["block"](i, j)
        k = ((blk >> 2) ^ x) + 2 * (((blk >> 1) & 1) ^ y)
        ordinal = pair["ordinal"](i, j)

        def send(slot):
            return _rcopy(stage_ref.at[slot], pair["dst"](pair_ref, k, i, j), send_sems.at[slot], recv_sem.at[0], (x, y, 1 - c))

        @pl.when((blk & 1) != c)
        def _():
            slot = ordinal & 1

            @pl.when(ordinal >= 2)
            def _():
                send(slot).wait_send()

            stage_ref[slot] = tile
            send(slot).start()

        @pl.when((i == grid[0] - 1) & (j == grid[1] - 1))
        def _():
            send(0).wait_send()
            send(1).wait_send()
            _rcopy(pair_ref, pair_ref, send_sems.at[0], recv_sem.at[0], (x, y, 1 - c)).wait_recv()

    blocks = _nbytes(a_blk, a.dtype) + tk * tn * jnp.dtype(b.dtype).itemsize + _nbytes((tm, tn), out_dtype)
    out_specs, out_shapes, scratch = [out_spec], [out_shape], [] if nk == 1 else [pltpu.VMEM((tm, tn), F32)]
    scratch_bytes = _nbytes((tm, tn), F32) * (nk > 1)
    if pair is not None:
        assert nk == 1
        out_specs, out_shapes = out_specs + [HBM_SPEC], out_shapes + [pair["like"]]
        scratch = [pltpu.VMEM((2, tm, tn), out_dtype), pltpu.SemaphoreType.DMA((2,)), pltpu.SemaphoreType.DMA((1,))]
        scratch_bytes = 2 * _nbytes((tm, tn), out_dtype)
    outs, bufs = _hosted(
        body, name=name, grid=grid,
        in_specs=[pl.BlockSpec(a_blk, a_map), b_spec], out_specs=out_specs, out_shape=out_shapes, args=[a, b],
        scratch_shapes=scratch, block_bytes=blocks, scratch_bytes=scratch_bytes, phases=phases)
    return outs[0], outs[1:] + bufs


def _dw_blocks(a, g, own_so_far, pair_so_far, m0, rows, name, phases=()):
    cw, win = GLA_COLS // N_DEV, 896
    continues = own_so_far is not None

    def body(*refs):
        a_ref, g_hbm = refs[:2]
        own_ref, pair_ref, win_ref, stage_ref, win_sems, local_sems, send_sems, recv_sem = refs[-8:]
        j = pl.program_id(0)
        x, y, c = _place()
        dst_rows = pl.ds(m0, rows)

        def window(jj):
            src = g_hbm.at[:, pl.ds(pl.multiple_of(768 * jj, 128), win)]
            return pltpu.make_async_copy(src, win_ref.at[jj % 2], win_sems.at[jj % 2])

        pl.when(j == 0)(lambda: window(j).start())
        pl.when(j + 1 < N_DEV)(lambda: window(j + 1).start())
        window(j).wait()
        p = _dot(a_ref[...], win_ref[j % 2], TN)
        tile = pltpu.roll(p, lax.rem(win - 2 * j, win), 1)[:, :cw].astype(BF16)
        k = ((j >> 2) ^ x) + 2 * (((j >> 1) & 1) ^ y)

        def local(slot, kk):
            return pltpu.make_async_copy(stage_ref.at[slot], own_ref.at[kk, dst_rows], local_sems.at[slot])

        def send(slot, kk):
            return _rcopy(stage_ref.at[slot], pair_ref.at[kk, dst_rows], send_sems.at[slot], recv_sem.at[0], (x, y, 1 - c))

        slot = j & 1

        @pl.when(slot == c)
        def _():
            pl.when(j >= 2)(lambda: local(slot, k).wait())
            stage_ref[slot] = tile
            local(slot, k).start()

        @pl.when(slot != c)
        def _():
            pl.when(j >= 2)(lambda: send(slot, k).wait_send())
            stage_ref[slot] = tile
            send(slot, k).start()

        @pl.when(j == N_DEV - 1)
        def _():
            local(c, 0).wait()
            send(1 - c, 0).wait_send()
            arrived = pair_ref.at[:, dst_rows]
            _rcopy(arrived, arrived, send_sems.at[0], recv_sem.at[0], (x, y, 1 - c)).wait_recv()

    like = _sds((4, D, cw), BF16)
    return _hosted(
        body, name=name, grid=(N_DEV,),
        in_specs=[pl.BlockSpec((S, rows), lambda j: (0, m0 // rows)), HBM_SPEC] + [HBM_SPEC] * (2 * continues),
        out_specs=[HBM_SPEC, HBM_SPEC], out_shape=[like, like], args=[a, g] + [own_so_far, pair_so_far] * continues,
        scratch_shapes=[pltpu.VMEM((2, S, win), BF16), pltpu.VMEM((2, rows, cw), BF16), pltpu.SemaphoreType.DMA((2,)),
                        pltpu.SemaphoreType.DMA((2,)), pltpu.SemaphoreType.DMA((2,)), pltpu.SemaphoreType.DMA((1,))],
        block_bytes=S * rows * 2, scratch_bytes=2 * S * win * 2 + 2 * rows * win * 2 + 2 * rows * win * 4, phases=phases,
        continued={2: 0, 3: 1} if continues else None)


NT_ROWS_TM = 1024
NN_ROWS_TM = 512
NT_ROWS_SUB = 64
NT_ROWS_RING = 3


def _nt_rows(a, b, *, tk, name, row_ins, vec_ins, outs, tail, b_blocked=False, b_tiled=False, phases=()):
    m, k = a.shape
    tm, sub, nk = NT_ROWS_TM, NT_ROWS_SUB, k // tk
    n_sub = tm // sub
    if b_blocked:
        assert b.shape[0] * b.shape[2] == k and b.shape[2] == tk and b.shape[1] == D
        b_blk = (D, tk)
    else:
        assert b_tiled and b.shape[0] * 128 == k and tk % 128 == 0 and b.shape[1] == D
        b_blk = (tk // 128, D, 128)
    row_spec, vec_spec = pl.BlockSpec((tm, D), lambda i, kk: (i, 0)), pl.BlockSpec((1, D), lambda i, kk: (0, 0))
    n_row, n_vec, n_out = len(row_ins), len(vec_ins), len(outs)
    n_steps = (m // tm) * nk
    assert nk >= 2 and n_steps >= NT_ROWS_RING

    def body(a_ref, b_hbm, *rest):
        row_hbm, vec_refs = rest[:n_row], rest[n_row:n_row + n_vec]
        out_refs = rest[n_row + n_vec:n_row + n_vec + n_out]
        acc_ref, row_sems, b_ring, b_sems = rest[n_row + n_vec + n_out], rest[-3], rest[-2], rest[-1]
        row_refs = rest[n_row + n_vec + n_out + 1:-3]
        i, kk = pl.program_id(0), pl.program_id(1)
        step = i * nk + kk

        def weight(s):
            blk, slot = lax.rem(s, nk), lax.rem(s, NT_ROWS_RING)
            src = b_hbm.at[blk] if b_blocked else b_hbm.at[pl.ds(blk * (tk // 128), tk // 128)]
            return pltpu.make_async_copy(src, b_ring.at[slot], b_sems.at[slot])

        @pl.when(step == 0)
        def _():
            for u in range(NT_ROWS_RING):
                weight(step + u).start()

        weight(step).wait()
        slot = lax.rem(step, NT_ROWS_RING)
        rhs = jnp.concatenate([b_ring[slot, u] for u in range(tk // 128)], axis=1) if b_tiled else b_ring[slot]
        p = _dot(a_ref[...], rhs, NT)

        def fetch(r, s):
            src = row_hbm[r].at[pl.ds(pl.multiple_of(i * tm + s * sub, sub), sub)]
            return pltpu.make_async_copy(src, row_refs[r].at[s % 2], row_sems.at[r, s % 2])

        @pl.when(kk == 0)
        def _():
            for r in range(n_row):
                fetch(r, 0).start()
            acc_ref[...] = p

        @pl.when(kk > 0)
        def _():
            acc_ref[...] += p

        @pl.when(kk == nk - 1)
        def _():
            for s in range(n_sub):
                for r in range(n_row):
                    if s + 1 < n_sub:
                        fetch(r, s + 1).start()
                    fetch(r, s).wait()
                rows = slice(s * sub, (s + 1) * sub)
                tail(acc_ref[rows, :], rows, (i == 0) if s == 0 else None, [ref[s % 2] for ref in row_refs], vec_refs, out_refs)

        pl.when(step + NT_ROWS_RING < n_steps)(lambda: weight(step + NT_ROWS_RING).start())

    out_specs = [row_spec if kind == "row" else vec_spec for kind, _ in outs]
    out_shape = [_sds((m, D) if kind == "row" else (1, D), dt) for kind, dt in outs]
    blocks = tm * tk * 2 + sum(tm * D * jnp.dtype(dt).itemsize for kind, dt in outs if kind == "row")
    scratch = ([pltpu.VMEM((tm, D), F32)] + [pltpu.VMEM((2, sub, D), x.dtype) for x in row_ins]
               + [pltpu.SemaphoreType.DMA((n_row, 2)), pltpu.VMEM((NT_ROWS_RING,) + b_blk, b.dtype),
                  pltpu.SemaphoreType.DMA((NT_ROWS_RING,))])
    return _hosted(body, name=name, grid=(m // tm, nk), in_specs=[pl.BlockSpec((tm, tk), lambda i, kk: (i, kk)), HBM_SPEC]
                   + [HBM_SPEC] * n_row + [vec_spec] * n_vec, out_specs=out_specs, out_shape=out_shape,
                   args=[a, b] + list(row_ins) + list(vec_ins), scratch_shapes=scratch, block_bytes=blocks,
                   scratch_bytes=tm * D * 4 + n_row * 2 * sub * D * 4 + NT_ROWS_RING * D * tk * 2, phases=phases)


def _nn_rows(a, b, *, name, row_ins, vec_ins, outs, tail):
    m, k = a.shape
    tm = NN_ROWS_TM
    assert b.shape == (k, D)
    row_spec, vec_spec = pl.BlockSpec((tm, D), lambda i: (i, 0)), pl.BlockSpec((1, D), lambda i: (0, 0))
    n_row, n_vec, n_out = len(row_ins), len(vec_ins), len(outs)

    def body(a_ref, b_ref, *rest):
        row_hbm, vec_refs = rest[:n_row], rest[n_row:n_row + n_vec]
        out_refs = rest[n_row + n_vec:n_row + n_vec + n_out]
        d_ref, row_sems = rest[n_row + n_vec + n_out], rest[-1]
        row_refs = rest[n_row + n_vec + n_out + 1:-1]
        i = pl.program_id(0)
        fetches = [pltpu.make_async_copy(row_hbm[r].at[pl.ds(pl.multiple_of(i * tm, tm), tm)], row_refs[r], row_sems.at[r])
                   for r in range(n_row)]
        for cp in fetches:
            cp.start()
        d_ref[...] = _dot(a_ref[...], b_ref[...])
        for cp in fetches:
            cp.wait()
        for s in range(tm // NT_ROWS_SUB):
            rows = slice(s * NT_ROWS_SUB, (s + 1) * NT_ROWS_SUB)
            tail(d_ref[rows, :], rows, (i == 0) if s == 0 else None, [ref[rows, :] for ref in row_refs], vec_refs, out_refs)

    out_specs = [row_spec if kind == "row" else vec_spec for kind, _ in outs]
    out_shape = [_sds((m, D) if kind == "row" else (1, D), dt) for kind, dt in outs]
    blocks = tm * k * 2 + k * D * 2 + sum(tm * D * jnp.dtype(dt).itemsize for kind, dt in outs if kind == "row")
    scratch = [pltpu.VMEM((tm, D), F32)] + [pltpu.VMEM((tm, D), x.dtype) for x in row_ins] + [pltpu.SemaphoreType.DMA((n_row,))]
    outs_, _ = _hosted(body, name=name, grid=(m // tm,), in_specs=[pl.BlockSpec((tm, k), lambda i: (i, 0)),
                                                                 pl.BlockSpec((k, D), lambda i: (0, 0))]
                       + [HBM_SPEC] * n_row + [vec_spec] * n_vec, out_specs=out_specs, out_shape=out_shape,
                       args=[a, b] + list(row_ins) + list(vec_ins), scratch_shapes=scratch,
                       block_bytes=blocks, scratch_bytes=(1 + n_row) * tm * D * 4)
    return outs_


def _vec_add(ref, value, first):
    if first is None:
        ref[...] += value
    else:
        pl.when(first)(lambda: ref.__setitem__(Ellipsis, value))
        pl.when(jnp.logical_not(first))(lambda: ref.__setitem__(Ellipsis, ref[...] + value))


RB = 256


def _row_spec(width):
    return pl.BlockSpec((RB, width), lambda i: (i, 0))


def _vec_spec(width):
    return pl.BlockSpec((1, width), lambda i: (0, 0))


def _rinv(x):
    return lax.rsqrt(jnp.mean(x * x, axis=-1, keepdims=True) + EPS)


def _norm_bwd(dyn, xhat, r):
    return r * (dyn - xhat * jnp.mean(dyn * xhat, axis=-1, keepdims=True))


def _colsum(x):
    return jnp.sum(x, axis=0, keepdims=True)


def _mid_fwd(x, y, npost, npre, phases=()):
    def body(x_ref, y_ref, po_ref, pr_ref, x1_ref, h1_ref):
        yv = y_ref[...]
        x1 = x_ref[...] + yv * _rinv(yv) * po_ref[...]
        x1_ref[...] = x1
        h1_ref[...] = (x1 * _rinv(x1) * pr_ref[...]).astype(BF16)

    return _hosted(body, name="mid_fwd", grid=(S // RB,), in_specs=[_row_spec(D), _row_spec(D), _vec_spec(D), _vec_spec(D)],
                   out_specs=[_row_spec(D), _row_spec(D)], out_shape=[_sds((S, D), F32), _sds((S, D), BF16)],
                   args=[x, y, npost, npre], block_bytes=RB * D * 14, phases=phases)


def _final_tail(yv, rows, first, row_vals, vec_refs, out_refs):
    (xv, tv), (po_ref,), (loss_ref, dx_ref, dy_ref, dpo_ref) = row_vals, vec_refs, out_refs
    r = _rinv(yv)
    yhat = yv * r
    err = xv + yhat * po_ref[...] - tv
    dx = err * (1.0 / D)
    dx_ref[rows, :] = dx
    dy_ref[rows, :] = _norm_bwd(dx * po_ref[...], yhat, r).astype(BF16)
    _vec_add(loss_ref, _colsum(err * err), first)
    _vec_add(dpo_ref, _colsum(dx * yhat), first)


def _mid_bwd_tail(dh, rows, first, row_vals, vec_refs, out_refs):
    (dx2, xv, yv), (pr_ref, po_ref), (dx1_ref, dy_ref, dpr_ref, dpo_ref) = row_vals, vec_refs, out_refs
    r = _rinv(xv)
    xhat = xv * r
    dx1 = dx2 + _norm_bwd(dh * pr_ref[...], xhat, r)
    dx1_ref[rows, :] = dx1
    ry = _rinv(yv)
    yhat = yv * ry
    dy_ref[rows, :] = _norm_bwd(dx1 * po_ref[...], yhat, ry).astype(BF16)
    _vec_add(dpr_ref, _colsum(dh * xhat), first)
    _vec_add(dpo_ref, _colsum(dx1 * yhat), first)


def _first_bwd_tail(dh, rows, first, row_vals, vec_refs, out_refs):
    (dx1, xv), (pr_ref,), (gx_ref, dpr_ref) = row_vals, vec_refs, out_refs
    r = _rinv(xv)
    xhat = xv * r
    gx_ref[rows, :] = dx1 + _norm_bwd(dh * pr_ref[...], xhat, r)
    _vec_add(dpr_ref, _colsum(dh * xhat), first)


GLA_RB = 256
GLA_CPB = GLA_RB // C


def _sigmoid(x):
    return 1.0 / (1.0 + jnp.exp(-x))


def _tri(strict):
    r = lax.broadcasted_iota(jnp.int32, (C, C), 0)
    c = lax.broadcasted_iota(jnp.int32, (C, C), 1)
    return jnp.where(c < r if strict else c <= r, 1.0, 0.0).astype(BF16)


def _tri_dot(tri, x):
    hi = x.astype(BF16)
    lo = (x - hi.astype(F32)).astype(BF16)
    return _dot(tri, hi) + _dot(tri, lo)


def _gla_gates(glr_b, w2, b, tri):
    z = _dot(glr_b, w2) + b
    log_a = (jnp.minimum(z, 0.0) - jnp.log(1.0 + jnp.exp(-jnp.abs(z)))) * (1.0 / GLA_TAU)
    bcum = _tri_dot(tri, log_a)
    b_end = jnp.sum(log_a, axis=0, keepdims=True)
    return z, jnp.exp(b_end - bcum), jnp.exp(b_end)


def _gla_fwd(proj, w2p, bgate, ogain, phases=()):
    def body(p_ref, w2_ref, b_ref, og_ref, y_ref, st_out_ref, st_ref):
        @pl.when(pl.program_id(0) == 0)
        def _():
            st_ref[...] = jnp.zeros_like(st_ref)

        tri = _tri(False)

        def chunk(ci, carry):
            rows = pl.ds(pl.multiple_of(ci * C, C), C)
            glr_b = p_ref[rows, LR0:LR0 + LRP].astype(BF16)
            _, ea_all, dec_all = _gla_gates(glr_b, w2_ref[...], b_ref[...], tri)
            for h in range(H):
                ea, dec = ea_all[:, h * DK:(h + 1) * DK], dec_all[:, h * DK:(h + 1) * DK]
                k_dec = (p_ref[rows, K0 + h * DK:K0 + (h + 1) * DK] * ea).astype(BF16)
                v_b = p_ref[rows, V0 + h * DV:V0 + (h + 1) * DV].astype(BF16)
                st = st_ref[h] * dec + _dot(v_b, k_dec, TN)
                st_ref[h] = st
                st_b = st.astype(BF16)
                st_out_ref[ci, h] = st_b
                q_b = (p_ref[rows, Q0 + h * DK:Q0 + (h + 1) * DK] * (DK ** -0.5)).astype(BF16)
                o = _dot(q_b, st_b, NT)
                on = o * _rinv(o)
                g = p_ref[rows, G0 + h * DV:G0 + (h + 1) * DV]
                y_ref[rows, h * DV:(h + 1) * DV] = (on * og_ref[:, h * DV:(h + 1) * DV] * (g * _sigmoid(g))).astype(BF16)
            return carry

        lax.fori_loop(0, GLA_CPB, chunk, 0, unroll=True)

    blocks = GLA_RB * GLA_PAD * 4 + GLA_RB * D * 2 + GLA_CPB * H * DV * DK * 2
    return _hosted(
        body, name="gla_fwd", grid=(S // GLA_RB,),
        in_specs=[pl.BlockSpec((GLA_RB, GLA_PAD), lambda i: (i, 0)),
                  pl.BlockSpec((LRP, H * DK), lambda i: (0, 0)),
                  pl.BlockSpec((1, H * DK), lambda i: (0, 0)),
                  pl.BlockSpec((1, H * DV), lambda i: (0, 0))],
        out_specs=[pl.BlockSpec((GLA_RB, H * DV), lambda i: (i, 0)),
                   pl.BlockSpec((GLA_CPB, H, DV, DK), lambda i: (i, 0, 0, 0))],
        out_shape=[_sds((S, H * DV), BF16), _sds((NC, H, DV, DK), BF16)],
        args=[proj, w2p, bgate, ogain], scratch_shapes=[pltpu.VMEM((H, DV, DK), F32)],
        block_bytes=blocks, scratch_bytes=H * DV * DK * 4, phases=phases)


def _gla_bwd(proj, dypre, states, w2p, bgate, ogain, phases=()):
    nb = S // GLA_RB

    def body(p_ref, dy_ref, st_blk_ref, st_prev_ref, w2_ref, b_ref, og_ref,
             dp_ref, dog_ref, dbg_ref, dw2_ref, r_ref):
        step = pl.program_id(0)

        @pl.when(step == 0)
        def _():
            r_ref[...] = jnp.zeros_like(r_ref)
            dog_ref[...] = jnp.zeros_like(dog_ref)
            dbg_ref[...] = jnp.zeros_like(dbg_ref)
            dw2_ref[...] = jnp.zeros_like(dw2_ref)

        tri = _tri(False)
        tri_strict = _tri(True)
        has_prev = jnp.where(step < nb - 1, 1.0, 0.0).astype(F32)

        def chunk(ci, st_prev_of):
            rows = pl.ds(ci * C if isinstance(ci, int) else pl.multiple_of(ci * C, C), C)
            glr_b = p_ref[rows, LR0:LR0 + LRP].astype(BF16)
            z, ea_all, dec_all = _gla_gates(glr_b, w2_ref[...], b_ref[...], tri)
            d_a, d_end = [], []
            for h in range(H):
                kcol = slice(h * DK, (h + 1) * DK)
                vcol = slice(h * DV, (h + 1) * DV)
                ea, dec = ea_all[:, kcol], dec_all[:, kcol]
                k_dec = p_ref[rows, K0 + h * DK:K0 + (h + 1) * DK] * ea
                k_dec_b = k_dec.astype(BF16)
                v_b = p_ref[rows, V0 + h * DV:V0 + (h + 1) * DV].astype(BF16)
                q_b = (p_ref[rows, Q0 + h * DK:Q0 + (h + 1) * DK] * (DK ** -0.5)).astype(BF16)
                st_b = st_blk_ref[ci, h]
                o = _dot(q_b, st_b, NT)
                rinv = _rinv(o)
                on = o * rinv
                g = p_ref[rows, G0 + h * DV:G0 + (h + 1) * DV]
                sg = _sigmoid(g)
                og = og_ref[:, vcol]
                dyp = dy_ref[rows, vcol]
                dp_ref[rows, G0 + h * DV:G0 + (h + 1) * DV] = (dyp * (on * og) * (sg * (1.0 + g * (1.0 - sg)))).astype(BF16)
                dpn = dyp * (g * sg)
                dog_ref[:, vcol] += _colsum(dpn * on)
                do_b = _norm_bwd(dpn * og, on, rinv).astype(BF16)
                gt = _dot(do_b, q_b, TN) + r_ref[h]
                gt_b = gt.astype(BF16)
                dp_ref[rows, Q0 + h * DK:Q0 + (h + 1) * DK] = (_dot(do_b, st_b) * (DK ** -0.5)).astype(BF16)
                dkd = _dot(v_b, gt_b)
                dp_ref[rows, V0 + h * DV:V0 + (h + 1) * DV] = _dot(k_dec_b, gt_b, NT).astype(BF16)
                dp_ref[rows, K0 + h * DK:K0 + (h + 1) * DK] = (dkd * ea).astype(BF16)
                d_a.append(dkd * k_dec)
                d_end.append(_colsum(gt * st_prev_of(h)) * dec)
                r_ref[h] = gt * dec
            dla = _tri_dot(tri_strict, jnp.concatenate(d_a, axis=1)) + jnp.concatenate(d_end, axis=1)
            dz = dla * (1.0 / GLA_TAU) * (1.0 - _sigmoid(z))
            dz_b = dz.astype(BF16)
            dbg_ref[...] += _colsum(dz)
            dw2_ref[...] += _dot(glr_b, dz_b, TN)
            dp_ref[rows, LR0:LR0 + LRP] = _dot(dz_b, w2_ref[...], NT).astype(BF16)

        def later_chunk(t, carry):
            ci = GLA_CPB - 1 - t
            chunk(ci, lambda h: st_blk_ref[ci - 1, h].astype(F32))
            return carry

        lax.fori_loop(0, GLA_CPB - 1, later_chunk, 0, unroll=True)
        chunk(0, lambda h: st_prev_ref[0, h].astype(F32) * has_prev)

    blocks = (GLA_RB * GLA_PAD * 4 + GLA_RB * D * 4 + (GLA_CPB + 1) * H * DV * DK * 2 + GLA_RB * GLA_PAD * 2)
    rev = lambda i: nb - 1 - i
    return _hosted(
        body, name="gla_bwd", grid=(nb,),
        in_specs=[pl.BlockSpec((GLA_RB, GLA_PAD), lambda i: (rev(i), 0)),
                  pl.BlockSpec((GLA_RB, H * DV), lambda i: (rev(i), 0)),
                  pl.BlockSpec((GLA_CPB, H, DV, DK), lambda i: (rev(i), 0, 0, 0)),
                  pl.BlockSpec((1, H, DV, DK), lambda i: (jnp.maximum(rev(i) * GLA_CPB - 1, 0), 0, 0, 0)),
                  pl.BlockSpec((LRP, H * DK), lambda i: (0, 0)),
                  pl.BlockSpec((1, H * DK), lambda i: (0, 0)),
                  pl.BlockSpec((1, H * DV), lambda i: (0, 0))],
        out_specs=[pl.BlockSpec((GLA_RB, GLA_PAD), lambda i: (rev(i), 0)),
                   pl.BlockSpec((1, H * DV), lambda i: (0, 0)),
                   pl.BlockSpec((1, H * DK), lambda i: (0, 0)),
                   pl.BlockSpec((LRP, H * DK), lambda i: (0, 0))],
        out_shape=[_sds((S, GLA_PAD), BF16), _sds((1, H * DV), F32), _sds((1, H * DK), F32), _sds((LRP, H * DK), F32)],
        args=[proj, dypre, states, states, w2p, bgate, ogain], scratch_shapes=[pltpu.VMEM((H, DV, DK), F32)],
        block_bytes=blocks, scratch_bytes=H * DV * DK * 4, phases=phases)


SGU_RB = 256
GELU_C = 0.7978845608028654
GELU_A = 0.044715


def _gelu(x):
    return 0.5 * x * (1.0 + jnp.tanh(GELU_C * (x + GELU_A * x * x * x)))


def _gelu_grad(x):
    t = jnp.tanh(GELU_C * (x + GELU_A * x * x * x))
    return 0.5 * (1.0 + t) + 0.5 * x * (1.0 - t * t) * (GELU_C * (1.0 + 3.0 * GELU_A * x * x))


def _causal_mask(transposed=False):
    i = lax.broadcasted_iota(jnp.int32, (SGU_BLOCK, SGU_BLOCK), 1 if transposed else 0)
    j = lax.broadcasted_iota(jnp.int32, (SGU_BLOCK, SGU_BLOCK), 0 if transposed else 1)
    return (i >= C) | (j < C)


def _layer_norm(vf, gain, bias):
    mu = jnp.mean(vf, axis=-1, keepdims=True)
    cen = vf - mu
    rstd = lax.rsqrt(jnp.mean(cen * cen, axis=-1, keepdims=True) + EPS)
    xhat = cen * rstd
    return xhat, rstd, xhat * gain + bias


def _sgu_fwd(proj, lng, lnb, ws, bsb, phases=()):
    def body(p_ref, g_ref, b_ref, ws_ref, bs_ref, o_ref):
        mask = _causal_mask()
        for n in range(SGU_RB // SGU_BLOCK):
            rows = slice(n * SGU_BLOCK, (n + 1) * SGU_BLOCK)
            _, _, vn = _layer_norm(_gelu(p_ref[rows, D:2 * D]), g_ref[...], b_ref[...])
            vn_b = vn.astype(BF16)
            for gi in range(SGU_G):
                cols = slice(gi * SGU_GD, (gi + 1) * SGU_GD)
                w = jnp.where(mask, ws_ref[gi], 0.0).astype(BF16)
                vs = _dot(w, vn_b[:, cols]) + bs_ref[gi]
                gate = p_ref[rows, 2 * D + gi * SGU_GD:2 * D + (gi + 1) * SGU_GD]
                o_ref[rows, cols] = (_gelu(p_ref[rows, cols]) * vs * (gate * _sigmoid(gate))).astype(BF16)

    blocks = SGU_RB * SGU_COLS * 4 + SGU_RB * D * 2 + SGU_G * SGU_BLOCK * (SGU_BLOCK + SGU_GD) * 4
    return _hosted(
        body, name="sgu_fwd", grid=(S // SGU_RB,),
        in_specs=[pl.BlockSpec((SGU_RB, SGU_COLS), lambda i: (i, 0)),
                  pl.BlockSpec((1, D), lambda i: (0, 0)), pl.BlockSpec((1, D), lambda i: (0, 0)),
                  pl.BlockSpec((SGU_G, SGU_BLOCK, SGU_BLOCK), lambda i: (0, 0, 0)),
                  pl.BlockSpec((SGU_G, SGU_BLOCK, SGU_GD), lambda i: (0, 0, 0))],
        out_specs=[pl.BlockSpec((SGU_RB, D), lambda i: (i, 0))], out_shape=[_sds((S, D), BF16)],
        args=[proj, lng, lnb, ws, bsb], block_bytes=blocks, phases=phases)


def _sgu_bwd(proj, dpre, lng, lnb, ws, wst, bsb, phases=()):
    nsteps = S // SGU_RB

    def body(p_ref, d_ref, g_ref, b_ref, ws_ref, wst_ref, bs_ref,
             dp_ref, dg_ref, db_ref, dws_ref, dbs_ref, dvn_ref, dvs_acc_ref):
        step = pl.program_id(0)

        @pl.when(step == 0)
        def _():
            dg_ref[...] = jnp.zeros_like(dg_ref)
            db_ref[...] = jnp.zeros_like(db_ref)
            dws_ref[...] = jnp.zeros_like(dws_ref)
            dvs_acc_ref[...] = jnp.zeros_like(dvs_acc_ref)

        mask = _causal_mask()
        maskt = _causal_mask(transposed=True)
        for n in range(SGU_RB // SGU_BLOCK):
            rows = slice(n * SGU_BLOCK, (n + 1) * SGU_BLOCK)
            v = p_ref[rows, D:2 * D]
            xhat, rstd, vn = _layer_norm(_gelu(v), g_ref[...], b_ref[...])
            vn_b = vn.astype(BF16)
            for gi in range(SGU_G):
                cols = slice(gi * SGU_GD, (gi + 1) * SGU_GD)
                w = jnp.where(mask, ws_ref[gi], 0.0).astype(BF16)
                wt = jnp.where(maskt, wst_ref[gi], 0.0).astype(BF16)
                vs = _dot(w, vn_b[:, cols]) + bs_ref[gi]
                u = p_ref[rows, cols]
                gate = p_ref[rows, 2 * D + gi * SGU_GD:2 * D + (gi + 1) * SGU_GD]
                sg = _sigmoid(gate)
                gu = _gelu(u)
                dpre_g = d_ref[rows, cols]
                t = dpre_g * (gate * sg)
                dp_ref[rows, cols] = (t * vs * _gelu_grad(u)).astype(BF16)
                dp_ref[rows, 2 * D + gi * SGU_GD:2 * D + (gi + 1) * SGU_GD] = (
                    dpre_g * gu * vs * (sg * (1.0 + gate * (1.0 - sg)))).astype(BF16)
                dvs = t * gu
                dvs_b = dvs.astype(BF16)
                dvs_acc_ref[:, cols] += dvs
                dws_ref[gi] += _dot(dvs_b, vn_b[:, cols], NT)
                dvn_ref[:, cols] = _dot(wt, dvs_b)
            dvn = dvn_ref[...]
            dg_ref[...] += _colsum(dvn * xhat)
            db_ref[...] += _colsum(dvn)
            dxh = dvn * g_ref[...]
            dvf = rstd * (dxh - jnp.mean(dxh, axis=-1, keepdims=True) - xhat * jnp.mean(dxh * xhat, axis=-1, keepdims=True))
            dp_ref[rows, D:2 * D] = (dvf * _gelu_grad(v)).astype(BF16)

        @pl.when(step == nsteps - 1)
        def _():
            lane = lax.broadcasted_iota(jnp.int32, (SGU_BLOCK, SGU_BLOCK), 1)
            out = jnp.zeros((SGU_BLOCK, SGU_BLOCK), F32)
            for gi in range(SGU_G):
                out = out + jnp.where(lane == gi, jnp.sum(dvs_acc_ref[:, gi * SGU_GD:(gi + 1) * SGU_GD], axis=1, keepdims=True), 0.0)
                dws_ref[gi] = jnp.where(mask, dws_ref[gi], 0.0)
            dbs_ref[...] = out

    blocks = SGU_RB * SGU_COLS * 6 + SGU_RB * D * 4 + SGU_G * SGU_BLOCK * (3 * SGU_BLOCK + SGU_GD) * 4
    const3 = lambda i: (0, 0, 0)
    return _hosted(
        body, name="sgu_bwd", grid=(nsteps,),
        in_specs=[pl.BlockSpec((SGU_RB, SGU_COLS), lambda i: (i, 0)),
                  pl.BlockSpec((SGU_RB, D), lambda i: (i, 0)),
                  pl.BlockSpec((1, D), lambda i: (0, 0)), pl.BlockSpec((1, D), lambda i: (0, 0)),
                  pl.BlockSpec((SGU_G, SGU_BLOCK, SGU_BLOCK), const3),
                  pl.BlockSpec((SGU_G, SGU_BLOCK, SGU_BLOCK), const3),
                  pl.BlockSpec((SGU_G, SGU_BLOCK, SGU_GD), const3)],
        out_specs=[pl.BlockSpec((SGU_RB, SGU_COLS), lambda i: (i, 0)),
                   pl.BlockSpec((1, D), lambda i: (0, 0)), pl.BlockSpec((1, D), lambda i: (0, 0)),
                   pl.BlockSpec((SGU_G, SGU_BLOCK, SGU_BLOCK), const3),
                   pl.BlockSpec((SGU_BLOCK, SGU_BLOCK), lambda i: (0, 0))],
        out_shape=[_sds((S, SGU_COLS), BF16), _sds((1, D), F32), _sds((1, D), F32),
                   _sds((SGU_G, SGU_BLOCK, SGU_BLOCK), F32), _sds((SGU_BLOCK, SGU_BLOCK), F32)],
        args=[proj, dpre, lng, lnb, ws, wst, bsb],
        scratch_shapes=[pltpu.VMEM((SGU_BLOCK, D), F32), pltpu.VMEM((SGU_BLOCK, D), F32)],
        block_bytes=blocks, scratch_bytes=2 * SGU_BLOCK * D * 4, phases=phases)


def _pair_sum(own, a, r0, nr, name, table=None):
    c = own.shape[2]
    tr = 256
    assert r0 % tr == 0 and nr % tr == 0

    def body(own_ref, sib_ref, o_ref):
        o_ref[...] = (own_ref[...].astype(F32) + sib_ref[...].astype(F32)).astype(BF16)

    own_map = ((lambda j, i: (1 + j, r0 // tr + i, 0)) if table is None else
               (lambda j, i, t: (t[1 + j], r0 // tr + i, 0)))
    cpad = -(-c // 128) * 128
    outs, _ = _hosted(
        body, name=name, grid=(3, nr // tr),
        in_specs=[pl.BlockSpec((None, tr, c), own_map),
                  pl.BlockSpec((None, tr, c), lambda j, i, *t: (1 + j, r0 // tr + i, 0))],
        out_specs=[pl.BlockSpec((None, tr, c), lambda j, i, *t: (j, i, 0))], out_shape=[_sds((3, nr, c), BF16)],
        args=[own, a], block_bytes=3 * tr * cpad * 2, table=table)
    return outs[0]


def _adamw_math(w, g, m, v):
    m = ADAM_B1 * m + (1.0 - ADAM_B1) * g
    v = ADAM_B2 * v + (1.0 - ADAM_B2) * (g * g)
    m_hat = m / (1.0 - ADAM_B1 ** ADAM_STEP)
    v_hat = v / (1.0 - ADAM_B2 ** ADAM_STEP)
    delta = -ADAM_LR * (m_hat / (jnp.sqrt(v_hat) + ADAM_EPS) + ADAM_WD * w)
    return delta, m, v


def _sum_adamw(own, a, b, w, m, v, *, name, phases=(), table=None):
    r, c = w.shape
    tr = 256

    def body(own_ref, sib_ref, far_ref, w_ref, m_ref, v_ref, g_ref, d_ref, nm_ref, nv_ref):
        g = own_ref[...].astype(F32) + sib_ref[...].astype(F32)
        for j in range(3):
            g = g + far_ref[j].astype(F32)
        g_ref[...] = g
        d_ref[...], nm_ref[...], nv_ref[...] = _adamw_math(w_ref[...], g, m_ref[...], v_ref[...])

    spec = pl.BlockSpec((tr, c), lambda i, *t: (i, 0))
    own_map = (lambda i: (0, i, 0)) if table is None else (lambda i, t: (t[0], i, 0))
    cpad = -(-c // 128) * 128
    return _hosted(
        body, name=name, grid=(r // tr,),
        in_specs=[pl.BlockSpec((None, tr, c), own_map), pl.BlockSpec((None, tr, c), lambda i, *t: (0, i, 0)),
                  pl.BlockSpec((3, tr, c), lambda i, *t: (0, i, 0)), spec, spec, spec],
        out_specs=[spec] * 4, out_shape=[_sds((r, c), F32)] * 4, args=[own, a, b, w, m, v],
        block_bytes=5 * tr * cpad * 2 + 7 * tr * cpad * 4, phases=phases, table=table)


def _sum_parts(parts, name):
    n, r, c = parts.shape

    def body(p_ref, o_ref):
        g = p_ref[0]
        for j in range(1, n):
            g = g + p_ref[j]
        o_ref[...] = g

    outs, _ = _hosted(body, name=name, grid=(1,), in_specs=[pl.BlockSpec((n, r, c), lambda i: (0, 0, 0))],
                      out_specs=[pl.BlockSpec((r, c), lambda i: (0, 0))], out_shape=[_sds((r, c), F32)], args=[parts],
                      block_bytes=(n + 1) * r * c * 4)
    return outs[0]


def _adamw(w, g, m, v, name):
    def body(w_ref, g_ref, m_ref, v_ref, d_ref, nm_ref, nv_ref):
        d_ref[...], nm_ref[...], nv_ref[...] = _adamw_math(w_ref[...], g_ref[...], m_ref[...], v_ref[...])

    spec = pl.BlockSpec(w.shape, lambda i: (0, 0))
    outs, _ = _hosted(body, name=name, grid=(1,), in_specs=[spec] * 4, out_specs=[spec] * 3, out_shape=[_sds(w.shape, F32)] * 3,
                      args=[w, g, m, v], block_bytes=7 * _nbytes(w.shape, F32))
    return outs


def _blocks_to_columns(g):
    n, r, c = g.shape
    return jnp.transpose(g, (1, 0, 2)).reshape(r, n * c)


def _pack(parts):
    return jnp.concatenate([p.reshape(-1) for p in parts]).reshape(-1, 128)


def _unpack(packed, like):
    flat, outs, off = packed.reshape(-1), [], 0
    for p in like:
        outs.append(flat[off:off + p.size].reshape(p.shape))
        off += p.size
    return outs


def kernel(x, norm_pre, norm_post, gla_w_in, gla_w_gate2, gla_b_gate, gla_o_gain, gla_w_out, sgu_w_in, sgu_ln_gain, sgu_ln_bias, sgu_w_spatial, sgu_b_spatial, sgu_w_out, loss_target, m_norm_pre, m_norm_post, m_gla_w_in, m_gla_w_gate2, m_gla_b_gate, m_gla_o_gain, m_gla_w_out, m_sgu_w_in, m_sgu_ln_gain, m_sgu_ln_bias, m_sgu_w_spatial, m_sgu_b_spatial, m_sgu_w_out, v_norm_pre, v_norm_post, v_gla_w_in, v_gla_w_gate2, v_gla_b_gate, v_gla_o_gain, v_gla_w_out, v_sgu_w_in, v_sgu_ln_gain, v_sgu_ln_bias, v_sgu_w_spatial, v_sgu_b_spatial, v_sgu_w_out):
    me = _index_of(*_place())
    x0 = x.reshape(S, D)
    tgt = loss_target.reshape(S, D)
    npre0, npre1 = norm_pre[0:1], norm_pre[1:2]
    npost0, npost1 = norm_post[0:1], norm_post[1:2]
    ws = sgu_w_spatial[0]
    wst = jnp.transpose(ws, (0, 2, 1))
    bsb = jnp.broadcast_to(sgu_b_spatial[0][:, :, None], (SGU_G, SGU_BLOCK, SGU_GD))
    W_ROWS = D // N_DEV
    IN_COLS_G, IN_COLS_S = GLA_COLS // N_DEV, SGU_COLS // N_DEV

    s_gwi, s_gwo = gla_w_in[0].astype(BF16), gla_w_out[0].astype(BF16)
    s_swi, s_swo = sgu_w_in[0].astype(BF16), sgu_w_out[0].astype(BF16)
    small = jnp.concatenate([jnp.pad(gla_w_gate2[0].reshape(4, 512), ((0, 4), (0, 0))),
                             jnp.pad(jnp.concatenate([sgu_ln_gain, sgu_ln_bias], axis=1), ((0, 7), (0, 0)))], axis=0)

    wg_in, g_small, h0 = _gather_first(s_gwi, small, x0, npre0, "gather_first")
    w2 =_blocks_to_columns(g_small[:, :4, :].reshape(N_DEV, LR, 128))
    w2p = jnp.pad(w2, ((0, LRP - LR), (0, 0))).astype(BF16)
    lng = g_small[:, 8, :256].reshape(1, D)
    lnb = g_small[:, 8, 256:].reshape(1, D)
    like_gwo, like_swi = _sds((N_DEV, W_ROWS, D), BF16), _sds((N_DEV, D, IN_COLS_S), BF16)

    proj0, (g_gwo, g_swi) = _mm(h0, wg_in, "nn", F32, tm=1024, tn=896, tk=D, name="gla_in", b_tiled=True, phases=[
        _Phase(like_gwo, None, [_gather_send(s_gwo, 0, W_ROWS)]),
        _Phase(like_swi, None, [_gather_send(s_swi, 0, 768, diagonal=False)])])
    (ypre0, states), (g_gwo, g_swi) = _gla_fwd(proj0, w2p, gla_b_gate, gla_o_gain, phases=[
        _Phase(like_gwo, g_gwo, [_gather_pass(0, W_ROWS)]),
        _Phase(like_swi, g_swi, [_gather_relay(0, 768), _gather_send(s_swi, 768, 512, diagonal=False)])])
    wg_out = g_gwo.reshape(D, D)
    y0, (g_swi,) = _mm(ypre0, wg_out, "nn", F32, tm=1024, tn=1024, tk=D, name="gla_out", phases=[
        _Phase(like_swi, g_swi, [_gather_pass(0, 768), _gather_relay(768, 512), _gather_send(s_swi, 1280, 512, diagonal=False)])])
    (x1, h1), (g_swi,) = _mid_fwd(x0, y0, npost0, npre1, phases=[
        _Phase(like_swi, g_swi, [_gather_pass(768, 512), _gather_relay(1280, 512), _gather_send(s_swi, 1792, 256, diagonal=False)])])
    g_swi, = _carry([_Phase(like_swi, g_swi, [_gather_pass(1280, 512), _gather_relay(1792, 256)])], "relay_sgu_w_in")
    g_swi, = _carry([_Phase(like_swi, g_swi, [_gather_pass(1792, 256)])], "pass_sgu_w_in")
    proj1, (g_swo,) = _mm(h1, g_swi, "nn", F32, tm=S, tn=IN_COLS_S, tk=D, name="sgu_in", b_blocked=True, phases=[
        _Phase(like_gwo, None, [_gather_send(s_swo, 0, W_ROWS)])])
    (pre1,), (g_swo,) = _sgu_fwd(proj1, lng, lnb, ws, bsb, phases=[_Phase(like_gwo, g_swo, [_gather_pass(0, W_ROWS)])])
    ws_out = g_swo.reshape(D, D)
    loss_cols, dx2, dy1, dnpost1 = _nn_rows(pre1, ws_out, name="sgu_out", row_ins=[x1, tgt], vec_ins=[npost1],
                                            outs=[("vec", F32), ("row", F32), ("row", BF16), ("vec", F32)], tail=_final_tail)
    loss_here = jnp.pad((0.5 * jnp.sum(loss_cols) / D).reshape(1, 1), ((0, 7), (0, 127)))

    like_b_out, like_b_swi = _sds((3, W_ROWS, D), BF16), _sds((3, D, IN_COLS_S), BF16)
    like_b_gwi = _sds((3, D, IN_COLS_G), BF16)
    row_pair = dict(like=_sds((4, W_ROWS, D), BF16), block=lambda i, j: i, ordinal=lambda i, j: i >> 1,
                    dst=lambda ref, k, i, j: ref.at[k])
    col_pair = dict(like=_sds((4, D, IN_COLS_S), BF16), block=lambda i, j: j, ordinal=lambda i, j: 4 * i + (j >> 1),
                    dst=lambda ref, k, i, j: ref.at[k, pl.ds(pl.multiple_of(i * 1024, 1024), 1024)])

    mine = _own_table()
    dws_out, (a_swo,) = _mm(pre1, dy1, "tn", BF16, tm=W_ROWS, tn=D, tk=S, name="sgu_out_dw", pair=row_pair)
    p_swo = dws_out.reshape(N_DEV, W_ROWS, D)
    t_swo = _pair_sum(p_swo, a_swo, 0, W_ROWS, "pair_sum_sgu_w_out", table=mine)
    dpre1, _ = _mm(dy1, ws_out, "nt", F32, tm=1024, tn=1024, tk=D, name="sgu_out_dx")
    (dproj1, dlng, dlnb, dwsp, dbsp), (b_swo,) = _sgu_bwd(proj1, dpre1, lng, lnb, ws, wst, bsb, phases=[
        _Phase(like_b_out, None, [_reduce_cross(t_swo, 0, 0, W_ROWS)])])
    p_swi, (a_swi,) = _mm(h1, dproj1, "tn", BF16, tm=1024, tn=IN_COLS_S, tk=S, name="sgu_in_dw", out_blocked=True, pair=col_pair)
    t_swi = _pair_sum(p_swi, a_swi, 0, D, "pair_sum_sgu_w_in", table=mine)
    (dx1, dy0, dnpre1, dnpost0), (b_swi,) = _nt_rows(
        dproj1, g_swi, tk=IN_COLS_S, name="sgu_in_dx", b_blocked=True, row_ins=[dx2, x1, y0], vec_ins=[npre1, npost0],
        outs=[("row", F32), ("row", BF16), ("vec", F32), ("vec", F32)], tail=_mid_bwd_tail, phases=[
            _Phase(like_b_swi, None, [_reduce_cross(t_swi, 0, 0, 1024)])])
    dwg_out, (a_gwo, b_swi) = _mm(ypre0, dy0, "tn", BF16, tm=W_ROWS, tn=D, tk=S, name="gla_out_dw", pair=row_pair, phases=[
        _Phase(like_b_swi, b_swi, [_reduce_cross(t_swi, 1024, 1024, 256)])])
    p_gwo = dwg_out.reshape(N_DEV, W_ROWS, D)
    t_gwo = _pair_sum(p_gwo, a_gwo, 0, W_ROWS, "pair_sum_gla_w_out", table=mine)
    dypre0, _ = _mm(dy0, wg_out, "nt", F32, tm=1024, tn=1024, tk=D, name="gla_out_dx")
    late = [dnpre1, dnpost1, dlng, dlnb, dwsp, jnp.transpose(dbsp[:, :SGU_G])]
    late_pack = _pack(late)
    (dproj0, dogain, dbgate, dw2), (b_swi, b_gwo, g_late) = _gla_bwd(proj0, dypre0, states, w2p, gla_b_gate, gla_o_gain, phases=[
        _Phase(like_b_swi, b_swi, [_reduce_cross(t_swi, 1280, 1280, 768)]),
        _Phase(like_b_out, None, [_reduce_cross(t_gwo, 0, 0, W_ROWS)]),
        _Phase(_sds((N_DEV,) + late_pack.shape, F32), None, [_gather_send(late_pack, 0, late_pack.shape[0])])])
    half = D // 2
    (own_gwi, a_gwi), (g_late,) = _dw_blocks(h0, dproj0, None, None, 0, half, "gla_in_dw_a", phases=[
        _Phase(_sds((N_DEV,) + late_pack.shape, F32), g_late, [_gather_pass(0, late_pack.shape[0])])])
    t_gwi_a = _pair_sum(own_gwi, a_gwi, 0, half, "pair_sum_gla_w_in_a")
    (own_gwi, a_gwi), (b_gwi,) = _dw_blocks(h0, dproj0, own_gwi, a_gwi, half, half, "gla_in_dw_b", phases=[
        _Phase(like_b_gwi, None, [_reduce_cross(t_gwi_a, 0, 0, 704)])])
    t_gwi_b = _pair_sum(own_gwi, a_gwi, half, half, "pair_sum_gla_w_in_b")
    (grad_x, dnpre0), (b_gwi,) = _nt_rows(
        dproj0, wg_in, tk=896, name="gla_in_dx", b_tiled=True, row_ins=[dx1, x0], vec_ins=[npre0],
        outs=[("row", F32), ("vec", F32)], tail=_first_bwd_tail, phases=[
            _Phase(like_b_gwi, b_gwi, [_reduce_cross(t_gwi_a, 704, 704, 320), _reduce_cross(t_gwi_b, 0, half, half)])])

    early = [dnpre0, dnpost0, dbgate, dogain, dw2[:LR], loss_here]
    early_pack = _pack(early)
    like_early = _sds((N_DEV,) + early_pack.shape, F32)
    (g_swo, d_swo, nm_swo, nv_swo), (g_early,) = _sum_adamw(
        p_swo, a_swo, b_swo, sgu_w_out[0], m_sgu_w_out[0], v_sgu_w_out[0], name="adamw_sgu_w_out", table=mine, phases=[
            _Phase(like_early, None, [_gather_send(early_pack, 0, early_pack.shape[0])])])
    (g_gwo_, d_gwo, nm_gwo, nv_gwo), (g_early,) = _sum_adamw(
        p_gwo, a_gwo, b_gwo, gla_w_out[0], m_gla_w_out[0], v_gla_w_out[0], name="adamw_gla_w_out", table=mine, phases=[
            _Phase(like_early, g_early, [_gather_pass(0, early_pack.shape[0])])])
    (g_swi_, d_swi, nm_swi, nv_swi), _ = _sum_adamw(
        p_swi, a_swi, b_swi, sgu_w_in[0], m_sgu_w_in[0], v_sgu_w_in[0], name="adamw_sgu_w_in", table=mine)
    (g_gwi_, d_gwi, nm_gwi, nv_gwi), _ = _sum_adamw(
        own_gwi, a_gwi, b_gwi, gla_w_in[0], m_gla_w_in[0], v_gla_w_in[0], name="adamw_gla_w_in")

    g_npre1, g_npost1, g_lng_full, g_lnb_full, g_wsp, g_bsp = _unpack(_sum_parts(g_late, "sum_late_small_grads"), late)
    g_npre0, g_npost0, g_bgate, g_ogain, g_w2_full, loss_all = _unpack(_sum_parts(g_early, "sum_early_small_grads"), early)
    loss = loss_all[0, 0]
    g_w2 = lax.dynamic_slice(g_w2_full, (0, me * 128), (LR, 128))
    g_lng = lax.dynamic_slice(g_lng_full, (0, me * 256), (1, 256))
    g_lnb = lax.dynamic_slice(g_lnb_full, (0, me * 256), (1, 256))
    small_g = [jnp.concatenate([g_npre0, g_npre1], 0), jnp.concatenate([g_npost0, g_npost1], 0), g_w2, g_bgate, g_ogain,
               g_lng, g_lnb, g_wsp, g_bsp]
    small_w = [norm_pre, norm_post, gla_w_gate2[0], gla_b_gate, gla_o_gain, sgu_ln_gain, sgu_ln_bias, sgu_w_spatial[0], sgu_b_spatial[0]]
    small_m = [m_norm_pre, m_norm_post, m_gla_w_gate2[0], m_gla_b_gate, m_gla_o_gain, m_sgu_ln_gain, m_sgu_ln_bias, m_sgu_w_spatial[0], m_sgu_b_spatial[0]]
    small_v = [v_norm_pre, v_norm_post, v_gla_w_gate2[0], v_gla_b_gate, v_gla_o_gain, v_sgu_ln_gain, v_sgu_ln_bias, v_sgu_w_spatial[0], v_sgu_b_spatial[0]]
    d_pack, nm_pack, nv_pack = _adamw(_pack(small_w), _pack(small_g), _pack(small_m), _pack(small_v), "adamw_small")

    out_like = [norm_pre, norm_post, gla_w_gate2, gla_b_gate, gla_o_gain, sgu_ln_gain, sgu_ln_bias, sgu_w_spatial, sgu_b_spatial]
    sg_ = [g.reshape(s.shape) for g, s in zip(small_g, out_like)]
    sd_, sm_, sv_ = (_unpack(pk, out_like) for pk in (d_pack, nm_pack, nv_pack))

    def assemble(small_list, w_in_g, w_out_g, w_in_s, w_out_s):
        npre_, npost_, w2_, bg_, og_, lg_, lb_, wsp_, bsp_ = small_list
        return [npre_, npost_, w_in_g[None], w2_, bg_, og_, w_out_g[None], w_in_s[None], lg_, lb_, wsp_, bsp_, w_out_s[None]]

    return (loss, grad_x.reshape(1, S, D),
            *assemble(sg_, g_gwi_, g_gwo_, g_swi_, g_swo),
            *assemble(sd_, d_gwi, d_gwo, d_swi, d_swo),
            *assemble(sm_, nm_gwi, nm_gwo, nm_swi, nm_swo),
            *assemble(sv_, nv_gwi, nv_gwo, nv_swi, nv_swo))
```

```python
import functools

import jax
import jax.numpy as jnp
from jax import lax
from jax.experimental import pallas as pl
from jax.experimental.pallas import tpu as pltpu

F32 = jnp.float32
BF16 = jnp.bfloat16

N_DEV = 8
S = 2048
D = 2048
H = 4
DK = 256
DV = 512
C = 64
NC = S // C
GLA_COLS = 6160
GLA_PAD = 6272
Q0, K0, V0, G0, LR0 = 0, 1024, 2048, 4096, 6144
LR = 16
LRP = 128
SGU_COLS = 6144
SGU_BLOCK = 128
SGU_G = 8
SGU_GD = 256
EPS = 1e-6
GLA_TAU = 16.0

ADAM_LR, ADAM_B1, ADAM_B2, ADAM_EPS, ADAM_WD, ADAM_STEP = 0.001, 0.9, 0.999, 1e-08, 0.01, 10

V7X_VMEM_BYTES = 64 * 1024 * 1024
VMEM_CEILING = V7X_VMEM_BYTES - 6 * 1024 * 1024
MESH = pl.DeviceIdType.MESH
HBM_SPEC = pl.BlockSpec(memory_space=pl.ANY)


def _sds(shape, dtype):
    return jax.ShapeDtypeStruct(tuple(shape), dtype)


def _nbytes(shape, dtype):
    n = 1
    for s in shape:
        n *= s
    return n * jnp.dtype(dtype).itemsize


def _dot(a, b, dims=(((1,), (0,)), ((), ())), precision=None):
    return lax.dot_general(a, b, dims, precision=precision, preferred_element_type=F32)


NN = (((1,), (0,)), ((), ()))
TN = (((0,), (0,)), ((), ()))
NT = (((1,), (1,)), ((), ()))


def _place():
    return lax.axis_index("x"), lax.axis_index("y"), lax.axis_index("c")


def _index_of(px, py, pc):
    return 4 * px + 2 * py + pc


def _chips(x, y):
    return [(1 - x, y), (x, 1 - y), (1 - x, 1 - y)]


def _rcopy(src, dst, send_sem, recv_sem, to):
    return pltpu.make_async_remote_copy(src_ref=src, dst_ref=dst, send_sem=send_sem, recv_sem=recv_sem,
                                        device_id=to, device_id_type=MESH)


class _Move:
    def __init__(self, ins, n_remote, make, stage=None):
        self.ins, self.n_remote, self.make, self.stage = list(ins), n_remote, make, stage

    def scratch(self):
        sems = [pltpu.SemaphoreType.DMA((self.n_remote,)), pltpu.SemaphoreType.DMA((self.n_remote,))]
        return sems if self.stage is None else sems + [pltpu.SemaphoreType.DMA((1,)), pltpu.VMEM(*self.stage)]

    def start(self, in_refs, buf, scratch):
        sends, _, local = self.make(in_refs, buf, scratch[0], scratch[1])
        if local is not None:
            pltpu.make_async_copy(local[0], scratch[3], scratch[2].at[0]).start()
        for cp in sends:
            cp.start()

    def finish(self, in_refs, buf, scratch):
        sends, arrivals, local = self.make(in_refs, buf, scratch[0], scratch[1])
        if local is not None:
            pltpu.make_async_copy(local[0], scratch[3], scratch[2].at[0]).wait()
            out = pltpu.make_async_copy(scratch[3], local[1], scratch[2].at[0])
            out.start()
        for cp in arrivals:
            cp.wait_recv()
        for cp in sends:
            cp.wait_send()
        if local is not None:
            out.wait()


class _Phase:
    def __init__(self, like, so_far, moves):
        self.like, self.so_far, self.moves = like, so_far, list(moves)


def _gather_send(shard, r0, nr, diagonal=True):
    def make(in_refs, g, ss, rs):
        sh, = in_refs
        x, y, c = _place()
        me = _index_of(x, y, c)
        rows = pl.ds(r0, nr)
        peers = [(x, y, 1 - c)] + [(px, py, c) for px, py in _chips(x, y)[:3 if diagonal else 2]]
        sends = [_rcopy(sh.at[rows], g.at[me, rows], ss.at[k], rs.at[k], p) for k, p in enumerate(peers)]
        arrivals = [_rcopy(sh.at[rows], g.at[_index_of(*p), rows], ss.at[k], rs.at[k], p) for k, p in enumerate(peers)]
        return sends, arrivals, (sh.at[rows], g.at[me, rows])

    return _Move([shard], 4 if diagonal else 3, make, stage=((nr, shard.shape[1]), shard.dtype))


def _gather_relay(r0, nr):
    def make(in_refs, g, ss, rs):
        x, y, c = _place()
        nx, ny, nd = [(px, py, c) for px, py in _chips(x, y)]
        first, second = pl.ds(r0, nr // 2), pl.ds(r0 + nr // 2, nr // 2)
        sends = [_rcopy(g.at[_index_of(*nx), first], g.at[_index_of(*nx), first], ss.at[0], rs.at[0], ny),
                 _rcopy(g.at[_index_of(*ny), second], g.at[_index_of(*ny), second], ss.at[1], rs.at[1], nx)]
        arrivals = [_rcopy(g.at[_index_of(*nx), first], g.at[_index_of(*nd), first], ss.at[0], rs.at[0], ny),
                    _rcopy(g.at[_index_of(*ny), second], g.at[_index_of(*nd), second], ss.at[1], rs.at[1], nx)]
        return sends, arrivals, None

    return _Move([], 2, make)


def _gather_pass(r0, nr):
    def make(in_refs, g, ss, rs):
        x, y, c = _place()
        rows = pl.ds(r0, nr)
        sends = [_rcopy(g.at[_index_of(px, py, c), rows], g.at[_index_of(px, py, c), rows], ss.at[j], rs.at[j], (x, y, 1 - c))
                 for j, (px, py) in enumerate(_chips(x, y))]
        arrivals = [_rcopy(g.at[_index_of(px, py, c), rows], g.at[_index_of(px, py, 1 - c), rows], ss.at[j], rs.at[j], (x, y, 1 - c))
                    for j, (px, py) in enumerate(_chips(x, y))]
        return sends, arrivals, None

    return _Move([], 3, make)


def _own_table():
    x, y, c = _place()
    return jnp.stack([_index_of(px, py, c) for px, py in [(x, y)] + _chips(x, y)]).astype(jnp.int32)


def _reduce_cross(sums, src_r0, dst_r0, nr):
    def make(in_refs, b, ss, rs):
        t, = in_refs
        x, y, c = _place()
        src, dst = pl.ds(src_r0, nr), pl.ds(dst_r0, nr)
        sends = [_rcopy(t.at[j, src], b.at[j, dst], ss.at[j], rs.at[j], (px, py, c)) for j, (px, py) in enumerate(_chips(x, y))]
        return sends, sends, None

    return _Move([sums], 3, make)


def _hosted(body, *, name, grid, in_specs, out_specs, out_shape, args, scratch_shapes=(), block_bytes, scratch_bytes=0,
            phases=(), table=None, continued=None):
    n_in, n_out, n_scr = len(args), len(out_shape), len(scratch_shapes)
    all_args, all_out_shape, sems, aliases, layout = list(args), list(out_shape), [], dict(continued or {}), []
    for j, ph in enumerate(phases):
        counts = []
        for mv in ph.moves:
            all_args += mv.ins
            counts.append(len(mv.ins))
            sems += mv.scratch()
        if ph.so_far is not None:
            aliases[len(all_args)] = n_out + j
            all_args.append(ph.so_far)
        layout.append((counts, ph.so_far is not None))
        all_out_shape.append(ph.like)
    n_extra_in = len(all_args) - n_in

    def wrapped(*refs):
        ins, pos = refs[:n_in], n_in
        move_ins = []
        for counts, continues in layout:
            per_move = []
            for cnt in counts:
                per_move.append(refs[pos:pos + cnt])
                pos += cnt
            pos += continues
            move_ins.append(per_move)
        outs = refs[pos:pos + n_out]
        bufs = refs[pos + n_out:pos + n_out + len(phases)]
        pos += n_out + len(phases)
        scratch = refs[pos:pos + n_scr]
        pos += n_scr
        move_sems = []
        for ph in phases:
            per_move = []
            for mv in ph.moves:
                count = len(mv.scratch())
                per_move.append(refs[pos:pos + count])
                pos += count
            move_sems.append(per_move)

        def each_move(fn_name):
            for ph, buf, per_in, per_sem in zip(phases, bufs, move_ins, move_sems):
                for mv, mv_in, mv_sem in zip(ph.moves, per_in, per_sem):
                    getattr(mv, fn_name)(mv_in, buf, mv_sem)

        if phases:
            first = functools.reduce(jnp.logical_and, [pl.program_id(a) == 0 for a in range(len(grid))])
            last = functools.reduce(jnp.logical_and, [pl.program_id(a) == grid[a] - 1 for a in range(len(grid))])
            pl.when(first)(lambda: each_move("start"))
        body(*ins, *outs, *scratch)
        if phases:
            pl.when(last)(lambda: each_move("finish"))

    all_args = [pltpu.with_memory_space_constraint(a, pltpu.HBM) for a in all_args]
    est = 2 * block_bytes + scratch_bytes
    params = pltpu.CompilerParams(dimension_semantics=("arbitrary",) * len(grid),
                                  vmem_limit_bytes=min(VMEM_CEILING, max(32 * 1024 * 1024, 2 * est)))
    all_in_specs, all_out_specs = list(in_specs) + [HBM_SPEC] * n_extra_in, list(out_specs) + [HBM_SPEC] * len(phases)
    if table is None:
        results = pl.pallas_call(
            wrapped, grid=grid, in_specs=all_in_specs, out_specs=all_out_specs, out_shape=all_out_shape,
            scratch_shapes=list(scratch_shapes) + sems, input_output_aliases=aliases, compiler_params=params, name=name,
        )(*all_args)
    else:
        results = pl.pallas_call(
            lambda table_ref, *refs: wrapped(*refs),
            grid_spec=pltpu.PrefetchScalarGridSpec(num_scalar_prefetch=1, grid=grid, in_specs=all_in_specs, out_specs=all_out_specs,
                                                   scratch_shapes=list(scratch_shapes) + sems),
            out_shape=all_out_shape, input_output_aliases={k + 1: v for k, v in aliases.items()}, compiler_params=params, name=name,
        )(table, *all_args)
    return list(results[:n_out]), list(results[n_out:])


class _Both:
    def __init__(self, copies):
        self.copies = copies

    def start(self):
        for cp in self.copies:
            cp.start()

    def wait_send(self):
        for cp in self.copies:
            cp.wait_send()

    def wait_recv(self):
        for cp in self.copies:
            cp.wait_recv()


def _carry(phases, name):
    def body(o_ref):
        o_ref[...] = jnp.zeros_like(o_ref)

    _, bufs = _hosted(body, name=name, grid=(1,), in_specs=[], out_specs=[pl.BlockSpec((8, 128), lambda i: (0, 0))],
                      out_shape=[_sds((8, 128), F32)], args=[], block_bytes=8 * 128 * 4, phases=phases)
    return bufs


def _gather_first(shard, small, xs, gain, name):
    cw, tr, n_tiles = shard.shape[1], 256, GLA_PAD // 128

    def body(sh_ref, sm_ref, xs_hbm, gain_ref, wn_ref, g_ref, gs_ref, h_hbm, wt_ref, win_ref, tmp_ref, xin_ref, hout_ref,
             send_sems, recv_sems, local_sems, xin_sems, hout_sems):
        x, y, c = _place()
        me, sibling = (x, y, c), (x, y, 1 - c)
        chips = _chips(x, y)

        def copy(base, out_ref, k, block, to, src=None):
            dst = out_ref.at[_index_of(*block)]
            return _rcopy(dst if src is None else src, dst, send_sems.at[base + k], recv_sems.at[base + k], to)

        icopy = functools.partial(copy, 0, g_ref)
        scopy = functools.partial(copy, 8, gs_ref)

        def wcopy(k, block, to, src=None):
            if k in (1, 2):
                return icopy(k, block, to, src)
            halves = []
            for part, sem in enumerate((k, {0: 15, 4: 16, 5: 17, 6: 18}[k])):
                rows = pl.ds(part * (D // 2), D // 2)
                dst = g_ref.at[_index_of(*block), rows]
                halves.append(_rcopy(dst if src is None else src.at[rows], dst, send_sems.at[sem], recv_sems.at[sem], to))
            return _Both(halves)

        def relay(k, block, half, to):
            rows = pl.ds(half * (D // 2), D // 2)
            ref = g_ref.at[_index_of(*block), rows]
            return _rcopy(ref, ref, send_sems.at[k], recv_sems.at[k], to)

        def load(src_ref, slot):
            cp = pltpu.make_async_copy(src_ref, win_ref.at[slot], local_sems.at[0])
            cp.start()
            cp.wait()

        def place(slot, block):
            b = _index_of(*block)

            def rows_chunk(r, carry):
                rows = pl.ds(pl.multiple_of(r * tr, tr), tr)
                tmp_ref[:, :cw] = win_ref[slot, rows, :].astype(F32)
                shifted = pltpu.roll(tmp_ref[...], 2 * b, 1)
                for u in range(7):
                    wt_ref[6 * b + u, rows, :] = (wt_ref[6 * b + u, rows, :].astype(F32) + shifted[:, 128 * u:128 * (u + 1)]).astype(BF16)
                return carry

            lax.fori_loop(0, D // tr, rows_chunk, 0)

        small_own = pltpu.make_async_copy(sm_ref, gs_ref.at[_index_of(*me)], local_sems.at[1])
        small_own.start()
        first = [wcopy(1 + j, me, (*chip, c), src=sh_ref) for j, chip in enumerate(chips[:2])]
        first += [scopy(0, me, sibling, src=sm_ref)] + [scopy(1 + j, me, (*chip, c), src=sm_ref) for j, chip in enumerate(chips)]
        for cp in first:
            cp.start()

        def clear(t, carry):
            wt_ref[t] = jnp.zeros((D, 128), BF16)
            return carry

        lax.fori_loop(0, n_tiles, clear, 0)
        tmp_ref[...] = jnp.zeros_like(tmp_ref)

        def emit(t):
            pltpu.make_async_copy(wt_ref.at[t], wn_ref.at[t], local_sems.at[2]).start()

        def take(block, slot, arrivals=None, pass_on=None):
            for cp in arrivals or ():
                cp.wait_recv()
            load(sh_ref if arrivals is None else g_ref.at[_index_of(*block)], slot)
            if pass_on is not None:
                pass_on.start()
            place(slot, block)
            for u in range(1, 6):
                emit(6 * _index_of(*block) + u)

        near_x, near_y, far = [(*chip, c) for chip in chips]
        to_sibling = wcopy(0, me, sibling, src=win_ref.at[0])
        pass_x = wcopy(4, near_x, sibling, src=win_ref.at[1])
        pass_y = wcopy(5, near_y, sibling, src=win_ref.at[0])
        pass_d = wcopy(6, far, sibling, src=win_ref.at[0])
        relays = [relay(3, near_x, 0, near_y), relay(7, near_y, 1, near_x)]
        take(me, 0, pass_on=to_sibling)

        def piece_in(r):
            return pltpu.make_async_copy(xs_hbm.at[pl.ds(r * RB, RB)], xin_ref.at[r % 2], xin_sems.at[r % 2])

        def piece_out(r):
            return pltpu.make_async_copy(hout_ref.at[r % 2], h_hbm.at[pl.ds(r * RB, RB)], hout_sems.at[r % 2])

        piece_in(0).start()
        for r in range(S // RB):
            if r + 1 < S // RB:
                piece_in(r + 1).start()
            piece_in(r).wait()
            if r >= 2:
                piece_out(r - 2).wait()
            xv = xin_ref[r % 2]
            hout_ref[r % 2] = (xv * _rinv(xv) * gain_ref[...]).astype(BF16)
            piece_out(r).start()
        piece_out(S // RB - 2).wait()
        piece_out(S // RB - 1).wait()

        take(near_x, 1, [wcopy(1, near_x, me)], pass_x)
        relays[0].start()
        to_sibling.wait_send()
        take(near_y, 0, [wcopy(2, near_y, me)], pass_y)
        relays[1].start()
        small_passed = []
        for j, chip in enumerate(chips):
            scopy(1 + j, (*chip, c), me).wait_recv()
            cp = scopy(4 + j, (*chip, c), sibling)
            cp.start()
            small_passed.append(cp)
        pass_x.wait_send()
        take(sibling, 1, [wcopy(0, sibling, me)])
        pass_y.wait_send()
        take((*chips[0], 1 - c), 0, [wcopy(4, (*chips[0], 1 - c), me)])
        take((*chips[1], 1 - c), 1, [wcopy(5, (*chips[1], 1 - c), me)])
        take(far, 0, [relay(3, far, 0, near_y), relay(7, far, 1, near_x)], pass_d)
        take((*chips[2], 1 - c), 1, [wcopy(6, (*chips[2], 1 - c), me)])
        for t in range(0, n_tiles, 6):
            emit(t)
        scopy(0, sibling, me).wait_recv()
        for j, chip in enumerate(chips):
            scopy(4 + j, (*chip, 1 - c), me).wait_recv()
        for cp in first + small_passed + relays + [pass_d]:
            cp.wait_send()
        small_own.wait()
        pltpu.make_async_copy(wn_ref, wn_ref, local_sems.at[2]).wait()

    wn, _, gs, h = pl.pallas_call(
        body,
        in_specs=[HBM_SPEC] * 3 + [pl.BlockSpec(memory_space=pltpu.VMEM)], out_specs=[HBM_SPEC] * 4,
        out_shape=[_sds((n_tiles, D, 128), BF16), _sds((N_DEV,) + shard.shape, BF16), _sds((N_DEV,) + small.shape, small.dtype),
                   _sds(xs.shape, BF16)],
        scratch_shapes=[pltpu.VMEM((n_tiles, D, 128), BF16), pltpu.VMEM((2, D, cw), BF16), pltpu.VMEM((tr, 7 * 128), F32),
                        pltpu.VMEM((2, RB, D), F32), pltpu.VMEM((2, RB, D), BF16),
                        pltpu.SemaphoreType.DMA((19,)), pltpu.SemaphoreType.DMA((19,)), pltpu.SemaphoreType.DMA((3,)),
                        pltpu.SemaphoreType.DMA((2,)), pltpu.SemaphoreType.DMA((2,))],
        compiler_params=pltpu.CompilerParams(vmem_limit_bytes=48 * 1024 * 1024),
        name=name,
    )(shard, small, xs, gain)
    return wn, gs, h


def _mm(a, b, mode, out_dtype, *, tm, tn, tk, name, b_blocked=False, b_tiled=False, out_blocked=False, pair=None, phases=()):
    if mode == "nn":
        (m, k), dims = a.shape, NN
        a_blk, a_map = (tm, tk), (lambda i, j, kk: (i, kk))
        if b_blocked:
            assert b.shape[1] == k and b.shape[2] == tn and tk == k
            n = b.shape[0] * tn
            b_spec = pl.BlockSpec((None, tk, tn), lambda i, j, kk: (j, kk, 0))
        elif b_tiled:
            assert b.shape[1] == k and b.shape[2] == 128 and tn % 128 == 0
            n = b.shape[0] * 128
            b_spec = pl.BlockSpec((tn // 128, tk, 128), lambda i, j, kk: (j, kk, 0))
        else:
            assert b.shape[0] == k
            n = b.shape[1]
            b_spec = pl.BlockSpec((tk, tn), lambda i, j, kk: (kk, j))
    elif mode == "tn":
        (k, m), n, dims = a.shape, b.shape[1], TN
        assert b.shape[0] == k
        a_blk, a_map = (tk, tm), (lambda i, j, kk: (kk, i))
        b_spec = pl.BlockSpec((tk, tn), lambda i, j, kk: (kk, j))
    else:
        (m, k), dims = a.shape, NT
        a_blk, a_map = (tm, tk), (lambda i, j, kk: (i, kk))
        if b_blocked:
            assert b.shape[0] * b.shape[2] == k and b.shape[2] == tk
            n = b.shape[1]
            b_spec = pl.BlockSpec((None, tn, tk), lambda i, j, kk: (kk, j, 0))
        elif b_tiled:
            assert b.shape[0] * 128 == k and b.shape[2] == 128 and tk % 128 == 0
            n = b.shape[1]
            b_spec = pl.BlockSpec((tk // 128, tn, 128), lambda i, j, kk: (kk, j, 0))
        else:
            assert b.shape[1] == k
            n = b.shape[0]
            b_spec = pl.BlockSpec((tn, tk), lambda i, j, kk: (j, kk))
    assert m % tm == 0 and n % tn == 0 and k % tk == 0, (a.shape, b.shape, mode)
    nk = k // tk
    n_row_tiles = m // tm
    if out_blocked:
        out_shape, out_spec = _sds((n // tn, n_row_tiles * tm, tn), out_dtype), pl.BlockSpec((None, tm, tn), lambda i, j, kk: (j, i, 0))
    else:
        out_shape, out_spec = _sds((n_row_tiles * tm, n), out_dtype), pl.BlockSpec((tm, tn), lambda i, j, kk: (i, j))

    grid = (n_row_tiles, n // tn, nk)

    def body(a_ref, b_ref, o_ref, *rest):
        rhs = jnp.concatenate([b_ref[u] for u in range(b_ref.shape[0])], axis=1) if b_tiled else b_ref[...]
        p = _dot(a_ref[...], rhs, dims)
        if nk == 1:
            o_ref[...] = p.astype(out_dtype)
            if pair is not None:
                _send_to_sibling(p.astype(out_dtype), *rest)
        else:
            acc_ref, = rest
            kk = pl.program_id(2)

            @pl.when(kk == 0)
            def _():
                acc_ref[...] = p

            @pl.when(kk > 0)
            def _():
                acc_ref[...] += p

            @pl.when(kk == nk - 1)
            def _():
                o_ref[...] = acc_ref[...].astype(out_dtype)

    def _send_to_sibling(tile, pair_ref, stage_ref, send_sems, recv_sem):
        i, j = pl.program_id(0), pl.program_id(1)
        x, y, c = _place()
        blk = pair["block"](i, j)
        k = ((blk >> 2) ^ x) + 2 * (((blk >> 1) & 1) ^ y)
        ordinal = pair["ordinal"](i, j)

        def send(slot):
            return _rcopy(stage_ref.at[slot], pair["dst"](pair_ref, k, i, j), send_sems.at[slot], recv_sem.at[0], (x, y, 1 - c))

        @pl.when((blk & 1) != c)
        def _():
            slot = ordinal & 1

            @pl.when(ordinal >= 2)
            def _():
                send(slot).wait_send()

            stage_ref[slot] = tile
            send(slot).start()

        @pl.when((i == grid[0] - 1) & (j == grid[1] - 1))
        def _():
            send(0).wait_send()
            send(1).wait_send()
            _rcopy(pair_ref, pair_ref, send_sems.at[0], recv_sem.at[0], (x, y, 1 - c)).wait_recv()

    blocks = _nbytes(a_blk, a.dtype) + tk * tn * jnp.dtype(b.dtype).itemsize + _nbytes((tm, tn), out_dtype)
    out_specs, out_shapes, scratch = [out_spec], [out_shape], [] if nk == 1 else [pltpu.VMEM((tm, tn), F32)]
    scratch_bytes = _nbytes((tm, tn), F32) * (nk > 1)
    if pair is not None:
        assert nk == 1
        out_specs, out_shapes = out_specs + [HBM_SPEC], out_shapes + [pair["like"]]
        scratch = [pltpu.VMEM((2, tm, tn), out_dtype), pltpu.SemaphoreType.DMA((2,)), pltpu.SemaphoreType.DMA((1,))]
        scratch_bytes = 2 * _nbytes((tm, tn), out_dtype)
    outs, bufs = _hosted(
        body, name=name, grid=grid,
        in_specs=[pl.BlockSpec(a_blk, a_map), b_spec], out_specs=out_specs, out_shape=out_shapes, args=[a, b],
        scratch_shapes=scratch, block_bytes=blocks, scratch_bytes=scratch_bytes, phases=phases)
    return outs[0], outs[1:] + bufs


def _dw_blocks(a, g, own_so_far, pair_so_far, m0, rows, name, phases=()):
    cw, win = GLA_COLS // N_DEV, 896
    continues = own_so_far is not None

    def body(*refs):
        a_ref, g_hbm = refs[:2]
        own_ref, pair_ref, win_ref, stage_ref, win_sems, local_sems, send_sems, recv_sem = refs[-8:]
        j = pl.program_id(0)
        x, y, c = _place()
        dst_rows = pl.ds(m0, rows)

        def window(jj):
            src = g_hbm.at[:, pl.ds(pl.multiple_of(768 * jj, 128), win)]
            return pltpu.make_async_copy(src, win_ref.at[jj % 2], win_sems.at[jj % 2])

        pl.when(j == 0)(lambda: window(j).start())
        pl.when(j + 1 < N_DEV)(lambda: window(j + 1).start())
        window(j).wait()
        p = _dot(a_ref[...], win_ref[j % 2], TN)
        tile = pltpu.roll(p, lax.rem(win - 2 * j, win), 1)[:, :cw].astype(BF16)
        k = ((j >> 2) ^ x) + 2 * (((j >> 1) & 1) ^ y)

        def local(slot, kk):
            return pltpu.make_async_copy(stage_ref.at[slot], own_ref.at[kk, dst_rows], local_sems.at[slot])

        def send(slot, kk):
            return _rcopy(stage_ref.at[slot], pair_ref.at[kk, dst_rows], send_sems.at[slot], recv_sem.at[0], (x, y, 1 - c))

        slot = j & 1

        @pl.when(slot == c)
        def _():
            pl.when(j >= 2)(lambda: local(slot, k).wait())
            stage_ref[slot] = tile
            local(slot, k).start()

        @pl.when(slot != c)
        def _():
            pl.when(j >= 2)(lambda: send(slot, k).wait_send())
            stage_ref[slot] = tile
            send(slot, k).start()

        @pl.when(j == N_DEV - 1)
        def _():
            local(c, 0).wait()
            send(1 - c, 0).wait_send()
            arrived = pair_ref.at[:, dst_rows]
            _rcopy(arrived, arrived, send_sems.at[0], recv_sem.at[0], (x, y, 1 - c)).wait_recv()

    like = _sds((4, D, cw), BF16)
    return _hosted(
        body, name=name, grid=(N_DEV,),
        in_specs=[pl.BlockSpec((S, rows), lambda j: (0, m0 // rows)), HBM_SPEC] + [HBM_SPEC] * (2 * continues),
        out_specs=[HBM_SPEC, HBM_SPEC], out_shape=[like, like], args=[a, g] + [own_so_far, pair_so_far] * continues,
        scratch_shapes=[pltpu.VMEM((2, S, win), BF16), pltpu.VMEM((2, rows, cw), BF16), pltpu.SemaphoreType.DMA((2,)),
                        pltpu.SemaphoreType.DMA((2,)), pltpu.SemaphoreType.DMA((2,)), pltpu.SemaphoreType.DMA((1,))],
        block_bytes=S * rows * 2, scratch_bytes=2 * S * win * 2 + 2 * rows * win * 2 + 2 * rows * win * 4, phases=phases,
        continued={2: 0, 3: 1} if continues else None)


NT_ROWS_TM = 1024
NN_ROWS_TM = 512
NT_ROWS_SUB = 128


def _nt_rows(a, b, *, tk, name, row_ins, vec_ins, outs, tail, b_blocked=False, b_tiled=False, phases=()):
    m, k = a.shape
    tm, sub, nk = NT_ROWS_TM, NT_ROWS_SUB, k // tk
    n_sub = tm // sub
    if b_blocked:
        assert b.shape[0] * b.shape[2] == k and b.shape[2] == tk and b.shape[1] == D
        b_spec = pl.BlockSpec((None, D, tk), lambda i, kk: (kk, 0, 0))
    else:
        assert b_tiled and b.shape[0] * 128 == k and tk % 128 == 0 and b.shape[1] == D
        b_spec = pl.BlockSpec((tk // 128, D, 128), lambda i, kk: (kk, 0, 0))
    row_spec, vec_spec = pl.BlockSpec((tm, D), lambda i, kk: (i, 0)), pl.BlockSpec((1, D), lambda i, kk: (0, 0))
    n_row, n_vec, n_out = len(row_ins), len(vec_ins), len(outs)
    assert nk >= 2

    def body(a_ref, b_ref, *rest):
        row_hbm, vec_refs = rest[:n_row], rest[n_row:n_row + n_vec]
        out_refs = rest[n_row + n_vec:n_row + n_vec + n_out]
        acc_ref, row_sems = rest[n_row + n_vec + n_out], rest[-1]
        row_refs = rest[n_row + n_vec + n_out + 1:-1]
        rhs = jnp.concatenate([b_ref[u] for u in range(b_ref.shape[0])], axis=1) if b_tiled else b_ref[...]
        p = _dot(a_ref[...], rhs, NT)
        i, kk = pl.program_id(0), pl.program_id(1)

        def fetch(r, s):
            src = row_hbm[r].at[pl.ds(pl.multiple_of(i * tm + s * sub, sub), sub)]
            return pltpu.make_async_copy(src, row_refs[r].at[s % 2], row_sems.at[r, s % 2])

        @pl.when(kk == 0)
        def _():
            for r in range(n_row):
                fetch(r, 0).start(priority=1)
            acc_ref[...] = p

        @pl.when(kk > 0)
        def _():
            acc_ref[...] += p

        @pl.when(kk == nk - 1)
        def _():
            for s in range(n_sub):
                for r in range(n_row):
                    if s + 1 < n_sub:
                        fetch(r, s + 1).start(priority=1)
                    fetch(r, s).wait()
                rows = slice(s * sub, (s + 1) * sub)
                tail(acc_ref[rows, :], rows, (i == 0) if s == 0 else None, [ref[s % 2] for ref in row_refs], vec_refs, out_refs)

    out_specs = [row_spec if kind == "row" else vec_spec for kind, _ in outs]
    out_shape = [_sds((m, D) if kind == "row" else (1, D), dt) for kind, dt in outs]
    blocks = tm * tk * 2 + D * tk * 2 + sum(tm * D * jnp.dtype(dt).itemsize for kind, dt in outs if kind == "row")
    scratch = ([pltpu.VMEM((tm, D), F32)] + [pltpu.VMEM((2, sub, D), x.dtype) for x in row_ins]
               + [pltpu.SemaphoreType.DMA((n_row, 2))])
    return _hosted(body, name=name, grid=(m // tm, nk), in_specs=[pl.BlockSpec((tm, tk), lambda i, kk: (i, kk)), b_spec]
                   + [HBM_SPEC] * n_row + [vec_spec] * n_vec, out_specs=out_specs, out_shape=out_shape,
                   args=[a, b] + list(row_ins) + list(vec_ins), scratch_shapes=scratch,
                   block_bytes=blocks, scratch_bytes=tm * D * 4 + n_row * 2 * sub * D * 4, phases=phases)


def _nn_rows(a, b, *, name, row_ins, vec_ins, outs, tail):
    m, k = a.shape
    tm = NN_ROWS_TM
    assert b.shape == (k, D)
    row_spec, vec_spec = pl.BlockSpec((tm, D), lambda i: (i, 0)), pl.BlockSpec((1, D), lambda i: (0, 0))
    n_row, n_vec, n_out = len(row_ins), len(vec_ins), len(outs)

    def body(a_ref, b_ref, *rest):
        row_hbm, vec_refs = rest[:n_row], rest[n_row:n_row + n_vec]
        out_refs = rest[n_row + n_vec:n_row + n_vec + n_out]
        d_ref, row_sems = rest[n_row + n_vec + n_out], rest[-1]
        row_refs = rest[n_row + n_vec + n_out + 1:-1]
        i = pl.program_id(0)
        fetches = [pltpu.make_async_copy(row_hbm[r].at[pl.ds(pl.multiple_of(i * tm, tm), tm)], row_refs[r], row_sems.at[r])
                   for r in range(n_row)]
        for cp in fetches:
            cp.start()
        d_ref[...] = _dot(a_ref[...], b_ref[...])
        for cp in fetches:
            cp.wait()
        for s in range(tm // NT_ROWS_SUB):
            rows = slice(s * NT_ROWS_SUB, (s + 1) * NT_ROWS_SUB)
            tail(d_ref[rows, :], rows, (i == 0) if s == 0 else None, [ref[rows, :] for ref in row_refs], vec_refs, out_refs)

    out_specs = [row_spec if kind == "row" else vec_spec for kind, _ in outs]
    out_shape = [_sds((m, D) if kind == "row" else (1, D), dt) for kind, dt in outs]
    blocks = tm * k * 2 + k * D * 2 + sum(tm * D * jnp.dtype(dt).itemsize for kind, dt in outs if kind == "row")
    scratch = [pltpu.VMEM((tm, D), F32)] + [pltpu.VMEM((tm, D), x.dtype) for x in row_ins] + [pltpu.SemaphoreType.DMA((n_row,))]
    outs_, _ = _hosted(body, name=name, grid=(m // tm,), in_specs=[pl.BlockSpec((tm, k), lambda i: (i, 0)),
                                                                 pl.BlockSpec((k, D), lambda i: (0, 0))]
                       + [HBM_SPEC] * n_row + [vec_spec] * n_vec, out_specs=out_specs, out_shape=out_shape,
                       args=[a, b] + list(row_ins) + list(vec_ins), scratch_shapes=scratch,
                       block_bytes=blocks, scratch_bytes=(1 + n_row) * tm * D * 4)
    return outs_


def _vec_add(ref, value, first):
    if first is None:
        ref[...] += value
    else:
        pl.when(first)(lambda: ref.__setitem__(Ellipsis, value))
        pl.when(jnp.logical_not(first))(lambda: ref.__setitem__(Ellipsis, ref[...] + value))


RB = 256


def _row_spec(width):
    return pl.BlockSpec((RB, width), lambda i: (i, 0))


def _vec_spec(width):
    return pl.BlockSpec((1, width), lambda i: (0, 0))


def _rinv(x):
    return lax.rsqrt(jnp.mean(x * x, axis=-1, keepdims=True) + EPS)


def _norm_bwd(dyn, xhat, r):
    return r * (dyn - xhat * jnp.mean(dyn * xhat, axis=-1, keepdims=True))


def _colsum(x):
    return jnp.sum(x, axis=0, keepdims=True)


def _mid_fwd(x, y, npost, npre, phases=()):
    def body(x_ref, y_ref, po_ref, pr_ref, x1_ref, h1_ref):
        yv = y_ref[...]
        x1 = x_ref[...] + yv * _rinv(yv) * po_ref[...]
        x1_ref[...] = x1
        h1_ref[...] = (x1 * _rinv(x1) * pr_ref[...]).astype(BF16)

    return _hosted(body, name="mid_fwd", grid=(S // RB,), in_specs=[_row_spec(D), _row_spec(D), _vec_spec(D), _vec_spec(D)],
                   out_specs=[_row_spec(D), _row_spec(D)], out_shape=[_sds((S, D), F32), _sds((S, D), BF16)],
                   args=[x, y, npost, npre], block_bytes=RB * D * 14, phases=phases)


def _final_tail(yv, rows, first, row_vals, vec_refs, out_refs):
    (xv, tv), (po_ref,), (loss_ref, dx_ref, dy_ref, dpo_ref) = row_vals, vec_refs, out_refs
    r = _rinv(yv)
    yhat = yv * r
    err = xv + yhat * po_ref[...] - tv
    dx = err * (1.0 / D)
    dx_ref[rows, :] = dx
    dy_ref[rows, :] = _norm_bwd(dx * po_ref[...], yhat, r).astype(BF16)
    _vec_add(loss_ref, _colsum(err * err), first)
    _vec_add(dpo_ref, _colsum(dx * yhat), first)


def _mid_bwd_tail(dh, rows, first, row_vals, vec_refs, out_refs):
    (dx2, xv, yv), (pr_ref, po_ref), (dx1_ref, dy_ref, dpr_ref, dpo_ref) = row_vals, vec_refs, out_refs
    r = _rinv(xv)
    xhat = xv * r
    dx1 = dx2 + _norm_bwd(dh * pr_ref[...], xhat, r)
    dx1_ref[rows, :] = dx1
    ry = _rinv(yv)
    yhat = yv * ry
    dy_ref[rows, :] = _norm_bwd(dx1 * po_ref[...], yhat, ry).astype(BF16)
    _vec_add(dpr_ref, _colsum(dh * xhat), first)
    _vec_add(dpo_ref, _colsum(dx1 * yhat), first)


def _first_bwd_tail(dh, rows, first, row_vals, vec_refs, out_refs):
    (dx1, xv), (pr_ref,), (gx_ref, dpr_ref) = row_vals, vec_refs, out_refs
    r = _rinv(xv)
    xhat = xv * r
    gx_ref[rows, :] = dx1 + _norm_bwd(dh * pr_ref[...], xhat, r)
    _vec_add(dpr_ref, _colsum(dh * xhat), first)


GLA_RB = 256
GLA_CPB = GLA_RB // C


def _sigmoid(x):
    return 1.0 / (1.0 + jnp.exp(-x))


def _tri(strict):
    r = lax.broadcasted_iota(jnp.int32, (C, C), 0)
    c = lax.broadcasted_iota(jnp.int32, (C, C), 1)
    return jnp.where(c < r if strict else c <= r, 1.0, 0.0).astype(BF16)


def _tri_dot(tri, x):
    hi = x.astype(BF16)
    lo = (x - hi.astype(F32)).astype(BF16)
    return _dot(tri, hi) + _dot(tri, lo)


def _gla_gates(glr_b, w2, b, tri):
    z = _dot(glr_b, w2) + b
    log_a = (jnp.minimum(z, 0.0) - jnp.log(1.0 + jnp.exp(-jnp.abs(z)))) * (1.0 / GLA_TAU)
    bcum = _tri_dot(tri, log_a)
    b_end = jnp.sum(log_a, axis=0, keepdims=True)
    return z, jnp.exp(b_end - bcum), jnp.exp(b_end)


def _gla_fwd(proj, w2p, bgate, ogain, phases=()):
    def body(p_ref, w2_ref, b_ref, og_ref, y_ref, st_out_ref, st_ref):
        @pl.when(pl.program_id(0) == 0)
        def _():
            st_ref[...] = jnp.zeros_like(st_ref)

        tri = _tri(False)

        def chunk(ci, carry):
            rows = pl.ds(pl.multiple_of(ci * C, C), C)
            glr_b = p_ref[rows, LR0:LR0 + LRP].astype(BF16)
            _, ea_all, dec_all = _gla_gates(glr_b, w2_ref[...], b_ref[...], tri)
            for h in range(H):
                ea, dec = ea_all[:, h * DK:(h + 1) * DK], dec_all[:, h * DK:(h + 1) * DK]
                k_dec = (p_ref[rows, K0 + h * DK:K0 + (h + 1) * DK] * ea).astype(BF16)
                v_b = p_ref[rows, V0 + h * DV:V0 + (h + 1) * DV].astype(BF16)
                st = st_ref[h] * dec + _dot(v_b, k_dec, TN)
                st_ref[h] = st
                st_b = st.astype(BF16)
                st_out_ref[ci, h] = st_b
                q_b = (p_ref[rows, Q0 + h * DK:Q0 + (h + 1) * DK] * (DK ** -0.5)).astype(BF16)
                o = _dot(q_b, st_b, NT)
                on = o * _rinv(o)
                g = p_ref[rows, G0 + h * DV:G0 + (h + 1) * DV]
                y_ref[rows, h * DV:(h + 1) * DV] = (on * og_ref[:, h * DV:(h + 1) * DV] * (g * _sigmoid(g))).astype(BF16)
            return carry

        lax.fori_loop(0, GLA_CPB, chunk, 0, unroll=True)

    blocks = GLA_RB * GLA_PAD * 4 + GLA_RB * D * 2 + GLA_CPB * H * DV * DK * 2
    return _hosted(
        body, name="gla_fwd", grid=(S // GLA_RB,),
        in_specs=[pl.BlockSpec((GLA_RB, GLA_PAD), lambda i: (i, 0)),
                  pl.BlockSpec((LRP, H * DK), lambda i: (0, 0)),
                  pl.BlockSpec((1, H * DK), lambda i: (0, 0)),
                  pl.BlockSpec((1, H * DV), lambda i: (0, 0))],
        out_specs=[pl.BlockSpec((GLA_RB, H * DV), lambda i: (i, 0)),
                   pl.BlockSpec((GLA_CPB, H, DV, DK), lambda i: (i, 0, 0, 0))],
        out_shape=[_sds((S, H * DV), BF16), _sds((NC, H, DV, DK), BF16)],
        args=[proj, w2p, bgate, ogain], scratch_shapes=[pltpu.VMEM((H, DV, DK), F32)],
        block_bytes=blocks, scratch_bytes=H * DV * DK * 4, phases=phases)


def _gla_bwd(proj, dypre, states, w2p, bgate, ogain, phases=()):
    nb = S // GLA_RB

    def body(p_ref, dy_ref, st_blk_ref, st_prev_ref, w2_ref, b_ref, og_ref,
             dp_ref, dog_ref, dbg_ref, dw2_ref, r_ref):
        step = pl.program_id(0)

        @pl.when(step == 0)
        def _():
            r_ref[...] = jnp.zeros_like(r_ref)
            dog_ref[...] = jnp.zeros_like(dog_ref)
            dbg_ref[...] = jnp.zeros_like(dbg_ref)
            dw2_ref[...] = jnp.zeros_like(dw2_ref)

        tri = _tri(False)
        tri_strict = _tri(True)
        has_prev = jnp.where(step < nb - 1, 1.0, 0.0).astype(F32)

        def chunk(ci, st_prev_of):
            rows = pl.ds(ci * C if isinstance(ci, int) else pl.multiple_of(ci * C, C), C)
            glr_b = p_ref[rows, LR0:LR0 + LRP].astype(BF16)
            z, ea_all, dec_all = _gla_gates(glr_b, w2_ref[...], b_ref[...], tri)
            d_a, d_end = [], []
            for h in range(H):
                kcol = slice(h * DK, (h + 1) * DK)
                vcol = slice(h * DV, (h + 1) * DV)
                ea, dec = ea_all[:, kcol], dec_all[:, kcol]
                k_dec = p_ref[rows, K0 + h * DK:K0 + (h + 1) * DK] * ea
                k_dec_b = k_dec.astype(BF16)
                v_b = p_ref[rows, V0 + h * DV:V0 + (h + 1) * DV].astype(BF16)
                q_b = (p_ref[rows, Q0 + h * DK:Q0 + (h + 1) * DK] * (DK ** -0.5)).astype(BF16)
                st_b = st_blk_ref[ci, h]
                o = _dot(q_b, st_b, NT)
                rinv = _rinv(o)
                on = o * rinv
                g = p_ref[rows, G0 + h * DV:G0 + (h + 1) * DV]
                sg = _sigmoid(g)
                og = og_ref[:, vcol]
                dyp = dy_ref[rows, vcol]
                dp_ref[rows, G0 + h * DV:G0 + (h + 1) * DV] = (dyp * (on * og) * (sg * (1.0 + g * (1.0 - sg)))).astype(BF16)
                dpn = dyp * (g * sg)
                dog_ref[:, vcol] += _colsum(dpn * on)
                do_b = _norm_bwd(dpn * og, on, rinv).astype(BF16)
                gt = _dot(do_b, q_b, TN) + r_ref[h]
                gt_b = gt.astype(BF16)
                dp_ref[rows, Q0 + h * DK:Q0 + (h + 1) * DK] = (_dot(do_b, st_b) * (DK ** -0.5)).astype(BF16)
                dkd = _dot(v_b, gt_b)
                dp_ref[rows, V0 + h * DV:V0 + (h + 1) * DV] = _dot(k_dec_b, gt_b, NT).astype(BF16)
                dp_ref[rows, K0 + h * DK:K0 + (h + 1) * DK] = (dkd * ea).astype(BF16)
                d_a.append(dkd * k_dec)
                d_end.append(_colsum(gt * st_prev_of(h)) * dec)
                r_ref[h] = gt * dec
            dla = _tri_dot(tri_strict, jnp.concatenate(d_a, axis=1)) + jnp.concatenate(d_end, axis=1)
            dz = dla * (1.0 / GLA_TAU) * (1.0 - _sigmoid(z))
            dz_b = dz.astype(BF16)
            dbg_ref[...] += _colsum(dz)
            dw2_ref[...] += _dot(glr_b, dz_b, TN)
            dp_ref[rows, LR0:LR0 + LRP] = _dot(dz_b, w2_ref[...], NT).astype(BF16)

        def later_chunk(t, carry):
            ci = GLA_CPB - 1 - t
            chunk(ci, lambda h: st_blk_ref[ci - 1, h].astype(F32))
            return carry

        lax.fori_loop(0, GLA_CPB - 1, later_chunk, 0, unroll=True)
        chunk(0, lambda h: st_prev_ref[0, h].astype(F32) * has_prev)

    blocks = (GLA_RB * GLA_PAD * 4 + GLA_RB * D * 4 + (GLA_CPB + 1) * H * DV * DK * 2 + GLA_RB * GLA_PAD * 2)
    rev = lambda i: nb - 1 - i
    return _hosted(
        body, name="gla_bwd", grid=(nb,),
        in_specs=[pl.BlockSpec((GLA_RB, GLA_PAD), lambda i: (rev(i), 0)),
                  pl.BlockSpec((GLA_RB, H * DV), lambda i: (rev(i), 0)),
                  pl.BlockSpec((GLA_CPB, H, DV, DK), lambda i: (rev(i), 0, 0, 0)),
                  pl.BlockSpec((1, H, DV, DK), lambda i: (jnp.maximum(rev(i) * GLA_CPB - 1, 0), 0, 0, 0)),
                  pl.BlockSpec((LRP, H * DK), lambda i: (0, 0)),
                  pl.BlockSpec((1, H * DK), lambda i: (0, 0)),
                  pl.BlockSpec((1, H * DV), lambda i: (0, 0))],
        out_specs=[pl.BlockSpec((GLA_RB, GLA_PAD), lambda i: (rev(i), 0)),
                   pl.BlockSpec((1, H * DV), lambda i: (0, 0)),
                   pl.BlockSpec((1, H * DK), lambda i: (0, 0)),
                   pl.BlockSpec((LRP, H * DK), lambda i: (0, 0))],
        out_shape=[_sds((S, GLA_PAD), BF16), _sds((1, H * DV), F32), _sds((1, H * DK), F32), _sds((LRP, H * DK), F32)],
        args=[proj, dypre, states, states, w2p, bgate, ogain], scratch_shapes=[pltpu.VMEM((H, DV, DK), F32)],
        block_bytes=blocks, scratch_bytes=H * DV * DK * 4, phases=phases)


SGU_RB = 256
GELU_C = 0.7978845608028654
GELU_A = 0.044715


def _gelu(x):
    return 0.5 * x * (1.0 + jnp.tanh(GELU_C * (x + GELU_A * x * x * x)))


def _gelu_grad(x):
    t = jnp.tanh(GELU_C * (x + GELU_A * x * x * x))
    return 0.5 * (1.0 + t) + 0.5 * x * (1.0 - t * t) * (GELU_C * (1.0 + 3.0 * GELU_A * x * x))


def _causal_mask(transposed=False):
    i = lax.broadcasted_iota(jnp.int32, (SGU_BLOCK, SGU_BLOCK), 1 if transposed else 0)
    j = lax.broadcasted_iota(jnp.int32, (SGU_BLOCK, SGU_BLOCK), 0 if transposed else 1)
    return (i >= C) | (j < C)


def _layer_norm(vf, gain, bias):
    mu = jnp.mean(vf, axis=-1, keepdims=True)
    cen = vf - mu
    rstd = lax.rsqrt(jnp.mean(cen * cen, axis=-1, keepdims=True) + EPS)
    xhat = cen * rstd
    return xhat, rstd, xhat * gain + bias


def _sgu_fwd(proj, lng, lnb, ws, bsb, phases=()):
    def body(p_ref, g_ref, b_ref, ws_ref, bs_ref, o_ref):
        mask = _causal_mask()
        for n in range(SGU_RB // SGU_BLOCK):
            rows = slice(n * SGU_BLOCK, (n + 1) * SGU_BLOCK)
            _, _, vn = _layer_norm(_gelu(p_ref[rows, D:2 * D]), g_ref[...], b_ref[...])
            vn_b = vn.astype(BF16)
            for gi in range(SGU_G):
                cols = slice(gi * SGU_GD, (gi + 1) * SGU_GD)
                w = jnp.where(mask, ws_ref[gi], 0.0).astype(BF16)
                vs = _dot(w, vn_b[:, cols]) + bs_ref[gi]
                gate = p_ref[rows, 2 * D + gi * SGU_GD:2 * D + (gi + 1) * SGU_GD]
                o_ref[rows, cols] = (_gelu(p_ref[rows, cols]) * vs * (gate * _sigmoid(gate))).astype(BF16)

    blocks = SGU_RB * SGU_COLS * 4 + SGU_RB * D * 2 + SGU_G * SGU_BLOCK * (SGU_BLOCK + SGU_GD) * 4
    return _hosted(
        body, name="sgu_fwd", grid=(S // SGU_RB,),
        in_specs=[pl.BlockSpec((SGU_RB, SGU_COLS), lambda i: (i, 0)),
                  pl.BlockSpec((1, D), lambda i: (0, 0)), pl.BlockSpec((1, D), lambda i: (0, 0)),
                  pl.BlockSpec((SGU_G, SGU_BLOCK, SGU_BLOCK), lambda i: (0, 0, 0)),
                  pl.BlockSpec((SGU_G, SGU_BLOCK, SGU_GD), lambda i: (0, 0, 0))],
        out_specs=[pl.BlockSpec((SGU_RB, D), lambda i: (i, 0))], out_shape=[_sds((S, D), BF16)],
        args=[proj, lng, lnb, ws, bsb], block_bytes=blocks, phases=phases)


def _sgu_bwd(proj, dpre, lng, lnb, ws, wst, bsb, phases=()):
    nsteps = S // SGU_RB

    def body(p_ref, d_ref, g_ref, b_ref, ws_ref, wst_ref, bs_ref,
             dp_ref, dg_ref, db_ref, dws_ref, dbs_ref, dvn_ref, dvs_acc_ref):
        step = pl.program_id(0)

        @pl.when(step == 0)
        def _():
            dg_ref[...] = jnp.zeros_like(dg_ref)
            db_ref[...] = jnp.zeros_like(db_ref)
            dws_ref[...] = jnp.zeros_like(dws_ref)
            dvs_acc_ref[...] = jnp.zeros_like(dvs_acc_ref)

        mask = _causal_mask()
        maskt = _causal_mask(transposed=True)
        for n in range(SGU_RB // SGU_BLOCK):
            rows = slice(n * SGU_BLOCK, (n + 1) * SGU_BLOCK)
            v = p_ref[rows, D:2 * D]
            xhat, rstd, vn = _layer_norm(_gelu(v), g_ref[...], b_ref[...])
            vn_b = vn.astype(BF16)
            for gi in range(SGU_G):
                cols = slice(gi * SGU_GD, (gi + 1) * SGU_GD)
                w = jnp.where(mask, ws_ref[gi], 0.0).astype(BF16)
                wt = jnp.where(maskt, wst_ref[gi], 0.0).astype(BF16)
                vs = _dot(w, vn_b[:, cols]) + bs_ref[gi]
                u = p_ref[rows, cols]
                gate = p_ref[rows, 2 * D + gi * SGU_GD:2 * D + (gi + 1) * SGU_GD]
                sg = _sigmoid(gate)
                gu = _gelu(u)
                dpre_g = d_ref[rows, cols]
                t = dpre_g * (gate * sg)
                dp_ref[rows, cols] = (t * vs * _gelu_grad(u)).astype(BF16)
                dp_ref[rows, 2 * D + gi * SGU_GD:2 * D + (gi + 1) * SGU_GD] = (
                    dpre_g * gu * vs * (sg * (1.0 + gate * (1.0 - sg)))).astype(BF16)
                dvs = t * gu
                dvs_b = dvs.astype(BF16)
                dvs_acc_ref[:, cols] += dvs
                dws_ref[gi] += _dot(dvs_b, vn_b[:, cols], NT)
                dvn_ref[:, cols] = _dot(wt, dvs_b)
            dvn = dvn_ref[...]
            dg_ref[...] += _colsum(dvn * xhat)
            db_ref[...] += _colsum(dvn)
            dxh = dvn * g_ref[...]
            dvf = rstd * (dxh - jnp.mean(dxh, axis=-1, keepdims=True) - xhat * jnp.mean(dxh * xhat, axis=-1, keepdims=True))
            dp_ref[rows, D:2 * D] = (dvf * _gelu_grad(v)).astype(BF16)

        @pl.when(step == nsteps - 1)
        def _():
            lane = lax.broadcasted_iota(jnp.int32, (SGU_BLOCK, SGU_BLOCK), 1)
            out = jnp.zeros((SGU_BLOCK, SGU_BLOCK), F32)
            for gi in range(SGU_G):
                out = out + jnp.where(lane == gi, jnp.sum(dvs_acc_ref[:, gi * SGU_GD:(gi + 1) * SGU_GD], axis=1, keepdims=True), 0.0)
                dws_ref[gi] = jnp.where(mask, dws_ref[gi], 0.0)
            dbs_ref[...] = out

    blocks = SGU_RB * SGU_COLS * 6 + SGU_RB * D * 4 + SGU_G * SGU_BLOCK * (3 * SGU_BLOCK + SGU_GD) * 4
    const3 = lambda i: (0, 0, 0)
    return _hosted(
        body, name="sgu_bwd", grid=(nsteps,),
        in_specs=[pl.BlockSpec((SGU_RB, SGU_COLS), lambda i: (i, 0)),
                  pl.BlockSpec((SGU_RB, D), lambda i: (i, 0)),
                  pl.BlockSpec((1, D), lambda i: (0, 0)), pl.BlockSpec((1, D), lambda i: (0, 0)),
                  pl.BlockSpec((SGU_G, SGU_BLOCK, SGU_BLOCK), const3),
                  pl.BlockSpec((SGU_G, SGU_BLOCK, SGU_BLOCK), const3),
                  pl.BlockSpec((SGU_G, SGU_BLOCK, SGU_GD), const3)],
        out_specs=[pl.BlockSpec((SGU_RB, SGU_COLS), lambda i: (i, 0)),
                   pl.BlockSpec((1, D), lambda i: (0, 0)), pl.BlockSpec((1, D), lambda i: (0, 0)),
                   pl.BlockSpec((SGU_G, SGU_BLOCK, SGU_BLOCK), const3),
                   pl.BlockSpec((SGU_BLOCK, SGU_BLOCK), lambda i: (0, 0))],
        out_shape=[_sds((S, SGU_COLS), BF16), _sds((1, D), F32), _sds((1, D), F32),
                   _sds((SGU_G, SGU_BLOCK, SGU_BLOCK), F32), _sds((SGU_BLOCK, SGU_BLOCK), F32)],
        args=[proj, dpre, lng, lnb, ws, wst, bsb],
        scratch_shapes=[pltpu.VMEM((SGU_BLOCK, D), F32), pltpu.VMEM((SGU_BLOCK, D), F32)],
        block_bytes=blocks, scratch_bytes=2 * SGU_BLOCK * D * 4, phases=phases)


def _pair_sum(own, a, r0, nr, name, table=None):
    c = own.shape[2]
    tr = 256
    assert r0 % tr == 0 and nr % tr == 0

    def body(own_ref, sib_ref, o_ref):
        o_ref[...] = (own_ref[...].astype(F32) + sib_ref[...].astype(F32)).astype(BF16)

    own_map = ((lambda j, i: (1 + j, r0 // tr + i, 0)) if table is None else
               (lambda j, i, t: (t[1 + j], r0 // tr + i, 0)))
    cpad = -(-c // 128) * 128
    outs, _ = _hosted(
        body, name=name, grid=(3, nr // tr),
        in_specs=[pl.BlockSpec((None, tr, c), own_map),
                  pl.BlockSpec((None, tr, c), lambda j, i, *t: (1 + j, r0 // tr + i, 0))],
        out_specs=[pl.BlockSpec((None, tr, c), lambda j, i, *t: (j, i, 0))], out_shape=[_sds((3, nr, c), BF16)],
        args=[own, a], block_bytes=3 * tr * cpad * 2, table=table)
    return outs[0]


def _adamw_math(w, g, m, v):
    m = ADAM_B1 * m + (1.0 - ADAM_B1) * g
    v = ADAM_B2 * v + (1.0 - ADAM_B2) * (g * g)
    m_hat = m / (1.0 - ADAM_B1 ** ADAM_STEP)
    v_hat = v / (1.0 - ADAM_B2 ** ADAM_STEP)
    delta = -ADAM_LR * (m_hat / (jnp.sqrt(v_hat) + ADAM_EPS) + ADAM_WD * w)
    return delta, m, v


def _sum_adamw(own, a, b, w, m, v, *, name, phases=(), table=None):
    r, c = w.shape
    tr = 256

    def body(own_ref, sib_ref, far_ref, w_ref, m_ref, v_ref, g_ref, d_ref, nm_ref, nv_ref):
        g = own_ref[...].astype(F32) + sib_ref[...].astype(F32)
        for j in range(3):
            g = g + far_ref[j].astype(F32)
        g_ref[...] = g
        d_ref[...], nm_ref[...], nv_ref[...] = _adamw_math(w_ref[...], g, m_ref[...], v_ref[...])

    spec = pl.BlockSpec((tr, c), lambda i, *t: (i, 0))
    own_map = (lambda i: (0, i, 0)) if table is None else (lambda i, t: (t[0], i, 0))
    cpad = -(-c // 128) * 128
    return _hosted(
        body, name=name, grid=(r // tr,),
        in_specs=[pl.BlockSpec((None, tr, c), own_map), pl.BlockSpec((None, tr, c), lambda i, *t: (0, i, 0)),
                  pl.BlockSpec((3, tr, c), lambda i, *t: (0, i, 0)), spec, spec, spec],
        out_specs=[spec] * 4, out_shape=[_sds((r, c), F32)] * 4, args=[own, a, b, w, m, v],
        block_bytes=5 * tr * cpad * 2 + 7 * tr * cpad * 4, phases=phases, table=table)


def _sum_parts(parts, name):
    n, r, c = parts.shape

    def body(p_ref, o_ref):
        g = p_ref[0]
        for j in range(1, n):
            g = g + p_ref[j]
        o_ref[...] = g

    outs, _ = _hosted(body, name=name, grid=(1,), in_specs=[pl.BlockSpec((n, r, c), lambda i: (0, 0, 0))],
                      out_specs=[pl.BlockSpec((r, c), lambda i: (0, 0))], out_shape=[_sds((r, c), F32)], args=[parts],
                      block_bytes=(n + 1) * r * c * 4)
    return outs[0]


def _adamw(w, g, m, v, name):
    def body(w_ref, g_ref, m_ref, v_ref, d_ref, nm_ref, nv_ref):
        d_ref[...], nm_ref[...], nv_ref[...] = _adamw_math(w_ref[...], g_ref[...], m_ref[...], v_ref[...])

    spec = pl.BlockSpec(w.shape, lambda i: (0, 0))
    outs, _ = _hosted(body, name=name, grid=(1,), in_specs=[spec] * 4, out_specs=[spec] * 3, out_shape=[_sds(w.shape, F32)] * 3,
                      args=[w, g, m, v], block_bytes=7 * _nbytes(w.shape, F32))
    return outs


def _blocks_to_columns(g):
    n, r, c = g.shape
    return jnp.transpose(g, (1, 0, 2)).reshape(r, n * c)


def _pack(parts):
    return jnp.concatenate([p.reshape(-1) for p in parts]).reshape(-1, 128)


def _unpack(packed, like):
    flat, outs, off = packed.reshape(-1), [], 0
    for p in like:
        outs.append(flat[off:off + p.size].reshape(p.shape))
        off += p.size
    return outs


def kernel(x, norm_pre, norm_post, gla_w_in, gla_w_gate2, gla_b_gate, gla_o_gain, gla_w_out, sgu_w_in, sgu_ln_gain, sgu_ln_bias, sgu_w_spatial, sgu_b_spatial, sgu_w_out, loss_target, m_norm_pre, m_norm_post, m_gla_w_in, m_gla_w_gate2, m_gla_b_gate, m_gla_o_gain, m_gla_w_out, m_sgu_w_in, m_sgu_ln_gain, m_sgu_ln_bias, m_sgu_w_spatial, m_sgu_b_spatial, m_sgu_w_out, v_norm_pre, v_norm_post, v_gla_w_in, v_gla_w_gate2, v_gla_b_gate, v_gla_o_gain, v_gla_w_out, v_sgu_w_in, v_sgu_ln_gain, v_sgu_ln_bias, v_sgu_w_spatial, v_sgu_b_spatial, v_sgu_w_out):
    me = _index_of(*_place())
    x0 = x.reshape(S, D)
    tgt = loss_target.reshape(S, D)
    npre0, npre1 = norm_pre[0:1], norm_pre[1:2]
    npost0, npost1 = norm_post[0:1], norm_post[1:2]
    ws = sgu_w_spatial[0]
    wst = jnp.transpose(ws, (0, 2, 1))
    bsb = jnp.broadcast_to(sgu_b_spatial[0][:, :, None], (SGU_G, SGU_BLOCK, SGU_GD))
    W_ROWS = D // N_DEV
    IN_COLS_G, IN_COLS_S = GLA_COLS // N_DEV, SGU_COLS // N_DEV

    s_gwi, s_gwo = gla_w_in[0].astype(BF16), gla_w_out[0].astype(BF16)
    s_swi, s_swo = sgu_w_in[0].astype(BF16), sgu_w_out[0].astype(BF16)
    small = jnp.concatenate([jnp.pad(gla_w_gate2[0].reshape(4, 512), ((0, 4), (0, 0))),
                             jnp.pad(jnp.concatenate([sgu_ln_gain, sgu_ln_bias], axis=1), ((0, 7), (0, 0)))], axis=0)

    wg_in, g_small, h0 = _gather_first(s_gwi, small, x0, npre0, "gather_first")
    w2 =_blocks_to_columns(g_small[:, :4, :].reshape(N_DEV, LR, 128))
    w2p = jnp.pad(w2, ((0, LRP - LR), (0, 0))).astype(BF16)
    lng = g_small[:, 8, :256].reshape(1, D)
    lnb = g_small[:, 8, 256:].reshape(1, D)
    like_gwo, like_swi = _sds((N_DEV, W_ROWS, D), BF16), _sds((N_DEV, D, IN_COLS_S), BF16)

    proj0, (g_gwo, g_swi) = _mm(h0, wg_in, "nn", F32, tm=1024, tn=896, tk=D, name="gla_in", b_tiled=True, phases=[
        _Phase(like_gwo, None, [_gather_send(s_gwo, 0, W_ROWS)]),
        _Phase(like_swi, None, [_gather_send(s_swi, 0, 768, diagonal=False)])])
    (ypre0, states), (g_gwo, g_swi) = _gla_fwd(proj0, w2p, gla_b_gate, gla_o_gain, phases=[
        _Phase(like_gwo, g_gwo, [_gather_pass(0, W_ROWS)]),
        _Phase(like_swi, g_swi, [_gather_relay(0, 768), _gather_send(s_swi, 768, 512, diagonal=False)])])
    wg_out = g_gwo.reshape(D, D)
    y0, (g_swi,) = _mm(ypre0, wg_out, "nn", F32, tm=1024, tn=1024, tk=D, name="gla_out", phases=[
        _Phase(like_swi, g_swi, [_gather_pass(0, 768), _gather_relay(768, 512), _gather_send(s_swi, 1280, 512, diagonal=False)])])
    (x1, h1), (g_swi,) = _mid_fwd(x0, y0, npost0, npre1, phases=[
        _Phase(like_swi, g_swi, [_gather_pass(768, 512), _gather_relay(1280, 512), _gather_send(s_swi, 1792, 256, diagonal=False)])])
    g_swi, = _carry([_Phase(like_swi, g_swi, [_gather_pass(1280, 512), _gather_relay(1792, 256)])], "relay_sgu_w_in")
    g_swi, = _carry([_Phase(like_swi, g_swi, [_gather_pass(1792, 256)])], "pass_sgu_w_in")
    proj1, (g_swo,) = _mm(h1, g_swi, "nn", F32, tm=S, tn=IN_COLS_S, tk=D, name="sgu_in", b_blocked=True, phases=[
        _Phase(like_gwo, None, [_gather_send(s_swo, 0, W_ROWS)])])
    (pre1,), (g_swo,) = _sgu_fwd(proj1, lng, lnb, ws, bsb, phases=[_Phase(like_gwo, g_swo, [_gather_pass(0, W_ROWS)])])
    ws_out = g_swo.reshape(D, D)
    loss_cols, dx2, dy1, dnpost1 = _nn_rows(pre1, ws_out, name="sgu_out", row_ins=[x1, tgt], vec_ins=[npost1],
                                            outs=[("vec", F32), ("row", F32), ("row", BF16), ("vec", F32)], tail=_final_tail)
    loss_here = jnp.pad((0.5 * jnp.sum(loss_cols) / D).reshape(1, 1), ((0, 7), (0, 127)))

    like_b_out, like_b_swi = _sds((3, W_ROWS, D), BF16), _sds((3, D, IN_COLS_S), BF16)
    like_b_gwi = _sds((3, D, IN_COLS_G), BF16)
    row_pair = dict(like=_sds((4, W_ROWS, D), BF16), block=lambda i, j: i, ordinal=lambda i, j: i >> 1,
                    dst=lambda ref, k, i, j: ref.at[k])
    col_pair = dict(like=_sds((4, D, IN_COLS_S), BF16), block=lambda i, j: j, ordinal=lambda i, j: 4 * i + (j >> 1),
                    dst=lambda ref, k, i, j: ref.at[k, pl.ds(pl.multiple_of(i * 1024, 1024), 1024)])

    mine = _own_table()
    dws_out, (a_swo,) = _mm(pre1, dy1, "tn", BF16, tm=W_ROWS, tn=D, tk=S, name="sgu_out_dw", pair=row_pair)
    p_swo = dws_out.reshape(N_DEV, W_ROWS, D)
    t_swo = _pair_sum(p_swo, a_swo, 0, W_ROWS, "pair_sum_sgu_w_out", table=mine)
    dpre1, _ = _mm(dy1, ws_out, "nt", F32, tm=1024, tn=1024, tk=D, name="sgu_out_dx")
    (dproj1, dlng, dlnb, dwsp, dbsp), (b_swo,) = _sgu_bwd(proj1, dpre1, lng, lnb, ws, wst, bsb, phases=[
        _Phase(like_b_out, None, [_reduce_cross(t_swo, 0, 0, W_ROWS)])])
    p_swi, (a_swi,) = _mm(h1, dproj1, "tn", BF16, tm=1024, tn=IN_COLS_S, tk=S, name="sgu_in_dw", out_blocked=True, pair=col_pair)
    t_swi = _pair_sum(p_swi, a_swi, 0, D, "pair_sum_sgu_w_in", table=mine)
    (dx1, dy0, dnpre1, dnpost0), (b_swi,) = _nt_rows(
        dproj1, g_swi, tk=IN_COLS_S, name="sgu_in_dx", b_blocked=True, row_ins=[dx2, x1, y0], vec_ins=[npre1, npost0],
        outs=[("row", F32), ("row", BF16), ("vec", F32), ("vec", F32)], tail=_mid_bwd_tail, phases=[
            _Phase(like_b_swi, None, [_reduce_cross(t_swi, 0, 0, 1024)])])
    dwg_out, (a_gwo, b_swi) = _mm(ypre0, dy0, "tn", BF16, tm=W_ROWS, tn=D, tk=S, name="gla_out_dw", pair=row_pair, phases=[
        _Phase(like_b_swi, b_swi, [_reduce_cross(t_swi, 1024, 1024, 256)])])
    p_gwo = dwg_out.reshape(N_DEV, W_ROWS, D)
    t_gwo = _pair_sum(p_gwo, a_gwo, 0, W_ROWS, "pair_sum_gla_w_out", table=mine)
    dypre0, _ = _mm(dy0, wg_out, "nt", F32, tm=1024, tn=1024, tk=D, name="gla_out_dx")
    late = [dnpre1, dnpost1, dlng, dlnb, dwsp, jnp.transpose(dbsp[:, :SGU_G])]
    late_pack = _pack(late)
    (dproj0, dogain, dbgate, dw2), (b_swi, b_gwo, g_late) = _gla_bwd(proj0, dypre0, states, w2p, gla_b_gate, gla_o_gain, phases=[
        _Phase(like_b_swi, b_swi, [_reduce_cross(t_swi, 1280, 1280, 768)]),
        _Phase(like_b_out, None, [_reduce_cross(t_gwo, 0, 0, W_ROWS)]),
        _Phase(_sds((N_DEV,) + late_pack.shape, F32), None, [_gather_send(late_pack, 0, late_pack.shape[0])])])
    half = D // 2
    (own_gwi, a_gwi), (g_late,) = _dw_blocks(h0, dproj0, None, None, 0, half, "gla_in_dw_a", phases=[
        _Phase(_sds((N_DEV,) + late_pack.shape, F32), g_late, [_gather_pass(0, late_pack.shape[0])])])
    t_gwi_a = _pair_sum(own_gwi, a_gwi, 0, half, "pair_sum_gla_w_in_a")
    (own_gwi, a_gwi), (b_gwi,) = _dw_blocks(h0, dproj0, own_gwi, a_gwi, half, half, "gla_in_dw_b", phases=[
        _Phase(like_b_gwi, None, [_reduce_cross(t_gwi_a, 0, 0, 704)])])
    t_gwi_b = _pair_sum(own_gwi, a_gwi, half, half, "pair_sum_gla_w_in_b")
    (grad_x, dnpre0), (b_gwi,) = _nt_rows(
        dproj0, wg_in, tk=896, name="gla_in_dx", b_tiled=True, row_ins=[dx1, x0], vec_ins=[npre0],
        outs=[("row", F32), ("vec", F32)], tail=_first_bwd_tail, phases=[
            _Phase(like_b_gwi, b_gwi, [_reduce_cross(t_gwi_a, 704, 704, 320), _reduce_cross(t_gwi_b, 0, half, half)])])

    early = [dnpre0, dnpost0, dbgate, dogain, dw2[:LR], loss_here]
    early_pack = _pack(early)
    like_early = _sds((N_DEV,) + early_pack.shape, F32)
    (g_swo, d_swo, nm_swo, nv_swo), (g_early,) = _sum_adamw(
        p_swo, a_swo, b_swo, sgu_w_out[0], m_sgu_w_out[0], v_sgu_w_out[0], name="adamw_sgu_w_out", table=mine, phases=[
            _Phase(like_early, None, [_gather_send(early_pack, 0, early_pack.shape[0])])])
    (g_gwo_, d_gwo, nm_gwo, nv_gwo), (g_early,) = _sum_adamw(
        p_gwo, a_gwo, b_gwo, gla_w_out[0], m_gla_w_out[0], v_gla_w_out[0], name="adamw_gla_w_out", table=mine, phases=[
            _Phase(like_early, g_early, [_gather_pass(0, early_pack.shape[0])])])
    (g_swi_, d_swi, nm_swi, nv_swi), _ = _sum_adamw(
        p_swi, a_swi, b_swi, sgu_w_in[0], m_sgu_w_in[0], v_sgu_w_in[0], name="adamw_sgu_w_in", table=mine)
    (g_gwi_, d_gwi, nm_gwi, nv_gwi), _ = _sum_adamw(
        own_gwi, a_gwi, b_gwi, gla_w_in[0], m_gla_w_in[0], v_gla_w_in[0], name="adamw_gla_w_in")

    g_npre1, g_npost1, g_lng_full, g_lnb_full, g_wsp, g_bsp = _unpack(_sum_parts(g_late, "sum_late_small_grads"), late)
    g_npre0, g_npost0, g_bgate, g_ogain, g_w2_full, loss_all = _unpack(_sum_parts(g_early, "sum_early_small_grads"), early)
    loss = loss_all[0, 0]
    g_w2 = lax.dynamic_slice(g_w2_full, (0, me * 128), (LR, 128))
    g_lng = lax.dynamic_slice(g_lng_full, (0, me * 256), (1, 256))
    g_lnb = lax.dynamic_slice(g_lnb_full, (0, me * 256), (1, 256))
    small_g = [jnp.concatenate([g_npre0, g_npre1], 0), jnp.concatenate([g_npost0, g_npost1], 0), g_w2, g_bgate, g_ogain,
               g_lng, g_lnb, g_wsp, g_bsp]
    small_w = [norm_pre, norm_post, gla_w_gate2[0], gla_b_gate, gla_o_gain, sgu_ln_gain, sgu_ln_bias, sgu_w_spatial[0], sgu_b_spatial[0]]
    small_m = [m_norm_pre, m_norm_post, m_gla_w_gate2[0], m_gla_b_gate, m_gla_o_gain, m_sgu_ln_gain, m_sgu_ln_bias, m_sgu_w_spatial[0], m_sgu_b_spatial[0]]
    small_v = [v_norm_pre, v_norm_post, v_gla_w_gate2[0], v_gla_b_gate, v_gla_o_gain, v_sgu_ln_gain, v_sgu_ln_bias, v_sgu_w_spatial[0], v_sgu_b_spatial[0]]
    d_pack, nm_pack, nv_pack = _adamw(_pack(small_w), _pack(small_g), _pack(small_m), _pack(small_v), "adamw_small")

    out_like = [norm_pre, norm_post, gla_w_gate2, gla_b_gate, gla_o_gain, sgu_ln_gain, sgu_ln_bias, sgu_w_spatial, sgu_b_spatial]
    sg_ = [g.reshape(s.shape) for g, s in zip(small_g, out_like)]
    sd_, sm_, sv_ = (_unpack(pk, out_like) for pk in (d_pack, nm_pack, nv_pack))

    def assemble(small_list, w_in_g, w_out_g, w_in_s, w_out_s):
        npre_, npost_, w2_, bg_, og_, lg_, lb_, wsp_, bsp_ = small_list
        return [npre_, npost_, w_in_g[None], w2_, bg_, og_, w_out_g[None], w_in_s[None], lg_, lb_, wsp_, bsp_, w_out_s[None]]

    return (loss, grad_x.reshape(1, S, D),
            *assemble(sg_, g_gwi_, g_gwo_, g_swi_, g_swo),
            *assemble(sd_, d_gwi, d_gwo, d_swi, d_swo),
            *assemble(sm_, nm_gwi, nm_gwo, nm_swi, nm_swo),
            *assemble(sv_, nv_gwi, nv_gwo, nv_swi, nv_swo))
```
